```python
import jax, jax.numpy as jnp
from jax import lax
import numpy as np

D_MODEL = 1024
BATCH = 8
SEQ = 4096
DEPTH = 2

CHUNK = 64
CONV_W = D_MODEL
CONV_A_K = 3
LRU_W = D_MODEL
LRU_HEADS = 8
LRU_BW = LRU_W // LRU_HEADS
CONV_B_K = 4
LRU_C = 8.0
D_FF = 4 * D_MODEL
N_MOD = 6
N_IN = 3 * CONV_W + 2 * LRU_W + 2 * D_MODEL
EPS = 1e-6
SPLITS = (CONV_W, 2 * CONV_W, 3 * CONV_W, 3 * CONV_W + LRU_W,
          3 * CONV_W + 2 * LRU_W, 3 * CONV_W + 2 * LRU_W + D_MODEL)

kernel_name = "hybrid_shortconv_rglru_sandwich_adaln"


def rms_norm(x, g):
    xf = x.astype(jnp.float32)
    y = xf * lax.rsqrt(jnp.mean(xf * xf, axis=-1, keepdims=True) + EPS)
    return (y * g.astype(jnp.float32)).astype(x.dtype)


def causal_dwconv(x, w, b):
    k_w = w.shape[0]
    s = x.shape[1]
    xp = jnp.pad(x, ((0, 0), (k_w - 1, 0), (0, 0)))
    y = b
    for k in range(k_w):
        y = y + xp[:, k:k + s, :] * w[k]
    return y


def block_diag_linear(x, w, b):
    bsz, s, _ = x.shape
    xh = x.reshape(bsz, s, LRU_HEADS, LRU_BW)
    y = jnp.einsum("bshi,hij->bshj", xh, w).reshape(bsz, s, LRU_W)
    return y + b


def _lin_combine(left, right):
    a1, b1 = left
    a2, b2 = right
    return a1 * a2, a2 * b1 + b2


def chunked_linear_scan(a, b):
    bsz, s, w = a.shape
    n = s // CHUNK
    a_c = a.reshape(bsz, n, CHUNK, w)
    b_c = b.reshape(bsz, n, CHUNK, w)
    a_cum, h_loc = lax.associative_scan(_lin_combine, (a_c, b_c), axis=2)

    def step(h_prev, inp):
        a_k, h_k = inp
        h_full = h_k + a_k * h_prev[:, None, :]
        return h_full[:, -1, :], h_full

    h0 = jnp.zeros((bsz, w), a.dtype)
    _, hs = lax.scan(step, h0, (jnp.swapaxes(a_cum, 0, 1), jnp.swapaxes(h_loc, 0, 1)))
    return jnp.swapaxes(hs, 0, 1).reshape(bsz, s, w)


def rg_lru(x, w_r, b_r, w_i, b_i, lam):
    xf = x.astype(jnp.float32)
    r = jax.nn.sigmoid(block_diag_linear(xf, w_r.astype(jnp.float32), b_r.astype(jnp.float32)))
    i = jax.nn.sigmoid(block_diag_linear(xf, w_i.astype(jnp.float32), b_i.astype(jnp.float32)))
    log_a = -LRU_C * r * jax.nn.softplus(-lam.astype(jnp.float32))
    a = jnp.exp(log_a)
    mult = jnp.sqrt(-jnp.expm1(2.0 * log_a))
    h = chunked_linear_scan(a, mult * (i * xf))
    return h.astype(x.dtype)


def hybrid_mixer(h, w_in, conv_a_w, conv_a_b, w_a_out, conv_b_w, conv_b_b,
                 w_gate_r, b_gate_r, w_gate_i, b_gate_i, lru_lambda, w_b_out, w_o):
    proj = h @ w_in
    b_a, c_a, v_a, x_b, g_b, u_a, u_b = jnp.split(proj, SPLITS, axis=-1)
    y_a = b_a * causal_dwconv(c_a * v_a, conv_a_w, conv_a_b)
    x_b = causal_dwconv(x_b, conv_b_w, conv_b_b)
    y_b = rg_lru(x_b, w_gate_r, b_gate_r, w_gate_i, b_gate_i, lru_lambda) * jax.nn.gelu(g_b)
    m = jax.nn.sigmoid(u_a) * (y_a @ w_a_out) + jax.nn.sigmoid(u_b) * (y_b @ w_b_out)
    return m @ w_o


def sq_relu_mlp(h, w_up, w_down):
    return jnp.square(jax.nn.relu(h @ w_up)) @ w_down


def _fwd_setup_inputs(seed: int = 0) -> dict:
    key = jax.random.key(seed)
    ks = jax.random.split(key, 24)
    L, D = DEPTH, D_MODEL
    f32 = jnp.float32

    def nrm(k, shape, scale):
        return jax.random.normal(k, shape, f32) * scale

    u = jax.random.uniform(ks[15], (L, LRU_W), f32, 0.9, 0.999)
    return {
        "x": nrm(ks[0], (BATCH, SEQ, D), 1.0),
        "c": nrm(ks[1], (BATCH, D), 1.0),
        "w_mod": nrm(ks[2], (L, D, N_MOD * D), 0.5 * D ** -0.5),
        "b_mod": nrm(ks[3], (L, N_MOD * D), 0.02),
        "g_pre_mix": 1.0 + nrm(ks[4], (L, D), 0.02),
        "g_post_mix": 1.0 + nrm(ks[5], (L, D), 0.02),
        "w_in": nrm(ks[6], (L, D, N_IN), D ** -0.5),
        "conv_a_w": nrm(ks[7], (L, CONV_A_K, CONV_W), CONV_A_K ** -0.5),
        "conv_a_b": nrm(ks[8], (L, CONV_W), 0.02),
        "w_a_out": nrm(ks[9], (L, CONV_W, D), CONV_W ** -0.5),
        "conv_b_w": nrm(ks[10], (L, CONV_B_K, LRU_W), CONV_B_K ** -0.5),
        "conv_b_b": nrm(ks[11], (L, LRU_W), 0.02),
        "w_gate_r": nrm(ks[12], (L, LRU_HEADS, LRU_BW, LRU_BW), LRU_BW ** -0.5),
        "b_gate_r": nrm(ks[13], (L, LRU_W), 0.02),
        "w_gate_i": nrm(ks[14], (L, LRU_HEADS, LRU_BW, LRU_BW), LRU_BW ** -0.5),
        "b_gate_i": nrm(ks[16], (L, LRU_W), 0.02),
        "lru_lambda": jnp.log(u) - jnp.log1p(-u),
        "w_b_out": nrm(ks[17], (L, LRU_W, D), LRU_W ** -0.5),
        "w_o": nrm(ks[18], (L, D, D), D ** -0.5),
        "g_pre_mlp": 1.0 + nrm(ks[19], (L, D), 0.02),
        "g_post_mlp": 1.0 + nrm(ks[20], (L, D), 0.02),
        "w_mlp_up": nrm(ks[21], (L, D, D_FF), D ** -0.5),
        "w_mlp_down": nrm(ks[22], (L, D_FF, D), D_FF ** -0.5),
    }


def _fwd_reference(x, c, w_mod, b_mod, g_pre_mix, g_post_mix, w_in, conv_a_w, conv_a_b,
              w_a_out, conv_b_w, conv_b_b, w_gate_r, b_gate_r, w_gate_i, b_gate_i,
              lru_lambda, w_b_out, w_o, g_pre_mlp, g_post_mlp, w_mlp_up, w_mlp_down):
    c_act = jax.nn.silu(c)
    for l in range(DEPTH):
        mod = (c_act @ w_mod[l] + b_mod[l])[:, None, :]
        sh_m, sc_m, gt_m, sh_f, sc_f, gt_f = jnp.split(mod, N_MOD, axis=-1)
        h = rms_norm(x, g_pre_mix[l]) * (1.0 + sc_m) + sh_m
        y = hybrid_mixer(h, w_in[l], conv_a_w[l], conv_a_b[l], w_a_out[l],
                         conv_b_w[l], conv_b_b[l], w_gate_r[l], b_gate_r[l],
                         w_gate_i[l], b_gate_i[l], lru_lambda[l], w_b_out[l], w_o[l])
        x = x + gt_m * rms_norm(y, g_post_mix[l])
        h = rms_norm(x, g_pre_mlp[l]) * (1.0 + sc_f) + sh_f
        y = sq_relu_mlp(h, w_mlp_up[l], w_mlp_down[l])
        x = x + gt_f * rms_norm(y, g_post_mlp[l])
    return x


import jax as _jax
import jax.numpy as _jnp

TWIN_FORMAT = 'train_step'
FWD_PARAMS = ['x', 'c', 'w_mod', 'b_mod', 'g_pre_mix', 'g_post_mix', 'w_in', 'conv_a_w', 'conv_a_b', 'w_a_out', 'conv_b_w', 'conv_b_b', 'w_gate_r', 'b_gate_r', 'w_gate_i', 'b_gate_i', 'lru_lambda', 'w_b_out', 'w_o', 'g_pre_mlp', 'g_post_mlp', 'w_mlp_up', 'w_mlp_down']
TWIN_WEIGHTS = ['w_mod', 'b_mod', 'g_pre_mix', 'g_post_mix', 'w_in', 'conv_a_w', 'conv_a_b', 'w_a_out', 'conv_b_w', 'conv_b_b', 'w_gate_r', 'b_gate_r', 'w_gate_i', 'b_gate_i', 'lru_lambda', 'w_b_out', 'w_o', 'g_pre_mlp', 'g_post_mlp', 'w_mlp_up', 'w_mlp_down']
TWIN_DIFF_INPUT = 'x'
TWIN_INPUTS = ['x', 'c', 'w_mod', 'b_mod', 'g_pre_mix', 'g_post_mix', 'w_in', 'conv_a_w', 'conv_a_b', 'w_a_out', 'conv_b_w', 'conv_b_b', 'w_gate_r', 'b_gate_r', 'w_gate_i', 'b_gate_i', 'lru_lambda', 'w_b_out', 'w_o', 'g_pre_mlp', 'g_post_mlp', 'w_mlp_up', 'w_mlp_down', 'loss_target', 'm_w_mod', 'm_b_mod', 'm_g_pre_mix', 'm_g_post_mix', 'm_w_in', 'm_conv_a_w', 'm_conv_a_b', 'm_w_a_out', 'm_conv_b_w', 'm_conv_b_b', 'm_w_gate_r', 'm_b_gate_r', 'm_w_gate_i', 'm_b_gate_i', 'm_lru_lambda', 'm_w_b_out', 'm_w_o', 'm_g_pre_mlp', 'm_g_post_mlp', 'm_w_mlp_up', 'm_w_mlp_down', 'v_w_mod', 'v_b_mod', 'v_g_pre_mix', 'v_g_post_mix', 'v_w_in', 'v_conv_a_w', 'v_conv_a_b', 'v_w_a_out', 'v_conv_b_w', 'v_conv_b_b', 'v_w_gate_r', 'v_b_gate_r', 'v_w_gate_i', 'v_b_gate_i', 'v_lru_lambda', 'v_w_b_out', 'v_w_o', 'v_g_pre_mlp', 'v_g_post_mlp', 'v_w_mlp_up', 'v_w_mlp_down']
TWIN_OUTPUTS = ['loss', 'grad_x', 'grad_w_mod', 'grad_b_mod', 'grad_g_pre_mix', 'grad_g_post_mix', 'grad_w_in', 'grad_conv_a_w', 'grad_conv_a_b', 'grad_w_a_out', 'grad_conv_b_w', 'grad_conv_b_b', 'grad_w_gate_r', 'grad_b_gate_r', 'grad_w_gate_i', 'grad_b_gate_i', 'grad_lru_lambda', 'grad_w_b_out', 'grad_w_o', 'grad_g_pre_mlp', 'grad_g_post_mlp', 'grad_w_mlp_up', 'grad_w_mlp_down', 'delta_w_mod', 'delta_b_mod', 'delta_g_pre_mix', 'delta_g_post_mix', 'delta_w_in', 'delta_conv_a_w', 'delta_conv_a_b', 'delta_w_a_out', 'delta_conv_b_w', 'delta_conv_b_b', 'delta_w_gate_r', 'delta_b_gate_r', 'delta_w_gate_i', 'delta_b_gate_i', 'delta_lru_lambda', 'delta_w_b_out', 'delta_w_o', 'delta_g_pre_mlp', 'delta_g_post_mlp', 'delta_w_mlp_up', 'delta_w_mlp_down', 'new_m_w_mod', 'new_m_b_mod', 'new_m_g_pre_mix', 'new_m_g_post_mix', 'new_m_w_in', 'new_m_conv_a_w', 'new_m_conv_a_b', 'new_m_w_a_out', 'new_m_conv_b_w', 'new_m_conv_b_b', 'new_m_w_gate_r', 'new_m_b_gate_r', 'new_m_w_gate_i', 'new_m_b_gate_i', 'new_m_lru_lambda', 'new_m_w_b_out', 'new_m_w_o', 'new_m_g_pre_mlp', 'new_m_g_post_mlp', 'new_m_w_mlp_up', 'new_m_w_mlp_down', 'new_v_w_mod', 'new_v_b_mod', 'new_v_g_pre_mix', 'new_v_g_post_mix', 'new_v_w_in', 'new_v_conv_a_w', 'new_v_conv_a_b', 'new_v_w_a_out', 'new_v_conv_b_w', 'new_v_conv_b_b', 'new_v_w_gate_r', 'new_v_b_gate_r', 'new_v_w_gate_i', 'new_v_b_gate_i', 'new_v_lru_lambda', 'new_v_w_b_out', 'new_v_w_o', 'new_v_g_pre_mlp', 'new_v_g_post_mlp', 'new_v_w_mlp_up', 'new_v_w_mlp_down']
TWIN_LEAF_KINDS = {'loss': 'loss', 'grad_x': 'grad_x', 'grad_w_mod': 'grad_w', 'grad_b_mod': 'grad_w', 'grad_g_pre_mix': 'grad_w', 'grad_g_post_mix': 'grad_w', 'grad_w_in': 'grad_w', 'grad_conv_a_w': 'grad_w', 'grad_conv_a_b': 'grad_w', 'grad_w_a_out': 'grad_w', 'grad_conv_b_w': 'grad_w', 'grad_conv_b_b': 'grad_w', 'grad_w_gate_r': 'grad_w', 'grad_b_gate_r': 'grad_w', 'grad_w_gate_i': 'grad_w', 'grad_b_gate_i': 'grad_w', 'grad_lru_lambda': 'grad_w', 'grad_w_b_out': 'grad_w', 'grad_w_o': 'grad_w', 'grad_g_pre_mlp': 'grad_w', 'grad_g_post_mlp': 'grad_w', 'grad_w_mlp_up': 'grad_w', 'grad_w_mlp_down': 'grad_w', 'delta_w_mod': 'delta_w', 'delta_b_mod': 'delta_w', 'delta_g_pre_mix': 'delta_w', 'delta_g_post_mix': 'delta_w', 'delta_w_in': 'delta_w', 'delta_conv_a_w': 'delta_w', 'delta_conv_a_b': 'delta_w', 'delta_w_a_out': 'delta_w', 'delta_conv_b_w': 'delta_w', 'delta_conv_b_b': 'delta_w', 'delta_w_gate_r': 'delta_w', 'delta_b_gate_r': 'delta_w', 'delta_w_gate_i': 'delta_w', 'delta_b_gate_i': 'delta_w', 'delta_lru_lambda': 'delta_w', 'delta_w_b_out': 'delta_w', 'delta_w_o': 'delta_w', 'delta_g_pre_mlp': 'delta_w', 'delta_g_post_mlp': 'delta_w', 'delta_w_mlp_up': 'delta_w', 'delta_w_mlp_down': 'delta_w', 'new_m_w_mod': 'new_m', 'new_m_b_mod': 'new_m', 'new_m_g_pre_mix': 'new_m', 'new_m_g_post_mix': 'new_m', 'new_m_w_in': 'new_m', 'new_m_conv_a_w': 'new_m', 'new_m_conv_a_b': 'new_m', 'new_m_w_a_out': 'new_m', 'new_m_conv_b_w': 'new_m', 'new_m_conv_b_b': 'new_m', 'new_m_w_gate_r': 'new_m', 'new_m_b_gate_r': 'new_m', 'new_m_w_gate_i': 'new_m', 'new_m_b_gate_i': 'new_m', 'new_m_lru_lambda': 'new_m', 'new_m_w_b_out': 'new_m', 'new_m_w_o': 'new_m', 'new_m_g_pre_mlp': 'new_m', 'new_m_g_post_mlp': 'new_m', 'new_m_w_mlp_up': 'new_m', 'new_m_w_mlp_down': 'new_m', 'new_v_w_mod': 'new_v', 'new_v_b_mod': 'new_v', 'new_v_g_pre_mix': 'new_v', 'new_v_g_post_mix': 'new_v', 'new_v_w_in': 'new_v', 'new_v_conv_a_w': 'new_v', 'new_v_conv_a_b': 'new_v', 'new_v_w_a_out': 'new_v', 'new_v_conv_b_w': 'new_v', 'new_v_conv_b_b': 'new_v', 'new_v_w_gate_r': 'new_v', 'new_v_b_gate_r': 'new_v', 'new_v_w_gate_i': 'new_v', 'new_v_b_gate_i': 'new_v', 'new_v_lru_lambda': 'new_v', 'new_v_w_b_out': 'new_v', 'new_v_w_o': 'new_v', 'new_v_g_pre_mlp': 'new_v', 'new_v_g_post_mlp': 'new_v', 'new_v_w_mlp_up': 'new_v', 'new_v_w_mlp_down': 'new_v'}


def _forward(args):
    return _fwd_reference(*[args[k] for k in FWD_PARAMS])


def _output_shape():
    out = _jax.eval_shape(lambda: _forward(_fwd_setup_inputs(0)))
    return out.shape, out.dtype

N_MICROBATCH = 1
ADAM_LR = 0.001
ADAM_B1 = 0.9
ADAM_B2 = 0.999
ADAM_EPS = 1e-08
ADAM_WD = 0.01
ADAM_STEP = 10
PER_EXAMPLE_BATCH_AXIS = {'x': 0, 'c': 0, 'loss_target': 0}
SHARED_INPUTS = []
_WEIGHT_DTYPES = {'w_mod': _jnp.float32, 'b_mod': _jnp.float32, 'g_pre_mix': _jnp.float32, 'g_post_mix': _jnp.float32, 'w_in': _jnp.float32, 'conv_a_w': _jnp.float32, 'conv_a_b': _jnp.float32, 'w_a_out': _jnp.float32, 'conv_b_w': _jnp.float32, 'conv_b_b': _jnp.float32, 'w_gate_r': _jnp.float32, 'b_gate_r': _jnp.float32, 'w_gate_i': _jnp.float32, 'b_gate_i': _jnp.float32, 'lru_lambda': _jnp.float32, 'w_b_out': _jnp.float32, 'w_o': _jnp.float32, 'g_pre_mlp': _jnp.float32, 'g_post_mlp': _jnp.float32, 'w_mlp_up': _jnp.float32, 'w_mlp_down': _jnp.float32}
MOMENT_SCALE = {'w_mod': 1.831618e+00, 'b_mod': 3.435227e+00, 'g_pre_mix': 1.640531e-01, 'g_post_mix': 3.863900e+00, 'w_in': 8.260416e-02, 'conv_a_w': 9.134922e-02, 'conv_a_b': 1.961906e-01, 'w_a_out': 9.226499e-02, 'conv_b_w': 2.088770e-01, 'conv_b_b': 6.059457e-01, 'w_gate_r': 1.510000e-02, 'b_gate_r': 3.761067e-02, 'w_gate_i': 3.452431e-02, 'b_gate_i': 8.826279e-02, 'lru_lambda': 1.016918e-01, 'w_b_out': 1.805561e-01, 'w_o': 1.706378e-01, 'g_pre_mlp': 1.443780e-01, 'g_post_mlp': 3.851259e+00, 'w_mlp_up': 1.104829e-01, 'w_mlp_down': 5.073566e-01}


def _to_microbatches(a, axis):
    t = _jnp.moveaxis(a, axis, 0)
    t = t.reshape((N_MICROBATCH, t.shape[0] // N_MICROBATCH) + t.shape[1:])
    return _jnp.moveaxis(t, 1, axis + 1)


def setup_inputs(seed: int = 0) -> dict:
    inp = _fwd_setup_inputs(seed)
    key = _jax.random.fold_in(_jax.random.key(seed), 7919)
    shape, _ = _output_shape()
    out = dict(inp)
    out["loss_target"] = _jax.random.normal(_jax.random.fold_in(key, 0), shape, _jnp.float32)
    for i, name in enumerate(TWIN_WEIGHTS):
        w = inp[name].astype(_jnp.float32)
        if MOMENT_SCALE is None:
            s = _jnp.sqrt(_jnp.mean(_jnp.square(w)) + 1e-30)
        else:
            s = MOMENT_SCALE[name]
        km, kv = _jax.random.split(_jax.random.fold_in(key, i + 1))
        out[name] = w
        out["m_" + name] = s * _jax.random.normal(km, w.shape, _jnp.float32)
        out["v_" + name] = (s * s) * _jax.random.uniform(kv, w.shape, _jnp.float32, 0.5, 1.5)
    if N_MICROBATCH > 1:
        for name, axis in PER_EXAMPLE_BATCH_AXIS.items():
            out[name] = _to_microbatches(out[name], axis)
    return {'x': out['x'], 'c': out['c'], 'w_mod': out['w_mod'], 'b_mod': out['b_mod'], 'g_pre_mix': out['g_pre_mix'], 'g_post_mix': out['g_post_mix'], 'w_in': out['w_in'], 'conv_a_w': out['conv_a_w'], 'conv_a_b': out['conv_a_b'], 'w_a_out': out['w_a_out'], 'conv_b_w': out['conv_b_w'], 'conv_b_b': out['conv_b_b'], 'w_gate_r': out['w_gate_r'], 'b_gate_r': out['b_gate_r'], 'w_gate_i': out['w_gate_i'], 'b_gate_i': out['b_gate_i'], 'lru_lambda': out['lru_lambda'], 'w_b_out': out['w_b_out'], 'w_o': out['w_o'], 'g_pre_mlp': out['g_pre_mlp'], 'g_post_mlp': out['g_post_mlp'], 'w_mlp_up': out['w_mlp_up'], 'w_mlp_down': out['w_mlp_down'], 'loss_target': out['loss_target'], 'm_w_mod': out['m_w_mod'], 'm_b_mod': out['m_b_mod'], 'm_g_pre_mix': out['m_g_pre_mix'], 'm_g_post_mix': out['m_g_post_mix'], 'm_w_in': out['m_w_in'], 'm_conv_a_w': out['m_conv_a_w'], 'm_conv_a_b': out['m_conv_a_b'], 'm_w_a_out': out['m_w_a_out'], 'm_conv_b_w': out['m_conv_b_w'], 'm_conv_b_b': out['m_conv_b_b'], 'm_w_gate_r': out['m_w_gate_r'], 'm_b_gate_r': out['m_b_gate_r'], 'm_w_gate_i': out['m_w_gate_i'], 'm_b_gate_i': out['m_b_gate_i'], 'm_lru_lambda': out['m_lru_lambda'], 'm_w_b_out': out['m_w_b_out'], 'm_w_o': out['m_w_o'], 'm_g_pre_mlp': out['m_g_pre_mlp'], 'm_g_post_mlp': out['m_g_post_mlp'], 'm_w_mlp_up': out['m_w_mlp_up'], 'm_w_mlp_down': out['m_w_mlp_down'], 'v_w_mod': out['v_w_mod'], 'v_b_mod': out['v_b_mod'], 'v_g_pre_mix': out['v_g_pre_mix'], 'v_g_post_mix': out['v_g_post_mix'], 'v_w_in': out['v_w_in'], 'v_conv_a_w': out['v_conv_a_w'], 'v_conv_a_b': out['v_conv_a_b'], 'v_w_a_out': out['v_w_a_out'], 'v_conv_b_w': out['v_conv_b_w'], 'v_conv_b_b': out['v_conv_b_b'], 'v_w_gate_r': out['v_w_gate_r'], 'v_b_gate_r': out['v_b_gate_r'], 'v_w_gate_i': out['v_w_gate_i'], 'v_b_gate_i': out['v_b_gate_i'], 'v_lru_lambda': out['v_lru_lambda'], 'v_w_b_out': out['v_w_b_out'], 'v_w_o': out['v_w_o'], 'v_g_pre_mlp': out['v_g_pre_mlp'], 'v_g_post_mlp': out['v_g_post_mlp'], 'v_w_mlp_up': out['v_w_mlp_up'], 'v_w_mlp_down': out['v_w_mlp_down']}


def _loss(weights, diff, rest, loss_target):
    with _jax.named_scope("forward"):
        args = {**rest, TWIN_DIFF_INPUT: diff, **{k: w.astype(_WEIGHT_DTYPES[k]) for k, w in weights.items()}}
        y = _forward(args)
    with _jax.named_scope("loss_head"):
        err = _jnp.square(y.astype(_jnp.float32) - loss_target)
        return 0.5 * _jnp.sum(_jnp.mean(err, axis=-1)) if err.ndim else 0.5 * err


def _adamw(w, g, m, v):
    m = ADAM_B1 * m + (1.0 - ADAM_B1) * g
    v = ADAM_B2 * v + (1.0 - ADAM_B2) * _jnp.square(g)
    m_hat = m / (1.0 - ADAM_B1 ** ADAM_STEP)
    v_hat = v / (1.0 - ADAM_B2 ** ADAM_STEP)
    delta = -ADAM_LR * (m_hat / (_jnp.sqrt(v_hat) + ADAM_EPS) + ADAM_WD * w)
    return delta, m, v


def reference(x, c, w_mod, b_mod, g_pre_mix, g_post_mix, w_in, conv_a_w, conv_a_b, w_a_out, conv_b_w, conv_b_b, w_gate_r, b_gate_r, w_gate_i, b_gate_i, lru_lambda, w_b_out, w_o, g_pre_mlp, g_post_mlp, w_mlp_up, w_mlp_down, loss_target, m_w_mod, m_b_mod, m_g_pre_mix, m_g_post_mix, m_w_in, m_conv_a_w, m_conv_a_b, m_w_a_out, m_conv_b_w, m_conv_b_b, m_w_gate_r, m_b_gate_r, m_w_gate_i, m_b_gate_i, m_lru_lambda, m_w_b_out, m_w_o, m_g_pre_mlp, m_g_post_mlp, m_w_mlp_up, m_w_mlp_down, v_w_mod, v_b_mod, v_g_pre_mix, v_g_post_mix, v_w_in, v_conv_a_w, v_conv_a_b, v_w_a_out, v_conv_b_w, v_conv_b_b, v_w_gate_r, v_b_gate_r, v_w_gate_i, v_b_gate_i, v_lru_lambda, v_w_b_out, v_w_o, v_g_pre_mlp, v_g_post_mlp, v_w_mlp_up, v_w_mlp_down):
    given = dict(x=x, c=c, w_mod=w_mod, b_mod=b_mod, g_pre_mix=g_pre_mix, g_post_mix=g_post_mix, w_in=w_in, conv_a_w=conv_a_w, conv_a_b=conv_a_b, w_a_out=w_a_out, conv_b_w=conv_b_w, conv_b_b=conv_b_b, w_gate_r=w_gate_r, b_gate_r=b_gate_r, w_gate_i=w_gate_i, b_gate_i=b_gate_i, lru_lambda=lru_lambda, w_b_out=w_b_out, w_o=w_o, g_pre_mlp=g_pre_mlp, g_post_mlp=g_post_mlp, w_mlp_up=w_mlp_up, w_mlp_down=w_mlp_down, loss_target=loss_target, m_w_mod=m_w_mod, m_b_mod=m_b_mod, m_g_pre_mix=m_g_pre_mix, m_g_post_mix=m_g_post_mix, m_w_in=m_w_in, m_conv_a_w=m_conv_a_w, m_conv_a_b=m_conv_a_b, m_w_a_out=m_w_a_out, m_conv_b_w=m_conv_b_w, m_conv_b_b=m_conv_b_b, m_w_gate_r=m_w_gate_r, m_b_gate_r=m_b_gate_r, m_w_gate_i=m_w_gate_i, m_b_gate_i=m_b_gate_i, m_lru_lambda=m_lru_lambda, m_w_b_out=m_w_b_out, m_w_o=m_w_o, m_g_pre_mlp=m_g_pre_mlp, m_g_post_mlp=m_g_post_mlp, m_w_mlp_up=m_w_mlp_up, m_w_mlp_down=m_w_mlp_down, v_w_mod=v_w_mod, v_b_mod=v_b_mod, v_g_pre_mix=v_g_pre_mix, v_g_post_mix=v_g_post_mix, v_w_in=v_w_in, v_conv_a_w=v_conv_a_w, v_conv_a_b=v_conv_a_b, v_w_a_out=v_w_a_out, v_conv_b_w=v_conv_b_w, v_conv_b_b=v_conv_b_b, v_w_gate_r=v_w_gate_r, v_b_gate_r=v_b_gate_r, v_w_gate_i=v_w_gate_i, v_b_gate_i=v_b_gate_i, v_lru_lambda=v_lru_lambda, v_w_b_out=v_w_b_out, v_w_o=v_w_o, v_g_pre_mlp=v_g_pre_mlp, v_g_post_mlp=v_g_post_mlp, v_w_mlp_up=v_w_mlp_up, v_w_mlp_down=v_w_mlp_down)
    weights = {n: given[n] for n in TWIN_WEIGHTS}
    shared = {n: given[n] for n in SHARED_INPUTS}
    per_example = {n: given[n] for n in ['x', 'c']}
    grad_fn = _jax.value_and_grad(_loss, argnums=(0, 1))

    def one_microbatch(ex, loss_target):
        ex = dict(ex)
        diff = ex.pop(TWIN_DIFF_INPUT)
        return grad_fn(weights, diff, {**shared, **ex}, loss_target)

    if N_MICROBATCH == 1:
        loss, (grad_w, grad_x) = one_microbatch(per_example, given["loss_target"])
    else:
        def body(carry, xs):
            loss_sum, grad_sum = carry
            l_k, (gw_k, gx_k) = one_microbatch(xs[0], xs[1])
            with _jax.named_scope("update"):
                return (loss_sum + l_k, _jax.tree.map(_jnp.add, grad_sum, gw_k)), gx_k

        init = (_jnp.zeros((), _jnp.float32), _jax.tree.map(_jnp.zeros_like, weights))
        (loss, grad_w), grad_x = _jax.lax.scan(body, init, (per_example, given["loss_target"]))
    with _jax.named_scope("update"):
        delta_w, new_m, new_v = {}, {}, {}
        for n in TWIN_WEIGHTS:
            delta_w[n], new_m[n], new_v[n] = _adamw(weights[n], grad_w[n], given["m_" + n], given["v_" + n])
    return (loss, grad_x, *[grad_w[n] for n in TWIN_WEIGHTS], *[delta_w[n] for n in TWIN_WEIGHTS],
            *[new_m[n] for n in TWIN_WEIGHTS], *[new_v[n] for n in TWIN_WEIGHTS])
```

```python
import functools

import jax
import jax.numpy as jnp
from jax import lax
from jax.experimental import pallas as pl
from jax.experimental.pallas import tpu as pltpu

F32, BF16 = jnp.float32, jnp.bfloat16
EPS = 1e-6
LRU_C = 8.0
N_DEV = 8
N_MOD = 6
SUBLANES = 8
VMEM_BUDGET = 56 * 1024 * 1024
ADAM_LR, ADAM_B1, ADAM_B2, ADAM_EPS, ADAM_WD, ADAM_STEP = 0.001, 0.9, 0.999, 1e-08, 0.01, 10
MESH = pl.DeviceIdType.MESH
VMEM_SPEC = pl.BlockSpec(memory_space=pltpu.VMEM)
ANY_SPEC = pl.BlockSpec(memory_space=pl.ANY)

R_G_PRE_MIX, R_G_POST_MIX, R_CONV_A_B, R_CONV_B_B, R_B_GATE_R, R_B_GATE_I, R_LAMBDA, R_G_PRE_MLP, R_G_POST_MLP = range(9)
N_ROWS = 16
M_SH_M, M_SC_M, M_GT_M, M_SH_F, M_SC_F, M_GT_F = range(6)
CW_A, CW_B, CW_ROWS = 0, 3, 8
G_MLP_GT, G_MLP_GPOST, G_MLP_SC, G_MLP_SH, G_MLP_GPRE, G_MLP_ROWS = 0, 1, 2, 3, 4, 8
(G_MIX_GT, G_MIX_GPOST, G_MIX_CAB, G_MIX_CAW, G_MIX_CBB, G_MIX_CBW, G_MIX_BR, G_MIX_BI, G_MIX_LAM) = 0, 1, 2, 3, 6, 7, 11, 12, 13
G_MIX_ROWS = 16
G_IN_SC, G_IN_SH, G_IN_GPRE, G_IN_ROWS = 0, 1, 2, 8
G_LAYER_ROWS = G_MLP_ROWS + G_MIX_ROWS + G_IN_ROWS


def _cparams(dims=None, vmem=None):
    kw = {}
    if dims is not None:
        kw["dimension_semantics"] = dims
    if vmem is not None:
        kw["vmem_limit_bytes"] = int(min(max(vmem, 16 * 1024 * 1024), VMEM_BUDGET))
    return pltpu.CompilerParams(**kw)


def _nbytes(shape, dtype):
    n = 1
    for s in shape:
        n *= s
    return n * jnp.dtype(dtype).itemsize


def _resident(block, index_map):
    return pl.BlockSpec(block, index_map, pipeline_mode=pl.Buffered(1))


def _my_position():
    x, y, c = lax.axis_index("x"), lax.axis_index("y"), lax.axis_index("c")
    return (x, y, c), 4 * x + 2 * y + c


def _peer(pos, k):
    x, y, c = pos
    px = 1 - x if k & 4 else x
    py = 1 - y if k & 2 else y
    pc = 1 - c if k & 1 else c
    return (px, py, pc), 4 * px + 2 * py + pc


def _remote(src, dst, ssem, rsem, peer):
    return pltpu.make_async_remote_copy(src_ref=src, dst_ref=dst, send_sem=ssem, recv_sem=rsem, device_id=peer, device_id_type=MESH)


def _dot(a, b):
    return jnp.dot(a, b, preferred_element_type=F32)


def _dot_nt(a, b):
    return lax.dot_general(a, b, (((1,), (1,)), ((), ())), preferred_element_type=F32)


def _dot_tn(a, b):
    return lax.dot_general(a, b, (((0,), (0,)), ((), ())), preferred_element_type=F32)


def _colsum(v):
    return jnp.sum(v, axis=0, keepdims=True)


def _sigmoid(v):
    return jax.nn.sigmoid(v)


def _gelu(v):
    k = 0.7978845608028654
    t = jnp.tanh(k * (v + 0.044715 * (v * v * v)))
    return 0.5 * v * (1.0 + t), t


def _gelu_grad(v, t):
    k = 0.7978845608028654
    return 0.5 * (1.0 + t) + 0.5 * v * (1.0 - t * t) * (k * (1.0 + 3.0 * 0.044715 * v * v))


def _expm1(v):
    u = jnp.exp(v)
    um1 = u - 1.0
    q = um1 * v / jnp.log(u)
    return jnp.where(um1 == 0.0, v, jnp.where(um1 == -1.0, -1.0, q))


def _softplus_neg(lam):
    z = -lam
    u = jnp.exp(-jnp.abs(z))
    w = 1.0 + u
    l1p = jnp.where(w == 1.0, u, jnp.log(w) * u / (w - 1.0))
    return jnp.maximum(z, 0.0) + l1p


def _rms(v):
    return lax.rsqrt(jnp.mean(v * v, axis=-1, keepdims=True) + EPS)


def _prenorm_bwd(xv, dh, g, sc):
    r = _rms(xv)
    xn = xv * r
    n = xn * g
    dsc = _colsum(dh * n)
    dsh = _colsum(dh)
    dn = dh * (1.0 + sc)
    dg = _colsum(dn * xn)
    dxn = dn * g
    dx = r * (dxn - xn * jnp.mean(dxn * xn, axis=-1, keepdims=True))
    return dx, dsc, dsh, dg


def _postnorm_bwd(yv, dout, g, gt):
    r = _rms(yv)
    yn = yv * r
    dgt = _colsum(dout * (yn * g))
    dn = dout * gt
    dg = _colsum(dn * yn)
    dyn = dn * g
    dy = r * (dyn - yn * jnp.mean(dyn * yn, axis=-1, keepdims=True))
    return dy, dgt, dg


def _gates(xc, wr_ref, wi_ref, b_r, b_i, sp, nh, bw):
    xcb = xc.astype(BF16)
    zr = jnp.concatenate([_dot(xcb[:, h * bw:(h + 1) * bw], wr_ref[h]) for h in range(nh)], axis=1) + b_r
    zi = jnp.concatenate([_dot(xcb[:, h * bw:(h + 1) * bw], wi_ref[h]) for h in range(nh)], axis=1) + b_i
    r = _sigmoid(zr)
    ig = _sigmoid(zi)
    la = (-LRU_C * r) * sp
    a = jnp.exp(la)
    mult = jnp.sqrt(-_expm1(2.0 * la))
    return xcb, r, ig, a, mult


def _scan_block(a8, b8, reverse):
    row = lax.broadcasted_iota(jnp.int32, a8.shape, 0)
    for s in (1, 2, 4):
        if reverse:
            keep = row < SUBLANES - s
            a_sh = pltpu.roll(a8, SUBLANES - s, 0)
            b_sh = pltpu.roll(b8, SUBLANES - s, 0)
        else:
            keep = row >= s
            a_sh = pltpu.roll(a8, s, 0)
            b_sh = pltpu.roll(b8, s, 0)
        b8 = b8 + a8 * jnp.where(keep, b_sh, 0.0)
        a8 = a8 * jnp.where(keep, a_sh, 1.0)
    return a8, b8


def _prep_small(c, w_mod, b_mod, cw):
    d = c.shape[1]
    cm = w_mod.shape[2]
    cwid = cw.shape[2]
    nl = w_mod.shape[0]

    def body(c_ref, wm_ref, bm_ref, cw_ref, mod_ref, cact_ref, cwf_ref, cbuf, pbuf, rbuf, ssem, rsem, lsem):
        pos, me = _my_position()
        me8 = pl.multiple_of(me * SUBLANES, SUBLANES)
        cbuf[pl.ds(me8, SUBLANES), :] = jnp.broadcast_to(c_ref[...], (SUBLANES, d))
        own_cw = pltpu.make_async_copy(cw_ref, cwf_ref.at[:, :, pl.ds(me * cwid, cwid)], lsem.at[0])
        own_cw.start()
        first = []
        for k in range(1, N_DEV):
            peer, _ = _peer(pos, k)
            rows = cbuf.at[pl.ds(me8, SUBLANES), :]
            first.append(_remote(rows, rows, ssem.at[0, k - 1], rsem.at[0, k - 1], peer))
            first.append(_remote(cw_ref, cwf_ref.at[:, :, pl.ds(me * cwid, cwid)], ssem.at[1, k - 1], rsem.at[1, k - 1], peer))
        for cp in first:
            cp.start()
        for k in range(1, N_DEV):
            peer, pj = _peer(pos, k)
            pj8 = pl.multiple_of(pj * SUBLANES, SUBLANES)
            rows = cbuf.at[pl.ds(pj8, SUBLANES), :]
            _remote(rows, rows, ssem.at[0, k - 1], rsem.at[0, k - 1], peer).wait_recv()
        cv = cbuf[...]
        cact = cv * _sigmoid(cv)
        cact_ref[...] = cact
        cb = cact.astype(BF16)
        for l in range(nl):
            pbuf[l] = _dot(cb, wm_ref[l].astype(BF16))
        own_p = pltpu.make_async_copy(pbuf.at[:, pl.ds(me8, SUBLANES), :], rbuf.at[me], lsem.at[1])
        own_p.start()
        second = []
        for k in range(1, N_DEV):
            peer, pj = _peer(pos, k)
            pj8 = pl.multiple_of(pj * SUBLANES, SUBLANES)
            second.append(_remote(pbuf.at[:, pl.ds(pj8, SUBLANES), :], rbuf.at[me], ssem.at[2, k - 1], rsem.at[2, k - 1], peer))
        for cp in second:
            cp.start()
        for k in range(1, N_DEV):
            peer, pj = _peer(pos, k)
            _remote(pbuf.at[:, pl.ds(0, SUBLANES), :], rbuf.at[pj], ssem.at[2, k - 1], rsem.at[2, k - 1], peer).wait_recv()
            _remote(cw_ref, cwf_ref.at[:, :, pl.ds(pj * cwid, cwid)], ssem.at[1, k - 1], rsem.at[1, k - 1], peer).wait_recv()
        own_p.wait()
        own_cw.wait()
        for l in range(nl):
            for j in range(N_DEV):
                mod_ref[l:l + 1, j * cm:(j + 1) * cm] = rbuf[j, l, 0:1, :] + bm_ref[l:l + 1, j * cm:(j + 1) * cm]
        for cp in first + second:
            cp.wait_send()

    return pl.pallas_call(
        body,
        name="prep_small",
        out_shape=(
            jax.ShapeDtypeStruct((nl, N_MOD * d), F32),
            jax.ShapeDtypeStruct((N_DEV * SUBLANES, d), F32),
            jax.ShapeDtypeStruct((nl, CW_ROWS, d), F32),
        ),
        in_specs=[VMEM_SPEC] * 4,
        out_specs=(VMEM_SPEC,) * 3,
        scratch_shapes=[
            pltpu.VMEM((N_DEV * SUBLANES, d), F32),
            pltpu.VMEM((nl, N_DEV * SUBLANES, cm), F32),
            pltpu.VMEM((N_DEV, nl, SUBLANES, cm), F32),
            pltpu.SemaphoreType.DMA((3, N_DEV - 1)),
            pltpu.SemaphoreType.DMA((3, N_DEV - 1)),
            pltpu.SemaphoreType.DMA((2,)),
        ],
        compiler_params=_cparams(vmem=3 * _nbytes(w_mod.shape, F32)),
    )(c, w_mod, b_mod, cw)


def _exchange_blocks(parts, gather, name):
    n = len(parts)

    def body(*refs):
        ins, outs = refs[:n], refs[n:2 * n]
        ssem, rsem, lsem = refs[2 * n:]
        pos, me = _my_position()
        local = []
        for t in range(n):
            src = ins[t] if gather else ins[t].at[:, me]
            local.append(pltpu.make_async_copy(src, outs[t].at[:, me], lsem.at[t]))
            local[-1].start()
        sends = []
        for k in range(1, N_DEV):
            peer, pj = _peer(pos, k)
            for t in range(n):
                src = ins[t] if gather else ins[t].at[:, pj]
                sends.append(_remote(src, outs[t].at[:, me], ssem.at[t, k - 1], rsem.at[t, k - 1], peer))
                sends[-1].start()
        for k in range(1, N_DEV):
            peer, pj = _peer(pos, k)
            for t in range(n):
                src = ins[t] if gather else ins[t].at[:, pj]
                _remote(src, outs[t].at[:, pj], ssem.at[t, k - 1], rsem.at[t, k - 1], peer).wait_recv()
        for cp in sends:
            cp.wait_send()
        for cp in local:
            cp.wait()

    def full(p):
        return (p.shape[0], N_DEV) + tuple(p.shape[1:]) if gather else tuple(p.shape)

    return pl.pallas_call(
        body,
        name=name,
        out_shape=tuple(jax.ShapeDtypeStruct(full(p), p.dtype) for p in parts),
        in_specs=[ANY_SPEC] * n,
        out_specs=(ANY_SPEC,) * n,
        scratch_shapes=[
            pltpu.SemaphoreType.DMA((n, N_DEV - 1)),
            pltpu.SemaphoreType.DMA((n, N_DEV - 1)),
            pltpu.SemaphoreType.DMA((n,)),
        ],
    )(*parts)


def _reduce_small(rows, gates, dm8, cact):
    r, d = rows.shape
    ng, bw, _ = gates.shape
    nl = dm8.shape[0]
    cm = dm8.shape[2] // N_DEV

    def body(rows_ref, gates_ref, dm_ref, cact_ref, orow_ref, ogate_ref, owm_ref, gr, gg, dmr, ssem, rsem, lsem):
        pos, me = _my_position()
        me8 = pl.multiple_of(me * SUBLANES, SUBLANES)
        gr[me] = rows_ref[...]
        gg[me] = gates_ref[...].astype(BF16)
        own_dm = pltpu.make_async_copy(dm_ref.at[:, :, pl.ds(me * cm, cm)], dmr.at[:, pl.ds(me8, SUBLANES), :], lsem.at[0])
        own_dm.start()
        sends = []
        for k in range(1, N_DEV):
            peer, pj = _peer(pos, k)
            sends.append(_remote(gr.at[me], gr.at[me], ssem.at[0, k - 1], rsem.at[0, k - 1], peer))
            sends.append(_remote(gg.at[me], gg.at[me], ssem.at[1, k - 1], rsem.at[1, k - 1], peer))
            sends.append(_remote(dm_ref.at[:, :, pl.ds(pj * cm, cm)], dmr.at[:, pl.ds(me8, SUBLANES), :],
                                 ssem.at[2, k - 1], rsem.at[2, k - 1], peer))
        for cp in sends:
            cp.start()
        for k in range(1, N_DEV):
            peer, pj = _peer(pos, k)
            pj8 = pl.multiple_of(pj * SUBLANES, SUBLANES)
            _remote(gr.at[pj], gr.at[pj], ssem.at[0, k - 1], rsem.at[0, k - 1], peer).wait_recv()
            _remote(gg.at[pj], gg.at[pj], ssem.at[1, k - 1], rsem.at[1, k - 1], peer).wait_recv()
            _remote(dm_ref.at[:, :, pl.ds(0, cm)], dmr.at[:, pl.ds(pj8, SUBLANES), :], ssem.at[2, k - 1], rsem.at[2, k - 1], peer).wait_recv()
        own_dm.wait()
        acc = gr[0]
        for j in range(1, N_DEV):
            acc = acc + gr[j]
        orow_ref[...] = acc
        accg = gg[0].astype(F32)
        for j in range(1, N_DEV):
            accg = accg + gg[j].astype(F32)
        ogate_ref[...] = accg
        cb = cact_ref[...].astype(BF16)
        for l in range(nl):
            owm_ref[l] = _dot_tn(cb, dmr[l].astype(BF16))
        for cp in sends:
            cp.wait_send()

    return pl.pallas_call(
        body,
        name="reduce_small",
        out_shape=(
            jax.ShapeDtypeStruct((r, d), F32),
            jax.ShapeDtypeStruct((ng, bw, bw), F32),
            jax.ShapeDtypeStruct((nl, d, cm), F32),
        ),
        in_specs=[VMEM_SPEC] * 4,
        out_specs=(VMEM_SPEC,) * 3,
        scratch_shapes=[
            pltpu.VMEM((N_DEV, r, d), F32),
            pltpu.VMEM((N_DEV, ng, bw, bw), BF16),
            pltpu.VMEM((nl, N_DEV * SUBLANES, cm), F32),
            pltpu.SemaphoreType.DMA((3, N_DEV - 1)),
            pltpu.SemaphoreType.DMA((3, N_DEV - 1)),
            pltpu.SemaphoreType.DMA((1,)),
        ],
        compiler_params=_cparams(vmem=4 * _nbytes((N_DEV, ng, bw, bw), F32) + 4 * _nbytes((N_DEV, r, d), F32)),
    )(rows, gates, dm8, cact)


def _in_proj_fwd(x, mod, rows, win_f, l, tm):
    s, d = x.shape
    nb, _, ci = win_f.shape[1:]

    def body(x_ref, mod_ref, rows_ref, w_ref, proj_ref, h_ref):
        xv = x_ref[...]
        g = rows_ref[R_G_PRE_MIX:R_G_PRE_MIX + 1, :]
        h = (xv * _rms(xv) * g) * (1.0 + mod_ref[M_SC_M:M_SC_M + 1, :]) + mod_ref[M_SH_M:M_SH_M + 1, :]
        hb = h.astype(BF16)
        h_ref[...] = hb
        for j in range(nb):
            proj_ref[:, j * ci:(j + 1) * ci] = _dot(hb, w_ref[j])

    return pl.pallas_call(
        body,
        name="in_proj_fwd",
        grid=(s // tm,),
        in_specs=[
            pl.BlockSpec((tm, d), lambda i: (i, 0)),
            _resident((None, N_MOD, d), lambda i: (l, 0, 0)),
            _resident((None, N_ROWS, d), lambda i: (l, 0, 0)),
            _resident((None, nb, d, ci), lambda i: (l, 0, 0, 0)),
        ],
        out_specs=(pl.BlockSpec((tm, nb * ci), lambda i: (i, 0)), pl.BlockSpec((tm, d), lambda i: (i, 0))),
        out_shape=(jax.ShapeDtypeStruct((s, nb * ci), F32), jax.ShapeDtypeStruct((s, d), BF16)),
        compiler_params=_cparams(("parallel",), _nbytes((nb, d, ci), BF16) + 3 * _nbytes((tm, nb * ci), F32) + 8 * _nbytes((tm, d), F32)),
    )(x, mod, rows, win_f)


def _mixer_core_fwd(proj, x, mod, rows, cwf, wr, wi, wa_f, wb_f, wo_f, l, tm):
    s, d = x.shape
    nh, bw, _ = wr.shape[1:]

    def body(proj_ref, x_ref, mod_ref, rows_ref, cw_ref, wr_ref, wi_ref, wa_ref, wb_ref, wo_ref,
             x1_ref, hs_ref, yap_ref, ybp_ref, y_ref, cvbuf, xbbuf, a_s, b_s, hprev):
        i = pl.program_id(0)

        @pl.when(i == 0)
        def _():
            cvbuf[pl.ds(0, SUBLANES), :] = jnp.zeros((SUBLANES, d), F32)
            xbbuf[pl.ds(0, SUBLANES), :] = jnp.zeros((SUBLANES, d), F32)
            hprev[...] = jnp.zeros((SUBLANES, d), F32)

        def row(r):
            return rows_ref[r:r + 1, :]

        def tap(r):
            return cw_ref[r:r + 1, :]

        ba = proj_ref[:, 0:d]
        cv = proj_ref[:, d:2 * d] * proj_ref[:, 2 * d:3 * d]
        cvbuf[pl.ds(SUBLANES, tm), :] = cv
        conv3 = ((row(R_CONV_A_B) + cvbuf[pl.ds(SUBLANES - 2, tm), :] * tap(CW_A)) + cvbuf[pl.ds(SUBLANES - 1, tm), :] * tap(CW_A + 1)) + cv * tap(CW_A + 2)
        ya = ba * conv3
        cvbuf[pl.ds(0, SUBLANES), :] = cvbuf[pl.ds(tm, SUBLANES), :]
        xb = proj_ref[:, 3 * d:4 * d]
        xbbuf[pl.ds(SUBLANES, tm), :] = xb
        xc = (((row(R_CONV_B_B) + xbbuf[pl.ds(SUBLANES - 3, tm), :] * tap(CW_B)) + xbbuf[pl.ds(SUBLANES - 2, tm), :] * tap(CW_B + 1))
              + xbbuf[pl.ds(SUBLANES - 1, tm), :] * tap(CW_B + 2)) + xb * tap(CW_B + 3)
        xbbuf[pl.ds(0, SUBLANES), :] = xbbuf[pl.ds(tm, SUBLANES), :]
        sp = _softplus_neg(row(R_LAMBDA))
        _, _, ig, a, mult = _gates(xc, wr_ref, wi_ref, row(R_B_GATE_R), row(R_B_GATE_I), sp, nh, bw)
        a_s[...] = a
        b_s[...] = mult * (ig * xc)

        def blk(j, hp):
            o = pl.multiple_of(j * SUBLANES, SUBLANES)
            a8, b8 = _scan_block(a_s[pl.ds(o, SUBLANES), :], b_s[pl.ds(o, SUBLANES), :], reverse=False)
            h8 = b8 + a8 * hp
            hs_ref[pl.ds(o, SUBLANES), :] = h8
            return jnp.broadcast_to(h8[SUBLANES - 1:SUBLANES, :], (SUBLANES, d))

        hprev[...] = lax.fori_loop(0, tm // SUBLANES, blk, hprev[...])
        gel, _ = _gelu(proj_ref[:, 4 * d:5 * d])
        yb = hs_ref[...] * gel
        yap = _dot(ya.astype(BF16), wa_ref[...])
        ybp = _dot(yb.astype(BF16), wb_ref[...])
        yap_ref[...] = yap
        ybp_ref[...] = ybp
        m = _sigmoid(proj_ref[:, 5 * d:6 * d]) * yap + _sigmoid(proj_ref[:, 6 * d:7 * d]) * ybp
        y = _dot(m.astype(BF16), wo_ref[...])
        y_ref[...] = y
        x1_ref[...] = x_ref[...] + mod_ref[M_GT_M:M_GT_M + 1, :] * ((y * _rms(y)) * row(R_G_POST_MIX))

    tile = pl.BlockSpec((tm, d), lambda i: (i, 0))
    return pl.pallas_call(
        body,
        name="mixer_core_fwd",
        grid=(s // tm,),
        in_specs=[
            pl.BlockSpec((tm, 7 * d), lambda i: (i, 0)),
            tile,
            _resident((None, N_MOD, d), lambda i: (l, 0, 0)),
            _resident((None, N_ROWS, d), lambda i: (l, 0, 0)),
            _resident((None, CW_ROWS, d), lambda i: (l, 0, 0)),
            _resident((None, nh, bw, bw), lambda i: (l, 0, 0, 0)),
            _resident((None, nh, bw, bw), lambda i: (l, 0, 0, 0)),
            _resident((None, d, d), lambda i: (l, 0, 0)),
            _resident((None, d, d), lambda i: (l, 0, 0)),
            _resident((None, d, d), lambda i: (l, 0, 0)),
        ],
        out_specs=(tile,) * 5,
        out_shape=(jax.ShapeDtypeStruct((s, d), F32),) * 5,
        scratch_shapes=[
            pltpu.VMEM((tm + SUBLANES, d), F32),
            pltpu.VMEM((tm + SUBLANES, d), F32),
            pltpu.VMEM((tm, d), F32),
            pltpu.VMEM((tm, d), F32),
            pltpu.VMEM((SUBLANES, d), F32),
        ],
        compiler_params=_cparams(("arbitrary",), 3 * _nbytes((d, d), BF16) + 2 * _nbytes((tm, 7 * d), F32) + 40 * _nbytes((tm, d), F32)),
    )(proj, x, mod, rows, cwf, wr, wi, wa_f, wb_f, wo_f)


def _mlp_fwd(x1, mod, rows, wup_f, wdn_f, l, tm):
    s, d = x1.shape
    nb, _, cu = wup_f.shape[1:]
    dff = nb * cu

    def body(x1_ref, mod_ref, rows_ref, wu_ref, wd_ref, x2_ref, ru_ref, y2_ref, h2_ref):
        xv = x1_ref[...]
        g = rows_ref[R_G_PRE_MLP:R_G_PRE_MLP + 1, :]
        h2 = ((xv * _rms(xv) * g) * (1.0 + mod_ref[M_SC_F:M_SC_F + 1, :]) + mod_ref[M_SH_F:M_SH_F + 1, :]).astype(BF16)
        h2_ref[...] = h2
        ru = jnp.concatenate([jnp.maximum(_dot(h2, wu_ref[j]), 0.0) for j in range(nb)], axis=1)
        ru_ref[...] = ru.astype(BF16)
        y2 = _dot((ru * ru).astype(BF16), wd_ref[...])
        y2_ref[...] = y2
        x2_ref[...] = xv + mod_ref[M_GT_F:M_GT_F + 1, :] * ((y2 * _rms(y2)) * rows_ref[R_G_POST_MLP:R_G_POST_MLP + 1, :])

    tile = pl.BlockSpec((tm, d), lambda i: (i, 0))
    wide = pl.BlockSpec((tm, dff), lambda i: (i, 0))
    return pl.pallas_call(
        body,
        name="mlp_fwd",
        grid=(s // tm,),
        in_specs=[
            tile,
            _resident((None, N_MOD, d), lambda i: (l, 0, 0)),
            _resident((None, N_ROWS, d), lambda i: (l, 0, 0)),
            _resident((None, nb, d, cu), lambda i: (l, 0, 0, 0)),
            _resident((None, dff, d), lambda i: (l, 0, 0)),
        ],
        out_specs=(tile, wide, tile, tile),
        out_shape=(jax.ShapeDtypeStruct((s, d), F32), jax.ShapeDtypeStruct((s, dff), BF16),
                   jax.ShapeDtypeStruct((s, d), F32), jax.ShapeDtypeStruct((s, d), BF16)),
        compiler_params=_cparams(("parallel",), 2 * _nbytes((dff, d), BF16) + 5 * _nbytes((tm, dff), F32) + 12 * _nbytes((tm, d), F32)),
    )(x1, mod, rows, wup_f, wdn_f)


def _loss_fwd_bwd(y, target, tm):
    s, d = y.shape

    def body(y_ref, t_ref, loss_ref, dy_ref):
        @pl.when(pl.program_id(0) == 0)
        def _():
            loss_ref[...] = jnp.zeros(loss_ref.shape, F32)

        e = y_ref[...] - t_ref[...]
        dy_ref[...] = e * (1.0 / d)
        loss_ref[...] += 0.5 * jnp.sum(jnp.mean(e * e, axis=-1, keepdims=True), axis=0, keepdims=True)

    tile = pl.BlockSpec((tm, d), lambda i: (i, 0))
    loss, dy = pl.pallas_call(
        body,
        name="loss",
        grid=(s // tm,),
        in_specs=[tile, tile],
        out_specs=(pl.BlockSpec((SUBLANES, 128), lambda i: (0, 0)), tile),
        out_shape=(jax.ShapeDtypeStruct((SUBLANES, 128), F32), jax.ShapeDtypeStruct((s, d), F32)),
        compiler_params=_cparams(("arbitrary",)),
    )(y, target)
    return loss[0, 0], dy


def _mlp_bwd(dx2, x1, y2, ru, mod, rows, wup_f, wdn_f, l, tm):
    s, d = x1.shape
    nb, _, cu = wup_f.shape[1:]
    dff = nb * cu

    def body(dx2_ref, x1_ref, y2_ref, ru_ref, mod_ref, rows_ref, wu_ref, wd_ref, dx1_ref, dy2_ref, dup_ref, act_ref, sm_ref):
        @pl.when(pl.program_id(0) == 0)
        def _():
            sm_ref[...] = jnp.zeros(sm_ref.shape, F32)

        dout = dx2_ref[...]
        dy2, dgt, dgpost = _postnorm_bwd(y2_ref[...], dout, rows_ref[R_G_POST_MLP:R_G_POST_MLP + 1, :], mod_ref[M_GT_F:M_GT_F + 1, :])
        dy2b = dy2.astype(BF16)
        dy2_ref[...] = dy2b
        ruv = ru_ref[...].astype(F32)
        act_ref[...] = (ruv * ruv).astype(BF16)
        dup = (_dot_nt(dy2b, wd_ref[...]) * (2.0 * ruv)).astype(BF16)
        dup_ref[...] = dup
        dh2 = _dot_nt(dup[:, 0:cu], wu_ref[0])
        for j in range(1, nb):
            dh2 = dh2 + _dot_nt(dup[:, j * cu:(j + 1) * cu], wu_ref[j])
        dxn, dsc, dsh, dgpre = _prenorm_bwd(x1_ref[...], dh2, rows_ref[R_G_PRE_MLP:R_G_PRE_MLP + 1, :], mod_ref[M_SC_F:M_SC_F + 1, :])
        dx1_ref[...] = dout + dxn
        for r, v in ((G_MLP_GT, dgt), (G_MLP_GPOST, dgpost), (G_MLP_SC, dsc), (G_MLP_SH, dsh), (G_MLP_GPRE, dgpre)):
            sm_ref[r:r + 1, :] += v

    tile = pl.BlockSpec((tm, d), lambda i: (i, 0))
    wide = pl.BlockSpec((tm, dff), lambda i: (i, 0))
    return pl.pallas_call(
        body,
        name="mlp_bwd",
        grid=(s // tm,),
        in_specs=[
            tile, tile, tile, wide,
            _resident((None, N_MOD, d), lambda i: (l, 0, 0)),
            _resident((None, N_ROWS, d), lambda i: (l, 0, 0)),
            _resident((None, nb, d, cu), lambda i: (l, 0, 0, 0)),
            _resident((None, dff, d), lambda i: (l, 0, 0)),
        ],
        out_specs=(tile, tile, wide, wide, pl.BlockSpec((G_MLP_ROWS, d), lambda i: (0, 0))),
        out_shape=(jax.ShapeDtypeStruct((s, d), F32), jax.ShapeDtypeStruct((s, d), BF16), jax.ShapeDtypeStruct((s, dff), BF16),
                   jax.ShapeDtypeStruct((s, dff), BF16), jax.ShapeDtypeStruct((G_MLP_ROWS, d), F32)),
        compiler_params=_cparams(("arbitrary",), 2 * _nbytes((dff, d), BF16) + 6 * _nbytes((tm, dff), F32) + 16 * _nbytes((tm, d), F32)),
    )(dx2, x1, y2, ru, mod, rows, wup_f, wdn_f)


def _mixer_core_bwd(dx1, y, yap, ybp, hs, proj, mod, rows, cwf, wr, wi, wa_f, wb_f, wo_f, l, tm):
    s, d = dx1.shape
    nh, bw, _ = wr.shape[1:]
    nt = s // tm
    per = tm // SUBLANES

    def body(dx1_ref, y_ref, yap_ref, ybp_ref, hs_ref, hsh_ref, proj_ref, projh_ref, mod_ref, rows_ref, cw_ref,
             wr_ref, wi_ref, wa_ref, wb_ref, wo_ref,
             dproj_ref, dy_ref, m_ref, dyap_ref, dybp_ref, ya_ref, yb_ref, sm_ref, dwg_ref,
             cvbuf, xbbuf, hsbuf, abuf, dcbuf, dxbuf, al_s, dh_s, lam_s, lnext):
        i = pl.program_id(0)
        first_tile = i == nt - 1

        @pl.when(i == 0)
        def _():
            sm_ref[...] = jnp.zeros(sm_ref.shape, F32)
            dwg_ref[...] = jnp.zeros(dwg_ref.shape, F32)
            zero = jnp.zeros((SUBLANES, d), F32)
            abuf[pl.ds(tm, SUBLANES), :] = zero
            dcbuf[pl.ds(tm, SUBLANES), :] = zero
            dxbuf[pl.ds(tm, SUBLANES), :] = zero
            lnext[...] = zero

        def row(r):
            return rows_ref[r:r + 1, :]

        def tap(r):
            return cw_ref[r:r + 1, :]

        def acc(r, v):
            sm_ref[r:r + 1, :] += v

        keep_halo = jnp.where(first_tile, 0.0, 1.0)
        dy, dgt, dgpost = _postnorm_bwd(y_ref[...], dx1_ref[...], row(R_G_POST_MIX), mod_ref[M_GT_M:M_GT_M + 1, :])
        acc(G_MIX_GT, dgt)
        acc(G_MIX_GPOST, dgpost)
        dyb16 = dy.astype(BF16)
        dy_ref[...] = dyb16
        dm = _dot_nt(dyb16, wo_ref[...])
        sa = _sigmoid(proj_ref[:, 5 * d:6 * d])
        sb = _sigmoid(proj_ref[:, 6 * d:7 * d])
        yap = yap_ref[...]
        ybp = ybp_ref[...]
        m_ref[...] = (sa * yap + sb * ybp).astype(BF16)
        dyap = (dm * sa).astype(BF16)
        dybp = (dm * sb).astype(BF16)
        dyap_ref[...] = dyap
        dybp_ref[...] = dybp
        dproj_ref[:, 5 * d:6 * d] = (dm * yap * sa * (1.0 - sa)).astype(BF16)
        dproj_ref[:, 6 * d:7 * d] = (dm * ybp * sb * (1.0 - sb)).astype(BF16)
        dya = _dot_nt(dyap, wa_ref[...])
        dyb = _dot_nt(dybp, wb_ref[...])
        ba = proj_ref[:, 0:d]
        ca = proj_ref[:, d:2 * d]
        va = proj_ref[:, 2 * d:3 * d]
        cv = ca * va
        cvbuf[pl.ds(0, SUBLANES), :] = keep_halo * (projh_ref[:, d:2 * d] * projh_ref[:, 2 * d:3 * d])
        cvbuf[pl.ds(SUBLANES, tm), :] = cv
        cvm2 = cvbuf[pl.ds(SUBLANES - 2, tm), :]
        cvm1 = cvbuf[pl.ds(SUBLANES - 1, tm), :]
        conv3 = ((row(R_CONV_A_B) + cvm2 * tap(CW_A)) + cvm1 * tap(CW_A + 1)) + cv * tap(CW_A + 2)
        ya_ref[...] = (ba * conv3).astype(BF16)
        dproj_ref[:, 0:d] = (dya * conv3).astype(BF16)
        dc3 = dya * ba
        acc(G_MIX_CAB, _colsum(dc3))
        acc(G_MIX_CAW, _colsum(dc3 * cvm2))
        acc(G_MIX_CAW + 1, _colsum(dc3 * cvm1))
        acc(G_MIX_CAW + 2, _colsum(dc3 * cv))
        dcbuf[pl.ds(0, tm), :] = dc3
        dcv = (dc3 * tap(CW_A + 2) + dcbuf[pl.ds(1, tm), :] * tap(CW_A + 1)) + dcbuf[pl.ds(2, tm), :] * tap(CW_A)
        dcbuf[pl.ds(tm, SUBLANES), :] = dcbuf[pl.ds(0, SUBLANES), :]
        dproj_ref[:, d:2 * d] = (dcv * va).astype(BF16)
        dproj_ref[:, 2 * d:3 * d] = (dcv * ca).astype(BF16)
        xb = proj_ref[:, 3 * d:4 * d]
        gb = proj_ref[:, 4 * d:5 * d]
        xbbuf[pl.ds(0, SUBLANES), :] = keep_halo * projh_ref[:, 3 * d:4 * d]
        xbbuf[pl.ds(SUBLANES, tm), :] = xb
        xm3 = xbbuf[pl.ds(SUBLANES - 3, tm), :]
        xm2 = xbbuf[pl.ds(SUBLANES - 2, tm), :]
        xm1 = xbbuf[pl.ds(SUBLANES - 1, tm), :]
        xc = (((row(R_CONV_B_B) + xm3 * tap(CW_B)) + xm2 * tap(CW_B + 1)) + xm1 * tap(CW_B + 2)) + xb * tap(CW_B + 3)
        lam = row(R_LAMBDA)
        sp = _softplus_neg(lam)
        xcb, r, ig, a, mult = _gates(xc, wr_ref, wi_ref, row(R_B_GATE_R), row(R_B_GATE_I), sp, nh, bw)
        gel, th = _gelu(gb)
        hs = hs_ref[...]
        yb_ref[...] = (hs * gel).astype(BF16)
        dproj_ref[:, 4 * d:5 * d] = (dyb * hs * _gelu_grad(gb, th)).astype(BF16)
        abuf[pl.ds(0, tm), :] = a
        al_s[...] = abuf[pl.ds(1, tm), :]
        abuf[pl.ds(tm, SUBLANES), :] = abuf[pl.ds(0, SUBLANES), :]
        dh_s[...] = dyb * gel

        def blk(j, ln):
            o = pl.multiple_of((per - 1 - j) * SUBLANES, SUBLANES)
            a8, b8 = _scan_block(al_s[pl.ds(o, SUBLANES), :], dh_s[pl.ds(o, SUBLANES), :], reverse=True)
            l8 = b8 + a8 * ln
            lam_s[pl.ds(o, SUBLANES), :] = l8
            return jnp.broadcast_to(l8[0:1, :], (SUBLANES, d))

        lnext[...] = lax.fori_loop(0, per, blk, lnext[...])
        dbb = lam_s[...]
        hsbuf[pl.ds(0, SUBLANES), :] = keep_halo * hsh_ref[...]
        hsbuf[pl.ds(SUBLANES, tm), :] = hs
        da = dbb * hsbuf[pl.ds(SUBLANES - 1, tm), :]
        dmult = dbb * (ig * xc)
        dig = dbb * (mult * xc)
        dxc = dbb * (mult * ig)
        dla = da * a - dmult * (a * a) / mult
        acc(G_MIX_LAM, _colsum(dla * (-LRU_C * r)) * (-_sigmoid(-lam)))
        dzr = (dla * (-LRU_C * sp)) * r * (1.0 - r)
        dzi = dig * ig * (1.0 - ig)
        acc(G_MIX_BR, _colsum(dzr))
        acc(G_MIX_BI, _colsum(dzi))
        dzrb = dzr.astype(BF16)
        dzib = dzi.astype(BF16)
        back = []
        for h in range(nh):
            sl = slice(h * bw, (h + 1) * bw)
            back.append(_dot_nt(dzrb[:, sl], wr_ref[h]) + _dot_nt(dzib[:, sl], wi_ref[h]))
            dwg_ref[0, h] += _dot_tn(xcb[:, sl], dzrb[:, sl])
            dwg_ref[1, h] += _dot_tn(xcb[:, sl], dzib[:, sl])
        dxc = dxc + jnp.concatenate(back, axis=1)
        acc(G_MIX_CBB, _colsum(dxc))
        acc(G_MIX_CBW, _colsum(dxc * xm3))
        acc(G_MIX_CBW + 1, _colsum(dxc * xm2))
        acc(G_MIX_CBW + 2, _colsum(dxc * xm1))
        acc(G_MIX_CBW + 3, _colsum(dxc * xb))
        dxbuf[pl.ds(0, tm), :] = dxc
        dxb = ((dxc * tap(CW_B + 3) + dxbuf[pl.ds(1, tm), :] * tap(CW_B + 2)) + dxbuf[pl.ds(2, tm), :] * tap(CW_B + 1)) + dxbuf[pl.ds(3, tm), :] * tap(CW_B)
        dxbuf[pl.ds(tm, SUBLANES), :] = dxbuf[pl.ds(0, SUBLANES), :]
        dproj_ref[:, 3 * d:4 * d] = dxb.astype(BF16)

    def rev(i):
        return (nt - 1 - i, 0)

    def halo(i):
        return (jnp.maximum((nt - 1 - i) * per - 1, 0), 0)

    tile = pl.BlockSpec((tm, d), rev)
    return pl.pallas_call(
        body,
        name="mixer_core_bwd",
        grid=(nt,),
        in_specs=[
            tile, tile, tile, tile, tile,
            pl.BlockSpec((SUBLANES, d), halo),
            pl.BlockSpec((tm, 7 * d), rev),
            pl.BlockSpec((SUBLANES, 7 * d), halo),
            _resident((None, N_MOD, d), lambda i: (l, 0, 0)),
            _resident((None, N_ROWS, d), lambda i: (l, 0, 0)),
            _resident((None, CW_ROWS, d), lambda i: (l, 0, 0)),
            _resident((None, nh, bw, bw), lambda i: (l, 0, 0, 0)),
            _resident((None, nh, bw, bw), lambda i: (l, 0, 0, 0)),
            _resident((None, d, d), lambda i: (l, 0, 0)),
            _resident((None, d, d), lambda i: (l, 0, 0)),
            _resident((None, d, d), lambda i: (l, 0, 0)),
        ],
        out_specs=(pl.BlockSpec((tm, 7 * d), rev),) + (tile,) * 6 + (
            pl.BlockSpec((G_MIX_ROWS, d), lambda i: (0, 0)), pl.BlockSpec((2, nh, bw, bw), lambda i: (0, 0, 0, 0))),
        out_shape=(jax.ShapeDtypeStruct((s, 7 * d), BF16),) + (jax.ShapeDtypeStruct((s, d), BF16),) * 6 + (
            jax.ShapeDtypeStruct((G_MIX_ROWS, d), F32), jax.ShapeDtypeStruct((2, nh, bw, bw), F32)),
        scratch_shapes=[pltpu.VMEM((tm + SUBLANES, d), F32)] * 6 + [pltpu.VMEM((tm, d), F32)] * 3 + [pltpu.VMEM((SUBLANES, d), F32)],
        compiler_params=_cparams(("arbitrary",), 3 * _nbytes((d, d), BF16) + 3 * _nbytes((tm, 7 * d), F32) + 64 * _nbytes((tm, d), F32)),
    )(dx1, y, yap, ybp, hs, hs, proj, proj, mod, rows, cwf, wr, wi, wa_f, wb_f, wo_f)


def _in_proj_bwd(dproj, x, dx1, mod, rows, win_f, l, tm):
    s, d = x.shape
    nb, _, ci = win_f.shape[1:]

    def body(dp_ref, x_ref, dx1_ref, mod_ref, rows_ref, w_ref, dx_ref, sm_ref):
        @pl.when(pl.program_id(0) == 0)
        def _():
            sm_ref[...] = jnp.zeros(sm_ref.shape, F32)

        dh = _dot_nt(dp_ref[:, 0:ci], w_ref[0])
        for j in range(1, nb):
            dh = dh + _dot_nt(dp_ref[:, j * ci:(j + 1) * ci], w_ref[j])
        dxn, dsc, dsh, dg = _prenorm_bwd(x_ref[...], dh, rows_ref[R_G_PRE_MIX:R_G_PRE_MIX + 1, :], mod_ref[M_SC_M:M_SC_M + 1, :])
        dx_ref[...] = dx1_ref[...] + dxn
        for r, v in ((G_IN_SC, dsc), (G_IN_SH, dsh), (G_IN_GPRE, dg)):
            sm_ref[r:r + 1, :] += v

    tile = pl.BlockSpec((tm, d), lambda i: (i, 0))
    return pl.pallas_call(
        body,
        name="in_proj_bwd",
        grid=(s // tm,),
        in_specs=[
            pl.BlockSpec((tm, nb * ci), lambda i: (i, 0)), tile, tile,
            _resident((None, N_MOD, d), lambda i: (l, 0, 0)),
            _resident((None, N_ROWS, d), lambda i: (l, 0, 0)),
            _resident((None, nb, d, ci), lambda i: (l, 0, 0, 0)),
        ],
        out_specs=(tile, pl.BlockSpec((G_IN_ROWS, d), lambda i: (0, 0))),
        out_shape=(jax.ShapeDtypeStruct((s, d), F32), jax.ShapeDtypeStruct((G_IN_ROWS, d), F32)),
        compiler_params=_cparams(("arbitrary",), _nbytes((nb, d, ci), BF16) + 2 * _nbytes((tm, nb * ci), BF16) + 16 * _nbytes((tm, d), F32)),
    )(dproj, x, dx1, mod, rows, win_f)


def _wgrad(a, b, l, into, cols_owned, ts):
    s, k1 = a.shape
    k2 = b.shape[1]
    ns = s // ts
    if cols_owned:
        bk1, bk2 = k1, k2 // N_DEV
        a_spec = pl.BlockSpec((ts, bk1), lambda j, t: (t, 0))
        b_spec = pl.BlockSpec((ts, bk2), lambda j, t: (t, j))
    else:
        bk1, bk2 = k1 // N_DEV, k2
        a_spec = pl.BlockSpec((ts, bk1), lambda j, t: (t, j))
        b_spec = pl.BlockSpec((ts, bk2), lambda j, t: (t, 0))
    shape = (2, N_DEV, bk1, bk2)

    def body(*refs):
        a_ref, b_ref, o_ref, acc_ref = refs[0], refs[1], refs[-2], refs[-1]
        t = pl.program_id(1)

        @pl.when(t == 0)
        def _():
            acc_ref[...] = jnp.zeros(acc_ref.shape, F32)

        acc_ref[...] += _dot_tn(a_ref[...], b_ref[...])

        @pl.when(t == ns - 1)
        def _():
            o_ref[...] = acc_ref[...].astype(BF16)

    o_spec = pl.BlockSpec((None, None, bk1, bk2), lambda j, t: (l, j, 0, 0))
    in_specs = [a_spec, b_spec]
    args = [a, b]
    aliases = {}
    if into is not None:
        in_specs.append(ANY_SPEC)
        args.append(into)
        aliases = {2: 0}
    return pl.pallas_call(
        body,
        name="wgrad",
        grid=(N_DEV, ns),
        in_specs=in_specs,
        out_specs=o_spec,
        out_shape=jax.ShapeDtypeStruct(shape, BF16),
        scratch_shapes=[pltpu.VMEM((bk1, bk2), F32)],
        input_output_aliases=aliases,
        compiler_params=_cparams(("parallel", "arbitrary"), 4 * _nbytes((bk1, bk2), F32) + 4 * _nbytes((ts, bk1 + bk2), BF16)),
    )(*args)


def _adam_update(w, g, m, v):
    m = ADAM_B1 * m + (1.0 - ADAM_B1) * g
    v = ADAM_B2 * v + (1.0 - ADAM_B2) * (g * g)
    m_hat = m / (1.0 - ADAM_B1 ** ADAM_STEP)
    v_hat = v / (1.0 - ADAM_B2 ** ADAM_STEP)
    delta = -ADAM_LR * (m_hat / (jnp.sqrt(v_hat) + ADAM_EPS) + ADAM_WD * w)
    return delta, m, v


def _sum_adamw(recv, w, m, v, tr):
    nl, _, ra, cb = recv.shape

    def body(r_ref, w_ref, m_ref, v_ref, g_ref, d_ref, nm_ref, nv_ref):
        g = r_ref[0].astype(F32)
        for j in range(1, N_DEV):
            g = g + r_ref[j].astype(F32)
        g_ref[...] = g
        d_ref[...], nm_ref[...], nv_ref[...] = _adam_update(w_ref[...], g, m_ref[...], v_ref[...])

    blk = pl.BlockSpec((None, tr, cb), lambda l, i: (l, i, 0))
    return pl.pallas_call(
        body,
        name="sum_adamw",
        grid=(nl, ra // tr),
        in_specs=[pl.BlockSpec((None, N_DEV, tr, cb), lambda l, i: (l, 0, i, 0)), blk, blk, blk],
        out_specs=(blk,) * 4,
        out_shape=(jax.ShapeDtypeStruct((nl, ra, cb), F32),) * 4,
        compiler_params=_cparams(("parallel", "parallel")),
    )(recv, w, m, v)


def _adamw(w, g, m, v):
    def body(w_ref, g_ref, m_ref, v_ref, d_ref, nm_ref, nv_ref):
        d_ref[...], nm_ref[...], nv_ref[...] = _adam_update(w_ref[...], g_ref[...], m_ref[...], v_ref[...])

    return pl.pallas_call(
        body,
        name="adamw",
        in_specs=[VMEM_SPEC] * 4,
        out_specs=(VMEM_SPEC,) * 3,
        out_shape=(jax.ShapeDtypeStruct(w.shape, F32),) * 3,
        compiler_params=_cparams(vmem=10 * _nbytes(w.shape, F32)),
    )(w, g, m, v)


def _adamw_tiled(w, g, m, v, tr):
    nl, ra, cb = w.shape

    def body(w_ref, g_ref, m_ref, v_ref, d_ref, nm_ref, nv_ref):
        d_ref[...], nm_ref[...], nv_ref[...] = _adam_update(w_ref[...], g_ref[...], m_ref[...], v_ref[...])

    blk = pl.BlockSpec((None, tr, cb), lambda l, i: (l, i, 0))
    return pl.pallas_call(
        body,
        name="adamw_tiled",
        grid=(nl, ra // tr),
        in_specs=[blk] * 4,
        out_specs=(blk,) * 3,
        out_shape=(jax.ShapeDtypeStruct((nl, ra, cb), F32),) * 3,
        compiler_params=_cparams(("parallel", "parallel")),
    )(w, g, m, v)


def _token_tile(s):
    return min(256, max(SUBLANES * 2, s // 4))


def kernel(x, c, w_mod, b_mod, g_pre_mix, g_post_mix, w_in, conv_a_w, conv_a_b, w_a_out, conv_b_w, conv_b_b, w_gate_r, b_gate_r, w_gate_i, b_gate_i, lru_lambda, w_b_out, w_o, g_pre_mlp, g_post_mlp, w_mlp_up, w_mlp_down, loss_target, m_w_mod, m_b_mod, m_g_pre_mix, m_g_post_mix, m_w_in, m_conv_a_w, m_conv_a_b, m_w_a_out, m_conv_b_w, m_conv_b_b, m_w_gate_r, m_b_gate_r, m_w_gate_i, m_b_gate_i, m_lru_lambda, m_w_b_out, m_w_o, m_g_pre_mlp, m_g_post_mlp, m_w_mlp_up, m_w_mlp_down, v_w_mod, v_b_mod, v_g_pre_mix, v_g_post_mix, v_w_in, v_conv_a_w, v_conv_a_b, v_w_a_out, v_conv_b_w, v_conv_b_b, v_w_gate_r, v_b_gate_r, v_w_gate_i, v_b_gate_i, v_lru_lambda, v_w_b_out, v_w_o, v_g_pre_mlp, v_g_post_mlp, v_w_mlp_up, v_w_mlp_down):
    nl = w_mod.shape[0]
    s, d = x.shape[1], x.shape[2]
    nh, bw = w_gate_r.shape[1], w_gate_r.shape[2]
    cwid = conv_a_w.shape[2]
    tm = _token_tile(s)
    ts = min(512, s)
    _, me = _my_position()
    xs = x.reshape(s, d)
    target = loss_target.reshape(s, d)

    vec_names = (g_pre_mix, g_post_mix, conv_a_b, conv_b_b, b_gate_r, b_gate_i, lru_lambda, g_pre_mlp, g_post_mlp)
    rows = jnp.concatenate([jnp.stack(vec_names, axis=1), jnp.zeros((nl, N_ROWS - len(vec_names), d), F32)], axis=1)
    cw = jnp.concatenate([conv_a_w, conv_b_w, jnp.zeros((nl, CW_ROWS - 7, cwid), F32)], axis=1)

    mod, cact, cwf = _prep_small(c, w_mod, b_mod, cw)
    mod = mod.reshape(nl, N_MOD, d)
    win_f, wup_f, wa_f, wb_f, wo_f, wdn_f = _exchange_blocks(
        [w.astype(BF16) for w in (w_in, w_mlp_up, w_a_out, w_b_out, w_o, w_mlp_down)], gather=True, name="gather_weights")
    wa_f, wb_f, wo_f = (w.reshape(nl, d, d) for w in (wa_f, wb_f, wo_f))
    wdn_f = wdn_f.reshape(nl, -1, d)
    wr = w_gate_r.astype(BF16)
    wi = w_gate_i.astype(BF16)

    saved = []
    xin = xs
    for l in range(nl):
        proj, h = _in_proj_fwd(xin, mod, rows, win_f, l, tm)
        x1, hs, yap, ybp, y = _mixer_core_fwd(proj, xin, mod, rows, cwf, wr, wi, wa_f, wb_f, wo_f, l, tm)
        x2, ru, y2, h2 = _mlp_fwd(x1, mod, rows, wup_f, wdn_f, l, tm)
        saved.append((xin, proj, h, x1, hs, yap, ybp, y, ru, y2, h2))
        xin = x2
    loss_part, dx = _loss_fwd_bwd(xin, target, tm)
    loss = lax.psum(loss_part, ("x", "y", "c"))

    parts = [None] * 6
    small = [None] * nl
    gate_parts = [None] * nl
    for l in reversed(range(nl)):
        xin, proj, h, x1, hs, yap, ybp, y, ru, y2, h2 = saved[l]
        dx1, dy2, dup, act, sm_mlp = _mlp_bwd(dx, x1, y2, ru, mod, rows, wup_f, wdn_f, l, tm)
        parts[1] = _wgrad(h2, dup, l, parts[1], True, ts)
        parts[5] = _wgrad(act, dy2, l, parts[5], False, ts)
        dproj, dy, m, dyap, dybp, ya, yb, sm_mix, dwg = _mixer_core_bwd(dx1, y, yap, ybp, hs, proj, mod, rows, cwf, wr, wi, wa_f, wb_f, wo_f, l, tm // 2)
        parts[2] = _wgrad(ya, dyap, l, parts[2], False, ts)
        parts[3] = _wgrad(yb, dybp, l, parts[3], False, ts)
        parts[4] = _wgrad(m, dy, l, parts[4], False, ts)
        dx, sm_in = _in_proj_bwd(dproj, xin, dx1, mod, rows, win_f, l, tm)
        parts[0] = _wgrad(h, dproj, l, parts[0], True, ts)
        small[l] = jnp.concatenate([sm_mlp, sm_mix, sm_in], axis=0)
        gate_parts[l] = dwg
    grad_x = dx.reshape(x.shape)

    recv = _exchange_blocks(parts, gather=False, name="scatter_grads")
    big = {}
    for name, r, w, m_, v_ in (("w_in", recv[0], w_in, m_w_in, v_w_in), ("w_mlp_up", recv[1], w_mlp_up, m_w_mlp_up, v_w_mlp_up),
                               ("w_a_out", recv[2], w_a_out, m_w_a_out, v_w_a_out), ("w_b_out", recv[3], w_b_out, m_w_b_out, v_w_b_out),
                               ("w_o", recv[4], w_o, m_w_o, v_w_o), ("w_mlp_down", recv[5], w_mlp_down, m_w_mlp_down, v_w_mlp_down)):
        big[name] = _sum_adamw(r, w, m_, v_, min(128, w.shape[1]))

    lrows = jnp.concatenate(small, axis=0)
    gates_p = jnp.concatenate(gate_parts, axis=0).reshape(nl * 2 * nh, bw, bw)

    def lrow(a, l, r):
        return a[l * G_LAYER_ROWS + r]

    dm = jnp.stack([jnp.concatenate([lrow(lrows, l, G_MLP_ROWS + G_MIX_ROWS + G_IN_SH), lrow(lrows, l, G_MLP_ROWS + G_MIX_ROWS + G_IN_SC),
                                     lrow(lrows, l, G_MLP_ROWS + G_MIX_GT), lrow(lrows, l, G_MLP_SH), lrow(lrows, l, G_MLP_SC),
                                     lrow(lrows, l, G_MLP_GT)]) for l in range(nl)])
    dm8 = jnp.concatenate([dm[:, None, :], jnp.zeros((nl, SUBLANES - 1, N_MOD * d), F32)], axis=1)
    srows, sgates, g_w_mod = _reduce_small(lrows, gates_p, dm8, cact)
    sgates = sgates.reshape(nl, 2, nh, bw, bw)

    def srow(l, r):
        return lrow(srows, l, r)

    def per_layer(r):
        return jnp.stack([srow(l, r) for l in range(nl)])

    mix0 = G_MLP_ROWS
    in0 = G_MLP_ROWS + G_MIX_ROWS
    g_b_mod = jnp.stack([jnp.concatenate([srow(l, in0 + G_IN_SH), srow(l, in0 + G_IN_SC), srow(l, mix0 + G_MIX_GT),
                                          srow(l, G_MLP_SH), srow(l, G_MLP_SC), srow(l, G_MLP_GT)]) for l in range(nl)])
    conv_a_full = jnp.stack([jnp.stack([srow(l, mix0 + G_MIX_CAW + k) for k in range(3)]) for l in range(nl)])
    conv_b_full = jnp.stack([jnp.stack([srow(l, mix0 + G_MIX_CBW + k) for k in range(4)]) for l in range(nl)])
    grads = {
        "b_mod": g_b_mod,
        "g_pre_mix": per_layer(in0 + G_IN_GPRE),
        "g_post_mix": per_layer(mix0 + G_MIX_GPOST),
        "conv_a_w": lax.dynamic_slice_in_dim(conv_a_full, me * cwid, cwid, axis=2),
        "conv_a_b": per_layer(mix0 + G_MIX_CAB),
        "conv_b_w": lax.dynamic_slice_in_dim(conv_b_full, me * cwid, cwid, axis=2),
        "conv_b_b": per_layer(mix0 + G_MIX_CBB),
        "w_gate_r": sgates[:, 0],
        "b_gate_r": per_layer(mix0 + G_MIX_BR),
        "w_gate_i": sgates[:, 1],
        "b_gate_i": per_layer(mix0 + G_MIX_BI),
        "lru_lambda": per_layer(mix0 + G_MIX_LAM),
        "g_pre_mlp": per_layer(G_MLP_GPRE),
        "g_post_mlp": per_layer(G_MLP_GPOST),
    }
    params = {
        "b_mod": (b_mod, m_b_mod, v_b_mod), "g_pre_mix": (g_pre_mix, m_g_pre_mix, v_g_pre_mix), "g_post_mix": (g_post_mix, m_g_post_mix, v_g_post_mix),
        "conv_a_w": (conv_a_w, m_conv_a_w, v_conv_a_w), "conv_a_b": (conv_a_b, m_conv_a_b, v_conv_a_b),
        "conv_b_w": (conv_b_w, m_conv_b_w, v_conv_b_w), "conv_b_b": (conv_b_b, m_conv_b_b, v_conv_b_b),
        "w_gate_r": (w_gate_r, m_w_gate_r, v_w_gate_r), "b_gate_r": (b_gate_r, m_b_gate_r, v_b_gate_r),
        "w_gate_i": (w_gate_i, m_w_gate_i, v_w_gate_i), "b_gate_i": (b_gate_i, m_b_gate_i, v_b_gate_i),
        "lru_lambda": (lru_lambda, m_lru_lambda, v_lru_lambda), "g_pre_mlp": (g_pre_mlp, m_g_pre_mlp, v_g_pre_mlp),
        "g_post_mlp": (g_post_mlp, m_g_post_mlp, v_g_post_mlp),
    }
    out = {}
    for name, g in grads.items():
        w, m_, v_ = params[name]
        flat = (-1, w.shape[-1])
        dl, nm, nv = _adamw(w.reshape(flat), g.reshape(flat), m_.reshape(flat), v_.reshape(flat))
        out[name] = (g.reshape(w.shape), dl.reshape(w.shape), nm.reshape(w.shape), nv.reshape(w.shape))
    out["w_mod"] = (g_w_mod,) + tuple(_adamw_tiled(w_mod, g_w_mod, m_w_mod, v_w_mod, min(128, d)))
    out.update(big)

    order = ("w_mod", "b_mod", "g_pre_mix", "g_post_mix", "w_in", "conv_a_w", "conv_a_b", "w_a_out", "conv_b_w", "conv_b_b", "w_gate_r", "b_gate_r",
             "w_gate_i", "b_gate_i", "lru_lambda", "w_b_out", "w_o", "g_pre_mlp", "g_post_mlp", "w_mlp_up", "w_mlp_down")
    return (loss, grad_x) + tuple(out[n][0] for n in order) + tuple(out[n][1] for n in order) + tuple(out[n][2] for n in order) + tuple(out[n][3] for n in order)
```

```python
import functools

import jax
import jax.numpy as jnp
from jax import lax
from jax.experimental import pallas as pl
from jax.experimental.pallas import tpu as pltpu

F32, BF16 = jnp.float32, jnp.bfloat16
EPS = 1e-6
LRU_C = 8.0
N_DEV = 8
N_MOD = 6
SUBLANES = 8
VMEM_BUDGET = 56 * 1024 * 1024
ADAM_LR, ADAM_B1, ADAM_B2, ADAM_EPS, ADAM_WD, ADAM_STEP = 0.001, 0.9, 0.999, 1e-08, 0.01, 10
MESH = pl.DeviceIdType.MESH
VMEM_SPEC = pl.BlockSpec(memory_space=pltpu.VMEM)
ANY_SPEC = pl.BlockSpec(memory_space=pl.ANY)
HBM_SPEC = pl.BlockSpec(memory_space=pltpu.HBM)
SEM_SPEC = pl.BlockSpec(memory_space=pltpu.SEMAPHORE)
SIDE_EFFECT = pltpu.SideEffectType.DATAFLOW_SIDE_EFFECTING

R_G_PRE_MIX, R_G_POST_MIX, R_CONV_A_B, R_CONV_B_B, R_B_GATE_R, R_B_GATE_I, R_LAMBDA, R_G_PRE_MLP, R_G_POST_MLP = range(9)
N_ROWS = 16
M_SH_M, M_SC_M, M_GT_M, M_SH_F, M_SC_F, M_GT_F = range(6)
CW_A, CW_B, CW_ROWS = 0, 3, 8
G_MLP_GT, G_MLP_GPOST, G_MLP_SC, G_MLP_SH, G_MLP_GPRE, G_MLP_ROWS = 0, 1, 2, 3, 4, 8
(G_MIX_GT, G_MIX_GPOST, G_MIX_CAB, G_MIX_CAW, G_MIX_CBB, G_MIX_CBW, G_MIX_BR, G_MIX_BI, G_MIX_LAM) = 0, 1, 2, 3, 6, 7, 11, 12, 13
G_MIX_ROWS = 16
G_IN_SC, G_IN_SH, G_IN_GPRE, G_IN_ROWS = 0, 1, 2, 8
G_LAYER_ROWS = G_MLP_ROWS + G_MIX_ROWS + G_IN_ROWS


def _cparams(dims=None, vmem=None):
    kw = {}
    if dims is not None:
        kw["dimension_semantics"] = dims
    if vmem is not None:
        kw["vmem_limit_bytes"] = int(min(max(vmem, 16 * 1024 * 1024), VMEM_BUDGET))
    return pltpu.CompilerParams(**kw)


def _nbytes(shape, dtype):
    n = 1
    for s in shape:
        n *= s
    return n * jnp.dtype(dtype).itemsize


def _resident(block, index_map):
    return pl.BlockSpec(block, index_map, pipeline_mode=pl.Buffered(1))


def _my_position():
    x, y, c = lax.axis_index("x"), lax.axis_index("y"), lax.axis_index("c")
    return (x, y, c), 4 * x + 2 * y + c


def _peer(pos, k):
    x, y, c = pos
    px = 1 - x if k & 4 else x
    py = 1 - y if k & 2 else y
    pc = 1 - c if k & 1 else c
    return (px, py, pc), 4 * px + 2 * py + pc


def _remote(src, dst, ssem, rsem, peer):
    return pltpu.make_async_remote_copy(src_ref=src, dst_ref=dst, send_sem=ssem, recv_sem=rsem, device_id=peer, device_id_type=MESH)


def _dot(a, b):
    return jnp.dot(a, b, preferred_element_type=F32)


def _dot_nt(a, b):
    return lax.dot_general(a, b, (((1,), (1,)), ((), ())), preferred_element_type=F32)


def _dot_tn(a, b):
    return lax.dot_general(a, b, (((0,), (0,)), ((), ())), preferred_element_type=F32)


def _colsum(v):
    return jnp.sum(v, axis=0, keepdims=True)


def _sigmoid(v):
    return jax.nn.sigmoid(v)


def _gelu(v):
    k = 0.7978845608028654
    t = jnp.tanh(k * (v + 0.044715 * (v * v * v)))
    return 0.5 * v * (1.0 + t), t


def _gelu_grad(v, t):
    k = 0.7978845608028654
    return 0.5 * (1.0 + t) + 0.5 * v * (1.0 - t * t) * (k * (1.0 + 3.0 * 0.044715 * v * v))


def _expm1(v):
    u = jnp.exp(v)
    um1 = u - 1.0
    q = um1 * v / jnp.log(u)
    return jnp.where(um1 == 0.0, v, jnp.where(um1 == -1.0, -1.0, q))


def _softplus_neg(lam):
    z = -lam
    u = jnp.exp(-jnp.abs(z))
    w = 1.0 + u
    l1p = jnp.where(w == 1.0, u, jnp.log(w) * u / (w - 1.0))
    return jnp.maximum(z, 0.0) + l1p


def _rms(v):
    return lax.rsqrt(jnp.mean(v * v, axis=-1, keepdims=True) + EPS)


def _prenorm_bwd(xv, dh, g, sc):
    r = _rms(xv)
    xn = xv * r
    n = xn * g
    dsc = _colsum(dh * n)
    dsh = _colsum(dh)
    dn = dh * (1.0 + sc)
    dg = _colsum(dn * xn)
    dxn = dn * g
    dx = r * (dxn - xn * jnp.mean(dxn * xn, axis=-1, keepdims=True))
    return dx, dsc, dsh, dg


def _postnorm_bwd(yv, dout, g, gt):
    r = _rms(yv)
    yn = yv * r
    dgt = _colsum(dout * (yn * g))
    dn = dout * gt
    dg = _colsum(dn * yn)
    dyn = dn * g
    dy = r * (dyn - yn * jnp.mean(dyn * yn, axis=-1, keepdims=True))
    return dy, dgt, dg


def _gates(xc, wr_ref, wi_ref, b_r, b_i, sp, nh, bw):
    xcb = xc.astype(BF16)
    zr = jnp.concatenate([_dot(xcb[:, h * bw:(h + 1) * bw], wr_ref[h]) for h in range(nh)], axis=1) + b_r
    zi = jnp.concatenate([_dot(xcb[:, h * bw:(h + 1) * bw], wi_ref[h]) for h in range(nh)], axis=1) + b_i
    r = _sigmoid(zr)
    ig = _sigmoid(zi)
    la = (-LRU_C * r) * sp
    a = jnp.exp(la)
    mult = jnp.sqrt(-_expm1(2.0 * la))
    return xcb, r, ig, a, mult


def _scan_block(a8, b8, reverse):
    row = lax.broadcasted_iota(jnp.int32, a8.shape, 0)
    for s in (1, 2, 4):
        if reverse:
            keep = row < SUBLANES - s
            a_sh = pltpu.roll(a8, SUBLANES - s, 0)
            b_sh = pltpu.roll(b8, SUBLANES - s, 0)
        else:
            keep = row >= s
            a_sh = pltpu.roll(a8, s, 0)
            b_sh = pltpu.roll(b8, s, 0)
        b8 = b8 + a8 * jnp.where(keep, b_sh, 0.0)
        a8 = a8 * jnp.where(keep, a_sh, 1.0)
    return a8, b8


def _prep_small(c, w_mod, b_mod, cw):
    d = c.shape[1]
    cm = w_mod.shape[2]
    cwid = cw.shape[2]
    nl = w_mod.shape[0]

    def body(c_ref, wm_ref, bm_ref, cw_ref, mod_ref, cact_ref, cwf_ref, cbuf, pbuf, rbuf, ssem, rsem, lsem):
        pos, me = _my_position()
        me8 = pl.multiple_of(me * SUBLANES, SUBLANES)
        cbuf[pl.ds(me8, SUBLANES), :] = jnp.broadcast_to(c_ref[...], (SUBLANES, d))
        own_cw = pltpu.make_async_copy(cw_ref, cwf_ref.at[:, :, pl.ds(me * cwid, cwid)], lsem.at[0])
        own_cw.start()
        first = []
        for k in range(1, N_DEV):
            peer, _ = _peer(pos, k)
            rows = cbuf.at[pl.ds(me8, SUBLANES), :]
            first.append(_remote(rows, rows, ssem.at[0, k - 1], rsem.at[0, k - 1], peer))
            first.append(_remote(cw_ref, cwf_ref.at[:, :, pl.ds(me * cwid, cwid)], ssem.at[1, k - 1], rsem.at[1, k - 1], peer))
        for cp in first:
            cp.start()
        for k in range(1, N_DEV):
            peer, pj = _peer(pos, k)
            pj8 = pl.multiple_of(pj * SUBLANES, SUBLANES)
            rows = cbuf.at[pl.ds(pj8, SUBLANES), :]
            _remote(rows, rows, ssem.at[0, k - 1], rsem.at[0, k - 1], peer).wait_recv()
        cv = cbuf[...]
        cact = cv * _sigmoid(cv)
        cact_ref[...] = cact
        cb = cact.astype(BF16)
        for l in range(nl):
            pbuf[l] = _dot(cb, wm_ref[l].astype(BF16))
        own_p = pltpu.make_async_copy(pbuf.at[:, pl.ds(me8, SUBLANES), :], rbuf.at[me], lsem.at[1])
        own_p.start()
        second = []
        for k in range(1, N_DEV):
            peer, pj = _peer(pos, k)
            pj8 = pl.multiple_of(pj * SUBLANES, SUBLANES)
            second.append(_remote(pbuf.at[:, pl.ds(pj8, SUBLANES), :], rbuf.at[me], ssem.at[2, k - 1], rsem.at[2, k - 1], peer))
        for cp in second:
            cp.start()
        for k in range(1, N_DEV):
            peer, pj = _peer(pos, k)
            _remote(pbuf.at[:, pl.ds(0, SUBLANES), :], rbuf.at[pj], ssem.at[2, k - 1], rsem.at[2, k - 1], peer).wait_recv()
            _remote(cw_ref, cwf_ref.at[:, :, pl.ds(pj * cwid, cwid)], ssem.at[1, k - 1], rsem.at[1, k - 1], peer).wait_recv()
        own_p.wait()
        own_cw.wait()
        for l in range(nl):
            for j in range(N_DEV):
                mod_ref[l:l + 1, j * cm:(j + 1) * cm] = rbuf[j, l, 0:1, :] + bm_ref[l:l + 1, j * cm:(j + 1) * cm]
        for cp in first + second:
            cp.wait_send()

    return pl.pallas_call(
        body,
        name="prep_small",
        out_shape=(
            jax.ShapeDtypeStruct((nl, N_MOD * d), F32),
            jax.ShapeDtypeStruct((N_DEV * SUBLANES, d), F32),
            jax.ShapeDtypeStruct((nl, CW_ROWS, d), F32),
        ),
        in_specs=[VMEM_SPEC] * 4,
        out_specs=(VMEM_SPEC,) * 3,
        scratch_shapes=[
            pltpu.VMEM((N_DEV * SUBLANES, d), F32),
            pltpu.VMEM((nl, N_DEV * SUBLANES, cm), F32),
            pltpu.VMEM((N_DEV, nl, SUBLANES, cm), F32),
            pltpu.SemaphoreType.DMA((3, N_DEV - 1)),
            pltpu.SemaphoreType.DMA((3, N_DEV - 1)),
            pltpu.SemaphoreType.DMA((2,)),
        ],
        compiler_params=_cparams(vmem=3 * _nbytes(w_mod.shape, F32)),
    )(c, w_mod, b_mod, cw)


def _exchange_start(parts, gather, name):
    n = len(parts)
    lands = [lax.empty(((N_DEV,) + tuple(p.shape)) if gather else tuple(p.shape), p.dtype) for p in parts]

    def body(*refs):
        ins, lnd = refs[:n], refs[n:2 * n]
        ssem, rsem, token = refs[2 * n], refs[2 * n + 1], refs[-1]
        pos, me = _my_position()
        for k in range(1, N_DEV):
            peer, pj = _peer(pos, k)
            for t in range(n):
                src = ins[t] if gather else ins[t].at[pj]
                q = t * (N_DEV - 1) + k - 1
                _remote(src, lnd[t].at[me], ssem.at[q], rsem.at[q], peer).start()
        token[...] = jnp.zeros(token.shape, F32)

    out = pl.pallas_call(
        body,
        name=name,
        out_shape=(pltpu.SemaphoreType.DMA((n * (N_DEV - 1),)), pltpu.SemaphoreType.DMA((n * (N_DEV - 1),)))
        + tuple(pltpu.HBM(p.shape, p.dtype) for p in parts) + tuple(pltpu.HBM(p.shape, p.dtype) for p in lands)
        + (jax.ShapeDtypeStruct((SUBLANES, 128), F32),),
        in_specs=[HBM_SPEC] * (2 * n),
        out_specs=(SEM_SPEC, SEM_SPEC) + (HBM_SPEC,) * (2 * n) + (VMEM_SPEC,),
        input_output_aliases={i: 2 + i for i in range(2 * n)},
        compiler_params=pltpu.CompilerParams(has_side_effects=SIDE_EFFECT),
    )(*[pltpu.with_memory_space_constraint(p, pltpu.HBM) for p in list(parts) + lands])
    return out[0], out[1], list(out[2:2 + n]), list(out[2 + n:2 + 2 * n]), out[-1]


def _exchange_wait(started, after, gather, name):
    ssem, rsem, parts, lands, _ = started
    n = len(parts)

    def body(*refs):
        ins, lnd = refs[:n], refs[n:2 * n]
        ssem_ref, rsem_ref = refs[2 * n], refs[2 * n + 1]
        lsem = refs[-1]
        pos, me = _my_position()
        local = []
        for t in range(n):
            src = ins[t] if gather else ins[t].at[me]
            local.append(pltpu.make_async_copy(src, lnd[t].at[me], lsem.at[t]))
            local[-1].start()
        for k in range(1, N_DEV):
            peer, pj = _peer(pos, k)
            for t in range(n):
                src = ins[t] if gather else ins[t].at[pj]
                q = t * (N_DEV - 1) + k - 1
                _remote(src, lnd[t].at[me], ssem_ref.at[q], rsem_ref.at[q], peer).wait_send()
                _remote(src, lnd[t].at[pj], ssem_ref.at[q], rsem_ref.at[q], peer).wait_recv()
        for cp in local:
            cp.wait()

    out = pl.pallas_call(
        body,
        name=name,
        out_shape=tuple(pltpu.HBM(p.shape, p.dtype) for p in parts) + tuple(pltpu.HBM(p.shape, p.dtype) for p in lands),
        in_specs=[HBM_SPEC] * (2 * n) + [SEM_SPEC, SEM_SPEC, ANY_SPEC],
        out_specs=(HBM_SPEC,) * (2 * n),
        input_output_aliases={i: i for i in range(2 * n)},
        scratch_shapes=[pltpu.SemaphoreType.DMA((n,))],
        compiler_params=pltpu.CompilerParams(has_side_effects=SIDE_EFFECT),
    )(*parts, *lands, ssem, rsem, after)
    return list(out[n:])


def _reduce_small(rows, gates, dm8, cact):
    r, d = rows.shape
    ng, bw, _ = gates.shape
    nl = dm8.shape[0]
    cm = dm8.shape[2] // N_DEV

    def body(rows_ref, gates_ref, dm_ref, cact_ref, orow_ref, ogate_ref, owm_ref, gr, gg, dmr, ssem, rsem, lsem):
        pos, me = _my_position()
        me8 = pl.multiple_of(me * SUBLANES, SUBLANES)
        gr[me] = rows_ref[...]
        gg[me] = gates_ref[...].astype(BF16)
        own_dm = pltpu.make_async_copy(dm_ref.at[:, :, pl.ds(me * cm, cm)], dmr.at[:, pl.ds(me8, SUBLANES), :], lsem.at[0])
        own_dm.start()
        sends = []
        for k in range(1, N_DEV):
            peer, pj = _peer(pos, k)
            sends.append(_remote(gr.at[me], gr.at[me], ssem.at[0, k - 1], rsem.at[0, k - 1], peer))
            sends.append(_remote(gg.at[me], gg.at[me], ssem.at[1, k - 1], rsem.at[1, k - 1], peer))
            sends.append(_remote(dm_ref.at[:, :, pl.ds(pj * cm, cm)], dmr.at[:, pl.ds(me8, SUBLANES), :],
                                 ssem.at[2, k - 1], rsem.at[2, k - 1], peer))
        for cp in sends:
            cp.start()
        for k in range(1, N_DEV):
            peer, pj = _peer(pos, k)
            pj8 = pl.multiple_of(pj * SUBLANES, SUBLANES)
            _remote(gr.at[pj], gr.at[pj], ssem.at[0, k - 1], rsem.at[0, k - 1], peer).wait_recv()
            _remote(gg.at[pj], gg.at[pj], ssem.at[1, k - 1], rsem.at[1, k - 1], peer).wait_recv()
            _remote(dm_ref.at[:, :, pl.ds(0, cm)], dmr.at[:, pl.ds(pj8, SUBLANES), :], ssem.at[2, k - 1], rsem.at[2, k - 1], peer).wait_recv()
        own_dm.wait()
        acc = gr[0]
        for j in range(1, N_DEV):
            acc = acc + gr[j]
        orow_ref[...] = acc
        accg = gg[0].astype(F32)
        for j in range(1, N_DEV):
            accg = accg + gg[j].astype(F32)
        ogate_ref[...] = accg
        cb = cact_ref[...].astype(BF16)
        for l in range(nl):
            owm_ref[l] = _dot_tn(cb, dmr[l].astype(BF16))
        for cp in sends:
            cp.wait_send()

    return pl.pallas_call(
        body,
        name="reduce_small",
        out_shape=(
            jax.ShapeDtypeStruct((r, d), F32),
            jax.ShapeDtypeStruct((ng, bw, bw), F32),
            jax.ShapeDtypeStruct((nl, d, cm), F32),
        ),
        in_specs=[VMEM_SPEC] * 4,
        out_specs=(VMEM_SPEC,) * 3,
        scratch_shapes=[
            pltpu.VMEM((N_DEV, r, d), F32),
            pltpu.VMEM((N_DEV, ng, bw, bw), BF16),
            pltpu.VMEM((nl, N_DEV * SUBLANES, cm), F32),
            pltpu.SemaphoreType.DMA((3, N_DEV - 1)),
            pltpu.SemaphoreType.DMA((3, N_DEV - 1)),
            pltpu.SemaphoreType.DMA((1,)),
        ],
        compiler_params=_cparams(vmem=4 * _nbytes((N_DEV, ng, bw, bw), F32) + 4 * _nbytes((N_DEV, r, d), F32)),
    )(rows, gates, dm8, cact)


def _in_proj_fwd(x, mod, rows, win_f, l, tm):
    s, d = x.shape
    nb, _, ci = win_f.shape

    def body(x_ref, mod_ref, rows_ref, w_ref, proj_ref, h_ref):
        xv = x_ref[...]
        g = rows_ref[R_G_PRE_MIX:R_G_PRE_MIX + 1, :]
        h = (xv * _rms(xv) * g) * (1.0 + mod_ref[M_SC_M:M_SC_M + 1, :]) + mod_ref[M_SH_M:M_SH_M + 1, :]
        hb = h.astype(BF16)
        h_ref[...] = hb
        for j in range(nb):
            proj_ref[:, j * ci:(j + 1) * ci] = _dot(hb, w_ref[j])

    return pl.pallas_call(
        body,
        name="in_proj_fwd",
        grid=(s // tm,),
        in_specs=[
            pl.BlockSpec((tm, d), lambda i: (i, 0)),
            _resident((None, N_MOD, d), lambda i: (l, 0, 0)),
            _resident((None, N_ROWS, d), lambda i: (l, 0, 0)),
            _resident((nb, d, ci), lambda i: (0, 0, 0)),
        ],
        out_specs=(pl.BlockSpec((tm, nb * ci), lambda i: (i, 0)), pl.BlockSpec((tm, d), lambda i: (i, 0))),
        out_shape=(jax.ShapeDtypeStruct((s, nb * ci), F32), jax.ShapeDtypeStruct((s, d), BF16)),
        compiler_params=_cparams(("parallel",), _nbytes((nb, d, ci), BF16) + 3 * _nbytes((tm, nb * ci), F32) + 8 * _nbytes((tm, d), F32)),
    )(x, mod, rows, win_f)


def _mixer_core_fwd(proj, x, mod, rows, cwf, wr, wi, wa_f, wb_f, wo_f, l, tm):
    s, d = x.shape
    nh, bw, _ = wr.shape[1:]

    def body(proj_ref, x_ref, mod_ref, rows_ref, cw_ref, wr_ref, wi_ref, wa_ref, wb_ref, wo_ref,
             x1_ref, hs_ref, yap_ref, ybp_ref, y_ref, cvbuf, xbbuf, a_s, b_s, hprev):
        i = pl.program_id(0)

        @pl.when(i == 0)
        def _():
            cvbuf[pl.ds(0, SUBLANES), :] = jnp.zeros((SUBLANES, d), F32)
            xbbuf[pl.ds(0, SUBLANES), :] = jnp.zeros((SUBLANES, d), F32)
            hprev[...] = jnp.zeros((SUBLANES, d), F32)

        def row(r):
            return rows_ref[r:r + 1, :]

        def tap(r):
            return cw_ref[r:r + 1, :]

        ba = proj_ref[:, 0:d]
        cv = proj_ref[:, d:2 * d] * proj_ref[:, 2 * d:3 * d]
        cvbuf[pl.ds(SUBLANES, tm), :] = cv
        conv3 = ((row(R_CONV_A_B) + cvbuf[pl.ds(SUBLANES - 2, tm), :] * tap(CW_A)) + cvbuf[pl.ds(SUBLANES - 1, tm), :] * tap(CW_A + 1)) + cv * tap(CW_A + 2)
        ya = ba * conv3
        cvbuf[pl.ds(0, SUBLANES), :] = cvbuf[pl.ds(tm, SUBLANES), :]
        xb = proj_ref[:, 3 * d:4 * d]
        xbbuf[pl.ds(SUBLANES, tm), :] = xb
        xc = (((row(R_CONV_B_B) + xbbuf[pl.ds(SUBLANES - 3, tm), :] * tap(CW_B)) + xbbuf[pl.ds(SUBLANES - 2, tm), :] * tap(CW_B + 1))
              + xbbuf[pl.ds(SUBLANES - 1, tm), :] * tap(CW_B + 2)) + xb * tap(CW_B + 3)
        xbbuf[pl.ds(0, SUBLANES), :] = xbbuf[pl.ds(tm, SUBLANES), :]
        sp = _softplus_neg(row(R_LAMBDA))
        _, _, ig, a, mult = _gates(xc, wr_ref, wi_ref, row(R_B_GATE_R), row(R_B_GATE_I), sp, nh, bw)
        a_s[...] = a
        b_s[...] = mult * (ig * xc)

        def blk(j, hp):
            o = pl.multiple_of(j * SUBLANES, SUBLANES)
            a8, b8 = _scan_block(a_s[pl.ds(o, SUBLANES), :], b_s[pl.ds(o, SUBLANES), :], reverse=False)
            h8 = b8 + a8 * hp
            hs_ref[pl.ds(o, SUBLANES), :] = h8
            return jnp.broadcast_to(h8[SUBLANES - 1:SUBLANES, :], (SUBLANES, d))

        hprev[...] = lax.fori_loop(0, tm // SUBLANES, blk, hprev[...])
        gel, _ = _gelu(proj_ref[:, 4 * d:5 * d])
        yb = hs_ref[...] * gel
        yap = _dot(ya.astype(BF16), wa_ref[...])
        ybp = _dot(yb.astype(BF16), wb_ref[...])
        yap_ref[...] = yap
        ybp_ref[...] = ybp
        m = _sigmoid(proj_ref[:, 5 * d:6 * d]) * yap + _sigmoid(proj_ref[:, 6 * d:7 * d]) * ybp
        y = _dot(m.astype(BF16), wo_ref[...])
        y_ref[...] = y
        x1_ref[...] = x_ref[...] + mod_ref[M_GT_M:M_GT_M + 1, :] * ((y * _rms(y)) * row(R_G_POST_MIX))

    tile = pl.BlockSpec((tm, d), lambda i: (i, 0))
    return pl.pallas_call(
        body,
        name="mixer_core_fwd",
        grid=(s // tm,),
        in_specs=[
            pl.BlockSpec((tm, 7 * d), lambda i: (i, 0)),
            tile,
            _resident((None, N_MOD, d), lambda i: (l, 0, 0)),
            _resident((None, N_ROWS, d), lambda i: (l, 0, 0)),
            _resident((None, CW_ROWS, d), lambda i: (l, 0, 0)),
            _resident((None, nh, bw, bw), lambda i: (l, 0, 0, 0)),
            _resident((None, nh, bw, bw), lambda i: (l, 0, 0, 0)),
            _resident((d, d), lambda i: (0, 0)),
            _resident((d, d), lambda i: (0, 0)),
            _resident((d, d), lambda i: (0, 0)),
        ],
        out_specs=(tile,) * 5,
        out_shape=(jax.ShapeDtypeStruct((s, d), F32),) * 5,
        scratch_shapes=[
            pltpu.VMEM((tm + SUBLANES, d), F32),
            pltpu.VMEM((tm + SUBLANES, d), F32),
            pltpu.VMEM((tm, d), F32),
            pltpu.VMEM((tm, d), F32),
            pltpu.VMEM((SUBLANES, d), F32),
        ],
        compiler_params=_cparams(("arbitrary",), 3 * _nbytes((d, d), BF16) + 2 * _nbytes((tm, 7 * d), F32) + 40 * _nbytes((tm, d), F32)),
    )(proj, x, mod, rows, cwf, wr, wi, wa_f, wb_f, wo_f)


def _mlp_fwd(x1, mod, rows, wup_f, wdn_f, l, tm):
    s, d = x1.shape
    nb, _, cu = wup_f.shape
    dff = nb * cu

    def body(x1_ref, mod_ref, rows_ref, wu_ref, wd_ref, x2_ref, ru_ref, y2_ref, h2_ref):
        xv = x1_ref[...]
        g = rows_ref[R_G_PRE_MLP:R_G_PRE_MLP + 1, :]
        h2 = ((xv * _rms(xv) * g) * (1.0 + mod_ref[M_SC_F:M_SC_F + 1, :]) + mod_ref[M_SH_F:M_SH_F + 1, :]).astype(BF16)
        h2_ref[...] = h2
        ru = jnp.concatenate([jnp.maximum(_dot(h2, wu_ref[j]), 0.0) for j in range(nb)], axis=1)
        ru_ref[...] = ru.astype(BF16)
        y2 = _dot((ru * ru).astype(BF16), wd_ref[...])
        y2_ref[...] = y2
        x2_ref[...] = xv + mod_ref[M_GT_F:M_GT_F + 1, :] * ((y2 * _rms(y2)) * rows_ref[R_G_POST_MLP:R_G_POST_MLP + 1, :])

    tile = pl.BlockSpec((tm, d), lambda i: (i, 0))
    wide = pl.BlockSpec((tm, dff), lambda i: (i, 0))
    return pl.pallas_call(
        body,
        name="mlp_fwd",
        grid=(s // tm,),
        in_specs=[
            tile,
            _resident((None, N_MOD, d), lambda i: (l, 0, 0)),
            _resident((None, N_ROWS, d), lambda i: (l, 0, 0)),
            _resident((nb, d, cu), lambda i: (0, 0, 0)),
            _resident((dff, d), lambda i: (0, 0)),
        ],
        out_specs=(tile, wide, tile, tile),
        out_shape=(jax.ShapeDtypeStruct((s, d), F32), jax.ShapeDtypeStruct((s, dff), BF16),
                   jax.ShapeDtypeStruct((s, d), F32), jax.ShapeDtypeStruct((s, d), BF16)),
        compiler_params=_cparams(("parallel",), 2 * _nbytes((dff, d), BF16) + 5 * _nbytes((tm, dff), F32) + 12 * _nbytes((tm, d), F32)),
    )(x1, mod, rows, wup_f, wdn_f)


def _loss_fwd_bwd(y, target, tm):
    s, d = y.shape

    def body(y_ref, t_ref, loss_ref, dy_ref):
        @pl.when(pl.program_id(0) == 0)
        def _():
            loss_ref[...] = jnp.zeros(loss_ref.shape, F32)

        e = y_ref[...] - t_ref[...]
        dy_ref[...] = e * (1.0 / d)
        loss_ref[...] += 0.5 * jnp.sum(jnp.mean(e * e, axis=-1, keepdims=True), axis=0, keepdims=True)

    tile = pl.BlockSpec((tm, d), lambda i: (i, 0))
    loss, dy = pl.pallas_call(
        body,
        name="loss",
        grid=(s // tm,),
        in_specs=[tile, tile],
        out_specs=(pl.BlockSpec((SUBLANES, 128), lambda i: (0, 0)), tile),
        out_shape=(jax.ShapeDtypeStruct((SUBLANES, 128), F32), jax.ShapeDtypeStruct((s, d), F32)),
        compiler_params=_cparams(("arbitrary",)),
    )(y, target)
    return loss[0, 0], dy


def _mlp_bwd(dx2, x1, y2, ru, mod, rows, wup_f, wdn_f, l, tm):
    s, d = x1.shape
    nb, _, cu = wup_f.shape
    dff = nb * cu

    def body(dx2_ref, x1_ref, y2_ref, ru_ref, mod_ref, rows_ref, wu_ref, wd_ref, dx1_ref, dy2_ref, dup_ref, act_ref, sm_ref):
        @pl.when(pl.program_id(0) == 0)
        def _():
            sm_ref[...] = jnp.zeros(sm_ref.shape, F32)

        dout = dx2_ref[...]
        dy2, dgt, dgpost = _postnorm_bwd(y2_ref[...], dout, rows_ref[R_G_POST_MLP:R_G_POST_MLP + 1, :], mod_ref[M_GT_F:M_GT_F + 1, :])
        dy2b = dy2.astype(BF16)
        dy2_ref[...] = dy2b
        ruv = ru_ref[...].astype(F32)
        act_ref[...] = (ruv * ruv).astype(BF16)
        dup = (_dot_nt(dy2b, wd_ref[...]) * (2.0 * ruv)).astype(BF16)
        dup_ref[...] = dup
        dh2 = _dot_nt(dup[:, 0:cu], wu_ref[0])
        for j in range(1, nb):
            dh2 = dh2 + _dot_nt(dup[:, j * cu:(j + 1) * cu], wu_ref[j])
        dxn, dsc, dsh, dgpre = _prenorm_bwd(x1_ref[...], dh2, rows_ref[R_G_PRE_MLP:R_G_PRE_MLP + 1, :], mod_ref[M_SC_F:M_SC_F + 1, :])
        dx1_ref[...] = dout + dxn
        for r, v in ((G_MLP_GT, dgt), (G_MLP_GPOST, dgpost), (G_MLP_SC, dsc), (G_MLP_SH, dsh), (G_MLP_GPRE, dgpre)):
            sm_ref[r:r + 1, :] += v

    tile = pl.BlockSpec((tm, d), lambda i: (i, 0))
    wide = pl.BlockSpec((tm, dff), lambda i: (i, 0))
    return pl.pallas_call(
        body,
        name="mlp_bwd",
        grid=(s // tm,),
        in_specs=[
            tile, tile, tile, wide,
            _resident((None, N_MOD, d), lambda i: (l, 0, 0)),
            _resident((None, N_ROWS, d), lambda i: (l, 0, 0)),
            _resident((nb, d, cu), lambda i: (0, 0, 0)),
            _resident((dff, d), lambda i: (0, 0)),
        ],
        out_specs=(tile, tile, wide, wide, pl.BlockSpec((G_MLP_ROWS, d), lambda i: (0, 0))),
        out_shape=(jax.ShapeDtypeStruct((s, d), F32), jax.ShapeDtypeStruct((s, d), BF16), jax.ShapeDtypeStruct((s, dff), BF16),
                   jax.ShapeDtypeStruct((s, dff), BF16), jax.ShapeDtypeStruct((G_MLP_ROWS, d), F32)),
        compiler_params=_cparams(("arbitrary",), 2 * _nbytes((dff, d), BF16) + 6 * _nbytes((tm, dff), F32) + 16 * _nbytes((tm, d), F32)),
    )(dx2, x1, y2, ru, mod, rows, wup_f, wdn_f)


def _mixer_core_bwd(dx1, y, yap, ybp, hs, proj, mod, rows, cwf, wr, wi, wa_f, wb_f, wo_f, l, tm):
    s, d = dx1.shape
    nh, bw, _ = wr.shape[1:]
    nt = s // tm
    per = tm // SUBLANES

    def body(dx1_ref, y_ref, yap_ref, ybp_ref, hs_ref, hsh_ref, proj_ref, projh_ref, mod_ref, rows_ref, cw_ref,
             wr_ref, wi_ref, wa_ref, wb_ref, wo_ref,
             dproj_ref, dy_ref, m_ref, dyap_ref, dybp_ref, ya_ref, yb_ref, sm_ref, dwg_ref,
             cvbuf, xbbuf, hsbuf, abuf, dcbuf, dxbuf, al_s, dh_s, lam_s, lnext):
        i = pl.program_id(0)
        first_tile = i == nt - 1

        @pl.when(i == 0)
        def _():
            sm_ref[...] = jnp.zeros(sm_ref.shape, F32)
            dwg_ref[...] = jnp.zeros(dwg_ref.shape, F32)
            zero = jnp.zeros((SUBLANES, d), F32)
            abuf[pl.ds(tm, SUBLANES), :] = zero
            dcbuf[pl.ds(tm, SUBLANES), :] = zero
            dxbuf[pl.ds(tm, SUBLANES), :] = zero
            lnext[...] = zero

        def row(r):
            return rows_ref[r:r + 1, :]

        def tap(r):
            return cw_ref[r:r + 1, :]

        def acc(r, v):
            sm_ref[r:r + 1, :] += v

        keep_halo = jnp.where(first_tile, 0.0, 1.0)
        dy, dgt, dgpost = _postnorm_bwd(y_ref[...], dx1_ref[...], row(R_G_POST_MIX), mod_ref[M_GT_M:M_GT_M + 1, :])
        acc(G_MIX_GT, dgt)
        acc(G_MIX_GPOST, dgpost)
        dyb16 = dy.astype(BF16)
        dy_ref[...] = dyb16
        dm = _dot_nt(dyb16, wo_ref[...])
        sa = _sigmoid(proj_ref[:, 5 * d:6 * d])
        sb = _sigmoid(proj_ref[:, 6 * d:7 * d])
        yap = yap_ref[...]
        ybp = ybp_ref[...]
        m_ref[...] = (sa * yap + sb * ybp).astype(BF16)
        dyap = (dm * sa).astype(BF16)
        dybp = (dm * sb).astype(BF16)
        dyap_ref[...] = dyap
        dybp_ref[...] = dybp
        dproj_ref[:, 5 * d:6 * d] = (dm * yap * sa * (1.0 - sa)).astype(BF16)
        dproj_ref[:, 6 * d:7 * d] = (dm * ybp * sb * (1.0 - sb)).astype(BF16)
        dya = _dot_nt(dyap, wa_ref[...])
        dyb = _dot_nt(dybp, wb_ref[...])
        ba = proj_ref[:, 0:d]
        ca = proj_ref[:, d:2 * d]
        va = proj_ref[:, 2 * d:3 * d]
        cv = ca * va
        cvbuf[pl.ds(0, SUBLANES), :] = keep_halo * (projh_ref[:, d:2 * d] * projh_ref[:, 2 * d:3 * d])
        cvbuf[pl.ds(SUBLANES, tm), :] = cv
        cvm2 = cvbuf[pl.ds(SUBLANES - 2, tm), :]
        cvm1 = cvbuf[pl.ds(SUBLANES - 1, tm), :]
        conv3 = ((row(R_CONV_A_B) + cvm2 * tap(CW_A)) + cvm1 * tap(CW_A + 1)) + cv * tap(CW_A + 2)
        ya_ref[...] = (ba * conv3).astype(BF16)
        dproj_ref[:, 0:d] = (dya * conv3).astype(BF16)
        dc3 = dya * ba
        acc(G_MIX_CAB, _colsum(dc3))
        acc(G_MIX_CAW, _colsum(dc3 * cvm2))
        acc(G_MIX_CAW + 1, _colsum(dc3 * cvm1))
        acc(G_MIX_CAW + 2, _colsum(dc3 * cv))
        dcbuf[pl.ds(0, tm), :] = dc3
        dcv = (dc3 * tap(CW_A + 2) + dcbuf[pl.ds(1, tm), :] * tap(CW_A + 1)) + dcbuf[pl.ds(2, tm), :] * tap(CW_A)
        dcbuf[pl.ds(tm, SUBLANES), :] = dcbuf[pl.ds(0, SUBLANES), :]
        dproj_ref[:, d:2 * d] = (dcv * va).astype(BF16)
        dproj_ref[:, 2 * d:3 * d] = (dcv * ca).astype(BF16)
        xb = proj_ref[:, 3 * d:4 * d]
        gb = proj_ref[:, 4 * d:5 * d]
        xbbuf[pl.ds(0, SUBLANES), :] = keep_halo * projh_ref[:, 3 * d:4 * d]
        xbbuf[pl.ds(SUBLANES, tm), :] = xb
        xm3 = xbbuf[pl.ds(SUBLANES - 3, tm), :]
        xm2 = xbbuf[pl.ds(SUBLANES - 2, tm), :]
        xm1 = xbbuf[pl.ds(SUBLANES - 1, tm), :]
        xc = (((row(R_CONV_B_B) + xm3 * tap(CW_B)) + xm2 * tap(CW_B + 1)) + xm1 * tap(CW_B + 2)) + xb * tap(CW_B + 3)
        lam = row(R_LAMBDA)
        sp = _softplus_neg(lam)
        xcb, r, ig, a, mult = _gates(xc, wr_ref, wi_ref, row(R_B_GATE_R), row(R_B_GATE_I), sp, nh, bw)
        gel, th = _gelu(gb)
        hs = hs_ref[...]
        yb_ref[...] = (hs * gel).astype(BF16)
        dproj_ref[:, 4 * d:5 * d] = (dyb * hs * _gelu_grad(gb, th)).astype(BF16)
        abuf[pl.ds(0, tm), :] = a
        al_s[...] = abuf[pl.ds(1, tm), :]
        abuf[pl.ds(tm, SUBLANES), :] = abuf[pl.ds(0, SUBLANES), :]
        dh_s[...] = dyb * gel

        def blk(j, ln):
            o = pl.multiple_of((per - 1 - j) * SUBLANES, SUBLANES)
            a8, b8 = _scan_block(al_s[pl.ds(o, SUBLANES), :], dh_s[pl.ds(o, SUBLANES), :], reverse=True)
            l8 = b8 + a8 * ln
            lam_s[pl.ds(o, SUBLANES), :] = l8
            return jnp.broadcast_to(l8[0:1, :], (SUBLANES, d))

        lnext[...] = lax.fori_loop(0, per, blk, lnext[...])
        dbb = lam_s[...]
        hsbuf[pl.ds(0, SUBLANES), :] = keep_halo * hsh_ref[...]
        hsbuf[pl.ds(SUBLANES, tm), :] = hs
        da = dbb * hsbuf[pl.ds(SUBLANES - 1, tm), :]
        dmult = dbb * (ig * xc)
        dig = dbb * (mult * xc)
        dxc = dbb * (mult * ig)
        dla = da * a - dmult * (a * a) / mult
        acc(G_MIX_LAM, _colsum(dla * (-LRU_C * r)) * (-_sigmoid(-lam)))
        dzr = (dla * (-LRU_C * sp)) * r * (1.0 - r)
        dzi = dig * ig * (1.0 - ig)
        acc(G_MIX_BR, _colsum(dzr))
        acc(G_MIX_BI, _colsum(dzi))
        dzrb = dzr.astype(BF16)
        dzib = dzi.astype(BF16)
        back = []
        for h in range(nh):
            sl = slice(h * bw, (h + 1) * bw)
            back.append(_dot_nt(dzrb[:, sl], wr_ref[h]) + _dot_nt(dzib[:, sl], wi_ref[h]))
            dwg_ref[0, h] += _dot_tn(xcb[:, sl], dzrb[:, sl])
            dwg_ref[1, h] += _dot_tn(xcb[:, sl], dzib[:, sl])
        dxc = dxc + jnp.concatenate(back, axis=1)
        acc(G_MIX_CBB, _colsum(dxc))
        acc(G_MIX_CBW, _colsum(dxc * xm3))
        acc(G_MIX_CBW + 1, _colsum(dxc * xm2))
        acc(G_MIX_CBW + 2, _colsum(dxc * xm1))
        acc(G_MIX_CBW + 3, _colsum(dxc * xb))
        dxbuf[pl.ds(0, tm), :] = dxc
        dxb = ((dxc * tap(CW_B + 3) + dxbuf[pl.ds(1, tm), :] * tap(CW_B + 2)) + dxbuf[pl.ds(2, tm), :] * tap(CW_B + 1)) + dxbuf[pl.ds(3, tm), :] * tap(CW_B)
        dxbuf[pl.ds(tm, SUBLANES), :] = dxbuf[pl.ds(0, SUBLANES), :]
        dproj_ref[:, 3 * d:4 * d] = dxb.astype(BF16)

    def rev(i):
        return (nt - 1 - i, 0)

    def halo(i):
        return (jnp.maximum((nt - 1 - i) * per - 1, 0), 0)

    tile = pl.BlockSpec((tm, d), rev)
    return pl.pallas_call(
        body,
        name="mixer_core_bwd",
        grid=(nt,),
        in_specs=[
            tile, tile, tile, tile, tile,
            pl.BlockSpec((SUBLANES, d), halo),
            pl.BlockSpec((tm, 7 * d), rev),
            pl.BlockSpec((SUBLANES, 7 * d), halo),
            _resident((None, N_MOD, d), lambda i: (l, 0, 0)),
            _resident((None, N_ROWS, d), lambda i: (l, 0, 0)),
            _resident((None, CW_ROWS, d), lambda i: (l, 0, 0)),
            _resident((None, nh, bw, bw), lambda i: (l, 0, 0, 0)),
            _resident((None, nh, bw, bw), lambda i: (l, 0, 0, 0)),
            _resident((d, d), lambda i: (0, 0)),
            _resident((d, d), lambda i: (0, 0)),
            _resident((d, d), lambda i: (0, 0)),
        ],
        out_specs=(pl.BlockSpec((tm, 7 * d), rev),) + (tile,) * 6 + (
            pl.BlockSpec((G_MIX_ROWS, d), lambda i: (0, 0)), pl.BlockSpec((2, nh, bw, bw), lambda i: (0, 0, 0, 0))),
        out_shape=(jax.ShapeDtypeStruct((s, 7 * d), BF16),) + (jax.ShapeDtypeStruct((s, d), BF16),) * 6 + (
            jax.ShapeDtypeStruct((G_MIX_ROWS, d), F32), jax.ShapeDtypeStruct((2, nh, bw, bw), F32)),
        scratch_shapes=[pltpu.VMEM((tm + SUBLANES, d), F32)] * 6 + [pltpu.VMEM((tm, d), F32)] * 3 + [pltpu.VMEM((SUBLANES, d), F32)],
        compiler_params=_cparams(("arbitrary",), 3 * _nbytes((d, d), BF16) + 3 * _nbytes((tm, 7 * d), F32) + 64 * _nbytes((tm, d), F32)),
    )(dx1, y, yap, ybp, hs, hs, proj, proj, mod, rows, cwf, wr, wi, wa_f, wb_f, wo_f)


def _in_proj_bwd(dproj, x, dx1, mod, rows, win_f, l, tm):
    s, d = x.shape
    nb, _, ci = win_f.shape

    def body(dp_ref, x_ref, dx1_ref, mod_ref, rows_ref, w_ref, dx_ref, sm_ref):
        @pl.when(pl.program_id(0) == 0)
        def _():
            sm_ref[...] = jnp.zeros(sm_ref.shape, F32)

        dh = _dot_nt(dp_ref[:, 0:ci], w_ref[0])
        for j in range(1, nb):
            dh = dh + _dot_nt(dp_ref[:, j * ci:(j + 1) * ci], w_ref[j])
        dxn, dsc, dsh, dg = _prenorm_bwd(x_ref[...], dh, rows_ref[R_G_PRE_MIX:R_G_PRE_MIX + 1, :], mod_ref[M_SC_M:M_SC_M + 1, :])
        dx_ref[...] = dx1_ref[...] + dxn
        for r, v in ((G_IN_SC, dsc), (G_IN_SH, dsh), (G_IN_GPRE, dg)):
            sm_ref[r:r + 1, :] += v

    tile = pl.BlockSpec((tm, d), lambda i: (i, 0))
    return pl.pallas_call(
        body,
        name="in_proj_bwd",
        grid=(s // tm,),
        in_specs=[
            pl.BlockSpec((tm, nb * ci), lambda i: (i, 0)), tile, tile,
            _resident((None, N_MOD, d), lambda i: (l, 0, 0)),
            _resident((None, N_ROWS, d), lambda i: (l, 0, 0)),
            _resident((nb, d, ci), lambda i: (0, 0, 0)),
        ],
        out_specs=(tile, pl.BlockSpec((G_IN_ROWS, d), lambda i: (0, 0))),
        out_shape=(jax.ShapeDtypeStruct((s, d), F32), jax.ShapeDtypeStruct((G_IN_ROWS, d), F32)),
        compiler_params=_cparams(("arbitrary",), _nbytes((nb, d, ci), BF16) + 2 * _nbytes((tm, nb * ci), BF16) + 16 * _nbytes((tm, d), F32)),
    )(dproj, x, dx1, mod, rows, win_f)


def _wgrad(a, b, cols_owned, ts):
    s, k1 = a.shape
    k2 = b.shape[1]
    ns = s // ts
    if cols_owned:
        bk1, bk2 = k1, k2 // N_DEV
        a_spec = pl.BlockSpec((ts, bk1), lambda j, t: (t, 0))
        b_spec = pl.BlockSpec((ts, bk2), lambda j, t: (t, j))
    else:
        bk1, bk2 = k1 // N_DEV, k2
        a_spec = pl.BlockSpec((ts, bk1), lambda j, t: (t, j))
        b_spec = pl.BlockSpec((ts, bk2), lambda j, t: (t, 0))

    def body(a_ref, b_ref, o_ref, acc_ref):
        t = pl.program_id(1)

        @pl.when(t == 0)
        def _():
            acc_ref[...] = jnp.zeros(acc_ref.shape, F32)

        acc_ref[...] += _dot_tn(a_ref[...], b_ref[...])

        @pl.when(t == ns - 1)
        def _():
            o_ref[...] = acc_ref[...].astype(BF16)

    return pl.pallas_call(
        body,
        name="wgrad",
        grid=(N_DEV, ns),
        in_specs=[a_spec, b_spec],
        out_specs=pl.BlockSpec((None, bk1, bk2), lambda j, t: (j, 0, 0)),
        out_shape=jax.ShapeDtypeStruct((N_DEV, bk1, bk2), BF16),
        scratch_shapes=[pltpu.VMEM((bk1, bk2), F32)],
        compiler_params=_cparams(("parallel", "arbitrary"), 4 * _nbytes((bk1, bk2), F32) + 4 * _nbytes((ts, bk1 + bk2), BF16)),
    )(a, b)


def _adam_update(w, g, m, v):
    m = ADAM_B1 * m + (1.0 - ADAM_B1) * g
    v = ADAM_B2 * v + (1.0 - ADAM_B2) * (g * g)
    m_hat = m / (1.0 - ADAM_B1 ** ADAM_STEP)
    v_hat = v / (1.0 - ADAM_B2 ** ADAM_STEP)
    delta = -ADAM_LR * (m_hat / (jnp.sqrt(v_hat) + ADAM_EPS) + ADAM_WD * w)
    return delta, m, v


def _sum_adamw(recv, w, m, v, tr):
    nl, ra, cb = w.shape
    assert nl == len(recv) == 2

    def body(r0_ref, r1_ref, w_ref, m_ref, v_ref, g_ref, d_ref, nm_ref, nv_ref):
        def total(r_ref):
            g = r_ref[0].astype(F32)
            for j in range(1, N_DEV):
                g = g + r_ref[j].astype(F32)
            return g

        g = jnp.where(pl.program_id(0) == 0, total(r0_ref), total(r1_ref))
        g_ref[...] = g
        d_ref[...], nm_ref[...], nv_ref[...] = _adam_update(w_ref[...], g, m_ref[...], v_ref[...])

    blk = pl.BlockSpec((None, tr, cb), lambda l, i: (l, i, 0))
    return pl.pallas_call(
        body,
        name="sum_adamw",
        grid=(nl, ra // tr),
        in_specs=[pl.BlockSpec((N_DEV, tr, cb), lambda l, i: (0, i * (1 - l), 0)),
                  pl.BlockSpec((N_DEV, tr, cb), lambda l, i: (0, i * l, 0)), blk, blk, blk],
        out_specs=(blk,) * 4,
        out_shape=(jax.ShapeDtypeStruct((nl, ra, cb), F32),) * 4,
        compiler_params=_cparams(("arbitrary", "arbitrary")),
    )(recv[0], recv[1], w, m, v)


def _adamw(w, g, m, v):
    def body(w_ref, g_ref, m_ref, v_ref, d_ref, nm_ref, nv_ref):
        d_ref[...], nm_ref[...], nv_ref[...] = _adam_update(w_ref[...], g_ref[...], m_ref[...], v_ref[...])

    return pl.pallas_call(
        body,
        name="adamw",
        in_specs=[VMEM_SPEC] * 4,
        out_specs=(VMEM_SPEC,) * 3,
        out_shape=(jax.ShapeDtypeStruct(w.shape, F32),) * 3,
        compiler_params=_cparams(vmem=10 * _nbytes(w.shape, F32)),
    )(w, g, m, v)


def _adamw_tiled(w, g, m, v, tr):
    nl, ra, cb = w.shape

    def body(w_ref, g_ref, m_ref, v_ref, d_ref, nm_ref, nv_ref):
        d_ref[...], nm_ref[...], nv_ref[...] = _adam_update(w_ref[...], g_ref[...], m_ref[...], v_ref[...])

    blk = pl.BlockSpec((None, tr, cb), lambda l, i: (l, i, 0))
    return pl.pallas_call(
        body,
        name="adamw_tiled",
        grid=(nl, ra // tr),
        in_specs=[blk] * 4,
        out_specs=(blk,) * 3,
        out_shape=(jax.ShapeDtypeStruct((nl, ra, cb), F32),) * 3,
        compiler_params=_cparams(("parallel", "parallel")),
    )(w, g, m, v)


def _token_tile(s):
    return min(256, max(SUBLANES * 2, s // 4))


def kernel(x, c, w_mod, b_mod, g_pre_mix, g_post_mix, w_in, conv_a_w, conv_a_b, w_a_out, conv_b_w, conv_b_b, w_gate_r, b_gate_r, w_gate_i, b_gate_i, lru_lambda, w_b_out, w_o, g_pre_mlp, g_post_mlp, w_mlp_up, w_mlp_down, loss_target, m_w_mod, m_b_mod, m_g_pre_mix, m_g_post_mix, m_w_in, m_conv_a_w, m_conv_a_b, m_w_a_out, m_conv_b_w, m_conv_b_b, m_w_gate_r, m_b_gate_r, m_w_gate_i, m_b_gate_i, m_lru_lambda, m_w_b_out, m_w_o, m_g_pre_mlp, m_g_post_mlp, m_w_mlp_up, m_w_mlp_down, v_w_mod, v_b_mod, v_g_pre_mix, v_g_post_mix, v_w_in, v_conv_a_w, v_conv_a_b, v_w_a_out, v_conv_b_w, v_conv_b_b, v_w_gate_r, v_b_gate_r, v_w_gate_i, v_b_gate_i, v_lru_lambda, v_w_b_out, v_w_o, v_g_pre_mlp, v_g_post_mlp, v_w_mlp_up, v_w_mlp_down):
    nl = w_mod.shape[0]
    s, d = x.shape[1], x.shape[2]
    nh, bw = w_gate_r.shape[1], w_gate_r.shape[2]
    cwid = conv_a_w.shape[2]
    tm = _token_tile(s)
    ts = min(512, s)
    _, me = _my_position()
    xs = x.reshape(s, d)
    target = loss_target.reshape(s, d)

    vec_names = (g_pre_mix, g_post_mix, conv_a_b, conv_b_b, b_gate_r, b_gate_i, lru_lambda, g_pre_mlp, g_post_mlp)
    rows = jnp.concatenate([jnp.stack(vec_names, axis=1), jnp.zeros((nl, N_ROWS - len(vec_names), d), F32)], axis=1)
    cw = jnp.concatenate([conv_a_w, conv_b_w, jnp.zeros((nl, CW_ROWS - 7, cwid), F32)], axis=1)

    w16 = {"w_in": w_in.astype(BF16), "w_a_out": w_a_out.astype(BF16), "w_b_out": w_b_out.astype(BF16), "w_o": w_o.astype(BF16),
           "w_mlp_up": w_mlp_up.astype(BF16), "w_mlp_down": w_mlp_down.astype(BF16)}
    groups = (("in", ("w_in",)), ("mix", ("w_a_out", "w_b_out", "w_o")), ("mlp", ("w_mlp_up", "w_mlp_down")))
    gathers = {}
    token = jnp.zeros((), F32)
    for l in range(nl):
        for gname, members in groups:
            gathers[l, gname] = _exchange_start([w16[n][l] for n in members], True, f"gather_start_{gname}{l}")
            token = token + gathers[l, gname][4][0, 0]
    rows = rows + token
    mod, cact, cwf = _prep_small(c + token, w_mod, b_mod, cw)
    mod = mod.reshape(nl, N_MOD, d)
    wr = w_gate_r.astype(BF16)
    wi = w_gate_i.astype(BF16)

    saved = []
    weights = []
    xin = xs
    for l in range(nl):
        (win_f,) = _exchange_wait(gathers[l, "in"], mod if l == 0 else xin, True, f"gather_wait_in{l}")
        proj, h = _in_proj_fwd(xin, mod, rows, win_f, l, tm)
        wa_f, wb_f, wo_f = (w.reshape(d, d) for w in _exchange_wait(gathers[l, "mix"], proj, True, f"gather_wait_mix{l}"))
        x1, hs, yap, ybp, y = _mixer_core_fwd(proj, xin, mod, rows, cwf, wr, wi, wa_f, wb_f, wo_f, l, tm)
        wup_f, wdn_f = _exchange_wait(gathers[l, "mlp"], x1, True, f"gather_wait_mlp{l}")
        wdn_f = wdn_f.reshape(-1, d)
        x2, ru, y2, h2 = _mlp_fwd(x1, mod, rows, wup_f, wdn_f, l, tm)
        saved.append((xin, proj, h, x1, hs, yap, ybp, y, ru, y2, h2))
        weights.append((win_f, wa_f, wb_f, wo_f, wup_f, wdn_f))
        xin = x2
    loss_part, dx = _loss_fwd_bwd(xin, target, tm)
    loss = lax.psum(loss_part, ("x", "y", "c"))

    scatters = {}
    small = [None] * nl
    gate_parts = [None] * nl

    def scatter(l, gname, parts, rows):
        scatters[l, gname] = _exchange_start(parts, False, f"scatter_start_{gname}{l}")
        return rows + scatters[l, gname][4][0, 0]

    for l in reversed(range(nl)):
        xin, proj, h, x1, hs, yap, ybp, y, ru, y2, h2 = saved[l]
        win_f, wa_f, wb_f, wo_f, wup_f, wdn_f = weights[l]
        dx1, dy2, dup, act, sm_mlp = _mlp_bwd(dx, x1, y2, ru, mod, rows, wup_f, wdn_f, l, tm)
        rows = scatter(l, "mlp", [_wgrad(h2, dup, True, ts), _wgrad(act, dy2, False, ts)], rows)
        dproj, dy, m, dyap, dybp, ya, yb, sm_mix, dwg = _mixer_core_bwd(dx1, y, yap, ybp, hs, proj, mod, rows, cwf, wr, wi, wa_f, wb_f, wo_f, l, tm // 2)
        rows = scatter(l, "mix", [_wgrad(ya, dyap, False, ts), _wgrad(yb, dybp, False, ts), _wgrad(m, dy, False, ts)], rows)
        rows = scatter(l, "in", [_wgrad(h, dproj, True, ts)], rows)
        dx, sm_in = _in_proj_bwd(dproj, xin, dx1, mod, rows, win_f, l, tm)
        small[l] = jnp.concatenate([sm_mlp, sm_mix, sm_in], axis=0)
        gate_parts[l] = dwg
    grad_x = dx.reshape(x.shape)

    lrows = jnp.concatenate(small, axis=0)
    gates_p = jnp.concatenate(gate_parts, axis=0).reshape(nl * 2 * nh, bw, bw)

    def lrow(a, l, r):
        return a[l * G_LAYER_ROWS + r]

    dm = jnp.stack([jnp.concatenate([lrow(lrows, l, G_MLP_ROWS + G_MIX_ROWS + G_IN_SH), lrow(lrows, l, G_MLP_ROWS + G_MIX_ROWS + G_IN_SC),
                                     lrow(lrows, l, G_MLP_ROWS + G_MIX_GT), lrow(lrows, l, G_MLP_SH), lrow(lrows, l, G_MLP_SC),
                                     lrow(lrows, l, G_MLP_GT)]) for l in range(nl)])
    dm8 = jnp.concatenate([dm[:, None, :], jnp.zeros((nl, SUBLANES - 1, N_MOD * d), F32)], axis=1)
    srows, sgates, g_w_mod = _reduce_small(lrows, gates_p, dm8, cact)
    sgates = sgates.reshape(nl, 2, nh, bw, bw)

    recv = {}
    for l in reversed(range(nl)):
        for gname, members in groups:
            for n, zone in zip(members, _exchange_wait(scatters[l, gname], srows, False, f"scatter_wait_{gname}{l}")):
                recv[n, l] = zone
    big = {}
    for name, w, m_, v_ in (("w_in", w_in, m_w_in, v_w_in), ("w_mlp_up", w_mlp_up, m_w_mlp_up, v_w_mlp_up),
                            ("w_a_out", w_a_out, m_w_a_out, v_w_a_out), ("w_b_out", w_b_out, m_w_b_out, v_w_b_out),
                            ("w_o", w_o, m_w_o, v_w_o), ("w_mlp_down", w_mlp_down, m_w_mlp_down, v_w_mlp_down)):
        big[name] = _sum_adamw([recv[name, l] for l in range(nl)], w, m_, v_, min(128, w.shape[1]))

    def srow(l, r):
        return lrow(srows, l, r)

    def per_layer(r):
        return jnp.stack([srow(l, r) for l in range(nl)])

    mix0 = G_MLP_ROWS
    in0 = G_MLP_ROWS + G_MIX_ROWS
    g_b_mod = jnp.stack([jnp.concatenate([srow(l, in0 + G_IN_SH), srow(l, in0 + G_IN_SC), srow(l, mix0 + G_MIX_GT),
                                          srow(l, G_MLP_SH), srow(l, G_MLP_SC), srow(l, G_MLP_GT)]) for l in range(nl)])
    conv_a_full = jnp.stack([jnp.stack([srow(l, mix0 + G_MIX_CAW + k) for k in range(3)]) for l in range(nl)])
    conv_b_full = jnp.stack([jnp.stack([srow(l, mix0 + G_MIX_CBW + k) for k in range(4)]) for l in range(nl)])
    grads = {
        "b_mod": g_b_mod,
        "g_pre_mix": per_layer(in0 + G_IN_GPRE),
        "g_post_mix": per_layer(mix0 + G_MIX_GPOST),
        "conv_a_w": lax.dynamic_slice_in_dim(conv_a_full, me * cwid, cwid, axis=2),
        "conv_a_b": per_layer(mix0 + G_MIX_CAB),
        "conv_b_w": lax.dynamic_slice_in_dim(conv_b_full, me * cwid, cwid, axis=2),
        "conv_b_b": per_layer(mix0 + G_MIX_CBB),
        "w_gate_r": sgates[:, 0],
        "b_gate_r": per_layer(mix0 + G_MIX_BR),
        "w_gate_i": sgates[:, 1],
        "b_gate_i": per_layer(mix0 + G_MIX_BI),
        "lru_lambda": per_layer(mix0 + G_MIX_LAM),
        "g_pre_mlp": per_layer(G_MLP_GPRE),
        "g_post_mlp": per_layer(G_MLP_GPOST),
    }
    params = {
        "b_mod": (b_mod, m_b_mod, v_b_mod), "g_pre_mix": (g_pre_mix, m_g_pre_mix, v_g_pre_mix), "g_post_mix": (g_post_mix, m_g_post_mix, v_g_post_mix),
        "conv_a_w": (conv_a_w, m_conv_a_w, v_conv_a_w), "conv_a_b": (conv_a_b, m_conv_a_b, v_conv_a_b),
        "conv_b_w": (conv_b_w, m_conv_b_w, v_conv_b_w), "conv_b_b": (conv_b_b, m_conv_b_b, v_conv_b_b),
        "w_gate_r": (w_gate_r, m_w_gate_r, v_w_gate_r), "b_gate_r": (b_gate_r, m_b_gate_r, v_b_gate_r),
        "w_gate_i": (w_gate_i, m_w_gate_i, v_w_gate_i), "b_gate_i": (b_gate_i, m_b_gate_i, v_b_gate_i),
        "lru_lambda": (lru_lambda, m_lru_lambda, v_lru_lambda), "g_pre_mlp": (g_pre_mlp, m_g_pre_mlp, v_g_pre_mlp),
        "g_post_mlp": (g_post_mlp, m_g_post_mlp, v_g_post_mlp),
    }
    out = {}
    for name, g in grads.items():
        w, m_, v_ = params[name]
        flat = (-1, w.shape[-1])
        dl, nm, nv = _adamw(w.reshape(flat), g.reshape(flat), m_.reshape(flat), v_.reshape(flat))
        out[name] = (g.reshape(w.shape), dl.reshape(w.shape), nm.reshape(w.shape), nv.reshape(w.shape))
    out["w_mod"] = (g_w_mod,) + tuple(_adamw_tiled(w_mod, g_w_mod, m_w_mod, v_w_mod, min(128, d)))
    out.update(big)

    order = ("w_mod", "b_mod", "g_pre_mix", "g_post_mix", "w_in", "conv_a_w", "conv_a_b", "w_a_out", "conv_b_w", "conv_b_b", "w_gate_r", "b_gate_r",
             "w_gate_i", "b_gate_i", "lru_lambda", "w_b_out", "w_o", "g_pre_mlp", "g_post_mlp", "w_mlp_up", "w_mlp_down")
    return (loss, grad_x) + tuple(out[n][0] for n in order) + tuple(out[n][1] for n in order) + tuple(out[n][2] for n in order) + tuple(out[n][3] for n in order)
```

```python
import functools

import jax
import jax.numpy as jnp
from jax import lax
from jax.experimental import pallas as pl
from jax.experimental.pallas import tpu as pltpu

F32, BF16 = jnp.float32, jnp.bfloat16
EPS = 1e-6
LRU_C = 8.0
N_DEV = 8
N_MOD = 6
SUBLANES = 8
VMEM_BUDGET = 56 * 1024 * 1024
ADAM_LR, ADAM_B1, ADAM_B2, ADAM_EPS, ADAM_WD, ADAM_STEP = 0.001, 0.9, 0.999, 1e-08, 0.01, 10
MESH = pl.DeviceIdType.MESH
VMEM_SPEC = pl.BlockSpec(memory_space=pltpu.VMEM)
ANY_SPEC = pl.BlockSpec(memory_space=pl.ANY)
HBM_SPEC = pl.BlockSpec(memory_space=pltpu.HBM)
SEM_SPEC = pl.BlockSpec(memory_space=pltpu.SEMAPHORE)
SIDE_EFFECT = pltpu.SideEffectType.DATAFLOW_SIDE_EFFECTING

R_G_PRE_MIX, R_G_POST_MIX, R_CONV_A_B, R_CONV_B_B, R_B_GATE_R, R_B_GATE_I, R_LAMBDA, R_G_PRE_MLP, R_G_POST_MLP = range(9)
N_ROWS = 16
M_SH_M, M_SC_M, M_GT_M, M_SH_F, M_SC_F, M_GT_F = range(6)
CW_A, CW_B, CW_ROWS = 0, 3, 8
G_MLP_GT, G_MLP_GPOST, G_MLP_SC, G_MLP_SH, G_MLP_GPRE, G_MLP_ROWS = 0, 1, 2, 3, 4, 8
(G_MIX_GT, G_MIX_GPOST, G_MIX_CAB, G_MIX_CAW, G_MIX_CBB, G_MIX_CBW, G_MIX_BR, G_MIX_BI, G_MIX_LAM) = 0, 1, 2, 3, 6, 7, 11, 12, 13
G_MIX_ROWS = 16
G_IN_SC, G_IN_SH, G_IN_GPRE, G_IN_ROWS = 0, 1, 2, 8
G_LAYER_ROWS = G_MLP_ROWS + G_MIX_ROWS + G_IN_ROWS


def _cparams(dims=None, vmem=None):
    kw = {}
    if dims is not None:
        kw["dimension_semantics"] = dims
    if vmem is not None:
        kw["vmem_limit_bytes"] = int(min(max(vmem, 16 * 1024 * 1024), VMEM_BUDGET))
    return pltpu.CompilerParams(**kw)


def _nbytes(shape, dtype):
    n = 1
    for s in shape:
        n *= s
    return n * jnp.dtype(dtype).itemsize


def _resident(block, index_map):
    return pl.BlockSpec(block, index_map, pipeline_mode=pl.Buffered(1))


def _my_position():
    x, y, c = lax.axis_index("x"), lax.axis_index("y"), lax.axis_index("c")
    return (x, y, c), 4 * x + 2 * y + c


def _peer(pos, k):
    x, y, c = pos
    px = 1 - x if k & 4 else x
    py = 1 - y if k & 2 else y
    pc = 1 - c if k & 1 else c
    return (px, py, pc), 4 * px + 2 * py + pc


def _remote(src, dst, ssem, rsem, peer):
    return pltpu.make_async_remote_copy(src_ref=src, dst_ref=dst, send_sem=ssem, recv_sem=rsem, device_id=peer, device_id_type=MESH)


def _dot(a, b):
    return jnp.dot(a, b, preferred_element_type=F32)


def _dot_nt(a, b):
    return lax.dot_general(a, b, (((1,), (1,)), ((), ())), preferred_element_type=F32)


def _dot_tn(a, b):
    return lax.dot_general(a, b, (((0,), (0,)), ((), ())), preferred_element_type=F32)


def _colsum(v):
    return jnp.sum(v, axis=0, keepdims=True)


def _sigmoid(v):
    return jax.nn.sigmoid(v)


def _gelu(v):
    k = 0.7978845608028654
    t = jnp.tanh(k * (v + 0.044715 * (v * v * v)))
    return 0.5 * v * (1.0 + t), t


def _gelu_grad(v, t):
    k = 0.7978845608028654
    return 0.5 * (1.0 + t) + 0.5 * v * (1.0 - t * t) * (k * (1.0 + 3.0 * 0.044715 * v * v))


def _expm1(v):
    u = jnp.exp(v)
    um1 = u - 1.0
    q = um1 * v / jnp.log(u)
    return jnp.where(um1 == 0.0, v, jnp.where(um1 == -1.0, -1.0, q))


def _softplus_neg(lam):
    z = -lam
    u = jnp.exp(-jnp.abs(z))
    w = 1.0 + u
    l1p = jnp.where(w == 1.0, u, jnp.log(w) * u / (w - 1.0))
    return jnp.maximum(z, 0.0) + l1p


def _rms(v):
    return lax.rsqrt(jnp.mean(v * v, axis=-1, keepdims=True) + EPS)


def _prenorm_bwd(xv, dh, g, sc):
    r = _rms(xv)
    xn = xv * r
    n = xn * g
    dsc = _colsum(dh * n)
    dsh = _colsum(dh)
    dn = dh * (1.0 + sc)
    dg = _colsum(dn * xn)
    dxn = dn * g
    dx = r * (dxn - xn * jnp.mean(dxn * xn, axis=-1, keepdims=True))
    return dx, dsc, dsh, dg


def _postnorm_bwd(yv, dout, g, gt):
    r = _rms(yv)
    yn = yv * r
    dgt = _colsum(dout * (yn * g))
    dn = dout * gt
    dg = _colsum(dn * yn)
    dyn = dn * g
    dy = r * (dyn - yn * jnp.mean(dyn * yn, axis=-1, keepdims=True))
    return dy, dgt, dg


def _gates(xc, wr_ref, wi_ref, b_r, b_i, sp, nh, bw):
    xcb = xc.astype(BF16)
    zr = jnp.concatenate([_dot(xcb[:, h * bw:(h + 1) * bw], wr_ref[h]) for h in range(nh)], axis=1) + b_r
    zi = jnp.concatenate([_dot(xcb[:, h * bw:(h + 1) * bw], wi_ref[h]) for h in range(nh)], axis=1) + b_i
    r = _sigmoid(zr)
    ig = _sigmoid(zi)
    la = (-LRU_C * r) * sp
    a = jnp.exp(la)
    mult = jnp.sqrt(-_expm1(2.0 * la))
    return xcb, r, ig, a, mult


def _scan_block(a8, b8, reverse):
    row = lax.broadcasted_iota(jnp.int32, a8.shape, 0)
    for s in (1, 2, 4):
        if reverse:
            keep = row < SUBLANES - s
            a_sh = pltpu.roll(a8, SUBLANES - s, 0)
            b_sh = pltpu.roll(b8, SUBLANES - s, 0)
        else:
            keep = row >= s
            a_sh = pltpu.roll(a8, s, 0)
            b_sh = pltpu.roll(b8, s, 0)
        b8 = b8 + a8 * jnp.where(keep, b_sh, 0.0)
        a8 = a8 * jnp.where(keep, a_sh, 1.0)
    return a8, b8


def _prep_small(c, w_mod, b_mod, cw):
    d = c.shape[1]
    cm = w_mod.shape[2]
    cwid = cw.shape[2]
    nl = w_mod.shape[0]

    def body(c_ref, wm_ref, bm_ref, cw_ref, mod_ref, cact_ref, cwf_ref, cbuf, pbuf, rbuf, ssem, rsem, lsem):
        pos, me = _my_position()
        me8 = pl.multiple_of(me * SUBLANES, SUBLANES)
        cbuf[pl.ds(me8, SUBLANES), :] = jnp.broadcast_to(c_ref[...], (SUBLANES, d))
        own_cw = pltpu.make_async_copy(cw_ref, cwf_ref.at[:, :, pl.ds(me * cwid, cwid)], lsem.at[0])
        own_cw.start()
        first = []
        for k in range(1, N_DEV):
            peer, _ = _peer(pos, k)
            rows = cbuf.at[pl.ds(me8, SUBLANES), :]
            first.append(_remote(rows, rows, ssem.at[0, k - 1], rsem.at[0, k - 1], peer))
            first.append(_remote(cw_ref, cwf_ref.at[:, :, pl.ds(me * cwid, cwid)], ssem.at[1, k - 1], rsem.at[1, k - 1], peer))
        for cp in first:
            cp.start()
        for k in range(1, N_DEV):
            peer, pj = _peer(pos, k)
            pj8 = pl.multiple_of(pj * SUBLANES, SUBLANES)
            rows = cbuf.at[pl.ds(pj8, SUBLANES), :]
            _remote(rows, rows, ssem.at[0, k - 1], rsem.at[0, k - 1], peer).wait_recv()
        cv = cbuf[...]
        cact = cv * _sigmoid(cv)
        cact_ref[...] = cact
        cb = cact.astype(BF16)
        for l in range(nl):
            pbuf[l] = _dot(cb, wm_ref[l].astype(BF16))
        own_p = pltpu.make_async_copy(pbuf.at[:, pl.ds(me8, SUBLANES), :], rbuf.at[me], lsem.at[1])
        own_p.start()
        second = []
        for k in range(1, N_DEV):
            peer, pj = _peer(pos, k)
            pj8 = pl.multiple_of(pj * SUBLANES, SUBLANES)
            second.append(_remote(pbuf.at[:, pl.ds(pj8, SUBLANES), :], rbuf.at[me], ssem.at[2, k - 1], rsem.at[2, k - 1], peer))
        for cp in second:
            cp.start()
        for k in range(1, N_DEV):
            peer, pj = _peer(pos, k)
            _remote(pbuf.at[:, pl.ds(0, SUBLANES), :], rbuf.at[pj], ssem.at[2, k - 1], rsem.at[2, k - 1], peer).wait_recv()
            _remote(cw_ref, cwf_ref.at[:, :, pl.ds(pj * cwid, cwid)], ssem.at[1, k - 1], rsem.at[1, k - 1], peer).wait_recv()
        own_p.wait()
        own_cw.wait()
        for l in range(nl):
            for j in range(N_DEV):
                mod_ref[l:l + 1, j * cm:(j + 1) * cm] = rbuf[j, l, 0:1, :] + bm_ref[l:l + 1, j * cm:(j + 1) * cm]
        for cp in first + second:
            cp.wait_send()

    return pl.pallas_call(
        body,
        name="prep_small",
        out_shape=(
            jax.ShapeDtypeStruct((nl, N_MOD * d), F32),
            jax.ShapeDtypeStruct((N_DEV * SUBLANES, d), F32),
            jax.ShapeDtypeStruct((nl, CW_ROWS, d), F32),
        ),
        in_specs=[VMEM_SPEC] * 4,
        out_specs=(VMEM_SPEC,) * 3,
        scratch_shapes=[
            pltpu.VMEM((N_DEV * SUBLANES, d), F32),
            pltpu.VMEM((nl, N_DEV * SUBLANES, cm), F32),
            pltpu.VMEM((N_DEV, nl, SUBLANES, cm), F32),
            pltpu.SemaphoreType.DMA((3, N_DEV - 1)),
            pltpu.SemaphoreType.DMA((3, N_DEV - 1)),
            pltpu.SemaphoreType.DMA((2,)),
        ],
        compiler_params=_cparams(vmem=3 * _nbytes(w_mod.shape, F32)),
    )(c, w_mod, b_mod, cw)


def _exchange_start(parts, gather, name):
    n = len(parts)
    lands = [lax.empty(((N_DEV,) + tuple(p.shape)) if gather else tuple(p.shape), p.dtype) for p in parts]

    def body(*refs):
        ins, lnd = refs[:n], refs[n:2 * n]
        ssem, rsem, token = refs[2 * n], refs[2 * n + 1], refs[-1]
        pos, me = _my_position()
        for k in range(1, N_DEV):
            peer, pj = _peer(pos, k)
            for t in range(n):
                src = ins[t] if gather else ins[t].at[pj]
                q = t * (N_DEV - 1) + k - 1
                _remote(src, lnd[t].at[me], ssem.at[q], rsem.at[q], peer).start()
        token[...] = jnp.zeros(token.shape, F32)

    out = pl.pallas_call(
        body,
        name=name,
        out_shape=(pltpu.SemaphoreType.DMA((n * (N_DEV - 1),)), pltpu.SemaphoreType.DMA((n * (N_DEV - 1),)))
        + tuple(pltpu.HBM(p.shape, p.dtype) for p in parts) + tuple(pltpu.HBM(p.shape, p.dtype) for p in lands)
        + (jax.ShapeDtypeStruct((SUBLANES, 128), F32),),
        in_specs=[HBM_SPEC] * (2 * n),
        out_specs=(SEM_SPEC, SEM_SPEC) + (HBM_SPEC,) * (2 * n) + (VMEM_SPEC,),
        input_output_aliases={i: 2 + i for i in range(2 * n)},
        compiler_params=pltpu.CompilerParams(has_side_effects=SIDE_EFFECT),
    )(*[pltpu.with_memory_space_constraint(p, pltpu.HBM) for p in list(parts) + lands])
    return out[0], out[1], list(out[2:2 + n]), list(out[2 + n:2 + 2 * n]), out[-1]


def _exchange_wait(started, after, gather, name):
    ssem, rsem, parts, lands, _ = started
    n = len(parts)

    def body(*refs):
        ins, lnd = refs[:n], refs[n:2 * n]
        ssem_ref, rsem_ref = refs[2 * n], refs[2 * n + 1]
        stage, lsem = refs[-1 - n:-1], refs[-1]
        pos, me = _my_position()
        load = []
        for t in range(n):
            src = ins[t] if gather else ins[t].at[me]
            load.append(pltpu.make_async_copy(src, stage[t], lsem.at[t]))
            load[-1].start()
        store = []
        for t in range(n):
            load[t].wait()
            store.append(pltpu.make_async_copy(stage[t], lnd[t].at[me], lsem.at[t]))
            store[-1].start()
        for k in range(1, N_DEV):
            peer, pj = _peer(pos, k)
            for t in range(n):
                src = ins[t] if gather else ins[t].at[pj]
                q = t * (N_DEV - 1) + k - 1
                _remote(src, lnd[t].at[me], ssem_ref.at[q], rsem_ref.at[q], peer).wait_send()
                _remote(src, lnd[t].at[pj], ssem_ref.at[q], rsem_ref.at[q], peer).wait_recv()
        for cp in store:
            cp.wait()

    out = pl.pallas_call(
        body,
        name=name,
        out_shape=tuple(pltpu.HBM(p.shape, p.dtype) for p in parts) + tuple(pltpu.HBM(p.shape, p.dtype) for p in lands),
        in_specs=[HBM_SPEC] * (2 * n) + [SEM_SPEC, SEM_SPEC, ANY_SPEC],
        out_specs=(HBM_SPEC,) * (2 * n),
        input_output_aliases={i: i for i in range(2 * n)},
        scratch_shapes=[pltpu.VMEM(tuple(z.shape[1:]), z.dtype) for z in lands] + [pltpu.SemaphoreType.DMA((n,))],
        compiler_params=pltpu.CompilerParams(has_side_effects=SIDE_EFFECT),
    )(*parts, *lands, ssem, rsem, after)
    return list(out[n:])


def _reduce_small(rows, gates, dm8, cact):
    r, d = rows.shape
    ng, bw, _ = gates.shape
    nl = dm8.shape[0]
    cm = dm8.shape[2] // N_DEV

    def body(rows_ref, gates_ref, dm_ref, cact_ref, orow_ref, ogate_ref, owm_ref, gr, gg, dmr, ssem, rsem, lsem):
        pos, me = _my_position()
        me8 = pl.multiple_of(me * SUBLANES, SUBLANES)
        gr[me] = rows_ref[...]
        gg[me] = gates_ref[...].astype(BF16)
        own_dm = pltpu.make_async_copy(dm_ref.at[:, :, pl.ds(me * cm, cm)], dmr.at[:, pl.ds(me8, SUBLANES), :], lsem.at[0])
        own_dm.start()
        sends = []
        for k in range(1, N_DEV):
            peer, pj = _peer(pos, k)
            sends.append(_remote(gr.at[me], gr.at[me], ssem.at[0, k - 1], rsem.at[0, k - 1], peer))
            sends.append(_remote(gg.at[me], gg.at[me], ssem.at[1, k - 1], rsem.at[1, k - 1], peer))
            sends.append(_remote(dm_ref.at[:, :, pl.ds(pj * cm, cm)], dmr.at[:, pl.ds(me8, SUBLANES), :],
                                 ssem.at[2, k - 1], rsem.at[2, k - 1], peer))
        for cp in sends:
            cp.start()
        for k in range(1, N_DEV):
            peer, pj = _peer(pos, k)
            pj8 = pl.multiple_of(pj * SUBLANES, SUBLANES)
            _remote(gr.at[pj], gr.at[pj], ssem.at[0, k - 1], rsem.at[0, k - 1], peer).wait_recv()
            _remote(gg.at[pj], gg.at[pj], ssem.at[1, k - 1], rsem.at[1, k - 1], peer).wait_recv()
            _remote(dm_ref.at[:, :, pl.ds(0, cm)], dmr.at[:, pl.ds(pj8, SUBLANES), :], ssem.at[2, k - 1], rsem.at[2, k - 1], peer).wait_recv()
        own_dm.wait()
        acc = gr[0]
        for j in range(1, N_DEV):
            acc = acc + gr[j]
        orow_ref[...] = acc
        accg = gg[0].astype(F32)
        for j in range(1, N_DEV):
            accg = accg + gg[j].astype(F32)
        ogate_ref[...] = accg
        cb = cact_ref[...].astype(BF16)
        for l in range(nl):
            owm_ref[l] = _dot_tn(cb, dmr[l].astype(BF16))
        for cp in sends:
            cp.wait_send()

    return pl.pallas_call(
        body,
        name="reduce_small",
        out_shape=(
            jax.ShapeDtypeStruct((r, d), F32),
            jax.ShapeDtypeStruct((ng, bw, bw), F32),
            jax.ShapeDtypeStruct((nl, d, cm), F32),
        ),
        in_specs=[VMEM_SPEC] * 4,
        out_specs=(VMEM_SPEC,) * 3,
        scratch_shapes=[
            pltpu.VMEM((N_DEV, r, d), F32),
            pltpu.VMEM((N_DEV, ng, bw, bw), BF16),
            pltpu.VMEM((nl, N_DEV * SUBLANES, cm), F32),
            pltpu.SemaphoreType.DMA((3, N_DEV - 1)),
            pltpu.SemaphoreType.DMA((3, N_DEV - 1)),
            pltpu.SemaphoreType.DMA((1,)),
        ],
        compiler_params=_cparams(vmem=4 * _nbytes((N_DEV, ng, bw, bw), F32) + 4 * _nbytes((N_DEV, r, d), F32)),
    )(rows, gates, dm8, cact)


def _in_proj_fwd(x, mod, rows, win_f, l, tm):
    s, d = x.shape
    nb, _, ci = win_f.shape

    def body(x_ref, mod_ref, rows_ref, w_ref, proj_ref, h_ref):
        xv = x_ref[...]
        g = rows_ref[R_G_PRE_MIX:R_G_PRE_MIX + 1, :]
        h = (xv * _rms(xv) * g) * (1.0 + mod_ref[M_SC_M:M_SC_M + 1, :]) + mod_ref[M_SH_M:M_SH_M + 1, :]
        hb = h.astype(BF16)
        h_ref[...] = hb
        for j in range(nb):
            proj_ref[:, j * ci:(j + 1) * ci] = _dot(hb, w_ref[j])

    return pl.pallas_call(
        body,
        name="in_proj_fwd",
        grid=(s // tm,),
        in_specs=[
            pl.BlockSpec((tm, d), lambda i: (i, 0)),
            _resident((None, N_MOD, d), lambda i: (l, 0, 0)),
            _resident((None, N_ROWS, d), lambda i: (l, 0, 0)),
            _resident((nb, d, ci), lambda i: (0, 0, 0)),
        ],
        out_specs=(pl.BlockSpec((tm, nb * ci), lambda i: (i, 0)), pl.BlockSpec((tm, d), lambda i: (i, 0))),
        out_shape=(jax.ShapeDtypeStruct((s, nb * ci), F32), jax.ShapeDtypeStruct((s, d), BF16)),
        compiler_params=_cparams(("parallel",), _nbytes((nb, d, ci), BF16) + 3 * _nbytes((tm, nb * ci), F32) + 8 * _nbytes((tm, d), F32)),
    )(x, mod, rows, win_f)


def _mixer_core_fwd(proj, x, mod, rows, cwf, wr, wi, wa_f, wb_f, wo_f, l, tm):
    s, d = x.shape
    nh, bw, _ = wr.shape[1:]

    def body(proj_ref, x_ref, mod_ref, rows_ref, cw_ref, wr_ref, wi_ref, wa_ref, wb_ref, wo_ref,
             x1_ref, hs_ref, yap_ref, ybp_ref, y_ref, cvbuf, xbbuf, a_s, b_s, hprev):
        i = pl.program_id(0)

        @pl.when(i == 0)
        def _():
            cvbuf[pl.ds(0, SUBLANES), :] = jnp.zeros((SUBLANES, d), F32)
            xbbuf[pl.ds(0, SUBLANES), :] = jnp.zeros((SUBLANES, d), F32)
            hprev[...] = jnp.zeros((SUBLANES, d), F32)

        def row(r):
            return rows_ref[r:r + 1, :]

        def tap(r):
            return cw_ref[r:r + 1, :]

        ba = proj_ref[:, 0:d]
        cv = proj_ref[:, d:2 * d] * proj_ref[:, 2 * d:3 * d]
        cvbuf[pl.ds(SUBLANES, tm), :] = cv
        conv3 = ((row(R_CONV_A_B) + cvbuf[pl.ds(SUBLANES - 2, tm), :] * tap(CW_A)) + cvbuf[pl.ds(SUBLANES - 1, tm), :] * tap(CW_A + 1)) + cv * tap(CW_A + 2)
        ya = ba * conv3
        cvbuf[pl.ds(0, SUBLANES), :] = cvbuf[pl.ds(tm, SUBLANES), :]
        xb = proj_ref[:, 3 * d:4 * d]
        xbbuf[pl.ds(SUBLANES, tm), :] = xb
        xc = (((row(R_CONV_B_B) + xbbuf[pl.ds(SUBLANES - 3, tm), :] * tap(CW_B)) + xbbuf[pl.ds(SUBLANES - 2, tm), :] * tap(CW_B + 1))
              + xbbuf[pl.ds(SUBLANES - 1, tm), :] * tap(CW_B + 2)) + xb * tap(CW_B + 3)
        xbbuf[pl.ds(0, SUBLANES), :] = xbbuf[pl.ds(tm, SUBLANES), :]
        sp = _softplus_neg(row(R_LAMBDA))
        _, _, ig, a, mult = _gates(xc, wr_ref, wi_ref, row(R_B_GATE_R), row(R_B_GATE_I), sp, nh, bw)
        a_s[...] = a
        b_s[...] = mult * (ig * xc)

        def blk(j, hp):
            o = pl.multiple_of(j * SUBLANES, SUBLANES)
            a8, b8 = _scan_block(a_s[pl.ds(o, SUBLANES), :], b_s[pl.ds(o, SUBLANES), :], reverse=False)
            h8 = b8 + a8 * hp
            hs_ref[pl.ds(o, SUBLANES), :] = h8
            return jnp.broadcast_to(h8[SUBLANES - 1:SUBLANES, :], (SUBLANES, d))

        hprev[...] = lax.fori_loop(0, tm // SUBLANES, blk, hprev[...])
        gel, _ = _gelu(proj_ref[:, 4 * d:5 * d])
        yb = hs_ref[...] * gel
        yap = _dot(ya.astype(BF16), wa_ref[...])
        ybp = _dot(yb.astype(BF16), wb_ref[...])
        yap_ref[...] = yap
        ybp_ref[...] = ybp
        m = _sigmoid(proj_ref[:, 5 * d:6 * d]) * yap + _sigmoid(proj_ref[:, 6 * d:7 * d]) * ybp
        y = _dot(m.astype(BF16), wo_ref[...])
        y_ref[...] = y
        x1_ref[...] = x_ref[...] + mod_ref[M_GT_M:M_GT_M + 1, :] * ((y * _rms(y)) * row(R_G_POST_MIX))

    tile = pl.BlockSpec((tm, d), lambda i: (i, 0))
    return pl.pallas_call(
        body,
        name="mixer_core_fwd",
        grid=(s // tm,),
        in_specs=[
            pl.BlockSpec((tm, 7 * d), lambda i: (i, 0)),
            tile,
            _resident((None, N_MOD, d), lambda i: (l, 0, 0)),
            _resident((None, N_ROWS, d), lambda i: (l, 0, 0)),
            _resident((None, CW_ROWS, d), lambda i: (l, 0, 0)),
            _resident((None, nh, bw, bw), lambda i: (l, 0, 0, 0)),
            _resident((None, nh, bw, bw), lambda i: (l, 0, 0, 0)),
            _resident((d, d), lambda i: (0, 0)),
            _resident((d, d), lambda i: (0, 0)),
            _resident((d, d), lambda i: (0, 0)),
        ],
        out_specs=(tile,) * 5,
        out_shape=(jax.ShapeDtypeStruct((s, d), F32),) * 5,
        scratch_shapes=[
            pltpu.VMEM((tm + SUBLANES, d), F32),
            pltpu.VMEM((tm + SUBLANES, d), F32),
            pltpu.VMEM((tm, d), F32),
            pltpu.VMEM((tm, d), F32),
            pltpu.VMEM((SUBLANES, d), F32),
        ],
        compiler_params=_cparams(("arbitrary",), 3 * _nbytes((d, d), BF16) + 2 * _nbytes((tm, 7 * d), F32) + 40 * _nbytes((tm, d), F32)),
    )(proj, x, mod, rows, cwf, wr, wi, wa_f, wb_f, wo_f)


def _mlp_fwd(x1, mod, rows, wup_f, wdn_f, l, tm):
    s, d = x1.shape
    nb, _, cu = wup_f.shape
    dff = nb * cu

    def body(x1_ref, mod_ref, rows_ref, wu_ref, wd_ref, x2_ref, ru_ref, y2_ref, h2_ref):
        xv = x1_ref[...]
        g = rows_ref[R_G_PRE_MLP:R_G_PRE_MLP + 1, :]
        h2 = ((xv * _rms(xv) * g) * (1.0 + mod_ref[M_SC_F:M_SC_F + 1, :]) + mod_ref[M_SH_F:M_SH_F + 1, :]).astype(BF16)
        h2_ref[...] = h2
        ru = jnp.concatenate([jnp.maximum(_dot(h2, wu_ref[j]), 0.0) for j in range(nb)], axis=1)
        ru_ref[...] = ru.astype(BF16)
        y2 = _dot((ru * ru).astype(BF16), wd_ref[...])
        y2_ref[...] = y2
        x2_ref[...] = xv + mod_ref[M_GT_F:M_GT_F + 1, :] * ((y2 * _rms(y2)) * rows_ref[R_G_POST_MLP:R_G_POST_MLP + 1, :])

    tile = pl.BlockSpec((tm, d), lambda i: (i, 0))
    wide = pl.BlockSpec((tm, dff), lambda i: (i, 0))
    return pl.pallas_call(
        body,
        name="mlp_fwd",
        grid=(s // tm,),
        in_specs=[
            tile,
            _resident((None, N_MOD, d), lambda i: (l, 0, 0)),
            _resident((None, N_ROWS, d), lambda i: (l, 0, 0)),
            _resident((nb, d, cu), lambda i: (0, 0, 0)),
            _resident((dff, d), lambda i: (0, 0)),
        ],
        out_specs=(tile, wide, tile, tile),
        out_shape=(jax.ShapeDtypeStruct((s, d), F32), jax.ShapeDtypeStruct((s, dff), BF16),
                   jax.ShapeDtypeStruct((s, d), F32), jax.ShapeDtypeStruct((s, d), BF16)),
        compiler_params=_cparams(("parallel",), 2 * _nbytes((dff, d), BF16) + 5 * _nbytes((tm, dff), F32) + 12 * _nbytes((tm, d), F32)),
    )(x1, mod, rows, wup_f, wdn_f)


def _loss_fwd_bwd(y, target, tm):
    s, d = y.shape

    def body(y_ref, t_ref, loss_ref, dy_ref):
        @pl.when(pl.program_id(0) == 0)
        def _():
            loss_ref[...] = jnp.zeros(loss_ref.shape, F32)

        e = y_ref[...] - t_ref[...]
        dy_ref[...] = e * (1.0 / d)
        loss_ref[...] += 0.5 * jnp.sum(jnp.mean(e * e, axis=-1, keepdims=True), axis=0, keepdims=True)

    tile = pl.BlockSpec((tm, d), lambda i: (i, 0))
    loss, dy = pl.pallas_call(
        body,
        name="loss",
        grid=(s // tm,),
        in_specs=[tile, tile],
        out_specs=(pl.BlockSpec((SUBLANES, 128), lambda i: (0, 0)), tile),
        out_shape=(jax.ShapeDtypeStruct((SUBLANES, 128), F32), jax.ShapeDtypeStruct((s, d), F32)),
        compiler_params=_cparams(("arbitrary",)),
    )(y, target)
    return loss[0, 0], dy


def _mlp_bwd(dx2, x1, y2, ru, mod, rows, wup_f, wdn_f, l, tm):
    s, d = x1.shape
    nb, _, cu = wup_f.shape
    dff = nb * cu

    def body(dx2_ref, x1_ref, y2_ref, ru_ref, mod_ref, rows_ref, wu_ref, wd_ref, dx1_ref, dy2_ref, dup_ref, act_ref, sm_ref):
        @pl.when(pl.program_id(0) == 0)
        def _():
            sm_ref[...] = jnp.zeros(sm_ref.shape, F32)

        dout = dx2_ref[...]
        dy2, dgt, dgpost = _postnorm_bwd(y2_ref[...], dout, rows_ref[R_G_POST_MLP:R_G_POST_MLP + 1, :], mod_ref[M_GT_F:M_GT_F + 1, :])
        dy2b = dy2.astype(BF16)
        dy2_ref[...] = dy2b
        ruv = ru_ref[...].astype(F32)
        act_ref[...] = (ruv * ruv).astype(BF16)
        dup = (_dot_nt(dy2b, wd_ref[...]) * (2.0 * ruv)).astype(BF16)
        dup_ref[...] = dup
        dh2 = _dot_nt(dup[:, 0:cu], wu_ref[0])
        for j in range(1, nb):
            dh2 = dh2 + _dot_nt(dup[:, j * cu:(j + 1) * cu], wu_ref[j])
        dxn, dsc, dsh, dgpre = _prenorm_bwd(x1_ref[...], dh2, rows_ref[R_G_PRE_MLP:R_G_PRE_MLP + 1, :], mod_ref[M_SC_F:M_SC_F + 1, :])
        dx1_ref[...] = dout + dxn
        for r, v in ((G_MLP_GT, dgt), (G_MLP_GPOST, dgpost), (G_MLP_SC, dsc), (G_MLP_SH, dsh), (G_MLP_GPRE, dgpre)):
            sm_ref[r:r + 1, :] += v

    tile = pl.BlockSpec((tm, d), lambda i: (i, 0))
    wide = pl.BlockSpec((tm, dff), lambda i: (i, 0))
    return pl.pallas_call(
        body,
        name="mlp_bwd",
        grid=(s // tm,),
        in_specs=[
            tile, tile, tile, wide,
            _resident((None, N_MOD, d), lambda i: (l, 0, 0)),
            _resident((None, N_ROWS, d), lambda i: (l, 0, 0)),
            _resident((nb, d, cu), lambda i: (0, 0, 0)),
            _resident((dff, d), lambda i: (0, 0)),
        ],
        out_specs=(tile, tile, wide, wide, pl.BlockSpec((G_MLP_ROWS, d), lambda i: (0, 0))),
        out_shape=(jax.ShapeDtypeStruct((s, d), F32), jax.ShapeDtypeStruct((s, d), BF16), jax.ShapeDtypeStruct((s, dff), BF16),
                   jax.ShapeDtypeStruct((s, dff), BF16), jax.ShapeDtypeStruct((G_MLP_ROWS, d), F32)),
        compiler_params=_cparams(("arbitrary",), 2 * _nbytes((dff, d), BF16) + 6 * _nbytes((tm, dff), F32) + 16 * _nbytes((tm, d), F32)),
    )(dx2, x1, y2, ru, mod, rows, wup_f, wdn_f)


def _mixer_core_bwd(dx1, y, yap, ybp, hs, proj, mod, rows, cwf, wr, wi, wa_f, wb_f, wo_f, l, tm):
    s, d = dx1.shape
    nh, bw, _ = wr.shape[1:]
    nt = s // tm
    per = tm // SUBLANES

    def body(dx1_ref, y_ref, yap_ref, ybp_ref, hs_ref, hsh_ref, proj_ref, projh_ref, mod_ref, rows_ref, cw_ref,
             wr_ref, wi_ref, wa_ref, wb_ref, wo_ref,
             dproj_ref, dy_ref, m_ref, dyap_ref, dybp_ref, ya_ref, yb_ref, sm_ref, dwg_ref,
             cvbuf, xbbuf, hsbuf, abuf, dcbuf, dxbuf, al_s, dh_s, lam_s, lnext):
        i = pl.program_id(0)
        first_tile = i == nt - 1

        @pl.when(i == 0)
        def _():
            sm_ref[...] = jnp.zeros(sm_ref.shape, F32)
            dwg_ref[...] = jnp.zeros(dwg_ref.shape, F32)
            zero = jnp.zeros((SUBLANES, d), F32)
            abuf[pl.ds(tm, SUBLANES), :] = zero
            dcbuf[pl.ds(tm, SUBLANES), :] = zero
            dxbuf[pl.ds(tm, SUBLANES), :] = zero
            lnext[...] = zero

        def row(r):
            return rows_ref[r:r + 1, :]

        def tap(r):
            return cw_ref[r:r + 1, :]

        def acc(r, v):
            sm_ref[r:r + 1, :] += v

        keep_halo = jnp.where(first_tile, 0.0, 1.0)
        dy, dgt, dgpost = _postnorm_bwd(y_ref[...], dx1_ref[...], row(R_G_POST_MIX), mod_ref[M_GT_M:M_GT_M + 1, :])
        acc(G_MIX_GT, dgt)
        acc(G_MIX_GPOST, dgpost)
        dyb16 = dy.astype(BF16)
        dy_ref[...] = dyb16
        dm = _dot_nt(dyb16, wo_ref[...])
        sa = _sigmoid(proj_ref[:, 5 * d:6 * d])
        sb = _sigmoid(proj_ref[:, 6 * d:7 * d])
        yap = yap_ref[...]
        ybp = ybp_ref[...]
        m_ref[...] = (sa * yap + sb * ybp).astype(BF16)
        dyap = (dm * sa).astype(BF16)
        dybp = (dm * sb).astype(BF16)
        dyap_ref[...] = dyap
        dybp_ref[...] = dybp
        dproj_ref[:, 5 * d:6 * d] = (dm * yap * sa * (1.0 - sa)).astype(BF16)
        dproj_ref[:, 6 * d:7 * d] = (dm * ybp * sb * (1.0 - sb)).astype(BF16)
        dya = _dot_nt(dyap, wa_ref[...])
        dyb = _dot_nt(dybp, wb_ref[...])
        ba = proj_ref[:, 0:d]
        ca = proj_ref[:, d:2 * d]
        va = proj_ref[:, 2 * d:3 * d]
        cv = ca * va
        cvbuf[pl.ds(0, SUBLANES), :] = keep_halo * (projh_ref[:, d:2 * d] * projh_ref[:, 2 * d:3 * d])
        cvbuf[pl.ds(SUBLANES, tm), :] = cv
        cvm2 = cvbuf[pl.ds(SUBLANES - 2, tm), :]
        cvm1 = cvbuf[pl.ds(SUBLANES - 1, tm), :]
        conv3 = ((row(R_CONV_A_B) + cvm2 * tap(CW_A)) + cvm1 * tap(CW_A + 1)) + cv * tap(CW_A + 2)
        ya_ref[...] = (ba * conv3).astype(BF16)
        dproj_ref[:, 0:d] = (dya * conv3).astype(BF16)
        dc3 = dya * ba
        acc(G_MIX_CAB, _colsum(dc3))
        acc(G_MIX_CAW, _colsum(dc3 * cvm2))
        acc(G_MIX_CAW + 1, _colsum(dc3 * cvm1))
        acc(G_MIX_CAW + 2, _colsum(dc3 * cv))
        dcbuf[pl.ds(0, tm), :] = dc3
        dcv = (dc3 * tap(CW_A + 2) + dcbuf[pl.ds(1, tm), :] * tap(CW_A + 1)) + dcbuf[pl.ds(2, tm), :] * tap(CW_A)
        dcbuf[pl.ds(tm, SUBLANES), :] = dcbuf[pl.ds(0, SUBLANES), :]
        dproj_ref[:, d:2 * d] = (dcv * va).astype(BF16)
        dproj_ref[:, 2 * d:3 * d] = (dcv * ca).astype(BF16)
        xb = proj_ref[:, 3 * d:4 * d]
        gb = proj_ref[:, 4 * d:5 * d]
        xbbuf[pl.ds(0, SUBLANES), :] = keep_halo * projh_ref[:, 3 * d:4 * d]
        xbbuf[pl.ds(SUBLANES, tm), :] = xb
        xm3 = xbbuf[pl.ds(SUBLANES - 3, tm), :]
        xm2 = xbbuf[pl.ds(SUBLANES - 2, tm), :]
        xm1 = xbbuf[pl.ds(SUBLANES - 1, tm), :]
        xc = (((row(R_CONV_B_B) + xm3 * tap(CW_B)) + xm2 * tap(CW_B + 1)) + xm1 * tap(CW_B + 2)) + xb * tap(CW_B + 3)
        lam = row(R_LAMBDA)
        sp = _softplus_neg(lam)
        xcb, r, ig, a, mult = _gates(xc, wr_ref, wi_ref, row(R_B_GATE_R), row(R_B_GATE_I), sp, nh, bw)
        gel, th = _gelu(gb)
        hs = hs_ref[...]
        yb_ref[...] = (hs * gel).astype(BF16)
        dproj_ref[:, 4 * d:5 * d] = (dyb * hs * _gelu_grad(gb, th)).astype(BF16)
        abuf[pl.ds(0, tm), :] = a
        al_s[...] = abuf[pl.ds(1, tm), :]
        abuf[pl.ds(tm, SUBLANES), :] = abuf[pl.ds(0, SUBLANES), :]
        dh_s[...] = dyb * gel

        def blk(j, ln):
            o = pl.multiple_of((per - 1 - j) * SUBLANES, SUBLANES)
            a8, b8 = _scan_block(al_s[pl.ds(o, SUBLANES), :], dh_s[pl.ds(o, SUBLANES), :], reverse=True)
            l8 = b8 + a8 * ln
            lam_s[pl.ds(o, SUBLANES), :] = l8
            return jnp.broadcast_to(l8[0:1, :], (SUBLANES, d))

        lnext[...] = lax.fori_loop(0, per, blk, lnext[...])
        dbb = lam_s[...]
        hsbuf[pl.ds(0, SUBLANES), :] = keep_halo * hsh_ref[...]
        hsbuf[pl.ds(SUBLANES, tm), :] = hs
        da = dbb * hsbuf[pl.ds(SUBLANES - 1, tm), :]
        dmult = dbb * (ig * xc)
        dig = dbb * (mult * xc)
        dxc = dbb * (mult * ig)
        dla = da * a - dmult * (a * a) / mult
        acc(G_MIX_LAM, _colsum(dla * (-LRU_C * r)) * (-_sigmoid(-lam)))
        dzr = (dla * (-LRU_C * sp)) * r * (1.0 - r)
        dzi = dig * ig * (1.0 - ig)
        acc(G_MIX_BR, _colsum(dzr))
        acc(G_MIX_BI, _colsum(dzi))
        dzrb = dzr.astype(BF16)
        dzib = dzi.astype(BF16)
        back = []
        for h in range(nh):
            sl = slice(h * bw, (h + 1) * bw)
            back.append(_dot_nt(dzrb[:, sl], wr_ref[h]) + _dot_nt(dzib[:, sl], wi_ref[h]))
            dwg_ref[0, h] += _dot_tn(xcb[:, sl], dzrb[:, sl])
            dwg_ref[1, h] += _dot_tn(xcb[:, sl], dzib[:, sl])
        dxc = dxc + jnp.concatenate(back, axis=1)
        acc(G_MIX_CBB, _colsum(dxc))
        acc(G_MIX_CBW, _colsum(dxc * xm3))
        acc(G_MIX_CBW + 1, _colsum(dxc * xm2))
        acc(G_MIX_CBW + 2, _colsum(dxc * xm1))
        acc(G_MIX_CBW + 3, _colsum(dxc * xb))
        dxbuf[pl.ds(0, tm), :] = dxc
        dxb = ((dxc * tap(CW_B + 3) + dxbuf[pl.ds(1, tm), :] * tap(CW_B + 2)) + dxbuf[pl.ds(2, tm), :] * tap(CW_B + 1)) + dxbuf[pl.ds(3, tm), :] * tap(CW_B)
        dxbuf[pl.ds(tm, SUBLANES), :] = dxbuf[pl.ds(0, SUBLANES), :]
        dproj_ref[:, 3 * d:4 * d] = dxb.astype(BF16)

    def rev(i):
        return (nt - 1 - i, 0)

    def halo(i):
        return (jnp.maximum((nt - 1 - i) * per - 1, 0), 0)

    tile = pl.BlockSpec((tm, d), rev)
    return pl.pallas_call(
        body,
        name="mixer_core_bwd",
        grid=(nt,),
        in_specs=[
            tile, tile, tile, tile, tile,
            pl.BlockSpec((SUBLANES, d), halo),
            pl.BlockSpec((tm, 7 * d), rev),
            pl.BlockSpec((SUBLANES, 7 * d), halo),
            _resident((None, N_MOD, d), lambda i: (l, 0, 0)),
            _resident((None, N_ROWS, d), lambda i: (l, 0, 0)),
            _resident((None, CW_ROWS, d), lambda i: (l, 0, 0)),
            _resident((None, nh, bw, bw), lambda i: (l, 0, 0, 0)),
            _resident((None, nh, bw, bw), lambda i: (l, 0, 0, 0)),
            _resident((d, d), lambda i: (0, 0)),
            _resident((d, d), lambda i: (0, 0)),
            _resident((d, d), lambda i: (0, 0)),
        ],
        out_specs=(pl.BlockSpec((tm, 7 * d), rev),) + (tile,) * 6 + (
            pl.BlockSpec((G_MIX_ROWS, d), lambda i: (0, 0)), pl.BlockSpec((2, nh, bw, bw), lambda i: (0, 0, 0, 0))),
        out_shape=(jax.ShapeDtypeStruct((s, 7 * d), BF16),) + (jax.ShapeDtypeStruct((s, d), BF16),) * 6 + (
            jax.ShapeDtypeStruct((G_MIX_ROWS, d), F32), jax.ShapeDtypeStruct((2, nh, bw, bw), F32)),
        scratch_shapes=[pltpu.VMEM((tm + SUBLANES, d), F32)] * 6 + [pltpu.VMEM((tm, d), F32)] * 3 + [pltpu.VMEM((SUBLANES, d), F32)],
        compiler_params=_cparams(("arbitrary",), 3 * _nbytes((d, d), BF16) + 3 * _nbytes((tm, 7 * d), F32) + 64 * _nbytes((tm, d), F32)),
    )(dx1, y, yap, ybp, hs, hs, proj, proj, mod, rows, cwf, wr, wi, wa_f, wb_f, wo_f)


def _in_proj_bwd(dproj, x, dx1, mod, rows, win_f, l, tm):
    s, d = x.shape
    nb, _, ci = win_f.shape

    def body(dp_ref, x_ref, dx1_ref, mod_ref, rows_ref, w_ref, dx_ref, sm_ref):
        @pl.when(pl.program_id(0) == 0)
        def _():
            sm_ref[...] = jnp.zeros(sm_ref.shape, F32)

        dh = _dot_nt(dp_ref[:, 0:ci], w_ref[0])
        for j in range(1, nb):
            dh = dh + _dot_nt(dp_ref[:, j * ci:(j + 1) * ci], w_ref[j])
        dxn, dsc, dsh, dg = _prenorm_bwd(x_ref[...], dh, rows_ref[R_G_PRE_MIX:R_G_PRE_MIX + 1, :], mod_ref[M_SC_M:M_SC_M + 1, :])
        dx_ref[...] = dx1_ref[...] + dxn
        for r, v in ((G_IN_SC, dsc), (G_IN_SH, dsh), (G_IN_GPRE, dg)):
            sm_ref[r:r + 1, :] += v

    tile = pl.BlockSpec((tm, d), lambda i: (i, 0))
    return pl.pallas_call(
        body,
        name="in_proj_bwd",
        grid=(s // tm,),
        in_specs=[
            pl.BlockSpec((tm, nb * ci), lambda i: (i, 0)), tile, tile,
            _resident((None, N_MOD, d), lambda i: (l, 0, 0)),
            _resident((None, N_ROWS, d), lambda i: (l, 0, 0)),
            _resident((nb, d, ci), lambda i: (0, 0, 0)),
        ],
        out_specs=(tile, pl.BlockSpec((G_IN_ROWS, d), lambda i: (0, 0))),
        out_shape=(jax.ShapeDtypeStruct((s, d), F32), jax.ShapeDtypeStruct((G_IN_ROWS, d), F32)),
        compiler_params=_cparams(("arbitrary",), _nbytes((nb, d, ci), BF16) + 2 * _nbytes((tm, nb * ci), BF16) + 16 * _nbytes((tm, d), F32)),
    )(dproj, x, dx1, mod, rows, win_f)


def _wgrad(a, b, cols_owned, ts):
    s, k1 = a.shape
    k2 = b.shape[1]
    ns = s // ts
    if cols_owned:
        bk1, bk2 = k1, k2 // N_DEV
        a_spec = pl.BlockSpec((ts, bk1), lambda j, t: (t, 0))
        b_spec = pl.BlockSpec((ts, bk2), lambda j, t: (t, j))
    else:
        bk1, bk2 = k1 // N_DEV, k2
        a_spec = pl.BlockSpec((ts, bk1), lambda j, t: (t, j))
        b_spec = pl.BlockSpec((ts, bk2), lambda j, t: (t, 0))

    def body(a_ref, b_ref, o_ref, acc_ref):
        t = pl.program_id(1)

        @pl.when(t == 0)
        def _():
            acc_ref[...] = jnp.zeros(acc_ref.shape, F32)

        acc_ref[...] += _dot_tn(a_ref[...], b_ref[...])

        @pl.when(t == ns - 1)
        def _():
            o_ref[...] = acc_ref[...].astype(BF16)

    return pl.pallas_call(
        body,
        name="wgrad",
        grid=(N_DEV, ns),
        in_specs=[a_spec, b_spec],
        out_specs=pl.BlockSpec((None, bk1, bk2), lambda j, t: (j, 0, 0)),
        out_shape=jax.ShapeDtypeStruct((N_DEV, bk1, bk2), BF16),
        scratch_shapes=[pltpu.VMEM((bk1, bk2), F32)],
        compiler_params=_cparams(("parallel", "arbitrary"), 4 * _nbytes((bk1, bk2), F32) + 4 * _nbytes((ts, bk1 + bk2), BF16)),
    )(a, b)


def _adam_update(w, g, m, v):
    m = ADAM_B1 * m + (1.0 - ADAM_B1) * g
    v = ADAM_B2 * v + (1.0 - ADAM_B2) * (g * g)
    m_hat = m / (1.0 - ADAM_B1 ** ADAM_STEP)
    v_hat = v / (1.0 - ADAM_B2 ** ADAM_STEP)
    delta = -ADAM_LR * (m_hat / (jnp.sqrt(v_hat) + ADAM_EPS) + ADAM_WD * w)
    return delta, m, v


def _sum_adamw(recv, w, m, v, tr):
    nl, ra, cb = w.shape
    assert nl == len(recv) == 2

    def body(r0_ref, r1_ref, w_ref, m_ref, v_ref, g_ref, d_ref, nm_ref, nv_ref):
        def total(r_ref):
            g = r_ref[0].astype(F32)
            for j in range(1, N_DEV):
                g = g + r_ref[j].astype(F32)
            return g

        g = jnp.where(pl.program_id(0) == 0, total(r0_ref), total(r1_ref))
        g_ref[...] = g
        d_ref[...], nm_ref[...], nv_ref[...] = _adam_update(w_ref[...], g, m_ref[...], v_ref[...])

    blk = pl.BlockSpec((None, tr, cb), lambda l, i: (l, i, 0))
    return pl.pallas_call(
        body,
        name="sum_adamw",
        grid=(nl, ra // tr),
        in_specs=[pl.BlockSpec((N_DEV, tr, cb), lambda l, i: (0, i * (1 - l), 0)),
                  pl.BlockSpec((N_DEV, tr, cb), lambda l, i: (0, i * l, 0)), blk, blk, blk],
        out_specs=(blk,) * 4,
        out_shape=(jax.ShapeDtypeStruct((nl, ra, cb), F32),) * 4,
        compiler_params=_cparams(("arbitrary", "arbitrary")),
    )(recv[0], recv[1], w, m, v)


def _adamw(w, g, m, v):
    def body(w_ref, g_ref, m_ref, v_ref, d_ref, nm_ref, nv_ref):
        d_ref[...], nm_ref[...], nv_ref[...] = _adam_update(w_ref[...], g_ref[...], m_ref[...], v_ref[...])

    return pl.pallas_call(
        body,
        name="adamw",
        in_specs=[VMEM_SPEC] * 4,
        out_specs=(VMEM_SPEC,) * 3,
        out_shape=(jax.ShapeDtypeStruct(w.shape, F32),) * 3,
        compiler_params=_cparams(vmem=10 * _nbytes(w.shape, F32)),
    )(w, g, m, v)


def _adamw_tiled(w, g, m, v, tr):
    nl, ra, cb = w.shape

    def body(w_ref, g_ref, m_ref, v_ref, d_ref, nm_ref, nv_ref):
        d_ref[...], nm_ref[...], nv_ref[...] = _adam_update(w_ref[...], g_ref[...], m_ref[...], v_ref[...])

    blk = pl.BlockSpec((None, tr, cb), lambda l, i: (l, i, 0))
    return pl.pallas_call(
        body,
        name="adamw_tiled",
        grid=(nl, ra // tr),
        in_specs=[blk] * 4,
        out_specs=(blk,) * 3,
        out_shape=(jax.ShapeDtypeStruct((nl, ra, cb), F32),) * 3,
        compiler_params=_cparams(("parallel", "parallel")),
    )(w, g, m, v)


def _token_tile(s):
    return min(256, max(SUBLANES * 2, s // 4))


def kernel(x, c, w_mod, b_mod, g_pre_mix, g_post_mix, w_in, conv_a_w, conv_a_b, w_a_out, conv_b_w, conv_b_b, w_gate_r, b_gate_r, w_gate_i, b_gate_i, lru_lambda, w_b_out, w_o, g_pre_mlp, g_post_mlp, w_mlp_up, w_mlp_down, loss_target, m_w_mod, m_b_mod, m_g_pre_mix, m_g_post_mix, m_w_in, m_conv_a_w, m_conv_a_b, m_w_a_out, m_conv_b_w, m_conv_b_b, m_w_gate_r, m_b_gate_r, m_w_gate_i, m_b_gate_i, m_lru_lambda, m_w_b_out, m_w_o, m_g_pre_mlp, m_g_post_mlp, m_w_mlp_up, m_w_mlp_down, v_w_mod, v_b_mod, v_g_pre_mix, v_g_post_mix, v_w_in, v_conv_a_w, v_conv_a_b, v_w_a_out, v_conv_b_w, v_conv_b_b, v_w_gate_r, v_b_gate_r, v_w_gate_i, v_b_gate_i, v_lru_lambda, v_w_b_out, v_w_o, v_g_pre_mlp, v_g_post_mlp, v_w_mlp_up, v_w_mlp_down):
    nl = w_mod.shape[0]
    s, d = x.shape[1], x.shape[2]
    nh, bw = w_gate_r.shape[1], w_gate_r.shape[2]
    cwid = conv_a_w.shape[2]
    tm = _token_tile(s)
    ts = min(512, s)
    _, me = _my_position()
    xs = x.reshape(s, d)
    target = loss_target.reshape(s, d)

    vec_names = (g_pre_mix, g_post_mix, conv_a_b, conv_b_b, b_gate_r, b_gate_i, lru_lambda, g_pre_mlp, g_post_mlp)
    rows = jnp.concatenate([jnp.stack(vec_names, axis=1), jnp.zeros((nl, N_ROWS - len(vec_names), d), F32)], axis=1)
    cw = jnp.concatenate([conv_a_w, conv_b_w, jnp.zeros((nl, CW_ROWS - 7, cwid), F32)], axis=1)

    w16 = {"w_in": w_in.astype(BF16), "w_a_out": w_a_out.astype(BF16), "w_b_out": w_b_out.astype(BF16), "w_o": w_o.astype(BF16),
           "w_mlp_up": w_mlp_up.astype(BF16), "w_mlp_down": w_mlp_down.astype(BF16)}
    groups = (("in", ("w_in",)), ("mix", ("w_a_out", "w_b_out", "w_o")), ("mlp", ("w_mlp_up", "w_mlp_down")))
    mod, cact, cwf = _prep_small(c, w_mod, b_mod, cw)
    mod = mod.reshape(nl, N_MOD, d)
    w16, _ = lax.optimization_barrier((w16, mod))
    gathers = {}
    token = jnp.zeros((), F32)
    for l in range(nl):
        for gname, members in groups:
            gathers[l, gname] = _exchange_start([w16[n][l] for n in members], True, f"gather_start_{gname}{l}")
            token = token + gathers[l, gname][4][0, 0]
    rows = rows + token
    wr = w_gate_r.astype(BF16)
    wi = w_gate_i.astype(BF16)

    saved = []
    weights = []
    xin = xs
    for l in range(nl):
        (win_f,) = _exchange_wait(gathers[l, "in"], mod if l == 0 else xin, True, f"gather_wait_in{l}")
        proj, h = _in_proj_fwd(xin, mod, rows, win_f, l, tm)
        wa_f, wb_f, wo_f = (w.reshape(d, d) for w in _exchange_wait(gathers[l, "mix"], proj, True, f"gather_wait_mix{l}"))
        x1, hs, yap, ybp, y = _mixer_core_fwd(proj, xin, mod, rows, cwf, wr, wi, wa_f, wb_f, wo_f, l, tm)
        wup_f, wdn_f = _exchange_wait(gathers[l, "mlp"], x1, True, f"gather_wait_mlp{l}")
        wdn_f = wdn_f.reshape(-1, d)
        x2, ru, y2, h2 = _mlp_fwd(x1, mod, rows, wup_f, wdn_f, l, tm)
        saved.append((xin, proj, h, x1, hs, yap, ybp, y, ru, y2, h2))
        weights.append((win_f, wa_f, wb_f, wo_f, wup_f, wdn_f))
        xin = x2
    loss_part, dx = _loss_fwd_bwd(xin, target, tm)
    loss = lax.psum(loss_part, ("x", "y", "c"))

    scatters = {}
    small = [None] * nl
    gate_parts = [None] * nl

    def scatter(l, gname, parts, rows):
        scatters[l, gname] = _exchange_start(parts, False, f"scatter_start_{gname}{l}")
        return rows + scatters[l, gname][4][0, 0]

    for l in reversed(range(nl)):
        xin, proj, h, x1, hs, yap, ybp, y, ru, y2, h2 = saved[l]
        win_f, wa_f, wb_f, wo_f, wup_f, wdn_f = weights[l]
        dx1, dy2, dup, act, sm_mlp = _mlp_bwd(dx, x1, y2, ru, mod, rows, wup_f, wdn_f, l, tm)
        rows = scatter(l, "mlp", [_wgrad(h2, dup, True, ts), _wgrad(act, dy2, False, ts)], rows)
        dproj, dy, m, dyap, dybp, ya, yb, sm_mix, dwg = _mixer_core_bwd(dx1, y, yap, ybp, hs, proj, mod, rows, cwf, wr, wi, wa_f, wb_f, wo_f, l, tm // 2)
        rows = scatter(l, "mix", [_wgrad(ya, dyap, False, ts), _wgrad(yb, dybp, False, ts), _wgrad(m, dy, False, ts)], rows)
        rows = scatter(l, "in", [_wgrad(h, dproj, True, ts)], rows)
        dx, sm_in = _in_proj_bwd(dproj, xin, dx1, mod, rows, win_f, l, tm)
        small[l] = jnp.concatenate([sm_mlp, sm_mix, sm_in], axis=0)
        gate_parts[l] = dwg
    grad_x = dx.reshape(x.shape)

    lrows = jnp.concatenate(small, axis=0)
    gates_p = jnp.concatenate(gate_parts, axis=0).reshape(nl * 2 * nh, bw, bw)

    def lrow(a, l, r):
        return a[l * G_LAYER_ROWS + r]

    dm = jnp.stack([jnp.concatenate([lrow(lrows, l, G_MLP_ROWS + G_MIX_ROWS + G_IN_SH), lrow(lrows, l, G_MLP_ROWS + G_MIX_ROWS + G_IN_SC),
                                     lrow(lrows, l, G_MLP_ROWS + G_MIX_GT), lrow(lrows, l, G_MLP_SH), lrow(lrows, l, G_MLP_SC),
                                     lrow(lrows, l, G_MLP_GT)]) for l in range(nl)])
    dm8 = jnp.concatenate([dm[:, None, :], jnp.zeros((nl, SUBLANES - 1, N_MOD * d), F32)], axis=1)
    srows, sgates, g_w_mod = _reduce_small(lrows, gates_p, dm8, cact)
    sgates = sgates.reshape(nl, 2, nh, bw, bw)

    recv = {}
    for l in reversed(range(nl)):
        for gname, members in groups:
            for n, zone in zip(members, _exchange_wait(scatters[l, gname], srows, False, f"scatter_wait_{gname}{l}")):
                recv[n, l] = zone
    big = {}
    for name, w, m_, v_ in (("w_in", w_in, m_w_in, v_w_in), ("w_mlp_up", w_mlp_up, m_w_mlp_up, v_w_mlp_up),
                            ("w_a_out", w_a_out, m_w_a_out, v_w_a_out), ("w_b_out", w_b_out, m_w_b_out, v_w_b_out),
                            ("w_o", w_o, m_w_o, v_w_o), ("w_mlp_down", w_mlp_down, m_w_mlp_down, v_w_mlp_down)):
        big[name] = _sum_adamw([recv[name, l] for l in range(nl)], w, m_, v_, min(128, w.shape[1]))

    def srow(l, r):
        return lrow(srows, l, r)

    def per_layer(r):
        return jnp.stack([srow(l, r) for l in range(nl)])

    mix0 = G_MLP_ROWS
    in0 = G_MLP_ROWS + G_MIX_ROWS
    g_b_mod = jnp.stack([jnp.concatenate([srow(l, in0 + G_IN_SH), srow(l, in0 + G_IN_SC), srow(l, mix0 + G_MIX_GT),
                                          srow(l, G_MLP_SH), srow(l, G_MLP_SC), srow(l, G_MLP_GT)]) for l in range(nl)])
    conv_a_full = jnp.stack([jnp.stack([srow(l, mix0 + G_MIX_CAW + k) for k in range(3)]) for l in range(nl)])
    conv_b_full = jnp.stack([jnp.stack([srow(l, mix0 + G_MIX_CBW + k) for k in range(4)]) for l in range(nl)])
    grads = {
        "b_mod": g_b_mod,
        "g_pre_mix": per_layer(in0 + G_IN_GPRE),
        "g_post_mix": per_layer(mix0 + G_MIX_GPOST),
        "conv_a_w": lax.dynamic_slice_in_dim(conv_a_full, me * cwid, cwid, axis=2),
        "conv_a_b": per_layer(mix0 + G_MIX_CAB),
        "conv_b_w": lax.dynamic_slice_in_dim(conv_b_full, me * cwid, cwid, axis=2),
        "conv_b_b": per_layer(mix0 + G_MIX_CBB),
        "w_gate_r": sgates[:, 0],
        "b_gate_r": per_layer(mix0 + G_MIX_BR),
        "w_gate_i": sgates[:, 1],
        "b_gate_i": per_layer(mix0 + G_MIX_BI),
        "lru_lambda": per_layer(mix0 + G_MIX_LAM),
        "g_pre_mlp": per_layer(G_MLP_GPRE),
        "g_post_mlp": per_layer(G_MLP_GPOST),
    }
    params = {
        "b_mod": (b_mod, m_b_mod, v_b_mod), "g_pre_mix": (g_pre_mix, m_g_pre_mix, v_g_pre_mix), "g_post_mix": (g_post_mix, m_g_post_mix, v_g_post_mix),
        "conv_a_w": (conv_a_w, m_conv_a_w, v_conv_a_w), "conv_a_b": (conv_a_b, m_conv_a_b, v_conv_a_b),
        "conv_b_w": (conv_b_w, m_conv_b_w, v_conv_b_w), "conv_b_b": (conv_b_b, m_conv_b_b, v_conv_b_b),
        "w_gate_r": (w_gate_r, m_w_gate_r, v_w_gate_r), "b_gate_r": (b_gate_r, m_b_gate_r, v_b_gate_r),
        "w_gate_i": (w_gate_i, m_w_gate_i, v_w_gate_i), "b_gate_i": (b_gate_i, m_b_gate_i, v_b_gate_i),
        "lru_lambda": (lru_lambda, m_lru_lambda, v_lru_lambda), "g_pre_mlp": (g_pre_mlp, m_g_pre_mlp, v_g_pre_mlp),
        "g_post_mlp": (g_post_mlp, m_g_post_mlp, v_g_post_mlp),
    }
    out = {}
    for name, g in grads.items():
        w, m_, v_ = params[name]
        flat = (-1, w.shape[-1])
        dl, nm, nv = _adamw(w.reshape(flat), g.reshape(flat), m_.reshape(flat), v_.reshape(flat))
        out[name] = (g.reshape(w.shape), dl.reshape(w.shape), nm.reshape(w.shape), nv.reshape(w.shape))
    out["w_mod"] = (g_w_mod,) + tuple(_adamw_tiled(w_mod, g_w_mod, m_w_mod, v_w_mod, min(128, d)))
    out.update(big)

    order = ("w_mod", "b_mod", "g_pre_mix", "g_post_mix", "w_in", "conv_a_w", "conv_a_b", "w_a_out", "conv_b_w", "conv_b_b", "w_gate_r", "b_gate_r",
             "w_gate_i", "b_gate_i", "lru_lambda", "w_b_out", "w_o", "g_pre_mlp", "g_post_mlp", "w_mlp_up", "w_mlp_down")
    return (loss, grad_x) + tuple(out[n][0] for n in order) + tuple(out[n][1] for n in order) + tuple(out[n][2] for n in order) + tuple(out[n][3] for n in order)
```

```python
import functools

import jax
import jax.numpy as jnp
from jax import lax
from jax.experimental import pallas as pl
from jax.experimental.pallas import tpu as pltpu

F32, BF16 = jnp.float32, jnp.bfloat16
EPS = 1e-6
LRU_C = 8.0
N_DEV = 8
N_MOD = 6
SUBLANES = 8
VMEM_BUDGET = 56 * 1024 * 1024
WGRAD_ROWS = 512
ADAM_LR, ADAM_B1, ADAM_B2, ADAM_EPS, ADAM_WD, ADAM_STEP = 0.001, 0.9, 0.999, 1e-08, 0.01, 10
MESH = pl.DeviceIdType.MESH
VMEM_SPEC = pl.BlockSpec(memory_space=pltpu.VMEM)
ANY_SPEC = pl.BlockSpec(memory_space=pl.ANY)
HBM_SPEC = pl.BlockSpec(memory_space=pltpu.HBM)
SEM_SPEC = pl.BlockSpec(memory_space=pltpu.SEMAPHORE)
SIDE_EFFECT = pltpu.SideEffectType.DATAFLOW_SIDE_EFFECTING

R_G_PRE_MIX, R_G_POST_MIX, R_CONV_A_B, R_CONV_B_B, R_B_GATE_R, R_B_GATE_I, R_LAMBDA, R_G_PRE_MLP, R_G_POST_MLP = range(9)
N_ROWS = 16
M_SH_M, M_SC_M, M_GT_M, M_SH_F, M_SC_F, M_GT_F = range(6)
CW_A, CW_B, CW_ROWS = 0, 3, 8
G_MLP_GT, G_MLP_GPOST, G_MLP_SC, G_MLP_SH, G_MLP_GPRE, G_MLP_ROWS = 0, 1, 2, 3, 4, 8
(G_MIX_GT, G_MIX_GPOST, G_MIX_CAB, G_MIX_CAW, G_MIX_CBB, G_MIX_CBW, G_MIX_BR, G_MIX_BI, G_MIX_LAM) = 0, 1, 2, 3, 6, 7, 11, 12, 13
G_MIX_ROWS = 16
G_IN_SC, G_IN_SH, G_IN_GPRE, G_IN_ROWS = 0, 1, 2, 8
G_LAYER_ROWS = G_MLP_ROWS + G_MIX_ROWS + G_IN_ROWS


def _cparams(dims=None, vmem=None):
    kw = {}
    if dims is not None:
        kw["dimension_semantics"] = dims
    if vmem is not None:
        kw["vmem_limit_bytes"] = int(min(max(vmem, 16 * 1024 * 1024), VMEM_BUDGET))
    return pltpu.CompilerParams(**kw)


def _nbytes(shape, dtype):
    n = 1
    for s in shape:
        n *= s
    return n * jnp.dtype(dtype).itemsize


def _resident(block, index_map):
    return pl.BlockSpec(block, index_map, pipeline_mode=pl.Buffered(1))


def _my_position():
    x, y, c = lax.axis_index("x"), lax.axis_index("y"), lax.axis_index("c")
    return (x, y, c), 4 * x + 2 * y + c


def _peer(pos, k):
    x, y, c = pos
    px = 1 - x if k & 4 else x
    py = 1 - y if k & 2 else y
    pc = 1 - c if k & 1 else c
    return (px, py, pc), 4 * px + 2 * py + pc


def _remote(src, dst, ssem, rsem, peer):
    return pltpu.make_async_remote_copy(src_ref=src, dst_ref=dst, send_sem=ssem, recv_sem=rsem, device_id=peer, device_id_type=MESH)


def _dot(a, b):
    return jnp.dot(a, b, preferred_element_type=F32)


def _dot_nt(a, b):
    return lax.dot_general(a, b, (((1,), (1,)), ((), ())), preferred_element_type=F32)


def _dot_tn(a, b):
    return lax.dot_general(a, b, (((0,), (0,)), ((), ())), preferred_element_type=F32)


def _colsum(v):
    return jnp.sum(v, axis=0, keepdims=True)


def _sigmoid(v):
    return jax.nn.sigmoid(v)


def _gelu(v):
    k = 0.7978845608028654
    t = jnp.tanh(k * (v + 0.044715 * (v * v * v)))
    return 0.5 * v * (1.0 + t), t


def _gelu_grad(v, t):
    k = 0.7978845608028654
    return 0.5 * (1.0 + t) + 0.5 * v * (1.0 - t * t) * (k * (1.0 + 3.0 * 0.044715 * v * v))


def _expm1(v):
    u = jnp.exp(v)
    um1 = u - 1.0
    q = um1 * v / jnp.log(u)
    return jnp.where(um1 == 0.0, v, jnp.where(um1 == -1.0, -1.0, q))


def _softplus_neg(lam):
    z = -lam
    u = jnp.exp(-jnp.abs(z))
    w = 1.0 + u
    l1p = jnp.where(w == 1.0, u, jnp.log(w) * u / (w - 1.0))
    return jnp.maximum(z, 0.0) + l1p


def _rms(v):
    return lax.rsqrt(jnp.mean(v * v, axis=-1, keepdims=True) + EPS)


def _prenorm_bwd(xv, dh, g, sc):
    r = _rms(xv)
    xn = xv * r
    n = xn * g
    dsc = _colsum(dh * n)
    dsh = _colsum(dh)
    dn = dh * (1.0 + sc)
    dg = _colsum(dn * xn)
    dxn = dn * g
    dx = r * (dxn - xn * jnp.mean(dxn * xn, axis=-1, keepdims=True))
    return dx, dsc, dsh, dg


def _postnorm_bwd(yv, dout, g, gt):
    r = _rms(yv)
    yn = yv * r
    dgt = _colsum(dout * (yn * g))
    dn = dout * gt
    dg = _colsum(dn * yn)
    dyn = dn * g
    dy = r * (dyn - yn * jnp.mean(dyn * yn, axis=-1, keepdims=True))
    return dy, dgt, dg


def _gates(xc, wr_ref, wi_ref, b_r, b_i, sp, nh, bw):
    xcb = xc.astype(BF16)
    zr = jnp.concatenate([_dot(xcb[:, h * bw:(h + 1) * bw], wr_ref[h]) for h in range(nh)], axis=1) + b_r
    zi = jnp.concatenate([_dot(xcb[:, h * bw:(h + 1) * bw], wi_ref[h]) for h in range(nh)], axis=1) + b_i
    r = _sigmoid(zr)
    ig = _sigmoid(zi)
    la = (-LRU_C * r) * sp
    a = jnp.exp(la)
    mult = jnp.sqrt(-_expm1(2.0 * la))
    return xcb, r, ig, a, mult


def _scan_block(a8, b8, reverse):
    row = lax.broadcasted_iota(jnp.int32, a8.shape, 0)
    for s in (1, 2, 4):
        if reverse:
            keep = row < SUBLANES - s
            a_sh = pltpu.roll(a8, SUBLANES - s, 0)
            b_sh = pltpu.roll(b8, SUBLANES - s, 0)
        else:
            keep = row >= s
            a_sh = pltpu.roll(a8, s, 0)
            b_sh = pltpu.roll(b8, s, 0)
        b8 = b8 + a8 * jnp.where(keep, b_sh, 0.0)
        a8 = a8 * jnp.where(keep, a_sh, 1.0)
    return a8, b8


def _prep_small(c, w_mod, b_mod, cw):
    d = c.shape[1]
    cm = w_mod.shape[2]
    cwid = cw.shape[2]
    nl = w_mod.shape[0]

    def body(c_ref, wm_ref, bm_ref, cw_ref, mod_ref, cact_ref, cwf_ref, cbuf, pbuf, rbuf, ssem, rsem, lsem):
        pos, me = _my_position()
        me8 = pl.multiple_of(me * SUBLANES, SUBLANES)
        cbuf[pl.ds(me8, SUBLANES), :] = jnp.broadcast_to(c_ref[...], (SUBLANES, d))
        own_cw = pltpu.make_async_copy(cw_ref, cwf_ref.at[:, :, pl.ds(me * cwid, cwid)], lsem.at[0])
        own_cw.start()
        first = []
        for k in range(1, N_DEV):
            peer, _ = _peer(pos, k)
            rows = cbuf.at[pl.ds(me8, SUBLANES), :]
            first.append(_remote(rows, rows, ssem.at[0, k - 1], rsem.at[0, k - 1], peer))
            first.append(_remote(cw_ref, cwf_ref.at[:, :, pl.ds(me * cwid, cwid)], ssem.at[1, k - 1], rsem.at[1, k - 1], peer))
        for cp in first:
            cp.start()
        for k in range(1, N_DEV):
            peer, pj = _peer(pos, k)
            pj8 = pl.multiple_of(pj * SUBLANES, SUBLANES)
            rows = cbuf.at[pl.ds(pj8, SUBLANES), :]
            _remote(rows, rows, ssem.at[0, k - 1], rsem.at[0, k - 1], peer).wait_recv()
        cv = cbuf[...]
        cact = cv * _sigmoid(cv)
        cact_ref[...] = cact
        cb = cact.astype(BF16)
        for l in range(nl):
            pbuf[l] = _dot(cb, wm_ref[l].astype(BF16))
        own_p = pltpu.make_async_copy(pbuf.at[:, pl.ds(me8, SUBLANES), :], rbuf.at[me], lsem.at[1])
        own_p.start()
        second = []
        for k in range(1, N_DEV):
            peer, pj = _peer(pos, k)
            pj8 = pl.multiple_of(pj * SUBLANES, SUBLANES)
            second.append(_remote(pbuf.at[:, pl.ds(pj8, SUBLANES), :], rbuf.at[me], ssem.at[2, k - 1], rsem.at[2, k - 1], peer))
        for cp in second:
            cp.start()
        for k in range(1, N_DEV):
            peer, pj = _peer(pos, k)
            _remote(pbuf.at[:, pl.ds(0, SUBLANES), :], rbuf.at[pj], ssem.at[2, k - 1], rsem.at[2, k - 1], peer).wait_recv()
            _remote(cw_ref, cwf_ref.at[:, :, pl.ds(pj * cwid, cwid)], ssem.at[1, k - 1], rsem.at[1, k - 1], peer).wait_recv()
        own_p.wait()
        own_cw.wait()
        for l in range(nl):
            for j in range(N_DEV):
                mod_ref[l:l + 1, j * cm:(j + 1) * cm] = rbuf[j, l, 0:1, :] + bm_ref[l:l + 1, j * cm:(j + 1) * cm]
        for cp in first + second:
            cp.wait_send()

    return pl.pallas_call(
        body,
        name="prep_small",
        out_shape=(
            jax.ShapeDtypeStruct((nl, N_MOD * d), F32),
            jax.ShapeDtypeStruct((N_DEV * SUBLANES, d), F32),
            jax.ShapeDtypeStruct((nl, CW_ROWS, d), F32),
        ),
        in_specs=[VMEM_SPEC] * 4,
        out_specs=(VMEM_SPEC,) * 3,
        scratch_shapes=[
            pltpu.VMEM((N_DEV * SUBLANES, d), F32),
            pltpu.VMEM((nl, N_DEV * SUBLANES, cm), F32),
            pltpu.VMEM((N_DEV, nl, SUBLANES, cm), F32),
            pltpu.SemaphoreType.DMA((3, N_DEV - 1)),
            pltpu.SemaphoreType.DMA((3, N_DEV - 1)),
            pltpu.SemaphoreType.DMA((2,)),
        ],
        compiler_params=_cparams(vmem=3 * _nbytes(w_mod.shape, F32)),
    )(c, w_mod, b_mod, cw)


def _exchange_start(parts, gather, name):
    n = len(parts)
    lands = [lax.empty(((N_DEV,) + tuple(p.shape)) if gather else tuple(p.shape), p.dtype) for p in parts]

    def body(*refs):
        ins, lnd = refs[:n], refs[n:2 * n]
        ssem, rsem, token = refs[2 * n], refs[2 * n + 1], refs[-1]
        pos, me = _my_position()
        for k in range(1, N_DEV):
            peer, pj = _peer(pos, k)
            for t in range(n):
                src = ins[t] if gather else ins[t].at[pj]
                q = t * (N_DEV - 1) + k - 1
                _remote(src, lnd[t].at[me], ssem.at[q], rsem.at[q], peer).start()
        token[...] = jnp.zeros(token.shape, F32)

    out = pl.pallas_call(
        body,
        name=name,
        out_shape=(pltpu.SemaphoreType.DMA((n * (N_DEV - 1),)), pltpu.SemaphoreType.DMA((n * (N_DEV - 1),)))
        + tuple(pltpu.HBM(p.shape, p.dtype) for p in parts) + tuple(pltpu.HBM(p.shape, p.dtype) for p in lands)
        + (jax.ShapeDtypeStruct((SUBLANES, 128), F32),),
        in_specs=[HBM_SPEC] * (2 * n),
        out_specs=(SEM_SPEC, SEM_SPEC) + (HBM_SPEC,) * (2 * n) + (VMEM_SPEC,),
        input_output_aliases={i: 2 + i for i in range(2 * n)},
        compiler_params=pltpu.CompilerParams(has_side_effects=SIDE_EFFECT),
    )(*[pltpu.with_memory_space_constraint(p, pltpu.HBM) for p in list(parts) + lands])
    return out[0], out[1], list(out[2:2 + n]), list(out[2 + n:2 + 2 * n]), out[-1]


def _exchange_wait(started, after, gather, name):
    ssem, rsem, parts, lands, _ = started
    n = len(parts)

    def body(*refs):
        ins, lnd = refs[:n], refs[n:2 * n]
        ssem_ref, rsem_ref = refs[2 * n], refs[2 * n + 1]
        stage, lsem = refs[-1 - n:-1], refs[-1]
        pos, me = _my_position()
        load = []
        for t in range(n):
            src = ins[t] if gather else ins[t].at[me]
            load.append(pltpu.make_async_copy(src, stage[t], lsem.at[t]))
            load[-1].start()
        store = []
        for t in range(n):
            load[t].wait()
            store.append(pltpu.make_async_copy(stage[t], lnd[t].at[me], lsem.at[t]))
            store[-1].start()
        for k in range(1, N_DEV):
            peer, pj = _peer(pos, k)
            for t in range(n):
                src = ins[t] if gather else ins[t].at[pj]
                q = t * (N_DEV - 1) + k - 1
                _remote(src, lnd[t].at[me], ssem_ref.at[q], rsem_ref.at[q], peer).wait_send()
                _remote(src, lnd[t].at[pj], ssem_ref.at[q], rsem_ref.at[q], peer).wait_recv()
        for cp in store:
            cp.wait()

    out = pl.pallas_call(
        body,
        name=name,
        out_shape=tuple(pltpu.HBM(p.shape, p.dtype) for p in parts) + tuple(pltpu.HBM(p.shape, p.dtype) for p in lands),
        in_specs=[HBM_SPEC] * (2 * n) + [SEM_SPEC, SEM_SPEC, ANY_SPEC],
        out_specs=(HBM_SPEC,) * (2 * n),
        input_output_aliases={i: i for i in range(2 * n)},
        scratch_shapes=[pltpu.VMEM(tuple(z.shape[1:]), z.dtype) for z in lands] + [pltpu.SemaphoreType.DMA((n,))],
        compiler_params=pltpu.CompilerParams(has_side_effects=SIDE_EFFECT),
    )(*parts, *lands, ssem, rsem, after)
    return list(out[n:])


def _reduce_small(rows, dm8, cact):
    r, d = rows.shape
    nl = dm8.shape[0]
    cm = dm8.shape[2] // N_DEV

    def body(rows_ref, dm_ref, cact_ref, orow_ref, owm_ref, gr, dmr, ssem, rsem, lsem):
        pos, me = _my_position()
        me8 = pl.multiple_of(me * SUBLANES, SUBLANES)
        gr[me] = rows_ref[...]
        own_dm = pltpu.make_async_copy(dm_ref.at[:, :, pl.ds(me * cm, cm)], dmr.at[:, pl.ds(me8, SUBLANES), :], lsem.at[0])
        own_dm.start()
        sends = []
        for k in range(1, N_DEV):
            peer, pj = _peer(pos, k)
            sends.append(_remote(gr.at[me], gr.at[me], ssem.at[0, k - 1], rsem.at[0, k - 1], peer))
            sends.append(_remote(dm_ref.at[:, :, pl.ds(pj * cm, cm)], dmr.at[:, pl.ds(me8, SUBLANES), :],
                                 ssem.at[1, k - 1], rsem.at[1, k - 1], peer))
        for cp in sends:
            cp.start()
        for k in range(1, N_DEV):
            peer, pj = _peer(pos, k)
            pj8 = pl.multiple_of(pj * SUBLANES, SUBLANES)
            _remote(gr.at[pj], gr.at[pj], ssem.at[0, k - 1], rsem.at[0, k - 1], peer).wait_recv()
            _remote(dm_ref.at[:, :, pl.ds(0, cm)], dmr.at[:, pl.ds(pj8, SUBLANES), :], ssem.at[1, k - 1], rsem.at[1, k - 1], peer).wait_recv()
        own_dm.wait()
        acc = gr[0]
        for j in range(1, N_DEV):
            acc = acc + gr[j]
        orow_ref[...] = acc
        cb = cact_ref[...].astype(BF16)
        for l in range(nl):
            owm_ref[l] = _dot_tn(cb, dmr[l].astype(BF16))
        for cp in sends:
            cp.wait_send()

    return pl.pallas_call(
        body,
        name="reduce_small",
        out_shape=(jax.ShapeDtypeStruct((r, d), F32), jax.ShapeDtypeStruct((nl, d, cm), F32)),
        in_specs=[VMEM_SPEC] * 3,
        out_specs=(VMEM_SPEC,) * 2,
        scratch_shapes=[
            pltpu.VMEM((N_DEV, r, d), F32),
            pltpu.VMEM((nl, N_DEV * SUBLANES, cm), F32),
            pltpu.SemaphoreType.DMA((2, N_DEV - 1)),
            pltpu.SemaphoreType.DMA((2, N_DEV - 1)),
            pltpu.SemaphoreType.DMA((1,)),
        ],
        compiler_params=_cparams(vmem=4 * _nbytes((N_DEV, r, d), F32) + 6 * _nbytes((nl, d, cm), F32)),
    )(rows, dm8, cact)


def _sum_gathered(zones):
    nl = len(zones)

    def body(*refs):
        for l in range(nl):
            acc = refs[l][0].astype(F32)
            for j in range(1, N_DEV):
                acc = acc + refs[l][j].astype(F32)
            refs[nl][l] = acc

    return pl.pallas_call(
        body,
        name="sum_gathered",
        out_shape=jax.ShapeDtypeStruct((nl,) + tuple(zones[0].shape[1:]), F32),
        in_specs=[VMEM_SPEC] * nl,
        out_specs=VMEM_SPEC,
        compiler_params=_cparams(vmem=8 * nl * _nbytes(zones[0].shape, BF16)),
    )(*zones)


def _in_proj_fwd(x, mod, rows, win_f, l, tm):
    s, d = x.shape
    nb, _, ci = win_f.shape

    def body(x_ref, mod_ref, rows_ref, w_ref, proj_ref, h_ref):
        xv = x_ref[...]
        g = rows_ref[R_G_PRE_MIX:R_G_PRE_MIX + 1, :]
        h = (xv * _rms(xv) * g) * (1.0 + mod_ref[M_SC_M:M_SC_M + 1, :]) + mod_ref[M_SH_M:M_SH_M + 1, :]
        hb = h.astype(BF16)
        h_ref[...] = hb
        for j in range(nb):
            proj_ref[:, j * ci:(j + 1) * ci] = _dot(hb, w_ref[j])

    return pl.pallas_call(
        body,
        name="in_proj_fwd",
        grid=(s // tm,),
        in_specs=[
            pl.BlockSpec((tm, d), lambda i: (i, 0)),
            _resident((None, N_MOD, d), lambda i: (l, 0, 0)),
            _resident((None, N_ROWS, d), lambda i: (l, 0, 0)),
            _resident((nb, d, ci), lambda i: (0, 0, 0)),
        ],
        out_specs=(pl.BlockSpec((tm, nb * ci), lambda i: (i, 0)), pl.BlockSpec((tm, d), lambda i: (i, 0))),
        out_shape=(jax.ShapeDtypeStruct((s, nb * ci), F32), jax.ShapeDtypeStruct((s, d), BF16)),
        compiler_params=_cparams(("parallel",), _nbytes((nb, d, ci), BF16) + 3 * _nbytes((tm, nb * ci), F32) + 8 * _nbytes((tm, d), F32)),
    )(x, mod, rows, win_f)


def _mixer_core_fwd(proj, x, mod, rows, cwf, wr, wi, wa_f, wb_f, wo_f, l, tm):
    s, d = x.shape
    nh, bw, _ = wr.shape[1:]

    def body(proj_ref, x_ref, mod_ref, rows_ref, cw_ref, wr_ref, wi_ref, wa_ref, wb_ref, wo_ref,
             x1_ref, hs_ref, yap_ref, ybp_ref, y_ref, cvbuf, xbbuf, a_s, b_s, hprev):
        i = pl.program_id(0)

        @pl.when(i == 0)
        def _():
            cvbuf[pl.ds(0, SUBLANES), :] = jnp.zeros((SUBLANES, d), F32)
            xbbuf[pl.ds(0, SUBLANES), :] = jnp.zeros((SUBLANES, d), F32)
            hprev[...] = jnp.zeros((SUBLANES, d), F32)

        def row(r):
            return rows_ref[r:r + 1, :]

        def tap(r):
            return cw_ref[r:r + 1, :]

        ba = proj_ref[:, 0:d]
        cv = proj_ref[:, d:2 * d] * proj_ref[:, 2 * d:3 * d]
        cvbuf[pl.ds(SUBLANES, tm), :] = cv
        conv3 = ((row(R_CONV_A_B) + cvbuf[pl.ds(SUBLANES - 2, tm), :] * tap(CW_A)) + cvbuf[pl.ds(SUBLANES - 1, tm), :] * tap(CW_A + 1)) + cv * tap(CW_A + 2)
        ya = ba * conv3
        cvbuf[pl.ds(0, SUBLANES), :] = cvbuf[pl.ds(tm, SUBLANES), :]
        xb = proj_ref[:, 3 * d:4 * d]
        xbbuf[pl.ds(SUBLANES, tm), :] = xb
        xc = (((row(R_CONV_B_B) + xbbuf[pl.ds(SUBLANES - 3, tm), :] * tap(CW_B)) + xbbuf[pl.ds(SUBLANES - 2, tm), :] * tap(CW_B + 1))
              + xbbuf[pl.ds(SUBLANES - 1, tm), :] * tap(CW_B + 2)) + xb * tap(CW_B + 3)
        xbbuf[pl.ds(0, SUBLANES), :] = xbbuf[pl.ds(tm, SUBLANES), :]
        sp = _softplus_neg(row(R_LAMBDA))
        _, _, ig, a, mult = _gates(xc, wr_ref, wi_ref, row(R_B_GATE_R), row(R_B_GATE_I), sp, nh, bw)
        a_s[...] = a
        b_s[...] = mult * (ig * xc)

        def blk(j, hp):
            o = pl.multiple_of(j * SUBLANES, SUBLANES)
            a8, b8 = _scan_block(a_s[pl.ds(o, SUBLANES), :], b_s[pl.ds(o, SUBLANES), :], reverse=False)
            h8 = b8 + a8 * hp
            hs_ref[pl.ds(o, SUBLANES), :] = h8
            return jnp.broadcast_to(h8[SUBLANES - 1:SUBLANES, :], (SUBLANES, d))

        hprev[...] = lax.fori_loop(0, tm // SUBLANES, blk, hprev[...])
        gel, _ = _gelu(proj_ref[:, 4 * d:5 * d])
        yb = hs_ref[...] * gel
        yap = _dot(ya.astype(BF16), wa_ref[...])
        ybp = _dot(yb.astype(BF16), wb_ref[...])
        yap_ref[...] = yap
        ybp_ref[...] = ybp
        m = _sigmoid(proj_ref[:, 5 * d:6 * d]) * yap + _sigmoid(proj_ref[:, 6 * d:7 * d]) * ybp
        y = _dot(m.astype(BF16), wo_ref[...])
        y_ref[...] = y
        x1_ref[...] = x_ref[...] + mod_ref[M_GT_M:M_GT_M + 1, :] * ((y * _rms(y)) * row(R_G_POST_MIX))

    tile = pl.BlockSpec((tm, d), lambda i: (i, 0))
    return pl.pallas_call(
        body,
        name="mixer_core_fwd",
        grid=(s // tm,),
        in_specs=[
            pl.BlockSpec((tm, 7 * d), lambda i: (i, 0)),
            tile,
            _resident((None, N_MOD, d), lambda i: (l, 0, 0)),
            _resident((None, N_ROWS, d), lambda i: (l, 0, 0)),
            _resident((None, CW_ROWS, d), lambda i: (l, 0, 0)),
            _resident((None, nh, bw, bw), lambda i: (l, 0, 0, 0)),
            _resident((None, nh, bw, bw), lambda i: (l, 0, 0, 0)),
            _resident((d, d), lambda i: (0, 0)),
            _resident((d, d), lambda i: (0, 0)),
            _resident((d, d), lambda i: (0, 0)),
        ],
        out_specs=(tile,) * 5,
        out_shape=(jax.ShapeDtypeStruct((s, d), F32),) * 5,
        scratch_shapes=[
            pltpu.VMEM((tm + SUBLANES, d), F32),
            pltpu.VMEM((tm + SUBLANES, d), F32),
            pltpu.VMEM((tm, d), F32),
            pltpu.VMEM((tm, d), F32),
            pltpu.VMEM((SUBLANES, d), F32),
        ],
        compiler_params=_cparams(("arbitrary",), 3 * _nbytes((d, d), BF16) + 2 * _nbytes((tm, 7 * d), F32) + 40 * _nbytes((tm, d), F32)),
    )(proj, x, mod, rows, cwf, wr, wi, wa_f, wb_f, wo_f)


def _mlp_fwd(x1, mod, rows, wup_f, wdn_f, l, tm):
    s, d = x1.shape
    nb, _, cu = wup_f.shape
    dff = nb * cu

    def body(x1_ref, mod_ref, rows_ref, wu_ref, wd_ref, x2_ref, ru_ref, y2_ref, h2_ref):
        xv = x1_ref[...]
        g = rows_ref[R_G_PRE_MLP:R_G_PRE_MLP + 1, :]
        h2 = ((xv * _rms(xv) * g) * (1.0 + mod_ref[M_SC_F:M_SC_F + 1, :]) + mod_ref[M_SH_F:M_SH_F + 1, :]).astype(BF16)
        h2_ref[...] = h2
        ru = jnp.concatenate([jnp.maximum(_dot(h2, wu_ref[j]), 0.0) for j in range(nb)], axis=1)
        ru_ref[...] = ru.astype(BF16)
        y2 = _dot((ru * ru).astype(BF16), wd_ref[...])
        y2_ref[...] = y2
        x2_ref[...] = xv + mod_ref[M_GT_F:M_GT_F + 1, :] * ((y2 * _rms(y2)) * rows_ref[R_G_POST_MLP:R_G_POST_MLP + 1, :])

    tile = pl.BlockSpec((tm, d), lambda i: (i, 0))
    wide = pl.BlockSpec((tm, dff), lambda i: (i, 0))
    return pl.pallas_call(
        body,
        name="mlp_fwd",
        grid=(s // tm,),
        in_specs=[
            tile,
            _resident((None, N_MOD, d), lambda i: (l, 0, 0)),
            _resident((None, N_ROWS, d), lambda i: (l, 0, 0)),
            _resident((nb, d, cu), lambda i: (0, 0, 0)),
            _resident((dff, d), lambda i: (0, 0)),
        ],
        out_specs=(tile, wide, tile, tile),
        out_shape=(jax.ShapeDtypeStruct((s, d), F32), jax.ShapeDtypeStruct((s, dff), BF16),
                   jax.ShapeDtypeStruct((s, d), F32), jax.ShapeDtypeStruct((s, d), BF16)),
        compiler_params=_cparams(("parallel",), 2 * _nbytes((dff, d), BF16) + 5 * _nbytes((tm, dff), F32) + 12 * _nbytes((tm, d), F32)),
    )(x1, mod, rows, wup_f, wdn_f)


def _loss_fwd_bwd(y, target, tm):
    s, d = y.shape

    def body(y_ref, t_ref, loss_ref, dy_ref):
        @pl.when(pl.program_id(0) == 0)
        def _():
            loss_ref[...] = jnp.zeros(loss_ref.shape, F32)

        e = y_ref[...] - t_ref[...]
        dy_ref[...] = e * (1.0 / d)
        loss_ref[...] += 0.5 * jnp.sum(jnp.mean(e * e, axis=-1, keepdims=True), axis=0, keepdims=True)

    tile = pl.BlockSpec((tm, d), lambda i: (i, 0))
    loss, dy = pl.pallas_call(
        body,
        name="loss",
        grid=(s // tm,),
        in_specs=[tile, tile],
        out_specs=(pl.BlockSpec((SUBLANES, 128), lambda i: (0, 0)), tile),
        out_shape=(jax.ShapeDtypeStruct((SUBLANES, 128), F32), jax.ShapeDtypeStruct((s, d), F32)),
        compiler_params=_cparams(("arbitrary",)),
    )(y, target)
    return loss[0, 0], dy


def _mlp_bwd(dx2, x1, y2, ru, mod, rows, wup_f, wdn_f, l, tm):
    s, d = x1.shape
    nb, _, cu = wup_f.shape
    dff = nb * cu

    def body(dx2_ref, x1_ref, y2_ref, ru_ref, mod_ref, rows_ref, wu_ref, wd_ref, dx1_ref, dy2_ref, dup_ref, act_ref, sm_ref):
        @pl.when(pl.program_id(0) == 0)
        def _():
            sm_ref[...] = jnp.zeros(sm_ref.shape, F32)

        dout = dx2_ref[...]
        dy2, dgt, dgpost = _postnorm_bwd(y2_ref[...], dout, rows_ref[R_G_POST_MLP:R_G_POST_MLP + 1, :], mod_ref[M_GT_F:M_GT_F + 1, :])
        dy2b = dy2.astype(BF16)
        dy2_ref[...] = dy2b
        ruv = ru_ref[...].astype(F32)
        act_ref[...] = (ruv * ruv).astype(BF16)
        dup = (_dot_nt(dy2b, wd_ref[...]) * (2.0 * ruv)).astype(BF16)
        dup_ref[...] = dup
        dh2 = _dot_nt(dup[:, 0:cu], wu_ref[0])
        for j in range(1, nb):
            dh2 = dh2 + _dot_nt(dup[:, j * cu:(j + 1) * cu], wu_ref[j])
        dxn, dsc, dsh, dgpre = _prenorm_bwd(x1_ref[...], dh2, rows_ref[R_G_PRE_MLP:R_G_PRE_MLP + 1, :], mod_ref[M_SC_F:M_SC_F + 1, :])
        dx1_ref[...] = dout + dxn
        for r, v in ((G_MLP_GT, dgt), (G_MLP_GPOST, dgpost), (G_MLP_SC, dsc), (G_MLP_SH, dsh), (G_MLP_GPRE, dgpre)):
            sm_ref[r:r + 1, :] += v

    tile = pl.BlockSpec((tm, d), lambda i: (i, 0))
    wide = pl.BlockSpec((tm, dff), lambda i: (i, 0))
    return pl.pallas_call(
        body,
        name="mlp_bwd",
        grid=(s // tm,),
        in_specs=[
            tile, tile, tile, wide,
            _resident((None, N_MOD, d), lambda i: (l, 0, 0)),
            _resident((None, N_ROWS, d), lambda i: (l, 0, 0)),
            _resident((nb, d, cu), lambda i: (0, 0, 0)),
            _resident((dff, d), lambda i: (0, 0)),
        ],
        out_specs=(tile, tile, wide, wide, pl.BlockSpec((G_MLP_ROWS, d), lambda i: (0, 0))),
        out_shape=(jax.ShapeDtypeStruct((s, d), F32), jax.ShapeDtypeStruct((s, d), BF16), jax.ShapeDtypeStruct((s, dff), BF16),
                   jax.ShapeDtypeStruct((s, dff), BF16), jax.ShapeDtypeStruct((G_MLP_ROWS, d), F32)),
        compiler_params=_cparams(("arbitrary",), 2 * _nbytes((dff, d), BF16) + 6 * _nbytes((tm, dff), F32) + 16 * _nbytes((tm, d), F32)),
    )(dx2, x1, y2, ru, mod, rows, wup_f, wdn_f)


def _mixer_core_bwd(dx1, y, yap, ybp, hs, proj, mod, rows, cwf, wr, wi, wa_f, wb_f, wo_f, l, tm):
    s, d = dx1.shape
    nh, bw, _ = wr.shape[1:]
    nt = s // tm
    per = tm // SUBLANES

    def body(dx1_ref, y_ref, yap_ref, ybp_ref, hs_ref, hsh_ref, proj_ref, projh_ref, mod_ref, rows_ref, cw_ref,
             wr_ref, wi_ref, wa_ref, wb_ref, wo_ref,
             dproj_ref, dy_ref, m_ref, dyap_ref, dybp_ref, ya_ref, yb_ref, sm_ref, dwg_ref,
             cvbuf, xbbuf, hsbuf, abuf, dcbuf, dxbuf, al_s, dh_s, lam_s, lnext):
        i = pl.program_id(0)
        first_tile = i == nt - 1

        @pl.when(i == 0)
        def _():
            sm_ref[...] = jnp.zeros(sm_ref.shape, F32)
            dwg_ref[...] = jnp.zeros(dwg_ref.shape, F32)
            zero = jnp.zeros((SUBLANES, d), F32)
            abuf[pl.ds(tm, SUBLANES), :] = zero
            dcbuf[pl.ds(tm, SUBLANES), :] = zero
            dxbuf[pl.ds(tm, SUBLANES), :] = zero
            lnext[...] = zero

        def row(r):
            return rows_ref[r:r + 1, :]

        def tap(r):
            return cw_ref[r:r + 1, :]

        def acc(r, v):
            sm_ref[r:r + 1, :] += v

        keep_halo = jnp.where(first_tile, 0.0, 1.0)
        dy, dgt, dgpost = _postnorm_bwd(y_ref[...], dx1_ref[...], row(R_G_POST_MIX), mod_ref[M_GT_M:M_GT_M + 1, :])
        acc(G_MIX_GT, dgt)
        acc(G_MIX_GPOST, dgpost)
        dyb16 = dy.astype(BF16)
        dy_ref[...] = dyb16
        dm = _dot_nt(dyb16, wo_ref[...])
        sa = _sigmoid(proj_ref[:, 5 * d:6 * d])
        sb = _sigmoid(proj_ref[:, 6 * d:7 * d])
        yap = yap_ref[...]
        ybp = ybp_ref[...]
        m_ref[...] = (sa * yap + sb * ybp).astype(BF16)
        dyap = (dm * sa).astype(BF16)
        dybp = (dm * sb).astype(BF16)
        dyap_ref[...] = dyap
        dybp_ref[...] = dybp
        dproj_ref[:, 5 * d:6 * d] = (dm * yap * sa * (1.0 - sa)).astype(BF16)
        dproj_ref[:, 6 * d:7 * d] = (dm * ybp * sb * (1.0 - sb)).astype(BF16)
        dya = _dot_nt(dyap, wa_ref[...])
        dyb = _dot_nt(dybp, wb_ref[...])
        ba = proj_ref[:, 0:d]
        ca = proj_ref[:, d:2 * d]
        va = proj_ref[:, 2 * d:3 * d]
        cv = ca * va
        cvbuf[pl.ds(0, SUBLANES), :] = keep_halo * (projh_ref[:, d:2 * d] * projh_ref[:, 2 * d:3 * d])
        cvbuf[pl.ds(SUBLANES, tm), :] = cv
        cvm2 = cvbuf[pl.ds(SUBLANES - 2, tm), :]
        cvm1 = cvbuf[pl.ds(SUBLANES - 1, tm), :]
        conv3 = ((row(R_CONV_A_B) + cvm2 * tap(CW_A)) + cvm1 * tap(CW_A + 1)) + cv * tap(CW_A + 2)
        ya_ref[...] = (ba * conv3).astype(BF16)
        dproj_ref[:, 0:d] = (dya * conv3).astype(BF16)
        dc3 = dya * ba
        acc(G_MIX_CAB, _colsum(dc3))
        acc(G_MIX_CAW, _colsum(dc3 * cvm2))
        acc(G_MIX_CAW + 1, _colsum(dc3 * cvm1))
        acc(G_MIX_CAW + 2, _colsum(dc3 * cv))
        dcbuf[pl.ds(0, tm), :] = dc3
        dcv = (dc3 * tap(CW_A + 2) + dcbuf[pl.ds(1, tm), :] * tap(CW_A + 1)) + dcbuf[pl.ds(2, tm), :] * tap(CW_A)
        dcbuf[pl.ds(tm, SUBLANES), :] = dcbuf[pl.ds(0, SUBLANES), :]
        dproj_ref[:, d:2 * d] = (dcv * va).astype(BF16)
        dproj_ref[:, 2 * d:3 * d] = (dcv * ca).astype(BF16)
        xb = proj_ref[:, 3 * d:4 * d]
        gb = proj_ref[:, 4 * d:5 * d]
        xbbuf[pl.ds(0, SUBLANES), :] = keep_halo * projh_ref[:, 3 * d:4 * d]
        xbbuf[pl.ds(SUBLANES, tm), :] = xb
        xm3 = xbbuf[pl.ds(SUBLANES - 3, tm), :]
        xm2 = xbbuf[pl.ds(SUBLANES - 2, tm), :]
        xm1 = xbbuf[pl.ds(SUBLANES - 1, tm), :]
        xc = (((row(R_CONV_B_B) + xm3 * tap(CW_B)) + xm2 * tap(CW_B + 1)) + xm1 * tap(CW_B + 2)) + xb * tap(CW_B + 3)
        lam = row(R_LAMBDA)
        sp = _softplus_neg(lam)
        xcb, r, ig, a, mult = _gates(xc, wr_ref, wi_ref, row(R_B_GATE_R), row(R_B_GATE_I), sp, nh, bw)
        gel, th = _gelu(gb)
        hs = hs_ref[...]
        yb_ref[...] = (hs * gel).astype(BF16)
        dproj_ref[:, 4 * d:5 * d] = (dyb * hs * _gelu_grad(gb, th)).astype(BF16)
        abuf[pl.ds(0, tm), :] = a
        al_s[...] = abuf[pl.ds(1, tm), :]
        abuf[pl.ds(tm, SUBLANES), :] = abuf[pl.ds(0, SUBLANES), :]
        dh_s[...] = dyb * gel

        def blk(j, ln):
            o = pl.multiple_of((per - 1 - j) * SUBLANES, SUBLANES)
            a8, b8 = _scan_block(al_s[pl.ds(o, SUBLANES), :], dh_s[pl.ds(o, SUBLANES), :], reverse=True)
            l8 = b8 + a8 * ln
            lam_s[pl.ds(o, SUBLANES), :] = l8
            return jnp.broadcast_to(l8[0:1, :], (SUBLANES, d))

        lnext[...] = lax.fori_loop(0, per, blk, lnext[...])
        dbb = lam_s[...]
        hsbuf[pl.ds(0, SUBLANES), :] = keep_halo * hsh_ref[...]
        hsbuf[pl.ds(SUBLANES, tm), :] = hs
        da = dbb * hsbuf[pl.ds(SUBLANES - 1, tm), :]
        dmult = dbb * (ig * xc)
        dig = dbb * (mult * xc)
        dxc = dbb * (mult * ig)
        dla = da * a - dmult * (a * a) / mult
        acc(G_MIX_LAM, _colsum(dla * (-LRU_C * r)) * (-_sigmoid(-lam)))
        dzr = (dla * (-LRU_C * sp)) * r * (1.0 - r)
        dzi = dig * ig * (1.0 - ig)
        acc(G_MIX_BR, _colsum(dzr))
        acc(G_MIX_BI, _colsum(dzi))
        dzrb = dzr.astype(BF16)
        dzib = dzi.astype(BF16)
        back = []
        for h in range(nh):
            sl = slice(h * bw, (h + 1) * bw)
            back.append(_dot_nt(dzrb[:, sl], wr_ref[h]) + _dot_nt(dzib[:, sl], wi_ref[h]))
            dwg_ref[0, h] += _dot_tn(xcb[:, sl], dzrb[:, sl])
            dwg_ref[1, h] += _dot_tn(xcb[:, sl], dzib[:, sl])
        dxc = dxc + jnp.concatenate(back, axis=1)
        acc(G_MIX_CBB, _colsum(dxc))
        acc(G_MIX_CBW, _colsum(dxc * xm3))
        acc(G_MIX_CBW + 1, _colsum(dxc * xm2))
        acc(G_MIX_CBW + 2, _colsum(dxc * xm1))
        acc(G_MIX_CBW + 3, _colsum(dxc * xb))
        dxbuf[pl.ds(0, tm), :] = dxc
        dxb = ((dxc * tap(CW_B + 3) + dxbuf[pl.ds(1, tm), :] * tap(CW_B + 2)) + dxbuf[pl.ds(2, tm), :] * tap(CW_B + 1)) + dxbuf[pl.ds(3, tm), :] * tap(CW_B)
        dxbuf[pl.ds(tm, SUBLANES), :] = dxbuf[pl.ds(0, SUBLANES), :]
        dproj_ref[:, 3 * d:4 * d] = dxb.astype(BF16)

    def rev(i):
        return (nt - 1 - i, 0)

    def halo(i):
        return (jnp.maximum((nt - 1 - i) * per - 1, 0), 0)

    tile = pl.BlockSpec((tm, d), rev)
    return pl.pallas_call(
        body,
        name="mixer_core_bwd",
        grid=(nt,),
        in_specs=[
            tile, tile, tile, tile, tile,
            pl.BlockSpec((SUBLANES, d), halo),
            pl.BlockSpec((tm, 7 * d), rev),
            pl.BlockSpec((SUBLANES, 7 * d), halo),
            _resident((None, N_MOD, d), lambda i: (l, 0, 0)),
            _resident((None, N_ROWS, d), lambda i: (l, 0, 0)),
            _resident((None, CW_ROWS, d), lambda i: (l, 0, 0)),
            _resident((None, nh, bw, bw), lambda i: (l, 0, 0, 0)),
            _resident((None, nh, bw, bw), lambda i: (l, 0, 0, 0)),
            _resident((d, d), lambda i: (0, 0)),
            _resident((d, d), lambda i: (0, 0)),
            _resident((d, d), lambda i: (0, 0)),
        ],
        out_specs=(pl.BlockSpec((tm, 7 * d), rev),) + (tile,) * 6 + (
            pl.BlockSpec((G_MIX_ROWS, d), lambda i: (0, 0)), pl.BlockSpec((2, nh, bw, bw), lambda i: (0, 0, 0, 0))),
        out_shape=(jax.ShapeDtypeStruct((s, 7 * d), BF16),) + (jax.ShapeDtypeStruct((s, d), BF16),) * 6 + (
            jax.ShapeDtypeStruct((G_MIX_ROWS, d), F32), jax.ShapeDtypeStruct((2, nh, bw, bw), F32)),
        scratch_shapes=[pltpu.VMEM((tm + SUBLANES, d), F32)] * 6 + [pltpu.VMEM((tm, d), F32)] * 3 + [pltpu.VMEM((SUBLANES, d), F32)],
        compiler_params=_cparams(("arbitrary",), 3 * _nbytes((d, d), BF16) + 3 * _nbytes((tm, 7 * d), F32) + 64 * _nbytes((tm, d), F32)),
    )(dx1, y, yap, ybp, hs, hs, proj, proj, mod, rows, cwf, wr, wi, wa_f, wb_f, wo_f)


def _in_proj_bwd(dproj, x, dx1, mod, rows, win_f, l, tm):
    s, d = x.shape
    nb, _, ci = win_f.shape

    def body(dp_ref, x_ref, dx1_ref, mod_ref, rows_ref, w_ref, dx_ref, sm_ref):
        @pl.when(pl.program_id(0) == 0)
        def _():
            sm_ref[...] = jnp.zeros(sm_ref.shape, F32)

        dh = _dot_nt(dp_ref[:, 0:ci], w_ref[0])
        for j in range(1, nb):
            dh = dh + _dot_nt(dp_ref[:, j * ci:(j + 1) * ci], w_ref[j])
        dxn, dsc, dsh, dg = _prenorm_bwd(x_ref[...], dh, rows_ref[R_G_PRE_MIX:R_G_PRE_MIX + 1, :], mod_ref[M_SC_M:M_SC_M + 1, :])
        dx_ref[...] = dx1_ref[...] + dxn
        for r, v in ((G_IN_SC, dsc), (G_IN_SH, dsh), (G_IN_GPRE, dg)):
            sm_ref[r:r + 1, :] += v

    tile = pl.BlockSpec((tm, d), lambda i: (i, 0))
    return pl.pallas_call(
        body,
        name="in_proj_bwd",
        grid=(s // tm,),
        in_specs=[
            pl.BlockSpec((tm, nb * ci), lambda i: (i, 0)), tile, tile,
            _resident((None, N_MOD, d), lambda i: (l, 0, 0)),
            _resident((None, N_ROWS, d), lambda i: (l, 0, 0)),
            _resident((nb, d, ci), lambda i: (0, 0, 0)),
        ],
        out_specs=(tile, pl.BlockSpec((G_IN_ROWS, d), lambda i: (0, 0))),
        out_shape=(jax.ShapeDtypeStruct((s, d), F32), jax.ShapeDtypeStruct((G_IN_ROWS, d), F32)),
        compiler_params=_cparams(("arbitrary",), _nbytes((nb, d, ci), BF16) + 2 * _nbytes((tm, nb * ci), BF16) + 16 * _nbytes((tm, d), F32)),
    )(dproj, x, dx1, mod, rows, win_f)


def _wgrad(a, b, cols_owned, ts):
    s, k1 = a.shape
    k2 = b.shape[1]
    ns = s // ts
    if cols_owned:
        nblk, bk1, bk2 = N_DEV, k1, k2 // N_DEV
        a_spec = pl.BlockSpec((ts, bk1), lambda j, t: (t, 0))
        b_spec = pl.BlockSpec((ts, bk2), lambda j, t: (t, j))
    else:
        bk1, bk2 = min(WGRAD_ROWS, k1), k2
        nblk = k1 // bk1
        a_spec = pl.BlockSpec((ts, bk1), lambda j, t: (t, j))
        b_spec = pl.BlockSpec((ts, bk2), lambda j, t: (t, 0))

    def body(a_ref, b_ref, o_ref, acc_ref):
        t = pl.program_id(1)

        @pl.when(t == 0)
        def _():
            acc_ref[...] = jnp.zeros(acc_ref.shape, F32)

        acc_ref[...] += _dot_tn(a_ref[...], b_ref[...])

        @pl.when(t == ns - 1)
        def _():
            o_ref[...] = acc_ref[...].astype(BF16)

    out = pl.pallas_call(
        body,
        name="wgrad",
        grid=(nblk, ns),
        in_specs=[a_spec, b_spec],
        out_specs=pl.BlockSpec((None, bk1, bk2), lambda j, t: (j, 0, 0)),
        out_shape=pltpu.HBM((nblk, bk1, bk2), BF16),
        scratch_shapes=[pltpu.VMEM((bk1, bk2), F32)],
        compiler_params=_cparams(("parallel", "arbitrary"), 4 * _nbytes((bk1, bk2), F32) + 4 * _nbytes((ts, bk1 + bk2), BF16)),
    )(pltpu.with_memory_space_constraint(a, pltpu.HBM), pltpu.with_memory_space_constraint(b, pltpu.HBM))
    return out if cols_owned else out.reshape(N_DEV, k1 // N_DEV, k2)


def _adam_update(w, g, m, v):
    m = ADAM_B1 * m + (1.0 - ADAM_B1) * g
    v = ADAM_B2 * v + (1.0 - ADAM_B2) * (g * g)
    m_hat = m / (1.0 - ADAM_B1 ** ADAM_STEP)
    v_hat = v / (1.0 - ADAM_B2 ** ADAM_STEP)
    delta = -ADAM_LR * (m_hat / (jnp.sqrt(v_hat) + ADAM_EPS) + ADAM_WD * w)
    return delta, m, v


def _sum_adamw(recv, w, m, v, tr):
    nl, ra, cb = w.shape
    assert nl == len(recv) == 2

    def body(r0_ref, r1_ref, w_ref, m_ref, v_ref, g_ref, d_ref, nm_ref, nv_ref):
        def total(r_ref):
            g = r_ref[0].astype(F32)
            for j in range(1, N_DEV):
                g = g + r_ref[j].astype(F32)
            return g

        g = jnp.where(pl.program_id(0) == 0, total(r0_ref), total(r1_ref))
        g_ref[...] = g
        d_ref[...], nm_ref[...], nv_ref[...] = _adam_update(w_ref[...], g, m_ref[...], v_ref[...])

    blk = pl.BlockSpec((None, tr, cb), lambda l, i: (l, i, 0))
    return pl.pallas_call(
        body,
        name="sum_adamw",
        grid=(nl, ra // tr),
        in_specs=[pl.BlockSpec((N_DEV, tr, cb), lambda l, i: (0, i * (1 - l), 0)),
                  pl.BlockSpec((N_DEV, tr, cb), lambda l, i: (0, i * l, 0)), blk, blk, blk],
        out_specs=(blk,) * 4,
        out_shape=(jax.ShapeDtypeStruct((nl, ra, cb), F32),) * 4,
        compiler_params=_cparams(("arbitrary", "arbitrary")),
    )(recv[0], recv[1], w, m, v)


def _adamw(w, g, m, v):
    def body(w_ref, g_ref, m_ref, v_ref, d_ref, nm_ref, nv_ref):
        d_ref[...], nm_ref[...], nv_ref[...] = _adam_update(w_ref[...], g_ref[...], m_ref[...], v_ref[...])

    return pl.pallas_call(
        body,
        name="adamw",
        in_specs=[VMEM_SPEC] * 4,
        out_specs=(VMEM_SPEC,) * 3,
        out_shape=(jax.ShapeDtypeStruct(w.shape, F32),) * 3,
        compiler_params=_cparams(vmem=10 * _nbytes(w.shape, F32)),
    )(w, g, m, v)


def _adamw_tiled(w, g, m, v, tr):
    nl, ra, cb = w.shape

    def body(w_ref, g_ref, m_ref, v_ref, d_ref, nm_ref, nv_ref):
        d_ref[...], nm_ref[...], nv_ref[...] = _adam_update(w_ref[...], g_ref[...], m_ref[...], v_ref[...])

    blk = pl.BlockSpec((None, tr, cb), lambda l, i: (l, i, 0))
    return pl.pallas_call(
        body,
        name="adamw_tiled",
        grid=(nl, ra // tr),
        in_specs=[blk] * 4,
        out_specs=(blk,) * 3,
        out_shape=(jax.ShapeDtypeStruct((nl, ra, cb), F32),) * 3,
        compiler_params=_cparams(("parallel", "parallel")),
    )(w, g, m, v)


def _token_tile(s):
    return min(256, max(SUBLANES * 2, s // 4))


def kernel(x, c, w_mod, b_mod, g_pre_mix, g_post_mix, w_in, conv_a_w, conv_a_b, w_a_out, conv_b_w, conv_b_b, w_gate_r, b_gate_r, w_gate_i, b_gate_i, lru_lambda, w_b_out, w_o, g_pre_mlp, g_post_mlp, w_mlp_up, w_mlp_down, loss_target, m_w_mod, m_b_mod, m_g_pre_mix, m_g_post_mix, m_w_in, m_conv_a_w, m_conv_a_b, m_w_a_out, m_conv_b_w, m_conv_b_b, m_w_gate_r, m_b_gate_r, m_w_gate_i, m_b_gate_i, m_lru_lambda, m_w_b_out, m_w_o, m_g_pre_mlp, m_g_post_mlp, m_w_mlp_up, m_w_mlp_down, v_w_mod, v_b_mod, v_g_pre_mix, v_g_post_mix, v_w_in, v_conv_a_w, v_conv_a_b, v_w_a_out, v_conv_b_w, v_conv_b_b, v_w_gate_r, v_b_gate_r, v_w_gate_i, v_b_gate_i, v_lru_lambda, v_w_b_out, v_w_o, v_g_pre_mlp, v_g_post_mlp, v_w_mlp_up, v_w_mlp_down):
    nl = w_mod.shape[0]
    s, d = x.shape[1], x.shape[2]
    nh, bw = w_gate_r.shape[1], w_gate_r.shape[2]
    cwid = conv_a_w.shape[2]
    tm = _token_tile(s)
    ts = min(1024, s)
    _, me = _my_position()
    xs = x.reshape(s, d)
    target = loss_target.reshape(s, d)

    vec_names = (g_pre_mix, g_post_mix, conv_a_b, conv_b_b, b_gate_r, b_gate_i, lru_lambda, g_pre_mlp, g_post_mlp)
    rows = jnp.concatenate([jnp.stack(vec_names, axis=1), jnp.zeros((nl, N_ROWS - len(vec_names), d), F32)], axis=1)
    cw = jnp.concatenate([conv_a_w, conv_b_w, jnp.zeros((nl, CW_ROWS - 7, cwid), F32)], axis=1)

    w16 = {"w_in": w_in.astype(BF16), "w_a_out": w_a_out.astype(BF16), "w_b_out": w_b_out.astype(BF16), "w_o": w_o.astype(BF16),
           "w_mlp_up": w_mlp_up.astype(BF16), "w_mlp_down": w_mlp_down.astype(BF16)}
    groups = (("in", ("w_in",)), ("mix", ("w_a_out", "w_b_out", "w_o")), ("mlp", ("w_mlp_up", "w_mlp_down")))
    mod, cact, cwf = _prep_small(c, w_mod, b_mod, cw)
    mod = mod.reshape(nl, N_MOD, d)
    w16, _ = lax.optimization_barrier((w16, mod))
    gathers = {}
    token = jnp.zeros((), F32)
    for l in range(nl):
        for gname, members in groups:
            gathers[l, gname] = _exchange_start([w16[n][l] for n in members], True, f"gather_start_{gname}{l}")
            token = token + gathers[l, gname][4][0, 0]
    rows = rows + token
    wr = w_gate_r.astype(BF16)
    wi = w_gate_i.astype(BF16)

    saved = []
    weights = []
    xin = xs
    for l in range(nl):
        (win_f,) = _exchange_wait(gathers[l, "in"], mod if l == 0 else xin, True, f"gather_wait_in{l}")
        proj, h = _in_proj_fwd(xin, mod, rows, win_f, l, tm)
        wa_f, wb_f, wo_f = (w.reshape(d, d) for w in _exchange_wait(gathers[l, "mix"], proj, True, f"gather_wait_mix{l}"))
        x1, hs, yap, ybp, y = _mixer_core_fwd(proj, xin, mod, rows, cwf, wr, wi, wa_f, wb_f, wo_f, l, tm)
        wup_f, wdn_f = _exchange_wait(gathers[l, "mlp"], x1, True, f"gather_wait_mlp{l}")
        wdn_f = wdn_f.reshape(-1, d)
        x2, ru, y2, h2 = _mlp_fwd(x1, mod, rows, wup_f, wdn_f, l, tm)
        saved.append((xin, proj, h, x1, hs, yap, ybp, y, ru, y2, h2))
        weights.append((win_f, wa_f, wb_f, wo_f, wup_f, wdn_f))
        xin = x2
    loss_part, dx = _loss_fwd_bwd(xin, target, tm)
    loss = lax.psum(loss_part, ("x", "y", "c"))

    scatters = {}
    small = [None] * nl
    gate_parts = [None] * nl

    def scatter(l, gname, parts, rows):
        scatters[l, gname] = _exchange_start(parts, False, f"scatter_start_{gname}{l}")
        return rows + scatters[l, gname][4][0, 0]

    for l in reversed(range(nl)):
        xin, proj, h, x1, hs, yap, ybp, y, ru, y2, h2 = saved[l]
        win_f, wa_f, wb_f, wo_f, wup_f, wdn_f = weights[l]
        dx1, dy2, dup, act, sm_mlp = _mlp_bwd(dx, x1, y2, ru, mod, rows, wup_f, wdn_f, l, tm)
        rows = scatter(l, "mlp", [_wgrad(h2, dup, True, ts), _wgrad(act, dy2, False, ts)], rows)
        dproj, dy, m, dyap, dybp, ya, yb, sm_mix, dwg = _mixer_core_bwd(dx1, y, yap, ybp, hs, proj, mod, rows, cwf, wr, wi, wa_f, wb_f, wo_f, l, tm // 2)
        gate_parts[l] = _exchange_start([dwg.astype(BF16)], True, f"gates_start{l}")
        rows = rows + gate_parts[l][4][0, 0]
        rows = scatter(l, "in", [_wgrad(h, dproj, True, ts)], rows)
        rows = scatter(l, "mix", [_wgrad(ya, dyap, False, ts), _wgrad(yb, dybp, False, ts), _wgrad(m, dy, False, ts)], rows)
        dx, sm_in = _in_proj_bwd(dproj, xin, dx1, mod, rows, win_f, l, tm)
        small[l] = jnp.concatenate([sm_mlp, sm_mix, sm_in], axis=0)
    grad_x = dx.reshape(x.shape)

    lrows = jnp.concatenate(small, axis=0)

    def lrow(a, l, r):
        return a[l * G_LAYER_ROWS + r]

    dm = jnp.stack([jnp.concatenate([lrow(lrows, l, G_MLP_ROWS + G_MIX_ROWS + G_IN_SH), lrow(lrows, l, G_MLP_ROWS + G_MIX_ROWS + G_IN_SC),
                                     lrow(lrows, l, G_MLP_ROWS + G_MIX_GT), lrow(lrows, l, G_MLP_SH), lrow(lrows, l, G_MLP_SC),
                                     lrow(lrows, l, G_MLP_GT)]) for l in range(nl)])
    dm8 = jnp.concatenate([dm[:, None, :], jnp.zeros((nl, SUBLANES - 1, N_MOD * d), F32)], axis=1)
    srows, g_w_mod = _reduce_small(lrows, dm8, cact)
    sgates = _sum_gathered([_exchange_wait(gate_parts[l], srows, True, f"gates_wait{l}")[0] for l in range(nl)])

    recv = {}
    for l in reversed(range(nl)):
        for gname, members in groups:
            for n, zone in zip(members, _exchange_wait(scatters[l, gname], srows, False, f"scatter_wait_{gname}{l}")):
                recv[n, l] = zone
    big = {}
    for name, w, m_, v_ in (("w_in", w_in, m_w_in, v_w_in), ("w_mlp_up", w_mlp_up, m_w_mlp_up, v_w_mlp_up),
                            ("w_a_out", w_a_out, m_w_a_out, v_w_a_out), ("w_b_out", w_b_out, m_w_b_out, v_w_b_out),
                            ("w_o", w_o, m_w_o, v_w_o), ("w_mlp_down", w_mlp_down, m_w_mlp_down, v_w_mlp_down)):
        big[name] = _sum_adamw([recv[name, l] for l in range(nl)], w, m_, v_, min(128, w.shape[1]))

    def srow(l, r):
        return lrow(srows, l, r)

    def per_layer(r):
        return jnp.stack([srow(l, r) for l in range(nl)])

    mix0 = G_MLP_ROWS
    in0 = G_MLP_ROWS + G_MIX_ROWS
    g_b_mod = jnp.stack([jnp.concatenate([srow(l, in0 + G_IN_SH), srow(l, in0 + G_IN_SC), srow(l, mix0 + G_MIX_GT),
                                          srow(l, G_MLP_SH), srow(l, G_MLP_SC), srow(l, G_MLP_GT)]) for l in range(nl)])
    conv_a_full = jnp.stack([jnp.stack([srow(l, mix0 + G_MIX_CAW + k) for k in range(3)]) for l in range(nl)])
    conv_b_full = jnp.stack([jnp.stack([srow(l, mix0 + G_MIX_CBW + k) for k in range(4)]) for l in range(nl)])
    grads = {
        "b_mod": g_b_mod,
        "g_pre_mix": per_layer(in0 + G_IN_GPRE),
        "g_post_mix": per_layer(mix0 + G_MIX_GPOST),
        "conv_a_w": lax.dynamic_slice_in_dim(conv_a_full, me * cwid, cwid, axis=2),
        "conv_a_b": per_layer(mix0 + G_MIX_CAB),
        "conv_b_w": lax.dynamic_slice_in_dim(conv_b_full, me * cwid, cwid, axis=2),
        "conv_b_b": per_layer(mix0 + G_MIX_CBB),
        "w_gate_r": sgates[:, 0],
        "b_gate_r": per_layer(mix0 + G_MIX_BR),
        "w_gate_i": sgates[:, 1],
        "b_gate_i": per_layer(mix0 + G_MIX_BI),
        "lru_lambda": per_layer(mix0 + G_MIX_LAM),
        "g_pre_mlp": per_layer(G_MLP_GPRE),
        "g_post_mlp": per_layer(G_MLP_GPOST),
    }
    params = {
        "b_mod": (b_mod, m_b_mod, v_b_mod), "g_pre_mix": (g_pre_mix, m_g_pre_mix, v_g_pre_mix), "g_post_mix": (g_post_mix, m_g_post_mix, v_g_post_mix),
        "conv_a_w": (conv_a_w, m_conv_a_w, v_conv_a_w), "conv_a_b": (conv_a_b, m_conv_a_b, v_conv_a_b),
        "conv_b_w": (conv_b_w, m_conv_b_w, v_conv_b_w), "conv_b_b": (conv_b_b, m_conv_b_b, v_conv_b_b),
        "w_gate_r": (w_gate_r, m_w_gate_r, v_w_gate_r), "b_gate_r": (b_gate_r, m_b_gate_r, v_b_gate_r),
        "w_gate_i": (w_gate_i, m_w_gate_i, v_w_gate_i), "b_gate_i": (b_gate_i, m_b_gate_i, v_b_gate_i),
        "lru_lambda": (lru_lambda, m_lru_lambda, v_lru_lambda), "g_pre_mlp": (g_pre_mlp, m_g_pre_mlp, v_g_pre_mlp),
        "g_post_mlp": (g_post_mlp, m_g_post_mlp, v_g_post_mlp),
    }
    out = {}
    for name, g in grads.items():
        w, m_, v_ = params[name]
        flat = (-1, w.shape[-1])
        dl, nm, nv = _adamw(w.reshape(flat), g.reshape(flat), m_.reshape(flat), v_.reshape(flat))
        out[name] = (g.reshape(w.shape), dl.reshape(w.shape), nm.reshape(w.shape), nv.reshape(w.shape))
    out["w_mod"] = (g_w_mod,) + tuple(_adamw_tiled(w_mod, g_w_mod, m_w_mod, v_w_mod, min(128, d)))
    out.update(big)

    order = ("w_mod", "b_mod", "g_pre_mix", "g_post_mix", "w_in", "conv_a_w", "conv_a_b", "w_a_out", "conv_b_w", "conv_b_b", "w_gate_r", "b_gate_r",
             "w_gate_i", "b_gate_i", "lru_lambda", "w_b_out", "w_o", "g_pre_mlp", "g_post_mlp", "w_mlp_up", "w_mlp_down")
    return (loss, grad_x) + tuple(out[n][0] for n in order) + tuple(out[n][1] for n in order) + tuple(out[n][2] for n in order) + tuple(out[n][3] for n in order)
```

```python
import functools

import jax
import jax.numpy as jnp
from jax import lax
from jax.experimental import pallas as pl
from jax.experimental.pallas import tpu as pltpu

F32, BF16 = jnp.float32, jnp.bfloat16
EPS = 1e-6
LRU_C = 8.0
N_DEV = 8
N_MOD = 6
SUBLANES = 8
VMEM_BUDGET = 56 * 1024 * 1024
WGRAD_ROWS = 512
ADAM_LR, ADAM_B1, ADAM_B2, ADAM_EPS, ADAM_WD, ADAM_STEP = 0.001, 0.9, 0.999, 1e-08, 0.01, 10
MESH = pl.DeviceIdType.MESH
VMEM_SPEC = pl.BlockSpec(memory_space=pltpu.VMEM)
ANY_SPEC = pl.BlockSpec(memory_space=pl.ANY)
HBM_SPEC = pl.BlockSpec(memory_space=pltpu.HBM)
SEM_SPEC = pl.BlockSpec(memory_space=pltpu.SEMAPHORE)
SIDE_EFFECT = pltpu.SideEffectType.DATAFLOW_SIDE_EFFECTING

R_G_PRE_MIX, R_G_POST_MIX, R_CONV_A_B, R_CONV_B_B, R_B_GATE_R, R_B_GATE_I, R_LAMBDA, R_G_PRE_MLP, R_G_POST_MLP = range(9)
N_ROWS = 16
M_SH_M, M_SC_M, M_GT_M, M_SH_F, M_SC_F, M_GT_F = range(6)
CW_A, CW_B, CW_ROWS = 0, 3, 8
G_MLP_GT, G_MLP_GPOST, G_MLP_SC, G_MLP_SH, G_MLP_GPRE, G_MLP_ROWS = 0, 1, 2, 3, 4, 8
(G_MIX_GT, G_MIX_GPOST, G_MIX_CAB, G_MIX_CAW, G_MIX_CBB, G_MIX_CBW, G_MIX_BR, G_MIX_BI, G_MIX_LAM) = 0, 1, 2, 3, 6, 7, 11, 12, 13
G_MIX_ROWS = 16
G_IN_SC, G_IN_SH, G_IN_GPRE, G_IN_ROWS = 0, 1, 2, 8
G_LAYER_ROWS = G_MLP_ROWS + G_MIX_ROWS + G_IN_ROWS


def _cparams(dims=None, vmem=None):
    kw = {}
    if dims is not None:
        kw["dimension_semantics"] = dims
    if vmem is not None:
        kw["vmem_limit_bytes"] = int(min(max(vmem, 16 * 1024 * 1024), VMEM_BUDGET))
    return pltpu.CompilerParams(**kw)


def _nbytes(shape, dtype):
    n = 1
    for s in shape:
        n *= s
    return n * jnp.dtype(dtype).itemsize


def _resident(block, index_map):
    return pl.BlockSpec(block, index_map, pipeline_mode=pl.Buffered(1))


def _my_position():
    x, y, c = lax.axis_index("x"), lax.axis_index("y"), lax.axis_index("c")
    return (x, y, c), 4 * x + 2 * y + c


def _peer(pos, k):
    x, y, c = pos
    px = 1 - x if k & 4 else x
    py = 1 - y if k & 2 else y
    pc = 1 - c if k & 1 else c
    return (px, py, pc), 4 * px + 2 * py + pc


def _remote(src, dst, ssem, rsem, peer):
    return pltpu.make_async_remote_copy(src_ref=src, dst_ref=dst, send_sem=ssem, recv_sem=rsem, device_id=peer, device_id_type=MESH)


def _dot(a, b):
    return jnp.dot(a, b, preferred_element_type=F32)


def _dot_nt(a, b):
    return lax.dot_general(a, b, (((1,), (1,)), ((), ())), preferred_element_type=F32)


def _dot_tn(a, b):
    return lax.dot_general(a, b, (((0,), (0,)), ((), ())), preferred_element_type=F32)


def _colsum(v):
    return jnp.sum(v, axis=0, keepdims=True)


def _sigmoid(v):
    return jax.nn.sigmoid(v)


def _gelu(v):
    k = 0.7978845608028654
    t = jnp.tanh(k * (v + 0.044715 * (v * v * v)))
    return 0.5 * v * (1.0 + t), t


def _gelu_grad(v, t):
    k = 0.7978845608028654
    return 0.5 * (1.0 + t) + 0.5 * v * (1.0 - t * t) * (k * (1.0 + 3.0 * 0.044715 * v * v))


def _expm1(v):
    u = jnp.exp(v)
    um1 = u - 1.0
    q = um1 * v / jnp.log(u)
    return jnp.where(um1 == 0.0, v, jnp.where(um1 == -1.0, -1.0, q))


def _softplus_neg(lam):
    z = -lam
    u = jnp.exp(-jnp.abs(z))
    w = 1.0 + u
    l1p = jnp.where(w == 1.0, u, jnp.log(w) * u / (w - 1.0))
    return jnp.maximum(z, 0.0) + l1p


def _rms(v):
    return lax.rsqrt(jnp.mean(v * v, axis=-1, keepdims=True) + EPS)


def _prenorm_bwd(xv, dh, g, sc):
    r = _rms(xv)
    xn = xv * r
    n = xn * g
    dsc = _colsum(dh * n)
    dsh = _colsum(dh)
    dn = dh * (1.0 + sc)
    dg = _colsum(dn * xn)
    dxn = dn * g
    dx = r * (dxn - xn * jnp.mean(dxn * xn, axis=-1, keepdims=True))
    return dx, dsc, dsh, dg


def _postnorm_bwd(yv, dout, g, gt):
    r = _rms(yv)
    yn = yv * r
    dgt = _colsum(dout * (yn * g))
    dn = dout * gt
    dg = _colsum(dn * yn)
    dyn = dn * g
    dy = r * (dyn - yn * jnp.mean(dyn * yn, axis=-1, keepdims=True))
    return dy, dgt, dg


def _gates(xc, wr_ref, wi_ref, b_r, b_i, sp, nh, bw):
    xcb = xc.astype(BF16)
    zr = jnp.concatenate([_dot(xcb[:, h * bw:(h + 1) * bw], wr_ref[h]) for h in range(nh)], axis=1) + b_r
    zi = jnp.concatenate([_dot(xcb[:, h * bw:(h + 1) * bw], wi_ref[h]) for h in range(nh)], axis=1) + b_i
    r = _sigmoid(zr)
    ig = _sigmoid(zi)
    la = (-LRU_C * r) * sp
    a = jnp.exp(la)
    mult = jnp.sqrt(-_expm1(2.0 * la))
    return xcb, r, ig, a, mult


def _scan_block(a8, b8, reverse):
    row = lax.broadcasted_iota(jnp.int32, a8.shape, 0)
    for s in (1, 2, 4):
        if reverse:
            keep = row < SUBLANES - s
            a_sh = pltpu.roll(a8, SUBLANES - s, 0)
            b_sh = pltpu.roll(b8, SUBLANES - s, 0)
        else:
            keep = row >= s
            a_sh = pltpu.roll(a8, s, 0)
            b_sh = pltpu.roll(b8, s, 0)
        b8 = b8 + a8 * jnp.where(keep, b_sh, 0.0)
        a8 = a8 * jnp.where(keep, a_sh, 1.0)
    return a8, b8


def _prep_small(c, w_mod, b_mod, cw):
    d = c.shape[1]
    cm = w_mod.shape[2]
    cwid = cw.shape[2]
    nl = w_mod.shape[0]

    def body(c_ref, wm_ref, bm_ref, cw_ref, mod_ref, cact_ref, cwf_ref, cbuf, pbuf, rbuf, ssem, rsem, lsem):
        pos, me = _my_position()
        me8 = pl.multiple_of(me * SUBLANES, SUBLANES)
        cbuf[pl.ds(me8, SUBLANES), :] = jnp.broadcast_to(c_ref[...], (SUBLANES, d))
        own_cw = pltpu.make_async_copy(cw_ref, cwf_ref.at[:, :, pl.ds(me * cwid, cwid)], lsem.at[0])
        own_cw.start()
        first = []
        for k in range(1, N_DEV):
            peer, _ = _peer(pos, k)
            rows = cbuf.at[pl.ds(me8, SUBLANES), :]
            first.append(_remote(rows, rows, ssem.at[0, k - 1], rsem.at[0, k - 1], peer))
            first.append(_remote(cw_ref, cwf_ref.at[:, :, pl.ds(me * cwid, cwid)], ssem.at[1, k - 1], rsem.at[1, k - 1], peer))
        for cp in first:
            cp.start()
        for k in range(1, N_DEV):
            peer, pj = _peer(pos, k)
            pj8 = pl.multiple_of(pj * SUBLANES, SUBLANES)
            rows = cbuf.at[pl.ds(pj8, SUBLANES), :]
            _remote(rows, rows, ssem.at[0, k - 1], rsem.at[0, k - 1], peer).wait_recv()
        cv = cbuf[...]
        cact = cv * _sigmoid(cv)
        cact_ref[...] = cact
        cb = cact.astype(BF16)
        for l in range(nl):
            pbuf[l] = _dot(cb, wm_ref[l].astype(BF16))
        own_p = pltpu.make_async_copy(pbuf.at[:, pl.ds(me8, SUBLANES), :], rbuf.at[me], lsem.at[1])
        own_p.start()
        second = []
        for k in range(1, N_DEV):
            peer, pj = _peer(pos, k)
            pj8 = pl.multiple_of(pj * SUBLANES, SUBLANES)
            second.append(_remote(pbuf.at[:, pl.ds(pj8, SUBLANES), :], rbuf.at[me], ssem.at[2, k - 1], rsem.at[2, k - 1], peer))
        for cp in second:
            cp.start()
        for k in range(1, N_DEV):
            peer, pj = _peer(pos, k)
            _remote(pbuf.at[:, pl.ds(0, SUBLANES), :], rbuf.at[pj], ssem.at[2, k - 1], rsem.at[2, k - 1], peer).wait_recv()
            _remote(cw_ref, cwf_ref.at[:, :, pl.ds(pj * cwid, cwid)], ssem.at[1, k - 1], rsem.at[1, k - 1], peer).wait_recv()
        own_p.wait()
        own_cw.wait()
        for l in range(nl):
            for j in range(N_DEV):
                mod_ref[l:l + 1, j * cm:(j + 1) * cm] = rbuf[j, l, 0:1, :] + bm_ref[l:l + 1, j * cm:(j + 1) * cm]
        for cp in first + second:
            cp.wait_send()

    return pl.pallas_call(
        body,
        name="prep_small",
        out_shape=(
            jax.ShapeDtypeStruct((nl, N_MOD * d), F32),
            jax.ShapeDtypeStruct((N_DEV * SUBLANES, d), F32),
            jax.ShapeDtypeStruct((nl, CW_ROWS, d), F32),
        ),
        in_specs=[VMEM_SPEC] * 4,
        out_specs=(VMEM_SPEC,) * 3,
        scratch_shapes=[
            pltpu.VMEM((N_DEV * SUBLANES, d), F32),
            pltpu.VMEM((nl, N_DEV * SUBLANES, cm), F32),
            pltpu.VMEM((N_DEV, nl, SUBLANES, cm), F32),
            pltpu.SemaphoreType.DMA((3, N_DEV - 1)),
            pltpu.SemaphoreType.DMA((3, N_DEV - 1)),
            pltpu.SemaphoreType.DMA((2,)),
        ],
        compiler_params=_cparams(vmem=3 * _nbytes(w_mod.shape, F32)),
    )(c, w_mod, b_mod, cw)


def _exchange_start(parts, gather, name):
    n = len(parts)
    lands = [lax.empty(((N_DEV,) + tuple(p.shape)) if gather else tuple(p.shape), p.dtype) for p in parts]

    def body(*refs):
        ins, lnd = refs[:n], refs[n:2 * n]
        ssem, rsem, token = refs[2 * n], refs[2 * n + 1], refs[-1]
        pos, me = _my_position()
        for k in range(1, N_DEV):
            peer, pj = _peer(pos, k)
            for t in range(n):
                src = ins[t] if gather else ins[t].at[pj]
                q = t * (N_DEV - 1) + k - 1
                _remote(src, lnd[t].at[me], ssem.at[q], rsem.at[q], peer).start()
        token[...] = jnp.zeros(token.shape, F32)

    out = pl.pallas_call(
        body,
        name=name,
        out_shape=(pltpu.SemaphoreType.DMA((n * (N_DEV - 1),)), pltpu.SemaphoreType.DMA((n * (N_DEV - 1),)))
        + tuple(pltpu.HBM(p.shape, p.dtype) for p in parts) + tuple(pltpu.HBM(p.shape, p.dtype) for p in lands)
        + (jax.ShapeDtypeStruct((SUBLANES, 128), F32),),
        in_specs=[HBM_SPEC] * (2 * n),
        out_specs=(SEM_SPEC, SEM_SPEC) + (HBM_SPEC,) * (2 * n) + (VMEM_SPEC,),
        input_output_aliases={i: 2 + i for i in range(2 * n)},
        compiler_params=pltpu.CompilerParams(has_side_effects=SIDE_EFFECT),
    )(*[pltpu.with_memory_space_constraint(p, pltpu.HBM) for p in list(parts) + lands])
    return out[0], out[1], list(out[2:2 + n]), list(out[2 + n:2 + 2 * n]), out[-1]


def _exchange_wait(started, after, gather, name):
    ssem, rsem, parts, lands, _ = started
    n = len(parts)

    def body(*refs):
        ins, lnd = refs[:n], refs[n:2 * n]
        ssem_ref, rsem_ref = refs[2 * n], refs[2 * n + 1]
        stage, lsem = refs[-1 - n:-1], refs[-1]
        pos, me = _my_position()
        load = []
        for t in range(n):
            src = ins[t] if gather else ins[t].at[me]
            load.append(pltpu.make_async_copy(src, stage[t], lsem.at[t]))
            load[-1].start()
        store = []
        for t in range(n):
            load[t].wait()
            store.append(pltpu.make_async_copy(stage[t], lnd[t].at[me], lsem.at[t]))
            store[-1].start()
        for k in range(1, N_DEV):
            peer, pj = _peer(pos, k)
            for t in range(n):
                src = ins[t] if gather else ins[t].at[pj]
                q = t * (N_DEV - 1) + k - 1
                _remote(src, lnd[t].at[me], ssem_ref.at[q], rsem_ref.at[q], peer).wait_send()
                _remote(src, lnd[t].at[pj], ssem_ref.at[q], rsem_ref.at[q], peer).wait_recv()
        for cp in store:
            cp.wait()

    out = pl.pallas_call(
        body,
        name=name,
        out_shape=tuple(pltpu.HBM(p.shape, p.dtype) for p in parts) + tuple(pltpu.HBM(p.shape, p.dtype) for p in lands),
        in_specs=[HBM_SPEC] * (2 * n) + [SEM_SPEC, SEM_SPEC, ANY_SPEC],
        out_specs=(HBM_SPEC,) * (2 * n),
        input_output_aliases={i: i for i in range(2 * n)},
        scratch_shapes=[pltpu.VMEM(tuple(z.shape[1:]), z.dtype) for z in lands] + [pltpu.SemaphoreType.DMA((n,))],
        compiler_params=pltpu.CompilerParams(has_side_effects=SIDE_EFFECT),
    )(*parts, *lands, ssem, rsem, after)
    return list(out[n:])


def _reduce_small(rows, dm8, cact):
    r, d = rows.shape
    nl = dm8.shape[0]
    cm = dm8.shape[2] // N_DEV

    def body(rows_ref, dm_ref, cact_ref, orow_ref, owm_ref, gr, dmr, ssem, rsem, lsem):
        pos, me = _my_position()
        me8 = pl.multiple_of(me * SUBLANES, SUBLANES)
        gr[me] = rows_ref[...]
        own_dm = pltpu.make_async_copy(dm_ref.at[:, :, pl.ds(me * cm, cm)], dmr.at[:, pl.ds(me8, SUBLANES), :], lsem.at[0])
        own_dm.start()
        sends = []
        for k in range(1, N_DEV):
            peer, pj = _peer(pos, k)
            sends.append(_remote(gr.at[me], gr.at[me], ssem.at[0, k - 1], rsem.at[0, k - 1], peer))
            sends.append(_remote(dm_ref.at[:, :, pl.ds(pj * cm, cm)], dmr.at[:, pl.ds(me8, SUBLANES), :],
                                 ssem.at[1, k - 1], rsem.at[1, k - 1], peer))
        for cp in sends:
            cp.start()
        for k in range(1, N_DEV):
            peer, pj = _peer(pos, k)
            pj8 = pl.multiple_of(pj * SUBLANES, SUBLANES)
            _remote(gr.at[pj], gr.at[pj], ssem.at[0, k - 1], rsem.at[0, k - 1], peer).wait_recv()
            _remote(dm_ref.at[:, :, pl.ds(0, cm)], dmr.at[:, pl.ds(pj8, SUBLANES), :], ssem.at[1, k - 1], rsem.at[1, k - 1], peer).wait_recv()
        own_dm.wait()
        acc = gr[0]
        for j in range(1, N_DEV):
            acc = acc + gr[j]
        orow_ref[...] = acc
        cb = cact_ref[...].astype(BF16)
        for l in range(nl):
            owm_ref[l] = _dot_tn(cb, dmr[l].astype(BF16))
        for cp in sends:
            cp.wait_send()

    return pl.pallas_call(
        body,
        name="reduce_small",
        out_shape=(jax.ShapeDtypeStruct((r, d), F32), jax.ShapeDtypeStruct((nl, d, cm), F32)),
        in_specs=[VMEM_SPEC] * 3,
        out_specs=(VMEM_SPEC,) * 2,
        scratch_shapes=[
            pltpu.VMEM((N_DEV, r, d), F32),
            pltpu.VMEM((nl, N_DEV * SUBLANES, cm), F32),
            pltpu.SemaphoreType.DMA((2, N_DEV - 1)),
            pltpu.SemaphoreType.DMA((2, N_DEV - 1)),
            pltpu.SemaphoreType.DMA((1,)),
        ],
        compiler_params=_cparams(vmem=4 * _nbytes((N_DEV, r, d), F32) + 6 * _nbytes((nl, d, cm), F32)),
    )(rows, dm8, cact)


def _sum_gathered(zones):
    nl = len(zones)

    def body(*refs):
        for l in range(nl):
            acc = refs[l][0].astype(F32)
            for j in range(1, N_DEV):
                acc = acc + refs[l][j].astype(F32)
            refs[nl][l] = acc

    return pl.pallas_call(
        body,
        name="sum_gathered",
        out_shape=jax.ShapeDtypeStruct((nl,) + tuple(zones[0].shape[1:]), F32),
        in_specs=[VMEM_SPEC] * nl,
        out_specs=VMEM_SPEC,
        compiler_params=_cparams(vmem=8 * nl * _nbytes(zones[0].shape, BF16)),
    )(*zones)


def _in_proj_fwd(x, mod, rows, win_f, l, tm):
    s, d = x.shape
    nb, _, ci = win_f.shape

    def body(x_ref, mod_ref, rows_ref, w_ref, proj_ref, h_ref):
        xv = x_ref[...]
        g = rows_ref[R_G_PRE_MIX:R_G_PRE_MIX + 1, :]
        h = (xv * _rms(xv) * g) * (1.0 + mod_ref[M_SC_M:M_SC_M + 1, :]) + mod_ref[M_SH_M:M_SH_M + 1, :]
        hb = h.astype(BF16)
        h_ref[...] = hb
        for j in range(nb):
            proj_ref[:, j * ci:(j + 1) * ci] = _dot(hb, w_ref[j])

    return pl.pallas_call(
        body,
        name="in_proj_fwd",
        grid=(s // tm,),
        in_specs=[
            pl.BlockSpec((tm, d), lambda i: (i, 0)),
            _resident((None, N_MOD, d), lambda i: (l, 0, 0)),
            _resident((None, N_ROWS, d), lambda i: (l, 0, 0)),
            _resident((nb, d, ci), lambda i: (0, 0, 0)),
        ],
        out_specs=(pl.BlockSpec((tm, nb * ci), lambda i: (i, 0)), pl.BlockSpec((tm, d), lambda i: (i, 0))),
        out_shape=(jax.ShapeDtypeStruct((s, nb * ci), F32), jax.ShapeDtypeStruct((s, d), BF16)),
        compiler_params=_cparams(("parallel",), _nbytes((nb, d, ci), BF16) + 3 * _nbytes((tm, nb * ci), F32) + 8 * _nbytes((tm, d), F32)),
    )(x, mod, rows, win_f)


def _mixer_core_fwd(proj, x, mod, rows, cwf, wr, wi, wa_f, wb_f, wo_f, l, tm):
    s, d = x.shape
    nh, bw, _ = wr.shape[1:]

    def body(proj_ref, x_ref, mod_ref, rows_ref, cw_ref, wr_ref, wi_ref, wa_ref, wb_ref, wo_ref,
             x1_ref, hs_ref, yap_ref, ybp_ref, y_ref, cvbuf, xbbuf, a_s, b_s, hprev):
        i = pl.program_id(0)

        @pl.when(i == 0)
        def _():
            cvbuf[pl.ds(0, SUBLANES), :] = jnp.zeros((SUBLANES, d), F32)
            xbbuf[pl.ds(0, SUBLANES), :] = jnp.zeros((SUBLANES, d), F32)
            hprev[...] = jnp.zeros((SUBLANES, d), F32)

        def row(r):
            return rows_ref[r:r + 1, :]

        def tap(r):
            return cw_ref[r:r + 1, :]

        ba = proj_ref[:, 0:d]
        cv = proj_ref[:, d:2 * d] * proj_ref[:, 2 * d:3 * d]
        cvbuf[pl.ds(SUBLANES, tm), :] = cv
        conv3 = ((row(R_CONV_A_B) + cvbuf[pl.ds(SUBLANES - 2, tm), :] * tap(CW_A)) + cvbuf[pl.ds(SUBLANES - 1, tm), :] * tap(CW_A + 1)) + cv * tap(CW_A + 2)
        ya = ba * conv3
        cvbuf[pl.ds(0, SUBLANES), :] = cvbuf[pl.ds(tm, SUBLANES), :]
        xb = proj_ref[:, 3 * d:4 * d]
        xbbuf[pl.ds(SUBLANES, tm), :] = xb
        xc = (((row(R_CONV_B_B) + xbbuf[pl.ds(SUBLANES - 3, tm), :] * tap(CW_B)) + xbbuf[pl.ds(SUBLANES - 2, tm), :] * tap(CW_B + 1))
              + xbbuf[pl.ds(SUBLANES - 1, tm), :] * tap(CW_B + 2)) + xb * tap(CW_B + 3)
        xbbuf[pl.ds(0, SUBLANES), :] = xbbuf[pl.ds(tm, SUBLANES), :]
        sp = _softplus_neg(row(R_LAMBDA))
        _, _, ig, a, mult = _gates(xc, wr_ref, wi_ref, row(R_B_GATE_R), row(R_B_GATE_I), sp, nh, bw)
        a_s[...] = a
        b_s[...] = mult * (ig * xc)

        def blk(j, hp):
            o = pl.multiple_of(j * SUBLANES, SUBLANES)
            a8, b8 = _scan_block(a_s[pl.ds(o, SUBLANES), :], b_s[pl.ds(o, SUBLANES), :], reverse=False)
            h8 = b8 + a8 * hp
            hs_ref[pl.ds(o, SUBLANES), :] = h8
            return jnp.broadcast_to(h8[SUBLANES - 1:SUBLANES, :], (SUBLANES, d))

        hprev[...] = lax.fori_loop(0, tm // SUBLANES, blk, hprev[...])
        gel, _ = _gelu(proj_ref[:, 4 * d:5 * d])
        yb = hs_ref[...] * gel
        yap = _dot(ya.astype(BF16), wa_ref[...])
        ybp = _dot(yb.astype(BF16), wb_ref[...])
        yap_ref[...] = yap
        ybp_ref[...] = ybp
        m = _sigmoid(proj_ref[:, 5 * d:6 * d]) * yap + _sigmoid(proj_ref[:, 6 * d:7 * d]) * ybp
        y = _dot(m.astype(BF16), wo_ref[...])
        y_ref[...] = y
        x1_ref[...] = x_ref[...] + mod_ref[M_GT_M:M_GT_M + 1, :] * ((y * _rms(y)) * row(R_G_POST_MIX))

    tile = pl.BlockSpec((tm, d), lambda i: (i, 0))
    return pl.pallas_call(
        body,
        name="mixer_core_fwd",
        grid=(s // tm,),
        in_specs=[
            pl.BlockSpec((tm, 7 * d), lambda i: (i, 0)),
            tile,
            _resident((None, N_MOD, d), lambda i: (l, 0, 0)),
            _resident((None, N_ROWS, d), lambda i: (l, 0, 0)),
            _resident((None, CW_ROWS, d), lambda i: (l, 0, 0)),
            _resident((None, nh, bw, bw), lambda i: (l, 0, 0, 0)),
            _resident((None, nh, bw, bw), lambda i: (l, 0, 0, 0)),
            _resident((d, d), lambda i: (0, 0)),
            _resident((d, d), lambda i: (0, 0)),
            _resident((d, d), lambda i: (0, 0)),
        ],
        out_specs=(tile,) * 5,
        out_shape=(jax.ShapeDtypeStruct((s, d), F32),) * 5,
        scratch_shapes=[
            pltpu.VMEM((tm + SUBLANES, d), F32),
            pltpu.VMEM((tm + SUBLANES, d), F32),
            pltpu.VMEM((tm, d), F32),
            pltpu.VMEM((tm, d), F32),
            pltpu.VMEM((SUBLANES, d), F32),
        ],
        compiler_params=_cparams(("arbitrary",), 3 * _nbytes((d, d), BF16) + 2 * _nbytes((tm, 7 * d), F32) + 40 * _nbytes((tm, d), F32)),
    )(proj, x, mod, rows, cwf, wr, wi, wa_f, wb_f, wo_f)


def _mlp_fwd(x1, mod, rows, wup_f, wdn_f, l, tm):
    s, d = x1.shape
    nb, _, cu = wup_f.shape
    dff = nb * cu

    def body(x1_ref, mod_ref, rows_ref, wu_ref, wd_ref, x2_ref, ru_ref, y2_ref, h2_ref):
        xv = x1_ref[...]
        g = rows_ref[R_G_PRE_MLP:R_G_PRE_MLP + 1, :]
        h2 = ((xv * _rms(xv) * g) * (1.0 + mod_ref[M_SC_F:M_SC_F + 1, :]) + mod_ref[M_SH_F:M_SH_F + 1, :]).astype(BF16)
        h2_ref[...] = h2
        ru = jnp.concatenate([jnp.maximum(_dot(h2, wu_ref[j]), 0.0) for j in range(nb)], axis=1)
        ru_ref[...] = ru.astype(BF16)
        y2 = _dot((ru * ru).astype(BF16), wd_ref[...])
        y2_ref[...] = y2
        x2_ref[...] = xv + mod_ref[M_GT_F:M_GT_F + 1, :] * ((y2 * _rms(y2)) * rows_ref[R_G_POST_MLP:R_G_POST_MLP + 1, :])

    tile = pl.BlockSpec((tm, d), lambda i: (i, 0))
    wide = pl.BlockSpec((tm, dff), lambda i: (i, 0))
    return pl.pallas_call(
        body,
        name="mlp_fwd",
        grid=(s // tm,),
        in_specs=[
            tile,
            _resident((None, N_MOD, d), lambda i: (l, 0, 0)),
            _resident((None, N_ROWS, d), lambda i: (l, 0, 0)),
            _resident((nb, d, cu), lambda i: (0, 0, 0)),
            _resident((dff, d), lambda i: (0, 0)),
        ],
        out_specs=(tile, wide, tile, tile),
        out_shape=(jax.ShapeDtypeStruct((s, d), F32), jax.ShapeDtypeStruct((s, dff), BF16),
                   jax.ShapeDtypeStruct((s, d), F32), jax.ShapeDtypeStruct((s, d), BF16)),
        compiler_params=_cparams(("parallel",), 2 * _nbytes((dff, d), BF16) + 5 * _nbytes((tm, dff), F32) + 12 * _nbytes((tm, d), F32)),
    )(x1, mod, rows, wup_f, wdn_f)


def _loss_fwd_bwd(y, target, tm):
    s, d = y.shape

    def body(y_ref, t_ref, loss_ref, dy_ref):
        @pl.when(pl.program_id(0) == 0)
        def _():
            loss_ref[...] = jnp.zeros(loss_ref.shape, F32)

        e = y_ref[...] - t_ref[...]
        dy_ref[...] = e * (1.0 / d)
        loss_ref[...] += 0.5 * jnp.sum(jnp.mean(e * e, axis=-1, keepdims=True), axis=0, keepdims=True)

    tile = pl.BlockSpec((tm, d), lambda i: (i, 0))
    loss, dy = pl.pallas_call(
        body,
        name="loss",
        grid=(s // tm,),
        in_specs=[tile, tile],
        out_specs=(pl.BlockSpec((SUBLANES, 128), lambda i: (0, 0)), tile),
        out_shape=(jax.ShapeDtypeStruct((SUBLANES, 128), F32), jax.ShapeDtypeStruct((s, d), F32)),
        compiler_params=_cparams(("arbitrary",)),
    )(y, target)
    return loss[0, 0], dy


def _mlp_bwd(dx2, x1, y2, ru, mod, rows, wup_f, wdn_f, l, tm):
    s, d = x1.shape
    nb, _, cu = wup_f.shape
    dff = nb * cu

    def body(dx2_ref, x1_ref, y2_ref, ru_ref, mod_ref, rows_ref, wu_ref, wd_ref, dx1_ref, dy2_ref, dup_ref, act_ref, sm_ref):
        @pl.when(pl.program_id(0) == 0)
        def _():
            sm_ref[...] = jnp.zeros(sm_ref.shape, F32)

        dout = dx2_ref[...]
        dy2, dgt, dgpost = _postnorm_bwd(y2_ref[...], dout, rows_ref[R_G_POST_MLP:R_G_POST_MLP + 1, :], mod_ref[M_GT_F:M_GT_F + 1, :])
        dy2b = dy2.astype(BF16)
        dy2_ref[...] = dy2b
        ruv = ru_ref[...].astype(F32)
        act_ref[...] = (ruv * ruv).astype(BF16)
        dup = (_dot_nt(dy2b, wd_ref[...]) * (2.0 * ruv)).astype(BF16)
        dup_ref[...] = dup
        dh2 = _dot_nt(dup[:, 0:cu], wu_ref[0])
        for j in range(1, nb):
            dh2 = dh2 + _dot_nt(dup[:, j * cu:(j + 1) * cu], wu_ref[j])
        dxn, dsc, dsh, dgpre = _prenorm_bwd(x1_ref[...], dh2, rows_ref[R_G_PRE_MLP:R_G_PRE_MLP + 1, :], mod_ref[M_SC_F:M_SC_F + 1, :])
        dx1_ref[...] = dout + dxn
        for r, v in ((G_MLP_GT, dgt), (G_MLP_GPOST, dgpost), (G_MLP_SC, dsc), (G_MLP_SH, dsh), (G_MLP_GPRE, dgpre)):
            sm_ref[r:r + 1, :] += v

    tile = pl.BlockSpec((tm, d), lambda i: (i, 0))
    wide = pl.BlockSpec((tm, dff), lambda i: (i, 0))
    return pl.pallas_call(
        body,
        name="mlp_bwd",
        grid=(s // tm,),
        in_specs=[
            tile, tile, tile, wide,
            _resident((None, N_MOD, d), lambda i: (l, 0, 0)),
            _resident((None, N_ROWS, d), lambda i: (l, 0, 0)),
            _resident((nb, d, cu), lambda i: (0, 0, 0)),
            _resident((dff, d), lambda i: (0, 0)),
        ],
        out_specs=(tile, tile, wide, wide, pl.BlockSpec((G_MLP_ROWS, d), lambda i: (0, 0))),
        out_shape=(jax.ShapeDtypeStruct((s, d), F32), jax.ShapeDtypeStruct((s, d), BF16), jax.ShapeDtypeStruct((s, dff), BF16),
                   jax.ShapeDtypeStruct((s, dff), BF16), jax.ShapeDtypeStruct((G_MLP_ROWS, d), F32)),
        compiler_params=_cparams(("arbitrary",), 2 * _nbytes((dff, d), BF16) + 6 * _nbytes((tm, dff), F32) + 16 * _nbytes((tm, d), F32)),
    )(dx2, x1, y2, ru, mod, rows, wup_f, wdn_f)


def _mixer_core_bwd(dx1, y, yap, ybp, hs, proj, mod, rows, cwf, wr, wi, wa_f, wb_f, wo_f, l, tm):
    s, d = dx1.shape
    nh, bw, _ = wr.shape[1:]
    nt = s // tm
    per = tm // SUBLANES

    def body(dx1_ref, y_ref, yap_ref, ybp_ref, hs_ref, hsh_ref, proj_ref, projh_ref, mod_ref, rows_ref, cw_ref,
             wr_ref, wi_ref, wa_ref, wb_ref, wo_ref,
             dproj_ref, dy_ref, m_ref, dyap_ref, dybp_ref, ya_ref, yb_ref, sm_ref, dwg_ref,
             cvbuf, xbbuf, hsbuf, abuf, dcbuf, dxbuf, al_s, dh_s, lam_s, lnext):
        i = pl.program_id(0)
        first_tile = i == nt - 1

        @pl.when(i == 0)
        def _():
            sm_ref[...] = jnp.zeros(sm_ref.shape, F32)
            dwg_ref[...] = jnp.zeros(dwg_ref.shape, F32)
            zero = jnp.zeros((SUBLANES, d), F32)
            abuf[pl.ds(tm, SUBLANES), :] = zero
            dcbuf[pl.ds(tm, SUBLANES), :] = zero
            dxbuf[pl.ds(tm, SUBLANES), :] = zero
            lnext[...] = zero

        def row(r):
            return rows_ref[r:r + 1, :]

        def tap(r):
            return cw_ref[r:r + 1, :]

        def acc(r, v):
            sm_ref[r:r + 1, :] += v

        keep_halo = jnp.where(first_tile, 0.0, 1.0)
        dy, dgt, dgpost = _postnorm_bwd(y_ref[...], dx1_ref[...], row(R_G_POST_MIX), mod_ref[M_GT_M:M_GT_M + 1, :])
        acc(G_MIX_GT, dgt)
        acc(G_MIX_GPOST, dgpost)
        dyb16 = dy.astype(BF16)
        dy_ref[...] = dyb16
        dm = _dot_nt(dyb16, wo_ref[...])
        sa = _sigmoid(proj_ref[:, 5 * d:6 * d])
        sb = _sigmoid(proj_ref[:, 6 * d:7 * d])
        yap = yap_ref[...]
        ybp = ybp_ref[...]
        m_ref[...] = (sa * yap + sb * ybp).astype(BF16)
        dyap = (dm * sa).astype(BF16)
        dybp = (dm * sb).astype(BF16)
        dyap_ref[...] = dyap
        dybp_ref[...] = dybp
        dproj_ref[:, 5 * d:6 * d] = (dm * yap * sa * (1.0 - sa)).astype(BF16)
        dproj_ref[:, 6 * d:7 * d] = (dm * ybp * sb * (1.0 - sb)).astype(BF16)
        dya = _dot_nt(dyap, wa_ref[...])
        dyb = _dot_nt(dybp, wb_ref[...])
        ba = proj_ref[:, 0:d]
        ca = proj_ref[:, d:2 * d]
        va = proj_ref[:, 2 * d:3 * d]
        cv = ca * va
        cvbuf[pl.ds(0, SUBLANES), :] = keep_halo * (projh_ref[:, d:2 * d] * projh_ref[:, 2 * d:3 * d])
        cvbuf[pl.ds(SUBLANES, tm), :] = cv
        cvm2 = cvbuf[pl.ds(SUBLANES - 2, tm), :]
        cvm1 = cvbuf[pl.ds(SUBLANES - 1, tm), :]
        conv3 = ((row(R_CONV_A_B) + cvm2 * tap(CW_A)) + cvm1 * tap(CW_A + 1)) + cv * tap(CW_A + 2)
        ya_ref[...] = (ba * conv3).astype(BF16)
        dproj_ref[:, 0:d] = (dya * conv3).astype(BF16)
        dc3 = dya * ba
        acc(G_MIX_CAB, _colsum(dc3))
        acc(G_MIX_CAW, _colsum(dc3 * cvm2))
        acc(G_MIX_CAW + 1, _colsum(dc3 * cvm1))
        acc(G_MIX_CAW + 2, _colsum(dc3 * cv))
        dcbuf[pl.ds(0, tm), :] = dc3
        dcv = (dc3 * tap(CW_A + 2) + dcbuf[pl.ds(1, tm), :] * tap(CW_A + 1)) + dcbuf[pl.ds(2, tm), :] * tap(CW_A)
        dcbuf[pl.ds(tm, SUBLANES), :] = dcbuf[pl.ds(0, SUBLANES), :]
        dproj_ref[:, d:2 * d] = (dcv * va).astype(BF16)
        dproj_ref[:, 2 * d:3 * d] = (dcv * ca).astype(BF16)
        xb = proj_ref[:, 3 * d:4 * d]
        gb = proj_ref[:, 4 * d:5 * d]
        xbbuf[pl.ds(0, SUBLANES), :] = keep_halo * projh_ref[:, 3 * d:4 * d]
        xbbuf[pl.ds(SUBLANES, tm), :] = xb
        xm3 = xbbuf[pl.ds(SUBLANES - 3, tm), :]
        xm2 = xbbuf[pl.ds(SUBLANES - 2, tm), :]
        xm1 = xbbuf[pl.ds(SUBLANES - 1, tm), :]
        xc = (((row(R_CONV_B_B) + xm3 * tap(CW_B)) + xm2 * tap(CW_B + 1)) + xm1 * tap(CW_B + 2)) + xb * tap(CW_B + 3)
        lam = row(R_LAMBDA)
        sp = _softplus_neg(lam)
        xcb, r, ig, a, mult = _gates(xc, wr_ref, wi_ref, row(R_B_GATE_R), row(R_B_GATE_I), sp, nh, bw)
        gel, th = _gelu(gb)
        hs = hs_ref[...]
        yb_ref[...] = (hs * gel).astype(BF16)
        dproj_ref[:, 4 * d:5 * d] = (dyb * hs * _gelu_grad(gb, th)).astype(BF16)
        abuf[pl.ds(0, tm), :] = a
        al_s[...] = abuf[pl.ds(1, tm), :]
        abuf[pl.ds(tm, SUBLANES), :] = abuf[pl.ds(0, SUBLANES), :]
        dh_s[...] = dyb * gel

        def blk(j, ln):
            o = pl.multiple_of((per - 1 - j) * SUBLANES, SUBLANES)
            a8, b8 = _scan_block(al_s[pl.ds(o, SUBLANES), :], dh_s[pl.ds(o, SUBLANES), :], reverse=True)
            l8 = b8 + a8 * ln
            lam_s[pl.ds(o, SUBLANES), :] = l8
            return jnp.broadcast_to(l8[0:1, :], (SUBLANES, d))

        lnext[...] = lax.fori_loop(0, per, blk, lnext[...])
        dbb = lam_s[...]
        hsbuf[pl.ds(0, SUBLANES), :] = keep_halo * hsh_ref[...]
        hsbuf[pl.ds(SUBLANES, tm), :] = hs
        da = dbb * hsbuf[pl.ds(SUBLANES - 1, tm), :]
        dmult = dbb * (ig * xc)
        dig = dbb * (mult * xc)
        dxc = dbb * (mult * ig)
        dla = da * a - dmult * (a * a) / mult
        acc(G_MIX_LAM, _colsum(dla * (-LRU_C * r)) * (-_sigmoid(-lam)))
        dzr = (dla * (-LRU_C * sp)) * r * (1.0 - r)
        dzi = dig * ig * (1.0 - ig)
        acc(G_MIX_BR, _colsum(dzr))
        acc(G_MIX_BI, _colsum(dzi))
        dzrb = dzr.astype(BF16)
        dzib = dzi.astype(BF16)
        back = []
        for h in range(nh):
            sl = slice(h * bw, (h + 1) * bw)
            back.append(_dot_nt(dzrb[:, sl], wr_ref[h]) + _dot_nt(dzib[:, sl], wi_ref[h]))
            dwg_ref[0, h] += _dot_tn(xcb[:, sl], dzrb[:, sl])
            dwg_ref[1, h] += _dot_tn(xcb[:, sl], dzib[:, sl])
        dxc = dxc + jnp.concatenate(back, axis=1)
        acc(G_MIX_CBB, _colsum(dxc))
        acc(G_MIX_CBW, _colsum(dxc * xm3))
        acc(G_MIX_CBW + 1, _colsum(dxc * xm2))
        acc(G_MIX_CBW + 2, _colsum(dxc * xm1))
        acc(G_MIX_CBW + 3, _colsum(dxc * xb))
        dxbuf[pl.ds(0, tm), :] = dxc
        dxb = ((dxc * tap(CW_B + 3) + dxbuf[pl.ds(1, tm), :] * tap(CW_B + 2)) + dxbuf[pl.ds(2, tm), :] * tap(CW_B + 1)) + dxbuf[pl.ds(3, tm), :] * tap(CW_B)
        dxbuf[pl.ds(tm, SUBLANES), :] = dxbuf[pl.ds(0, SUBLANES), :]
        dproj_ref[:, 3 * d:4 * d] = dxb.astype(BF16)

    def rev(i):
        return (nt - 1 - i, 0)

    def halo(i):
        return (jnp.maximum((nt - 1 - i) * per - 1, 0), 0)

    tile = pl.BlockSpec((tm, d), rev)
    return pl.pallas_call(
        body,
        name="mixer_core_bwd",
        grid=(nt,),
        in_specs=[
            tile, tile, tile, tile, tile,
            pl.BlockSpec((SUBLANES, d), halo),
            pl.BlockSpec((tm, 7 * d), rev),
            pl.BlockSpec((SUBLANES, 7 * d), halo),
            _resident((None, N_MOD, d), lambda i: (l, 0, 0)),
            _resident((None, N_ROWS, d), lambda i: (l, 0, 0)),
            _resident((None, CW_ROWS, d), lambda i: (l, 0, 0)),
            _resident((None, nh, bw, bw), lambda i: (l, 0, 0, 0)),
            _resident((None, nh, bw, bw), lambda i: (l, 0, 0, 0)),
            _resident((d, d), lambda i: (0, 0)),
            _resident((d, d), lambda i: (0, 0)),
            _resident((d, d), lambda i: (0, 0)),
        ],
        out_specs=(pl.BlockSpec((tm, 7 * d), rev),) + (tile,) * 6 + (
            pl.BlockSpec((G_MIX_ROWS, d), lambda i: (0, 0)), pl.BlockSpec((2, nh, bw, bw), lambda i: (0, 0, 0, 0))),
        out_shape=(jax.ShapeDtypeStruct((s, 7 * d), BF16),) + (jax.ShapeDtypeStruct((s, d), BF16),) * 6 + (
            jax.ShapeDtypeStruct((G_MIX_ROWS, d), F32), jax.ShapeDtypeStruct((2, nh, bw, bw), F32)),
        scratch_shapes=[pltpu.VMEM((tm + SUBLANES, d), F32)] * 6 + [pltpu.VMEM((tm, d), F32)] * 3 + [pltpu.VMEM((SUBLANES, d), F32)],
        compiler_params=_cparams(("arbitrary",), 3 * _nbytes((d, d), BF16) + 3 * _nbytes((tm, 7 * d), F32) + 64 * _nbytes((tm, d), F32)),
    )(dx1, y, yap, ybp, hs, hs, proj, proj, mod, rows, cwf, wr, wi, wa_f, wb_f, wo_f)


def _in_proj_bwd(dproj, x, dx1, mod, rows, win_f, l, tm):
    s, d = x.shape
    nb, _, ci = win_f.shape

    def body(dp_ref, x_ref, dx1_ref, mod_ref, rows_ref, w_ref, dx_ref, sm_ref):
        @pl.when(pl.program_id(0) == 0)
        def _():
            sm_ref[...] = jnp.zeros(sm_ref.shape, F32)

        dh = _dot_nt(dp_ref[:, 0:ci], w_ref[0])
        for j in range(1, nb):
            dh = dh + _dot_nt(dp_ref[:, j * ci:(j + 1) * ci], w_ref[j])
        dxn, dsc, dsh, dg = _prenorm_bwd(x_ref[...], dh, rows_ref[R_G_PRE_MIX:R_G_PRE_MIX + 1, :], mod_ref[M_SC_M:M_SC_M + 1, :])
        dx_ref[...] = dx1_ref[...] + dxn
        for r, v in ((G_IN_SC, dsc), (G_IN_SH, dsh), (G_IN_GPRE, dg)):
            sm_ref[r:r + 1, :] += v

    tile = pl.BlockSpec((tm, d), lambda i: (i, 0))
    return pl.pallas_call(
        body,
        name="in_proj_bwd",
        grid=(s // tm,),
        in_specs=[
            pl.BlockSpec((tm, nb * ci), lambda i: (i, 0)), tile, tile,
            _resident((None, N_MOD, d), lambda i: (l, 0, 0)),
            _resident((None, N_ROWS, d), lambda i: (l, 0, 0)),
            _resident((nb, d, ci), lambda i: (0, 0, 0)),
        ],
        out_specs=(tile, pl.BlockSpec((G_IN_ROWS, d), lambda i: (0, 0))),
        out_shape=(jax.ShapeDtypeStruct((s, d), F32), jax.ShapeDtypeStruct((G_IN_ROWS, d), F32)),
        compiler_params=_cparams(("arbitrary",), _nbytes((nb, d, ci), BF16) + 2 * _nbytes((tm, nb * ci), BF16) + 16 * _nbytes((tm, d), F32)),
    )(dproj, x, dx1, mod, rows, win_f)


def _wgrad(a, b, cols_owned, ts):
    s, k1 = a.shape
    k2 = b.shape[1]
    ns = s // ts
    if cols_owned:
        nblk, bk1, bk2 = N_DEV, k1, k2 // N_DEV
        a_spec = pl.BlockSpec((ts, bk1), lambda j, t: (t, 0))
        b_spec = pl.BlockSpec((ts, bk2), lambda j, t: (t, j))
    else:
        bk1, bk2 = min(WGRAD_ROWS, k1), k2
        nblk = k1 // bk1
        a_spec = pl.BlockSpec((ts, bk1), lambda j, t: (t, j))
        b_spec = pl.BlockSpec((ts, bk2), lambda j, t: (t, 0))

    def body(a_ref, b_ref, o_ref, acc_ref):
        t = pl.program_id(1)

        @pl.when(t == 0)
        def _():
            acc_ref[...] = jnp.zeros(acc_ref.shape, F32)

        acc_ref[...] += _dot_tn(a_ref[...], b_ref[...])

        @pl.when(t == ns - 1)
        def _():
            o_ref[...] = acc_ref[...].astype(BF16)

    out = pl.pallas_call(
        body,
        name="wgrad",
        grid=(nblk, ns),
        in_specs=[a_spec, b_spec],
        out_specs=pl.BlockSpec((None, bk1, bk2), lambda j, t: (j, 0, 0)),
        out_shape=pltpu.HBM((nblk, bk1, bk2), BF16),
        scratch_shapes=[pltpu.VMEM((bk1, bk2), F32)],
        compiler_params=_cparams(("parallel", "arbitrary"), 4 * _nbytes((bk1, bk2), F32) + 4 * _nbytes((ts, bk1 + bk2), BF16)),
    )(pltpu.with_memory_space_constraint(a, pltpu.HBM), pltpu.with_memory_space_constraint(b, pltpu.HBM))
    return out if cols_owned else out.reshape(N_DEV, k1 // N_DEV, k2)


def _adam_update(w, g, m, v):
    m = ADAM_B1 * m + (1.0 - ADAM_B1) * g
    v = ADAM_B2 * v + (1.0 - ADAM_B2) * (g * g)
    m_hat = m / (1.0 - ADAM_B1 ** ADAM_STEP)
    v_hat = v / (1.0 - ADAM_B2 ** ADAM_STEP)
    delta = -ADAM_LR * (m_hat / (jnp.sqrt(v_hat) + ADAM_EPS) + ADAM_WD * w)
    return delta, m, v


def _sum_adamw(recv, w, m, v, tr):
    nl, ra, cb = w.shape
    assert nl == len(recv) == 2

    def body(r0_ref, r1_ref, w_ref, m_ref, v_ref, g_ref, d_ref, nm_ref, nv_ref):
        def total(r_ref):
            g = r_ref[0].astype(F32)
            for j in range(1, N_DEV):
                g = g + r_ref[j].astype(F32)
            return g

        g = jnp.where(pl.program_id(0) == 0, total(r0_ref), total(r1_ref))
        g_ref[...] = g
        d_ref[...], nm_ref[...], nv_ref[...] = _adam_update(w_ref[...], g, m_ref[...], v_ref[...])

    blk = pl.BlockSpec((None, tr, cb), lambda l, i: (l, i, 0))
    return pl.pallas_call(
        body,
        name="sum_adamw",
        grid=(nl, ra // tr),
        in_specs=[pl.BlockSpec((N_DEV, tr, cb), lambda l, i: (0, i * (1 - l), 0)),
                  pl.BlockSpec((N_DEV, tr, cb), lambda l, i: (0, i * l, 0)), blk, blk, blk],
        out_specs=(blk,) * 4,
        out_shape=(jax.ShapeDtypeStruct((nl, ra, cb), F32),) * 4,
        compiler_params=_cparams(("arbitrary", "arbitrary")),
    )(recv[0], recv[1], w, m, v)


def _adamw(w, g, m, v):
    def body(w_ref, g_ref, m_ref, v_ref, d_ref, nm_ref, nv_ref):
        d_ref[...], nm_ref[...], nv_ref[...] = _adam_update(w_ref[...], g_ref[...], m_ref[...], v_ref[...])

    return pl.pallas_call(
        body,
        name="adamw",
        in_specs=[VMEM_SPEC] * 4,
        out_specs=(VMEM_SPEC,) * 3,
        out_shape=(jax.ShapeDtypeStruct(w.shape, F32),) * 3,
        compiler_params=_cparams(vmem=10 * _nbytes(w.shape, F32)),
    )(w, g, m, v)


def _adamw_tiled(w, g, m, v, tr):
    nl, ra, cb = w.shape

    def body(w_ref, g_ref, m_ref, v_ref, d_ref, nm_ref, nv_ref):
        d_ref[...], nm_ref[...], nv_ref[...] = _adam_update(w_ref[...], g_ref[...], m_ref[...], v_ref[...])

    blk = pl.BlockSpec((None, tr, cb), lambda l, i: (l, i, 0))
    return pl.pallas_call(
        body,
        name="adamw_tiled",
        grid=(nl, ra // tr),
        in_specs=[blk] * 4,
        out_specs=(blk,) * 3,
        out_shape=(jax.ShapeDtypeStruct((nl, ra, cb), F32),) * 3,
        compiler_params=_cparams(("parallel", "parallel")),
    )(w, g, m, v)


def _token_tile(s):
    return min(256, max(SUBLANES * 2, s // 4))


def kernel(x, c, w_mod, b_mod, g_pre_mix, g_post_mix, w_in, conv_a_w, conv_a_b, w_a_out, conv_b_w, conv_b_b, w_gate_r, b_gate_r, w_gate_i, b_gate_i, lru_lambda, w_b_out, w_o, g_pre_mlp, g_post_mlp, w_mlp_up, w_mlp_down, loss_target, m_w_mod, m_b_mod, m_g_pre_mix, m_g_post_mix, m_w_in, m_conv_a_w, m_conv_a_b, m_w_a_out, m_conv_b_w, m_conv_b_b, m_w_gate_r, m_b_gate_r, m_w_gate_i, m_b_gate_i, m_lru_lambda, m_w_b_out, m_w_o, m_g_pre_mlp, m_g_post_mlp, m_w_mlp_up, m_w_mlp_down, v_w_mod, v_b_mod, v_g_pre_mix, v_g_post_mix, v_w_in, v_conv_a_w, v_conv_a_b, v_w_a_out, v_conv_b_w, v_conv_b_b, v_w_gate_r, v_b_gate_r, v_w_gate_i, v_b_gate_i, v_lru_lambda, v_w_b_out, v_w_o, v_g_pre_mlp, v_g_post_mlp, v_w_mlp_up, v_w_mlp_down):
    nl = w_mod.shape[0]
    s, d = x.shape[1], x.shape[2]
    nh, bw = w_gate_r.shape[1], w_gate_r.shape[2]
    cwid = conv_a_w.shape[2]
    tm = _token_tile(s)
    ts = s
    _, me = _my_position()
    xs = x.reshape(s, d)
    target = loss_target.reshape(s, d)

    vec_names = (g_pre_mix, g_post_mix, conv_a_b, conv_b_b, b_gate_r, b_gate_i, lru_lambda, g_pre_mlp, g_post_mlp)
    rows = jnp.concatenate([jnp.stack(vec_names, axis=1), jnp.zeros((nl, N_ROWS - len(vec_names), d), F32)], axis=1)
    cw = jnp.concatenate([conv_a_w, conv_b_w, jnp.zeros((nl, CW_ROWS - 7, cwid), F32)], axis=1)

    w16 = {"w_in": w_in.astype(BF16), "w_a_out": w_a_out.astype(BF16), "w_b_out": w_b_out.astype(BF16), "w_o": w_o.astype(BF16),
           "w_mlp_up": w_mlp_up.astype(BF16), "w_mlp_down": w_mlp_down.astype(BF16)}
    groups = (("in", ("w_in",)), ("mix", ("w_a_out", "w_b_out", "w_o")), ("mlp", ("w_mlp_up", "w_mlp_down")))
    mod, cact, cwf = _prep_small(c, w_mod, b_mod, cw)
    mod = mod.reshape(nl, N_MOD, d)
    w16, _ = lax.optimization_barrier((w16, mod))
    gathers = {}
    token = jnp.zeros((), F32)
    for l in range(nl):
        for gname, members in groups:
            gathers[l, gname] = _exchange_start([w16[n][l] for n in members], True, f"gather_start_{gname}{l}")
            token = token + gathers[l, gname][4][0, 0]
    rows = rows + token
    wr = w_gate_r.astype(BF16)
    wi = w_gate_i.astype(BF16)

    saved = []
    weights = []
    xin = xs
    for l in range(nl):
        (win_f,) = _exchange_wait(gathers[l, "in"], mod if l == 0 else xin, True, f"gather_wait_in{l}")
        proj, h = _in_proj_fwd(xin, mod, rows, win_f, l, tm)
        wa_f, wb_f, wo_f = (w.reshape(d, d) for w in _exchange_wait(gathers[l, "mix"], proj, True, f"gather_wait_mix{l}"))
        x1, hs, yap, ybp, y = _mixer_core_fwd(proj, xin, mod, rows, cwf, wr, wi, wa_f, wb_f, wo_f, l, tm)
        wup_f, wdn_f = _exchange_wait(gathers[l, "mlp"], x1, True, f"gather_wait_mlp{l}")
        wdn_f = wdn_f.reshape(-1, d)
        x2, ru, y2, h2 = _mlp_fwd(x1, mod, rows, wup_f, wdn_f, l, tm)
        saved.append((xin, proj, h, x1, hs, yap, ybp, y, ru, y2, h2))
        weights.append((win_f, wa_f, wb_f, wo_f, wup_f, wdn_f))
        xin = x2
    loss_part, dx = _loss_fwd_bwd(xin, target, tm)
    loss = lax.psum(loss_part, ("x", "y", "c"))

    scatters = {}
    small = [None] * nl
    gate_parts = [None] * nl

    def scatter(l, gname, parts, rows):
        scatters[l, gname] = _exchange_start(parts, False, f"scatter_start_{gname}{l}")
        return rows + scatters[l, gname][4][0, 0]

    for l in reversed(range(nl)):
        xin, proj, h, x1, hs, yap, ybp, y, ru, y2, h2 = saved[l]
        win_f, wa_f, wb_f, wo_f, wup_f, wdn_f = weights[l]
        dx1, dy2, dup, act, sm_mlp = _mlp_bwd(dx, x1, y2, ru, mod, rows, wup_f, wdn_f, l, tm)
        rows = scatter(l, "mlp", [_wgrad(h2, dup, True, ts), _wgrad(act, dy2, False, ts)], rows)
        dproj, dy, m, dyap, dybp, ya, yb, sm_mix, dwg = _mixer_core_bwd(dx1, y, yap, ybp, hs, proj, mod, rows, cwf, wr, wi, wa_f, wb_f, wo_f, l, tm // 2)
        gate_parts[l] = _exchange_start([dwg.astype(BF16)], True, f"gates_start{l}")
        rows = rows + gate_parts[l][4][0, 0]
        rows = scatter(l, "in", [_wgrad(h, dproj, True, ts)], rows)
        rows = scatter(l, "mix", [_wgrad(ya, dyap, False, ts), _wgrad(yb, dybp, False, ts), _wgrad(m, dy, False, ts)], rows)
        dx, sm_in = _in_proj_bwd(dproj, xin, dx1, mod, rows, win_f, l, tm)
        small[l] = jnp.concatenate([sm_mlp, sm_mix, sm_in], axis=0)
    grad_x = dx.reshape(x.shape)

    recv = {}
    big = {}
    moments = {"w_in": (w_in, m_w_in, v_w_in), "w_mlp_up": (w_mlp_up, m_w_mlp_up, v_w_mlp_up), "w_a_out": (w_a_out, m_w_a_out, v_w_a_out),
               "w_b_out": (w_b_out, m_w_b_out, v_w_b_out), "w_o": (w_o, m_w_o, v_w_o), "w_mlp_down": (w_mlp_down, m_w_mlp_down, v_w_mlp_down)}

    def collect(l, gname, after):
        for n, zone in zip(dict(groups)[gname], _exchange_wait(scatters[l, gname], after, False, f"scatter_wait_{gname}{l}")):
            recv[n, l] = zone

    def update(name):
        w, m_, v_ = moments[name]
        big[name] = _sum_adamw([recv[name, l] for l in range(nl)], w, m_, v_, min(128, w.shape[1]))

    for l, gname in ((1, "mlp"), (1, "in"), (1, "mix"), (0, "mlp")):
        collect(l, gname, dx)
    update("w_mlp_up")
    update("w_mlp_down")

    lrows = jnp.concatenate(small, axis=0)
    lrows, _ = lax.optimization_barrier((lrows, big["w_mlp_down"][1]))

    def lrow(a, l, r):
        return a[l * G_LAYER_ROWS + r]

    dm = jnp.stack([jnp.concatenate([lrow(lrows, l, G_MLP_ROWS + G_MIX_ROWS + G_IN_SH), lrow(lrows, l, G_MLP_ROWS + G_MIX_ROWS + G_IN_SC),
                                     lrow(lrows, l, G_MLP_ROWS + G_MIX_GT), lrow(lrows, l, G_MLP_SH), lrow(lrows, l, G_MLP_SC),
                                     lrow(lrows, l, G_MLP_GT)]) for l in range(nl)])
    dm8 = jnp.concatenate([dm[:, None, :], jnp.zeros((nl, SUBLANES - 1, N_MOD * d), F32)], axis=1)
    srows, g_w_mod = _reduce_small(lrows, dm8, cact)
    sgates = _sum_gathered([_exchange_wait(gate_parts[l], srows, True, f"gates_wait{l}")[0] for l in range(nl)])
    for gname in ("in", "mix"):
        collect(0, gname, srows)
    for name in ("w_in", "w_a_out", "w_b_out", "w_o"):
        update(name)

    def srow(l, r):
        return lrow(srows, l, r)

    def per_layer(r):
        return jnp.stack([srow(l, r) for l in range(nl)])

    mix0 = G_MLP_ROWS
    in0 = G_MLP_ROWS + G_MIX_ROWS
    g_b_mod = jnp.stack([jnp.concatenate([srow(l, in0 + G_IN_SH), srow(l, in0 + G_IN_SC), srow(l, mix0 + G_MIX_GT),
                                          srow(l, G_MLP_SH), srow(l, G_MLP_SC), srow(l, G_MLP_GT)]) for l in range(nl)])
    conv_a_full = jnp.stack([jnp.stack([srow(l, mix0 + G_MIX_CAW + k) for k in range(3)]) for l in range(nl)])
    conv_b_full = jnp.stack([jnp.stack([srow(l, mix0 + G_MIX_CBW + k) for k in range(4)]) for l in range(nl)])
    grads = {
        "b_mod": g_b_mod,
        "g_pre_mix": per_layer(in0 + G_IN_GPRE),
        "g_post_mix": per_layer(mix0 + G_MIX_GPOST),
        "conv_a_w": lax.dynamic_slice_in_dim(conv_a_full, me * cwid, cwid, axis=2),
        "conv_a_b": per_layer(mix0 + G_MIX_CAB),
        "conv_b_w": lax.dynamic_slice_in_dim(conv_b_full, me * cwid, cwid, axis=2),
        "conv_b_b": per_layer(mix0 + G_MIX_CBB),
        "w_gate_r": sgates[:, 0],
        "b_gate_r": per_layer(mix0 + G_MIX_BR),
        "w_gate_i": sgates[:, 1],
        "b_gate_i": per_layer(mix0 + G_MIX_BI),
        "lru_lambda": per_layer(mix0 + G_MIX_LAM),
        "g_pre_mlp": per_layer(G_MLP_GPRE),
        "g_post_mlp": per_layer(G_MLP_GPOST),
    }
    params = {
        "b_mod": (b_mod, m_b_mod, v_b_mod), "g_pre_mix": (g_pre_mix, m_g_pre_mix, v_g_pre_mix), "g_post_mix": (g_post_mix, m_g_post_mix, v_g_post_mix),
        "conv_a_w": (conv_a_w, m_conv_a_w, v_conv_a_w), "conv_a_b": (conv_a_b, m_conv_a_b, v_conv_a_b),
        "conv_b_w": (conv_b_w, m_conv_b_w, v_conv_b_w), "conv_b_b": (conv_b_b, m_conv_b_b, v_conv_b_b),
        "w_gate_r": (w_gate_r, m_w_gate_r, v_w_gate_r), "b_gate_r": (b_gate_r, m_b_gate_r, v_b_gate_r),
        "w_gate_i": (w_gate_i, m_w_gate_i, v_w_gate_i), "b_gate_i": (b_gate_i, m_b_gate_i, v_b_gate_i),
        "lru_lambda": (lru_lambda, m_lru_lambda, v_lru_lambda), "g_pre_mlp": (g_pre_mlp, m_g_pre_mlp, v_g_pre_mlp),
        "g_post_mlp": (g_post_mlp, m_g_post_mlp, v_g_post_mlp),
    }
    out = {}
    for name, g in grads.items():
        w, m_, v_ = params[name]
        flat = (-1, w.shape[-1])
        dl, nm, nv = _adamw(w.reshape(flat), g.reshape(flat), m_.reshape(flat), v_.reshape(flat))
        out[name] = (g.reshape(w.shape), dl.reshape(w.shape), nm.reshape(w.shape), nv.reshape(w.shape))
    out["w_mod"] = (g_w_mod,) + tuple(_adamw_tiled(w_mod, g_w_mod, m_w_mod, v_w_mod, min(128, d)))
    out.update(big)

    order = ("w_mod", "b_mod", "g_pre_mix", "g_post_mix", "w_in", "conv_a_w", "conv_a_b", "w_a_out", "conv_b_w", "conv_b_b", "w_gate_r", "b_gate_r",
             "w_gate_i", "b_gate_i", "lru_lambda", "w_b_out", "w_o", "g_pre_mlp", "g_post_mlp", "w_mlp_up", "w_mlp_down")
    return (loss, grad_x) + tuple(out[n][0] for n in order) + tuple(out[n][1] for n in order) + tuple(out[n][2] for n in order) + tuple(out[n][3] for n in order)
```

```python
import functools

import jax
import jax.numpy as jnp
from jax import lax
from jax.experimental import pallas as pl
from jax.experimental.pallas import tpu as pltpu

F32, BF16 = jnp.float32, jnp.bfloat16
EPS = 1e-6
LRU_C = 8.0
N_DEV = 8
N_MOD = 6
SUBLANES = 8
VMEM_BUDGET = 56 * 1024 * 1024
WGRAD_ROWS = 512
ADAM_LR, ADAM_B1, ADAM_B2, ADAM_EPS, ADAM_WD, ADAM_STEP = 0.001, 0.9, 0.999, 1e-08, 0.01, 10
MESH = pl.DeviceIdType.MESH
VMEM_SPEC = pl.BlockSpec(memory_space=pltpu.VMEM)
ANY_SPEC = pl.BlockSpec(memory_space=pl.ANY)
HBM_SPEC = pl.BlockSpec(memory_space=pltpu.HBM)
SEM_SPEC = pl.BlockSpec(memory_space=pltpu.SEMAPHORE)
SIDE_EFFECT = pltpu.SideEffectType.DATAFLOW_SIDE_EFFECTING

R_G_PRE_MIX, R_G_POST_MIX, R_CONV_A_B, R_CONV_B_B, R_B_GATE_R, R_B_GATE_I, R_LAMBDA, R_G_PRE_MLP, R_G_POST_MLP = range(9)
N_ROWS = 16
M_SH_M, M_SC_M, M_GT_M, M_SH_F, M_SC_F, M_GT_F = range(6)
CW_A, CW_B, CW_ROWS = 0, 3, 8
G_MLP_GT, G_MLP_GPOST, G_MLP_SC, G_MLP_SH, G_MLP_GPRE, G_MLP_ROWS = 0, 1, 2, 3, 4, 8
(G_MIX_GT, G_MIX_GPOST, G_MIX_CAB, G_MIX_CAW, G_MIX_CBB, G_MIX_CBW, G_MIX_BR, G_MIX_BI, G_MIX_LAM) = 0, 1, 2, 3, 6, 7, 11, 12, 13
G_MIX_ROWS = 16
G_IN_SC, G_IN_SH, G_IN_GPRE, G_IN_ROWS = 0, 1, 2, 8
G_LAYER_ROWS = G_MLP_ROWS + G_MIX_ROWS + G_IN_ROWS


def _cparams(dims=None, vmem=None):
    kw = {}
    if dims is not None:
        kw["dimension_semantics"] = dims
    if vmem is not None:
        kw["vmem_limit_bytes"] = int(min(max(vmem, 16 * 1024 * 1024), VMEM_BUDGET))
    return pltpu.CompilerParams(**kw)


def _nbytes(shape, dtype):
    n = 1
    for s in shape:
        n *= s
    return n * jnp.dtype(dtype).itemsize


def _resident(block, index_map):
    return pl.BlockSpec(block, index_map, pipeline_mode=pl.Buffered(1))


def _my_position():
    x, y, c = lax.axis_index("x"), lax.axis_index("y"), lax.axis_index("c")
    return (x, y, c), 4 * x + 2 * y + c


def _peer(pos, k):
    x, y, c = pos
    px = 1 - x if k & 4 else x
    py = 1 - y if k & 2 else y
    pc = 1 - c if k & 1 else c
    return (px, py, pc), 4 * px + 2 * py + pc


def _remote(src, dst, ssem, rsem, peer):
    return pltpu.make_async_remote_copy(src_ref=src, dst_ref=dst, send_sem=ssem, recv_sem=rsem, device_id=peer, device_id_type=MESH)


def _dot(a, b):
    return jnp.dot(a, b, preferred_element_type=F32)


def _dot_nt(a, b):
    return lax.dot_general(a, b, (((1,), (1,)), ((), ())), preferred_element_type=F32)


def _dot_tn(a, b):
    return lax.dot_general(a, b, (((0,), (0,)), ((), ())), preferred_element_type=F32)


def _colsum(v):
    return jnp.sum(v, axis=0, keepdims=True)


def _sigmoid(v):
    return jax.nn.sigmoid(v)


def _gelu(v):
    k = 0.7978845608028654
    t = jnp.tanh(k * (v + 0.044715 * (v * v * v)))
    return 0.5 * v * (1.0 + t), t


def _gelu_grad(v, t):
    k = 0.7978845608028654
    return 0.5 * (1.0 + t) + 0.5 * v * (1.0 - t * t) * (k * (1.0 + 3.0 * 0.044715 * v * v))


def _expm1(v):
    u = jnp.exp(v)
    um1 = u - 1.0
    q = um1 * v / jnp.log(u)
    return jnp.where(um1 == 0.0, v, jnp.where(um1 == -1.0, -1.0, q))


def _softplus_neg(lam):
    z = -lam
    u = jnp.exp(-jnp.abs(z))
    w = 1.0 + u
    l1p = jnp.where(w == 1.0, u, jnp.log(w) * u / (w - 1.0))
    return jnp.maximum(z, 0.0) + l1p


def _rms(v):
    return lax.rsqrt(jnp.mean(v * v, axis=-1, keepdims=True) + EPS)


def _prenorm_bwd(xv, dh, g, sc):
    r = _rms(xv)
    xn = xv * r
    n = xn * g
    dsc = _colsum(dh * n)
    dsh = _colsum(dh)
    dn = dh * (1.0 + sc)
    dg = _colsum(dn * xn)
    dxn = dn * g
    dx = r * (dxn - xn * jnp.mean(dxn * xn, axis=-1, keepdims=True))
    return dx, dsc, dsh, dg


def _postnorm_bwd(yv, dout, g, gt):
    r = _rms(yv)
    yn = yv * r
    dgt = _colsum(dout * (yn * g))
    dn = dout * gt
    dg = _colsum(dn * yn)
    dyn = dn * g
    dy = r * (dyn - yn * jnp.mean(dyn * yn, axis=-1, keepdims=True))
    return dy, dgt, dg


def _gates(xc, wr_ref, wi_ref, b_r, b_i, sp, nh, bw):
    xcb = xc.astype(BF16)
    zr = jnp.concatenate([_dot(xcb[:, h * bw:(h + 1) * bw], wr_ref[h]) for h in range(nh)], axis=1) + b_r
    zi = jnp.concatenate([_dot(xcb[:, h * bw:(h + 1) * bw], wi_ref[h]) for h in range(nh)], axis=1) + b_i
    r = _sigmoid(zr)
    ig = _sigmoid(zi)
    la = (-LRU_C * r) * sp
    a = jnp.exp(la)
    mult = jnp.sqrt(-_expm1(2.0 * la))
    return xcb, r, ig, a, mult


def _scan_block(a8, b8, reverse):
    row = lax.broadcasted_iota(jnp.int32, a8.shape, 0)
    for s in (1, 2, 4):
        if reverse:
            keep = row < SUBLANES - s
            a_sh = pltpu.roll(a8, SUBLANES - s, 0)
            b_sh = pltpu.roll(b8, SUBLANES - s, 0)
        else:
            keep = row >= s
            a_sh = pltpu.roll(a8, s, 0)
            b_sh = pltpu.roll(b8, s, 0)
        b8 = b8 + a8 * jnp.where(keep, b_sh, 0.0)
        a8 = a8 * jnp.where(keep, a_sh, 1.0)
    return a8, b8


def _prep_small(c, w_mod, b_mod, cw):
    d = c.shape[1]
    cm = w_mod.shape[2]
    cwid = cw.shape[2]
    nl = w_mod.shape[0]

    def body(c_ref, wm_ref, bm_ref, cw_ref, mod_ref, cact_ref, cwf_ref, cbuf, pbuf, rbuf, ssem, rsem, lsem):
        pos, me = _my_position()
        me8 = pl.multiple_of(me * SUBLANES, SUBLANES)
        cbuf[pl.ds(me8, SUBLANES), :] = jnp.broadcast_to(c_ref[...], (SUBLANES, d))
        own_cw = pltpu.make_async_copy(cw_ref, cwf_ref.at[:, :, pl.ds(me * cwid, cwid)], lsem.at[0])
        own_cw.start()
        first = []
        for k in range(1, N_DEV):
            peer, _ = _peer(pos, k)
            rows = cbuf.at[pl.ds(me8, SUBLANES), :]
            first.append(_remote(rows, rows, ssem.at[0, k - 1], rsem.at[0, k - 1], peer))
            first.append(_remote(cw_ref, cwf_ref.at[:, :, pl.ds(me * cwid, cwid)], ssem.at[1, k - 1], rsem.at[1, k - 1], peer))
        for cp in first:
            cp.start()
        for k in range(1, N_DEV):
            peer, pj = _peer(pos, k)
            pj8 = pl.multiple_of(pj * SUBLANES, SUBLANES)
            rows = cbuf.at[pl.ds(pj8, SUBLANES), :]
            _remote(rows, rows, ssem.at[0, k - 1], rsem.at[0, k - 1], peer).wait_recv()
        cv = cbuf[...]
        cact = cv * _sigmoid(cv)
        cact_ref[...] = cact
        cb = cact.astype(BF16)
        for l in range(nl):
            pbuf[l] = _dot(cb, wm_ref[l].astype(BF16))
        own_p = pltpu.make_async_copy(pbuf.at[:, pl.ds(me8, SUBLANES), :], rbuf.at[me], lsem.at[1])
        own_p.start()
        second = []
        for k in range(1, N_DEV):
            peer, pj = _peer(pos, k)
            pj8 = pl.multiple_of(pj * SUBLANES, SUBLANES)
            second.append(_remote(pbuf.at[:, pl.ds(pj8, SUBLANES), :], rbuf.at[me], ssem.at[2, k - 1], rsem.at[2, k - 1], peer))
        for cp in second:
            cp.start()
        for k in range(1, N_DEV):
            peer, pj = _peer(pos, k)
            _remote(pbuf.at[:, pl.ds(0, SUBLANES), :], rbuf.at[pj], ssem.at[2, k - 1], rsem.at[2, k - 1], peer).wait_recv()
            _remote(cw_ref, cwf_ref.at[:, :, pl.ds(pj * cwid, cwid)], ssem.at[1, k - 1], rsem.at[1, k - 1], peer).wait_recv()
        own_p.wait()
        own_cw.wait()
        for l in range(nl):
            for j in range(N_DEV):
                mod_ref[l:l + 1, j * cm:(j + 1) * cm] = rbuf[j, l, 0:1, :] + bm_ref[l:l + 1, j * cm:(j + 1) * cm]
        for cp in first + second:
            cp.wait_send()

    return pl.pallas_call(
        body,
        name="prep_small",
        out_shape=(
            jax.ShapeDtypeStruct((nl, N_MOD * d), F32),
            jax.ShapeDtypeStruct((N_DEV * SUBLANES, d), F32),
            jax.ShapeDtypeStruct((nl, CW_ROWS, d), F32),
        ),
        in_specs=[VMEM_SPEC] * 4,
        out_specs=(VMEM_SPEC,) * 3,
        scratch_shapes=[
            pltpu.VMEM((N_DEV * SUBLANES, d), F32),
            pltpu.VMEM((nl, N_DEV * SUBLANES, cm), F32),
            pltpu.VMEM((N_DEV, nl, SUBLANES, cm), F32),
            pltpu.SemaphoreType.DMA((3, N_DEV - 1)),
            pltpu.SemaphoreType.DMA((3, N_DEV - 1)),
            pltpu.SemaphoreType.DMA((2,)),
        ],
        compiler_params=_cparams(vmem=3 * _nbytes(w_mod.shape, F32)),
    )(c, w_mod, b_mod, cw)


def _exchange_start(parts, gather, name):
    n = len(parts)
    lands = [lax.empty(((N_DEV,) + tuple(p.shape)) if gather else tuple(p.shape), p.dtype) for p in parts]

    def body(*refs):
        ins, lnd = refs[:n], refs[n:2 * n]
        ssem, rsem, token = refs[2 * n], refs[2 * n + 1], refs[-1]
        pos, me = _my_position()
        for k in range(1, N_DEV):
            peer, pj = _peer(pos, k)
            for t in range(n):
                src = ins[t] if gather else ins[t].at[pj]
                q = t * (N_DEV - 1) + k - 1
                _remote(src, lnd[t].at[me], ssem.at[q], rsem.at[q], peer).start()
        token[...] = jnp.zeros(token.shape, F32)

    out = pl.pallas_call(
        body,
        name=name,
        out_shape=(pltpu.SemaphoreType.DMA((n * (N_DEV - 1),)), pltpu.SemaphoreType.DMA((n * (N_DEV - 1),)))
        + tuple(pltpu.HBM(p.shape, p.dtype) for p in parts) + tuple(pltpu.HBM(p.shape, p.dtype) for p in lands)
        + (jax.ShapeDtypeStruct((SUBLANES, 128), F32),),
        in_specs=[HBM_SPEC] * (2 * n),
        out_specs=(SEM_SPEC, SEM_SPEC) + (HBM_SPEC,) * (2 * n) + (VMEM_SPEC,),
        input_output_aliases={i: 2 + i for i in range(2 * n)},
        compiler_params=pltpu.CompilerParams(has_side_effects=SIDE_EFFECT),
    )(*[pltpu.with_memory_space_constraint(p, pltpu.HBM) for p in list(parts) + lands])
    return out[0], out[1], list(out[2:2 + n]), list(out[2 + n:2 + 2 * n]), out[-1]


def _exchange_wait(started, after, gather, name):
    ssem, rsem, parts, lands, _ = started
    n = len(parts)

    def body(*refs):
        ins, lnd = refs[:n], refs[n:2 * n]
        ssem_ref, rsem_ref = refs[2 * n], refs[2 * n + 1]
        stage, lsem = refs[-1 - n:-1], refs[-1]
        pos, me = _my_position()
        load = []
        for t in range(n):
            src = ins[t] if gather else ins[t].at[me]
            load.append(pltpu.make_async_copy(src, stage[t], lsem.at[t]))
            load[-1].start()
        store = []
        for t in range(n):
            load[t].wait()
            store.append(pltpu.make_async_copy(stage[t], lnd[t].at[me], lsem.at[t]))
            store[-1].start()
        for k in range(1, N_DEV):
            peer, pj = _peer(pos, k)
            for t in range(n):
                src = ins[t] if gather else ins[t].at[pj]
                q = t * (N_DEV - 1) + k - 1
                _remote(src, lnd[t].at[me], ssem_ref.at[q], rsem_ref.at[q], peer).wait_send()
                _remote(src, lnd[t].at[pj], ssem_ref.at[q], rsem_ref.at[q], peer).wait_recv()
        for cp in store:
            cp.wait()

    out = pl.pallas_call(
        body,
        name=name,
        out_shape=tuple(pltpu.HBM(p.shape, p.dtype) for p in parts) + tuple(pltpu.HBM(p.shape, p.dtype) for p in lands),
        in_specs=[HBM_SPEC] * (2 * n) + [SEM_SPEC, SEM_SPEC, ANY_SPEC],
        out_specs=(HBM_SPEC,) * (2 * n),
        input_output_aliases={i: i for i in range(2 * n)},
        scratch_shapes=[pltpu.VMEM(tuple(z.shape[1:]), z.dtype) for z in lands] + [pltpu.SemaphoreType.DMA((n,))],
        compiler_params=pltpu.CompilerParams(has_side_effects=SIDE_EFFECT),
    )(*parts, *lands, ssem, rsem, after)
    return list(out[n:])


def _reduce_small(rows, dm8, cact):
    r, d = rows.shape
    nl = dm8.shape[0]
    cm = dm8.shape[2] // N_DEV

    def body(rows_ref, dm_ref, cact_ref, orow_ref, owm_ref, gr, dmr, ssem, rsem, lsem):
        pos, me = _my_position()
        me8 = pl.multiple_of(me * SUBLANES, SUBLANES)
        gr[me] = rows_ref[...]
        own_dm = pltpu.make_async_copy(dm_ref.at[:, :, pl.ds(me * cm, cm)], dmr.at[:, pl.ds(me8, SUBLANES), :], lsem.at[0])
        own_dm.start()
        sends = []
        for k in range(1, N_DEV):
            peer, pj = _peer(pos, k)
            sends.append(_remote(gr.at[me], gr.at[me], ssem.at[0, k - 1], rsem.at[0, k - 1], peer))
            sends.append(_remote(dm_ref.at[:, :, pl.ds(pj * cm, cm)], dmr.at[:, pl.ds(me8, SUBLANES), :],
                                 ssem.at[1, k - 1], rsem.at[1, k - 1], peer))
        for cp in sends:
            cp.start()
        for k in range(1, N_DEV):
            peer, pj = _peer(pos, k)
            pj8 = pl.multiple_of(pj * SUBLANES, SUBLANES)
            _remote(gr.at[pj], gr.at[pj], ssem.at[0, k - 1], rsem.at[0, k - 1], peer).wait_recv()
            _remote(dm_ref.at[:, :, pl.ds(0, cm)], dmr.at[:, pl.ds(pj8, SUBLANES), :], ssem.at[1, k - 1], rsem.at[1, k - 1], peer).wait_recv()
        own_dm.wait()
        acc = gr[0]
        for j in range(1, N_DEV):
            acc = acc + gr[j]
        orow_ref[...] = acc
        cb = cact_ref[...].astype(BF16)
        for l in range(nl):
            owm_ref[l] = _dot_tn(cb, dmr[l].astype(BF16))
        for cp in sends:
            cp.wait_send()

    return pl.pallas_call(
        body,
        name="reduce_small",
        out_shape=(jax.ShapeDtypeStruct((r, d), F32), jax.ShapeDtypeStruct((nl, d, cm), F32)),
        in_specs=[VMEM_SPEC] * 3,
        out_specs=(VMEM_SPEC,) * 2,
        scratch_shapes=[
            pltpu.VMEM((N_DEV, r, d), F32),
            pltpu.VMEM((nl, N_DEV * SUBLANES, cm), F32),
            pltpu.SemaphoreType.DMA((2, N_DEV - 1)),
            pltpu.SemaphoreType.DMA((2, N_DEV - 1)),
            pltpu.SemaphoreType.DMA((1,)),
        ],
        compiler_params=_cparams(vmem=4 * _nbytes((N_DEV, r, d), F32) + 6 * _nbytes((nl, d, cm), F32)),
    )(rows, dm8, cact)


def _sum_gathered(zones):
    nl = len(zones)

    def body(*refs):
        for l in range(nl):
            acc = refs[l][0].astype(F32)
            for j in range(1, N_DEV):
                acc = acc + refs[l][j].astype(F32)
            refs[nl][l] = acc

    return pl.pallas_call(
        body,
        name="sum_gathered",
        out_shape=jax.ShapeDtypeStruct((nl,) + tuple(zones[0].shape[1:]), F32),
        in_specs=[VMEM_SPEC] * nl,
        out_specs=VMEM_SPEC,
        compiler_params=_cparams(vmem=8 * nl * _nbytes(zones[0].shape, BF16)),
    )(*zones)


def _mixer_fwd(x, mod, rows, cwf, wr, wi, win_f, wa_f, wb_f, wo_f, l, tm):
    s, d = x.shape
    nh, bw, _ = wr.shape[1:]
    nb, _, ci = win_f.shape

    def body(x_ref, mod_ref, rows_ref, cw_ref, wr_ref, wi_ref, win_ref, wa_ref, wb_ref, wo_ref,
             proj_ref, h_ref, x1_ref, hs_ref, yap_ref, ybp_ref, y_ref, cvbuf, xbbuf, a_s, b_s, hprev):
        i = pl.program_id(0)
        xv = x_ref[...]
        hb = ((xv * _rms(xv) * rows_ref[R_G_PRE_MIX:R_G_PRE_MIX + 1, :]) * (1.0 + mod_ref[M_SC_M:M_SC_M + 1, :])
              + mod_ref[M_SH_M:M_SH_M + 1, :]).astype(BF16)
        h_ref[...] = hb
        for j in range(nb):
            proj_ref[:, j * ci:(j + 1) * ci] = _dot(hb, win_ref[j])

        @pl.when(i == 0)
        def _():
            cvbuf[pl.ds(0, SUBLANES), :] = jnp.zeros((SUBLANES, d), F32)
            xbbuf[pl.ds(0, SUBLANES), :] = jnp.zeros((SUBLANES, d), F32)
            hprev[...] = jnp.zeros((SUBLANES, d), F32)

        def row(r):
            return rows_ref[r:r + 1, :]

        def tap(r):
            return cw_ref[r:r + 1, :]

        ba = proj_ref[:, 0:d]
        cv = proj_ref[:, d:2 * d] * proj_ref[:, 2 * d:3 * d]
        cvbuf[pl.ds(SUBLANES, tm), :] = cv
        conv3 = ((row(R_CONV_A_B) + cvbuf[pl.ds(SUBLANES - 2, tm), :] * tap(CW_A)) + cvbuf[pl.ds(SUBLANES - 1, tm), :] * tap(CW_A + 1)) + cv * tap(CW_A + 2)
        ya = ba * conv3
        cvbuf[pl.ds(0, SUBLANES), :] = cvbuf[pl.ds(tm, SUBLANES), :]
        xb = proj_ref[:, 3 * d:4 * d]
        xbbuf[pl.ds(SUBLANES, tm), :] = xb
        xc = (((row(R_CONV_B_B) + xbbuf[pl.ds(SUBLANES - 3, tm), :] * tap(CW_B)) + xbbuf[pl.ds(SUBLANES - 2, tm), :] * tap(CW_B + 1))
              + xbbuf[pl.ds(SUBLANES - 1, tm), :] * tap(CW_B + 2)) + xb * tap(CW_B + 3)
        xbbuf[pl.ds(0, SUBLANES), :] = xbbuf[pl.ds(tm, SUBLANES), :]
        sp = _softplus_neg(row(R_LAMBDA))
        _, _, ig, a, mult = _gates(xc, wr_ref, wi_ref, row(R_B_GATE_R), row(R_B_GATE_I), sp, nh, bw)
        a_s[...] = a
        b_s[...] = mult * (ig * xc)

        def blk(j, hp):
            o = pl.multiple_of(j * SUBLANES, SUBLANES)
            a8, b8 = _scan_block(a_s[pl.ds(o, SUBLANES), :], b_s[pl.ds(o, SUBLANES), :], reverse=False)
            h8 = b8 + a8 * hp
            hs_ref[pl.ds(o, SUBLANES), :] = h8
            return jnp.broadcast_to(h8[SUBLANES - 1:SUBLANES, :], (SUBLANES, d))

        hprev[...] = lax.fori_loop(0, tm // SUBLANES, blk, hprev[...])
        gel, _ = _gelu(proj_ref[:, 4 * d:5 * d])
        yb = hs_ref[...] * gel
        yap = _dot(ya.astype(BF16), wa_ref[...])
        ybp = _dot(yb.astype(BF16), wb_ref[...])
        yap_ref[...] = yap
        ybp_ref[...] = ybp
        m = _sigmoid(proj_ref[:, 5 * d:6 * d]) * yap + _sigmoid(proj_ref[:, 6 * d:7 * d]) * ybp
        y = _dot(m.astype(BF16), wo_ref[...])
        y_ref[...] = y
        x1_ref[...] = xv + mod_ref[M_GT_M:M_GT_M + 1, :] * ((y * _rms(y)) * row(R_G_POST_MIX))

    tile = pl.BlockSpec((tm, d), lambda i: (i, 0))
    return pl.pallas_call(
        body,
        name="mixer_fwd",
        grid=(s // tm,),
        in_specs=[
            tile,
            _resident((None, N_MOD, d), lambda i: (l, 0, 0)),
            _resident((None, N_ROWS, d), lambda i: (l, 0, 0)),
            _resident((None, CW_ROWS, d), lambda i: (l, 0, 0)),
            _resident((None, nh, bw, bw), lambda i: (l, 0, 0, 0)),
            _resident((None, nh, bw, bw), lambda i: (l, 0, 0, 0)),
            _resident((nb, d, ci), lambda i: (0, 0, 0)),
            _resident((d, d), lambda i: (0, 0)),
            _resident((d, d), lambda i: (0, 0)),
            _resident((d, d), lambda i: (0, 0)),
        ],
        out_specs=(pl.BlockSpec((tm, 7 * d), lambda i: (i, 0)), tile) + (tile,) * 5,
        out_shape=(jax.ShapeDtypeStruct((s, 7 * d), F32), jax.ShapeDtypeStruct((s, d), BF16)) + (jax.ShapeDtypeStruct((s, d), F32),) * 5,
        scratch_shapes=[
            pltpu.VMEM((tm + SUBLANES, d), F32),
            pltpu.VMEM((tm + SUBLANES, d), F32),
            pltpu.VMEM((tm, d), F32),
            pltpu.VMEM((tm, d), F32),
            pltpu.VMEM((SUBLANES, d), F32),
        ],
        compiler_params=_cparams(("arbitrary",), 10 * _nbytes((d, d), BF16) + 3 * _nbytes((tm, 7 * d), F32) + 48 * _nbytes((tm, d), F32)),
    )(x, mod, rows, cwf, wr, wi, win_f, wa_f, wb_f, wo_f)


def _mlp_fwd(x1, mod, rows, wup_f, wdn_f, l, tm):
    s, d = x1.shape
    nb, _, cu = wup_f.shape
    dff = nb * cu

    def body(x1_ref, mod_ref, rows_ref, wu_ref, wd_ref, x2_ref, ru_ref, y2_ref, h2_ref):
        xv = x1_ref[...]
        g = rows_ref[R_G_PRE_MLP:R_G_PRE_MLP + 1, :]
        h2 = ((xv * _rms(xv) * g) * (1.0 + mod_ref[M_SC_F:M_SC_F + 1, :]) + mod_ref[M_SH_F:M_SH_F + 1, :]).astype(BF16)
        h2_ref[...] = h2
        ru = jnp.concatenate([jnp.maximum(_dot(h2, wu_ref[j]), 0.0) for j in range(nb)], axis=1)
        ru_ref[...] = ru.astype(BF16)
        y2 = _dot((ru * ru).astype(BF16), wd_ref[...])
        y2_ref[...] = y2
        x2_ref[...] = xv + mod_ref[M_GT_F:M_GT_F + 1, :] * ((y2 * _rms(y2)) * rows_ref[R_G_POST_MLP:R_G_POST_MLP + 1, :])

    tile = pl.BlockSpec((tm, d), lambda i: (i, 0))
    wide = pl.BlockSpec((tm, dff), lambda i: (i, 0))
    return pl.pallas_call(
        body,
        name="mlp_fwd",
        grid=(s // tm,),
        in_specs=[
            tile,
            _resident((None, N_MOD, d), lambda i: (l, 0, 0)),
            _resident((None, N_ROWS, d), lambda i: (l, 0, 0)),
            _resident((nb, d, cu), lambda i: (0, 0, 0)),
            _resident((dff, d), lambda i: (0, 0)),
        ],
        out_specs=(tile, wide, tile, tile),
        out_shape=(jax.ShapeDtypeStruct((s, d), F32), jax.ShapeDtypeStruct((s, dff), BF16),
                   jax.ShapeDtypeStruct((s, d), F32), jax.ShapeDtypeStruct((s, d), BF16)),
        compiler_params=_cparams(("parallel",), 2 * _nbytes((dff, d), BF16) + 5 * _nbytes((tm, dff), F32) + 12 * _nbytes((tm, d), F32)),
    )(x1, mod, rows, wup_f, wdn_f)


def _loss_fwd_bwd(y, target, tm):
    s, d = y.shape

    def body(y_ref, t_ref, loss_ref, dy_ref):
        @pl.when(pl.program_id(0) == 0)
        def _():
            loss_ref[...] = jnp.zeros(loss_ref.shape, F32)

        e = y_ref[...] - t_ref[...]
        dy_ref[...] = e * (1.0 / d)
        loss_ref[...] += 0.5 * jnp.sum(jnp.mean(e * e, axis=-1, keepdims=True), axis=0, keepdims=True)

    tile = pl.BlockSpec((tm, d), lambda i: (i, 0))
    loss, dy = pl.pallas_call(
        body,
        name="loss",
        grid=(s // tm,),
        in_specs=[tile, tile],
        out_specs=(pl.BlockSpec((SUBLANES, 128), lambda i: (0, 0)), tile),
        out_shape=(jax.ShapeDtypeStruct((SUBLANES, 128), F32), jax.ShapeDtypeStruct((s, d), F32)),
        compiler_params=_cparams(("arbitrary",)),
    )(y, target)
    return loss[0, 0], dy


def _mlp_bwd(dx2, x1, y2, ru, mod, rows, wup_f, wdn_f, l, tm):
    s, d = x1.shape
    nb, _, cu = wup_f.shape
    dff = nb * cu

    def body(dx2_ref, x1_ref, y2_ref, ru_ref, mod_ref, rows_ref, wu_ref, wd_ref, dx1_ref, dy2_ref, dup_ref, act_ref, sm_ref):
        @pl.when(pl.program_id(0) == 0)
        def _():
            sm_ref[...] = jnp.zeros(sm_ref.shape, F32)

        dout = dx2_ref[...]
        dy2, dgt, dgpost = _postnorm_bwd(y2_ref[...], dout, rows_ref[R_G_POST_MLP:R_G_POST_MLP + 1, :], mod_ref[M_GT_F:M_GT_F + 1, :])
        dy2b = dy2.astype(BF16)
        dy2_ref[...] = dy2b
        ruv = ru_ref[...].astype(F32)
        act_ref[...] = (ruv * ruv).astype(BF16)
        dup = (_dot_nt(dy2b, wd_ref[...]) * (2.0 * ruv)).astype(BF16)
        dup_ref[...] = dup
        dh2 = _dot_nt(dup[:, 0:cu], wu_ref[0])
        for j in range(1, nb):
            dh2 = dh2 + _dot_nt(dup[:, j * cu:(j + 1) * cu], wu_ref[j])
        dxn, dsc, dsh, dgpre = _prenorm_bwd(x1_ref[...], dh2, rows_ref[R_G_PRE_MLP:R_G_PRE_MLP + 1, :], mod_ref[M_SC_F:M_SC_F + 1, :])
        dx1_ref[...] = dout + dxn
        for r, v in ((G_MLP_GT, dgt), (G_MLP_GPOST, dgpost), (G_MLP_SC, dsc), (G_MLP_SH, dsh), (G_MLP_GPRE, dgpre)):
            sm_ref[r:r + 1, :] += v

    tile = pl.BlockSpec((tm, d), lambda i: (i, 0))
    wide = pl.BlockSpec((tm, dff), lambda i: (i, 0))
    return pl.pallas_call(
        body,
        name="mlp_bwd",
        grid=(s // tm,),
        in_specs=[
            tile, tile, tile, wide,
            _resident((None, N_MOD, d), lambda i: (l, 0, 0)),
            _resident((None, N_ROWS, d), lambda i: (l, 0, 0)),
            _resident((nb, d, cu), lambda i: (0, 0, 0)),
            _resident((dff, d), lambda i: (0, 0)),
        ],
        out_specs=(tile, tile, wide, wide, pl.BlockSpec((G_MLP_ROWS, d), lambda i: (0, 0))),
        out_shape=(jax.ShapeDtypeStruct((s, d), F32), jax.ShapeDtypeStruct((s, d), BF16), jax.ShapeDtypeStruct((s, dff), BF16),
                   jax.ShapeDtypeStruct((s, dff), BF16), jax.ShapeDtypeStruct((G_MLP_ROWS, d), F32)),
        compiler_params=_cparams(("arbitrary",), 2 * _nbytes((dff, d), BF16) + 6 * _nbytes((tm, dff), F32) + 16 * _nbytes((tm, d), F32)),
    )(dx2, x1, y2, ru, mod, rows, wup_f, wdn_f)


def _mixer_bwd(dx1, x, y, yap, ybp, hs, proj, mod, rows, cwf, wr, wi, win_f, wa_f, wb_f, wo_f, l, tm):
    s, d = dx1.shape
    nh, bw, _ = wr.shape[1:]
    nb, _, ci = win_f.shape
    nt = s // tm
    per = tm // SUBLANES

    def body(dx1_ref, x_ref, y_ref, yap_ref, ybp_ref, hs_ref, hsh_ref, proj_ref, projh_ref, mod_ref, rows_ref, cw_ref,
             wr_ref, wi_ref, win_ref, wa_ref, wb_ref, wo_ref,
             dproj_ref, dy_ref, m_ref, dyap_ref, dybp_ref, ya_ref, yb_ref, sm_ref, dwg_ref, dx_ref, smin_ref,
             cvbuf, xbbuf, hsbuf, abuf, dcbuf, dxbuf, al_s, dh_s, lam_s, lnext):
        i = pl.program_id(0)
        first_tile = i == nt - 1

        @pl.when(i == 0)
        def _():
            sm_ref[...] = jnp.zeros(sm_ref.shape, F32)
            smin_ref[...] = jnp.zeros(smin_ref.shape, F32)
            dwg_ref[...] = jnp.zeros(dwg_ref.shape, F32)
            zero = jnp.zeros((SUBLANES, d), F32)
            abuf[pl.ds(tm, SUBLANES), :] = zero
            dcbuf[pl.ds(tm, SUBLANES), :] = zero
            dxbuf[pl.ds(tm, SUBLANES), :] = zero
            lnext[...] = zero

        def row(r):
            return rows_ref[r:r + 1, :]

        def tap(r):
            return cw_ref[r:r + 1, :]

        def acc(r, v):
            sm_ref[r:r + 1, :] += v

        keep_halo = jnp.where(first_tile, 0.0, 1.0)
        dx1 = dx1_ref[...]
        dy, dgt, dgpost = _postnorm_bwd(y_ref[...], dx1, row(R_G_POST_MIX), mod_ref[M_GT_M:M_GT_M + 1, :])
        acc(G_MIX_GT, dgt)
        acc(G_MIX_GPOST, dgpost)
        dyb16 = dy.astype(BF16)
        dy_ref[...] = dyb16
        dm = _dot_nt(dyb16, wo_ref[...])
        sa = _sigmoid(proj_ref[:, 5 * d:6 * d])
        sb = _sigmoid(proj_ref[:, 6 * d:7 * d])
        yap = yap_ref[...]
        ybp = ybp_ref[...]
        m_ref[...] = (sa * yap + sb * ybp).astype(BF16)
        dyap = (dm * sa).astype(BF16)
        dybp = (dm * sb).astype(BF16)
        dyap_ref[...] = dyap
        dybp_ref[...] = dybp
        dproj_ref[:, 5 * d:6 * d] = (dm * yap * sa * (1.0 - sa)).astype(BF16)
        dproj_ref[:, 6 * d:7 * d] = (dm * ybp * sb * (1.0 - sb)).astype(BF16)
        dya = _dot_nt(dyap, wa_ref[...])
        dyb = _dot_nt(dybp, wb_ref[...])
        ba = proj_ref[:, 0:d]
        ca = proj_ref[:, d:2 * d]
        va = proj_ref[:, 2 * d:3 * d]
        cv = ca * va
        cvbuf[pl.ds(0, SUBLANES), :] = keep_halo * (projh_ref[:, d:2 * d] * projh_ref[:, 2 * d:3 * d])
        cvbuf[pl.ds(SUBLANES, tm), :] = cv
        cvm2 = cvbuf[pl.ds(SUBLANES - 2, tm), :]
        cvm1 = cvbuf[pl.ds(SUBLANES - 1, tm), :]
        conv3 = ((row(R_CONV_A_B) + cvm2 * tap(CW_A)) + cvm1 * tap(CW_A + 1)) + cv * tap(CW_A + 2)
        ya_ref[...] = (ba * conv3).astype(BF16)
        dproj_ref[:, 0:d] = (dya * conv3).astype(BF16)
        dc3 = dya * ba
        acc(G_MIX_CAB, _colsum(dc3))
        acc(G_MIX_CAW, _colsum(dc3 * cvm2))
        acc(G_MIX_CAW + 1, _colsum(dc3 * cvm1))
        acc(G_MIX_CAW + 2, _colsum(dc3 * cv))
        dcbuf[pl.ds(0, tm), :] = dc3
        dcv = (dc3 * tap(CW_A + 2) + dcbuf[pl.ds(1, tm), :] * tap(CW_A + 1)) + dcbuf[pl.ds(2, tm), :] * tap(CW_A)
        dcbuf[pl.ds(tm, SUBLANES), :] = dcbuf[pl.ds(0, SUBLANES), :]
        dproj_ref[:, d:2 * d] = (dcv * va).astype(BF16)
        dproj_ref[:, 2 * d:3 * d] = (dcv * ca).astype(BF16)
        xb = proj_ref[:, 3 * d:4 * d]
        gb = proj_ref[:, 4 * d:5 * d]
        xbbuf[pl.ds(0, SUBLANES), :] = keep_halo * projh_ref[:, 3 * d:4 * d]
        xbbuf[pl.ds(SUBLANES, tm), :] = xb
        xm3 = xbbuf[pl.ds(SUBLANES - 3, tm), :]
        xm2 = xbbuf[pl.ds(SUBLANES - 2, tm), :]
        xm1 = xbbuf[pl.ds(SUBLANES - 1, tm), :]
        xc = (((row(R_CONV_B_B) + xm3 * tap(CW_B)) + xm2 * tap(CW_B + 1)) + xm1 * tap(CW_B + 2)) + xb * tap(CW_B + 3)
        lam = row(R_LAMBDA)
        sp = _softplus_neg(lam)
        xcb, r, ig, a, mult = _gates(xc, wr_ref, wi_ref, row(R_B_GATE_R), row(R_B_GATE_I), sp, nh, bw)
        gel, th = _gelu(gb)
        hs = hs_ref[...]
        yb_ref[...] = (hs * gel).astype(BF16)
        dproj_ref[:, 4 * d:5 * d] = (dyb * hs * _gelu_grad(gb, th)).astype(BF16)
        abuf[pl.ds(0, tm), :] = a
        al_s[...] = abuf[pl.ds(1, tm), :]
        abuf[pl.ds(tm, SUBLANES), :] = abuf[pl.ds(0, SUBLANES), :]
        dh_s[...] = dyb * gel

        def blk(j, ln):
            o = pl.multiple_of((per - 1 - j) * SUBLANES, SUBLANES)
            a8, b8 = _scan_block(al_s[pl.ds(o, SUBLANES), :], dh_s[pl.ds(o, SUBLANES), :], reverse=True)
            l8 = b8 + a8 * ln
            lam_s[pl.ds(o, SUBLANES), :] = l8
            return jnp.broadcast_to(l8[0:1, :], (SUBLANES, d))

        lnext[...] = lax.fori_loop(0, per, blk, lnext[...])
        dbb = lam_s[...]
        hsbuf[pl.ds(0, SUBLANES), :] = keep_halo * hsh_ref[...]
        hsbuf[pl.ds(SUBLANES, tm), :] = hs
        da = dbb * hsbuf[pl.ds(SUBLANES - 1, tm), :]
        dmult = dbb * (ig * xc)
        dig = dbb * (mult * xc)
        dxc = dbb * (mult * ig)
        dla = da * a - dmult * (a * a) / mult
        acc(G_MIX_LAM, _colsum(dla * (-LRU_C * r)) * (-_sigmoid(-lam)))
        dzr = (dla * (-LRU_C * sp)) * r * (1.0 - r)
        dzi = dig * ig * (1.0 - ig)
        acc(G_MIX_BR, _colsum(dzr))
        acc(G_MIX_BI, _colsum(dzi))
        dzrb = dzr.astype(BF16)
        dzib = dzi.astype(BF16)
        back = []
        for h in range(nh):
            sl = slice(h * bw, (h + 1) * bw)
            back.append(_dot_nt(dzrb[:, sl], wr_ref[h]) + _dot_nt(dzib[:, sl], wi_ref[h]))
            dwg_ref[0, h] += _dot_tn(xcb[:, sl], dzrb[:, sl])
            dwg_ref[1, h] += _dot_tn(xcb[:, sl], dzib[:, sl])
        dxc = dxc + jnp.concatenate(back, axis=1)
        acc(G_MIX_CBB, _colsum(dxc))
        acc(G_MIX_CBW, _colsum(dxc * xm3))
        acc(G_MIX_CBW + 1, _colsum(dxc * xm2))
        acc(G_MIX_CBW + 2, _colsum(dxc * xm1))
        acc(G_MIX_CBW + 3, _colsum(dxc * xb))
        dxbuf[pl.ds(0, tm), :] = dxc
        dxb = ((dxc * tap(CW_B + 3) + dxbuf[pl.ds(1, tm), :] * tap(CW_B + 2)) + dxbuf[pl.ds(2, tm), :] * tap(CW_B + 1)) + dxbuf[pl.ds(3, tm), :] * tap(CW_B)
        dxbuf[pl.ds(tm, SUBLANES), :] = dxbuf[pl.ds(0, SUBLANES), :]
        dproj_ref[:, 3 * d:4 * d] = dxb.astype(BF16)
        dh = _dot_nt(dproj_ref[:, 0:ci], win_ref[0])
        for j in range(1, nb):
            dh = dh + _dot_nt(dproj_ref[:, j * ci:(j + 1) * ci], win_ref[j])
        dxn, dsc, dsh, dgpre = _prenorm_bwd(x_ref[...], dh, row(R_G_PRE_MIX), mod_ref[M_SC_M:M_SC_M + 1, :])
        dx_ref[...] = dx1 + dxn
        for r, v in ((G_IN_SC, dsc), (G_IN_SH, dsh), (G_IN_GPRE, dgpre)):
            smin_ref[r:r + 1, :] += v

    def rev(i):
        return (nt - 1 - i, 0)

    def halo(i):
        return (jnp.maximum((nt - 1 - i) * per - 1, 0), 0)

    tile = pl.BlockSpec((tm, d), rev)
    return pl.pallas_call(
        body,
        name="mixer_bwd",
        grid=(nt,),
        in_specs=[
            tile, tile, tile, tile, tile, tile,
            pl.BlockSpec((SUBLANES, d), halo),
            pl.BlockSpec((tm, 7 * d), rev),
            pl.BlockSpec((SUBLANES, 7 * d), halo),
            _resident((None, N_MOD, d), lambda i: (l, 0, 0)),
            _resident((None, N_ROWS, d), lambda i: (l, 0, 0)),
            _resident((None, CW_ROWS, d), lambda i: (l, 0, 0)),
            _resident((None, nh, bw, bw), lambda i: (l, 0, 0, 0)),
            _resident((None, nh, bw, bw), lambda i: (l, 0, 0, 0)),
            _resident((nb, d, ci), lambda i: (0, 0, 0)),
            _resident((d, d), lambda i: (0, 0)),
            _resident((d, d), lambda i: (0, 0)),
            _resident((d, d), lambda i: (0, 0)),
        ],
        out_specs=(pl.BlockSpec((tm, 7 * d), rev),) + (tile,) * 6 + (
            pl.BlockSpec((G_MIX_ROWS, d), lambda i: (0, 0)), pl.BlockSpec((2, nh, bw, bw), lambda i: (0, 0, 0, 0)),
            tile, pl.BlockSpec((G_IN_ROWS, d), lambda i: (0, 0))),
        out_shape=(jax.ShapeDtypeStruct((s, 7 * d), BF16),) + (jax.ShapeDtypeStruct((s, d), BF16),) * 6 + (
            jax.ShapeDtypeStruct((G_MIX_ROWS, d), F32), jax.ShapeDtypeStruct((2, nh, bw, bw), F32),
            jax.ShapeDtypeStruct((s, d), F32), jax.ShapeDtypeStruct((G_IN_ROWS, d), F32)),
        scratch_shapes=[pltpu.VMEM((tm + SUBLANES, d), F32)] * 6 + [pltpu.VMEM((tm, d), F32)] * 3 + [pltpu.VMEM((SUBLANES, d), F32)],
        compiler_params=_cparams(("arbitrary",), 10 * _nbytes((d, d), BF16) + 3 * _nbytes((tm, 7 * d), F32) + 64 * _nbytes((tm, d), F32)),
    )(dx1, x, y, yap, ybp, hs, hs, proj, proj, mod, rows, cwf, wr, wi, win_f, wa_f, wb_f, wo_f)


def _wgrad(a, b, cols_owned, ts):
    s, k1 = a.shape
    k2 = b.shape[1]
    ns = s // ts
    if cols_owned:
        nblk, bk1, bk2 = N_DEV, k1, k2 // N_DEV
        a_spec = pl.BlockSpec((ts, bk1), lambda j, t: (t, 0))
        b_spec = pl.BlockSpec((ts, bk2), lambda j, t: (t, j))
    else:
        bk1, bk2 = min(WGRAD_ROWS, k1), k2
        nblk = k1 // bk1
        a_spec = pl.BlockSpec((ts, bk1), lambda j, t: (t, j))
        b_spec = pl.BlockSpec((ts, bk2), lambda j, t: (t, 0))

    def body(a_ref, b_ref, o_ref, acc_ref):
        t = pl.program_id(1)

        @pl.when(t == 0)
        def _():
            acc_ref[...] = jnp.zeros(acc_ref.shape, F32)

        acc_ref[...] += _dot_tn(a_ref[...], b_ref[...])

        @pl.when(t == ns - 1)
        def _():
            o_ref[...] = acc_ref[...].astype(BF16)

    out = pl.pallas_call(
        body,
        name="wgrad",
        grid=(nblk, ns),
        in_specs=[a_spec, b_spec],
        out_specs=pl.BlockSpec((None, bk1, bk2), lambda j, t: (j, 0, 0)),
        out_shape=pltpu.HBM((nblk, bk1, bk2), BF16),
        scratch_shapes=[pltpu.VMEM((bk1, bk2), F32)],
        compiler_params=_cparams(("parallel", "arbitrary"), 4 * _nbytes((bk1, bk2), F32) + 4 * _nbytes((ts, bk1 + bk2), BF16)),
    )(pltpu.with_memory_space_constraint(a, pltpu.HBM), pltpu.with_memory_space_constraint(b, pltpu.HBM))
    return out if cols_owned else out.reshape(N_DEV, k1 // N_DEV, k2)


def _adam_update(w, g, m, v):
    m = ADAM_B1 * m + (1.0 - ADAM_B1) * g
    v = ADAM_B2 * v + (1.0 - ADAM_B2) * (g * g)
    m_hat = m / (1.0 - ADAM_B1 ** ADAM_STEP)
    v_hat = v / (1.0 - ADAM_B2 ** ADAM_STEP)
    delta = -ADAM_LR * (m_hat / (jnp.sqrt(v_hat) + ADAM_EPS) + ADAM_WD * w)
    return delta, m, v


def _sum_adamw(recv, w, m, v, tr):
    nl, ra, cb = w.shape
    assert nl == len(recv) == 2

    def body(r0_ref, r1_ref, w_ref, m_ref, v_ref, g_ref, d_ref, nm_ref, nv_ref):
        def total(r_ref):
            g = r_ref[0].astype(F32)
            for j in range(1, N_DEV):
                g = g + r_ref[j].astype(F32)
            return g

        g = jnp.where(pl.program_id(0) == 0, total(r0_ref), total(r1_ref))
        g_ref[...] = g
        d_ref[...], nm_ref[...], nv_ref[...] = _adam_update(w_ref[...], g, m_ref[...], v_ref[...])

    blk = pl.BlockSpec((None, tr, cb), lambda l, i: (l, i, 0))
    return pl.pallas_call(
        body,
        name="sum_adamw",
        grid=(nl, ra // tr),
        in_specs=[pl.BlockSpec((N_DEV, tr, cb), lambda l, i: (0, i * (1 - l), 0)),
                  pl.BlockSpec((N_DEV, tr, cb), lambda l, i: (0, i * l, 0)), blk, blk, blk],
        out_specs=(blk,) * 4,
        out_shape=(jax.ShapeDtypeStruct((nl, ra, cb), F32),) * 4,
        compiler_params=_cparams(("arbitrary", "arbitrary")),
    )(recv[0], recv[1], w, m, v)


def _adamw(w, g, m, v):
    def body(w_ref, g_ref, m_ref, v_ref, d_ref, nm_ref, nv_ref):
        d_ref[...], nm_ref[...], nv_ref[...] = _adam_update(w_ref[...], g_ref[...], m_ref[...], v_ref[...])

    return pl.pallas_call(
        body,
        name="adamw",
        in_specs=[VMEM_SPEC] * 4,
        out_specs=(VMEM_SPEC,) * 3,
        out_shape=(jax.ShapeDtypeStruct(w.shape, F32),) * 3,
        compiler_params=_cparams(vmem=10 * _nbytes(w.shape, F32)),
    )(w, g, m, v)


def _adamw_tiled(w, g, m, v, tr):
    nl, ra, cb = w.shape

    def body(w_ref, g_ref, m_ref, v_ref, d_ref, nm_ref, nv_ref):
        d_ref[...], nm_ref[...], nv_ref[...] = _adam_update(w_ref[...], g_ref[...], m_ref[...], v_ref[...])

    blk = pl.BlockSpec((None, tr, cb), lambda l, i: (l, i, 0))
    return pl.pallas_call(
        body,
        name="adamw_tiled",
        grid=(nl, ra // tr),
        in_specs=[blk] * 4,
        out_specs=(blk,) * 3,
        out_shape=(jax.ShapeDtypeStruct((nl, ra, cb), F32),) * 3,
        compiler_params=_cparams(("parallel", "parallel")),
    )(w, g, m, v)


def _token_tile(s):
    return min(256, max(SUBLANES * 2, s // 4))


def kernel(x, c, w_mod, b_mod, g_pre_mix, g_post_mix, w_in, conv_a_w, conv_a_b, w_a_out, conv_b_w, conv_b_b, w_gate_r, b_gate_r, w_gate_i, b_gate_i, lru_lambda, w_b_out, w_o, g_pre_mlp, g_post_mlp, w_mlp_up, w_mlp_down, loss_target, m_w_mod, m_b_mod, m_g_pre_mix, m_g_post_mix, m_w_in, m_conv_a_w, m_conv_a_b, m_w_a_out, m_conv_b_w, m_conv_b_b, m_w_gate_r, m_b_gate_r, m_w_gate_i, m_b_gate_i, m_lru_lambda, m_w_b_out, m_w_o, m_g_pre_mlp, m_g_post_mlp, m_w_mlp_up, m_w_mlp_down, v_w_mod, v_b_mod, v_g_pre_mix, v_g_post_mix, v_w_in, v_conv_a_w, v_conv_a_b, v_w_a_out, v_conv_b_w, v_conv_b_b, v_w_gate_r, v_b_gate_r, v_w_gate_i, v_b_gate_i, v_lru_lambda, v_w_b_out, v_w_o, v_g_pre_mlp, v_g_post_mlp, v_w_mlp_up, v_w_mlp_down):
    nl = w_mod.shape[0]
    s, d = x.shape[1], x.shape[2]
    nh, bw = w_gate_r.shape[1], w_gate_r.shape[2]
    cwid = conv_a_w.shape[2]
    tm = _token_tile(s)
    ts = s
    _, me = _my_position()
    xs = x.reshape(s, d)
    target = loss_target.reshape(s, d)

    vec_names = (g_pre_mix, g_post_mix, conv_a_b, conv_b_b, b_gate_r, b_gate_i, lru_lambda, g_pre_mlp, g_post_mlp)
    rows = jnp.concatenate([jnp.stack(vec_names, axis=1), jnp.zeros((nl, N_ROWS - len(vec_names), d), F32)], axis=1)
    cw = jnp.concatenate([conv_a_w, conv_b_w, jnp.zeros((nl, CW_ROWS - 7, cwid), F32)], axis=1)

    w16 = {"w_in": w_in.astype(BF16), "w_a_out": w_a_out.astype(BF16), "w_b_out": w_b_out.astype(BF16), "w_o": w_o.astype(BF16),
           "w_mlp_up": w_mlp_up.astype(BF16), "w_mlp_down": w_mlp_down.astype(BF16)}
    groups = (("in", ("w_in",)), ("mix", ("w_a_out", "w_b_out", "w_o")), ("mlp", ("w_mlp_up", "w_mlp_down")))
    mod, cact, cwf = _prep_small(c, w_mod, b_mod, cw)
    mod = mod.reshape(nl, N_MOD, d)
    w16, _ = lax.optimization_barrier((w16, mod))
    gathers = {}
    token = jnp.zeros((), F32)
    for l in range(nl):
        for gname, members in groups:
            gathers[l, gname] = _exchange_start([w16[n][l] for n in members], True, f"gather_start_{gname}{l}")
            token = token + gathers[l, gname][4][0, 0]
    rows = rows + token
    wr = w_gate_r.astype(BF16)
    wi = w_gate_i.astype(BF16)

    saved = []
    weights = []
    xin = xs
    for l in range(nl):
        after = mod if l == 0 else xin
        (win_f,) = _exchange_wait(gathers[l, "in"], after, True, f"gather_wait_in{l}")
        wa_f, wb_f, wo_f = (w.reshape(d, d) for w in _exchange_wait(gathers[l, "mix"], after, True, f"gather_wait_mix{l}"))
        proj, h, x1, hs, yap, ybp, y = _mixer_fwd(xin, mod, rows, cwf, wr, wi, win_f, wa_f, wb_f, wo_f, l, tm // 2)
        wup_f, wdn_f = _exchange_wait(gathers[l, "mlp"], x1, True, f"gather_wait_mlp{l}")
        wdn_f = wdn_f.reshape(-1, d)
        x2, ru, y2, h2 = _mlp_fwd(x1, mod, rows, wup_f, wdn_f, l, tm)
        saved.append((xin, proj, h, x1, hs, yap, ybp, y, ru, y2, h2))
        weights.append((win_f, wa_f, wb_f, wo_f, wup_f, wdn_f))
        xin = x2
    loss_part, dx = _loss_fwd_bwd(xin, target, tm)
    loss = lax.psum(loss_part, ("x", "y", "c"))

    scatters = {}
    small = [None] * nl
    gate_parts = [None] * nl

    def scatter(l, gname, parts, rows):
        scatters[l, gname] = _exchange_start(parts, False, f"scatter_start_{gname}{l}")
        return rows + scatters[l, gname][4][0, 0]

    for l in reversed(range(nl)):
        xin, proj, h, x1, hs, yap, ybp, y, ru, y2, h2 = saved[l]
        win_f, wa_f, wb_f, wo_f, wup_f, wdn_f = weights[l]
        dx1, dy2, dup, act, sm_mlp = _mlp_bwd(dx, x1, y2, ru, mod, rows, wup_f, wdn_f, l, tm)
        rows = scatter(l, "mlp", [_wgrad(h2, dup, True, ts), _wgrad(act, dy2, False, ts)], rows)
        dproj, dy, m, dyap, dybp, ya, yb, sm_mix, dwg, dx, sm_in = _mixer_bwd(
            dx1, xin, y, yap, ybp, hs, proj, mod, rows, cwf, wr, wi, win_f, wa_f, wb_f, wo_f, l, tm // 2)
        gate_parts[l] = _exchange_start([dwg.astype(BF16)], True, f"gates_start{l}")
        rows = rows + gate_parts[l][4][0, 0]
        rows = scatter(l, "in", [_wgrad(h, dproj, True, ts)], rows)
        rows = scatter(l, "mix", [_wgrad(ya, dyap, False, ts), _wgrad(yb, dybp, False, ts), _wgrad(m, dy, False, ts)], rows)
        small[l] = jnp.concatenate([sm_mlp, sm_mix, sm_in], axis=0)
    grad_x = dx.reshape(x.shape)

    recv = {}
    big = {}
    moments = {"w_in": (w_in, m_w_in, v_w_in), "w_mlp_up": (w_mlp_up, m_w_mlp_up, v_w_mlp_up), "w_a_out": (w_a_out, m_w_a_out, v_w_a_out),
               "w_b_out": (w_b_out, m_w_b_out, v_w_b_out), "w_o": (w_o, m_w_o, v_w_o), "w_mlp_down": (w_mlp_down, m_w_mlp_down, v_w_mlp_down)}

    def collect(l, gname, after):
        for n, zone in zip(dict(groups)[gname], _exchange_wait(scatters[l, gname], after, False, f"scatter_wait_{gname}{l}")):
            recv[n, l] = zone

    def update(name):
        w, m_, v_ = moments[name]
        big[name] = _sum_adamw([recv[name, l] for l in range(nl)], w, m_, v_, min(128, w.shape[1]))

    for l, gname in ((1, "mlp"), (1, "in"), (1, "mix"), (0, "mlp")):
        collect(l, gname, dx)
    update("w_mlp_up")
    update("w_mlp_down")

    lrows = jnp.concatenate(small, axis=0)
    lrows, _ = lax.optimization_barrier((lrows, big["w_mlp_down"][1]))

    def lrow(a, l, r):
        return a[l * G_LAYER_ROWS + r]

    dm = jnp.stack([jnp.concatenate([lrow(lrows, l, G_MLP_ROWS + G_MIX_ROWS + G_IN_SH), lrow(lrows, l, G_MLP_ROWS + G_MIX_ROWS + G_IN_SC),
                                     lrow(lrows, l, G_MLP_ROWS + G_MIX_GT), lrow(lrows, l, G_MLP_SH), lrow(lrows, l, G_MLP_SC),
                                     lrow(lrows, l, G_MLP_GT)]) for l in range(nl)])
    dm8 = jnp.concatenate([dm[:, None, :], jnp.zeros((nl, SUBLANES - 1, N_MOD * d), F32)], axis=1)
    srows, g_w_mod = _reduce_small(lrows, dm8, cact)
    sgates = _sum_gathered([_exchange_wait(gate_parts[l], srows, True, f"gates_wait{l}")[0] for l in range(nl)])
    for gname in ("in", "mix"):
        collect(0, gname, srows)
    for name in ("w_in", "w_a_out", "w_b_out", "w_o"):
        update(name)

    def srow(l, r):
        return lrow(srows, l, r)

    def per_layer(r):
        return jnp.stack([srow(l, r) for l in range(nl)])

    mix0 = G_MLP_ROWS
    in0 = G_MLP_ROWS + G_MIX_ROWS
    g_b_mod = jnp.stack([jnp.concatenate([srow(l, in0 + G_IN_SH), srow(l, in0 + G_IN_SC), srow(l, mix0 + G_MIX_GT),
                                          srow(l, G_MLP_SH), srow(l, G_MLP_SC), srow(l, G_MLP_GT)]) for l in range(nl)])
    conv_a_full = jnp.stack([jnp.stack([srow(l, mix0 + G_MIX_CAW + k) for k in range(3)]) for l in range(nl)])
    conv_b_full = jnp.stack([jnp.stack([srow(l, mix0 + G_MIX_CBW + k) for k in range(4)]) for l in range(nl)])
    grads = {
        "b_mod": g_b_mod,
        "g_pre_mix": per_layer(in0 + G_IN_GPRE),
        "g_post_mix": per_layer(mix0 + G_MIX_GPOST),
        "conv_a_w": lax.dynamic_slice_in_dim(conv_a_full, me * cwid, cwid, axis=2),
        "conv_a_b": per_layer(mix0 + G_MIX_CAB),
        "conv_b_w": lax.dynamic_slice_in_dim(conv_b_full, me * cwid, cwid, axis=2),
        "conv_b_b": per_layer(mix0 + G_MIX_CBB),
        "w_gate_r": sgates[:, 0],
        "b_gate_r": per_layer(mix0 + G_MIX_BR),
        "w_gate_i": sgates[:, 1],
        "b_gate_i": per_layer(mix0 + G_MIX_BI),
        "lru_lambda": per_layer(mix0 + G_MIX_LAM),
        "g_pre_mlp": per_layer(G_MLP_GPRE),
        "g_post_mlp": per_layer(G_MLP_GPOST),
    }
    params = {
        "b_mod": (b_mod, m_b_mod, v_b_mod), "g_pre_mix": (g_pre_mix, m_g_pre_mix, v_g_pre_mix), "g_post_mix": (g_post_mix, m_g_post_mix, v_g_post_mix),
        "conv_a_w": (conv_a_w, m_conv_a_w, v_conv_a_w), "conv_a_b": (conv_a_b, m_conv_a_b, v_conv_a_b),
        "conv_b_w": (conv_b_w, m_conv_b_w, v_conv_b_w), "conv_b_b": (conv_b_b, m_conv_b_b, v_conv_b_b),
        "w_gate_r": (w_gate_r, m_w_gate_r, v_w_gate_r), "b_gate_r": (b_gate_r, m_b_gate_r, v_b_gate_r),
        "w_gate_i": (w_gate_i, m_w_gate_i, v_w_gate_i), "b_gate_i": (b_gate_i, m_b_gate_i, v_b_gate_i),
        "lru_lambda": (lru_lambda, m_lru_lambda, v_lru_lambda), "g_pre_mlp": (g_pre_mlp, m_g_pre_mlp, v_g_pre_mlp),
        "g_post_mlp": (g_post_mlp, m_g_post_mlp, v_g_post_mlp),
    }
    out = {}
    for name, g in grads.items():
        w, m_, v_ = params[name]
        flat = (-1, w.shape[-1])
        dl, nm, nv = _adamw(w.reshape(flat), g.reshape(flat), m_.reshape(flat), v_.reshape(flat))
        out[name] = (g.reshape(w.shape), dl.reshape(w.shape), nm.reshape(w.shape), nv.reshape(w.shape))
    out["w_mod"] = (g_w_mod,) + tuple(_adamw_tiled(w_mod, g_w_mod, m_w_mod, v_w_mod, min(128, d)))
    out.update(big)

    order = ("w_mod", "b_mod", "g_pre_mix", "g_post_mix", "w_in", "conv_a_w", "conv_a_b", "w_a_out", "conv_b_w", "conv_b_b", "w_gate_r", "b_gate_r",
             "w_gate_i", "b_gate_i", "lru_lambda", "w_b_out", "w_o", "g_pre_mlp", "g_post_mlp", "w_mlp_up", "w_mlp_down")
    return (loss, grad_x) + tuple(out[n][0] for n in order) + tuple(out[n][1] for n in order) + tuple(out[n][2] for n in order) + tuple(out[n][3] for n in order)
```

```python
import functools

import jax
import jax.numpy as jnp
from jax import lax
from jax.experimental import pallas as pl
from jax.experimental.pallas import tpu as pltpu

F32, BF16 = jnp.float32, jnp.bfloat16
EPS = 1e-6
LRU_C = 8.0
N_DEV = 8
N_MOD = 6
SUBLANES = 8
VMEM_BUDGET = 56 * 1024 * 1024
WGRAD_ROWS = 512
ADAM_LR, ADAM_B1, ADAM_B2, ADAM_EPS, ADAM_WD, ADAM_STEP = 0.001, 0.9, 0.999, 1e-08, 0.01, 10
MESH = pl.DeviceIdType.MESH
VMEM_SPEC = pl.BlockSpec(memory_space=pltpu.VMEM)
ANY_SPEC = pl.BlockSpec(memory_space=pl.ANY)
HBM_SPEC = pl.BlockSpec(memory_space=pltpu.HBM)
SEM_SPEC = pl.BlockSpec(memory_space=pltpu.SEMAPHORE)
SIDE_EFFECT = pltpu.SideEffectType.DATAFLOW_SIDE_EFFECTING

R_G_PRE_MIX, R_G_POST_MIX, R_CONV_A_B, R_CONV_B_B, R_B_GATE_R, R_B_GATE_I, R_LAMBDA, R_G_PRE_MLP, R_G_POST_MLP = range(9)
N_ROWS = 16
M_SH_M, M_SC_M, M_GT_M, M_SH_F, M_SC_F, M_GT_F = range(6)
CW_A, CW_B, CW_ROWS = 0, 3, 8
G_MLP_GT, G_MLP_GPOST, G_MLP_SC, G_MLP_SH, G_MLP_GPRE, G_MLP_ROWS = 0, 1, 2, 3, 4, 8
(G_MIX_GT, G_MIX_GPOST, G_MIX_CAB, G_MIX_CAW, G_MIX_CBB, G_MIX_CBW, G_MIX_BR, G_MIX_BI, G_MIX_LAM) = 0, 1, 2, 3, 6, 7, 11, 12, 13
G_MIX_ROWS = 16
G_IN_SC, G_IN_SH, G_IN_GPRE, G_IN_ROWS = 0, 1, 2, 8
G_LAYER_ROWS = G_MLP_ROWS + G_MIX_ROWS + G_IN_ROWS


def _cparams(dims=None, vmem=None):
    kw = {}
    if dims is not None:
        kw["dimension_semantics"] = dims
    if vmem is not None:
        kw["vmem_limit_bytes"] = int(min(max(vmem, 16 * 1024 * 1024), VMEM_BUDGET))
    return pltpu.CompilerParams(**kw)


def _nbytes(shape, dtype):
    n = 1
    for s in shape:
        n *= s
    return n * jnp.dtype(dtype).itemsize


def _resident(block, index_map):
    return pl.BlockSpec(block, index_map, pipeline_mode=pl.Buffered(1))


def _my_position():
    x, y, c = lax.axis_index("x"), lax.axis_index("y"), lax.axis_index("c")
    return (x, y, c), 4 * x + 2 * y + c


def _peer(pos, k):
    x, y, c = pos
    px = 1 - x if k & 4 else x
    py = 1 - y if k & 2 else y
    pc = 1 - c if k & 1 else c
    return (px, py, pc), 4 * px + 2 * py + pc


def _remote(src, dst, ssem, rsem, peer):
    return pltpu.make_async_remote_copy(src_ref=src, dst_ref=dst, send_sem=ssem, recv_sem=rsem, device_id=peer, device_id_type=MESH)


def _dot(a, b):
    return jnp.dot(a, b, preferred_element_type=F32)


def _dot_nt(a, b):
    return lax.dot_general(a, b, (((1,), (1,)), ((), ())), preferred_element_type=F32)


def _dot_tn(a, b):
    return lax.dot_general(a, b, (((0,), (0,)), ((), ())), preferred_element_type=F32)


def _colsum(v):
    return jnp.sum(v, axis=0, keepdims=True)


def _sigmoid(v):
    return jax.nn.sigmoid(v)


def _gelu(v):
    k = 0.7978845608028654
    t = jnp.tanh(k * (v + 0.044715 * (v * v * v)))
    return 0.5 * v * (1.0 + t), t


def _gelu_grad(v, t):
    k = 0.7978845608028654
    return 0.5 * (1.0 + t) + 0.5 * v * (1.0 - t * t) * (k * (1.0 + 3.0 * 0.044715 * v * v))


def _expm1(v):
    u = jnp.exp(v)
    um1 = u - 1.0
    q = um1 * v / jnp.log(u)
    return jnp.where(um1 == 0.0, v, jnp.where(um1 == -1.0, -1.0, q))


def _softplus_neg(lam):
    z = -lam
    u = jnp.exp(-jnp.abs(z))
    w = 1.0 + u
    l1p = jnp.where(w == 1.0, u, jnp.log(w) * u / (w - 1.0))
    return jnp.maximum(z, 0.0) + l1p


def _rms(v):
    return lax.rsqrt(jnp.mean(v * v, axis=-1, keepdims=True) + EPS)


def _prenorm_bwd(xv, dh, g, sc):
    r = _rms(xv)
    xn = xv * r
    n = xn * g
    dsc = _colsum(dh * n)
    dsh = _colsum(dh)
    dn = dh * (1.0 + sc)
    dg = _colsum(dn * xn)
    dxn = dn * g
    dx = r * (dxn - xn * jnp.mean(dxn * xn, axis=-1, keepdims=True))
    return dx, dsc, dsh, dg


def _postnorm_bwd(yv, dout, g, gt):
    r = _rms(yv)
    yn = yv * r
    dgt = _colsum(dout * (yn * g))
    dn = dout * gt
    dg = _colsum(dn * yn)
    dyn = dn * g
    dy = r * (dyn - yn * jnp.mean(dyn * yn, axis=-1, keepdims=True))
    return dy, dgt, dg


def _gates(xc, wr_ref, wi_ref, b_r, b_i, sp, nh, bw):
    xcb = xc.astype(BF16)
    zr = jnp.concatenate([_dot(xcb[:, h * bw:(h + 1) * bw], wr_ref[h]) for h in range(nh)], axis=1) + b_r
    zi = jnp.concatenate([_dot(xcb[:, h * bw:(h + 1) * bw], wi_ref[h]) for h in range(nh)], axis=1) + b_i
    r = _sigmoid(zr)
    ig = _sigmoid(zi)
    la = (-LRU_C * r) * sp
    a = jnp.exp(la)
    mult = jnp.sqrt(-_expm1(2.0 * la))
    return xcb, r, ig, a, mult


def _scan_block(a8, b8, reverse):
    row = lax.broadcasted_iota(jnp.int32, a8.shape, 0)
    for s in (1, 2, 4):
        if reverse:
            keep = row < SUBLANES - s
            a_sh = pltpu.roll(a8, SUBLANES - s, 0)
            b_sh = pltpu.roll(b8, SUBLANES - s, 0)
        else:
            keep = row >= s
            a_sh = pltpu.roll(a8, s, 0)
            b_sh = pltpu.roll(b8, s, 0)
        b8 = b8 + a8 * jnp.where(keep, b_sh, 0.0)
        a8 = a8 * jnp.where(keep, a_sh, 1.0)
    return a8, b8


def _prep_small(c, w_mod, b_mod, cw):
    d = c.shape[1]
    cm = w_mod.shape[2]
    cwid = cw.shape[2]
    nl = w_mod.shape[0]

    def body(c_ref, wm_ref, bm_ref, cw_ref, mod_ref, cact_ref, cwf_ref, cbuf, pbuf, rbuf, ssem, rsem, lsem):
        pos, me = _my_position()
        me8 = pl.multiple_of(me * SUBLANES, SUBLANES)
        cbuf[pl.ds(me8, SUBLANES), :] = jnp.broadcast_to(c_ref[...], (SUBLANES, d))
        own_cw = pltpu.make_async_copy(cw_ref, cwf_ref.at[:, :, pl.ds(me * cwid, cwid)], lsem.at[0])
        own_cw.start()
        first = []
        for k in range(1, N_DEV):
            peer, _ = _peer(pos, k)
            rows = cbuf.at[pl.ds(me8, SUBLANES), :]
            first.append(_remote(rows, rows, ssem.at[0, k - 1], rsem.at[0, k - 1], peer))
            first.append(_remote(cw_ref, cwf_ref.at[:, :, pl.ds(me * cwid, cwid)], ssem.at[1, k - 1], rsem.at[1, k - 1], peer))
        for cp in first:
            cp.start()
        for k in range(1, N_DEV):
            peer, pj = _peer(pos, k)
            pj8 = pl.multiple_of(pj * SUBLANES, SUBLANES)
            rows = cbuf.at[pl.ds(pj8, SUBLANES), :]
            _remote(rows, rows, ssem.at[0, k - 1], rsem.at[0, k - 1], peer).wait_recv()
        cv = cbuf[...]
        cact = cv * _sigmoid(cv)
        cact_ref[...] = cact
        cb = cact.astype(BF16)
        for l in range(nl):
            pbuf[l] = _dot(cb, wm_ref[l].astype(BF16))
        own_p = pltpu.make_async_copy(pbuf.at[:, pl.ds(me8, SUBLANES), :], rbuf.at[me], lsem.at[1])
        own_p.start()
        second = []
        for k in range(1, N_DEV):
            peer, pj = _peer(pos, k)
            pj8 = pl.multiple_of(pj * SUBLANES, SUBLANES)
            second.append(_remote(pbuf.at[:, pl.ds(pj8, SUBLANES), :], rbuf.at[me], ssem.at[2, k - 1], rsem.at[2, k - 1], peer))
        for cp in second:
            cp.start()
        for k in range(1, N_DEV):
            peer, pj = _peer(pos, k)
            _remote(pbuf.at[:, pl.ds(0, SUBLANES), :], rbuf.at[pj], ssem.at[2, k - 1], rsem.at[2, k - 1], peer).wait_recv()
            _remote(cw_ref, cwf_ref.at[:, :, pl.ds(pj * cwid, cwid)], ssem.at[1, k - 1], rsem.at[1, k - 1], peer).wait_recv()
        own_p.wait()
        own_cw.wait()
        for l in range(nl):
            for j in range(N_DEV):
                mod_ref[l:l + 1, j * cm:(j + 1) * cm] = rbuf[j, l, 0:1, :] + bm_ref[l:l + 1, j * cm:(j + 1) * cm]
        for cp in first + second:
            cp.wait_send()

    return pl.pallas_call(
        body,
        name="prep_small",
        out_shape=(
            jax.ShapeDtypeStruct((nl, N_MOD * d), F32),
            jax.ShapeDtypeStruct((N_DEV * SUBLANES, d), F32),
            jax.ShapeDtypeStruct((nl, CW_ROWS, d), F32),
        ),
        in_specs=[VMEM_SPEC] * 4,
        out_specs=(VMEM_SPEC,) * 3,
        scratch_shapes=[
            pltpu.VMEM((N_DEV * SUBLANES, d), F32),
            pltpu.VMEM((nl, N_DEV * SUBLANES, cm), F32),
            pltpu.VMEM((N_DEV, nl, SUBLANES, cm), F32),
            pltpu.SemaphoreType.DMA((3, N_DEV - 1)),
            pltpu.SemaphoreType.DMA((3, N_DEV - 1)),
            pltpu.SemaphoreType.DMA((2,)),
        ],
        compiler_params=_cparams(vmem=3 * _nbytes(w_mod.shape, F32)),
    )(c, w_mod, b_mod, cw)


def _exchange_start(parts, gather, name):
    n = len(parts)
    lands = [lax.empty(((N_DEV,) + tuple(p.shape)) if gather else tuple(p.shape), p.dtype) for p in parts]

    def body(*refs):
        ins, lnd = refs[:n], refs[n:2 * n]
        ssem, rsem, token = refs[2 * n], refs[2 * n + 1], refs[-1]
        pos, me = _my_position()
        for k in range(1, N_DEV):
            peer, pj = _peer(pos, k)
            for t in range(n):
                src = ins[t] if gather else ins[t].at[pj]
                q = t * (N_DEV - 1) + k - 1
                _remote(src, lnd[t].at[me], ssem.at[q], rsem.at[q], peer).start()
        token[...] = jnp.zeros(token.shape, F32)

    out = pl.pallas_call(
        body,
        name=name,
        out_shape=(pltpu.SemaphoreType.DMA((n * (N_DEV - 1),)), pltpu.SemaphoreType.DMA((n * (N_DEV - 1),)))
        + tuple(pltpu.HBM(p.shape, p.dtype) for p in parts) + tuple(pltpu.HBM(p.shape, p.dtype) for p in lands)
        + (jax.ShapeDtypeStruct((SUBLANES, 128), F32),),
        in_specs=[HBM_SPEC] * (2 * n),
        out_specs=(SEM_SPEC, SEM_SPEC) + (HBM_SPEC,) * (2 * n) + (VMEM_SPEC,),
        input_output_aliases={i: 2 + i for i in range(2 * n)},
        compiler_params=pltpu.CompilerParams(has_side_effects=SIDE_EFFECT),
    )(*[pltpu.with_memory_space_constraint(p, pltpu.HBM) for p in list(parts) + lands])
    return out[0], out[1], list(out[2:2 + n]), list(out[2 + n:2 + 2 * n]), out[-1]


def _exchange_wait(started, after, gather, name):
    ssem, rsem, parts, lands, _ = started
    n = len(parts)

    def body(*refs):
        ins, lnd = refs[:n], refs[n:2 * n]
        ssem_ref, rsem_ref = refs[2 * n], refs[2 * n + 1]
        stage, lsem = refs[-1 - n:-1], refs[-1]
        pos, me = _my_position()
        load = []
        for t in range(n):
            src = ins[t] if gather else ins[t].at[me]
            load.append(pltpu.make_async_copy(src, stage[t], lsem.at[t]))
            load[-1].start()
        store = []
        for t in range(n):
            load[t].wait()
            store.append(pltpu.make_async_copy(stage[t], lnd[t].at[me], lsem.at[t]))
            store[-1].start()
        for k in range(1, N_DEV):
            peer, pj = _peer(pos, k)
            for t in range(n):
                src = ins[t] if gather else ins[t].at[pj]
                q = t * (N_DEV - 1) + k - 1
                _remote(src, lnd[t].at[me], ssem_ref.at[q], rsem_ref.at[q], peer).wait_send()
                _remote(src, lnd[t].at[pj], ssem_ref.at[q], rsem_ref.at[q], peer).wait_recv()
        for cp in store:
            cp.wait()

    out = pl.pallas_call(
        body,
        name=name,
        out_shape=tuple(pltpu.HBM(p.shape, p.dtype) for p in parts) + tuple(pltpu.HBM(p.shape, p.dtype) for p in lands),
        in_specs=[HBM_SPEC] * (2 * n) + [SEM_SPEC, SEM_SPEC, ANY_SPEC],
        out_specs=(HBM_SPEC,) * (2 * n),
        input_output_aliases={i: i for i in range(2 * n)},
        scratch_shapes=[pltpu.VMEM(tuple(z.shape[1:]), z.dtype) for z in lands] + [pltpu.SemaphoreType.DMA((n,))],
        compiler_params=pltpu.CompilerParams(has_side_effects=SIDE_EFFECT),
    )(*parts, *lands, ssem, rsem, after)
    return list(out[n:])


def _reduce_small(rows, dm8, cact, after):
    r, d = rows.shape
    nl = dm8.shape[0]
    cm = dm8.shape[2] // N_DEV

    def body(rows_ref, dm_ref, cact_ref, after_ref, orow_ref, owm_ref, gr, dmr, ssem, rsem, lsem):
        pos, me = _my_position()
        me8 = pl.multiple_of(me * SUBLANES, SUBLANES)
        gr[me] = rows_ref[...]
        own_dm = pltpu.make_async_copy(dm_ref.at[:, :, pl.ds(me * cm, cm)], dmr.at[:, pl.ds(me8, SUBLANES), :], lsem.at[0])
        own_dm.start()
        sends = []
        for k in range(1, N_DEV):
            peer, pj = _peer(pos, k)
            sends.append(_remote(gr.at[me], gr.at[me], ssem.at[0, k - 1], rsem.at[0, k - 1], peer))
            sends.append(_remote(dm_ref.at[:, :, pl.ds(pj * cm, cm)], dmr.at[:, pl.ds(me8, SUBLANES), :],
                                 ssem.at[1, k - 1], rsem.at[1, k - 1], peer))
        for cp in sends:
            cp.start()
        for k in range(1, N_DEV):
            peer, pj = _peer(pos, k)
            pj8 = pl.multiple_of(pj * SUBLANES, SUBLANES)
            _remote(gr.at[pj], gr.at[pj], ssem.at[0, k - 1], rsem.at[0, k - 1], peer).wait_recv()
            _remote(dm_ref.at[:, :, pl.ds(0, cm)], dmr.at[:, pl.ds(pj8, SUBLANES), :], ssem.at[1, k - 1], rsem.at[1, k - 1], peer).wait_recv()
        own_dm.wait()
        acc = gr[0]
        for j in range(1, N_DEV):
            acc = acc + gr[j]
        orow_ref[...] = acc
        cb = cact_ref[...].astype(BF16)
        for l in range(nl):
            owm_ref[l] = _dot_tn(cb, dmr[l].astype(BF16))
        for cp in sends:
            cp.wait_send()

    return pl.pallas_call(
        body,
        name="reduce_small",
        out_shape=(jax.ShapeDtypeStruct((r, d), F32), jax.ShapeDtypeStruct((nl, d, cm), F32)),
        in_specs=[VMEM_SPEC] * 3 + [ANY_SPEC],
        out_specs=(VMEM_SPEC,) * 2,
        scratch_shapes=[
            pltpu.VMEM((N_DEV, r, d), F32),
            pltpu.VMEM((nl, N_DEV * SUBLANES, cm), F32),
            pltpu.SemaphoreType.DMA((2, N_DEV - 1)),
            pltpu.SemaphoreType.DMA((2, N_DEV - 1)),
            pltpu.SemaphoreType.DMA((1,)),
        ],
        compiler_params=_cparams(vmem=4 * _nbytes((N_DEV, r, d), F32) + 6 * _nbytes((nl, d, cm), F32)),
    )(rows, dm8, cact, after)


def _sum_gathered(zones):
    nl = len(zones)

    def body(*refs):
        for l in range(nl):
            acc = refs[l][0].astype(F32)
            for j in range(1, N_DEV):
                acc = acc + refs[l][j].astype(F32)
            refs[nl][l] = acc

    return pl.pallas_call(
        body,
        name="sum_gathered",
        out_shape=jax.ShapeDtypeStruct((nl,) + tuple(zones[0].shape[1:]), F32),
        in_specs=[VMEM_SPEC] * nl,
        out_specs=VMEM_SPEC,
        compiler_params=_cparams(vmem=8 * nl * _nbytes(zones[0].shape, BF16)),
    )(*zones)


def _in_proj_fwd(x, mod, rows, win_f, l, tm):
    s, d = x.shape
    nb, _, ci = win_f.shape

    def body(x_ref, mod_ref, rows_ref, w_ref, proj_ref, h_ref):
        xv = x_ref[...]
        g = rows_ref[R_G_PRE_MIX:R_G_PRE_MIX + 1, :]
        h = (xv * _rms(xv) * g) * (1.0 + mod_ref[M_SC_M:M_SC_M + 1, :]) + mod_ref[M_SH_M:M_SH_M + 1, :]
        hb = h.astype(BF16)
        h_ref[...] = hb
        for j in range(nb):
            proj_ref[:, j * ci:(j + 1) * ci] = _dot(hb, w_ref[j])

    return pl.pallas_call(
        body,
        name="in_proj_fwd",
        grid=(s // tm,),
        in_specs=[
            pl.BlockSpec((tm, d), lambda i: (i, 0)),
            _resident((None, N_MOD, d), lambda i: (l, 0, 0)),
            _resident((None, N_ROWS, d), lambda i: (l, 0, 0)),
            _resident((nb, d, ci), lambda i: (0, 0, 0)),
        ],
        out_specs=(pl.BlockSpec((tm, nb * ci), lambda i: (i, 0)), pl.BlockSpec((tm, d), lambda i: (i, 0))),
        out_shape=(jax.ShapeDtypeStruct((s, nb * ci), F32), jax.ShapeDtypeStruct((s, d), BF16)),
        compiler_params=_cparams(("parallel",), _nbytes((nb, d, ci), BF16) + 3 * _nbytes((tm, nb * ci), F32) + 8 * _nbytes((tm, d), F32)),
    )(x, mod, rows, win_f)


def _mixer_core_fwd(proj, x, mod, rows, cwf, wr, wi, wa_f, wb_f, wo_f, l, tm):
    s, d = x.shape
    nh, bw, _ = wr.shape[1:]

    def body(proj_ref, x_ref, mod_ref, rows_ref, cw_ref, wr_ref, wi_ref, wa_ref, wb_ref, wo_ref,
             x1_ref, hs_ref, yap_ref, ybp_ref, y_ref, cvbuf, xbbuf, a_s, b_s, hprev):
        i = pl.program_id(0)

        @pl.when(i == 0)
        def _():
            cvbuf[pl.ds(0, SUBLANES), :] = jnp.zeros((SUBLANES, d), F32)
            xbbuf[pl.ds(0, SUBLANES), :] = jnp.zeros((SUBLANES, d), F32)
            hprev[...] = jnp.zeros((SUBLANES, d), F32)

        def row(r):
            return rows_ref[r:r + 1, :]

        def tap(r):
            return cw_ref[r:r + 1, :]

        ba = proj_ref[:, 0:d]
        cv = proj_ref[:, d:2 * d] * proj_ref[:, 2 * d:3 * d]
        cvbuf[pl.ds(SUBLANES, tm), :] = cv
        conv3 = ((row(R_CONV_A_B) + cvbuf[pl.ds(SUBLANES - 2, tm), :] * tap(CW_A)) + cvbuf[pl.ds(SUBLANES - 1, tm), :] * tap(CW_A + 1)) + cv * tap(CW_A + 2)
        ya = ba * conv3
        cvbuf[pl.ds(0, SUBLANES), :] = cvbuf[pl.ds(tm, SUBLANES), :]
        xb = proj_ref[:, 3 * d:4 * d]
        xbbuf[pl.ds(SUBLANES, tm), :] = xb
        xc = (((row(R_CONV_B_B) + xbbuf[pl.ds(SUBLANES - 3, tm), :] * tap(CW_B)) + xbbuf[pl.ds(SUBLANES - 2, tm), :] * tap(CW_B + 1))
              + xbbuf[pl.ds(SUBLANES - 1, tm), :] * tap(CW_B + 2)) + xb * tap(CW_B + 3)
        xbbuf[pl.ds(0, SUBLANES), :] = xbbuf[pl.ds(tm, SUBLANES), :]
        sp = _softplus_neg(row(R_LAMBDA))
        _, _, ig, a, mult = _gates(xc, wr_ref, wi_ref, row(R_B_GATE_R), row(R_B_GATE_I), sp, nh, bw)
        a_s[...] = a
        b_s[...] = mult * (ig * xc)

        def blk(j, hp):
            o = pl.multiple_of(j * SUBLANES, SUBLANES)
            a8, b8 = _scan_block(a_s[pl.ds(o, SUBLANES), :], b_s[pl.ds(o, SUBLANES), :], reverse=False)
            h8 = b8 + a8 * hp
            hs_ref[pl.ds(o, SUBLANES), :] = h8
            return jnp.broadcast_to(h8[SUBLANES - 1:SUBLANES, :], (SUBLANES, d))

        hprev[...] = lax.fori_loop(0, tm // SUBLANES, blk, hprev[...])
        gel, _ = _gelu(proj_ref[:, 4 * d:5 * d])
        yb = hs_ref[...] * gel
        yap = _dot(ya.astype(BF16), wa_ref[...])
        ybp = _dot(yb.astype(BF16), wb_ref[...])
        yap_ref[...] = yap
        ybp_ref[...] = ybp
        m = _sigmoid(proj_ref[:, 5 * d:6 * d]) * yap + _sigmoid(proj_ref[:, 6 * d:7 * d]) * ybp
        y = _dot(m.astype(BF16), wo_ref[...])
        y_ref[...] = y
        x1_ref[...] = x_ref[...] + mod_ref[M_GT_M:M_GT_M + 1, :] * ((y * _rms(y)) * row(R_G_POST_MIX))

    tile = pl.BlockSpec((tm, d), lambda i: (i, 0))
    return pl.pallas_call(
        body,
        name="mixer_core_fwd",
        grid=(s // tm,),
        in_specs=[
            pl.BlockSpec((tm, 7 * d), lambda i: (i, 0)),
            tile,
            _resident((None, N_MOD, d), lambda i: (l, 0, 0)),
            _resident((None, N_ROWS, d), lambda i: (l, 0, 0)),
            _resident((None, CW_ROWS, d), lambda i: (l, 0, 0)),
            _resident((None, nh, bw, bw), lambda i: (l, 0, 0, 0)),
            _resident((None, nh, bw, bw), lambda i: (l, 0, 0, 0)),
            _resident((d, d), lambda i: (0, 0)),
            _resident((d, d), lambda i: (0, 0)),
            _resident((d, d), lambda i: (0, 0)),
        ],
        out_specs=(tile,) * 5,
        out_shape=(jax.ShapeDtypeStruct((s, d), F32),) * 5,
        scratch_shapes=[
            pltpu.VMEM((tm + SUBLANES, d), F32),
            pltpu.VMEM((tm + SUBLANES, d), F32),
            pltpu.VMEM((tm, d), F32),
            pltpu.VMEM((tm, d), F32),
            pltpu.VMEM((SUBLANES, d), F32),
        ],
        compiler_params=_cparams(("arbitrary",), 3 * _nbytes((d, d), BF16) + 2 * _nbytes((tm, 7 * d), F32) + 40 * _nbytes((tm, d), F32)),
    )(proj, x, mod, rows, cwf, wr, wi, wa_f, wb_f, wo_f)


def _mlp_fwd(x1, mod, rows, wup_f, wdn_f, l, tm):
    s, d = x1.shape
    nb, _, cu = wup_f.shape
    dff = nb * cu

    def body(x1_ref, mod_ref, rows_ref, wu_ref, wd_ref, x2_ref, ru_ref, y2_ref, h2_ref):
        xv = x1_ref[...]
        g = rows_ref[R_G_PRE_MLP:R_G_PRE_MLP + 1, :]
        h2 = ((xv * _rms(xv) * g) * (1.0 + mod_ref[M_SC_F:M_SC_F + 1, :]) + mod_ref[M_SH_F:M_SH_F + 1, :]).astype(BF16)
        h2_ref[...] = h2
        ru = jnp.concatenate([jnp.maximum(_dot(h2, wu_ref[j]), 0.0) for j in range(nb)], axis=1)
        ru_ref[...] = ru.astype(BF16)
        y2 = _dot((ru * ru).astype(BF16), wd_ref[...])
        y2_ref[...] = y2
        x2_ref[...] = xv + mod_ref[M_GT_F:M_GT_F + 1, :] * ((y2 * _rms(y2)) * rows_ref[R_G_POST_MLP:R_G_POST_MLP + 1, :])

    tile = pl.BlockSpec((tm, d), lambda i: (i, 0))
    wide = pl.BlockSpec((tm, dff), lambda i: (i, 0))
    return pl.pallas_call(
        body,
        name="mlp_fwd",
        grid=(s // tm,),
        in_specs=[
            tile,
            _resident((None, N_MOD, d), lambda i: (l, 0, 0)),
            _resident((None, N_ROWS, d), lambda i: (l, 0, 0)),
            _resident((nb, d, cu), lambda i: (0, 0, 0)),
            _resident((dff, d), lambda i: (0, 0)),
        ],
        out_specs=(tile, wide, tile, tile),
        out_shape=(jax.ShapeDtypeStruct((s, d), F32), jax.ShapeDtypeStruct((s, dff), BF16),
                   jax.ShapeDtypeStruct((s, d), F32), jax.ShapeDtypeStruct((s, d), BF16)),
        compiler_params=_cparams(("parallel",), 2 * _nbytes((dff, d), BF16) + 5 * _nbytes((tm, dff), F32) + 12 * _nbytes((tm, d), F32)),
    )(x1, mod, rows, wup_f, wdn_f)


def _loss_fwd_bwd(y, target, tm):
    s, d = y.shape

    def body(y_ref, t_ref, loss_ref, dy_ref):
        @pl.when(pl.program_id(0) == 0)
        def _():
            loss_ref[...] = jnp.zeros(loss_ref.shape, F32)

        e = y_ref[...] - t_ref[...]
        dy_ref[...] = e * (1.0 / d)
        loss_ref[...] += 0.5 * jnp.sum(jnp.mean(e * e, axis=-1, keepdims=True), axis=0, keepdims=True)

    tile = pl.BlockSpec((tm, d), lambda i: (i, 0))
    loss, dy = pl.pallas_call(
        body,
        name="loss",
        grid=(s // tm,),
        in_specs=[tile, tile],
        out_specs=(pl.BlockSpec((SUBLANES, 128), lambda i: (0, 0)), tile),
        out_shape=(jax.ShapeDtypeStruct((SUBLANES, 128), F32), jax.ShapeDtypeStruct((s, d), F32)),
        compiler_params=_cparams(("arbitrary",)),
    )(y, target)
    return loss[0, 0], dy


def _mlp_bwd(dx2, x1, y2, ru, mod, rows, wup_f, wdn_f, l, tm):
    s, d = x1.shape
    nb, _, cu = wup_f.shape
    dff = nb * cu

    def body(dx2_ref, x1_ref, y2_ref, ru_ref, mod_ref, rows_ref, wu_ref, wd_ref, dx1_ref, dy2_ref, dup_ref, act_ref, sm_ref):
        @pl.when(pl.program_id(0) == 0)
        def _():
            sm_ref[...] = jnp.zeros(sm_ref.shape, F32)

        dout = dx2_ref[...]
        dy2, dgt, dgpost = _postnorm_bwd(y2_ref[...], dout, rows_ref[R_G_POST_MLP:R_G_POST_MLP + 1, :], mod_ref[M_GT_F:M_GT_F + 1, :])
        dy2b = dy2.astype(BF16)
        dy2_ref[...] = dy2b
        ruv = ru_ref[...].astype(F32)
        act_ref[...] = (ruv * ruv).astype(BF16)
        dup = (_dot_nt(dy2b, wd_ref[...]) * (2.0 * ruv)).astype(BF16)
        dup_ref[...] = dup
        dh2 = _dot_nt(dup[:, 0:cu], wu_ref[0])
        for j in range(1, nb):
            dh2 = dh2 + _dot_nt(dup[:, j * cu:(j + 1) * cu], wu_ref[j])
        dxn, dsc, dsh, dgpre = _prenorm_bwd(x1_ref[...], dh2, rows_ref[R_G_PRE_MLP:R_G_PRE_MLP + 1, :], mod_ref[M_SC_F:M_SC_F + 1, :])
        dx1_ref[...] = dout + dxn
        for r, v in ((G_MLP_GT, dgt), (G_MLP_GPOST, dgpost), (G_MLP_SC, dsc), (G_MLP_SH, dsh), (G_MLP_GPRE, dgpre)):
            sm_ref[r:r + 1, :] += v

    tile = pl.BlockSpec((tm, d), lambda i: (i, 0))
    wide = pl.BlockSpec((tm, dff), lambda i: (i, 0))
    return pl.pallas_call(
        body,
        name="mlp_bwd",
        grid=(s // tm,),
        in_specs=[
            tile, tile, tile, wide,
            _resident((None, N_MOD, d), lambda i: (l, 0, 0)),
            _resident((None, N_ROWS, d), lambda i: (l, 0, 0)),
            _resident((nb, d, cu), lambda i: (0, 0, 0)),
            _resident((dff, d), lambda i: (0, 0)),
        ],
        out_specs=(tile, tile, wide, wide, pl.BlockSpec((G_MLP_ROWS, d), lambda i: (0, 0))),
        out_shape=(jax.ShapeDtypeStruct((s, d), F32), jax.ShapeDtypeStruct((s, d), BF16), jax.ShapeDtypeStruct((s, dff), BF16),
                   jax.ShapeDtypeStruct((s, dff), BF16), jax.ShapeDtypeStruct((G_MLP_ROWS, d), F32)),
        compiler_params=_cparams(("arbitrary",), 2 * _nbytes((dff, d), BF16) + 6 * _nbytes((tm, dff), F32) + 16 * _nbytes((tm, d), F32)),
    )(dx2, x1, y2, ru, mod, rows, wup_f, wdn_f)


def _mixer_core_bwd(dx1, y, yap, ybp, hs, proj, mod, rows, cwf, wr, wi, wa_f, wb_f, wo_f, l, tm):
    s, d = dx1.shape
    nh, bw, _ = wr.shape[1:]
    nt = s // tm
    per = tm // SUBLANES

    def body(dx1_ref, y_ref, yap_ref, ybp_ref, hs_ref, hsh_ref, proj_ref, projh_ref, mod_ref, rows_ref, cw_ref,
             wr_ref, wi_ref, wa_ref, wb_ref, wo_ref,
             dproj_ref, dy_ref, m_ref, dyap_ref, dybp_ref, ya_ref, yb_ref, sm_ref, dwg_ref,
             cvbuf, xbbuf, hsbuf, abuf, dcbuf, dxbuf, al_s, dh_s, lam_s, lnext):
        i = pl.program_id(0)
        first_tile = i == nt - 1

        @pl.when(i == 0)
        def _():
            sm_ref[...] = jnp.zeros(sm_ref.shape, F32)
            dwg_ref[...] = jnp.zeros(dwg_ref.shape, F32)
            zero = jnp.zeros((SUBLANES, d), F32)
            abuf[pl.ds(tm, SUBLANES), :] = zero
            dcbuf[pl.ds(tm, SUBLANES), :] = zero
            dxbuf[pl.ds(tm, SUBLANES), :] = zero
            lnext[...] = zero

        def row(r):
            return rows_ref[r:r + 1, :]

        def tap(r):
            return cw_ref[r:r + 1, :]

        def acc(r, v):
            sm_ref[r:r + 1, :] += v

        keep_halo = jnp.where(first_tile, 0.0, 1.0)
        dy, dgt, dgpost = _postnorm_bwd(y_ref[...], dx1_ref[...], row(R_G_POST_MIX), mod_ref[M_GT_M:M_GT_M + 1, :])
        acc(G_MIX_GT, dgt)
        acc(G_MIX_GPOST, dgpost)
        dyb16 = dy.astype(BF16)
        dy_ref[...] = dyb16
        dm = _dot_nt(dyb16, wo_ref[...])
        sa = _sigmoid(proj_ref[:, 5 * d:6 * d])
        sb = _sigmoid(proj_ref[:, 6 * d:7 * d])
        yap = yap_ref[...]
        ybp = ybp_ref[...]
        m_ref[...] = (sa * yap + sb * ybp).astype(BF16)
        dyap = (dm * sa).astype(BF16)
        dybp = (dm * sb).astype(BF16)
        dyap_ref[...] = dyap
        dybp_ref[...] = dybp
        dproj_ref[:, 5 * d:6 * d] = (dm * yap * sa * (1.0 - sa)).astype(BF16)
        dproj_ref[:, 6 * d:7 * d] = (dm * ybp * sb * (1.0 - sb)).astype(BF16)
        dya = _dot_nt(dyap, wa_ref[...])
        dyb = _dot_nt(dybp, wb_ref[...])
        ba = proj_ref[:, 0:d]
        ca = proj_ref[:, d:2 * d]
        va = proj_ref[:, 2 * d:3 * d]
        cv = ca * va
        cvbuf[pl.ds(0, SUBLANES), :] = keep_halo * (projh_ref[:, d:2 * d] * projh_ref[:, 2 * d:3 * d])
        cvbuf[pl.ds(SUBLANES, tm), :] = cv
        cvm2 = cvbuf[pl.ds(SUBLANES - 2, tm), :]
        cvm1 = cvbuf[pl.ds(SUBLANES - 1, tm), :]
        conv3 = ((row(R_CONV_A_B) + cvm2 * tap(CW_A)) + cvm1 * tap(CW_A + 1)) + cv * tap(CW_A + 2)
        ya_ref[...] = (ba * conv3).astype(BF16)
        dproj_ref[:, 0:d] = (dya * conv3).astype(BF16)
        dc3 = dya * ba
        acc(G_MIX_CAB, _colsum(dc3))
        acc(G_MIX_CAW, _colsum(dc3 * cvm2))
        acc(G_MIX_CAW + 1, _colsum(dc3 * cvm1))
        acc(G_MIX_CAW + 2, _colsum(dc3 * cv))
        dcbuf[pl.ds(0, tm), :] = dc3
        dcv = (dc3 * tap(CW_A + 2) + dcbuf[pl.ds(1, tm), :] * tap(CW_A + 1)) + dcbuf[pl.ds(2, tm), :] * tap(CW_A)
        dcbuf[pl.ds(tm, SUBLANES), :] = dcbuf[pl.ds(0, SUBLANES), :]
        dproj_ref[:, d:2 * d] = (dcv * va).astype(BF16)
        dproj_ref[:, 2 * d:3 * d] = (dcv * ca).astype(BF16)
        xb = proj_ref[:, 3 * d:4 * d]
        gb = proj_ref[:, 4 * d:5 * d]
        xbbuf[pl.ds(0, SUBLANES), :] = keep_halo * projh_ref[:, 3 * d:4 * d]
        xbbuf[pl.ds(SUBLANES, tm), :] = xb
        xm3 = xbbuf[pl.ds(SUBLANES - 3, tm), :]
        xm2 = xbbuf[pl.ds(SUBLANES - 2, tm), :]
        xm1 = xbbuf[pl.ds(SUBLANES - 1, tm), :]
        xc = (((row(R_CONV_B_B) + xm3 * tap(CW_B)) + xm2 * tap(CW_B + 1)) + xm1 * tap(CW_B + 2)) + xb * tap(CW_B + 3)
        lam = row(R_LAMBDA)
        sp = _softplus_neg(lam)
        xcb, r, ig, a, mult = _gates(xc, wr_ref, wi_ref, row(R_B_GATE_R), row(R_B_GATE_I), sp, nh, bw)
        gel, th = _gelu(gb)
        hs = hs_ref[...]
        yb_ref[...] = (hs * gel).astype(BF16)
        dproj_ref[:, 4 * d:5 * d] = (dyb * hs * _gelu_grad(gb, th)).astype(BF16)
        abuf[pl.ds(0, tm), :] = a
        al_s[...] = abuf[pl.ds(1, tm), :]
        abuf[pl.ds(tm, SUBLANES), :] = abuf[pl.ds(0, SUBLANES), :]
        dh_s[...] = dyb * gel

        def blk(j, ln):
            o = pl.multiple_of((per - 1 - j) * SUBLANES, SUBLANES)
            a8, b8 = _scan_block(al_s[pl.ds(o, SUBLANES), :], dh_s[pl.ds(o, SUBLANES), :], reverse=True)
            l8 = b8 + a8 * ln
            lam_s[pl.ds(o, SUBLANES), :] = l8
            return jnp.broadcast_to(l8[0:1, :], (SUBLANES, d))

        lnext[...] = lax.fori_loop(0, per, blk, lnext[...])
        dbb = lam_s[...]
        hsbuf[pl.ds(0, SUBLANES), :] = keep_halo * hsh_ref[...]
        hsbuf[pl.ds(SUBLANES, tm), :] = hs
        da = dbb * hsbuf[pl.ds(SUBLANES - 1, tm), :]
        dmult = dbb * (ig * xc)
        dig = dbb * (mult * xc)
        dxc = dbb * (mult * ig)
        dla = da * a - dmult * (a * a) / mult
        acc(G_MIX_LAM, _colsum(dla * (-LRU_C * r)) * (-_sigmoid(-lam)))
        dzr = (dla * (-LRU_C * sp)) * r * (1.0 - r)
        dzi = dig * ig * (1.0 - ig)
        acc(G_MIX_BR, _colsum(dzr))
        acc(G_MIX_BI, _colsum(dzi))
        dzrb = dzr.astype(BF16)
        dzib = dzi.astype(BF16)
        back = []
        for h in range(nh):
            sl = slice(h * bw, (h + 1) * bw)
            back.append(_dot_nt(dzrb[:, sl], wr_ref[h]) + _dot_nt(dzib[:, sl], wi_ref[h]))
            dwg_ref[0, h] += _dot_tn(xcb[:, sl], dzrb[:, sl])
            dwg_ref[1, h] += _dot_tn(xcb[:, sl], dzib[:, sl])
        dxc = dxc + jnp.concatenate(back, axis=1)
        acc(G_MIX_CBB, _colsum(dxc))
        acc(G_MIX_CBW, _colsum(dxc * xm3))
        acc(G_MIX_CBW + 1, _colsum(dxc * xm2))
        acc(G_MIX_CBW + 2, _colsum(dxc * xm1))
        acc(G_MIX_CBW + 3, _colsum(dxc * xb))
        dxbuf[pl.ds(0, tm), :] = dxc
        dxb = ((dxc * tap(CW_B + 3) + dxbuf[pl.ds(1, tm), :] * tap(CW_B + 2)) + dxbuf[pl.ds(2, tm), :] * tap(CW_B + 1)) + dxbuf[pl.ds(3, tm), :] * tap(CW_B)
        dxbuf[pl.ds(tm, SUBLANES), :] = dxbuf[pl.ds(0, SUBLANES), :]
        dproj_ref[:, 3 * d:4 * d] = dxb.astype(BF16)

    def rev(i):
        return (nt - 1 - i, 0)

    def halo(i):
        return (jnp.maximum((nt - 1 - i) * per - 1, 0), 0)

    tile = pl.BlockSpec((tm, d), rev)
    return pl.pallas_call(
        body,
        name="mixer_core_bwd",
        grid=(nt,),
        in_specs=[
            tile, tile, tile, tile, tile,
            pl.BlockSpec((SUBLANES, d), halo),
            pl.BlockSpec((tm, 7 * d), rev),
            pl.BlockSpec((SUBLANES, 7 * d), halo),
            _resident((None, N_MOD, d), lambda i: (l, 0, 0)),
            _resident((None, N_ROWS, d), lambda i: (l, 0, 0)),
            _resident((None, CW_ROWS, d), lambda i: (l, 0, 0)),
            _resident((None, nh, bw, bw), lambda i: (l, 0, 0, 0)),
            _resident((None, nh, bw, bw), lambda i: (l, 0, 0, 0)),
            _resident((d, d), lambda i: (0, 0)),
            _resident((d, d), lambda i: (0, 0)),
            _resident((d, d), lambda i: (0, 0)),
        ],
        out_specs=(pl.BlockSpec((tm, 7 * d), rev),) + (tile,) * 6 + (
            pl.BlockSpec((G_MIX_ROWS, d), lambda i: (0, 0)), pl.BlockSpec((2, nh, bw, bw), lambda i: (0, 0, 0, 0))),
        out_shape=(jax.ShapeDtypeStruct((s, 7 * d), BF16),) + (jax.ShapeDtypeStruct((s, d), BF16),) * 6 + (
            jax.ShapeDtypeStruct((G_MIX_ROWS, d), F32), jax.ShapeDtypeStruct((2, nh, bw, bw), F32)),
        scratch_shapes=[pltpu.VMEM((tm + SUBLANES, d), F32)] * 6 + [pltpu.VMEM((tm, d), F32)] * 3 + [pltpu.VMEM((SUBLANES, d), F32)],
        compiler_params=_cparams(("arbitrary",), 3 * _nbytes((d, d), BF16) + 3 * _nbytes((tm, 7 * d), F32) + 64 * _nbytes((tm, d), F32)),
    )(dx1, y, yap, ybp, hs, hs, proj, proj, mod, rows, cwf, wr, wi, wa_f, wb_f, wo_f)


def _in_proj_bwd(dproj, x, dx1, mod, rows, win_f, l, tm):
    s, d = x.shape
    nb, _, ci = win_f.shape

    def body(dp_ref, x_ref, dx1_ref, mod_ref, rows_ref, w_ref, dx_ref, sm_ref):
        @pl.when(pl.program_id(0) == 0)
        def _():
            sm_ref[...] = jnp.zeros(sm_ref.shape, F32)

        dh = _dot_nt(dp_ref[:, 0:ci], w_ref[0])
        for j in range(1, nb):
            dh = dh + _dot_nt(dp_ref[:, j * ci:(j + 1) * ci], w_ref[j])
        dxn, dsc, dsh, dg = _prenorm_bwd(x_ref[...], dh, rows_ref[R_G_PRE_MIX:R_G_PRE_MIX + 1, :], mod_ref[M_SC_M:M_SC_M + 1, :])
        dx_ref[...] = dx1_ref[...] + dxn
        for r, v in ((G_IN_SC, dsc), (G_IN_SH, dsh), (G_IN_GPRE, dg)):
            sm_ref[r:r + 1, :] += v

    tile = pl.BlockSpec((tm, d), lambda i: (i, 0))
    return pl.pallas_call(
        body,
        name="in_proj_bwd",
        grid=(s // tm,),
        in_specs=[
            pl.BlockSpec((tm, nb * ci), lambda i: (i, 0)), tile, tile,
            _resident((None, N_MOD, d), lambda i: (l, 0, 0)),
            _resident((None, N_ROWS, d), lambda i: (l, 0, 0)),
            _resident((nb, d, ci), lambda i: (0, 0, 0)),
        ],
        out_specs=(tile, pl.BlockSpec((G_IN_ROWS, d), lambda i: (0, 0))),
        out_shape=(jax.ShapeDtypeStruct((s, d), F32), jax.ShapeDtypeStruct((G_IN_ROWS, d), F32)),
        compiler_params=_cparams(("arbitrary",), _nbytes((nb, d, ci), BF16) + 2 * _nbytes((tm, nb * ci), BF16) + 16 * _nbytes((tm, d), F32)),
    )(dproj, x, dx1, mod, rows, win_f)


def _wgrad(a, b, cols_owned, ts, after):
    s, k1 = a.shape
    k2 = b.shape[1]
    ns = s // ts
    if cols_owned:
        nblk, bk1, bk2 = N_DEV, k1, k2 // N_DEV
        a_spec = pl.BlockSpec((ts, bk1), lambda j, t: (t, 0))
        b_spec = pl.BlockSpec((ts, bk2), lambda j, t: (t, j))
    else:
        bk1, bk2 = min(WGRAD_ROWS, k1), k2
        nblk = k1 // bk1
        a_spec = pl.BlockSpec((ts, bk1), lambda j, t: (t, j))
        b_spec = pl.BlockSpec((ts, bk2), lambda j, t: (t, 0))

    def body(a_ref, b_ref, after_ref, o_ref, acc_ref):
        t = pl.program_id(1)

        @pl.when(t == 0)
        def _():
            acc_ref[...] = jnp.zeros(acc_ref.shape, F32)

        acc_ref[...] += _dot_tn(a_ref[...], b_ref[...])

        @pl.when(t == ns - 1)
        def _():
            o_ref[...] = acc_ref[...].astype(BF16)

    out = pl.pallas_call(
        body,
        name="wgrad",
        grid=(nblk, ns),
        in_specs=[a_spec, b_spec, ANY_SPEC],
        out_specs=pl.BlockSpec((None, bk1, bk2), lambda j, t: (j, 0, 0)),
        out_shape=pltpu.HBM((nblk, bk1, bk2), BF16),
        scratch_shapes=[pltpu.VMEM((bk1, bk2), F32)],
        compiler_params=_cparams(("parallel", "arbitrary"), 4 * _nbytes((bk1, bk2), F32) + 4 * _nbytes((ts, bk1 + bk2), BF16)),
    )(pltpu.with_memory_space_constraint(a, pltpu.HBM), pltpu.with_memory_space_constraint(b, pltpu.HBM), after)
    return out if cols_owned else out.reshape(N_DEV, k1 // N_DEV, k2)


def _adam_update(w, g, m, v):
    m = ADAM_B1 * m + (1.0 - ADAM_B1) * g
    v = ADAM_B2 * v + (1.0 - ADAM_B2) * (g * g)
    m_hat = m / (1.0 - ADAM_B1 ** ADAM_STEP)
    v_hat = v / (1.0 - ADAM_B2 ** ADAM_STEP)
    delta = -ADAM_LR * (m_hat / (jnp.sqrt(v_hat) + ADAM_EPS) + ADAM_WD * w)
    return delta, m, v


def _sum_adamw(recv, w, m, v, tr):
    nl, ra, cb = w.shape
    assert nl == len(recv) == 2

    def body(r0_ref, r1_ref, w_ref, m_ref, v_ref, g_ref, d_ref, nm_ref, nv_ref):
        def total(r_ref):
            g = r_ref[0].astype(F32)
            for j in range(1, N_DEV):
                g = g + r_ref[j].astype(F32)
            return g

        g = jnp.where(pl.program_id(0) == 0, total(r0_ref), total(r1_ref))
        g_ref[...] = g
        d_ref[...], nm_ref[...], nv_ref[...] = _adam_update(w_ref[...], g, m_ref[...], v_ref[...])

    blk = pl.BlockSpec((None, tr, cb), lambda l, i: (l, i, 0))
    return pl.pallas_call(
        body,
        name="sum_adamw",
        grid=(nl, ra // tr),
        in_specs=[pl.BlockSpec((N_DEV, tr, cb), lambda l, i: (0, i * (1 - l), 0)),
                  pl.BlockSpec((N_DEV, tr, cb), lambda l, i: (0, i * l, 0)), blk, blk, blk],
        out_specs=(blk,) * 4,
        out_shape=(jax.ShapeDtypeStruct((nl, ra, cb), F32),) * 4,
        compiler_params=_cparams(("arbitrary", "arbitrary")),
    )(recv[0], recv[1], w, m, v)


def _adamw(w, g, m, v):
    def body(w_ref, g_ref, m_ref, v_ref, d_ref, nm_ref, nv_ref):
        d_ref[...], nm_ref[...], nv_ref[...] = _adam_update(w_ref[...], g_ref[...], m_ref[...], v_ref[...])

    return pl.pallas_call(
        body,
        name="adamw",
        in_specs=[VMEM_SPEC] * 4,
        out_specs=(VMEM_SPEC,) * 3,
        out_shape=(jax.ShapeDtypeStruct(w.shape, F32),) * 3,
        compiler_params=_cparams(vmem=10 * _nbytes(w.shape, F32)),
    )(w, g, m, v)


def _adamw_tiled(w, g, m, v, tr):
    nl, ra, cb = w.shape

    def body(w_ref, g_ref, m_ref, v_ref, d_ref, nm_ref, nv_ref):
        d_ref[...], nm_ref[...], nv_ref[...] = _adam_update(w_ref[...], g_ref[...], m_ref[...], v_ref[...])

    blk = pl.BlockSpec((None, tr, cb), lambda l, i: (l, i, 0))
    return pl.pallas_call(
        body,
        name="adamw_tiled",
        grid=(nl, ra // tr),
        in_specs=[blk] * 4,
        out_specs=(blk,) * 3,
        out_shape=(jax.ShapeDtypeStruct((nl, ra, cb), F32),) * 3,
        compiler_params=_cparams(("parallel", "parallel")),
    )(w, g, m, v)


def _token_tile(s):
    return min(256, max(SUBLANES * 2, s // 4))


def kernel(x, c, w_mod, b_mod, g_pre_mix, g_post_mix, w_in, conv_a_w, conv_a_b, w_a_out, conv_b_w, conv_b_b, w_gate_r, b_gate_r, w_gate_i, b_gate_i, lru_lambda, w_b_out, w_o, g_pre_mlp, g_post_mlp, w_mlp_up, w_mlp_down, loss_target, m_w_mod, m_b_mod, m_g_pre_mix, m_g_post_mix, m_w_in, m_conv_a_w, m_conv_a_b, m_w_a_out, m_conv_b_w, m_conv_b_b, m_w_gate_r, m_b_gate_r, m_w_gate_i, m_b_gate_i, m_lru_lambda, m_w_b_out, m_w_o, m_g_pre_mlp, m_g_post_mlp, m_w_mlp_up, m_w_mlp_down, v_w_mod, v_b_mod, v_g_pre_mix, v_g_post_mix, v_w_in, v_conv_a_w, v_conv_a_b, v_w_a_out, v_conv_b_w, v_conv_b_b, v_w_gate_r, v_b_gate_r, v_w_gate_i, v_b_gate_i, v_lru_lambda, v_w_b_out, v_w_o, v_g_pre_mlp, v_g_post_mlp, v_w_mlp_up, v_w_mlp_down):
    nl = w_mod.shape[0]
    s, d = x.shape[1], x.shape[2]
    nh, bw = w_gate_r.shape[1], w_gate_r.shape[2]
    cwid = conv_a_w.shape[2]
    tm = _token_tile(s)
    ts = s
    _, me = _my_position()
    xs = x.reshape(s, d)
    target = loss_target.reshape(s, d)

    vec_names = (g_pre_mix, g_post_mix, conv_a_b, conv_b_b, b_gate_r, b_gate_i, lru_lambda, g_pre_mlp, g_post_mlp)
    rows = jnp.concatenate([jnp.stack(vec_names, axis=1), jnp.zeros((nl, N_ROWS - len(vec_names), d), F32)], axis=1)
    cw = jnp.concatenate([conv_a_w, conv_b_w, jnp.zeros((nl, CW_ROWS - 7, cwid), F32)], axis=1)

    w16 = {"w_in": w_in.astype(BF16), "w_a_out": w_a_out.astype(BF16), "w_b_out": w_b_out.astype(BF16), "w_o": w_o.astype(BF16),
           "w_mlp_up": w_mlp_up.astype(BF16), "w_mlp_down": w_mlp_down.astype(BF16)}
    groups = (("in", ("w_in",)), ("mix", ("w_a_out", "w_b_out", "w_o")), ("mlp", ("w_mlp_up", "w_mlp_down")))
    mod, cact, cwf = _prep_small(c, w_mod, b_mod, cw)
    mod = mod.reshape(nl, N_MOD, d)
    w16, _ = lax.optimization_barrier((w16, mod))
    gathers = {}
    token = jnp.zeros((), F32)
    for l in range(nl):
        for gname, members in groups:
            gathers[l, gname] = _exchange_start([w16[n][l] for n in members], True, f"gather_start_{gname}{l}")
            token = token + gathers[l, gname][4][0, 0]
    rows = rows + token
    wr = w_gate_r.astype(BF16)
    wi = w_gate_i.astype(BF16)

    saved = []
    weights = []
    xin = xs
    for l in range(nl):
        (win_f,) = _exchange_wait(gathers[l, "in"], mod if l == 0 else xin, True, f"gather_wait_in{l}")
        proj, h = _in_proj_fwd(xin, mod, rows, win_f, l, tm)
        wa_f, wb_f, wo_f = (w.reshape(d, d) for w in _exchange_wait(gathers[l, "mix"], proj, True, f"gather_wait_mix{l}"))
        x1, hs, yap, ybp, y = _mixer_core_fwd(proj, xin, mod, rows, cwf, wr, wi, wa_f, wb_f, wo_f, l, tm)
        wup_f, wdn_f = _exchange_wait(gathers[l, "mlp"], x1, True, f"gather_wait_mlp{l}")
        wdn_f = wdn_f.reshape(-1, d)
        x2, ru, y2, h2 = _mlp_fwd(x1, mod, rows, wup_f, wdn_f, l, tm)
        saved.append((xin, proj, h, x1, hs, yap, ybp, y, ru, y2, h2))
        weights.append((win_f, wa_f, wb_f, wo_f, wup_f, wdn_f))
        xin = x2
    loss_part, dx = _loss_fwd_bwd(xin, target, tm)
    loss = lax.psum(loss_part, ("x", "y", "c"))

    scatters = {}
    small = [None] * nl
    gate_parts = [None] * nl

    def scatter(l, gname, parts, rows):
        scatters[l, gname] = _exchange_start(parts, False, f"scatter_start_{gname}{l}")
        return rows + scatters[l, gname][4][0, 0]

    for l in reversed(range(nl)):
        xin, proj, h, x1, hs, yap, ybp, y, ru, y2, h2 = saved[l]
        win_f, wa_f, wb_f, wo_f, wup_f, wdn_f = weights[l]
        dx1, dy2, dup, act, sm_mlp = _mlp_bwd(dx, x1, y2, ru, mod, rows, wup_f, wdn_f, l, tm)
        g_up = _wgrad(h2, dup, True, ts, dx1)
        g_dn = _wgrad(act, dy2, False, ts, g_up)
        rows = scatter(l, "mlp", [g_up, g_dn], rows)
        dproj, dy, m, dyap, dybp, ya, yb, sm_mix, dwg = _mixer_core_bwd(dx1, y, yap, ybp, hs, proj, mod, rows, cwf, wr, wi, wa_f, wb_f, wo_f, l, tm // 2)
        gate_parts[l] = _exchange_start([dwg.astype(BF16)], True, f"gates_start{l}")
        rows = rows + gate_parts[l][4][0, 0]
        rows = scatter(l, "in", [_wgrad(h, dproj, True, ts, gate_parts[l][4])], rows)
        g_a = _wgrad(ya, dyap, False, ts, scatters[l, "in"][4])
        g_b = _wgrad(yb, dybp, False, ts, g_a)
        g_o = _wgrad(m, dy, False, ts, g_b)
        rows = scatter(l, "mix", [g_a, g_b, g_o], rows)
        dx, sm_in = _in_proj_bwd(dproj, xin, dx1, mod, rows, win_f, l, tm)
        small[l] = jnp.concatenate([sm_mlp, sm_mix, sm_in], axis=0)
    grad_x = dx.reshape(x.shape)

    recv = {}
    big = {}
    moments = {"w_in": (w_in, m_w_in, v_w_in), "w_mlp_up": (w_mlp_up, m_w_mlp_up, v_w_mlp_up), "w_a_out": (w_a_out, m_w_a_out, v_w_a_out),
               "w_b_out": (w_b_out, m_w_b_out, v_w_b_out), "w_o": (w_o, m_w_o, v_w_o), "w_mlp_down": (w_mlp_down, m_w_mlp_down, v_w_mlp_down)}

    def collect(l, gname, after):
        for n, zone in zip(dict(groups)[gname], _exchange_wait(scatters[l, gname], after, False, f"scatter_wait_{gname}{l}")):
            recv[n, l] = zone

    def update(name):
        w, m_, v_ = moments[name]
        big[name] = _sum_adamw([recv[name, l] for l in range(nl)], w, m_, v_, min(128, w.shape[1]))

    for l, gname in ((1, "mlp"), (1, "in"), (1, "mix"), (0, "mlp")):
        collect(l, gname, dx)
    update("w_mlp_up")
    update("w_mlp_down")

    lrows = jnp.concatenate(small, axis=0)

    def lrow(a, l, r):
        return a[l * G_LAYER_ROWS + r]

    dm = jnp.stack([jnp.concatenate([lrow(lrows, l, G_MLP_ROWS + G_MIX_ROWS + G_IN_SH), lrow(lrows, l, G_MLP_ROWS + G_MIX_ROWS + G_IN_SC),
                                     lrow(lrows, l, G_MLP_ROWS + G_MIX_GT), lrow(lrows, l, G_MLP_SH), lrow(lrows, l, G_MLP_SC),
                                     lrow(lrows, l, G_MLP_GT)]) for l in range(nl)])
    dm8 = jnp.concatenate([dm[:, None, :], jnp.zeros((nl, SUBLANES - 1, N_MOD * d), F32)], axis=1)
    srows, g_w_mod = _reduce_small(lrows, dm8, cact, big["w_mlp_down"][1])
    sgates = _sum_gathered([_exchange_wait(gate_parts[l], srows, True, f"gates_wait{l}")[0] for l in range(nl)])
    for gname in ("in", "mix"):
        collect(0, gname, srows)
    for name in ("w_in", "w_a_out", "w_b_out", "w_o"):
        update(name)

    def srow(l, r):
        return lrow(srows, l, r)

    def per_layer(r):
        return jnp.stack([srow(l, r) for l in range(nl)])

    mix0 = G_MLP_ROWS
    in0 = G_MLP_ROWS + G_MIX_ROWS
    g_b_mod = jnp.stack([jnp.concatenate([srow(l, in0 + G_IN_SH), srow(l, in0 + G_IN_SC), srow(l, mix0 + G_MIX_GT),
                                          srow(l, G_MLP_SH), srow(l, G_MLP_SC), srow(l, G_MLP_GT)]) for l in range(nl)])
    conv_a_full = jnp.stack([jnp.stack([srow(l, mix0 + G_MIX_CAW + k) for k in range(3)]) for l in range(nl)])
    conv_b_full = jnp.stack([jnp.stack([srow(l, mix0 + G_MIX_CBW + k) for k in range(4)]) for l in range(nl)])
    grads = {
        "b_mod": g_b_mod,
        "g_pre_mix": per_layer(in0 + G_IN_GPRE),
        "g_post_mix": per_layer(mix0 + G_MIX_GPOST),
        "conv_a_w": lax.dynamic_slice_in_dim(conv_a_full, me * cwid, cwid, axis=2),
        "conv_a_b": per_layer(mix0 + G_MIX_CAB),
        "conv_b_w": lax.dynamic_slice_in_dim(conv_b_full, me * cwid, cwid, axis=2),
        "conv_b_b": per_layer(mix0 + G_MIX_CBB),
        "w_gate_r": sgates[:, 0],
        "b_gate_r": per_layer(mix0 + G_MIX_BR),
        "w_gate_i": sgates[:, 1],
        "b_gate_i": per_layer(mix0 + G_MIX_BI),
        "lru_lambda": per_layer(mix0 + G_MIX_LAM),
        "g_pre_mlp": per_layer(G_MLP_GPRE),
        "g_post_mlp": per_layer(G_MLP_GPOST),
    }
    params = {
        "b_mod": (b_mod, m_b_mod, v_b_mod), "g_pre_mix": (g_pre_mix, m_g_pre_mix, v_g_pre_mix), "g_post_mix": (g_post_mix, m_g_post_mix, v_g_post_mix),
        "conv_a_w": (conv_a_w, m_conv_a_w, v_conv_a_w), "conv_a_b": (conv_a_b, m_conv_a_b, v_conv_a_b),
        "conv_b_w": (conv_b_w, m_conv_b_w, v_conv_b_w), "conv_b_b": (conv_b_b, m_conv_b_b, v_conv_b_b),
        "w_gate_r": (w_gate_r, m_w_gate_r, v_w_gate_r), "b_gate_r": (b_gate_r, m_b_gate_r, v_b_gate_r),
        "w_gate_i": (w_gate_i, m_w_gate_i, v_w_gate_i), "b_gate_i": (b_gate_i, m_b_gate_i, v_b_gate_i),
        "lru_lambda": (lru_lambda, m_lru_lambda, v_lru_lambda), "g_pre_mlp": (g_pre_mlp, m_g_pre_mlp, v_g_pre_mlp),
        "g_post_mlp": (g_post_mlp, m_g_post_mlp, v_g_post_mlp),
    }
    out = {}
    for name, g in grads.items():
        w, m_, v_ = params[name]
        flat = (-1, w.shape[-1])
        dl, nm, nv = _adamw(w.reshape(flat), g.reshape(flat), m_.reshape(flat), v_.reshape(flat))
        out[name] = (g.reshape(w.shape), dl.reshape(w.shape), nm.reshape(w.shape), nv.reshape(w.shape))
    out["w_mod"] = (g_w_mod,) + tuple(_adamw_tiled(w_mod, g_w_mod, m_w_mod, v_w_mod, min(128, d)))
    out.update(big)

    order = ("w_mod", "b_mod", "g_pre_mix", "g_post_mix", "w_in", "conv_a_w", "conv_a_b", "w_a_out", "conv_b_w", "conv_b_b", "w_gate_r", "b_gate_r",
             "w_gate_i", "b_gate_i", "lru_lambda", "w_b_out", "w_o", "g_pre_mlp", "g_post_mlp", "w_mlp_up", "w_mlp_down")
    return (loss, grad_x) + tuple(out[n][0] for n in order) + tuple(out[n][1] for n in order) + tuple(out[n][2] for n in order) + tuple(out[n][3] for n in order)
```

```python
import functools

import jax
import jax.numpy as jnp
from jax import lax
from jax.experimental import pallas as pl
from jax.experimental.pallas import tpu as pltpu

F32, BF16 = jnp.float32, jnp.bfloat16
EPS = 1e-6
LRU_C = 8.0
N_DEV = 8
N_MOD = 6
SUBLANES = 8
VMEM_BUDGET = 56 * 1024 * 1024
WGRAD_ROWS = 512
ADAM_LR, ADAM_B1, ADAM_B2, ADAM_EPS, ADAM_WD, ADAM_STEP = 0.001, 0.9, 0.999, 1e-08, 0.01, 10
MESH = pl.DeviceIdType.MESH
VMEM_SPEC = pl.BlockSpec(memory_space=pltpu.VMEM)
ANY_SPEC = pl.BlockSpec(memory_space=pl.ANY)
HBM_SPEC = pl.BlockSpec(memory_space=pltpu.HBM)
SEM_SPEC = pl.BlockSpec(memory_space=pltpu.SEMAPHORE)
SIDE_EFFECT = pltpu.SideEffectType.DATAFLOW_SIDE_EFFECTING

R_G_PRE_MIX, R_G_POST_MIX, R_CONV_A_B, R_CONV_B_B, R_B_GATE_R, R_B_GATE_I, R_LAMBDA, R_G_PRE_MLP, R_G_POST_MLP = range(9)
N_ROWS = 16
M_SH_M, M_SC_M, M_GT_M, M_SH_F, M_SC_F, M_GT_F = range(6)
CW_A, CW_B, CW_ROWS = 0, 3, 8
G_MLP_GT, G_MLP_GPOST, G_MLP_SC, G_MLP_SH, G_MLP_GPRE, G_MLP_ROWS = 0, 1, 2, 3, 4, 8
(G_MIX_GT, G_MIX_GPOST, G_MIX_CAB, G_MIX_CAW, G_MIX_CBB, G_MIX_CBW, G_MIX_BR, G_MIX_BI, G_MIX_LAM) = 0, 1, 2, 3, 6, 7, 11, 12, 13
G_MIX_ROWS = 16
G_IN_SC, G_IN_SH, G_IN_GPRE, G_IN_ROWS = 0, 1, 2, 8
G_LAYER_ROWS = G_MLP_ROWS + G_MIX_ROWS + G_IN_ROWS


def _cparams(dims=None, vmem=None):
    kw = {}
    if dims is not None:
        kw["dimension_semantics"] = dims
    if vmem is not None:
        kw["vmem_limit_bytes"] = int(min(max(vmem, 16 * 1024 * 1024), VMEM_BUDGET))
    return pltpu.CompilerParams(**kw)


def _nbytes(shape, dtype):
    n = 1
    for s in shape:
        n *= s
    return n * jnp.dtype(dtype).itemsize


def _resident(block, index_map):
    return pl.BlockSpec(block, index_map, pipeline_mode=pl.Buffered(1))


def _my_position():
    x, y, c = lax.axis_index("x"), lax.axis_index("y"), lax.axis_index("c")
    return (x, y, c), 4 * x + 2 * y + c


def _peer(pos, k):
    x, y, c = pos
    px = 1 - x if k & 4 else x
    py = 1 - y if k & 2 else y
    pc = 1 - c if k & 1 else c
    return (px, py, pc), 4 * px + 2 * py + pc


def _remote(src, dst, ssem, rsem, peer):
    return pltpu.make_async_remote_copy(src_ref=src, dst_ref=dst, send_sem=ssem, recv_sem=rsem, device_id=peer, device_id_type=MESH)


def _dot(a, b):
    return jnp.dot(a, b, preferred_element_type=F32)


def _dot_nt(a, b):
    return lax.dot_general(a, b, (((1,), (1,)), ((), ())), preferred_element_type=F32)


def _dot_tn(a, b):
    return lax.dot_general(a, b, (((0,), (0,)), ((), ())), preferred_element_type=F32)


def _colsum(v):
    return jnp.sum(v, axis=0, keepdims=True)


def _sigmoid(v):
    return jax.nn.sigmoid(v)


def _gelu(v):
    k = 0.7978845608028654
    t = jnp.tanh(k * (v + 0.044715 * (v * v * v)))
    return 0.5 * v * (1.0 + t), t


def _gelu_grad(v, t):
    k = 0.7978845608028654
    return 0.5 * (1.0 + t) + 0.5 * v * (1.0 - t * t) * (k * (1.0 + 3.0 * 0.044715 * v * v))


def _expm1(v):
    u = jnp.exp(v)
    um1 = u - 1.0
    q = um1 * v / jnp.log(u)
    return jnp.where(um1 == 0.0, v, jnp.where(um1 == -1.0, -1.0, q))


def _softplus_neg(lam):
    z = -lam
    u = jnp.exp(-jnp.abs(z))
    w = 1.0 + u
    l1p = jnp.where(w == 1.0, u, jnp.log(w) * u / (w - 1.0))
    return jnp.maximum(z, 0.0) + l1p


def _rms(v):
    return lax.rsqrt(jnp.mean(v * v, axis=-1, keepdims=True) + EPS)


def _prenorm_bwd(xv, dh, g, sc):
    r = _rms(xv)
    xn = xv * r
    n = xn * g
    dsc = _colsum(dh * n)
    dsh = _colsum(dh)
    dn = dh * (1.0 + sc)
    dg = _colsum(dn * xn)
    dxn = dn * g
    dx = r * (dxn - xn * jnp.mean(dxn * xn, axis=-1, keepdims=True))
    return dx, dsc, dsh, dg


def _postnorm_bwd(yv, dout, g, gt):
    r = _rms(yv)
    yn = yv * r
    dgt = _colsum(dout * (yn * g))
    dn = dout * gt
    dg = _colsum(dn * yn)
    dyn = dn * g
    dy = r * (dyn - yn * jnp.mean(dyn * yn, axis=-1, keepdims=True))
    return dy, dgt, dg


def _gates(xc, wr_ref, wi_ref, b_r, b_i, sp, nh, bw):
    xcb = xc.astype(BF16)
    zr = jnp.concatenate([_dot(xcb[:, h * bw:(h + 1) * bw], wr_ref[h]) for h in range(nh)], axis=1) + b_r
    zi = jnp.concatenate([_dot(xcb[:, h * bw:(h + 1) * bw], wi_ref[h]) for h in range(nh)], axis=1) + b_i
    r = _sigmoid(zr)
    ig = _sigmoid(zi)
    la = (-LRU_C * r) * sp
    a = jnp.exp(la)
    mult = jnp.sqrt(-_expm1(2.0 * la))
    return xcb, r, ig, a, mult


def _scan_block(a8, b8, reverse):
    row = lax.broadcasted_iota(jnp.int32, a8.shape, 0)
    for s in (1, 2, 4):
        if reverse:
            keep = row < SUBLANES - s
            a_sh = pltpu.roll(a8, SUBLANES - s, 0)
            b_sh = pltpu.roll(b8, SUBLANES - s, 0)
        else:
            keep = row >= s
            a_sh = pltpu.roll(a8, s, 0)
            b_sh = pltpu.roll(b8, s, 0)
        b8 = b8 + a8 * jnp.where(keep, b_sh, 0.0)
        a8 = a8 * jnp.where(keep, a_sh, 1.0)
    return a8, b8


def _prep_small(c, w_mod, b_mod, cw):
    d = c.shape[1]
    cm = w_mod.shape[2]
    cwid = cw.shape[2]
    nl = w_mod.shape[0]

    def body(c_ref, wm_ref, bm_ref, cw_ref, mod_ref, cact_ref, cwf_ref, cbuf, pbuf, rbuf, ssem, rsem, lsem):
        pos, me = _my_position()
        me8 = pl.multiple_of(me * SUBLANES, SUBLANES)
        cbuf[pl.ds(me8, SUBLANES), :] = jnp.broadcast_to(c_ref[...], (SUBLANES, d))
        own_cw = pltpu.make_async_copy(cw_ref, cwf_ref.at[:, :, pl.ds(me * cwid, cwid)], lsem.at[0])
        own_cw.start()
        first = []
        for k in range(1, N_DEV):
            peer, _ = _peer(pos, k)
            rows = cbuf.at[pl.ds(me8, SUBLANES), :]
            first.append(_remote(rows, rows, ssem.at[0, k - 1], rsem.at[0, k - 1], peer))
            first.append(_remote(cw_ref, cwf_ref.at[:, :, pl.ds(me * cwid, cwid)], ssem.at[1, k - 1], rsem.at[1, k - 1], peer))
        for cp in first:
            cp.start()
        for k in range(1, N_DEV):
            peer, pj = _peer(pos, k)
            pj8 = pl.multiple_of(pj * SUBLANES, SUBLANES)
            rows = cbuf.at[pl.ds(pj8, SUBLANES), :]
            _remote(rows, rows, ssem.at[0, k - 1], rsem.at[0, k - 1], peer).wait_recv()
        cv = cbuf[...]
        cact = cv * _sigmoid(cv)
        cact_ref[...] = cact
        cb = cact.astype(BF16)
        for l in range(nl):
            pbuf[l] = _dot(cb, wm_ref[l].astype(BF16))
        own_p = pltpu.make_async_copy(pbuf.at[:, pl.ds(me8, SUBLANES), :], rbuf.at[me], lsem.at[1])
        own_p.start()
        second = []
        for k in range(1, N_DEV):
            peer, pj = _peer(pos, k)
            pj8 = pl.multiple_of(pj * SUBLANES, SUBLANES)
            second.append(_remote(pbuf.at[:, pl.ds(pj8, SUBLANES), :], rbuf.at[me], ssem.at[2, k - 1], rsem.at[2, k - 1], peer))
        for cp in second:
            cp.start()
        for k in range(1, N_DEV):
            peer, pj = _peer(pos, k)
            _remote(pbuf.at[:, pl.ds(0, SUBLANES), :], rbuf.at[pj], ssem.at[2, k - 1], rsem.at[2, k - 1], peer).wait_recv()
            _remote(cw_ref, cwf_ref.at[:, :, pl.ds(pj * cwid, cwid)], ssem.at[1, k - 1], rsem.at[1, k - 1], peer).wait_recv()
        own_p.wait()
        own_cw.wait()
        for l in range(nl):
            for j in range(N_DEV):
                mod_ref[l:l + 1, j * cm:(j + 1) * cm] = rbuf[j, l, 0:1, :] + bm_ref[l:l + 1, j * cm:(j + 1) * cm]
        for cp in first + second:
            cp.wait_send()

    return pl.pallas_call(
        body,
        name="prep_small",
        out_shape=(
            jax.ShapeDtypeStruct((nl, N_MOD * d), F32),
            jax.ShapeDtypeStruct((N_DEV * SUBLANES, d), F32),
            jax.ShapeDtypeStruct((nl, CW_ROWS, d), F32),
        ),
        in_specs=[VMEM_SPEC] * 4,
        out_specs=(VMEM_SPEC,) * 3,
        scratch_shapes=[
            pltpu.VMEM((N_DEV * SUBLANES, d), F32),
            pltpu.VMEM((nl, N_DEV * SUBLANES, cm), F32),
            pltpu.VMEM((N_DEV, nl, SUBLANES, cm), F32),
            pltpu.SemaphoreType.DMA((3, N_DEV - 1)),
            pltpu.SemaphoreType.DMA((3, N_DEV - 1)),
            pltpu.SemaphoreType.DMA((2,)),
        ],
        compiler_params=_cparams(vmem=3 * _nbytes(w_mod.shape, F32)),
    )(c, w_mod, b_mod, cw)


def _exchange_start(parts, gather, name):
    n = len(parts)
    lands = [lax.empty(((N_DEV,) + tuple(p.shape)) if gather else tuple(p.shape), p.dtype) for p in parts]

    def body(*refs):
        ins, lnd = refs[:n], refs[n:2 * n]
        ssem, rsem, token = refs[2 * n], refs[2 * n + 1], refs[-1]
        pos, me = _my_position()
        for k in range(1, N_DEV):
            peer, pj = _peer(pos, k)
            for t in range(n):
                src = ins[t] if gather else ins[t].at[pj]
                q = t * (N_DEV - 1) + k - 1
                _remote(src, lnd[t].at[me], ssem.at[q], rsem.at[q], peer).start()
        token[...] = jnp.zeros(token.shape, F32)

    out = pl.pallas_call(
        body,
        name=name,
        out_shape=(pltpu.SemaphoreType.DMA((n * (N_DEV - 1),)), pltpu.SemaphoreType.DMA((n * (N_DEV - 1),)))
        + tuple(pltpu.HBM(p.shape, p.dtype) for p in parts) + tuple(pltpu.HBM(p.shape, p.dtype) for p in lands)
        + (jax.ShapeDtypeStruct((SUBLANES, 128), F32),),
        in_specs=[HBM_SPEC] * (2 * n),
        out_specs=(SEM_SPEC, SEM_SPEC) + (HBM_SPEC,) * (2 * n) + (VMEM_SPEC,),
        input_output_aliases={i: 2 + i for i in range(2 * n)},
        compiler_params=pltpu.CompilerParams(has_side_effects=SIDE_EFFECT),
    )(*[pltpu.with_memory_space_constraint(p, pltpu.HBM) for p in list(parts) + lands])
    return out[0], out[1], list(out[2:2 + n]), list(out[2 + n:2 + 2 * n]), out[-1]


def _exchange_wait(started, after, gather, name):
    ssem, rsem, parts, lands, _ = started
    n = len(parts)

    def body(*refs):
        ins, lnd = refs[:n], refs[n:2 * n]
        ssem_ref, rsem_ref = refs[2 * n], refs[2 * n + 1]
        stage, lsem = refs[-1 - n:-1], refs[-1]
        pos, me = _my_position()
        load = []
        for t in range(n):
            src = ins[t] if gather else ins[t].at[me]
            load.append(pltpu.make_async_copy(src, stage[t], lsem.at[t]))
            load[-1].start()
        store = []
        for t in range(n):
            load[t].wait()
            store.append(pltpu.make_async_copy(stage[t], lnd[t].at[me], lsem.at[t]))
            store[-1].start()
        for k in range(1, N_DEV):
            peer, pj = _peer(pos, k)
            for t in range(n):
                src = ins[t] if gather else ins[t].at[pj]
                q = t * (N_DEV - 1) + k - 1
                _remote(src, lnd[t].at[me], ssem_ref.at[q], rsem_ref.at[q], peer).wait_send()
                _remote(src, lnd[t].at[pj], ssem_ref.at[q], rsem_ref.at[q], peer).wait_recv()
        for cp in store:
            cp.wait()

    out = pl.pallas_call(
        body,
        name=name,
        out_shape=tuple(pltpu.HBM(p.shape, p.dtype) for p in parts) + tuple(pltpu.HBM(p.shape, p.dtype) for p in lands),
        in_specs=[HBM_SPEC] * (2 * n) + [SEM_SPEC, SEM_SPEC, ANY_SPEC],
        out_specs=(HBM_SPEC,) * (2 * n),
        input_output_aliases={i: i for i in range(2 * n)},
        scratch_shapes=[pltpu.VMEM(tuple(z.shape[1:]), z.dtype) for z in lands] + [pltpu.SemaphoreType.DMA((n,))],
        compiler_params=pltpu.CompilerParams(has_side_effects=SIDE_EFFECT),
    )(*parts, *lands, ssem, rsem, after)
    return list(out[n:])


CHIP_PEERS = 3


def _chip_peer(pos, k, core):
    x, y, _ = pos
    px = 1 - x if k & 2 else x
    py = 1 - y if k & 1 else y
    return (px, py, core), 4 * px + 2 * py + core


def _gather2_start(parts, name):
    n = len(parts)
    lands = [lax.empty((N_DEV,) + tuple(p.shape), p.dtype) for p in parts]
    per = CHIP_PEERS + 1

    def body(*refs):
        ins, lnd = refs[:n], refs[n:2 * n]
        ssem, rsem, token = refs[2 * n], refs[2 * n + 1], refs[-1]
        pos, me = _my_position()
        sibling = (pos[0], pos[1], 1 - pos[2])
        for t in range(n):
            _remote(ins[t], lnd[t].at[me], ssem.at[per * t], rsem.at[per * t], sibling).start()
        for k in range(1, per):
            peer, _ = _chip_peer(pos, k, pos[2])
            for t in range(n):
                _remote(ins[t], lnd[t].at[me], ssem.at[per * t + k], rsem.at[per * t + k], peer).start()
        token[...] = jnp.zeros(token.shape, F32)

    out = pl.pallas_call(
        body,
        name=name,
        out_shape=(pltpu.SemaphoreType.DMA((n * per,)), pltpu.SemaphoreType.DMA((n * per,)))
        + tuple(pltpu.HBM(p.shape, p.dtype) for p in parts) + tuple(pltpu.HBM(p.shape, p.dtype) for p in lands)
        + (jax.ShapeDtypeStruct((SUBLANES, 128), F32),),
        in_specs=[HBM_SPEC] * (2 * n),
        out_specs=(SEM_SPEC, SEM_SPEC) + (HBM_SPEC,) * (2 * n) + (VMEM_SPEC,),
        input_output_aliases={i: 2 + i for i in range(2 * n)},
        compiler_params=pltpu.CompilerParams(has_side_effects=SIDE_EFFECT),
    )(*[pltpu.with_memory_space_constraint(p, pltpu.HBM) for p in list(parts) + lands])
    return out[0], out[1], list(out[2:2 + n]), list(out[2 + n:2 + 2 * n]), out[-1]


def _gather2_forward(started, after, name):
    ssem, rsem, parts, lands, token = started
    n = len(lands)
    per = CHIP_PEERS + 1

    def body(*refs):
        lnd = refs[:n]
        rsem_a, fsend, frecv = refs[n], refs[2 * n + 2], refs[2 * n + 3]
        pos, _ = _my_position()
        sibling = (pos[0], pos[1], 1 - pos[2])
        for k in range(1, per):
            peer, pk = _chip_peer(pos, k, pos[2])
            for t in range(n):
                block = lnd[t].at[pk]
                _remote(block, block, rsem_a.at[per * t + k], rsem_a.at[per * t + k], peer).wait_recv()
                q = CHIP_PEERS * t + k - 1
                _remote(block, block, fsend.at[q], frecv.at[q], sibling).start()

    out = pl.pallas_call(
        body,
        name=name,
        out_shape=tuple(pltpu.HBM(p.shape, p.dtype) for p in lands)
        + (pltpu.SemaphoreType.DMA((n * CHIP_PEERS,)), pltpu.SemaphoreType.DMA((n * CHIP_PEERS,))),
        in_specs=[HBM_SPEC] * n + [SEM_SPEC, ANY_SPEC],
        out_specs=(HBM_SPEC,) * n + (SEM_SPEC, SEM_SPEC),
        input_output_aliases={i: i for i in range(n)},
        compiler_params=pltpu.CompilerParams(has_side_effects=SIDE_EFFECT),
    )(*lands, rsem, after)
    return ssem, rsem, parts, list(out[:n]), token, out[n], out[n + 1]


def _gather2_wait(forwarded, after, name):
    ssem, rsem, parts, lands, _, fsend, frecv = forwarded
    n = len(parts)
    per = CHIP_PEERS + 1

    def body(*refs):
        ins, lnd = refs[:n], refs[n:2 * n]
        ssem_a, rsem_a, fs, fr = refs[2 * n:2 * n + 4]
        stage, lsem = refs[-1 - n:-1], refs[-1]
        pos, me = _my_position()
        sibling = (pos[0], pos[1], 1 - pos[2])
        sib = 4 * pos[0] + 2 * pos[1] + 1 - pos[2]
        load = []
        for t in range(n):
            load.append(pltpu.make_async_copy(ins[t], stage[t], lsem.at[t]))
            load[-1].start()
        store = []
        for t in range(n):
            load[t].wait()
            store.append(pltpu.make_async_copy(stage[t], lnd[t].at[me], lsem.at[t]))
            store[-1].start()
        for t in range(n):
            _remote(ins[t], lnd[t].at[me], ssem_a.at[per * t], rsem_a.at[per * t], sibling).wait_send()
            _remote(ins[t], lnd[t].at[sib], ssem_a.at[per * t], rsem_a.at[per * t], sibling).wait_recv()
        for k in range(1, per):
            peer, pk = _chip_peer(pos, k, pos[2])
            _, qk = _chip_peer(pos, k, 1 - pos[2])
            for t in range(n):
                q = CHIP_PEERS * t + k - 1
                _remote(ins[t], lnd[t].at[me], ssem_a.at[per * t + k], rsem_a.at[per * t + k], peer).wait_send()
                _remote(lnd[t].at[pk], lnd[t].at[pk], fs.at[q], fr.at[q], sibling).wait_send()
                _remote(lnd[t].at[qk], lnd[t].at[qk], fs.at[q], fr.at[q], sibling).wait_recv()
        for cp in store:
            cp.wait()

    out = pl.pallas_call(
        body,
        name=name,
        out_shape=tuple(pltpu.HBM(p.shape, p.dtype) for p in parts) + tuple(pltpu.HBM(p.shape, p.dtype) for p in lands),
        in_specs=[HBM_SPEC] * (2 * n) + [SEM_SPEC] * 4 + [ANY_SPEC],
        out_specs=(HBM_SPEC,) * (2 * n),
        input_output_aliases={i: i for i in range(2 * n)},
        scratch_shapes=[pltpu.VMEM(tuple(z.shape[1:]), z.dtype) for z in lands] + [pltpu.SemaphoreType.DMA((n,))],
        compiler_params=pltpu.CompilerParams(has_side_effects=SIDE_EFFECT),
    )(*parts, *lands, ssem, rsem, fsend, frecv, after)
    return list(out[n:])


def _reduce_small(rows, dm8, cact, after):
    r, d = rows.shape
    nl = dm8.shape[0]
    cm = dm8.shape[2] // N_DEV

    def body(rows_ref, dm_ref, cact_ref, after_ref, orow_ref, owm_ref, gr, dmr, ssem, rsem, lsem):
        pos, me = _my_position()
        me8 = pl.multiple_of(me * SUBLANES, SUBLANES)
        gr[me] = rows_ref[...]
        own_dm = pltpu.make_async_copy(dm_ref.at[:, :, pl.ds(me * cm, cm)], dmr.at[:, pl.ds(me8, SUBLANES), :], lsem.at[0])
        own_dm.start()
        sends = []
        for k in range(1, N_DEV):
            peer, pj = _peer(pos, k)
            sends.append(_remote(gr.at[me], gr.at[me], ssem.at[0, k - 1], rsem.at[0, k - 1], peer))
            sends.append(_remote(dm_ref.at[:, :, pl.ds(pj * cm, cm)], dmr.at[:, pl.ds(me8, SUBLANES), :],
                                 ssem.at[1, k - 1], rsem.at[1, k - 1], peer))
        for cp in sends:
            cp.start()
        for k in range(1, N_DEV):
            peer, pj = _peer(pos, k)
            pj8 = pl.multiple_of(pj * SUBLANES, SUBLANES)
            _remote(gr.at[pj], gr.at[pj], ssem.at[0, k - 1], rsem.at[0, k - 1], peer).wait_recv()
            _remote(dm_ref.at[:, :, pl.ds(0, cm)], dmr.at[:, pl.ds(pj8, SUBLANES), :], ssem.at[1, k - 1], rsem.at[1, k - 1], peer).wait_recv()
        own_dm.wait()
        acc = gr[0]
        for j in range(1, N_DEV):
            acc = acc + gr[j]
        orow_ref[...] = acc
        cb = cact_ref[...].astype(BF16)
        for l in range(nl):
            owm_ref[l] = _dot_tn(cb, dmr[l].astype(BF16))
        for cp in sends:
            cp.wait_send()

    return pl.pallas_call(
        body,
        name="reduce_small",
        out_shape=(jax.ShapeDtypeStruct((r, d), F32), jax.ShapeDtypeStruct((nl, d, cm), F32)),
        in_specs=[VMEM_SPEC] * 3 + [ANY_SPEC],
        out_specs=(VMEM_SPEC,) * 2,
        scratch_shapes=[
            pltpu.VMEM((N_DEV, r, d), F32),
            pltpu.VMEM((nl, N_DEV * SUBLANES, cm), F32),
            pltpu.SemaphoreType.DMA((2, N_DEV - 1)),
            pltpu.SemaphoreType.DMA((2, N_DEV - 1)),
            pltpu.SemaphoreType.DMA((1,)),
        ],
        compiler_params=_cparams(vmem=4 * _nbytes((N_DEV, r, d), F32) + 6 * _nbytes((nl, d, cm), F32)),
    )(rows, dm8, cact, after)


def _sum_gathered(zones):
    nl = len(zones)

    def body(*refs):
        for l in range(nl):
            acc = refs[l][0].astype(F32)
            for j in range(1, N_DEV):
                acc = acc + refs[l][j].astype(F32)
            refs[nl][l] = acc

    return pl.pallas_call(
        body,
        name="sum_gathered",
        out_shape=jax.ShapeDtypeStruct((nl,) + tuple(zones[0].shape[1:]), F32),
        in_specs=[VMEM_SPEC] * nl,
        out_specs=VMEM_SPEC,
        compiler_params=_cparams(vmem=8 * nl * _nbytes(zones[0].shape, BF16)),
    )(*zones)


def _in_proj_fwd(x, mod, rows, win_f, l, tm):
    s, d = x.shape
    nb, _, ci = win_f.shape

    def body(x_ref, mod_ref, rows_ref, w_ref, proj_ref, h_ref):
        xv = x_ref[...]
        g = rows_ref[R_G_PRE_MIX:R_G_PRE_MIX + 1, :]
        h = (xv * _rms(xv) * g) * (1.0 + mod_ref[M_SC_M:M_SC_M + 1, :]) + mod_ref[M_SH_M:M_SH_M + 1, :]
        hb = h.astype(BF16)
        h_ref[...] = hb
        for j in range(nb):
            proj_ref[:, j * ci:(j + 1) * ci] = _dot(hb, w_ref[j])

    return pl.pallas_call(
        body,
        name="in_proj_fwd",
        grid=(s // tm,),
        in_specs=[
            pl.BlockSpec((tm, d), lambda i: (i, 0)),
            _resident((None, N_MOD, d), lambda i: (l, 0, 0)),
            _resident((None, N_ROWS, d), lambda i: (l, 0, 0)),
            _resident((nb, d, ci), lambda i: (0, 0, 0)),
        ],
        out_specs=(pl.BlockSpec((tm, nb * ci), lambda i: (i, 0)), pl.BlockSpec((tm, d), lambda i: (i, 0))),
        out_shape=(jax.ShapeDtypeStruct((s, nb * ci), F32), jax.ShapeDtypeStruct((s, d), BF16)),
        compiler_params=_cparams(("parallel",), _nbytes((nb, d, ci), BF16) + 3 * _nbytes((tm, nb * ci), F32) + 8 * _nbytes((tm, d), F32)),
    )(x, mod, rows, win_f)


def _mixer_core_fwd(proj, x, mod, rows, cwf, wr, wi, wa_f, wb_f, wo_f, l, tm):
    s, d = x.shape
    nh, bw, _ = wr.shape[1:]

    def body(proj_ref, x_ref, mod_ref, rows_ref, cw_ref, wr_ref, wi_ref, wa_ref, wb_ref, wo_ref,
             x1_ref, hs_ref, yap_ref, ybp_ref, y_ref, cvbuf, xbbuf, a_s, b_s, hprev):
        i = pl.program_id(0)

        @pl.when(i == 0)
        def _():
            cvbuf[pl.ds(0, SUBLANES), :] = jnp.zeros((SUBLANES, d), F32)
            xbbuf[pl.ds(0, SUBLANES), :] = jnp.zeros((SUBLANES, d), F32)
            hprev[...] = jnp.zeros((SUBLANES, d), F32)

        def row(r):
            return rows_ref[r:r + 1, :]

        def tap(r):
            return cw_ref[r:r + 1, :]

        ba = proj_ref[:, 0:d]
        cv = proj_ref[:, d:2 * d] * proj_ref[:, 2 * d:3 * d]
        cvbuf[pl.ds(SUBLANES, tm), :] = cv
        conv3 = ((row(R_CONV_A_B) + cvbuf[pl.ds(SUBLANES - 2, tm), :] * tap(CW_A)) + cvbuf[pl.ds(SUBLANES - 1, tm), :] * tap(CW_A + 1)) + cv * tap(CW_A + 2)
        ya = ba * conv3
        cvbuf[pl.ds(0, SUBLANES), :] = cvbuf[pl.ds(tm, SUBLANES), :]
        xb = proj_ref[:, 3 * d:4 * d]
        xbbuf[pl.ds(SUBLANES, tm), :] = xb
        xc = (((row(R_CONV_B_B) + xbbuf[pl.ds(SUBLANES - 3, tm), :] * tap(CW_B)) + xbbuf[pl.ds(SUBLANES - 2, tm), :] * tap(CW_B + 1))
              + xbbuf[pl.ds(SUBLANES - 1, tm), :] * tap(CW_B + 2)) + xb * tap(CW_B + 3)
        xbbuf[pl.ds(0, SUBLANES), :] = xbbuf[pl.ds(tm, SUBLANES), :]
        sp = _softplus_neg(row(R_LAMBDA))
        _, _, ig, a, mult = _gates(xc, wr_ref, wi_ref, row(R_B_GATE_R), row(R_B_GATE_I), sp, nh, bw)
        a_s[...] = a
        b_s[...] = mult * (ig * xc)

        def blk(j, hp):
            o = pl.multiple_of(j * SUBLANES, SUBLANES)
            a8, b8 = _scan_block(a_s[pl.ds(o, SUBLANES), :], b_s[pl.ds(o, SUBLANES), :], reverse=False)
            h8 = b8 + a8 * hp
            hs_ref[pl.ds(o, SUBLANES), :] = h8
            return jnp.broadcast_to(h8[SUBLANES - 1:SUBLANES, :], (SUBLANES, d))

        hprev[...] = lax.fori_loop(0, tm // SUBLANES, blk, hprev[...])
        gel, _ = _gelu(proj_ref[:, 4 * d:5 * d])
        yb = hs_ref[...] * gel
        yap = _dot(ya.astype(BF16), wa_ref[...])
        ybp = _dot(yb.astype(BF16), wb_ref[...])
        yap_ref[...] = yap
        ybp_ref[...] = ybp
        m = _sigmoid(proj_ref[:, 5 * d:6 * d]) * yap + _sigmoid(proj_ref[:, 6 * d:7 * d]) * ybp
        y = _dot(m.astype(BF16), wo_ref[...])
        y_ref[...] = y
        x1_ref[...] = x_ref[...] + mod_ref[M_GT_M:M_GT_M + 1, :] * ((y * _rms(y)) * row(R_G_POST_MIX))

    tile = pl.BlockSpec((tm, d), lambda i: (i, 0))
    return pl.pallas_call(
        body,
        name="mixer_core_fwd",
        grid=(s // tm,),
        in_specs=[
            pl.BlockSpec((tm, 7 * d), lambda i: (i, 0)),
            tile,
            _resident((None, N_MOD, d), lambda i: (l, 0, 0)),
            _resident((None, N_ROWS, d), lambda i: (l, 0, 0)),
            _resident((None, CW_ROWS, d), lambda i: (l, 0, 0)),
            _resident((None, nh, bw, bw), lambda i: (l, 0, 0, 0)),
            _resident((None, nh, bw, bw), lambda i: (l, 0, 0, 0)),
            _resident((d, d), lambda i: (0, 0)),
            _resident((d, d), lambda i: (0, 0)),
            _resident((d, d), lambda i: (0, 0)),
        ],
        out_specs=(tile,) * 5,
        out_shape=(jax.ShapeDtypeStruct((s, d), F32),) * 5,
        scratch_shapes=[
            pltpu.VMEM((tm + SUBLANES, d), F32),
            pltpu.VMEM((tm + SUBLANES, d), F32),
            pltpu.VMEM((tm, d), F32),
            pltpu.VMEM((tm, d), F32),
            pltpu.VMEM((SUBLANES, d), F32),
        ],
        compiler_params=_cparams(("arbitrary",), 3 * _nbytes((d, d), BF16) + 2 * _nbytes((tm, 7 * d), F32) + 40 * _nbytes((tm, d), F32)),
    )(proj, x, mod, rows, cwf, wr, wi, wa_f, wb_f, wo_f)


def _mlp_fwd(x1, mod, rows, wup_f, wdn_f, l, tm):
    s, d = x1.shape
    nb, _, cu = wup_f.shape
    dff = nb * cu

    def body(x1_ref, mod_ref, rows_ref, wu_ref, wd_ref, x2_ref, ru_ref, y2_ref, h2_ref):
        xv = x1_ref[...]
        g = rows_ref[R_G_PRE_MLP:R_G_PRE_MLP + 1, :]
        h2 = ((xv * _rms(xv) * g) * (1.0 + mod_ref[M_SC_F:M_SC_F + 1, :]) + mod_ref[M_SH_F:M_SH_F + 1, :]).astype(BF16)
        h2_ref[...] = h2
        ru = jnp.concatenate([jnp.maximum(_dot(h2, wu_ref[j]), 0.0) for j in range(nb)], axis=1)
        ru_ref[...] = ru.astype(BF16)
        y2 = _dot((ru * ru).astype(BF16), wd_ref[...])
        y2_ref[...] = y2
        x2_ref[...] = xv + mod_ref[M_GT_F:M_GT_F + 1, :] * ((y2 * _rms(y2)) * rows_ref[R_G_POST_MLP:R_G_POST_MLP + 1, :])

    tile = pl.BlockSpec((tm, d), lambda i: (i, 0))
    wide = pl.BlockSpec((tm, dff), lambda i: (i, 0))
    return pl.pallas_call(
        body,
        name="mlp_fwd",
        grid=(s // tm,),
        in_specs=[
            tile,
            _resident((None, N_MOD, d), lambda i: (l, 0, 0)),
            _resident((None, N_ROWS, d), lambda i: (l, 0, 0)),
            _resident((nb, d, cu), lambda i: (0, 0, 0)),
            _resident((dff, d), lambda i: (0, 0)),
        ],
        out_specs=(tile, wide, tile, tile),
        out_shape=(jax.ShapeDtypeStruct((s, d), F32), jax.ShapeDtypeStruct((s, dff), BF16),
                   jax.ShapeDtypeStruct((s, d), F32), jax.ShapeDtypeStruct((s, d), BF16)),
        compiler_params=_cparams(("parallel",), 2 * _nbytes((dff, d), BF16) + 5 * _nbytes((tm, dff), F32) + 12 * _nbytes((tm, d), F32)),
    )(x1, mod, rows, wup_f, wdn_f)


def _loss_fwd_bwd(y, target, tm):
    s, d = y.shape

    def body(y_ref, t_ref, loss_ref, dy_ref):
        @pl.when(pl.program_id(0) == 0)
        def _():
            loss_ref[...] = jnp.zeros(loss_ref.shape, F32)

        e = y_ref[...] - t_ref[...]
        dy_ref[...] = e * (1.0 / d)
        loss_ref[...] += 0.5 * jnp.sum(jnp.mean(e * e, axis=-1, keepdims=True), axis=0, keepdims=True)

    tile = pl.BlockSpec((tm, d), lambda i: (i, 0))
    loss, dy = pl.pallas_call(
        body,
        name="loss",
        grid=(s // tm,),
        in_specs=[tile, tile],
        out_specs=(pl.BlockSpec((SUBLANES, 128), lambda i: (0, 0)), tile),
        out_shape=(jax.ShapeDtypeStruct((SUBLANES, 128), F32), jax.ShapeDtypeStruct((s, d), F32)),
        compiler_params=_cparams(("arbitrary",)),
    )(y, target)
    return loss[0, 0], dy


def _mlp_bwd(dx2, x1, y2, ru, mod, rows, wup_f, wdn_f, l, tm):
    s, d = x1.shape
    nb, _, cu = wup_f.shape
    dff = nb * cu

    def body(dx2_ref, x1_ref, y2_ref, ru_ref, mod_ref, rows_ref, wu_ref, wd_ref, dx1_ref, dy2_ref, dup_ref, act_ref, sm_ref):
        @pl.when(pl.program_id(0) == 0)
        def _():
            sm_ref[...] = jnp.zeros(sm_ref.shape, F32)

        dout = dx2_ref[...]
        dy2, dgt, dgpost = _postnorm_bwd(y2_ref[...], dout, rows_ref[R_G_POST_MLP:R_G_POST_MLP + 1, :], mod_ref[M_GT_F:M_GT_F + 1, :])
        dy2b = dy2.astype(BF16)
        dy2_ref[...] = dy2b
        ruv = ru_ref[...].astype(F32)
        act_ref[...] = (ruv * ruv).astype(BF16)
        dup = (_dot_nt(dy2b, wd_ref[...]) * (2.0 * ruv)).astype(BF16)
        dup_ref[...] = dup
        dh2 = _dot_nt(dup[:, 0:cu], wu_ref[0])
        for j in range(1, nb):
            dh2 = dh2 + _dot_nt(dup[:, j * cu:(j + 1) * cu], wu_ref[j])
        dxn, dsc, dsh, dgpre = _prenorm_bwd(x1_ref[...], dh2, rows_ref[R_G_PRE_MLP:R_G_PRE_MLP + 1, :], mod_ref[M_SC_F:M_SC_F + 1, :])
        dx1_ref[...] = dout + dxn
        for r, v in ((G_MLP_GT, dgt), (G_MLP_GPOST, dgpost), (G_MLP_SC, dsc), (G_MLP_SH, dsh), (G_MLP_GPRE, dgpre)):
            sm_ref[r:r + 1, :] += v

    tile = pl.BlockSpec((tm, d), lambda i: (i, 0))
    wide = pl.BlockSpec((tm, dff), lambda i: (i, 0))
    return pl.pallas_call(
        body,
        name="mlp_bwd",
        grid=(s // tm,),
        in_specs=[
            tile, tile, tile, wide,
            _resident((None, N_MOD, d), lambda i: (l, 0, 0)),
            _resident((None, N_ROWS, d), lambda i: (l, 0, 0)),
            _resident((nb, d, cu), lambda i: (0, 0, 0)),
            _resident((dff, d), lambda i: (0, 0)),
        ],
        out_specs=(tile, tile, wide, wide, pl.BlockSpec((G_MLP_ROWS, d), lambda i: (0, 0))),
        out_shape=(jax.ShapeDtypeStruct((s, d), F32), jax.ShapeDtypeStruct((s, d), BF16), jax.ShapeDtypeStruct((s, dff), BF16),
                   jax.ShapeDtypeStruct((s, dff), BF16), jax.ShapeDtypeStruct((G_MLP_ROWS, d), F32)),
        compiler_params=_cparams(("arbitrary",), 2 * _nbytes((dff, d), BF16) + 6 * _nbytes((tm, dff), F32) + 16 * _nbytes((tm, d), F32)),
    )(dx2, x1, y2, ru, mod, rows, wup_f, wdn_f)


def _mixer_core_bwd(dx1, y, yap, ybp, hs, proj, mod, rows, cwf, wr, wi, wa_f, wb_f, wo_f, l, tm):
    s, d = dx1.shape
    nh, bw, _ = wr.shape[1:]
    nt = s // tm
    per = tm // SUBLANES

    def body(dx1_ref, y_ref, yap_ref, ybp_ref, hs_ref, hsh_ref, proj_ref, projh_ref, mod_ref, rows_ref, cw_ref,
             wr_ref, wi_ref, wa_ref, wb_ref, wo_ref,
             dproj_ref, dy_ref, m_ref, dyap_ref, dybp_ref, ya_ref, yb_ref, sm_ref, dwg_ref,
             cvbuf, xbbuf, hsbuf, abuf, dcbuf, dxbuf, al_s, dh_s, lam_s, lnext):
        i = pl.program_id(0)
        first_tile = i == nt - 1

        @pl.when(i == 0)
        def _():
            sm_ref[...] = jnp.zeros(sm_ref.shape, F32)
            dwg_ref[...] = jnp.zeros(dwg_ref.shape, F32)
            zero = jnp.zeros((SUBLANES, d), F32)
            abuf[pl.ds(tm, SUBLANES), :] = zero
            dcbuf[pl.ds(tm, SUBLANES), :] = zero
            dxbuf[pl.ds(tm, SUBLANES), :] = zero
            lnext[...] = zero

        def row(r):
            return rows_ref[r:r + 1, :]

        def tap(r):
            return cw_ref[r:r + 1, :]

        def acc(r, v):
            sm_ref[r:r + 1, :] += v

        keep_halo = jnp.where(first_tile, 0.0, 1.0)
        dy, dgt, dgpost = _postnorm_bwd(y_ref[...], dx1_ref[...], row(R_G_POST_MIX), mod_ref[M_GT_M:M_GT_M + 1, :])
        acc(G_MIX_GT, dgt)
        acc(G_MIX_GPOST, dgpost)
        dyb16 = dy.astype(BF16)
        dy_ref[...] = dyb16
        dm = _dot_nt(dyb16, wo_ref[...])
        sa = _sigmoid(proj_ref[:, 5 * d:6 * d])
        sb = _sigmoid(proj_ref[:, 6 * d:7 * d])
        yap = yap_ref[...]
        ybp = ybp_ref[...]
        m_ref[...] = (sa * yap + sb * ybp).astype(BF16)
        dyap = (dm * sa).astype(BF16)
        dybp = (dm * sb).astype(BF16)
        dyap_ref[...] = dyap
        dybp_ref[...] = dybp
        dproj_ref[:, 5 * d:6 * d] = (dm * yap * sa * (1.0 - sa)).astype(BF16)
        dproj_ref[:, 6 * d:7 * d] = (dm * ybp * sb * (1.0 - sb)).astype(BF16)
        dya = _dot_nt(dyap, wa_ref[...])
        dyb = _dot_nt(dybp, wb_ref[...])
        ba = proj_ref[:, 0:d]
        ca = proj_ref[:, d:2 * d]
        va = proj_ref[:, 2 * d:3 * d]
        cv = ca * va
        cvbuf[pl.ds(0, SUBLANES), :] = keep_halo * (projh_ref[:, d:2 * d] * projh_ref[:, 2 * d:3 * d])
        cvbuf[pl.ds(SUBLANES, tm), :] = cv
        cvm2 = cvbuf[pl.ds(SUBLANES - 2, tm), :]
        cvm1 = cvbuf[pl.ds(SUBLANES - 1, tm), :]
        conv3 = ((row(R_CONV_A_B) + cvm2 * tap(CW_A)) + cvm1 * tap(CW_A + 1)) + cv * tap(CW_A + 2)
        ya_ref[...] = (ba * conv3).astype(BF16)
        dproj_ref[:, 0:d] = (dya * conv3).astype(BF16)
        dc3 = dya * ba
        acc(G_MIX_CAB, _colsum(dc3))
        acc(G_MIX_CAW, _colsum(dc3 * cvm2))
        acc(G_MIX_CAW + 1, _colsum(dc3 * cvm1))
        acc(G_MIX_CAW + 2, _colsum(dc3 * cv))
        dcbuf[pl.ds(0, tm), :] = dc3
        dcv = (dc3 * tap(CW_A + 2) + dcbuf[pl.ds(1, tm), :] * tap(CW_A + 1)) + dcbuf[pl.ds(2, tm), :] * tap(CW_A)
        dcbuf[pl.ds(tm, SUBLANES), :] = dcbuf[pl.ds(0, SUBLANES), :]
        dproj_ref[:, d:2 * d] = (dcv * va).astype(BF16)
        dproj_ref[:, 2 * d:3 * d] = (dcv * ca).astype(BF16)
        xb = proj_ref[:, 3 * d:4 * d]
        gb = proj_ref[:, 4 * d:5 * d]
        xbbuf[pl.ds(0, SUBLANES), :] = keep_halo * projh_ref[:, 3 * d:4 * d]
        xbbuf[pl.ds(SUBLANES, tm), :] = xb
        xm3 = xbbuf[pl.ds(SUBLANES - 3, tm), :]
        xm2 = xbbuf[pl.ds(SUBLANES - 2, tm), :]
        xm1 = xbbuf[pl.ds(SUBLANES - 1, tm), :]
        xc = (((row(R_CONV_B_B) + xm3 * tap(CW_B)) + xm2 * tap(CW_B + 1)) + xm1 * tap(CW_B + 2)) + xb * tap(CW_B + 3)
        lam = row(R_LAMBDA)
        sp = _softplus_neg(lam)
        xcb, r, ig, a, mult = _gates(xc, wr_ref, wi_ref, row(R_B_GATE_R), row(R_B_GATE_I), sp, nh, bw)
        gel, th = _gelu(gb)
        hs = hs_ref[...]
        yb_ref[...] = (hs * gel).astype(BF16)
        dproj_ref[:, 4 * d:5 * d] = (dyb * hs * _gelu_grad(gb, th)).astype(BF16)
        abuf[pl.ds(0, tm), :] = a
        al_s[...] = abuf[pl.ds(1, tm), :]
        abuf[pl.ds(tm, SUBLANES), :] = abuf[pl.ds(0, SUBLANES), :]
        dh_s[...] = dyb * gel

        def blk(j, ln):
            o = pl.multiple_of((per - 1 - j) * SUBLANES, SUBLANES)
            a8, b8 = _scan_block(al_s[pl.ds(o, SUBLANES), :], dh_s[pl.ds(o, SUBLANES), :], reverse=True)
            l8 = b8 + a8 * ln
            lam_s[pl.ds(o, SUBLANES), :] = l8
            return jnp.broadcast_to(l8[0:1, :], (SUBLANES, d))

        lnext[...] = lax.fori_loop(0, per, blk, lnext[...])
        dbb = lam_s[...]
        hsbuf[pl.ds(0, SUBLANES), :] = keep_halo * hsh_ref[...]
        hsbuf[pl.ds(SUBLANES, tm), :] = hs
        da = dbb * hsbuf[pl.ds(SUBLANES - 1, tm), :]
        dmult = dbb * (ig * xc)
        dig = dbb * (mult * xc)
        dxc = dbb * (mult * ig)
        dla = da * a - dmult * (a * a) / mult
        acc(G_MIX_LAM, _colsum(dla * (-LRU_C * r)) * (-_sigmoid(-lam)))
        dzr = (dla * (-LRU_C * sp)) * r * (1.0 - r)
        dzi = dig * ig * (1.0 - ig)
        acc(G_MIX_BR, _colsum(dzr))
        acc(G_MIX_BI, _colsum(dzi))
        dzrb = dzr.astype(BF16)
        dzib = dzi.astype(BF16)
        back = []
        for h in range(nh):
            sl = slice(h * bw, (h + 1) * bw)
            back.append(_dot_nt(dzrb[:, sl], wr_ref[h]) + _dot_nt(dzib[:, sl], wi_ref[h]))
            dwg_ref[0, h] += _dot_tn(xcb[:, sl], dzrb[:, sl])
            dwg_ref[1, h] += _dot_tn(xcb[:, sl], dzib[:, sl])
        dxc = dxc + jnp.concatenate(back, axis=1)
        acc(G_MIX_CBB, _colsum(dxc))
        acc(G_MIX_CBW, _colsum(dxc * xm3))
        acc(G_MIX_CBW + 1, _colsum(dxc * xm2))
        acc(G_MIX_CBW + 2, _colsum(dxc * xm1))
        acc(G_MIX_CBW + 3, _colsum(dxc * xb))
        dxbuf[pl.ds(0, tm), :] = dxc
        dxb = ((dxc * tap(CW_B + 3) + dxbuf[pl.ds(1, tm), :] * tap(CW_B + 2)) + dxbuf[pl.ds(2, tm), :] * tap(CW_B + 1)) + dxbuf[pl.ds(3, tm), :] * tap(CW_B)
        dxbuf[pl.ds(tm, SUBLANES), :] = dxbuf[pl.ds(0, SUBLANES), :]
        dproj_ref[:, 3 * d:4 * d] = dxb.astype(BF16)

    def rev(i):
        return (nt - 1 - i, 0)

    def halo(i):
        return (jnp.maximum((nt - 1 - i) * per - 1, 0), 0)

    tile = pl.BlockSpec((tm, d), rev)
    return pl.pallas_call(
        body,
        name="mixer_core_bwd",
        grid=(nt,),
        in_specs=[
            tile, tile, tile, tile, tile,
            pl.BlockSpec((SUBLANES, d), halo),
            pl.BlockSpec((tm, 7 * d), rev),
            pl.BlockSpec((SUBLANES, 7 * d), halo),
            _resident((None, N_MOD, d), lambda i: (l, 0, 0)),
            _resident((None, N_ROWS, d), lambda i: (l, 0, 0)),
            _resident((None, CW_ROWS, d), lambda i: (l, 0, 0)),
            _resident((None, nh, bw, bw), lambda i: (l, 0, 0, 0)),
            _resident((None, nh, bw, bw), lambda i: (l, 0, 0, 0)),
            _resident((d, d), lambda i: (0, 0)),
            _resident((d, d), lambda i: (0, 0)),
            _resident((d, d), lambda i: (0, 0)),
        ],
        out_specs=(pl.BlockSpec((tm, 7 * d), rev),) + (tile,) * 6 + (
            pl.BlockSpec((G_MIX_ROWS, d), lambda i: (0, 0)), pl.BlockSpec((2, nh, bw, bw), lambda i: (0, 0, 0, 0))),
        out_shape=(jax.ShapeDtypeStruct((s, 7 * d), BF16),) + (jax.ShapeDtypeStruct((s, d), BF16),) * 6 + (
            jax.ShapeDtypeStruct((G_MIX_ROWS, d), F32), jax.ShapeDtypeStruct((2, nh, bw, bw), F32)),
        scratch_shapes=[pltpu.VMEM((tm + SUBLANES, d), F32)] * 6 + [pltpu.VMEM((tm, d), F32)] * 3 + [pltpu.VMEM((SUBLANES, d), F32)],
        compiler_params=_cparams(("arbitrary",), 3 * _nbytes((d, d), BF16) + 3 * _nbytes((tm, 7 * d), F32) + 64 * _nbytes((tm, d), F32)),
    )(dx1, y, yap, ybp, hs, hs, proj, proj, mod, rows, cwf, wr, wi, wa_f, wb_f, wo_f)


def _in_proj_bwd(dproj, x, dx1, mod, rows, win_f, l, tm):
    s, d = x.shape
    nb, _, ci = win_f.shape

    def body(dp_ref, x_ref, dx1_ref, mod_ref, rows_ref, w_ref, dx_ref, sm_ref):
        @pl.when(pl.program_id(0) == 0)
        def _():
            sm_ref[...] = jnp.zeros(sm_ref.shape, F32)

        dh = _dot_nt(dp_ref[:, 0:ci], w_ref[0])
        for j in range(1, nb):
            dh = dh + _dot_nt(dp_ref[:, j * ci:(j + 1) * ci], w_ref[j])
        dxn, dsc, dsh, dg = _prenorm_bwd(x_ref[...], dh, rows_ref[R_G_PRE_MIX:R_G_PRE_MIX + 1, :], mod_ref[M_SC_M:M_SC_M + 1, :])
        dx_ref[...] = dx1_ref[...] + dxn
        for r, v in ((G_IN_SC, dsc), (G_IN_SH, dsh), (G_IN_GPRE, dg)):
            sm_ref[r:r + 1, :] += v

    tile = pl.BlockSpec((tm, d), lambda i: (i, 0))
    return pl.pallas_call(
        body,
        name="in_proj_bwd",
        grid=(s // tm,),
        in_specs=[
            pl.BlockSpec((tm, nb * ci), lambda i: (i, 0)), tile, tile,
            _resident((None, N_MOD, d), lambda i: (l, 0, 0)),
            _resident((None, N_ROWS, d), lambda i: (l, 0, 0)),
            _resident((nb, d, ci), lambda i: (0, 0, 0)),
        ],
        out_specs=(tile, pl.BlockSpec((G_IN_ROWS, d), lambda i: (0, 0))),
        out_shape=(jax.ShapeDtypeStruct((s, d), F32), jax.ShapeDtypeStruct((G_IN_ROWS, d), F32)),
        compiler_params=_cparams(("arbitrary",), _nbytes((nb, d, ci), BF16) + 2 * _nbytes((tm, nb * ci), BF16) + 16 * _nbytes((tm, d), F32)),
    )(dproj, x, dx1, mod, rows, win_f)


def _wgrad(a, b, cols_owned, ts, after):
    s, k1 = a.shape
    k2 = b.shape[1]
    ns = s // ts
    if cols_owned:
        nblk, bk1, bk2 = N_DEV, k1, k2 // N_DEV
        a_spec = pl.BlockSpec((ts, bk1), lambda j, t: (t, 0))
        b_spec = pl.BlockSpec((ts, bk2), lambda j, t: (t, j))
    else:
        bk1, bk2 = min(WGRAD_ROWS, k1), k2
        nblk = k1 // bk1
        a_spec = pl.BlockSpec((ts, bk1), lambda j, t: (t, j))
        b_spec = pl.BlockSpec((ts, bk2), lambda j, t: (t, 0))

    def body(a_ref, b_ref, after_ref, o_ref, acc_ref):
        t = pl.program_id(1)

        @pl.when(t == 0)
        def _():
            acc_ref[...] = jnp.zeros(acc_ref.shape, F32)

        acc_ref[...] += _dot_tn(a_ref[...], b_ref[...])

        @pl.when(t == ns - 1)
        def _():
            o_ref[...] = acc_ref[...].astype(BF16)

    out = pl.pallas_call(
        body,
        name="wgrad",
        grid=(nblk, ns),
        in_specs=[a_spec, b_spec, ANY_SPEC],
        out_specs=pl.BlockSpec((None, bk1, bk2), lambda j, t: (j, 0, 0)),
        out_shape=pltpu.HBM((nblk, bk1, bk2), BF16),
        scratch_shapes=[pltpu.VMEM((bk1, bk2), F32)],
        compiler_params=_cparams(("parallel", "arbitrary"), 4 * _nbytes((bk1, bk2), F32) + 4 * _nbytes((ts, bk1 + bk2), BF16)),
    )(pltpu.with_memory_space_constraint(a, pltpu.HBM), pltpu.with_memory_space_constraint(b, pltpu.HBM), after)
    return out if cols_owned else out.reshape(N_DEV, k1 // N_DEV, k2)


def _adam_update(w, g, m, v):
    m = ADAM_B1 * m + (1.0 - ADAM_B1) * g
    v = ADAM_B2 * v + (1.0 - ADAM_B2) * (g * g)
    m_hat = m / (1.0 - ADAM_B1 ** ADAM_STEP)
    v_hat = v / (1.0 - ADAM_B2 ** ADAM_STEP)
    delta = -ADAM_LR * (m_hat / (jnp.sqrt(v_hat) + ADAM_EPS) + ADAM_WD * w)
    return delta, m, v


def _sum_adamw(recv, w, m, v, tr, after):
    nl, ra, cb = w.shape
    assert nl == len(recv) == 2

    def body(r0_ref, r1_ref, w_ref, m_ref, v_ref, after_ref, g_ref, d_ref, nm_ref, nv_ref):
        def total(r_ref):
            g = r_ref[0].astype(F32)
            for j in range(1, N_DEV):
                g = g + r_ref[j].astype(F32)
            return g

        g = jnp.where(pl.program_id(0) == 0, total(r0_ref), total(r1_ref))
        g_ref[...] = g
        d_ref[...], nm_ref[...], nv_ref[...] = _adam_update(w_ref[...], g, m_ref[...], v_ref[...])

    blk = pl.BlockSpec((None, tr, cb), lambda l, i: (l, i, 0))
    return pl.pallas_call(
        body,
        name="sum_adamw",
        grid=(nl, ra // tr),
        in_specs=[pl.BlockSpec((N_DEV, tr, cb), lambda l, i: (0, i * (1 - l), 0)),
                  pl.BlockSpec((N_DEV, tr, cb), lambda l, i: (0, i * l, 0)), blk, blk, blk, ANY_SPEC],
        out_specs=(blk,) * 4,
        out_shape=(jax.ShapeDtypeStruct((nl, ra, cb), F32),) * 4,
        compiler_params=_cparams(("arbitrary", "arbitrary"), 4 * _nbytes((N_DEV, tr, cb), BF16) + 16 * _nbytes((tr, cb), F32)),
    )(recv[0], recv[1], w, m, v, after)


def _adamw(w, g, m, v):
    def body(w_ref, g_ref, m_ref, v_ref, d_ref, nm_ref, nv_ref):
        d_ref[...], nm_ref[...], nv_ref[...] = _adam_update(w_ref[...], g_ref[...], m_ref[...], v_ref[...])

    return pl.pallas_call(
        body,
        name="adamw",
        in_specs=[VMEM_SPEC] * 4,
        out_specs=(VMEM_SPEC,) * 3,
        out_shape=(jax.ShapeDtypeStruct(w.shape, F32),) * 3,
        compiler_params=_cparams(vmem=10 * _nbytes(w.shape, F32)),
    )(w, g, m, v)


def _adamw_tiled(w, g, m, v, tr):
    nl, ra, cb = w.shape

    def body(w_ref, g_ref, m_ref, v_ref, d_ref, nm_ref, nv_ref):
        d_ref[...], nm_ref[...], nv_ref[...] = _adam_update(w_ref[...], g_ref[...], m_ref[...], v_ref[...])

    blk = pl.BlockSpec((None, tr, cb), lambda l, i: (l, i, 0))
    return pl.pallas_call(
        body,
        name="adamw_tiled",
        grid=(nl, ra // tr),
        in_specs=[blk] * 4,
        out_specs=(blk,) * 3,
        out_shape=(jax.ShapeDtypeStruct((nl, ra, cb), F32),) * 3,
        compiler_params=_cparams(("parallel", "parallel")),
    )(w, g, m, v)


def _token_tile(s):
    return min(256, max(SUBLANES * 2, s // 4))


def kernel(x, c, w_mod, b_mod, g_pre_mix, g_post_mix, w_in, conv_a_w, conv_a_b, w_a_out, conv_b_w, conv_b_b, w_gate_r, b_gate_r, w_gate_i, b_gate_i, lru_lambda, w_b_out, w_o, g_pre_mlp, g_post_mlp, w_mlp_up, w_mlp_down, loss_target, m_w_mod, m_b_mod, m_g_pre_mix, m_g_post_mix, m_w_in, m_conv_a_w, m_conv_a_b, m_w_a_out, m_conv_b_w, m_conv_b_b, m_w_gate_r, m_b_gate_r, m_w_gate_i, m_b_gate_i, m_lru_lambda, m_w_b_out, m_w_o, m_g_pre_mlp, m_g_post_mlp, m_w_mlp_up, m_w_mlp_down, v_w_mod, v_b_mod, v_g_pre_mix, v_g_post_mix, v_w_in, v_conv_a_w, v_conv_a_b, v_w_a_out, v_conv_b_w, v_conv_b_b, v_w_gate_r, v_b_gate_r, v_w_gate_i, v_b_gate_i, v_lru_lambda, v_w_b_out, v_w_o, v_g_pre_mlp, v_g_post_mlp, v_w_mlp_up, v_w_mlp_down):
    nl = w_mod.shape[0]
    s, d = x.shape[1], x.shape[2]
    nh, bw = w_gate_r.shape[1], w_gate_r.shape[2]
    cwid = conv_a_w.shape[2]
    tm = _token_tile(s)
    ts = s
    _, me = _my_position()
    xs = x.reshape(s, d)
    target = loss_target.reshape(s, d)

    vec_names = (g_pre_mix, g_post_mix, conv_a_b, conv_b_b, b_gate_r, b_gate_i, lru_lambda, g_pre_mlp, g_post_mlp)
    rows = jnp.concatenate([jnp.stack(vec_names, axis=1), jnp.zeros((nl, N_ROWS - len(vec_names), d), F32)], axis=1)
    cw = jnp.concatenate([conv_a_w, conv_b_w, jnp.zeros((nl, CW_ROWS - 7, cwid), F32)], axis=1)

    w16 = {"w_in": w_in.astype(BF16), "w_a_out": w_a_out.astype(BF16), "w_b_out": w_b_out.astype(BF16), "w_o": w_o.astype(BF16),
           "w_mlp_up": w_mlp_up.astype(BF16), "w_mlp_down": w_mlp_down.astype(BF16)}
    groups = (("in", ("w_in",)), ("mix", ("w_a_out", "w_b_out", "w_o")), ("mlp", ("w_mlp_up", "w_mlp_down")))
    mod, cact, cwf = _prep_small(c, w_mod, b_mod, cw)
    mod = mod.reshape(nl, N_MOD, d)
    w16, _ = lax.optimization_barrier((w16, mod))
    gathers = {}
    token = jnp.zeros((), F32)
    for l in range(nl):
        for gname, members in groups:
            gathers[l, gname] = _gather2_start([w16[n][l] for n in members], f"gather_start_{gname}{l}")
            token = token + gathers[l, gname][4][0, 0]
    rows = rows + token
    wr = w_gate_r.astype(BF16)
    wi = w_gate_i.astype(BF16)

    def gathered(l, gname, after):
        forwarded = _gather2_forward(gathers[l, gname], after, f"gather_forward_{gname}{l}")
        return _gather2_wait(forwarded, after, f"gather_wait_{gname}{l}")

    saved = []
    weights = []
    xin = xs
    for l in range(nl):
        (win_f,) = gathered(l, "in", mod if l == 0 else xin)
        proj, h = _in_proj_fwd(xin, mod, rows, win_f, l, tm)
        wa_f, wb_f, wo_f = (w.reshape(d, d) for w in gathered(l, "mix", proj))
        x1, hs, yap, ybp, y = _mixer_core_fwd(proj, xin, mod, rows, cwf, wr, wi, wa_f, wb_f, wo_f, l, tm)
        wup_f, wdn_f = gathered(l, "mlp", x1)
        wdn_f = wdn_f.reshape(-1, d)
        x2, ru, y2, h2 = _mlp_fwd(x1, mod, rows, wup_f, wdn_f, l, tm)
        saved.append((xin, proj, h, x1, hs, yap, ybp, y, ru, y2, h2))
        weights.append((win_f, wa_f, wb_f, wo_f, wup_f, wdn_f))
        xin = x2
    loss_part, dx = _loss_fwd_bwd(xin, target, tm)
    loss = lax.psum(loss_part, ("x", "y", "c"))

    scatters = {}
    small = [None] * nl
    gate_parts = [None] * nl

    def scatter(l, gname, parts, rows):
        scatters[l, gname] = _exchange_start(parts, False, f"scatter_start_{gname}{l}")
        return rows + scatters[l, gname][4][0, 0]

    for l in reversed(range(nl)):
        xin, proj, h, x1, hs, yap, ybp, y, ru, y2, h2 = saved[l]
        win_f, wa_f, wb_f, wo_f, wup_f, wdn_f = weights[l]
        dx1, dy2, dup, act, sm_mlp = _mlp_bwd(dx, x1, y2, ru, mod, rows, wup_f, wdn_f, l, tm)
        g_up = _wgrad(h2, dup, True, ts, dx1)
        g_dn = _wgrad(act, dy2, False, ts, g_up)
        rows = scatter(l, "mlp", [g_up, g_dn], rows)
        dproj, dy, m, dyap, dybp, ya, yb, sm_mix, dwg = _mixer_core_bwd(dx1, y, yap, ybp, hs, proj, mod, rows, cwf, wr, wi, wa_f, wb_f, wo_f, l, tm // 2)
        gate_parts[l] = _exchange_start([dwg.astype(BF16)], True, f"gates_start{l}")
        rows = rows + gate_parts[l][4][0, 0]
        rows = scatter(l, "in", [_wgrad(h, dproj, True, ts, gate_parts[l][4])], rows)
        g_a = _wgrad(ya, dyap, False, ts, scatters[l, "in"][4])
        g_b = _wgrad(yb, dybp, False, ts, g_a)
        g_o = _wgrad(m, dy, False, ts, g_b)
        rows = scatter(l, "mix", [g_a, g_b, g_o], rows)
        dx, sm_in = _in_proj_bwd(dproj, xin, dx1, mod, rows, win_f, l, tm)
        small[l] = jnp.concatenate([sm_mlp, sm_mix, sm_in], axis=0)
    grad_x = dx.reshape(x.shape)

    recv = {}
    big = {}
    moments = {"w_in": (w_in, m_w_in, v_w_in), "w_mlp_up": (w_mlp_up, m_w_mlp_up, v_w_mlp_up), "w_a_out": (w_a_out, m_w_a_out, v_w_a_out),
               "w_b_out": (w_b_out, m_w_b_out, v_w_b_out), "w_o": (w_o, m_w_o, v_w_o), "w_mlp_down": (w_mlp_down, m_w_mlp_down, v_w_mlp_down)}

    def collect(l, gname, after):
        for n, zone in zip(dict(groups)[gname], _exchange_wait(scatters[l, gname], after, False, f"scatter_wait_{gname}{l}")):
            recv[n, l] = zone

    def update(name, after):
        w, m_, v_ = moments[name]
        big[name] = _sum_adamw([recv[name, l] for l in range(nl)], w, m_, v_, min(256, w.shape[1]), after)
        return big[name][1]

    for l, gname in ((1, "mlp"), (1, "in"), (1, "mix"), (0, "mlp")):
        collect(l, gname, dx)
    early = update("w_mlp_down", update("w_mlp_up", dx))

    lrows = jnp.concatenate(small, axis=0)

    def lrow(a, l, r):
        return a[l * G_LAYER_ROWS + r]

    dm = jnp.stack([jnp.concatenate([lrow(lrows, l, G_MLP_ROWS + G_MIX_ROWS + G_IN_SH), lrow(lrows, l, G_MLP_ROWS + G_MIX_ROWS + G_IN_SC),
                                     lrow(lrows, l, G_MLP_ROWS + G_MIX_GT), lrow(lrows, l, G_MLP_SH), lrow(lrows, l, G_MLP_SC),
                                     lrow(lrows, l, G_MLP_GT)]) for l in range(nl)])
    dm8 = jnp.concatenate([dm[:, None, :], jnp.zeros((nl, SUBLANES - 1, N_MOD * d), F32)], axis=1)
    srows, g_w_mod = _reduce_small(lrows, dm8, cact, early)
    sgates = _sum_gathered([_exchange_wait(gate_parts[l], srows, True, f"gates_wait{l}")[0] for l in range(nl)])
    for gname in ("in", "mix"):
        collect(0, gname, srows)
    for name in ("w_in", "w_a_out", "w_b_out", "w_o"):
        update(name, srows)

    def srow(l, r):
        return lrow(srows, l, r)

    def per_layer(r):
        return jnp.stack([srow(l, r) for l in range(nl)])

    mix0 = G_MLP_ROWS
    in0 = G_MLP_ROWS + G_MIX_ROWS
    g_b_mod = jnp.stack([jnp.concatenate([srow(l, in0 + G_IN_SH), srow(l, in0 + G_IN_SC), srow(l, mix0 + G_MIX_GT),
                                          srow(l, G_MLP_SH), srow(l, G_MLP_SC), srow(l, G_MLP_GT)]) for l in range(nl)])
    conv_a_full = jnp.stack([jnp.stack([srow(l, mix0 + G_MIX_CAW + k) for k in range(3)]) for l in range(nl)])
    conv_b_full = jnp.stack([jnp.stack([srow(l, mix0 + G_MIX_CBW + k) for k in range(4)]) for l in range(nl)])
    grads = {
        "b_mod": g_b_mod,
        "g_pre_mix": per_layer(in0 + G_IN_GPRE),
        "g_post_mix": per_layer(mix0 + G_MIX_GPOST),
        "conv_a_w": lax.dynamic_slice_in_dim(conv_a_full, me * cwid, cwid, axis=2),
        "conv_a_b": per_layer(mix0 + G_MIX_CAB),
        "conv_b_w": lax.dynamic_slice_in_dim(conv_b_full, me * cwid, cwid, axis=2),
        "conv_b_b": per_layer(mix0 + G_MIX_CBB),
        "w_gate_r": sgates[:, 0],
        "b_gate_r": per_layer(mix0 + G_MIX_BR),
        "w_gate_i": sgates[:, 1],
        "b_gate_i": per_layer(mix0 + G_MIX_BI),
        "lru_lambda": per_layer(mix0 + G_MIX_LAM),
        "g_pre_mlp": per_layer(G_MLP_GPRE),
        "g_post_mlp": per_layer(G_MLP_GPOST),
    }
    params = {
        "b_mod": (b_mod, m_b_mod, v_b_mod), "g_pre_mix": (g_pre_mix, m_g_pre_mix, v_g_pre_mix), "g_post_mix": (g_post_mix, m_g_post_mix, v_g_post_mix),
        "conv_a_w": (conv_a_w, m_conv_a_w, v_conv_a_w), "conv_a_b": (conv_a_b, m_conv_a_b, v_conv_a_b),
        "conv_b_w": (conv_b_w, m_conv_b_w, v_conv_b_w), "conv_b_b": (conv_b_b, m_conv_b_b, v_conv_b_b),
        "w_gate_r": (w_gate_r, m_w_gate_r, v_w_gate_r), "b_gate_r": (b_gate_r, m_b_gate_r, v_b_gate_r),
        "w_gate_i": (w_gate_i, m_w_gate_i, v_w_gate_i), "b_gate_i": (b_gate_i, m_b_gate_i, v_b_gate_i),
        "lru_lambda": (lru_lambda, m_lru_lambda, v_lru_lambda), "g_pre_mlp": (g_pre_mlp, m_g_pre_mlp, v_g_pre_mlp),
        "g_post_mlp": (g_post_mlp, m_g_post_mlp, v_g_post_mlp),
    }
    out = {}
    for name, g in grads.items():
        w, m_, v_ = params[name]
        flat = (-1, w.shape[-1])
        dl, nm, nv = _adamw(w.reshape(flat), g.reshape(flat), m_.reshape(flat), v_.reshape(flat))
        out[name] = (g.reshape(w.shape), dl.reshape(w.shape), nm.reshape(w.shape), nv.reshape(w.shape))
    out["w_mod"] = (g_w_mod,) + tuple(_adamw_tiled(w_mod, g_w_mod, m_w_mod, v_w_mod, min(128, d)))
    out.update(big)

    order = ("w_mod", "b_mod", "g_pre_mix", "g_post_mix", "w_in", "conv_a_w", "conv_a_b", "w_a_out", "conv_b_w", "conv_b_b", "w_gate_r", "b_gate_r",
             "w_gate_i", "b_gate_i", "lru_lambda", "w_b_out", "w_o", "g_pre_mlp", "g_post_mlp", "w_mlp_up", "w_mlp_down")
    return (loss, grad_x) + tuple(out[n][0] for n in order) + tuple(out[n][1] for n in order) + tuple(out[n][2] for n in order) + tuple(out[n][3] for n in order)
```

```python
import functools

import jax
import jax.numpy as jnp
from jax import lax
from jax.experimental import pallas as pl
from jax.experimental.pallas import tpu as pltpu

F32, BF16 = jnp.float32, jnp.bfloat16
EPS = 1e-6
LRU_C = 8.0
N_DEV = 8
N_MOD = 6
SUBLANES = 8
VMEM_BUDGET = 56 * 1024 * 1024
WGRAD_ROWS = 512
ADAM_LR, ADAM_B1, ADAM_B2, ADAM_EPS, ADAM_WD, ADAM_STEP = 0.001, 0.9, 0.999, 1e-08, 0.01, 10
MESH = pl.DeviceIdType.MESH
VMEM_SPEC = pl.BlockSpec(memory_space=pltpu.VMEM)
ANY_SPEC = pl.BlockSpec(memory_space=pl.ANY)
HBM_SPEC = pl.BlockSpec(memory_space=pltpu.HBM)
SEM_SPEC = pl.BlockSpec(memory_space=pltpu.SEMAPHORE)
SIDE_EFFECT = pltpu.SideEffectType.DATAFLOW_SIDE_EFFECTING

R_G_PRE_MIX, R_G_POST_MIX, R_CONV_A_B, R_CONV_B_B, R_B_GATE_R, R_B_GATE_I, R_LAMBDA, R_G_PRE_MLP, R_G_POST_MLP = range(9)
N_ROWS = 16
M_SH_M, M_SC_M, M_GT_M, M_SH_F, M_SC_F, M_GT_F = range(6)
CW_A, CW_B, CW_ROWS = 0, 3, 8
G_MLP_GT, G_MLP_GPOST, G_MLP_SC, G_MLP_SH, G_MLP_GPRE, G_MLP_ROWS = 0, 1, 2, 3, 4, 8
(G_MIX_GT, G_MIX_GPOST, G_MIX_CAB, G_MIX_CAW, G_MIX_CBB, G_MIX_CBW, G_MIX_BR, G_MIX_BI, G_MIX_LAM) = 0, 1, 2, 3, 6, 7, 11, 12, 13
G_MIX_ROWS = 16
G_IN_SC, G_IN_SH, G_IN_GPRE, G_IN_ROWS = 0, 1, 2, 8
G_LAYER_ROWS = G_MLP_ROWS + G_MIX_ROWS + G_IN_ROWS


def _cparams(dims=None, vmem=None):
    kw = {}
    if dims is not None:
        kw["dimension_semantics"] = dims
    if vmem is not None:
        kw["vmem_limit_bytes"] = int(min(max(vmem, 16 * 1024 * 1024), VMEM_BUDGET))
    return pltpu.CompilerParams(**kw)


def _nbytes(shape, dtype):
    n = 1
    for s in shape:
        n *= s
    return n * jnp.dtype(dtype).itemsize


def _hbm(*arrays):
    return tuple(pltpu.with_memory_space_constraint(a, pltpu.HBM) for a in arrays)


def _resident(block, index_map):
    return pl.BlockSpec(block, index_map, pipeline_mode=pl.Buffered(1))


def _my_position():
    x, y, c = lax.axis_index("x"), lax.axis_index("y"), lax.axis_index("c")
    return (x, y, c), 4 * x + 2 * y + c


def _peer(pos, k):
    x, y, c = pos
    px = 1 - x if k & 4 else x
    py = 1 - y if k & 2 else y
    pc = 1 - c if k & 1 else c
    return (px, py, pc), 4 * px + 2 * py + pc


def _remote(src, dst, ssem, rsem, peer):
    return pltpu.make_async_remote_copy(src_ref=src, dst_ref=dst, send_sem=ssem, recv_sem=rsem, device_id=peer, device_id_type=MESH)


def _dot(a, b):
    return jnp.dot(a, b, preferred_element_type=F32)


def _dot_nt(a, b):
    return lax.dot_general(a, b, (((1,), (1,)), ((), ())), preferred_element_type=F32)


def _dot_tn(a, b):
    return lax.dot_general(a, b, (((0,), (0,)), ((), ())), preferred_element_type=F32)


def _colsum(v):
    return jnp.sum(v, axis=0, keepdims=True)


def _sigmoid(v):
    return jax.nn.sigmoid(v)


def _gelu(v):
    k = 0.7978845608028654
    t = jnp.tanh(k * (v + 0.044715 * (v * v * v)))
    return 0.5 * v * (1.0 + t), t


def _gelu_grad(v, t):
    k = 0.7978845608028654
    return 0.5 * (1.0 + t) + 0.5 * v * (1.0 - t * t) * (k * (1.0 + 3.0 * 0.044715 * v * v))


def _expm1(v):
    u = jnp.exp(v)
    um1 = u - 1.0
    q = um1 * v / jnp.log(u)
    return jnp.where(um1 == 0.0, v, jnp.where(um1 == -1.0, -1.0, q))


def _softplus_neg(lam):
    z = -lam
    u = jnp.exp(-jnp.abs(z))
    w = 1.0 + u
    l1p = jnp.where(w == 1.0, u, jnp.log(w) * u / (w - 1.0))
    return jnp.maximum(z, 0.0) + l1p


def _rms(v):
    return lax.rsqrt(jnp.mean(v * v, axis=-1, keepdims=True) + EPS)


def _prenorm_bwd(xv, dh, g, sc):
    r = _rms(xv)
    xn = xv * r
    n = xn * g
    dsc = _colsum(dh * n)
    dsh = _colsum(dh)
    dn = dh * (1.0 + sc)
    dg = _colsum(dn * xn)
    dxn = dn * g
    dx = r * (dxn - xn * jnp.mean(dxn * xn, axis=-1, keepdims=True))
    return dx, dsc, dsh, dg


def _postnorm_bwd(yv, dout, g, gt):
    r = _rms(yv)
    yn = yv * r
    dgt = _colsum(dout * (yn * g))
    dn = dout * gt
    dg = _colsum(dn * yn)
    dyn = dn * g
    dy = r * (dyn - yn * jnp.mean(dyn * yn, axis=-1, keepdims=True))
    return dy, dgt, dg


def _gates(xc, wr_ref, wi_ref, b_r, b_i, sp, nh, bw):
    xcb = xc.astype(BF16)
    zr = jnp.concatenate([_dot(xcb[:, h * bw:(h + 1) * bw], wr_ref[h]) for h in range(nh)], axis=1) + b_r
    zi = jnp.concatenate([_dot(xcb[:, h * bw:(h + 1) * bw], wi_ref[h]) for h in range(nh)], axis=1) + b_i
    r = _sigmoid(zr)
    ig = _sigmoid(zi)
    la = (-LRU_C * r) * sp
    a = jnp.exp(la)
    mult = jnp.sqrt(-_expm1(2.0 * la))
    return xcb, r, ig, a, mult


def _scan_block(a8, b8, reverse):
    row = lax.broadcasted_iota(jnp.int32, a8.shape, 0)
    for s in (1, 2, 4):
        if reverse:
            keep = row < SUBLANES - s
            a_sh = pltpu.roll(a8, SUBLANES - s, 0)
            b_sh = pltpu.roll(b8, SUBLANES - s, 0)
        else:
            keep = row >= s
            a_sh = pltpu.roll(a8, s, 0)
            b_sh = pltpu.roll(b8, s, 0)
        b8 = b8 + a8 * jnp.where(keep, b_sh, 0.0)
        a8 = a8 * jnp.where(keep, a_sh, 1.0)
    return a8, b8


def _prep_small(c, w_mod, b_mod, cw):
    d = c.shape[1]
    cm = w_mod.shape[2]
    cwid = cw.shape[2]
    nl = w_mod.shape[0]

    def body(c_ref, wm_ref, bm_ref, cw_ref, mod_ref, cact_ref, cwf_ref, cbuf, pbuf, rbuf, ssem, rsem, lsem):
        pos, me = _my_position()
        me8 = pl.multiple_of(me * SUBLANES, SUBLANES)
        cbuf[pl.ds(me8, SUBLANES), :] = jnp.broadcast_to(c_ref[...], (SUBLANES, d))
        own_cw = pltpu.make_async_copy(cw_ref, cwf_ref.at[:, :, pl.ds(me * cwid, cwid)], lsem.at[0])
        own_cw.start()
        first = []
        for k in range(1, N_DEV):
            peer, _ = _peer(pos, k)
            rows = cbuf.at[pl.ds(me8, SUBLANES), :]
            first.append(_remote(rows, rows, ssem.at[0, k - 1], rsem.at[0, k - 1], peer))
            first.append(_remote(cw_ref, cwf_ref.at[:, :, pl.ds(me * cwid, cwid)], ssem.at[1, k - 1], rsem.at[1, k - 1], peer))
        for cp in first:
            cp.start()
        for k in range(1, N_DEV):
            peer, pj = _peer(pos, k)
            pj8 = pl.multiple_of(pj * SUBLANES, SUBLANES)
            rows = cbuf.at[pl.ds(pj8, SUBLANES), :]
            _remote(rows, rows, ssem.at[0, k - 1], rsem.at[0, k - 1], peer).wait_recv()
        cv = cbuf[...]
        cact = cv * _sigmoid(cv)
        cact_ref[...] = cact
        cb = cact.astype(BF16)
        for l in range(nl):
            pbuf[l] = _dot(cb, wm_ref[l].astype(BF16))
        own_p = pltpu.make_async_copy(pbuf.at[:, pl.ds(me8, SUBLANES), :], rbuf.at[me], lsem.at[1])
        own_p.start()
        second = []
        for k in range(1, N_DEV):
            peer, pj = _peer(pos, k)
            pj8 = pl.multiple_of(pj * SUBLANES, SUBLANES)
            second.append(_remote(pbuf.at[:, pl.ds(pj8, SUBLANES), :], rbuf.at[me], ssem.at[2, k - 1], rsem.at[2, k - 1], peer))
        for cp in second:
            cp.start()
        for k in range(1, N_DEV):
            peer, pj = _peer(pos, k)
            _remote(pbuf.at[:, pl.ds(0, SUBLANES), :], rbuf.at[pj], ssem.at[2, k - 1], rsem.at[2, k - 1], peer).wait_recv()
            _remote(cw_ref, cwf_ref.at[:, :, pl.ds(pj * cwid, cwid)], ssem.at[1, k - 1], rsem.at[1, k - 1], peer).wait_recv()
        own_p.wait()
        own_cw.wait()
        for l in range(nl):
            for j in range(N_DEV):
                mod_ref[l:l + 1, j * cm:(j + 1) * cm] = rbuf[j, l, 0:1, :] + bm_ref[l:l + 1, j * cm:(j + 1) * cm]
        for cp in first + second:
            cp.wait_send()

    return pl.pallas_call(
        body,
        name="prep_small",
        out_shape=(
            jax.ShapeDtypeStruct((nl, N_MOD * d), F32),
            jax.ShapeDtypeStruct((N_DEV * SUBLANES, d), F32),
            jax.ShapeDtypeStruct((nl, CW_ROWS, d), F32),
        ),
        in_specs=[VMEM_SPEC] * 4,
        out_specs=(VMEM_SPEC,) * 3,
        scratch_shapes=[
            pltpu.VMEM((N_DEV * SUBLANES, d), F32),
            pltpu.VMEM((nl, N_DEV * SUBLANES, cm), F32),
            pltpu.VMEM((N_DEV, nl, SUBLANES, cm), F32),
            pltpu.SemaphoreType.DMA((3, N_DEV - 1)),
            pltpu.SemaphoreType.DMA((3, N_DEV - 1)),
            pltpu.SemaphoreType.DMA((2,)),
        ],
        compiler_params=_cparams(vmem=3 * _nbytes(w_mod.shape, F32)),
    )(c, w_mod, b_mod, cw)


def _exchange_start(parts, gather, name):
    n = len(parts)
    lands = [lax.empty(((N_DEV,) + tuple(p.shape)) if gather else tuple(p.shape), p.dtype) for p in parts]

    def body(*refs):
        ins, lnd = refs[:n], refs[n:2 * n]
        ssem, rsem, token = refs[2 * n], refs[2 * n + 1], refs[-1]
        pos, me = _my_position()
        for k in range(1, N_DEV):
            peer, pj = _peer(pos, k)
            for t in range(n):
                src = ins[t] if gather else ins[t].at[pj]
                q = t * (N_DEV - 1) + k - 1
                _remote(src, lnd[t].at[me], ssem.at[q], rsem.at[q], peer).start()
        token[...] = jnp.zeros(token.shape, F32)

    out = pl.pallas_call(
        body,
        name=name,
        out_shape=(pltpu.SemaphoreType.DMA((n * (N_DEV - 1),)), pltpu.SemaphoreType.DMA((n * (N_DEV - 1),)))
        + tuple(pltpu.HBM(p.shape, p.dtype) for p in parts) + tuple(pltpu.HBM(p.shape, p.dtype) for p in lands)
        + (jax.ShapeDtypeStruct((SUBLANES, 128), F32),),
        in_specs=[HBM_SPEC] * (2 * n),
        out_specs=(SEM_SPEC, SEM_SPEC) + (HBM_SPEC,) * (2 * n) + (VMEM_SPEC,),
        input_output_aliases={i: 2 + i for i in range(2 * n)},
        compiler_params=pltpu.CompilerParams(has_side_effects=SIDE_EFFECT),
    )(*[pltpu.with_memory_space_constraint(p, pltpu.HBM) for p in list(parts) + lands])
    return out[0], out[1], list(out[2:2 + n]), list(out[2 + n:2 + 2 * n]), out[-1]


def _exchange_wait(started, after, gather, name):
    ssem, rsem, parts, lands, _ = started
    n = len(parts)

    def body(*refs):
        ins, lnd = refs[:n], refs[n:2 * n]
        ssem_ref, rsem_ref = refs[2 * n], refs[2 * n + 1]
        stage, lsem = refs[-1 - n:-1], refs[-1]
        pos, me = _my_position()
        load = []
        for t in range(n):
            src = ins[t] if gather else ins[t].at[me]
            load.append(pltpu.make_async_copy(src, stage[t], lsem.at[t]))
            load[-1].start()
        store = []
        for t in range(n):
            load[t].wait()
            store.append(pltpu.make_async_copy(stage[t], lnd[t].at[me], lsem.at[t]))
            store[-1].start()
        for k in range(1, N_DEV):
            peer, pj = _peer(pos, k)
            for t in range(n):
                src = ins[t] if gather else ins[t].at[pj]
                q = t * (N_DEV - 1) + k - 1
                _remote(src, lnd[t].at[me], ssem_ref.at[q], rsem_ref.at[q], peer).wait_send()
                _remote(src, lnd[t].at[pj], ssem_ref.at[q], rsem_ref.at[q], peer).wait_recv()
        for cp in store:
            cp.wait()

    out = pl.pallas_call(
        body,
        name=name,
        out_shape=tuple(pltpu.HBM(p.shape, p.dtype) for p in parts) + tuple(pltpu.HBM(p.shape, p.dtype) for p in lands),
        in_specs=[HBM_SPEC] * (2 * n) + [SEM_SPEC, SEM_SPEC, ANY_SPEC],
        out_specs=(HBM_SPEC,) * (2 * n),
        input_output_aliases={i: i for i in range(2 * n)},
        scratch_shapes=[pltpu.VMEM(tuple(z.shape[1:]), z.dtype) for z in lands] + [pltpu.SemaphoreType.DMA((n,))],
        compiler_params=pltpu.CompilerParams(has_side_effects=SIDE_EFFECT),
    )(*parts, *lands, ssem, rsem, after)
    return list(out[n:])


CHIP_PEERS = 3


def _chip_peer(pos, k, core):
    x, y, _ = pos
    px = 1 - x if k & 2 else x
    py = 1 - y if k & 1 else y
    return (px, py, core), 4 * px + 2 * py + core


def _gather2_start(parts, name):
    n = len(parts)
    lands = [lax.empty((N_DEV,) + tuple(p.shape), p.dtype) for p in parts]
    per = CHIP_PEERS + 1

    def body(*refs):
        ins, lnd = refs[:n], refs[n:2 * n]
        ssem, rsem, token = refs[2 * n], refs[2 * n + 1], refs[-1]
        pos, me = _my_position()
        sibling = (pos[0], pos[1], 1 - pos[2])
        for t in range(n):
            _remote(ins[t], lnd[t].at[me], ssem.at[per * t], rsem.at[per * t], sibling).start()
        for k in range(1, per):
            peer, _ = _chip_peer(pos, k, pos[2])
            for t in range(n):
                _remote(ins[t], lnd[t].at[me], ssem.at[per * t + k], rsem.at[per * t + k], peer).start()
        token[...] = jnp.zeros(token.shape, F32)

    out = pl.pallas_call(
        body,
        name=name,
        out_shape=(pltpu.SemaphoreType.DMA((n * per,)), pltpu.SemaphoreType.DMA((n * per,)))
        + tuple(pltpu.HBM(p.shape, p.dtype) for p in parts) + tuple(pltpu.HBM(p.shape, p.dtype) for p in lands)
        + (jax.ShapeDtypeStruct((SUBLANES, 128), F32),),
        in_specs=[HBM_SPEC] * (2 * n),
        out_specs=(SEM_SPEC, SEM_SPEC) + (HBM_SPEC,) * (2 * n) + (VMEM_SPEC,),
        input_output_aliases={i: 2 + i for i in range(2 * n)},
        compiler_params=pltpu.CompilerParams(has_side_effects=SIDE_EFFECT),
    )(*[pltpu.with_memory_space_constraint(p, pltpu.HBM) for p in list(parts) + lands])
    return out[0], out[1], list(out[2:2 + n]), list(out[2 + n:2 + 2 * n]), out[-1]


def _gather2_forward(started, after, name):
    ssem, rsem, parts, lands, token = started
    n = len(lands)
    per = CHIP_PEERS + 1

    def body(*refs):
        lnd = refs[:n]
        rsem_a, fsend, frecv = refs[n], refs[2 * n + 2], refs[2 * n + 3]
        pos, _ = _my_position()
        sibling = (pos[0], pos[1], 1 - pos[2])
        for k in range(1, per):
            peer, pk = _chip_peer(pos, k, pos[2])
            for t in range(n):
                block = lnd[t].at[pk]
                _remote(block, block, rsem_a.at[per * t + k], rsem_a.at[per * t + k], peer).wait_recv()
                q = CHIP_PEERS * t + k - 1
                _remote(block, block, fsend.at[q], frecv.at[q], sibling).start()

    out = pl.pallas_call(
        body,
        name=name,
        out_shape=tuple(pltpu.HBM(p.shape, p.dtype) for p in lands)
        + (pltpu.SemaphoreType.DMA((n * CHIP_PEERS,)), pltpu.SemaphoreType.DMA((n * CHIP_PEERS,))),
        in_specs=[HBM_SPEC] * n + [SEM_SPEC, ANY_SPEC],
        out_specs=(HBM_SPEC,) * n + (SEM_SPEC, SEM_SPEC),
        input_output_aliases={i: i for i in range(n)},
        compiler_params=pltpu.CompilerParams(has_side_effects=SIDE_EFFECT),
    )(*lands, rsem, after)
    return ssem, rsem, parts, list(out[:n]), token, out[n], out[n + 1]


def _gather2_wait(forwarded, after, name):
    ssem, rsem, parts, lands, _, fsend, frecv = forwarded
    n = len(parts)
    per = CHIP_PEERS + 1

    def body(*refs):
        ins, lnd = refs[:n], refs[n:2 * n]
        ssem_a, rsem_a, fs, fr = refs[2 * n:2 * n + 4]
        stage, lsem = refs[-1 - n:-1], refs[-1]
        pos, me = _my_position()
        sibling = (pos[0], pos[1], 1 - pos[2])
        sib = 4 * pos[0] + 2 * pos[1] + 1 - pos[2]
        load = []
        for t in range(n):
            load.append(pltpu.make_async_copy(ins[t], stage[t], lsem.at[t]))
            load[-1].start()
        store = []
        for t in range(n):
            load[t].wait()
            store.append(pltpu.make_async_copy(stage[t], lnd[t].at[me], lsem.at[t]))
            store[-1].start()
        for t in range(n):
            _remote(ins[t], lnd[t].at[me], ssem_a.at[per * t], rsem_a.at[per * t], sibling).wait_send()
            _remote(ins[t], lnd[t].at[sib], ssem_a.at[per * t], rsem_a.at[per * t], sibling).wait_recv()
        for k in range(1, per):
            peer, pk = _chip_peer(pos, k, pos[2])
            _, qk = _chip_peer(pos, k, 1 - pos[2])
            for t in range(n):
                q = CHIP_PEERS * t + k - 1
                _remote(ins[t], lnd[t].at[me], ssem_a.at[per * t + k], rsem_a.at[per * t + k], peer).wait_send()
                _remote(lnd[t].at[pk], lnd[t].at[pk], fs.at[q], fr.at[q], sibling).wait_send()
                _remote(lnd[t].at[qk], lnd[t].at[qk], fs.at[q], fr.at[q], sibling).wait_recv()
        for cp in store:
            cp.wait()

    out = pl.pallas_call(
        body,
        name=name,
        out_shape=tuple(pltpu.HBM(p.shape, p.dtype) for p in parts) + tuple(pltpu.HBM(p.shape, p.dtype) for p in lands),
        in_specs=[HBM_SPEC] * (2 * n) + [SEM_SPEC] * 4 + [ANY_SPEC],
        out_specs=(HBM_SPEC,) * (2 * n),
        input_output_aliases={i: i for i in range(2 * n)},
        scratch_shapes=[pltpu.VMEM(tuple(z.shape[1:]), z.dtype) for z in lands] + [pltpu.SemaphoreType.DMA((n,))],
        compiler_params=pltpu.CompilerParams(has_side_effects=SIDE_EFFECT),
    )(*parts, *lands, ssem, rsem, fsend, frecv, after)
    return list(out[n:])


def _reduce_small(rows, dm8, cact, after):
    r, d = rows.shape
    nl = dm8.shape[0]
    cm = dm8.shape[2] // N_DEV

    def body(rows_ref, dm_ref, cact_ref, after_ref, orow_ref, owm_ref, gr, dmr, ssem, rsem, lsem):
        pos, me = _my_position()
        me8 = pl.multiple_of(me * SUBLANES, SUBLANES)
        gr[me] = rows_ref[...]
        own_dm = pltpu.make_async_copy(dm_ref.at[:, :, pl.ds(me * cm, cm)], dmr.at[:, pl.ds(me8, SUBLANES), :], lsem.at[0])
        own_dm.start()
        sends = []
        for k in range(1, N_DEV):
            peer, pj = _peer(pos, k)
            sends.append(_remote(gr.at[me], gr.at[me], ssem.at[0, k - 1], rsem.at[0, k - 1], peer))
            sends.append(_remote(dm_ref.at[:, :, pl.ds(pj * cm, cm)], dmr.at[:, pl.ds(me8, SUBLANES), :],
                                 ssem.at[1, k - 1], rsem.at[1, k - 1], peer))
        for cp in sends:
            cp.start()
        for k in range(1, N_DEV):
            peer, pj = _peer(pos, k)
            pj8 = pl.multiple_of(pj * SUBLANES, SUBLANES)
            _remote(gr.at[pj], gr.at[pj], ssem.at[0, k - 1], rsem.at[0, k - 1], peer).wait_recv()
            _remote(dm_ref.at[:, :, pl.ds(0, cm)], dmr.at[:, pl.ds(pj8, SUBLANES), :], ssem.at[1, k - 1], rsem.at[1, k - 1], peer).wait_recv()
        own_dm.wait()
        acc = gr[0]
        for j in range(1, N_DEV):
            acc = acc + gr[j]
        orow_ref[...] = acc
        cb = cact_ref[...].astype(BF16)
        for l in range(nl):
            owm_ref[l] = _dot_tn(cb, dmr[l].astype(BF16))
        for cp in sends:
            cp.wait_send()

    return pl.pallas_call(
        body,
        name="reduce_small",
        out_shape=(jax.ShapeDtypeStruct((r, d), F32), jax.ShapeDtypeStruct((nl, d, cm), F32)),
        in_specs=[VMEM_SPEC] * 3 + [ANY_SPEC],
        out_specs=(VMEM_SPEC,) * 2,
        scratch_shapes=[
            pltpu.VMEM((N_DEV, r, d), F32),
            pltpu.VMEM((nl, N_DEV * SUBLANES, cm), F32),
            pltpu.SemaphoreType.DMA((2, N_DEV - 1)),
            pltpu.SemaphoreType.DMA((2, N_DEV - 1)),
            pltpu.SemaphoreType.DMA((1,)),
        ],
        compiler_params=_cparams(vmem=4 * _nbytes((N_DEV, r, d), F32) + 6 * _nbytes((nl, d, cm), F32)),
    )(rows, dm8, cact, after)


def _sum_gathered(zones):
    nl = len(zones)

    def body(*refs):
        for l in range(nl):
            acc = refs[l][0].astype(F32)
            for j in range(1, N_DEV):
                acc = acc + refs[l][j].astype(F32)
            refs[nl][l] = acc

    return pl.pallas_call(
        body,
        name="sum_gathered",
        out_shape=jax.ShapeDtypeStruct((nl,) + tuple(zones[0].shape[1:]), F32),
        in_specs=[VMEM_SPEC] * nl,
        out_specs=VMEM_SPEC,
        compiler_params=_cparams(vmem=8 * nl * _nbytes(zones[0].shape, BF16)),
    )(*zones)


def _in_proj_fwd(x, mod, rows, win_f, l, tm):
    s, d = x.shape
    nb, _, ci = win_f.shape

    def body(x_ref, mod_ref, rows_ref, w_ref, proj_ref, h_ref):
        xv = x_ref[...]
        g = rows_ref[R_G_PRE_MIX:R_G_PRE_MIX + 1, :]
        h = (xv * _rms(xv) * g) * (1.0 + mod_ref[M_SC_M:M_SC_M + 1, :]) + mod_ref[M_SH_M:M_SH_M + 1, :]
        hb = h.astype(BF16)
        h_ref[...] = hb
        for j in range(nb):
            proj_ref[:, j * ci:(j + 1) * ci] = _dot(hb, w_ref[j])

    return pl.pallas_call(
        body,
        name="in_proj_fwd",
        grid=(s // tm,),
        in_specs=[
            pl.BlockSpec((tm, d), lambda i: (i, 0)),
            _resident((None, N_MOD, d), lambda i: (l, 0, 0)),
            _resident((None, N_ROWS, d), lambda i: (l, 0, 0)),
            _resident((nb, d, ci), lambda i: (0, 0, 0)),
        ],
        out_specs=(pl.BlockSpec((tm, nb * ci), lambda i: (i, 0)), pl.BlockSpec((tm, d), lambda i: (i, 0))),
        out_shape=(jax.ShapeDtypeStruct((s, nb * ci), F32), jax.ShapeDtypeStruct((s, d), BF16)),
        compiler_params=_cparams(("parallel",), _nbytes((nb, d, ci), BF16) + 3 * _nbytes((tm, nb * ci), F32) + 8 * _nbytes((tm, d), F32)),
    )(*_hbm(x), mod, rows, win_f)


def _mixer_core_fwd(proj, x, mod, rows, cwf, wr, wi, wa_f, wb_f, wo_f, l, tm):
    s, d = x.shape
    nh, bw, _ = wr.shape[1:]

    def body(proj_ref, x_ref, mod_ref, rows_ref, cw_ref, wr_ref, wi_ref, wa_ref, wb_ref, wo_ref,
             x1_ref, hs_ref, yap_ref, ybp_ref, y_ref, cvbuf, xbbuf, a_s, b_s, hprev):
        i = pl.program_id(0)

        @pl.when(i == 0)
        def _():
            cvbuf[pl.ds(0, SUBLANES), :] = jnp.zeros((SUBLANES, d), F32)
            xbbuf[pl.ds(0, SUBLANES), :] = jnp.zeros((SUBLANES, d), F32)
            hprev[...] = jnp.zeros((SUBLANES, d), F32)

        def row(r):
            return rows_ref[r:r + 1, :]

        def tap(r):
            return cw_ref[r:r + 1, :]

        ba = proj_ref[:, 0:d]
        cv = proj_ref[:, d:2 * d] * proj_ref[:, 2 * d:3 * d]
        cvbuf[pl.ds(SUBLANES, tm), :] = cv
        conv3 = ((row(R_CONV_A_B) + cvbuf[pl.ds(SUBLANES - 2, tm), :] * tap(CW_A)) + cvbuf[pl.ds(SUBLANES - 1, tm), :] * tap(CW_A + 1)) + cv * tap(CW_A + 2)
        ya = ba * conv3
        cvbuf[pl.ds(0, SUBLANES), :] = cvbuf[pl.ds(tm, SUBLANES), :]
        xb = proj_ref[:, 3 * d:4 * d]
        xbbuf[pl.ds(SUBLANES, tm), :] = xb
        xc = (((row(R_CONV_B_B) + xbbuf[pl.ds(SUBLANES - 3, tm), :] * tap(CW_B)) + xbbuf[pl.ds(SUBLANES - 2, tm), :] * tap(CW_B + 1))
              + xbbuf[pl.ds(SUBLANES - 1, tm), :] * tap(CW_B + 2)) + xb * tap(CW_B + 3)
        xbbuf[pl.ds(0, SUBLANES), :] = xbbuf[pl.ds(tm, SUBLANES), :]
        sp = _softplus_neg(row(R_LAMBDA))
        _, _, ig, a, mult = _gates(xc, wr_ref, wi_ref, row(R_B_GATE_R), row(R_B_GATE_I), sp, nh, bw)
        a_s[...] = a
        b_s[...] = mult * (ig * xc)

        def blk(j, hp):
            o = pl.multiple_of(j * SUBLANES, SUBLANES)
            a8, b8 = _scan_block(a_s[pl.ds(o, SUBLANES), :], b_s[pl.ds(o, SUBLANES), :], reverse=False)
            h8 = b8 + a8 * hp
            hs_ref[pl.ds(o, SUBLANES), :] = h8
            return jnp.broadcast_to(h8[SUBLANES - 1:SUBLANES, :], (SUBLANES, d))

        hprev[...] = lax.fori_loop(0, tm // SUBLANES, blk, hprev[...])
        gel, _ = _gelu(proj_ref[:, 4 * d:5 * d])
        yb = hs_ref[...] * gel
        yap = _dot(ya.astype(BF16), wa_ref[...])
        ybp = _dot(yb.astype(BF16), wb_ref[...])
        yap_ref[...] = yap
        ybp_ref[...] = ybp
        m = _sigmoid(proj_ref[:, 5 * d:6 * d]) * yap + _sigmoid(proj_ref[:, 6 * d:7 * d]) * ybp
        y = _dot(m.astype(BF16), wo_ref[...])
        y_ref[...] = y
        x1_ref[...] = x_ref[...] + mod_ref[M_GT_M:M_GT_M + 1, :] * ((y * _rms(y)) * row(R_G_POST_MIX))

    tile = pl.BlockSpec((tm, d), lambda i: (i, 0))
    return pl.pallas_call(
        body,
        name="mixer_core_fwd",
        grid=(s // tm,),
        in_specs=[
            pl.BlockSpec((tm, 7 * d), lambda i: (i, 0)),
            tile,
            _resident((None, N_MOD, d), lambda i: (l, 0, 0)),
            _resident((None, N_ROWS, d), lambda i: (l, 0, 0)),
            _resident((None, CW_ROWS, d), lambda i: (l, 0, 0)),
            _resident((None, nh, bw, bw), lambda i: (l, 0, 0, 0)),
            _resident((None, nh, bw, bw), lambda i: (l, 0, 0, 0)),
            _resident((d, d), lambda i: (0, 0)),
            _resident((d, d), lambda i: (0, 0)),
            _resident((d, d), lambda i: (0, 0)),
        ],
        out_specs=(tile,) * 5,
        out_shape=(jax.ShapeDtypeStruct((s, d), F32),) * 5,
        scratch_shapes=[
            pltpu.VMEM((tm + SUBLANES, d), F32),
            pltpu.VMEM((tm + SUBLANES, d), F32),
            pltpu.VMEM((tm, d), F32),
            pltpu.VMEM((tm, d), F32),
            pltpu.VMEM((SUBLANES, d), F32),
        ],
        compiler_params=_cparams(("arbitrary",), 3 * _nbytes((d, d), BF16) + 2 * _nbytes((tm, 7 * d), F32) + 40 * _nbytes((tm, d), F32)),
    )(*_hbm(proj, x), mod, rows, cwf, wr, wi, wa_f, wb_f, wo_f)


def _mlp_fwd(x1, mod, rows, wup_f, wdn_f, l, tm):
    s, d = x1.shape
    nb, _, cu = wup_f.shape
    dff = nb * cu

    def body(x1_ref, mod_ref, rows_ref, wu_ref, wd_ref, x2_ref, ru_ref, y2_ref, h2_ref):
        xv = x1_ref[...]
        g = rows_ref[R_G_PRE_MLP:R_G_PRE_MLP + 1, :]
        h2 = ((xv * _rms(xv) * g) * (1.0 + mod_ref[M_SC_F:M_SC_F + 1, :]) + mod_ref[M_SH_F:M_SH_F + 1, :]).astype(BF16)
        h2_ref[...] = h2
        ru = jnp.concatenate([jnp.maximum(_dot(h2, wu_ref[j]), 0.0) for j in range(nb)], axis=1)
        ru_ref[...] = ru.astype(BF16)
        y2 = _dot((ru * ru).astype(BF16), wd_ref[...])
        y2_ref[...] = y2
        x2_ref[...] = xv + mod_ref[M_GT_F:M_GT_F + 1, :] * ((y2 * _rms(y2)) * rows_ref[R_G_POST_MLP:R_G_POST_MLP + 1, :])

    tile = pl.BlockSpec((tm, d), lambda i: (i, 0))
    wide = pl.BlockSpec((tm, dff), lambda i: (i, 0))
    return pl.pallas_call(
        body,
        name="mlp_fwd",
        grid=(s // tm,),
        in_specs=[
            tile,
            _resident((None, N_MOD, d), lambda i: (l, 0, 0)),
            _resident((None, N_ROWS, d), lambda i: (l, 0, 0)),
            _resident((nb, d, cu), lambda i: (0, 0, 0)),
            _resident((dff, d), lambda i: (0, 0)),
        ],
        out_specs=(tile, wide, tile, tile),
        out_shape=(jax.ShapeDtypeStruct((s, d), F32), jax.ShapeDtypeStruct((s, dff), BF16),
                   jax.ShapeDtypeStruct((s, d), F32), jax.ShapeDtypeStruct((s, d), BF16)),
        compiler_params=_cparams(("parallel",), 2 * _nbytes((dff, d), BF16) + 5 * _nbytes((tm, dff), F32) + 12 * _nbytes((tm, d), F32)),
    )(*_hbm(x1), mod, rows, wup_f, wdn_f)


def _loss_fwd_bwd(y, target, tm):
    s, d = y.shape

    def body(y_ref, t_ref, loss_ref, dy_ref):
        @pl.when(pl.program_id(0) == 0)
        def _():
            loss_ref[...] = jnp.zeros(loss_ref.shape, F32)

        e = y_ref[...] - t_ref[...]
        dy_ref[...] = e * (1.0 / d)
        loss_ref[...] += 0.5 * jnp.sum(jnp.mean(e * e, axis=-1, keepdims=True), axis=0, keepdims=True)

    tile = pl.BlockSpec((tm, d), lambda i: (i, 0))
    loss, dy = pl.pallas_call(
        body,
        name="loss",
        grid=(s // tm,),
        in_specs=[tile, tile],
        out_specs=(pl.BlockSpec((SUBLANES, 128), lambda i: (0, 0)), tile),
        out_shape=(jax.ShapeDtypeStruct((SUBLANES, 128), F32), jax.ShapeDtypeStruct((s, d), F32)),
        compiler_params=_cparams(("arbitrary",)),
    )(*_hbm(y, target))
    return loss[0, 0], dy


def _mlp_bwd(dx2, x1, y2, ru, mod, rows, wup_f, wdn_f, l, tm):
    s, d = x1.shape
    nb, _, cu = wup_f.shape
    dff = nb * cu

    def body(dx2_ref, x1_ref, y2_ref, ru_ref, mod_ref, rows_ref, wu_ref, wd_ref, dx1_ref, dy2_ref, dup_ref, act_ref, sm_ref):
        @pl.when(pl.program_id(0) == 0)
        def _():
            sm_ref[...] = jnp.zeros(sm_ref.shape, F32)

        dout = dx2_ref[...]
        dy2, dgt, dgpost = _postnorm_bwd(y2_ref[...], dout, rows_ref[R_G_POST_MLP:R_G_POST_MLP + 1, :], mod_ref[M_GT_F:M_GT_F + 1, :])
        dy2b = dy2.astype(BF16)
        dy2_ref[...] = dy2b
        ruv = ru_ref[...].astype(F32)
        act_ref[...] = (ruv * ruv).astype(BF16)
        dup = (_dot_nt(dy2b, wd_ref[...]) * (2.0 * ruv)).astype(BF16)
        dup_ref[...] = dup
        dh2 = _dot_nt(dup[:, 0:cu], wu_ref[0])
        for j in range(1, nb):
            dh2 = dh2 + _dot_nt(dup[:, j * cu:(j + 1) * cu], wu_ref[j])
        dxn, dsc, dsh, dgpre = _prenorm_bwd(x1_ref[...], dh2, rows_ref[R_G_PRE_MLP:R_G_PRE_MLP + 1, :], mod_ref[M_SC_F:M_SC_F + 1, :])
        dx1_ref[...] = dout + dxn
        for r, v in ((G_MLP_GT, dgt), (G_MLP_GPOST, dgpost), (G_MLP_SC, dsc), (G_MLP_SH, dsh), (G_MLP_GPRE, dgpre)):
            sm_ref[r:r + 1, :] += v

    tile = pl.BlockSpec((tm, d), lambda i: (i, 0))
    wide = pl.BlockSpec((tm, dff), lambda i: (i, 0))
    return pl.pallas_call(
        body,
        name="mlp_bwd",
        grid=(s // tm,),
        in_specs=[
            tile, tile, tile, wide,
            _resident((None, N_MOD, d), lambda i: (l, 0, 0)),
            _resident((None, N_ROWS, d), lambda i: (l, 0, 0)),
            _resident((nb, d, cu), lambda i: (0, 0, 0)),
            _resident((dff, d), lambda i: (0, 0)),
        ],
        out_specs=(tile, tile, wide, wide, pl.BlockSpec((G_MLP_ROWS, d), lambda i: (0, 0))),
        out_shape=(jax.ShapeDtypeStruct((s, d), F32), jax.ShapeDtypeStruct((s, d), BF16), jax.ShapeDtypeStruct((s, dff), BF16),
                   jax.ShapeDtypeStruct((s, dff), BF16), jax.ShapeDtypeStruct((G_MLP_ROWS, d), F32)),
        compiler_params=_cparams(("arbitrary",), 2 * _nbytes((dff, d), BF16) + 6 * _nbytes((tm, dff), F32) + 16 * _nbytes((tm, d), F32)),
    )(*_hbm(dx2, x1, y2, ru), mod, rows, wup_f, wdn_f)


def _mixer_core_bwd(dx1, y, yap, ybp, hs, proj, mod, rows, cwf, wr, wi, wa_f, wb_f, wo_f, l, tm):
    s, d = dx1.shape
    nh, bw, _ = wr.shape[1:]
    nt = s // tm
    per = tm // SUBLANES

    def body(dx1_ref, y_ref, yap_ref, ybp_ref, hs_ref, hsh_ref, proj_ref, projh_ref, mod_ref, rows_ref, cw_ref,
             wr_ref, wi_ref, wa_ref, wb_ref, wo_ref,
             dproj_ref, dy_ref, m_ref, dyap_ref, dybp_ref, ya_ref, yb_ref, sm_ref, dwg_ref,
             cvbuf, xbbuf, hsbuf, abuf, dcbuf, dxbuf, al_s, dh_s, lam_s, lnext):
        i = pl.program_id(0)
        first_tile = i == nt - 1

        @pl.when(i == 0)
        def _():
            sm_ref[...] = jnp.zeros(sm_ref.shape, F32)
            dwg_ref[...] = jnp.zeros(dwg_ref.shape, F32)
            zero = jnp.zeros((SUBLANES, d), F32)
            abuf[pl.ds(tm, SUBLANES), :] = zero
            dcbuf[pl.ds(tm, SUBLANES), :] = zero
            dxbuf[pl.ds(tm, SUBLANES), :] = zero
            lnext[...] = zero

        def row(r):
            return rows_ref[r:r + 1, :]

        def tap(r):
            return cw_ref[r:r + 1, :]

        def acc(r, v):
            sm_ref[r:r + 1, :] += v

        keep_halo = jnp.where(first_tile, 0.0, 1.0)
        dy, dgt, dgpost = _postnorm_bwd(y_ref[...], dx1_ref[...], row(R_G_POST_MIX), mod_ref[M_GT_M:M_GT_M + 1, :])
        acc(G_MIX_GT, dgt)
        acc(G_MIX_GPOST, dgpost)
        dyb16 = dy.astype(BF16)
        dy_ref[...] = dyb16
        dm = _dot_nt(dyb16, wo_ref[...])
        sa = _sigmoid(proj_ref[:, 5 * d:6 * d])
        sb = _sigmoid(proj_ref[:, 6 * d:7 * d])
        yap = yap_ref[...]
        ybp = ybp_ref[...]
        m_ref[...] = (sa * yap + sb * ybp).astype(BF16)
        dyap = (dm * sa).astype(BF16)
        dybp = (dm * sb).astype(BF16)
        dyap_ref[...] = dyap
        dybp_ref[...] = dybp
        dproj_ref[:, 5 * d:6 * d] = (dm * yap * sa * (1.0 - sa)).astype(BF16)
        dproj_ref[:, 6 * d:7 * d] = (dm * ybp * sb * (1.0 - sb)).astype(BF16)
        dya = _dot_nt(dyap, wa_ref[...])
        dyb = _dot_nt(dybp, wb_ref[...])
        ba = proj_ref[:, 0:d]
        ca = proj_ref[:, d:2 * d]
        va = proj_ref[:, 2 * d:3 * d]
        cv = ca * va
        cvbuf[pl.ds(0, SUBLANES), :] = keep_halo * (projh_ref[:, d:2 * d] * projh_ref[:, 2 * d:3 * d])
        cvbuf[pl.ds(SUBLANES, tm), :] = cv
        cvm2 = cvbuf[pl.ds(SUBLANES - 2, tm), :]
        cvm1 = cvbuf[pl.ds(SUBLANES - 1, tm), :]
        conv3 = ((row(R_CONV_A_B) + cvm2 * tap(CW_A)) + cvm1 * tap(CW_A + 1)) + cv * tap(CW_A + 2)
        ya_ref[...] = (ba * conv3).astype(BF16)
        dproj_ref[:, 0:d] = (dya * conv3).astype(BF16)
        dc3 = dya * ba
        acc(G_MIX_CAB, _colsum(dc3))
        acc(G_MIX_CAW, _colsum(dc3 * cvm2))
        acc(G_MIX_CAW + 1, _colsum(dc3 * cvm1))
        acc(G_MIX_CAW + 2, _colsum(dc3 * cv))
        dcbuf[pl.ds(0, tm), :] = dc3
        dcv = (dc3 * tap(CW_A + 2) + dcbuf[pl.ds(1, tm), :] * tap(CW_A + 1)) + dcbuf[pl.ds(2, tm), :] * tap(CW_A)
        dcbuf[pl.ds(tm, SUBLANES), :] = dcbuf[pl.ds(0, SUBLANES), :]
        dproj_ref[:, d:2 * d] = (dcv * va).astype(BF16)
        dproj_ref[:, 2 * d:3 * d] = (dcv * ca).astype(BF16)
        xb = proj_ref[:, 3 * d:4 * d]
        gb = proj_ref[:, 4 * d:5 * d]
        xbbuf[pl.ds(0, SUBLANES), :] = keep_halo * projh_ref[:, 3 * d:4 * d]
        xbbuf[pl.ds(SUBLANES, tm), :] = xb
        xm3 = xbbuf[pl.ds(SUBLANES - 3, tm), :]
        xm2 = xbbuf[pl.ds(SUBLANES - 2, tm), :]
        xm1 = xbbuf[pl.ds(SUBLANES - 1, tm), :]
        xc = (((row(R_CONV_B_B) + xm3 * tap(CW_B)) + xm2 * tap(CW_B + 1)) + xm1 * tap(CW_B + 2)) + xb * tap(CW_B + 3)
        lam = row(R_LAMBDA)
        sp = _softplus_neg(lam)
        xcb, r, ig, a, mult = _gates(xc, wr_ref, wi_ref, row(R_B_GATE_R), row(R_B_GATE_I), sp, nh, bw)
        gel, th = _gelu(gb)
        hs = hs_ref[...]
        yb_ref[...] = (hs * gel).astype(BF16)
        dproj_ref[:, 4 * d:5 * d] = (dyb * hs * _gelu_grad(gb, th)).astype(BF16)
        abuf[pl.ds(0, tm), :] = a
        al_s[...] = abuf[pl.ds(1, tm), :]
        abuf[pl.ds(tm, SUBLANES), :] = abuf[pl.ds(0, SUBLANES), :]
        dh_s[...] = dyb * gel

        def blk(j, ln):
            o = pl.multiple_of((per - 1 - j) * SUBLANES, SUBLANES)
            a8, b8 = _scan_block(al_s[pl.ds(o, SUBLANES), :], dh_s[pl.ds(o, SUBLANES), :], reverse=True)
            l8 = b8 + a8 * ln
            lam_s[pl.ds(o, SUBLANES), :] = l8
            return jnp.broadcast_to(l8[0:1, :], (SUBLANES, d))

        lnext[...] = lax.fori_loop(0, per, blk, lnext[...])
        dbb = lam_s[...]
        hsbuf[pl.ds(0, SUBLANES), :] = keep_halo * hsh_ref[...]
        hsbuf[pl.ds(SUBLANES, tm), :] = hs
        da = dbb * hsbuf[pl.ds(SUBLANES - 1, tm), :]
        dmult = dbb * (ig * xc)
        dig = dbb * (mult * xc)
        dxc = dbb * (mult * ig)
        dla = da * a - dmult * (a * a) / mult
        acc(G_MIX_LAM, _colsum(dla * (-LRU_C * r)) * (-_sigmoid(-lam)))
        dzr = (dla * (-LRU_C * sp)) * r * (1.0 - r)
        dzi = dig * ig * (1.0 - ig)
        acc(G_MIX_BR, _colsum(dzr))
        acc(G_MIX_BI, _colsum(dzi))
        dzrb = dzr.astype(BF16)
        dzib = dzi.astype(BF16)
        back = []
        for h in range(nh):
            sl = slice(h * bw, (h + 1) * bw)
            back.append(_dot_nt(dzrb[:, sl], wr_ref[h]) + _dot_nt(dzib[:, sl], wi_ref[h]))
            dwg_ref[0, h] += _dot_tn(xcb[:, sl], dzrb[:, sl])
            dwg_ref[1, h] += _dot_tn(xcb[:, sl], dzib[:, sl])
        dxc = dxc + jnp.concatenate(back, axis=1)
        acc(G_MIX_CBB, _colsum(dxc))
        acc(G_MIX_CBW, _colsum(dxc * xm3))
        acc(G_MIX_CBW + 1, _colsum(dxc * xm2))
        acc(G_MIX_CBW + 2, _colsum(dxc * xm1))
        acc(G_MIX_CBW + 3, _colsum(dxc * xb))
        dxbuf[pl.ds(0, tm), :] = dxc
        dxb = ((dxc * tap(CW_B + 3) + dxbuf[pl.ds(1, tm), :] * tap(CW_B + 2)) + dxbuf[pl.ds(2, tm), :] * tap(CW_B + 1)) + dxbuf[pl.ds(3, tm), :] * tap(CW_B)
        dxbuf[pl.ds(tm, SUBLANES), :] = dxbuf[pl.ds(0, SUBLANES), :]
        dproj_ref[:, 3 * d:4 * d] = dxb.astype(BF16)

    def rev(i):
        return (nt - 1 - i, 0)

    def halo(i):
        return (jnp.maximum((nt - 1 - i) * per - 1, 0), 0)

    tile = pl.BlockSpec((tm, d), rev)
    return pl.pallas_call(
        body,
        name="mixer_core_bwd",
        grid=(nt,),
        in_specs=[
            tile, tile, tile, tile, tile,
            pl.BlockSpec((SUBLANES, d), halo),
            pl.BlockSpec((tm, 7 * d), rev),
            pl.BlockSpec((SUBLANES, 7 * d), halo),
            _resident((None, N_MOD, d), lambda i: (l, 0, 0)),
            _resident((None, N_ROWS, d), lambda i: (l, 0, 0)),
            _resident((None, CW_ROWS, d), lambda i: (l, 0, 0)),
            _resident((None, nh, bw, bw), lambda i: (l, 0, 0, 0)),
            _resident((None, nh, bw, bw), lambda i: (l, 0, 0, 0)),
            _resident((d, d), lambda i: (0, 0)),
            _resident((d, d), lambda i: (0, 0)),
            _resident((d, d), lambda i: (0, 0)),
        ],
        out_specs=(pl.BlockSpec((tm, 7 * d), rev),) + (tile,) * 6 + (
            pl.BlockSpec((G_MIX_ROWS, d), lambda i: (0, 0)), pl.BlockSpec((2, nh, bw, bw), lambda i: (0, 0, 0, 0))),
        out_shape=(jax.ShapeDtypeStruct((s, 7 * d), BF16),) + (jax.ShapeDtypeStruct((s, d), BF16),) * 6 + (
            jax.ShapeDtypeStruct((G_MIX_ROWS, d), F32), jax.ShapeDtypeStruct((2, nh, bw, bw), F32)),
        scratch_shapes=[pltpu.VMEM((tm + SUBLANES, d), F32)] * 6 + [pltpu.VMEM((tm, d), F32)] * 3 + [pltpu.VMEM((SUBLANES, d), F32)],
        compiler_params=_cparams(("arbitrary",), 3 * _nbytes((d, d), BF16) + 3 * _nbytes((tm, 7 * d), F32) + 64 * _nbytes((tm, d), F32)),
    )(*_hbm(dx1, y, yap, ybp, hs, hs, proj, proj), mod, rows, cwf, wr, wi, wa_f, wb_f, wo_f)


def _in_proj_bwd(dproj, x, dx1, mod, rows, win_f, l, tm):
    s, d = x.shape
    nb, _, ci = win_f.shape

    def body(dp_ref, x_ref, dx1_ref, mod_ref, rows_ref, w_ref, dx_ref, sm_ref):
        @pl.when(pl.program_id(0) == 0)
        def _():
            sm_ref[...] = jnp.zeros(sm_ref.shape, F32)

        dh = _dot_nt(dp_ref[:, 0:ci], w_ref[0])
        for j in range(1, nb):
            dh = dh + _dot_nt(dp_ref[:, j * ci:(j + 1) * ci], w_ref[j])
        dxn, dsc, dsh, dg = _prenorm_bwd(x_ref[...], dh, rows_ref[R_G_PRE_MIX:R_G_PRE_MIX + 1, :], mod_ref[M_SC_M:M_SC_M + 1, :])
        dx_ref[...] = dx1_ref[...] + dxn
        for r, v in ((G_IN_SC, dsc), (G_IN_SH, dsh), (G_IN_GPRE, dg)):
            sm_ref[r:r + 1, :] += v

    tile = pl.BlockSpec((tm, d), lambda i: (i, 0))
    return pl.pallas_call(
        body,
        name="in_proj_bwd",
        grid=(s // tm,),
        in_specs=[
            pl.BlockSpec((tm, nb * ci), lambda i: (i, 0)), tile, tile,
            _resident((None, N_MOD, d), lambda i: (l, 0, 0)),
            _resident((None, N_ROWS, d), lambda i: (l, 0, 0)),
            _resident((nb, d, ci), lambda i: (0, 0, 0)),
        ],
        out_specs=(tile, pl.BlockSpec((G_IN_ROWS, d), lambda i: (0, 0))),
        out_shape=(jax.ShapeDtypeStruct((s, d), F32), jax.ShapeDtypeStruct((G_IN_ROWS, d), F32)),
        compiler_params=_cparams(("arbitrary",), _nbytes((nb, d, ci), BF16) + 2 * _nbytes((tm, nb * ci), BF16) + 16 * _nbytes((tm, d), F32)),
    )(*_hbm(dproj, x, dx1), mod, rows, win_f)


def _wgrad(a, b, cols_owned, ts, after):
    s, k1 = a.shape
    k2 = b.shape[1]
    ns = s // ts
    if cols_owned:
        nblk, bk1, bk2 = N_DEV, k1, k2 // N_DEV
        a_spec = pl.BlockSpec((ts, bk1), lambda j, t: (t, 0))
        b_spec = pl.BlockSpec((ts, bk2), lambda j, t: (t, j))
    else:
        bk1, bk2 = min(WGRAD_ROWS, k1), k2
        nblk = k1 // bk1
        a_spec = pl.BlockSpec((ts, bk1), lambda j, t: (t, j))
        b_spec = pl.BlockSpec((ts, bk2), lambda j, t: (t, 0))

    def body(a_ref, b_ref, after_ref, o_ref, acc_ref):
        t = pl.program_id(1)

        @pl.when(t == 0)
        def _():
            acc_ref[...] = jnp.zeros(acc_ref.shape, F32)

        acc_ref[...] += _dot_tn(a_ref[...], b_ref[...])

        @pl.when(t == ns - 1)
        def _():
            o_ref[...] = acc_ref[...].astype(BF16)

    out = pl.pallas_call(
        body,
        name="wgrad",
        grid=(nblk, ns),
        in_specs=[a_spec, b_spec, ANY_SPEC],
        out_specs=pl.BlockSpec((None, bk1, bk2), lambda j, t: (j, 0, 0)),
        out_shape=pltpu.HBM((nblk, bk1, bk2), BF16),
        scratch_shapes=[pltpu.VMEM((bk1, bk2), F32)],
        compiler_params=_cparams(("parallel", "arbitrary"), 4 * _nbytes((bk1, bk2), F32) + 4 * _nbytes((ts, bk1 + bk2), BF16)),
    )(pltpu.with_memory_space_constraint(a, pltpu.HBM), pltpu.with_memory_space_constraint(b, pltpu.HBM), after)
    return out if cols_owned else out.reshape(N_DEV, k1 // N_DEV, k2)


def _adam_update(w, g, m, v):
    m = ADAM_B1 * m + (1.0 - ADAM_B1) * g
    v = ADAM_B2 * v + (1.0 - ADAM_B2) * (g * g)
    m_hat = m / (1.0 - ADAM_B1 ** ADAM_STEP)
    v_hat = v / (1.0 - ADAM_B2 ** ADAM_STEP)
    delta = -ADAM_LR * (m_hat / (jnp.sqrt(v_hat) + ADAM_EPS) + ADAM_WD * w)
    return delta, m, v


def _sum_adamw(recv, w, m, v, tr, after):
    nl, ra, cb = w.shape
    assert nl == len(recv) == 2

    def body(r0_ref, r1_ref, w_ref, m_ref, v_ref, after_ref, g_ref, d_ref, nm_ref, nv_ref):
        def total(r_ref):
            g = r_ref[0].astype(F32)
            for j in range(1, N_DEV):
                g = g + r_ref[j].astype(F32)
            return g

        g = jnp.where(pl.program_id(0) == 0, total(r0_ref), total(r1_ref))
        g_ref[...] = g
        d_ref[...], nm_ref[...], nv_ref[...] = _adam_update(w_ref[...], g, m_ref[...], v_ref[...])

    blk = pl.BlockSpec((None, tr, cb), lambda l, i: (l, i, 0))
    return pl.pallas_call(
        body,
        name="sum_adamw",
        grid=(nl, ra // tr),
        in_specs=[pl.BlockSpec((N_DEV, tr, cb), lambda l, i: (0, i * (1 - l), 0)),
                  pl.BlockSpec((N_DEV, tr, cb), lambda l, i: (0, i * l, 0)), blk, blk, blk, ANY_SPEC],
        out_specs=(blk,) * 4,
        out_shape=(jax.ShapeDtypeStruct((nl, ra, cb), F32),) * 4,
        compiler_params=_cparams(("arbitrary", "arbitrary"), 6 * _nbytes((N_DEV, tr, cb), BF16) + 32 * _nbytes((tr, cb), F32)),
    )(*_hbm(recv[0], recv[1], w, m, v), after)


def _adamw(w, g, m, v):
    def body(w_ref, g_ref, m_ref, v_ref, d_ref, nm_ref, nv_ref):
        d_ref[...], nm_ref[...], nv_ref[...] = _adam_update(w_ref[...], g_ref[...], m_ref[...], v_ref[...])

    return pl.pallas_call(
        body,
        name="adamw",
        in_specs=[VMEM_SPEC] * 4,
        out_specs=(VMEM_SPEC,) * 3,
        out_shape=(jax.ShapeDtypeStruct(w.shape, F32),) * 3,
        compiler_params=_cparams(vmem=10 * _nbytes(w.shape, F32)),
    )(w, g, m, v)


def _adamw_tiled(w, g, m, v, tr):
    nl, ra, cb = w.shape

    def body(w_ref, g_ref, m_ref, v_ref, d_ref, nm_ref, nv_ref):
        d_ref[...], nm_ref[...], nv_ref[...] = _adam_update(w_ref[...], g_ref[...], m_ref[...], v_ref[...])

    blk = pl.BlockSpec((None, tr, cb), lambda l, i: (l, i, 0))
    return pl.pallas_call(
        body,
        name="adamw_tiled",
        grid=(nl, ra // tr),
        in_specs=[blk] * 4,
        out_specs=(blk,) * 3,
        out_shape=(jax.ShapeDtypeStruct((nl, ra, cb), F32),) * 3,
        compiler_params=_cparams(("parallel", "parallel")),
    )(*_hbm(w, g, m, v))


def _token_tile(s):
    return min(256, max(SUBLANES * 2, s // 4))


def kernel(x, c, w_mod, b_mod, g_pre_mix, g_post_mix, w_in, conv_a_w, conv_a_b, w_a_out, conv_b_w, conv_b_b, w_gate_r, b_gate_r, w_gate_i, b_gate_i, lru_lambda, w_b_out, w_o, g_pre_mlp, g_post_mlp, w_mlp_up, w_mlp_down, loss_target, m_w_mod, m_b_mod, m_g_pre_mix, m_g_post_mix, m_w_in, m_conv_a_w, m_conv_a_b, m_w_a_out, m_conv_b_w, m_conv_b_b, m_w_gate_r, m_b_gate_r, m_w_gate_i, m_b_gate_i, m_lru_lambda, m_w_b_out, m_w_o, m_g_pre_mlp, m_g_post_mlp, m_w_mlp_up, m_w_mlp_down, v_w_mod, v_b_mod, v_g_pre_mix, v_g_post_mix, v_w_in, v_conv_a_w, v_conv_a_b, v_w_a_out, v_conv_b_w, v_conv_b_b, v_w_gate_r, v_b_gate_r, v_w_gate_i, v_b_gate_i, v_lru_lambda, v_w_b_out, v_w_o, v_g_pre_mlp, v_g_post_mlp, v_w_mlp_up, v_w_mlp_down):
    nl = w_mod.shape[0]
    s, d = x.shape[1], x.shape[2]
    nh, bw = w_gate_r.shape[1], w_gate_r.shape[2]
    cwid = conv_a_w.shape[2]
    tm = _token_tile(s)
    ts = s
    _, me = _my_position()
    xs = x.reshape(s, d)
    target = loss_target.reshape(s, d)

    vec_names = (g_pre_mix, g_post_mix, conv_a_b, conv_b_b, b_gate_r, b_gate_i, lru_lambda, g_pre_mlp, g_post_mlp)
    rows = jnp.concatenate([jnp.stack(vec_names, axis=1), jnp.zeros((nl, N_ROWS - len(vec_names), d), F32)], axis=1)
    cw = jnp.concatenate([conv_a_w, conv_b_w, jnp.zeros((nl, CW_ROWS - 7, cwid), F32)], axis=1)

    w16 = {"w_in": w_in.astype(BF16), "w_a_out": w_a_out.astype(BF16), "w_b_out": w_b_out.astype(BF16), "w_o": w_o.astype(BF16),
           "w_mlp_up": w_mlp_up.astype(BF16), "w_mlp_down": w_mlp_down.astype(BF16)}
    groups = (("in", ("w_in",)), ("mix", ("w_a_out", "w_b_out", "w_o")), ("mlp", ("w_mlp_up", "w_mlp_down")))
    mod, cact, cwf = _prep_small(c, w_mod, b_mod, cw)
    mod = mod.reshape(nl, N_MOD, d)
    w16, _ = lax.optimization_barrier((w16, mod))
    gathers = {}
    token = jnp.zeros((), F32)
    for l in range(nl):
        for gname, members in groups:
            gathers[l, gname] = _gather2_start([w16[n][l] for n in members], f"gather_start_{gname}{l}")
            token = token + gathers[l, gname][4][0, 0]
    rows = rows + token
    wr = w_gate_r.astype(BF16)
    wi = w_gate_i.astype(BF16)

    def gathered(l, gname, after):
        forwarded = _gather2_forward(gathers[l, gname], after, f"gather_forward_{gname}{l}")
        return _gather2_wait(forwarded, after, f"gather_wait_{gname}{l}")

    saved = []
    weights = []
    xin = xs
    for l in range(nl):
        (win_f,) = gathered(l, "in", mod if l == 0 else xin)
        proj, h = _in_proj_fwd(xin, mod, rows, win_f, l, tm)
        wa_f, wb_f, wo_f = (w.reshape(d, d) for w in gathered(l, "mix", proj))
        x1, hs, yap, ybp, y = _mixer_core_fwd(proj, xin, mod, rows, cwf, wr, wi, wa_f, wb_f, wo_f, l, tm)
        wup_f, wdn_f = gathered(l, "mlp", x1)
        wdn_f = wdn_f.reshape(-1, d)
        x2, ru, y2, h2 = _mlp_fwd(x1, mod, rows, wup_f, wdn_f, l, tm)
        saved.append((xin, proj, h, x1, hs, yap, ybp, y, ru, y2, h2))
        weights.append((win_f, wa_f, wb_f, wo_f, wup_f, wdn_f))
        xin = x2
    loss_part, dx = _loss_fwd_bwd(xin, target, tm)
    loss = lax.psum(loss_part, ("x", "y", "c"))

    scatters = {}
    small = [None] * nl
    gate_parts = [None] * nl

    def scatter(l, gname, parts, rows):
        scatters[l, gname] = _exchange_start(parts, False, f"scatter_start_{gname}{l}")
        return rows + scatters[l, gname][4][0, 0]

    for l in reversed(range(nl)):
        xin, proj, h, x1, hs, yap, ybp, y, ru, y2, h2 = saved[l]
        win_f, wa_f, wb_f, wo_f, wup_f, wdn_f = weights[l]
        dx1, dy2, dup, act, sm_mlp = _mlp_bwd(dx, x1, y2, ru, mod, rows, wup_f, wdn_f, l, tm)
        g_up = _wgrad(h2, dup, True, ts, dx1)
        g_dn = _wgrad(act, dy2, False, ts, g_up)
        rows = scatter(l, "mlp", [g_up, g_dn], rows)
        dproj, dy, m, dyap, dybp, ya, yb, sm_mix, dwg = _mixer_core_bwd(dx1, y, yap, ybp, hs, proj, mod, rows, cwf, wr, wi, wa_f, wb_f, wo_f, l, tm // 2)
        gate_parts[l] = _exchange_start([dwg.astype(BF16)], True, f"gates_start{l}")
        rows = rows + gate_parts[l][4][0, 0]
        g_a = _wgrad(ya, dyap, False, ts, gate_parts[l][4])
        g_b = _wgrad(yb, dybp, False, ts, g_a)
        g_o = _wgrad(m, dy, False, ts, g_b)
        rows = scatter(l, "mix", [g_a, g_b, g_o], rows)
        rows = scatter(l, "in", [_wgrad(h, dproj, True, ts, scatters[l, "mix"][4])], rows)
        dx, sm_in = _in_proj_bwd(dproj, xin, dx1, mod, rows, win_f, l, tm)
        small[l] = jnp.concatenate([sm_mlp, sm_mix, sm_in], axis=0)
    grad_x = dx.reshape(x.shape)

    recv = {}
    big = {}
    moments = {"w_in": (w_in, m_w_in, v_w_in), "w_mlp_up": (w_mlp_up, m_w_mlp_up, v_w_mlp_up), "w_a_out": (w_a_out, m_w_a_out, v_w_a_out),
               "w_b_out": (w_b_out, m_w_b_out, v_w_b_out), "w_o": (w_o, m_w_o, v_w_o), "w_mlp_down": (w_mlp_down, m_w_mlp_down, v_w_mlp_down)}

    def collect(l, gname, after):
        for n, zone in zip(dict(groups)[gname], _exchange_wait(scatters[l, gname], after, False, f"scatter_wait_{gname}{l}")):
            recv[n, l] = zone

    def update(name, after):
        w, m_, v_ = moments[name]
        big[name] = _sum_adamw([recv[name, l] for l in range(nl)], w, m_, v_, min(256, w.shape[1]), after)
        return big[name][1]

    for l, gname in ((1, "mlp"), (1, "in"), (1, "mix"), (0, "mlp")):
        collect(l, gname, dx)
    early = update("w_mlp_down", update("w_mlp_up", dx))

    lrows = jnp.concatenate(small, axis=0)

    def lrow(a, l, r):
        return a[l * G_LAYER_ROWS + r]

    dm = jnp.stack([jnp.concatenate([lrow(lrows, l, G_MLP_ROWS + G_MIX_ROWS + G_IN_SH), lrow(lrows, l, G_MLP_ROWS + G_MIX_ROWS + G_IN_SC),
                                     lrow(lrows, l, G_MLP_ROWS + G_MIX_GT), lrow(lrows, l, G_MLP_SH), lrow(lrows, l, G_MLP_SC),
                                     lrow(lrows, l, G_MLP_GT)]) for l in range(nl)])
    dm8 = jnp.concatenate([dm[:, None, :], jnp.zeros((nl, SUBLANES - 1, N_MOD * d), F32)], axis=1)
    srows, g_w_mod = _reduce_small(lrows, dm8, cact, early)
    sgates = _sum_gathered([_exchange_wait(gate_parts[l], srows, True, f"gates_wait{l}")[0] for l in range(nl)])
    for gname in ("in", "mix"):
        collect(0, gname, srows)
    for name in ("w_in", "w_a_out", "w_b_out", "w_o"):
        update(name, srows)

    def srow(l, r):
        return lrow(srows, l, r)

    def per_layer(r):
        return jnp.stack([srow(l, r) for l in range(nl)])

    mix0 = G_MLP_ROWS
    in0 = G_MLP_ROWS + G_MIX_ROWS
    g_b_mod = jnp.stack([jnp.concatenate([srow(l, in0 + G_IN_SH), srow(l, in0 + G_IN_SC), srow(l, mix0 + G_MIX_GT),
                                          srow(l, G_MLP_SH), srow(l, G_MLP_SC), srow(l, G_MLP_GT)]) for l in range(nl)])
    conv_a_full = jnp.stack([jnp.stack([srow(l, mix0 + G_MIX_CAW + k) for k in range(3)]) for l in range(nl)])
    conv_b_full = jnp.stack([jnp.stack([srow(l, mix0 + G_MIX_CBW + k) for k in range(4)]) for l in range(nl)])
    grads = {
        "b_mod": g_b_mod,
        "g_pre_mix": per_layer(in0 + G_IN_GPRE),
        "g_post_mix": per_layer(mix0 + G_MIX_GPOST),
        "conv_a_w": lax.dynamic_slice_in_dim(conv_a_full, me * cwid, cwid, axis=2),
        "conv_a_b": per_layer(mix0 + G_MIX_CAB),
        "conv_b_w": lax.dynamic_slice_in_dim(conv_b_full, me * cwid, cwid, axis=2),
        "conv_b_b": per_layer(mix0 + G_MIX_CBB),
        "w_gate_r": sgates[:, 0],
        "b_gate_r": per_layer(mix0 + G_MIX_BR),
        "w_gate_i": sgates[:, 1],
        "b_gate_i": per_layer(mix0 + G_MIX_BI),
        "lru_lambda": per_layer(mix0 + G_MIX_LAM),
        "g_pre_mlp": per_layer(G_MLP_GPRE),
        "g_post_mlp": per_layer(G_MLP_GPOST),
    }
    params = {
        "b_mod": (b_mod, m_b_mod, v_b_mod), "g_pre_mix": (g_pre_mix, m_g_pre_mix, v_g_pre_mix), "g_post_mix": (g_post_mix, m_g_post_mix, v_g_post_mix),
        "conv_a_w": (conv_a_w, m_conv_a_w, v_conv_a_w), "conv_a_b": (conv_a_b, m_conv_a_b, v_conv_a_b),
        "conv_b_w": (conv_b_w, m_conv_b_w, v_conv_b_w), "conv_b_b": (conv_b_b, m_conv_b_b, v_conv_b_b),
        "w_gate_r": (w_gate_r, m_w_gate_r, v_w_gate_r), "b_gate_r": (b_gate_r, m_b_gate_r, v_b_gate_r),
        "w_gate_i": (w_gate_i, m_w_gate_i, v_w_gate_i), "b_gate_i": (b_gate_i, m_b_gate_i, v_b_gate_i),
        "lru_lambda": (lru_lambda, m_lru_lambda, v_lru_lambda), "g_pre_mlp": (g_pre_mlp, m_g_pre_mlp, v_g_pre_mlp),
        "g_post_mlp": (g_post_mlp, m_g_post_mlp, v_g_post_mlp),
    }
    out = {}
    for name, g in grads.items():
        w, m_, v_ = params[name]
        flat = (-1, w.shape[-1])
        dl, nm, nv = _adamw(w.reshape(flat), g.reshape(flat), m_.reshape(flat), v_.reshape(flat))
        out[name] = (g.reshape(w.shape), dl.reshape(w.shape), nm.reshape(w.shape), nv.reshape(w.shape))
    out["w_mod"] = (g_w_mod,) + tuple(_adamw_tiled(w_mod, g_w_mod, m_w_mod, v_w_mod, min(128, d)))
    out.update(big)

    order = ("w_mod", "b_mod", "g_pre_mix", "g_post_mix", "w_in", "conv_a_w", "conv_a_b", "w_a_out", "conv_b_w", "conv_b_b", "w_gate_r", "b_gate_r",
             "w_gate_i", "b_gate_i", "lru_lambda", "w_b_out", "w_o", "g_pre_mlp", "g_post_mlp", "w_mlp_up", "w_mlp_down")
    return (loss, grad_x) + tuple(out[n][0] for n in order) + tuple(out[n][1] for n in order) + tuple(out[n][2] for n in order) + tuple(out[n][3] for n in order)
```

```python
import functools

import jax
import jax.numpy as jnp
from jax import lax
from jax.experimental import pallas as pl
from jax.experimental.pallas import tpu as pltpu

F32, BF16 = jnp.float32, jnp.bfloat16
EPS = 1e-6
LRU_C = 8.0
N_DEV = 8
N_MOD = 6
SUBLANES = 8
VMEM_BUDGET = 56 * 1024 * 1024
WGRAD_ROWS = 512
ADAM_LR, ADAM_B1, ADAM_B2, ADAM_EPS, ADAM_WD, ADAM_STEP = 0.001, 0.9, 0.999, 1e-08, 0.01, 10
MESH = pl.DeviceIdType.MESH
VMEM_SPEC = pl.BlockSpec(memory_space=pltpu.VMEM)
ANY_SPEC = pl.BlockSpec(memory_space=pl.ANY)
HBM_SPEC = pl.BlockSpec(memory_space=pltpu.HBM)
SEM_SPEC = pl.BlockSpec(memory_space=pltpu.SEMAPHORE)
SIDE_EFFECT = pltpu.SideEffectType.DATAFLOW_SIDE_EFFECTING

R_G_PRE_MIX, R_G_POST_MIX, R_CONV_A_B, R_CONV_B_B, R_B_GATE_R, R_B_GATE_I, R_LAMBDA, R_G_PRE_MLP, R_G_POST_MLP = range(9)
N_ROWS = 16
M_SH_M, M_SC_M, M_GT_M, M_SH_F, M_SC_F, M_GT_F = range(6)
CW_A, CW_B, CW_ROWS = 0, 3, 8
G_MLP_GT, G_MLP_GPOST, G_MLP_SC, G_MLP_SH, G_MLP_GPRE, G_MLP_ROWS = 0, 1, 2, 3, 4, 8
(G_MIX_GT, G_MIX_GPOST, G_MIX_CAB, G_MIX_CAW, G_MIX_CBB, G_MIX_CBW, G_MIX_BR, G_MIX_BI, G_MIX_LAM) = 0, 1, 2, 3, 6, 7, 11, 12, 13
G_MIX_ROWS = 16
G_IN_SC, G_IN_SH, G_IN_GPRE, G_IN_ROWS = 0, 1, 2, 8
G_LAYER_ROWS = G_MLP_ROWS + G_MIX_ROWS + G_IN_ROWS


def _cparams(dims=None, vmem=None):
    kw = {}
    if dims is not None:
        kw["dimension_semantics"] = dims
    if vmem is not None:
        kw["vmem_limit_bytes"] = int(min(max(vmem, 16 * 1024 * 1024), VMEM_BUDGET))
    return pltpu.CompilerParams(**kw)


def _nbytes(shape, dtype):
    n = 1
    for s in shape:
        n *= s
    return n * jnp.dtype(dtype).itemsize


def _hbm(*arrays):
    return tuple(pltpu.with_memory_space_constraint(a, pltpu.HBM) for a in arrays)


def _resident(block, index_map):
    return pl.BlockSpec(block, index_map, pipeline_mode=pl.Buffered(1))


def _my_position():
    x, y, c = lax.axis_index("x"), lax.axis_index("y"), lax.axis_index("c")
    return (x, y, c), 4 * x + 2 * y + c


def _peer(pos, k):
    x, y, c = pos
    px = 1 - x if k & 4 else x
    py = 1 - y if k & 2 else y
    pc = 1 - c if k & 1 else c
    return (px, py, pc), 4 * px + 2 * py + pc


def _remote(src, dst, ssem, rsem, peer):
    return pltpu.make_async_remote_copy(src_ref=src, dst_ref=dst, send_sem=ssem, recv_sem=rsem, device_id=peer, device_id_type=MESH)


def _dot(a, b):
    return jnp.dot(a, b, preferred_element_type=F32)


def _dot_nt(a, b):
    return lax.dot_general(a, b, (((1,), (1,)), ((), ())), preferred_element_type=F32)


def _dot_tn(a, b):
    return lax.dot_general(a, b, (((0,), (0,)), ((), ())), preferred_element_type=F32)


def _colsum(v):
    return jnp.sum(v, axis=0, keepdims=True)


def _sigmoid(v):
    return jax.nn.sigmoid(v)


def _gelu(v):
    k = 0.7978845608028654
    t = jnp.tanh(k * (v + 0.044715 * (v * v * v)))
    return 0.5 * v * (1.0 + t), t


def _gelu_grad(v, t):
    k = 0.7978845608028654
    return 0.5 * (1.0 + t) + 0.5 * v * (1.0 - t * t) * (k * (1.0 + 3.0 * 0.044715 * v * v))


def _expm1(v):
    u = jnp.exp(v)
    um1 = u - 1.0
    q = um1 * v / jnp.log(u)
    return jnp.where(um1 == 0.0, v, jnp.where(um1 == -1.0, -1.0, q))


def _softplus_neg(lam):
    z = -lam
    u = jnp.exp(-jnp.abs(z))
    w = 1.0 + u
    l1p = jnp.where(w == 1.0, u, jnp.log(w) * u / (w - 1.0))
    return jnp.maximum(z, 0.0) + l1p


def _rms(v):
    return lax.rsqrt(jnp.mean(v * v, axis=-1, keepdims=True) + EPS)


def _prenorm_bwd(xv, dh, g, sc):
    r = _rms(xv)
    xn = xv * r
    n = xn * g
    dsc = _colsum(dh * n)
    dsh = _colsum(dh)
    dn = dh * (1.0 + sc)
    dg = _colsum(dn * xn)
    dxn = dn * g
    dx = r * (dxn - xn * jnp.mean(dxn * xn, axis=-1, keepdims=True))
    return dx, dsc, dsh, dg


def _postnorm_bwd(yv, dout, g, gt):
    r = _rms(yv)
    yn = yv * r
    dgt = _colsum(dout * (yn * g))
    dn = dout * gt
    dg = _colsum(dn * yn)
    dyn = dn * g
    dy = r * (dyn - yn * jnp.mean(dyn * yn, axis=-1, keepdims=True))
    return dy, dgt, dg


def _gates(xc, wr_ref, wi_ref, b_r, b_i, sp, nh, bw):
    xcb = xc.astype(BF16)
    zr = jnp.concatenate([_dot(xcb[:, h * bw:(h + 1) * bw], wr_ref[h]) for h in range(nh)], axis=1) + b_r
    zi = jnp.concatenate([_dot(xcb[:, h * bw:(h + 1) * bw], wi_ref[h]) for h in range(nh)], axis=1) + b_i
    r = _sigmoid(zr)
    ig = _sigmoid(zi)
    la = (-LRU_C * r) * sp
    a = jnp.exp(la)
    mult = jnp.sqrt(-_expm1(2.0 * la))
    return xcb, r, ig, a, mult


def _shift_rows(cur, edge, k, up):
    t, dd = cur.shape
    blocks = cur.reshape(t // SUBLANES, SUBLANES, dd)
    row = lax.broadcasted_iota(jnp.int32, (1, SUBLANES, dd), 1)
    if up:
        r = pltpu.roll(blocks, SUBLANES - k, 1)
        nxt = jnp.concatenate([r[1:], pltpu.roll(edge, SUBLANES - k, 0)[None]], axis=0)
        out = jnp.where(row >= SUBLANES - k, nxt, r)
    else:
        r = pltpu.roll(blocks, k, 1)
        prv = jnp.concatenate([pltpu.roll(edge, k, 0)[None], r[:-1]], axis=0)
        out = jnp.where(row < k, prv, r)
    return out.reshape(t, dd)


def _scan_block(a8, b8, reverse):
    row = lax.broadcasted_iota(jnp.int32, a8.shape, 0)
    for s in (1, 2, 4):
        if reverse:
            keep = row < SUBLANES - s
            a_sh = pltpu.roll(a8, SUBLANES - s, 0)
            b_sh = pltpu.roll(b8, SUBLANES - s, 0)
        else:
            keep = row >= s
            a_sh = pltpu.roll(a8, s, 0)
            b_sh = pltpu.roll(b8, s, 0)
        b8 = b8 + a8 * jnp.where(keep, b_sh, 0.0)
        a8 = a8 * jnp.where(keep, a_sh, 1.0)
    return a8, b8


def _prep_small(c, w_mod, b_mod, cw):
    d = c.shape[1]
    cm = w_mod.shape[2]
    cwid = cw.shape[2]
    nl = w_mod.shape[0]

    def body(c_ref, wm_ref, bm_ref, cw_ref, mod_ref, cact_ref, cwf_ref, cbuf, pbuf, rbuf, ssem, rsem, lsem):
        pos, me = _my_position()
        me8 = pl.multiple_of(me * SUBLANES, SUBLANES)
        cbuf[pl.ds(me8, SUBLANES), :] = jnp.broadcast_to(c_ref[...], (SUBLANES, d))
        own_cw = pltpu.make_async_copy(cw_ref, cwf_ref.at[:, :, pl.ds(me * cwid, cwid)], lsem.at[0])
        own_cw.start()
        first = []
        for k in range(1, N_DEV):
            peer, _ = _peer(pos, k)
            rows = cbuf.at[pl.ds(me8, SUBLANES), :]
            first.append(_remote(rows, rows, ssem.at[0, k - 1], rsem.at[0, k - 1], peer))
            first.append(_remote(cw_ref, cwf_ref.at[:, :, pl.ds(me * cwid, cwid)], ssem.at[1, k - 1], rsem.at[1, k - 1], peer))
        for cp in first:
            cp.start()
        for k in range(1, N_DEV):
            peer, pj = _peer(pos, k)
            pj8 = pl.multiple_of(pj * SUBLANES, SUBLANES)
            rows = cbuf.at[pl.ds(pj8, SUBLANES), :]
            _remote(rows, rows, ssem.at[0, k - 1], rsem.at[0, k - 1], peer).wait_recv()
        cv = cbuf[...]
        cact = cv * _sigmoid(cv)
        cact_ref[...] = cact
        cb = cact.astype(BF16)
        for l in range(nl):
            pbuf[l] = _dot(cb, wm_ref[l].astype(BF16))
        own_p = pltpu.make_async_copy(pbuf.at[:, pl.ds(me8, SUBLANES), :], rbuf.at[me], lsem.at[1])
        own_p.start()
        second = []
        for k in range(1, N_DEV):
            peer, pj = _peer(pos, k)
            pj8 = pl.multiple_of(pj * SUBLANES, SUBLANES)
            second.append(_remote(pbuf.at[:, pl.ds(pj8, SUBLANES), :], rbuf.at[me], ssem.at[2, k - 1], rsem.at[2, k - 1], peer))
        for cp in second:
            cp.start()
        for k in range(1, N_DEV):
            peer, pj = _peer(pos, k)
            _remote(pbuf.at[:, pl.ds(0, SUBLANES), :], rbuf.at[pj], ssem.at[2, k - 1], rsem.at[2, k - 1], peer).wait_recv()
            _remote(cw_ref, cwf_ref.at[:, :, pl.ds(pj * cwid, cwid)], ssem.at[1, k - 1], rsem.at[1, k - 1], peer).wait_recv()
        own_p.wait()
        own_cw.wait()
        for l in range(nl):
            for j in range(N_DEV):
                mod_ref[l:l + 1, j * cm:(j + 1) * cm] = rbuf[j, l, 0:1, :] + bm_ref[l:l + 1, j * cm:(j + 1) * cm]
        for cp in first + second:
            cp.wait_send()

    return pl.pallas_call(
        body,
        name="prep_small",
        out_shape=(
            jax.ShapeDtypeStruct((nl, N_MOD * d), F32),
            jax.ShapeDtypeStruct((N_DEV * SUBLANES, d), F32),
            jax.ShapeDtypeStruct((nl, CW_ROWS, d), F32),
        ),
        in_specs=[VMEM_SPEC] * 4,
        out_specs=(VMEM_SPEC,) * 3,
        scratch_shapes=[
            pltpu.VMEM((N_DEV * SUBLANES, d), F32),
            pltpu.VMEM((nl, N_DEV * SUBLANES, cm), F32),
            pltpu.VMEM((N_DEV, nl, SUBLANES, cm), F32),
            pltpu.SemaphoreType.DMA((3, N_DEV - 1)),
            pltpu.SemaphoreType.DMA((3, N_DEV - 1)),
            pltpu.SemaphoreType.DMA((2,)),
        ],
        compiler_params=_cparams(vmem=3 * _nbytes(w_mod.shape, F32)),
    )(c, w_mod, b_mod, cw)


def _exchange_start(parts, gather, name):
    n = len(parts)
    lands = [lax.empty(((N_DEV,) + tuple(p.shape)) if gather else tuple(p.shape), p.dtype) for p in parts]

    def body(*refs):
        ins, lnd = refs[:n], refs[n:2 * n]
        ssem, rsem, token = refs[2 * n], refs[2 * n + 1], refs[-1]
        pos, me = _my_position()
        for k in range(1, N_DEV):
            peer, pj = _peer(pos, k)
            for t in range(n):
                src = ins[t] if gather else ins[t].at[pj]
                q = t * (N_DEV - 1) + k - 1
                _remote(src, lnd[t].at[me], ssem.at[q], rsem.at[q], peer).start()
        token[...] = jnp.zeros(token.shape, F32)

    out = pl.pallas_call(
        body,
        name=name,
        out_shape=(pltpu.SemaphoreType.DMA((n * (N_DEV - 1),)), pltpu.SemaphoreType.DMA((n * (N_DEV - 1),)))
        + tuple(pltpu.HBM(p.shape, p.dtype) for p in parts) + tuple(pltpu.HBM(p.shape, p.dtype) for p in lands)
        + (jax.ShapeDtypeStruct((SUBLANES, 128), F32),),
        in_specs=[HBM_SPEC] * (2 * n),
        out_specs=(SEM_SPEC, SEM_SPEC) + (HBM_SPEC,) * (2 * n) + (VMEM_SPEC,),
        input_output_aliases={i: 2 + i for i in range(2 * n)},
        compiler_params=pltpu.CompilerParams(has_side_effects=SIDE_EFFECT),
    )(*[pltpu.with_memory_space_constraint(p, pltpu.HBM) for p in list(parts) + lands])
    return out[0], out[1], list(out[2:2 + n]), list(out[2 + n:2 + 2 * n]), out[-1]


def _exchange_wait(started, after, gather, name):
    ssem, rsem, parts, lands, _ = started
    n = len(parts)

    def body(*refs):
        ins, lnd = refs[:n], refs[n:2 * n]
        ssem_ref, rsem_ref = refs[2 * n], refs[2 * n + 1]
        stage, lsem = refs[-1 - n:-1], refs[-1]
        pos, me = _my_position()
        load = []
        for t in range(n):
            src = ins[t] if gather else ins[t].at[me]
            load.append(pltpu.make_async_copy(src, stage[t], lsem.at[t]))
            load[-1].start()
        store = []
        for t in range(n):
            load[t].wait()
            store.append(pltpu.make_async_copy(stage[t], lnd[t].at[me], lsem.at[t]))
            store[-1].start()
        for k in range(1, N_DEV):
            peer, pj = _peer(pos, k)
            for t in range(n):
                src = ins[t] if gather else ins[t].at[pj]
                q = t * (N_DEV - 1) + k - 1
                _remote(src, lnd[t].at[me], ssem_ref.at[q], rsem_ref.at[q], peer).wait_send()
                _remote(src, lnd[t].at[pj], ssem_ref.at[q], rsem_ref.at[q], peer).wait_recv()
        for cp in store:
            cp.wait()

    out = pl.pallas_call(
        body,
        name=name,
        out_shape=tuple(pltpu.HBM(p.shape, p.dtype) for p in parts) + tuple(pltpu.HBM(p.shape, p.dtype) for p in lands),
        in_specs=[HBM_SPEC] * (2 * n) + [SEM_SPEC, SEM_SPEC, ANY_SPEC],
        out_specs=(HBM_SPEC,) * (2 * n),
        input_output_aliases={i: i for i in range(2 * n)},
        scratch_shapes=[pltpu.VMEM(tuple(z.shape[1:]), z.dtype) for z in lands] + [pltpu.SemaphoreType.DMA((n,))],
        compiler_params=pltpu.CompilerParams(has_side_effects=SIDE_EFFECT),
    )(*parts, *lands, ssem, rsem, after)
    return list(out[n:])


CHIP_PEERS = 3


def _chip_peer(pos, k, core):
    x, y, _ = pos
    px = 1 - x if k & 2 else x
    py = 1 - y if k & 1 else y
    return (px, py, core), 4 * px + 2 * py + core


def _gather2_start(parts, name):
    n = len(parts)
    lands = [lax.empty((N_DEV,) + tuple(p.shape), p.dtype) for p in parts]
    per = CHIP_PEERS + 1

    def body(*refs):
        ins, lnd = refs[:n], refs[n:2 * n]
        ssem, rsem, token = refs[2 * n], refs[2 * n + 1], refs[-1]
        pos, me = _my_position()
        sibling = (pos[0], pos[1], 1 - pos[2])
        for t in range(n):
            _remote(ins[t], lnd[t].at[me], ssem.at[per * t], rsem.at[per * t], sibling).start()
        for k in range(1, per):
            peer, _ = _chip_peer(pos, k, pos[2])
            for t in range(n):
                _remote(ins[t], lnd[t].at[me], ssem.at[per * t + k], rsem.at[per * t + k], peer).start()
        token[...] = jnp.zeros(token.shape, F32)

    out = pl.pallas_call(
        body,
        name=name,
        out_shape=(pltpu.SemaphoreType.DMA((n * per,)), pltpu.SemaphoreType.DMA((n * per,)))
        + tuple(pltpu.HBM(p.shape, p.dtype) for p in parts) + tuple(pltpu.HBM(p.shape, p.dtype) for p in lands)
        + (jax.ShapeDtypeStruct((SUBLANES, 128), F32),),
        in_specs=[HBM_SPEC] * (2 * n),
        out_specs=(SEM_SPEC, SEM_SPEC) + (HBM_SPEC,) * (2 * n) + (VMEM_SPEC,),
        input_output_aliases={i: 2 + i for i in range(2 * n)},
        compiler_params=pltpu.CompilerParams(has_side_effects=SIDE_EFFECT),
    )(*[pltpu.with_memory_space_constraint(p, pltpu.HBM) for p in list(parts) + lands])
    return out[0], out[1], list(out[2:2 + n]), list(out[2 + n:2 + 2 * n]), out[-1]


def _gather2_forward(started, after, name):
    ssem, rsem, parts, lands, token = started
    n = len(lands)
    per = CHIP_PEERS + 1

    def body(*refs):
        lnd = refs[:n]
        rsem_a, fsend, frecv = refs[n], refs[2 * n + 2], refs[2 * n + 3]
        pos, _ = _my_position()
        sibling = (pos[0], pos[1], 1 - pos[2])
        for k in range(1, per):
            peer, pk = _chip_peer(pos, k, pos[2])
            for t in range(n):
                block = lnd[t].at[pk]
                _remote(block, block, rsem_a.at[per * t + k], rsem_a.at[per * t + k], peer).wait_recv()
                q = CHIP_PEERS * t + k - 1
                _remote(block, block, fsend.at[q], frecv.at[q], sibling).start()

    out = pl.pallas_call(
        body,
        name=name,
        out_shape=tuple(pltpu.HBM(p.shape, p.dtype) for p in lands)
        + (pltpu.SemaphoreType.DMA((n * CHIP_PEERS,)), pltpu.SemaphoreType.DMA((n * CHIP_PEERS,))),
        in_specs=[HBM_SPEC] * n + [SEM_SPEC, ANY_SPEC],
        out_specs=(HBM_SPEC,) * n + (SEM_SPEC, SEM_SPEC),
        input_output_aliases={i: i for i in range(n)},
        compiler_params=pltpu.CompilerParams(has_side_effects=SIDE_EFFECT),
    )(*lands, rsem, after)
    return ssem, rsem, parts, list(out[:n]), token, out[n], out[n + 1]


def _gather2_wait(forwarded, after, name):
    ssem, rsem, parts, lands, _, fsend, frecv = forwarded
    n = len(parts)
    per = CHIP_PEERS + 1

    def body(*refs):
        ins, lnd = refs[:n], refs[n:2 * n]
        ssem_a, rsem_a, fs, fr = refs[2 * n:2 * n + 4]
        stage, lsem = refs[-1 - n:-1], refs[-1]
        pos, me = _my_position()
        sibling = (pos[0], pos[1], 1 - pos[2])
        sib = 4 * pos[0] + 2 * pos[1] + 1 - pos[2]
        load = []
        for t in range(n):
            load.append(pltpu.make_async_copy(ins[t], stage[t], lsem.at[t]))
            load[-1].start()
        store = []
        for t in range(n):
            load[t].wait()
            store.append(pltpu.make_async_copy(stage[t], lnd[t].at[me], lsem.at[t]))
            store[-1].start()
        for t in range(n):
            _remote(ins[t], lnd[t].at[me], ssem_a.at[per * t], rsem_a.at[per * t], sibling).wait_send()
            _remote(ins[t], lnd[t].at[sib], ssem_a.at[per * t], rsem_a.at[per * t], sibling).wait_recv()
        for k in range(1, per):
            peer, pk = _chip_peer(pos, k, pos[2])
            _, qk = _chip_peer(pos, k, 1 - pos[2])
            for t in range(n):
                q = CHIP_PEERS * t + k - 1
                _remote(ins[t], lnd[t].at[me], ssem_a.at[per * t + k], rsem_a.at[per * t + k], peer).wait_send()
                _remote(lnd[t].at[pk], lnd[t].at[pk], fs.at[q], fr.at[q], sibling).wait_send()
                _remote(lnd[t].at[qk], lnd[t].at[qk], fs.at[q], fr.at[q], sibling).wait_recv()
        for cp in store:
            cp.wait()

    out = pl.pallas_call(
        body,
        name=name,
        out_shape=tuple(pltpu.HBM(p.shape, p.dtype) for p in parts) + tuple(pltpu.HBM(p.shape, p.dtype) for p in lands),
        in_specs=[HBM_SPEC] * (2 * n) + [SEM_SPEC] * 4 + [ANY_SPEC],
        out_specs=(HBM_SPEC,) * (2 * n),
        input_output_aliases={i: i for i in range(2 * n)},
        scratch_shapes=[pltpu.VMEM(tuple(z.shape[1:]), z.dtype) for z in lands] + [pltpu.SemaphoreType.DMA((n,))],
        compiler_params=pltpu.CompilerParams(has_side_effects=SIDE_EFFECT),
    )(*parts, *lands, ssem, rsem, fsend, frecv, after)
    return list(out[n:])


def _reduce_small(rows, dm8, cact, after):
    r, d = rows.shape
    nl = dm8.shape[0]
    cm = dm8.shape[2] // N_DEV

    def body(rows_ref, dm_ref, cact_ref, after_ref, orow_ref, owm_ref, gr, dmr, ssem, rsem, lsem):
        pos, me = _my_position()
        me8 = pl.multiple_of(me * SUBLANES, SUBLANES)
        gr[me] = rows_ref[...]
        own_dm = pltpu.make_async_copy(dm_ref.at[:, :, pl.ds(me * cm, cm)], dmr.at[:, pl.ds(me8, SUBLANES), :], lsem.at[0])
        own_dm.start()
        sends = []
        for k in range(1, N_DEV):
            peer, pj = _peer(pos, k)
            sends.append(_remote(gr.at[me], gr.at[me], ssem.at[0, k - 1], rsem.at[0, k - 1], peer))
            sends.append(_remote(dm_ref.at[:, :, pl.ds(pj * cm, cm)], dmr.at[:, pl.ds(me8, SUBLANES), :],
                                 ssem.at[1, k - 1], rsem.at[1, k - 1], peer))
        for cp in sends:
            cp.start()
        for k in range(1, N_DEV):
            peer, pj = _peer(pos, k)
            pj8 = pl.multiple_of(pj * SUBLANES, SUBLANES)
            _remote(gr.at[pj], gr.at[pj], ssem.at[0, k - 1], rsem.at[0, k - 1], peer).wait_recv()
            _remote(dm_ref.at[:, :, pl.ds(0, cm)], dmr.at[:, pl.ds(pj8, SUBLANES), :], ssem.at[1, k - 1], rsem.at[1, k - 1], peer).wait_recv()
        own_dm.wait()
        acc = gr[0]
        for j in range(1, N_DEV):
            acc = acc + gr[j]
        orow_ref[...] = acc
        cb = cact_ref[...].astype(BF16)
        for l in range(nl):
            owm_ref[l] = _dot_tn(cb, dmr[l].astype(BF16))
        for cp in sends:
            cp.wait_send()

    return pl.pallas_call(
        body,
        name="reduce_small",
        out_shape=(jax.ShapeDtypeStruct((r, d), F32), jax.ShapeDtypeStruct((nl, d, cm), F32)),
        in_specs=[VMEM_SPEC] * 3 + [ANY_SPEC],
        out_specs=(VMEM_SPEC,) * 2,
        scratch_shapes=[
            pltpu.VMEM((N_DEV, r, d), F32),
            pltpu.VMEM((nl, N_DEV * SUBLANES, cm), F32),
            pltpu.SemaphoreType.DMA((2, N_DEV - 1)),
            pltpu.SemaphoreType.DMA((2, N_DEV - 1)),
            pltpu.SemaphoreType.DMA((1,)),
        ],
        compiler_params=_cparams(vmem=4 * _nbytes((N_DEV, r, d), F32) + 6 * _nbytes((nl, d, cm), F32)),
    )(rows, dm8, cact, after)


def _sum_gathered(zones):
    nl = len(zones)

    def body(*refs):
        for l in range(nl):
            acc = refs[l][0].astype(F32)
            for j in range(1, N_DEV):
                acc = acc + refs[l][j].astype(F32)
            refs[nl][l] = acc

    return pl.pallas_call(
        body,
        name="sum_gathered",
        out_shape=jax.ShapeDtypeStruct((nl,) + tuple(zones[0].shape[1:]), F32),
        in_specs=[VMEM_SPEC] * nl,
        out_specs=VMEM_SPEC,
        compiler_params=_cparams(vmem=8 * nl * _nbytes(zones[0].shape, BF16)),
    )(*zones)


def _in_proj_fwd(x, mod, rows, win_f, l, tm):
    s, d = x.shape
    nb, _, ci = win_f.shape

    def body(x_ref, mod_ref, rows_ref, w_ref, proj_ref, h_ref):
        xv = x_ref[...]
        g = rows_ref[R_G_PRE_MIX:R_G_PRE_MIX + 1, :]
        h = (xv * _rms(xv) * g) * (1.0 + mod_ref[M_SC_M:M_SC_M + 1, :]) + mod_ref[M_SH_M:M_SH_M + 1, :]
        hb = h.astype(BF16)
        h_ref[...] = hb
        for j in range(nb):
            proj_ref[:, j * ci:(j + 1) * ci] = _dot(hb, w_ref[j])

    return pl.pallas_call(
        body,
        name="in_proj_fwd",
        grid=(s // tm,),
        in_specs=[
            pl.BlockSpec((tm, d), lambda i: (i, 0)),
            _resident((None, N_MOD, d), lambda i: (l, 0, 0)),
            _resident((None, N_ROWS, d), lambda i: (l, 0, 0)),
            _resident((nb, d, ci), lambda i: (0, 0, 0)),
        ],
        out_specs=(pl.BlockSpec((tm, nb * ci), lambda i: (i, 0)), pl.BlockSpec((tm, d), lambda i: (i, 0))),
        out_shape=(jax.ShapeDtypeStruct((s, nb * ci), F32), jax.ShapeDtypeStruct((s, d), BF16)),
        compiler_params=_cparams(("parallel",), _nbytes((nb, d, ci), BF16) + 3 * _nbytes((tm, nb * ci), F32) + 8 * _nbytes((tm, d), F32)),
    )(*_hbm(x), mod, rows, win_f)


def _mixer_core_fwd(proj, x, mod, rows, cwf, wr, wi, wa_f, wb_f, wo_f, l, tm):
    s, d = x.shape
    nh, bw, _ = wr.shape[1:]

    def body(proj_ref, x_ref, mod_ref, rows_ref, cw_ref, wr_ref, wi_ref, wa_ref, wb_ref, wo_ref,
             x1_ref, hs_ref, yap_ref, ybp_ref, y_ref, cvbuf, xbbuf, a_s, b_s, hprev):
        i = pl.program_id(0)

        @pl.when(i == 0)
        def _():
            cvbuf[...] = jnp.zeros((SUBLANES, d), F32)
            xbbuf[...] = jnp.zeros((SUBLANES, d), F32)
            hprev[...] = jnp.zeros((SUBLANES, d), F32)

        def row(r):
            return rows_ref[r:r + 1, :]

        def tap(r):
            return cw_ref[r:r + 1, :]

        ba = proj_ref[:, 0:d]
        cv = proj_ref[:, d:2 * d] * proj_ref[:, 2 * d:3 * d]
        cvt = cvbuf[...]
        conv3 = ((row(R_CONV_A_B) + _shift_rows(cv, cvt, 2, False) * tap(CW_A)) + _shift_rows(cv, cvt, 1, False) * tap(CW_A + 1)) + cv * tap(CW_A + 2)
        ya = ba * conv3
        cvbuf[...] = cv[tm - SUBLANES:, :]
        xb = proj_ref[:, 3 * d:4 * d]
        xbt = xbbuf[...]
        xc = (((row(R_CONV_B_B) + _shift_rows(xb, xbt, 3, False) * tap(CW_B)) + _shift_rows(xb, xbt, 2, False) * tap(CW_B + 1))
              + _shift_rows(xb, xbt, 1, False) * tap(CW_B + 2)) + xb * tap(CW_B + 3)
        xbbuf[...] = xb[tm - SUBLANES:, :]
        sp = _softplus_neg(row(R_LAMBDA))
        _, _, ig, a, mult = _gates(xc, wr_ref, wi_ref, row(R_B_GATE_R), row(R_B_GATE_I), sp, nh, bw)
        a_s[...] = a
        b_s[...] = mult * (ig * xc)

        def blk(j, hp):
            o = pl.multiple_of(j * SUBLANES, SUBLANES)
            a8, b8 = _scan_block(a_s[pl.ds(o, SUBLANES), :], b_s[pl.ds(o, SUBLANES), :], reverse=False)
            h8 = b8 + a8 * hp
            hs_ref[pl.ds(o, SUBLANES), :] = h8
            return jnp.broadcast_to(h8[SUBLANES - 1:SUBLANES, :], (SUBLANES, d))

        hprev[...] = lax.fori_loop(0, tm // SUBLANES, blk, hprev[...])
        gel, _ = _gelu(proj_ref[:, 4 * d:5 * d])
        yb = hs_ref[...] * gel
        yap = _dot(ya.astype(BF16), wa_ref[...])
        ybp = _dot(yb.astype(BF16), wb_ref[...])
        yap_ref[...] = yap
        ybp_ref[...] = ybp
        m = _sigmoid(proj_ref[:, 5 * d:6 * d]) * yap + _sigmoid(proj_ref[:, 6 * d:7 * d]) * ybp
        y = _dot(m.astype(BF16), wo_ref[...])
        y_ref[...] = y
        x1_ref[...] = x_ref[...] + mod_ref[M_GT_M:M_GT_M + 1, :] * ((y * _rms(y)) * row(R_G_POST_MIX))

    tile = pl.BlockSpec((tm, d), lambda i: (i, 0))
    return pl.pallas_call(
        body,
        name="mixer_core_fwd",
        grid=(s // tm,),
        in_specs=[
            pl.BlockSpec((tm, 7 * d), lambda i: (i, 0)),
            tile,
            _resident((None, N_MOD, d), lambda i: (l, 0, 0)),
            _resident((None, N_ROWS, d), lambda i: (l, 0, 0)),
            _resident((None, CW_ROWS, d), lambda i: (l, 0, 0)),
            _resident((None, nh, bw, bw), lambda i: (l, 0, 0, 0)),
            _resident((None, nh, bw, bw), lambda i: (l, 0, 0, 0)),
            _resident((d, d), lambda i: (0, 0)),
            _resident((d, d), lambda i: (0, 0)),
            _resident((d, d), lambda i: (0, 0)),
        ],
        out_specs=(tile,) * 5,
        out_shape=(jax.ShapeDtypeStruct((s, d), F32),) * 5,
        scratch_shapes=[
            pltpu.VMEM((SUBLANES, d), F32),
            pltpu.VMEM((SUBLANES, d), F32),
            pltpu.VMEM((tm, d), F32),
            pltpu.VMEM((tm, d), F32),
            pltpu.VMEM((SUBLANES, d), F32),
        ],
        compiler_params=_cparams(("arbitrary",), 3 * _nbytes((d, d), BF16) + 2 * _nbytes((tm, 7 * d), F32) + 40 * _nbytes((tm, d), F32)),
    )(*_hbm(proj, x), mod, rows, cwf, wr, wi, wa_f, wb_f, wo_f)


def _mlp_fwd(x1, mod, rows, wup_f, wdn_f, l, tm):
    s, d = x1.shape
    nb, _, cu = wup_f.shape
    dff = nb * cu

    def body(x1_ref, mod_ref, rows_ref, wu_ref, wd_ref, x2_ref, ru_ref, y2_ref, h2_ref):
        xv = x1_ref[...]
        g = rows_ref[R_G_PRE_MLP:R_G_PRE_MLP + 1, :]
        h2 = ((xv * _rms(xv) * g) * (1.0 + mod_ref[M_SC_F:M_SC_F + 1, :]) + mod_ref[M_SH_F:M_SH_F + 1, :]).astype(BF16)
        h2_ref[...] = h2
        ru = jnp.concatenate([jnp.maximum(_dot(h2, wu_ref[j]), 0.0) for j in range(nb)], axis=1)
        ru_ref[...] = ru.astype(BF16)
        y2 = _dot((ru * ru).astype(BF16), wd_ref[...])
        y2_ref[...] = y2
        x2_ref[...] = xv + mod_ref[M_GT_F:M_GT_F + 1, :] * ((y2 * _rms(y2)) * rows_ref[R_G_POST_MLP:R_G_POST_MLP + 1, :])

    tile = pl.BlockSpec((tm, d), lambda i: (i, 0))
    wide = pl.BlockSpec((tm, dff), lambda i: (i, 0))
    return pl.pallas_call(
        body,
        name="mlp_fwd",
        grid=(s // tm,),
        in_specs=[
            tile,
            _resident((None, N_MOD, d), lambda i: (l, 0, 0)),
            _resident((None, N_ROWS, d), lambda i: (l, 0, 0)),
            _resident((nb, d, cu), lambda i: (0, 0, 0)),
            _resident((dff, d), lambda i: (0, 0)),
        ],
        out_specs=(tile, wide, tile, tile),
        out_shape=(jax.ShapeDtypeStruct((s, d), F32), jax.ShapeDtypeStruct((s, dff), BF16),
                   jax.ShapeDtypeStruct((s, d), F32), jax.ShapeDtypeStruct((s, d), BF16)),
        compiler_params=_cparams(("parallel",), 2 * _nbytes((dff, d), BF16) + 5 * _nbytes((tm, dff), F32) + 12 * _nbytes((tm, d), F32)),
    )(*_hbm(x1), mod, rows, wup_f, wdn_f)


def _loss_fwd_bwd(y, target, tm):
    s, d = y.shape

    def body(y_ref, t_ref, loss_ref, dy_ref):
        @pl.when(pl.program_id(0) == 0)
        def _():
            loss_ref[...] = jnp.zeros(loss_ref.shape, F32)

        e = y_ref[...] - t_ref[...]
        dy_ref[...] = e * (1.0 / d)
        loss_ref[...] += 0.5 * jnp.sum(jnp.mean(e * e, axis=-1, keepdims=True), axis=0, keepdims=True)

    tile = pl.BlockSpec((tm, d), lambda i: (i, 0))
    loss, dy = pl.pallas_call(
        body,
        name="loss",
        grid=(s // tm,),
        in_specs=[tile, tile],
        out_specs=(pl.BlockSpec((SUBLANES, 128), lambda i: (0, 0)), tile),
        out_shape=(jax.ShapeDtypeStruct((SUBLANES, 128), F32), jax.ShapeDtypeStruct((s, d), F32)),
        compiler_params=_cparams(("arbitrary",)),
    )(*_hbm(y, target))
    return loss[0, 0], dy


def _mlp_bwd(dx2, x1, y2, ru, mod, rows, wup_f, wdn_f, l, tm):
    s, d = x1.shape
    nb, _, cu = wup_f.shape
    dff = nb * cu

    def body(dx2_ref, x1_ref, y2_ref, ru_ref, mod_ref, rows_ref, wu_ref, wd_ref, dx1_ref, dy2_ref, dup_ref, act_ref, sm_ref):
        @pl.when(pl.program_id(0) == 0)
        def _():
            sm_ref[...] = jnp.zeros(sm_ref.shape, F32)

        dout = dx2_ref[...]
        dy2, dgt, dgpost = _postnorm_bwd(y2_ref[...], dout, rows_ref[R_G_POST_MLP:R_G_POST_MLP + 1, :], mod_ref[M_GT_F:M_GT_F + 1, :])
        dy2b = dy2.astype(BF16)
        dy2_ref[...] = dy2b
        ruv = ru_ref[...].astype(F32)
        act_ref[...] = (ruv * ruv).astype(BF16)
        dup = (_dot_nt(dy2b, wd_ref[...]) * (2.0 * ruv)).astype(BF16)
        dup_ref[...] = dup
        dh2 = _dot_nt(dup[:, 0:cu], wu_ref[0])
        for j in range(1, nb):
            dh2 = dh2 + _dot_nt(dup[:, j * cu:(j + 1) * cu], wu_ref[j])
        dxn, dsc, dsh, dgpre = _prenorm_bwd(x1_ref[...], dh2, rows_ref[R_G_PRE_MLP:R_G_PRE_MLP + 1, :], mod_ref[M_SC_F:M_SC_F + 1, :])
        dx1_ref[...] = dout + dxn
        for r, v in ((G_MLP_GT, dgt), (G_MLP_GPOST, dgpost), (G_MLP_SC, dsc), (G_MLP_SH, dsh), (G_MLP_GPRE, dgpre)):
            sm_ref[r:r + 1, :] += v

    tile = pl.BlockSpec((tm, d), lambda i: (i, 0))
    wide = pl.BlockSpec((tm, dff), lambda i: (i, 0))
    return pl.pallas_call(
        body,
        name="mlp_bwd",
        grid=(s // tm,),
        in_specs=[
            tile, tile, tile, wide,
            _resident((None, N_MOD, d), lambda i: (l, 0, 0)),
            _resident((None, N_ROWS, d), lambda i: (l, 0, 0)),
            _resident((nb, d, cu), lambda i: (0, 0, 0)),
            _resident((dff, d), lambda i: (0, 0)),
        ],
        out_specs=(tile, tile, wide, wide, pl.BlockSpec((G_MLP_ROWS, d), lambda i: (0, 0))),
        out_shape=(jax.ShapeDtypeStruct((s, d), F32), jax.ShapeDtypeStruct((s, d), BF16), jax.ShapeDtypeStruct((s, dff), BF16),
                   jax.ShapeDtypeStruct((s, dff), BF16), jax.ShapeDtypeStruct((G_MLP_ROWS, d), F32)),
        compiler_params=_cparams(("arbitrary",), 2 * _nbytes((dff, d), BF16) + 6 * _nbytes((tm, dff), F32) + 16 * _nbytes((tm, d), F32)),
    )(*_hbm(dx2, x1, y2, ru), mod, rows, wup_f, wdn_f)


def _mixer_core_bwd(dx1, y, yap, ybp, hs, proj, mod, rows, cwf, wr, wi, wa_f, wb_f, wo_f, l, tm):
    s, d = dx1.shape
    nh, bw, _ = wr.shape[1:]
    nt = s // tm
    per = tm // SUBLANES

    def body(dx1_ref, y_ref, yap_ref, ybp_ref, hs_ref, hsh_ref, proj_ref, projh_ref, mod_ref, rows_ref, cw_ref,
             wr_ref, wi_ref, wa_ref, wb_ref, wo_ref,
             dproj_ref, dy_ref, m_ref, dyap_ref, dybp_ref, ya_ref, yb_ref, sm_ref, dwg_ref,
             abuf, dcbuf, dxbuf, al_s, dh_s, lam_s, lnext):
        i = pl.program_id(0)
        first_tile = i == nt - 1

        @pl.when(i == 0)
        def _():
            sm_ref[...] = jnp.zeros(sm_ref.shape, F32)
            dwg_ref[...] = jnp.zeros(dwg_ref.shape, F32)
            zero = jnp.zeros((SUBLANES, d), F32)
            abuf[...] = zero
            dcbuf[...] = zero
            dxbuf[...] = zero
            lnext[...] = zero

        def row(r):
            return rows_ref[r:r + 1, :]

        def tap(r):
            return cw_ref[r:r + 1, :]

        def acc(r, v):
            sm_ref[r:r + 1, :] += v

        keep_halo = jnp.where(first_tile, 0.0, 1.0)
        dy, dgt, dgpost = _postnorm_bwd(y_ref[...], dx1_ref[...], row(R_G_POST_MIX), mod_ref[M_GT_M:M_GT_M + 1, :])
        acc(G_MIX_GT, dgt)
        acc(G_MIX_GPOST, dgpost)
        dyb16 = dy.astype(BF16)
        dy_ref[...] = dyb16
        dm = _dot_nt(dyb16, wo_ref[...])
        sa = _sigmoid(proj_ref[:, 5 * d:6 * d])
        sb = _sigmoid(proj_ref[:, 6 * d:7 * d])
        yap = yap_ref[...]
        ybp = ybp_ref[...]
        m_ref[...] = (sa * yap + sb * ybp).astype(BF16)
        dyap = (dm * sa).astype(BF16)
        dybp = (dm * sb).astype(BF16)
        dyap_ref[...] = dyap
        dybp_ref[...] = dybp
        dproj_ref[:, 5 * d:6 * d] = (dm * yap * sa * (1.0 - sa)).astype(BF16)
        dproj_ref[:, 6 * d:7 * d] = (dm * ybp * sb * (1.0 - sb)).astype(BF16)
        dya = _dot_nt(dyap, wa_ref[...])
        dyb = _dot_nt(dybp, wb_ref[...])
        ba = proj_ref[:, 0:d]
        ca = proj_ref[:, d:2 * d]
        va = proj_ref[:, 2 * d:3 * d]
        cv = ca * va
        cvh = keep_halo * (projh_ref[:, d:2 * d] * projh_ref[:, 2 * d:3 * d])
        cvm2 = _shift_rows(cv, cvh, 2, False)
        cvm1 = _shift_rows(cv, cvh, 1, False)
        conv3 = ((row(R_CONV_A_B) + cvm2 * tap(CW_A)) + cvm1 * tap(CW_A + 1)) + cv * tap(CW_A + 2)
        ya_ref[...] = (ba * conv3).astype(BF16)
        dproj_ref[:, 0:d] = (dya * conv3).astype(BF16)
        dc3 = dya * ba
        acc(G_MIX_CAB, _colsum(dc3))
        acc(G_MIX_CAW, _colsum(dc3 * cvm2))
        acc(G_MIX_CAW + 1, _colsum(dc3 * cvm1))
        acc(G_MIX_CAW + 2, _colsum(dc3 * cv))
        dct = dcbuf[...]
        dcv = (dc3 * tap(CW_A + 2) + _shift_rows(dc3, dct, 1, True) * tap(CW_A + 1)) + _shift_rows(dc3, dct, 2, True) * tap(CW_A)
        dcbuf[...] = dc3[:SUBLANES, :]
        dproj_ref[:, d:2 * d] = (dcv * va).astype(BF16)
        dproj_ref[:, 2 * d:3 * d] = (dcv * ca).astype(BF16)
        xb = proj_ref[:, 3 * d:4 * d]
        gb = proj_ref[:, 4 * d:5 * d]
        xbh = keep_halo * projh_ref[:, 3 * d:4 * d]
        xm3 = _shift_rows(xb, xbh, 3, False)
        xm2 = _shift_rows(xb, xbh, 2, False)
        xm1 = _shift_rows(xb, xbh, 1, False)
        xc = (((row(R_CONV_B_B) + xm3 * tap(CW_B)) + xm2 * tap(CW_B + 1)) + xm1 * tap(CW_B + 2)) + xb * tap(CW_B + 3)
        lam = row(R_LAMBDA)
        sp = _softplus_neg(lam)
        xcb, r, ig, a, mult = _gates(xc, wr_ref, wi_ref, row(R_B_GATE_R), row(R_B_GATE_I), sp, nh, bw)
        gel, th = _gelu(gb)
        hs = hs_ref[...]
        yb_ref[...] = (hs * gel).astype(BF16)
        dproj_ref[:, 4 * d:5 * d] = (dyb * hs * _gelu_grad(gb, th)).astype(BF16)
        al_s[...] = _shift_rows(a, abuf[...], 1, True)
        abuf[...] = a[:SUBLANES, :]
        dh_s[...] = dyb * gel

        def blk(j, ln):
            o = pl.multiple_of((per - 1 - j) * SUBLANES, SUBLANES)
            a8, b8 = _scan_block(al_s[pl.ds(o, SUBLANES), :], dh_s[pl.ds(o, SUBLANES), :], reverse=True)
            l8 = b8 + a8 * ln
            lam_s[pl.ds(o, SUBLANES), :] = l8
            return jnp.broadcast_to(l8[0:1, :], (SUBLANES, d))

        lnext[...] = lax.fori_loop(0, per, blk, lnext[...])
        dbb = lam_s[...]
        da = dbb * _shift_rows(hs, keep_halo * hsh_ref[...], 1, False)
        dmult = dbb * (ig * xc)
        dig = dbb * (mult * xc)
        dxc = dbb * (mult * ig)
        dla = da * a - dmult * (a * a) / mult
        acc(G_MIX_LAM, _colsum(dla * (-LRU_C * r)) * (-_sigmoid(-lam)))
        dzr = (dla * (-LRU_C * sp)) * r * (1.0 - r)
        dzi = dig * ig * (1.0 - ig)
        acc(G_MIX_BR, _colsum(dzr))
        acc(G_MIX_BI, _colsum(dzi))
        dzrb = dzr.astype(BF16)
        dzib = dzi.astype(BF16)
        back = []
        for h in range(nh):
            sl = slice(h * bw, (h + 1) * bw)
            back.append(_dot_nt(dzrb[:, sl], wr_ref[h]) + _dot_nt(dzib[:, sl], wi_ref[h]))
            dwg_ref[0, h] += _dot_tn(xcb[:, sl], dzrb[:, sl])
            dwg_ref[1, h] += _dot_tn(xcb[:, sl], dzib[:, sl])
        dxc = dxc + jnp.concatenate(back, axis=1)
        acc(G_MIX_CBB, _colsum(dxc))
        acc(G_MIX_CBW, _colsum(dxc * xm3))
        acc(G_MIX_CBW + 1, _colsum(dxc * xm2))
        acc(G_MIX_CBW + 2, _colsum(dxc * xm1))
        acc(G_MIX_CBW + 3, _colsum(dxc * xb))
        dxt = dxbuf[...]
        dxb = (((dxc * tap(CW_B + 3) + _shift_rows(dxc, dxt, 1, True) * tap(CW_B + 2)) + _shift_rows(dxc, dxt, 2, True) * tap(CW_B + 1))
               + _shift_rows(dxc, dxt, 3, True) * tap(CW_B))
        dxbuf[...] = dxc[:SUBLANES, :]
        dproj_ref[:, 3 * d:4 * d] = dxb.astype(BF16)

    def rev(i):
        return (nt - 1 - i, 0)

    def halo(i):
        return (jnp.maximum((nt - 1 - i) * per - 1, 0), 0)

    tile = pl.BlockSpec((tm, d), rev)
    return pl.pallas_call(
        body,
        name="mixer_core_bwd",
        grid=(nt,),
        in_specs=[
            tile, tile, tile, tile, tile,
            pl.BlockSpec((SUBLANES, d), halo),
            pl.BlockSpec((tm, 7 * d), rev),
            pl.BlockSpec((SUBLANES, 7 * d), halo),
            _resident((None, N_MOD, d), lambda i: (l, 0, 0)),
            _resident((None, N_ROWS, d), lambda i: (l, 0, 0)),
            _resident((None, CW_ROWS, d), lambda i: (l, 0, 0)),
            _resident((None, nh, bw, bw), lambda i: (l, 0, 0, 0)),
            _resident((None, nh, bw, bw), lambda i: (l, 0, 0, 0)),
            _resident((d, d), lambda i: (0, 0)),
            _resident((d, d), lambda i: (0, 0)),
            _resident((d, d), lambda i: (0, 0)),
        ],
        out_specs=(pl.BlockSpec((tm, 7 * d), rev),) + (tile,) * 6 + (
            pl.BlockSpec((G_MIX_ROWS, d), lambda i: (0, 0)), pl.BlockSpec((2, nh, bw, bw), lambda i: (0, 0, 0, 0))),
        out_shape=(jax.ShapeDtypeStruct((s, 7 * d), BF16),) + (jax.ShapeDtypeStruct((s, d), BF16),) * 6 + (
            jax.ShapeDtypeStruct((G_MIX_ROWS, d), F32), jax.ShapeDtypeStruct((2, nh, bw, bw), F32)),
        scratch_shapes=[pltpu.VMEM((SUBLANES, d), F32)] * 3 + [pltpu.VMEM((tm, d), F32)] * 3 + [pltpu.VMEM((SUBLANES, d), F32)],
        compiler_params=_cparams(("arbitrary",), 3 * _nbytes((d, d), BF16) + 3 * _nbytes((tm, 7 * d), F32) + 64 * _nbytes((tm, d), F32)),
    )(*_hbm(dx1, y, yap, ybp, hs, hs, proj, proj), mod, rows, cwf, wr, wi, wa_f, wb_f, wo_f)


def _in_proj_bwd(dproj, x, dx1, mod, rows, win_f, l, tm):
    s, d = x.shape
    nb, _, ci = win_f.shape

    def body(dp_ref, x_ref, dx1_ref, mod_ref, rows_ref, w_ref, dx_ref, sm_ref):
        @pl.when(pl.program_id(0) == 0)
        def _():
            sm_ref[...] = jnp.zeros(sm_ref.shape, F32)

        dh = _dot_nt(dp_ref[:, 0:ci], w_ref[0])
        for j in range(1, nb):
            dh = dh + _dot_nt(dp_ref[:, j * ci:(j + 1) * ci], w_ref[j])
        dxn, dsc, dsh, dg = _prenorm_bwd(x_ref[...], dh, rows_ref[R_G_PRE_MIX:R_G_PRE_MIX + 1, :], mod_ref[M_SC_M:M_SC_M + 1, :])
        dx_ref[...] = dx1_ref[...] + dxn
        for r, v in ((G_IN_SC, dsc), (G_IN_SH, dsh), (G_IN_GPRE, dg)):
            sm_ref[r:r + 1, :] += v

    tile = pl.BlockSpec((tm, d), lambda i: (i, 0))
    return pl.pallas_call(
        body,
        name="in_proj_bwd",
        grid=(s // tm,),
        in_specs=[
            pl.BlockSpec((tm, nb * ci), lambda i: (i, 0)), tile, tile,
            _resident((None, N_MOD, d), lambda i: (l, 0, 0)),
            _resident((None, N_ROWS, d), lambda i: (l, 0, 0)),
            _resident((nb, d, ci), lambda i: (0, 0, 0)),
        ],
        out_specs=(tile, pl.BlockSpec((G_IN_ROWS, d), lambda i: (0, 0))),
        out_shape=(jax.ShapeDtypeStruct((s, d), F32), jax.ShapeDtypeStruct((G_IN_ROWS, d), F32)),
        compiler_params=_cparams(("arbitrary",), _nbytes((nb, d, ci), BF16) + 2 * _nbytes((tm, nb * ci), BF16) + 16 * _nbytes((tm, d), F32)),
    )(*_hbm(dproj, x, dx1), mod, rows, win_f)


def _wgrad(a, b, cols_owned, ts, after):
    s, k1 = a.shape
    k2 = b.shape[1]
    ns = s // ts
    if cols_owned:
        nblk, bk1, bk2 = N_DEV, k1, k2 // N_DEV
        a_spec = pl.BlockSpec((ts, bk1), lambda j, t: (t, 0))
        b_spec = pl.BlockSpec((ts, bk2), lambda j, t: (t, j))
    else:
        bk1, bk2 = min(WGRAD_ROWS, k1), k2
        nblk = k1 // bk1
        a_spec = pl.BlockSpec((ts, bk1), lambda j, t: (t, j))
        b_spec = pl.BlockSpec((ts, bk2), lambda j, t: (t, 0))

    def body(a_ref, b_ref, after_ref, o_ref, acc_ref):
        t = pl.program_id(1)

        @pl.when(t == 0)
        def _():
            acc_ref[...] = jnp.zeros(acc_ref.shape, F32)

        acc_ref[...] += _dot_tn(a_ref[...], b_ref[...])

        @pl.when(t == ns - 1)
        def _():
            o_ref[...] = acc_ref[...].astype(BF16)

    out = pl.pallas_call(
        body,
        name="wgrad",
        grid=(nblk, ns),
        in_specs=[a_spec, b_spec, ANY_SPEC],
        out_specs=pl.BlockSpec((None, bk1, bk2), lambda j, t: (j, 0, 0)),
        out_shape=pltpu.HBM((nblk, bk1, bk2), BF16),
        scratch_shapes=[pltpu.VMEM((bk1, bk2), F32)],
        compiler_params=_cparams(("parallel", "arbitrary"), 4 * _nbytes((bk1, bk2), F32) + 4 * _nbytes((ts, bk1 + bk2), BF16)),
    )(pltpu.with_memory_space_constraint(a, pltpu.HBM), pltpu.with_memory_space_constraint(b, pltpu.HBM), after)
    return out if cols_owned else out.reshape(N_DEV, k1 // N_DEV, k2)


def _adam_update(w, g, m, v):
    m = ADAM_B1 * m + (1.0 - ADAM_B1) * g
    v = ADAM_B2 * v + (1.0 - ADAM_B2) * (g * g)
    m_hat = m / (1.0 - ADAM_B1 ** ADAM_STEP)
    v_hat = v / (1.0 - ADAM_B2 ** ADAM_STEP)
    delta = -ADAM_LR * (m_hat / (jnp.sqrt(v_hat) + ADAM_EPS) + ADAM_WD * w)
    return delta, m, v


def _sum_adamw(recv, w, m, v, tr, after):
    nl, ra, cb = w.shape
    assert nl == len(recv) == 2

    def body(r0_ref, r1_ref, w_ref, m_ref, v_ref, after_ref, g_ref, d_ref, nm_ref, nv_ref):
        def total(r_ref):
            g = r_ref[0].astype(F32)
            for j in range(1, N_DEV):
                g = g + r_ref[j].astype(F32)
            return g

        g = jnp.where(pl.program_id(0) == 0, total(r0_ref), total(r1_ref))
        g_ref[...] = g
        d_ref[...], nm_ref[...], nv_ref[...] = _adam_update(w_ref[...], g, m_ref[...], v_ref[...])

    blk = pl.BlockSpec((None, tr, cb), lambda l, i: (l, i, 0))
    return pl.pallas_call(
        body,
        name="sum_adamw",
        grid=(nl, ra // tr),
        in_specs=[pl.BlockSpec((N_DEV, tr, cb), lambda l, i: (0, i * (1 - l), 0)),
                  pl.BlockSpec((N_DEV, tr, cb), lambda l, i: (0, i * l, 0)), blk, blk, blk, ANY_SPEC],
        out_specs=(blk,) * 4,
        out_shape=(jax.ShapeDtypeStruct((nl, ra, cb), F32),) * 4,
        compiler_params=_cparams(("arbitrary", "arbitrary"), 6 * _nbytes((N_DEV, tr, cb), BF16) + 32 * _nbytes((tr, cb), F32)),
    )(*_hbm(recv[0], recv[1], w, m, v), after)


def _adamw(w, g, m, v):
    def body(w_ref, g_ref, m_ref, v_ref, d_ref, nm_ref, nv_ref):
        d_ref[...], nm_ref[...], nv_ref[...] = _adam_update(w_ref[...], g_ref[...], m_ref[...], v_ref[...])

    return pl.pallas_call(
        body,
        name="adamw",
        in_specs=[VMEM_SPEC] * 4,
        out_specs=(VMEM_SPEC,) * 3,
        out_shape=(jax.ShapeDtypeStruct(w.shape, F32),) * 3,
        compiler_params=_cparams(vmem=10 * _nbytes(w.shape, F32)),
    )(w, g, m, v)


def _adamw_tiled(w, g, m, v, tr):
    nl, ra, cb = w.shape

    def body(w_ref, g_ref, m_ref, v_ref, d_ref, nm_ref, nv_ref):
        d_ref[...], nm_ref[...], nv_ref[...] = _adam_update(w_ref[...], g_ref[...], m_ref[...], v_ref[...])

    blk = pl.BlockSpec((None, tr, cb), lambda l, i: (l, i, 0))
    return pl.pallas_call(
        body,
        name="adamw_tiled",
        grid=(nl, ra // tr),
        in_specs=[blk] * 4,
        out_specs=(blk,) * 3,
        out_shape=(jax.ShapeDtypeStruct((nl, ra, cb), F32),) * 3,
        compiler_params=_cparams(("parallel", "parallel")),
    )(*_hbm(w, g, m, v))


def _token_tile(s):
    return min(256, max(SUBLANES * 2, s // 4))


def kernel(x, c, w_mod, b_mod, g_pre_mix, g_post_mix, w_in, conv_a_w, conv_a_b, w_a_out, conv_b_w, conv_b_b, w_gate_r, b_gate_r, w_gate_i, b_gate_i, lru_lambda, w_b_out, w_o, g_pre_mlp, g_post_mlp, w_mlp_up, w_mlp_down, loss_target, m_w_mod, m_b_mod, m_g_pre_mix, m_g_post_mix, m_w_in, m_conv_a_w, m_conv_a_b, m_w_a_out, m_conv_b_w, m_conv_b_b, m_w_gate_r, m_b_gate_r, m_w_gate_i, m_b_gate_i, m_lru_lambda, m_w_b_out, m_w_o, m_g_pre_mlp, m_g_post_mlp, m_w_mlp_up, m_w_mlp_down, v_w_mod, v_b_mod, v_g_pre_mix, v_g_post_mix, v_w_in, v_conv_a_w, v_conv_a_b, v_w_a_out, v_conv_b_w, v_conv_b_b, v_w_gate_r, v_b_gate_r, v_w_gate_i, v_b_gate_i, v_lru_lambda, v_w_b_out, v_w_o, v_g_pre_mlp, v_g_post_mlp, v_w_mlp_up, v_w_mlp_down):
    nl = w_mod.shape[0]
    s, d = x.shape[1], x.shape[2]
    nh, bw = w_gate_r.shape[1], w_gate_r.shape[2]
    cwid = conv_a_w.shape[2]
    tm = _token_tile(s)
    ts = s
    _, me = _my_position()
    xs = x.reshape(s, d)
    target = loss_target.reshape(s, d)

    vec_names = (g_pre_mix, g_post_mix, conv_a_b, conv_b_b, b_gate_r, b_gate_i, lru_lambda, g_pre_mlp, g_post_mlp)
    rows = jnp.concatenate([jnp.stack(vec_names, axis=1), jnp.zeros((nl, N_ROWS - len(vec_names), d), F32)], axis=1)
    cw = jnp.concatenate([conv_a_w, conv_b_w, jnp.zeros((nl, CW_ROWS - 7, cwid), F32)], axis=1)

    w16 = {"w_in": w_in.astype(BF16), "w_a_out": w_a_out.astype(BF16), "w_b_out": w_b_out.astype(BF16), "w_o": w_o.astype(BF16),
           "w_mlp_up": w_mlp_up.astype(BF16), "w_mlp_down": w_mlp_down.astype(BF16)}
    groups = (("in", ("w_in",)), ("mix", ("w_a_out", "w_b_out", "w_o")), ("mlp", ("w_mlp_up", "w_mlp_down")))
    mod, cact, cwf = _prep_small(c, w_mod, b_mod, cw)
    mod = mod.reshape(nl, N_MOD, d)
    w16, _ = lax.optimization_barrier((w16, mod))
    gathers = {}
    token = jnp.zeros((), F32)
    for l in range(nl):
        for gname, members in groups:
            gathers[l, gname] = _gather2_start([w16[n][l] for n in members], f"gather_start_{gname}{l}")
            token = token + gathers[l, gname][4][0, 0]
    rows = rows + token
    wr = w_gate_r.astype(BF16)
    wi = w_gate_i.astype(BF16)

    def gathered(l, gname, after):
        forwarded = _gather2_forward(gathers[l, gname], after, f"gather_forward_{gname}{l}")
        return _gather2_wait(forwarded, after, f"gather_wait_{gname}{l}")

    saved = []
    weights = []
    xin = xs
    for l in range(nl):
        (win_f,) = gathered(l, "in", mod if l == 0 else xin)
        proj, h = _in_proj_fwd(xin, mod, rows, win_f, l, tm)
        wa_f, wb_f, wo_f = (w.reshape(d, d) for w in gathered(l, "mix", proj))
        x1, hs, yap, ybp, y = _mixer_core_fwd(proj, xin, mod, rows, cwf, wr, wi, wa_f, wb_f, wo_f, l, tm)
        wup_f, wdn_f = gathered(l, "mlp", x1)
        wdn_f = wdn_f.reshape(-1, d)
        x2, ru, y2, h2 = _mlp_fwd(x1, mod, rows, wup_f, wdn_f, l, tm)
        saved.append((xin, proj, h, x1, hs, yap, ybp, y, ru, y2, h2))
        weights.append((win_f, wa_f, wb_f, wo_f, wup_f, wdn_f))
        xin = x2
    loss_part, dx = _loss_fwd_bwd(xin, target, tm)
    loss = lax.psum(loss_part, ("x", "y", "c"))

    scatters = {}
    small = [None] * nl
    gate_parts = [None] * nl

    def scatter(l, gname, parts, rows):
        scatters[l, gname] = _exchange_start(parts, False, f"scatter_start_{gname}{l}")
        return rows + scatters[l, gname][4][0, 0]

    for l in reversed(range(nl)):
        xin, proj, h, x1, hs, yap, ybp, y, ru, y2, h2 = saved[l]
        win_f, wa_f, wb_f, wo_f, wup_f, wdn_f = weights[l]
        dx1, dy2, dup, act, sm_mlp = _mlp_bwd(dx, x1, y2, ru, mod, rows, wup_f, wdn_f, l, tm)
        g_up = _wgrad(h2, dup, True, ts, dx1)
        g_dn = _wgrad(act, dy2, False, ts, g_up)
        rows = scatter(l, "mlp", [g_up, g_dn], rows)
        dproj, dy, m, dyap, dybp, ya, yb, sm_mix, dwg = _mixer_core_bwd(dx1, y, yap, ybp, hs, proj, mod, rows, cwf, wr, wi, wa_f, wb_f, wo_f, l, tm // 2)
        gate_parts[l] = _exchange_start([dwg.astype(BF16)], True, f"gates_start{l}")
        rows = rows + gate_parts[l][4][0, 0]
        g_a = _wgrad(ya, dyap, False, ts, gate_parts[l][4])
        g_b = _wgrad(yb, dybp, False, ts, g_a)
        g_o = _wgrad(m, dy, False, ts, g_b)
        rows = scatter(l, "mix", [g_a, g_b, g_o], rows)
        rows = scatter(l, "in", [_wgrad(h, dproj, True, ts, scatters[l, "mix"][4])], rows)
        dx, sm_in = _in_proj_bwd(dproj, xin, dx1, mod, rows, win_f, l, tm)
        small[l] = jnp.concatenate([sm_mlp, sm_mix, sm_in], axis=0)
    grad_x = dx.reshape(x.shape)

    recv = {}
    big = {}
    moments = {"w_in": (w_in, m_w_in, v_w_in), "w_mlp_up": (w_mlp_up, m_w_mlp_up, v_w_mlp_up), "w_a_out": (w_a_out, m_w_a_out, v_w_a_out),
               "w_b_out": (w_b_out, m_w_b_out, v_w_b_out), "w_o": (w_o, m_w_o, v_w_o), "w_mlp_down": (w_mlp_down, m_w_mlp_down, v_w_mlp_down)}

    def collect(l, gname, after):
        for n, zone in zip(dict(groups)[gname], _exchange_wait(scatters[l, gname], after, False, f"scatter_wait_{gname}{l}")):
            recv[n, l] = zone

    def update(name, after):
        w, m_, v_ = moments[name]
        big[name] = _sum_adamw([recv[name, l] for l in range(nl)], w, m_, v_, min(256, w.shape[1]), after)
        return big[name][1]

    for l, gname in ((1, "mlp"), (1, "in"), (1, "mix"), (0, "mlp")):
        collect(l, gname, dx)
    early = update("w_mlp_down", update("w_mlp_up", dx))

    lrows = jnp.concatenate(small, axis=0)

    def lrow(a, l, r):
        return a[l * G_LAYER_ROWS + r]

    dm = jnp.stack([jnp.concatenate([lrow(lrows, l, G_MLP_ROWS + G_MIX_ROWS + G_IN_SH), lrow(lrows, l, G_MLP_ROWS + G_MIX_ROWS + G_IN_SC),
                                     lrow(lrows, l, G_MLP_ROWS + G_MIX_GT), lrow(lrows, l, G_MLP_SH), lrow(lrows, l, G_MLP_SC),
                                     lrow(lrows, l, G_MLP_GT)]) for l in range(nl)])
    dm8 = jnp.concatenate([dm[:, None, :], jnp.zeros((nl, SUBLANES - 1, N_MOD * d), F32)], axis=1)
    srows, g_w_mod = _reduce_small(lrows, dm8, cact, early)
    sgates = _sum_gathered([_exchange_wait(gate_parts[l], srows, True, f"gates_wait{l}")[0] for l in range(nl)])
    for gname in ("in", "mix"):
        collect(0, gname, srows)
    for name in ("w_in", "w_a_out", "w_b_out", "w_o"):
        update(name, srows)

    def srow(l, r):
        return lrow(srows, l, r)

    def per_layer(r):
        return jnp.stack([srow(l, r) for l in range(nl)])

    mix0 = G_MLP_ROWS
    in0 = G_MLP_ROWS + G_MIX_ROWS
    g_b_mod = jnp.stack([jnp.concatenate([srow(l, in0 + G_IN_SH), srow(l, in0 + G_IN_SC), srow(l, mix0 + G_MIX_GT),
                                          srow(l, G_MLP_SH), srow(l, G_MLP_SC), srow(l, G_MLP_GT)]) for l in range(nl)])
    conv_a_full = jnp.stack([jnp.stack([srow(l, mix0 + G_MIX_CAW + k) for k in range(3)]) for l in range(nl)])
    conv_b_full = jnp.stack([jnp.stack([srow(l, mix0 + G_MIX_CBW + k) for k in range(4)]) for l in range(nl)])
    grads = {
        "b_mod": g_b_mod,
        "g_pre_mix": per_layer(in0 + G_IN_GPRE),
        "g_post_mix": per_layer(mix0 + G_MIX_GPOST),
        "conv_a_w": lax.dynamic_slice_in_dim(conv_a_full, me * cwid, cwid, axis=2),
        "conv_a_b": per_layer(mix0 + G_MIX_CAB),
        "conv_b_w": lax.dynamic_slice_in_dim(conv_b_full, me * cwid, cwid, axis=2),
        "conv_b_b": per_layer(mix0 + G_MIX_CBB),
        "w_gate_r": sgates[:, 0],
        "b_gate_r": per_layer(mix0 + G_MIX_BR),
        "w_gate_i": sgates[:, 1],
        "b_gate_i": per_layer(mix0 + G_MIX_BI),
        "lru_lambda": per_layer(mix0 + G_MIX_LAM),
        "g_pre_mlp": per_layer(G_MLP_GPRE),
        "g_post_mlp": per_layer(G_MLP_GPOST),
    }
    params = {
        "b_mod": (b_mod, m_b_mod, v_b_mod), "g_pre_mix": (g_pre_mix, m_g_pre_mix, v_g_pre_mix), "g_post_mix": (g_post_mix, m_g_post_mix, v_g_post_mix),
        "conv_a_w": (conv_a_w, m_conv_a_w, v_conv_a_w), "conv_a_b": (conv_a_b, m_conv_a_b, v_conv_a_b),
        "conv_b_w": (conv_b_w, m_conv_b_w, v_conv_b_w), "conv_b_b": (conv_b_b, m_conv_b_b, v_conv_b_b),
        "w_gate_r": (w_gate_r, m_w_gate_r, v_w_gate_r), "b_gate_r": (b_gate_r, m_b_gate_r, v_b_gate_r),
        "w_gate_i": (w_gate_i, m_w_gate_i, v_w_gate_i), "b_gate_i": (b_gate_i, m_b_gate_i, v_b_gate_i),
        "lru_lambda": (lru_lambda, m_lru_lambda, v_lru_lambda), "g_pre_mlp": (g_pre_mlp, m_g_pre_mlp, v_g_pre_mlp),
        "g_post_mlp": (g_post_mlp, m_g_post_mlp, v_g_post_mlp),
    }
    out = {}
    for name, g in grads.items():
        w, m_, v_ = params[name]
        flat = (-1, w.shape[-1])
        dl, nm, nv = _adamw(w.reshape(flat), g.reshape(flat), m_.reshape(flat), v_.reshape(flat))
        out[name] = (g.reshape(w.shape), dl.reshape(w.shape), nm.reshape(w.shape), nv.reshape(w.shape))
    out["w_mod"] = (g_w_mod,) + tuple(_adamw_tiled(w_mod, g_w_mod, m_w_mod, v_w_mod, min(128, d)))
    out.update(big)

    order = ("w_mod", "b_mod", "g_pre_mix", "g_post_mix", "w_in", "conv_a_w", "conv_a_b", "w_a_out", "conv_b_w", "conv_b_b", "w_gate_r", "b_gate_r",
             "w_gate_i", "b_gate_i", "lru_lambda", "w_b_out", "w_o", "g_pre_mlp", "g_post_mlp", "w_mlp_up", "w_mlp_down")
    return (loss, grad_x) + tuple(out[n][0] for n in order) + tuple(out[n][1] for n in order) + tuple(out[n][2] for n in order) + tuple(out[n][3] for n in order)
```

```python
import functools

import jax
import jax.numpy as jnp
from jax import lax
from jax.experimental import pallas as pl
from jax.experimental.pallas import tpu as pltpu

F32, BF16 = jnp.float32, jnp.bfloat16
EPS = 1e-6
LRU_C = 8.0
N_DEV = 8
N_MOD = 6
SUBLANES = 8
VMEM_BUDGET = 56 * 1024 * 1024
WGRAD_ROWS = 512
ADAM_LR, ADAM_B1, ADAM_B2, ADAM_EPS, ADAM_WD, ADAM_STEP = 0.001, 0.9, 0.999, 1e-08, 0.01, 10
MESH = pl.DeviceIdType.MESH
VMEM_SPEC = pl.BlockSpec(memory_space=pltpu.VMEM)
ANY_SPEC = pl.BlockSpec(memory_space=pl.ANY)
HBM_SPEC = pl.BlockSpec(memory_space=pltpu.HBM)
SEM_SPEC = pl.BlockSpec(memory_space=pltpu.SEMAPHORE)
SIDE_EFFECT = pltpu.SideEffectType.DATAFLOW_SIDE_EFFECTING

R_G_PRE_MIX, R_G_POST_MIX, R_CONV_A_B, R_CONV_B_B, R_B_GATE_R, R_B_GATE_I, R_LAMBDA, R_G_PRE_MLP, R_G_POST_MLP = range(9)
N_ROWS = 16
M_SH_M, M_SC_M, M_GT_M, M_SH_F, M_SC_F, M_GT_F = range(6)
CW_A, CW_B, CW_ROWS = 0, 3, 8
G_MLP_GT, G_MLP_GPOST, G_MLP_SC, G_MLP_SH, G_MLP_GPRE, G_LOSS_ROW, G_MLP_ROWS = 0, 1, 2, 3, 4, 7, 8
(G_MIX_GT, G_MIX_GPOST, G_MIX_CAB, G_MIX_CAW, G_MIX_CBB, G_MIX_CBW, G_MIX_BR, G_MIX_BI, G_MIX_LAM) = 0, 1, 2, 3, 6, 7, 11, 12, 13
G_MIX_ROWS = 16
G_IN_SC, G_IN_SH, G_IN_GPRE, G_IN_ROWS = 0, 1, 2, 8
G_LAYER_ROWS = G_MLP_ROWS + G_MIX_ROWS + G_IN_ROWS


def _cparams(dims=None, vmem=None):
    kw = {}
    if dims is not None:
        kw["dimension_semantics"] = dims
    if vmem is not None:
        kw["vmem_limit_bytes"] = int(min(max(vmem, 16 * 1024 * 1024), VMEM_BUDGET))
    return pltpu.CompilerParams(**kw)


def _nbytes(shape, dtype):
    n = 1
    for s in shape:
        n *= s
    return n * jnp.dtype(dtype).itemsize


def _hbm(*arrays):
    return tuple(pltpu.with_memory_space_constraint(a, pltpu.HBM) for a in arrays)


def _resident(block, index_map):
    return pl.BlockSpec(block, index_map, pipeline_mode=pl.Buffered(1))


def _my_position():
    x, y, c = lax.axis_index("x"), lax.axis_index("y"), lax.axis_index("c")
    return (x, y, c), 4 * x + 2 * y + c


def _peer(pos, k):
    x, y, c = pos
    px = 1 - x if k & 4 else x
    py = 1 - y if k & 2 else y
    pc = 1 - c if k & 1 else c
    return (px, py, pc), 4 * px + 2 * py + pc


def _remote(src, dst, ssem, rsem, peer):
    return pltpu.make_async_remote_copy(src_ref=src, dst_ref=dst, send_sem=ssem, recv_sem=rsem, device_id=peer, device_id_type=MESH)


def _dot(a, b):
    return jnp.dot(a, b, preferred_element_type=F32)


def _dot_nt(a, b):
    return lax.dot_general(a, b, (((1,), (1,)), ((), ())), preferred_element_type=F32)


def _dot_tn(a, b):
    return lax.dot_general(a, b, (((0,), (0,)), ((), ())), preferred_element_type=F32)


def _colsum(v):
    return jnp.sum(v, axis=0, keepdims=True)


def _sigmoid(v):
    return jax.nn.sigmoid(v)


def _gelu(v):
    k = 0.7978845608028654
    t = jnp.tanh(k * (v + 0.044715 * (v * v * v)))
    return 0.5 * v * (1.0 + t), t


def _gelu_grad(v, t):
    k = 0.7978845608028654
    return 0.5 * (1.0 + t) + 0.5 * v * (1.0 - t * t) * (k * (1.0 + 3.0 * 0.044715 * v * v))


def _expm1(v):
    u = jnp.exp(v)
    um1 = u - 1.0
    q = um1 * v / jnp.log(u)
    return jnp.where(um1 == 0.0, v, jnp.where(um1 == -1.0, -1.0, q))


def _softplus_neg(lam):
    z = -lam
    u = jnp.exp(-jnp.abs(z))
    w = 1.0 + u
    l1p = jnp.where(w == 1.0, u, jnp.log(w) * u / (w - 1.0))
    return jnp.maximum(z, 0.0) + l1p


def _rms(v):
    return lax.rsqrt(jnp.mean(v * v, axis=-1, keepdims=True) + EPS)


def _prenorm_bwd(xv, dh, g, sc):
    r = _rms(xv)
    xn = xv * r
    n = xn * g
    dsc = _colsum(dh * n)
    dsh = _colsum(dh)
    dn = dh * (1.0 + sc)
    dg = _colsum(dn * xn)
    dxn = dn * g
    dx = r * (dxn - xn * jnp.mean(dxn * xn, axis=-1, keepdims=True))
    return dx, dsc, dsh, dg


def _postnorm_bwd(yv, dout, g, gt):
    r = _rms(yv)
    yn = yv * r
    dgt = _colsum(dout * (yn * g))
    dn = dout * gt
    dg = _colsum(dn * yn)
    dyn = dn * g
    dy = r * (dyn - yn * jnp.mean(dyn * yn, axis=-1, keepdims=True))
    return dy, dgt, dg


def _gates(xc, wr_ref, wi_ref, b_r, b_i, sp, nh, bw):
    xcb = xc.astype(BF16)
    zr = jnp.concatenate([_dot(xcb[:, h * bw:(h + 1) * bw], wr_ref[h]) for h in range(nh)], axis=1) + b_r
    zi = jnp.concatenate([_dot(xcb[:, h * bw:(h + 1) * bw], wi_ref[h]) for h in range(nh)], axis=1) + b_i
    r = _sigmoid(zr)
    ig = _sigmoid(zi)
    la = (-LRU_C * r) * sp
    a = jnp.exp(la)
    mult = jnp.sqrt(-_expm1(2.0 * la))
    return xcb, r, ig, a, mult


def _shift_rows(cur, edge, k, up):
    t, dd = cur.shape
    blocks = cur.reshape(t // SUBLANES, SUBLANES, dd)
    row = lax.broadcasted_iota(jnp.int32, (1, SUBLANES, dd), 1)
    if up:
        r = pltpu.roll(blocks, SUBLANES - k, 1)
        nxt = jnp.concatenate([r[1:], pltpu.roll(edge, SUBLANES - k, 0)[None]], axis=0)
        out = jnp.where(row >= SUBLANES - k, nxt, r)
    else:
        r = pltpu.roll(blocks, k, 1)
        prv = jnp.concatenate([pltpu.roll(edge, k, 0)[None], r[:-1]], axis=0)
        out = jnp.where(row < k, prv, r)
    return out.reshape(t, dd)


def _scan_block(a8, b8, reverse):
    row = lax.broadcasted_iota(jnp.int32, a8.shape, 0)
    for s in (1, 2, 4):
        if reverse:
            keep = row < SUBLANES - s
            a_sh = pltpu.roll(a8, SUBLANES - s, 0)
            b_sh = pltpu.roll(b8, SUBLANES - s, 0)
        else:
            keep = row >= s
            a_sh = pltpu.roll(a8, s, 0)
            b_sh = pltpu.roll(b8, s, 0)
        b8 = b8 + a8 * jnp.where(keep, b_sh, 0.0)
        a8 = a8 * jnp.where(keep, a_sh, 1.0)
    return a8, b8


def _prep_small(c, w_mod, b_mod, cw):
    d = c.shape[1]
    cm = w_mod.shape[2]
    cwid = cw.shape[2]
    nl = w_mod.shape[0]

    def body(c_ref, wm_ref, bm_ref, cw_ref, mod_ref, cact_ref, cwf_ref, cbuf, pbuf, rbuf, ssem, rsem, lsem):
        pos, me = _my_position()
        me8 = pl.multiple_of(me * SUBLANES, SUBLANES)
        cbuf[pl.ds(me8, SUBLANES), :] = jnp.broadcast_to(c_ref[...], (SUBLANES, d))
        own_cw = pltpu.make_async_copy(cw_ref, cwf_ref.at[:, :, pl.ds(me * cwid, cwid)], lsem.at[0])
        own_cw.start()
        first = []
        for k in range(1, N_DEV):
            peer, _ = _peer(pos, k)
            rows = cbuf.at[pl.ds(me8, SUBLANES), :]
            first.append(_remote(rows, rows, ssem.at[0, k - 1], rsem.at[0, k - 1], peer))
            first.append(_remote(cw_ref, cwf_ref.at[:, :, pl.ds(me * cwid, cwid)], ssem.at[1, k - 1], rsem.at[1, k - 1], peer))
        for cp in first:
            cp.start()
        for k in range(1, N_DEV):
            peer, pj = _peer(pos, k)
            pj8 = pl.multiple_of(pj * SUBLANES, SUBLANES)
            rows = cbuf.at[pl.ds(pj8, SUBLANES), :]
            _remote(rows, rows, ssem.at[0, k - 1], rsem.at[0, k - 1], peer).wait_recv()
        cv = cbuf[...]
        cact = cv * _sigmoid(cv)
        cact_ref[...] = cact
        cb = cact.astype(BF16)
        for l in range(nl):
            pbuf[l] = _dot(cb, wm_ref[l].astype(BF16))
        own_p = pltpu.make_async_copy(pbuf.at[:, pl.ds(me8, SUBLANES), :], rbuf.at[me], lsem.at[1])
        own_p.start()
        second = []
        for k in range(1, N_DEV):
            peer, pj = _peer(pos, k)
            pj8 = pl.multiple_of(pj * SUBLANES, SUBLANES)
            second.append(_remote(pbuf.at[:, pl.ds(pj8, SUBLANES), :], rbuf.at[me], ssem.at[2, k - 1], rsem.at[2, k - 1], peer))
        for cp in second:
            cp.start()
        for k in range(1, N_DEV):
            peer, pj = _peer(pos, k)
            _remote(pbuf.at[:, pl.ds(0, SUBLANES), :], rbuf.at[pj], ssem.at[2, k - 1], rsem.at[2, k - 1], peer).wait_recv()
            _remote(cw_ref, cwf_ref.at[:, :, pl.ds(pj * cwid, cwid)], ssem.at[1, k - 1], rsem.at[1, k - 1], peer).wait_recv()
        own_p.wait()
        own_cw.wait()
        for l in range(nl):
            for j in range(N_DEV):
                mod_ref[l:l + 1, j * cm:(j + 1) * cm] = rbuf[j, l, 0:1, :] + bm_ref[l:l + 1, j * cm:(j + 1) * cm]
        for cp in first + second:
            cp.wait_send()

    return pl.pallas_call(
        body,
        name="prep_small",
        out_shape=(
            jax.ShapeDtypeStruct((nl, N_MOD * d), F32),
            jax.ShapeDtypeStruct((N_DEV * SUBLANES, d), F32),
            jax.ShapeDtypeStruct((nl, CW_ROWS, d), F32),
        ),
        in_specs=[VMEM_SPEC] * 4,
        out_specs=(VMEM_SPEC,) * 3,
        scratch_shapes=[
            pltpu.VMEM((N_DEV * SUBLANES, d), F32),
            pltpu.VMEM((nl, N_DEV * SUBLANES, cm), F32),
            pltpu.VMEM((N_DEV, nl, SUBLANES, cm), F32),
            pltpu.SemaphoreType.DMA((3, N_DEV - 1)),
            pltpu.SemaphoreType.DMA((3, N_DEV - 1)),
            pltpu.SemaphoreType.DMA((2,)),
        ],
        compiler_params=_cparams(vmem=3 * _nbytes(w_mod.shape, F32)),
    )(c, w_mod, b_mod, cw)


def _exchange_start(parts, gather, name):
    n = len(parts)
    lands = [lax.empty(((N_DEV,) + tuple(p.shape)) if gather else tuple(p.shape), p.dtype) for p in parts]

    def body(*refs):
        ins, lnd = refs[:n], refs[n:2 * n]
        ssem, rsem, token = refs[2 * n], refs[2 * n + 1], refs[-1]
        pos, me = _my_position()
        for k in range(1, N_DEV):
            peer, pj = _peer(pos, k)
            for t in range(n):
                src = ins[t] if gather else ins[t].at[pj]
                q = t * (N_DEV - 1) + k - 1
                _remote(src, lnd[t].at[me], ssem.at[q], rsem.at[q], peer).start()
        token[...] = jnp.zeros(token.shape, F32)

    out = pl.pallas_call(
        body,
        name=name,
        out_shape=(pltpu.SemaphoreType.DMA((n * (N_DEV - 1),)), pltpu.SemaphoreType.DMA((n * (N_DEV - 1),)))
        + tuple(pltpu.HBM(p.shape, p.dtype) for p in parts) + tuple(pltpu.HBM(p.shape, p.dtype) for p in lands)
        + (jax.ShapeDtypeStruct((SUBLANES, 128), F32),),
        in_specs=[HBM_SPEC] * (2 * n),
        out_specs=(SEM_SPEC, SEM_SPEC) + (HBM_SPEC,) * (2 * n) + (VMEM_SPEC,),
        input_output_aliases={i: 2 + i for i in range(2 * n)},
        compiler_params=pltpu.CompilerParams(has_side_effects=SIDE_EFFECT),
    )(*[pltpu.with_memory_space_constraint(p, pltpu.HBM) for p in list(parts) + lands])
    return out[0], out[1], list(out[2:2 + n]), list(out[2 + n:2 + 2 * n]), out[-1]


def _exchange_wait(started, after, gather, name):
    ssem, rsem, parts, lands, _ = started
    n = len(parts)

    def body(*refs):
        ins, lnd = refs[:n], refs[n:2 * n]
        ssem_ref, rsem_ref = refs[2 * n], refs[2 * n + 1]
        stage, lsem = refs[-1 - n:-1], refs[-1]
        pos, me = _my_position()
        load = []
        for t in range(n):
            src = ins[t] if gather else ins[t].at[me]
            load.append(pltpu.make_async_copy(src, stage[t], lsem.at[t]))
            load[-1].start()
        store = []
        for t in range(n):
            load[t].wait()
            store.append(pltpu.make_async_copy(stage[t], lnd[t].at[me], lsem.at[t]))
            store[-1].start()
        for k in range(1, N_DEV):
            peer, pj = _peer(pos, k)
            for t in range(n):
                src = ins[t] if gather else ins[t].at[pj]
                q = t * (N_DEV - 1) + k - 1
                _remote(src, lnd[t].at[me], ssem_ref.at[q], rsem_ref.at[q], peer).wait_send()
                _remote(src, lnd[t].at[pj], ssem_ref.at[q], rsem_ref.at[q], peer).wait_recv()
        for cp in store:
            cp.wait()

    out = pl.pallas_call(
        body,
        name=name,
        out_shape=tuple(pltpu.HBM(p.shape, p.dtype) for p in parts) + tuple(pltpu.HBM(p.shape, p.dtype) for p in lands),
        in_specs=[HBM_SPEC] * (2 * n) + [SEM_SPEC, SEM_SPEC, ANY_SPEC],
        out_specs=(HBM_SPEC,) * (2 * n),
        input_output_aliases={i: i for i in range(2 * n)},
        scratch_shapes=[pltpu.VMEM(tuple(z.shape[1:]), z.dtype) for z in lands] + [pltpu.SemaphoreType.DMA((n,))],
        compiler_params=pltpu.CompilerParams(has_side_effects=SIDE_EFFECT),
    )(*parts, *lands, ssem, rsem, after)
    return list(out[n:])


CHIP_PEERS = 3


def _chip_peer(pos, k, core):
    x, y, _ = pos
    px = 1 - x if k & 2 else x
    py = 1 - y if k & 1 else y
    return (px, py, core), 4 * px + 2 * py + core


def _gather2_start(parts, name):
    n = len(parts)
    lands = [lax.empty((N_DEV,) + tuple(p.shape), p.dtype) for p in parts]
    per = CHIP_PEERS + 1

    def body(*refs):
        ins, lnd = refs[:n], refs[n:2 * n]
        ssem, rsem, token = refs[2 * n], refs[2 * n + 1], refs[-1]
        pos, me = _my_position()
        sibling = (pos[0], pos[1], 1 - pos[2])
        for t in range(n):
            _remote(ins[t], lnd[t].at[me], ssem.at[per * t], rsem.at[per * t], sibling).start()
        for k in range(1, per):
            peer, _ = _chip_peer(pos, k, pos[2])
            for t in range(n):
                _remote(ins[t], lnd[t].at[me], ssem.at[per * t + k], rsem.at[per * t + k], peer).start()
        token[...] = jnp.zeros(token.shape, F32)

    out = pl.pallas_call(
        body,
        name=name,
        out_shape=(pltpu.SemaphoreType.DMA((n * per,)), pltpu.SemaphoreType.DMA((n * per,)))
        + tuple(pltpu.HBM(p.shape, p.dtype) for p in parts) + tuple(pltpu.HBM(p.shape, p.dtype) for p in lands)
        + (jax.ShapeDtypeStruct((SUBLANES, 128), F32),),
        in_specs=[HBM_SPEC] * (2 * n),
        out_specs=(SEM_SPEC, SEM_SPEC) + (HBM_SPEC,) * (2 * n) + (VMEM_SPEC,),
        input_output_aliases={i: 2 + i for i in range(2 * n)},
        compiler_params=pltpu.CompilerParams(has_side_effects=SIDE_EFFECT),
    )(*[pltpu.with_memory_space_constraint(p, pltpu.HBM) for p in list(parts) + lands])
    return out[0], out[1], list(out[2:2 + n]), list(out[2 + n:2 + 2 * n]), out[-1]


def _gather2_forward(started, after, name):
    ssem, rsem, parts, lands, _ = started
    n = len(lands)
    per = CHIP_PEERS + 1

    def body(*refs):
        lnd = refs[:n]
        rsem_a, fsend, frecv, token = refs[n], refs[2 * n + 2], refs[2 * n + 3], refs[2 * n + 4]
        pos, _ = _my_position()
        sibling = (pos[0], pos[1], 1 - pos[2])
        for k in range(1, per):
            peer, pk = _chip_peer(pos, k, pos[2])
            for t in range(n):
                block = lnd[t].at[pk]
                _remote(block, block, rsem_a.at[per * t + k], rsem_a.at[per * t + k], peer).wait_recv()
                q = CHIP_PEERS * t + k - 1
                _remote(block, block, fsend.at[q], frecv.at[q], sibling).start()
        token[...] = jnp.zeros(token.shape, F32)

    out = pl.pallas_call(
        body,
        name=name,
        out_shape=tuple(pltpu.HBM(p.shape, p.dtype) for p in lands)
        + (pltpu.SemaphoreType.DMA((n * CHIP_PEERS,)), pltpu.SemaphoreType.DMA((n * CHIP_PEERS,)),
           jax.ShapeDtypeStruct((SUBLANES, 128), F32)),
        in_specs=[HBM_SPEC] * n + [SEM_SPEC, ANY_SPEC],
        out_specs=(HBM_SPEC,) * n + (SEM_SPEC, SEM_SPEC, VMEM_SPEC),
        input_output_aliases={i: i for i in range(n)},
        compiler_params=pltpu.CompilerParams(has_side_effects=SIDE_EFFECT),
    )(*lands, rsem, after)
    return ssem, rsem, parts, list(out[:n]), out[n + 2], out[n], out[n + 1]


def _gather2_wait(forwarded, after, name):
    ssem, rsem, parts, lands, _, fsend, frecv = forwarded
    n = len(parts)
    per = CHIP_PEERS + 1

    def body(*refs):
        ins, lnd = refs[:n], refs[n:2 * n]
        ssem_a, rsem_a, fs, fr = refs[2 * n:2 * n + 4]
        stage, lsem = refs[-1 - n:-1], refs[-1]
        pos, me = _my_position()
        sibling = (pos[0], pos[1], 1 - pos[2])
        sib = 4 * pos[0] + 2 * pos[1] + 1 - pos[2]
        load = []
        for t in range(n):
            load.append(pltpu.make_async_copy(ins[t], stage[t], lsem.at[t]))
            load[-1].start()
        store = []
        for t in range(n):
            load[t].wait()
            store.append(pltpu.make_async_copy(stage[t], lnd[t].at[me], lsem.at[t]))
            store[-1].start()
        for t in range(n):
            _remote(ins[t], lnd[t].at[me], ssem_a.at[per * t], rsem_a.at[per * t], sibling).wait_send()
            _remote(ins[t], lnd[t].at[sib], ssem_a.at[per * t], rsem_a.at[per * t], sibling).wait_recv()
        for k in range(1, per):
            peer, pk = _chip_peer(pos, k, pos[2])
            _, qk = _chip_peer(pos, k, 1 - pos[2])
            for t in range(n):
                q = CHIP_PEERS * t + k - 1
                _remote(ins[t], lnd[t].at[me], ssem_a.at[per * t + k], rsem_a.at[per * t + k], peer).wait_send()
                _remote(lnd[t].at[pk], lnd[t].at[pk], fs.at[q], fr.at[q], sibling).wait_send()
                _remote(lnd[t].at[qk], lnd[t].at[qk], fs.at[q], fr.at[q], sibling).wait_recv()
        for cp in store:
            cp.wait()

    out = pl.pallas_call(
        body,
        name=name,
        out_shape=tuple(pltpu.HBM(p.shape, p.dtype) for p in parts) + tuple(pltpu.HBM(p.shape, p.dtype) for p in lands),
        in_specs=[HBM_SPEC] * (2 * n) + [SEM_SPEC] * 4 + [ANY_SPEC],
        out_specs=(HBM_SPEC,) * (2 * n),
        input_output_aliases={i: i for i in range(2 * n)},
        scratch_shapes=[pltpu.VMEM(tuple(z.shape[1:]), z.dtype) for z in lands] + [pltpu.SemaphoreType.DMA((n,))],
        compiler_params=pltpu.CompilerParams(has_side_effects=SIDE_EFFECT),
    )(*parts, *lands, ssem, rsem, fsend, frecv, after)
    return list(out[n:])


def _reduce_small(rows, dm8, cact, after):
    r, d = rows.shape
    nl = dm8.shape[0]
    cm = dm8.shape[2] // N_DEV

    def body(rows_ref, dm_ref, cact_ref, after_ref, orow_ref, owm_ref, gr, dmr, ssem, rsem, lsem):
        pos, me = _my_position()
        me8 = pl.multiple_of(me * SUBLANES, SUBLANES)
        gr[me] = rows_ref[...]
        own_dm = pltpu.make_async_copy(dm_ref.at[:, :, pl.ds(me * cm, cm)], dmr.at[:, pl.ds(me8, SUBLANES), :], lsem.at[0])
        own_dm.start()
        sends = []
        for k in range(1, N_DEV):
            peer, pj = _peer(pos, k)
            sends.append(_remote(gr.at[me], gr.at[me], ssem.at[0, k - 1], rsem.at[0, k - 1], peer))
            sends.append(_remote(dm_ref.at[:, :, pl.ds(pj * cm, cm)], dmr.at[:, pl.ds(me8, SUBLANES), :],
                                 ssem.at[1, k - 1], rsem.at[1, k - 1], peer))
        for cp in sends:
            cp.start()
        for k in range(1, N_DEV):
            peer, pj = _peer(pos, k)
            pj8 = pl.multiple_of(pj * SUBLANES, SUBLANES)
            _remote(gr.at[pj], gr.at[pj], ssem.at[0, k - 1], rsem.at[0, k - 1], peer).wait_recv()
            _remote(dm_ref.at[:, :, pl.ds(0, cm)], dmr.at[:, pl.ds(pj8, SUBLANES), :], ssem.at[1, k - 1], rsem.at[1, k - 1], peer).wait_recv()
        own_dm.wait()
        acc = gr[0]
        for j in range(1, N_DEV):
            acc = acc + gr[j]
        orow_ref[...] = acc
        cb = cact_ref[...].astype(BF16)
        for l in range(nl):
            owm_ref[l] = _dot_tn(cb, dmr[l].astype(BF16))
        for cp in sends:
            cp.wait_send()

    return pl.pallas_call(
        body,
        name="reduce_small",
        out_shape=(jax.ShapeDtypeStruct((r, d), F32), jax.ShapeDtypeStruct((nl, d, cm), F32)),
        in_specs=[VMEM_SPEC] * 3 + [ANY_SPEC],
        out_specs=(VMEM_SPEC,) * 2,
        scratch_shapes=[
            pltpu.VMEM((N_DEV, r, d), F32),
            pltpu.VMEM((nl, N_DEV * SUBLANES, cm), F32),
            pltpu.SemaphoreType.DMA((2, N_DEV - 1)),
            pltpu.SemaphoreType.DMA((2, N_DEV - 1)),
            pltpu.SemaphoreType.DMA((1,)),
        ],
        compiler_params=_cparams(vmem=4 * _nbytes((N_DEV, r, d), F32) + 6 * _nbytes((nl, d, cm), F32)),
    )(rows, dm8, cact, after)


def _sum_gathered(zones):
    nl = len(zones)

    def body(*refs):
        for l in range(nl):
            acc = refs[l][0].astype(F32)
            for j in range(1, N_DEV):
                acc = acc + refs[l][j].astype(F32)
            refs[nl][l] = acc

    return pl.pallas_call(
        body,
        name="sum_gathered",
        out_shape=jax.ShapeDtypeStruct((nl,) + tuple(zones[0].shape[1:]), F32),
        in_specs=[VMEM_SPEC] * nl,
        out_specs=VMEM_SPEC,
        compiler_params=_cparams(vmem=8 * nl * _nbytes(zones[0].shape, BF16)),
    )(*zones)


def _in_proj_fwd(x, mod, rows, win_f, l, tm):
    s, d = x.shape
    nb, _, ci = win_f.shape

    def body(x_ref, mod_ref, rows_ref, w_ref, proj_ref, h_ref):
        xv = x_ref[...]
        g = rows_ref[R_G_PRE_MIX:R_G_PRE_MIX + 1, :]
        h = (xv * _rms(xv) * g) * (1.0 + mod_ref[M_SC_M:M_SC_M + 1, :]) + mod_ref[M_SH_M:M_SH_M + 1, :]
        hb = h.astype(BF16)
        h_ref[...] = hb
        for j in range(nb):
            proj_ref[:, j * ci:(j + 1) * ci] = _dot(hb, w_ref[j])

    return pl.pallas_call(
        body,
        name="in_proj_fwd",
        grid=(s // tm,),
        in_specs=[
            pl.BlockSpec((tm, d), lambda i: (i, 0)),
            _resident((None, N_MOD, d), lambda i: (l, 0, 0)),
            _resident((None, N_ROWS, d), lambda i: (l, 0, 0)),
            _resident((nb, d, ci), lambda i: (0, 0, 0)),
        ],
        out_specs=(pl.BlockSpec((tm, nb * ci), lambda i: (i, 0)), pl.BlockSpec((tm, d), lambda i: (i, 0))),
        out_shape=(jax.ShapeDtypeStruct((s, nb * ci), F32), jax.ShapeDtypeStruct((s, d), BF16)),
        compiler_params=_cparams(("parallel",), _nbytes((nb, d, ci), BF16) + 3 * _nbytes((tm, nb * ci), F32) + 8 * _nbytes((tm, d), F32)),
    )(*_hbm(x), mod, rows, win_f)


def _mixer_core_fwd(proj, x, mod, rows, cwf, wr, wi, wa_f, wb_f, wo_f, l, tm):
    s, d = x.shape
    nh, bw, _ = wr.shape[1:]

    def body(proj_ref, x_ref, mod_ref, rows_ref, cw_ref, wr_ref, wi_ref, wa_ref, wb_ref, wo_ref,
             x1_ref, hs_ref, yap_ref, ybp_ref, y_ref, cvbuf, xbbuf, a_s, b_s, hprev):
        i = pl.program_id(0)

        @pl.when(i == 0)
        def _():
            cvbuf[...] = jnp.zeros((SUBLANES, d), F32)
            xbbuf[...] = jnp.zeros((SUBLANES, d), F32)
            hprev[...] = jnp.zeros((SUBLANES, d), F32)

        def row(r):
            return rows_ref[r:r + 1, :]

        def tap(r):
            return cw_ref[r:r + 1, :]

        ba = proj_ref[:, 0:d]
        cv = proj_ref[:, d:2 * d] * proj_ref[:, 2 * d:3 * d]
        cvt = cvbuf[...]
        conv3 = ((row(R_CONV_A_B) + _shift_rows(cv, cvt, 2, False) * tap(CW_A)) + _shift_rows(cv, cvt, 1, False) * tap(CW_A + 1)) + cv * tap(CW_A + 2)
        ya = ba * conv3
        cvbuf[...] = cv[tm - SUBLANES:, :]
        xb = proj_ref[:, 3 * d:4 * d]
        xbt = xbbuf[...]
        xc = (((row(R_CONV_B_B) + _shift_rows(xb, xbt, 3, False) * tap(CW_B)) + _shift_rows(xb, xbt, 2, False) * tap(CW_B + 1))
              + _shift_rows(xb, xbt, 1, False) * tap(CW_B + 2)) + xb * tap(CW_B + 3)
        xbbuf[...] = xb[tm - SUBLANES:, :]
        sp = _softplus_neg(row(R_LAMBDA))
        _, _, ig, a, mult = _gates(xc, wr_ref, wi_ref, row(R_B_GATE_R), row(R_B_GATE_I), sp, nh, bw)
        a_s[...] = a
        b_s[...] = mult * (ig * xc)

        def blk(j, hp):
            o = pl.multiple_of(j * SUBLANES, SUBLANES)
            a8, b8 = _scan_block(a_s[pl.ds(o, SUBLANES), :], b_s[pl.ds(o, SUBLANES), :], reverse=False)
            h8 = b8 + a8 * hp
            hs_ref[pl.ds(o, SUBLANES), :] = h8
            return jnp.broadcast_to(h8[SUBLANES - 1:SUBLANES, :], (SUBLANES, d))

        hprev[...] = lax.fori_loop(0, tm // SUBLANES, blk, hprev[...])
        gel, _ = _gelu(proj_ref[:, 4 * d:5 * d])
        yb = hs_ref[...] * gel
        yap = _dot(ya.astype(BF16), wa_ref[...])
        ybp = _dot(yb.astype(BF16), wb_ref[...])
        yap_ref[...] = yap
        ybp_ref[...] = ybp
        m = _sigmoid(proj_ref[:, 5 * d:6 * d]) * yap + _sigmoid(proj_ref[:, 6 * d:7 * d]) * ybp
        y = _dot(m.astype(BF16), wo_ref[...])
        y_ref[...] = y
        x1_ref[...] = x_ref[...] + mod_ref[M_GT_M:M_GT_M + 1, :] * ((y * _rms(y)) * row(R_G_POST_MIX))

    tile = pl.BlockSpec((tm, d), lambda i: (i, 0))
    return pl.pallas_call(
        body,
        name="mixer_core_fwd",
        grid=(s // tm,),
        in_specs=[
            pl.BlockSpec((tm, 7 * d), lambda i: (i, 0)),
            tile,
            _resident((None, N_MOD, d), lambda i: (l, 0, 0)),
            _resident((None, N_ROWS, d), lambda i: (l, 0, 0)),
            _resident((None, CW_ROWS, d), lambda i: (l, 0, 0)),
            _resident((None, nh, bw, bw), lambda i: (l, 0, 0, 0)),
            _resident((None, nh, bw, bw), lambda i: (l, 0, 0, 0)),
            _resident((d, d), lambda i: (0, 0)),
            _resident((d, d), lambda i: (0, 0)),
            _resident((d, d), lambda i: (0, 0)),
        ],
        out_specs=(tile,) * 5,
        out_shape=(jax.ShapeDtypeStruct((s, d), F32),) * 5,
        scratch_shapes=[
            pltpu.VMEM((SUBLANES, d), F32),
            pltpu.VMEM((SUBLANES, d), F32),
            pltpu.VMEM((tm, d), F32),
            pltpu.VMEM((tm, d), F32),
            pltpu.VMEM((SUBLANES, d), F32),
        ],
        compiler_params=_cparams(("arbitrary",), 3 * _nbytes((d, d), BF16) + 2 * _nbytes((tm, 7 * d), F32) + 40 * _nbytes((tm, d), F32)),
    )(*_hbm(proj, x), mod, rows, cwf, wr, wi, wa_f, wb_f, wo_f)


def _mlp_fwd(x1, mod, rows, wup_f, wdn_f, l, tm):
    s, d = x1.shape
    nb, _, cu = wup_f.shape
    dff = nb * cu

    def body(x1_ref, mod_ref, rows_ref, wu_ref, wd_ref, x2_ref, ru_ref, y2_ref, h2_ref):
        xv = x1_ref[...]
        g = rows_ref[R_G_PRE_MLP:R_G_PRE_MLP + 1, :]
        h2 = ((xv * _rms(xv) * g) * (1.0 + mod_ref[M_SC_F:M_SC_F + 1, :]) + mod_ref[M_SH_F:M_SH_F + 1, :]).astype(BF16)
        h2_ref[...] = h2
        ru = jnp.concatenate([jnp.maximum(_dot(h2, wu_ref[j]), 0.0) for j in range(nb)], axis=1)
        ru_ref[...] = ru.astype(BF16)
        y2 = _dot((ru * ru).astype(BF16), wd_ref[...])
        y2_ref[...] = y2
        x2_ref[...] = xv + mod_ref[M_GT_F:M_GT_F + 1, :] * ((y2 * _rms(y2)) * rows_ref[R_G_POST_MLP:R_G_POST_MLP + 1, :])

    tile = pl.BlockSpec((tm, d), lambda i: (i, 0))
    wide = pl.BlockSpec((tm, dff), lambda i: (i, 0))
    return pl.pallas_call(
        body,
        name="mlp_fwd",
        grid=(s // tm,),
        in_specs=[
            tile,
            _resident((None, N_MOD, d), lambda i: (l, 0, 0)),
            _resident((None, N_ROWS, d), lambda i: (l, 0, 0)),
            _resident((nb, d, cu), lambda i: (0, 0, 0)),
            _resident((dff, d), lambda i: (0, 0)),
        ],
        out_specs=(tile, wide, tile, tile),
        out_shape=(jax.ShapeDtypeStruct((s, d), F32), jax.ShapeDtypeStruct((s, dff), BF16),
                   jax.ShapeDtypeStruct((s, d), F32), jax.ShapeDtypeStruct((s, d), BF16)),
        compiler_params=_cparams(("parallel",), 2 * _nbytes((dff, d), BF16) + 5 * _nbytes((tm, dff), F32) + 12 * _nbytes((tm, d), F32)),
    )(*_hbm(x1), mod, rows, wup_f, wdn_f)


def _loss_fwd_bwd(y, target, tm):
    s, d = y.shape

    def body(y_ref, t_ref, loss_ref, dy_ref):
        @pl.when(pl.program_id(0) == 0)
        def _():
            loss_ref[...] = jnp.zeros(loss_ref.shape, F32)

        e = y_ref[...] - t_ref[...]
        dy_ref[...] = e * (1.0 / d)
        loss_ref[...] += 0.5 * jnp.sum(jnp.mean(e * e, axis=-1, keepdims=True), axis=0, keepdims=True)

    tile = pl.BlockSpec((tm, d), lambda i: (i, 0))
    loss, dy = pl.pallas_call(
        body,
        name="loss",
        grid=(s // tm,),
        in_specs=[tile, tile],
        out_specs=(pl.BlockSpec((SUBLANES, 128), lambda i: (0, 0)), tile),
        out_shape=(jax.ShapeDtypeStruct((SUBLANES, 128), F32), jax.ShapeDtypeStruct((s, d), F32)),
        compiler_params=_cparams(("arbitrary",)),
    )(*_hbm(y, target))
    return loss[0, 0], dy


def _mlp_bwd(dx2, x1, y2, ru, mod, rows, wup_f, wdn_f, l, tm):
    s, d = x1.shape
    nb, _, cu = wup_f.shape
    dff = nb * cu

    def body(dx2_ref, x1_ref, y2_ref, ru_ref, mod_ref, rows_ref, wu_ref, wd_ref, dx1_ref, dy2_ref, dup_ref, act_ref, sm_ref):
        @pl.when(pl.program_id(0) == 0)
        def _():
            sm_ref[...] = jnp.zeros(sm_ref.shape, F32)

        dout = dx2_ref[...]
        dy2, dgt, dgpost = _postnorm_bwd(y2_ref[...], dout, rows_ref[R_G_POST_MLP:R_G_POST_MLP + 1, :], mod_ref[M_GT_F:M_GT_F + 1, :])
        dy2b = dy2.astype(BF16)
        dy2_ref[...] = dy2b
        ruv = ru_ref[...].astype(F32)
        act_ref[...] = (ruv * ruv).astype(BF16)
        dup = (_dot_nt(dy2b, wd_ref[...]) * (2.0 * ruv)).astype(BF16)
        dup_ref[...] = dup
        dh2 = _dot_nt(dup[:, 0:cu], wu_ref[0])
        for j in range(1, nb):
            dh2 = dh2 + _dot_nt(dup[:, j * cu:(j + 1) * cu], wu_ref[j])
        dxn, dsc, dsh, dgpre = _prenorm_bwd(x1_ref[...], dh2, rows_ref[R_G_PRE_MLP:R_G_PRE_MLP + 1, :], mod_ref[M_SC_F:M_SC_F + 1, :])
        dx1_ref[...] = dout + dxn
        for r, v in ((G_MLP_GT, dgt), (G_MLP_GPOST, dgpost), (G_MLP_SC, dsc), (G_MLP_SH, dsh), (G_MLP_GPRE, dgpre)):
            sm_ref[r:r + 1, :] += v

    tile = pl.BlockSpec((tm, d), lambda i: (i, 0))
    wide = pl.BlockSpec((tm, dff), lambda i: (i, 0))
    return pl.pallas_call(
        body,
        name="mlp_bwd",
        grid=(s // tm,),
        in_specs=[
            tile, tile, tile, wide,
            _resident((None, N_MOD, d), lambda i: (l, 0, 0)),
            _resident((None, N_ROWS, d), lambda i: (l, 0, 0)),
            _resident((nb, d, cu), lambda i: (0, 0, 0)),
            _resident((dff, d), lambda i: (0, 0)),
        ],
        out_specs=(tile, tile, wide, wide, pl.BlockSpec((G_MLP_ROWS, d), lambda i: (0, 0))),
        out_shape=(jax.ShapeDtypeStruct((s, d), F32), jax.ShapeDtypeStruct((s, d), BF16), jax.ShapeDtypeStruct((s, dff), BF16),
                   jax.ShapeDtypeStruct((s, dff), BF16), jax.ShapeDtypeStruct((G_MLP_ROWS, d), F32)),
        compiler_params=_cparams(("arbitrary",), 2 * _nbytes((dff, d), BF16) + 6 * _nbytes((tm, dff), F32) + 16 * _nbytes((tm, d), F32)),
    )(*_hbm(dx2, x1, y2, ru), mod, rows, wup_f, wdn_f)


def _mixer_core_bwd(dx1, y, yap, ybp, hs, proj, mod, rows, cwf, wr, wi, wa_f, wb_f, wo_f, l, tm):
    s, d = dx1.shape
    nh, bw, _ = wr.shape[1:]
    nt = s // tm
    per = tm // SUBLANES

    def body(dx1_ref, y_ref, yap_ref, ybp_ref, hs_ref, hsh_ref, proj_ref, projh_ref, mod_ref, rows_ref, cw_ref,
             wr_ref, wi_ref, wa_ref, wb_ref, wo_ref,
             dproj_ref, dy_ref, m_ref, dyap_ref, dybp_ref, ya_ref, yb_ref, sm_ref, dwg_ref,
             abuf, dcbuf, dxbuf, al_s, dh_s, lam_s, lnext):
        i = pl.program_id(0)
        first_tile = i == nt - 1

        @pl.when(i == 0)
        def _():
            sm_ref[...] = jnp.zeros(sm_ref.shape, F32)
            dwg_ref[...] = jnp.zeros(dwg_ref.shape, F32)
            zero = jnp.zeros((SUBLANES, d), F32)
            abuf[...] = zero
            dcbuf[...] = zero
            dxbuf[...] = zero
            lnext[...] = zero

        def row(r):
            return rows_ref[r:r + 1, :]

        def tap(r):
            return cw_ref[r:r + 1, :]

        def acc(r, v):
            sm_ref[r:r + 1, :] += v

        keep_halo = jnp.where(first_tile, 0.0, 1.0)
        dy, dgt, dgpost = _postnorm_bwd(y_ref[...], dx1_ref[...], row(R_G_POST_MIX), mod_ref[M_GT_M:M_GT_M + 1, :])
        acc(G_MIX_GT, dgt)
        acc(G_MIX_GPOST, dgpost)
        dyb16 = dy.astype(BF16)
        dy_ref[...] = dyb16
        dm = _dot_nt(dyb16, wo_ref[...])
        sa = _sigmoid(proj_ref[:, 5 * d:6 * d])
        sb = _sigmoid(proj_ref[:, 6 * d:7 * d])
        yap = yap_ref[...]
        ybp = ybp_ref[...]
        m_ref[...] = (sa * yap + sb * ybp).astype(BF16)
        dyap = (dm * sa).astype(BF16)
        dybp = (dm * sb).astype(BF16)
        dyap_ref[...] = dyap
        dybp_ref[...] = dybp
        dproj_ref[:, 5 * d:6 * d] = (dm * yap * sa * (1.0 - sa)).astype(BF16)
        dproj_ref[:, 6 * d:7 * d] = (dm * ybp * sb * (1.0 - sb)).astype(BF16)
        dya = _dot_nt(dyap, wa_ref[...])
        dyb = _dot_nt(dybp, wb_ref[...])
        ba = proj_ref[:, 0:d]
        ca = proj_ref[:, d:2 * d]
        va = proj_ref[:, 2 * d:3 * d]
        cv = ca * va
        cvh = keep_halo * (projh_ref[:, d:2 * d] * projh_ref[:, 2 * d:3 * d])
        cvm2 = _shift_rows(cv, cvh, 2, False)
        cvm1 = _shift_rows(cv, cvh, 1, False)
        conv3 = ((row(R_CONV_A_B) + cvm2 * tap(CW_A)) + cvm1 * tap(CW_A + 1)) + cv * tap(CW_A + 2)
        ya_ref[...] = (ba * conv3).astype(BF16)
        dproj_ref[:, 0:d] = (dya * conv3).astype(BF16)
        dc3 = dya * ba
        acc(G_MIX_CAB, _colsum(dc3))
        acc(G_MIX_CAW, _colsum(dc3 * cvm2))
        acc(G_MIX_CAW + 1, _colsum(dc3 * cvm1))
        acc(G_MIX_CAW + 2, _colsum(dc3 * cv))
        dct = dcbuf[...]
        dcv = (dc3 * tap(CW_A + 2) + _shift_rows(dc3, dct, 1, True) * tap(CW_A + 1)) + _shift_rows(dc3, dct, 2, True) * tap(CW_A)
        dcbuf[...] = dc3[:SUBLANES, :]
        dproj_ref[:, d:2 * d] = (dcv * va).astype(BF16)
        dproj_ref[:, 2 * d:3 * d] = (dcv * ca).astype(BF16)
        xb = proj_ref[:, 3 * d:4 * d]
        gb = proj_ref[:, 4 * d:5 * d]
        xbh = keep_halo * projh_ref[:, 3 * d:4 * d]
        xm3 = _shift_rows(xb, xbh, 3, False)
        xm2 = _shift_rows(xb, xbh, 2, False)
        xm1 = _shift_rows(xb, xbh, 1, False)
        xc = (((row(R_CONV_B_B) + xm3 * tap(CW_B)) + xm2 * tap(CW_B + 1)) + xm1 * tap(CW_B + 2)) + xb * tap(CW_B + 3)
        lam = row(R_LAMBDA)
        sp = _softplus_neg(lam)
        xcb, r, ig, a, mult = _gates(xc, wr_ref, wi_ref, row(R_B_GATE_R), row(R_B_GATE_I), sp, nh, bw)
        gel, th = _gelu(gb)
        hs = hs_ref[...]
        yb_ref[...] = (hs * gel).astype(BF16)
        dproj_ref[:, 4 * d:5 * d] = (dyb * hs * _gelu_grad(gb, th)).astype(BF16)
        al_s[...] = _shift_rows(a, abuf[...], 1, True)
        abuf[...] = a[:SUBLANES, :]
        dh_s[...] = dyb * gel

        def blk(j, ln):
            o = pl.multiple_of((per - 1 - j) * SUBLANES, SUBLANES)
            a8, b8 = _scan_block(al_s[pl.ds(o, SUBLANES), :], dh_s[pl.ds(o, SUBLANES), :], reverse=True)
            l8 = b8 + a8 * ln
            lam_s[pl.ds(o, SUBLANES), :] = l8
            return jnp.broadcast_to(l8[0:1, :], (SUBLANES, d))

        lnext[...] = lax.fori_loop(0, per, blk, lnext[...])
        dbb = lam_s[...]
        da = dbb * _shift_rows(hs, keep_halo * hsh_ref[...], 1, False)
        dmult = dbb * (ig * xc)
        dig = dbb * (mult * xc)
        dxc = dbb * (mult * ig)
        dla = da * a - dmult * (a * a) / mult
        acc(G_MIX_LAM, _colsum(dla * (-LRU_C * r)) * (-_sigmoid(-lam)))
        dzr = (dla * (-LRU_C * sp)) * r * (1.0 - r)
        dzi = dig * ig * (1.0 - ig)
        acc(G_MIX_BR, _colsum(dzr))
        acc(G_MIX_BI, _colsum(dzi))
        dzrb = dzr.astype(BF16)
        dzib = dzi.astype(BF16)
        back = []
        for h in range(nh):
            sl = slice(h * bw, (h + 1) * bw)
            back.append(_dot_nt(dzrb[:, sl], wr_ref[h]) + _dot_nt(dzib[:, sl], wi_ref[h]))
            dwg_ref[0, h] += _dot_tn(xcb[:, sl], dzrb[:, sl])
            dwg_ref[1, h] += _dot_tn(xcb[:, sl], dzib[:, sl])
        dxc = dxc + jnp.concatenate(back, axis=1)
        acc(G_MIX_CBB, _colsum(dxc))
        acc(G_MIX_CBW, _colsum(dxc * xm3))
        acc(G_MIX_CBW + 1, _colsum(dxc * xm2))
        acc(G_MIX_CBW + 2, _colsum(dxc * xm1))
        acc(G_MIX_CBW + 3, _colsum(dxc * xb))
        dxt = dxbuf[...]
        dxb = (((dxc * tap(CW_B + 3) + _shift_rows(dxc, dxt, 1, True) * tap(CW_B + 2)) + _shift_rows(dxc, dxt, 2, True) * tap(CW_B + 1))
               + _shift_rows(dxc, dxt, 3, True) * tap(CW_B))
        dxbuf[...] = dxc[:SUBLANES, :]
        dproj_ref[:, 3 * d:4 * d] = dxb.astype(BF16)

    def rev(i):
        return (nt - 1 - i, 0)

    def halo(i):
        return (jnp.maximum((nt - 1 - i) * per - 1, 0), 0)

    tile = pl.BlockSpec((tm, d), rev)
    return pl.pallas_call(
        body,
        name="mixer_core_bwd",
        grid=(nt,),
        in_specs=[
            tile, tile, tile, tile, tile,
            pl.BlockSpec((SUBLANES, d), halo),
            pl.BlockSpec((tm, 7 * d), rev),
            pl.BlockSpec((SUBLANES, 7 * d), halo),
            _resident((None, N_MOD, d), lambda i: (l, 0, 0)),
            _resident((None, N_ROWS, d), lambda i: (l, 0, 0)),
            _resident((None, CW_ROWS, d), lambda i: (l, 0, 0)),
            _resident((None, nh, bw, bw), lambda i: (l, 0, 0, 0)),
            _resident((None, nh, bw, bw), lambda i: (l, 0, 0, 0)),
            _resident((d, d), lambda i: (0, 0)),
            _resident((d, d), lambda i: (0, 0)),
            _resident((d, d), lambda i: (0, 0)),
        ],
        out_specs=(pl.BlockSpec((tm, 7 * d), rev),) + (tile,) * 6 + (
            pl.BlockSpec((G_MIX_ROWS, d), lambda i: (0, 0)), pl.BlockSpec((2, nh, bw, bw), lambda i: (0, 0, 0, 0))),
        out_shape=(jax.ShapeDtypeStruct((s, 7 * d), BF16),) + (jax.ShapeDtypeStruct((s, d), BF16),) * 6 + (
            jax.ShapeDtypeStruct((G_MIX_ROWS, d), F32), jax.ShapeDtypeStruct((2, nh, bw, bw), F32)),
        scratch_shapes=[pltpu.VMEM((SUBLANES, d), F32)] * 3 + [pltpu.VMEM((tm, d), F32)] * 3 + [pltpu.VMEM((SUBLANES, d), F32)],
        compiler_params=_cparams(("arbitrary",), 3 * _nbytes((d, d), BF16) + 3 * _nbytes((tm, 7 * d), F32) + 64 * _nbytes((tm, d), F32)),
    )(*_hbm(dx1, y, yap, ybp, hs, hs, proj, proj), mod, rows, cwf, wr, wi, wa_f, wb_f, wo_f)


def _in_proj_bwd(dproj, x, dx1, mod, rows, win_f, l, tm):
    s, d = x.shape
    nb, _, ci = win_f.shape

    def body(dp_ref, x_ref, dx1_ref, mod_ref, rows_ref, w_ref, dx_ref, sm_ref):
        @pl.when(pl.program_id(0) == 0)
        def _():
            sm_ref[...] = jnp.zeros(sm_ref.shape, F32)

        dh = _dot_nt(dp_ref[:, 0:ci], w_ref[0])
        for j in range(1, nb):
            dh = dh + _dot_nt(dp_ref[:, j * ci:(j + 1) * ci], w_ref[j])
        dxn, dsc, dsh, dg = _prenorm_bwd(x_ref[...], dh, rows_ref[R_G_PRE_MIX:R_G_PRE_MIX + 1, :], mod_ref[M_SC_M:M_SC_M + 1, :])
        dx_ref[...] = dx1_ref[...] + dxn
        for r, v in ((G_IN_SC, dsc), (G_IN_SH, dsh), (G_IN_GPRE, dg)):
            sm_ref[r:r + 1, :] += v

    tile = pl.BlockSpec((tm, d), lambda i: (i, 0))
    return pl.pallas_call(
        body,
        name="in_proj_bwd",
        grid=(s // tm,),
        in_specs=[
            pl.BlockSpec((tm, nb * ci), lambda i: (i, 0)), tile, tile,
            _resident((None, N_MOD, d), lambda i: (l, 0, 0)),
            _resident((None, N_ROWS, d), lambda i: (l, 0, 0)),
            _resident((nb, d, ci), lambda i: (0, 0, 0)),
        ],
        out_specs=(tile, pl.BlockSpec((G_IN_ROWS, d), lambda i: (0, 0))),
        out_shape=(jax.ShapeDtypeStruct((s, d), F32), jax.ShapeDtypeStruct((G_IN_ROWS, d), F32)),
        compiler_params=_cparams(("arbitrary",), _nbytes((nb, d, ci), BF16) + 2 * _nbytes((tm, nb * ci), BF16) + 16 * _nbytes((tm, d), F32)),
    )(*_hbm(dproj, x, dx1), mod, rows, win_f)


def _wgrad(a, b, cols_owned, ts, after):
    s, k1 = a.shape
    k2 = b.shape[1]
    ns = s // ts
    if cols_owned:
        nblk, bk1, bk2 = N_DEV, k1, k2 // N_DEV
        a_spec = pl.BlockSpec((ts, bk1), lambda j, t: (t, 0))
        b_spec = pl.BlockSpec((ts, bk2), lambda j, t: (t, j))
    else:
        bk1, bk2 = min(WGRAD_ROWS, k1), k2
        nblk = k1 // bk1
        a_spec = pl.BlockSpec((ts, bk1), lambda j, t: (t, j))
        b_spec = pl.BlockSpec((ts, bk2), lambda j, t: (t, 0))

    def body(a_ref, b_ref, after_ref, o_ref, acc_ref):
        t = pl.program_id(1)

        @pl.when(t == 0)
        def _():
            acc_ref[...] = jnp.zeros(acc_ref.shape, F32)

        acc_ref[...] += _dot_tn(a_ref[...], b_ref[...])

        @pl.when(t == ns - 1)
        def _():
            o_ref[...] = acc_ref[...].astype(BF16)

    out = pl.pallas_call(
        body,
        name="wgrad",
        grid=(nblk, ns),
        in_specs=[a_spec, b_spec, ANY_SPEC],
        out_specs=pl.BlockSpec((None, bk1, bk2), lambda j, t: (j, 0, 0)),
        out_shape=pltpu.HBM((nblk, bk1, bk2), BF16),
        scratch_shapes=[pltpu.VMEM((bk1, bk2), F32)],
        compiler_params=_cparams(("parallel", "arbitrary"), 4 * _nbytes((bk1, bk2), F32) + 4 * _nbytes((ts, bk1 + bk2), BF16)),
    )(pltpu.with_memory_space_constraint(a, pltpu.HBM), pltpu.with_memory_space_constraint(b, pltpu.HBM), after)
    return out if cols_owned else out.reshape(N_DEV, k1 // N_DEV, k2)


def _adam_update(w, g, m, v):
    m = ADAM_B1 * m + (1.0 - ADAM_B1) * g
    v = ADAM_B2 * v + (1.0 - ADAM_B2) * (g * g)
    m_hat = m / (1.0 - ADAM_B1 ** ADAM_STEP)
    v_hat = v / (1.0 - ADAM_B2 ** ADAM_STEP)
    delta = -ADAM_LR * (m_hat / (jnp.sqrt(v_hat) + ADAM_EPS) + ADAM_WD * w)
    return delta, m, v


def _sum_adamw(recv, w, m, v, tr, after):
    nl, ra, cb = w.shape
    assert nl == len(recv) == 2

    def body(r0_ref, r1_ref, w_ref, m_ref, v_ref, after_ref, g_ref, d_ref, nm_ref, nv_ref):
        def total(r_ref):
            g = r_ref[0].astype(F32)
            for j in range(1, N_DEV):
                g = g + r_ref[j].astype(F32)
            return g

        g = jnp.where(pl.program_id(0) == 0, total(r0_ref), total(r1_ref))
        g_ref[...] = g
        d_ref[...], nm_ref[...], nv_ref[...] = _adam_update(w_ref[...], g, m_ref[...], v_ref[...])

    blk = pl.BlockSpec((None, tr, cb), lambda l, i: (l, i, 0))
    return pl.pallas_call(
        body,
        name="sum_adamw",
        grid=(nl, ra // tr),
        in_specs=[pl.BlockSpec((N_DEV, tr, cb), lambda l, i: (0, i * (1 - l), 0)),
                  pl.BlockSpec((N_DEV, tr, cb), lambda l, i: (0, i * l, 0)), blk, blk, blk, ANY_SPEC],
        out_specs=(blk,) * 4,
        out_shape=(jax.ShapeDtypeStruct((nl, ra, cb), F32),) * 4,
        compiler_params=_cparams(("arbitrary", "arbitrary"), 6 * _nbytes((N_DEV, tr, cb), BF16) + 32 * _nbytes((tr, cb), F32)),
    )(recv[0], recv[1], w, m, v, after)


def _adamw(w, g, m, v):
    def body(w_ref, g_ref, m_ref, v_ref, d_ref, nm_ref, nv_ref):
        d_ref[...], nm_ref[...], nv_ref[...] = _adam_update(w_ref[...], g_ref[...], m_ref[...], v_ref[...])

    return pl.pallas_call(
        body,
        name="adamw",
        in_specs=[VMEM_SPEC] * 4,
        out_specs=(VMEM_SPEC,) * 3,
        out_shape=(jax.ShapeDtypeStruct(w.shape, F32),) * 3,
        compiler_params=_cparams(vmem=10 * _nbytes(w.shape, F32)),
    )(w, g, m, v)


def _adamw_tiled(w, g, m, v, tr):
    nl, ra, cb = w.shape

    def body(w_ref, g_ref, m_ref, v_ref, d_ref, nm_ref, nv_ref):
        d_ref[...], nm_ref[...], nv_ref[...] = _adam_update(w_ref[...], g_ref[...], m_ref[...], v_ref[...])

    blk = pl.BlockSpec((None, tr, cb), lambda l, i: (l, i, 0))
    return pl.pallas_call(
        body,
        name="adamw_tiled",
        grid=(nl, ra // tr),
        in_specs=[blk] * 4,
        out_specs=(blk,) * 3,
        out_shape=(jax.ShapeDtypeStruct((nl, ra, cb), F32),) * 3,
        compiler_params=_cparams(("parallel", "parallel")),
    )(*_hbm(w, g, m, v))


def _token_tile(s):
    return min(256, max(SUBLANES * 2, s // 4))


def kernel(x, c, w_mod, b_mod, g_pre_mix, g_post_mix, w_in, conv_a_w, conv_a_b, w_a_out, conv_b_w, conv_b_b, w_gate_r, b_gate_r, w_gate_i, b_gate_i, lru_lambda, w_b_out, w_o, g_pre_mlp, g_post_mlp, w_mlp_up, w_mlp_down, loss_target, m_w_mod, m_b_mod, m_g_pre_mix, m_g_post_mix, m_w_in, m_conv_a_w, m_conv_a_b, m_w_a_out, m_conv_b_w, m_conv_b_b, m_w_gate_r, m_b_gate_r, m_w_gate_i, m_b_gate_i, m_lru_lambda, m_w_b_out, m_w_o, m_g_pre_mlp, m_g_post_mlp, m_w_mlp_up, m_w_mlp_down, v_w_mod, v_b_mod, v_g_pre_mix, v_g_post_mix, v_w_in, v_conv_a_w, v_conv_a_b, v_w_a_out, v_conv_b_w, v_conv_b_b, v_w_gate_r, v_b_gate_r, v_w_gate_i, v_b_gate_i, v_lru_lambda, v_w_b_out, v_w_o, v_g_pre_mlp, v_g_post_mlp, v_w_mlp_up, v_w_mlp_down):
    nl = w_mod.shape[0]
    s, d = x.shape[1], x.shape[2]
    nh, bw = w_gate_r.shape[1], w_gate_r.shape[2]
    cwid = conv_a_w.shape[2]
    tm = _token_tile(s)
    ts = s
    _, me = _my_position()
    xs = x.reshape(s, d)
    target = loss_target.reshape(s, d)

    vec_names = (g_pre_mix, g_post_mix, conv_a_b, conv_b_b, b_gate_r, b_gate_i, lru_lambda, g_pre_mlp, g_post_mlp)
    rows = jnp.concatenate([jnp.stack(vec_names, axis=1), jnp.zeros((nl, N_ROWS - len(vec_names), d), F32)], axis=1)
    cw = jnp.concatenate([conv_a_w, conv_b_w, jnp.zeros((nl, CW_ROWS - 7, cwid), F32)], axis=1)

    w16 = {"w_in": w_in.astype(BF16), "w_a_out": w_a_out.astype(BF16), "w_b_out": w_b_out.astype(BF16), "w_o": w_o.astype(BF16),
           "w_mlp_up": w_mlp_up.astype(BF16), "w_mlp_down": w_mlp_down.astype(BF16)}
    groups = (("in", ("w_in",)), ("mix", ("w_a_out", "w_b_out", "w_o")), ("mlp", ("w_mlp_up", "w_mlp_down")))
    mod, cact, cwf = _prep_small(c, w_mod, b_mod, cw)
    mod = mod.reshape(nl, N_MOD, d)
    w16, _ = lax.optimization_barrier((w16, mod))
    gathers = {}
    token = jnp.zeros((), F32)
    for l in range(nl):
        for gname, members in groups:
            gathers[l, gname] = _gather2_start([w16[n][l] for n in members], f"gather_start_{gname}{l}")
            token = token + gathers[l, gname][4][0, 0]
    rows = rows + token
    wr = w_gate_r.astype(BF16)
    wi = w_gate_i.astype(BF16)

    forwarded = {}

    def forward(l, gname, after, rows):
        forwarded[l, gname] = _gather2_forward(gathers[l, gname], after, f"gather_forward_{gname}{l}")
        return rows + forwarded[l, gname][4][0, 0]

    def gathered(l, gname, after):
        return _gather2_wait(forwarded[l, gname], after, f"gather_wait_{gname}{l}")

    saved = []
    weights = []
    xin = xs
    for l in range(nl):
        if l == 0:
            rows = forward(0, "in", mod, rows)
        (win_f,) = gathered(l, "in", mod if l == 0 else xin)
        if l > 0:
            rows = forward(l, "mix", win_f, rows)
        proj, h = _in_proj_fwd(xin, mod, rows, win_f, l, tm)
        if l == 0:
            rows = forward(0, "mix", proj, rows)
        wa_f, wb_f, wo_f = (w.reshape(d, d) for w in gathered(l, "mix", proj))
        rows = forward(l, "mlp", wo_f, rows)
        x1, hs, yap, ybp, y = _mixer_core_fwd(proj, xin, mod, rows, cwf, wr, wi, wa_f, wb_f, wo_f, l, tm)
        wup_f, wdn_f = gathered(l, "mlp", x1)
        wdn_f = wdn_f.reshape(-1, d)
        if l + 1 < nl:
            rows = forward(l + 1, "in", wdn_f, rows)
        x2, ru, y2, h2 = _mlp_fwd(x1, mod, rows, wup_f, wdn_f, l, tm)
        saved.append((xin, proj, h, x1, hs, yap, ybp, y, ru, y2, h2))
        weights.append((win_f, wa_f, wb_f, wo_f, wup_f, wdn_f))
        xin = x2
    loss_part, dx = _loss_fwd_bwd(xin, target, tm)

    scatters = {}
    small = [None] * nl
    gate_parts = [None] * nl

    def scatter(l, gname, parts, rows):
        scatters[l, gname] = _exchange_start(parts, False, f"scatter_start_{gname}{l}")
        return rows + scatters[l, gname][4][0, 0]

    for l in reversed(range(nl)):
        xin, proj, h, x1, hs, yap, ybp, y, ru, y2, h2 = saved[l]
        win_f, wa_f, wb_f, wo_f, wup_f, wdn_f = weights[l]
        dx1, dy2, dup, act, sm_mlp = _mlp_bwd(dx, x1, y2, ru, mod, rows, wup_f, wdn_f, l, tm)
        g_up = _wgrad(h2, dup, True, ts, dx1)
        g_dn = _wgrad(act, dy2, False, ts, g_up)
        rows = scatter(l, "mlp", [g_up, g_dn], rows)
        dproj, dy, m, dyap, dybp, ya, yb, sm_mix, dwg = _mixer_core_bwd(dx1, y, yap, ybp, hs, proj, mod, rows, cwf, wr, wi, wa_f, wb_f, wo_f, l, tm // 2)
        gate_parts[l] = _exchange_start([dwg.astype(BF16)], True, f"gates_start{l}")
        rows = rows + gate_parts[l][4][0, 0]
        g_a = _wgrad(ya, dyap, False, ts, gate_parts[l][4])
        g_b = _wgrad(yb, dybp, False, ts, g_a)
        g_o = _wgrad(m, dy, False, ts, g_b)
        rows = scatter(l, "mix", [g_a, g_b, g_o], rows)
        rows = scatter(l, "in", [_wgrad(h, dproj, True, ts, scatters[l, "mix"][4])], rows)
        dx, sm_in = _in_proj_bwd(dproj, xin, dx1, mod, rows, win_f, l, tm)
        small[l] = jnp.concatenate([sm_mlp, sm_mix, sm_in], axis=0)
    grad_x = dx.reshape(x.shape)

    recv = {}
    big = {}
    moments = {"w_in": (w_in, m_w_in, v_w_in), "w_mlp_up": (w_mlp_up, m_w_mlp_up, v_w_mlp_up), "w_a_out": (w_a_out, m_w_a_out, v_w_a_out),
               "w_b_out": (w_b_out, m_w_b_out, v_w_b_out), "w_o": (w_o, m_w_o, v_w_o), "w_mlp_down": (w_mlp_down, m_w_mlp_down, v_w_mlp_down)}

    def collect(l, gname, after):
        for n, zone in zip(dict(groups)[gname], _exchange_wait(scatters[l, gname], after, False, f"scatter_wait_{gname}{l}")):
            recv[n, l] = zone

    def update(name, after):
        w, m_, v_ = moments[name]
        big[name] = _sum_adamw([recv[name, l] for l in range(nl)], w, m_, v_, min(256, w.shape[1]), after)
        return big[name][1]

    for l, gname in ((1, "mlp"), (1, "in"), (1, "mix"), (0, "mlp")):
        collect(l, gname, dx)
    early = update("w_mlp_down", update("w_mlp_up", dx))

    lrows = jnp.concatenate(small, axis=0)
    lrows = lrows.at[G_LOSS_ROW, 0].set(loss_part)

    def lrow(a, l, r):
        return a[l * G_LAYER_ROWS + r]

    dm = jnp.stack([jnp.concatenate([lrow(lrows, l, G_MLP_ROWS + G_MIX_ROWS + G_IN_SH), lrow(lrows, l, G_MLP_ROWS + G_MIX_ROWS + G_IN_SC),
                                     lrow(lrows, l, G_MLP_ROWS + G_MIX_GT), lrow(lrows, l, G_MLP_SH), lrow(lrows, l, G_MLP_SC),
                                     lrow(lrows, l, G_MLP_GT)]) for l in range(nl)])
    dm8 = jnp.concatenate([dm[:, None, :], jnp.zeros((nl, SUBLANES - 1, N_MOD * d), F32)], axis=1)
    srows, g_w_mod = _reduce_small(lrows, dm8, cact, early)
    loss = srows[G_LOSS_ROW, 0]
    sgates = _sum_gathered([_exchange_wait(gate_parts[l], srows, True, f"gates_wait{l}")[0] for l in range(nl)])
    for gname in ("in", "mix"):
        collect(0, gname, srows)
    for name in ("w_in", "w_a_out", "w_b_out", "w_o"):
        update(name, srows)

    def srow(l, r):
        return lrow(srows, l, r)

    def per_layer(r):
        return jnp.stack([srow(l, r) for l in range(nl)])

    mix0 = G_MLP_ROWS
    in0 = G_MLP_ROWS + G_MIX_ROWS
    g_b_mod = jnp.stack([jnp.concatenate([srow(l, in0 + G_IN_SH), srow(l, in0 + G_IN_SC), srow(l, mix0 + G_MIX_GT),
                                          srow(l, G_MLP_SH), srow(l, G_MLP_SC), srow(l, G_MLP_GT)]) for l in range(nl)])
    conv_a_full = jnp.stack([jnp.stack([srow(l, mix0 + G_MIX_CAW + k) for k in range(3)]) for l in range(nl)])
    conv_b_full = jnp.stack([jnp.stack([srow(l, mix0 + G_MIX_CBW + k) for k in range(4)]) for l in range(nl)])
    grads = {
        "b_mod": g_b_mod,
        "g_pre_mix": per_layer(in0 + G_IN_GPRE),
        "g_post_mix": per_layer(mix0 + G_MIX_GPOST),
        "conv_a_w": lax.dynamic_slice_in_dim(conv_a_full, me * cwid, cwid, axis=2),
        "conv_a_b": per_layer(mix0 + G_MIX_CAB),
        "conv_b_w": lax.dynamic_slice_in_dim(conv_b_full, me * cwid, cwid, axis=2),
        "conv_b_b": per_layer(mix0 + G_MIX_CBB),
        "w_gate_r": sgates[:, 0],
        "b_gate_r": per_layer(mix0 + G_MIX_BR),
        "w_gate_i": sgates[:, 1],
        "b_gate_i": per_layer(mix0 + G_MIX_BI),
        "lru_lambda": per_layer(mix0 + G_MIX_LAM),
        "g_pre_mlp": per_layer(G_MLP_GPRE),
        "g_post_mlp": per_layer(G_MLP_GPOST),
    }
    params = {
        "b_mod": (b_mod, m_b_mod, v_b_mod), "g_pre_mix": (g_pre_mix, m_g_pre_mix, v_g_pre_mix), "g_post_mix": (g_post_mix, m_g_post_mix, v_g_post_mix),
        "conv_a_w": (conv_a_w, m_conv_a_w, v_conv_a_w), "conv_a_b": (conv_a_b, m_conv_a_b, v_conv_a_b),
        "conv_b_w": (conv_b_w, m_conv_b_w, v_conv_b_w), "conv_b_b": (conv_b_b, m_conv_b_b, v_conv_b_b),
        "w_gate_r": (w_gate_r, m_w_gate_r, v_w_gate_r), "b_gate_r": (b_gate_r, m_b_gate_r, v_b_gate_r),
        "w_gate_i": (w_gate_i, m_w_gate_i, v_w_gate_i), "b_gate_i": (b_gate_i, m_b_gate_i, v_b_gate_i),
        "lru_lambda": (lru_lambda, m_lru_lambda, v_lru_lambda), "g_pre_mlp": (g_pre_mlp, m_g_pre_mlp, v_g_pre_mlp),
        "g_post_mlp": (g_post_mlp, m_g_post_mlp, v_g_post_mlp),
    }
    out = {}
    for name, g in grads.items():
        w, m_, v_ = params[name]
        flat = (-1, w.shape[-1])
        dl, nm, nv = _adamw(w.reshape(flat), g.reshape(flat), m_.reshape(flat), v_.reshape(flat))
        out[name] = (g.reshape(w.shape), dl.reshape(w.shape), nm.reshape(w.shape), nv.reshape(w.shape))
    out["w_mod"] = (g_w_mod,) + tuple(_adamw_tiled(w_mod, g_w_mod, m_w_mod, v_w_mod, min(128, d)))
    out.update(big)

    order = ("w_mod", "b_mod", "g_pre_mix", "g_post_mix", "w_in", "conv_a_w", "conv_a_b", "w_a_out", "conv_b_w", "conv_b_b", "w_gate_r", "b_gate_r",
             "w_gate_i", "b_gate_i", "lru_lambda", "w_b_out", "w_o", "g_pre_mlp", "g_post_mlp", "w_mlp_up", "w_mlp_down")
    return (loss, grad_x) + tuple(out[n][0] for n in order) + tuple(out[n][1] for n in order) + tuple(out[n][2] for n in order) + tuple(out[n][3] for n in order)
```

```python
import functools

import jax
import jax.numpy as jnp
from jax import lax
from jax.experimental import pallas as pl
from jax.experimental.pallas import tpu as pltpu

F32, BF16 = jnp.float32, jnp.bfloat16
EPS = 1e-6
LRU_C = 8.0
N_DEV = 8
N_MOD = 6
SUBLANES = 8
VMEM_BUDGET = 56 * 1024 * 1024
WGRAD_ROWS = 512
ADAM_LR, ADAM_B1, ADAM_B2, ADAM_EPS, ADAM_WD, ADAM_STEP = 0.001, 0.9, 0.999, 1e-08, 0.01, 10
MESH = pl.DeviceIdType.MESH
VMEM_SPEC = pl.BlockSpec(memory_space=pltpu.VMEM)
ANY_SPEC = pl.BlockSpec(memory_space=pl.ANY)
HBM_SPEC = pl.BlockSpec(memory_space=pltpu.HBM)
SEM_SPEC = pl.BlockSpec(memory_space=pltpu.SEMAPHORE)
SIDE_EFFECT = pltpu.SideEffectType.DATAFLOW_SIDE_EFFECTING

R_G_PRE_MIX, R_G_POST_MIX, R_CONV_A_B, R_CONV_B_B, R_B_GATE_R, R_B_GATE_I, R_LAMBDA, R_G_PRE_MLP, R_G_POST_MLP = range(9)
N_ROWS = 16
M_SH_M, M_SC_M, M_GT_M, M_SH_F, M_SC_F, M_GT_F = range(6)
CW_A, CW_B, CW_ROWS = 0, 3, 8
G_MLP_GT, G_MLP_GPOST, G_MLP_SC, G_MLP_SH, G_MLP_GPRE, G_LOSS_ROW, G_MLP_ROWS = 0, 1, 2, 3, 4, 7, 8
(G_MIX_GT, G_MIX_GPOST, G_MIX_CAB, G_MIX_CAW, G_MIX_CBB, G_MIX_CBW, G_MIX_BR, G_MIX_BI, G_MIX_LAM) = 0, 1, 2, 3, 6, 7, 11, 12, 13
G_MIX_ROWS = 16
G_IN_SC, G_IN_SH, G_IN_GPRE, G_IN_ROWS = 0, 1, 2, 8
G_LAYER_ROWS = G_MLP_ROWS + G_MIX_ROWS + G_IN_ROWS


def _cparams(dims=None, vmem=None):
    kw = {}
    if dims is not None:
        kw["dimension_semantics"] = dims
    if vmem is not None:
        kw["vmem_limit_bytes"] = int(min(max(vmem, 16 * 1024 * 1024), VMEM_BUDGET))
    return pltpu.CompilerParams(**kw)


def _nbytes(shape, dtype):
    n = 1
    for s in shape:
        n *= s
    return n * jnp.dtype(dtype).itemsize


def _hbm(*arrays):
    return tuple(pltpu.with_memory_space_constraint(a, pltpu.HBM) for a in arrays)


def _resident(block, index_map):
    return pl.BlockSpec(block, index_map, pipeline_mode=pl.Buffered(1))


def _my_position():
    x, y, c = lax.axis_index("x"), lax.axis_index("y"), lax.axis_index("c")
    return (x, y, c), 4 * x + 2 * y + c


def _peer(pos, k):
    x, y, c = pos
    px = 1 - x if k & 4 else x
    py = 1 - y if k & 2 else y
    pc = 1 - c if k & 1 else c
    return (px, py, pc), 4 * px + 2 * py + pc


def _remote(src, dst, ssem, rsem, peer):
    return pltpu.make_async_remote_copy(src_ref=src, dst_ref=dst, send_sem=ssem, recv_sem=rsem, device_id=peer, device_id_type=MESH)


def _dot(a, b):
    return jnp.dot(a, b, preferred_element_type=F32)


def _dot_nt(a, b):
    return lax.dot_general(a, b, (((1,), (1,)), ((), ())), preferred_element_type=F32)


def _dot_tn(a, b):
    return lax.dot_general(a, b, (((0,), (0,)), ((), ())), preferred_element_type=F32)


def _colsum(v):
    return jnp.sum(v, axis=0, keepdims=True)


def _sigmoid(v):
    return jax.nn.sigmoid(v)


def _gelu(v):
    k = 0.7978845608028654
    t = jnp.tanh(k * (v + 0.044715 * (v * v * v)))
    return 0.5 * v * (1.0 + t), t


def _gelu_grad(v, t):
    k = 0.7978845608028654
    return 0.5 * (1.0 + t) + 0.5 * v * (1.0 - t * t) * (k * (1.0 + 3.0 * 0.044715 * v * v))


def _expm1(v):
    u = jnp.exp(v)
    um1 = u - 1.0
    q = um1 * v / jnp.log(u)
    return jnp.where(um1 == 0.0, v, jnp.where(um1 == -1.0, -1.0, q))


def _softplus_neg(lam):
    z = -lam
    u = jnp.exp(-jnp.abs(z))
    w = 1.0 + u
    l1p = jnp.where(w == 1.0, u, jnp.log(w) * u / (w - 1.0))
    return jnp.maximum(z, 0.0) + l1p


def _rms(v):
    return lax.rsqrt(jnp.mean(v * v, axis=-1, keepdims=True) + EPS)


def _prenorm_bwd(xv, dh, g, sc):
    r = _rms(xv)
    xn = xv * r
    n = xn * g
    dsc = _colsum(dh * n)
    dsh = _colsum(dh)
    dn = dh * (1.0 + sc)
    dg = _colsum(dn * xn)
    dxn = dn * g
    dx = r * (dxn - xn * jnp.mean(dxn * xn, axis=-1, keepdims=True))
    return dx, dsc, dsh, dg


def _postnorm_bwd(yv, dout, g, gt):
    r = _rms(yv)
    yn = yv * r
    dgt = _colsum(dout * (yn * g))
    dn = dout * gt
    dg = _colsum(dn * yn)
    dyn = dn * g
    dy = r * (dyn - yn * jnp.mean(dyn * yn, axis=-1, keepdims=True))
    return dy, dgt, dg


def _gates(xc, wr_ref, wi_ref, b_r, b_i, sp, nh, bw):
    xcb = xc.astype(BF16)
    zr = jnp.concatenate([_dot(xcb[:, h * bw:(h + 1) * bw], wr_ref[h]) for h in range(nh)], axis=1) + b_r
    zi = jnp.concatenate([_dot(xcb[:, h * bw:(h + 1) * bw], wi_ref[h]) for h in range(nh)], axis=1) + b_i
    r = _sigmoid(zr)
    ig = _sigmoid(zi)
    la = (-LRU_C * r) * sp
    a = jnp.exp(la)
    mult = jnp.sqrt(-_expm1(2.0 * la))
    return xcb, r, ig, a, mult


def _shift_rows(cur, edge, k, up):
    t, dd = cur.shape
    blocks = cur.reshape(t // SUBLANES, SUBLANES, dd)
    row = lax.broadcasted_iota(jnp.int32, (1, SUBLANES, dd), 1)
    if up:
        r = pltpu.roll(blocks, SUBLANES - k, 1)
        nxt = jnp.concatenate([r[1:], pltpu.roll(edge, SUBLANES - k, 0)[None]], axis=0)
        out = jnp.where(row >= SUBLANES - k, nxt, r)
    else:
        r = pltpu.roll(blocks, k, 1)
        prv = jnp.concatenate([pltpu.roll(edge, k, 0)[None], r[:-1]], axis=0)
        out = jnp.where(row < k, prv, r)
    return out.reshape(t, dd)


def _scan_block(a8, b8, reverse):
    row = lax.broadcasted_iota(jnp.int32, a8.shape, 0)
    for s in (1, 2, 4):
        if reverse:
            keep = row < SUBLANES - s
            a_sh = pltpu.roll(a8, SUBLANES - s, 0)
            b_sh = pltpu.roll(b8, SUBLANES - s, 0)
        else:
            keep = row >= s
            a_sh = pltpu.roll(a8, s, 0)
            b_sh = pltpu.roll(b8, s, 0)
        b8 = b8 + a8 * jnp.where(keep, b_sh, 0.0)
        a8 = a8 * jnp.where(keep, a_sh, 1.0)
    return a8, b8


def _prep_small(c, w_mod, b_mod, cw):
    d = c.shape[1]
    cm = w_mod.shape[2]
    cwid = cw.shape[2]
    nl = w_mod.shape[0]

    def body(c_ref, wm_ref, bm_ref, cw_ref, mod_ref, cact_ref, cwf_ref, cbuf, pbuf, rbuf, ssem, rsem, lsem):
        pos, me = _my_position()
        me8 = pl.multiple_of(me * SUBLANES, SUBLANES)
        cbuf[pl.ds(me8, SUBLANES), :] = jnp.broadcast_to(c_ref[...], (SUBLANES, d))
        own_cw = pltpu.make_async_copy(cw_ref, cwf_ref.at[:, :, pl.ds(me * cwid, cwid)], lsem.at[0])
        own_cw.start()
        first = []
        for k in range(1, N_DEV):
            peer, _ = _peer(pos, k)
            rows = cbuf.at[pl.ds(me8, SUBLANES), :]
            first.append(_remote(rows, rows, ssem.at[0, k - 1], rsem.at[0, k - 1], peer))
            first.append(_remote(cw_ref, cwf_ref.at[:, :, pl.ds(me * cwid, cwid)], ssem.at[1, k - 1], rsem.at[1, k - 1], peer))
        for cp in first:
            cp.start()
        for k in range(1, N_DEV):
            peer, pj = _peer(pos, k)
            pj8 = pl.multiple_of(pj * SUBLANES, SUBLANES)
            rows = cbuf.at[pl.ds(pj8, SUBLANES), :]
            _remote(rows, rows, ssem.at[0, k - 1], rsem.at[0, k - 1], peer).wait_recv()
        cv = cbuf[...]
        cact = cv * _sigmoid(cv)
        cact_ref[...] = cact
        cb = cact.astype(BF16)
        for l in range(nl):
            pbuf[l] = _dot(cb, wm_ref[l].astype(BF16))
        own_p = pltpu.make_async_copy(pbuf.at[:, pl.ds(me8, SUBLANES), :], rbuf.at[me], lsem.at[1])
        own_p.start()
        second = []
        for k in range(1, N_DEV):
            peer, pj = _peer(pos, k)
            pj8 = pl.multiple_of(pj * SUBLANES, SUBLANES)
            second.append(_remote(pbuf.at[:, pl.ds(pj8, SUBLANES), :], rbuf.at[me], ssem.at[2, k - 1], rsem.at[2, k - 1], peer))
        for cp in second:
            cp.start()
        for k in range(1, N_DEV):
            peer, pj = _peer(pos, k)
            _remote(pbuf.at[:, pl.ds(0, SUBLANES), :], rbuf.at[pj], ssem.at[2, k - 1], rsem.at[2, k - 1], peer).wait_recv()
            _remote(cw_ref, cwf_ref.at[:, :, pl.ds(pj * cwid, cwid)], ssem.at[1, k - 1], rsem.at[1, k - 1], peer).wait_recv()
        own_p.wait()
        own_cw.wait()
        for l in range(nl):
            for j in range(N_DEV):
                mod_ref[l:l + 1, j * cm:(j + 1) * cm] = rbuf[j, l, 0:1, :] + bm_ref[l:l + 1, j * cm:(j + 1) * cm]
        for cp in first + second:
            cp.wait_send()

    return pl.pallas_call(
        body,
        name="prep_small",
        out_shape=(
            jax.ShapeDtypeStruct((nl, N_MOD * d), F32),
            jax.ShapeDtypeStruct((N_DEV * SUBLANES, d), F32),
            jax.ShapeDtypeStruct((nl, CW_ROWS, d), F32),
        ),
        in_specs=[VMEM_SPEC] * 4,
        out_specs=(VMEM_SPEC,) * 3,
        scratch_shapes=[
            pltpu.VMEM((N_DEV * SUBLANES, d), F32),
            pltpu.VMEM((nl, N_DEV * SUBLANES, cm), F32),
            pltpu.VMEM((N_DEV, nl, SUBLANES, cm), F32),
            pltpu.SemaphoreType.DMA((3, N_DEV - 1)),
            pltpu.SemaphoreType.DMA((3, N_DEV - 1)),
            pltpu.SemaphoreType.DMA((2,)),
        ],
        compiler_params=_cparams(vmem=3 * _nbytes(w_mod.shape, F32)),
    )(c, w_mod, b_mod, cw)


def _exchange_start(parts, gather, name):
    n = len(parts)
    lands = [lax.empty(((N_DEV,) + tuple(p.shape)) if gather else tuple(p.shape), p.dtype) for p in parts]

    def body(*refs):
        ins, lnd = refs[:n], refs[n:2 * n]
        ssem, rsem, token = refs[2 * n], refs[2 * n + 1], refs[-1]
        pos, me = _my_position()
        for k in range(1, N_DEV):
            peer, pj = _peer(pos, k)
            for t in range(n):
                src = ins[t] if gather else ins[t].at[pj]
                q = t * (N_DEV - 1) + k - 1
                _remote(src, lnd[t].at[me], ssem.at[q], rsem.at[q], peer).start()
        token[...] = jnp.zeros(token.shape, F32)

    out = pl.pallas_call(
        body,
        name=name,
        out_shape=(pltpu.SemaphoreType.DMA((n * (N_DEV - 1),)), pltpu.SemaphoreType.DMA((n * (N_DEV - 1),)))
        + tuple(pltpu.HBM(p.shape, p.dtype) for p in parts) + tuple(pltpu.HBM(p.shape, p.dtype) for p in lands)
        + (jax.ShapeDtypeStruct((SUBLANES, 128), F32),),
        in_specs=[HBM_SPEC] * (2 * n),
        out_specs=(SEM_SPEC, SEM_SPEC) + (HBM_SPEC,) * (2 * n) + (VMEM_SPEC,),
        input_output_aliases={i: 2 + i for i in range(2 * n)},
        compiler_params=pltpu.CompilerParams(has_side_effects=SIDE_EFFECT),
    )(*[pltpu.with_memory_space_constraint(p, pltpu.HBM) for p in list(parts) + lands])
    return out[0], out[1], list(out[2:2 + n]), list(out[2 + n:2 + 2 * n]), out[-1]


def _exchange_wait(started, after, gather, name):
    ssem, rsem, parts, lands, _ = started
    n = len(parts)

    def body(*refs):
        ins, lnd = refs[:n], refs[n:2 * n]
        ssem_ref, rsem_ref = refs[2 * n], refs[2 * n + 1]
        stage, lsem = refs[-1 - n:-1], refs[-1]
        pos, me = _my_position()
        load = []
        for t in range(n):
            src = ins[t] if gather else ins[t].at[me]
            load.append(pltpu.make_async_copy(src, stage[t], lsem.at[t]))
            load[-1].start()
        store = []
        for t in range(n):
            load[t].wait()
            store.append(pltpu.make_async_copy(stage[t], lnd[t].at[me], lsem.at[t]))
            store[-1].start()
        for k in range(1, N_DEV):
            peer, pj = _peer(pos, k)
            for t in range(n):
                src = ins[t] if gather else ins[t].at[pj]
                q = t * (N_DEV - 1) + k - 1
                _remote(src, lnd[t].at[me], ssem_ref.at[q], rsem_ref.at[q], peer).wait_send()
                _remote(src, lnd[t].at[pj], ssem_ref.at[q], rsem_ref.at[q], peer).wait_recv()
        for cp in store:
            cp.wait()

    out = pl.pallas_call(
        body,
        name=name,
        out_shape=tuple(pltpu.HBM(p.shape, p.dtype) for p in parts) + tuple(pltpu.HBM(p.shape, p.dtype) for p in lands),
        in_specs=[HBM_SPEC] * (2 * n) + [SEM_SPEC, SEM_SPEC, ANY_SPEC],
        out_specs=(HBM_SPEC,) * (2 * n),
        input_output_aliases={i: i for i in range(2 * n)},
        scratch_shapes=[pltpu.VMEM(tuple(z.shape[1:]), z.dtype) for z in lands] + [pltpu.SemaphoreType.DMA((n,))],
        compiler_params=pltpu.CompilerParams(has_side_effects=SIDE_EFFECT),
    )(*parts, *lands, ssem, rsem, after)
    return list(out[n:])


CHIP_PEERS = 3


def _chip_peer(pos, k, core):
    x, y, _ = pos
    px = 1 - x if k & 2 else x
    py = 1 - y if k & 1 else y
    return (px, py, core), 4 * px + 2 * py + core


def _gather2_start(parts, after, name):
    n = len(parts)
    lands = [lax.empty((N_DEV,) + tuple(p.shape), p.dtype) for p in parts]
    per = CHIP_PEERS + 1

    def body(*refs):
        ins, lnd = refs[:n], refs[n:2 * n]
        ssem, rsem, token = refs[2 * n + 1], refs[2 * n + 2], refs[-1]
        pos, me = _my_position()
        sibling = (pos[0], pos[1], 1 - pos[2])
        for t in range(n):
            _remote(ins[t], lnd[t].at[me], ssem.at[per * t], rsem.at[per * t], sibling).start()
        for k in range(1, per):
            peer, _ = _chip_peer(pos, k, pos[2])
            for t in range(n):
                _remote(ins[t], lnd[t].at[me], ssem.at[per * t + k], rsem.at[per * t + k], peer).start()
        token[...] = jnp.zeros(token.shape, F32)

    out = pl.pallas_call(
        body,
        name=name,
        out_shape=(pltpu.SemaphoreType.DMA((n * per,)), pltpu.SemaphoreType.DMA((n * per,)))
        + tuple(pltpu.HBM(p.shape, p.dtype) for p in parts) + tuple(pltpu.HBM(p.shape, p.dtype) for p in lands)
        + (jax.ShapeDtypeStruct((SUBLANES, 128), F32),),
        in_specs=[HBM_SPEC] * (2 * n) + [ANY_SPEC],
        out_specs=(SEM_SPEC, SEM_SPEC) + (HBM_SPEC,) * (2 * n) + (VMEM_SPEC,),
        input_output_aliases={i: 2 + i for i in range(2 * n)},
        compiler_params=pltpu.CompilerParams(has_side_effects=SIDE_EFFECT),
    )(*[pltpu.with_memory_space_constraint(p, pltpu.HBM) for p in list(parts) + lands], after)
    return out[0], out[1], list(out[2:2 + n]), list(out[2 + n:2 + 2 * n]), out[-1]


def _gather2_forward(started, after, name):
    ssem, rsem, parts, lands, _ = started
    n = len(lands)
    per = CHIP_PEERS + 1

    def body(*refs):
        lnd = refs[:n]
        rsem_a, fsend, frecv, token = refs[n], refs[2 * n + 2], refs[2 * n + 3], refs[2 * n + 4]
        pos, _ = _my_position()
        sibling = (pos[0], pos[1], 1 - pos[2])
        for k in range(1, per):
            peer, pk = _chip_peer(pos, k, pos[2])
            for t in range(n):
                block = lnd[t].at[pk]
                _remote(block, block, rsem_a.at[per * t + k], rsem_a.at[per * t + k], peer).wait_recv()
                q = CHIP_PEERS * t + k - 1
                _remote(block, block, fsend.at[q], frecv.at[q], sibling).start()
        token[...] = jnp.zeros(token.shape, F32)

    out = pl.pallas_call(
        body,
        name=name,
        out_shape=tuple(pltpu.HBM(p.shape, p.dtype) for p in lands)
        + (pltpu.SemaphoreType.DMA((n * CHIP_PEERS,)), pltpu.SemaphoreType.DMA((n * CHIP_PEERS,)),
           jax.ShapeDtypeStruct((SUBLANES, 128), F32)),
        in_specs=[HBM_SPEC] * n + [SEM_SPEC, ANY_SPEC],
        out_specs=(HBM_SPEC,) * n + (SEM_SPEC, SEM_SPEC, VMEM_SPEC),
        input_output_aliases={i: i for i in range(n)},
        compiler_params=pltpu.CompilerParams(has_side_effects=SIDE_EFFECT),
    )(*lands, rsem, after)
    return ssem, rsem, parts, list(out[:n]), out[n + 2], out[n], out[n + 1]


def _gather2_wait(forwarded, after, name):
    ssem, rsem, parts, lands, _, fsend, frecv = forwarded
    n = len(parts)
    per = CHIP_PEERS + 1

    def body(*refs):
        ins, lnd = refs[:n], refs[n:2 * n]
        ssem_a, rsem_a, fs, fr = refs[2 * n:2 * n + 4]
        stage, lsem = refs[-1 - n:-1], refs[-1]
        pos, me = _my_position()
        sibling = (pos[0], pos[1], 1 - pos[2])
        sib = 4 * pos[0] + 2 * pos[1] + 1 - pos[2]
        load = []
        for t in range(n):
            load.append(pltpu.make_async_copy(ins[t], stage[t], lsem.at[t]))
            load[-1].start()
        store = []
        for t in range(n):
            load[t].wait()
            store.append(pltpu.make_async_copy(stage[t], lnd[t].at[me], lsem.at[t]))
            store[-1].start()
        for t in range(n):
            _remote(ins[t], lnd[t].at[me], ssem_a.at[per * t], rsem_a.at[per * t], sibling).wait_send()
            _remote(ins[t], lnd[t].at[sib], ssem_a.at[per * t], rsem_a.at[per * t], sibling).wait_recv()
        for k in range(1, per):
            peer, pk = _chip_peer(pos, k, pos[2])
            _, qk = _chip_peer(pos, k, 1 - pos[2])
            for t in range(n):
                q = CHIP_PEERS * t + k - 1
                _remote(ins[t], lnd[t].at[me], ssem_a.at[per * t + k], rsem_a.at[per * t + k], peer).wait_send()
                _remote(lnd[t].at[pk], lnd[t].at[pk], fs.at[q], fr.at[q], sibling).wait_send()
                _remote(lnd[t].at[qk], lnd[t].at[qk], fs.at[q], fr.at[q], sibling).wait_recv()
        for cp in store:
            cp.wait()

    out = pl.pallas_call(
        body,
        name=name,
        out_shape=tuple(pltpu.HBM(p.shape, p.dtype) for p in parts) + tuple(pltpu.HBM(p.shape, p.dtype) for p in lands),
        in_specs=[HBM_SPEC] * (2 * n) + [SEM_SPEC] * 4 + [ANY_SPEC],
        out_specs=(HBM_SPEC,) * (2 * n),
        input_output_aliases={i: i for i in range(2 * n)},
        scratch_shapes=[pltpu.VMEM(tuple(z.shape[1:]), z.dtype) for z in lands] + [pltpu.SemaphoreType.DMA((n,))],
        compiler_params=pltpu.CompilerParams(has_side_effects=SIDE_EFFECT),
    )(*parts, *lands, ssem, rsem, fsend, frecv, after)
    return list(out[n:])


def _reduce_small(rows, dm8, cact, after):
    r, d = rows.shape
    nl = dm8.shape[0]
    cm = dm8.shape[2] // N_DEV

    def body(rows_ref, dm_ref, cact_ref, after_ref, orow_ref, owm_ref, gr, dmr, ssem, rsem, lsem):
        pos, me = _my_position()
        me8 = pl.multiple_of(me * SUBLANES, SUBLANES)
        gr[me] = rows_ref[...]
        own_dm = pltpu.make_async_copy(dm_ref.at[:, :, pl.ds(me * cm, cm)], dmr.at[:, pl.ds(me8, SUBLANES), :], lsem.at[0])
        own_dm.start()
        sends = []
        for k in range(1, N_DEV):
            peer, pj = _peer(pos, k)
            sends.append(_remote(gr.at[me], gr.at[me], ssem.at[0, k - 1], rsem.at[0, k - 1], peer))
            sends.append(_remote(dm_ref.at[:, :, pl.ds(pj * cm, cm)], dmr.at[:, pl.ds(me8, SUBLANES), :],
                                 ssem.at[1, k - 1], rsem.at[1, k - 1], peer))
        for cp in sends:
            cp.start()
        for k in range(1, N_DEV):
            peer, pj = _peer(pos, k)
            pj8 = pl.multiple_of(pj * SUBLANES, SUBLANES)
            _remote(gr.at[pj], gr.at[pj], ssem.at[0, k - 1], rsem.at[0, k - 1], peer).wait_recv()
            _remote(dm_ref.at[:, :, pl.ds(0, cm)], dmr.at[:, pl.ds(pj8, SUBLANES), :], ssem.at[1, k - 1], rsem.at[1, k - 1], peer).wait_recv()
        own_dm.wait()
        acc = gr[0]
        for j in range(1, N_DEV):
            acc = acc + gr[j]
        orow_ref[...] = acc
        cb = cact_ref[...].astype(BF16)
        for l in range(nl):
            owm_ref[l] = _dot_tn(cb, dmr[l].astype(BF16))
        for cp in sends:
            cp.wait_send()

    return pl.pallas_call(
        body,
        name="reduce_small",
        out_shape=(jax.ShapeDtypeStruct((r, d), F32), jax.ShapeDtypeStruct((nl, d, cm), F32)),
        in_specs=[VMEM_SPEC] * 3 + [ANY_SPEC],
        out_specs=(VMEM_SPEC,) * 2,
        scratch_shapes=[
            pltpu.VMEM((N_DEV, r, d), F32),
            pltpu.VMEM((nl, N_DEV * SUBLANES, cm), F32),
            pltpu.SemaphoreType.DMA((2, N_DEV - 1)),
            pltpu.SemaphoreType.DMA((2, N_DEV - 1)),
            pltpu.SemaphoreType.DMA((1,)),
        ],
        compiler_params=_cparams(vmem=4 * _nbytes((N_DEV, r, d), F32) + 6 * _nbytes((nl, d, cm), F32)),
    )(rows, dm8, cact, after)


def _sum_gathered(zones):
    nl = len(zones)

    def body(*refs):
        for l in range(nl):
            acc = refs[l][0].astype(F32)
            for j in range(1, N_DEV):
                acc = acc + refs[l][j].astype(F32)
            refs[nl][l] = acc

    return pl.pallas_call(
        body,
        name="sum_gathered",
        out_shape=jax.ShapeDtypeStruct((nl,) + tuple(zones[0].shape[1:]), F32),
        in_specs=[VMEM_SPEC] * nl,
        out_specs=VMEM_SPEC,
        compiler_params=_cparams(vmem=8 * nl * _nbytes(zones[0].shape, BF16)),
    )(*zones)


def _in_proj_fwd(x, mod, rows, win_f, l, tm):
    s, d = x.shape
    nb, _, ci = win_f.shape

    def body(x_ref, mod_ref, rows_ref, w_ref, proj_ref, h_ref):
        xv = x_ref[...]
        g = rows_ref[R_G_PRE_MIX:R_G_PRE_MIX + 1, :]
        h = (xv * _rms(xv) * g) * (1.0 + mod_ref[M_SC_M:M_SC_M + 1, :]) + mod_ref[M_SH_M:M_SH_M + 1, :]
        hb = h.astype(BF16)
        h_ref[...] = hb
        for j in range(nb):
            proj_ref[:, j * ci:(j + 1) * ci] = _dot(hb, w_ref[j])

    return pl.pallas_call(
        body,
        name="in_proj_fwd",
        grid=(s // tm,),
        in_specs=[
            pl.BlockSpec((tm, d), lambda i: (i, 0)),
            _resident((None, N_MOD, d), lambda i: (l, 0, 0)),
            _resident((None, N_ROWS, d), lambda i: (l, 0, 0)),
            _resident((nb, d, ci), lambda i: (0, 0, 0)),
        ],
        out_specs=(pl.BlockSpec((tm, nb * ci), lambda i: (i, 0)), pl.BlockSpec((tm, d), lambda i: (i, 0))),
        out_shape=(jax.ShapeDtypeStruct((s, nb * ci), F32), jax.ShapeDtypeStruct((s, d), BF16)),
        compiler_params=_cparams(("parallel",), _nbytes((nb, d, ci), BF16) + 3 * _nbytes((tm, nb * ci), F32) + 8 * _nbytes((tm, d), F32)),
    )(*_hbm(x), mod, rows, win_f)


def _mixer_core_fwd(proj, x, mod, rows, cwf, wr, wi, wa_f, wb_f, wo_f, l, tm):
    s, d = x.shape
    nh, bw, _ = wr.shape[1:]

    def body(proj_ref, x_ref, mod_ref, rows_ref, cw_ref, wr_ref, wi_ref, wa_ref, wb_ref, wo_ref,
             x1_ref, hs_ref, yap_ref, ybp_ref, y_ref, cvbuf, xbbuf, a_s, b_s, hprev):
        i = pl.program_id(0)

        @pl.when(i == 0)
        def _():
            cvbuf[...] = jnp.zeros((SUBLANES, d), F32)
            xbbuf[...] = jnp.zeros((SUBLANES, d), F32)
            hprev[...] = jnp.zeros((SUBLANES, d), F32)

        def row(r):
            return rows_ref[r:r + 1, :]

        def tap(r):
            return cw_ref[r:r + 1, :]

        ba = proj_ref[:, 0:d]
        cv = proj_ref[:, d:2 * d] * proj_ref[:, 2 * d:3 * d]
        cvt = cvbuf[...]
        conv3 = ((row(R_CONV_A_B) + _shift_rows(cv, cvt, 2, False) * tap(CW_A)) + _shift_rows(cv, cvt, 1, False) * tap(CW_A + 1)) + cv * tap(CW_A + 2)
        ya = ba * conv3
        cvbuf[...] = cv[tm - SUBLANES:, :]
        xb = proj_ref[:, 3 * d:4 * d]
        xbt = xbbuf[...]
        xc = (((row(R_CONV_B_B) + _shift_rows(xb, xbt, 3, False) * tap(CW_B)) + _shift_rows(xb, xbt, 2, False) * tap(CW_B + 1))
              + _shift_rows(xb, xbt, 1, False) * tap(CW_B + 2)) + xb * tap(CW_B + 3)
        xbbuf[...] = xb[tm - SUBLANES:, :]
        sp = _softplus_neg(row(R_LAMBDA))
        _, _, ig, a, mult = _gates(xc, wr_ref, wi_ref, row(R_B_GATE_R), row(R_B_GATE_I), sp, nh, bw)
        a_s[...] = a
        b_s[...] = mult * (ig * xc)

        def blk(j, hp):
            o = pl.multiple_of(j * SUBLANES, SUBLANES)
            a8, b8 = _scan_block(a_s[pl.ds(o, SUBLANES), :], b_s[pl.ds(o, SUBLANES), :], reverse=False)
            h8 = b8 + a8 * hp
            hs_ref[pl.ds(o, SUBLANES), :] = h8
            return jnp.broadcast_to(h8[SUBLANES - 1:SUBLANES, :], (SUBLANES, d))

        hprev[...] = lax.fori_loop(0, tm // SUBLANES, blk, hprev[...])
        gel, _ = _gelu(proj_ref[:, 4 * d:5 * d])
        yb = hs_ref[...] * gel
        yap = _dot(ya.astype(BF16), wa_ref[...])
        ybp = _dot(yb.astype(BF16), wb_ref[...])
        yap_ref[...] = yap
        ybp_ref[...] = ybp
        m = _sigmoid(proj_ref[:, 5 * d:6 * d]) * yap + _sigmoid(proj_ref[:, 6 * d:7 * d]) * ybp
        y = _dot(m.astype(BF16), wo_ref[...])
        y_ref[...] = y
        x1_ref[...] = x_ref[...] + mod_ref[M_GT_M:M_GT_M + 1, :] * ((y * _rms(y)) * row(R_G_POST_MIX))

    tile = pl.BlockSpec((tm, d), lambda i: (i, 0))
    return pl.pallas_call(
        body,
        name="mixer_core_fwd",
        grid=(s // tm,),
        in_specs=[
            pl.BlockSpec((tm, 7 * d), lambda i: (i, 0)),
            tile,
            _resident((None, N_MOD, d), lambda i: (l, 0, 0)),
            _resident((None, N_ROWS, d), lambda i: (l, 0, 0)),
            _resident((None, CW_ROWS, d), lambda i: (l, 0, 0)),
            _resident((None, nh, bw, bw), lambda i: (l, 0, 0, 0)),
            _resident((None, nh, bw, bw), lambda i: (l, 0, 0, 0)),
            _resident((d, d), lambda i: (0, 0)),
            _resident((d, d), lambda i: (0, 0)),
            _resident((d, d), lambda i: (0, 0)),
        ],
        out_specs=(tile,) * 5,
        out_shape=(jax.ShapeDtypeStruct((s, d), F32),) * 5,
        scratch_shapes=[
            pltpu.VMEM((SUBLANES, d), F32),
            pltpu.VMEM((SUBLANES, d), F32),
            pltpu.VMEM((tm, d), F32),
            pltpu.VMEM((tm, d), F32),
            pltpu.VMEM((SUBLANES, d), F32),
        ],
        compiler_params=_cparams(("arbitrary",), 3 * _nbytes((d, d), BF16) + 2 * _nbytes((tm, 7 * d), F32) + 40 * _nbytes((tm, d), F32)),
    )(*_hbm(proj, x), mod, rows, cwf, wr, wi, wa_f, wb_f, wo_f)


def _mlp_fwd(x1, mod, rows, wup_f, wdn_f, l, tm):
    s, d = x1.shape
    nb, _, cu = wup_f.shape
    dff = nb * cu

    def body(x1_ref, mod_ref, rows_ref, wu_ref, wd_ref, x2_ref, ru_ref, y2_ref, h2_ref):
        xv = x1_ref[...]
        g = rows_ref[R_G_PRE_MLP:R_G_PRE_MLP + 1, :]
        h2 = ((xv * _rms(xv) * g) * (1.0 + mod_ref[M_SC_F:M_SC_F + 1, :]) + mod_ref[M_SH_F:M_SH_F + 1, :]).astype(BF16)
        h2_ref[...] = h2
        ru = jnp.concatenate([jnp.maximum(_dot(h2, wu_ref[j]), 0.0) for j in range(nb)], axis=1)
        ru_ref[...] = ru.astype(BF16)
        y2 = _dot((ru * ru).astype(BF16), wd_ref[...])
        y2_ref[...] = y2
        x2_ref[...] = xv + mod_ref[M_GT_F:M_GT_F + 1, :] * ((y2 * _rms(y2)) * rows_ref[R_G_POST_MLP:R_G_POST_MLP + 1, :])

    tile = pl.BlockSpec((tm, d), lambda i: (i, 0))
    wide = pl.BlockSpec((tm, dff), lambda i: (i, 0))
    return pl.pallas_call(
        body,
        name="mlp_fwd",
        grid=(s // tm,),
        in_specs=[
            tile,
            _resident((None, N_MOD, d), lambda i: (l, 0, 0)),
            _resident((None, N_ROWS, d), lambda i: (l, 0, 0)),
            _resident((nb, d, cu), lambda i: (0, 0, 0)),
            _resident((dff, d), lambda i: (0, 0)),
        ],
        out_specs=(tile, wide, tile, tile),
        out_shape=(jax.ShapeDtypeStruct((s, d), F32), jax.ShapeDtypeStruct((s, dff), BF16),
                   jax.ShapeDtypeStruct((s, d), F32), jax.ShapeDtypeStruct((s, d), BF16)),
        compiler_params=_cparams(("parallel",), 2 * _nbytes((dff, d), BF16) + 5 * _nbytes((tm, dff), F32) + 12 * _nbytes((tm, d), F32)),
    )(*_hbm(x1), mod, rows, wup_f, wdn_f)


def _loss_fwd_bwd(y, target, tm):
    s, d = y.shape

    def body(y_ref, t_ref, loss_ref, dy_ref):
        @pl.when(pl.program_id(0) == 0)
        def _():
            loss_ref[...] = jnp.zeros(loss_ref.shape, F32)

        e = y_ref[...] - t_ref[...]
        dy_ref[...] = e * (1.0 / d)
        loss_ref[...] += 0.5 * jnp.sum(jnp.mean(e * e, axis=-1, keepdims=True), axis=0, keepdims=True)

    tile = pl.BlockSpec((tm, d), lambda i: (i, 0))
    loss, dy = pl.pallas_call(
        body,
        name="loss",
        grid=(s // tm,),
        in_specs=[tile, tile],
        out_specs=(pl.BlockSpec((SUBLANES, 128), lambda i: (0, 0)), tile),
        out_shape=(jax.ShapeDtypeStruct((SUBLANES, 128), F32), jax.ShapeDtypeStruct((s, d), F32)),
        compiler_params=_cparams(("arbitrary",)),
    )(*_hbm(y, target))
    return loss[0, 0], dy


def _mlp_bwd(dx2, x1, y2, ru, mod, rows, wup_f, wdn_f, l, tm):
    s, d = x1.shape
    nb, _, cu = wup_f.shape
    dff = nb * cu

    def body(dx2_ref, x1_ref, y2_ref, ru_ref, mod_ref, rows_ref, wu_ref, wd_ref, dx1_ref, dy2_ref, dup_ref, act_ref, sm_ref):
        @pl.when(pl.program_id(0) == 0)
        def _():
            sm_ref[...] = jnp.zeros(sm_ref.shape, F32)

        dout = dx2_ref[...]
        dy2, dgt, dgpost = _postnorm_bwd(y2_ref[...], dout, rows_ref[R_G_POST_MLP:R_G_POST_MLP + 1, :], mod_ref[M_GT_F:M_GT_F + 1, :])
        dy2b = dy2.astype(BF16)
        dy2_ref[...] = dy2b
        ruv = ru_ref[...].astype(F32)
        act_ref[...] = (ruv * ruv).astype(BF16)
        dup = (_dot_nt(dy2b, wd_ref[...]) * (2.0 * ruv)).astype(BF16)
        dup_ref[...] = dup
        dh2 = _dot_nt(dup[:, 0:cu], wu_ref[0])
        for j in range(1, nb):
            dh2 = dh2 + _dot_nt(dup[:, j * cu:(j + 1) * cu], wu_ref[j])
        dxn, dsc, dsh, dgpre = _prenorm_bwd(x1_ref[...], dh2, rows_ref[R_G_PRE_MLP:R_G_PRE_MLP + 1, :], mod_ref[M_SC_F:M_SC_F + 1, :])
        dx1_ref[...] = dout + dxn
        for r, v in ((G_MLP_GT, dgt), (G_MLP_GPOST, dgpost), (G_MLP_SC, dsc), (G_MLP_SH, dsh), (G_MLP_GPRE, dgpre)):
            sm_ref[r:r + 1, :] += v

    tile = pl.BlockSpec((tm, d), lambda i: (i, 0))
    wide = pl.BlockSpec((tm, dff), lambda i: (i, 0))
    return pl.pallas_call(
        body,
        name="mlp_bwd",
        grid=(s // tm,),
        in_specs=[
            tile, tile, tile, wide,
            _resident((None, N_MOD, d), lambda i: (l, 0, 0)),
            _resident((None, N_ROWS, d), lambda i: (l, 0, 0)),
            _resident((nb, d, cu), lambda i: (0, 0, 0)),
            _resident((dff, d), lambda i: (0, 0)),
        ],
        out_specs=(tile, tile, wide, wide, pl.BlockSpec((G_MLP_ROWS, d), lambda i: (0, 0))),
        out_shape=(jax.ShapeDtypeStruct((s, d), F32), jax.ShapeDtypeStruct((s, d), BF16), jax.ShapeDtypeStruct((s, dff), BF16),
                   jax.ShapeDtypeStruct((s, dff), BF16), jax.ShapeDtypeStruct((G_MLP_ROWS, d), F32)),
        compiler_params=_cparams(("arbitrary",), 2 * _nbytes((dff, d), BF16) + 6 * _nbytes((tm, dff), F32) + 16 * _nbytes((tm, d), F32)),
    )(*_hbm(dx2, x1, y2, ru), mod, rows, wup_f, wdn_f)


def _mixer_core_bwd(dx1, y, yap, ybp, hs, proj, mod, rows, cwf, wr, wi, wa_f, wb_f, wo_f, l, tm):
    s, d = dx1.shape
    nh, bw, _ = wr.shape[1:]
    nt = s // tm
    per = tm // SUBLANES

    def body(dx1_ref, y_ref, yap_ref, ybp_ref, hs_ref, hsh_ref, proj_ref, projh_ref, mod_ref, rows_ref, cw_ref,
             wr_ref, wi_ref, wa_ref, wb_ref, wo_ref,
             dproj_ref, dy_ref, m_ref, dyap_ref, dybp_ref, ya_ref, yb_ref, sm_ref, dwg_ref,
             abuf, dcbuf, dxbuf, al_s, dh_s, lam_s, lnext):
        i = pl.program_id(0)
        first_tile = i == nt - 1

        @pl.when(i == 0)
        def _():
            sm_ref[...] = jnp.zeros(sm_ref.shape, F32)
            dwg_ref[...] = jnp.zeros(dwg_ref.shape, F32)
            zero = jnp.zeros((SUBLANES, d), F32)
            abuf[...] = zero
            dcbuf[...] = zero
            dxbuf[...] = zero
            lnext[...] = zero

        def row(r):
            return rows_ref[r:r + 1, :]

        def tap(r):
            return cw_ref[r:r + 1, :]

        def acc(r, v):
            sm_ref[r:r + 1, :] += v

        keep_halo = jnp.where(first_tile, 0.0, 1.0)
        dy, dgt, dgpost = _postnorm_bwd(y_ref[...], dx1_ref[...], row(R_G_POST_MIX), mod_ref[M_GT_M:M_GT_M + 1, :])
        acc(G_MIX_GT, dgt)
        acc(G_MIX_GPOST, dgpost)
        dyb16 = dy.astype(BF16)
        dy_ref[...] = dyb16
        dm = _dot_nt(dyb16, wo_ref[...])
        sa = _sigmoid(proj_ref[:, 5 * d:6 * d])
        sb = _sigmoid(proj_ref[:, 6 * d:7 * d])
        yap = yap_ref[...]
        ybp = ybp_ref[...]
        m_ref[...] = (sa * yap + sb * ybp).astype(BF16)
        dyap = (dm * sa).astype(BF16)
        dybp = (dm * sb).astype(BF16)
        dyap_ref[...] = dyap
        dybp_ref[...] = dybp
        dproj_ref[:, 5 * d:6 * d] = (dm * yap * sa * (1.0 - sa)).astype(BF16)
        dproj_ref[:, 6 * d:7 * d] = (dm * ybp * sb * (1.0 - sb)).astype(BF16)
        dya = _dot_nt(dyap, wa_ref[...])
        dyb = _dot_nt(dybp, wb_ref[...])
        ba = proj_ref[:, 0:d]
        ca = proj_ref[:, d:2 * d]
        va = proj_ref[:, 2 * d:3 * d]
        cv = ca * va
        cvh = keep_halo * (projh_ref[:, d:2 * d] * projh_ref[:, 2 * d:3 * d])
        cvm2 = _shift_rows(cv, cvh, 2, False)
        cvm1 = _shift_rows(cv, cvh, 1, False)
        conv3 = ((row(R_CONV_A_B) + cvm2 * tap(CW_A)) + cvm1 * tap(CW_A + 1)) + cv * tap(CW_A + 2)
        ya_ref[...] = (ba * conv3).astype(BF16)
        dproj_ref[:, 0:d] = (dya * conv3).astype(BF16)
        dc3 = dya * ba
        acc(G_MIX_CAB, _colsum(dc3))
        acc(G_MIX_CAW, _colsum(dc3 * cvm2))
        acc(G_MIX_CAW + 1, _colsum(dc3 * cvm1))
        acc(G_MIX_CAW + 2, _colsum(dc3 * cv))
        dct = dcbuf[...]
        dcv = (dc3 * tap(CW_A + 2) + _shift_rows(dc3, dct, 1, True) * tap(CW_A + 1)) + _shift_rows(dc3, dct, 2, True) * tap(CW_A)
        dcbuf[...] = dc3[:SUBLANES, :]
        dproj_ref[:, d:2 * d] = (dcv * va).astype(BF16)
        dproj_ref[:, 2 * d:3 * d] = (dcv * ca).astype(BF16)
        xb = proj_ref[:, 3 * d:4 * d]
        gb = proj_ref[:, 4 * d:5 * d]
        xbh = keep_halo * projh_ref[:, 3 * d:4 * d]
        xm3 = _shift_rows(xb, xbh, 3, False)
        xm2 = _shift_rows(xb, xbh, 2, False)
        xm1 = _shift_rows(xb, xbh, 1, False)
        xc = (((row(R_CONV_B_B) + xm3 * tap(CW_B)) + xm2 * tap(CW_B + 1)) + xm1 * tap(CW_B + 2)) + xb * tap(CW_B + 3)
        lam = row(R_LAMBDA)
        sp = _softplus_neg(lam)
        xcb, r, ig, a, mult = _gates(xc, wr_ref, wi_ref, row(R_B_GATE_R), row(R_B_GATE_I), sp, nh, bw)
        gel, th = _gelu(gb)
        hs = hs_ref[...]
        yb_ref[...] = (hs * gel).astype(BF16)
        dproj_ref[:, 4 * d:5 * d] = (dyb * hs * _gelu_grad(gb, th)).astype(BF16)
        al_s[...] = _shift_rows(a, abuf[...], 1, True)
        abuf[...] = a[:SUBLANES, :]
        dh_s[...] = dyb * gel

        def blk(j, ln):
            o = pl.multiple_of((per - 1 - j) * SUBLANES, SUBLANES)
            a8, b8 = _scan_block(al_s[pl.ds(o, SUBLANES), :], dh_s[pl.ds(o, SUBLANES), :], reverse=True)
            l8 = b8 + a8 * ln
            lam_s[pl.ds(o, SUBLANES), :] = l8
            return jnp.broadcast_to(l8[0:1, :], (SUBLANES, d))

        lnext[...] = lax.fori_loop(0, per, blk, lnext[...])
        dbb = lam_s[...]
        da = dbb * _shift_rows(hs, keep_halo * hsh_ref[...], 1, False)
        dmult = dbb * (ig * xc)
        dig = dbb * (mult * xc)
        dxc = dbb * (mult * ig)
        dla = da * a - dmult * (a * a) / mult
        acc(G_MIX_LAM, _colsum(dla * (-LRU_C * r)) * (-_sigmoid(-lam)))
        dzr = (dla * (-LRU_C * sp)) * r * (1.0 - r)
        dzi = dig * ig * (1.0 - ig)
        acc(G_MIX_BR, _colsum(dzr))
        acc(G_MIX_BI, _colsum(dzi))
        dzrb = dzr.astype(BF16)
        dzib = dzi.astype(BF16)
        back = []
        for h in range(nh):
            sl = slice(h * bw, (h + 1) * bw)
            back.append(_dot_nt(dzrb[:, sl], wr_ref[h]) + _dot_nt(dzib[:, sl], wi_ref[h]))
            dwg_ref[0, h] += _dot_tn(xcb[:, sl], dzrb[:, sl])
            dwg_ref[1, h] += _dot_tn(xcb[:, sl], dzib[:, sl])
        dxc = dxc + jnp.concatenate(back, axis=1)
        acc(G_MIX_CBB, _colsum(dxc))
        acc(G_MIX_CBW, _colsum(dxc * xm3))
        acc(G_MIX_CBW + 1, _colsum(dxc * xm2))
        acc(G_MIX_CBW + 2, _colsum(dxc * xm1))
        acc(G_MIX_CBW + 3, _colsum(dxc * xb))
        dxt = dxbuf[...]
        dxb = (((dxc * tap(CW_B + 3) + _shift_rows(dxc, dxt, 1, True) * tap(CW_B + 2)) + _shift_rows(dxc, dxt, 2, True) * tap(CW_B + 1))
               + _shift_rows(dxc, dxt, 3, True) * tap(CW_B))
        dxbuf[...] = dxc[:SUBLANES, :]
        dproj_ref[:, 3 * d:4 * d] = dxb.astype(BF16)

    def rev(i):
        return (nt - 1 - i, 0)

    def halo(i):
        return (jnp.maximum((nt - 1 - i) * per - 1, 0), 0)

    tile = pl.BlockSpec((tm, d), rev)
    return pl.pallas_call(
        body,
        name="mixer_core_bwd",
        grid=(nt,),
        in_specs=[
            tile, tile, tile, tile, tile,
            pl.BlockSpec((SUBLANES, d), halo),
            pl.BlockSpec((tm, 7 * d), rev),
            pl.BlockSpec((SUBLANES, 7 * d), halo),
            _resident((None, N_MOD, d), lambda i: (l, 0, 0)),
            _resident((None, N_ROWS, d), lambda i: (l, 0, 0)),
            _resident((None, CW_ROWS, d), lambda i: (l, 0, 0)),
            _resident((None, nh, bw, bw), lambda i: (l, 0, 0, 0)),
            _resident((None, nh, bw, bw), lambda i: (l, 0, 0, 0)),
            _resident((d, d), lambda i: (0, 0)),
            _resident((d, d), lambda i: (0, 0)),
            _resident((d, d), lambda i: (0, 0)),
        ],
        out_specs=(pl.BlockSpec((tm, 7 * d), rev),) + (tile,) * 6 + (
            pl.BlockSpec((G_MIX_ROWS, d), lambda i: (0, 0)), pl.BlockSpec((2, nh, bw, bw), lambda i: (0, 0, 0, 0))),
        out_shape=(jax.ShapeDtypeStruct((s, 7 * d), BF16),) + (jax.ShapeDtypeStruct((s, d), BF16),) * 6 + (
            jax.ShapeDtypeStruct((G_MIX_ROWS, d), F32), jax.ShapeDtypeStruct((2, nh, bw, bw), F32)),
        scratch_shapes=[pltpu.VMEM((SUBLANES, d), F32)] * 3 + [pltpu.VMEM((tm, d), F32)] * 3 + [pltpu.VMEM((SUBLANES, d), F32)],
        compiler_params=_cparams(("arbitrary",), 3 * _nbytes((d, d), BF16) + 3 * _nbytes((tm, 7 * d), F32) + 64 * _nbytes((tm, d), F32)),
    )(*_hbm(dx1, y, yap, ybp, hs, hs, proj, proj), mod, rows, cwf, wr, wi, wa_f, wb_f, wo_f)


def _in_proj_bwd(dproj, x, dx1, mod, rows, win_f, l, tm):
    s, d = x.shape
    nb, _, ci = win_f.shape

    def body(dp_ref, x_ref, dx1_ref, mod_ref, rows_ref, w_ref, dx_ref, sm_ref):
        @pl.when(pl.program_id(0) == 0)
        def _():
            sm_ref[...] = jnp.zeros(sm_ref.shape, F32)

        dh = _dot_nt(dp_ref[:, 0:ci], w_ref[0])
        for j in range(1, nb):
            dh = dh + _dot_nt(dp_ref[:, j * ci:(j + 1) * ci], w_ref[j])
        dxn, dsc, dsh, dg = _prenorm_bwd(x_ref[...], dh, rows_ref[R_G_PRE_MIX:R_G_PRE_MIX + 1, :], mod_ref[M_SC_M:M_SC_M + 1, :])
        dx_ref[...] = dx1_ref[...] + dxn
        for r, v in ((G_IN_SC, dsc), (G_IN_SH, dsh), (G_IN_GPRE, dg)):
            sm_ref[r:r + 1, :] += v

    tile = pl.BlockSpec((tm, d), lambda i: (i, 0))
    return pl.pallas_call(
        body,
        name="in_proj_bwd",
        grid=(s // tm,),
        in_specs=[
            pl.BlockSpec((tm, nb * ci), lambda i: (i, 0)), tile, tile,
            _resident((None, N_MOD, d), lambda i: (l, 0, 0)),
            _resident((None, N_ROWS, d), lambda i: (l, 0, 0)),
            _resident((nb, d, ci), lambda i: (0, 0, 0)),
        ],
        out_specs=(tile, pl.BlockSpec((G_IN_ROWS, d), lambda i: (0, 0))),
        out_shape=(jax.ShapeDtypeStruct((s, d), F32), jax.ShapeDtypeStruct((G_IN_ROWS, d), F32)),
        compiler_params=_cparams(("arbitrary",), _nbytes((nb, d, ci), BF16) + 2 * _nbytes((tm, nb * ci), BF16) + 16 * _nbytes((tm, d), F32)),
    )(*_hbm(dproj, x, dx1), mod, rows, win_f)


def _wgrad(a, b, cols_owned, ts, after):
    s, k1 = a.shape
    k2 = b.shape[1]
    ns = s // ts
    if cols_owned:
        nblk, bk1, bk2 = N_DEV, k1, k2 // N_DEV
        a_spec = pl.BlockSpec((ts, bk1), lambda j, t: (t, 0))
        b_spec = pl.BlockSpec((ts, bk2), lambda j, t: (t, j))
    else:
        bk1, bk2 = min(WGRAD_ROWS, k1), k2
        nblk = k1 // bk1
        a_spec = pl.BlockSpec((ts, bk1), lambda j, t: (t, j))
        b_spec = pl.BlockSpec((ts, bk2), lambda j, t: (t, 0))

    def body(a_ref, b_ref, after_ref, o_ref, acc_ref):
        t = pl.program_id(1)

        @pl.when(t == 0)
        def _():
            acc_ref[...] = jnp.zeros(acc_ref.shape, F32)

        acc_ref[...] += _dot_tn(a_ref[...], b_ref[...])

        @pl.when(t == ns - 1)
        def _():
            o_ref[...] = acc_ref[...].astype(BF16)

    out = pl.pallas_call(
        body,
        name="wgrad",
        grid=(nblk, ns),
        in_specs=[a_spec, b_spec, ANY_SPEC],
        out_specs=pl.BlockSpec((None, bk1, bk2), lambda j, t: (j, 0, 0)),
        out_shape=pltpu.HBM((nblk, bk1, bk2), BF16),
        scratch_shapes=[pltpu.VMEM((bk1, bk2), F32)],
        compiler_params=_cparams(("parallel", "arbitrary"), 4 * _nbytes((bk1, bk2), F32) + 4 * _nbytes((ts, bk1 + bk2), BF16)),
    )(pltpu.with_memory_space_constraint(a, pltpu.HBM), pltpu.with_memory_space_constraint(b, pltpu.HBM), after)
    return out if cols_owned else out.reshape(N_DEV, k1 // N_DEV, k2)


def _adam_update(w, g, m, v):
    m = ADAM_B1 * m + (1.0 - ADAM_B1) * g
    v = ADAM_B2 * v + (1.0 - ADAM_B2) * (g * g)
    m_hat = m / (1.0 - ADAM_B1 ** ADAM_STEP)
    v_hat = v / (1.0 - ADAM_B2 ** ADAM_STEP)
    delta = -ADAM_LR * (m_hat / (jnp.sqrt(v_hat) + ADAM_EPS) + ADAM_WD * w)
    return delta, m, v


def _sum_adamw(recv, w, m, v, tr, after):
    nl, ra, cb = w.shape
    assert nl == len(recv) == 2

    def body(r0_ref, r1_ref, w_ref, m_ref, v_ref, after_ref, g_ref, d_ref, nm_ref, nv_ref):
        def total(r_ref):
            g = r_ref[0].astype(F32)
            for j in range(1, N_DEV):
                g = g + r_ref[j].astype(F32)
            return g

        g = jnp.where(pl.program_id(0) == 0, total(r0_ref), total(r1_ref))
        g_ref[...] = g
        d_ref[...], nm_ref[...], nv_ref[...] = _adam_update(w_ref[...], g, m_ref[...], v_ref[...])

    blk = pl.BlockSpec((None, tr, cb), lambda l, i: (l, i, 0))
    return pl.pallas_call(
        body,
        name="sum_adamw",
        grid=(nl, ra // tr),
        in_specs=[pl.BlockSpec((N_DEV, tr, cb), lambda l, i: (0, i * (1 - l), 0)),
                  pl.BlockSpec((N_DEV, tr, cb), lambda l, i: (0, i * l, 0)), blk, blk, blk, ANY_SPEC],
        out_specs=(blk,) * 4,
        out_shape=(jax.ShapeDtypeStruct((nl, ra, cb), F32),) * 4,
        compiler_params=_cparams(("arbitrary", "arbitrary"), 6 * _nbytes((N_DEV, tr, cb), BF16) + 32 * _nbytes((tr, cb), F32)),
    )(recv[0], recv[1], w, m, v, after)


def _adamw(w, g, m, v):
    def body(w_ref, g_ref, m_ref, v_ref, d_ref, nm_ref, nv_ref):
        d_ref[...], nm_ref[...], nv_ref[...] = _adam_update(w_ref[...], g_ref[...], m_ref[...], v_ref[...])

    return pl.pallas_call(
        body,
        name="adamw",
        in_specs=[VMEM_SPEC] * 4,
        out_specs=(VMEM_SPEC,) * 3,
        out_shape=(jax.ShapeDtypeStruct(w.shape, F32),) * 3,
        compiler_params=_cparams(vmem=10 * _nbytes(w.shape, F32)),
    )(w, g, m, v)


def _adamw_tiled(w, g, m, v, tr):
    nl, ra, cb = w.shape

    def body(w_ref, g_ref, m_ref, v_ref, d_ref, nm_ref, nv_ref):
        d_ref[...], nm_ref[...], nv_ref[...] = _adam_update(w_ref[...], g_ref[...], m_ref[...], v_ref[...])

    blk = pl.BlockSpec((None, tr, cb), lambda l, i: (l, i, 0))
    return pl.pallas_call(
        body,
        name="adamw_tiled",
        grid=(nl, ra // tr),
        in_specs=[blk] * 4,
        out_specs=(blk,) * 3,
        out_shape=(jax.ShapeDtypeStruct((nl, ra, cb), F32),) * 3,
        compiler_params=_cparams(("parallel", "parallel")),
    )(*_hbm(w, g, m, v))


def _token_tile(s):
    return min(256, max(SUBLANES * 2, s // 4))


def kernel(x, c, w_mod, b_mod, g_pre_mix, g_post_mix, w_in, conv_a_w, conv_a_b, w_a_out, conv_b_w, conv_b_b, w_gate_r, b_gate_r, w_gate_i, b_gate_i, lru_lambda, w_b_out, w_o, g_pre_mlp, g_post_mlp, w_mlp_up, w_mlp_down, loss_target, m_w_mod, m_b_mod, m_g_pre_mix, m_g_post_mix, m_w_in, m_conv_a_w, m_conv_a_b, m_w_a_out, m_conv_b_w, m_conv_b_b, m_w_gate_r, m_b_gate_r, m_w_gate_i, m_b_gate_i, m_lru_lambda, m_w_b_out, m_w_o, m_g_pre_mlp, m_g_post_mlp, m_w_mlp_up, m_w_mlp_down, v_w_mod, v_b_mod, v_g_pre_mix, v_g_post_mix, v_w_in, v_conv_a_w, v_conv_a_b, v_w_a_out, v_conv_b_w, v_conv_b_b, v_w_gate_r, v_b_gate_r, v_w_gate_i, v_b_gate_i, v_lru_lambda, v_w_b_out, v_w_o, v_g_pre_mlp, v_g_post_mlp, v_w_mlp_up, v_w_mlp_down):
    nl = w_mod.shape[0]
    s, d = x.shape[1], x.shape[2]
    nh, bw = w_gate_r.shape[1], w_gate_r.shape[2]
    cwid = conv_a_w.shape[2]
    tm = _token_tile(s)
    ts = s
    _, me = _my_position()
    xs = x.reshape(s, d)
    target = loss_target.reshape(s, d)

    vec_names = (g_pre_mix, g_post_mix, conv_a_b, conv_b_b, b_gate_r, b_gate_i, lru_lambda, g_pre_mlp, g_post_mlp)
    rows = jnp.concatenate([jnp.stack(vec_names, axis=1), jnp.zeros((nl, N_ROWS - len(vec_names), d), F32)], axis=1)
    cw = jnp.concatenate([conv_a_w, conv_b_w, jnp.zeros((nl, CW_ROWS - 7, cwid), F32)], axis=1)

    w16 = {"w_in": w_in.astype(BF16), "w_a_out": w_a_out.astype(BF16), "w_b_out": w_b_out.astype(BF16), "w_o": w_o.astype(BF16),
           "w_mlp_up": w_mlp_up.astype(BF16), "w_mlp_down": w_mlp_down.astype(BF16)}
    groups = (("in", ("w_in",)), ("mix", ("w_a_out", "w_b_out", "w_o")), ("mlp", ("w_mlp_up", "w_mlp_down")))
    mod, cact, cwf = _prep_small(c, w_mod, b_mod, cw)
    mod = mod.reshape(nl, N_MOD, d)
    gathers = {}
    token = jnp.zeros((), F32)
    after = mod
    for l in range(nl):
        for gname, members in groups:
            gathers[l, gname] = _gather2_start([w16[n][l] for n in members], after, f"gather_start_{gname}{l}")
            after = gathers[l, gname][4]
            token = token + after[0, 0]
    rows = rows + token
    wr = w_gate_r.astype(BF16)
    wi = w_gate_i.astype(BF16)

    forwarded = {}

    def forward(l, gname, after, rows):
        forwarded[l, gname] = _gather2_forward(gathers[l, gname], after, f"gather_forward_{gname}{l}")
        return rows + forwarded[l, gname][4][0, 0]

    def gathered(l, gname, after):
        return _gather2_wait(forwarded[l, gname], after, f"gather_wait_{gname}{l}")

    saved = []
    weights = []
    xin = xs
    for l in range(nl):
        if l == 0:
            rows = forward(0, "in", mod, rows)
        (win_f,) = gathered(l, "in", mod if l == 0 else xin)
        if l > 0:
            rows = forward(l, "mix", win_f, rows)
        proj, h = _in_proj_fwd(xin, mod, rows, win_f, l, tm)
        if l == 0:
            rows = forward(0, "mix", proj, rows)
        wa_f, wb_f, wo_f = (w.reshape(d, d) for w in gathered(l, "mix", proj))
        rows = forward(l, "mlp", wo_f, rows)
        x1, hs, yap, ybp, y = _mixer_core_fwd(proj, xin, mod, rows, cwf, wr, wi, wa_f, wb_f, wo_f, l, tm)
        wup_f, wdn_f = gathered(l, "mlp", x1)
        wdn_f = wdn_f.reshape(-1, d)
        if l + 1 < nl:
            rows = forward(l + 1, "in", wdn_f, rows)
        x2, ru, y2, h2 = _mlp_fwd(x1, mod, rows, wup_f, wdn_f, l, tm)
        saved.append((xin, proj, h, x1, hs, yap, ybp, y, ru, y2, h2))
        weights.append((win_f, wa_f, wb_f, wo_f, wup_f, wdn_f))
        xin = x2
    loss_part, dx = _loss_fwd_bwd(xin, target, tm)

    scatters = {}
    small = [None] * nl
    gate_parts = [None] * nl

    def scatter(l, gname, parts, rows):
        scatters[l, gname] = _exchange_start(parts, False, f"scatter_start_{gname}{l}")
        return rows + scatters[l, gname][4][0, 0]

    for l in reversed(range(nl)):
        xin, proj, h, x1, hs, yap, ybp, y, ru, y2, h2 = saved[l]
        win_f, wa_f, wb_f, wo_f, wup_f, wdn_f = weights[l]
        dx1, dy2, dup, act, sm_mlp = _mlp_bwd(dx, x1, y2, ru, mod, rows, wup_f, wdn_f, l, tm)
        g_up = _wgrad(h2, dup, True, ts, dx1)
        g_dn = _wgrad(act, dy2, False, ts, g_up)
        rows = scatter(l, "mlp", [g_up, g_dn], rows)
        dproj, dy, m, dyap, dybp, ya, yb, sm_mix, dwg = _mixer_core_bwd(dx1, y, yap, ybp, hs, proj, mod, rows, cwf, wr, wi, wa_f, wb_f, wo_f, l, tm // 2)
        gate_parts[l] = _exchange_start([dwg.astype(BF16)], True, f"gates_start{l}")
        rows = rows + gate_parts[l][4][0, 0]
        g_a = _wgrad(ya, dyap, False, ts, gate_parts[l][4])
        g_b = _wgrad(yb, dybp, False, ts, g_a)
        g_o = _wgrad(m, dy, False, ts, g_b)
        rows = scatter(l, "mix", [g_a, g_b, g_o], rows)
        rows = scatter(l, "in", [_wgrad(h, dproj, True, ts, scatters[l, "mix"][4])], rows)
        dx, sm_in = _in_proj_bwd(dproj, xin, dx1, mod, rows, win_f, l, tm)
        small[l] = jnp.concatenate([sm_mlp, sm_mix, sm_in], axis=0)
    grad_x = dx.reshape(x.shape)

    recv = {}
    big = {}
    moments = {"w_in": (w_in, m_w_in, v_w_in), "w_mlp_up": (w_mlp_up, m_w_mlp_up, v_w_mlp_up), "w_a_out": (w_a_out, m_w_a_out, v_w_a_out),
               "w_b_out": (w_b_out, m_w_b_out, v_w_b_out), "w_o": (w_o, m_w_o, v_w_o), "w_mlp_down": (w_mlp_down, m_w_mlp_down, v_w_mlp_down)}

    def collect(l, gname, after):
        for n, zone in zip(dict(groups)[gname], _exchange_wait(scatters[l, gname], after, False, f"scatter_wait_{gname}{l}")):
            recv[n, l] = zone

    def update(name, after):
        w, m_, v_ = moments[name]
        big[name] = _sum_adamw([recv[name, l] for l in range(nl)], w, m_, v_, min(256, w.shape[1]), after)
        return big[name][1]

    for l, gname in ((1, "mlp"), (1, "in"), (1, "mix"), (0, "mlp")):
        collect(l, gname, dx)
    early = update("w_mlp_down", update("w_mlp_up", dx))

    lrows = jnp.concatenate(small, axis=0)
    lrows = lrows.at[G_LOSS_ROW, 0].set(loss_part)

    def lrow(a, l, r):
        return a[l * G_LAYER_ROWS + r]

    dm = jnp.stack([jnp.concatenate([lrow(lrows, l, G_MLP_ROWS + G_MIX_ROWS + G_IN_SH), lrow(lrows, l, G_MLP_ROWS + G_MIX_ROWS + G_IN_SC),
                                     lrow(lrows, l, G_MLP_ROWS + G_MIX_GT), lrow(lrows, l, G_MLP_SH), lrow(lrows, l, G_MLP_SC),
                                     lrow(lrows, l, G_MLP_GT)]) for l in range(nl)])
    dm8 = jnp.concatenate([dm[:, None, :], jnp.zeros((nl, SUBLANES - 1, N_MOD * d), F32)], axis=1)
    srows, g_w_mod = _reduce_small(lrows, dm8, cact, early)
    loss = srows[G_LOSS_ROW, 0]
    sgates = _sum_gathered([_exchange_wait(gate_parts[l], srows, True, f"gates_wait{l}")[0] for l in range(nl)])
    for gname in ("in", "mix"):
        collect(0, gname, srows)
    for name in ("w_in", "w_a_out", "w_b_out", "w_o"):
        update(name, srows)

    def srow(l, r):
        return lrow(srows, l, r)

    def per_layer(r):
        return jnp.stack([srow(l, r) for l in range(nl)])

    mix0 = G_MLP_ROWS
    in0 = G_MLP_ROWS + G_MIX_ROWS
    g_b_mod = jnp.stack([jnp.concatenate([srow(l, in0 + G_IN_SH), srow(l, in0 + G_IN_SC), srow(l, mix0 + G_MIX_GT),
                                          srow(l, G_MLP_SH), srow(l, G_MLP_SC), srow(l, G_MLP_GT)]) for l in range(nl)])
    conv_a_full = jnp.stack([jnp.stack([srow(l, mix0 + G_MIX_CAW + k) for k in range(3)]) for l in range(nl)])
    conv_b_full = jnp.stack([jnp.stack([srow(l, mix0 + G_MIX_CBW + k) for k in range(4)]) for l in range(nl)])
    grads = {
        "b_mod": g_b_mod,
        "g_pre_mix": per_layer(in0 + G_IN_GPRE),
        "g_post_mix": per_layer(mix0 + G_MIX_GPOST),
        "conv_a_w": lax.dynamic_slice_in_dim(conv_a_full, me * cwid, cwid, axis=2),
        "conv_a_b": per_layer(mix0 + G_MIX_CAB),
        "conv_b_w": lax.dynamic_slice_in_dim(conv_b_full, me * cwid, cwid, axis=2),
        "conv_b_b": per_layer(mix0 + G_MIX_CBB),
        "w_gate_r": sgates[:, 0],
        "b_gate_r": per_layer(mix0 + G_MIX_BR),
        "w_gate_i": sgates[:, 1],
        "b_gate_i": per_layer(mix0 + G_MIX_BI),
        "lru_lambda": per_layer(mix0 + G_MIX_LAM),
        "g_pre_mlp": per_layer(G_MLP_GPRE),
        "g_post_mlp": per_layer(G_MLP_GPOST),
    }
    params = {
        "b_mod": (b_mod, m_b_mod, v_b_mod), "g_pre_mix": (g_pre_mix, m_g_pre_mix, v_g_pre_mix), "g_post_mix": (g_post_mix, m_g_post_mix, v_g_post_mix),
        "conv_a_w": (conv_a_w, m_conv_a_w, v_conv_a_w), "conv_a_b": (conv_a_b, m_conv_a_b, v_conv_a_b),
        "conv_b_w": (conv_b_w, m_conv_b_w, v_conv_b_w), "conv_b_b": (conv_b_b, m_conv_b_b, v_conv_b_b),
        "w_gate_r": (w_gate_r, m_w_gate_r, v_w_gate_r), "b_gate_r": (b_gate_r, m_b_gate_r, v_b_gate_r),
        "w_gate_i": (w_gate_i, m_w_gate_i, v_w_gate_i), "b_gate_i": (b_gate_i, m_b_gate_i, v_b_gate_i),
        "lru_lambda": (lru_lambda, m_lru_lambda, v_lru_lambda), "g_pre_mlp": (g_pre_mlp, m_g_pre_mlp, v_g_pre_mlp),
        "g_post_mlp": (g_post_mlp, m_g_post_mlp, v_g_post_mlp),
    }
    out = {}
    for name, g in grads.items():
        w, m_, v_ = params[name]
        flat = (-1, w.shape[-1])
        dl, nm, nv = _adamw(w.reshape(flat), g.reshape(flat), m_.reshape(flat), v_.reshape(flat))
        out[name] = (g.reshape(w.shape), dl.reshape(w.shape), nm.reshape(w.shape), nv.reshape(w.shape))
    out["w_mod"] = (g_w_mod,) + tuple(_adamw_tiled(w_mod, g_w_mod, m_w_mod, v_w_mod, min(128, d)))
    out.update(big)

    order = ("w_mod", "b_mod", "g_pre_mix", "g_post_mix", "w_in", "conv_a_w", "conv_a_b", "w_a_out", "conv_b_w", "conv_b_b", "w_gate_r", "b_gate_r",
             "w_gate_i", "b_gate_i", "lru_lambda", "w_b_out", "w_o", "g_pre_mlp", "g_post_mlp", "w_mlp_up", "w_mlp_down")
    return (loss, grad_x) + tuple(out[n][0] for n in order) + tuple(out[n][1] for n in order) + tuple(out[n][2] for n in order) + tuple(out[n][3] for n in order)
```

```python
import functools

import jax
import jax.numpy as jnp
from jax import lax
from jax.experimental import pallas as pl
from jax.experimental.pallas import tpu as pltpu

F32, BF16 = jnp.float32, jnp.bfloat16
EPS = 1e-6
LRU_C = 8.0
N_DEV = 8
N_MOD = 6
SUBLANES = 8
VMEM_BUDGET = 56 * 1024 * 1024
WGRAD_ROWS = 512
ADAM_LR, ADAM_B1, ADAM_B2, ADAM_EPS, ADAM_WD, ADAM_STEP = 0.001, 0.9, 0.999, 1e-08, 0.01, 10
MESH = pl.DeviceIdType.MESH
VMEM_SPEC = pl.BlockSpec(memory_space=pltpu.VMEM)
ANY_SPEC = pl.BlockSpec(memory_space=pl.ANY)
HBM_SPEC = pl.BlockSpec(memory_space=pltpu.HBM)
SEM_SPEC = pl.BlockSpec(memory_space=pltpu.SEMAPHORE)
SIDE_EFFECT = pltpu.SideEffectType.DATAFLOW_SIDE_EFFECTING

R_G_PRE_MIX, R_G_POST_MIX, R_CONV_A_B, R_CONV_B_B, R_B_GATE_R, R_B_GATE_I, R_LAMBDA, R_G_PRE_MLP, R_G_POST_MLP = range(9)
N_ROWS = 16
M_SH_M, M_SC_M, M_GT_M, M_SH_F, M_SC_F, M_GT_F = range(6)
CW_A, CW_B, CW_ROWS = 0, 3, 8
G_MLP_GT, G_MLP_GPOST, G_MLP_SC, G_MLP_SH, G_MLP_GPRE, G_LOSS_ROW, G_MLP_ROWS = 0, 1, 2, 3, 4, 7, 8
(G_MIX_GT, G_MIX_GPOST, G_MIX_CAB, G_MIX_CAW, G_MIX_CBB, G_MIX_CBW, G_MIX_BR, G_MIX_BI, G_MIX_LAM) = 0, 1, 2, 3, 6, 7, 11, 12, 13
G_MIX_ROWS = 16
G_IN_SC, G_IN_SH, G_IN_GPRE, G_IN_ROWS = 0, 1, 2, 8
G_LAYER_ROWS = G_MLP_ROWS + G_MIX_ROWS + G_IN_ROWS


def _cparams(dims=None, vmem=None):
    kw = {}
    if dims is not None:
        kw["dimension_semantics"] = dims
    if vmem is not None:
        kw["vmem_limit_bytes"] = int(min(max(vmem, 16 * 1024 * 1024), VMEM_BUDGET))
    return pltpu.CompilerParams(**kw)


def _nbytes(shape, dtype):
    n = 1
    for s in shape:
        n *= s
    return n * jnp.dtype(dtype).itemsize


def _hbm(*arrays):
    return tuple(pltpu.with_memory_space_constraint(a, pltpu.HBM) for a in arrays)


def _resident(block, index_map):
    return pl.BlockSpec(block, index_map, pipeline_mode=pl.Buffered(1))


def _my_position():
    x, y, c = lax.axis_index("x"), lax.axis_index("y"), lax.axis_index("c")
    return (x, y, c), 4 * x + 2 * y + c


def _peer(pos, k):
    x, y, c = pos
    px = 1 - x if k & 4 else x
    py = 1 - y if k & 2 else y
    pc = 1 - c if k & 1 else c
    return (px, py, pc), 4 * px + 2 * py + pc


def _remote(src, dst, ssem, rsem, peer):
    return pltpu.make_async_remote_copy(src_ref=src, dst_ref=dst, send_sem=ssem, recv_sem=rsem, device_id=peer, device_id_type=MESH)


def _dot(a, b):
    return jnp.dot(a, b, preferred_element_type=F32)


def _dot_nt(a, b):
    return lax.dot_general(a, b, (((1,), (1,)), ((), ())), preferred_element_type=F32)


def _dot_tn(a, b):
    return lax.dot_general(a, b, (((0,), (0,)), ((), ())), preferred_element_type=F32)


def _colsum(v):
    return jnp.sum(v, axis=0, keepdims=True)


def _sigmoid(v):
    return jax.nn.sigmoid(v)


def _gelu(v):
    k = 0.7978845608028654
    t = jnp.tanh(k * (v + 0.044715 * (v * v * v)))
    return 0.5 * v * (1.0 + t), t


def _gelu_grad(v, t):
    k = 0.7978845608028654
    return 0.5 * (1.0 + t) + 0.5 * v * (1.0 - t * t) * (k * (1.0 + 3.0 * 0.044715 * v * v))


def _expm1(v):
    u = jnp.exp(v)
    um1 = u - 1.0
    q = um1 * v / jnp.log(u)
    return jnp.where(um1 == 0.0, v, jnp.where(um1 == -1.0, -1.0, q))


def _softplus_neg(lam):
    z = -lam
    u = jnp.exp(-jnp.abs(z))
    w = 1.0 + u
    l1p = jnp.where(w == 1.0, u, jnp.log(w) * u / (w - 1.0))
    return jnp.maximum(z, 0.0) + l1p


def _rms(v):
    return lax.rsqrt(jnp.mean(v * v, axis=-1, keepdims=True) + EPS)


def _prenorm_bwd(xv, dh, g, sc):
    r = _rms(xv)
    xn = xv * r
    n = xn * g
    dsc = _colsum(dh * n)
    dsh = _colsum(dh)
    dn = dh * (1.0 + sc)
    dg = _colsum(dn * xn)
    dxn = dn * g
    dx = r * (dxn - xn * jnp.mean(dxn * xn, axis=-1, keepdims=True))
    return dx, dsc, dsh, dg


def _postnorm_bwd(yv, dout, g, gt):
    r = _rms(yv)
    yn = yv * r
    dgt = _colsum(dout * (yn * g))
    dn = dout * gt
    dg = _colsum(dn * yn)
    dyn = dn * g
    dy = r * (dyn - yn * jnp.mean(dyn * yn, axis=-1, keepdims=True))
    return dy, dgt, dg


def _gates(xc, wr_ref, wi_ref, b_r, b_i, sp, nh, bw):
    xcb = xc.astype(BF16)
    zr = jnp.concatenate([_dot(xcb[:, h * bw:(h + 1) * bw], wr_ref[h]) for h in range(nh)], axis=1) + b_r
    zi = jnp.concatenate([_dot(xcb[:, h * bw:(h + 1) * bw], wi_ref[h]) for h in range(nh)], axis=1) + b_i
    r = _sigmoid(zr)
    ig = _sigmoid(zi)
    la = (-LRU_C * r) * sp
    a = jnp.exp(la)
    mult = jnp.sqrt(-_expm1(2.0 * la))
    return xcb, r, ig, a, mult


def _shift_rows(cur, edge, k, up):
    t, dd = cur.shape
    blocks = cur.reshape(t // SUBLANES, SUBLANES, dd)
    row = lax.broadcasted_iota(jnp.int32, (1, SUBLANES, dd), 1)
    if up:
        r = pltpu.roll(blocks, SUBLANES - k, 1)
        nxt = jnp.concatenate([r[1:], pltpu.roll(edge, SUBLANES - k, 0)[None]], axis=0)
        out = jnp.where(row >= SUBLANES - k, nxt, r)
    else:
        r = pltpu.roll(blocks, k, 1)
        prv = jnp.concatenate([pltpu.roll(edge, k, 0)[None], r[:-1]], axis=0)
        out = jnp.where(row < k, prv, r)
    return out.reshape(t, dd)


def _scan_block(a8, b8, reverse):
    row = lax.broadcasted_iota(jnp.int32, a8.shape, 0)
    for s in (1, 2, 4):
        if reverse:
            keep = row < SUBLANES - s
            a_sh = pltpu.roll(a8, SUBLANES - s, 0)
            b_sh = pltpu.roll(b8, SUBLANES - s, 0)
        else:
            keep = row >= s
            a_sh = pltpu.roll(a8, s, 0)
            b_sh = pltpu.roll(b8, s, 0)
        b8 = b8 + a8 * jnp.where(keep, b_sh, 0.0)
        a8 = a8 * jnp.where(keep, a_sh, 1.0)
    return a8, b8


def _prep_small(c, w_mod, b_mod, cw):
    d = c.shape[1]
    cm = w_mod.shape[2]
    cwid = cw.shape[2]
    nl = w_mod.shape[0]

    def body(c_ref, wm_ref, bm_ref, cw_ref, mod_ref, cact_ref, cwf_ref, cbuf, pbuf, rbuf, ssem, rsem, lsem):
        pos, me = _my_position()
        me8 = pl.multiple_of(me * SUBLANES, SUBLANES)
        cbuf[pl.ds(me8, SUBLANES), :] = jnp.broadcast_to(c_ref[...], (SUBLANES, d))
        own_cw = pltpu.make_async_copy(cw_ref, cwf_ref.at[:, :, pl.ds(me * cwid, cwid)], lsem.at[0])
        own_cw.start()
        first = []
        for k in range(1, N_DEV):
            peer, _ = _peer(pos, k)
            rows = cbuf.at[pl.ds(me8, SUBLANES), :]
            first.append(_remote(rows, rows, ssem.at[0, k - 1], rsem.at[0, k - 1], peer))
            first.append(_remote(cw_ref, cwf_ref.at[:, :, pl.ds(me * cwid, cwid)], ssem.at[1, k - 1], rsem.at[1, k - 1], peer))
        for cp in first:
            cp.start()
        for k in range(1, N_DEV):
            peer, pj = _peer(pos, k)
            pj8 = pl.multiple_of(pj * SUBLANES, SUBLANES)
            rows = cbuf.at[pl.ds(pj8, SUBLANES), :]
            _remote(rows, rows, ssem.at[0, k - 1], rsem.at[0, k - 1], peer).wait_recv()
        cv = cbuf[...]
        cact = cv * _sigmoid(cv)
        cact_ref[...] = cact
        cb = cact.astype(BF16)
        for l in range(nl):
            pbuf[l] = _dot(cb, wm_ref[l].astype(BF16))
        own_p = pltpu.make_async_copy(pbuf.at[:, pl.ds(me8, SUBLANES), :], rbuf.at[me], lsem.at[1])
        own_p.start()
        second = []
        for k in range(1, N_DEV):
            peer, pj = _peer(pos, k)
            pj8 = pl.multiple_of(pj * SUBLANES, SUBLANES)
            second.append(_remote(pbuf.at[:, pl.ds(pj8, SUBLANES), :], rbuf.at[me], ssem.at[2, k - 1], rsem.at[2, k - 1], peer))
        for cp in second:
            cp.start()
        for k in range(1, N_DEV):
            peer, pj = _peer(pos, k)
            _remote(pbuf.at[:, pl.ds(0, SUBLANES), :], rbuf.at[pj], ssem.at[2, k - 1], rsem.at[2, k - 1], peer).wait_recv()
            _remote(cw_ref, cwf_ref.at[:, :, pl.ds(pj * cwid, cwid)], ssem.at[1, k - 1], rsem.at[1, k - 1], peer).wait_recv()
        own_p.wait()
        own_cw.wait()
        for l in range(nl):
            for j in range(N_DEV):
                mod_ref[l:l + 1, j * cm:(j + 1) * cm] = rbuf[j, l, 0:1, :] + bm_ref[l:l + 1, j * cm:(j + 1) * cm]
        for cp in first + second:
            cp.wait_send()

    return pl.pallas_call(
        body,
        name="prep_small",
        out_shape=(
            jax.ShapeDtypeStruct((nl, N_MOD * d), F32),
            jax.ShapeDtypeStruct((N_DEV * SUBLANES, d), F32),
            jax.ShapeDtypeStruct((nl, CW_ROWS, d), F32),
        ),
        in_specs=[VMEM_SPEC] * 4,
        out_specs=(VMEM_SPEC,) * 3,
        scratch_shapes=[
            pltpu.VMEM((N_DEV * SUBLANES, d), F32),
            pltpu.VMEM((nl, N_DEV * SUBLANES, cm), F32),
            pltpu.VMEM((N_DEV, nl, SUBLANES, cm), F32),
            pltpu.SemaphoreType.DMA((3, N_DEV - 1)),
            pltpu.SemaphoreType.DMA((3, N_DEV - 1)),
            pltpu.SemaphoreType.DMA((2,)),
        ],
        compiler_params=_cparams(vmem=3 * _nbytes(w_mod.shape, F32)),
    )(c, w_mod, b_mod, cw)


def _exchange_start(parts, gather, name):
    n = len(parts)
    lands = [lax.empty(((N_DEV,) + tuple(p.shape)) if gather else tuple(p.shape), p.dtype) for p in parts]

    def body(*refs):
        ins, lnd = refs[:n], refs[n:2 * n]
        ssem, rsem, token = refs[2 * n], refs[2 * n + 1], refs[-1]
        pos, me = _my_position()
        for k in range(1, N_DEV):
            peer, pj = _peer(pos, k)
            for t in range(n):
                src = ins[t] if gather else ins[t].at[pj]
                q = t * (N_DEV - 1) + k - 1
                _remote(src, lnd[t].at[me], ssem.at[q], rsem.at[q], peer).start()
        token[...] = jnp.zeros(token.shape, F32)

    out = pl.pallas_call(
        body,
        name=name,
        out_shape=(pltpu.SemaphoreType.DMA((n * (N_DEV - 1),)), pltpu.SemaphoreType.DMA((n * (N_DEV - 1),)))
        + tuple(pltpu.HBM(p.shape, p.dtype) for p in parts) + tuple(pltpu.HBM(p.shape, p.dtype) for p in lands)
        + (jax.ShapeDtypeStruct((SUBLANES, 128), F32),),
        in_specs=[HBM_SPEC] * (2 * n),
        out_specs=(SEM_SPEC, SEM_SPEC) + (HBM_SPEC,) * (2 * n) + (VMEM_SPEC,),
        input_output_aliases={i: 2 + i for i in range(2 * n)},
        compiler_params=pltpu.CompilerParams(has_side_effects=SIDE_EFFECT),
    )(*[pltpu.with_memory_space_constraint(p, pltpu.HBM) for p in list(parts) + lands])
    return out[0], out[1], list(out[2:2 + n]), list(out[2 + n:2 + 2 * n]), out[-1]


def _exchange_wait(started, after, gather, name):
    ssem, rsem, parts, lands, _ = started
    n = len(parts)

    def body(*refs):
        ins, lnd = refs[:n], refs[n:2 * n]
        ssem_ref, rsem_ref = refs[2 * n], refs[2 * n + 1]
        stage, lsem = refs[-1 - n:-1], refs[-1]
        pos, me = _my_position()
        load = []
        for t in range(n):
            src = ins[t] if gather else ins[t].at[me]
            load.append(pltpu.make_async_copy(src, stage[t], lsem.at[t]))
            load[-1].start()
        store = []
        for t in range(n):
            load[t].wait()
            store.append(pltpu.make_async_copy(stage[t], lnd[t].at[me], lsem.at[t]))
            store[-1].start()
        for k in range(1, N_DEV):
            peer, pj = _peer(pos, k)
            for t in range(n):
                src = ins[t] if gather else ins[t].at[pj]
                q = t * (N_DEV - 1) + k - 1
                _remote(src, lnd[t].at[me], ssem_ref.at[q], rsem_ref.at[q], peer).wait_send()
                _remote(src, lnd[t].at[pj], ssem_ref.at[q], rsem_ref.at[q], peer).wait_recv()
        for cp in store:
            cp.wait()

    out = pl.pallas_call(
        body,
        name=name,
        out_shape=tuple(pltpu.HBM(p.shape, p.dtype) for p in parts) + tuple(pltpu.HBM(p.shape, p.dtype) for p in lands),
        in_specs=[HBM_SPEC] * (2 * n) + [SEM_SPEC, SEM_SPEC, ANY_SPEC],
        out_specs=(HBM_SPEC,) * (2 * n),
        input_output_aliases={i: i for i in range(2 * n)},
        scratch_shapes=[pltpu.VMEM(tuple(z.shape[1:]), z.dtype) for z in lands] + [pltpu.SemaphoreType.DMA((n,))],
        compiler_params=pltpu.CompilerParams(has_side_effects=SIDE_EFFECT),
    )(*parts, *lands, ssem, rsem, after)
    return list(out[n:])


CHIP_PEERS = 3


def _chip_peer(pos, k, core):
    x, y, _ = pos
    px = 1 - x if k & 2 else x
    py = 1 - y if k & 1 else y
    return (px, py, core), 4 * px + 2 * py + core


def _gather2_start(parts, after, name):
    n = len(parts)
    lands = [lax.empty((N_DEV,) + tuple(p.shape), p.dtype) for p in parts]
    per = CHIP_PEERS + 1

    def body(*refs):
        ins, lnd = refs[:n], refs[n:2 * n]
        ssem, rsem, token = refs[2 * n + 1], refs[2 * n + 2], refs[-1]
        pos, me = _my_position()
        sibling = (pos[0], pos[1], 1 - pos[2])
        for t in range(n):
            _remote(ins[t], lnd[t].at[me], ssem.at[per * t], rsem.at[per * t], sibling).start()
        for k in range(1, per):
            peer, _ = _chip_peer(pos, k, pos[2])
            for t in range(n):
                _remote(ins[t], lnd[t].at[me], ssem.at[per * t + k], rsem.at[per * t + k], peer).start()
        token[...] = jnp.zeros(token.shape, F32)

    out = pl.pallas_call(
        body,
        name=name,
        out_shape=(pltpu.SemaphoreType.DMA((n * per,)), pltpu.SemaphoreType.DMA((n * per,)))
        + tuple(pltpu.HBM(p.shape, p.dtype) for p in parts) + tuple(pltpu.HBM(p.shape, p.dtype) for p in lands)
        + (jax.ShapeDtypeStruct((SUBLANES, 128), F32),),
        in_specs=[HBM_SPEC] * (2 * n) + [ANY_SPEC],
        out_specs=(SEM_SPEC, SEM_SPEC) + (HBM_SPEC,) * (2 * n) + (VMEM_SPEC,),
        input_output_aliases={i: 2 + i for i in range(2 * n)},
        compiler_params=pltpu.CompilerParams(has_side_effects=SIDE_EFFECT),
    )(*[pltpu.with_memory_space_constraint(p, pltpu.HBM) for p in list(parts) + lands], after)
    return out[0], out[1], list(out[2:2 + n]), list(out[2 + n:2 + 2 * n]), out[-1]


def _gather2_forward(started, after, name):
    ssem, rsem, parts, lands, _ = started
    n = len(lands)
    per = CHIP_PEERS + 1

    def body(*refs):
        lnd = refs[:n]
        rsem_a, fsend, frecv, token = refs[n], refs[2 * n + 2], refs[2 * n + 3], refs[2 * n + 4]
        pos, _ = _my_position()
        sibling = (pos[0], pos[1], 1 - pos[2])
        for k in range(1, per):
            peer, pk = _chip_peer(pos, k, pos[2])
            for t in range(n):
                block = lnd[t].at[pk]
                _remote(block, block, rsem_a.at[per * t + k], rsem_a.at[per * t + k], peer).wait_recv()
                q = CHIP_PEERS * t + k - 1
                _remote(block, block, fsend.at[q], frecv.at[q], sibling).start()
        token[...] = jnp.zeros(token.shape, F32)

    out = pl.pallas_call(
        body,
        name=name,
        out_shape=tuple(pltpu.HBM(p.shape, p.dtype) for p in lands)
        + (pltpu.SemaphoreType.DMA((n * CHIP_PEERS,)), pltpu.SemaphoreType.DMA((n * CHIP_PEERS,)),
           jax.ShapeDtypeStruct((SUBLANES, 128), F32)),
        in_specs=[HBM_SPEC] * n + [SEM_SPEC, ANY_SPEC],
        out_specs=(HBM_SPEC,) * n + (SEM_SPEC, SEM_SPEC, VMEM_SPEC),
        input_output_aliases={i: i for i in range(n)},
        compiler_params=pltpu.CompilerParams(has_side_effects=SIDE_EFFECT),
    )(*lands, rsem, after)
    return ssem, rsem, parts, list(out[:n]), out[n + 2], out[n], out[n + 1]


def _gather2_wait(forwarded, after, name):
    ssem, rsem, parts, lands, _, fsend, frecv = forwarded
    n = len(parts)
    per = CHIP_PEERS + 1

    def body(*refs):
        ins, lnd = refs[:n], refs[n:2 * n]
        ssem_a, rsem_a, fs, fr = refs[2 * n:2 * n + 4]
        stage, lsem = refs[-1 - n:-1], refs[-1]
        pos, me = _my_position()
        sibling = (pos[0], pos[1], 1 - pos[2])
        sib = 4 * pos[0] + 2 * pos[1] + 1 - pos[2]
        load = []
        for t in range(n):
            load.append(pltpu.make_async_copy(ins[t], stage[t], lsem.at[t]))
            load[-1].start()
        store = []
        for t in range(n):
            load[t].wait()
            store.append(pltpu.make_async_copy(stage[t], lnd[t].at[me], lsem.at[t]))
            store[-1].start()
        for t in range(n):
            _remote(ins[t], lnd[t].at[me], ssem_a.at[per * t], rsem_a.at[per * t], sibling).wait_send()
            _remote(ins[t], lnd[t].at[sib], ssem_a.at[per * t], rsem_a.at[per * t], sibling).wait_recv()
        for k in range(1, per):
            peer, pk = _chip_peer(pos, k, pos[2])
            _, qk = _chip_peer(pos, k, 1 - pos[2])
            for t in range(n):
                q = CHIP_PEERS * t + k - 1
                _remote(ins[t], lnd[t].at[me], ssem_a.at[per * t + k], rsem_a.at[per * t + k], peer).wait_send()
                _remote(lnd[t].at[pk], lnd[t].at[pk], fs.at[q], fr.at[q], sibling).wait_send()
                _remote(lnd[t].at[qk], lnd[t].at[qk], fs.at[q], fr.at[q], sibling).wait_recv()
        for cp in store:
            cp.wait()

    out = pl.pallas_call(
        body,
        name=name,
        out_shape=tuple(pltpu.HBM(p.shape, p.dtype) for p in parts) + tuple(pltpu.HBM(p.shape, p.dtype) for p in lands),
        in_specs=[HBM_SPEC] * (2 * n) + [SEM_SPEC] * 4 + [ANY_SPEC],
        out_specs=(HBM_SPEC,) * (2 * n),
        input_output_aliases={i: i for i in range(2 * n)},
        scratch_shapes=[pltpu.VMEM(tuple(z.shape[1:]), z.dtype) for z in lands] + [pltpu.SemaphoreType.DMA((n,))],
        compiler_params=pltpu.CompilerParams(has_side_effects=SIDE_EFFECT),
    )(*parts, *lands, ssem, rsem, fsend, frecv, after)
    return list(out[n:])


def _reduce_small(rows, dm8, cact, after):
    r, d = rows.shape
    nl = dm8.shape[0]
    cm = dm8.shape[2] // N_DEV

    def body(rows_ref, dm_ref, cact_ref, after_ref, orow_ref, owm_ref, gr, dmr, ssem, rsem, lsem):
        pos, me = _my_position()
        me8 = pl.multiple_of(me * SUBLANES, SUBLANES)
        gr[me] = rows_ref[...]
        own_dm = pltpu.make_async_copy(dm_ref.at[:, :, pl.ds(me * cm, cm)], dmr.at[:, pl.ds(me8, SUBLANES), :], lsem.at[0])
        own_dm.start()
        sends = []
        for k in range(1, N_DEV):
            peer, pj = _peer(pos, k)
            sends.append(_remote(gr.at[me], gr.at[me], ssem.at[0, k - 1], rsem.at[0, k - 1], peer))
            sends.append(_remote(dm_ref.at[:, :, pl.ds(pj * cm, cm)], dmr.at[:, pl.ds(me8, SUBLANES), :],
                                 ssem.at[1, k - 1], rsem.at[1, k - 1], peer))
        for cp in sends:
            cp.start()
        for k in range(1, N_DEV):
            peer, pj = _peer(pos, k)
            pj8 = pl.multiple_of(pj * SUBLANES, SUBLANES)
            _remote(gr.at[pj], gr.at[pj], ssem.at[0, k - 1], rsem.at[0, k - 1], peer).wait_recv()
            _remote(dm_ref.at[:, :, pl.ds(0, cm)], dmr.at[:, pl.ds(pj8, SUBLANES), :], ssem.at[1, k - 1], rsem.at[1, k - 1], peer).wait_recv()
        own_dm.wait()
        acc = gr[0]
        for j in range(1, N_DEV):
            acc = acc + gr[j]
        orow_ref[...] = acc
        cb = cact_ref[...].astype(BF16)
        for l in range(nl):
            owm_ref[l] = _dot_tn(cb, dmr[l].astype(BF16))
        for cp in sends:
            cp.wait_send()

    return pl.pallas_call(
        body,
        name="reduce_small",
        out_shape=(jax.ShapeDtypeStruct((r, d), F32), jax.ShapeDtypeStruct((nl, d, cm), F32)),
        in_specs=[VMEM_SPEC] * 3 + [ANY_SPEC],
        out_specs=(VMEM_SPEC,) * 2,
        scratch_shapes=[
            pltpu.VMEM((N_DEV, r, d), F32),
            pltpu.VMEM((nl, N_DEV * SUBLANES, cm), F32),
            pltpu.SemaphoreType.DMA((2, N_DEV - 1)),
            pltpu.SemaphoreType.DMA((2, N_DEV - 1)),
            pltpu.SemaphoreType.DMA((1,)),
        ],
        compiler_params=_cparams(vmem=4 * _nbytes((N_DEV, r, d), F32) + 6 * _nbytes((nl, d, cm), F32)),
    )(rows, dm8, cact, after)


def _sum_gathered(zones):
    nl = len(zones)

    def body(*refs):
        for l in range(nl):
            acc = refs[l][0].astype(F32)
            for j in range(1, N_DEV):
                acc = acc + refs[l][j].astype(F32)
            refs[nl][l] = acc

    return pl.pallas_call(
        body,
        name="sum_gathered",
        out_shape=jax.ShapeDtypeStruct((nl,) + tuple(zones[0].shape[1:]), F32),
        in_specs=[VMEM_SPEC] * nl,
        out_specs=VMEM_SPEC,
        compiler_params=_cparams(vmem=8 * nl * _nbytes(zones[0].shape, BF16)),
    )(*zones)


def _in_proj_fwd(x, mod, rows, win_f, l, tm):
    s, d = x.shape
    nb, _, ci = win_f.shape

    def body(x_ref, mod_ref, rows_ref, w_ref, proj_ref, h_ref):
        xv = x_ref[...]
        g = rows_ref[R_G_PRE_MIX:R_G_PRE_MIX + 1, :]
        h = (xv * _rms(xv) * g) * (1.0 + mod_ref[M_SC_M:M_SC_M + 1, :]) + mod_ref[M_SH_M:M_SH_M + 1, :]
        hb = h.astype(BF16)
        h_ref[...] = hb
        for j in range(nb):
            proj_ref[:, j * ci:(j + 1) * ci] = _dot(hb, w_ref[j])

    return pl.pallas_call(
        body,
        name="in_proj_fwd",
        grid=(s // tm,),
        in_specs=[
            pl.BlockSpec((tm, d), lambda i: (i, 0)),
            _resident((None, N_MOD, d), lambda i: (l, 0, 0)),
            _resident((None, N_ROWS, d), lambda i: (l, 0, 0)),
            _resident((nb, d, ci), lambda i: (0, 0, 0)),
        ],
        out_specs=(pl.BlockSpec((tm, nb * ci), lambda i: (i, 0)), pl.BlockSpec((tm, d), lambda i: (i, 0))),
        out_shape=(jax.ShapeDtypeStruct((s, nb * ci), F32), jax.ShapeDtypeStruct((s, d), BF16)),
        compiler_params=_cparams(("parallel",), _nbytes((nb, d, ci), BF16) + 3 * _nbytes((tm, nb * ci), F32) + 8 * _nbytes((tm, d), F32)),
    )(*_hbm(x), mod, rows, win_f)


def _mixer_core_fwd(proj, x, mod, rows, cwf, wr, wi, wa_f, wb_f, wo_f, l, tm):
    s, d = x.shape
    nh, bw, _ = wr.shape[1:]

    def body(proj_ref, x_ref, mod_ref, rows_ref, cw_ref, wr_ref, wi_ref, wa_ref, wb_ref, wo_ref,
             x1_ref, hs_ref, yap_ref, ybp_ref, y_ref, cvbuf, xbbuf, a_s, b_s, hprev):
        i = pl.program_id(0)

        @pl.when(i == 0)
        def _():
            cvbuf[...] = jnp.zeros((SUBLANES, d), F32)
            xbbuf[...] = jnp.zeros((SUBLANES, d), F32)
            hprev[...] = jnp.zeros((SUBLANES, d), F32)

        def row(r):
            return rows_ref[r:r + 1, :]

        def tap(r):
            return cw_ref[r:r + 1, :]

        ba = proj_ref[:, 0:d]
        cv = proj_ref[:, d:2 * d] * proj_ref[:, 2 * d:3 * d]
        cvt = cvbuf[...]
        conv3 = ((row(R_CONV_A_B) + _shift_rows(cv, cvt, 2, False) * tap(CW_A)) + _shift_rows(cv, cvt, 1, False) * tap(CW_A + 1)) + cv * tap(CW_A + 2)
        ya = ba * conv3
        cvbuf[...] = cv[tm - SUBLANES:, :]
        xb = proj_ref[:, 3 * d:4 * d]
        xbt = xbbuf[...]
        xc = (((row(R_CONV_B_B) + _shift_rows(xb, xbt, 3, False) * tap(CW_B)) + _shift_rows(xb, xbt, 2, False) * tap(CW_B + 1))
              + _shift_rows(xb, xbt, 1, False) * tap(CW_B + 2)) + xb * tap(CW_B + 3)
        xbbuf[...] = xb[tm - SUBLANES:, :]
        sp = _softplus_neg(row(R_LAMBDA))
        _, _, ig, a, mult = _gates(xc, wr_ref, wi_ref, row(R_B_GATE_R), row(R_B_GATE_I), sp, nh, bw)
        a_s[...] = a
        b_s[...] = mult * (ig * xc)

        def blk(j, hp):
            o = pl.multiple_of(j * SUBLANES, SUBLANES)
            a8, b8 = _scan_block(a_s[pl.ds(o, SUBLANES), :], b_s[pl.ds(o, SUBLANES), :], reverse=False)
            h8 = b8 + a8 * hp
            hs_ref[pl.ds(o, SUBLANES), :] = h8
            return jnp.broadcast_to(h8[SUBLANES - 1:SUBLANES, :], (SUBLANES, d))

        hprev[...] = lax.fori_loop(0, tm // SUBLANES, blk, hprev[...])
        gel, _ = _gelu(proj_ref[:, 4 * d:5 * d])
        yb = hs_ref[...] * gel
        yap = _dot(ya.astype(BF16), wa_ref[...])
        ybp = _dot(yb.astype(BF16), wb_ref[...])
        yap_ref[...] = yap
        ybp_ref[...] = ybp
        m = _sigmoid(proj_ref[:, 5 * d:6 * d]) * yap + _sigmoid(proj_ref[:, 6 * d:7 * d]) * ybp
        y = _dot(m.astype(BF16), wo_ref[...])
        y_ref[...] = y
        x1_ref[...] = x_ref[...] + mod_ref[M_GT_M:M_GT_M + 1, :] * ((y * _rms(y)) * row(R_G_POST_MIX))

    tile = pl.BlockSpec((tm, d), lambda i: (i, 0))
    return pl.pallas_call(
        body,
        name="mixer_core_fwd",
        grid=(s // tm,),
        in_specs=[
            pl.BlockSpec((tm, 7 * d), lambda i: (i, 0)),
            tile,
            _resident((None, N_MOD, d), lambda i: (l, 0, 0)),
            _resident((None, N_ROWS, d), lambda i: (l, 0, 0)),
            _resident((None, CW_ROWS, d), lambda i: (l, 0, 0)),
            _resident((None, nh, bw, bw), lambda i: (l, 0, 0, 0)),
            _resident((None, nh, bw, bw), lambda i: (l, 0, 0, 0)),
            _resident((d, d), lambda i: (0, 0)),
            _resident((d, d), lambda i: (0, 0)),
            _resident((d, d), lambda i: (0, 0)),
        ],
        out_specs=(tile,) * 5,
        out_shape=(jax.ShapeDtypeStruct((s, d), F32),) * 5,
        scratch_shapes=[
            pltpu.VMEM((SUBLANES, d), F32),
            pltpu.VMEM((SUBLANES, d), F32),
            pltpu.VMEM((tm, d), F32),
            pltpu.VMEM((tm, d), F32),
            pltpu.VMEM((SUBLANES, d), F32),
        ],
        compiler_params=_cparams(("arbitrary",), 3 * _nbytes((d, d), BF16) + 2 * _nbytes((tm, 7 * d), F32) + 40 * _nbytes((tm, d), F32)),
    )(*_hbm(proj, x), mod, rows, cwf, wr, wi, wa_f, wb_f, wo_f)


def _mlp_fwd(x1, mod, rows, wup_f, wdn_f, l, tm):
    s, d = x1.shape
    nb, _, cu = wup_f.shape
    dff = nb * cu

    def body(x1_ref, mod_ref, rows_ref, wu_ref, wd_ref, x2_ref, ru_ref, y2_ref, h2_ref):
        xv = x1_ref[...]
        g = rows_ref[R_G_PRE_MLP:R_G_PRE_MLP + 1, :]
        h2 = ((xv * _rms(xv) * g) * (1.0 + mod_ref[M_SC_F:M_SC_F + 1, :]) + mod_ref[M_SH_F:M_SH_F + 1, :]).astype(BF16)
        h2_ref[...] = h2
        ru = jnp.concatenate([jnp.maximum(_dot(h2, wu_ref[j]), 0.0) for j in range(nb)], axis=1)
        ru_ref[...] = ru.astype(BF16)
        y2 = _dot((ru * ru).astype(BF16), wd_ref[...])
        y2_ref[...] = y2
        x2_ref[...] = xv + mod_ref[M_GT_F:M_GT_F + 1, :] * ((y2 * _rms(y2)) * rows_ref[R_G_POST_MLP:R_G_POST_MLP + 1, :])

    tile = pl.BlockSpec((tm, d), lambda i: (i, 0))
    wide = pl.BlockSpec((tm, dff), lambda i: (i, 0))
    return pl.pallas_call(
        body,
        name="mlp_fwd",
        grid=(s // tm,),
        in_specs=[
            tile,
            _resident((None, N_MOD, d), lambda i: (l, 0, 0)),
            _resident((None, N_ROWS, d), lambda i: (l, 0, 0)),
            _resident((nb, d, cu), lambda i: (0, 0, 0)),
            _resident((dff, d), lambda i: (0, 0)),
        ],
        out_specs=(tile, wide, tile, tile),
        out_shape=(jax.ShapeDtypeStruct((s, d), F32), jax.ShapeDtypeStruct((s, dff), BF16),
                   jax.ShapeDtypeStruct((s, d), F32), jax.ShapeDtypeStruct((s, d), BF16)),
        compiler_params=_cparams(("parallel",), 2 * _nbytes((dff, d), BF16) + 5 * _nbytes((tm, dff), F32) + 12 * _nbytes((tm, d), F32)),
    )(*_hbm(x1), mod, rows, wup_f, wdn_f)


def _loss_fwd_bwd(y, target, tm):
    s, d = y.shape

    def body(y_ref, t_ref, loss_ref, dy_ref):
        @pl.when(pl.program_id(0) == 0)
        def _():
            loss_ref[...] = jnp.zeros(loss_ref.shape, F32)

        e = y_ref[...] - t_ref[...]
        dy_ref[...] = e * (1.0 / d)
        loss_ref[...] += 0.5 * jnp.sum(jnp.mean(e * e, axis=-1, keepdims=True), axis=0, keepdims=True)

    tile = pl.BlockSpec((tm, d), lambda i: (i, 0))
    loss, dy = pl.pallas_call(
        body,
        name="loss",
        grid=(s // tm,),
        in_specs=[tile, tile],
        out_specs=(pl.BlockSpec((SUBLANES, 128), lambda i: (0, 0)), tile),
        out_shape=(jax.ShapeDtypeStruct((SUBLANES, 128), F32), jax.ShapeDtypeStruct((s, d), F32)),
        compiler_params=_cparams(("arbitrary",)),
    )(*_hbm(y, target))
    return loss[0, 0], dy


def _mlp_bwd(dx2, x1, y2, ru, mod, rows, wup_f, wdn_f, l, tm):
    s, d = x1.shape
    nb, _, cu = wup_f.shape
    dff = nb * cu

    def body(dx2_ref, x1_ref, y2_ref, ru_ref, mod_ref, rows_ref, wu_ref, wd_ref, dx1_ref, dy2_ref, dup_ref, act_ref, sm_ref):
        @pl.when(pl.program_id(0) == 0)
        def _():
            sm_ref[...] = jnp.zeros(sm_ref.shape, F32)

        dout = dx2_ref[...]
        dy2, dgt, dgpost = _postnorm_bwd(y2_ref[...], dout, rows_ref[R_G_POST_MLP:R_G_POST_MLP + 1, :], mod_ref[M_GT_F:M_GT_F + 1, :])
        dy2b = dy2.astype(BF16)
        dy2_ref[...] = dy2b
        ruv = ru_ref[...].astype(F32)
        act_ref[...] = (ruv * ruv).astype(BF16)
        dup = (_dot_nt(dy2b, wd_ref[...]) * (2.0 * ruv)).astype(BF16)
        dup_ref[...] = dup
        dh2 = _dot_nt(dup[:, 0:cu], wu_ref[0])
        for j in range(1, nb):
            dh2 = dh2 + _dot_nt(dup[:, j * cu:(j + 1) * cu], wu_ref[j])
        dxn, dsc, dsh, dgpre = _prenorm_bwd(x1_ref[...], dh2, rows_ref[R_G_PRE_MLP:R_G_PRE_MLP + 1, :], mod_ref[M_SC_F:M_SC_F + 1, :])
        dx1_ref[...] = dout + dxn
        for r, v in ((G_MLP_GT, dgt), (G_MLP_GPOST, dgpost), (G_MLP_SC, dsc), (G_MLP_SH, dsh), (G_MLP_GPRE, dgpre)):
            sm_ref[r:r + 1, :] += v

    tile = pl.BlockSpec((tm, d), lambda i: (i, 0))
    wide = pl.BlockSpec((tm, dff), lambda i: (i, 0))
    return pl.pallas_call(
        body,
        name="mlp_bwd",
        grid=(s // tm,),
        in_specs=[
            tile, tile, tile, wide,
            _resident((None, N_MOD, d), lambda i: (l, 0, 0)),
            _resident((None, N_ROWS, d), lambda i: (l, 0, 0)),
            _resident((nb, d, cu), lambda i: (0, 0, 0)),
            _resident((dff, d), lambda i: (0, 0)),
        ],
        out_specs=(tile, tile, wide, wide, pl.BlockSpec((G_MLP_ROWS, d), lambda i: (0, 0))),
        out_shape=(jax.ShapeDtypeStruct((s, d), F32), jax.ShapeDtypeStruct((s, d), BF16), jax.ShapeDtypeStruct((s, dff), BF16),
                   jax.ShapeDtypeStruct((s, dff), BF16), jax.ShapeDtypeStruct((G_MLP_ROWS, d), F32)),
        compiler_params=_cparams(("arbitrary",), 2 * _nbytes((dff, d), BF16) + 6 * _nbytes((tm, dff), F32) + 16 * _nbytes((tm, d), F32)),
    )(*_hbm(dx2, x1, y2, ru), mod, rows, wup_f, wdn_f)


def _mixer_core_bwd(dx1, y, yap, ybp, hs, proj, mod, rows, cwf, wr, wi, wa_f, wb_f, wo_f, l, tm):
    s, d = dx1.shape
    nh, bw, _ = wr.shape[1:]
    nt = s // tm
    per = tm // SUBLANES

    def body(dx1_ref, y_ref, yap_ref, ybp_ref, hs_ref, hsh_ref, proj_ref, projh_ref, mod_ref, rows_ref, cw_ref,
             wr_ref, wi_ref, wa_ref, wb_ref, wo_ref,
             dproj_ref, dy_ref, m_ref, dyap_ref, dybp_ref, ya_ref, yb_ref, sm_ref, dwg_ref,
             abuf, dcbuf, dxbuf, al_s, dh_s, lam_s, lnext):
        i = pl.program_id(0)
        first_tile = i == nt - 1

        @pl.when(i == 0)
        def _():
            sm_ref[...] = jnp.zeros(sm_ref.shape, F32)
            dwg_ref[...] = jnp.zeros(dwg_ref.shape, F32)
            zero = jnp.zeros((SUBLANES, d), F32)
            abuf[...] = zero
            dcbuf[...] = zero
            dxbuf[...] = zero
            lnext[...] = zero

        def row(r):
            return rows_ref[r:r + 1, :]

        def tap(r):
            return cw_ref[r:r + 1, :]

        def acc(r, v):
            sm_ref[r:r + 1, :] += v

        keep_halo = jnp.where(first_tile, 0.0, 1.0)
        dy, dgt, dgpost = _postnorm_bwd(y_ref[...], dx1_ref[...], row(R_G_POST_MIX), mod_ref[M_GT_M:M_GT_M + 1, :])
        acc(G_MIX_GT, dgt)
        acc(G_MIX_GPOST, dgpost)
        dyb16 = dy.astype(BF16)
        dy_ref[...] = dyb16
        dm = _dot_nt(dyb16, wo_ref[...])
        sa = _sigmoid(proj_ref[:, 5 * d:6 * d])
        sb = _sigmoid(proj_ref[:, 6 * d:7 * d])
        yap = yap_ref[...]
        ybp = ybp_ref[...]
        m_ref[...] = (sa * yap + sb * ybp).astype(BF16)
        dyap = (dm * sa).astype(BF16)
        dybp = (dm * sb).astype(BF16)
        dyap_ref[...] = dyap
        dybp_ref[...] = dybp
        dproj_ref[:, 5 * d:6 * d] = (dm * yap * sa * (1.0 - sa)).astype(BF16)
        dproj_ref[:, 6 * d:7 * d] = (dm * ybp * sb * (1.0 - sb)).astype(BF16)
        dya = _dot_nt(dyap, wa_ref[...])
        dyb = _dot_nt(dybp, wb_ref[...])
        ba = proj_ref[:, 0:d]
        ca = proj_ref[:, d:2 * d]
        va = proj_ref[:, 2 * d:3 * d]
        cv = ca * va
        cvh = keep_halo * (projh_ref[:, d:2 * d] * projh_ref[:, 2 * d:3 * d])
        cvm2 = _shift_rows(cv, cvh, 2, False)
        cvm1 = _shift_rows(cv, cvh, 1, False)
        conv3 = ((row(R_CONV_A_B) + cvm2 * tap(CW_A)) + cvm1 * tap(CW_A + 1)) + cv * tap(CW_A + 2)
        ya_ref[...] = (ba * conv3).astype(BF16)
        dproj_ref[:, 0:d] = (dya * conv3).astype(BF16)
        dc3 = dya * ba
        acc(G_MIX_CAB, _colsum(dc3))
        acc(G_MIX_CAW, _colsum(dc3 * cvm2))
        acc(G_MIX_CAW + 1, _colsum(dc3 * cvm1))
        acc(G_MIX_CAW + 2, _colsum(dc3 * cv))
        dct = dcbuf[...]
        dcv = (dc3 * tap(CW_A + 2) + _shift_rows(dc3, dct, 1, True) * tap(CW_A + 1)) + _shift_rows(dc3, dct, 2, True) * tap(CW_A)
        dcbuf[...] = dc3[:SUBLANES, :]
        dproj_ref[:, d:2 * d] = (dcv * va).astype(BF16)
        dproj_ref[:, 2 * d:3 * d] = (dcv * ca).astype(BF16)
        xb = proj_ref[:, 3 * d:4 * d]
        gb = proj_ref[:, 4 * d:5 * d]
        xbh = keep_halo * projh_ref[:, 3 * d:4 * d]
        xm3 = _shift_rows(xb, xbh, 3, False)
        xm2 = _shift_rows(xb, xbh, 2, False)
        xm1 = _shift_rows(xb, xbh, 1, False)
        xc = (((row(R_CONV_B_B) + xm3 * tap(CW_B)) + xm2 * tap(CW_B + 1)) + xm1 * tap(CW_B + 2)) + xb * tap(CW_B + 3)
        lam = row(R_LAMBDA)
        sp = _softplus_neg(lam)
        xcb, r, ig, a, mult = _gates(xc, wr_ref, wi_ref, row(R_B_GATE_R), row(R_B_GATE_I), sp, nh, bw)
        gel, th = _gelu(gb)
        hs = hs_ref[...]
        yb_ref[...] = (hs * gel).astype(BF16)
        dproj_ref[:, 4 * d:5 * d] = (dyb * hs * _gelu_grad(gb, th)).astype(BF16)
        al_s[...] = _shift_rows(a, abuf[...], 1, True)
        abuf[...] = a[:SUBLANES, :]
        dh_s[...] = dyb * gel

        def blk(j, ln):
            o = pl.multiple_of((per - 1 - j) * SUBLANES, SUBLANES)
            a8, b8 = _scan_block(al_s[pl.ds(o, SUBLANES), :], dh_s[pl.ds(o, SUBLANES), :], reverse=True)
            l8 = b8 + a8 * ln
            lam_s[pl.ds(o, SUBLANES), :] = l8
            return jnp.broadcast_to(l8[0:1, :], (SUBLANES, d))

        lnext[...] = lax.fori_loop(0, per, blk, lnext[...])
        dbb = lam_s[...]
        da = dbb * _shift_rows(hs, keep_halo * hsh_ref[...], 1, False)
        dmult = dbb * (ig * xc)
        dig = dbb * (mult * xc)
        dxc = dbb * (mult * ig)
        dla = da * a - dmult * (a * a) / mult
        acc(G_MIX_LAM, _colsum(dla * (-LRU_C * r)) * (-_sigmoid(-lam)))
        dzr = (dla * (-LRU_C * sp)) * r * (1.0 - r)
        dzi = dig * ig * (1.0 - ig)
        acc(G_MIX_BR, _colsum(dzr))
        acc(G_MIX_BI, _colsum(dzi))
        dzrb = dzr.astype(BF16)
        dzib = dzi.astype(BF16)
        back = []
        for h in range(nh):
            sl = slice(h * bw, (h + 1) * bw)
            back.append(_dot_nt(dzrb[:, sl], wr_ref[h]) + _dot_nt(dzib[:, sl], wi_ref[h]))
            dwg_ref[0, h] += _dot_tn(xcb[:, sl], dzrb[:, sl])
            dwg_ref[1, h] += _dot_tn(xcb[:, sl], dzib[:, sl])
        dxc = dxc + jnp.concatenate(back, axis=1)
        acc(G_MIX_CBB, _colsum(dxc))
        acc(G_MIX_CBW, _colsum(dxc * xm3))
        acc(G_MIX_CBW + 1, _colsum(dxc * xm2))
        acc(G_MIX_CBW + 2, _colsum(dxc * xm1))
        acc(G_MIX_CBW + 3, _colsum(dxc * xb))
        dxt = dxbuf[...]
        dxb = (((dxc * tap(CW_B + 3) + _shift_rows(dxc, dxt, 1, True) * tap(CW_B + 2)) + _shift_rows(dxc, dxt, 2, True) * tap(CW_B + 1))
               + _shift_rows(dxc, dxt, 3, True) * tap(CW_B))
        dxbuf[...] = dxc[:SUBLANES, :]
        dproj_ref[:, 3 * d:4 * d] = dxb.astype(BF16)

    def rev(i):
        return (nt - 1 - i, 0)

    def halo(i):
        return (jnp.maximum((nt - 1 - i) * per - 1, 0), 0)

    tile = pl.BlockSpec((tm, d), rev)
    return pl.pallas_call(
        body,
        name="mixer_core_bwd",
        grid=(nt,),
        in_specs=[
            tile, tile, tile, tile, tile,
            pl.BlockSpec((SUBLANES, d), halo),
            pl.BlockSpec((tm, 7 * d), rev),
            pl.BlockSpec((SUBLANES, 7 * d), halo),
            _resident((None, N_MOD, d), lambda i: (l, 0, 0)),
            _resident((None, N_ROWS, d), lambda i: (l, 0, 0)),
            _resident((None, CW_ROWS, d), lambda i: (l, 0, 0)),
            _resident((None, nh, bw, bw), lambda i: (l, 0, 0, 0)),
            _resident((None, nh, bw, bw), lambda i: (l, 0, 0, 0)),
            _resident((d, d), lambda i: (0, 0)),
            _resident((d, d), lambda i: (0, 0)),
            _resident((d, d), lambda i: (0, 0)),
        ],
        out_specs=(pl.BlockSpec((tm, 7 * d), rev),) + (tile,) * 6 + (
            pl.BlockSpec((G_MIX_ROWS, d), lambda i: (0, 0)), pl.BlockSpec((2, nh, bw, bw), lambda i: (0, 0, 0, 0))),
        out_shape=(jax.ShapeDtypeStruct((s, 7 * d), BF16),) + (jax.ShapeDtypeStruct((s, d), BF16),) * 6 + (
            jax.ShapeDtypeStruct((G_MIX_ROWS, d), F32), jax.ShapeDtypeStruct((2, nh, bw, bw), F32)),
        scratch_shapes=[pltpu.VMEM((SUBLANES, d), F32)] * 3 + [pltpu.VMEM((tm, d), F32)] * 3 + [pltpu.VMEM((SUBLANES, d), F32)],
        compiler_params=_cparams(("arbitrary",), 3 * _nbytes((d, d), BF16) + 3 * _nbytes((tm, 7 * d), F32) + 64 * _nbytes((tm, d), F32)),
    )(*_hbm(dx1, y, yap, ybp, hs, hs, proj, proj), mod, rows, cwf, wr, wi, wa_f, wb_f, wo_f)


def _in_proj_bwd(dproj, x, dx1, mod, rows, win_f, l, tm):
    s, d = x.shape
    nb, _, ci = win_f.shape

    def body(dp_ref, x_ref, dx1_ref, mod_ref, rows_ref, w_ref, dx_ref, sm_ref):
        @pl.when(pl.program_id(0) == 0)
        def _():
            sm_ref[...] = jnp.zeros(sm_ref.shape, F32)

        dh = _dot_nt(dp_ref[:, 0:ci], w_ref[0])
        for j in range(1, nb):
            dh = dh + _dot_nt(dp_ref[:, j * ci:(j + 1) * ci], w_ref[j])
        dxn, dsc, dsh, dg = _prenorm_bwd(x_ref[...], dh, rows_ref[R_G_PRE_MIX:R_G_PRE_MIX + 1, :], mod_ref[M_SC_M:M_SC_M + 1, :])
        dx_ref[...] = dx1_ref[...] + dxn
        for r, v in ((G_IN_SC, dsc), (G_IN_SH, dsh), (G_IN_GPRE, dg)):
            sm_ref[r:r + 1, :] += v

    tile = pl.BlockSpec((tm, d), lambda i: (i, 0))
    return pl.pallas_call(
        body,
        name="in_proj_bwd",
        grid=(s // tm,),
        in_specs=[
            pl.BlockSpec((tm, nb * ci), lambda i: (i, 0)), tile, tile,
            _resident((None, N_MOD, d), lambda i: (l, 0, 0)),
            _resident((None, N_ROWS, d), lambda i: (l, 0, 0)),
            _resident((nb, d, ci), lambda i: (0, 0, 0)),
        ],
        out_specs=(tile, pl.BlockSpec((G_IN_ROWS, d), lambda i: (0, 0))),
        out_shape=(jax.ShapeDtypeStruct((s, d), F32), jax.ShapeDtypeStruct((G_IN_ROWS, d), F32)),
        compiler_params=_cparams(("arbitrary",), _nbytes((nb, d, ci), BF16) + 2 * _nbytes((tm, nb * ci), BF16) + 16 * _nbytes((tm, d), F32)),
    )(*_hbm(dproj, x, dx1), mod, rows, win_f)


def _wgrad(a, b, cols_owned, ts, after):
    s, k1 = a.shape
    k2 = b.shape[1]
    ns = s // ts
    if cols_owned:
        nblk, bk1, bk2 = N_DEV, k1, k2 // N_DEV
        a_spec = pl.BlockSpec((ts, bk1), lambda j, t: (t, 0))
        b_spec = pl.BlockSpec((ts, bk2), lambda j, t: (t, j))
    else:
        bk1, bk2 = min(WGRAD_ROWS, k1), k2
        nblk = k1 // bk1
        a_spec = pl.BlockSpec((ts, bk1), lambda j, t: (t, j))
        b_spec = pl.BlockSpec((ts, bk2), lambda j, t: (t, 0))

    def body(a_ref, b_ref, after_ref, o_ref, acc_ref):
        t = pl.program_id(1)

        @pl.when(t == 0)
        def _():
            acc_ref[...] = jnp.zeros(acc_ref.shape, F32)

        acc_ref[...] += _dot_tn(a_ref[...], b_ref[...])

        @pl.when(t == ns - 1)
        def _():
            o_ref[...] = acc_ref[...].astype(BF16)

    out = pl.pallas_call(
        body,
        name="wgrad",
        grid=(nblk, ns),
        in_specs=[a_spec, b_spec, ANY_SPEC],
        out_specs=pl.BlockSpec((None, bk1, bk2), lambda j, t: (j, 0, 0)),
        out_shape=pltpu.HBM((nblk, bk1, bk2), BF16),
        scratch_shapes=[pltpu.VMEM((bk1, bk2), F32)],
        compiler_params=_cparams(("parallel", "arbitrary"), 4 * _nbytes((bk1, bk2), F32) + 4 * _nbytes((ts, bk1 + bk2), BF16)),
    )(pltpu.with_memory_space_constraint(a, pltpu.HBM), pltpu.with_memory_space_constraint(b, pltpu.HBM), after)
    return out if cols_owned else out.reshape(N_DEV, k1 // N_DEV, k2)


def _adam_update(w, g, m, v):
    m = ADAM_B1 * m + (1.0 - ADAM_B1) * g
    v = ADAM_B2 * v + (1.0 - ADAM_B2) * (g * g)
    m_hat = m / (1.0 - ADAM_B1 ** ADAM_STEP)
    v_hat = v / (1.0 - ADAM_B2 ** ADAM_STEP)
    delta = -ADAM_LR * (m_hat / (jnp.sqrt(v_hat) + ADAM_EPS) + ADAM_WD * w)
    return delta, m, v


def _sum_adamw(recv, w, m, v, tr, after):
    nl, ra, cb = w.shape
    assert nl == len(recv) == 2

    def body(r0_ref, r1_ref, w_ref, m_ref, v_ref, after_ref, g_ref, d_ref, nm_ref, nv_ref):
        def total(r_ref):
            g = r_ref[0].astype(F32)
            for j in range(1, N_DEV):
                g = g + r_ref[j].astype(F32)
            return g

        g = jnp.where(pl.program_id(0) == 0, total(r0_ref), total(r1_ref))
        g_ref[...] = g
        d_ref[...], nm_ref[...], nv_ref[...] = _adam_update(w_ref[...], g, m_ref[...], v_ref[...])

    blk = pl.BlockSpec((None, tr, cb), lambda l, i: (l, i, 0))
    return pl.pallas_call(
        body,
        name="sum_adamw",
        grid=(nl, ra // tr),
        in_specs=[pl.BlockSpec((N_DEV, tr, cb), lambda l, i: (0, i * (1 - l), 0)),
                  pl.BlockSpec((N_DEV, tr, cb), lambda l, i: (0, i * l, 0)), blk, blk, blk, ANY_SPEC],
        out_specs=(blk,) * 4,
        out_shape=(jax.ShapeDtypeStruct((nl, ra, cb), F32),) * 4,
        compiler_params=_cparams(("arbitrary", "arbitrary"), 6 * _nbytes((N_DEV, tr, cb), BF16) + 32 * _nbytes((tr, cb), F32)),
    )(recv[0], recv[1], w, m, v, after)


def _adamw(w, g, m, v):
    def body(w_ref, g_ref, m_ref, v_ref, d_ref, nm_ref, nv_ref):
        d_ref[...], nm_ref[...], nv_ref[...] = _adam_update(w_ref[...], g_ref[...], m_ref[...], v_ref[...])

    return pl.pallas_call(
        body,
        name="adamw",
        in_specs=[VMEM_SPEC] * 4,
        out_specs=(VMEM_SPEC,) * 3,
        out_shape=(jax.ShapeDtypeStruct(w.shape, F32),) * 3,
        compiler_params=_cparams(vmem=10 * _nbytes(w.shape, F32)),
    )(w, g, m, v)


def _adamw_tiled(w, g, m, v, tr):
    nl, ra, cb = w.shape

    def body(w_ref, g_ref, m_ref, v_ref, d_ref, nm_ref, nv_ref):
        d_ref[...], nm_ref[...], nv_ref[...] = _adam_update(w_ref[...], g_ref[...], m_ref[...], v_ref[...])

    blk = pl.BlockSpec((None, tr, cb), lambda l, i: (l, i, 0))
    return pl.pallas_call(
        body,
        name="adamw_tiled",
        grid=(nl, ra // tr),
        in_specs=[blk] * 4,
        out_specs=(blk,) * 3,
        out_shape=(jax.ShapeDtypeStruct((nl, ra, cb), F32),) * 3,
        compiler_params=_cparams(("parallel", "parallel")),
    )(*_hbm(w, g, m, v))


def _token_tile(s):
    return min(256, max(SUBLANES * 2, s // 4))


def kernel(x, c, w_mod, b_mod, g_pre_mix, g_post_mix, w_in, conv_a_w, conv_a_b, w_a_out, conv_b_w, conv_b_b, w_gate_r, b_gate_r, w_gate_i, b_gate_i, lru_lambda, w_b_out, w_o, g_pre_mlp, g_post_mlp, w_mlp_up, w_mlp_down, loss_target, m_w_mod, m_b_mod, m_g_pre_mix, m_g_post_mix, m_w_in, m_conv_a_w, m_conv_a_b, m_w_a_out, m_conv_b_w, m_conv_b_b, m_w_gate_r, m_b_gate_r, m_w_gate_i, m_b_gate_i, m_lru_lambda, m_w_b_out, m_w_o, m_g_pre_mlp, m_g_post_mlp, m_w_mlp_up, m_w_mlp_down, v_w_mod, v_b_mod, v_g_pre_mix, v_g_post_mix, v_w_in, v_conv_a_w, v_conv_a_b, v_w_a_out, v_conv_b_w, v_conv_b_b, v_w_gate_r, v_b_gate_r, v_w_gate_i, v_b_gate_i, v_lru_lambda, v_w_b_out, v_w_o, v_g_pre_mlp, v_g_post_mlp, v_w_mlp_up, v_w_mlp_down):
    nl = w_mod.shape[0]
    s, d = x.shape[1], x.shape[2]
    nh, bw = w_gate_r.shape[1], w_gate_r.shape[2]
    cwid = conv_a_w.shape[2]
    tm = _token_tile(s)
    tmx = min(2 * tm, s)
    ts = s
    _, me = _my_position()
    xs = x.reshape(s, d)
    target = loss_target.reshape(s, d)

    vec_names = (g_pre_mix, g_post_mix, conv_a_b, conv_b_b, b_gate_r, b_gate_i, lru_lambda, g_pre_mlp, g_post_mlp)
    rows = jnp.concatenate([jnp.stack(vec_names, axis=1), jnp.zeros((nl, N_ROWS - len(vec_names), d), F32)], axis=1)
    cw = jnp.concatenate([conv_a_w, conv_b_w, jnp.zeros((nl, CW_ROWS - 7, cwid), F32)], axis=1)

    w16 = {"w_in": w_in.astype(BF16), "w_a_out": w_a_out.astype(BF16), "w_b_out": w_b_out.astype(BF16), "w_o": w_o.astype(BF16),
           "w_mlp_up": w_mlp_up.astype(BF16), "w_mlp_down": w_mlp_down.astype(BF16)}
    groups = (("in", ("w_in",)), ("mix", ("w_a_out", "w_b_out", "w_o")), ("mlp", ("w_mlp_up", "w_mlp_down")))
    mod, cact, cwf = _prep_small(c, w_mod, b_mod, cw)
    mod = mod.reshape(nl, N_MOD, d)
    gathers = {}
    token = jnp.zeros((), F32)
    after = mod
    for l in range(nl):
        for gname, members in groups:
            gathers[l, gname] = _gather2_start([w16[n][l] for n in members], after, f"gather_start_{gname}{l}")
            after = gathers[l, gname][4]
            token = token + after[0, 0]
    rows = rows + token
    wr = w_gate_r.astype(BF16)
    wi = w_gate_i.astype(BF16)

    forwarded = {}

    def forward(l, gname, after, rows):
        forwarded[l, gname] = _gather2_forward(gathers[l, gname], after, f"gather_forward_{gname}{l}")
        return rows + forwarded[l, gname][4][0, 0]

    def gathered(l, gname, after):
        return _gather2_wait(forwarded[l, gname], after, f"gather_wait_{gname}{l}")

    saved = []
    weights = []
    xin = xs
    for l in range(nl):
        if l == 0:
            rows = forward(0, "in", mod, rows)
        (win_f,) = gathered(l, "in", mod if l == 0 else xin)
        if l > 0:
            rows = forward(l, "mix", win_f, rows)
        proj, h = _in_proj_fwd(xin, mod, rows, win_f, l, tmx)
        if l == 0:
            rows = forward(0, "mix", proj, rows)
        wa_f, wb_f, wo_f = (w.reshape(d, d) for w in gathered(l, "mix", proj))
        rows = forward(l, "mlp", wo_f, rows)
        x1, hs, yap, ybp, y = _mixer_core_fwd(proj, xin, mod, rows, cwf, wr, wi, wa_f, wb_f, wo_f, l, tm)
        wup_f, wdn_f = gathered(l, "mlp", x1)
        wdn_f = wdn_f.reshape(-1, d)
        if l + 1 < nl:
            rows = forward(l + 1, "in", wdn_f, rows)
        x2, ru, y2, h2 = _mlp_fwd(x1, mod, rows, wup_f, wdn_f, l, tmx)
        saved.append((xin, proj, h, x1, hs, yap, ybp, y, ru, y2, h2))
        weights.append((win_f, wa_f, wb_f, wo_f, wup_f, wdn_f))
        xin = x2
    loss_part, dx = _loss_fwd_bwd(xin, target, tm)

    scatters = {}
    small = [None] * nl
    gate_parts = [None] * nl

    def scatter(l, gname, parts, rows):
        scatters[l, gname] = _exchange_start(parts, False, f"scatter_start_{gname}{l}")
        return rows + scatters[l, gname][4][0, 0]

    for l in reversed(range(nl)):
        xin, proj, h, x1, hs, yap, ybp, y, ru, y2, h2 = saved[l]
        win_f, wa_f, wb_f, wo_f, wup_f, wdn_f = weights[l]
        dx1, dy2, dup, act, sm_mlp = _mlp_bwd(dx, x1, y2, ru, mod, rows, wup_f, wdn_f, l, tm)
        g_up = _wgrad(h2, dup, True, ts, dx1)
        g_dn = _wgrad(act, dy2, False, ts, g_up)
        rows = scatter(l, "mlp", [g_up, g_dn], rows)
        dproj, dy, m, dyap, dybp, ya, yb, sm_mix, dwg = _mixer_core_bwd(dx1, y, yap, ybp, hs, proj, mod, rows, cwf, wr, wi, wa_f, wb_f, wo_f, l, tm // 2)
        gate_parts[l] = _exchange_start([dwg.astype(BF16)], True, f"gates_start{l}")
        rows = rows + gate_parts[l][4][0, 0]
        g_a = _wgrad(ya, dyap, False, ts, gate_parts[l][4])
        g_b = _wgrad(yb, dybp, False, ts, g_a)
        g_o = _wgrad(m, dy, False, ts, g_b)
        rows = scatter(l, "mix", [g_a, g_b, g_o], rows)
        rows = scatter(l, "in", [_wgrad(h, dproj, True, ts, scatters[l, "mix"][4])], rows)
        dx, sm_in = _in_proj_bwd(dproj, xin, dx1, mod, rows, win_f, l, tmx)
        small[l] = jnp.concatenate([sm_mlp, sm_mix, sm_in], axis=0)
    grad_x = dx.reshape(x.shape)

    recv = {}
    big = {}
    moments = {"w_in": (w_in, m_w_in, v_w_in), "w_mlp_up": (w_mlp_up, m_w_mlp_up, v_w_mlp_up), "w_a_out": (w_a_out, m_w_a_out, v_w_a_out),
               "w_b_out": (w_b_out, m_w_b_out, v_w_b_out), "w_o": (w_o, m_w_o, v_w_o), "w_mlp_down": (w_mlp_down, m_w_mlp_down, v_w_mlp_down)}

    def collect(l, gname, after):
        for n, zone in zip(dict(groups)[gname], _exchange_wait(scatters[l, gname], after, False, f"scatter_wait_{gname}{l}")):
            recv[n, l] = zone

    def update(name, after):
        w, m_, v_ = moments[name]
        big[name] = _sum_adamw([recv[name, l] for l in range(nl)], w, m_, v_, min(256, w.shape[1]), after)
        return big[name][1]

    for l, gname in ((1, "mlp"), (1, "in"), (1, "mix"), (0, "mlp")):
        collect(l, gname, dx)
    early = update("w_mlp_down", update("w_mlp_up", dx))

    lrows = jnp.concatenate(small, axis=0)
    lrows = lrows.at[G_LOSS_ROW, 0].set(loss_part)

    def lrow(a, l, r):
        return a[l * G_LAYER_ROWS + r]

    dm = jnp.stack([jnp.concatenate([lrow(lrows, l, G_MLP_ROWS + G_MIX_ROWS + G_IN_SH), lrow(lrows, l, G_MLP_ROWS + G_MIX_ROWS + G_IN_SC),
                                     lrow(lrows, l, G_MLP_ROWS + G_MIX_GT), lrow(lrows, l, G_MLP_SH), lrow(lrows, l, G_MLP_SC),
                                     lrow(lrows, l, G_MLP_GT)]) for l in range(nl)])
    dm8 = jnp.concatenate([dm[:, None, :], jnp.zeros((nl, SUBLANES - 1, N_MOD * d), F32)], axis=1)
    srows, g_w_mod = _reduce_small(lrows, dm8, cact, early)
    loss = srows[G_LOSS_ROW, 0]
    sgates = _sum_gathered([_exchange_wait(gate_parts[l], srows, True, f"gates_wait{l}")[0] for l in range(nl)])
    for gname in ("in", "mix"):
        collect(0, gname, srows)
    for name in ("w_in", "w_a_out", "w_b_out", "w_o"):
        update(name, srows)

    def srow(l, r):
        return lrow(srows, l, r)

    def per_layer(r):
        return jnp.stack([srow(l, r) for l in range(nl)])

    mix0 = G_MLP_ROWS
    in0 = G_MLP_ROWS + G_MIX_ROWS
    g_b_mod = jnp.stack([jnp.concatenate([srow(l, in0 + G_IN_SH), srow(l, in0 + G_IN_SC), srow(l, mix0 + G_MIX_GT),
                                          srow(l, G_MLP_SH), srow(l, G_MLP_SC), srow(l, G_MLP_GT)]) for l in range(nl)])
    conv_a_full = jnp.stack([jnp.stack([srow(l, mix0 + G_MIX_CAW + k) for k in range(3)]) for l in range(nl)])
    conv_b_full = jnp.stack([jnp.stack([srow(l, mix0 + G_MIX_CBW + k) for k in range(4)]) for l in range(nl)])
    grads = {
        "b_mod": g_b_mod,
        "g_pre_mix": per_layer(in0 + G_IN_GPRE),
        "g_post_mix": per_layer(mix0 + G_MIX_GPOST),
        "conv_a_w": lax.dynamic_slice_in_dim(conv_a_full, me * cwid, cwid, axis=2),
        "conv_a_b": per_layer(mix0 + G_MIX_CAB),
        "conv_b_w": lax.dynamic_slice_in_dim(conv_b_full, me * cwid, cwid, axis=2),
        "conv_b_b": per_layer(mix0 + G_MIX_CBB),
        "w_gate_r": sgates[:, 0],
        "b_gate_r": per_layer(mix0 + G_MIX_BR),
        "w_gate_i": sgates[:, 1],
        "b_gate_i": per_layer(mix0 + G_MIX_BI),
        "lru_lambda": per_layer(mix0 + G_MIX_LAM),
        "g_pre_mlp": per_layer(G_MLP_GPRE),
        "g_post_mlp": per_layer(G_MLP_GPOST),
    }
    params = {
        "b_mod": (b_mod, m_b_mod, v_b_mod), "g_pre_mix": (g_pre_mix, m_g_pre_mix, v_g_pre_mix), "g_post_mix": (g_post_mix, m_g_post_mix, v_g_post_mix),
        "conv_a_w": (conv_a_w, m_conv_a_w, v_conv_a_w), "conv_a_b": (conv_a_b, m_conv_a_b, v_conv_a_b),
        "conv_b_w": (conv_b_w, m_conv_b_w, v_conv_b_w), "conv_b_b": (conv_b_b, m_conv_b_b, v_conv_b_b),
        "w_gate_r": (w_gate_r, m_w_gate_r, v_w_gate_r), "b_gate_r": (b_gate_r, m_b_gate_r, v_b_gate_r),
        "w_gate_i": (w_gate_i, m_w_gate_i, v_w_gate_i), "b_gate_i": (b_gate_i, m_b_gate_i, v_b_gate_i),
        "lru_lambda": (lru_lambda, m_lru_lambda, v_lru_lambda), "g_pre_mlp": (g_pre_mlp, m_g_pre_mlp, v_g_pre_mlp),
        "g_post_mlp": (g_post_mlp, m_g_post_mlp, v_g_post_mlp),
    }
    out = {}
    for name, g in grads.items():
        w, m_, v_ = params[name]
        flat = (-1, w.shape[-1])
        dl, nm, nv = _adamw(w.reshape(flat), g.reshape(flat), m_.reshape(flat), v_.reshape(flat))
        out[name] = (g.reshape(w.shape), dl.reshape(w.shape), nm.reshape(w.shape), nv.reshape(w.shape))
    out["w_mod"] = (g_w_mod,) + tuple(_adamw_tiled(w_mod, g_w_mod, m_w_mod, v_w_mod, min(128, d)))
    out.update(big)

    order = ("w_mod", "b_mod", "g_pre_mix", "g_post_mix", "w_in", "conv_a_w", "conv_a_b", "w_a_out", "conv_b_w", "conv_b_b", "w_gate_r", "b_gate_r",
             "w_gate_i", "b_gate_i", "lru_lambda", "w_b_out", "w_o", "g_pre_mlp", "g_post_mlp", "w_mlp_up", "w_mlp_down")
    return (loss, grad_x) + tuple(out[n][0] for n in order) + tuple(out[n][1] for n in order) + tuple(out[n][2] for n in order) + tuple(out[n][3] for n in order)
```

```python
import functools

import jax
import jax.numpy as jnp
from jax import lax
from jax.experimental import pallas as pl
from jax.experimental.pallas import tpu as pltpu

F32, BF16 = jnp.float32, jnp.bfloat16
EPS = 1e-6
LRU_C = 8.0
N_DEV = 8
N_MOD = 6
SUBLANES = 8
VMEM_BUDGET = 56 * 1024 * 1024
WGRAD_ROWS = 512
ADAM_LR, ADAM_B1, ADAM_B2, ADAM_EPS, ADAM_WD, ADAM_STEP = 0.001, 0.9, 0.999, 1e-08, 0.01, 10
MESH = pl.DeviceIdType.MESH
VMEM_SPEC = pl.BlockSpec(memory_space=pltpu.VMEM)
ANY_SPEC = pl.BlockSpec(memory_space=pl.ANY)
HBM_SPEC = pl.BlockSpec(memory_space=pltpu.HBM)
SEM_SPEC = pl.BlockSpec(memory_space=pltpu.SEMAPHORE)
SIDE_EFFECT = pltpu.SideEffectType.DATAFLOW_SIDE_EFFECTING

R_G_PRE_MIX, R_G_POST_MIX, R_CONV_A_B, R_CONV_B_B, R_B_GATE_R, R_B_GATE_I, R_LAMBDA, R_G_PRE_MLP, R_G_POST_MLP = range(9)
N_ROWS = 16
M_SH_M, M_SC_M, M_GT_M, M_SH_F, M_SC_F, M_GT_F = range(6)
CW_A, CW_B, CW_ROWS = 0, 3, 8
G_MLP_GT, G_MLP_GPOST, G_MLP_SC, G_MLP_SH, G_MLP_GPRE, G_LOSS_ROW, G_MLP_ROWS = 0, 1, 2, 3, 4, 7, 8
(G_MIX_GT, G_MIX_GPOST, G_MIX_CAB, G_MIX_CAW, G_MIX_CBB, G_MIX_CBW, G_MIX_BR, G_MIX_BI, G_MIX_LAM) = 0, 1, 2, 3, 6, 7, 11, 12, 13
G_MIX_ROWS = 16
G_IN_SC, G_IN_SH, G_IN_GPRE, G_IN_ROWS = 0, 1, 2, 8
G_LAYER_ROWS = G_MLP_ROWS + G_MIX_ROWS + G_IN_ROWS


def _cparams(dims=None, vmem=None):
    kw = {}
    if dims is not None:
        kw["dimension_semantics"] = dims
    if vmem is not None:
        kw["vmem_limit_bytes"] = int(min(max(vmem, 16 * 1024 * 1024), VMEM_BUDGET))
    return pltpu.CompilerParams(**kw)


def _nbytes(shape, dtype):
    n = 1
    for s in shape:
        n *= s
    return n * jnp.dtype(dtype).itemsize


def _hbm(*arrays):
    return tuple(pltpu.with_memory_space_constraint(a, pltpu.HBM) for a in arrays)


def _resident(block, index_map):
    return pl.BlockSpec(block, index_map, pipeline_mode=pl.Buffered(1))


def _my_position():
    x, y, c = lax.axis_index("x"), lax.axis_index("y"), lax.axis_index("c")
    return (x, y, c), 4 * x + 2 * y + c


def _peer(pos, k):
    x, y, c = pos
    px = 1 - x if k & 4 else x
    py = 1 - y if k & 2 else y
    pc = 1 - c if k & 1 else c
    return (px, py, pc), 4 * px + 2 * py + pc


def _remote(src, dst, ssem, rsem, peer):
    return pltpu.make_async_remote_copy(src_ref=src, dst_ref=dst, send_sem=ssem, recv_sem=rsem, device_id=peer, device_id_type=MESH)


def _dot(a, b):
    return jnp.dot(a, b, preferred_element_type=F32)


def _dot_nt(a, b):
    return lax.dot_general(a, b, (((1,), (1,)), ((), ())), preferred_element_type=F32)


def _dot_tn(a, b):
    return lax.dot_general(a, b, (((0,), (0,)), ((), ())), preferred_element_type=F32)


def _colsum(v):
    return jnp.sum(v, axis=0, keepdims=True)


def _sigmoid(v):
    return jax.nn.sigmoid(v)


GELU_K, GELU_C = 0.7978845608028654, 0.044715


def _gelu(v):
    s = 1.0 / (1.0 + jnp.exp(v * (-2.0 * GELU_K - (2.0 * GELU_K * GELU_C) * (v * v))))
    return v * s, s


def _gelu_grad(v, s):
    return s * (1.0 + (v * (1.0 - s)) * (2.0 * GELU_K + (6.0 * GELU_K * GELU_C) * (v * v)))


def _neg_expm1_twice(v):
    t = jnp.tanh(v)
    return (-2.0 * t) / (1.0 - t)


def _softplus_neg(lam):
    z = -lam
    u = jnp.exp(-jnp.abs(z))
    w = 1.0 + u
    l1p = jnp.where(w == 1.0, u, jnp.log(w) * u / (w - 1.0))
    return jnp.maximum(z, 0.0) + l1p


def _rms(v):
    return lax.rsqrt(jnp.mean(v * v, axis=-1, keepdims=True) + EPS)


def _prenorm_bwd(xv, dh, g, sc):
    r = _rms(xv)
    xn = xv * r
    n = xn * g
    dsc = _colsum(dh * n)
    dsh = _colsum(dh)
    dn = dh * (1.0 + sc)
    dg = _colsum(dn * xn)
    dxn = dn * g
    dx = r * (dxn - xn * jnp.mean(dxn * xn, axis=-1, keepdims=True))
    return dx, dsc, dsh, dg


def _postnorm_bwd(yv, dout, g, gt):
    r = _rms(yv)
    yn = yv * r
    dgt = _colsum(dout * (yn * g))
    dn = dout * gt
    dg = _colsum(dn * yn)
    dyn = dn * g
    dy = r * (dyn - yn * jnp.mean(dyn * yn, axis=-1, keepdims=True))
    return dy, dgt, dg


def _gates(xc, wr_ref, wi_ref, b_r, b_i, sp, nh, bw):
    xcb = xc.astype(BF16)
    zr = jnp.concatenate([_dot(xcb[:, h * bw:(h + 1) * bw], wr_ref[h]) for h in range(nh)], axis=1) + b_r
    zi = jnp.concatenate([_dot(xcb[:, h * bw:(h + 1) * bw], wi_ref[h]) for h in range(nh)], axis=1) + b_i
    r = _sigmoid(zr)
    ig = _sigmoid(zi)
    la = (-LRU_C * r) * sp
    a = jnp.exp(la)
    mult = jnp.sqrt(_neg_expm1_twice(la))
    return xcb, r, ig, a, mult


def _shift_rows(cur, edge, k, up):
    t, dd = cur.shape
    blocks = cur.reshape(t // SUBLANES, SUBLANES, dd)
    row = lax.broadcasted_iota(jnp.int32, (1, SUBLANES, dd), 1)
    if up:
        r = pltpu.roll(blocks, SUBLANES - k, 1)
        nxt = jnp.concatenate([r[1:], pltpu.roll(edge, SUBLANES - k, 0)[None]], axis=0)
        out = jnp.where(row >= SUBLANES - k, nxt, r)
    else:
        r = pltpu.roll(blocks, k, 1)
        prv = jnp.concatenate([pltpu.roll(edge, k, 0)[None], r[:-1]], axis=0)
        out = jnp.where(row < k, prv, r)
    return out.reshape(t, dd)


def _scan_block(a8, b8, reverse):
    row = lax.broadcasted_iota(jnp.int32, a8.shape, 0)
    for s in (1, 2, 4):
        if reverse:
            keep = row < SUBLANES - s
            a_sh = pltpu.roll(a8, SUBLANES - s, 0)
            b_sh = pltpu.roll(b8, SUBLANES - s, 0)
        else:
            keep = row >= s
            a_sh = pltpu.roll(a8, s, 0)
            b_sh = pltpu.roll(b8, s, 0)
        b8 = b8 + a8 * jnp.where(keep, b_sh, 0.0)
        a8 = a8 * jnp.where(keep, a_sh, 1.0)
    return a8, b8


def _prep_small(c, w_mod, b_mod, cw):
    d = c.shape[1]
    cm = w_mod.shape[2]
    cwid = cw.shape[2]
    nl = w_mod.shape[0]

    def body(c_ref, wm_ref, bm_ref, cw_ref, mod_ref, cact_ref, cwf_ref, cbuf, pbuf, rbuf, ssem, rsem, lsem):
        pos, me = _my_position()
        me8 = pl.multiple_of(me * SUBLANES, SUBLANES)
        cbuf[pl.ds(me8, SUBLANES), :] = jnp.broadcast_to(c_ref[...], (SUBLANES, d))
        own_cw = pltpu.make_async_copy(cw_ref, cwf_ref.at[:, :, pl.ds(me * cwid, cwid)], lsem.at[0])
        own_cw.start()
        first = []
        for k in range(1, N_DEV):
            peer, _ = _peer(pos, k)
            rows = cbuf.at[pl.ds(me8, SUBLANES), :]
            first.append(_remote(rows, rows, ssem.at[0, k - 1], rsem.at[0, k - 1], peer))
            first.append(_remote(cw_ref, cwf_ref.at[:, :, pl.ds(me * cwid, cwid)], ssem.at[1, k - 1], rsem.at[1, k - 1], peer))
        for cp in first:
            cp.start()
        for k in range(1, N_DEV):
            peer, pj = _peer(pos, k)
            pj8 = pl.multiple_of(pj * SUBLANES, SUBLANES)
            rows = cbuf.at[pl.ds(pj8, SUBLANES), :]
            _remote(rows, rows, ssem.at[0, k - 1], rsem.at[0, k - 1], peer).wait_recv()
        cv = cbuf[...]
        cact = cv * _sigmoid(cv)
        cact_ref[...] = cact
        cb = cact.astype(BF16)
        for l in range(nl):
            pbuf[l] = _dot(cb, wm_ref[l].astype(BF16))
        own_p = pltpu.make_async_copy(pbuf.at[:, pl.ds(me8, SUBLANES), :], rbuf.at[me], lsem.at[1])
        own_p.start()
        second = []
        for k in range(1, N_DEV):
            peer, pj = _peer(pos, k)
            pj8 = pl.multiple_of(pj * SUBLANES, SUBLANES)
            second.append(_remote(pbuf.at[:, pl.ds(pj8, SUBLANES), :], rbuf.at[me], ssem.at[2, k - 1], rsem.at[2, k - 1], peer))
        for cp in second:
            cp.start()
        for k in range(1, N_DEV):
            peer, pj = _peer(pos, k)
            _remote(pbuf.at[:, pl.ds(0, SUBLANES), :], rbuf.at[pj], ssem.at[2, k - 1], rsem.at[2, k - 1], peer).wait_recv()
            _remote(cw_ref, cwf_ref.at[:, :, pl.ds(pj * cwid, cwid)], ssem.at[1, k - 1], rsem.at[1, k - 1], peer).wait_recv()
        own_p.wait()
        own_cw.wait()
        for l in range(nl):
            for j in range(N_DEV):
                mod_ref[l:l + 1, j * cm:(j + 1) * cm] = rbuf[j, l, 0:1, :] + bm_ref[l:l + 1, j * cm:(j + 1) * cm]
        for cp in first + second:
            cp.wait_send()

    return pl.pallas_call(
        body,
        name="prep_small",
        out_shape=(
            jax.ShapeDtypeStruct((nl, N_MOD * d), F32),
            jax.ShapeDtypeStruct((N_DEV * SUBLANES, d), F32),
            jax.ShapeDtypeStruct((nl, CW_ROWS, d), F32),
        ),
        in_specs=[VMEM_SPEC] * 4,
        out_specs=(VMEM_SPEC,) * 3,
        scratch_shapes=[
            pltpu.VMEM((N_DEV * SUBLANES, d), F32),
            pltpu.VMEM((nl, N_DEV * SUBLANES, cm), F32),
            pltpu.VMEM((N_DEV, nl, SUBLANES, cm), F32),
            pltpu.SemaphoreType.DMA((3, N_DEV - 1)),
            pltpu.SemaphoreType.DMA((3, N_DEV - 1)),
            pltpu.SemaphoreType.DMA((2,)),
        ],
        compiler_params=_cparams(vmem=3 * _nbytes(w_mod.shape, F32)),
    )(c, w_mod, b_mod, cw)


def _exchange_start(parts, gather, name):
    n = len(parts)
    lands = [lax.empty(((N_DEV,) + tuple(p.shape)) if gather else tuple(p.shape), p.dtype) for p in parts]

    def body(*refs):
        ins, lnd = refs[:n], refs[n:2 * n]
        ssem, rsem, token = refs[2 * n], refs[2 * n + 1], refs[-1]
        pos, me = _my_position()
        for k in range(1, N_DEV):
            peer, pj = _peer(pos, k)
            for t in range(n):
                src = ins[t] if gather else ins[t].at[pj]
                q = t * (N_DEV - 1) + k - 1
                _remote(src, lnd[t].at[me], ssem.at[q], rsem.at[q], peer).start()
        token[...] = jnp.zeros(token.shape, F32)

    out = pl.pallas_call(
        body,
        name=name,
        out_shape=(pltpu.SemaphoreType.DMA((n * (N_DEV - 1),)), pltpu.SemaphoreType.DMA((n * (N_DEV - 1),)))
        + tuple(pltpu.HBM(p.shape, p.dtype) for p in parts) + tuple(pltpu.HBM(p.shape, p.dtype) for p in lands)
        + (jax.ShapeDtypeStruct((SUBLANES, 128), F32),),
        in_specs=[HBM_SPEC] * (2 * n),
        out_specs=(SEM_SPEC, SEM_SPEC) + (HBM_SPEC,) * (2 * n) + (VMEM_SPEC,),
        input_output_aliases={i: 2 + i for i in range(2 * n)},
        compiler_params=pltpu.CompilerParams(has_side_effects=SIDE_EFFECT),
    )(*[pltpu.with_memory_space_constraint(p, pltpu.HBM) for p in list(parts) + lands])
    return out[0], out[1], list(out[2:2 + n]), list(out[2 + n:2 + 2 * n]), out[-1]


def _exchange_wait(started, after, gather, name):
    ssem, rsem, parts, lands, _ = started
    n = len(parts)

    def body(*refs):
        ins, lnd = refs[:n], refs[n:2 * n]
        ssem_ref, rsem_ref = refs[2 * n], refs[2 * n + 1]
        stage, lsem = refs[-1 - n:-1], refs[-1]
        pos, me = _my_position()
        load = []
        for t in range(n):
            src = ins[t] if gather else ins[t].at[me]
            load.append(pltpu.make_async_copy(src, stage[t], lsem.at[t]))
            load[-1].start()
        store = []
        for t in range(n):
            load[t].wait()
            store.append(pltpu.make_async_copy(stage[t], lnd[t].at[me], lsem.at[t]))
            store[-1].start()
        for k in range(1, N_DEV):
            peer, pj = _peer(pos, k)
            for t in range(n):
                src = ins[t] if gather else ins[t].at[pj]
                q = t * (N_DEV - 1) + k - 1
                _remote(src, lnd[t].at[me], ssem_ref.at[q], rsem_ref.at[q], peer).wait_send()
                _remote(src, lnd[t].at[pj], ssem_ref.at[q], rsem_ref.at[q], peer).wait_recv()
        for cp in store:
            cp.wait()

    out = pl.pallas_call(
        body,
        name=name,
        out_shape=tuple(pltpu.HBM(p.shape, p.dtype) for p in parts) + tuple(pltpu.HBM(p.shape, p.dtype) for p in lands),
        in_specs=[HBM_SPEC] * (2 * n) + [SEM_SPEC, SEM_SPEC, ANY_SPEC],
        out_specs=(HBM_SPEC,) * (2 * n),
        input_output_aliases={i: i for i in range(2 * n)},
        scratch_shapes=[pltpu.VMEM(tuple(z.shape[1:]), z.dtype) for z in lands] + [pltpu.SemaphoreType.DMA((n,))],
        compiler_params=pltpu.CompilerParams(has_side_effects=SIDE_EFFECT),
    )(*parts, *lands, ssem, rsem, after)
    return list(out[n:])


CHIP_PEERS = 3


def _chip_peer(pos, k, core):
    x, y, _ = pos
    px = 1 - x if k & 2 else x
    py = 1 - y if k & 1 else y
    return (px, py, core), 4 * px + 2 * py + core


def _gather2_start(parts, after, name):
    n = len(parts)
    lands = [lax.empty((N_DEV,) + tuple(p.shape), p.dtype) for p in parts]
    per = CHIP_PEERS + 1

    def body(*refs):
        ins, lnd = refs[:n], refs[n:2 * n]
        ssem, rsem, token = refs[2 * n + 1], refs[2 * n + 2], refs[-1]
        pos, me = _my_position()
        sibling = (pos[0], pos[1], 1 - pos[2])
        for t in range(n):
            _remote(ins[t], lnd[t].at[me], ssem.at[per * t], rsem.at[per * t], sibling).start()
        for k in range(1, per):
            peer, _ = _chip_peer(pos, k, pos[2])
            for t in range(n):
                _remote(ins[t], lnd[t].at[me], ssem.at[per * t + k], rsem.at[per * t + k], peer).start()
        token[...] = jnp.zeros(token.shape, F32)

    out = pl.pallas_call(
        body,
        name=name,
        out_shape=(pltpu.SemaphoreType.DMA((n * per,)), pltpu.SemaphoreType.DMA((n * per,)))
        + tuple(pltpu.HBM(p.shape, p.dtype) for p in parts) + tuple(pltpu.HBM(p.shape, p.dtype) for p in lands)
        + (jax.ShapeDtypeStruct((SUBLANES, 128), F32),),
        in_specs=[HBM_SPEC] * (2 * n) + [ANY_SPEC],
        out_specs=(SEM_SPEC, SEM_SPEC) + (HBM_SPEC,) * (2 * n) + (VMEM_SPEC,),
        input_output_aliases={i: 2 + i for i in range(2 * n)},
        compiler_params=pltpu.CompilerParams(has_side_effects=SIDE_EFFECT),
    )(*[pltpu.with_memory_space_constraint(p, pltpu.HBM) for p in list(parts) + lands], after)
    return out[0], out[1], list(out[2:2 + n]), list(out[2 + n:2 + 2 * n]), out[-1]


def _gather2_forward(started, after, name):
    ssem, rsem, parts, lands, _ = started
    n = len(lands)
    per = CHIP_PEERS + 1

    def body(*refs):
        lnd = refs[:n]
        rsem_a, fsend, frecv, token = refs[n], refs[2 * n + 2], refs[2 * n + 3], refs[2 * n + 4]
        pos, _ = _my_position()
        sibling = (pos[0], pos[1], 1 - pos[2])
        for k in range(1, per):
            peer, pk = _chip_peer(pos, k, pos[2])
            for t in range(n):
                block = lnd[t].at[pk]
                _remote(block, block, rsem_a.at[per * t + k], rsem_a.at[per * t + k], peer).wait_recv()
                q = CHIP_PEERS * t + k - 1
                _remote(block, block, fsend.at[q], frecv.at[q], sibling).start()
        token[...] = jnp.zeros(token.shape, F32)

    out = pl.pallas_call(
        body,
        name=name,
        out_shape=tuple(pltpu.HBM(p.shape, p.dtype) for p in lands)
        + (pltpu.SemaphoreType.DMA((n * CHIP_PEERS,)), pltpu.SemaphoreType.DMA((n * CHIP_PEERS,)),
           jax.ShapeDtypeStruct((SUBLANES, 128), F32)),
        in_specs=[HBM_SPEC] * n + [SEM_SPEC, ANY_SPEC],
        out_specs=(HBM_SPEC,) * n + (SEM_SPEC, SEM_SPEC, VMEM_SPEC),
        input_output_aliases={i: i for i in range(n)},
        compiler_params=pltpu.CompilerParams(has_side_effects=SIDE_EFFECT),
    )(*lands, rsem, after)
    return ssem, rsem, parts, list(out[:n]), out[n + 2], out[n], out[n + 1]


def _gather2_wait(forwarded, after, name):
    ssem, rsem, parts, lands, _, fsend, frecv = forwarded
    n = len(parts)
    per = CHIP_PEERS + 1

    def body(*refs):
        ins, lnd = refs[:n], refs[n:2 * n]
        ssem_a, rsem_a, fs, fr = refs[2 * n:2 * n + 4]
        stage, lsem = refs[-1 - n:-1], refs[-1]
        pos, me = _my_position()
        sibling = (pos[0], pos[1], 1 - pos[2])
        sib = 4 * pos[0] + 2 * pos[1] + 1 - pos[2]
        load = []
        for t in range(n):
            load.append(pltpu.make_async_copy(ins[t], stage[t], lsem.at[t]))
            load[-1].start()
        store = []
        for t in range(n):
            load[t].wait()
            store.append(pltpu.make_async_copy(stage[t], lnd[t].at[me], lsem.at[t]))
            store[-1].start()
        for t in range(n):
            _remote(ins[t], lnd[t].at[me], ssem_a.at[per * t], rsem_a.at[per * t], sibling).wait_send()
            _remote(ins[t], lnd[t].at[sib], ssem_a.at[per * t], rsem_a.at[per * t], sibling).wait_recv()
        for k in range(1, per):
            peer, pk = _chip_peer(pos, k, pos[2])
            _, qk = _chip_peer(pos, k, 1 - pos[2])
            for t in range(n):
                q = CHIP_PEERS * t + k - 1
                _remote(ins[t], lnd[t].at[me], ssem_a.at[per * t + k], rsem_a.at[per * t + k], peer).wait_send()
                _remote(lnd[t].at[pk], lnd[t].at[pk], fs.at[q], fr.at[q], sibling).wait_send()
                _remote(lnd[t].at[qk], lnd[t].at[qk], fs.at[q], fr.at[q], sibling).wait_recv()
        for cp in store:
            cp.wait()

    out = pl.pallas_call(
        body,
        name=name,
        out_shape=tuple(pltpu.HBM(p.shape, p.dtype) for p in parts) + tuple(pltpu.HBM(p.shape, p.dtype) for p in lands),
        in_specs=[HBM_SPEC] * (2 * n) + [SEM_SPEC] * 4 + [ANY_SPEC],
        out_specs=(HBM_SPEC,) * (2 * n),
        input_output_aliases={i: i for i in range(2 * n)},
        scratch_shapes=[pltpu.VMEM(tuple(z.shape[1:]), z.dtype) for z in lands] + [pltpu.SemaphoreType.DMA((n,))],
        compiler_params=pltpu.CompilerParams(has_side_effects=SIDE_EFFECT),
    )(*parts, *lands, ssem, rsem, fsend, frecv, after)
    return list(out[n:])


def _reduce_small(rows, dm8, cact, after):
    r, d = rows.shape
    nl = dm8.shape[0]
    cm = dm8.shape[2] // N_DEV

    def body(rows_ref, dm_ref, cact_ref, after_ref, orow_ref, owm_ref, gr, dmr, ssem, rsem, lsem):
        pos, me = _my_position()
        me8 = pl.multiple_of(me * SUBLANES, SUBLANES)
        gr[me] = rows_ref[...]
        own_dm = pltpu.make_async_copy(dm_ref.at[:, :, pl.ds(me * cm, cm)], dmr.at[:, pl.ds(me8, SUBLANES), :], lsem.at[0])
        own_dm.start()
        sends = []
        for k in range(1, N_DEV):
            peer, pj = _peer(pos, k)
            sends.append(_remote(gr.at[me], gr.at[me], ssem.at[0, k - 1], rsem.at[0, k - 1], peer))
            sends.append(_remote(dm_ref.at[:, :, pl.ds(pj * cm, cm)], dmr.at[:, pl.ds(me8, SUBLANES), :],
                                 ssem.at[1, k - 1], rsem.at[1, k - 1], peer))
        for cp in sends:
            cp.start()
        for k in range(1, N_DEV):
            peer, pj = _peer(pos, k)
            pj8 = pl.multiple_of(pj * SUBLANES, SUBLANES)
            _remote(gr.at[pj], gr.at[pj], ssem.at[0, k - 1], rsem.at[0, k - 1], peer).wait_recv()
            _remote(dm_ref.at[:, :, pl.ds(0, cm)], dmr.at[:, pl.ds(pj8, SUBLANES), :], ssem.at[1, k - 1], rsem.at[1, k - 1], peer).wait_recv()
        own_dm.wait()
        acc = gr[0]
        for j in range(1, N_DEV):
            acc = acc + gr[j]
        orow_ref[...] = acc
        cb = cact_ref[...].astype(BF16)
        for l in range(nl):
            owm_ref[l] = _dot_tn(cb, dmr[l].astype(BF16))
        for cp in sends:
            cp.wait_send()

    return pl.pallas_call(
        body,
        name="reduce_small",
        out_shape=(jax.ShapeDtypeStruct((r, d), F32), jax.ShapeDtypeStruct((nl, d, cm), F32)),
        in_specs=[VMEM_SPEC] * 3 + [ANY_SPEC],
        out_specs=(VMEM_SPEC,) * 2,
        scratch_shapes=[
            pltpu.VMEM((N_DEV, r, d), F32),
            pltpu.VMEM((nl, N_DEV * SUBLANES, cm), F32),
            pltpu.SemaphoreType.DMA((2, N_DEV - 1)),
            pltpu.SemaphoreType.DMA((2, N_DEV - 1)),
            pltpu.SemaphoreType.DMA((1,)),
        ],
        compiler_params=_cparams(vmem=4 * _nbytes((N_DEV, r, d), F32) + 6 * _nbytes((nl, d, cm), F32)),
    )(rows, dm8, cact, after)


def _sum_gathered(zones):
    nl = len(zones)

    def body(*refs):
        for l in range(nl):
            acc = refs[l][0].astype(F32)
            for j in range(1, N_DEV):
                acc = acc + refs[l][j].astype(F32)
            refs[nl][l] = acc

    return pl.pallas_call(
        body,
        name="sum_gathered",
        out_shape=jax.ShapeDtypeStruct((nl,) + tuple(zones[0].shape[1:]), F32),
        in_specs=[VMEM_SPEC] * nl,
        out_specs=VMEM_SPEC,
        compiler_params=_cparams(vmem=8 * nl * _nbytes(zones[0].shape, BF16)),
    )(*zones)


def _in_proj_fwd(x, mod, rows, win_f, l, tm):
    s, d = x.shape
    nb, _, ci = win_f.shape

    def body(x_ref, mod_ref, rows_ref, w_ref, proj_ref, h_ref):
        xv = x_ref[...]
        g = rows_ref[R_G_PRE_MIX:R_G_PRE_MIX + 1, :]
        h = (xv * _rms(xv) * g) * (1.0 + mod_ref[M_SC_M:M_SC_M + 1, :]) + mod_ref[M_SH_M:M_SH_M + 1, :]
        hb = h.astype(BF16)
        h_ref[...] = hb
        for j in range(nb):
            proj_ref[:, j * ci:(j + 1) * ci] = _dot(hb, w_ref[j])

    return pl.pallas_call(
        body,
        name="in_proj_fwd",
        grid=(s // tm,),
        in_specs=[
            pl.BlockSpec((tm, d), lambda i: (i, 0)),
            _resident((None, N_MOD, d), lambda i: (l, 0, 0)),
            _resident((None, N_ROWS, d), lambda i: (l, 0, 0)),
            _resident((nb, d, ci), lambda i: (0, 0, 0)),
        ],
        out_specs=(pl.BlockSpec((tm, nb * ci), lambda i: (i, 0)), pl.BlockSpec((tm, d), lambda i: (i, 0))),
        out_shape=(jax.ShapeDtypeStruct((s, nb * ci), F32), jax.ShapeDtypeStruct((s, d), BF16)),
        compiler_params=_cparams(("parallel",), _nbytes((nb, d, ci), BF16) + 3 * _nbytes((tm, nb * ci), F32) + 8 * _nbytes((tm, d), F32)),
    )(*_hbm(x), mod, rows, win_f)


def _mixer_core_fwd(proj, x, mod, rows, cwf, wr, wi, wa_f, wb_f, wo_f, l, tm):
    s, d = x.shape
    nh, bw, _ = wr.shape[1:]

    def body(proj_ref, x_ref, mod_ref, rows_ref, cw_ref, wr_ref, wi_ref, wa_ref, wb_ref, wo_ref,
             x1_ref, hs_ref, yap_ref, ybp_ref, y_ref, cvbuf, xbbuf, a_s, b_s, hprev):
        i = pl.program_id(0)

        @pl.when(i == 0)
        def _():
            cvbuf[...] = jnp.zeros((SUBLANES, d), F32)
            xbbuf[...] = jnp.zeros((SUBLANES, d), F32)
            hprev[...] = jnp.zeros((SUBLANES, d), F32)

        def row(r):
            return rows_ref[r:r + 1, :]

        def tap(r):
            return cw_ref[r:r + 1, :]

        ba = proj_ref[:, 0:d]
        cv = proj_ref[:, d:2 * d] * proj_ref[:, 2 * d:3 * d]
        cvt = cvbuf[...]
        conv3 = ((row(R_CONV_A_B) + _shift_rows(cv, cvt, 2, False) * tap(CW_A)) + _shift_rows(cv, cvt, 1, False) * tap(CW_A + 1)) + cv * tap(CW_A + 2)
        ya = ba * conv3
        cvbuf[...] = cv[tm - SUBLANES:, :]
        xb = proj_ref[:, 3 * d:4 * d]
        xbt = xbbuf[...]
        xc = (((row(R_CONV_B_B) + _shift_rows(xb, xbt, 3, False) * tap(CW_B)) + _shift_rows(xb, xbt, 2, False) * tap(CW_B + 1))
              + _shift_rows(xb, xbt, 1, False) * tap(CW_B + 2)) + xb * tap(CW_B + 3)
        xbbuf[...] = xb[tm - SUBLANES:, :]
        sp = _softplus_neg(row(R_LAMBDA))
        _, _, ig, a, mult = _gates(xc, wr_ref, wi_ref, row(R_B_GATE_R), row(R_B_GATE_I), sp, nh, bw)
        a_s[...] = a
        b_s[...] = mult * (ig * xc)

        def blk(j, hp):
            o = pl.multiple_of(j * SUBLANES, SUBLANES)
            a8, b8 = _scan_block(a_s[pl.ds(o, SUBLANES), :], b_s[pl.ds(o, SUBLANES), :], reverse=False)
            h8 = b8 + a8 * hp
            hs_ref[pl.ds(o, SUBLANES), :] = h8
            return jnp.broadcast_to(h8[SUBLANES - 1:SUBLANES, :], (SUBLANES, d))

        hprev[...] = lax.fori_loop(0, tm // SUBLANES, blk, hprev[...])
        gel, _ = _gelu(proj_ref[:, 4 * d:5 * d])
        yb = hs_ref[...] * gel
        yap = _dot(ya.astype(BF16), wa_ref[...])
        ybp = _dot(yb.astype(BF16), wb_ref[...])
        yap_ref[...] = yap
        ybp_ref[...] = ybp
        m = _sigmoid(proj_ref[:, 5 * d:6 * d]) * yap + _sigmoid(proj_ref[:, 6 * d:7 * d]) * ybp
        y = _dot(m.astype(BF16), wo_ref[...])
        y_ref[...] = y
        x1_ref[...] = x_ref[...] + mod_ref[M_GT_M:M_GT_M + 1, :] * ((y * _rms(y)) * row(R_G_POST_MIX))

    tile = pl.BlockSpec((tm, d), lambda i: (i, 0))
    return pl.pallas_call(
        body,
        name="mixer_core_fwd",
        grid=(s // tm,),
        in_specs=[
            pl.BlockSpec((tm, 7 * d), lambda i: (i, 0)),
            tile,
            _resident((None, N_MOD, d), lambda i: (l, 0, 0)),
            _resident((None, N_ROWS, d), lambda i: (l, 0, 0)),
            _resident((None, CW_ROWS, d), lambda i: (l, 0, 0)),
            _resident((None, nh, bw, bw), lambda i: (l, 0, 0, 0)),
            _resident((None, nh, bw, bw), lambda i: (l, 0, 0, 0)),
            _resident((d, d), lambda i: (0, 0)),
            _resident((d, d), lambda i: (0, 0)),
            _resident((d, d), lambda i: (0, 0)),
        ],
        out_specs=(tile,) * 5,
        out_shape=(jax.ShapeDtypeStruct((s, d), F32),) * 5,
        scratch_shapes=[
            pltpu.VMEM((SUBLANES, d), F32),
            pltpu.VMEM((SUBLANES, d), F32),
            pltpu.VMEM((tm, d), F32),
            pltpu.VMEM((tm, d), F32),
            pltpu.VMEM((SUBLANES, d), F32),
        ],
        compiler_params=_cparams(("arbitrary",), 3 * _nbytes((d, d), BF16) + 2 * _nbytes((tm, 7 * d), F32) + 40 * _nbytes((tm, d), F32)),
    )(*_hbm(proj, x), mod, rows, cwf, wr, wi, wa_f, wb_f, wo_f)


def _mlp_fwd(x1, mod, rows, wup_f, wdn_f, l, tm):
    s, d = x1.shape
    nb, _, cu = wup_f.shape
    dff = nb * cu

    def body(x1_ref, mod_ref, rows_ref, wu_ref, wd_ref, x2_ref, ru_ref, y2_ref, h2_ref):
        xv = x1_ref[...]
        g = rows_ref[R_G_PRE_MLP:R_G_PRE_MLP + 1, :]
        h2 = ((xv * _rms(xv) * g) * (1.0 + mod_ref[M_SC_F:M_SC_F + 1, :]) + mod_ref[M_SH_F:M_SH_F + 1, :]).astype(BF16)
        h2_ref[...] = h2
        ru = jnp.concatenate([jnp.maximum(_dot(h2, wu_ref[j]), 0.0) for j in range(nb)], axis=1)
        ru_ref[...] = ru.astype(BF16)
        y2 = _dot((ru * ru).astype(BF16), wd_ref[...])
        y2_ref[...] = y2
        x2_ref[...] = xv + mod_ref[M_GT_F:M_GT_F + 1, :] * ((y2 * _rms(y2)) * rows_ref[R_G_POST_MLP:R_G_POST_MLP + 1, :])

    tile = pl.BlockSpec((tm, d), lambda i: (i, 0))
    wide = pl.BlockSpec((tm, dff), lambda i: (i, 0))
    return pl.pallas_call(
        body,
        name="mlp_fwd",
        grid=(s // tm,),
        in_specs=[
            tile,
            _resident((None, N_MOD, d), lambda i: (l, 0, 0)),
            _resident((None, N_ROWS, d), lambda i: (l, 0, 0)),
            _resident((nb, d, cu), lambda i: (0, 0, 0)),
            _resident((dff, d), lambda i: (0, 0)),
        ],
        out_specs=(tile, wide, tile, tile),
        out_shape=(jax.ShapeDtypeStruct((s, d), F32), jax.ShapeDtypeStruct((s, dff), BF16),
                   jax.ShapeDtypeStruct((s, d), F32), jax.ShapeDtypeStruct((s, d), BF16)),
        compiler_params=_cparams(("parallel",), 2 * _nbytes((dff, d), BF16) + 5 * _nbytes((tm, dff), F32) + 12 * _nbytes((tm, d), F32)),
    )(*_hbm(x1), mod, rows, wup_f, wdn_f)


def _loss_fwd_bwd(y, target, tm):
    s, d = y.shape

    def body(y_ref, t_ref, loss_ref, dy_ref):
        @pl.when(pl.program_id(0) == 0)
        def _():
            loss_ref[...] = jnp.zeros(loss_ref.shape, F32)

        e = y_ref[...] - t_ref[...]
        dy_ref[...] = e * (1.0 / d)
        loss_ref[...] += 0.5 * jnp.sum(jnp.mean(e * e, axis=-1, keepdims=True), axis=0, keepdims=True)

    tile = pl.BlockSpec((tm, d), lambda i: (i, 0))
    loss, dy = pl.pallas_call(
        body,
        name="loss",
        grid=(s // tm,),
        in_specs=[tile, tile],
        out_specs=(pl.BlockSpec((SUBLANES, 128), lambda i: (0, 0)), tile),
        out_shape=(jax.ShapeDtypeStruct((SUBLANES, 128), F32), jax.ShapeDtypeStruct((s, d), F32)),
        compiler_params=_cparams(("arbitrary",)),
    )(*_hbm(y, target))
    return loss[0, 0], dy


def _mlp_bwd(dx2, x1, y2, ru, mod, rows, wup_f, wdn_f, l, tm):
    s, d = x1.shape
    nb, _, cu = wup_f.shape
    dff = nb * cu

    def body(dx2_ref, x1_ref, y2_ref, ru_ref, mod_ref, rows_ref, wu_ref, wd_ref, dx1_ref, dy2_ref, dup_ref, act_ref, sm_ref):
        @pl.when(pl.program_id(0) == 0)
        def _():
            sm_ref[...] = jnp.zeros(sm_ref.shape, F32)

        dout = dx2_ref[...]
        dy2, dgt, dgpost = _postnorm_bwd(y2_ref[...], dout, rows_ref[R_G_POST_MLP:R_G_POST_MLP + 1, :], mod_ref[M_GT_F:M_GT_F + 1, :])
        dy2b = dy2.astype(BF16)
        dy2_ref[...] = dy2b
        ruv = ru_ref[...].astype(F32)
        act_ref[...] = (ruv * ruv).astype(BF16)
        dup = (_dot_nt(dy2b, wd_ref[...]) * (2.0 * ruv)).astype(BF16)
        dup_ref[...] = dup
        dh2 = _dot_nt(dup[:, 0:cu], wu_ref[0])
        for j in range(1, nb):
            dh2 = dh2 + _dot_nt(dup[:, j * cu:(j + 1) * cu], wu_ref[j])
        dxn, dsc, dsh, dgpre = _prenorm_bwd(x1_ref[...], dh2, rows_ref[R_G_PRE_MLP:R_G_PRE_MLP + 1, :], mod_ref[M_SC_F:M_SC_F + 1, :])
        dx1_ref[...] = dout + dxn
        for r, v in ((G_MLP_GT, dgt), (G_MLP_GPOST, dgpost), (G_MLP_SC, dsc), (G_MLP_SH, dsh), (G_MLP_GPRE, dgpre)):
            sm_ref[r:r + 1, :] += v

    tile = pl.BlockSpec((tm, d), lambda i: (i, 0))
    wide = pl.BlockSpec((tm, dff), lambda i: (i, 0))
    return pl.pallas_call(
        body,
        name="mlp_bwd",
        grid=(s // tm,),
        in_specs=[
            tile, tile, tile, wide,
            _resident((None, N_MOD, d), lambda i: (l, 0, 0)),
            _resident((None, N_ROWS, d), lambda i: (l, 0, 0)),
            _resident((nb, d, cu), lambda i: (0, 0, 0)),
            _resident((dff, d), lambda i: (0, 0)),
        ],
        out_specs=(tile, tile, wide, wide, pl.BlockSpec((G_MLP_ROWS, d), lambda i: (0, 0))),
        out_shape=(jax.ShapeDtypeStruct((s, d), F32), jax.ShapeDtypeStruct((s, d), BF16), jax.ShapeDtypeStruct((s, dff), BF16),
                   jax.ShapeDtypeStruct((s, dff), BF16), jax.ShapeDtypeStruct((G_MLP_ROWS, d), F32)),
        compiler_params=_cparams(("arbitrary",), 2 * _nbytes((dff, d), BF16) + 6 * _nbytes((tm, dff), F32) + 16 * _nbytes((tm, d), F32)),
    )(*_hbm(dx2, x1, y2, ru), mod, rows, wup_f, wdn_f)


def _mixer_core_bwd(dx1, y, yap, ybp, hs, proj, mod, rows, cwf, wr, wi, wa_f, wb_f, wo_f, l, tm):
    s, d = dx1.shape
    nh, bw, _ = wr.shape[1:]
    nt = s // tm
    per = tm // SUBLANES

    def body(dx1_ref, y_ref, yap_ref, ybp_ref, hs_ref, hsh_ref, proj_ref, projh_ref, mod_ref, rows_ref, cw_ref,
             wr_ref, wi_ref, wa_ref, wb_ref, wo_ref,
             dproj_ref, dy_ref, m_ref, dyap_ref, dybp_ref, ya_ref, yb_ref, sm_ref, dwg_ref,
             abuf, dcbuf, dxbuf, al_s, dh_s, lam_s, lnext):
        i = pl.program_id(0)
        first_tile = i == nt - 1

        @pl.when(i == 0)
        def _():
            sm_ref[...] = jnp.zeros(sm_ref.shape, F32)
            dwg_ref[...] = jnp.zeros(dwg_ref.shape, F32)
            zero = jnp.zeros((SUBLANES, d), F32)
            abuf[...] = zero
            dcbuf[...] = zero
            dxbuf[...] = zero
            lnext[...] = zero

        def row(r):
            return rows_ref[r:r + 1, :]

        def tap(r):
            return cw_ref[r:r + 1, :]

        def acc(r, v):
            sm_ref[r:r + 1, :] += v

        keep_halo = jnp.where(first_tile, 0.0, 1.0)
        dy, dgt, dgpost = _postnorm_bwd(y_ref[...], dx1_ref[...], row(R_G_POST_MIX), mod_ref[M_GT_M:M_GT_M + 1, :])
        acc(G_MIX_GT, dgt)
        acc(G_MIX_GPOST, dgpost)
        dyb16 = dy.astype(BF16)
        dy_ref[...] = dyb16
        dm = _dot_nt(dyb16, wo_ref[...])
        sa = _sigmoid(proj_ref[:, 5 * d:6 * d])
        sb = _sigmoid(proj_ref[:, 6 * d:7 * d])
        yap = yap_ref[...]
        ybp = ybp_ref[...]
        m_ref[...] = (sa * yap + sb * ybp).astype(BF16)
        dyap_f = dm * sa
        dybp_f = dm * sb
        dyap = dyap_f.astype(BF16)
        dybp = dybp_f.astype(BF16)
        dyap_ref[...] = dyap
        dybp_ref[...] = dybp
        dproj_ref[:, 5 * d:6 * d] = (dyap_f * yap * (1.0 - sa)).astype(BF16)
        dproj_ref[:, 6 * d:7 * d] = (dybp_f * ybp * (1.0 - sb)).astype(BF16)
        dya = _dot_nt(dyap, wa_ref[...])
        dyb = _dot_nt(dybp, wb_ref[...])
        ba = proj_ref[:, 0:d]
        ca = proj_ref[:, d:2 * d]
        va = proj_ref[:, 2 * d:3 * d]
        cv = ca * va
        cvh = keep_halo * (projh_ref[:, d:2 * d] * projh_ref[:, 2 * d:3 * d])
        cvm2 = _shift_rows(cv, cvh, 2, False)
        cvm1 = _shift_rows(cv, cvh, 1, False)
        conv3 = ((row(R_CONV_A_B) + cvm2 * tap(CW_A)) + cvm1 * tap(CW_A + 1)) + cv * tap(CW_A + 2)
        ya_ref[...] = (ba * conv3).astype(BF16)
        dproj_ref[:, 0:d] = (dya * conv3).astype(BF16)
        dc3 = dya * ba
        acc(G_MIX_CAB, _colsum(dc3))
        acc(G_MIX_CAW, _colsum(dc3 * cvm2))
        acc(G_MIX_CAW + 1, _colsum(dc3 * cvm1))
        acc(G_MIX_CAW + 2, _colsum(dc3 * cv))
        dct = dcbuf[...]
        dcv = (dc3 * tap(CW_A + 2) + _shift_rows(dc3, dct, 1, True) * tap(CW_A + 1)) + _shift_rows(dc3, dct, 2, True) * tap(CW_A)
        dcbuf[...] = dc3[:SUBLANES, :]
        dproj_ref[:, d:2 * d] = (dcv * va).astype(BF16)
        dproj_ref[:, 2 * d:3 * d] = (dcv * ca).astype(BF16)
        xb = proj_ref[:, 3 * d:4 * d]
        gb = proj_ref[:, 4 * d:5 * d]
        xbh = keep_halo * projh_ref[:, 3 * d:4 * d]
        xm3 = _shift_rows(xb, xbh, 3, False)
        xm2 = _shift_rows(xb, xbh, 2, False)
        xm1 = _shift_rows(xb, xbh, 1, False)
        xc = (((row(R_CONV_B_B) + xm3 * tap(CW_B)) + xm2 * tap(CW_B + 1)) + xm1 * tap(CW_B + 2)) + xb * tap(CW_B + 3)
        lam = row(R_LAMBDA)
        sp = _softplus_neg(lam)
        xcb, r, ig, a, mult = _gates(xc, wr_ref, wi_ref, row(R_B_GATE_R), row(R_B_GATE_I), sp, nh, bw)
        gel, gsig = _gelu(gb)
        hs = hs_ref[...]
        yb_ref[...] = (hs * gel).astype(BF16)
        dproj_ref[:, 4 * d:5 * d] = (dyb * hs * _gelu_grad(gb, gsig)).astype(BF16)
        al_s[...] = _shift_rows(a, abuf[...], 1, True)
        abuf[...] = a[:SUBLANES, :]
        dh_s[...] = dyb * gel

        def blk(j, ln):
            o = pl.multiple_of((per - 1 - j) * SUBLANES, SUBLANES)
            a8, b8 = _scan_block(al_s[pl.ds(o, SUBLANES), :], dh_s[pl.ds(o, SUBLANES), :], reverse=True)
            l8 = b8 + a8 * ln
            lam_s[pl.ds(o, SUBLANES), :] = l8
            return jnp.broadcast_to(l8[0:1, :], (SUBLANES, d))

        lnext[...] = lax.fori_loop(0, per, blk, lnext[...])
        dbb = lam_s[...]
        da = dbb * _shift_rows(hs, keep_halo * hsh_ref[...], 1, False)
        dbx = dbb * xc
        dmult = dbx * ig
        dig = dbx * mult
        dxc = (dbb * mult) * ig
        dla = a * (da - (dmult * a) / mult)
        dlar = dla * r
        acc(G_MIX_LAM, _colsum(dlar) * (LRU_C * _sigmoid(-lam)))
        dzr = (dlar * (1.0 - r)) * (-LRU_C * sp)
        dzi = dig * ig * (1.0 - ig)
        acc(G_MIX_BR, _colsum(dzr))
        acc(G_MIX_BI, _colsum(dzi))
        dzrb = dzr.astype(BF16)
        dzib = dzi.astype(BF16)
        back = []
        for h in range(nh):
            sl = slice(h * bw, (h + 1) * bw)
            back.append(_dot_nt(dzrb[:, sl], wr_ref[h]) + _dot_nt(dzib[:, sl], wi_ref[h]))
            dwg_ref[0, h] += _dot_tn(xcb[:, sl], dzrb[:, sl])
            dwg_ref[1, h] += _dot_tn(xcb[:, sl], dzib[:, sl])
        dxc = dxc + jnp.concatenate(back, axis=1)
        acc(G_MIX_CBB, _colsum(dxc))
        acc(G_MIX_CBW, _colsum(dxc * xm3))
        acc(G_MIX_CBW + 1, _colsum(dxc * xm2))
        acc(G_MIX_CBW + 2, _colsum(dxc * xm1))
        acc(G_MIX_CBW + 3, _colsum(dxc * xb))
        dxt = dxbuf[...]
        dxb = (((dxc * tap(CW_B + 3) + _shift_rows(dxc, dxt, 1, True) * tap(CW_B + 2)) + _shift_rows(dxc, dxt, 2, True) * tap(CW_B + 1))
               + _shift_rows(dxc, dxt, 3, True) * tap(CW_B))
        dxbuf[...] = dxc[:SUBLANES, :]
        dproj_ref[:, 3 * d:4 * d] = dxb.astype(BF16)

    def rev(i):
        return (nt - 1 - i, 0)

    def halo(i):
        return (jnp.maximum((nt - 1 - i) * per - 1, 0), 0)

    tile = pl.BlockSpec((tm, d), rev)
    return pl.pallas_call(
        body,
        name="mixer_core_bwd",
        grid=(nt,),
        in_specs=[
            tile, tile, tile, tile, tile,
            pl.BlockSpec((SUBLANES, d), halo),
            pl.BlockSpec((tm, 7 * d), rev),
            pl.BlockSpec((SUBLANES, 7 * d), halo),
            _resident((None, N_MOD, d), lambda i: (l, 0, 0)),
            _resident((None, N_ROWS, d), lambda i: (l, 0, 0)),
            _resident((None, CW_ROWS, d), lambda i: (l, 0, 0)),
            _resident((None, nh, bw, bw), lambda i: (l, 0, 0, 0)),
            _resident((None, nh, bw, bw), lambda i: (l, 0, 0, 0)),
            _resident((d, d), lambda i: (0, 0)),
            _resident((d, d), lambda i: (0, 0)),
            _resident((d, d), lambda i: (0, 0)),
        ],
        out_specs=(pl.BlockSpec((tm, 7 * d), rev),) + (tile,) * 6 + (
            pl.BlockSpec((G_MIX_ROWS, d), lambda i: (0, 0)), pl.BlockSpec((2, nh, bw, bw), lambda i: (0, 0, 0, 0))),
        out_shape=(jax.ShapeDtypeStruct((s, 7 * d), BF16),) + (jax.ShapeDtypeStruct((s, d), BF16),) * 6 + (
            jax.ShapeDtypeStruct((G_MIX_ROWS, d), F32), jax.ShapeDtypeStruct((2, nh, bw, bw), F32)),
        scratch_shapes=[pltpu.VMEM((SUBLANES, d), F32)] * 3 + [pltpu.VMEM((tm, d), F32)] * 3 + [pltpu.VMEM((SUBLANES, d), F32)],
        compiler_params=_cparams(("arbitrary",), 3 * _nbytes((d, d), BF16) + 3 * _nbytes((tm, 7 * d), F32) + 64 * _nbytes((tm, d), F32)),
    )(*_hbm(dx1, y, yap, ybp, hs, hs, proj, proj), mod, rows, cwf, wr, wi, wa_f, wb_f, wo_f)


def _in_proj_bwd(dproj, x, dx1, mod, rows, win_f, l, tm):
    s, d = x.shape
    nb, _, ci = win_f.shape

    def body(dp_ref, x_ref, dx1_ref, mod_ref, rows_ref, w_ref, dx_ref, sm_ref):
        @pl.when(pl.program_id(0) == 0)
        def _():
            sm_ref[...] = jnp.zeros(sm_ref.shape, F32)

        dh = _dot_nt(dp_ref[:, 0:ci], w_ref[0])
        for j in range(1, nb):
            dh = dh + _dot_nt(dp_ref[:, j * ci:(j + 1) * ci], w_ref[j])
        dxn, dsc, dsh, dg = _prenorm_bwd(x_ref[...], dh, rows_ref[R_G_PRE_MIX:R_G_PRE_MIX + 1, :], mod_ref[M_SC_M:M_SC_M + 1, :])
        dx_ref[...] = dx1_ref[...] + dxn
        for r, v in ((G_IN_SC, dsc), (G_IN_SH, dsh), (G_IN_GPRE, dg)):
            sm_ref[r:r + 1, :] += v

    tile = pl.BlockSpec((tm, d), lambda i: (i, 0))
    return pl.pallas_call(
        body,
        name="in_proj_bwd",
        grid=(s // tm,),
        in_specs=[
            pl.BlockSpec((tm, nb * ci), lambda i: (i, 0)), tile, tile,
            _resident((None, N_MOD, d), lambda i: (l, 0, 0)),
            _resident((None, N_ROWS, d), lambda i: (l, 0, 0)),
            _resident((nb, d, ci), lambda i: (0, 0, 0)),
        ],
        out_specs=(tile, pl.BlockSpec((G_IN_ROWS, d), lambda i: (0, 0))),
        out_shape=(jax.ShapeDtypeStruct((s, d), F32), jax.ShapeDtypeStruct((G_IN_ROWS, d), F32)),
        compiler_params=_cparams(("arbitrary",), _nbytes((nb, d, ci), BF16) + 2 * _nbytes((tm, nb * ci), BF16) + 16 * _nbytes((tm, d), F32)),
    )(*_hbm(dproj, x, dx1), mod, rows, win_f)


def _wgrad(a, b, cols_owned, ts, after):
    s, k1 = a.shape
    k2 = b.shape[1]
    ns = s // ts
    if cols_owned:
        nblk, bk1, bk2 = N_DEV, k1, k2 // N_DEV
        a_spec = pl.BlockSpec((ts, bk1), lambda j, t: (t, 0))
        b_spec = pl.BlockSpec((ts, bk2), lambda j, t: (t, j))
    else:
        bk1, bk2 = min(WGRAD_ROWS, k1), k2
        nblk = k1 // bk1
        a_spec = pl.BlockSpec((ts, bk1), lambda j, t: (t, j))
        b_spec = pl.BlockSpec((ts, bk2), lambda j, t: (t, 0))

    def body(a_ref, b_ref, after_ref, o_ref, acc_ref):
        t = pl.program_id(1)

        @pl.when(t == 0)
        def _():
            acc_ref[...] = jnp.zeros(acc_ref.shape, F32)

        acc_ref[...] += _dot_tn(a_ref[...], b_ref[...])

        @pl.when(t == ns - 1)
        def _():
            o_ref[...] = acc_ref[...].astype(BF16)

    out = pl.pallas_call(
        body,
        name="wgrad",
        grid=(nblk, ns),
        in_specs=[a_spec, b_spec, ANY_SPEC],
        out_specs=pl.BlockSpec((None, bk1, bk2), lambda j, t: (j, 0, 0)),
        out_shape=pltpu.HBM((nblk, bk1, bk2), BF16),
        scratch_shapes=[pltpu.VMEM((bk1, bk2), F32)],
        compiler_params=_cparams(("parallel", "arbitrary"), 4 * _nbytes((bk1, bk2), F32) + 4 * _nbytes((ts, bk1 + bk2), BF16)),
    )(pltpu.with_memory_space_constraint(a, pltpu.HBM), pltpu.with_memory_space_constraint(b, pltpu.HBM), after)
    return out if cols_owned else out.reshape(N_DEV, k1 // N_DEV, k2)


def _adam_update(w, g, m, v):
    m = ADAM_B1 * m + (1.0 - ADAM_B1) * g
    v = ADAM_B2 * v + (1.0 - ADAM_B2) * (g * g)
    m_hat = m / (1.0 - ADAM_B1 ** ADAM_STEP)
    v_hat = v / (1.0 - ADAM_B2 ** ADAM_STEP)
    delta = -ADAM_LR * (m_hat / (jnp.sqrt(v_hat) + ADAM_EPS) + ADAM_WD * w)
    return delta, m, v


def _sum_adamw(recv, w, m, v, tr, after):
    nl, ra, cb = w.shape
    assert nl == len(recv) == 2

    def body(r0_ref, r1_ref, w_ref, m_ref, v_ref, after_ref, g_ref, d_ref, nm_ref, nv_ref):
        def total(r_ref):
            g = r_ref[0].astype(F32)
            for j in range(1, N_DEV):
                g = g + r_ref[j].astype(F32)
            return g

        g = jnp.where(pl.program_id(0) == 0, total(r0_ref), total(r1_ref))
        g_ref[...] = g
        d_ref[...], nm_ref[...], nv_ref[...] = _adam_update(w_ref[...], g, m_ref[...], v_ref[...])

    blk = pl.BlockSpec((None, tr, cb), lambda l, i: (l, i, 0))
    return pl.pallas_call(
        body,
        name="sum_adamw",
        grid=(nl, ra // tr),
        in_specs=[pl.BlockSpec((N_DEV, tr, cb), lambda l, i: (0, i * (1 - l), 0)),
                  pl.BlockSpec((N_DEV, tr, cb), lambda l, i: (0, i * l, 0)), blk, blk, blk, ANY_SPEC],
        out_specs=(blk,) * 4,
        out_shape=(jax.ShapeDtypeStruct((nl, ra, cb), F32),) * 4,
        compiler_params=_cparams(("arbitrary", "arbitrary"), 6 * _nbytes((N_DEV, tr, cb), BF16) + 32 * _nbytes((tr, cb), F32)),
    )(recv[0], recv[1], w, m, v, after)


def _adamw(w, g, m, v):
    def body(w_ref, g_ref, m_ref, v_ref, d_ref, nm_ref, nv_ref):
        d_ref[...], nm_ref[...], nv_ref[...] = _adam_update(w_ref[...], g_ref[...], m_ref[...], v_ref[...])

    return pl.pallas_call(
        body,
        name="adamw",
        in_specs=[VMEM_SPEC] * 4,
        out_specs=(VMEM_SPEC,) * 3,
        out_shape=(jax.ShapeDtypeStruct(w.shape, F32),) * 3,
        compiler_params=_cparams(vmem=10 * _nbytes(w.shape, F32)),
    )(w, g, m, v)


def _adamw_tiled(w, g, m, v, tr):
    nl, ra, cb = w.shape

    def body(w_ref, g_ref, m_ref, v_ref, d_ref, nm_ref, nv_ref):
        d_ref[...], nm_ref[...], nv_ref[...] = _adam_update(w_ref[...], g_ref[...], m_ref[...], v_ref[...])

    blk = pl.BlockSpec((None, tr, cb), lambda l, i: (l, i, 0))
    return pl.pallas_call(
        body,
        name="adamw_tiled",
        grid=(nl, ra // tr),
        in_specs=[blk] * 4,
        out_specs=(blk,) * 3,
        out_shape=(jax.ShapeDtypeStruct((nl, ra, cb), F32),) * 3,
        compiler_params=_cparams(("parallel", "parallel")),
    )(*_hbm(w, g, m, v))


def _token_tile(s):
    return min(256, max(SUBLANES * 2, s // 4))


def kernel(x, c, w_mod, b_mod, g_pre_mix, g_post_mix, w_in, conv_a_w, conv_a_b, w_a_out, conv_b_w, conv_b_b, w_gate_r, b_gate_r, w_gate_i, b_gate_i, lru_lambda, w_b_out, w_o, g_pre_mlp, g_post_mlp, w_mlp_up, w_mlp_down, loss_target, m_w_mod, m_b_mod, m_g_pre_mix, m_g_post_mix, m_w_in, m_conv_a_w, m_conv_a_b, m_w_a_out, m_conv_b_w, m_conv_b_b, m_w_gate_r, m_b_gate_r, m_w_gate_i, m_b_gate_i, m_lru_lambda, m_w_b_out, m_w_o, m_g_pre_mlp, m_g_post_mlp, m_w_mlp_up, m_w_mlp_down, v_w_mod, v_b_mod, v_g_pre_mix, v_g_post_mix, v_w_in, v_conv_a_w, v_conv_a_b, v_w_a_out, v_conv_b_w, v_conv_b_b, v_w_gate_r, v_b_gate_r, v_w_gate_i, v_b_gate_i, v_lru_lambda, v_w_b_out, v_w_o, v_g_pre_mlp, v_g_post_mlp, v_w_mlp_up, v_w_mlp_down):
    nl = w_mod.shape[0]
    s, d = x.shape[1], x.shape[2]
    nh, bw = w_gate_r.shape[1], w_gate_r.shape[2]
    cwid = conv_a_w.shape[2]
    tm = _token_tile(s)
    tmx = min(2 * tm, s)
    ts = s
    _, me = _my_position()
    xs = x.reshape(s, d)
    target = loss_target.reshape(s, d)

    vec_names = (g_pre_mix, g_post_mix, conv_a_b, conv_b_b, b_gate_r, b_gate_i, lru_lambda, g_pre_mlp, g_post_mlp)
    rows = jnp.concatenate([jnp.stack(vec_names, axis=1), jnp.zeros((nl, N_ROWS - len(vec_names), d), F32)], axis=1)
    cw = jnp.concatenate([conv_a_w, conv_b_w, jnp.zeros((nl, CW_ROWS - 7, cwid), F32)], axis=1)

    w16 = {"w_in": w_in.astype(BF16), "w_a_out": w_a_out.astype(BF16), "w_b_out": w_b_out.astype(BF16), "w_o": w_o.astype(BF16),
           "w_mlp_up": w_mlp_up.astype(BF16), "w_mlp_down": w_mlp_down.astype(BF16)}
    groups = (("in", ("w_in",)), ("mix", ("w_a_out", "w_b_out", "w_o")), ("mlp", ("w_mlp_up", "w_mlp_down")))
    mod, cact, cwf = _prep_small(c, w_mod, b_mod, cw)
    mod = mod.reshape(nl, N_MOD, d)
    gathers = {}
    token = jnp.zeros((), F32)
    after = mod
    for l in range(nl):
        for gname, members in groups:
            gathers[l, gname] = _gather2_start([w16[n][l] for n in members], after, f"gather_start_{gname}{l}")
            after = gathers[l, gname][4]
            token = token + after[0, 0]
    rows = rows + token
    wr = w_gate_r.astype(BF16)
    wi = w_gate_i.astype(BF16)

    forwarded = {}

    def forward(l, gname, after, rows):
        forwarded[l, gname] = _gather2_forward(gathers[l, gname], after, f"gather_forward_{gname}{l}")
        return rows + forwarded[l, gname][4][0, 0]

    def gathered(l, gname, after):
        return _gather2_wait(forwarded[l, gname], after, f"gather_wait_{gname}{l}")

    saved = []
    weights = []
    xin = xs
    for l in range(nl):
        if l == 0:
            rows = forward(0, "in", mod, rows)
        (win_f,) = gathered(l, "in", mod if l == 0 else xin)
        if l > 0:
            rows = forward(l, "mix", win_f, rows)
        proj, h = _in_proj_fwd(xin, mod, rows, win_f, l, tmx)
        if l == 0:
            rows = forward(0, "mix", proj, rows)
        wa_f, wb_f, wo_f = (w.reshape(d, d) for w in gathered(l, "mix", proj))
        rows = forward(l, "mlp", wo_f, rows)
        x1, hs, yap, ybp, y = _mixer_core_fwd(proj, xin, mod, rows, cwf, wr, wi, wa_f, wb_f, wo_f, l, tm)
        wup_f, wdn_f = gathered(l, "mlp", x1)
        wdn_f = wdn_f.reshape(-1, d)
        if l + 1 < nl:
            rows = forward(l + 1, "in", wdn_f, rows)
        x2, ru, y2, h2 = _mlp_fwd(x1, mod, rows, wup_f, wdn_f, l, tmx)
        saved.append((xin, proj, h, x1, hs, yap, ybp, y, ru, y2, h2))
        weights.append((win_f, wa_f, wb_f, wo_f, wup_f, wdn_f))
        xin = x2
    loss_part, dx = _loss_fwd_bwd(xin, target, tm)

    scatters = {}
    small = [None] * nl
    gate_parts = [None] * nl

    def scatter(l, gname, parts, rows):
        scatters[l, gname] = _exchange_start(parts, False, f"scatter_start_{gname}{l}")
        return rows + scatters[l, gname][4][0, 0]

    for l in reversed(range(nl)):
        xin, proj, h, x1, hs, yap, ybp, y, ru, y2, h2 = saved[l]
        win_f, wa_f, wb_f, wo_f, wup_f, wdn_f = weights[l]
        dx1, dy2, dup, act, sm_mlp = _mlp_bwd(dx, x1, y2, ru, mod, rows, wup_f, wdn_f, l, tm)
        g_up = _wgrad(h2, dup, True, ts, dx1)
        g_dn = _wgrad(act, dy2, False, ts, g_up)
        rows = scatter(l, "mlp", [g_up, g_dn], rows)
        dproj, dy, m, dyap, dybp, ya, yb, sm_mix, dwg = _mixer_core_bwd(dx1, y, yap, ybp, hs, proj, mod, rows, cwf, wr, wi, wa_f, wb_f, wo_f, l, tm // 2)
        gate_parts[l] = _exchange_start([dwg.astype(BF16)], True, f"gates_start{l}")
        rows = rows + gate_parts[l][4][0, 0]
        g_a = _wgrad(ya, dyap, False, ts, gate_parts[l][4])
        g_b = _wgrad(yb, dybp, False, ts, g_a)
        g_o = _wgrad(m, dy, False, ts, g_b)
        rows = scatter(l, "mix", [g_a, g_b, g_o], rows)
        rows = scatter(l, "in", [_wgrad(h, dproj, True, ts, scatters[l, "mix"][4])], rows)
        dx, sm_in = _in_proj_bwd(dproj, xin, dx1, mod, rows, win_f, l, tmx)
        small[l] = jnp.concatenate([sm_mlp, sm_mix, sm_in], axis=0)
    grad_x = dx.reshape(x.shape)

    recv = {}
    big = {}
    moments = {"w_in": (w_in, m_w_in, v_w_in), "w_mlp_up": (w_mlp_up, m_w_mlp_up, v_w_mlp_up), "w_a_out": (w_a_out, m_w_a_out, v_w_a_out),
               "w_b_out": (w_b_out, m_w_b_out, v_w_b_out), "w_o": (w_o, m_w_o, v_w_o), "w_mlp_down": (w_mlp_down, m_w_mlp_down, v_w_mlp_down)}

    def collect(l, gname, after):
        for n, zone in zip(dict(groups)[gname], _exchange_wait(scatters[l, gname], after, False, f"scatter_wait_{gname}{l}")):
            recv[n, l] = zone

    def update(name, after):
        w, m_, v_ = moments[name]
        big[name] = _sum_adamw([recv[name, l] for l in range(nl)], w, m_, v_, min(256, w.shape[1]), after)
        return big[name][1]

    for l, gname in ((1, "mlp"), (1, "in"), (1, "mix"), (0, "mlp")):
        collect(l, gname, dx)
    early = update("w_mlp_down", update("w_mlp_up", dx))

    lrows = jnp.concatenate(small, axis=0)
    lrows = lrows.at[G_LOSS_ROW, 0].set(loss_part)

    def lrow(a, l, r):
        return a[l * G_LAYER_ROWS + r]

    dm = jnp.stack([jnp.concatenate([lrow(lrows, l, G_MLP_ROWS + G_MIX_ROWS + G_IN_SH), lrow(lrows, l, G_MLP_ROWS + G_MIX_ROWS + G_IN_SC),
                                     lrow(lrows, l, G_MLP_ROWS + G_MIX_GT), lrow(lrows, l, G_MLP_SH), lrow(lrows, l, G_MLP_SC),
                                     lrow(lrows, l, G_MLP_GT)]) for l in range(nl)])
    dm8 = jnp.concatenate([dm[:, None, :], jnp.zeros((nl, SUBLANES - 1, N_MOD * d), F32)], axis=1)
    srows, g_w_mod = _reduce_small(lrows, dm8, cact, early)
    loss = srows[G_LOSS_ROW, 0]
    sgates = _sum_gathered([_exchange_wait(gate_parts[l], srows, True, f"gates_wait{l}")[0] for l in range(nl)])
    for gname in ("in", "mix"):
        collect(0, gname, srows)
    for name in ("w_in", "w_a_out", "w_b_out", "w_o"):
        update(name, srows)

    def srow(l, r):
        return lrow(srows, l, r)

    def per_layer(r):
        return jnp.stack([srow(l, r) for l in range(nl)])

    mix0 = G_MLP_ROWS
    in0 = G_MLP_ROWS + G_MIX_ROWS
    g_b_mod = jnp.stack([jnp.concatenate([srow(l, in0 + G_IN_SH), srow(l, in0 + G_IN_SC), srow(l, mix0 + G_MIX_GT),
                                          srow(l, G_MLP_SH), srow(l, G_MLP_SC), srow(l, G_MLP_GT)]) for l in range(nl)])
    conv_a_full = jnp.stack([jnp.stack([srow(l, mix0 + G_MIX_CAW + k) for k in range(3)]) for l in range(nl)])
    conv_b_full = jnp.stack([jnp.stack([srow(l, mix0 + G_MIX_CBW + k) for k in range(4)]) for l in range(nl)])
    grads = {
        "b_mod": g_b_mod,
        "g_pre_mix": per_layer(in0 + G_IN_GPRE),
        "g_post_mix": per_layer(mix0 + G_MIX_GPOST),
        "conv_a_w": lax.dynamic_slice_in_dim(conv_a_full, me * cwid, cwid, axis=2),
        "conv_a_b": per_layer(mix0 + G_MIX_CAB),
        "conv_b_w": lax.dynamic_slice_in_dim(conv_b_full, me * cwid, cwid, axis=2),
        "conv_b_b": per_layer(mix0 + G_MIX_CBB),
        "w_gate_r": sgates[:, 0],
        "b_gate_r": per_layer(mix0 + G_MIX_BR),
        "w_gate_i": sgates[:, 1],
        "b_gate_i": per_layer(mix0 + G_MIX_BI),
        "lru_lambda": per_layer(mix0 + G_MIX_LAM),
        "g_pre_mlp": per_layer(G_MLP_GPRE),
        "g_post_mlp": per_layer(G_MLP_GPOST),
    }
    params = {
        "b_mod": (b_mod, m_b_mod, v_b_mod), "g_pre_mix": (g_pre_mix, m_g_pre_mix, v_g_pre_mix), "g_post_mix": (g_post_mix, m_g_post_mix, v_g_post_mix),
        "conv_a_w": (conv_a_w, m_conv_a_w, v_conv_a_w), "conv_a_b": (conv_a_b, m_conv_a_b, v_conv_a_b),
        "conv_b_w": (conv_b_w, m_conv_b_w, v_conv_b_w), "conv_b_b": (conv_b_b, m_conv_b_b, v_conv_b_b),
        "w_gate_r": (w_gate_r, m_w_gate_r, v_w_gate_r), "b_gate_r": (b_gate_r, m_b_gate_r, v_b_gate_r),
        "w_gate_i": (w_gate_i, m_w_gate_i, v_w_gate_i), "b_gate_i": (b_gate_i, m_b_gate_i, v_b_gate_i),
        "lru_lambda": (lru_lambda, m_lru_lambda, v_lru_lambda), "g_pre_mlp": (g_pre_mlp, m_g_pre_mlp, v_g_pre_mlp),
        "g_post_mlp": (g_post_mlp, m_g_post_mlp, v_g_post_mlp),
    }
    out = {}
    for name, g in grads.items():
        w, m_, v_ = params[name]
        flat = (-1, w.shape[-1])
        dl, nm, nv = _adamw(w.reshape(flat), g.reshape(flat), m_.reshape(flat), v_.reshape(flat))
        out[name] = (g.reshape(w.shape), dl.reshape(w.shape), nm.reshape(w.shape), nv.reshape(w.shape))
    out["w_mod"] = (g_w_mod,) + tuple(_adamw_tiled(w_mod, g_w_mod, m_w_mod, v_w_mod, min(128, d)))
    out.update(big)

    order = ("w_mod", "b_mod", "g_pre_mix", "g_post_mix", "w_in", "conv_a_w", "conv_a_b", "w_a_out", "conv_b_w", "conv_b_b", "w_gate_r", "b_gate_r",
             "w_gate_i", "b_gate_i", "lru_lambda", "w_b_out", "w_o", "g_pre_mlp", "g_post_mlp", "w_mlp_up", "w_mlp_down")
    return (loss, grad_x) + tuple(out[n][0] for n in order) + tuple(out[n][1] for n in order) + tuple(out[n][2] for n in order) + tuple(out[n][3] for n in order)
```

```python
import functools

import jax
import jax.numpy as jnp
from jax import lax
from jax.experimental import pallas as pl
from jax.experimental.pallas import tpu as pltpu

F32, BF16 = jnp.float32, jnp.bfloat16
EPS = 1e-6
LRU_C = 8.0
N_DEV = 8
N_MOD = 6
SUBLANES = 8
VMEM_BUDGET = 56 * 1024 * 1024
WGRAD_ROWS = 512
ADAM_LR, ADAM_B1, ADAM_B2, ADAM_EPS, ADAM_WD, ADAM_STEP = 0.001, 0.9, 0.999, 1e-08, 0.01, 10
MESH = pl.DeviceIdType.MESH
VMEM_SPEC = pl.BlockSpec(memory_space=pltpu.VMEM)
ANY_SPEC = pl.BlockSpec(memory_space=pl.ANY)
HBM_SPEC = pl.BlockSpec(memory_space=pltpu.HBM)
SEM_SPEC = pl.BlockSpec(memory_space=pltpu.SEMAPHORE)
SIDE_EFFECT = pltpu.SideEffectType.DATAFLOW_SIDE_EFFECTING

R_G_PRE_MIX, R_G_POST_MIX, R_CONV_A_B, R_CONV_B_B, R_B_GATE_R, R_B_GATE_I, R_LAMBDA, R_G_PRE_MLP, R_G_POST_MLP = range(9)
N_ROWS = 16
M_SH_M, M_SC_M, M_GT_M, M_SH_F, M_SC_F, M_GT_F = range(6)
CW_A, CW_B, CW_ROWS = 0, 3, 8
G_MLP_GT, G_MLP_GPOST, G_MLP_SC, G_MLP_SH, G_MLP_GPRE, G_LOSS_ROW, G_MLP_ROWS = 0, 1, 2, 3, 4, 7, 8
(G_MIX_GT, G_MIX_GPOST, G_MIX_CAB, G_MIX_CAW, G_MIX_CBB, G_MIX_CBW, G_MIX_BR, G_MIX_BI, G_MIX_LAM) = 0, 1, 2, 3, 6, 7, 11, 12, 13
G_MIX_ROWS = 16
G_IN_SC, G_IN_SH, G_IN_GPRE, G_IN_ROWS = 0, 1, 2, 8
G_LAYER_ROWS = G_MLP_ROWS + G_MIX_ROWS + G_IN_ROWS


def _cparams(dims=None, vmem=None):
    kw = {}
    if dims is not None:
        kw["dimension_semantics"] = dims
    if vmem is not None:
        kw["vmem_limit_bytes"] = int(min(max(vmem, 16 * 1024 * 1024), VMEM_BUDGET))
    return pltpu.CompilerParams(**kw)


def _nbytes(shape, dtype):
    n = 1
    for s in shape:
        n *= s
    return n * jnp.dtype(dtype).itemsize


def _hbm(*arrays):
    return tuple(pltpu.with_memory_space_constraint(a, pltpu.HBM) for a in arrays)


def _resident(block, index_map):
    return pl.BlockSpec(block, index_map, pipeline_mode=pl.Buffered(1))


def _my_position():
    x, y, c = lax.axis_index("x"), lax.axis_index("y"), lax.axis_index("c")
    return (x, y, c), 4 * x + 2 * y + c


def _peer(pos, k):
    x, y, c = pos
    px = 1 - x if k & 4 else x
    py = 1 - y if k & 2 else y
    pc = 1 - c if k & 1 else c
    return (px, py, pc), 4 * px + 2 * py + pc


def _remote(src, dst, ssem, rsem, peer):
    return pltpu.make_async_remote_copy(src_ref=src, dst_ref=dst, send_sem=ssem, recv_sem=rsem, device_id=peer, device_id_type=MESH)


def _dot(a, b):
    return jnp.dot(a, b, preferred_element_type=F32)


def _dot_nt(a, b):
    return lax.dot_general(a, b, (((1,), (1,)), ((), ())), preferred_element_type=F32)


def _dot_tn(a, b):
    return lax.dot_general(a, b, (((0,), (0,)), ((), ())), preferred_element_type=F32)


def _colsum(v):
    return jnp.sum(v, axis=0, keepdims=True)


def _sigmoid(v):
    return jax.nn.sigmoid(v)


GELU_K, GELU_C = 0.7978845608028654, 0.044715


def _gelu(v):
    s = 1.0 / (1.0 + jnp.exp(v * (-2.0 * GELU_K - (2.0 * GELU_K * GELU_C) * (v * v))))
    return v * s, s


def _gelu_grad(v, s):
    return s * (1.0 + (v * (1.0 - s)) * (2.0 * GELU_K + (6.0 * GELU_K * GELU_C) * (v * v)))


def _neg_expm1_twice(v):
    t = jnp.tanh(v)
    return (-2.0 * t) / (1.0 - t)


def _softplus_neg(lam):
    z = -lam
    u = jnp.exp(-jnp.abs(z))
    w = 1.0 + u
    l1p = jnp.where(w == 1.0, u, jnp.log(w) * u / (w - 1.0))
    return jnp.maximum(z, 0.0) + l1p


def _rms(v):
    return lax.rsqrt(jnp.mean(v * v, axis=-1, keepdims=True) + EPS)


def _prenorm_bwd(xv, dh, g, sc):
    r = _rms(xv)
    xn = xv * r
    n = xn * g
    dsc = _colsum(dh * n)
    dsh = _colsum(dh)
    dn = dh * (1.0 + sc)
    dg = _colsum(dn * xn)
    dxn = dn * g
    dx = r * (dxn - xn * jnp.mean(dxn * xn, axis=-1, keepdims=True))
    return dx, dsc, dsh, dg


def _postnorm_bwd(yv, dout, g, gt):
    r = _rms(yv)
    yn = yv * r
    dgt = _colsum(dout * (yn * g))
    dn = dout * gt
    dg = _colsum(dn * yn)
    dyn = dn * g
    dy = r * (dyn - yn * jnp.mean(dyn * yn, axis=-1, keepdims=True))
    return dy, dgt, dg


def _gates(xc, wr_ref, wi_ref, b_r, b_i, sp, nh, bw):
    xcb = xc.astype(BF16)
    zr = jnp.concatenate([_dot(xcb[:, h * bw:(h + 1) * bw], wr_ref[h]) for h in range(nh)], axis=1) + b_r
    zi = jnp.concatenate([_dot(xcb[:, h * bw:(h + 1) * bw], wi_ref[h]) for h in range(nh)], axis=1) + b_i
    r = _sigmoid(zr)
    ig = _sigmoid(zi)
    la = (-LRU_C * r) * sp
    a = jnp.exp(la)
    mult = jnp.sqrt(_neg_expm1_twice(la))
    return xcb, r, ig, a, mult


def _shift_rows(cur, edge, k, up):
    t, dd = cur.shape
    blocks = cur.reshape(t // SUBLANES, SUBLANES, dd)
    row = lax.broadcasted_iota(jnp.int32, (1, SUBLANES, dd), 1)
    if up:
        r = pltpu.roll(blocks, SUBLANES - k, 1)
        nxt = jnp.concatenate([r[1:], pltpu.roll(edge, SUBLANES - k, 0)[None]], axis=0)
        out = jnp.where(row >= SUBLANES - k, nxt, r)
    else:
        r = pltpu.roll(blocks, k, 1)
        prv = jnp.concatenate([pltpu.roll(edge, k, 0)[None], r[:-1]], axis=0)
        out = jnp.where(row < k, prv, r)
    return out.reshape(t, dd)


def _scan_block(a8, b8, reverse):
    row = lax.broadcasted_iota(jnp.int32, a8.shape, 0)
    for s in (1, 2, 4):
        if reverse:
            keep = row < SUBLANES - s
            a_sh = pltpu.roll(a8, SUBLANES - s, 0)
            b_sh = pltpu.roll(b8, SUBLANES - s, 0)
        else:
            keep = row >= s
            a_sh = pltpu.roll(a8, s, 0)
            b_sh = pltpu.roll(b8, s, 0)
        b8 = b8 + a8 * jnp.where(keep, b_sh, 0.0)
        a8 = a8 * jnp.where(keep, a_sh, 1.0)
    return a8, b8


def _prep_small(c, w_mod, b_mod, cw):
    d = c.shape[1]
    cm = w_mod.shape[2]
    cwid = cw.shape[2]
    nl = w_mod.shape[0]

    def body(c_ref, wm_ref, bm_ref, cw_ref, mod_ref, cact_ref, cwf_ref, cbuf, pbuf, rbuf, ssem, rsem, lsem):
        pos, me = _my_position()
        me8 = pl.multiple_of(me * SUBLANES, SUBLANES)
        cbuf[pl.ds(me8, SUBLANES), :] = jnp.broadcast_to(c_ref[...], (SUBLANES, d))
        own_cw = pltpu.make_async_copy(cw_ref, cwf_ref.at[:, :, pl.ds(me * cwid, cwid)], lsem.at[0])
        own_cw.start()
        first = []
        for k in range(1, N_DEV):
            peer, _ = _peer(pos, k)
            rows = cbuf.at[pl.ds(me8, SUBLANES), :]
            first.append(_remote(rows, rows, ssem.at[0, k - 1], rsem.at[0, k - 1], peer))
            first.append(_remote(cw_ref, cwf_ref.at[:, :, pl.ds(me * cwid, cwid)], ssem.at[1, k - 1], rsem.at[1, k - 1], peer))
        for cp in first:
            cp.start()
        for k in range(1, N_DEV):
            peer, pj = _peer(pos, k)
            pj8 = pl.multiple_of(pj * SUBLANES, SUBLANES)
            rows = cbuf.at[pl.ds(pj8, SUBLANES), :]
            _remote(rows, rows, ssem.at[0, k - 1], rsem.at[0, k - 1], peer).wait_recv()
        cv = cbuf[...]
        cact = cv * _sigmoid(cv)
        cact_ref[...] = cact
        cb = cact.astype(BF16)
        for l in range(nl):
            pbuf[l] = _dot(cb, wm_ref[l].astype(BF16))
        own_p = pltpu.make_async_copy(pbuf.at[:, pl.ds(me8, SUBLANES), :], rbuf.at[me], lsem.at[1])
        own_p.start()
        second = []
        for k in range(1, N_DEV):
            peer, pj = _peer(pos, k)
            pj8 = pl.multiple_of(pj * SUBLANES, SUBLANES)
            second.append(_remote(pbuf.at[:, pl.ds(pj8, SUBLANES), :], rbuf.at[me], ssem.at[2, k - 1], rsem.at[2, k - 1], peer))
        for cp in second:
            cp.start()
        for k in range(1, N_DEV):
            peer, pj = _peer(pos, k)
            _remote(pbuf.at[:, pl.ds(0, SUBLANES), :], rbuf.at[pj], ssem.at[2, k - 1], rsem.at[2, k - 1], peer).wait_recv()
            _remote(cw_ref, cwf_ref.at[:, :, pl.ds(pj * cwid, cwid)], ssem.at[1, k - 1], rsem.at[1, k - 1], peer).wait_recv()
        own_p.wait()
        own_cw.wait()
        for l in range(nl):
            for j in range(N_DEV):
                mod_ref[l:l + 1, j * cm:(j + 1) * cm] = rbuf[j, l, 0:1, :] + bm_ref[l:l + 1, j * cm:(j + 1) * cm]
        for cp in first + second:
            cp.wait_send()

    return pl.pallas_call(
        body,
        name="prep_small",
        out_shape=(
            jax.ShapeDtypeStruct((nl, N_MOD * d), F32),
            jax.ShapeDtypeStruct((N_DEV * SUBLANES, d), F32),
            jax.ShapeDtypeStruct((nl, CW_ROWS, d), F32),
        ),
        in_specs=[VMEM_SPEC] * 4,
        out_specs=(VMEM_SPEC,) * 3,
        scratch_shapes=[
            pltpu.VMEM((N_DEV * SUBLANES, d), F32),
            pltpu.VMEM((nl, N_DEV * SUBLANES, cm), F32),
            pltpu.VMEM((N_DEV, nl, SUBLANES, cm), F32),
            pltpu.SemaphoreType.DMA((3, N_DEV - 1)),
            pltpu.SemaphoreType.DMA((3, N_DEV - 1)),
            pltpu.SemaphoreType.DMA((2,)),
        ],
        compiler_params=_cparams(vmem=3 * _nbytes(w_mod.shape, F32)),
    )(c, w_mod, b_mod, cw)


def _exchange_start(parts, gather, name):
    n = len(parts)
    lands = [lax.empty(((N_DEV,) + tuple(p.shape)) if gather else tuple(p.shape), p.dtype) for p in parts]

    def body(*refs):
        ins, lnd = refs[:n], refs[n:2 * n]
        ssem, rsem, token = refs[2 * n], refs[2 * n + 1], refs[-1]
        pos, me = _my_position()
        for k in range(1, N_DEV):
            peer, pj = _peer(pos, k)
            for t in range(n):
                src = ins[t] if gather else ins[t].at[pj]
                q = t * (N_DEV - 1) + k - 1
                _remote(src, lnd[t].at[me], ssem.at[q], rsem.at[q], peer).start()
        token[...] = jnp.zeros(token.shape, F32)

    out = pl.pallas_call(
        body,
        name=name,
        out_shape=(pltpu.SemaphoreType.DMA((n * (N_DEV - 1),)), pltpu.SemaphoreType.DMA((n * (N_DEV - 1),)))
        + tuple(pltpu.HBM(p.shape, p.dtype) for p in parts) + tuple(pltpu.HBM(p.shape, p.dtype) for p in lands)
        + (jax.ShapeDtypeStruct((SUBLANES, 128), F32),),
        in_specs=[HBM_SPEC] * (2 * n),
        out_specs=(SEM_SPEC, SEM_SPEC) + (HBM_SPEC,) * (2 * n) + (VMEM_SPEC,),
        input_output_aliases={i: 2 + i for i in range(2 * n)},
        compiler_params=pltpu.CompilerParams(has_side_effects=SIDE_EFFECT),
    )(*[pltpu.with_memory_space_constraint(p, pltpu.HBM) for p in list(parts) + lands])
    return out[0], out[1], list(out[2:2 + n]), list(out[2 + n:2 + 2 * n]), out[-1]


def _exchange_wait(started, after, gather, name):
    ssem, rsem, parts, lands, _ = started
    n = len(parts)

    def body(*refs):
        ins, lnd = refs[:n], refs[n:2 * n]
        ssem_ref, rsem_ref = refs[2 * n], refs[2 * n + 1]
        stage, lsem = refs[-1 - n:-1], refs[-1]
        pos, me = _my_position()
        load = []
        for t in range(n):
            src = ins[t] if gather else ins[t].at[me]
            load.append(pltpu.make_async_copy(src, stage[t], lsem.at[t]))
            load[-1].start()
        store = []
        for t in range(n):
            load[t].wait()
            store.append(pltpu.make_async_copy(stage[t], lnd[t].at[me], lsem.at[t]))
            store[-1].start()
        for k in range(1, N_DEV):
            peer, pj = _peer(pos, k)
            for t in range(n):
                src = ins[t] if gather else ins[t].at[pj]
                q = t * (N_DEV - 1) + k - 1
                _remote(src, lnd[t].at[me], ssem_ref.at[q], rsem_ref.at[q], peer).wait_send()
                _remote(src, lnd[t].at[pj], ssem_ref.at[q], rsem_ref.at[q], peer).wait_recv()
        for cp in store:
            cp.wait()

    out = pl.pallas_call(
        body,
        name=name,
        out_shape=tuple(pltpu.HBM(p.shape, p.dtype) for p in parts) + tuple(pltpu.HBM(p.shape, p.dtype) for p in lands),
        in_specs=[HBM_SPEC] * (2 * n) + [SEM_SPEC, SEM_SPEC, ANY_SPEC],
        out_specs=(HBM_SPEC,) * (2 * n),
        input_output_aliases={i: i for i in range(2 * n)},
        scratch_shapes=[pltpu.VMEM(tuple(z.shape[1:]), z.dtype) for z in lands] + [pltpu.SemaphoreType.DMA((n,))],
        compiler_params=pltpu.CompilerParams(has_side_effects=SIDE_EFFECT),
    )(*parts, *lands, ssem, rsem, after)
    return list(out[n:])


CHIP_PEERS = 3


def _chip_peer(pos, k, core):
    x, y, _ = pos
    px = 1 - x if k & 2 else x
    py = 1 - y if k & 1 else y
    return (px, py, core), 4 * px + 2 * py + core


def _gather2_start(parts, after, name):
    n = len(parts)
    lands = [lax.empty((N_DEV,) + tuple(p.shape), p.dtype) for p in parts]
    per = CHIP_PEERS + 1

    def body(*refs):
        ins, lnd = refs[:n], refs[n:2 * n]
        ssem, rsem, token = refs[2 * n + 1], refs[2 * n + 2], refs[-1]
        pos, me = _my_position()
        sibling = (pos[0], pos[1], 1 - pos[2])
        for t in range(n):
            _remote(ins[t], lnd[t].at[me], ssem.at[per * t], rsem.at[per * t], sibling).start()
        for k in range(1, per):
            peer, _ = _chip_peer(pos, k, pos[2])
            for t in range(n):
                _remote(ins[t], lnd[t].at[me], ssem.at[per * t + k], rsem.at[per * t + k], peer).start()
        token[...] = jnp.zeros(token.shape, F32)

    out = pl.pallas_call(
        body,
        name=name,
        out_shape=(pltpu.SemaphoreType.DMA((n * per,)), pltpu.SemaphoreType.DMA((n * per,)))
        + tuple(pltpu.HBM(p.shape, p.dtype) for p in parts) + tuple(pltpu.HBM(p.shape, p.dtype) for p in lands)
        + (jax.ShapeDtypeStruct((SUBLANES, 128), F32),),
        in_specs=[HBM_SPEC] * (2 * n) + [ANY_SPEC],
        out_specs=(SEM_SPEC, SEM_SPEC) + (HBM_SPEC,) * (2 * n) + (VMEM_SPEC,),
        input_output_aliases={i: 2 + i for i in range(2 * n)},
        compiler_params=pltpu.CompilerParams(has_side_effects=SIDE_EFFECT),
    )(*[pltpu.with_memory_space_constraint(p, pltpu.HBM) for p in list(parts) + lands], after)
    return out[0], out[1], list(out[2:2 + n]), list(out[2 + n:2 + 2 * n]), out[-1]


def _gather2_forward(started, after, name):
    ssem, rsem, parts, lands, _ = started
    n = len(lands)
    per = CHIP_PEERS + 1

    def body(*refs):
        lnd = refs[:n]
        rsem_a, fsend, frecv, token = refs[n], refs[2 * n + 2], refs[2 * n + 3], refs[2 * n + 4]
        pos, _ = _my_position()
        sibling = (pos[0], pos[1], 1 - pos[2])
        for k in range(1, per):
            peer, pk = _chip_peer(pos, k, pos[2])
            for t in range(n):
                block = lnd[t].at[pk]
                _remote(block, block, rsem_a.at[per * t + k], rsem_a.at[per * t + k], peer).wait_recv()
                q = CHIP_PEERS * t + k - 1
                _remote(block, block, fsend.at[q], frecv.at[q], sibling).start()
        token[...] = jnp.zeros(token.shape, F32)

    out = pl.pallas_call(
        body,
        name=name,
        out_shape=tuple(pltpu.HBM(p.shape, p.dtype) for p in lands)
        + (pltpu.SemaphoreType.DMA((n * CHIP_PEERS,)), pltpu.SemaphoreType.DMA((n * CHIP_PEERS,)),
           jax.ShapeDtypeStruct((SUBLANES, 128), F32)),
        in_specs=[HBM_SPEC] * n + [SEM_SPEC, ANY_SPEC],
        out_specs=(HBM_SPEC,) * n + (SEM_SPEC, SEM_SPEC, VMEM_SPEC),
        input_output_aliases={i: i for i in range(n)},
        compiler_params=pltpu.CompilerParams(has_side_effects=SIDE_EFFECT),
    )(*lands, rsem, after)
    return ssem, rsem, parts, list(out[:n]), out[n + 2], out[n], out[n + 1]


def _gather2_wait(forwarded, after, name):
    ssem, rsem, parts, lands, _, fsend, frecv = forwarded
    n = len(parts)
    per = CHIP_PEERS + 1

    def body(*refs):
        ins, lnd = refs[:n], refs[n:2 * n]
        ssem_a, rsem_a, fs, fr = refs[2 * n:2 * n + 4]
        stage, lsem = refs[-1 - n:-1], refs[-1]
        pos, me = _my_position()
        sibling = (pos[0], pos[1], 1 - pos[2])
        sib = 4 * pos[0] + 2 * pos[1] + 1 - pos[2]
        load = []
        for t in range(n):
            load.append(pltpu.make_async_copy(ins[t], stage[t], lsem.at[t]))
            load[-1].start()
        store = []
        for t in range(n):
            load[t].wait()
            store.append(pltpu.make_async_copy(stage[t], lnd[t].at[me], lsem.at[t]))
            store[-1].start()
        for t in range(n):
            _remote(ins[t], lnd[t].at[me], ssem_a.at[per * t], rsem_a.at[per * t], sibling).wait_send()
            _remote(ins[t], lnd[t].at[sib], ssem_a.at[per * t], rsem_a.at[per * t], sibling).wait_recv()
        for k in range(1, per):
            peer, pk = _chip_peer(pos, k, pos[2])
            _, qk = _chip_peer(pos, k, 1 - pos[2])
            for t in range(n):
                q = CHIP_PEERS * t + k - 1
                _remote(ins[t], lnd[t].at[me], ssem_a.at[per * t + k], rsem_a.at[per * t + k], peer).wait_send()
                _remote(lnd[t].at[pk], lnd[t].at[pk], fs.at[q], fr.at[q], sibling).wait_send()
                _remote(lnd[t].at[qk], lnd[t].at[qk], fs.at[q], fr.at[q], sibling).wait_recv()
        for cp in store:
            cp.wait()

    out = pl.pallas_call(
        body,
        name=name,
        out_shape=tuple(pltpu.HBM(p.shape, p.dtype) for p in parts) + tuple(pltpu.HBM(p.shape, p.dtype) for p in lands),
        in_specs=[HBM_SPEC] * (2 * n) + [SEM_SPEC] * 4 + [ANY_SPEC],
        out_specs=(HBM_SPEC,) * (2 * n),
        input_output_aliases={i: i for i in range(2 * n)},
        scratch_shapes=[pltpu.VMEM(tuple(z.shape[1:]), z.dtype) for z in lands] + [pltpu.SemaphoreType.DMA((n,))],
        compiler_params=pltpu.CompilerParams(has_side_effects=SIDE_EFFECT),
    )(*parts, *lands, ssem, rsem, fsend, frecv, after)
    return list(out[n:])


def _reduce_small(rows, dm8, cact, after):
    r, d = rows.shape
    nl = dm8.shape[0]
    cm = dm8.shape[2] // N_DEV

    def body(rows_ref, dm_ref, cact_ref, after_ref, orow_ref, owm_ref, gr, dmr, ssem, rsem, lsem):
        pos, me = _my_position()
        me8 = pl.multiple_of(me * SUBLANES, SUBLANES)
        gr[me] = rows_ref[...]
        own_dm = pltpu.make_async_copy(dm_ref.at[:, :, pl.ds(me * cm, cm)], dmr.at[:, pl.ds(me8, SUBLANES), :], lsem.at[0])
        own_dm.start()
        sends = []
        for k in range(1, N_DEV):
            peer, pj = _peer(pos, k)
            sends.append(_remote(gr.at[me], gr.at[me], ssem.at[0, k - 1], rsem.at[0, k - 1], peer))
            sends.append(_remote(dm_ref.at[:, :, pl.ds(pj * cm, cm)], dmr.at[:, pl.ds(me8, SUBLANES), :],
                                 ssem.at[1, k - 1], rsem.at[1, k - 1], peer))
        for cp in sends:
            cp.start()
        for k in range(1, N_DEV):
            peer, pj = _peer(pos, k)
            pj8 = pl.multiple_of(pj * SUBLANES, SUBLANES)
            _remote(gr.at[pj], gr.at[pj], ssem.at[0, k - 1], rsem.at[0, k - 1], peer).wait_recv()
            _remote(dm_ref.at[:, :, pl.ds(0, cm)], dmr.at[:, pl.ds(pj8, SUBLANES), :], ssem.at[1, k - 1], rsem.at[1, k - 1], peer).wait_recv()
        own_dm.wait()
        acc = gr[0]
        for j in range(1, N_DEV):
            acc = acc + gr[j]
        orow_ref[...] = acc
        cb = cact_ref[...].astype(BF16)
        for l in range(nl):
            owm_ref[l] = _dot_tn(cb, dmr[l].astype(BF16))
        for cp in sends:
            cp.wait_send()

    return pl.pallas_call(
        body,
        name="reduce_small",
        out_shape=(jax.ShapeDtypeStruct((r, d), F32), jax.ShapeDtypeStruct((nl, d, cm), F32)),
        in_specs=[VMEM_SPEC] * 3 + [ANY_SPEC],
        out_specs=(VMEM_SPEC,) * 2,
        scratch_shapes=[
            pltpu.VMEM((N_DEV, r, d), F32),
            pltpu.VMEM((nl, N_DEV * SUBLANES, cm), F32),
            pltpu.SemaphoreType.DMA((2, N_DEV - 1)),
            pltpu.SemaphoreType.DMA((2, N_DEV - 1)),
            pltpu.SemaphoreType.DMA((1,)),
        ],
        compiler_params=_cparams(vmem=4 * _nbytes((N_DEV, r, d), F32) + 6 * _nbytes((nl, d, cm), F32)),
    )(rows, dm8, cact, after)


def _sum_gathered(zones):
    nl = len(zones)

    def body(*refs):
        for l in range(nl):
            acc = refs[l][0].astype(F32)
            for j in range(1, N_DEV):
                acc = acc + refs[l][j].astype(F32)
            refs[nl][l] = acc

    return pl.pallas_call(
        body,
        name="sum_gathered",
        out_shape=jax.ShapeDtypeStruct((nl,) + tuple(zones[0].shape[1:]), F32),
        in_specs=[VMEM_SPEC] * nl,
        out_specs=VMEM_SPEC,
        compiler_params=_cparams(vmem=8 * nl * _nbytes(zones[0].shape, BF16)),
    )(*zones)


def _in_proj_fwd(x, mod, rows, win_f, l, tm, after):
    s, d = x.shape
    nb, _, ci = win_f.shape

    def body(x_ref, mod_ref, rows_ref, w_ref, after_ref, proj_ref, h_ref):
        xv = x_ref[...]
        g = rows_ref[R_G_PRE_MIX:R_G_PRE_MIX + 1, :]
        h = (xv * _rms(xv) * g) * (1.0 + mod_ref[M_SC_M:M_SC_M + 1, :]) + mod_ref[M_SH_M:M_SH_M + 1, :]
        hb = h.astype(BF16)
        h_ref[...] = hb
        for j in range(nb):
            proj_ref[:, j * ci:(j + 1) * ci] = _dot(hb, w_ref[j])

    return pl.pallas_call(
        body,
        name="in_proj_fwd",
        grid=(s // tm,),
        in_specs=[
            pl.BlockSpec((tm, d), lambda i: (i, 0)),
            _resident((None, N_MOD, d), lambda i: (l, 0, 0)),
            _resident((None, N_ROWS, d), lambda i: (l, 0, 0)),
            _resident((nb, d, ci), lambda i: (0, 0, 0)),
            ANY_SPEC,
        ],
        out_specs=(pl.BlockSpec((tm, nb * ci), lambda i: (i, 0)), pl.BlockSpec((tm, d), lambda i: (i, 0))),
        out_shape=(jax.ShapeDtypeStruct((s, nb * ci), F32), jax.ShapeDtypeStruct((s, d), BF16)),
        compiler_params=_cparams(("parallel",), _nbytes((nb, d, ci), BF16) + 3 * _nbytes((tm, nb * ci), F32) + 8 * _nbytes((tm, d), F32)),
    )(*_hbm(x), mod, rows, win_f, after)


def _mixer_core_fwd(proj, x, mod, rows, cwf, wr, wi, wa_f, wb_f, wo_f, l, tm, after):
    s, d = x.shape
    nh, bw, _ = wr.shape[1:]

    def body(proj_ref, x_ref, mod_ref, rows_ref, cw_ref, wr_ref, wi_ref, wa_ref, wb_ref, wo_ref, after_ref,
             x1_ref, hs_ref, yap_ref, ybp_ref, y_ref, cvbuf, xbbuf, a_s, b_s, hprev):
        i = pl.program_id(0)

        @pl.when(i == 0)
        def _():
            cvbuf[...] = jnp.zeros((SUBLANES, d), F32)
            xbbuf[...] = jnp.zeros((SUBLANES, d), F32)
            hprev[...] = jnp.zeros((SUBLANES, d), F32)

        def row(r):
            return rows_ref[r:r + 1, :]

        def tap(r):
            return cw_ref[r:r + 1, :]

        ba = proj_ref[:, 0:d]
        cv = proj_ref[:, d:2 * d] * proj_ref[:, 2 * d:3 * d]
        cvt = cvbuf[...]
        conv3 = ((row(R_CONV_A_B) + _shift_rows(cv, cvt, 2, False) * tap(CW_A)) + _shift_rows(cv, cvt, 1, False) * tap(CW_A + 1)) + cv * tap(CW_A + 2)
        ya = ba * conv3
        cvbuf[...] = cv[tm - SUBLANES:, :]
        xb = proj_ref[:, 3 * d:4 * d]
        xbt = xbbuf[...]
        xc = (((row(R_CONV_B_B) + _shift_rows(xb, xbt, 3, False) * tap(CW_B)) + _shift_rows(xb, xbt, 2, False) * tap(CW_B + 1))
              + _shift_rows(xb, xbt, 1, False) * tap(CW_B + 2)) + xb * tap(CW_B + 3)
        xbbuf[...] = xb[tm - SUBLANES:, :]
        sp = _softplus_neg(row(R_LAMBDA))
        _, _, ig, a, mult = _gates(xc, wr_ref, wi_ref, row(R_B_GATE_R), row(R_B_GATE_I), sp, nh, bw)
        a_s[...] = a
        b_s[...] = mult * (ig * xc)

        def blk(j, hp):
            o = pl.multiple_of(j * SUBLANES, SUBLANES)
            a8, b8 = _scan_block(a_s[pl.ds(o, SUBLANES), :], b_s[pl.ds(o, SUBLANES), :], reverse=False)
            h8 = b8 + a8 * hp
            hs_ref[pl.ds(o, SUBLANES), :] = h8
            return jnp.broadcast_to(h8[SUBLANES - 1:SUBLANES, :], (SUBLANES, d))

        hprev[...] = lax.fori_loop(0, tm // SUBLANES, blk, hprev[...])
        gel, _ = _gelu(proj_ref[:, 4 * d:5 * d])
        yb = hs_ref[...] * gel
        yap = _dot(ya.astype(BF16), wa_ref[...])
        ybp = _dot(yb.astype(BF16), wb_ref[...])
        yap_ref[...] = yap
        ybp_ref[...] = ybp
        m = _sigmoid(proj_ref[:, 5 * d:6 * d]) * yap + _sigmoid(proj_ref[:, 6 * d:7 * d]) * ybp
        y = _dot(m.astype(BF16), wo_ref[...])
        y_ref[...] = y
        x1_ref[...] = x_ref[...] + mod_ref[M_GT_M:M_GT_M + 1, :] * ((y * _rms(y)) * row(R_G_POST_MIX))

    tile = pl.BlockSpec((tm, d), lambda i: (i, 0))
    return pl.pallas_call(
        body,
        name="mixer_core_fwd",
        grid=(s // tm,),
        in_specs=[
            pl.BlockSpec((tm, 7 * d), lambda i: (i, 0)),
            tile,
            _resident((None, N_MOD, d), lambda i: (l, 0, 0)),
            _resident((None, N_ROWS, d), lambda i: (l, 0, 0)),
            _resident((None, CW_ROWS, d), lambda i: (l, 0, 0)),
            _resident((None, nh, bw, bw), lambda i: (l, 0, 0, 0)),
            _resident((None, nh, bw, bw), lambda i: (l, 0, 0, 0)),
            _resident((d, d), lambda i: (0, 0)),
            _resident((d, d), lambda i: (0, 0)),
            _resident((d, d), lambda i: (0, 0)),
            ANY_SPEC,
        ],
        out_specs=(tile,) * 5,
        out_shape=(jax.ShapeDtypeStruct((s, d), F32),) * 5,
        scratch_shapes=[
            pltpu.VMEM((SUBLANES, d), F32),
            pltpu.VMEM((SUBLANES, d), F32),
            pltpu.VMEM((tm, d), F32),
            pltpu.VMEM((tm, d), F32),
            pltpu.VMEM((SUBLANES, d), F32),
        ],
        compiler_params=_cparams(("arbitrary",), 3 * _nbytes((d, d), BF16) + 2 * _nbytes((tm, 7 * d), F32) + 40 * _nbytes((tm, d), F32)),
    )(*_hbm(proj, x), mod, rows, cwf, wr, wi, wa_f, wb_f, wo_f, after)


def _mlp_fwd(x1, mod, rows, wup_f, wdn_f, l, tm, after):
    s, d = x1.shape
    nb, _, cu = wup_f.shape
    dff = nb * cu

    def body(x1_ref, mod_ref, rows_ref, wu_ref, wd_ref, after_ref, x2_ref, ru_ref, y2_ref, h2_ref):
        xv = x1_ref[...]
        g = rows_ref[R_G_PRE_MLP:R_G_PRE_MLP + 1, :]
        h2 = ((xv * _rms(xv) * g) * (1.0 + mod_ref[M_SC_F:M_SC_F + 1, :]) + mod_ref[M_SH_F:M_SH_F + 1, :]).astype(BF16)
        h2_ref[...] = h2
        ru = jnp.concatenate([jnp.maximum(_dot(h2, wu_ref[j]), 0.0) for j in range(nb)], axis=1)
        ru_ref[...] = ru.astype(BF16)
        y2 = _dot((ru * ru).astype(BF16), wd_ref[...])
        y2_ref[...] = y2
        x2_ref[...] = xv + mod_ref[M_GT_F:M_GT_F + 1, :] * ((y2 * _rms(y2)) * rows_ref[R_G_POST_MLP:R_G_POST_MLP + 1, :])

    tile = pl.BlockSpec((tm, d), lambda i: (i, 0))
    wide = pl.BlockSpec((tm, dff), lambda i: (i, 0))
    return pl.pallas_call(
        body,
        name="mlp_fwd",
        grid=(s // tm,),
        in_specs=[
            tile,
            _resident((None, N_MOD, d), lambda i: (l, 0, 0)),
            _resident((None, N_ROWS, d), lambda i: (l, 0, 0)),
            _resident((nb, d, cu), lambda i: (0, 0, 0)),
            _resident((dff, d), lambda i: (0, 0)),
            ANY_SPEC,
        ],
        out_specs=(tile, wide, tile, tile),
        out_shape=(jax.ShapeDtypeStruct((s, d), F32), jax.ShapeDtypeStruct((s, dff), BF16),
                   jax.ShapeDtypeStruct((s, d), F32), jax.ShapeDtypeStruct((s, d), BF16)),
        compiler_params=_cparams(("parallel",), 2 * _nbytes((dff, d), BF16) + 5 * _nbytes((tm, dff), F32) + 12 * _nbytes((tm, d), F32)),
    )(*_hbm(x1), mod, rows, wup_f, wdn_f, after)


def _loss_fwd_bwd(y, target, tm):
    s, d = y.shape

    def body(y_ref, t_ref, loss_ref, dy_ref):
        @pl.when(pl.program_id(0) == 0)
        def _():
            loss_ref[...] = jnp.zeros(loss_ref.shape, F32)

        e = y_ref[...] - t_ref[...]
        dy_ref[...] = e * (1.0 / d)
        loss_ref[...] += 0.5 * jnp.sum(jnp.mean(e * e, axis=-1, keepdims=True), axis=0, keepdims=True)

    tile = pl.BlockSpec((tm, d), lambda i: (i, 0))
    loss, dy = pl.pallas_call(
        body,
        name="loss",
        grid=(s // tm,),
        in_specs=[tile, tile],
        out_specs=(pl.BlockSpec((SUBLANES, 128), lambda i: (0, 0)), tile),
        out_shape=(jax.ShapeDtypeStruct((SUBLANES, 128), F32), jax.ShapeDtypeStruct((s, d), F32)),
        compiler_params=_cparams(("arbitrary",)),
    )(*_hbm(y, target))
    return loss[0, 0], dy


def _mlp_bwd(dx2, x1, y2, ru, mod, rows, wup_f, wdn_f, l, tm):
    s, d = x1.shape
    nb, _, cu = wup_f.shape
    dff = nb * cu

    def body(dx2_ref, x1_ref, y2_ref, ru_ref, mod_ref, rows_ref, wu_ref, wd_ref, dx1_ref, dy2_ref, dup_ref, act_ref, sm_ref):
        @pl.when(pl.program_id(0) == 0)
        def _():
            sm_ref[...] = jnp.zeros(sm_ref.shape, F32)

        dout = dx2_ref[...]
        dy2, dgt, dgpost = _postnorm_bwd(y2_ref[...], dout, rows_ref[R_G_POST_MLP:R_G_POST_MLP + 1, :], mod_ref[M_GT_F:M_GT_F + 1, :])
        dy2b = dy2.astype(BF16)
        dy2_ref[...] = dy2b
        ruv = ru_ref[...].astype(F32)
        act_ref[...] = (ruv * ruv).astype(BF16)
        dup = (_dot_nt(dy2b, wd_ref[...]) * (2.0 * ruv)).astype(BF16)
        dup_ref[...] = dup
        dh2 = _dot_nt(dup[:, 0:cu], wu_ref[0])
        for j in range(1, nb):
            dh2 = dh2 + _dot_nt(dup[:, j * cu:(j + 1) * cu], wu_ref[j])
        dxn, dsc, dsh, dgpre = _prenorm_bwd(x1_ref[...], dh2, rows_ref[R_G_PRE_MLP:R_G_PRE_MLP + 1, :], mod_ref[M_SC_F:M_SC_F + 1, :])
        dx1_ref[...] = dout + dxn
        for r, v in ((G_MLP_GT, dgt), (G_MLP_GPOST, dgpost), (G_MLP_SC, dsc), (G_MLP_SH, dsh), (G_MLP_GPRE, dgpre)):
            sm_ref[r:r + 1, :] += v

    tile = pl.BlockSpec((tm, d), lambda i: (i, 0))
    wide = pl.BlockSpec((tm, dff), lambda i: (i, 0))
    return pl.pallas_call(
        body,
        name="mlp_bwd",
        grid=(s // tm,),
        in_specs=[
            tile, tile, tile, wide,
            _resident((None, N_MOD, d), lambda i: (l, 0, 0)),
            _resident((None, N_ROWS, d), lambda i: (l, 0, 0)),
            _resident((nb, d, cu), lambda i: (0, 0, 0)),
            _resident((dff, d), lambda i: (0, 0)),
        ],
        out_specs=(tile, tile, wide, wide, pl.BlockSpec((G_MLP_ROWS, d), lambda i: (0, 0))),
        out_shape=(jax.ShapeDtypeStruct((s, d), F32), jax.ShapeDtypeStruct((s, d), BF16), jax.ShapeDtypeStruct((s, dff), BF16),
                   jax.ShapeDtypeStruct((s, dff), BF16), jax.ShapeDtypeStruct((G_MLP_ROWS, d), F32)),
        compiler_params=_cparams(("arbitrary",), 2 * _nbytes((dff, d), BF16) + 6 * _nbytes((tm, dff), F32) + 16 * _nbytes((tm, d), F32)),
    )(*_hbm(dx2, x1, y2, ru), mod, rows, wup_f, wdn_f)


def _mixer_core_bwd(dx1, y, yap, ybp, hs, proj, mod, rows, cwf, wr, wi, wa_f, wb_f, wo_f, l, tm, after):
    s, d = dx1.shape
    nh, bw, _ = wr.shape[1:]
    nt = s // tm
    per = tm // SUBLANES

    def body(dx1_ref, y_ref, yap_ref, ybp_ref, hs_ref, hsh_ref, proj_ref, projh_ref, mod_ref, rows_ref, cw_ref,
             wr_ref, wi_ref, wa_ref, wb_ref, wo_ref, after_ref,
             dproj_ref, dy_ref, m_ref, dyap_ref, dybp_ref, ya_ref, yb_ref, sm_ref, dwg_ref,
             abuf, dcbuf, dxbuf, al_s, dh_s, lam_s, lnext):
        i = pl.program_id(0)
        first_tile = i == nt - 1

        @pl.when(i == 0)
        def _():
            sm_ref[...] = jnp.zeros(sm_ref.shape, F32)
            dwg_ref[...] = jnp.zeros(dwg_ref.shape, F32)
            zero = jnp.zeros((SUBLANES, d), F32)
            abuf[...] = zero
            dcbuf[...] = zero
            dxbuf[...] = zero
            lnext[...] = zero

        def row(r):
            return rows_ref[r:r + 1, :]

        def tap(r):
            return cw_ref[r:r + 1, :]

        def acc(r, v):
            sm_ref[r:r + 1, :] += v

        keep_halo = jnp.where(first_tile, 0.0, 1.0)
        dy, dgt, dgpost = _postnorm_bwd(y_ref[...], dx1_ref[...], row(R_G_POST_MIX), mod_ref[M_GT_M:M_GT_M + 1, :])
        acc(G_MIX_GT, dgt)
        acc(G_MIX_GPOST, dgpost)
        dyb16 = dy.astype(BF16)
        dy_ref[...] = dyb16
        dm = _dot_nt(dyb16, wo_ref[...])
        sa = _sigmoid(proj_ref[:, 5 * d:6 * d])
        sb = _sigmoid(proj_ref[:, 6 * d:7 * d])
        yap = yap_ref[...]
        ybp = ybp_ref[...]
        m_ref[...] = (sa * yap + sb * ybp).astype(BF16)
        dyap_f = dm * sa
        dybp_f = dm * sb
        dyap = dyap_f.astype(BF16)
        dybp = dybp_f.astype(BF16)
        dyap_ref[...] = dyap
        dybp_ref[...] = dybp
        dproj_ref[:, 5 * d:6 * d] = (dyap_f * yap * (1.0 - sa)).astype(BF16)
        dproj_ref[:, 6 * d:7 * d] = (dybp_f * ybp * (1.0 - sb)).astype(BF16)
        dya = _dot_nt(dyap, wa_ref[...])
        dyb = _dot_nt(dybp, wb_ref[...])
        ba = proj_ref[:, 0:d]
        ca = proj_ref[:, d:2 * d]
        va = proj_ref[:, 2 * d:3 * d]
        cv = ca * va
        cvh = keep_halo * (projh_ref[:, d:2 * d] * projh_ref[:, 2 * d:3 * d])
        cvm2 = _shift_rows(cv, cvh, 2, False)
        cvm1 = _shift_rows(cv, cvh, 1, False)
        conv3 = ((row(R_CONV_A_B) + cvm2 * tap(CW_A)) + cvm1 * tap(CW_A + 1)) + cv * tap(CW_A + 2)
        ya_ref[...] = (ba * conv3).astype(BF16)
        dproj_ref[:, 0:d] = (dya * conv3).astype(BF16)
        dc3 = dya * ba
        acc(G_MIX_CAB, _colsum(dc3))
        acc(G_MIX_CAW, _colsum(dc3 * cvm2))
        acc(G_MIX_CAW + 1, _colsum(dc3 * cvm1))
        acc(G_MIX_CAW + 2, _colsum(dc3 * cv))
        dct = dcbuf[...]
        dcv = (dc3 * tap(CW_A + 2) + _shift_rows(dc3, dct, 1, True) * tap(CW_A + 1)) + _shift_rows(dc3, dct, 2, True) * tap(CW_A)
        dcbuf[...] = dc3[:SUBLANES, :]
        dproj_ref[:, d:2 * d] = (dcv * va).astype(BF16)
        dproj_ref[:, 2 * d:3 * d] = (dcv * ca).astype(BF16)
        xb = proj_ref[:, 3 * d:4 * d]
        gb = proj_ref[:, 4 * d:5 * d]
        xbh = keep_halo * projh_ref[:, 3 * d:4 * d]
        xm3 = _shift_rows(xb, xbh, 3, False)
        xm2 = _shift_rows(xb, xbh, 2, False)
        xm1 = _shift_rows(xb, xbh, 1, False)
        xc = (((row(R_CONV_B_B) + xm3 * tap(CW_B)) + xm2 * tap(CW_B + 1)) + xm1 * tap(CW_B + 2)) + xb * tap(CW_B + 3)
        lam = row(R_LAMBDA)
        sp = _softplus_neg(lam)
        xcb, r, ig, a, mult = _gates(xc, wr_ref, wi_ref, row(R_B_GATE_R), row(R_B_GATE_I), sp, nh, bw)
        gel, gsig = _gelu(gb)
        hs = hs_ref[...]
        yb_ref[...] = (hs * gel).astype(BF16)
        dproj_ref[:, 4 * d:5 * d] = (dyb * hs * _gelu_grad(gb, gsig)).astype(BF16)
        al_s[...] = _shift_rows(a, abuf[...], 1, True)
        abuf[...] = a[:SUBLANES, :]
        dh_s[...] = dyb * gel

        def blk(j, ln):
            o = pl.multiple_of((per - 1 - j) * SUBLANES, SUBLANES)
            a8, b8 = _scan_block(al_s[pl.ds(o, SUBLANES), :], dh_s[pl.ds(o, SUBLANES), :], reverse=True)
            l8 = b8 + a8 * ln
            lam_s[pl.ds(o, SUBLANES), :] = l8
            return jnp.broadcast_to(l8[0:1, :], (SUBLANES, d))

        lnext[...] = lax.fori_loop(0, per, blk, lnext[...])
        dbb = lam_s[...]
        da = dbb * _shift_rows(hs, keep_halo * hsh_ref[...], 1, False)
        dbx = dbb * xc
        dmult = dbx * ig
        dig = dbx * mult
        dxc = (dbb * mult) * ig
        dla = a * (da - (dmult * a) / mult)
        dlar = dla * r
        acc(G_MIX_LAM, _colsum(dlar) * (LRU_C * _sigmoid(-lam)))
        dzr = (dlar * (1.0 - r)) * (-LRU_C * sp)
        dzi = dig * ig * (1.0 - ig)
        acc(G_MIX_BR, _colsum(dzr))
        acc(G_MIX_BI, _colsum(dzi))
        dzrb = dzr.astype(BF16)
        dzib = dzi.astype(BF16)
        back = []
        for h in range(nh):
            sl = slice(h * bw, (h + 1) * bw)
            back.append(_dot_nt(dzrb[:, sl], wr_ref[h]) + _dot_nt(dzib[:, sl], wi_ref[h]))
            dwg_ref[0, h] += _dot_tn(xcb[:, sl], dzrb[:, sl])
            dwg_ref[1, h] += _dot_tn(xcb[:, sl], dzib[:, sl])
        dxc = dxc + jnp.concatenate(back, axis=1)
        acc(G_MIX_CBB, _colsum(dxc))
        acc(G_MIX_CBW, _colsum(dxc * xm3))
        acc(G_MIX_CBW + 1, _colsum(dxc * xm2))
        acc(G_MIX_CBW + 2, _colsum(dxc * xm1))
        acc(G_MIX_CBW + 3, _colsum(dxc * xb))
        dxt = dxbuf[...]
        dxb = (((dxc * tap(CW_B + 3) + _shift_rows(dxc, dxt, 1, True) * tap(CW_B + 2)) + _shift_rows(dxc, dxt, 2, True) * tap(CW_B + 1))
               + _shift_rows(dxc, dxt, 3, True) * tap(CW_B))
        dxbuf[...] = dxc[:SUBLANES, :]
        dproj_ref[:, 3 * d:4 * d] = dxb.astype(BF16)

    def rev(i):
        return (nt - 1 - i, 0)

    def halo(i):
        return (jnp.maximum((nt - 1 - i) * per - 1, 0), 0)

    tile = pl.BlockSpec((tm, d), rev)
    return pl.pallas_call(
        body,
        name="mixer_core_bwd",
        grid=(nt,),
        in_specs=[
            tile, tile, tile, tile, tile,
            pl.BlockSpec((SUBLANES, d), halo),
            pl.BlockSpec((tm, 7 * d), rev),
            pl.BlockSpec((SUBLANES, 7 * d), halo),
            _resident((None, N_MOD, d), lambda i: (l, 0, 0)),
            _resident((None, N_ROWS, d), lambda i: (l, 0, 0)),
            _resident((None, CW_ROWS, d), lambda i: (l, 0, 0)),
            _resident((None, nh, bw, bw), lambda i: (l, 0, 0, 0)),
            _resident((None, nh, bw, bw), lambda i: (l, 0, 0, 0)),
            _resident((d, d), lambda i: (0, 0)),
            _resident((d, d), lambda i: (0, 0)),
            _resident((d, d), lambda i: (0, 0)),
            ANY_SPEC,
        ],
        out_specs=(pl.BlockSpec((tm, 7 * d), rev),) + (tile,) * 6 + (
            pl.BlockSpec((G_MIX_ROWS, d), lambda i: (0, 0)), pl.BlockSpec((2, nh, bw, bw), lambda i: (0, 0, 0, 0))),
        out_shape=(jax.ShapeDtypeStruct((s, 7 * d), BF16),) + (jax.ShapeDtypeStruct((s, d), BF16),) * 6 + (
            jax.ShapeDtypeStruct((G_MIX_ROWS, d), F32), jax.ShapeDtypeStruct((2, nh, bw, bw), F32)),
        scratch_shapes=[pltpu.VMEM((SUBLANES, d), F32)] * 3 + [pltpu.VMEM((tm, d), F32)] * 3 + [pltpu.VMEM((SUBLANES, d), F32)],
        compiler_params=_cparams(("arbitrary",), 3 * _nbytes((d, d), BF16) + 3 * _nbytes((tm, 7 * d), F32) + 64 * _nbytes((tm, d), F32)),
    )(*_hbm(dx1, y, yap, ybp, hs, hs, proj, proj), mod, rows, cwf, wr, wi, wa_f, wb_f, wo_f, after)


def _in_proj_bwd(dproj, x, dx1, mod, rows, win_f, l, tm, after):
    s, d = x.shape
    nb, _, ci = win_f.shape

    def body(dp_ref, x_ref, dx1_ref, mod_ref, rows_ref, w_ref, after_ref, dx_ref, sm_ref):
        @pl.when(pl.program_id(0) == 0)
        def _():
            sm_ref[...] = jnp.zeros(sm_ref.shape, F32)

        dh = _dot_nt(dp_ref[:, 0:ci], w_ref[0])
        for j in range(1, nb):
            dh = dh + _dot_nt(dp_ref[:, j * ci:(j + 1) * ci], w_ref[j])
        dxn, dsc, dsh, dg = _prenorm_bwd(x_ref[...], dh, rows_ref[R_G_PRE_MIX:R_G_PRE_MIX + 1, :], mod_ref[M_SC_M:M_SC_M + 1, :])
        dx_ref[...] = dx1_ref[...] + dxn
        for r, v in ((G_IN_SC, dsc), (G_IN_SH, dsh), (G_IN_GPRE, dg)):
            sm_ref[r:r + 1, :] += v

    tile = pl.BlockSpec((tm, d), lambda i: (i, 0))
    return pl.pallas_call(
        body,
        name="in_proj_bwd",
        grid=(s // tm,),
        in_specs=[
            pl.BlockSpec((tm, nb * ci), lambda i: (i, 0)), tile, tile,
            _resident((None, N_MOD, d), lambda i: (l, 0, 0)),
            _resident((None, N_ROWS, d), lambda i: (l, 0, 0)),
            _resident((nb, d, ci), lambda i: (0, 0, 0)),
            ANY_SPEC,
        ],
        out_specs=(tile, pl.BlockSpec((G_IN_ROWS, d), lambda i: (0, 0))),
        out_shape=(jax.ShapeDtypeStruct((s, d), F32), jax.ShapeDtypeStruct((G_IN_ROWS, d), F32)),
        compiler_params=_cparams(("arbitrary",), _nbytes((nb, d, ci), BF16) + 2 * _nbytes((tm, nb * ci), BF16) + 16 * _nbytes((tm, d), F32)),
    )(*_hbm(dproj, x, dx1), mod, rows, win_f, after)


def _wgrad(a, b, cols_owned, ts, after):
    s, k1 = a.shape
    k2 = b.shape[1]
    ns = s // ts
    if cols_owned:
        nblk, bk1, bk2 = N_DEV, k1, k2 // N_DEV
        a_spec = pl.BlockSpec((ts, bk1), lambda j, t: (t, 0))
        b_spec = pl.BlockSpec((ts, bk2), lambda j, t: (t, j))
    else:
        bk1, bk2 = min(WGRAD_ROWS, k1), k2
        nblk = k1 // bk1
        a_spec = pl.BlockSpec((ts, bk1), lambda j, t: (t, j))
        b_spec = pl.BlockSpec((ts, bk2), lambda j, t: (t, 0))

    def body(a_ref, b_ref, after_ref, o_ref, acc_ref):
        t = pl.program_id(1)

        @pl.when(t == 0)
        def _():
            acc_ref[...] = jnp.zeros(acc_ref.shape, F32)

        acc_ref[...] += _dot_tn(a_ref[...], b_ref[...])

        @pl.when(t == ns - 1)
        def _():
            o_ref[...] = acc_ref[...].astype(BF16)

    out = pl.pallas_call(
        body,
        name="wgrad",
        grid=(nblk, ns),
        in_specs=[a_spec, b_spec, ANY_SPEC],
        out_specs=pl.BlockSpec((None, bk1, bk2), lambda j, t: (j, 0, 0)),
        out_shape=pltpu.HBM((nblk, bk1, bk2), BF16),
        scratch_shapes=[pltpu.VMEM((bk1, bk2), F32)],
        compiler_params=_cparams(("parallel", "arbitrary"), 4 * _nbytes((bk1, bk2), F32) + 4 * _nbytes((ts, bk1 + bk2), BF16)),
    )(pltpu.with_memory_space_constraint(a, pltpu.HBM), pltpu.with_memory_space_constraint(b, pltpu.HBM), after)
    return out if cols_owned else out.reshape(N_DEV, k1 // N_DEV, k2)


def _adam_update(w, g, m, v):
    m = ADAM_B1 * m + (1.0 - ADAM_B1) * g
    v = ADAM_B2 * v + (1.0 - ADAM_B2) * (g * g)
    m_hat = m / (1.0 - ADAM_B1 ** ADAM_STEP)
    v_hat = v / (1.0 - ADAM_B2 ** ADAM_STEP)
    delta = -ADAM_LR * (m_hat / (jnp.sqrt(v_hat) + ADAM_EPS) + ADAM_WD * w)
    return delta, m, v


def _sum_adamw(recv, w, m, v, tr, after):
    nl, ra, cb = w.shape
    assert nl == len(recv) == 2

    def body(r0_ref, r1_ref, w_ref, m_ref, v_ref, after_ref, g_ref, d_ref, nm_ref, nv_ref):
        def total(r_ref):
            g = r_ref[0].astype(F32)
            for j in range(1, N_DEV):
                g = g + r_ref[j].astype(F32)
            return g

        g = jnp.where(pl.program_id(0) == 0, total(r0_ref), total(r1_ref))
        g_ref[...] = g
        d_ref[...], nm_ref[...], nv_ref[...] = _adam_update(w_ref[...], g, m_ref[...], v_ref[...])

    blk = pl.BlockSpec((None, tr, cb), lambda l, i: (l, i, 0))
    return pl.pallas_call(
        body,
        name="sum_adamw",
        grid=(nl, ra // tr),
        in_specs=[pl.BlockSpec((N_DEV, tr, cb), lambda l, i: (0, i * (1 - l), 0)),
                  pl.BlockSpec((N_DEV, tr, cb), lambda l, i: (0, i * l, 0)), blk, blk, blk, ANY_SPEC],
        out_specs=(blk,) * 4,
        out_shape=(jax.ShapeDtypeStruct((nl, ra, cb), F32),) * 4,
        compiler_params=_cparams(("arbitrary", "arbitrary"), 6 * _nbytes((N_DEV, tr, cb), BF16) + 32 * _nbytes((tr, cb), F32)),
    )(recv[0], recv[1], w, m, v, after)


def _adamw(w, g, m, v):
    def body(w_ref, g_ref, m_ref, v_ref, d_ref, nm_ref, nv_ref):
        d_ref[...], nm_ref[...], nv_ref[...] = _adam_update(w_ref[...], g_ref[...], m_ref[...], v_ref[...])

    return pl.pallas_call(
        body,
        name="adamw",
        in_specs=[VMEM_SPEC] * 4,
        out_specs=(VMEM_SPEC,) * 3,
        out_shape=(jax.ShapeDtypeStruct(w.shape, F32),) * 3,
        compiler_params=_cparams(vmem=10 * _nbytes(w.shape, F32)),
    )(w, g, m, v)


def _adamw_tiled(w, g, m, v, tr):
    nl, ra, cb = w.shape

    def body(w_ref, g_ref, m_ref, v_ref, d_ref, nm_ref, nv_ref):
        d_ref[...], nm_ref[...], nv_ref[...] = _adam_update(w_ref[...], g_ref[...], m_ref[...], v_ref[...])

    blk = pl.BlockSpec((None, tr, cb), lambda l, i: (l, i, 0))
    return pl.pallas_call(
        body,
        name="adamw_tiled",
        grid=(nl, ra // tr),
        in_specs=[blk] * 4,
        out_specs=(blk,) * 3,
        out_shape=(jax.ShapeDtypeStruct((nl, ra, cb), F32),) * 3,
        compiler_params=_cparams(("parallel", "parallel")),
    )(*_hbm(w, g, m, v))


def _token_tile(s):
    return min(256, max(SUBLANES * 2, s // 4))


def kernel(x, c, w_mod, b_mod, g_pre_mix, g_post_mix, w_in, conv_a_w, conv_a_b, w_a_out, conv_b_w, conv_b_b, w_gate_r, b_gate_r, w_gate_i, b_gate_i, lru_lambda, w_b_out, w_o, g_pre_mlp, g_post_mlp, w_mlp_up, w_mlp_down, loss_target, m_w_mod, m_b_mod, m_g_pre_mix, m_g_post_mix, m_w_in, m_conv_a_w, m_conv_a_b, m_w_a_out, m_conv_b_w, m_conv_b_b, m_w_gate_r, m_b_gate_r, m_w_gate_i, m_b_gate_i, m_lru_lambda, m_w_b_out, m_w_o, m_g_pre_mlp, m_g_post_mlp, m_w_mlp_up, m_w_mlp_down, v_w_mod, v_b_mod, v_g_pre_mix, v_g_post_mix, v_w_in, v_conv_a_w, v_conv_a_b, v_w_a_out, v_conv_b_w, v_conv_b_b, v_w_gate_r, v_b_gate_r, v_w_gate_i, v_b_gate_i, v_lru_lambda, v_w_b_out, v_w_o, v_g_pre_mlp, v_g_post_mlp, v_w_mlp_up, v_w_mlp_down):
    nl = w_mod.shape[0]
    s, d = x.shape[1], x.shape[2]
    nh, bw = w_gate_r.shape[1], w_gate_r.shape[2]
    cwid = conv_a_w.shape[2]
    tm = _token_tile(s)
    tmx = min(2 * tm, s)
    ts = s
    _, me = _my_position()
    xs = x.reshape(s, d)
    target = loss_target.reshape(s, d)

    vec_names = (g_pre_mix, g_post_mix, conv_a_b, conv_b_b, b_gate_r, b_gate_i, lru_lambda, g_pre_mlp, g_post_mlp)
    rows = jnp.concatenate([jnp.stack(vec_names, axis=1), jnp.zeros((nl, N_ROWS - len(vec_names), d), F32)], axis=1)
    cw = jnp.concatenate([conv_a_w, conv_b_w, jnp.zeros((nl, CW_ROWS - 7, cwid), F32)], axis=1)

    w16 = {"w_in": w_in.astype(BF16), "w_a_out": w_a_out.astype(BF16), "w_b_out": w_b_out.astype(BF16), "w_o": w_o.astype(BF16),
           "w_mlp_up": w_mlp_up.astype(BF16), "w_mlp_down": w_mlp_down.astype(BF16)}
    groups = (("in", ("w_in",)), ("mix", ("w_a_out", "w_b_out", "w_o")), ("mlp", ("w_mlp_up", "w_mlp_down")))
    mod, cact, cwf = _prep_small(c, w_mod, b_mod, cw)
    mod = mod.reshape(nl, N_MOD, d)
    gathers = {}
    tok = mod
    for l in range(nl):
        for gname, members in groups:
            gathers[l, gname] = _gather2_start([w16[n][l] for n in members], tok, f"gather_start_{gname}{l}")
            tok = gathers[l, gname][4]
    wr = w_gate_r.astype(BF16)
    wi = w_gate_i.astype(BF16)

    forwarded = {}

    def forward(l, gname, after):
        forwarded[l, gname] = _gather2_forward(gathers[l, gname], after, f"gather_forward_{gname}{l}")
        return forwarded[l, gname][4]

    def gathered(l, gname, after):
        return _gather2_wait(forwarded[l, gname], after, f"gather_wait_{gname}{l}")

    saved = []
    weights = []
    xin = xs
    for l in range(nl):
        if l == 0:
            tok = forward(0, "in", tok)
        (win_f,) = gathered(l, "in", tok if l == 0 else xin)
        if l > 0:
            tok = forward(l, "mix", win_f)
        proj, h = _in_proj_fwd(xin, mod, rows, win_f, l, tmx, tok)
        if l == 0:
            tok = forward(0, "mix", proj)
        wa_f, wb_f, wo_f = (w.reshape(d, d) for w in gathered(l, "mix", proj))
        tok = forward(l, "mlp", wo_f)
        x1, hs, yap, ybp, y = _mixer_core_fwd(proj, xin, mod, rows, cwf, wr, wi, wa_f, wb_f, wo_f, l, tm, tok)
        wup_f, wdn_f = gathered(l, "mlp", x1)
        wdn_f = wdn_f.reshape(-1, d)
        if l + 1 < nl:
            tok = forward(l + 1, "in", wdn_f)
        x2, ru, y2, h2 = _mlp_fwd(x1, mod, rows, wup_f, wdn_f, l, tmx, tok)
        saved.append((xin, proj, h, x1, hs, yap, ybp, y, ru, y2, h2))
        weights.append((win_f, wa_f, wb_f, wo_f, wup_f, wdn_f))
        xin = x2
    loss_part, dx = _loss_fwd_bwd(xin, target, tm)

    scatters = {}
    small = [None] * nl
    gate_parts = [None] * nl

    def scatter(l, gname, parts):
        scatters[l, gname] = _exchange_start(parts, False, f"scatter_start_{gname}{l}")
        return scatters[l, gname][4]

    for l in reversed(range(nl)):
        xin, proj, h, x1, hs, yap, ybp, y, ru, y2, h2 = saved[l]
        win_f, wa_f, wb_f, wo_f, wup_f, wdn_f = weights[l]
        dx1, dy2, dup, act, sm_mlp = _mlp_bwd(dx, x1, y2, ru, mod, rows, wup_f, wdn_f, l, tm)
        g_up = _wgrad(h2, dup, True, ts, dx1)
        g_dn = _wgrad(act, dy2, False, ts, g_up)
        tok = scatter(l, "mlp", [g_up, g_dn])
        dproj, dy, m, dyap, dybp, ya, yb, sm_mix, dwg = _mixer_core_bwd(
            dx1, y, yap, ybp, hs, proj, mod, rows, cwf, wr, wi, wa_f, wb_f, wo_f, l, tm // 2, tok)
        gate_parts[l] = _exchange_start([dwg.astype(BF16)], True, f"gates_start{l}")
        g_a = _wgrad(ya, dyap, False, ts, gate_parts[l][4])
        g_b = _wgrad(yb, dybp, False, ts, g_a)
        g_o = _wgrad(m, dy, False, ts, g_b)
        tok = scatter(l, "mix", [g_a, g_b, g_o])
        tok = scatter(l, "in", [_wgrad(h, dproj, True, ts, tok)])
        dx, sm_in = _in_proj_bwd(dproj, xin, dx1, mod, rows, win_f, l, tmx, tok)
        small[l] = jnp.concatenate([sm_mlp, sm_mix, sm_in], axis=0)
    grad_x = dx.reshape(x.shape)

    recv = {}
    big = {}
    moments = {"w_in": (w_in, m_w_in, v_w_in), "w_mlp_up": (w_mlp_up, m_w_mlp_up, v_w_mlp_up), "w_a_out": (w_a_out, m_w_a_out, v_w_a_out),
               "w_b_out": (w_b_out, m_w_b_out, v_w_b_out), "w_o": (w_o, m_w_o, v_w_o), "w_mlp_down": (w_mlp_down, m_w_mlp_down, v_w_mlp_down)}

    def collect(l, gname, after):
        for n, zone in zip(dict(groups)[gname], _exchange_wait(scatters[l, gname], after, False, f"scatter_wait_{gname}{l}")):
            recv[n, l] = zone

    def update(name, after):
        w, m_, v_ = moments[name]
        big[name] = _sum_adamw([recv[name, l] for l in range(nl)], w, m_, v_, min(256, w.shape[1]), after)
        return big[name][1]

    for l, gname in ((1, "mlp"), (1, "in"), (1, "mix"), (0, "mlp")):
        collect(l, gname, dx)
    early = update("w_mlp_down", update("w_mlp_up", dx))

    lrows = jnp.concatenate(small, axis=0)
    lrows = lrows.at[G_LOSS_ROW, 0].set(loss_part)

    def lrow(a, l, r):
        return a[l * G_LAYER_ROWS + r]

    dm = jnp.stack([jnp.concatenate([lrow(lrows, l, G_MLP_ROWS + G_MIX_ROWS + G_IN_SH), lrow(lrows, l, G_MLP_ROWS + G_MIX_ROWS + G_IN_SC),
                                     lrow(lrows, l, G_MLP_ROWS + G_MIX_GT), lrow(lrows, l, G_MLP_SH), lrow(lrows, l, G_MLP_SC),
                                     lrow(lrows, l, G_MLP_GT)]) for l in range(nl)])
    dm8 = jnp.concatenate([dm[:, None, :], jnp.zeros((nl, SUBLANES - 1, N_MOD * d), F32)], axis=1)
    srows, g_w_mod = _reduce_small(lrows, dm8, cact, early)
    loss = srows[G_LOSS_ROW, 0]
    sgates = _sum_gathered([_exchange_wait(gate_parts[l], srows, True, f"gates_wait{l}")[0] for l in range(nl)])
    for gname in ("in", "mix"):
        collect(0, gname, srows)
    for name in ("w_in", "w_a_out", "w_b_out", "w_o"):
        update(name, srows)

    def srow(l, r):
        return lrow(srows, l, r)

    def per_layer(r):
        return jnp.stack([srow(l, r) for l in range(nl)])

    mix0 = G_MLP_ROWS
    in0 = G_MLP_ROWS + G_MIX_ROWS
    g_b_mod = jnp.stack([jnp.concatenate([srow(l, in0 + G_IN_SH), srow(l, in0 + G_IN_SC), srow(l, mix0 + G_MIX_GT),
                                          srow(l, G_MLP_SH), srow(l, G_MLP_SC), srow(l, G_MLP_GT)]) for l in range(nl)])
    conv_a_full = jnp.stack([jnp.stack([srow(l, mix0 + G_MIX_CAW + k) for k in range(3)]) for l in range(nl)])
    conv_b_full = jnp.stack([jnp.stack([srow(l, mix0 + G_MIX_CBW + k) for k in range(4)]) for l in range(nl)])
    grads = {
        "b_mod": g_b_mod,
        "g_pre_mix": per_layer(in0 + G_IN_GPRE),
        "g_post_mix": per_layer(mix0 + G_MIX_GPOST),
        "conv_a_w": lax.dynamic_slice_in_dim(conv_a_full, me * cwid, cwid, axis=2),
        "conv_a_b": per_layer(mix0 + G_MIX_CAB),
        "conv_b_w": lax.dynamic_slice_in_dim(conv_b_full, me * cwid, cwid, axis=2),
        "conv_b_b": per_layer(mix0 + G_MIX_CBB),
        "w_gate_r": sgates[:, 0],
        "b_gate_r": per_layer(mix0 + G_MIX_BR),
        "w_gate_i": sgates[:, 1],
        "b_gate_i": per_layer(mix0 + G_MIX_BI),
        "lru_lambda": per_layer(mix0 + G_MIX_LAM),
        "g_pre_mlp": per_layer(G_MLP_GPRE),
        "g_post_mlp": per_layer(G_MLP_GPOST),
    }
    params = {
        "b_mod": (b_mod, m_b_mod, v_b_mod), "g_pre_mix": (g_pre_mix, m_g_pre_mix, v_g_pre_mix), "g_post_mix": (g_post_mix, m_g_post_mix, v_g_post_mix),
        "conv_a_w": (conv_a_w, m_conv_a_w, v_conv_a_w), "conv_a_b": (conv_a_b, m_conv_a_b, v_conv_a_b),
        "conv_b_w": (conv_b_w, m_conv_b_w, v_conv_b_w), "conv_b_b": (conv_b_b, m_conv_b_b, v_conv_b_b),
        "w_gate_r": (w_gate_r, m_w_gate_r, v_w_gate_r), "b_gate_r": (b_gate_r, m_b_gate_r, v_b_gate_r),
        "w_gate_i": (w_gate_i, m_w_gate_i, v_w_gate_i), "b_gate_i": (b_gate_i, m_b_gate_i, v_b_gate_i),
        "lru_lambda": (lru_lambda, m_lru_lambda, v_lru_lambda), "g_pre_mlp": (g_pre_mlp, m_g_pre_mlp, v_g_pre_mlp),
        "g_post_mlp": (g_post_mlp, m_g_post_mlp, v_g_post_mlp),
    }
    out = {}
    for name, g in grads.items():
        w, m_, v_ = params[name]
        flat = (-1, w.shape[-1])
        dl, nm, nv = _adamw(w.reshape(flat), g.reshape(flat), m_.reshape(flat), v_.reshape(flat))
        out[name] = (g.reshape(w.shape), dl.reshape(w.shape), nm.reshape(w.shape), nv.reshape(w.shape))
    out["w_mod"] = (g_w_mod,) + tuple(_adamw_tiled(w_mod, g_w_mod, m_w_mod, v_w_mod, min(128, d)))
    out.update(big)

    order = ("w_mod", "b_mod", "g_pre_mix", "g_post_mix", "w_in", "conv_a_w", "conv_a_b", "w_a_out", "conv_b_w", "conv_b_b", "w_gate_r", "b_gate_r",
             "w_gate_i", "b_gate_i", "lru_lambda", "w_b_out", "w_o", "g_pre_mlp", "g_post_mlp", "w_mlp_up", "w_mlp_down")
    return (loss, grad_x) + tuple(out[n][0] for n in order) + tuple(out[n][1] for n in order) + tuple(out[n][2] for n in order) + tuple(out[n][3] for n in order)
```

```python
import functools

import jax
import jax.numpy as jnp
from jax import lax
from jax.experimental import pallas as pl
from jax.experimental.pallas import tpu as pltpu

F32, BF16 = jnp.float32, jnp.bfloat16
EPS = 1e-6
LRU_C = 8.0
N_DEV = 8
N_MOD = 6
SUBLANES = 8
VMEM_BUDGET = 56 * 1024 * 1024
WGRAD_ROWS = 512
ADAM_LR, ADAM_B1, ADAM_B2, ADAM_EPS, ADAM_WD, ADAM_STEP = 0.001, 0.9, 0.999, 1e-08, 0.01, 10
MESH = pl.DeviceIdType.MESH
VMEM_SPEC = pl.BlockSpec(memory_space=pltpu.VMEM)
ANY_SPEC = pl.BlockSpec(memory_space=pl.ANY)
HBM_SPEC = pl.BlockSpec(memory_space=pltpu.HBM)
SEM_SPEC = pl.BlockSpec(memory_space=pltpu.SEMAPHORE)
SIDE_EFFECT = pltpu.SideEffectType.DATAFLOW_SIDE_EFFECTING

R_G_PRE_MIX, R_G_POST_MIX, R_CONV_A_B, R_CONV_B_B, R_B_GATE_R, R_B_GATE_I, R_LAMBDA, R_G_PRE_MLP, R_G_POST_MLP = range(9)
N_ROWS = 16
M_SH_M, M_SC_M, M_GT_M, M_SH_F, M_SC_F, M_GT_F = range(6)
CW_A, CW_B, CW_ROWS = 0, 3, 8
G_MLP_GT, G_MLP_GPOST, G_MLP_SC, G_MLP_SH, G_MLP_GPRE, G_LOSS_ROW, G_MLP_ROWS = 0, 1, 2, 3, 4, 7, 8
(G_MIX_GT, G_MIX_GPOST, G_MIX_CAB, G_MIX_CAW, G_MIX_CBB, G_MIX_CBW, G_MIX_BR, G_MIX_BI, G_MIX_LAM) = 0, 1, 2, 3, 6, 7, 11, 12, 13
G_MIX_ROWS = 16
G_IN_SC, G_IN_SH, G_IN_GPRE, G_IN_ROWS = 0, 1, 2, 8
G_LAYER_ROWS = G_MLP_ROWS + G_MIX_ROWS + G_IN_ROWS


def _cparams(dims=None, vmem=None):
    kw = {}
    if dims is not None:
        kw["dimension_semantics"] = dims
    if vmem is not None:
        kw["vmem_limit_bytes"] = int(min(max(vmem, 16 * 1024 * 1024), VMEM_BUDGET))
    return pltpu.CompilerParams(**kw)


def _nbytes(shape, dtype):
    n = 1
    for s in shape:
        n *= s
    return n * jnp.dtype(dtype).itemsize


def _hbm(*arrays):
    return tuple(pltpu.with_memory_space_constraint(a, pltpu.HBM) for a in arrays)


def _resident(block, index_map):
    return pl.BlockSpec(block, index_map, pipeline_mode=pl.Buffered(1))


def _my_position():
    x, y, c = lax.axis_index("x"), lax.axis_index("y"), lax.axis_index("c")
    return (x, y, c), 4 * x + 2 * y + c


def _peer(pos, k):
    x, y, c = pos
    px = 1 - x if k & 4 else x
    py = 1 - y if k & 2 else y
    pc = 1 - c if k & 1 else c
    return (px, py, pc), 4 * px + 2 * py + pc


def _remote(src, dst, ssem, rsem, peer):
    return pltpu.make_async_remote_copy(src_ref=src, dst_ref=dst, send_sem=ssem, recv_sem=rsem, device_id=peer, device_id_type=MESH)


def _dot(a, b):
    return jnp.dot(a, b, preferred_element_type=F32)


def _dot_nt(a, b):
    return lax.dot_general(a, b, (((1,), (1,)), ((), ())), preferred_element_type=F32)


def _dot_tn(a, b):
    return lax.dot_general(a, b, (((0,), (0,)), ((), ())), preferred_element_type=F32)


def _colsum(v):
    return jnp.sum(v, axis=0, keepdims=True)


def _sigmoid(v):
    return jax.nn.sigmoid(v)


GELU_K, GELU_C = 0.7978845608028654, 0.044715


def _gelu(v):
    s = 1.0 / (1.0 + jnp.exp(v * (-2.0 * GELU_K - (2.0 * GELU_K * GELU_C) * (v * v))))
    return v * s, s


def _gelu_grad(v, s):
    return s * (1.0 + (v * (1.0 - s)) * (2.0 * GELU_K + (6.0 * GELU_K * GELU_C) * (v * v)))


def _neg_expm1_twice(v):
    t = jnp.tanh(v)
    return (-2.0 * t) / (1.0 - t)


def _softplus_neg(lam):
    z = -lam
    u = jnp.exp(-jnp.abs(z))
    w = 1.0 + u
    l1p = jnp.where(w == 1.0, u, jnp.log(w) * u / (w - 1.0))
    return jnp.maximum(z, 0.0) + l1p


def _rms(v):
    return lax.rsqrt(jnp.mean(v * v, axis=-1, keepdims=True) + EPS)


def _prenorm_bwd(xv, dh, g, sc):
    r = _rms(xv)
    xn = xv * r
    n = xn * g
    dsc = _colsum(dh * n)
    dsh = _colsum(dh)
    dn = dh * (1.0 + sc)
    dg = _colsum(dn * xn)
    dxn = dn * g
    dx = r * (dxn - xn * jnp.mean(dxn * xn, axis=-1, keepdims=True))
    return dx, dsc, dsh, dg


def _postnorm_bwd(yv, dout, g, gt):
    r = _rms(yv)
    yn = yv * r
    dgt = _colsum(dout * (yn * g))
    dn = dout * gt
    dg = _colsum(dn * yn)
    dyn = dn * g
    dy = r * (dyn - yn * jnp.mean(dyn * yn, axis=-1, keepdims=True))
    return dy, dgt, dg


def _gates(xc, wr_ref, wi_ref, b_r, b_i, sp, nh, bw):
    xcb = xc.astype(BF16)
    zr = jnp.concatenate([_dot(xcb[:, h * bw:(h + 1) * bw], wr_ref[h]) for h in range(nh)], axis=1) + b_r
    zi = jnp.concatenate([_dot(xcb[:, h * bw:(h + 1) * bw], wi_ref[h]) for h in range(nh)], axis=1) + b_i
    r = _sigmoid(zr)
    ig = _sigmoid(zi)
    la = (-LRU_C * r) * sp
    a = jnp.exp(la)
    mult = jnp.sqrt(_neg_expm1_twice(la))
    return xcb, r, ig, a, mult


def _shift_rows(cur, edge, k, up):
    t, dd = cur.shape
    blocks = cur.reshape(t // SUBLANES, SUBLANES, dd)
    row = lax.broadcasted_iota(jnp.int32, (1, SUBLANES, dd), 1)
    if up:
        r = pltpu.roll(blocks, SUBLANES - k, 1)
        nxt = jnp.concatenate([r[1:], pltpu.roll(edge, SUBLANES - k, 0)[None]], axis=0)
        out = jnp.where(row >= SUBLANES - k, nxt, r)
    else:
        r = pltpu.roll(blocks, k, 1)
        prv = jnp.concatenate([pltpu.roll(edge, k, 0)[None], r[:-1]], axis=0)
        out = jnp.where(row < k, prv, r)
    return out.reshape(t, dd)


def _scan_block(a8, b8, reverse):
    row = lax.broadcasted_iota(jnp.int32, a8.shape, 0)
    for s in (1, 2, 4):
        if reverse:
            keep = row < SUBLANES - s
            a_sh = pltpu.roll(a8, SUBLANES - s, 0)
            b_sh = pltpu.roll(b8, SUBLANES - s, 0)
        else:
            keep = row >= s
            a_sh = pltpu.roll(a8, s, 0)
            b_sh = pltpu.roll(b8, s, 0)
        b8 = b8 + a8 * jnp.where(keep, b_sh, 0.0)
        a8 = a8 * jnp.where(keep, a_sh, 1.0)
    return a8, b8


def _prep_small(c, w_mod, b_mod, cw):
    d = c.shape[1]
    cm = w_mod.shape[2]
    cwid = cw.shape[2]
    nl = w_mod.shape[0]

    def body(c_ref, wm_ref, bm_ref, cw_ref, mod_ref, cact_ref, cwf_ref, cbuf, pbuf, rbuf, ssem, rsem, lsem):
        pos, me = _my_position()
        me8 = pl.multiple_of(me * SUBLANES, SUBLANES)
        cbuf[pl.ds(me8, SUBLANES), :] = jnp.broadcast_to(c_ref[...], (SUBLANES, d))
        own_cw = pltpu.make_async_copy(cw_ref, cwf_ref.at[:, :, pl.ds(me * cwid, cwid)], lsem.at[0])
        own_cw.start()
        first = []
        for k in range(1, N_DEV):
            peer, _ = _peer(pos, k)
            rows = cbuf.at[pl.ds(me8, SUBLANES), :]
            first.append(_remote(rows, rows, ssem.at[0, k - 1], rsem.at[0, k - 1], peer))
            first.append(_remote(cw_ref, cwf_ref.at[:, :, pl.ds(me * cwid, cwid)], ssem.at[1, k - 1], rsem.at[1, k - 1], peer))
        for cp in first:
            cp.start()
        for k in range(1, N_DEV):
            peer, pj = _peer(pos, k)
            pj8 = pl.multiple_of(pj * SUBLANES, SUBLANES)
            rows = cbuf.at[pl.ds(pj8, SUBLANES), :]
            _remote(rows, rows, ssem.at[0, k - 1], rsem.at[0, k - 1], peer).wait_recv()
        cv = cbuf[...]
        cact = cv * _sigmoid(cv)
        cact_ref[...] = cact
        cb = cact.astype(BF16)
        for l in range(nl):
            pbuf[l] = _dot(cb, wm_ref[l].astype(BF16))
        own_p = pltpu.make_async_copy(pbuf.at[:, pl.ds(me8, SUBLANES), :], rbuf.at[me], lsem.at[1])
        own_p.start()
        second = []
        for k in range(1, N_DEV):
            peer, pj = _peer(pos, k)
            pj8 = pl.multiple_of(pj * SUBLANES, SUBLANES)
            second.append(_remote(pbuf.at[:, pl.ds(pj8, SUBLANES), :], rbuf.at[me], ssem.at[2, k - 1], rsem.at[2, k - 1], peer))
        for cp in second:
            cp.start()
        for k in range(1, N_DEV):
            peer, pj = _peer(pos, k)
            _remote(pbuf.at[:, pl.ds(0, SUBLANES), :], rbuf.at[pj], ssem.at[2, k - 1], rsem.at[2, k - 1], peer).wait_recv()
            _remote(cw_ref, cwf_ref.at[:, :, pl.ds(pj * cwid, cwid)], ssem.at[1, k - 1], rsem.at[1, k - 1], peer).wait_recv()
        own_p.wait()
        own_cw.wait()
        for l in range(nl):
            for j in range(N_DEV):
                mod_ref[l:l + 1, j * cm:(j + 1) * cm] = rbuf[j, l, 0:1, :] + bm_ref[l:l + 1, j * cm:(j + 1) * cm]
        for cp in first + second:
            cp.wait_send()

    return pl.pallas_call(
        body,
        name="prep_small",
        out_shape=(
            jax.ShapeDtypeStruct((nl, N_MOD * d), F32),
            jax.ShapeDtypeStruct((N_DEV * SUBLANES, d), F32),
            jax.ShapeDtypeStruct((nl, CW_ROWS, d), F32),
        ),
        in_specs=[VMEM_SPEC] * 4,
        out_specs=(VMEM_SPEC,) * 3,
        scratch_shapes=[
            pltpu.VMEM((N_DEV * SUBLANES, d), F32),
            pltpu.VMEM((nl, N_DEV * SUBLANES, cm), F32),
            pltpu.VMEM((N_DEV, nl, SUBLANES, cm), F32),
            pltpu.SemaphoreType.DMA((3, N_DEV - 1)),
            pltpu.SemaphoreType.DMA((3, N_DEV - 1)),
            pltpu.SemaphoreType.DMA((2,)),
        ],
        compiler_params=_cparams(vmem=3 * _nbytes(w_mod.shape, F32)),
    )(c, w_mod, b_mod, cw)


def _exchange_start(parts, gather, name):
    n = len(parts)
    lands = [lax.empty(((N_DEV,) + tuple(p.shape)) if gather else tuple(p.shape), p.dtype) for p in parts]

    def body(*refs):
        ins, lnd = refs[:n], refs[n:2 * n]
        ssem, rsem, token = refs[2 * n], refs[2 * n + 1], refs[-1]
        pos, me = _my_position()
        for k in range(1, N_DEV):
            peer, pj = _peer(pos, k)
            for t in range(n):
                src = ins[t] if gather else ins[t].at[pj]
                q = t * (N_DEV - 1) + k - 1
                _remote(src, lnd[t].at[me], ssem.at[q], rsem.at[q], peer).start()
        token[...] = jnp.zeros(token.shape, F32)

    out = pl.pallas_call(
        body,
        name=name,
        out_shape=(pltpu.SemaphoreType.DMA((n * (N_DEV - 1),)), pltpu.SemaphoreType.DMA((n * (N_DEV - 1),)))
        + tuple(pltpu.HBM(p.shape, p.dtype) for p in parts) + tuple(pltpu.HBM(p.shape, p.dtype) for p in lands)
        + (jax.ShapeDtypeStruct((SUBLANES, 128), F32),),
        in_specs=[HBM_SPEC] * (2 * n),
        out_specs=(SEM_SPEC, SEM_SPEC) + (HBM_SPEC,) * (2 * n) + (VMEM_SPEC,),
        input_output_aliases={i: 2 + i for i in range(2 * n)},
        compiler_params=pltpu.CompilerParams(has_side_effects=SIDE_EFFECT),
    )(*[pltpu.with_memory_space_constraint(p, pltpu.HBM) for p in list(parts) + lands])
    return out[0], out[1], list(out[2:2 + n]), list(out[2 + n:2 + 2 * n]), out[-1]


def _exchange_wait(started, after, gather, name):
    ssem, rsem, parts, lands, _ = started
    n = len(parts)

    def body(*refs):
        ins, lnd = refs[:n], refs[n:2 * n]
        ssem_ref, rsem_ref = refs[2 * n], refs[2 * n + 1]
        stage, lsem = refs[-1 - n:-1], refs[-1]
        pos, me = _my_position()
        load = []
        for t in range(n):
            src = ins[t] if gather else ins[t].at[me]
            load.append(pltpu.make_async_copy(src, stage[t], lsem.at[t]))
            load[-1].start()
        store = []
        for t in range(n):
            load[t].wait()
            store.append(pltpu.make_async_copy(stage[t], lnd[t].at[me], lsem.at[t]))
            store[-1].start()
        for k in range(1, N_DEV):
            peer, pj = _peer(pos, k)
            for t in range(n):
                src = ins[t] if gather else ins[t].at[pj]
                q = t * (N_DEV - 1) + k - 1
                _remote(src, lnd[t].at[me], ssem_ref.at[q], rsem_ref.at[q], peer).wait_send()
                _remote(src, lnd[t].at[pj], ssem_ref.at[q], rsem_ref.at[q], peer).wait_recv()
        for cp in store:
            cp.wait()

    out = pl.pallas_call(
        body,
        name=name,
        out_shape=tuple(pltpu.HBM(p.shape, p.dtype) for p in parts) + tuple(pltpu.HBM(p.shape, p.dtype) for p in lands),
        in_specs=[HBM_SPEC] * (2 * n) + [SEM_SPEC, SEM_SPEC, ANY_SPEC],
        out_specs=(HBM_SPEC,) * (2 * n),
        input_output_aliases={i: i for i in range(2 * n)},
        scratch_shapes=[pltpu.VMEM(tuple(z.shape[1:]), z.dtype) for z in lands] + [pltpu.SemaphoreType.DMA((n,))],
        compiler_params=pltpu.CompilerParams(has_side_effects=SIDE_EFFECT),
    )(*parts, *lands, ssem, rsem, after)
    return list(out[n:])


CHIP_PEERS = 3


def _chip_peer(pos, k, core):
    x, y, _ = pos
    px = 1 - x if k & 2 else x
    py = 1 - y if k & 1 else y
    return (px, py, core), 4 * px + 2 * py + core


def _gather2_start(parts, after, name):
    n = len(parts)
    lands = [lax.empty((N_DEV,) + tuple(p.shape), p.dtype) for p in parts]
    per = CHIP_PEERS + 1

    def body(*refs):
        ins, lnd = refs[:n], refs[n:2 * n]
        ssem, rsem, token = refs[2 * n + 1], refs[2 * n + 2], refs[-1]
        pos, me = _my_position()
        sibling = (pos[0], pos[1], 1 - pos[2])
        for t in range(n):
            _remote(ins[t], lnd[t].at[me], ssem.at[per * t], rsem.at[per * t], sibling).start()
        for k in range(1, per):
            peer, _ = _chip_peer(pos, k, pos[2])
            for t in range(n):
                _remote(ins[t], lnd[t].at[me], ssem.at[per * t + k], rsem.at[per * t + k], peer).start()
        token[...] = jnp.zeros(token.shape, F32)

    out = pl.pallas_call(
        body,
        name=name,
        out_shape=(pltpu.SemaphoreType.DMA((n * per,)), pltpu.SemaphoreType.DMA((n * per,)))
        + tuple(pltpu.HBM(p.shape, p.dtype) for p in parts) + tuple(pltpu.HBM(p.shape, p.dtype) for p in lands)
        + (jax.ShapeDtypeStruct((SUBLANES, 128), F32),),
        in_specs=[HBM_SPEC] * (2 * n) + [ANY_SPEC],
        out_specs=(SEM_SPEC, SEM_SPEC) + (HBM_SPEC,) * (2 * n) + (VMEM_SPEC,),
        input_output_aliases={i: 2 + i for i in range(2 * n)},
        compiler_params=pltpu.CompilerParams(has_side_effects=SIDE_EFFECT),
    )(*[pltpu.with_memory_space_constraint(p, pltpu.HBM) for p in list(parts) + lands], after)
    return out[0], out[1], list(out[2:2 + n]), list(out[2 + n:2 + 2 * n]), out[-1]


def _gather2_forward(started, after, name):
    ssem, rsem, parts, lands, _ = started
    n = len(lands)
    per = CHIP_PEERS + 1

    def body(*refs):
        lnd = refs[:n]
        rsem_a, fsend, frecv, token = refs[n], refs[2 * n + 2], refs[2 * n + 3], refs[2 * n + 4]
        pos, _ = _my_position()
        sibling = (pos[0], pos[1], 1 - pos[2])
        for k in range(1, per):
            peer, pk = _chip_peer(pos, k, pos[2])
            for t in range(n):
                block = lnd[t].at[pk]
                _remote(block, block, rsem_a.at[per * t + k], rsem_a.at[per * t + k], peer).wait_recv()
                q = CHIP_PEERS * t + k - 1
                _remote(block, block, fsend.at[q], frecv.at[q], sibling).start()
        token[...] = jnp.zeros(token.shape, F32)

    out = pl.pallas_call(
        body,
        name=name,
        out_shape=tuple(pltpu.HBM(p.shape, p.dtype) for p in lands)
        + (pltpu.SemaphoreType.DMA((n * CHIP_PEERS,)), pltpu.SemaphoreType.DMA((n * CHIP_PEERS,)),
           jax.ShapeDtypeStruct((SUBLANES, 128), F32)),
        in_specs=[HBM_SPEC] * n + [SEM_SPEC, ANY_SPEC],
        out_specs=(HBM_SPEC,) * n + (SEM_SPEC, SEM_SPEC, VMEM_SPEC),
        input_output_aliases={i: i for i in range(n)},
        compiler_params=pltpu.CompilerParams(has_side_effects=SIDE_EFFECT),
    )(*lands, rsem, after)
    return ssem, rsem, parts, list(out[:n]), out[n + 2], out[n], out[n + 1]


def _gather2_wait(forwarded, after, name):
    ssem, rsem, parts, lands, _, fsend, frecv = forwarded
    n = len(parts)
    per = CHIP_PEERS + 1

    def body(*refs):
        ins, lnd = refs[:n], refs[n:2 * n]
        ssem_a, rsem_a, fs, fr = refs[2 * n:2 * n + 4]
        stage, lsem = refs[-1 - n:-1], refs[-1]
        pos, me = _my_position()
        sibling = (pos[0], pos[1], 1 - pos[2])
        sib = 4 * pos[0] + 2 * pos[1] + 1 - pos[2]
        load = []
        for t in range(n):
            load.append(pltpu.make_async_copy(ins[t], stage[t], lsem.at[t]))
            load[-1].start()
        store = []
        for t in range(n):
            load[t].wait()
            store.append(pltpu.make_async_copy(stage[t], lnd[t].at[me], lsem.at[t]))
            store[-1].start()
        for t in range(n):
            _remote(ins[t], lnd[t].at[me], ssem_a.at[per * t], rsem_a.at[per * t], sibling).wait_send()
            _remote(ins[t], lnd[t].at[sib], ssem_a.at[per * t], rsem_a.at[per * t], sibling).wait_recv()
        for k in range(1, per):
            peer, pk = _chip_peer(pos, k, pos[2])
            _, qk = _chip_peer(pos, k, 1 - pos[2])
            for t in range(n):
                q = CHIP_PEERS * t + k - 1
                _remote(ins[t], lnd[t].at[me], ssem_a.at[per * t + k], rsem_a.at[per * t + k], peer).wait_send()
                _remote(lnd[t].at[pk], lnd[t].at[pk], fs.at[q], fr.at[q], sibling).wait_send()
                _remote(lnd[t].at[qk], lnd[t].at[qk], fs.at[q], fr.at[q], sibling).wait_recv()
        for cp in store:
            cp.wait()

    out = pl.pallas_call(
        body,
        name=name,
        out_shape=tuple(pltpu.HBM(p.shape, p.dtype) for p in parts) + tuple(pltpu.HBM(p.shape, p.dtype) for p in lands),
        in_specs=[HBM_SPEC] * (2 * n) + [SEM_SPEC] * 4 + [ANY_SPEC],
        out_specs=(HBM_SPEC,) * (2 * n),
        input_output_aliases={i: i for i in range(2 * n)},
        scratch_shapes=[pltpu.VMEM(tuple(z.shape[1:]), z.dtype) for z in lands] + [pltpu.SemaphoreType.DMA((n,))],
        compiler_params=pltpu.CompilerParams(has_side_effects=SIDE_EFFECT),
    )(*parts, *lands, ssem, rsem, fsend, frecv, after)
    return list(out[n:])


def _reduce_small(rows, dm8, cact, after):
    r, d = rows.shape
    nl = dm8.shape[0]
    cm = dm8.shape[2] // N_DEV

    def body(rows_ref, dm_ref, cact_ref, after_ref, orow_ref, owm_ref, gr, dmr, ssem, rsem, lsem):
        pos, me = _my_position()
        me8 = pl.multiple_of(me * SUBLANES, SUBLANES)
        gr[me] = rows_ref[...]
        own_dm = pltpu.make_async_copy(dm_ref.at[:, :, pl.ds(me * cm, cm)], dmr.at[:, pl.ds(me8, SUBLANES), :], lsem.at[0])
        own_dm.start()
        sends = []
        for k in range(1, N_DEV):
            peer, pj = _peer(pos, k)
            sends.append(_remote(gr.at[me], gr.at[me], ssem.at[0, k - 1], rsem.at[0, k - 1], peer))
            sends.append(_remote(dm_ref.at[:, :, pl.ds(pj * cm, cm)], dmr.at[:, pl.ds(me8, SUBLANES), :],
                                 ssem.at[1, k - 1], rsem.at[1, k - 1], peer))
        for cp in sends:
            cp.start()
        for k in range(1, N_DEV):
            peer, pj = _peer(pos, k)
            pj8 = pl.multiple_of(pj * SUBLANES, SUBLANES)
            _remote(gr.at[pj], gr.at[pj], ssem.at[0, k - 1], rsem.at[0, k - 1], peer).wait_recv()
            _remote(dm_ref.at[:, :, pl.ds(0, cm)], dmr.at[:, pl.ds(pj8, SUBLANES), :], ssem.at[1, k - 1], rsem.at[1, k - 1], peer).wait_recv()
        own_dm.wait()
        acc = gr[0]
        for j in range(1, N_DEV):
            acc = acc + gr[j]
        orow_ref[...] = acc
        cb = cact_ref[...].astype(BF16)
        for l in range(nl):
            owm_ref[l] = _dot_tn(cb, dmr[l].astype(BF16))
        for cp in sends:
            cp.wait_send()

    return pl.pallas_call(
        body,
        name="reduce_small",
        out_shape=(jax.ShapeDtypeStruct((r, d), F32), jax.ShapeDtypeStruct((nl, d, cm), F32)),
        in_specs=[VMEM_SPEC] * 3 + [ANY_SPEC],
        out_specs=(VMEM_SPEC,) * 2,
        scratch_shapes=[
            pltpu.VMEM((N_DEV, r, d), F32),
            pltpu.VMEM((nl, N_DEV * SUBLANES, cm), F32),
            pltpu.SemaphoreType.DMA((2, N_DEV - 1)),
            pltpu.SemaphoreType.DMA((2, N_DEV - 1)),
            pltpu.SemaphoreType.DMA((1,)),
        ],
        compiler_params=_cparams(vmem=4 * _nbytes((N_DEV, r, d), F32) + 6 * _nbytes((nl, d, cm), F32)),
    )(rows, dm8, cact, after)


def _sum_gathered(zones):
    nl = len(zones)

    def body(*refs):
        for l in range(nl):
            acc = refs[l][0].astype(F32)
            for j in range(1, N_DEV):
                acc = acc + refs[l][j].astype(F32)
            refs[nl][l] = acc

    return pl.pallas_call(
        body,
        name="sum_gathered",
        out_shape=jax.ShapeDtypeStruct((nl,) + tuple(zones[0].shape[1:]), F32),
        in_specs=[VMEM_SPEC] * nl,
        out_specs=VMEM_SPEC,
        compiler_params=_cparams(vmem=8 * nl * _nbytes(zones[0].shape, BF16)),
    )(*zones)


def _in_proj_fwd(x, mod, rows, win_f, l, tm, after):
    s, d = x.shape
    nb, _, ci = win_f.shape

    def body(x_ref, mod_ref, rows_ref, w_ref, after_ref, proj_ref, h_ref):
        xv = x_ref[...]
        g = rows_ref[R_G_PRE_MIX:R_G_PRE_MIX + 1, :]
        h = (xv * _rms(xv) * g) * (1.0 + mod_ref[M_SC_M:M_SC_M + 1, :]) + mod_ref[M_SH_M:M_SH_M + 1, :]
        hb = h.astype(BF16)
        h_ref[...] = hb
        for j in range(nb):
            proj_ref[:, j * ci:(j + 1) * ci] = _dot(hb, w_ref[j])

    return pl.pallas_call(
        body,
        name="in_proj_fwd",
        grid=(s // tm,),
        in_specs=[
            pl.BlockSpec((tm, d), lambda i: (i, 0)),
            _resident((None, N_MOD, d), lambda i: (l, 0, 0)),
            _resident((None, N_ROWS, d), lambda i: (l, 0, 0)),
            _resident((nb, d, ci), lambda i: (0, 0, 0)),
            ANY_SPEC,
        ],
        out_specs=(pl.BlockSpec((tm, nb * ci), lambda i: (i, 0)), pl.BlockSpec((tm, d), lambda i: (i, 0))),
        out_shape=(jax.ShapeDtypeStruct((s, nb * ci), F32), jax.ShapeDtypeStruct((s, d), BF16)),
        compiler_params=_cparams(("parallel",), _nbytes((nb, d, ci), BF16) + 3 * _nbytes((tm, nb * ci), F32) + 8 * _nbytes((tm, d), F32)),
    )(*_hbm(x), mod, rows, win_f, after)


def _mixer_core_fwd(proj, x, mod, rows, cwf, wr, wi, wa_f, wb_f, wo_f, l, tm, after):
    s, d = x.shape
    nh, bw, _ = wr.shape[1:]

    def body(proj_ref, x_ref, mod_ref, rows_ref, cw_ref, wr_ref, wi_ref, wa_ref, wb_ref, wo_ref, after_ref,
             x1_ref, hs_ref, yap_ref, ybp_ref, y_ref, cvbuf, xbbuf, a_s, b_s, hprev):
        i = pl.program_id(0)

        @pl.when(i == 0)
        def _():
            cvbuf[...] = jnp.zeros((SUBLANES, d), F32)
            xbbuf[...] = jnp.zeros((SUBLANES, d), F32)
            hprev[...] = jnp.zeros((SUBLANES, d), F32)

        def row(r):
            return rows_ref[r:r + 1, :]

        def tap(r):
            return cw_ref[r:r + 1, :]

        ba = proj_ref[:, 0:d]
        cv = proj_ref[:, d:2 * d] * proj_ref[:, 2 * d:3 * d]
        cvt = cvbuf[...]
        conv3 = ((row(R_CONV_A_B) + _shift_rows(cv, cvt, 2, False) * tap(CW_A)) + _shift_rows(cv, cvt, 1, False) * tap(CW_A + 1)) + cv * tap(CW_A + 2)
        ya = ba * conv3
        cvbuf[...] = cv[tm - SUBLANES:, :]
        xb = proj_ref[:, 3 * d:4 * d]
        xbt = xbbuf[...]
        xc = (((row(R_CONV_B_B) + _shift_rows(xb, xbt, 3, False) * tap(CW_B)) + _shift_rows(xb, xbt, 2, False) * tap(CW_B + 1))
              + _shift_rows(xb, xbt, 1, False) * tap(CW_B + 2)) + xb * tap(CW_B + 3)
        xbbuf[...] = xb[tm - SUBLANES:, :]
        sp = _softplus_neg(row(R_LAMBDA))
        _, _, ig, a, mult = _gates(xc, wr_ref, wi_ref, row(R_B_GATE_R), row(R_B_GATE_I), sp, nh, bw)
        a_s[...] = a
        b_s[...] = mult * (ig * xc)

        def blk(j, hp):
            o = pl.multiple_of(j * SUBLANES, SUBLANES)
            a8, b8 = _scan_block(a_s[pl.ds(o, SUBLANES), :], b_s[pl.ds(o, SUBLANES), :], reverse=False)
            h8 = b8 + a8 * hp
            hs_ref[pl.ds(o, SUBLANES), :] = h8
            return jnp.broadcast_to(h8[SUBLANES - 1:SUBLANES, :], (SUBLANES, d))

        hprev[...] = lax.fori_loop(0, tm // SUBLANES, blk, hprev[...])
        gel, _ = _gelu(proj_ref[:, 4 * d:5 * d])
        yb = hs_ref[...] * gel
        yap = _dot(ya.astype(BF16), wa_ref[...])
        ybp = _dot(yb.astype(BF16), wb_ref[...])
        yap_ref[...] = yap
        ybp_ref[...] = ybp
        m = _sigmoid(proj_ref[:, 5 * d:6 * d]) * yap + _sigmoid(proj_ref[:, 6 * d:7 * d]) * ybp
        y = _dot(m.astype(BF16), wo_ref[...])
        y_ref[...] = y
        x1_ref[...] = x_ref[...] + mod_ref[M_GT_M:M_GT_M + 1, :] * ((y * _rms(y)) * row(R_G_POST_MIX))

    tile = pl.BlockSpec((tm, d), lambda i: (i, 0))
    return pl.pallas_call(
        body,
        name="mixer_core_fwd",
        grid=(s // tm,),
        in_specs=[
            pl.BlockSpec((tm, 7 * d), lambda i: (i, 0)),
            tile,
            _resident((None, N_MOD, d), lambda i: (l, 0, 0)),
            _resident((None, N_ROWS, d), lambda i: (l, 0, 0)),
            _resident((None, CW_ROWS, d), lambda i: (l, 0, 0)),
            _resident((None, nh, bw, bw), lambda i: (l, 0, 0, 0)),
            _resident((None, nh, bw, bw), lambda i: (l, 0, 0, 0)),
            _resident((d, d), lambda i: (0, 0)),
            _resident((d, d), lambda i: (0, 0)),
            _resident((d, d), lambda i: (0, 0)),
            ANY_SPEC,
        ],
        out_specs=(tile,) * 5,
        out_shape=(jax.ShapeDtypeStruct((s, d), F32),) * 5,
        scratch_shapes=[
            pltpu.VMEM((SUBLANES, d), F32),
            pltpu.VMEM((SUBLANES, d), F32),
            pltpu.VMEM((tm, d), F32),
            pltpu.VMEM((tm, d), F32),
            pltpu.VMEM((SUBLANES, d), F32),
        ],
        compiler_params=_cparams(("arbitrary",), 3 * _nbytes((d, d), BF16) + 2 * _nbytes((tm, 7 * d), F32) + 40 * _nbytes((tm, d), F32)),
    )(*_hbm(proj, x), mod, rows, cwf, wr, wi, wa_f, wb_f, wo_f, after)


def _mlp_fwd(x1, mod, rows, wup_f, wdn_f, l, tm, after):
    s, d = x1.shape
    nb, _, cu = wup_f.shape
    dff = nb * cu

    def body(x1_ref, mod_ref, rows_ref, wu_ref, wd_ref, after_ref, x2_ref, ru_ref, y2_ref, h2_ref):
        xv = x1_ref[...]
        g = rows_ref[R_G_PRE_MLP:R_G_PRE_MLP + 1, :]
        h2 = ((xv * _rms(xv) * g) * (1.0 + mod_ref[M_SC_F:M_SC_F + 1, :]) + mod_ref[M_SH_F:M_SH_F + 1, :]).astype(BF16)
        h2_ref[...] = h2
        ru = jnp.concatenate([jnp.maximum(_dot(h2, wu_ref[j]), 0.0) for j in range(nb)], axis=1)
        ru_ref[...] = ru.astype(BF16)
        y2 = _dot((ru * ru).astype(BF16), wd_ref[...])
        y2_ref[...] = y2
        x2_ref[...] = xv + mod_ref[M_GT_F:M_GT_F + 1, :] * ((y2 * _rms(y2)) * rows_ref[R_G_POST_MLP:R_G_POST_MLP + 1, :])

    tile = pl.BlockSpec((tm, d), lambda i: (i, 0))
    wide = pl.BlockSpec((tm, dff), lambda i: (i, 0))
    return pl.pallas_call(
        body,
        name="mlp_fwd",
        grid=(s // tm,),
        in_specs=[
            tile,
            _resident((None, N_MOD, d), lambda i: (l, 0, 0)),
            _resident((None, N_ROWS, d), lambda i: (l, 0, 0)),
            _resident((nb, d, cu), lambda i: (0, 0, 0)),
            _resident((dff, d), lambda i: (0, 0)),
            ANY_SPEC,
        ],
        out_specs=(tile, wide, tile, tile),
        out_shape=(jax.ShapeDtypeStruct((s, d), F32), jax.ShapeDtypeStruct((s, dff), BF16),
                   jax.ShapeDtypeStruct((s, d), F32), jax.ShapeDtypeStruct((s, d), BF16)),
        compiler_params=_cparams(("parallel",), 2 * _nbytes((dff, d), BF16) + 5 * _nbytes((tm, dff), F32) + 12 * _nbytes((tm, d), F32)),
    )(*_hbm(x1), mod, rows, wup_f, wdn_f, after)


def _loss_fwd_bwd(y, target, tm):
    s, d = y.shape

    def body(y_ref, t_ref, loss_ref, dy_ref):
        @pl.when(pl.program_id(0) == 0)
        def _():
            loss_ref[...] = jnp.zeros(loss_ref.shape, F32)

        e = y_ref[...] - t_ref[...]
        dy_ref[...] = e * (1.0 / d)
        loss_ref[...] += 0.5 * jnp.sum(jnp.mean(e * e, axis=-1, keepdims=True), axis=0, keepdims=True)

    tile = pl.BlockSpec((tm, d), lambda i: (i, 0))
    loss, dy = pl.pallas_call(
        body,
        name="loss",
        grid=(s // tm,),
        in_specs=[tile, tile],
        out_specs=(pl.BlockSpec((SUBLANES, 128), lambda i: (0, 0)), tile),
        out_shape=(jax.ShapeDtypeStruct((SUBLANES, 128), F32), jax.ShapeDtypeStruct((s, d), F32)),
        compiler_params=_cparams(("arbitrary",)),
    )(*_hbm(y, target))
    return loss[0, 0], dy


def _mlp_bwd(dx2, x1, y2, ru, mod, rows, wup_f, wdn_f, l, tm):
    s, d = x1.shape
    nb, _, cu = wup_f.shape
    dff = nb * cu

    def body(dx2_ref, x1_ref, y2_ref, ru_ref, mod_ref, rows_ref, wu_ref, wd_ref, dx1_ref, dy2_ref, dup_ref, act_ref, sm_ref):
        @pl.when(pl.program_id(0) == 0)
        def _():
            sm_ref[...] = jnp.zeros(sm_ref.shape, F32)

        dout = dx2_ref[...]
        dy2, dgt, dgpost = _postnorm_bwd(y2_ref[...], dout, rows_ref[R_G_POST_MLP:R_G_POST_MLP + 1, :], mod_ref[M_GT_F:M_GT_F + 1, :])
        dy2b = dy2.astype(BF16)
        dy2_ref[...] = dy2b
        ruv = ru_ref[...].astype(F32)
        act_ref[...] = (ruv * ruv).astype(BF16)
        dup = (_dot_nt(dy2b, wd_ref[...]) * (2.0 * ruv)).astype(BF16)
        dup_ref[...] = dup
        dh2 = _dot_nt(dup[:, 0:cu], wu_ref[0])
        for j in range(1, nb):
            dh2 = dh2 + _dot_nt(dup[:, j * cu:(j + 1) * cu], wu_ref[j])
        dxn, dsc, dsh, dgpre = _prenorm_bwd(x1_ref[...], dh2, rows_ref[R_G_PRE_MLP:R_G_PRE_MLP + 1, :], mod_ref[M_SC_F:M_SC_F + 1, :])
        dx1_ref[...] = dout + dxn
        for r, v in ((G_MLP_GT, dgt), (G_MLP_GPOST, dgpost), (G_MLP_SC, dsc), (G_MLP_SH, dsh), (G_MLP_GPRE, dgpre)):
            sm_ref[r:r + 1, :] += v

    tile = pl.BlockSpec((tm, d), lambda i: (i, 0))
    wide = pl.BlockSpec((tm, dff), lambda i: (i, 0))
    return pl.pallas_call(
        body,
        name="mlp_bwd",
        grid=(s // tm,),
        in_specs=[
            tile, tile, tile, wide,
            _resident((None, N_MOD, d), lambda i: (l, 0, 0)),
            _resident((None, N_ROWS, d), lambda i: (l, 0, 0)),
            _resident((nb, d, cu), lambda i: (0, 0, 0)),
            _resident((dff, d), lambda i: (0, 0)),
        ],
        out_specs=(tile, tile, wide, wide, pl.BlockSpec((G_MLP_ROWS, d), lambda i: (0, 0))),
        out_shape=(jax.ShapeDtypeStruct((s, d), F32), jax.ShapeDtypeStruct((s, d), BF16), jax.ShapeDtypeStruct((s, dff), BF16),
                   jax.ShapeDtypeStruct((s, dff), BF16), jax.ShapeDtypeStruct((G_MLP_ROWS, d), F32)),
        compiler_params=_cparams(("arbitrary",), 2 * _nbytes((dff, d), BF16) + 6 * _nbytes((tm, dff), F32) + 16 * _nbytes((tm, d), F32)),
    )(*_hbm(dx2, x1, y2, ru), mod, rows, wup_f, wdn_f)


def _mixer_core_bwd(dx1, y, yap, ybp, hs, proj, mod, rows, cwf, wr, wi, wa_f, wb_f, wo_f, l, tm, after):
    s, d = dx1.shape
    nh, bw, _ = wr.shape[1:]
    nt = s // tm
    per = tm // SUBLANES

    def body(dx1_ref, y_ref, yap_ref, ybp_ref, hs_ref, hsh_ref, proj_ref, projh_ref, mod_ref, rows_ref, cw_ref,
             wr_ref, wi_ref, wa_ref, wb_ref, wo_ref, after_ref,
             dproj_ref, dy_ref, m_ref, dyap_ref, dybp_ref, ya_ref, yb_ref, sm_ref, dwg_ref,
             abuf, dcbuf, dxbuf, al_s, dh_s, lam_s, lnext):
        i = pl.program_id(0)
        first_tile = i == nt - 1

        @pl.when(i == 0)
        def _():
            sm_ref[...] = jnp.zeros(sm_ref.shape, F32)
            dwg_ref[...] = jnp.zeros(dwg_ref.shape, F32)
            zero = jnp.zeros((SUBLANES, d), F32)
            abuf[...] = zero
            dcbuf[...] = zero
            dxbuf[...] = zero
            lnext[...] = zero

        def row(r):
            return rows_ref[r:r + 1, :]

        def tap(r):
            return cw_ref[r:r + 1, :]

        def acc(r, v):
            sm_ref[r:r + 1, :] += v

        keep_halo = jnp.where(first_tile, 0.0, 1.0)
        dy, dgt, dgpost = _postnorm_bwd(y_ref[...], dx1_ref[...], row(R_G_POST_MIX), mod_ref[M_GT_M:M_GT_M + 1, :])
        acc(G_MIX_GT, dgt)
        acc(G_MIX_GPOST, dgpost)
        dyb16 = dy.astype(BF16)
        dy_ref[...] = dyb16
        dm = _dot_nt(dyb16, wo_ref[...])
        sa = _sigmoid(proj_ref[:, 5 * d:6 * d])
        sb = _sigmoid(proj_ref[:, 6 * d:7 * d])
        yap = yap_ref[...]
        ybp = ybp_ref[...]
        m_ref[...] = (sa * yap + sb * ybp).astype(BF16)
        dyap_f = dm * sa
        dybp_f = dm * sb
        dyap = dyap_f.astype(BF16)
        dybp = dybp_f.astype(BF16)
        dyap_ref[...] = dyap
        dybp_ref[...] = dybp
        dproj_ref[:, 5 * d:6 * d] = (dyap_f * yap * (1.0 - sa)).astype(BF16)
        dproj_ref[:, 6 * d:7 * d] = (dybp_f * ybp * (1.0 - sb)).astype(BF16)
        dya = _dot_nt(dyap, wa_ref[...])
        dyb = _dot_nt(dybp, wb_ref[...])
        ba = proj_ref[:, 0:d]
        ca = proj_ref[:, d:2 * d]
        va = proj_ref[:, 2 * d:3 * d]
        cv = ca * va
        cvh = keep_halo * (projh_ref[:, d:2 * d] * projh_ref[:, 2 * d:3 * d])
        cvm2 = _shift_rows(cv, cvh, 2, False)
        cvm1 = _shift_rows(cv, cvh, 1, False)
        conv3 = ((row(R_CONV_A_B) + cvm2 * tap(CW_A)) + cvm1 * tap(CW_A + 1)) + cv * tap(CW_A + 2)
        ya_ref[...] = (ba * conv3).astype(BF16)
        dproj_ref[:, 0:d] = (dya * conv3).astype(BF16)
        dc3 = dya * ba
        acc(G_MIX_CAB, _colsum(dc3))
        acc(G_MIX_CAW, _colsum(dc3 * cvm2))
        acc(G_MIX_CAW + 1, _colsum(dc3 * cvm1))
        acc(G_MIX_CAW + 2, _colsum(dc3 * cv))
        dct = dcbuf[...]
        dcv = (dc3 * tap(CW_A + 2) + _shift_rows(dc3, dct, 1, True) * tap(CW_A + 1)) + _shift_rows(dc3, dct, 2, True) * tap(CW_A)
        dcbuf[...] = dc3[:SUBLANES, :]
        dproj_ref[:, d:2 * d] = (dcv * va).astype(BF16)
        dproj_ref[:, 2 * d:3 * d] = (dcv * ca).astype(BF16)
        xb = proj_ref[:, 3 * d:4 * d]
        gb = proj_ref[:, 4 * d:5 * d]
        xbh = keep_halo * projh_ref[:, 3 * d:4 * d]
        xm3 = _shift_rows(xb, xbh, 3, False)
        xm2 = _shift_rows(xb, xbh, 2, False)
        xm1 = _shift_rows(xb, xbh, 1, False)
        xc = (((row(R_CONV_B_B) + xm3 * tap(CW_B)) + xm2 * tap(CW_B + 1)) + xm1 * tap(CW_B + 2)) + xb * tap(CW_B + 3)
        lam = row(R_LAMBDA)
        sp = _softplus_neg(lam)
        xcb, r, ig, a, mult = _gates(xc, wr_ref, wi_ref, row(R_B_GATE_R), row(R_B_GATE_I), sp, nh, bw)
        gel, gsig = _gelu(gb)
        hs = hs_ref[...]
        yb_ref[...] = (hs * gel).astype(BF16)
        dproj_ref[:, 4 * d:5 * d] = (dyb * hs * _gelu_grad(gb, gsig)).astype(BF16)
        al_s[...] = _shift_rows(a, abuf[...], 1, True)
        abuf[...] = a[:SUBLANES, :]
        dh_s[...] = dyb * gel

        def blk(j, ln):
            o = pl.multiple_of((per - 1 - j) * SUBLANES, SUBLANES)
            a8, b8 = _scan_block(al_s[pl.ds(o, SUBLANES), :], dh_s[pl.ds(o, SUBLANES), :], reverse=True)
            l8 = b8 + a8 * ln
            lam_s[pl.ds(o, SUBLANES), :] = l8
            return jnp.broadcast_to(l8[0:1, :], (SUBLANES, d))

        lnext[...] = lax.fori_loop(0, per, blk, lnext[...])
        dbb = lam_s[...]
        da = dbb * _shift_rows(hs, keep_halo * hsh_ref[...], 1, False)
        dbx = dbb * xc
        dmult = dbx * ig
        dig = dbx * mult
        dxc = (dbb * mult) * ig
        dla = a * (da - (dmult * a) / mult)
        dlar = dla * r
        acc(G_MIX_LAM, _colsum(dlar) * (LRU_C * _sigmoid(-lam)))
        dzr = (dlar * (1.0 - r)) * (-LRU_C * sp)
        dzi = dig * ig * (1.0 - ig)
        acc(G_MIX_BR, _colsum(dzr))
        acc(G_MIX_BI, _colsum(dzi))
        dzrb = dzr.astype(BF16)
        dzib = dzi.astype(BF16)
        back = []
        for h in range(nh):
            sl = slice(h * bw, (h + 1) * bw)
            back.append(_dot_nt(dzrb[:, sl], wr_ref[h]) + _dot_nt(dzib[:, sl], wi_ref[h]))
            dwg_ref[0, h] += _dot_tn(xcb[:, sl], dzrb[:, sl])
            dwg_ref[1, h] += _dot_tn(xcb[:, sl], dzib[:, sl])
        dxc = dxc + jnp.concatenate(back, axis=1)
        acc(G_MIX_CBB, _colsum(dxc))
        acc(G_MIX_CBW, _colsum(dxc * xm3))
        acc(G_MIX_CBW + 1, _colsum(dxc * xm2))
        acc(G_MIX_CBW + 2, _colsum(dxc * xm1))
        acc(G_MIX_CBW + 3, _colsum(dxc * xb))
        dxt = dxbuf[...]
        dxb = (((dxc * tap(CW_B + 3) + _shift_rows(dxc, dxt, 1, True) * tap(CW_B + 2)) + _shift_rows(dxc, dxt, 2, True) * tap(CW_B + 1))
               + _shift_rows(dxc, dxt, 3, True) * tap(CW_B))
        dxbuf[...] = dxc[:SUBLANES, :]
        dproj_ref[:, 3 * d:4 * d] = dxb.astype(BF16)

    def rev(i):
        return (nt - 1 - i, 0)

    def halo(i):
        return (jnp.maximum((nt - 1 - i) * per - 1, 0), 0)

    tile = pl.BlockSpec((tm, d), rev)
    return pl.pallas_call(
        body,
        name="mixer_core_bwd",
        grid=(nt,),
        in_specs=[
            tile, tile, tile, tile, tile,
            pl.BlockSpec((SUBLANES, d), halo),
            pl.BlockSpec((tm, 7 * d), rev),
            pl.BlockSpec((SUBLANES, 7 * d), halo),
            _resident((None, N_MOD, d), lambda i: (l, 0, 0)),
            _resident((None, N_ROWS, d), lambda i: (l, 0, 0)),
            _resident((None, CW_ROWS, d), lambda i: (l, 0, 0)),
            _resident((None, nh, bw, bw), lambda i: (l, 0, 0, 0)),
            _resident((None, nh, bw, bw), lambda i: (l, 0, 0, 0)),
            _resident((d, d), lambda i: (0, 0)),
            _resident((d, d), lambda i: (0, 0)),
            _resident((d, d), lambda i: (0, 0)),
            ANY_SPEC,
        ],
        out_specs=(pl.BlockSpec((tm, 7 * d), rev),) + (tile,) * 6 + (
            pl.BlockSpec((G_MIX_ROWS, d), lambda i: (0, 0)), pl.BlockSpec((2, nh, bw, bw), lambda i: (0, 0, 0, 0))),
        out_shape=(jax.ShapeDtypeStruct((s, 7 * d), BF16),) + (jax.ShapeDtypeStruct((s, d), BF16),) * 6 + (
            jax.ShapeDtypeStruct((G_MIX_ROWS, d), F32), jax.ShapeDtypeStruct((2, nh, bw, bw), F32)),
        scratch_shapes=[pltpu.VMEM((SUBLANES, d), F32)] * 3 + [pltpu.VMEM((tm, d), F32)] * 3 + [pltpu.VMEM((SUBLANES, d), F32)],
        compiler_params=_cparams(("arbitrary",), 3 * _nbytes((d, d), BF16) + 3 * _nbytes((tm, 7 * d), F32) + 64 * _nbytes((tm, d), F32)),
    )(*_hbm(dx1, y, yap, ybp, hs, hs, proj, proj), mod, rows, cwf, wr, wi, wa_f, wb_f, wo_f, after)


def _in_proj_bwd(dproj, x, dx1, mod, rows, win_f, l, tm, after):
    s, d = x.shape
    nb, _, ci = win_f.shape

    def body(dp_ref, x_ref, dx1_ref, mod_ref, rows_ref, w_ref, after_ref, dx_ref, sm_ref):
        @pl.when(pl.program_id(0) == 0)
        def _():
            sm_ref[...] = jnp.zeros(sm_ref.shape, F32)

        dh = _dot_nt(dp_ref[:, 0:ci], w_ref[0])
        for j in range(1, nb):
            dh = dh + _dot_nt(dp_ref[:, j * ci:(j + 1) * ci], w_ref[j])
        dxn, dsc, dsh, dg = _prenorm_bwd(x_ref[...], dh, rows_ref[R_G_PRE_MIX:R_G_PRE_MIX + 1, :], mod_ref[M_SC_M:M_SC_M + 1, :])
        dx_ref[...] = dx1_ref[...] + dxn
        for r, v in ((G_IN_SC, dsc), (G_IN_SH, dsh), (G_IN_GPRE, dg)):
            sm_ref[r:r + 1, :] += v

    tile = pl.BlockSpec((tm, d), lambda i: (i, 0))
    return pl.pallas_call(
        body,
        name="in_proj_bwd",
        grid=(s // tm,),
        in_specs=[
            pl.BlockSpec((tm, nb * ci), lambda i: (i, 0)), tile, tile,
            _resident((None, N_MOD, d), lambda i: (l, 0, 0)),
            _resident((None, N_ROWS, d), lambda i: (l, 0, 0)),
            _resident((nb, d, ci), lambda i: (0, 0, 0)),
            ANY_SPEC,
        ],
        out_specs=(tile, pl.BlockSpec((G_IN_ROWS, d), lambda i: (0, 0))),
        out_shape=(jax.ShapeDtypeStruct((s, d), F32), jax.ShapeDtypeStruct((G_IN_ROWS, d), F32)),
        compiler_params=_cparams(("arbitrary",), _nbytes((nb, d, ci), BF16) + 2 * _nbytes((tm, nb * ci), BF16) + 16 * _nbytes((tm, d), F32)),
    )(*_hbm(dproj, x, dx1), mod, rows, win_f, after)


def _wgrad(a, b, cols_owned, ts, after):
    s, k1 = a.shape
    k2 = b.shape[1]
    ns = s // ts
    if cols_owned:
        nblk, bk1, bk2 = N_DEV, k1, k2 // N_DEV
        a_spec = pl.BlockSpec((ts, bk1), lambda j, t: (t, 0))
        b_spec = pl.BlockSpec((ts, bk2), lambda j, t: (t, j))
    else:
        bk1, bk2 = min(WGRAD_ROWS, k1), k2
        nblk = k1 // bk1
        a_spec = pl.BlockSpec((ts, bk1), lambda j, t: (t, j))
        b_spec = pl.BlockSpec((ts, bk2), lambda j, t: (t, 0))

    def body(a_ref, b_ref, after_ref, o_ref, acc_ref):
        t = pl.program_id(1)

        @pl.when(t == 0)
        def _():
            acc_ref[...] = jnp.zeros(acc_ref.shape, F32)

        acc_ref[...] += _dot_tn(a_ref[...], b_ref[...])

        @pl.when(t == ns - 1)
        def _():
            o_ref[...] = acc_ref[...].astype(BF16)

    out = pl.pallas_call(
        body,
        name="wgrad",
        grid=(nblk, ns),
        in_specs=[a_spec, b_spec, ANY_SPEC],
        out_specs=pl.BlockSpec((None, bk1, bk2), lambda j, t: (j, 0, 0)),
        out_shape=pltpu.HBM((nblk, bk1, bk2), BF16),
        scratch_shapes=[pltpu.VMEM((bk1, bk2), F32)],
        compiler_params=_cparams(("parallel", "arbitrary"), 4 * _nbytes((bk1, bk2), F32) + 4 * _nbytes((ts, bk1 + bk2), BF16)),
    )(pltpu.with_memory_space_constraint(a, pltpu.HBM), pltpu.with_memory_space_constraint(b, pltpu.HBM), after)
    return out if cols_owned else out.reshape(N_DEV, k1 // N_DEV, k2)


def _adam_update(w, g, m, v):
    m = ADAM_B1 * m + (1.0 - ADAM_B1) * g
    v = ADAM_B2 * v + (1.0 - ADAM_B2) * (g * g)
    m_hat = m / (1.0 - ADAM_B1 ** ADAM_STEP)
    v_hat = v / (1.0 - ADAM_B2 ** ADAM_STEP)
    delta = -ADAM_LR * (m_hat / (jnp.sqrt(v_hat) + ADAM_EPS) + ADAM_WD * w)
    return delta, m, v


def _sum_adamw(recv, w, m, v, tr, after):
    nl, ra, cb = w.shape
    assert nl == len(recv) == 2

    def body(r0_ref, r1_ref, w_ref, m_ref, v_ref, after_ref, g_ref, d_ref, nm_ref, nv_ref):
        def total(r_ref):
            g = r_ref[0].astype(F32)
            for j in range(1, N_DEV):
                g = g + r_ref[j].astype(F32)
            return g

        g = jnp.where(pl.program_id(0) == 0, total(r0_ref), total(r1_ref))
        g_ref[...] = g
        d_ref[...], nm_ref[...], nv_ref[...] = _adam_update(w_ref[...], g, m_ref[...], v_ref[...])

    blk = pl.BlockSpec((None, tr, cb), lambda l, i: (l, i, 0))
    return pl.pallas_call(
        body,
        name="sum_adamw",
        grid=(nl, ra // tr),
        in_specs=[pl.BlockSpec((N_DEV, tr, cb), lambda l, i: (0, i * (1 - l), 0)),
                  pl.BlockSpec((N_DEV, tr, cb), lambda l, i: (0, i * l, 0)), blk, blk, blk, ANY_SPEC],
        out_specs=(blk,) * 4,
        out_shape=(jax.ShapeDtypeStruct((nl, ra, cb), F32),) * 4,
        compiler_params=_cparams(("arbitrary", "arbitrary"), 6 * _nbytes((N_DEV, tr, cb), BF16) + 32 * _nbytes((tr, cb), F32)),
    )(recv[0], recv[1], w, m, v, after)


def _adamw(w, g, m, v):
    def body(w_ref, g_ref, m_ref, v_ref, d_ref, nm_ref, nv_ref):
        d_ref[...], nm_ref[...], nv_ref[...] = _adam_update(w_ref[...], g_ref[...], m_ref[...], v_ref[...])

    return pl.pallas_call(
        body,
        name="adamw",
        in_specs=[VMEM_SPEC] * 4,
        out_specs=(VMEM_SPEC,) * 3,
        out_shape=(jax.ShapeDtypeStruct(w.shape, F32),) * 3,
        compiler_params=_cparams(vmem=10 * _nbytes(w.shape, F32)),
    )(w, g, m, v)


def _adamw_tiled(w, g, m, v, tr):
    nl, ra, cb = w.shape

    def body(w_ref, g_ref, m_ref, v_ref, d_ref, nm_ref, nv_ref):
        d_ref[...], nm_ref[...], nv_ref[...] = _adam_update(w_ref[...], g_ref[...], m_ref[...], v_ref[...])

    blk = pl.BlockSpec((None, tr, cb), lambda l, i: (l, i, 0))
    return pl.pallas_call(
        body,
        name="adamw_tiled",
        grid=(nl, ra // tr),
        in_specs=[blk] * 4,
        out_specs=(blk,) * 3,
        out_shape=(jax.ShapeDtypeStruct((nl, ra, cb), F32),) * 3,
        compiler_params=_cparams(("parallel", "parallel")),
    )(*_hbm(w, g, m, v))


def _token_tile(s):
    return min(256, max(SUBLANES * 2, s // 4))


def kernel(x, c, w_mod, b_mod, g_pre_mix, g_post_mix, w_in, conv_a_w, conv_a_b, w_a_out, conv_b_w, conv_b_b, w_gate_r, b_gate_r, w_gate_i, b_gate_i, lru_lambda, w_b_out, w_o, g_pre_mlp, g_post_mlp, w_mlp_up, w_mlp_down, loss_target, m_w_mod, m_b_mod, m_g_pre_mix, m_g_post_mix, m_w_in, m_conv_a_w, m_conv_a_b, m_w_a_out, m_conv_b_w, m_conv_b_b, m_w_gate_r, m_b_gate_r, m_w_gate_i, m_b_gate_i, m_lru_lambda, m_w_b_out, m_w_o, m_g_pre_mlp, m_g_post_mlp, m_w_mlp_up, m_w_mlp_down, v_w_mod, v_b_mod, v_g_pre_mix, v_g_post_mix, v_w_in, v_conv_a_w, v_conv_a_b, v_w_a_out, v_conv_b_w, v_conv_b_b, v_w_gate_r, v_b_gate_r, v_w_gate_i, v_b_gate_i, v_lru_lambda, v_w_b_out, v_w_o, v_g_pre_mlp, v_g_post_mlp, v_w_mlp_up, v_w_mlp_down):
    nl = w_mod.shape[0]
    s, d = x.shape[1], x.shape[2]
    nh, bw = w_gate_r.shape[1], w_gate_r.shape[2]
    cwid = conv_a_w.shape[2]
    tm = _token_tile(s)
    tmx = min(2 * tm, s)
    ts = s
    _, me = _my_position()
    xs = x.reshape(s, d)
    target = loss_target.reshape(s, d)

    vec_names = (g_pre_mix, g_post_mix, conv_a_b, conv_b_b, b_gate_r, b_gate_i, lru_lambda, g_pre_mlp, g_post_mlp)
    rows = jnp.concatenate([jnp.stack(vec_names, axis=1), jnp.zeros((nl, N_ROWS - len(vec_names), d), F32)], axis=1)
    cw = jnp.concatenate([conv_a_w, conv_b_w, jnp.zeros((nl, CW_ROWS - 7, cwid), F32)], axis=1)

    large = {"w_in": w_in, "w_a_out": w_a_out, "w_b_out": w_b_out, "w_o": w_o, "w_mlp_up": w_mlp_up, "w_mlp_down": w_mlp_down}
    groups = (("in", ("w_in",)), ("mix", ("w_a_out", "w_b_out", "w_o")), ("mlp", ("w_mlp_up", "w_mlp_down")))
    mod, cact, cwf = _prep_small(c, w_mod, b_mod, cw)
    mod = mod.reshape(nl, N_MOD, d)
    gathers = {}
    tok = mod
    for l in range(nl):
        for gname, members in groups:
            gathers[l, gname] = _gather2_start([large[n][l].astype(BF16) for n in members], tok, f"gather_start_{gname}{l}")
            tok = gathers[l, gname][4]
    wr = w_gate_r.astype(BF16)
    wi = w_gate_i.astype(BF16)

    forwarded = {}

    def forward(l, gname, after):
        forwarded[l, gname] = _gather2_forward(gathers[l, gname], after, f"gather_forward_{gname}{l}")
        return forwarded[l, gname][4]

    def gathered(l, gname, after):
        return _gather2_wait(forwarded[l, gname], after, f"gather_wait_{gname}{l}")

    saved = []
    weights = []
    xin = xs
    for l in range(nl):
        if l == 0:
            tok = forward(0, "in", tok)
        (win_f,) = gathered(l, "in", tok if l == 0 else xin)
        if l > 0:
            tok = forward(l, "mix", win_f)
        proj, h = _in_proj_fwd(xin, mod, rows, win_f, l, tmx, tok)
        if l == 0:
            tok = forward(0, "mix", proj)
        wa_f, wb_f, wo_f = (w.reshape(d, d) for w in gathered(l, "mix", proj))
        if l > 0:
            tok = forward(l, "mlp", wo_f)
        x1, hs, yap, ybp, y = _mixer_core_fwd(proj, xin, mod, rows, cwf, wr, wi, wa_f, wb_f, wo_f, l, tm, tok)
        if l == 0:
            forward(0, "mlp", x1)
        wup_f, wdn_f = gathered(l, "mlp", x1)
        wdn_f = wdn_f.reshape(-1, d)
        if l + 1 < nl:
            tok = forward(l + 1, "in", wdn_f)
        x2, ru, y2, h2 = _mlp_fwd(x1, mod, rows, wup_f, wdn_f, l, tmx, tok)
        saved.append((xin, proj, h, x1, hs, yap, ybp, y, ru, y2, h2))
        weights.append((win_f, wa_f, wb_f, wo_f, wup_f, wdn_f))
        xin = x2
    loss_part, dx = _loss_fwd_bwd(xin, target, tm)

    scatters = {}
    small = [None] * nl
    gate_parts = [None] * nl

    def scatter(l, gname, parts):
        scatters[l, gname] = _exchange_start(parts, False, f"scatter_start_{gname}{l}")
        return scatters[l, gname][4]

    for l in reversed(range(nl)):
        xin, proj, h, x1, hs, yap, ybp, y, ru, y2, h2 = saved[l]
        win_f, wa_f, wb_f, wo_f, wup_f, wdn_f = weights[l]
        dx1, dy2, dup, act, sm_mlp = _mlp_bwd(dx, x1, y2, ru, mod, rows, wup_f, wdn_f, l, tm)
        g_up = _wgrad(h2, dup, True, ts, dx1)
        g_dn = _wgrad(act, dy2, False, ts, g_up)
        tok = scatter(l, "mlp", [g_up, g_dn])
        dproj, dy, m, dyap, dybp, ya, yb, sm_mix, dwg = _mixer_core_bwd(
            dx1, y, yap, ybp, hs, proj, mod, rows, cwf, wr, wi, wa_f, wb_f, wo_f, l, tm // 2, tok)
        gate_parts[l] = _exchange_start([dwg.astype(BF16)], True, f"gates_start{l}")
        g_a = _wgrad(ya, dyap, False, ts, gate_parts[l][4])
        g_b = _wgrad(yb, dybp, False, ts, g_a)
        g_o = _wgrad(m, dy, False, ts, g_b)
        tok = scatter(l, "mix", [g_a, g_b, g_o])
        tok = scatter(l, "in", [_wgrad(h, dproj, True, ts, tok)])
        dx, sm_in = _in_proj_bwd(dproj, xin, dx1, mod, rows, win_f, l, tm, tok)
        small[l] = jnp.concatenate([sm_mlp, sm_mix, sm_in], axis=0)
    grad_x = dx.reshape(x.shape)

    recv = {}
    big = {}
    moments = {"w_in": (w_in, m_w_in, v_w_in), "w_mlp_up": (w_mlp_up, m_w_mlp_up, v_w_mlp_up), "w_a_out": (w_a_out, m_w_a_out, v_w_a_out),
               "w_b_out": (w_b_out, m_w_b_out, v_w_b_out), "w_o": (w_o, m_w_o, v_w_o), "w_mlp_down": (w_mlp_down, m_w_mlp_down, v_w_mlp_down)}

    def collect(l, gname, after):
        for n, zone in zip(dict(groups)[gname], _exchange_wait(scatters[l, gname], after, False, f"scatter_wait_{gname}{l}")):
            recv[n, l] = zone

    def update(name, after):
        w, m_, v_ = moments[name]
        big[name] = _sum_adamw([recv[name, l] for l in range(nl)], w, m_, v_, min(256, w.shape[1]), after)
        return big[name][1]

    for l, gname in ((1, "mlp"), (1, "in"), (1, "mix"), (0, "mlp")):
        collect(l, gname, dx)
    early = update("w_mlp_down", update("w_mlp_up", dx))

    lrows = jnp.concatenate(small, axis=0)
    lrows = lrows.at[G_LOSS_ROW, 0].set(loss_part)

    def lrow(a, l, r):
        return a[l * G_LAYER_ROWS + r]

    dm = jnp.stack([jnp.concatenate([lrow(lrows, l, G_MLP_ROWS + G_MIX_ROWS + G_IN_SH), lrow(lrows, l, G_MLP_ROWS + G_MIX_ROWS + G_IN_SC),
                                     lrow(lrows, l, G_MLP_ROWS + G_MIX_GT), lrow(lrows, l, G_MLP_SH), lrow(lrows, l, G_MLP_SC),
                                     lrow(lrows, l, G_MLP_GT)]) for l in range(nl)])
    dm8 = jnp.concatenate([dm[:, None, :], jnp.zeros((nl, SUBLANES - 1, N_MOD * d), F32)], axis=1)
    srows, g_w_mod = _reduce_small(lrows, dm8, cact, early)
    loss = srows[G_LOSS_ROW, 0]
    sgates = _sum_gathered([_exchange_wait(gate_parts[l], srows, True, f"gates_wait{l}")[0] for l in range(nl)])
    for gname in ("in", "mix"):
        collect(0, gname, srows)
    for name in ("w_in", "w_a_out", "w_b_out", "w_o"):
        update(name, srows)

    def srow(l, r):
        return lrow(srows, l, r)

    def per_layer(r):
        return jnp.stack([srow(l, r) for l in range(nl)])

    mix0 = G_MLP_ROWS
    in0 = G_MLP_ROWS + G_MIX_ROWS
    g_b_mod = jnp.stack([jnp.concatenate([srow(l, in0 + G_IN_SH), srow(l, in0 + G_IN_SC), srow(l, mix0 + G_MIX_GT),
                                          srow(l, G_MLP_SH), srow(l, G_MLP_SC), srow(l, G_MLP_GT)]) for l in range(nl)])
    conv_a_full = jnp.stack([jnp.stack([srow(l, mix0 + G_MIX_CAW + k) for k in range(3)]) for l in range(nl)])
    conv_b_full = jnp.stack([jnp.stack([srow(l, mix0 + G_MIX_CBW + k) for k in range(4)]) for l in range(nl)])
    grads = {
        "b_mod": g_b_mod,
        "g_pre_mix": per_layer(in0 + G_IN_GPRE),
        "g_post_mix": per_layer(mix0 + G_MIX_GPOST),
        "conv_a_w": lax.dynamic_slice_in_dim(conv_a_full, me * cwid, cwid, axis=2),
        "conv_a_b": per_layer(mix0 + G_MIX_CAB),
        "conv_b_w": lax.dynamic_slice_in_dim(conv_b_full, me * cwid, cwid, axis=2),
        "conv_b_b": per_layer(mix0 + G_MIX_CBB),
        "w_gate_r": sgates[:, 0],
        "b_gate_r": per_layer(mix0 + G_MIX_BR),
        "w_gate_i": sgates[:, 1],
        "b_gate_i": per_layer(mix0 + G_MIX_BI),
        "lru_lambda": per_layer(mix0 + G_MIX_LAM),
        "g_pre_mlp": per_layer(G_MLP_GPRE),
        "g_post_mlp": per_layer(G_MLP_GPOST),
    }
    params = {
        "b_mod": (b_mod, m_b_mod, v_b_mod), "g_pre_mix": (g_pre_mix, m_g_pre_mix, v_g_pre_mix), "g_post_mix": (g_post_mix, m_g_post_mix, v_g_post_mix),
        "conv_a_w": (conv_a_w, m_conv_a_w, v_conv_a_w), "conv_a_b": (conv_a_b, m_conv_a_b, v_conv_a_b),
        "conv_b_w": (conv_b_w, m_conv_b_w, v_conv_b_w), "conv_b_b": (conv_b_b, m_conv_b_b, v_conv_b_b),
        "w_gate_r": (w_gate_r, m_w_gate_r, v_w_gate_r), "b_gate_r": (b_gate_r, m_b_gate_r, v_b_gate_r),
        "w_gate_i": (w_gate_i, m_w_gate_i, v_w_gate_i), "b_gate_i": (b_gate_i, m_b_gate_i, v_b_gate_i),
        "lru_lambda": (lru_lambda, m_lru_lambda, v_lru_lambda), "g_pre_mlp": (g_pre_mlp, m_g_pre_mlp, v_g_pre_mlp),
        "g_post_mlp": (g_post_mlp, m_g_post_mlp, v_g_post_mlp),
    }
    out = {}
    for name, g in grads.items():
        w, m_, v_ = params[name]
        flat = (-1, w.shape[-1])
        dl, nm, nv = _adamw(w.reshape(flat), g.reshape(flat), m_.reshape(flat), v_.reshape(flat))
        out[name] = (g.reshape(w.shape), dl.reshape(w.shape), nm.reshape(w.shape), nv.reshape(w.shape))
    out["w_mod"] = (g_w_mod,) + tuple(_adamw_tiled(w_mod, g_w_mod, m_w_mod, v_w_mod, min(128, d)))
    out.update(big)

    order = ("w_mod", "b_mod", "g_pre_mix", "g_post_mix", "w_in", "conv_a_w", "conv_a_b", "w_a_out", "conv_b_w", "conv_b_b", "w_gate_r", "b_gate_r",
             "w_gate_i", "b_gate_i", "lru_lambda", "w_b_out", "w_o", "g_pre_mlp", "g_post_mlp", "w_mlp_up", "w_mlp_down")
    return (loss, grad_x) + tuple(out[n][0] for n in order) + tuple(out[n][1] for n in order) + tuple(out[n][2] for n in order) + tuple(out[n][3] for n in order)
```

```python
import functools

import jax
import jax.numpy as jnp
from jax import lax
from jax.experimental import pallas as pl
from jax.experimental.pallas import tpu as pltpu

F32, BF16 = jnp.float32, jnp.bfloat16
EPS = 1e-6
LRU_C = 8.0
N_DEV = 8
N_MOD = 6
SUBLANES = 8
VMEM_BUDGET = 56 * 1024 * 1024
WGRAD_ROWS = 512
ADAM_LR, ADAM_B1, ADAM_B2, ADAM_EPS, ADAM_WD, ADAM_STEP = 0.001, 0.9, 0.999, 1e-08, 0.01, 10
MESH = pl.DeviceIdType.MESH
VMEM_SPEC = pl.BlockSpec(memory_space=pltpu.VMEM)
ANY_SPEC = pl.BlockSpec(memory_space=pl.ANY)
HBM_SPEC = pl.BlockSpec(memory_space=pltpu.HBM)
SEM_SPEC = pl.BlockSpec(memory_space=pltpu.SEMAPHORE)
SIDE_EFFECT = pltpu.SideEffectType.DATAFLOW_SIDE_EFFECTING

R_G_PRE_MIX, R_G_POST_MIX, R_CONV_A_B, R_CONV_B_B, R_B_GATE_R, R_B_GATE_I, R_LAMBDA, R_G_PRE_MLP, R_G_POST_MLP = range(9)
N_ROWS = 16
M_SH_M, M_SC_M, M_GT_M, M_SH_F, M_SC_F, M_GT_F = range(6)
CW_A, CW_B, CW_ROWS = 0, 3, 8
G_MLP_GT, G_MLP_GPOST, G_MLP_SC, G_MLP_SH, G_MLP_GPRE, G_LOSS_ROW, G_MLP_ROWS = 0, 1, 2, 3, 4, 7, 8
(G_MIX_GT, G_MIX_GPOST, G_MIX_CAB, G_MIX_CAW, G_MIX_CBB, G_MIX_CBW, G_MIX_BR, G_MIX_BI, G_MIX_LAM) = 0, 1, 2, 3, 6, 7, 11, 12, 13
G_MIX_ROWS = 16
G_IN_SC, G_IN_SH, G_IN_GPRE, G_IN_ROWS = 0, 1, 2, 8
G_LAYER_ROWS = G_MLP_ROWS + G_MIX_ROWS + G_IN_ROWS


def _cparams(dims=None, vmem=None):
    kw = {}
    if dims is not None:
        kw["dimension_semantics"] = dims
    if vmem is not None:
        kw["vmem_limit_bytes"] = int(min(max(vmem, 16 * 1024 * 1024), VMEM_BUDGET))
    return pltpu.CompilerParams(**kw)


def _nbytes(shape, dtype):
    n = 1
    for s in shape:
        n *= s
    return n * jnp.dtype(dtype).itemsize


def _hbm(*arrays):
    return tuple(pltpu.with_memory_space_constraint(a, pltpu.HBM) for a in arrays)


def _resident(block, index_map):
    return pl.BlockSpec(block, index_map, pipeline_mode=pl.Buffered(1))


def _my_position():
    x, y, c = lax.axis_index("x"), lax.axis_index("y"), lax.axis_index("c")
    return (x, y, c), 4 * x + 2 * y + c


def _peer(pos, k):
    x, y, c = pos
    px = 1 - x if k & 4 else x
    py = 1 - y if k & 2 else y
    pc = 1 - c if k & 1 else c
    return (px, py, pc), 4 * px + 2 * py + pc


def _remote(src, dst, ssem, rsem, peer):
    return pltpu.make_async_remote_copy(src_ref=src, dst_ref=dst, send_sem=ssem, recv_sem=rsem, device_id=peer, device_id_type=MESH)


def _dot(a, b):
    return jnp.dot(a, b, preferred_element_type=F32)


def _dot_nt(a, b):
    return lax.dot_general(a, b, (((1,), (1,)), ((), ())), preferred_element_type=F32)


def _dot_tn(a, b):
    return lax.dot_general(a, b, (((0,), (0,)), ((), ())), preferred_element_type=F32)


def _colsum(v):
    return jnp.sum(v, axis=0, keepdims=True)


def _sigmoid(v):
    return jax.nn.sigmoid(v)


GELU_K, GELU_C = 0.7978845608028654, 0.044715


def _gelu(v):
    s = 1.0 / (1.0 + jnp.exp(v * (-2.0 * GELU_K - (2.0 * GELU_K * GELU_C) * (v * v))))
    return v * s, s


def _gelu_grad(v, s):
    return s * (1.0 + (v * (1.0 - s)) * (2.0 * GELU_K + (6.0 * GELU_K * GELU_C) * (v * v)))


def _neg_expm1_twice(v):
    t = jnp.tanh(v)
    return (-2.0 * t) / (1.0 - t)


def _softplus_neg(lam):
    z = -lam
    u = jnp.exp(-jnp.abs(z))
    w = 1.0 + u
    l1p = jnp.where(w == 1.0, u, jnp.log(w) * u / (w - 1.0))
    return jnp.maximum(z, 0.0) + l1p


def _rms(v):
    return lax.rsqrt(jnp.mean(v * v, axis=-1, keepdims=True) + EPS)


def _prenorm_bwd(xv, dh, g, sc):
    r = _rms(xv)
    xn = xv * r
    n = xn * g
    dsc = _colsum(dh * n)
    dsh = _colsum(dh)
    dn = dh * (1.0 + sc)
    dg = _colsum(dn * xn)
    dxn = dn * g
    dx = r * (dxn - xn * jnp.mean(dxn * xn, axis=-1, keepdims=True))
    return dx, dsc, dsh, dg


def _postnorm_bwd(yv, dout, g, gt):
    r = _rms(yv)
    yn = yv * r
    dgt = _colsum(dout * (yn * g))
    dn = dout * gt
    dg = _colsum(dn * yn)
    dyn = dn * g
    dy = r * (dyn - yn * jnp.mean(dyn * yn, axis=-1, keepdims=True))
    return dy, dgt, dg


def _gates(xc, wr_ref, wi_ref, b_r, b_i, sp, nh, bw):
    xcb = xc.astype(BF16)
    zr = jnp.concatenate([_dot(xcb[:, h * bw:(h + 1) * bw], wr_ref[h]) for h in range(nh)], axis=1) + b_r
    zi = jnp.concatenate([_dot(xcb[:, h * bw:(h + 1) * bw], wi_ref[h]) for h in range(nh)], axis=1) + b_i
    r = _sigmoid(zr)
    ig = _sigmoid(zi)
    la = (-LRU_C * r) * sp
    a = jnp.exp(la)
    mult = jnp.sqrt(_neg_expm1_twice(la))
    return xcb, r, ig, a, mult


def _shift_rows(cur, edge, k, up):
    t, dd = cur.shape
    blocks = cur.reshape(t // SUBLANES, SUBLANES, dd)
    row = lax.broadcasted_iota(jnp.int32, (1, SUBLANES, dd), 1)
    if up:
        r = pltpu.roll(blocks, SUBLANES - k, 1)
        nxt = jnp.concatenate([r[1:], pltpu.roll(edge, SUBLANES - k, 0)[None]], axis=0)
        out = jnp.where(row >= SUBLANES - k, nxt, r)
    else:
        r = pltpu.roll(blocks, k, 1)
        prv = jnp.concatenate([pltpu.roll(edge, k, 0)[None], r[:-1]], axis=0)
        out = jnp.where(row < k, prv, r)
    return out.reshape(t, dd)


def _scan_block(a8, b8, reverse):
    row = lax.broadcasted_iota(jnp.int32, a8.shape, 0)
    for s in (1, 2, 4):
        if reverse:
            keep = row < SUBLANES - s
            a_sh = pltpu.roll(a8, SUBLANES - s, 0)
            b_sh = pltpu.roll(b8, SUBLANES - s, 0)
        else:
            keep = row >= s
            a_sh = pltpu.roll(a8, s, 0)
            b_sh = pltpu.roll(b8, s, 0)
        b8 = b8 + a8 * jnp.where(keep, b_sh, 0.0)
        a8 = a8 * jnp.where(keep, a_sh, 1.0)
    return a8, b8


def _prep_small(c, w_mod, b_mod, cw):
    d = c.shape[1]
    cm = w_mod.shape[2]
    cwid = cw.shape[2]
    nl = w_mod.shape[0]

    def body(c_ref, wm_ref, bm_ref, cw_ref, mod_ref, cact_ref, cwf_ref, cbuf, pbuf, rbuf, ssem, rsem, lsem):
        pos, me = _my_position()
        me8 = pl.multiple_of(me * SUBLANES, SUBLANES)
        cbuf[pl.ds(me8, SUBLANES), :] = jnp.broadcast_to(c_ref[...], (SUBLANES, d))
        own_cw = pltpu.make_async_copy(cw_ref, cwf_ref.at[:, :, pl.ds(me * cwid, cwid)], lsem.at[0])
        own_cw.start()
        first = []
        for k in range(1, N_DEV):
            peer, _ = _peer(pos, k)
            rows = cbuf.at[pl.ds(me8, SUBLANES), :]
            first.append(_remote(rows, rows, ssem.at[0, k - 1], rsem.at[0, k - 1], peer))
            first.append(_remote(cw_ref, cwf_ref.at[:, :, pl.ds(me * cwid, cwid)], ssem.at[1, k - 1], rsem.at[1, k - 1], peer))
        for cp in first:
            cp.start()
        for k in range(1, N_DEV):
            peer, pj = _peer(pos, k)
            pj8 = pl.multiple_of(pj * SUBLANES, SUBLANES)
            rows = cbuf.at[pl.ds(pj8, SUBLANES), :]
            _remote(rows, rows, ssem.at[0, k - 1], rsem.at[0, k - 1], peer).wait_recv()
        cv = cbuf[...]
        cact = cv * _sigmoid(cv)
        cact_ref[...] = cact
        cb = cact.astype(BF16)
        for l in range(nl):
            pbuf[l] = _dot(cb, wm_ref[l].astype(BF16))
        own_p = pltpu.make_async_copy(pbuf.at[:, pl.ds(me8, SUBLANES), :], rbuf.at[me], lsem.at[1])
        own_p.start()
        second = []
        for k in range(1, N_DEV):
            peer, pj = _peer(pos, k)
            pj8 = pl.multiple_of(pj * SUBLANES, SUBLANES)
            second.append(_remote(pbuf.at[:, pl.ds(pj8, SUBLANES), :], rbuf.at[me], ssem.at[2, k - 1], rsem.at[2, k - 1], peer))
        for cp in second:
            cp.start()
        for k in range(1, N_DEV):
            peer, pj = _peer(pos, k)
            _remote(pbuf.at[:, pl.ds(0, SUBLANES), :], rbuf.at[pj], ssem.at[2, k - 1], rsem.at[2, k - 1], peer).wait_recv()
            _remote(cw_ref, cwf_ref.at[:, :, pl.ds(pj * cwid, cwid)], ssem.at[1, k - 1], rsem.at[1, k - 1], peer).wait_recv()
        own_p.wait()
        own_cw.wait()
        for l in range(nl):
            for j in range(N_DEV):
                mod_ref[l:l + 1, j * cm:(j + 1) * cm] = rbuf[j, l, 0:1, :] + bm_ref[l:l + 1, j * cm:(j + 1) * cm]
        for cp in first + second:
            cp.wait_send()

    return pl.pallas_call(
        body,
        name="prep_small",
        out_shape=(
            jax.ShapeDtypeStruct((nl, N_MOD * d), F32),
            jax.ShapeDtypeStruct((N_DEV * SUBLANES, d), F32),
            jax.ShapeDtypeStruct((nl, CW_ROWS, d), F32),
        ),
        in_specs=[VMEM_SPEC] * 4,
        out_specs=(VMEM_SPEC,) * 3,
        scratch_shapes=[
            pltpu.VMEM((N_DEV * SUBLANES, d), F32),
            pltpu.VMEM((nl, N_DEV * SUBLANES, cm), F32),
            pltpu.VMEM((N_DEV, nl, SUBLANES, cm), F32),
            pltpu.SemaphoreType.DMA((3, N_DEV - 1)),
            pltpu.SemaphoreType.DMA((3, N_DEV - 1)),
            pltpu.SemaphoreType.DMA((2,)),
        ],
        compiler_params=_cparams(vmem=3 * _nbytes(w_mod.shape, F32)),
    )(c, w_mod, b_mod, cw)


def _exchange_start(parts, gather, name):
    n = len(parts)
    lands = [lax.empty(((N_DEV,) + tuple(p.shape)) if gather else tuple(p.shape), p.dtype) for p in parts]

    def body(*refs):
        ins, lnd = refs[:n], refs[n:2 * n]
        ssem, rsem, token = refs[2 * n], refs[2 * n + 1], refs[-1]
        pos, me = _my_position()
        for k in range(1, N_DEV):
            peer, pj = _peer(pos, k)
            for t in range(n):
                src = ins[t] if gather else ins[t].at[pj]
                q = t * (N_DEV - 1) + k - 1
                _remote(src, lnd[t].at[me], ssem.at[q], rsem.at[q], peer).start()
        token[...] = jnp.zeros(token.shape, F32)

    out = pl.pallas_call(
        body,
        name=name,
        out_shape=(pltpu.SemaphoreType.DMA((n * (N_DEV - 1),)), pltpu.SemaphoreType.DMA((n * (N_DEV - 1),)))
        + tuple(pltpu.HBM(p.shape, p.dtype) for p in parts) + tuple(pltpu.HBM(p.shape, p.dtype) for p in lands)
        + (jax.ShapeDtypeStruct((SUBLANES, 128), F32),),
        in_specs=[HBM_SPEC] * (2 * n),
        out_specs=(SEM_SPEC, SEM_SPEC) + (HBM_SPEC,) * (2 * n) + (VMEM_SPEC,),
        input_output_aliases={i: 2 + i for i in range(2 * n)},
        compiler_params=pltpu.CompilerParams(has_side_effects=SIDE_EFFECT),
    )(*[pltpu.with_memory_space_constraint(p, pltpu.HBM) for p in list(parts) + lands])
    return out[0], out[1], list(out[2:2 + n]), list(out[2 + n:2 + 2 * n]), out[-1]


def _exchange_wait(started, after, gather, name):
    ssem, rsem, parts, lands, _ = started
    n = len(parts)

    def body(*refs):
        ins, lnd = refs[:n], refs[n:2 * n]
        ssem_ref, rsem_ref = refs[2 * n], refs[2 * n + 1]
        stage, lsem = refs[-1 - n:-1], refs[-1]
        pos, me = _my_position()
        load = []
        for t in range(n):
            src = ins[t] if gather else ins[t].at[me]
            load.append(pltpu.make_async_copy(src, stage[t], lsem.at[t]))
            load[-1].start()
        store = []
        for t in range(n):
            load[t].wait()
            store.append(pltpu.make_async_copy(stage[t], lnd[t].at[me], lsem.at[t]))
            store[-1].start()
        for k in range(1, N_DEV):
            peer, pj = _peer(pos, k)
            for t in range(n):
                src = ins[t] if gather else ins[t].at[pj]
                q = t * (N_DEV - 1) + k - 1
                _remote(src, lnd[t].at[me], ssem_ref.at[q], rsem_ref.at[q], peer).wait_send()
                _remote(src, lnd[t].at[pj], ssem_ref.at[q], rsem_ref.at[q], peer).wait_recv()
        for cp in store:
            cp.wait()

    out = pl.pallas_call(
        body,
        name=name,
        out_shape=tuple(pltpu.HBM(p.shape, p.dtype) for p in parts) + tuple(pltpu.HBM(p.shape, p.dtype) for p in lands),
        in_specs=[HBM_SPEC] * (2 * n) + [SEM_SPEC, SEM_SPEC, ANY_SPEC],
        out_specs=(HBM_SPEC,) * (2 * n),
        input_output_aliases={i: i for i in range(2 * n)},
        scratch_shapes=[pltpu.VMEM(tuple(z.shape[1:]), z.dtype) for z in lands] + [pltpu.SemaphoreType.DMA((n,))],
        compiler_params=pltpu.CompilerParams(has_side_effects=SIDE_EFFECT),
    )(*parts, *lands, ssem, rsem, after)
    return list(out[n:])


CHIP_PEERS = 3


def _chip_peer(pos, k, core):
    x, y, _ = pos
    px = 1 - x if k & 2 else x
    py = 1 - y if k & 1 else y
    return (px, py, core), 4 * px + 2 * py + core


def _gather2_start(parts, after, name):
    n = len(parts)
    lands = [lax.empty((N_DEV,) + tuple(p.shape), p.dtype) for p in parts]
    per = CHIP_PEERS + 1

    def body(*refs):
        ins, lnd = refs[:n], refs[n:2 * n]
        ssem, rsem, token = refs[2 * n + 1], refs[2 * n + 2], refs[-1]
        pos, me = _my_position()
        sibling = (pos[0], pos[1], 1 - pos[2])
        for t in range(n):
            _remote(ins[t], lnd[t].at[me], ssem.at[per * t], rsem.at[per * t], sibling).start()
        for k in range(1, per):
            peer, _ = _chip_peer(pos, k, pos[2])
            for t in range(n):
                _remote(ins[t], lnd[t].at[me], ssem.at[per * t + k], rsem.at[per * t + k], peer).start()
        token[...] = jnp.zeros(token.shape, F32)

    out = pl.pallas_call(
        body,
        name=name,
        out_shape=(pltpu.SemaphoreType.DMA((n * per,)), pltpu.SemaphoreType.DMA((n * per,)))
        + tuple(pltpu.HBM(p.shape, p.dtype) for p in parts) + tuple(pltpu.HBM(p.shape, p.dtype) for p in lands)
        + (jax.ShapeDtypeStruct((SUBLANES, 128), F32),),
        in_specs=[HBM_SPEC] * (2 * n) + [ANY_SPEC],
        out_specs=(SEM_SPEC, SEM_SPEC) + (HBM_SPEC,) * (2 * n) + (VMEM_SPEC,),
        input_output_aliases={i: 2 + i for i in range(2 * n)},
        compiler_params=pltpu.CompilerParams(has_side_effects=SIDE_EFFECT),
    )(*[pltpu.with_memory_space_constraint(p, pltpu.HBM) for p in list(parts) + lands], after)
    return out[0], out[1], list(out[2:2 + n]), list(out[2 + n:2 + 2 * n]), out[-1]


def _gather2_forward(started, after, name):
    ssem, rsem, parts, lands, _ = started
    n = len(lands)
    per = CHIP_PEERS + 1

    def body(*refs):
        lnd = refs[:n]
        rsem_a, fsend, frecv, token = refs[n], refs[2 * n + 2], refs[2 * n + 3], refs[2 * n + 4]
        pos, _ = _my_position()
        sibling = (pos[0], pos[1], 1 - pos[2])
        for k in range(1, per):
            peer, pk = _chip_peer(pos, k, pos[2])
            for t in range(n):
                block = lnd[t].at[pk]
                _remote(block, block, rsem_a.at[per * t + k], rsem_a.at[per * t + k], peer).wait_recv()
                q = CHIP_PEERS * t + k - 1
                _remote(block, block, fsend.at[q], frecv.at[q], sibling).start()
        token[...] = jnp.zeros(token.shape, F32)

    out = pl.pallas_call(
        body,
        name=name,
        out_shape=tuple(pltpu.HBM(p.shape, p.dtype) for p in lands)
        + (pltpu.SemaphoreType.DMA((n * CHIP_PEERS,)), pltpu.SemaphoreType.DMA((n * CHIP_PEERS,)),
           jax.ShapeDtypeStruct((SUBLANES, 128), F32)),
        in_specs=[HBM_SPEC] * n + [SEM_SPEC, ANY_SPEC],
        out_specs=(HBM_SPEC,) * n + (SEM_SPEC, SEM_SPEC, VMEM_SPEC),
        input_output_aliases={i: i for i in range(n)},
        compiler_params=pltpu.CompilerParams(has_side_effects=SIDE_EFFECT),
    )(*lands, rsem, after)
    return ssem, rsem, parts, list(out[:n]), out[n + 2], out[n], out[n + 1]


def _gather2_wait(forwarded, after, name):
    ssem, rsem, parts, lands, _, fsend, frecv = forwarded
    n = len(parts)
    per = CHIP_PEERS + 1

    def body(*refs):
        ins, lnd = refs[:n], refs[n:2 * n]
        ssem_a, rsem_a, fs, fr = refs[2 * n:2 * n + 4]
        stage, lsem = refs[-1 - n:-1], refs[-1]
        pos, me = _my_position()
        sibling = (pos[0], pos[1], 1 - pos[2])
        sib = 4 * pos[0] + 2 * pos[1] + 1 - pos[2]
        load = []
        for t in range(n):
            load.append(pltpu.make_async_copy(ins[t], stage[t], lsem.at[t]))
            load[-1].start()
        store = []
        for t in range(n):
            load[t].wait()
            store.append(pltpu.make_async_copy(stage[t], lnd[t].at[me], lsem.at[t]))
            store[-1].start()
        for t in range(n):
            _remote(ins[t], lnd[t].at[me], ssem_a.at[per * t], rsem_a.at[per * t], sibling).wait_send()
            _remote(ins[t], lnd[t].at[sib], ssem_a.at[per * t], rsem_a.at[per * t], sibling).wait_recv()
        for k in range(1, per):
            peer, pk = _chip_peer(pos, k, pos[2])
            _, qk = _chip_peer(pos, k, 1 - pos[2])
            for t in range(n):
                q = CHIP_PEERS * t + k - 1
                _remote(ins[t], lnd[t].at[me], ssem_a.at[per * t + k], rsem_a.at[per * t + k], peer).wait_send()
                _remote(lnd[t].at[pk], lnd[t].at[pk], fs.at[q], fr.at[q], sibling).wait_send()
                _remote(lnd[t].at[qk], lnd[t].at[qk], fs.at[q], fr.at[q], sibling).wait_recv()
        for cp in store:
            cp.wait()

    out = pl.pallas_call(
        body,
        name=name,
        out_shape=tuple(pltpu.HBM(p.shape, p.dtype) for p in parts) + tuple(pltpu.HBM(p.shape, p.dtype) for p in lands),
        in_specs=[HBM_SPEC] * (2 * n) + [SEM_SPEC] * 4 + [ANY_SPEC],
        out_specs=(HBM_SPEC,) * (2 * n),
        input_output_aliases={i: i for i in range(2 * n)},
        scratch_shapes=[pltpu.VMEM(tuple(z.shape[1:]), z.dtype) for z in lands] + [pltpu.SemaphoreType.DMA((n,))],
        compiler_params=pltpu.CompilerParams(has_side_effects=SIDE_EFFECT),
    )(*parts, *lands, ssem, rsem, fsend, frecv, after)
    return list(out[n:])


def _reduce_small(rows, dm8, cact, after):
    r, d = rows.shape
    nl = dm8.shape[0]
    cm = dm8.shape[2] // N_DEV

    def body(rows_ref, dm_ref, cact_ref, after_ref, orow_ref, owm_ref, gr, dmr, ssem, rsem, lsem):
        pos, me = _my_position()
        me8 = pl.multiple_of(me * SUBLANES, SUBLANES)
        gr[me] = rows_ref[...]
        own_dm = pltpu.make_async_copy(dm_ref.at[:, :, pl.ds(me * cm, cm)], dmr.at[:, pl.ds(me8, SUBLANES), :], lsem.at[0])
        own_dm.start()
        sends = []
        for k in range(1, N_DEV):
            peer, pj = _peer(pos, k)
            sends.append(_remote(gr.at[me], gr.at[me], ssem.at[0, k - 1], rsem.at[0, k - 1], peer))
            sends.append(_remote(dm_ref.at[:, :, pl.ds(pj * cm, cm)], dmr.at[:, pl.ds(me8, SUBLANES), :],
                                 ssem.at[1, k - 1], rsem.at[1, k - 1], peer))
        for cp in sends:
            cp.start()
        for k in range(1, N_DEV):
            peer, pj = _peer(pos, k)
            pj8 = pl.multiple_of(pj * SUBLANES, SUBLANES)
            _remote(gr.at[pj], gr.at[pj], ssem.at[0, k - 1], rsem.at[0, k - 1], peer).wait_recv()
            _remote(dm_ref.at[:, :, pl.ds(0, cm)], dmr.at[:, pl.ds(pj8, SUBLANES), :], ssem.at[1, k - 1], rsem.at[1, k - 1], peer).wait_recv()
        own_dm.wait()
        acc = gr[0]
        for j in range(1, N_DEV):
            acc = acc + gr[j]
        orow_ref[...] = acc
        cb = cact_ref[...].astype(BF16)
        for l in range(nl):
            owm_ref[l] = _dot_tn(cb, dmr[l].astype(BF16))
        for cp in sends:
            cp.wait_send()

    return pl.pallas_call(
        body,
        name="reduce_small",
        out_shape=(jax.ShapeDtypeStruct((r, d), F32), jax.ShapeDtypeStruct((nl, d, cm), F32)),
        in_specs=[VMEM_SPEC] * 3 + [ANY_SPEC],
        out_specs=(VMEM_SPEC,) * 2,
        scratch_shapes=[
            pltpu.VMEM((N_DEV, r, d), F32),
            pltpu.VMEM((nl, N_DEV * SUBLANES, cm), F32),
            pltpu.SemaphoreType.DMA((2, N_DEV - 1)),
            pltpu.SemaphoreType.DMA((2, N_DEV - 1)),
            pltpu.SemaphoreType.DMA((1,)),
        ],
        compiler_params=_cparams(vmem=4 * _nbytes((N_DEV, r, d), F32) + 6 * _nbytes((nl, d, cm), F32)),
    )(rows, dm8, cact, after)


def _sum_gathered(zones):
    nl = len(zones)

    def body(*refs):
        for l in range(nl):
            acc = refs[l][0].astype(F32)
            for j in range(1, N_DEV):
                acc = acc + refs[l][j].astype(F32)
            refs[nl][l] = acc

    return pl.pallas_call(
        body,
        name="sum_gathered",
        out_shape=jax.ShapeDtypeStruct((nl,) + tuple(zones[0].shape[1:]), F32),
        in_specs=[VMEM_SPEC] * nl,
        out_specs=VMEM_SPEC,
        compiler_params=_cparams(vmem=8 * nl * _nbytes(zones[0].shape, BF16)),
    )(*zones)


def _in_proj_fwd(x, mod, rows, win_f, l, tm, after):
    s, d = x.shape
    nb, _, ci = win_f.shape

    def body(x_ref, mod_ref, rows_ref, w_ref, after_ref, proj_ref, h_ref):
        xv = x_ref[...]
        g = rows_ref[R_G_PRE_MIX:R_G_PRE_MIX + 1, :]
        h = (xv * _rms(xv) * g) * (1.0 + mod_ref[M_SC_M:M_SC_M + 1, :]) + mod_ref[M_SH_M:M_SH_M + 1, :]
        hb = h.astype(BF16)
        h_ref[...] = hb
        for j in range(nb):
            proj_ref[:, j * ci:(j + 1) * ci] = _dot(hb, w_ref[j])

    return pl.pallas_call(
        body,
        name="in_proj_fwd",
        grid=(s // tm,),
        in_specs=[
            pl.BlockSpec((tm, d), lambda i: (i, 0)),
            _resident((None, N_MOD, d), lambda i: (l, 0, 0)),
            _resident((None, N_ROWS, d), lambda i: (l, 0, 0)),
            _resident((nb, d, ci), lambda i: (0, 0, 0)),
            ANY_SPEC,
        ],
        out_specs=(pl.BlockSpec((tm, nb * ci), lambda i: (i, 0)), pl.BlockSpec((tm, d), lambda i: (i, 0))),
        out_shape=(jax.ShapeDtypeStruct((s, nb * ci), F32), jax.ShapeDtypeStruct((s, d), BF16)),
        compiler_params=_cparams(("parallel",), _nbytes((nb, d, ci), BF16) + 3 * _nbytes((tm, nb * ci), F32) + 8 * _nbytes((tm, d), F32)),
    )(*_hbm(x), mod, rows, win_f, after)


def _mixer_core_fwd(proj, x, mod, rows, cwf, wr, wi, wa_f, wb_f, wo_f, l, tm, after):
    s, d = x.shape
    nh, bw, _ = wr.shape[1:]

    def body(proj_ref, x_ref, mod_ref, rows_ref, cw_ref, wr_ref, wi_ref, wa_ref, wb_ref, wo_ref, after_ref,
             x1_ref, hs_ref, yap_ref, ybp_ref, y_ref, cvbuf, xbbuf, a_s, b_s, hprev):
        i = pl.program_id(0)

        @pl.when(i == 0)
        def _():
            cvbuf[...] = jnp.zeros((SUBLANES, d), F32)
            xbbuf[...] = jnp.zeros((SUBLANES, d), F32)
            hprev[...] = jnp.zeros((SUBLANES, d), F32)

        def row(r):
            return rows_ref[r:r + 1, :]

        def tap(r):
            return cw_ref[r:r + 1, :]

        ba = proj_ref[:, 0:d]
        cv = proj_ref[:, d:2 * d] * proj_ref[:, 2 * d:3 * d]
        cvt = cvbuf[...]
        conv3 = ((row(R_CONV_A_B) + _shift_rows(cv, cvt, 2, False) * tap(CW_A)) + _shift_rows(cv, cvt, 1, False) * tap(CW_A + 1)) + cv * tap(CW_A + 2)
        ya = ba * conv3
        cvbuf[...] = cv[tm - SUBLANES:, :]
        xb = proj_ref[:, 3 * d:4 * d]
        xbt = xbbuf[...]
        xc = (((row(R_CONV_B_B) + _shift_rows(xb, xbt, 3, False) * tap(CW_B)) + _shift_rows(xb, xbt, 2, False) * tap(CW_B + 1))
              + _shift_rows(xb, xbt, 1, False) * tap(CW_B + 2)) + xb * tap(CW_B + 3)
        xbbuf[...] = xb[tm - SUBLANES:, :]
        sp = _softplus_neg(row(R_LAMBDA))
        _, _, ig, a, mult = _gates(xc, wr_ref, wi_ref, row(R_B_GATE_R), row(R_B_GATE_I), sp, nh, bw)
        a_s[...] = a
        b_s[...] = mult * (ig * xc)

        def blk(j, hp):
            o = pl.multiple_of(j * SUBLANES, SUBLANES)
            a8, b8 = _scan_block(a_s[pl.ds(o, SUBLANES), :], b_s[pl.ds(o, SUBLANES), :], reverse=False)
            h8 = b8 + a8 * hp
            hs_ref[pl.ds(o, SUBLANES), :] = h8
            return jnp.broadcast_to(h8[SUBLANES - 1:SUBLANES, :], (SUBLANES, d))

        hprev[...] = lax.fori_loop(0, tm // SUBLANES, blk, hprev[...])
        gel, _ = _gelu(proj_ref[:, 4 * d:5 * d])
        yb = hs_ref[...] * gel
        yap = _dot(ya.astype(BF16), wa_ref[...])
        ybp = _dot(yb.astype(BF16), wb_ref[...])
        yap_ref[...] = yap
        ybp_ref[...] = ybp
        m = _sigmoid(proj_ref[:, 5 * d:6 * d]) * yap + _sigmoid(proj_ref[:, 6 * d:7 * d]) * ybp
        y = _dot(m.astype(BF16), wo_ref[...])
        y_ref[...] = y
        x1_ref[...] = x_ref[...] + mod_ref[M_GT_M:M_GT_M + 1, :] * ((y * _rms(y)) * row(R_G_POST_MIX))

    tile = pl.BlockSpec((tm, d), lambda i: (i, 0))
    return pl.pallas_call(
        body,
        name="mixer_core_fwd",
        grid=(s // tm,),
        in_specs=[
            pl.BlockSpec((tm, 7 * d), lambda i: (i, 0)),
            tile,
            _resident((None, N_MOD, d), lambda i: (l, 0, 0)),
            _resident((None, N_ROWS, d), lambda i: (l, 0, 0)),
            _resident((None, CW_ROWS, d), lambda i: (l, 0, 0)),
            _resident((None, nh, bw, bw), lambda i: (l, 0, 0, 0)),
            _resident((None, nh, bw, bw), lambda i: (l, 0, 0, 0)),
            _resident((d, d), lambda i: (0, 0)),
            _resident((d, d), lambda i: (0, 0)),
            _resident((d, d), lambda i: (0, 0)),
            ANY_SPEC,
        ],
        out_specs=(tile,) * 5,
        out_shape=(jax.ShapeDtypeStruct((s, d), F32),) * 5,
        scratch_shapes=[
            pltpu.VMEM((SUBLANES, d), F32),
            pltpu.VMEM((SUBLANES, d), F32),
            pltpu.VMEM((tm, d), F32),
            pltpu.VMEM((tm, d), F32),
            pltpu.VMEM((SUBLANES, d), F32),
        ],
        compiler_params=_cparams(("arbitrary",), 3 * _nbytes((d, d), BF16) + 2 * _nbytes((tm, 7 * d), F32) + 40 * _nbytes((tm, d), F32)),
    )(*_hbm(proj, x), mod, rows, cwf, wr, wi, wa_f, wb_f, wo_f, after)


def _mlp_fwd(x1, mod, rows, wup_f, wdn_f, l, tm, after):
    s, d = x1.shape
    nb, _, cu = wup_f.shape
    dff = nb * cu

    def body(x1_ref, mod_ref, rows_ref, wu_ref, wd_ref, after_ref, x2_ref, ru_ref, y2_ref, h2_ref):
        xv = x1_ref[...]
        g = rows_ref[R_G_PRE_MLP:R_G_PRE_MLP + 1, :]
        h2 = ((xv * _rms(xv) * g) * (1.0 + mod_ref[M_SC_F:M_SC_F + 1, :]) + mod_ref[M_SH_F:M_SH_F + 1, :]).astype(BF16)
        h2_ref[...] = h2
        ru = jnp.concatenate([jnp.maximum(_dot(h2, wu_ref[j]), 0.0) for j in range(nb)], axis=1)
        ru_ref[...] = ru.astype(BF16)
        y2 = _dot((ru * ru).astype(BF16), wd_ref[...])
        y2_ref[...] = y2
        x2_ref[...] = xv + mod_ref[M_GT_F:M_GT_F + 1, :] * ((y2 * _rms(y2)) * rows_ref[R_G_POST_MLP:R_G_POST_MLP + 1, :])

    tile = pl.BlockSpec((tm, d), lambda i: (i, 0))
    wide = pl.BlockSpec((tm, dff), lambda i: (i, 0))
    return pl.pallas_call(
        body,
        name="mlp_fwd",
        grid=(s // tm,),
        in_specs=[
            tile,
            _resident((None, N_MOD, d), lambda i: (l, 0, 0)),
            _resident((None, N_ROWS, d), lambda i: (l, 0, 0)),
            _resident((nb, d, cu), lambda i: (0, 0, 0)),
            _resident((dff, d), lambda i: (0, 0)),
            ANY_SPEC,
        ],
        out_specs=(tile, wide, tile, tile),
        out_shape=(jax.ShapeDtypeStruct((s, d), F32), jax.ShapeDtypeStruct((s, dff), BF16),
                   jax.ShapeDtypeStruct((s, d), F32), jax.ShapeDtypeStruct((s, d), BF16)),
        compiler_params=_cparams(("parallel",), 2 * _nbytes((dff, d), BF16) + 5 * _nbytes((tm, dff), F32) + 12 * _nbytes((tm, d), F32)),
    )(*_hbm(x1), mod, rows, wup_f, wdn_f, after)


def _loss_fwd_bwd(y, target, tm):
    s, d = y.shape

    def body(y_ref, t_ref, loss_ref, dy_ref):
        @pl.when(pl.program_id(0) == 0)
        def _():
            loss_ref[...] = jnp.zeros(loss_ref.shape, F32)

        e = y_ref[...] - t_ref[...]
        dy_ref[...] = e * (1.0 / d)
        loss_ref[...] += 0.5 * jnp.sum(jnp.mean(e * e, axis=-1, keepdims=True), axis=0, keepdims=True)

    tile = pl.BlockSpec((tm, d), lambda i: (i, 0))
    loss, dy = pl.pallas_call(
        body,
        name="loss",
        grid=(s // tm,),
        in_specs=[tile, tile],
        out_specs=(pl.BlockSpec((SUBLANES, 128), lambda i: (0, 0)), tile),
        out_shape=(jax.ShapeDtypeStruct((SUBLANES, 128), F32), jax.ShapeDtypeStruct((s, d), F32)),
        compiler_params=_cparams(("arbitrary",)),
    )(*_hbm(y, target))
    return loss[0, 0], dy


def _mlp_bwd(dx2, x1, y2, ru, mod, rows, wup_f, wdn_f, l, tm):
    s, d = x1.shape
    nb, _, cu = wup_f.shape
    dff = nb * cu

    def body(dx2_ref, x1_ref, y2_ref, ru_ref, mod_ref, rows_ref, wu_ref, wd_ref, dx1_ref, dy2_ref, dup_ref, act_ref, sm_ref):
        @pl.when(pl.program_id(0) == 0)
        def _():
            sm_ref[...] = jnp.zeros(sm_ref.shape, F32)

        dout = dx2_ref[...]
        dy2, dgt, dgpost = _postnorm_bwd(y2_ref[...], dout, rows_ref[R_G_POST_MLP:R_G_POST_MLP + 1, :], mod_ref[M_GT_F:M_GT_F + 1, :])
        dy2b = dy2.astype(BF16)
        dy2_ref[...] = dy2b
        ruv = ru_ref[...].astype(F32)
        act_ref[...] = (ruv * ruv).astype(BF16)
        dup = (_dot_nt(dy2b, wd_ref[...]) * (2.0 * ruv)).astype(BF16)
        dup_ref[...] = dup
        dh2 = _dot_nt(dup[:, 0:cu], wu_ref[0])
        for j in range(1, nb):
            dh2 = dh2 + _dot_nt(dup[:, j * cu:(j + 1) * cu], wu_ref[j])
        dxn, dsc, dsh, dgpre = _prenorm_bwd(x1_ref[...], dh2, rows_ref[R_G_PRE_MLP:R_G_PRE_MLP + 1, :], mod_ref[M_SC_F:M_SC_F + 1, :])
        dx1_ref[...] = dout + dxn
        for r, v in ((G_MLP_GT, dgt), (G_MLP_GPOST, dgpost), (G_MLP_SC, dsc), (G_MLP_SH, dsh), (G_MLP_GPRE, dgpre)):
            sm_ref[r:r + 1, :] += v

    tile = pl.BlockSpec((tm, d), lambda i: (i, 0))
    wide = pl.BlockSpec((tm, dff), lambda i: (i, 0))
    return pl.pallas_call(
        body,
        name="mlp_bwd",
        grid=(s // tm,),
        in_specs=[
            tile, tile, tile, wide,
            _resident((None, N_MOD, d), lambda i: (l, 0, 0)),
            _resident((None, N_ROWS, d), lambda i: (l, 0, 0)),
            _resident((nb, d, cu), lambda i: (0, 0, 0)),
            _resident((dff, d), lambda i: (0, 0)),
        ],
        out_specs=(tile, tile, wide, wide, pl.BlockSpec((G_MLP_ROWS, d), lambda i: (0, 0))),
        out_shape=(jax.ShapeDtypeStruct((s, d), F32), jax.ShapeDtypeStruct((s, d), BF16), jax.ShapeDtypeStruct((s, dff), BF16),
                   jax.ShapeDtypeStruct((s, dff), BF16), jax.ShapeDtypeStruct((G_MLP_ROWS, d), F32)),
        compiler_params=_cparams(("arbitrary",), 2 * _nbytes((dff, d), BF16) + 6 * _nbytes((tm, dff), F32) + 16 * _nbytes((tm, d), F32)),
    )(*_hbm(dx2, x1, y2, ru), mod, rows, wup_f, wdn_f)


def _mixer_core_bwd(dx1, y, yap, ybp, hs, proj, mod, rows, cwf, wr, wi, wa_f, wb_f, wo_f, l, tm, after):
    s, d = dx1.shape
    nh, bw, _ = wr.shape[1:]
    nt = s // tm
    per = tm // SUBLANES

    def body(dx1_ref, y_ref, yap_ref, ybp_ref, hs_ref, hsh_ref, proj_ref, projh_ref, mod_ref, rows_ref, cw_ref,
             wr_ref, wi_ref, wa_ref, wb_ref, wo_ref, after_ref,
             dproj_ref, dy_ref, m_ref, dyap_ref, dybp_ref, ya_ref, yb_ref, sm_ref, dwg_ref,
             abuf, dcbuf, dxbuf, al_s, dh_s, lam_s, lnext):
        i = pl.program_id(0)
        first_tile = i == nt - 1

        @pl.when(i == 0)
        def _():
            sm_ref[...] = jnp.zeros(sm_ref.shape, F32)
            dwg_ref[...] = jnp.zeros(dwg_ref.shape, F32)
            zero = jnp.zeros((SUBLANES, d), F32)
            abuf[...] = zero
            dcbuf[...] = zero
            dxbuf[...] = zero
            lnext[...] = zero

        def row(r):
            return rows_ref[r:r + 1, :]

        def tap(r):
            return cw_ref[r:r + 1, :]

        def acc(r, v):
            sm_ref[r:r + 1, :] += v

        keep_halo = jnp.where(first_tile, 0.0, 1.0)
        dy, dgt, dgpost = _postnorm_bwd(y_ref[...], dx1_ref[...], row(R_G_POST_MIX), mod_ref[M_GT_M:M_GT_M + 1, :])
        acc(G_MIX_GT, dgt)
        acc(G_MIX_GPOST, dgpost)
        dyb16 = dy.astype(BF16)
        dy_ref[...] = dyb16
        dm = _dot_nt(dyb16, wo_ref[...])
        sa = _sigmoid(proj_ref[:, 5 * d:6 * d])
        sb = _sigmoid(proj_ref[:, 6 * d:7 * d])
        yap = yap_ref[...]
        ybp = ybp_ref[...]
        m_ref[...] = (sa * yap + sb * ybp).astype(BF16)
        dyap_f = dm * sa
        dybp_f = dm * sb
        dyap = dyap_f.astype(BF16)
        dybp = dybp_f.astype(BF16)
        dyap_ref[...] = dyap
        dybp_ref[...] = dybp
        dproj_ref[:, 5 * d:6 * d] = (dyap_f * yap * (1.0 - sa)).astype(BF16)
        dproj_ref[:, 6 * d:7 * d] = (dybp_f * ybp * (1.0 - sb)).astype(BF16)
        dya = _dot_nt(dyap, wa_ref[...])
        dyb = _dot_nt(dybp, wb_ref[...])
        ba = proj_ref[:, 0:d]
        ca = proj_ref[:, d:2 * d]
        va = proj_ref[:, 2 * d:3 * d]
        cv = ca * va
        cvh = keep_halo * (projh_ref[:, d:2 * d] * projh_ref[:, 2 * d:3 * d])
        cvm2 = _shift_rows(cv, cvh, 2, False)
        cvm1 = _shift_rows(cv, cvh, 1, False)
        conv3 = ((row(R_CONV_A_B) + cvm2 * tap(CW_A)) + cvm1 * tap(CW_A + 1)) + cv * tap(CW_A + 2)
        ya_ref[...] = (ba * conv3).astype(BF16)
        dproj_ref[:, 0:d] = (dya * conv3).astype(BF16)
        dc3 = dya * ba
        acc(G_MIX_CAB, _colsum(dc3))
        acc(G_MIX_CAW, _colsum(dc3 * cvm2))
        acc(G_MIX_CAW + 1, _colsum(dc3 * cvm1))
        acc(G_MIX_CAW + 2, _colsum(dc3 * cv))
        dct = dcbuf[...]
        dcv = (dc3 * tap(CW_A + 2) + _shift_rows(dc3, dct, 1, True) * tap(CW_A + 1)) + _shift_rows(dc3, dct, 2, True) * tap(CW_A)
        dcbuf[...] = dc3[:SUBLANES, :]
        dproj_ref[:, d:2 * d] = (dcv * va).astype(BF16)
        dproj_ref[:, 2 * d:3 * d] = (dcv * ca).astype(BF16)
        xb = proj_ref[:, 3 * d:4 * d]
        gb = proj_ref[:, 4 * d:5 * d]
        xbh = keep_halo * projh_ref[:, 3 * d:4 * d]
        xm3 = _shift_rows(xb, xbh, 3, False)
        xm2 = _shift_rows(xb, xbh, 2, False)
        xm1 = _shift_rows(xb, xbh, 1, False)
        xc = (((row(R_CONV_B_B) + xm3 * tap(CW_B)) + xm2 * tap(CW_B + 1)) + xm1 * tap(CW_B + 2)) + xb * tap(CW_B + 3)
        lam = row(R_LAMBDA)
        sp = _softplus_neg(lam)
        xcb, r, ig, a, mult = _gates(xc, wr_ref, wi_ref, row(R_B_GATE_R), row(R_B_GATE_I), sp, nh, bw)
        gel, gsig = _gelu(gb)
        hs = hs_ref[...]
        yb_ref[...] = (hs * gel).astype(BF16)
        dproj_ref[:, 4 * d:5 * d] = (dyb * hs * _gelu_grad(gb, gsig)).astype(BF16)
        al_s[...] = _shift_rows(a, abuf[...], 1, True)
        abuf[...] = a[:SUBLANES, :]
        dh_s[...] = dyb * gel

        def blk(j, ln):
            o = pl.multiple_of((per - 1 - j) * SUBLANES, SUBLANES)
            a8, b8 = _scan_block(al_s[pl.ds(o, SUBLANES), :], dh_s[pl.ds(o, SUBLANES), :], reverse=True)
            l8 = b8 + a8 * ln
            lam_s[pl.ds(o, SUBLANES), :] = l8
            return jnp.broadcast_to(l8[0:1, :], (SUBLANES, d))

        lnext[...] = lax.fori_loop(0, per, blk, lnext[...])
        dbb = lam_s[...]
        da = dbb * _shift_rows(hs, keep_halo * hsh_ref[...], 1, False)
        dbx = dbb * xc
        dmult = dbx * ig
        dig = dbx * mult
        dxc = (dbb * mult) * ig
        dla = a * (da - (dmult * a) / mult)
        dlar = dla * r
        acc(G_MIX_LAM, _colsum(dlar) * (LRU_C * _sigmoid(-lam)))
        dzr = (dlar * (1.0 - r)) * (-LRU_C * sp)
        dzi = dig * ig * (1.0 - ig)
        acc(G_MIX_BR, _colsum(dzr))
        acc(G_MIX_BI, _colsum(dzi))
        dzrb = dzr.astype(BF16)
        dzib = dzi.astype(BF16)
        back = []
        for h in range(nh):
            sl = slice(h * bw, (h + 1) * bw)
            back.append(_dot_nt(dzrb[:, sl], wr_ref[h]) + _dot_nt(dzib[:, sl], wi_ref[h]))
            dwg_ref[0, h] += _dot_tn(xcb[:, sl], dzrb[:, sl])
            dwg_ref[1, h] += _dot_tn(xcb[:, sl], dzib[:, sl])
        dxc = dxc + jnp.concatenate(back, axis=1)
        acc(G_MIX_CBB, _colsum(dxc))
        acc(G_MIX_CBW, _colsum(dxc * xm3))
        acc(G_MIX_CBW + 1, _colsum(dxc * xm2))
        acc(G_MIX_CBW + 2, _colsum(dxc * xm1))
        acc(G_MIX_CBW + 3, _colsum(dxc * xb))
        dxt = dxbuf[...]
        dxb = (((dxc * tap(CW_B + 3) + _shift_rows(dxc, dxt, 1, True) * tap(CW_B + 2)) + _shift_rows(dxc, dxt, 2, True) * tap(CW_B + 1))
               + _shift_rows(dxc, dxt, 3, True) * tap(CW_B))
        dxbuf[...] = dxc[:SUBLANES, :]
        dproj_ref[:, 3 * d:4 * d] = dxb.astype(BF16)

    def rev(i):
        return (nt - 1 - i, 0)

    def halo(i):
        return (jnp.maximum((nt - 1 - i) * per - 1, 0), 0)

    tile = pl.BlockSpec((tm, d), rev)
    return pl.pallas_call(
        body,
        name="mixer_core_bwd",
        grid=(nt,),
        in_specs=[
            tile, tile, tile, tile, tile,
            pl.BlockSpec((SUBLANES, d), halo),
            pl.BlockSpec((tm, 7 * d), rev),
            pl.BlockSpec((SUBLANES, 7 * d), halo),
            _resident((None, N_MOD, d), lambda i: (l, 0, 0)),
            _resident((None, N_ROWS, d), lambda i: (l, 0, 0)),
            _resident((None, CW_ROWS, d), lambda i: (l, 0, 0)),
            _resident((None, nh, bw, bw), lambda i: (l, 0, 0, 0)),
            _resident((None, nh, bw, bw), lambda i: (l, 0, 0, 0)),
            _resident((d, d), lambda i: (0, 0)),
            _resident((d, d), lambda i: (0, 0)),
            _resident((d, d), lambda i: (0, 0)),
            ANY_SPEC,
        ],
        out_specs=(pl.BlockSpec((tm, 7 * d), rev),) + (tile,) * 6 + (
            pl.BlockSpec((G_MIX_ROWS, d), lambda i: (0, 0)), pl.BlockSpec((2, nh, bw, bw), lambda i: (0, 0, 0, 0))),
        out_shape=(jax.ShapeDtypeStruct((s, 7 * d), BF16),) + (jax.ShapeDtypeStruct((s, d), BF16),) * 6 + (
            jax.ShapeDtypeStruct((G_MIX_ROWS, d), F32), jax.ShapeDtypeStruct((2, nh, bw, bw), F32)),
        scratch_shapes=[pltpu.VMEM((SUBLANES, d), F32)] * 3 + [pltpu.VMEM((tm, d), F32)] * 3 + [pltpu.VMEM((SUBLANES, d), F32)],
        compiler_params=_cparams(("arbitrary",), 3 * _nbytes((d, d), BF16) + 3 * _nbytes((tm, 7 * d), F32) + 64 * _nbytes((tm, d), F32)),
    )(*_hbm(dx1, y, yap, ybp, hs, hs, proj, proj), mod, rows, cwf, wr, wi, wa_f, wb_f, wo_f, after)


def _in_proj_bwd(dproj, x, dx1, mod, rows, win_f, l, tm, after):
    s, d = x.shape
    nb, _, ci = win_f.shape

    def body(dp_ref, x_ref, dx1_ref, mod_ref, rows_ref, w_ref, after_ref, dx_ref, sm_ref):
        @pl.when(pl.program_id(0) == 0)
        def _():
            sm_ref[...] = jnp.zeros(sm_ref.shape, F32)

        dh = _dot_nt(dp_ref[:, 0:ci], w_ref[0])
        for j in range(1, nb):
            dh = dh + _dot_nt(dp_ref[:, j * ci:(j + 1) * ci], w_ref[j])
        dxn, dsc, dsh, dg = _prenorm_bwd(x_ref[...], dh, rows_ref[R_G_PRE_MIX:R_G_PRE_MIX + 1, :], mod_ref[M_SC_M:M_SC_M + 1, :])
        dx_ref[...] = dx1_ref[...] + dxn
        for r, v in ((G_IN_SC, dsc), (G_IN_SH, dsh), (G_IN_GPRE, dg)):
            sm_ref[r:r + 1, :] += v

    tile = pl.BlockSpec((tm, d), lambda i: (i, 0))
    return pl.pallas_call(
        body,
        name="in_proj_bwd",
        grid=(s // tm,),
        in_specs=[
            pl.BlockSpec((tm, nb * ci), lambda i: (i, 0)), tile, tile,
            _resident((None, N_MOD, d), lambda i: (l, 0, 0)),
            _resident((None, N_ROWS, d), lambda i: (l, 0, 0)),
            _resident((nb, d, ci), lambda i: (0, 0, 0)),
            ANY_SPEC,
        ],
        out_specs=(tile, pl.BlockSpec((G_IN_ROWS, d), lambda i: (0, 0))),
        out_shape=(jax.ShapeDtypeStruct((s, d), F32), jax.ShapeDtypeStruct((G_IN_ROWS, d), F32)),
        compiler_params=_cparams(("arbitrary",), _nbytes((nb, d, ci), BF16) + 2 * _nbytes((tm, nb * ci), BF16) + 16 * _nbytes((tm, d), F32)),
    )(*_hbm(dproj, x, dx1), mod, rows, win_f, after)


def _wgrad(a, b, cols_owned, ts, after):
    s, k1 = a.shape
    k2 = b.shape[1]
    ns = s // ts
    if cols_owned:
        nblk, bk1, bk2 = N_DEV, k1, k2 // N_DEV
        a_spec = pl.BlockSpec((ts, bk1), lambda j, t: (t, 0))
        b_spec = pl.BlockSpec((ts, bk2), lambda j, t: (t, j))
    else:
        bk1, bk2 = min(WGRAD_ROWS, k1), k2
        nblk = k1 // bk1
        a_spec = pl.BlockSpec((ts, bk1), lambda j, t: (t, j))
        b_spec = pl.BlockSpec((ts, bk2), lambda j, t: (t, 0))

    def body(a_ref, b_ref, after_ref, o_ref, acc_ref):
        t = pl.program_id(1)

        @pl.when(t == 0)
        def _():
            acc_ref[...] = jnp.zeros(acc_ref.shape, F32)

        acc_ref[...] += _dot_tn(a_ref[...], b_ref[...])

        @pl.when(t == ns - 1)
        def _():
            o_ref[...] = acc_ref[...].astype(BF16)

    out = pl.pallas_call(
        body,
        name="wgrad",
        grid=(nblk, ns),
        in_specs=[a_spec, b_spec, ANY_SPEC],
        out_specs=pl.BlockSpec((None, bk1, bk2), lambda j, t: (j, 0, 0)),
        out_shape=pltpu.HBM((nblk, bk1, bk2), BF16),
        scratch_shapes=[pltpu.VMEM((bk1, bk2), F32)],
        compiler_params=_cparams(("parallel", "arbitrary"), 4 * _nbytes((bk1, bk2), F32) + 4 * _nbytes((ts, bk1 + bk2), BF16)),
    )(pltpu.with_memory_space_constraint(a, pltpu.HBM), pltpu.with_memory_space_constraint(b, pltpu.HBM), after)
    return out if cols_owned else out.reshape(N_DEV, k1 // N_DEV, k2)


def _adam_update(w, g, m, v):
    m = ADAM_B1 * m + (1.0 - ADAM_B1) * g
    v = ADAM_B2 * v + (1.0 - ADAM_B2) * (g * g)
    m_hat = m / (1.0 - ADAM_B1 ** ADAM_STEP)
    v_hat = v / (1.0 - ADAM_B2 ** ADAM_STEP)
    delta = -ADAM_LR * (m_hat / (jnp.sqrt(v_hat) + ADAM_EPS) + ADAM_WD * w)
    return delta, m, v


def _sum_adamw(recv, w, m, v, tr, after):
    nl, ra, cb = w.shape
    assert nl == len(recv) == 2

    def body(r0_ref, r1_ref, w_ref, m_ref, v_ref, after_ref, g_ref, d_ref, nm_ref, nv_ref):
        def total(r_ref):
            g = r_ref[0].astype(F32)
            for j in range(1, N_DEV):
                g = g + r_ref[j].astype(F32)
            return g

        g = jnp.where(pl.program_id(0) == 0, total(r0_ref), total(r1_ref))
        g_ref[...] = g
        d_ref[...], nm_ref[...], nv_ref[...] = _adam_update(w_ref[...], g, m_ref[...], v_ref[...])

    blk = pl.BlockSpec((None, tr, cb), lambda l, i: (l, i, 0))
    return pl.pallas_call(
        body,
        name="sum_adamw",
        grid=(nl, ra // tr),
        in_specs=[pl.BlockSpec((N_DEV, tr, cb), lambda l, i: (0, i * (1 - l), 0)),
                  pl.BlockSpec((N_DEV, tr, cb), lambda l, i: (0, i * l, 0)), blk, blk, blk, ANY_SPEC],
        out_specs=(blk,) * 4,
        out_shape=(jax.ShapeDtypeStruct((nl, ra, cb), F32),) * 4,
        compiler_params=_cparams(("arbitrary", "arbitrary"), 6 * _nbytes((N_DEV, tr, cb), BF16) + 32 * _nbytes((tr, cb), F32)),
    )(recv[0], recv[1], w, m, v, after)


def _adamw(w, g, m, v):
    def body(w_ref, g_ref, m_ref, v_ref, d_ref, nm_ref, nv_ref):
        d_ref[...], nm_ref[...], nv_ref[...] = _adam_update(w_ref[...], g_ref[...], m_ref[...], v_ref[...])

    return pl.pallas_call(
        body,
        name="adamw",
        in_specs=[VMEM_SPEC] * 4,
        out_specs=(VMEM_SPEC,) * 3,
        out_shape=(jax.ShapeDtypeStruct(w.shape, F32),) * 3,
        compiler_params=_cparams(vmem=10 * _nbytes(w.shape, F32)),
    )(w, g, m, v)


def _adamw_tiled(w, g, m, v, tr):
    nl, ra, cb = w.shape

    def body(w_ref, g_ref, m_ref, v_ref, d_ref, nm_ref, nv_ref):
        d_ref[...], nm_ref[...], nv_ref[...] = _adam_update(w_ref[...], g_ref[...], m_ref[...], v_ref[...])

    blk = pl.BlockSpec((None, tr, cb), lambda l, i: (l, i, 0))
    return pl.pallas_call(
        body,
        name="adamw_tiled",
        grid=(nl, ra // tr),
        in_specs=[blk] * 4,
        out_specs=(blk,) * 3,
        out_shape=(jax.ShapeDtypeStruct((nl, ra, cb), F32),) * 3,
        compiler_params=_cparams(("parallel", "parallel")),
    )(*_hbm(w, g, m, v))


def _token_tile(s):
    return min(256, max(SUBLANES * 2, s // 4))


def kernel(x, c, w_mod, b_mod, g_pre_mix, g_post_mix, w_in, conv_a_w, conv_a_b, w_a_out, conv_b_w, conv_b_b, w_gate_r, b_gate_r, w_gate_i, b_gate_i, lru_lambda, w_b_out, w_o, g_pre_mlp, g_post_mlp, w_mlp_up, w_mlp_down, loss_target, m_w_mod, m_b_mod, m_g_pre_mix, m_g_post_mix, m_w_in, m_conv_a_w, m_conv_a_b, m_w_a_out, m_conv_b_w, m_conv_b_b, m_w_gate_r, m_b_gate_r, m_w_gate_i, m_b_gate_i, m_lru_lambda, m_w_b_out, m_w_o, m_g_pre_mlp, m_g_post_mlp, m_w_mlp_up, m_w_mlp_down, v_w_mod, v_b_mod, v_g_pre_mix, v_g_post_mix, v_w_in, v_conv_a_w, v_conv_a_b, v_w_a_out, v_conv_b_w, v_conv_b_b, v_w_gate_r, v_b_gate_r, v_w_gate_i, v_b_gate_i, v_lru_lambda, v_w_b_out, v_w_o, v_g_pre_mlp, v_g_post_mlp, v_w_mlp_up, v_w_mlp_down):
    nl = w_mod.shape[0]
    s, d = x.shape[1], x.shape[2]
    nh, bw = w_gate_r.shape[1], w_gate_r.shape[2]
    cwid = conv_a_w.shape[2]
    tm = _token_tile(s)
    tmx = min(2 * tm, s)
    ts = s
    _, me = _my_position()
    xs = x.reshape(s, d)
    target = loss_target.reshape(s, d)

    vec_names = (g_pre_mix, g_post_mix, conv_a_b, conv_b_b, b_gate_r, b_gate_i, lru_lambda, g_pre_mlp, g_post_mlp)
    rows = jnp.concatenate([jnp.stack(vec_names, axis=1), jnp.zeros((nl, N_ROWS - len(vec_names), d), F32)], axis=1)
    cw = jnp.concatenate([conv_a_w, conv_b_w, jnp.zeros((nl, CW_ROWS - 7, cwid), F32)], axis=1)

    large = {"w_in": w_in, "w_a_out": w_a_out, "w_b_out": w_b_out, "w_o": w_o, "w_mlp_up": w_mlp_up, "w_mlp_down": w_mlp_down}
    groups = (("in", ("w_in",)), ("mix", ("w_a_out", "w_b_out", "w_o")), ("mlp", ("w_mlp_up", "w_mlp_down")))
    mod, cact, cwf = _prep_small(c, w_mod, b_mod, cw)
    mod = mod.reshape(nl, N_MOD, d)
    gathers = {}
    tok = mod
    for l in range(nl):
        for gname, members in groups:
            gathers[l, gname] = _gather2_start([large[n][l].astype(BF16) for n in members], tok, f"gather_start_{gname}{l}")
            tok = gathers[l, gname][4]
    wr = w_gate_r.astype(BF16)
    wi = w_gate_i.astype(BF16)

    forwarded = {}

    def forward(l, gname, after):
        forwarded[l, gname] = _gather2_forward(gathers[l, gname], after, f"gather_forward_{gname}{l}")
        return forwarded[l, gname][4]

    def gathered(l, gname, after):
        return _gather2_wait(forwarded[l, gname], after, f"gather_wait_{gname}{l}")

    saved = []
    weights = []
    xin = xs
    for l in range(nl):
        if l == 0:
            tok = forward(0, "in", tok)
        (win_f,) = gathered(l, "in", tok if l == 0 else xin)
        if l > 0:
            tok = forward(l, "mix", win_f)
        proj, h = _in_proj_fwd(xin, mod, rows, win_f, l, tmx, tok)
        if l == 0:
            tok = forward(0, "mix", proj)
        wa_f, wb_f, wo_f = (w.reshape(d, d) for w in gathered(l, "mix", proj))
        if l > 0:
            tok = forward(l, "mlp", wo_f)
        x1, hs, yap, ybp, y = _mixer_core_fwd(proj, xin, mod, rows, cwf, wr, wi, wa_f, wb_f, wo_f, l, tm, tok)
        if l == 0:
            forward(0, "mlp", x1)
        wup_f, wdn_f = gathered(l, "mlp", x1)
        wdn_f = wdn_f.reshape(-1, d)
        if l + 1 < nl:
            tok = forward(l + 1, "in", wdn_f)
        x2, ru, y2, h2 = _mlp_fwd(x1, mod, rows, wup_f, wdn_f, l, tmx, tok)
        saved.append((xin, proj, h, x1, hs, yap, ybp, y, ru, y2, h2))
        weights.append((win_f, wa_f, wb_f, wo_f, wup_f, wdn_f))
        xin = x2
    loss_part, dx = _loss_fwd_bwd(xin, target, tm)

    scatters = {}
    small = [None] * nl
    gate_parts = [None] * nl

    def scatter(l, gname, parts):
        scatters[l, gname] = _exchange_start(parts, False, f"scatter_start_{gname}{l}")
        return scatters[l, gname][4]

    for l in reversed(range(nl)):
        xin, proj, h, x1, hs, yap, ybp, y, ru, y2, h2 = saved[l]
        win_f, wa_f, wb_f, wo_f, wup_f, wdn_f = weights[l]
        dx1, dy2, dup, act, sm_mlp = _mlp_bwd(dx, x1, y2, ru, mod, rows, wup_f, wdn_f, l, tm)
        g_up = _wgrad(h2, dup, True, ts, dx1)
        g_dn = _wgrad(act, dy2, False, ts, g_up)
        tok = scatter(l, "mlp", [g_up, g_dn])
        dproj, dy, m, dyap, dybp, ya, yb, sm_mix, dwg = _mixer_core_bwd(
            dx1, y, yap, ybp, hs, proj, mod, rows, cwf, wr, wi, wa_f, wb_f, wo_f, l, tm // 2, tok)
        gate_parts[l] = _exchange_start([dwg.astype(BF16)], True, f"gates_start{l}")
        g_a = _wgrad(ya, dyap, False, ts, gate_parts[l][4])
        g_b = _wgrad(yb, dybp, False, ts, g_a)
        g_o = _wgrad(m, dy, False, ts, g_b)
        tok = scatter(l, "mix", [g_a, g_b, g_o])
        tok = scatter(l, "in", [_wgrad(h, dproj, True, ts, tok)])
        dx, sm_in = _in_proj_bwd(dproj, xin, dx1, mod, rows, win_f, l, tm, tok)
        small[l] = jnp.concatenate([sm_mlp, sm_mix, sm_in], axis=0)
    grad_x = dx.reshape(x.shape)

    recv = {}
    big = {}
    moments = {"w_in": (w_in, m_w_in, v_w_in), "w_mlp_up": (w_mlp_up, m_w_mlp_up, v_w_mlp_up), "w_a_out": (w_a_out, m_w_a_out, v_w_a_out),
               "w_b_out": (w_b_out, m_w_b_out, v_w_b_out), "w_o": (w_o, m_w_o, v_w_o), "w_mlp_down": (w_mlp_down, m_w_mlp_down, v_w_mlp_down)}

    def collect(l, gname, after):
        for n, zone in zip(dict(groups)[gname], _exchange_wait(scatters[l, gname], after, False, f"scatter_wait_{gname}{l}")):
            recv[n, l] = zone

    def update(name, after):
        w, m_, v_ = moments[name]
        big[name] = _sum_adamw([recv[name, l] for l in range(nl)], w, m_, v_, min(256, w.shape[1]), after)
        return big[name][1]

    for l, gname in ((1, "mlp"), (1, "in"), (1, "mix"), (0, "mlp")):
        collect(l, gname, dx)
    done = update("w_mlp_down", update("w_mlp_up", dx))
    collect(0, "mix", done)
    for name in ("w_a_out", "w_b_out", "w_o"):
        done = update(name, done)
    collect(0, "in", done)
    done = update("w_in", done)

    lrows = jnp.concatenate(small, axis=0)
    lrows = lrows.at[G_LOSS_ROW, 0].set(loss_part)

    def lrow(a, l, r):
        return a[l * G_LAYER_ROWS + r]

    dm = jnp.stack([jnp.concatenate([lrow(lrows, l, G_MLP_ROWS + G_MIX_ROWS + G_IN_SH), lrow(lrows, l, G_MLP_ROWS + G_MIX_ROWS + G_IN_SC),
                                     lrow(lrows, l, G_MLP_ROWS + G_MIX_GT), lrow(lrows, l, G_MLP_SH), lrow(lrows, l, G_MLP_SC),
                                     lrow(lrows, l, G_MLP_GT)]) for l in range(nl)])
    dm8 = jnp.concatenate([dm[:, None, :], jnp.zeros((nl, SUBLANES - 1, N_MOD * d), F32)], axis=1)
    srows, g_w_mod = _reduce_small(lrows, dm8, cact, done)
    loss = srows[G_LOSS_ROW, 0]
    sgates = _sum_gathered([_exchange_wait(gate_parts[l], srows, True, f"gates_wait{l}")[0] for l in range(nl)])

    def srow(l, r):
        return lrow(srows, l, r)

    def per_layer(r):
        return jnp.stack([srow(l, r) for l in range(nl)])

    mix0 = G_MLP_ROWS
    in0 = G_MLP_ROWS + G_MIX_ROWS
    g_b_mod = jnp.stack([jnp.concatenate([srow(l, in0 + G_IN_SH), srow(l, in0 + G_IN_SC), srow(l, mix0 + G_MIX_GT),
                                          srow(l, G_MLP_SH), srow(l, G_MLP_SC), srow(l, G_MLP_GT)]) for l in range(nl)])
    conv_a_full = jnp.stack([jnp.stack([srow(l, mix0 + G_MIX_CAW + k) for k in range(3)]) for l in range(nl)])
    conv_b_full = jnp.stack([jnp.stack([srow(l, mix0 + G_MIX_CBW + k) for k in range(4)]) for l in range(nl)])
    grads = {
        "b_mod": g_b_mod,
        "g_pre_mix": per_layer(in0 + G_IN_GPRE),
        "g_post_mix": per_layer(mix0 + G_MIX_GPOST),
        "conv_a_w": lax.dynamic_slice_in_dim(conv_a_full, me * cwid, cwid, axis=2),
        "conv_a_b": per_layer(mix0 + G_MIX_CAB),
        "conv_b_w": lax.dynamic_slice_in_dim(conv_b_full, me * cwid, cwid, axis=2),
        "conv_b_b": per_layer(mix0 + G_MIX_CBB),
        "w_gate_r": sgates[:, 0],
        "b_gate_r": per_layer(mix0 + G_MIX_BR),
        "w_gate_i": sgates[:, 1],
        "b_gate_i": per_layer(mix0 + G_MIX_BI),
        "lru_lambda": per_layer(mix0 + G_MIX_LAM),
        "g_pre_mlp": per_layer(G_MLP_GPRE),
        "g_post_mlp": per_layer(G_MLP_GPOST),
    }
    params = {
        "b_mod": (b_mod, m_b_mod, v_b_mod), "g_pre_mix": (g_pre_mix, m_g_pre_mix, v_g_pre_mix), "g_post_mix": (g_post_mix, m_g_post_mix, v_g_post_mix),
        "conv_a_w": (conv_a_w, m_conv_a_w, v_conv_a_w), "conv_a_b": (conv_a_b, m_conv_a_b, v_conv_a_b),
        "conv_b_w": (conv_b_w, m_conv_b_w, v_conv_b_w), "conv_b_b": (conv_b_b, m_conv_b_b, v_conv_b_b),
        "w_gate_r": (w_gate_r, m_w_gate_r, v_w_gate_r), "b_gate_r": (b_gate_r, m_b_gate_r, v_b_gate_r),
        "w_gate_i": (w_gate_i, m_w_gate_i, v_w_gate_i), "b_gate_i": (b_gate_i, m_b_gate_i, v_b_gate_i),
        "lru_lambda": (lru_lambda, m_lru_lambda, v_lru_lambda), "g_pre_mlp": (g_pre_mlp, m_g_pre_mlp, v_g_pre_mlp),
        "g_post_mlp": (g_post_mlp, m_g_post_mlp, v_g_post_mlp),
    }
    out = {}
    for name, g in grads.items():
        w, m_, v_ = params[name]
        flat = (-1, w.shape[-1])
        dl, nm, nv = _adamw(w.reshape(flat), g.reshape(flat), m_.reshape(flat), v_.reshape(flat))
        out[name] = (g.reshape(w.shape), dl.reshape(w.shape), nm.reshape(w.shape), nv.reshape(w.shape))
    out["w_mod"] = (g_w_mod,) + tuple(_adamw_tiled(w_mod, g_w_mod, m_w_mod, v_w_mod, min(128, d)))
    out.update(big)

    order = ("w_mod", "b_mod", "g_pre_mix", "g_post_mix", "w_in", "conv_a_w", "conv_a_b", "w_a_out", "conv_b_w", "conv_b_b", "w_gate_r", "b_gate_r",
             "w_gate_i", "b_gate_i", "lru_lambda", "w_b_out", "w_o", "g_pre_mlp", "g_post_mlp", "w_mlp_up", "w_mlp_down")
    return (loss, grad_x) + tuple(out[n][0] for n in order) + tuple(out[n][1] for n in order) + tuple(out[n][2] for n in order) + tuple(out[n][3] for n in order)
```

```python
import functools

import jax
import jax.numpy as jnp
from jax import lax
from jax.experimental import pallas as pl
from jax.experimental.pallas import tpu as pltpu

F32, BF16 = jnp.float32, jnp.bfloat16
EPS = 1e-6
LRU_C = 8.0
N_DEV = 8
N_MOD = 6
SUBLANES = 8
VMEM_BUDGET = 56 * 1024 * 1024
WGRAD_ROWS = 512
ADAM_LR, ADAM_B1, ADAM_B2, ADAM_EPS, ADAM_WD, ADAM_STEP = 0.001, 0.9, 0.999, 1e-08, 0.01, 10
MESH = pl.DeviceIdType.MESH
VMEM_SPEC = pl.BlockSpec(memory_space=pltpu.VMEM)
ANY_SPEC = pl.BlockSpec(memory_space=pl.ANY)
HBM_SPEC = pl.BlockSpec(memory_space=pltpu.HBM)
SEM_SPEC = pl.BlockSpec(memory_space=pltpu.SEMAPHORE)
SIDE_EFFECT = pltpu.SideEffectType.DATAFLOW_SIDE_EFFECTING

R_G_PRE_MIX, R_G_POST_MIX, R_CONV_A_B, R_CONV_B_B, R_B_GATE_R, R_B_GATE_I, R_LAMBDA, R_G_PRE_MLP, R_G_POST_MLP = range(9)
N_ROWS = 16
M_SH_M, M_SC_M, M_GT_M, M_SH_F, M_SC_F, M_GT_F = range(6)
CW_A, CW_B, CW_ROWS = 0, 3, 8
G_MLP_GT, G_MLP_GPOST, G_MLP_SC, G_MLP_SH, G_MLP_GPRE, G_LOSS_ROW, G_MLP_ROWS = 0, 1, 2, 3, 4, 7, 8
(G_MIX_GT, G_MIX_GPOST, G_MIX_CAB, G_MIX_CAW, G_MIX_CBB, G_MIX_CBW, G_MIX_BR, G_MIX_BI, G_MIX_LAM) = 0, 1, 2, 3, 6, 7, 11, 12, 13
G_MIX_ROWS = 16
G_IN_SC, G_IN_SH, G_IN_GPRE, G_IN_ROWS = 0, 1, 2, 8
G_LAYER_ROWS = G_MLP_ROWS + G_MIX_ROWS + G_IN_ROWS


def _cparams(dims=None, vmem=None):
    kw = {}
    if dims is not None:
        kw["dimension_semantics"] = dims
    if vmem is not None:
        kw["vmem_limit_bytes"] = int(min(max(vmem, 16 * 1024 * 1024), VMEM_BUDGET))
    return pltpu.CompilerParams(**kw)


def _nbytes(shape, dtype):
    n = 1
    for s in shape:
        n *= s
    return n * jnp.dtype(dtype).itemsize


def _hbm(*arrays):
    return tuple(pltpu.with_memory_space_constraint(a, pltpu.HBM) for a in arrays)


def _resident(block, index_map):
    return pl.BlockSpec(block, index_map, pipeline_mode=pl.Buffered(1))


def _my_position():
    x, y, c = lax.axis_index("x"), lax.axis_index("y"), lax.axis_index("c")
    return (x, y, c), 4 * x + 2 * y + c


def _peer(pos, k):
    x, y, c = pos
    px = 1 - x if k & 4 else x
    py = 1 - y if k & 2 else y
    pc = 1 - c if k & 1 else c
    return (px, py, pc), 4 * px + 2 * py + pc


def _remote(src, dst, ssem, rsem, peer):
    return pltpu.make_async_remote_copy(src_ref=src, dst_ref=dst, send_sem=ssem, recv_sem=rsem, device_id=peer, device_id_type=MESH)


def _dot(a, b):
    return jnp.dot(a, b, preferred_element_type=F32)


def _dot_nt(a, b):
    return lax.dot_general(a, b, (((1,), (1,)), ((), ())), preferred_element_type=F32)


def _dot_tn(a, b):
    return lax.dot_general(a, b, (((0,), (0,)), ((), ())), preferred_element_type=F32)


def _colsum(v):
    return jnp.sum(v, axis=0, keepdims=True)


def _sigmoid(v):
    return jax.nn.sigmoid(v)


GELU_K, GELU_C = 0.7978845608028654, 0.044715


def _gelu(v):
    s = 1.0 / (1.0 + jnp.exp(v * (-2.0 * GELU_K - (2.0 * GELU_K * GELU_C) * (v * v))))
    return v * s, s


def _gelu_grad(v, s):
    return s * (1.0 + (v * (1.0 - s)) * (2.0 * GELU_K + (6.0 * GELU_K * GELU_C) * (v * v)))


def _neg_expm1_twice(v):
    t = jnp.tanh(v)
    return (-2.0 * t) / (1.0 - t)


def _softplus_neg(lam):
    z = -lam
    u = jnp.exp(-jnp.abs(z))
    w = 1.0 + u
    l1p = jnp.where(w == 1.0, u, jnp.log(w) * u / (w - 1.0))
    return jnp.maximum(z, 0.0) + l1p


def _rms(v):
    return lax.rsqrt(jnp.mean(v * v, axis=-1, keepdims=True) + EPS)


def _prenorm_bwd(xv, dh, g, sc):
    r = _rms(xv)
    xn = xv * r
    n = xn * g
    dsc = _colsum(dh * n)
    dsh = _colsum(dh)
    dn = dh * (1.0 + sc)
    dg = _colsum(dn * xn)
    dxn = dn * g
    dx = r * (dxn - xn * jnp.mean(dxn * xn, axis=-1, keepdims=True))
    return dx, dsc, dsh, dg


def _postnorm_bwd(yv, dout, g, gt):
    r = _rms(yv)
    yn = yv * r
    dgt = _colsum(dout * (yn * g))
    dn = dout * gt
    dg = _colsum(dn * yn)
    dyn = dn * g
    dy = r * (dyn - yn * jnp.mean(dyn * yn, axis=-1, keepdims=True))
    return dy, dgt, dg


def _gates(xc, wr_ref, wi_ref, b_r, b_i, sp, nh, bw):
    xcb = xc.astype(BF16)
    zr = jnp.concatenate([_dot(xcb[:, h * bw:(h + 1) * bw], wr_ref[h]) for h in range(nh)], axis=1) + b_r
    zi = jnp.concatenate([_dot(xcb[:, h * bw:(h + 1) * bw], wi_ref[h]) for h in range(nh)], axis=1) + b_i
    r = _sigmoid(zr)
    ig = _sigmoid(zi)
    la = (-LRU_C * r) * sp
    a = jnp.exp(la)
    mult = jnp.sqrt(_neg_expm1_twice(la))
    return xcb, r, ig, a, mult


def _shift_rows(cur, edge, k, up):
    t, dd = cur.shape
    blocks = cur.reshape(t // SUBLANES, SUBLANES, dd)
    row = lax.broadcasted_iota(jnp.int32, (1, SUBLANES, dd), 1)
    if up:
        r = pltpu.roll(blocks, SUBLANES - k, 1)
        nxt = jnp.concatenate([r[1:], pltpu.roll(edge, SUBLANES - k, 0)[None]], axis=0)
        out = jnp.where(row >= SUBLANES - k, nxt, r)
    else:
        r = pltpu.roll(blocks, k, 1)
        prv = jnp.concatenate([pltpu.roll(edge, k, 0)[None], r[:-1]], axis=0)
        out = jnp.where(row < k, prv, r)
    return out.reshape(t, dd)


def _scan_block(a8, b8, reverse):
    row = lax.broadcasted_iota(jnp.int32, a8.shape, 0)
    for s in (1, 2, 4):
        if reverse:
            keep = row < SUBLANES - s
            a_sh = pltpu.roll(a8, SUBLANES - s, 0)
            b_sh = pltpu.roll(b8, SUBLANES - s, 0)
        else:
            keep = row >= s
            a_sh = pltpu.roll(a8, s, 0)
            b_sh = pltpu.roll(b8, s, 0)
        b8 = b8 + a8 * jnp.where(keep, b_sh, 0.0)
        a8 = a8 * jnp.where(keep, a_sh, 1.0)
    return a8, b8


def _prep_small(c, w_mod, b_mod, cw):
    d = c.shape[1]
    cm = w_mod.shape[2]
    cwid = cw.shape[2]
    nl = w_mod.shape[0]

    def body(c_ref, wm_ref, bm_ref, cw_ref, mod_ref, cact_ref, cwf_ref, cbuf, pbuf, rbuf, ssem, rsem, lsem):
        pos, me = _my_position()
        me8 = pl.multiple_of(me * SUBLANES, SUBLANES)
        cbuf[pl.ds(me8, SUBLANES), :] = jnp.broadcast_to(c_ref[...], (SUBLANES, d))
        own_cw = pltpu.make_async_copy(cw_ref, cwf_ref.at[:, :, pl.ds(me * cwid, cwid)], lsem.at[0])
        own_cw.start()
        first = []
        for k in range(1, N_DEV):
            peer, _ = _peer(pos, k)
            rows = cbuf.at[pl.ds(me8, SUBLANES), :]
            first.append(_remote(rows, rows, ssem.at[0, k - 1], rsem.at[0, k - 1], peer))
            first.append(_remote(cw_ref, cwf_ref.at[:, :, pl.ds(me * cwid, cwid)], ssem.at[1, k - 1], rsem.at[1, k - 1], peer))
        for cp in first:
            cp.start()
        for k in range(1, N_DEV):
            peer, pj = _peer(pos, k)
            pj8 = pl.multiple_of(pj * SUBLANES, SUBLANES)
            rows = cbuf.at[pl.ds(pj8, SUBLANES), :]
            _remote(rows, rows, ssem.at[0, k - 1], rsem.at[0, k - 1], peer).wait_recv()
        cv = cbuf[...]
        cact = cv * _sigmoid(cv)
        cact_ref[...] = cact
        cb = cact.astype(BF16)
        for l in range(nl):
            pbuf[l] = _dot(cb, wm_ref[l].astype(BF16))
        own_p = pltpu.make_async_copy(pbuf.at[:, pl.ds(me8, SUBLANES), :], rbuf.at[me], lsem.at[1])
        own_p.start()
        second = []
        for k in range(1, N_DEV):
            peer, pj = _peer(pos, k)
            pj8 = pl.multiple_of(pj * SUBLANES, SUBLANES)
            second.append(_remote(pbuf.at[:, pl.ds(pj8, SUBLANES), :], rbuf.at[me], ssem.at[2, k - 1], rsem.at[2, k - 1], peer))
        for cp in second:
            cp.start()
        for k in range(1, N_DEV):
            peer, pj = _peer(pos, k)
            _remote(pbuf.at[:, pl.ds(0, SUBLANES), :], rbuf.at[pj], ssem.at[2, k - 1], rsem.at[2, k - 1], peer).wait_recv()
            _remote(cw_ref, cwf_ref.at[:, :, pl.ds(pj * cwid, cwid)], ssem.at[1, k - 1], rsem.at[1, k - 1], peer).wait_recv()
        own_p.wait()
        own_cw.wait()
        for l in range(nl):
            for j in range(N_DEV):
                mod_ref[l:l + 1, j * cm:(j + 1) * cm] = rbuf[j, l, 0:1, :] + bm_ref[l:l + 1, j * cm:(j + 1) * cm]
        for cp in first + second:
            cp.wait_send()

    return pl.pallas_call(
        body,
        name="prep_small",
        out_shape=(
            jax.ShapeDtypeStruct((nl, N_MOD * d), F32),
            jax.ShapeDtypeStruct((N_DEV * SUBLANES, d), F32),
            jax.ShapeDtypeStruct((nl, CW_ROWS, d), F32),
        ),
        in_specs=[VMEM_SPEC] * 4,
        out_specs=(VMEM_SPEC,) * 3,
        scratch_shapes=[
            pltpu.VMEM((N_DEV * SUBLANES, d), F32),
            pltpu.VMEM((nl, N_DEV * SUBLANES, cm), F32),
            pltpu.VMEM((N_DEV, nl, SUBLANES, cm), F32),
            pltpu.SemaphoreType.DMA((3, N_DEV - 1)),
            pltpu.SemaphoreType.DMA((3, N_DEV - 1)),
            pltpu.SemaphoreType.DMA((2,)),
        ],
        compiler_params=_cparams(vmem=3 * _nbytes(w_mod.shape, F32)),
    )(c, w_mod, b_mod, cw)


def _exchange_start(parts, gather, name):
    n = len(parts)
    lands = [lax.empty(((N_DEV,) + tuple(p.shape)) if gather else tuple(p.shape), p.dtype) for p in parts]

    def body(*refs):
        ins, lnd = refs[:n], refs[n:2 * n]
        ssem, rsem, token = refs[2 * n], refs[2 * n + 1], refs[-1]
        pos, me = _my_position()
        for k in range(1, N_DEV):
            peer, pj = _peer(pos, k)
            for t in range(n):
                src = ins[t] if gather else ins[t].at[pj]
                q = t * (N_DEV - 1) + k - 1
                _remote(src, lnd[t].at[me], ssem.at[q], rsem.at[q], peer).start()
        token[...] = jnp.zeros(token.shape, F32)

    out = pl.pallas_call(
        body,
        name=name,
        out_shape=(pltpu.SemaphoreType.DMA((n * (N_DEV - 1),)), pltpu.SemaphoreType.DMA((n * (N_DEV - 1),)))
        + tuple(pltpu.HBM(p.shape, p.dtype) for p in parts) + tuple(pltpu.HBM(p.shape, p.dtype) for p in lands)
        + (jax.ShapeDtypeStruct((SUBLANES, 128), F32),),
        in_specs=[HBM_SPEC] * (2 * n),
        out_specs=(SEM_SPEC, SEM_SPEC) + (HBM_SPEC,) * (2 * n) + (VMEM_SPEC,),
        input_output_aliases={i: 2 + i for i in range(2 * n)},
        compiler_params=pltpu.CompilerParams(has_side_effects=SIDE_EFFECT),
    )(*[pltpu.with_memory_space_constraint(p, pltpu.HBM) for p in list(parts) + lands])
    return out[0], out[1], list(out[2:2 + n]), list(out[2 + n:2 + 2 * n]), out[-1]


def _exchange_wait(started, after, gather, name):
    ssem, rsem, parts, lands, _ = started
    n = len(parts)

    def body(*refs):
        ins, lnd = refs[:n], refs[n:2 * n]
        ssem_ref, rsem_ref = refs[2 * n], refs[2 * n + 1]
        stage, lsem = refs[-1 - n:-1], refs[-1]
        pos, me = _my_position()
        load = []
        for t in range(n):
            src = ins[t] if gather else ins[t].at[me]
            load.append(pltpu.make_async_copy(src, stage[t], lsem.at[t]))
            load[-1].start()
        store = []
        for t in range(n):
            load[t].wait()
            store.append(pltpu.make_async_copy(stage[t], lnd[t].at[me], lsem.at[t]))
            store[-1].start()
        for k in range(1, N_DEV):
            peer, pj = _peer(pos, k)
            for t in range(n):
                src = ins[t] if gather else ins[t].at[pj]
                q = t * (N_DEV - 1) + k - 1
                _remote(src, lnd[t].at[me], ssem_ref.at[q], rsem_ref.at[q], peer).wait_send()
                _remote(src, lnd[t].at[pj], ssem_ref.at[q], rsem_ref.at[q], peer).wait_recv()
        for cp in store:
            cp.wait()

    out = pl.pallas_call(
        body,
        name=name,
        out_shape=tuple(pltpu.HBM(p.shape, p.dtype) for p in parts) + tuple(pltpu.HBM(p.shape, p.dtype) for p in lands),
        in_specs=[HBM_SPEC] * (2 * n) + [SEM_SPEC, SEM_SPEC, ANY_SPEC],
        out_specs=(HBM_SPEC,) * (2 * n),
        input_output_aliases={i: i for i in range(2 * n)},
        scratch_shapes=[pltpu.VMEM(tuple(z.shape[1:]), z.dtype) for z in lands] + [pltpu.SemaphoreType.DMA((n,))],
        compiler_params=pltpu.CompilerParams(has_side_effects=SIDE_EFFECT),
    )(*parts, *lands, ssem, rsem, after)
    return list(out[n:])


CHIP_PEERS = 3


def _chip_peer(pos, k, core):
    x, y, _ = pos
    px = 1 - x if k & 2 else x
    py = 1 - y if k & 1 else y
    return (px, py, core), 4 * px + 2 * py + core


def _gather2_start(parts, after, name):
    n = len(parts)
    lands = [lax.empty((N_DEV,) + tuple(p.shape), p.dtype) for p in parts]
    per = CHIP_PEERS + 1

    def body(*refs):
        ins, lnd = refs[:n], refs[n:2 * n]
        ssem, rsem, token = refs[2 * n + 1], refs[2 * n + 2], refs[-1]
        pos, me = _my_position()
        sibling = (pos[0], pos[1], 1 - pos[2])
        for t in range(n):
            _remote(ins[t], lnd[t].at[me], ssem.at[per * t], rsem.at[per * t], sibling).start()
        for k in range(1, per):
            peer, _ = _chip_peer(pos, k, pos[2])
            for t in range(n):
                _remote(ins[t], lnd[t].at[me], ssem.at[per * t + k], rsem.at[per * t + k], peer).start()
        token[...] = jnp.zeros(token.shape, F32)

    out = pl.pallas_call(
        body,
        name=name,
        out_shape=(pltpu.SemaphoreType.DMA((n * per,)), pltpu.SemaphoreType.DMA((n * per,)))
        + tuple(pltpu.HBM(p.shape, p.dtype) for p in parts) + tuple(pltpu.HBM(p.shape, p.dtype) for p in lands)
        + (jax.ShapeDtypeStruct((SUBLANES, 128), F32),),
        in_specs=[HBM_SPEC] * (2 * n) + [ANY_SPEC],
        out_specs=(SEM_SPEC, SEM_SPEC) + (HBM_SPEC,) * (2 * n) + (VMEM_SPEC,),
        input_output_aliases={i: 2 + i for i in range(2 * n)},
        compiler_params=pltpu.CompilerParams(has_side_effects=SIDE_EFFECT),
    )(*[pltpu.with_memory_space_constraint(p, pltpu.HBM) for p in list(parts) + lands], after)
    return out[0], out[1], list(out[2:2 + n]), list(out[2 + n:2 + 2 * n]), out[-1]


def _gather2_forward(started, after, name):
    ssem, rsem, parts, lands, _ = started
    n = len(lands)
    per = CHIP_PEERS + 1

    def body(*refs):
        lnd = refs[:n]
        rsem_a, fsend, frecv, token = refs[n], refs[2 * n + 2], refs[2 * n + 3], refs[2 * n + 4]
        pos, _ = _my_position()
        sibling = (pos[0], pos[1], 1 - pos[2])
        for k in range(1, per):
            peer, pk = _chip_peer(pos, k, pos[2])
            for t in range(n):
                block = lnd[t].at[pk]
                _remote(block, block, rsem_a.at[per * t + k], rsem_a.at[per * t + k], peer).wait_recv()
                q = CHIP_PEERS * t + k - 1
                _remote(block, block, fsend.at[q], frecv.at[q], sibling).start()
        token[...] = jnp.zeros(token.shape, F32)

    out = pl.pallas_call(
        body,
        name=name,
        out_shape=tuple(pltpu.HBM(p.shape, p.dtype) for p in lands)
        + (pltpu.SemaphoreType.DMA((n * CHIP_PEERS,)), pltpu.SemaphoreType.DMA((n * CHIP_PEERS,)),
           jax.ShapeDtypeStruct((SUBLANES, 128), F32)),
        in_specs=[HBM_SPEC] * n + [SEM_SPEC, ANY_SPEC],
        out_specs=(HBM_SPEC,) * n + (SEM_SPEC, SEM_SPEC, VMEM_SPEC),
        input_output_aliases={i: i for i in range(n)},
        compiler_params=pltpu.CompilerParams(has_side_effects=SIDE_EFFECT),
    )(*lands, rsem, after)
    return ssem, rsem, parts, list(out[:n]), out[n + 2], out[n], out[n + 1]


def _gather2_wait(forwarded, after, name):
    ssem, rsem, parts, lands, _, fsend, frecv = forwarded
    n = len(parts)
    per = CHIP_PEERS + 1

    def body(*refs):
        ins, lnd = refs[:n], refs[n:2 * n]
        ssem_a, rsem_a, fs, fr = refs[2 * n:2 * n + 4]
        stage, lsem = refs[-1 - n:-1], refs[-1]
        pos, me = _my_position()
        sibling = (pos[0], pos[1], 1 - pos[2])
        sib = 4 * pos[0] + 2 * pos[1] + 1 - pos[2]
        load = []
        for t in range(n):
            load.append(pltpu.make_async_copy(ins[t], stage[t], lsem.at[t]))
            load[-1].start()
        store = []
        for t in range(n):
            load[t].wait()
            store.append(pltpu.make_async_copy(stage[t], lnd[t].at[me], lsem.at[t]))
            store[-1].start()
        for t in range(n):
            _remote(ins[t], lnd[t].at[me], ssem_a.at[per * t], rsem_a.at[per * t], sibling).wait_send()
            _remote(ins[t], lnd[t].at[sib], ssem_a.at[per * t], rsem_a.at[per * t], sibling).wait_recv()
        for k in range(1, per):
            peer, pk = _chip_peer(pos, k, pos[2])
            _, qk = _chip_peer(pos, k, 1 - pos[2])
            for t in range(n):
                q = CHIP_PEERS * t + k - 1
                _remote(ins[t], lnd[t].at[me], ssem_a.at[per * t + k], rsem_a.at[per * t + k], peer).wait_send()
                _remote(lnd[t].at[pk], lnd[t].at[pk], fs.at[q], fr.at[q], sibling).wait_send()
                _remote(lnd[t].at[qk], lnd[t].at[qk], fs.at[q], fr.at[q], sibling).wait_recv()
        for cp in store:
            cp.wait()

    out = pl.pallas_call(
        body,
        name=name,
        out_shape=tuple(pltpu.HBM(p.shape, p.dtype) for p in parts) + tuple(pltpu.HBM(p.shape, p.dtype) for p in lands),
        in_specs=[HBM_SPEC] * (2 * n) + [SEM_SPEC] * 4 + [ANY_SPEC],
        out_specs=(HBM_SPEC,) * (2 * n),
        input_output_aliases={i: i for i in range(2 * n)},
        scratch_shapes=[pltpu.VMEM(tuple(z.shape[1:]), z.dtype) for z in lands] + [pltpu.SemaphoreType.DMA((n,))],
        compiler_params=pltpu.CompilerParams(has_side_effects=SIDE_EFFECT),
    )(*parts, *lands, ssem, rsem, fsend, frecv, after)
    return list(out[n:])


def _reduce_small(rows, dm8, cact, after):
    r, d = rows.shape
    nl = dm8.shape[0]
    cm = dm8.shape[2] // N_DEV

    def body(rows_ref, dm_ref, cact_ref, after_ref, orow_ref, owm_ref, gr, dmr, ssem, rsem, lsem):
        pos, me = _my_position()
        me8 = pl.multiple_of(me * SUBLANES, SUBLANES)
        gr[me] = rows_ref[...]
        own_dm = pltpu.make_async_copy(dm_ref.at[:, :, pl.ds(me * cm, cm)], dmr.at[:, pl.ds(me8, SUBLANES), :], lsem.at[0])
        own_dm.start()
        sends = []
        for k in range(1, N_DEV):
            peer, pj = _peer(pos, k)
            sends.append(_remote(gr.at[me], gr.at[me], ssem.at[0, k - 1], rsem.at[0, k - 1], peer))
            sends.append(_remote(dm_ref.at[:, :, pl.ds(pj * cm, cm)], dmr.at[:, pl.ds(me8, SUBLANES), :],
                                 ssem.at[1, k - 1], rsem.at[1, k - 1], peer))
        for cp in sends:
            cp.start()
        for k in range(1, N_DEV):
            peer, pj = _peer(pos, k)
            pj8 = pl.multiple_of(pj * SUBLANES, SUBLANES)
            _remote(gr.at[pj], gr.at[pj], ssem.at[0, k - 1], rsem.at[0, k - 1], peer).wait_recv()
            _remote(dm_ref.at[:, :, pl.ds(0, cm)], dmr.at[:, pl.ds(pj8, SUBLANES), :], ssem.at[1, k - 1], rsem.at[1, k - 1], peer).wait_recv()
        own_dm.wait()
        acc = gr[0]
        for j in range(1, N_DEV):
            acc = acc + gr[j]
        orow_ref[...] = acc
        cb = cact_ref[...].astype(BF16)
        for l in range(nl):
            owm_ref[l] = _dot_tn(cb, dmr[l].astype(BF16))
        for cp in sends:
            cp.wait_send()

    return pl.pallas_call(
        body,
        name="reduce_small",
        out_shape=(jax.ShapeDtypeStruct((r, d), F32), jax.ShapeDtypeStruct((nl, d, cm), F32)),
        in_specs=[VMEM_SPEC] * 3 + [ANY_SPEC],
        out_specs=(VMEM_SPEC,) * 2,
        scratch_shapes=[
            pltpu.VMEM((N_DEV, r, d), F32),
            pltpu.VMEM((nl, N_DEV * SUBLANES, cm), F32),
            pltpu.SemaphoreType.DMA((2, N_DEV - 1)),
            pltpu.SemaphoreType.DMA((2, N_DEV - 1)),
            pltpu.SemaphoreType.DMA((1,)),
        ],
        compiler_params=_cparams(vmem=4 * _nbytes((N_DEV, r, d), F32) + 6 * _nbytes((nl, d, cm), F32)),
    )(rows, dm8, cact, after)


def _sum_gathered(zones):
    nl = len(zones)

    def body(*refs):
        for l in range(nl):
            acc = refs[l][0].astype(F32)
            for j in range(1, N_DEV):
                acc = acc + refs[l][j].astype(F32)
            refs[nl][l] = acc

    return pl.pallas_call(
        body,
        name="sum_gathered",
        out_shape=jax.ShapeDtypeStruct((nl,) + tuple(zones[0].shape[1:]), F32),
        in_specs=[VMEM_SPEC] * nl,
        out_specs=VMEM_SPEC,
        compiler_params=_cparams(vmem=8 * nl * _nbytes(zones[0].shape, BF16)),
    )(*zones)


def _in_proj_fwd(x, mod, rows, win_f, l, tm, after):
    s, d = x.shape
    nb, _, ci = win_f.shape

    def body(x_ref, mod_ref, rows_ref, w_ref, after_ref, proj_ref, h_ref):
        xv = x_ref[...]
        g = rows_ref[R_G_PRE_MIX:R_G_PRE_MIX + 1, :]
        h = (xv * _rms(xv) * g) * (1.0 + mod_ref[M_SC_M:M_SC_M + 1, :]) + mod_ref[M_SH_M:M_SH_M + 1, :]
        hb = h.astype(BF16)
        h_ref[...] = hb
        for j in range(nb):
            proj_ref[:, j * ci:(j + 1) * ci] = _dot(hb, w_ref[j])

    return pl.pallas_call(
        body,
        name="in_proj_fwd",
        grid=(s // tm,),
        in_specs=[
            pl.BlockSpec((tm, d), lambda i: (i, 0)),
            _resident((None, N_MOD, d), lambda i: (l, 0, 0)),
            _resident((None, N_ROWS, d), lambda i: (l, 0, 0)),
            _resident((nb, d, ci), lambda i: (0, 0, 0)),
            ANY_SPEC,
        ],
        out_specs=(pl.BlockSpec((tm, nb * ci), lambda i: (i, 0)), pl.BlockSpec((tm, d), lambda i: (i, 0))),
        out_shape=(jax.ShapeDtypeStruct((s, nb * ci), F32), jax.ShapeDtypeStruct((s, d), BF16)),
        compiler_params=_cparams(("parallel",), _nbytes((nb, d, ci), BF16) + 3 * _nbytes((tm, nb * ci), F32) + 8 * _nbytes((tm, d), F32)),
    )(*_hbm(x), mod, rows, win_f, after)


def _mixer_core_fwd(proj, x, mod, rows, cwf, wr, wi, wa_f, wb_f, wo_f, l, tm, after):
    s, d = x.shape
    nh, bw, _ = wr.shape[1:]

    def body(proj_ref, x_ref, mod_ref, rows_ref, cw_ref, wr_ref, wi_ref, wa_ref, wb_ref, wo_ref, after_ref,
             x1_ref, hs_ref, yap_ref, ybp_ref, y_ref, cvbuf, xbbuf, a_s, b_s, hprev):
        i = pl.program_id(0)

        @pl.when(i == 0)
        def _():
            cvbuf[...] = jnp.zeros((SUBLANES, d), F32)
            xbbuf[...] = jnp.zeros((SUBLANES, d), F32)
            hprev[...] = jnp.zeros((SUBLANES, d), F32)

        def row(r):
            return rows_ref[r:r + 1, :]

        def tap(r):
            return cw_ref[r:r + 1, :]

        ba = proj_ref[:, 0:d]
        cv = proj_ref[:, d:2 * d] * proj_ref[:, 2 * d:3 * d]
        cvt = cvbuf[...]
        conv3 = ((row(R_CONV_A_B) + _shift_rows(cv, cvt, 2, False) * tap(CW_A)) + _shift_rows(cv, cvt, 1, False) * tap(CW_A + 1)) + cv * tap(CW_A + 2)
        ya = ba * conv3
        cvbuf[...] = cv[tm - SUBLANES:, :]
        xb = proj_ref[:, 3 * d:4 * d]
        xbt = xbbuf[...]
        xc = (((row(R_CONV_B_B) + _shift_rows(xb, xbt, 3, False) * tap(CW_B)) + _shift_rows(xb, xbt, 2, False) * tap(CW_B + 1))
              + _shift_rows(xb, xbt, 1, False) * tap(CW_B + 2)) + xb * tap(CW_B + 3)
        xbbuf[...] = xb[tm - SUBLANES:, :]
        sp = _softplus_neg(row(R_LAMBDA))
        _, _, ig, a, mult = _gates(xc, wr_ref, wi_ref, row(R_B_GATE_R), row(R_B_GATE_I), sp, nh, bw)
        a_s[...] = a
        b_s[...] = mult * (ig * xc)

        def blk(j, hp):
            o = pl.multiple_of(j * SUBLANES, SUBLANES)
            a8, b8 = _scan_block(a_s[pl.ds(o, SUBLANES), :], b_s[pl.ds(o, SUBLANES), :], reverse=False)
            h8 = b8 + a8 * hp
            hs_ref[pl.ds(o, SUBLANES), :] = h8
            return jnp.broadcast_to(h8[SUBLANES - 1:SUBLANES, :], (SUBLANES, d))

        hprev[...] = lax.fori_loop(0, tm // SUBLANES, blk, hprev[...])
        gel, _ = _gelu(proj_ref[:, 4 * d:5 * d])
        yb = hs_ref[...] * gel
        yap = _dot(ya.astype(BF16), wa_ref[...])
        ybp = _dot(yb.astype(BF16), wb_ref[...])
        yap_ref[...] = yap
        ybp_ref[...] = ybp
        m = _sigmoid(proj_ref[:, 5 * d:6 * d]) * yap + _sigmoid(proj_ref[:, 6 * d:7 * d]) * ybp
        y = _dot(m.astype(BF16), wo_ref[...])
        y_ref[...] = y
        x1_ref[...] = x_ref[...] + mod_ref[M_GT_M:M_GT_M + 1, :] * ((y * _rms(y)) * row(R_G_POST_MIX))

    tile = pl.BlockSpec((tm, d), lambda i: (i, 0))
    return pl.pallas_call(
        body,
        name="mixer_core_fwd",
        grid=(s // tm,),
        in_specs=[
            pl.BlockSpec((tm, 7 * d), lambda i: (i, 0)),
            tile,
            _resident((None, N_MOD, d), lambda i: (l, 0, 0)),
            _resident((None, N_ROWS, d), lambda i: (l, 0, 0)),
            _resident((None, CW_ROWS, d), lambda i: (l, 0, 0)),
            _resident((None, nh, bw, bw), lambda i: (l, 0, 0, 0)),
            _resident((None, nh, bw, bw), lambda i: (l, 0, 0, 0)),
            _resident((d, d), lambda i: (0, 0)),
            _resident((d, d), lambda i: (0, 0)),
            _resident((d, d), lambda i: (0, 0)),
            ANY_SPEC,
        ],
        out_specs=(tile,) * 5,
        out_shape=(jax.ShapeDtypeStruct((s, d), F32),) * 5,
        scratch_shapes=[
            pltpu.VMEM((SUBLANES, d), F32),
            pltpu.VMEM((SUBLANES, d), F32),
            pltpu.VMEM((tm, d), F32),
            pltpu.VMEM((tm, d), F32),
            pltpu.VMEM((SUBLANES, d), F32),
        ],
        compiler_params=_cparams(("arbitrary",), 3 * _nbytes((d, d), BF16) + 2 * _nbytes((tm, 7 * d), F32) + 40 * _nbytes((tm, d), F32)),
    )(*_hbm(proj, x), mod, rows, cwf, wr, wi, wa_f, wb_f, wo_f, after)


def _mlp_fwd(x1, mod, rows, wup_f, wdn_f, l, tm, after, target=None):
    s, d = x1.shape
    nb, _, cu = wup_f.shape
    dff = nb * cu
    with_loss = target is not None

    def body(x1_ref, mod_ref, rows_ref, wu_ref, wd_ref, after_ref, *refs):
        x2_ref, ru_ref, y2_ref, h2_ref = refs[-6:-2] if with_loss else refs
        xv = x1_ref[...]
        g = rows_ref[R_G_PRE_MLP:R_G_PRE_MLP + 1, :]
        h2 = ((xv * _rms(xv) * g) * (1.0 + mod_ref[M_SC_F:M_SC_F + 1, :]) + mod_ref[M_SH_F:M_SH_F + 1, :]).astype(BF16)
        h2_ref[...] = h2
        ru = jnp.concatenate([jnp.maximum(_dot(h2, wu_ref[j]), 0.0) for j in range(nb)], axis=1)
        ru_ref[...] = ru.astype(BF16)
        y2 = _dot((ru * ru).astype(BF16), wd_ref[...])
        y2_ref[...] = y2
        x2 = xv + mod_ref[M_GT_F:M_GT_F + 1, :] * ((y2 * _rms(y2)) * rows_ref[R_G_POST_MLP:R_G_POST_MLP + 1, :])
        x2_ref[...] = x2
        if with_loss:
            t_ref, dy_ref, loss_ref = refs[0], refs[-2], refs[-1]

            @pl.when(pl.program_id(0) == 0)
            def _():
                loss_ref[...] = jnp.zeros(loss_ref.shape, F32)

            e = x2 - t_ref[...]
            dy_ref[...] = e * (1.0 / d)
            loss_ref[...] += 0.5 * jnp.sum(jnp.mean(e * e, axis=-1, keepdims=True), axis=0, keepdims=True)

    tile = pl.BlockSpec((tm, d), lambda i: (i, 0))
    wide = pl.BlockSpec((tm, dff), lambda i: (i, 0))
    out_specs = (tile, wide, tile, tile)
    out_shape = (jax.ShapeDtypeStruct((s, d), F32), jax.ShapeDtypeStruct((s, dff), BF16),
                 jax.ShapeDtypeStruct((s, d), F32), jax.ShapeDtypeStruct((s, d), BF16))
    if with_loss:
        out_specs += (tile, pl.BlockSpec((SUBLANES, 128), lambda i: (0, 0)))
        out_shape += (jax.ShapeDtypeStruct((s, d), F32), jax.ShapeDtypeStruct((SUBLANES, 128), F32))
    return pl.pallas_call(
        body,
        name="mlp_fwd",
        grid=(s // tm,),
        in_specs=[
            tile,
            _resident((None, N_MOD, d), lambda i: (l, 0, 0)),
            _resident((None, N_ROWS, d), lambda i: (l, 0, 0)),
            _resident((nb, d, cu), lambda i: (0, 0, 0)),
            _resident((dff, d), lambda i: (0, 0)),
            ANY_SPEC,
        ] + ([tile] if with_loss else []),
        out_specs=out_specs,
        out_shape=out_shape,
        compiler_params=_cparams(("arbitrary",), 2 * _nbytes((dff, d), BF16) + 5 * _nbytes((tm, dff), F32) + 16 * _nbytes((tm, d), F32)),
    )(*_hbm(x1), mod, rows, wup_f, wdn_f, after, *(_hbm(target) if with_loss else ()))


def _mlp_bwd(dx2, x1, y2, ru, mod, rows, wup_f, wdn_f, l, tm):
    s, d = x1.shape
    nb, _, cu = wup_f.shape
    dff = nb * cu

    def body(dx2_ref, x1_ref, y2_ref, ru_ref, mod_ref, rows_ref, wu_ref, wd_ref, dx1_ref, dy2_ref, dup_ref, act_ref, sm_ref):
        @pl.when(pl.program_id(0) == 0)
        def _():
            sm_ref[...] = jnp.zeros(sm_ref.shape, F32)

        dout = dx2_ref[...]
        dy2, dgt, dgpost = _postnorm_bwd(y2_ref[...], dout, rows_ref[R_G_POST_MLP:R_G_POST_MLP + 1, :], mod_ref[M_GT_F:M_GT_F + 1, :])
        dy2b = dy2.astype(BF16)
        dy2_ref[...] = dy2b
        ruv = ru_ref[...].astype(F32)
        act_ref[...] = (ruv * ruv).astype(BF16)
        dup = (_dot_nt(dy2b, wd_ref[...]) * (2.0 * ruv)).astype(BF16)
        dup_ref[...] = dup
        dh2 = _dot_nt(dup[:, 0:cu], wu_ref[0])
        for j in range(1, nb):
            dh2 = dh2 + _dot_nt(dup[:, j * cu:(j + 1) * cu], wu_ref[j])
        dxn, dsc, dsh, dgpre = _prenorm_bwd(x1_ref[...], dh2, rows_ref[R_G_PRE_MLP:R_G_PRE_MLP + 1, :], mod_ref[M_SC_F:M_SC_F + 1, :])
        dx1_ref[...] = dout + dxn
        for r, v in ((G_MLP_GT, dgt), (G_MLP_GPOST, dgpost), (G_MLP_SC, dsc), (G_MLP_SH, dsh), (G_MLP_GPRE, dgpre)):
            sm_ref[r:r + 1, :] += v

    tile = pl.BlockSpec((tm, d), lambda i: (i, 0))
    wide = pl.BlockSpec((tm, dff), lambda i: (i, 0))
    return pl.pallas_call(
        body,
        name="mlp_bwd",
        grid=(s // tm,),
        in_specs=[
            tile, tile, tile, wide,
            _resident((None, N_MOD, d), lambda i: (l, 0, 0)),
            _resident((None, N_ROWS, d), lambda i: (l, 0, 0)),
            _resident((nb, d, cu), lambda i: (0, 0, 0)),
            _resident((dff, d), lambda i: (0, 0)),
        ],
        out_specs=(tile, tile, wide, wide, pl.BlockSpec((G_MLP_ROWS, d), lambda i: (0, 0))),
        out_shape=(jax.ShapeDtypeStruct((s, d), F32), jax.ShapeDtypeStruct((s, d), BF16), jax.ShapeDtypeStruct((s, dff), BF16),
                   jax.ShapeDtypeStruct((s, dff), BF16), jax.ShapeDtypeStruct((G_MLP_ROWS, d), F32)),
        compiler_params=_cparams(("arbitrary",), 2 * _nbytes((dff, d), BF16) + 6 * _nbytes((tm, dff), F32) + 16 * _nbytes((tm, d), F32)),
    )(*_hbm(dx2, x1, y2, ru), mod, rows, wup_f, wdn_f)


def _mixer_core_bwd(dx1, y, yap, ybp, hs, proj, mod, rows, cwf, wr, wi, wa_f, wb_f, wo_f, l, tm, after):
    s, d = dx1.shape
    nh, bw, _ = wr.shape[1:]
    nt = s // tm
    per = tm // SUBLANES

    def body(dx1_ref, y_ref, yap_ref, ybp_ref, hs_ref, hsh_ref, proj_ref, projh_ref, mod_ref, rows_ref, cw_ref,
             wr_ref, wi_ref, wa_ref, wb_ref, wo_ref, after_ref,
             dproj_ref, dy_ref, m_ref, dyap_ref, dybp_ref, ya_ref, yb_ref, sm_ref, dwg_ref,
             abuf, dcbuf, dxbuf, al_s, dh_s, lam_s, lnext):
        i = pl.program_id(0)
        first_tile = i == nt - 1

        @pl.when(i == 0)
        def _():
            sm_ref[...] = jnp.zeros(sm_ref.shape, F32)
            dwg_ref[...] = jnp.zeros(dwg_ref.shape, F32)
            zero = jnp.zeros((SUBLANES, d), F32)
            abuf[...] = zero
            dcbuf[...] = zero
            dxbuf[...] = zero
            lnext[...] = zero

        def row(r):
            return rows_ref[r:r + 1, :]

        def tap(r):
            return cw_ref[r:r + 1, :]

        def acc(r, v):
            sm_ref[r:r + 1, :] += v

        keep_halo = jnp.where(first_tile, 0.0, 1.0)
        dy, dgt, dgpost = _postnorm_bwd(y_ref[...], dx1_ref[...], row(R_G_POST_MIX), mod_ref[M_GT_M:M_GT_M + 1, :])
        acc(G_MIX_GT, dgt)
        acc(G_MIX_GPOST, dgpost)
        dyb16 = dy.astype(BF16)
        dy_ref[...] = dyb16
        dm = _dot_nt(dyb16, wo_ref[...])
        sa = _sigmoid(proj_ref[:, 5 * d:6 * d])
        sb = _sigmoid(proj_ref[:, 6 * d:7 * d])
        yap = yap_ref[...]
        ybp = ybp_ref[...]
        m_ref[...] = (sa * yap + sb * ybp).astype(BF16)
        dyap_f = dm * sa
        dybp_f = dm * sb
        dyap = dyap_f.astype(BF16)
        dybp = dybp_f.astype(BF16)
        dyap_ref[...] = dyap
        dybp_ref[...] = dybp
        dproj_ref[:, 5 * d:6 * d] = (dyap_f * yap * (1.0 - sa)).astype(BF16)
        dproj_ref[:, 6 * d:7 * d] = (dybp_f * ybp * (1.0 - sb)).astype(BF16)
        dya = _dot_nt(dyap, wa_ref[...])
        dyb = _dot_nt(dybp, wb_ref[...])
        ba = proj_ref[:, 0:d]
        ca = proj_ref[:, d:2 * d]
        va = proj_ref[:, 2 * d:3 * d]
        cv = ca * va
        cvh = keep_halo * (projh_ref[:, d:2 * d] * projh_ref[:, 2 * d:3 * d])
        cvm2 = _shift_rows(cv, cvh, 2, False)
        cvm1 = _shift_rows(cv, cvh, 1, False)
        conv3 = ((row(R_CONV_A_B) + cvm2 * tap(CW_A)) + cvm1 * tap(CW_A + 1)) + cv * tap(CW_A + 2)
        ya_ref[...] = (ba * conv3).astype(BF16)
        dproj_ref[:, 0:d] = (dya * conv3).astype(BF16)
        dc3 = dya * ba
        acc(G_MIX_CAB, _colsum(dc3))
        acc(G_MIX_CAW, _colsum(dc3 * cvm2))
        acc(G_MIX_CAW + 1, _colsum(dc3 * cvm1))
        acc(G_MIX_CAW + 2, _colsum(dc3 * cv))
        dct = dcbuf[...]
        dcv = (dc3 * tap(CW_A + 2) + _shift_rows(dc3, dct, 1, True) * tap(CW_A + 1)) + _shift_rows(dc3, dct, 2, True) * tap(CW_A)
        dcbuf[...] = dc3[:SUBLANES, :]
        dproj_ref[:, d:2 * d] = (dcv * va).astype(BF16)
        dproj_ref[:, 2 * d:3 * d] = (dcv * ca).astype(BF16)
        xb = proj_ref[:, 3 * d:4 * d]
        gb = proj_ref[:, 4 * d:5 * d]
        xbh = keep_halo * projh_ref[:, 3 * d:4 * d]
        xm3 = _shift_rows(xb, xbh, 3, False)
        xm2 = _shift_rows(xb, xbh, 2, False)
        xm1 = _shift_rows(xb, xbh, 1, False)
        xc = (((row(R_CONV_B_B) + xm3 * tap(CW_B)) + xm2 * tap(CW_B + 1)) + xm1 * tap(CW_B + 2)) + xb * tap(CW_B + 3)
        lam = row(R_LAMBDA)
        sp = _softplus_neg(lam)
        xcb, r, ig, a, mult = _gates(xc, wr_ref, wi_ref, row(R_B_GATE_R), row(R_B_GATE_I), sp, nh, bw)
        gel, gsig = _gelu(gb)
        hs = hs_ref[...]
        yb_ref[...] = (hs * gel).astype(BF16)
        dproj_ref[:, 4 * d:5 * d] = (dyb * hs * _gelu_grad(gb, gsig)).astype(BF16)
        al_s[...] = _shift_rows(a, abuf[...], 1, True)
        abuf[...] = a[:SUBLANES, :]
        dh_s[...] = dyb * gel

        def blk(j, ln):
            o = pl.multiple_of((per - 1 - j) * SUBLANES, SUBLANES)
            a8, b8 = _scan_block(al_s[pl.ds(o, SUBLANES), :], dh_s[pl.ds(o, SUBLANES), :], reverse=True)
            l8 = b8 + a8 * ln
            lam_s[pl.ds(o, SUBLANES), :] = l8
            return jnp.broadcast_to(l8[0:1, :], (SUBLANES, d))

        lnext[...] = lax.fori_loop(0, per, blk, lnext[...])
        dbb = lam_s[...]
        da = dbb * _shift_rows(hs, keep_halo * hsh_ref[...], 1, False)
        dbx = dbb * xc
        dmult = dbx * ig
        dig = dbx * mult
        dxc = (dbb * mult) * ig
        dla = a * (da - (dmult * a) / mult)
        dlar = dla * r
        acc(G_MIX_LAM, _colsum(dlar) * (LRU_C * _sigmoid(-lam)))
        dzr = (dlar * (1.0 - r)) * (-LRU_C * sp)
        dzi = dig * ig * (1.0 - ig)
        acc(G_MIX_BR, _colsum(dzr))
        acc(G_MIX_BI, _colsum(dzi))
        dzrb = dzr.astype(BF16)
        dzib = dzi.astype(BF16)
        back = []
        for h in range(nh):
            sl = slice(h * bw, (h + 1) * bw)
            back.append(_dot_nt(dzrb[:, sl], wr_ref[h]) + _dot_nt(dzib[:, sl], wi_ref[h]))
            dwg_ref[0, h] += _dot_tn(xcb[:, sl], dzrb[:, sl])
            dwg_ref[1, h] += _dot_tn(xcb[:, sl], dzib[:, sl])
        dxc = dxc + jnp.concatenate(back, axis=1)
        acc(G_MIX_CBB, _colsum(dxc))
        acc(G_MIX_CBW, _colsum(dxc * xm3))
        acc(G_MIX_CBW + 1, _colsum(dxc * xm2))
        acc(G_MIX_CBW + 2, _colsum(dxc * xm1))
        acc(G_MIX_CBW + 3, _colsum(dxc * xb))
        dxt = dxbuf[...]
        dxb = (((dxc * tap(CW_B + 3) + _shift_rows(dxc, dxt, 1, True) * tap(CW_B + 2)) + _shift_rows(dxc, dxt, 2, True) * tap(CW_B + 1))
               + _shift_rows(dxc, dxt, 3, True) * tap(CW_B))
        dxbuf[...] = dxc[:SUBLANES, :]
        dproj_ref[:, 3 * d:4 * d] = dxb.astype(BF16)

    def rev(i):
        return (nt - 1 - i, 0)

    def halo(i):
        return (jnp.maximum((nt - 1 - i) * per - 1, 0), 0)

    tile = pl.BlockSpec((tm, d), rev)
    return pl.pallas_call(
        body,
        name="mixer_core_bwd",
        grid=(nt,),
        in_specs=[
            tile, tile, tile, tile, tile,
            pl.BlockSpec((SUBLANES, d), halo),
            pl.BlockSpec((tm, 7 * d), rev),
            pl.BlockSpec((SUBLANES, 7 * d), halo),
            _resident((None, N_MOD, d), lambda i: (l, 0, 0)),
            _resident((None, N_ROWS, d), lambda i: (l, 0, 0)),
            _resident((None, CW_ROWS, d), lambda i: (l, 0, 0)),
            _resident((None, nh, bw, bw), lambda i: (l, 0, 0, 0)),
            _resident((None, nh, bw, bw), lambda i: (l, 0, 0, 0)),
            _resident((d, d), lambda i: (0, 0)),
            _resident((d, d), lambda i: (0, 0)),
            _resident((d, d), lambda i: (0, 0)),
            ANY_SPEC,
        ],
        out_specs=(pl.BlockSpec((tm, 7 * d), rev),) + (tile,) * 6 + (
            pl.BlockSpec((G_MIX_ROWS, d), lambda i: (0, 0)), pl.BlockSpec((2, nh, bw, bw), lambda i: (0, 0, 0, 0))),
        out_shape=(jax.ShapeDtypeStruct((s, 7 * d), BF16),) + (jax.ShapeDtypeStruct((s, d), BF16),) * 6 + (
            jax.ShapeDtypeStruct((G_MIX_ROWS, d), F32), jax.ShapeDtypeStruct((2, nh, bw, bw), F32)),
        scratch_shapes=[pltpu.VMEM((SUBLANES, d), F32)] * 3 + [pltpu.VMEM((tm, d), F32)] * 3 + [pltpu.VMEM((SUBLANES, d), F32)],
        compiler_params=_cparams(("arbitrary",), 3 * _nbytes((d, d), BF16) + 3 * _nbytes((tm, 7 * d), F32) + 64 * _nbytes((tm, d), F32)),
    )(*_hbm(dx1, y, yap, ybp, hs, hs, proj, proj), mod, rows, cwf, wr, wi, wa_f, wb_f, wo_f, after)


def _in_proj_bwd(dproj, x, dx1, mod, rows, win_f, l, tm, after):
    s, d = x.shape
    nb, _, ci = win_f.shape

    def body(dp_ref, x_ref, dx1_ref, mod_ref, rows_ref, w_ref, after_ref, dx_ref, sm_ref):
        @pl.when(pl.program_id(0) == 0)
        def _():
            sm_ref[...] = jnp.zeros(sm_ref.shape, F32)

        dh = _dot_nt(dp_ref[:, 0:ci], w_ref[0])
        for j in range(1, nb):
            dh = dh + _dot_nt(dp_ref[:, j * ci:(j + 1) * ci], w_ref[j])
        dxn, dsc, dsh, dg = _prenorm_bwd(x_ref[...], dh, rows_ref[R_G_PRE_MIX:R_G_PRE_MIX + 1, :], mod_ref[M_SC_M:M_SC_M + 1, :])
        dx_ref[...] = dx1_ref[...] + dxn
        for r, v in ((G_IN_SC, dsc), (G_IN_SH, dsh), (G_IN_GPRE, dg)):
            sm_ref[r:r + 1, :] += v

    tile = pl.BlockSpec((tm, d), lambda i: (i, 0))
    return pl.pallas_call(
        body,
        name="in_proj_bwd",
        grid=(s // tm,),
        in_specs=[
            pl.BlockSpec((tm, nb * ci), lambda i: (i, 0)), tile, tile,
            _resident((None, N_MOD, d), lambda i: (l, 0, 0)),
            _resident((None, N_ROWS, d), lambda i: (l, 0, 0)),
            _resident((nb, d, ci), lambda i: (0, 0, 0)),
            ANY_SPEC,
        ],
        out_specs=(tile, pl.BlockSpec((G_IN_ROWS, d), lambda i: (0, 0))),
        out_shape=(jax.ShapeDtypeStruct((s, d), F32), jax.ShapeDtypeStruct((G_IN_ROWS, d), F32)),
        compiler_params=_cparams(("arbitrary",), _nbytes((nb, d, ci), BF16) + 2 * _nbytes((tm, nb * ci), BF16) + 16 * _nbytes((tm, d), F32)),
    )(*_hbm(dproj, x, dx1), mod, rows, win_f, after)


def _wgrad(a, b, cols_owned, ts, after):
    s, k1 = a.shape
    k2 = b.shape[1]
    ns = s // ts
    if cols_owned:
        nblk, bk1, bk2 = N_DEV, k1, k2 // N_DEV
        a_spec = pl.BlockSpec((ts, bk1), lambda j, t: (t, 0))
        b_spec = pl.BlockSpec((ts, bk2), lambda j, t: (t, j))
    else:
        bk1, bk2 = min(WGRAD_ROWS, k1), k2
        nblk = k1 // bk1
        a_spec = pl.BlockSpec((ts, bk1), lambda j, t: (t, j))
        b_spec = pl.BlockSpec((ts, bk2), lambda j, t: (t, 0))

    def body(a_ref, b_ref, after_ref, o_ref, acc_ref):
        t = pl.program_id(1)

        @pl.when(t == 0)
        def _():
            acc_ref[...] = jnp.zeros(acc_ref.shape, F32)

        acc_ref[...] += _dot_tn(a_ref[...], b_ref[...])

        @pl.when(t == ns - 1)
        def _():
            o_ref[...] = acc_ref[...].astype(BF16)

    out = pl.pallas_call(
        body,
        name="wgrad",
        grid=(nblk, ns),
        in_specs=[a_spec, b_spec, ANY_SPEC],
        out_specs=pl.BlockSpec((None, bk1, bk2), lambda j, t: (j, 0, 0)),
        out_shape=pltpu.HBM((nblk, bk1, bk2), BF16),
        scratch_shapes=[pltpu.VMEM((bk1, bk2), F32)],
        compiler_params=_cparams(("parallel", "arbitrary"), 4 * _nbytes((bk1, bk2), F32) + 4 * _nbytes((ts, bk1 + bk2), BF16)),
    )(pltpu.with_memory_space_constraint(a, pltpu.HBM), pltpu.with_memory_space_constraint(b, pltpu.HBM), after)
    return out if cols_owned else out.reshape(N_DEV, k1 // N_DEV, k2)


def _adam_update(w, g, m, v):
    m = ADAM_B1 * m + (1.0 - ADAM_B1) * g
    v = ADAM_B2 * v + (1.0 - ADAM_B2) * (g * g)
    m_hat = m / (1.0 - ADAM_B1 ** ADAM_STEP)
    v_hat = v / (1.0 - ADAM_B2 ** ADAM_STEP)
    delta = -ADAM_LR * (m_hat / (jnp.sqrt(v_hat) + ADAM_EPS) + ADAM_WD * w)
    return delta, m, v


def _sum_adamw(recv, w, m, v, tr, after):
    nl, ra, cb = w.shape
    assert nl == len(recv) == 2

    def body(r0_ref, r1_ref, w_ref, m_ref, v_ref, after_ref, g_ref, d_ref, nm_ref, nv_ref):
        def total(r_ref):
            g = r_ref[0].astype(F32)
            for j in range(1, N_DEV):
                g = g + r_ref[j].astype(F32)
            return g

        g = jnp.where(pl.program_id(0) == 0, total(r0_ref), total(r1_ref))
        g_ref[...] = g
        d_ref[...], nm_ref[...], nv_ref[...] = _adam_update(w_ref[...], g, m_ref[...], v_ref[...])

    blk = pl.BlockSpec((None, tr, cb), lambda l, i: (l, i, 0))
    return pl.pallas_call(
        body,
        name="sum_adamw",
        grid=(nl, ra // tr),
        in_specs=[pl.BlockSpec((N_DEV, tr, cb), lambda l, i: (0, i * (1 - l), 0)),
                  pl.BlockSpec((N_DEV, tr, cb), lambda l, i: (0, i * l, 0)), blk, blk, blk, ANY_SPEC],
        out_specs=(blk,) * 4,
        out_shape=(jax.ShapeDtypeStruct((nl, ra, cb), F32),) * 4,
        compiler_params=_cparams(("arbitrary", "arbitrary"), 6 * _nbytes((N_DEV, tr, cb), BF16) + 32 * _nbytes((tr, cb), F32)),
    )(recv[0], recv[1], w, m, v, after)


def _adamw(w, g, m, v):
    def body(w_ref, g_ref, m_ref, v_ref, d_ref, nm_ref, nv_ref):
        d_ref[...], nm_ref[...], nv_ref[...] = _adam_update(w_ref[...], g_ref[...], m_ref[...], v_ref[...])

    return pl.pallas_call(
        body,
        name="adamw",
        in_specs=[VMEM_SPEC] * 4,
        out_specs=(VMEM_SPEC,) * 3,
        out_shape=(jax.ShapeDtypeStruct(w.shape, F32),) * 3,
        compiler_params=_cparams(vmem=10 * _nbytes(w.shape, F32)),
    )(w, g, m, v)


def _adamw_tiled(w, g, m, v, tr):
    nl, ra, cb = w.shape

    def body(w_ref, g_ref, m_ref, v_ref, d_ref, nm_ref, nv_ref):
        d_ref[...], nm_ref[...], nv_ref[...] = _adam_update(w_ref[...], g_ref[...], m_ref[...], v_ref[...])

    blk = pl.BlockSpec((None, tr, cb), lambda l, i: (l, i, 0))
    return pl.pallas_call(
        body,
        name="adamw_tiled",
        grid=(nl, ra // tr),
        in_specs=[blk] * 4,
        out_specs=(blk,) * 3,
        out_shape=(jax.ShapeDtypeStruct((nl, ra, cb), F32),) * 3,
        compiler_params=_cparams(("parallel", "parallel")),
    )(*_hbm(w, g, m, v))


def _token_tile(s):
    return min(256, max(SUBLANES * 2, s // 4))


def kernel(x, c, w_mod, b_mod, g_pre_mix, g_post_mix, w_in, conv_a_w, conv_a_b, w_a_out, conv_b_w, conv_b_b, w_gate_r, b_gate_r, w_gate_i, b_gate_i, lru_lambda, w_b_out, w_o, g_pre_mlp, g_post_mlp, w_mlp_up, w_mlp_down, loss_target, m_w_mod, m_b_mod, m_g_pre_mix, m_g_post_mix, m_w_in, m_conv_a_w, m_conv_a_b, m_w_a_out, m_conv_b_w, m_conv_b_b, m_w_gate_r, m_b_gate_r, m_w_gate_i, m_b_gate_i, m_lru_lambda, m_w_b_out, m_w_o, m_g_pre_mlp, m_g_post_mlp, m_w_mlp_up, m_w_mlp_down, v_w_mod, v_b_mod, v_g_pre_mix, v_g_post_mix, v_w_in, v_conv_a_w, v_conv_a_b, v_w_a_out, v_conv_b_w, v_conv_b_b, v_w_gate_r, v_b_gate_r, v_w_gate_i, v_b_gate_i, v_lru_lambda, v_w_b_out, v_w_o, v_g_pre_mlp, v_g_post_mlp, v_w_mlp_up, v_w_mlp_down):
    nl = w_mod.shape[0]
    s, d = x.shape[1], x.shape[2]
    nh, bw = w_gate_r.shape[1], w_gate_r.shape[2]
    cwid = conv_a_w.shape[2]
    tm = _token_tile(s)
    tmx = min(2 * tm, s)
    ts = s
    _, me = _my_position()
    xs = x.reshape(s, d)
    target = loss_target.reshape(s, d)

    vec_names = (g_pre_mix, g_post_mix, conv_a_b, conv_b_b, b_gate_r, b_gate_i, lru_lambda, g_pre_mlp, g_post_mlp)
    rows = jnp.concatenate([jnp.stack(vec_names, axis=1), jnp.zeros((nl, N_ROWS - len(vec_names), d), F32)], axis=1)
    cw = jnp.concatenate([conv_a_w, conv_b_w, jnp.zeros((nl, CW_ROWS - 7, cwid), F32)], axis=1)

    large = {"w_in": w_in, "w_a_out": w_a_out, "w_b_out": w_b_out, "w_o": w_o, "w_mlp_up": w_mlp_up, "w_mlp_down": w_mlp_down}
    groups = (("in", ("w_in",)), ("mix", ("w_a_out", "w_b_out", "w_o")), ("mlp", ("w_mlp_up", "w_mlp_down")))
    mod, cact, cwf = _prep_small(c, w_mod, b_mod, cw)
    mod = mod.reshape(nl, N_MOD, d)
    gathers = {}
    tok = mod
    for l in range(nl):
        for gname, members in groups:
            gathers[l, gname] = _gather2_start([large[n][l].astype(BF16) for n in members], tok, f"gather_start_{gname}{l}")
            tok = gathers[l, gname][4]
    wr = w_gate_r.astype(BF16)
    wi = w_gate_i.astype(BF16)

    forwarded = {}

    def forward(l, gname, after):
        forwarded[l, gname] = _gather2_forward(gathers[l, gname], after, f"gather_forward_{gname}{l}")
        return forwarded[l, gname][4]

    def gathered(l, gname, after):
        return _gather2_wait(forwarded[l, gname], after, f"gather_wait_{gname}{l}")

    saved = []
    weights = []
    xin = xs
    for l in range(nl):
        if l == 0:
            tok = forward(0, "in", tok)
        (win_f,) = gathered(l, "in", tok if l == 0 else xin)
        if l > 0:
            tok = forward(l, "mix", win_f)
        proj, h = _in_proj_fwd(xin, mod, rows, win_f, l, tmx, tok)
        if l == 0:
            tok = forward(0, "mix", proj)
        wa_f, wb_f, wo_f = (w.reshape(d, d) for w in gathered(l, "mix", proj))
        if l > 0:
            tok = forward(l, "mlp", wo_f)
        x1, hs, yap, ybp, y = _mixer_core_fwd(proj, xin, mod, rows, cwf, wr, wi, wa_f, wb_f, wo_f, l, tm, tok)
        if l == 0:
            forward(0, "mlp", x1)
        wup_f, wdn_f = gathered(l, "mlp", x1)
        wdn_f = wdn_f.reshape(-1, d)
        if l + 1 < nl:
            tok = forward(l + 1, "in", wdn_f)
        x2, ru, y2, h2, *last = _mlp_fwd(x1, mod, rows, wup_f, wdn_f, l, tmx, tok, target if l + 1 == nl else None)
        saved.append((xin, proj, h, x1, hs, yap, ybp, y, ru, y2, h2))
        weights.append((win_f, wa_f, wb_f, wo_f, wup_f, wdn_f))
        xin = x2
    dx, loss_part = last[0], last[1][0, 0]

    scatters = {}
    small = [None] * nl
    gate_parts = [None] * nl

    def scatter(l, gname, parts):
        scatters[l, gname] = _exchange_start(parts, False, f"scatter_start_{gname}{l}")
        return scatters[l, gname][4]

    for l in reversed(range(nl)):
        xin, proj, h, x1, hs, yap, ybp, y, ru, y2, h2 = saved[l]
        win_f, wa_f, wb_f, wo_f, wup_f, wdn_f = weights[l]
        dx1, dy2, dup, act, sm_mlp = _mlp_bwd(dx, x1, y2, ru, mod, rows, wup_f, wdn_f, l, tm)
        g_up = _wgrad(h2, dup, True, ts, dx1)
        g_dn = _wgrad(act, dy2, False, ts, g_up)
        tok = scatter(l, "mlp", [g_up, g_dn])
        dproj, dy, m, dyap, dybp, ya, yb, sm_mix, dwg = _mixer_core_bwd(
            dx1, y, yap, ybp, hs, proj, mod, rows, cwf, wr, wi, wa_f, wb_f, wo_f, l, tm // 2, tok)
        gate_parts[l] = _exchange_start([dwg.astype(BF16)], True, f"gates_start{l}")
        g_a = _wgrad(ya, dyap, False, ts, gate_parts[l][4])
        g_b = _wgrad(yb, dybp, False, ts, g_a)
        g_o = _wgrad(m, dy, False, ts, g_b)
        tok = scatter(l, "mix", [g_a, g_b, g_o])
        tok = scatter(l, "in", [_wgrad(h, dproj, True, ts, tok)])
        dx, sm_in = _in_proj_bwd(dproj, xin, dx1, mod, rows, win_f, l, tm, tok)
        small[l] = jnp.concatenate([sm_mlp, sm_mix, sm_in], axis=0)
    grad_x = dx.reshape(x.shape)

    recv = {}
    big = {}
    moments = {"w_in": (w_in, m_w_in, v_w_in), "w_mlp_up": (w_mlp_up, m_w_mlp_up, v_w_mlp_up), "w_a_out": (w_a_out, m_w_a_out, v_w_a_out),
               "w_b_out": (w_b_out, m_w_b_out, v_w_b_out), "w_o": (w_o, m_w_o, v_w_o), "w_mlp_down": (w_mlp_down, m_w_mlp_down, v_w_mlp_down)}

    def collect(l, gname, after):
        for n, zone in zip(dict(groups)[gname], _exchange_wait(scatters[l, gname], after, False, f"scatter_wait_{gname}{l}")):
            recv[n, l] = zone

    def update(name, after):
        w, m_, v_ = moments[name]
        big[name] = _sum_adamw([recv[name, l] for l in range(nl)], w, m_, v_, min(256, w.shape[1]), after)
        return big[name][1]

    for l, gname in ((1, "mlp"), (1, "in"), (1, "mix"), (0, "mlp")):
        collect(l, gname, dx)
    done = update("w_mlp_down", update("w_mlp_up", dx))
    collect(0, "mix", done)
    for name in ("w_a_out", "w_b_out", "w_o"):
        done = update(name, done)
    collect(0, "in", done)
    done = update("w_in", done)

    lrows = jnp.concatenate(small, axis=0)
    lrows = lrows.at[G_LOSS_ROW, 0].set(loss_part)

    def lrow(a, l, r):
        return a[l * G_LAYER_ROWS + r]

    dm = jnp.stack([jnp.concatenate([lrow(lrows, l, G_MLP_ROWS + G_MIX_ROWS + G_IN_SH), lrow(lrows, l, G_MLP_ROWS + G_MIX_ROWS + G_IN_SC),
                                     lrow(lrows, l, G_MLP_ROWS + G_MIX_GT), lrow(lrows, l, G_MLP_SH), lrow(lrows, l, G_MLP_SC),
                                     lrow(lrows, l, G_MLP_GT)]) for l in range(nl)])
    dm8 = jnp.concatenate([dm[:, None, :], jnp.zeros((nl, SUBLANES - 1, N_MOD * d), F32)], axis=1)
    srows, g_w_mod = _reduce_small(lrows, dm8, cact, done)
    loss = srows[G_LOSS_ROW, 0]
    sgates = _sum_gathered([_exchange_wait(gate_parts[l], srows, True, f"gates_wait{l}")[0] for l in range(nl)])

    def srow(l, r):
        return lrow(srows, l, r)

    def per_layer(r):
        return jnp.stack([srow(l, r) for l in range(nl)])

    mix0 = G_MLP_ROWS
    in0 = G_MLP_ROWS + G_MIX_ROWS
    g_b_mod = jnp.stack([jnp.concatenate([srow(l, in0 + G_IN_SH), srow(l, in0 + G_IN_SC), srow(l, mix0 + G_MIX_GT),
                                          srow(l, G_MLP_SH), srow(l, G_MLP_SC), srow(l, G_MLP_GT)]) for l in range(nl)])
    conv_a_full = jnp.stack([jnp.stack([srow(l, mix0 + G_MIX_CAW + k) for k in range(3)]) for l in range(nl)])
    conv_b_full = jnp.stack([jnp.stack([srow(l, mix0 + G_MIX_CBW + k) for k in range(4)]) for l in range(nl)])
    grads = {
        "b_mod": g_b_mod,
        "g_pre_mix": per_layer(in0 + G_IN_GPRE),
        "g_post_mix": per_layer(mix0 + G_MIX_GPOST),
        "conv_a_w": lax.dynamic_slice_in_dim(conv_a_full, me * cwid, cwid, axis=2),
        "conv_a_b": per_layer(mix0 + G_MIX_CAB),
        "conv_b_w": lax.dynamic_slice_in_dim(conv_b_full, me * cwid, cwid, axis=2),
        "conv_b_b": per_layer(mix0 + G_MIX_CBB),
        "w_gate_r": sgates[:, 0],
        "b_gate_r": per_layer(mix0 + G_MIX_BR),
        "w_gate_i": sgates[:, 1],
        "b_gate_i": per_layer(mix0 + G_MIX_BI),
        "lru_lambda": per_layer(mix0 + G_MIX_LAM),
        "g_pre_mlp": per_layer(G_MLP_GPRE),
        "g_post_mlp": per_layer(G_MLP_GPOST),
    }
    params = {
        "b_mod": (b_mod, m_b_mod, v_b_mod), "g_pre_mix": (g_pre_mix, m_g_pre_mix, v_g_pre_mix), "g_post_mix": (g_post_mix, m_g_post_mix, v_g_post_mix),
        "conv_a_w": (conv_a_w, m_conv_a_w, v_conv_a_w), "conv_a_b": (conv_a_b, m_conv_a_b, v_conv_a_b),
        "conv_b_w": (conv_b_w, m_conv_b_w, v_conv_b_w), "conv_b_b": (conv_b_b, m_conv_b_b, v_conv_b_b),
        "w_gate_r": (w_gate_r, m_w_gate_r, v_w_gate_r), "b_gate_r": (b_gate_r, m_b_gate_r, v_b_gate_r),
        "w_gate_i": (w_gate_i, m_w_gate_i, v_w_gate_i), "b_gate_i": (b_gate_i, m_b_gate_i, v_b_gate_i),
        "lru_lambda": (lru_lambda, m_lru_lambda, v_lru_lambda), "g_pre_mlp": (g_pre_mlp, m_g_pre_mlp, v_g_pre_mlp),
        "g_post_mlp": (g_post_mlp, m_g_post_mlp, v_g_post_mlp),
    }
    out = {}
    for name, g in grads.items():
        w, m_, v_ = params[name]
        flat = (-1, w.shape[-1])
        dl, nm, nv = _adamw(w.reshape(flat), g.reshape(flat), m_.reshape(flat), v_.reshape(flat))
        out[name] = (g.reshape(w.shape), dl.reshape(w.shape), nm.reshape(w.shape), nv.reshape(w.shape))
    out["w_mod"] = (g_w_mod,) + tuple(_adamw_tiled(w_mod, g_w_mod, m_w_mod, v_w_mod, min(128, d)))
    out.update(big)

    order = ("w_mod", "b_mod", "g_pre_mix", "g_post_mix", "w_in", "conv_a_w", "conv_a_b", "w_a_out", "conv_b_w", "conv_b_b", "w_gate_r", "b_gate_r",
             "w_gate_i", "b_gate_i", "lru_lambda", "w_b_out", "w_o", "g_pre_mlp", "g_post_mlp", "w_mlp_up", "w_mlp_down")
    return (loss, grad_x) + tuple(out[n][0] for n in order) + tuple(out[n][1] for n in order) + tuple(out[n][2] for n in order) + tuple(out[n][3] for n in order)
```

```python
import functools

import jax
import jax.numpy as jnp
from jax import lax
from jax.experimental import pallas as pl
from jax.experimental.pallas import tpu as pltpu

F32, BF16 = jnp.float32, jnp.bfloat16
EPS = 1e-6
LRU_C = 8.0
N_DEV = 8
N_MOD = 6
SUBLANES = 8
VMEM_BUDGET = 56 * 1024 * 1024
WGRAD_ROWS = 512
ADAM_LR, ADAM_B1, ADAM_B2, ADAM_EPS, ADAM_WD, ADAM_STEP = 0.001, 0.9, 0.999, 1e-08, 0.01, 10
MESH = pl.DeviceIdType.MESH
VMEM_SPEC = pl.BlockSpec(memory_space=pltpu.VMEM)
ANY_SPEC = pl.BlockSpec(memory_space=pl.ANY)
HBM_SPEC = pl.BlockSpec(memory_space=pltpu.HBM)
SEM_SPEC = pl.BlockSpec(memory_space=pltpu.SEMAPHORE)
SIDE_EFFECT = pltpu.SideEffectType.DATAFLOW_SIDE_EFFECTING

R_G_PRE_MIX, R_G_POST_MIX, R_CONV_A_B, R_CONV_B_B, R_B_GATE_R, R_B_GATE_I, R_LAMBDA, R_G_PRE_MLP, R_G_POST_MLP = range(9)
N_ROWS = 16
M_SH_M, M_SC_M, M_GT_M, M_SH_F, M_SC_F, M_GT_F = range(6)
CW_A, CW_B, CW_ROWS = 0, 3, 8
G_MLP_GT, G_MLP_GPOST, G_MLP_SC, G_MLP_SH, G_MLP_GPRE, G_LOSS_ROW, G_MLP_ROWS = 0, 1, 2, 3, 4, 7, 8
(G_MIX_GT, G_MIX_GPOST, G_MIX_CAB, G_MIX_CAW, G_MIX_CBB, G_MIX_CBW, G_MIX_BR, G_MIX_BI, G_MIX_LAM) = 0, 1, 2, 3, 6, 7, 11, 12, 13
G_MIX_ROWS = 16
G_IN_SC, G_IN_SH, G_IN_GPRE, G_IN_ROWS = 0, 1, 2, 8
G_LAYER_ROWS = G_MLP_ROWS + G_MIX_ROWS + G_IN_ROWS


def _cparams(dims=None, vmem=None):
    kw = {}
    if dims is not None:
        kw["dimension_semantics"] = dims
    if vmem is not None:
        kw["vmem_limit_bytes"] = int(min(max(vmem, 16 * 1024 * 1024), VMEM_BUDGET))
    return pltpu.CompilerParams(**kw)


def _nbytes(shape, dtype):
    n = 1
    for s in shape:
        n *= s
    return n * jnp.dtype(dtype).itemsize


def _hbm(*arrays):
    return tuple(pltpu.with_memory_space_constraint(a, pltpu.HBM) for a in arrays)


def _resident(block, index_map):
    return pl.BlockSpec(block, index_map, pipeline_mode=pl.Buffered(1))


def _my_position():
    x, y, c = lax.axis_index("x"), lax.axis_index("y"), lax.axis_index("c")
    return (x, y, c), 4 * x + 2 * y + c


def _peer(pos, k):
    x, y, c = pos
    px = 1 - x if k & 4 else x
    py = 1 - y if k & 2 else y
    pc = 1 - c if k & 1 else c
    return (px, py, pc), 4 * px + 2 * py + pc


def _remote(src, dst, ssem, rsem, peer):
    return pltpu.make_async_remote_copy(src_ref=src, dst_ref=dst, send_sem=ssem, recv_sem=rsem, device_id=peer, device_id_type=MESH)


def _dot(a, b):
    return jnp.dot(a, b, preferred_element_type=F32)


def _dot_nt(a, b):
    return lax.dot_general(a, b, (((1,), (1,)), ((), ())), preferred_element_type=F32)


def _dot_tn(a, b):
    return lax.dot_general(a, b, (((0,), (0,)), ((), ())), preferred_element_type=F32)


def _colsum(v):
    return jnp.sum(v, axis=0, keepdims=True)


def _sigmoid(v):
    return jax.nn.sigmoid(v)


GELU_K, GELU_C = 0.7978845608028654, 0.044715


def _gelu(v):
    s = 1.0 / (1.0 + jnp.exp(v * (-2.0 * GELU_K - (2.0 * GELU_K * GELU_C) * (v * v))))
    return v * s, s


def _gelu_grad(v, s):
    return s * (1.0 + (v * (1.0 - s)) * (2.0 * GELU_K + (6.0 * GELU_K * GELU_C) * (v * v)))


def _neg_expm1_twice(v):
    t = jnp.tanh(v)
    return (-2.0 * t) / (1.0 - t)


def _softplus_neg(lam):
    z = -lam
    u = jnp.exp(-jnp.abs(z))
    w = 1.0 + u
    l1p = jnp.where(w == 1.0, u, jnp.log(w) * u / (w - 1.0))
    return jnp.maximum(z, 0.0) + l1p


def _rms(v):
    return lax.rsqrt(jnp.mean(v * v, axis=-1, keepdims=True) + EPS)


def _prenorm_bwd(xv, dh, g, sc):
    r = _rms(xv)
    xn = xv * r
    n = xn * g
    dsc = _colsum(dh * n)
    dsh = _colsum(dh)
    dn = dh * (1.0 + sc)
    dg = _colsum(dn * xn)
    dxn = dn * g
    dx = r * (dxn - xn * jnp.mean(dxn * xn, axis=-1, keepdims=True))
    return dx, dsc, dsh, dg


def _postnorm_bwd(yv, dout, g, gt):
    r = _rms(yv)
    yn = yv * r
    dgt = _colsum(dout * (yn * g))
    dn = dout * gt
    dg = _colsum(dn * yn)
    dyn = dn * g
    dy = r * (dyn - yn * jnp.mean(dyn * yn, axis=-1, keepdims=True))
    return dy, dgt, dg


def _gates(xc, wr_ref, wi_ref, b_r, b_i, sp, nh, bw):
    xcb = xc.astype(BF16)
    zr = jnp.concatenate([_dot(xcb[:, h * bw:(h + 1) * bw], wr_ref[h]) for h in range(nh)], axis=1) + b_r
    zi = jnp.concatenate([_dot(xcb[:, h * bw:(h + 1) * bw], wi_ref[h]) for h in range(nh)], axis=1) + b_i
    r = _sigmoid(zr)
    ig = _sigmoid(zi)
    la = (-LRU_C * r) * sp
    a = jnp.exp(la)
    mult = jnp.sqrt(_neg_expm1_twice(la))
    return xcb, r, ig, a, mult


def _shift_rows(cur, edge, k, up):
    t, dd = cur.shape
    blocks = cur.reshape(t // SUBLANES, SUBLANES, dd)
    row = lax.broadcasted_iota(jnp.int32, (1, SUBLANES, dd), 1)
    if up:
        r = pltpu.roll(blocks, SUBLANES - k, 1)
        nxt = jnp.concatenate([r[1:], pltpu.roll(edge, SUBLANES - k, 0)[None]], axis=0)
        out = jnp.where(row >= SUBLANES - k, nxt, r)
    else:
        r = pltpu.roll(blocks, k, 1)
        prv = jnp.concatenate([pltpu.roll(edge, k, 0)[None], r[:-1]], axis=0)
        out = jnp.where(row < k, prv, r)
    return out.reshape(t, dd)


def _scan_block(a8, b8, reverse):
    row = lax.broadcasted_iota(jnp.int32, a8.shape, 0)
    for s in (1, 2, 4):
        if reverse:
            keep = row < SUBLANES - s
            a_sh = pltpu.roll(a8, SUBLANES - s, 0)
            b_sh = pltpu.roll(b8, SUBLANES - s, 0)
        else:
            keep = row >= s
            a_sh = pltpu.roll(a8, s, 0)
            b_sh = pltpu.roll(b8, s, 0)
        b8 = b8 + a8 * jnp.where(keep, b_sh, 0.0)
        a8 = a8 * jnp.where(keep, a_sh, 1.0)
    return a8, b8


def _prep_small(c, w_mod, b_mod, cw):
    d = c.shape[1]
    cm = w_mod.shape[2]
    cwid = cw.shape[2]
    nl = w_mod.shape[0]

    def body(c_ref, wm_ref, bm_ref, cw_ref, mod_ref, cact_ref, cwf_ref, cbuf, pbuf, rbuf, ssem, rsem, lsem):
        pos, me = _my_position()
        me8 = pl.multiple_of(me * SUBLANES, SUBLANES)
        cbuf[pl.ds(me8, SUBLANES), :] = jnp.broadcast_to(c_ref[...], (SUBLANES, d))
        own_cw = pltpu.make_async_copy(cw_ref, cwf_ref.at[:, :, pl.ds(me * cwid, cwid)], lsem.at[0])
        own_cw.start()
        first = []
        for k in range(1, N_DEV):
            peer, _ = _peer(pos, k)
            rows = cbuf.at[pl.ds(me8, SUBLANES), :]
            first.append(_remote(rows, rows, ssem.at[0, k - 1], rsem.at[0, k - 1], peer))
            first.append(_remote(cw_ref, cwf_ref.at[:, :, pl.ds(me * cwid, cwid)], ssem.at[1, k - 1], rsem.at[1, k - 1], peer))
        for cp in first:
            cp.start()
        for k in range(1, N_DEV):
            peer, pj = _peer(pos, k)
            pj8 = pl.multiple_of(pj * SUBLANES, SUBLANES)
            rows = cbuf.at[pl.ds(pj8, SUBLANES), :]
            _remote(rows, rows, ssem.at[0, k - 1], rsem.at[0, k - 1], peer).wait_recv()
        cv = cbuf[...]
        cact = cv * _sigmoid(cv)
        cact_ref[...] = cact
        cb = cact.astype(BF16)
        for l in range(nl):
            pbuf[l] = _dot(cb, wm_ref[l].astype(BF16))
        own_p = pltpu.make_async_copy(pbuf.at[:, pl.ds(me8, SUBLANES), :], rbuf.at[me], lsem.at[1])
        own_p.start()
        second = []
        for k in range(1, N_DEV):
            peer, pj = _peer(pos, k)
            pj8 = pl.multiple_of(pj * SUBLANES, SUBLANES)
            second.append(_remote(pbuf.at[:, pl.ds(pj8, SUBLANES), :], rbuf.at[me], ssem.at[2, k - 1], rsem.at[2, k - 1], peer))
        for cp in second:
            cp.start()
        for k in range(1, N_DEV):
            peer, pj = _peer(pos, k)
            _remote(pbuf.at[:, pl.ds(0, SUBLANES), :], rbuf.at[pj], ssem.at[2, k - 1], rsem.at[2, k - 1], peer).wait_recv()
            _remote(cw_ref, cwf_ref.at[:, :, pl.ds(pj * cwid, cwid)], ssem.at[1, k - 1], rsem.at[1, k - 1], peer).wait_recv()
        own_p.wait()
        own_cw.wait()
        for l in range(nl):
            for j in range(N_DEV):
                mod_ref[l:l + 1, j * cm:(j + 1) * cm] = rbuf[j, l, 0:1, :] + bm_ref[l:l + 1, j * cm:(j + 1) * cm]
        for cp in first + second:
            cp.wait_send()

    return pl.pallas_call(
        body,
        name="prep_small",
        out_shape=(
            jax.ShapeDtypeStruct((nl, N_MOD * d), F32),
            jax.ShapeDtypeStruct((N_DEV * SUBLANES, d), F32),
            jax.ShapeDtypeStruct((nl, CW_ROWS, d), F32),
        ),
        in_specs=[VMEM_SPEC] * 4,
        out_specs=(VMEM_SPEC,) * 3,
        scratch_shapes=[
            pltpu.VMEM((N_DEV * SUBLANES, d), F32),
            pltpu.VMEM((nl, N_DEV * SUBLANES, cm), F32),
            pltpu.VMEM((N_DEV, nl, SUBLANES, cm), F32),
            pltpu.SemaphoreType.DMA((3, N_DEV - 1)),
            pltpu.SemaphoreType.DMA((3, N_DEV - 1)),
            pltpu.SemaphoreType.DMA((2,)),
        ],
        compiler_params=_cparams(vmem=3 * _nbytes(w_mod.shape, F32)),
    )(c, w_mod, b_mod, cw)


def _exchange_start(parts, gather, name):
    n = len(parts)
    lands = [lax.empty(((N_DEV,) + tuple(p.shape)) if gather else tuple(p.shape), p.dtype) for p in parts]

    def body(*refs):
        ins, lnd = refs[:n], refs[n:2 * n]
        ssem, rsem, token = refs[2 * n], refs[2 * n + 1], refs[-1]
        pos, me = _my_position()
        for k in range(1, N_DEV):
            peer, pj = _peer(pos, k)
            for t in range(n):
                src = ins[t] if gather else ins[t].at[pj]
                q = t * (N_DEV - 1) + k - 1
                _remote(src, lnd[t].at[me], ssem.at[q], rsem.at[q], peer).start()
        token[...] = jnp.zeros(token.shape, F32)

    out = pl.pallas_call(
        body,
        name=name,
        out_shape=(pltpu.SemaphoreType.DMA((n * (N_DEV - 1),)), pltpu.SemaphoreType.DMA((n * (N_DEV - 1),)))
        + tuple(pltpu.HBM(p.shape, p.dtype) for p in parts) + tuple(pltpu.HBM(p.shape, p.dtype) for p in lands)
        + (jax.ShapeDtypeStruct((SUBLANES, 128), F32),),
        in_specs=[HBM_SPEC] * (2 * n),
        out_specs=(SEM_SPEC, SEM_SPEC) + (HBM_SPEC,) * (2 * n) + (VMEM_SPEC,),
        input_output_aliases={i: 2 + i for i in range(2 * n)},
        compiler_params=pltpu.CompilerParams(has_side_effects=SIDE_EFFECT),
    )(*[pltpu.with_memory_space_constraint(p, pltpu.HBM) for p in list(parts) + lands])
    return out[0], out[1], list(out[2:2 + n]), list(out[2 + n:2 + 2 * n]), out[-1]


def _exchange_wait(started, after, gather, name):
    ssem, rsem, parts, lands, _ = started
    n = len(parts)

    def body(*refs):
        ins, lnd = refs[:n], refs[n:2 * n]
        ssem_ref, rsem_ref = refs[2 * n], refs[2 * n + 1]
        stage, lsem = refs[-1 - n:-1], refs[-1]
        pos, me = _my_position()
        load = []
        for t in range(n):
            src = ins[t] if gather else ins[t].at[me]
            load.append(pltpu.make_async_copy(src, stage[t], lsem.at[t]))
            load[-1].start()
        store = []
        for t in range(n):
            load[t].wait()
            store.append(pltpu.make_async_copy(stage[t], lnd[t].at[me], lsem.at[t]))
            store[-1].start()
        for k in range(1, N_DEV):
            peer, pj = _peer(pos, k)
            for t in range(n):
                src = ins[t] if gather else ins[t].at[pj]
                q = t * (N_DEV - 1) + k - 1
                _remote(src, lnd[t].at[me], ssem_ref.at[q], rsem_ref.at[q], peer).wait_send()
                _remote(src, lnd[t].at[pj], ssem_ref.at[q], rsem_ref.at[q], peer).wait_recv()
        for cp in store:
            cp.wait()

    out = pl.pallas_call(
        body,
        name=name,
        out_shape=tuple(pltpu.HBM(p.shape, p.dtype) for p in parts) + tuple(pltpu.HBM(p.shape, p.dtype) for p in lands),
        in_specs=[HBM_SPEC] * (2 * n) + [SEM_SPEC, SEM_SPEC, ANY_SPEC],
        out_specs=(HBM_SPEC,) * (2 * n),
        input_output_aliases={i: i for i in range(2 * n)},
        scratch_shapes=[pltpu.VMEM(tuple(z.shape[1:]), z.dtype) for z in lands] + [pltpu.SemaphoreType.DMA((n,))],
        compiler_params=pltpu.CompilerParams(has_side_effects=SIDE_EFFECT),
    )(*parts, *lands, ssem, rsem, after)
    return list(out[n:])


CHIP_PEERS = 3


def _chip_peer(pos, k, core):
    x, y, _ = pos
    px = 1 - x if k & 2 else x
    py = 1 - y if k & 1 else y
    return (px, py, core), 4 * px + 2 * py + core


def _gather2_start(parts, after, name):
    n = len(parts)
    lands = [lax.empty((N_DEV,) + tuple(p.shape), p.dtype) for p in parts]
    per = CHIP_PEERS + 1

    def body(*refs):
        ins, lnd = refs[:n], refs[n:2 * n]
        ssem, rsem, token = refs[2 * n + 1], refs[2 * n + 2], refs[-1]
        pos, me = _my_position()
        sibling = (pos[0], pos[1], 1 - pos[2])
        for t in range(n):
            _remote(ins[t], lnd[t].at[me], ssem.at[per * t], rsem.at[per * t], sibling).start()
        for k in range(1, per):
            peer, _ = _chip_peer(pos, k, pos[2])
            for t in range(n):
                _remote(ins[t], lnd[t].at[me], ssem.at[per * t + k], rsem.at[per * t + k], peer).start()
        token[...] = jnp.zeros(token.shape, F32)

    out = pl.pallas_call(
        body,
        name=name,
        out_shape=(pltpu.SemaphoreType.DMA((n * per,)), pltpu.SemaphoreType.DMA((n * per,)))
        + tuple(pltpu.HBM(p.shape, p.dtype) for p in parts) + tuple(pltpu.HBM(p.shape, p.dtype) for p in lands)
        + (jax.ShapeDtypeStruct((SUBLANES, 128), F32),),
        in_specs=[HBM_SPEC] * (2 * n) + [ANY_SPEC],
        out_specs=(SEM_SPEC, SEM_SPEC) + (HBM_SPEC,) * (2 * n) + (VMEM_SPEC,),
        input_output_aliases={i: 2 + i for i in range(2 * n)},
        compiler_params=pltpu.CompilerParams(has_side_effects=SIDE_EFFECT),
    )(*[pltpu.with_memory_space_constraint(p, pltpu.HBM) for p in list(parts) + lands], after)
    return out[0], out[1], list(out[2:2 + n]), list(out[2 + n:2 + 2 * n]), out[-1]


def _gather2_forward(started, after, name):
    ssem, rsem, parts, lands, _ = started
    n = len(lands)
    per = CHIP_PEERS + 1

    def body(*refs):
        lnd = refs[:n]
        rsem_a, fsend, frecv, token = refs[n], refs[2 * n + 2], refs[2 * n + 3], refs[2 * n + 4]
        pos, _ = _my_position()
        sibling = (pos[0], pos[1], 1 - pos[2])
        for k in range(1, per):
            peer, pk = _chip_peer(pos, k, pos[2])
            for t in range(n):
                block = lnd[t].at[pk]
                _remote(block, block, rsem_a.at[per * t + k], rsem_a.at[per * t + k], peer).wait_recv()
                q = CHIP_PEERS * t + k - 1
                _remote(block, block, fsend.at[q], frecv.at[q], sibling).start()
        token[...] = jnp.zeros(token.shape, F32)

    out = pl.pallas_call(
        body,
        name=name,
        out_shape=tuple(pltpu.HBM(p.shape, p.dtype) for p in lands)
        + (pltpu.SemaphoreType.DMA((n * CHIP_PEERS,)), pltpu.SemaphoreType.DMA((n * CHIP_PEERS,)),
           jax.ShapeDtypeStruct((SUBLANES, 128), F32)),
        in_specs=[HBM_SPEC] * n + [SEM_SPEC, ANY_SPEC],
        out_specs=(HBM_SPEC,) * n + (SEM_SPEC, SEM_SPEC, VMEM_SPEC),
        input_output_aliases={i: i for i in range(n)},
        compiler_params=pltpu.CompilerParams(has_side_effects=SIDE_EFFECT),
    )(*lands, rsem, after)
    return ssem, rsem, parts, list(out[:n]), out[n + 2], out[n], out[n + 1]


def _gather2_wait(forwarded, after, name):
    ssem, rsem, parts, lands, _, fsend, frecv = forwarded
    n = len(parts)
    per = CHIP_PEERS + 1

    def body(*refs):
        ins, lnd = refs[:n], refs[n:2 * n]
        ssem_a, rsem_a, fs, fr = refs[2 * n:2 * n + 4]
        stage, lsem = refs[-1 - n:-1], refs[-1]
        pos, me = _my_position()
        sibling = (pos[0], pos[1], 1 - pos[2])
        sib = 4 * pos[0] + 2 * pos[1] + 1 - pos[2]
        load = []
        for t in range(n):
            load.append(pltpu.make_async_copy(ins[t], stage[t], lsem.at[t]))
            load[-1].start()
        store = []
        for t in range(n):
            load[t].wait()
            store.append(pltpu.make_async_copy(stage[t], lnd[t].at[me], lsem.at[t]))
            store[-1].start()
        for t in range(n):
            _remote(ins[t], lnd[t].at[me], ssem_a.at[per * t], rsem_a.at[per * t], sibling).wait_send()
            _remote(ins[t], lnd[t].at[sib], ssem_a.at[per * t], rsem_a.at[per * t], sibling).wait_recv()
        for k in range(1, per):
            peer, pk = _chip_peer(pos, k, pos[2])
            _, qk = _chip_peer(pos, k, 1 - pos[2])
            for t in range(n):
                q = CHIP_PEERS * t + k - 1
                _remote(ins[t], lnd[t].at[me], ssem_a.at[per * t + k], rsem_a.at[per * t + k], peer).wait_send()
                _remote(lnd[t].at[pk], lnd[t].at[pk], fs.at[q], fr.at[q], sibling).wait_send()
                _remote(lnd[t].at[qk], lnd[t].at[qk], fs.at[q], fr.at[q], sibling).wait_recv()
        for cp in store:
            cp.wait()

    out = pl.pallas_call(
        body,
        name=name,
        out_shape=tuple(pltpu.HBM(p.shape, p.dtype) for p in parts) + tuple(pltpu.HBM(p.shape, p.dtype) for p in lands),
        in_specs=[HBM_SPEC] * (2 * n) + [SEM_SPEC] * 4 + [ANY_SPEC],
        out_specs=(HBM_SPEC,) * (2 * n),
        input_output_aliases={i: i for i in range(2 * n)},
        scratch_shapes=[pltpu.VMEM(tuple(z.shape[1:]), z.dtype) for z in lands] + [pltpu.SemaphoreType.DMA((n,))],
        compiler_params=pltpu.CompilerParams(has_side_effects=SIDE_EFFECT),
    )(*parts, *lands, ssem, rsem, fsend, frecv, after)
    return list(out[n:])


def _reduce_small(rows, dm8, cact, after):
    r, d = rows.shape
    nl = dm8.shape[0]
    cm = dm8.shape[2] // N_DEV

    def body(rows_ref, dm_ref, cact_ref, after_ref, orow_ref, owm_ref, gr, dmr, ssem, rsem, lsem):
        pos, me = _my_position()
        me8 = pl.multiple_of(me * SUBLANES, SUBLANES)
        gr[me] = rows_ref[...]
        own_dm = pltpu.make_async_copy(dm_ref.at[:, :, pl.ds(me * cm, cm)], dmr.at[:, pl.ds(me8, SUBLANES), :], lsem.at[0])
        own_dm.start()
        sends = []
        for k in range(1, N_DEV):
            peer, pj = _peer(pos, k)
            sends.append(_remote(gr.at[me], gr.at[me], ssem.at[0, k - 1], rsem.at[0, k - 1], peer))
            sends.append(_remote(dm_ref.at[:, :, pl.ds(pj * cm, cm)], dmr.at[:, pl.ds(me8, SUBLANES), :],
                                 ssem.at[1, k - 1], rsem.at[1, k - 1], peer))
        for cp in sends:
            cp.start()
        for k in range(1, N_DEV):
            peer, pj = _peer(pos, k)
            pj8 = pl.multiple_of(pj * SUBLANES, SUBLANES)
            _remote(gr.at[pj], gr.at[pj], ssem.at[0, k - 1], rsem.at[0, k - 1], peer).wait_recv()
            _remote(dm_ref.at[:, :, pl.ds(0, cm)], dmr.at[:, pl.ds(pj8, SUBLANES), :], ssem.at[1, k - 1], rsem.at[1, k - 1], peer).wait_recv()
        own_dm.wait()
        acc = gr[0]
        for j in range(1, N_DEV):
            acc = acc + gr[j]
        orow_ref[...] = acc
        cb = cact_ref[...].astype(BF16)
        for l in range(nl):
            owm_ref[l] = _dot_tn(cb, dmr[l].astype(BF16))
        for cp in sends:
            cp.wait_send()

    return pl.pallas_call(
        body,
        name="reduce_small",
        out_shape=(jax.ShapeDtypeStruct((r, d), F32), jax.ShapeDtypeStruct((nl, d, cm), F32)),
        in_specs=[VMEM_SPEC] * 3 + [ANY_SPEC],
        out_specs=(VMEM_SPEC,) * 2,
        scratch_shapes=[
            pltpu.VMEM((N_DEV, r, d), F32),
            pltpu.VMEM((nl, N_DEV * SUBLANES, cm), F32),
            pltpu.SemaphoreType.DMA((2, N_DEV - 1)),
            pltpu.SemaphoreType.DMA((2, N_DEV - 1)),
            pltpu.SemaphoreType.DMA((1,)),
        ],
        compiler_params=_cparams(vmem=4 * _nbytes((N_DEV, r, d), F32) + 6 * _nbytes((nl, d, cm), F32)),
    )(rows, dm8, cact, after)


def _sum_gathered(zones):
    nl = len(zones)

    def body(*refs):
        for l in range(nl):
            acc = refs[l][0].astype(F32)
            for j in range(1, N_DEV):
                acc = acc + refs[l][j].astype(F32)
            refs[nl][l] = acc

    return pl.pallas_call(
        body,
        name="sum_gathered",
        out_shape=jax.ShapeDtypeStruct((nl,) + tuple(zones[0].shape[1:]), F32),
        in_specs=[VMEM_SPEC] * nl,
        out_specs=VMEM_SPEC,
        compiler_params=_cparams(vmem=8 * nl * _nbytes(zones[0].shape, BF16)),
    )(*zones)


def _in_proj_fwd(x, mod, rows, win_f, l, tm, after):
    s, d = x.shape
    nb, _, ci = win_f.shape

    def body(x_ref, mod_ref, rows_ref, w_ref, after_ref, proj_ref, h_ref):
        xv = x_ref[...]
        g = rows_ref[R_G_PRE_MIX:R_G_PRE_MIX + 1, :]
        h = (xv * _rms(xv) * g) * (1.0 + mod_ref[M_SC_M:M_SC_M + 1, :]) + mod_ref[M_SH_M:M_SH_M + 1, :]
        hb = h.astype(BF16)
        h_ref[...] = hb
        for j in range(nb):
            proj_ref[:, j * ci:(j + 1) * ci] = _dot(hb, w_ref[j])

    return pl.pallas_call(
        body,
        name="in_proj_fwd",
        grid=(s // tm,),
        in_specs=[
            pl.BlockSpec((tm, d), lambda i: (i, 0)),
            _resident((None, N_MOD, d), lambda i: (l, 0, 0)),
            _resident((None, N_ROWS, d), lambda i: (l, 0, 0)),
            _resident((nb, d, ci), lambda i: (0, 0, 0)),
            ANY_SPEC,
        ],
        out_specs=(pl.BlockSpec((tm, nb * ci), lambda i: (i, 0)), pl.BlockSpec((tm, d), lambda i: (i, 0))),
        out_shape=(jax.ShapeDtypeStruct((s, nb * ci), F32), jax.ShapeDtypeStruct((s, d), BF16)),
        compiler_params=_cparams(("parallel",), _nbytes((nb, d, ci), BF16) + 3 * _nbytes((tm, nb * ci), F32) + 8 * _nbytes((tm, d), F32)),
    )(*_hbm(x), mod, rows, win_f, after)


def _mixer_core_fwd(proj, x, mod, rows, cwf, wr, wi, wa_f, wb_f, wo_f, l, tm, after):
    s, d = x.shape
    nh, bw, _ = wr.shape[1:]

    def body(proj_ref, x_ref, mod_ref, rows_ref, cw_ref, wr_ref, wi_ref, wa_ref, wb_ref, wo_ref, after_ref,
             x1_ref, hs_ref, yap_ref, ybp_ref, y_ref, cvbuf, xbbuf, a_s, b_s, hprev):
        i = pl.program_id(0)

        @pl.when(i == 0)
        def _():
            cvbuf[...] = jnp.zeros((SUBLANES, d), F32)
            xbbuf[...] = jnp.zeros((SUBLANES, d), F32)
            hprev[...] = jnp.zeros((SUBLANES, d), F32)

        def row(r):
            return rows_ref[r:r + 1, :]

        def tap(r):
            return cw_ref[r:r + 1, :]

        ba = proj_ref[:, 0:d]
        cv = proj_ref[:, d:2 * d] * proj_ref[:, 2 * d:3 * d]
        cvt = cvbuf[...]
        conv3 = ((row(R_CONV_A_B) + _shift_rows(cv, cvt, 2, False) * tap(CW_A)) + _shift_rows(cv, cvt, 1, False) * tap(CW_A + 1)) + cv * tap(CW_A + 2)
        ya = ba * conv3
        cvbuf[...] = cv[tm - SUBLANES:, :]
        xb = proj_ref[:, 3 * d:4 * d]
        xbt = xbbuf[...]
        xc = (((row(R_CONV_B_B) + _shift_rows(xb, xbt, 3, False) * tap(CW_B)) + _shift_rows(xb, xbt, 2, False) * tap(CW_B + 1))
              + _shift_rows(xb, xbt, 1, False) * tap(CW_B + 2)) + xb * tap(CW_B + 3)
        xbbuf[...] = xb[tm - SUBLANES:, :]
        sp = _softplus_neg(row(R_LAMBDA))
        _, _, ig, a, mult = _gates(xc, wr_ref, wi_ref, row(R_B_GATE_R), row(R_B_GATE_I), sp, nh, bw)
        a_s[...] = a
        b_s[...] = mult * (ig * xc)

        def blk(j, hp):
            o = pl.multiple_of(j * SUBLANES, SUBLANES)
            a8, b8 = _scan_block(a_s[pl.ds(o, SUBLANES), :], b_s[pl.ds(o, SUBLANES), :], reverse=False)
            h8 = b8 + a8 * hp
            hs_ref[pl.ds(o, SUBLANES), :] = h8
            return jnp.broadcast_to(h8[SUBLANES - 1:SUBLANES, :], (SUBLANES, d))

        hprev[...] = lax.fori_loop(0, tm // SUBLANES, blk, hprev[...])
        gel, _ = _gelu(proj_ref[:, 4 * d:5 * d])
        yb = hs_ref[...] * gel
        yap = _dot(ya.astype(BF16), wa_ref[...])
        ybp = _dot(yb.astype(BF16), wb_ref[...])
        yap_ref[...] = yap
        ybp_ref[...] = ybp
        m = _sigmoid(proj_ref[:, 5 * d:6 * d]) * yap + _sigmoid(proj_ref[:, 6 * d:7 * d]) * ybp
        y = _dot(m.astype(BF16), wo_ref[...])
        y_ref[...] = y
        x1_ref[...] = x_ref[...] + mod_ref[M_GT_M:M_GT_M + 1, :] * ((y * _rms(y)) * row(R_G_POST_MIX))

    tile = pl.BlockSpec((tm, d), lambda i: (i, 0))
    return pl.pallas_call(
        body,
        name="mixer_core_fwd",
        grid=(s // tm,),
        in_specs=[
            pl.BlockSpec((tm, 7 * d), lambda i: (i, 0)),
            tile,
            _resident((None, N_MOD, d), lambda i: (l, 0, 0)),
            _resident((None, N_ROWS, d), lambda i: (l, 0, 0)),
            _resident((None, CW_ROWS, d), lambda i: (l, 0, 0)),
            _resident((None, nh, bw, bw), lambda i: (l, 0, 0, 0)),
            _resident((None, nh, bw, bw), lambda i: (l, 0, 0, 0)),
            _resident((d, d), lambda i: (0, 0)),
            _resident((d, d), lambda i: (0, 0)),
            _resident((d, d), lambda i: (0, 0)),
            ANY_SPEC,
        ],
        out_specs=(tile,) * 5,
        out_shape=(jax.ShapeDtypeStruct((s, d), F32),) * 5,
        scratch_shapes=[
            pltpu.VMEM((SUBLANES, d), F32),
            pltpu.VMEM((SUBLANES, d), F32),
            pltpu.VMEM((tm, d), F32),
            pltpu.VMEM((tm, d), F32),
            pltpu.VMEM((SUBLANES, d), F32),
        ],
        compiler_params=_cparams(("arbitrary",), 3 * _nbytes((d, d), BF16) + 2 * _nbytes((tm, 7 * d), F32) + 40 * _nbytes((tm, d), F32)),
    )(*_hbm(proj, x), mod, rows, cwf, wr, wi, wa_f, wb_f, wo_f, after)


def _mlp_fwd(x1, mod, rows, wup_f, wdn_f, l, tm, after, target=None):
    s, d = x1.shape
    nb, _, cu = wup_f.shape
    dff = nb * cu
    with_loss = target is not None

    def body(x1_ref, mod_ref, rows_ref, wu_ref, wd_ref, after_ref, *refs):
        x2_ref, ru_ref, y2_ref, h2_ref = refs[-6:-2] if with_loss else refs
        xv = x1_ref[...]
        g = rows_ref[R_G_PRE_MLP:R_G_PRE_MLP + 1, :]
        h2 = ((xv * _rms(xv) * g) * (1.0 + mod_ref[M_SC_F:M_SC_F + 1, :]) + mod_ref[M_SH_F:M_SH_F + 1, :]).astype(BF16)
        h2_ref[...] = h2
        ru = jnp.concatenate([jnp.maximum(_dot(h2, wu_ref[j]), 0.0) for j in range(nb)], axis=1)
        ru_ref[...] = ru.astype(BF16)
        y2 = _dot((ru * ru).astype(BF16), wd_ref[...])
        y2_ref[...] = y2
        x2 = xv + mod_ref[M_GT_F:M_GT_F + 1, :] * ((y2 * _rms(y2)) * rows_ref[R_G_POST_MLP:R_G_POST_MLP + 1, :])
        x2_ref[...] = x2
        if with_loss:
            t_ref, dy_ref, loss_ref = refs[0], refs[-2], refs[-1]

            @pl.when(pl.program_id(0) == 0)
            def _():
                loss_ref[...] = jnp.zeros(loss_ref.shape, F32)

            e = x2 - t_ref[...]
            dy_ref[...] = e * (1.0 / d)
            loss_ref[...] += 0.5 * jnp.sum(jnp.mean(e * e, axis=-1, keepdims=True), axis=0, keepdims=True)

    tile = pl.BlockSpec((tm, d), lambda i: (i, 0))
    wide = pl.BlockSpec((tm, dff), lambda i: (i, 0))
    out_specs = (tile, wide, tile, tile)
    out_shape = (jax.ShapeDtypeStruct((s, d), F32), jax.ShapeDtypeStruct((s, dff), BF16),
                 jax.ShapeDtypeStruct((s, d), F32), jax.ShapeDtypeStruct((s, d), BF16))
    if with_loss:
        out_specs += (tile, pl.BlockSpec((SUBLANES, 128), lambda i: (0, 0)))
        out_shape += (jax.ShapeDtypeStruct((s, d), F32), jax.ShapeDtypeStruct((SUBLANES, 128), F32))
    return pl.pallas_call(
        body,
        name="mlp_fwd",
        grid=(s // tm,),
        in_specs=[
            tile,
            _resident((None, N_MOD, d), lambda i: (l, 0, 0)),
            _resident((None, N_ROWS, d), lambda i: (l, 0, 0)),
            _resident((nb, d, cu), lambda i: (0, 0, 0)),
            _resident((dff, d), lambda i: (0, 0)),
            ANY_SPEC,
        ] + ([tile] if with_loss else []),
        out_specs=out_specs,
        out_shape=out_shape,
        compiler_params=_cparams(("arbitrary",), 2 * _nbytes((dff, d), BF16) + 5 * _nbytes((tm, dff), F32) + 16 * _nbytes((tm, d), F32)),
    )(*_hbm(x1), mod, rows, wup_f, wdn_f, after, *(_hbm(target) if with_loss else ()))


def _mlp_bwd(dx2, x1, y2, ru, mod, rows, wup_f, wdn_f, l, tm):
    s, d = x1.shape
    nb, _, cu = wup_f.shape
    dff = nb * cu

    def body(dx2_ref, x1_ref, y2_ref, ru_ref, mod_ref, rows_ref, wu_ref, wd_ref, dx1_ref, dy2_ref, dup_ref, act_ref, sm_ref):
        @pl.when(pl.program_id(0) == 0)
        def _():
            sm_ref[...] = jnp.zeros(sm_ref.shape, F32)

        dout = dx2_ref[...]
        dy2, dgt, dgpost = _postnorm_bwd(y2_ref[...], dout, rows_ref[R_G_POST_MLP:R_G_POST_MLP + 1, :], mod_ref[M_GT_F:M_GT_F + 1, :])
        dy2b = dy2.astype(BF16)
        dy2_ref[...] = dy2b
        ruv = ru_ref[...].astype(F32)
        act_ref[...] = (ruv * ruv).astype(BF16)
        dup = (_dot_nt(dy2b, wd_ref[...]) * (2.0 * ruv)).astype(BF16)
        dup_ref[...] = dup
        dh2 = _dot_nt(dup[:, 0:cu], wu_ref[0])
        for j in range(1, nb):
            dh2 = dh2 + _dot_nt(dup[:, j * cu:(j + 1) * cu], wu_ref[j])
        dxn, dsc, dsh, dgpre = _prenorm_bwd(x1_ref[...], dh2, rows_ref[R_G_PRE_MLP:R_G_PRE_MLP + 1, :], mod_ref[M_SC_F:M_SC_F + 1, :])
        dx1_ref[...] = dout + dxn
        for r, v in ((G_MLP_GT, dgt), (G_MLP_GPOST, dgpost), (G_MLP_SC, dsc), (G_MLP_SH, dsh), (G_MLP_GPRE, dgpre)):
            sm_ref[r:r + 1, :] += v

    tile = pl.BlockSpec((tm, d), lambda i: (i, 0))
    wide = pl.BlockSpec((tm, dff), lambda i: (i, 0))
    return pl.pallas_call(
        body,
        name="mlp_bwd",
        grid=(s // tm,),
        in_specs=[
            tile, tile, tile, wide,
            _resident((None, N_MOD, d), lambda i: (l, 0, 0)),
            _resident((None, N_ROWS, d), lambda i: (l, 0, 0)),
            _resident((nb, d, cu), lambda i: (0, 0, 0)),
            _resident((dff, d), lambda i: (0, 0)),
        ],
        out_specs=(tile, tile, wide, wide, pl.BlockSpec((G_MLP_ROWS, d), lambda i: (0, 0))),
        out_shape=(jax.ShapeDtypeStruct((s, d), F32), jax.ShapeDtypeStruct((s, d), BF16), jax.ShapeDtypeStruct((s, dff), BF16),
                   jax.ShapeDtypeStruct((s, dff), BF16), jax.ShapeDtypeStruct((G_MLP_ROWS, d), F32)),
        compiler_params=_cparams(("arbitrary",), 2 * _nbytes((dff, d), BF16) + 6 * _nbytes((tm, dff), F32) + 16 * _nbytes((tm, d), F32)),
    )(*_hbm(dx2, x1, y2, ru), mod, rows, wup_f, wdn_f)


def _mixer_core_bwd(dx1, y, yap, ybp, hs, proj, mod, rows, cwf, wr, wi, wa_f, wb_f, wo_f, l, tm, after):
    s, d = dx1.shape
    nh, bw, _ = wr.shape[1:]
    nt = s // tm
    per = tm // SUBLANES

    def body(dx1_ref, y_ref, yap_ref, ybp_ref, hs_ref, hsh_ref, proj_ref, projh_ref, mod_ref, rows_ref, cw_ref,
             wr_ref, wi_ref, wa_ref, wb_ref, wo_ref, after_ref,
             dproj_ref, dy_ref, m_ref, dyap_ref, dybp_ref, ya_ref, yb_ref, sm_ref, dwg_ref,
             abuf, dcbuf, dxbuf, al_s, dh_s, lam_s, lnext):
        i = pl.program_id(0)
        first_tile = i == nt - 1

        @pl.when(i == 0)
        def _():
            sm_ref[...] = jnp.zeros(sm_ref.shape, F32)
            dwg_ref[...] = jnp.zeros(dwg_ref.shape, F32)
            zero = jnp.zeros((SUBLANES, d), F32)
            abuf[...] = zero
            dcbuf[...] = zero
            dxbuf[...] = zero
            lnext[...] = zero

        def row(r):
            return rows_ref[r:r + 1, :]

        def tap(r):
            return cw_ref[r:r + 1, :]

        def acc(r, v):
            sm_ref[r:r + 1, :] += v

        keep_halo = jnp.where(first_tile, 0.0, 1.0)
        dy, dgt, dgpost = _postnorm_bwd(y_ref[...], dx1_ref[...], row(R_G_POST_MIX), mod_ref[M_GT_M:M_GT_M + 1, :])
        acc(G_MIX_GT, dgt)
        acc(G_MIX_GPOST, dgpost)
        dyb16 = dy.astype(BF16)
        dy_ref[...] = dyb16
        dm = _dot_nt(dyb16, wo_ref[...])
        sa = _sigmoid(proj_ref[:, 5 * d:6 * d])
        sb = _sigmoid(proj_ref[:, 6 * d:7 * d])
        yap = yap_ref[...]
        ybp = ybp_ref[...]
        m_ref[...] = (sa * yap + sb * ybp).astype(BF16)
        dyap_f = dm * sa
        dybp_f = dm * sb
        dyap = dyap_f.astype(BF16)
        dybp = dybp_f.astype(BF16)
        dyap_ref[...] = dyap
        dybp_ref[...] = dybp
        dproj_ref[:, 5 * d:6 * d] = (dyap_f * yap * (1.0 - sa)).astype(BF16)
        dproj_ref[:, 6 * d:7 * d] = (dybp_f * ybp * (1.0 - sb)).astype(BF16)
        dya = _dot_nt(dyap, wa_ref[...])
        dyb = _dot_nt(dybp, wb_ref[...])
        ba = proj_ref[:, 0:d]
        ca = proj_ref[:, d:2 * d]
        va = proj_ref[:, 2 * d:3 * d]
        cv = ca * va
        cvh = keep_halo * (projh_ref[:, d:2 * d] * projh_ref[:, 2 * d:3 * d])
        cvm2 = _shift_rows(cv, cvh, 2, False)
        cvm1 = _shift_rows(cv, cvh, 1, False)
        conv3 = ((row(R_CONV_A_B) + cvm2 * tap(CW_A)) + cvm1 * tap(CW_A + 1)) + cv * tap(CW_A + 2)
        ya_ref[...] = (ba * conv3).astype(BF16)
        dproj_ref[:, 0:d] = (dya * conv3).astype(BF16)
        dc3 = dya * ba
        acc(G_MIX_CAB, _colsum(dc3))
        acc(G_MIX_CAW, _colsum(dc3 * cvm2))
        acc(G_MIX_CAW + 1, _colsum(dc3 * cvm1))
        acc(G_MIX_CAW + 2, _colsum(dc3 * cv))
        dct = dcbuf[...]
        dcv = (dc3 * tap(CW_A + 2) + _shift_rows(dc3, dct, 1, True) * tap(CW_A + 1)) + _shift_rows(dc3, dct, 2, True) * tap(CW_A)
        dcbuf[...] = dc3[:SUBLANES, :]
        dproj_ref[:, d:2 * d] = (dcv * va).astype(BF16)
        dproj_ref[:, 2 * d:3 * d] = (dcv * ca).astype(BF16)
        xb = proj_ref[:, 3 * d:4 * d]
        gb = proj_ref[:, 4 * d:5 * d]
        xbh = keep_halo * projh_ref[:, 3 * d:4 * d]
        xm3 = _shift_rows(xb, xbh, 3, False)
        xm2 = _shift_rows(xb, xbh, 2, False)
        xm1 = _shift_rows(xb, xbh, 1, False)
        xc = (((row(R_CONV_B_B) + xm3 * tap(CW_B)) + xm2 * tap(CW_B + 1)) + xm1 * tap(CW_B + 2)) + xb * tap(CW_B + 3)
        lam = row(R_LAMBDA)
        sp = _softplus_neg(lam)
        xcb, r, ig, a, mult = _gates(xc, wr_ref, wi_ref, row(R_B_GATE_R), row(R_B_GATE_I), sp, nh, bw)
        gel, gsig = _gelu(gb)
        hs = hs_ref[...]
        yb_ref[...] = (hs * gel).astype(BF16)
        dproj_ref[:, 4 * d:5 * d] = (dyb * hs * _gelu_grad(gb, gsig)).astype(BF16)
        al_s[...] = _shift_rows(a, abuf[...], 1, True)
        abuf[...] = a[:SUBLANES, :]
        dh_s[...] = dyb * gel

        def blk(j, ln):
            o = pl.multiple_of((per - 1 - j) * SUBLANES, SUBLANES)
            a8, b8 = _scan_block(al_s[pl.ds(o, SUBLANES), :], dh_s[pl.ds(o, SUBLANES), :], reverse=True)
            l8 = b8 + a8 * ln
            lam_s[pl.ds(o, SUBLANES), :] = l8
            return jnp.broadcast_to(l8[0:1, :], (SUBLANES, d))

        lnext[...] = lax.fori_loop(0, per, blk, lnext[...])
        dbb = lam_s[...]
        da = dbb * _shift_rows(hs, keep_halo * hsh_ref[...], 1, False)
        dbx = dbb * xc
        dmult = dbx * ig
        dig = dbx * mult
        dxc = (dbb * mult) * ig
        dla = a * (da - (dmult * a) / mult)
        dlar = dla * r
        acc(G_MIX_LAM, _colsum(dlar) * (LRU_C * _sigmoid(-lam)))
        dzr = (dlar * (1.0 - r)) * (-LRU_C * sp)
        dzi = dig * ig * (1.0 - ig)
        acc(G_MIX_BR, _colsum(dzr))
        acc(G_MIX_BI, _colsum(dzi))
        dzrb = dzr.astype(BF16)
        dzib = dzi.astype(BF16)
        back = []
        for h in range(nh):
            sl = slice(h * bw, (h + 1) * bw)
            back.append(_dot_nt(dzrb[:, sl], wr_ref[h]) + _dot_nt(dzib[:, sl], wi_ref[h]))
            dwg_ref[0, h] += _dot_tn(xcb[:, sl], dzrb[:, sl])
            dwg_ref[1, h] += _dot_tn(xcb[:, sl], dzib[:, sl])
        dxc = dxc + jnp.concatenate(back, axis=1)
        acc(G_MIX_CBB, _colsum(dxc))
        acc(G_MIX_CBW, _colsum(dxc * xm3))
        acc(G_MIX_CBW + 1, _colsum(dxc * xm2))
        acc(G_MIX_CBW + 2, _colsum(dxc * xm1))
        acc(G_MIX_CBW + 3, _colsum(dxc * xb))
        dxt = dxbuf[...]
        dxb = (((dxc * tap(CW_B + 3) + _shift_rows(dxc, dxt, 1, True) * tap(CW_B + 2)) + _shift_rows(dxc, dxt, 2, True) * tap(CW_B + 1))
               + _shift_rows(dxc, dxt, 3, True) * tap(CW_B))
        dxbuf[...] = dxc[:SUBLANES, :]
        dproj_ref[:, 3 * d:4 * d] = dxb.astype(BF16)

    def rev(i):
        return (nt - 1 - i, 0)

    def halo(i):
        return (jnp.maximum((nt - 1 - i) * per - 1, 0), 0)

    tile = pl.BlockSpec((tm, d), rev)
    return pl.pallas_call(
        body,
        name="mixer_core_bwd",
        grid=(nt,),
        in_specs=[
            tile, tile, tile, tile, tile,
            pl.BlockSpec((SUBLANES, d), halo),
            pl.BlockSpec((tm, 7 * d), rev),
            pl.BlockSpec((SUBLANES, 7 * d), halo),
            _resident((None, N_MOD, d), lambda i: (l, 0, 0)),
            _resident((None, N_ROWS, d), lambda i: (l, 0, 0)),
            _resident((None, CW_ROWS, d), lambda i: (l, 0, 0)),
            _resident((None, nh, bw, bw), lambda i: (l, 0, 0, 0)),
            _resident((None, nh, bw, bw), lambda i: (l, 0, 0, 0)),
            _resident((d, d), lambda i: (0, 0)),
            _resident((d, d), lambda i: (0, 0)),
            _resident((d, d), lambda i: (0, 0)),
            ANY_SPEC,
        ],
        out_specs=(pl.BlockSpec((tm, 7 * d), rev),) + (tile,) * 6 + (
            pl.BlockSpec((G_MIX_ROWS, d), lambda i: (0, 0)), pl.BlockSpec((2, nh, bw, bw), lambda i: (0, 0, 0, 0))),
        out_shape=(jax.ShapeDtypeStruct((s, 7 * d), BF16),) + (jax.ShapeDtypeStruct((s, d), BF16),) * 6 + (
            jax.ShapeDtypeStruct((G_MIX_ROWS, d), F32), jax.ShapeDtypeStruct((2, nh, bw, bw), F32)),
        scratch_shapes=[pltpu.VMEM((SUBLANES, d), F32)] * 3 + [pltpu.VMEM((tm, d), F32)] * 3 + [pltpu.VMEM((SUBLANES, d), F32)],
        compiler_params=_cparams(("arbitrary",), 3 * _nbytes((d, d), BF16) + 3 * _nbytes((tm, 7 * d), F32) + 64 * _nbytes((tm, d), F32)),
    )(*_hbm(dx1, y, yap, ybp, hs, hs, proj, proj), mod, rows, cwf, wr, wi, wa_f, wb_f, wo_f, after)


def _in_proj_bwd(dproj, x, dx1, mod, rows, win_f, l, tm, after):
    s, d = x.shape
    nb, _, ci = win_f.shape

    def body(dp_ref, x_ref, dx1_ref, mod_ref, rows_ref, w_ref, after_ref, dx_ref, sm_ref):
        @pl.when(pl.program_id(0) == 0)
        def _():
            sm_ref[...] = jnp.zeros(sm_ref.shape, F32)

        dh = _dot_nt(dp_ref[:, 0:ci], w_ref[0])
        for j in range(1, nb):
            dh = dh + _dot_nt(dp_ref[:, j * ci:(j + 1) * ci], w_ref[j])
        dxn, dsc, dsh, dg = _prenorm_bwd(x_ref[...], dh, rows_ref[R_G_PRE_MIX:R_G_PRE_MIX + 1, :], mod_ref[M_SC_M:M_SC_M + 1, :])
        dx_ref[...] = dx1_ref[...] + dxn
        for r, v in ((G_IN_SC, dsc), (G_IN_SH, dsh), (G_IN_GPRE, dg)):
            sm_ref[r:r + 1, :] += v

    tile = pl.BlockSpec((tm, d), lambda i: (i, 0))
    return pl.pallas_call(
        body,
        name="in_proj_bwd",
        grid=(s // tm,),
        in_specs=[
            pl.BlockSpec((tm, nb * ci), lambda i: (i, 0)), tile, tile,
            _resident((None, N_MOD, d), lambda i: (l, 0, 0)),
            _resident((None, N_ROWS, d), lambda i: (l, 0, 0)),
            _resident((nb, d, ci), lambda i: (0, 0, 0)),
            ANY_SPEC,
        ],
        out_specs=(tile, pl.BlockSpec((G_IN_ROWS, d), lambda i: (0, 0))),
        out_shape=(jax.ShapeDtypeStruct((s, d), F32), jax.ShapeDtypeStruct((G_IN_ROWS, d), F32)),
        compiler_params=_cparams(("arbitrary",), _nbytes((nb, d, ci), BF16) + 2 * _nbytes((tm, nb * ci), BF16) + 16 * _nbytes((tm, d), F32)),
    )(*_hbm(dproj, x, dx1), mod, rows, win_f, after)


def _wgrad(a, b, cols_owned, ts, after):
    s, k1 = a.shape
    k2 = b.shape[1]
    ns = s // ts
    if cols_owned:
        nblk, bk1, bk2 = N_DEV, k1, k2 // N_DEV
        a_spec = pl.BlockSpec((ts, bk1), lambda j, t: (t, 0))
        b_spec = pl.BlockSpec((ts, bk2), lambda j, t: (t, j))
    else:
        bk1, bk2 = min(WGRAD_ROWS, k1), k2
        nblk = k1 // bk1
        a_spec = pl.BlockSpec((ts, bk1), lambda j, t: (t, j))
        b_spec = pl.BlockSpec((ts, bk2), lambda j, t: (t, 0))

    def body(a_ref, b_ref, after_ref, o_ref, acc_ref):
        t = pl.program_id(1)

        @pl.when(t == 0)
        def _():
            acc_ref[...] = jnp.zeros(acc_ref.shape, F32)

        acc_ref[...] += _dot_tn(a_ref[...], b_ref[...])

        @pl.when(t == ns - 1)
        def _():
            o_ref[...] = acc_ref[...].astype(BF16)

    out = pl.pallas_call(
        body,
        name="wgrad",
        grid=(nblk, ns),
        in_specs=[a_spec, b_spec, ANY_SPEC],
        out_specs=pl.BlockSpec((None, bk1, bk2), lambda j, t: (j, 0, 0)),
        out_shape=pltpu.HBM((nblk, bk1, bk2), BF16),
        scratch_shapes=[pltpu.VMEM((bk1, bk2), F32)],
        compiler_params=_cparams(("parallel", "arbitrary"), 4 * _nbytes((bk1, bk2), F32) + 4 * _nbytes((ts, bk1 + bk2), BF16)),
    )(pltpu.with_memory_space_constraint(a, pltpu.HBM), pltpu.with_memory_space_constraint(b, pltpu.HBM), after)
    return out if cols_owned else out.reshape(N_DEV, k1 // N_DEV, k2)


def _adam_update(w, g, m, v):
    m = ADAM_B1 * m + (1.0 - ADAM_B1) * g
    v = ADAM_B2 * v + (1.0 - ADAM_B2) * (g * g)
    m_hat = m / (1.0 - ADAM_B1 ** ADAM_STEP)
    v_hat = v / (1.0 - ADAM_B2 ** ADAM_STEP)
    delta = -ADAM_LR * (m_hat / (jnp.sqrt(v_hat) + ADAM_EPS) + ADAM_WD * w)
    return delta, m, v


def _sum_adamw(recv, w, m, v, tr, after):
    nl, ra, cb = w.shape
    assert nl == len(recv) == 2

    def body(r0_ref, r1_ref, w_ref, m_ref, v_ref, after_ref, g_ref, d_ref, nm_ref, nv_ref):
        def total(r_ref):
            g = r_ref[0].astype(F32)
            for j in range(1, N_DEV):
                g = g + r_ref[j].astype(F32)
            return g

        g = jnp.where(pl.program_id(0) == 0, total(r0_ref), total(r1_ref))
        g_ref[...] = g
        d_ref[...], nm_ref[...], nv_ref[...] = _adam_update(w_ref[...], g, m_ref[...], v_ref[...])

    blk = pl.BlockSpec((None, tr, cb), lambda l, i: (l, i, 0))
    return pl.pallas_call(
        body,
        name="sum_adamw",
        grid=(nl, ra // tr),
        in_specs=[pl.BlockSpec((N_DEV, tr, cb), lambda l, i: (0, i * (1 - l), 0)),
                  pl.BlockSpec((N_DEV, tr, cb), lambda l, i: (0, i * l, 0)), blk, blk, blk, ANY_SPEC],
        out_specs=(blk,) * 4,
        out_shape=(jax.ShapeDtypeStruct((nl, ra, cb), F32),) * 4,
        compiler_params=_cparams(("arbitrary", "arbitrary"), 6 * _nbytes((N_DEV, tr, cb), BF16) + 32 * _nbytes((tr, cb), F32)),
    )(recv[0], recv[1], w, m, v, after)


def _adamw_small(quads):
    n = len(quads)
    flat = [a for q in quads for a in q]

    def body(*refs):
        ins, outs = refs[:4 * n], refs[4 * n:7 * n]
        vin, vout, sem = refs[7 * n:11 * n], refs[11 * n:14 * n], refs[-1]
        loads = [pltpu.make_async_copy(ins[i], vin[i], sem.at[i]) for i in range(4 * n)]
        for cp in loads:
            cp.start()
        stores = []
        for p in range(n):
            for cp in loads[4 * p:4 * p + 4]:
                cp.wait()
            new = _adam_update(vin[4 * p][...], vin[4 * p + 1][...], vin[4 * p + 2][...], vin[4 * p + 3][...])
            for i in range(3):
                vout[3 * p + i][...] = new[i]
                stores.append(pltpu.make_async_copy(vout[3 * p + i], outs[3 * p + i], sem.at[4 * n + 3 * p + i]))
                stores[-1].start()
        for cp in stores:
            cp.wait()

    out = pl.pallas_call(
        body,
        name="adamw_small",
        in_specs=[ANY_SPEC] * (4 * n),
        out_specs=(ANY_SPEC,) * (3 * n),
        out_shape=tuple(jax.ShapeDtypeStruct(q[0].shape, F32) for q in quads for _ in range(3)),
        scratch_shapes=[pltpu.VMEM(a.shape, F32) for a in flat] + [pltpu.VMEM(q[0].shape, F32) for q in quads for _ in range(3)]
        + [pltpu.SemaphoreType.DMA((7 * n,))],
        compiler_params=_cparams(vmem=2 * sum(_nbytes(a.shape, F32) for a in flat)),
    )(*flat)
    return [tuple(out[3 * p:3 * p + 3]) for p in range(n)]


def _adamw_tiled(w, g, m, v, tr):
    nl, ra, cb = w.shape

    def body(w_ref, g_ref, m_ref, v_ref, d_ref, nm_ref, nv_ref):
        d_ref[...], nm_ref[...], nv_ref[...] = _adam_update(w_ref[...], g_ref[...], m_ref[...], v_ref[...])

    blk = pl.BlockSpec((None, tr, cb), lambda l, i: (l, i, 0))
    return pl.pallas_call(
        body,
        name="adamw_tiled",
        grid=(nl, ra // tr),
        in_specs=[blk] * 4,
        out_specs=(blk,) * 3,
        out_shape=(jax.ShapeDtypeStruct((nl, ra, cb), F32),) * 3,
        compiler_params=_cparams(("parallel", "parallel")),
    )(*_hbm(w, g, m, v))


def _token_tile(s):
    return min(256, max(SUBLANES * 2, s // 4))


def kernel(x, c, w_mod, b_mod, g_pre_mix, g_post_mix, w_in, conv_a_w, conv_a_b, w_a_out, conv_b_w, conv_b_b, w_gate_r, b_gate_r, w_gate_i, b_gate_i, lru_lambda, w_b_out, w_o, g_pre_mlp, g_post_mlp, w_mlp_up, w_mlp_down, loss_target, m_w_mod, m_b_mod, m_g_pre_mix, m_g_post_mix, m_w_in, m_conv_a_w, m_conv_a_b, m_w_a_out, m_conv_b_w, m_conv_b_b, m_w_gate_r, m_b_gate_r, m_w_gate_i, m_b_gate_i, m_lru_lambda, m_w_b_out, m_w_o, m_g_pre_mlp, m_g_post_mlp, m_w_mlp_up, m_w_mlp_down, v_w_mod, v_b_mod, v_g_pre_mix, v_g_post_mix, v_w_in, v_conv_a_w, v_conv_a_b, v_w_a_out, v_conv_b_w, v_conv_b_b, v_w_gate_r, v_b_gate_r, v_w_gate_i, v_b_gate_i, v_lru_lambda, v_w_b_out, v_w_o, v_g_pre_mlp, v_g_post_mlp, v_w_mlp_up, v_w_mlp_down):
    nl = w_mod.shape[0]
    s, d = x.shape[1], x.shape[2]
    nh, bw = w_gate_r.shape[1], w_gate_r.shape[2]
    cwid = conv_a_w.shape[2]
    tm = _token_tile(s)
    tmx = min(2 * tm, s)
    ts = s
    _, me = _my_position()
    xs = x.reshape(s, d)
    target = loss_target.reshape(s, d)

    vec_names = (g_pre_mix, g_post_mix, conv_a_b, conv_b_b, b_gate_r, b_gate_i, lru_lambda, g_pre_mlp, g_post_mlp)
    rows = jnp.concatenate([jnp.stack(vec_names, axis=1), jnp.zeros((nl, N_ROWS - len(vec_names), d), F32)], axis=1)
    cw = jnp.concatenate([conv_a_w, conv_b_w, jnp.zeros((nl, CW_ROWS - 7, cwid), F32)], axis=1)

    large = {"w_in": w_in, "w_a_out": w_a_out, "w_b_out": w_b_out, "w_o": w_o, "w_mlp_up": w_mlp_up, "w_mlp_down": w_mlp_down}
    groups = (("in", ("w_in",)), ("mix", ("w_a_out", "w_b_out", "w_o")), ("mlp", ("w_mlp_up", "w_mlp_down")))
    mod, cact, cwf = _prep_small(c, w_mod, b_mod, cw)
    mod = mod.reshape(nl, N_MOD, d)
    gathers = {}
    tok = mod
    for l in range(nl):
        for gname, members in groups:
            gathers[l, gname] = _gather2_start([large[n][l].astype(BF16) for n in members], tok, f"gather_start_{gname}{l}")
            tok = gathers[l, gname][4]
    wr = w_gate_r.astype(BF16)
    wi = w_gate_i.astype(BF16)

    forwarded = {}

    def forward(l, gname, after):
        forwarded[l, gname] = _gather2_forward(gathers[l, gname], after, f"gather_forward_{gname}{l}")
        return forwarded[l, gname][4]

    def gathered(l, gname, after):
        return _gather2_wait(forwarded[l, gname], after, f"gather_wait_{gname}{l}")

    saved = []
    weights = []
    xin = xs
    for l in range(nl):
        if l == 0:
            tok = forward(0, "in", tok)
        (win_f,) = gathered(l, "in", tok if l == 0 else xin)
        if l > 0:
            tok = forward(l, "mix", win_f)
        proj, h = _in_proj_fwd(xin, mod, rows, win_f, l, tmx, tok)
        if l == 0:
            tok = forward(0, "mix", proj)
        wa_f, wb_f, wo_f = (w.reshape(d, d) for w in gathered(l, "mix", proj))
        if l > 0:
            tok = forward(l, "mlp", wo_f)
        x1, hs, yap, ybp, y = _mixer_core_fwd(proj, xin, mod, rows, cwf, wr, wi, wa_f, wb_f, wo_f, l, tm, tok)
        if l == 0:
            forward(0, "mlp", x1)
        wup_f, wdn_f = gathered(l, "mlp", x1)
        wdn_f = wdn_f.reshape(-1, d)
        if l + 1 < nl:
            tok = forward(l + 1, "in", wdn_f)
        x2, ru, y2, h2, *last = _mlp_fwd(x1, mod, rows, wup_f, wdn_f, l, tmx, tok, target if l + 1 == nl else None)
        saved.append((xin, proj, h, x1, hs, yap, ybp, y, ru, y2, h2))
        weights.append((win_f, wa_f, wb_f, wo_f, wup_f, wdn_f))
        xin = x2
    dx, loss_part = last[0], last[1][0, 0]

    scatters = {}
    small = [None] * nl
    gate_parts = [None] * nl

    def scatter(l, gname, parts):
        scatters[l, gname] = _exchange_start(parts, False, f"scatter_start_{gname}{l}")
        return scatters[l, gname][4]

    for l in reversed(range(nl)):
        xin, proj, h, x1, hs, yap, ybp, y, ru, y2, h2 = saved[l]
        win_f, wa_f, wb_f, wo_f, wup_f, wdn_f = weights[l]
        dx1, dy2, dup, act, sm_mlp = _mlp_bwd(dx, x1, y2, ru, mod, rows, wup_f, wdn_f, l, tm)
        g_up = _wgrad(h2, dup, True, ts, dx1)
        g_dn = _wgrad(act, dy2, False, ts, g_up)
        tok = scatter(l, "mlp", [g_up, g_dn])
        dproj, dy, m, dyap, dybp, ya, yb, sm_mix, dwg = _mixer_core_bwd(
            dx1, y, yap, ybp, hs, proj, mod, rows, cwf, wr, wi, wa_f, wb_f, wo_f, l, tm // 2, tok)
        gate_parts[l] = _exchange_start([dwg.astype(BF16)], True, f"gates_start{l}")
        g_a = _wgrad(ya, dyap, False, ts, gate_parts[l][4])
        g_b = _wgrad(yb, dybp, False, ts, g_a)
        g_o = _wgrad(m, dy, False, ts, g_b)
        tok = scatter(l, "mix", [g_a, g_b, g_o])
        tok = scatter(l, "in", [_wgrad(h, dproj, True, ts, tok)])
        dx, sm_in = _in_proj_bwd(dproj, xin, dx1, mod, rows, win_f, l, tm, tok)
        small[l] = jnp.concatenate([sm_mlp, sm_mix, sm_in], axis=0)
    grad_x = dx.reshape(x.shape)

    recv = {}
    big = {}
    moments = {"w_in": (w_in, m_w_in, v_w_in), "w_mlp_up": (w_mlp_up, m_w_mlp_up, v_w_mlp_up), "w_a_out": (w_a_out, m_w_a_out, v_w_a_out),
               "w_b_out": (w_b_out, m_w_b_out, v_w_b_out), "w_o": (w_o, m_w_o, v_w_o), "w_mlp_down": (w_mlp_down, m_w_mlp_down, v_w_mlp_down)}

    def collect(l, gname, after):
        for n, zone in zip(dict(groups)[gname], _exchange_wait(scatters[l, gname], after, False, f"scatter_wait_{gname}{l}")):
            recv[n, l] = zone

    def update(name, after):
        w, m_, v_ = moments[name]
        big[name] = _sum_adamw([recv[name, l] for l in range(nl)], w, m_, v_, min(256, w.shape[1]), after)
        return big[name][1]

    for l, gname in ((1, "mlp"), (1, "in"), (1, "mix"), (0, "mlp")):
        collect(l, gname, dx)
    done = update("w_mlp_down", update("w_mlp_up", dx))
    collect(0, "mix", done)
    for name in ("w_a_out", "w_b_out", "w_o"):
        done = update(name, done)
    collect(0, "in", done)
    done = update("w_in", done)

    lrows = jnp.concatenate(small, axis=0)
    lrows = lrows.at[G_LOSS_ROW, 0].set(loss_part)

    def lrow(a, l, r):
        return a[l * G_LAYER_ROWS + r]

    dm = jnp.stack([jnp.concatenate([lrow(lrows, l, G_MLP_ROWS + G_MIX_ROWS + G_IN_SH), lrow(lrows, l, G_MLP_ROWS + G_MIX_ROWS + G_IN_SC),
                                     lrow(lrows, l, G_MLP_ROWS + G_MIX_GT), lrow(lrows, l, G_MLP_SH), lrow(lrows, l, G_MLP_SC),
                                     lrow(lrows, l, G_MLP_GT)]) for l in range(nl)])
    dm8 = jnp.concatenate([dm[:, None, :], jnp.zeros((nl, SUBLANES - 1, N_MOD * d), F32)], axis=1)
    srows, g_w_mod = _reduce_small(lrows, dm8, cact, done)
    loss = srows[G_LOSS_ROW, 0]
    sgates = _sum_gathered([_exchange_wait(gate_parts[l], srows, True, f"gates_wait{l}")[0] for l in range(nl)])

    def srow(l, r):
        return lrow(srows, l, r)

    def per_layer(r):
        return jnp.stack([srow(l, r) for l in range(nl)])

    mix0 = G_MLP_ROWS
    in0 = G_MLP_ROWS + G_MIX_ROWS
    g_b_mod = jnp.stack([jnp.concatenate([srow(l, in0 + G_IN_SH), srow(l, in0 + G_IN_SC), srow(l, mix0 + G_MIX_GT),
                                          srow(l, G_MLP_SH), srow(l, G_MLP_SC), srow(l, G_MLP_GT)]) for l in range(nl)])
    conv_a_full = jnp.stack([jnp.stack([srow(l, mix0 + G_MIX_CAW + k) for k in range(3)]) for l in range(nl)])
    conv_b_full = jnp.stack([jnp.stack([srow(l, mix0 + G_MIX_CBW + k) for k in range(4)]) for l in range(nl)])
    grads = {
        "b_mod": g_b_mod,
        "g_pre_mix": per_layer(in0 + G_IN_GPRE),
        "g_post_mix": per_layer(mix0 + G_MIX_GPOST),
        "conv_a_w": lax.dynamic_slice_in_dim(conv_a_full, me * cwid, cwid, axis=2),
        "conv_a_b": per_layer(mix0 + G_MIX_CAB),
        "conv_b_w": lax.dynamic_slice_in_dim(conv_b_full, me * cwid, cwid, axis=2),
        "conv_b_b": per_layer(mix0 + G_MIX_CBB),
        "w_gate_r": sgates[:, 0],
        "b_gate_r": per_layer(mix0 + G_MIX_BR),
        "w_gate_i": sgates[:, 1],
        "b_gate_i": per_layer(mix0 + G_MIX_BI),
        "lru_lambda": per_layer(mix0 + G_MIX_LAM),
        "g_pre_mlp": per_layer(G_MLP_GPRE),
        "g_post_mlp": per_layer(G_MLP_GPOST),
    }
    params = {
        "b_mod": (b_mod, m_b_mod, v_b_mod), "g_pre_mix": (g_pre_mix, m_g_pre_mix, v_g_pre_mix), "g_post_mix": (g_post_mix, m_g_post_mix, v_g_post_mix),
        "conv_a_w": (conv_a_w, m_conv_a_w, v_conv_a_w), "conv_a_b": (conv_a_b, m_conv_a_b, v_conv_a_b),
        "conv_b_w": (conv_b_w, m_conv_b_w, v_conv_b_w), "conv_b_b": (conv_b_b, m_conv_b_b, v_conv_b_b),
        "w_gate_r": (w_gate_r, m_w_gate_r, v_w_gate_r), "b_gate_r": (b_gate_r, m_b_gate_r, v_b_gate_r),
        "w_gate_i": (w_gate_i, m_w_gate_i, v_w_gate_i), "b_gate_i": (b_gate_i, m_b_gate_i, v_b_gate_i),
        "lru_lambda": (lru_lambda, m_lru_lambda, v_lru_lambda), "g_pre_mlp": (g_pre_mlp, m_g_pre_mlp, v_g_pre_mlp),
        "g_post_mlp": (g_post_mlp, m_g_post_mlp, v_g_post_mlp),
    }
    out = {}
    names = list(grads)
    quads = [tuple(a.reshape(-1, a.shape[-1]) for a in (params[n][0], grads[n], params[n][1], params[n][2])) for n in names]
    for name, new in zip(names, _adamw_small(quads)):
        shape = params[name][0].shape
        out[name] = (grads[name].reshape(shape),) + tuple(a.reshape(shape) for a in new)
    out["w_mod"] = (g_w_mod,) + tuple(_adamw_tiled(w_mod, g_w_mod, m_w_mod, v_w_mod, min(128, d)))
    out.update(big)

    order = ("w_mod", "b_mod", "g_pre_mix", "g_post_mix", "w_in", "conv_a_w", "conv_a_b", "w_a_out", "conv_b_w", "conv_b_b", "w_gate_r", "b_gate_r",
             "w_gate_i", "b_gate_i", "lru_lambda", "w_b_out", "w_o", "g_pre_mlp", "g_post_mlp", "w_mlp_up", "w_mlp_down")
    return (loss, grad_x) + tuple(out[n][0] for n in order) + tuple(out[n][1] for n in order) + tuple(out[n][2] for n in order) + tuple(out[n][3] for n in order)
```

```python
import jax
import jax.numpy as jnp
from jax import lax
from jax.experimental import pallas as pl
from jax.experimental.pallas import tpu as pltpu

F32, BF16 = jnp.float32, jnp.bfloat16
EPS = 1e-6
LRU_C = 8.0
N_DEV = 8
N_MOD = 6
SUBLANES = 8
VMEM_BUDGET = 56 * 1024 * 1024
WGRAD_ROWS = 512
TOKEN_TILE = 256
ADAM_ROWS = 256
ADAM_LR, ADAM_B1, ADAM_B2, ADAM_EPS, ADAM_WD, ADAM_STEP = 0.001, 0.9, 0.999, 1e-08, 0.01, 10
MESH = pl.DeviceIdType.MESH
VMEM_SPEC = pl.BlockSpec(memory_space=pltpu.VMEM)
ANY_SPEC = pl.BlockSpec(memory_space=pl.ANY)
HBM_SPEC = pl.BlockSpec(memory_space=pltpu.HBM)
SEM_SPEC = pl.BlockSpec(memory_space=pltpu.SEMAPHORE)
SIDE_EFFECT = pltpu.SideEffectType.DATAFLOW_SIDE_EFFECTING

R_G_PRE_MIX, R_G_POST_MIX, R_CONV_A_B, R_CONV_B_B, R_B_GATE_R, R_B_GATE_I, R_LAMBDA, R_G_PRE_MLP, R_G_POST_MLP = range(9)
N_ROWS = 16
M_SH_M, M_SC_M, M_GT_M, M_SH_F, M_SC_F, M_GT_F = range(6)
CW_A, CW_B, CW_ROWS = 0, 3, 8
G_MLP_GT, G_MLP_GPOST, G_MLP_SC, G_MLP_SH, G_MLP_GPRE, G_LOSS_ROW, G_MLP_ROWS = 0, 1, 2, 3, 4, 7, 8
(G_MIX_GT, G_MIX_GPOST, G_MIX_CAB, G_MIX_CAW, G_MIX_CBB, G_MIX_CBW, G_MIX_BR, G_MIX_BI, G_MIX_LAM) = 0, 1, 2, 3, 6, 7, 11, 12, 13
G_MIX_ROWS = 16
G_IN_SC, G_IN_SH, G_IN_GPRE, G_IN_ROWS = 0, 1, 2, 8
G_LAYER_ROWS = G_MLP_ROWS + G_MIX_ROWS + G_IN_ROWS


def _cparams(dims=None, vmem=None):
    kw = {}
    if dims is not None:
        kw["dimension_semantics"] = dims
    if vmem is not None:
        kw["vmem_limit_bytes"] = int(min(max(vmem, 16 * 1024 * 1024), VMEM_BUDGET))
    return pltpu.CompilerParams(**kw)


def _nbytes(shape, dtype):
    n = 1
    for s in shape:
        n *= s
    return n * jnp.dtype(dtype).itemsize


def _hbm(*arrays):
    return tuple(pltpu.with_memory_space_constraint(a, pltpu.HBM) for a in arrays)


def _resident(block, index_map):
    return pl.BlockSpec(block, index_map, pipeline_mode=pl.Buffered(1))


def _my_position():
    x, y, c = lax.axis_index("x"), lax.axis_index("y"), lax.axis_index("c")
    return (x, y, c), 4 * x + 2 * y + c


def _peer(pos, k):
    x, y, c = pos
    px = 1 - x if k & 4 else x
    py = 1 - y if k & 2 else y
    pc = 1 - c if k & 1 else c
    return (px, py, pc), 4 * px + 2 * py + pc


def _remote(src, dst, ssem, rsem, peer):
    return pltpu.make_async_remote_copy(src_ref=src, dst_ref=dst, send_sem=ssem, recv_sem=rsem, device_id=peer, device_id_type=MESH)


def _dot(a, b):
    return jnp.dot(a, b, preferred_element_type=F32)


def _dot_nt(a, b):
    return lax.dot_general(a, b, (((1,), (1,)), ((), ())), preferred_element_type=F32)


def _dot_tn(a, b):
    return lax.dot_general(a, b, (((0,), (0,)), ((), ())), preferred_element_type=F32)


def _colsum(v):
    return jnp.sum(v, axis=0, keepdims=True)


def _sigmoid(v):
    return jax.nn.sigmoid(v)


GELU_K, GELU_C = 0.7978845608028654, 0.044715


def _gelu(v):
    s = 1.0 / (1.0 + jnp.exp(v * (-2.0 * GELU_K - (2.0 * GELU_K * GELU_C) * (v * v))))
    return v * s, s


def _gelu_grad(v, s):
    return s * (1.0 + (v * (1.0 - s)) * (2.0 * GELU_K + (6.0 * GELU_K * GELU_C) * (v * v)))


def _neg_expm1_twice(v):
    t = jnp.tanh(v)
    return (-2.0 * t) / (1.0 - t)


def _softplus_neg(lam):
    z = -lam
    u = jnp.exp(-jnp.abs(z))
    w = 1.0 + u
    l1p = jnp.where(w == 1.0, u, jnp.log(w) * u / (w - 1.0))
    return jnp.maximum(z, 0.0) + l1p


def _rms(v):
    return lax.rsqrt(jnp.mean(v * v, axis=-1, keepdims=True) + EPS)


def _prenorm_bwd(xv, dh, g, sc):
    r = _rms(xv)
    xn = xv * r
    n = xn * g
    dsc = _colsum(dh * n)
    dsh = _colsum(dh)
    dn = dh * (1.0 + sc)
    dg = _colsum(dn * xn)
    dxn = dn * g
    dx = r * (dxn - xn * jnp.mean(dxn * xn, axis=-1, keepdims=True))
    return dx, dsc, dsh, dg


def _postnorm_bwd(yv, dout, g, gt):
    r = _rms(yv)
    yn = yv * r
    dgt = _colsum(dout * (yn * g))
    dn = dout * gt
    dg = _colsum(dn * yn)
    dyn = dn * g
    dy = r * (dyn - yn * jnp.mean(dyn * yn, axis=-1, keepdims=True))
    return dy, dgt, dg


def _gates(xc, wr_ref, wi_ref, b_r, b_i, sp, nh, bw):
    xcb = xc.astype(BF16)
    zr = jnp.concatenate([_dot(xcb[:, h * bw:(h + 1) * bw], wr_ref[h]) for h in range(nh)], axis=1) + b_r
    zi = jnp.concatenate([_dot(xcb[:, h * bw:(h + 1) * bw], wi_ref[h]) for h in range(nh)], axis=1) + b_i
    r = _sigmoid(zr)
    ig = _sigmoid(zi)
    la = (-LRU_C * r) * sp
    a = jnp.exp(la)
    mult = jnp.sqrt(_neg_expm1_twice(la))
    return xcb, r, ig, a, mult


def _shift_rows(cur, edge, k, up):
    t, dd = cur.shape
    blocks = cur.reshape(t // SUBLANES, SUBLANES, dd)
    row = lax.broadcasted_iota(jnp.int32, (1, SUBLANES, dd), 1)
    if up:
        r = pltpu.roll(blocks, SUBLANES - k, 1)
        nxt = jnp.concatenate([r[1:], pltpu.roll(edge, SUBLANES - k, 0)[None]], axis=0)
        out = jnp.where(row >= SUBLANES - k, nxt, r)
    else:
        r = pltpu.roll(blocks, k, 1)
        prv = jnp.concatenate([pltpu.roll(edge, k, 0)[None], r[:-1]], axis=0)
        out = jnp.where(row < k, prv, r)
    return out.reshape(t, dd)


def _scan_block(a8, b8, reverse):
    row = lax.broadcasted_iota(jnp.int32, a8.shape, 0)
    for s in (1, 2, 4):
        if reverse:
            keep = row < SUBLANES - s
            a_sh = pltpu.roll(a8, SUBLANES - s, 0)
            b_sh = pltpu.roll(b8, SUBLANES - s, 0)
        else:
            keep = row >= s
            a_sh = pltpu.roll(a8, s, 0)
            b_sh = pltpu.roll(b8, s, 0)
        b8 = b8 + a8 * jnp.where(keep, b_sh, 0.0)
        a8 = a8 * jnp.where(keep, a_sh, 1.0)
    return a8, b8


def _prep_small(c, w_mod, b_mod, cw):
    d = c.shape[1]
    cm = w_mod.shape[2]
    cwid = cw.shape[2]
    nl = w_mod.shape[0]

    def body(c_ref, wm_ref, bm_ref, cw_ref, mod_ref, cact_ref, cwf_ref, cbuf, pbuf, rbuf, ssem, rsem, lsem):
        pos, me = _my_position()
        me8 = pl.multiple_of(me * SUBLANES, SUBLANES)
        cbuf[pl.ds(me8, SUBLANES), :] = jnp.broadcast_to(c_ref[...], (SUBLANES, d))
        own_cw = pltpu.make_async_copy(cw_ref, cwf_ref.at[:, :, pl.ds(me * cwid, cwid)], lsem.at[0])
        own_cw.start()
        first = []
        for k in range(1, N_DEV):
            peer, _ = _peer(pos, k)
            rows = cbuf.at[pl.ds(me8, SUBLANES), :]
            first.append(_remote(rows, rows, ssem.at[0, k - 1], rsem.at[0, k - 1], peer))
            first.append(_remote(cw_ref, cwf_ref.at[:, :, pl.ds(me * cwid, cwid)], ssem.at[1, k - 1], rsem.at[1, k - 1], peer))
        for cp in first:
            cp.start()
        for k in range(1, N_DEV):
            peer, pj = _peer(pos, k)
            pj8 = pl.multiple_of(pj * SUBLANES, SUBLANES)
            rows = cbuf.at[pl.ds(pj8, SUBLANES), :]
            _remote(rows, rows, ssem.at[0, k - 1], rsem.at[0, k - 1], peer).wait_recv()
        cv = cbuf[...]
        cact = cv * _sigmoid(cv)
        cact_ref[...] = cact
        cb = cact.astype(BF16)
        for l in range(nl):
            pbuf[l] = _dot(cb, wm_ref[l].astype(BF16))
        own_p = pltpu.make_async_copy(pbuf.at[:, pl.ds(me8, SUBLANES), :], rbuf.at[me], lsem.at[1])
        own_p.start()
        second = []
        for k in range(1, N_DEV):
            peer, pj = _peer(pos, k)
            pj8 = pl.multiple_of(pj * SUBLANES, SUBLANES)
            second.append(_remote(pbuf.at[:, pl.ds(pj8, SUBLANES), :], rbuf.at[me], ssem.at[2, k - 1], rsem.at[2, k - 1], peer))
        for cp in second:
            cp.start()
        for k in range(1, N_DEV):
            peer, pj = _peer(pos, k)
            _remote(pbuf.at[:, pl.ds(0, SUBLANES), :], rbuf.at[pj], ssem.at[2, k - 1], rsem.at[2, k - 1], peer).wait_recv()
            _remote(cw_ref, cwf_ref.at[:, :, pl.ds(pj * cwid, cwid)], ssem.at[1, k - 1], rsem.at[1, k - 1], peer).wait_recv()
        own_p.wait()
        own_cw.wait()
        for l in range(nl):
            for j in range(N_DEV):
                mod_ref[l:l + 1, j * cm:(j + 1) * cm] = rbuf[j, l, 0:1, :] + bm_ref[l:l + 1, j * cm:(j + 1) * cm]
        for cp in first + second:
            cp.wait_send()

    return pl.pallas_call(
        body,
        name="prep_small",
        out_shape=(
            jax.ShapeDtypeStruct((nl, N_MOD * d), F32),
            jax.ShapeDtypeStruct((N_DEV * SUBLANES, d), F32),
            jax.ShapeDtypeStruct((nl, CW_ROWS, d), F32),
        ),
        in_specs=[VMEM_SPEC] * 4,
        out_specs=(VMEM_SPEC,) * 3,
        scratch_shapes=[
            pltpu.VMEM((N_DEV * SUBLANES, d), F32),
            pltpu.VMEM((nl, N_DEV * SUBLANES, cm), F32),
            pltpu.VMEM((N_DEV, nl, SUBLANES, cm), F32),
            pltpu.SemaphoreType.DMA((3, N_DEV - 1)),
            pltpu.SemaphoreType.DMA((3, N_DEV - 1)),
            pltpu.SemaphoreType.DMA((2,)),
        ],
        compiler_params=_cparams(vmem=3 * _nbytes(w_mod.shape, F32)),
    )(c, w_mod, b_mod, cw)


def _exchange_start(parts, gather, name):
    n = len(parts)
    lands = [lax.empty(((N_DEV,) + tuple(p.shape)) if gather else tuple(p.shape), p.dtype) for p in parts]

    def body(*refs):
        ins, lnd = refs[:n], refs[n:2 * n]
        ssem, rsem, token = refs[2 * n], refs[2 * n + 1], refs[-1]
        pos, me = _my_position()
        for k in range(1, N_DEV):
            peer, pj = _peer(pos, k)
            for t in range(n):
                src = ins[t] if gather else ins[t].at[pj]
                q = t * (N_DEV - 1) + k - 1
                _remote(src, lnd[t].at[me], ssem.at[q], rsem.at[q], peer).start()
        token[...] = jnp.zeros(token.shape, F32)

    out = pl.pallas_call(
        body,
        name=name,
        out_shape=(pltpu.SemaphoreType.DMA((n * (N_DEV - 1),)), pltpu.SemaphoreType.DMA((n * (N_DEV - 1),)))
        + tuple(pltpu.HBM(p.shape, p.dtype) for p in parts) + tuple(pltpu.HBM(p.shape, p.dtype) for p in lands)
        + (jax.ShapeDtypeStruct((SUBLANES, 128), F32),),
        in_specs=[HBM_SPEC] * (2 * n),
        out_specs=(SEM_SPEC, SEM_SPEC) + (HBM_SPEC,) * (2 * n) + (VMEM_SPEC,),
        input_output_aliases={i: 2 + i for i in range(2 * n)},
        compiler_params=pltpu.CompilerParams(has_side_effects=SIDE_EFFECT),
    )(*[pltpu.with_memory_space_constraint(p, pltpu.HBM) for p in list(parts) + lands])
    return out[0], out[1], list(out[2:2 + n]), list(out[2 + n:2 + 2 * n]), out[-1]


def _exchange_wait(started, after, gather, name):
    ssem, rsem, parts, lands, _ = started
    n = len(parts)

    def body(*refs):
        ins, lnd = refs[:n], refs[n:2 * n]
        ssem_ref, rsem_ref = refs[2 * n], refs[2 * n + 1]
        stage, lsem = refs[-1 - n:-1], refs[-1]
        pos, me = _my_position()
        load = []
        for t in range(n):
            src = ins[t] if gather else ins[t].at[me]
            load.append(pltpu.make_async_copy(src, stage[t], lsem.at[t]))
            load[-1].start()
        store = []
        for t in range(n):
            load[t].wait()
            store.append(pltpu.make_async_copy(stage[t], lnd[t].at[me], lsem.at[t]))
            store[-1].start()
        for k in range(1, N_DEV):
            peer, pj = _peer(pos, k)
            for t in range(n):
                src = ins[t] if gather else ins[t].at[pj]
                q = t * (N_DEV - 1) + k - 1
                _remote(src, lnd[t].at[me], ssem_ref.at[q], rsem_ref.at[q], peer).wait_send()
                _remote(src, lnd[t].at[pj], ssem_ref.at[q], rsem_ref.at[q], peer).wait_recv()
        for cp in store:
            cp.wait()

    out = pl.pallas_call(
        body,
        name=name,
        out_shape=tuple(pltpu.HBM(p.shape, p.dtype) for p in parts) + tuple(pltpu.HBM(p.shape, p.dtype) for p in lands),
        in_specs=[HBM_SPEC] * (2 * n) + [SEM_SPEC, SEM_SPEC, ANY_SPEC],
        out_specs=(HBM_SPEC,) * (2 * n),
        input_output_aliases={i: i for i in range(2 * n)},
        scratch_shapes=[pltpu.VMEM(tuple(z.shape[1:]), z.dtype) for z in lands] + [pltpu.SemaphoreType.DMA((n,))],
        compiler_params=pltpu.CompilerParams(has_side_effects=SIDE_EFFECT),
    )(*parts, *lands, ssem, rsem, after)
    return list(out[n:])


CHIP_PEERS = 3


def _chip_peer(pos, k, core):
    x, y, _ = pos
    px = 1 - x if k & 2 else x
    py = 1 - y if k & 1 else y
    return (px, py, core), 4 * px + 2 * py + core


def _gather2_start(parts, after, name):
    n = len(parts)
    lands = [lax.empty((N_DEV,) + tuple(p.shape), p.dtype) for p in parts]
    per = CHIP_PEERS + 1

    def body(*refs):
        ins, lnd = refs[:n], refs[n:2 * n]
        ssem, rsem, token = refs[2 * n + 1], refs[2 * n + 2], refs[-1]
        pos, me = _my_position()
        sibling = (pos[0], pos[1], 1 - pos[2])
        for t in range(n):
            _remote(ins[t], lnd[t].at[me], ssem.at[per * t], rsem.at[per * t], sibling).start()
        for k in range(1, per):
            peer, _ = _chip_peer(pos, k, pos[2])
            for t in range(n):
                _remote(ins[t], lnd[t].at[me], ssem.at[per * t + k], rsem.at[per * t + k], peer).start()
        token[...] = jnp.zeros(token.shape, F32)

    out = pl.pallas_call(
        body,
        name=name,
        out_shape=(pltpu.SemaphoreType.DMA((n * per,)), pltpu.SemaphoreType.DMA((n * per,)))
        + tuple(pltpu.HBM(p.shape, p.dtype) for p in parts) + tuple(pltpu.HBM(p.shape, p.dtype) for p in lands)
        + (jax.ShapeDtypeStruct((SUBLANES, 128), F32),),
        in_specs=[HBM_SPEC] * (2 * n) + [ANY_SPEC],
        out_specs=(SEM_SPEC, SEM_SPEC) + (HBM_SPEC,) * (2 * n) + (VMEM_SPEC,),
        input_output_aliases={i: 2 + i for i in range(2 * n)},
        compiler_params=pltpu.CompilerParams(has_side_effects=SIDE_EFFECT),
    )(*[pltpu.with_memory_space_constraint(p, pltpu.HBM) for p in list(parts) + lands], after)
    return out[0], out[1], list(out[2:2 + n]), list(out[2 + n:2 + 2 * n]), out[-1]


def _gather2_forward(started, after, name):
    ssem, rsem, parts, lands, _ = started
    n = len(lands)
    per = CHIP_PEERS + 1

    def body(*refs):
        lnd = refs[:n]
        rsem_a, fsend, frecv, token = refs[n], refs[2 * n + 2], refs[2 * n + 3], refs[2 * n + 4]
        pos, _ = _my_position()
        sibling = (pos[0], pos[1], 1 - pos[2])
        for k in range(1, per):
            peer, pk = _chip_peer(pos, k, pos[2])
            for t in range(n):
                block = lnd[t].at[pk]
                _remote(block, block, rsem_a.at[per * t + k], rsem_a.at[per * t + k], peer).wait_recv()
                q = CHIP_PEERS * t + k - 1
                _remote(block, block, fsend.at[q], frecv.at[q], sibling).start()
        token[...] = jnp.zeros(token.shape, F32)

    out = pl.pallas_call(
        body,
        name=name,
        out_shape=tuple(pltpu.HBM(p.shape, p.dtype) for p in lands)
        + (pltpu.SemaphoreType.DMA((n * CHIP_PEERS,)), pltpu.SemaphoreType.DMA((n * CHIP_PEERS,)),
           jax.ShapeDtypeStruct((SUBLANES, 128), F32)),
        in_specs=[HBM_SPEC] * n + [SEM_SPEC, ANY_SPEC],
        out_specs=(HBM_SPEC,) * n + (SEM_SPEC, SEM_SPEC, VMEM_SPEC),
        input_output_aliases={i: i for i in range(n)},
        compiler_params=pltpu.CompilerParams(has_side_effects=SIDE_EFFECT),
    )(*lands, rsem, after)
    return ssem, rsem, parts, list(out[:n]), out[n + 2], out[n], out[n + 1]


def _gather2_wait(forwarded, after, name):
    ssem, rsem, parts, lands, _, fsend, frecv = forwarded
    n = len(parts)
    per = CHIP_PEERS + 1

    def body(*refs):
        ins, lnd = refs[:n], refs[n:2 * n]
        ssem_a, rsem_a, fs, fr = refs[2 * n:2 * n + 4]
        stage, lsem = refs[-1 - n:-1], refs[-1]
        pos, me = _my_position()
        sibling = (pos[0], pos[1], 1 - pos[2])
        sib = 4 * pos[0] + 2 * pos[1] + 1 - pos[2]
        load = []
        for t in range(n):
            load.append(pltpu.make_async_copy(ins[t], stage[t], lsem.at[t]))
            load[-1].start()
        store = []
        for t in range(n):
            load[t].wait()
            store.append(pltpu.make_async_copy(stage[t], lnd[t].at[me], lsem.at[t]))
            store[-1].start()
        for t in range(n):
            _remote(ins[t], lnd[t].at[me], ssem_a.at[per * t], rsem_a.at[per * t], sibling).wait_send()
            _remote(ins[t], lnd[t].at[sib], ssem_a.at[per * t], rsem_a.at[per * t], sibling).wait_recv()
        for k in range(1, per):
            peer, pk = _chip_peer(pos, k, pos[2])
            _, qk = _chip_peer(pos, k, 1 - pos[2])
            for t in range(n):
                q = CHIP_PEERS * t + k - 1
                _remote(ins[t], lnd[t].at[me], ssem_a.at[per * t + k], rsem_a.at[per * t + k], peer).wait_send()
                _remote(lnd[t].at[pk], lnd[t].at[pk], fs.at[q], fr.at[q], sibling).wait_send()
                _remote(lnd[t].at[qk], lnd[t].at[qk], fs.at[q], fr.at[q], sibling).wait_recv()
        for cp in store:
            cp.wait()

    out = pl.pallas_call(
        body,
        name=name,
        out_shape=tuple(pltpu.HBM(p.shape, p.dtype) for p in parts) + tuple(pltpu.HBM(p.shape, p.dtype) for p in lands),
        in_specs=[HBM_SPEC] * (2 * n) + [SEM_SPEC] * 4 + [ANY_SPEC],
        out_specs=(HBM_SPEC,) * (2 * n),
        input_output_aliases={i: i for i in range(2 * n)},
        scratch_shapes=[pltpu.VMEM(tuple(z.shape[1:]), z.dtype) for z in lands] + [pltpu.SemaphoreType.DMA((n,))],
        compiler_params=pltpu.CompilerParams(has_side_effects=SIDE_EFFECT),
    )(*parts, *lands, ssem, rsem, fsend, frecv, after)
    return list(out[n:])


def _reduce_small(rows, dm8, cact, after):
    r, d = rows.shape
    nl = dm8.shape[0]
    cm = dm8.shape[2] // N_DEV

    def body(rows_ref, dm_ref, cact_ref, after_ref, orow_ref, owm_ref, gr, dmr, ssem, rsem, lsem):
        pos, me = _my_position()
        me8 = pl.multiple_of(me * SUBLANES, SUBLANES)
        gr[me] = rows_ref[...]
        own_dm = pltpu.make_async_copy(dm_ref.at[:, :, pl.ds(me * cm, cm)], dmr.at[:, pl.ds(me8, SUBLANES), :], lsem.at[0])
        own_dm.start()
        sends = []
        for k in range(1, N_DEV):
            peer, pj = _peer(pos, k)
            sends.append(_remote(gr.at[me], gr.at[me], ssem.at[0, k - 1], rsem.at[0, k - 1], peer))
            sends.append(_remote(dm_ref.at[:, :, pl.ds(pj * cm, cm)], dmr.at[:, pl.ds(me8, SUBLANES), :],
                                 ssem.at[1, k - 1], rsem.at[1, k - 1], peer))
        for cp in sends:
            cp.start()
        for k in range(1, N_DEV):
            peer, pj = _peer(pos, k)
            pj8 = pl.multiple_of(pj * SUBLANES, SUBLANES)
            _remote(gr.at[pj], gr.at[pj], ssem.at[0, k - 1], rsem.at[0, k - 1], peer).wait_recv()
            _remote(dm_ref.at[:, :, pl.ds(0, cm)], dmr.at[:, pl.ds(pj8, SUBLANES), :], ssem.at[1, k - 1], rsem.at[1, k - 1], peer).wait_recv()
        own_dm.wait()
        acc = gr[0]
        for j in range(1, N_DEV):
            acc = acc + gr[j]
        orow_ref[...] = acc
        cb = cact_ref[...].astype(BF16)
        for l in range(nl):
            owm_ref[l] = _dot_tn(cb, dmr[l].astype(BF16))
        for cp in sends:
            cp.wait_send()

    return pl.pallas_call(
        body,
        name="reduce_small",
        out_shape=(jax.ShapeDtypeStruct((r, d), F32), jax.ShapeDtypeStruct((nl, d, cm), F32)),
        in_specs=[VMEM_SPEC] * 3 + [ANY_SPEC],
        out_specs=(VMEM_SPEC,) * 2,
        scratch_shapes=[
            pltpu.VMEM((N_DEV, r, d), F32),
            pltpu.VMEM((nl, N_DEV * SUBLANES, cm), F32),
            pltpu.SemaphoreType.DMA((2, N_DEV - 1)),
            pltpu.SemaphoreType.DMA((2, N_DEV - 1)),
            pltpu.SemaphoreType.DMA((1,)),
        ],
        compiler_params=_cparams(vmem=4 * _nbytes((N_DEV, r, d), F32) + 6 * _nbytes((nl, d, cm), F32)),
    )(rows, dm8, cact, after)


def _sum_gathered(zones):
    nl = len(zones)

    def body(*refs):
        for l in range(nl):
            acc = refs[l][0].astype(F32)
            for j in range(1, N_DEV):
                acc = acc + refs[l][j].astype(F32)
            refs[nl][l] = acc

    return pl.pallas_call(
        body,
        name="sum_gathered",
        out_shape=jax.ShapeDtypeStruct((nl,) + tuple(zones[0].shape[1:]), F32),
        in_specs=[VMEM_SPEC] * nl,
        out_specs=VMEM_SPEC,
        compiler_params=_cparams(vmem=8 * nl * _nbytes(zones[0].shape, BF16)),
    )(*zones)


def _in_proj_fwd(x, mod, rows, win_f, l, tm, after):
    s, d = x.shape
    nb, _, ci = win_f.shape

    def body(x_ref, mod_ref, rows_ref, w_ref, after_ref, proj_ref, h_ref):
        xv = x_ref[...]
        g = rows_ref[R_G_PRE_MIX:R_G_PRE_MIX + 1, :]
        h = (xv * _rms(xv) * g) * (1.0 + mod_ref[M_SC_M:M_SC_M + 1, :]) + mod_ref[M_SH_M:M_SH_M + 1, :]
        hb = h.astype(BF16)
        h_ref[...] = hb
        for j in range(nb):
            proj_ref[:, j * ci:(j + 1) * ci] = _dot(hb, w_ref[j])

    return pl.pallas_call(
        body,
        name="in_proj_fwd",
        grid=(s // tm,),
        in_specs=[
            pl.BlockSpec((tm, d), lambda i: (i, 0)),
            _resident((None, N_MOD, d), lambda i: (l, 0, 0)),
            _resident((None, N_ROWS, d), lambda i: (l, 0, 0)),
            _resident((nb, d, ci), lambda i: (0, 0, 0)),
            ANY_SPEC,
        ],
        out_specs=(pl.BlockSpec((tm, nb * ci), lambda i: (i, 0)), pl.BlockSpec((tm, d), lambda i: (i, 0))),
        out_shape=(jax.ShapeDtypeStruct((s, nb * ci), F32), jax.ShapeDtypeStruct((s, d), BF16)),
        compiler_params=_cparams(("parallel",), _nbytes((nb, d, ci), BF16) + 3 * _nbytes((tm, nb * ci), F32) + 8 * _nbytes((tm, d), F32)),
    )(*_hbm(x), mod, rows, win_f, after)


def _mixer_core_fwd(proj, x, mod, rows, cwf, wr, wi, wa_f, wb_f, wo_f, l, tm, after):
    s, d = x.shape
    nh, bw, _ = wr.shape[1:]

    def body(proj_ref, x_ref, mod_ref, rows_ref, cw_ref, wr_ref, wi_ref, wa_ref, wb_ref, wo_ref, after_ref,
             x1_ref, hs_ref, yap_ref, ybp_ref, y_ref, cvbuf, xbbuf, a_s, b_s, hprev):
        i = pl.program_id(0)

        @pl.when(i == 0)
        def _():
            cvbuf[...] = jnp.zeros((SUBLANES, d), F32)
            xbbuf[...] = jnp.zeros((SUBLANES, d), F32)
            hprev[...] = jnp.zeros((SUBLANES, d), F32)

        def row(r):
            return rows_ref[r:r + 1, :]

        def tap(r):
            return cw_ref[r:r + 1, :]

        ba = proj_ref[:, 0:d]
        cv = proj_ref[:, d:2 * d] * proj_ref[:, 2 * d:3 * d]
        cvt = cvbuf[...]
        conv3 = ((row(R_CONV_A_B) + _shift_rows(cv, cvt, 2, False) * tap(CW_A)) + _shift_rows(cv, cvt, 1, False) * tap(CW_A + 1)) + cv * tap(CW_A + 2)
        ya = ba * conv3
        cvbuf[...] = cv[tm - SUBLANES:, :]
        xb = proj_ref[:, 3 * d:4 * d]
        xbt = xbbuf[...]
        xc = (((row(R_CONV_B_B) + _shift_rows(xb, xbt, 3, False) * tap(CW_B)) + _shift_rows(xb, xbt, 2, False) * tap(CW_B + 1))
              + _shift_rows(xb, xbt, 1, False) * tap(CW_B + 2)) + xb * tap(CW_B + 3)
        xbbuf[...] = xb[tm - SUBLANES:, :]
        sp = _softplus_neg(row(R_LAMBDA))
        _, _, ig, a, mult = _gates(xc, wr_ref, wi_ref, row(R_B_GATE_R), row(R_B_GATE_I), sp, nh, bw)
        a_s[...] = a
        b_s[...] = mult * (ig * xc)

        def blk(j, hp):
            o = pl.multiple_of(j * SUBLANES, SUBLANES)
            a8, b8 = _scan_block(a_s[pl.ds(o, SUBLANES), :], b_s[pl.ds(o, SUBLANES), :], reverse=False)
            h8 = b8 + a8 * hp
            hs_ref[pl.ds(o, SUBLANES), :] = h8
            return jnp.broadcast_to(h8[SUBLANES - 1:SUBLANES, :], (SUBLANES, d))

        hprev[...] = lax.fori_loop(0, tm // SUBLANES, blk, hprev[...])
        gel, _ = _gelu(proj_ref[:, 4 * d:5 * d])
        yb = hs_ref[...] * gel
        yap = _dot(ya.astype(BF16), wa_ref[...])
        ybp = _dot(yb.astype(BF16), wb_ref[...])
        yap_ref[...] = yap
        ybp_ref[...] = ybp
        m = _sigmoid(proj_ref[:, 5 * d:6 * d]) * yap + _sigmoid(proj_ref[:, 6 * d:7 * d]) * ybp
        y = _dot(m.astype(BF16), wo_ref[...])
        y_ref[...] = y
        x1_ref[...] = x_ref[...] + mod_ref[M_GT_M:M_GT_M + 1, :] * ((y * _rms(y)) * row(R_G_POST_MIX))

    tile = pl.BlockSpec((tm, d), lambda i: (i, 0))
    return pl.pallas_call(
        body,
        name="mixer_core_fwd",
        grid=(s // tm,),
        in_specs=[
            pl.BlockSpec((tm, 7 * d), lambda i: (i, 0)),
            tile,
            _resident((None, N_MOD, d), lambda i: (l, 0, 0)),
            _resident((None, N_ROWS, d), lambda i: (l, 0, 0)),
            _resident((None, CW_ROWS, d), lambda i: (l, 0, 0)),
            _resident((None, nh, bw, bw), lambda i: (l, 0, 0, 0)),
            _resident((None, nh, bw, bw), lambda i: (l, 0, 0, 0)),
            _resident((d, d), lambda i: (0, 0)),
            _resident((d, d), lambda i: (0, 0)),
            _resident((d, d), lambda i: (0, 0)),
            ANY_SPEC,
        ],
        out_specs=(tile,) * 5,
        out_shape=(jax.ShapeDtypeStruct((s, d), F32),) * 5,
        scratch_shapes=[
            pltpu.VMEM((SUBLANES, d), F32),
            pltpu.VMEM((SUBLANES, d), F32),
            pltpu.VMEM((tm, d), F32),
            pltpu.VMEM((tm, d), F32),
            pltpu.VMEM((SUBLANES, d), F32),
        ],
        compiler_params=_cparams(("arbitrary",), 3 * _nbytes((d, d), BF16) + 2 * _nbytes((tm, 7 * d), F32) + 40 * _nbytes((tm, d), F32)),
    )(*_hbm(proj, x), mod, rows, cwf, wr, wi, wa_f, wb_f, wo_f, after)


def _mlp_fwd(x1, mod, rows, wup_f, wdn_f, l, tm, after, target=None):
    s, d = x1.shape
    nb, _, cu = wup_f.shape
    dff = nb * cu
    with_loss = target is not None

    def body(x1_ref, mod_ref, rows_ref, wu_ref, wd_ref, after_ref, *refs):
        x2_ref, ru_ref, y2_ref, h2_ref = refs[-6:-2] if with_loss else refs
        xv = x1_ref[...]
        g = rows_ref[R_G_PRE_MLP:R_G_PRE_MLP + 1, :]
        h2 = ((xv * _rms(xv) * g) * (1.0 + mod_ref[M_SC_F:M_SC_F + 1, :]) + mod_ref[M_SH_F:M_SH_F + 1, :]).astype(BF16)
        h2_ref[...] = h2
        ru = jnp.concatenate([jnp.maximum(_dot(h2, wu_ref[j]), 0.0) for j in range(nb)], axis=1)
        ru_ref[...] = ru.astype(BF16)
        y2 = _dot((ru * ru).astype(BF16), wd_ref[...])
        y2_ref[...] = y2
        x2 = xv + mod_ref[M_GT_F:M_GT_F + 1, :] * ((y2 * _rms(y2)) * rows_ref[R_G_POST_MLP:R_G_POST_MLP + 1, :])
        x2_ref[...] = x2
        if with_loss:
            t_ref, dy_ref, loss_ref = refs[0], refs[-2], refs[-1]

            @pl.when(pl.program_id(0) == 0)
            def _():
                loss_ref[...] = jnp.zeros(loss_ref.shape, F32)

            e = x2 - t_ref[...]
            dy_ref[...] = e * (1.0 / d)
            loss_ref[...] += 0.5 * jnp.sum(jnp.mean(e * e, axis=-1, keepdims=True), axis=0, keepdims=True)

    tile = pl.BlockSpec((tm, d), lambda i: (i, 0))
    wide = pl.BlockSpec((tm, dff), lambda i: (i, 0))
    out_specs = (tile, wide, tile, tile)
    out_shape = (jax.ShapeDtypeStruct((s, d), F32), jax.ShapeDtypeStruct((s, dff), BF16),
                 jax.ShapeDtypeStruct((s, d), F32), jax.ShapeDtypeStruct((s, d), BF16))
    if with_loss:
        out_specs += (tile, pl.BlockSpec((SUBLANES, 128), lambda i: (0, 0)))
        out_shape += (jax.ShapeDtypeStruct((s, d), F32), jax.ShapeDtypeStruct((SUBLANES, 128), F32))
    return pl.pallas_call(
        body,
        name="mlp_fwd",
        grid=(s // tm,),
        in_specs=[
            tile,
            _resident((None, N_MOD, d), lambda i: (l, 0, 0)),
            _resident((None, N_ROWS, d), lambda i: (l, 0, 0)),
            _resident((nb, d, cu), lambda i: (0, 0, 0)),
            _resident((dff, d), lambda i: (0, 0)),
            ANY_SPEC,
        ] + ([tile] if with_loss else []),
        out_specs=out_specs,
        out_shape=out_shape,
        compiler_params=_cparams(("arbitrary",), 2 * _nbytes((dff, d), BF16) + 5 * _nbytes((tm, dff), F32) + 16 * _nbytes((tm, d), F32)),
    )(*_hbm(x1), mod, rows, wup_f, wdn_f, after, *(_hbm(target) if with_loss else ()))


def _mlp_bwd(dx2, x1, y2, ru, mod, rows, wup_f, wdn_f, l, tm):
    s, d = x1.shape
    nb, _, cu = wup_f.shape
    dff = nb * cu

    def body(dx2_ref, x1_ref, y2_ref, ru_ref, mod_ref, rows_ref, wu_ref, wd_ref, dx1_ref, dy2_ref, dup_ref, act_ref, sm_ref):
        @pl.when(pl.program_id(0) == 0)
        def _():
            sm_ref[...] = jnp.zeros(sm_ref.shape, F32)

        dout = dx2_ref[...]
        dy2, dgt, dgpost = _postnorm_bwd(y2_ref[...], dout, rows_ref[R_G_POST_MLP:R_G_POST_MLP + 1, :], mod_ref[M_GT_F:M_GT_F + 1, :])
        dy2b = dy2.astype(BF16)
        dy2_ref[...] = dy2b
        ruv = ru_ref[...].astype(F32)
        act_ref[...] = (ruv * ruv).astype(BF16)
        dup = (_dot_nt(dy2b, wd_ref[...]) * (2.0 * ruv)).astype(BF16)
        dup_ref[...] = dup
        dh2 = _dot_nt(dup[:, 0:cu], wu_ref[0])
        for j in range(1, nb):
            dh2 = dh2 + _dot_nt(dup[:, j * cu:(j + 1) * cu], wu_ref[j])
        dxn, dsc, dsh, dgpre = _prenorm_bwd(x1_ref[...], dh2, rows_ref[R_G_PRE_MLP:R_G_PRE_MLP + 1, :], mod_ref[M_SC_F:M_SC_F + 1, :])
        dx1_ref[...] = dout + dxn
        for r, v in ((G_MLP_GT, dgt), (G_MLP_GPOST, dgpost), (G_MLP_SC, dsc), (G_MLP_SH, dsh), (G_MLP_GPRE, dgpre)):
            sm_ref[r:r + 1, :] += v

    tile = pl.BlockSpec((tm, d), lambda i: (i, 0))
    wide = pl.BlockSpec((tm, dff), lambda i: (i, 0))
    return pl.pallas_call(
        body,
        name="mlp_bwd",
        grid=(s // tm,),
        in_specs=[
            tile, tile, tile, wide,
            _resident((None, N_MOD, d), lambda i: (l, 0, 0)),
            _resident((None, N_ROWS, d), lambda i: (l, 0, 0)),
            _resident((nb, d, cu), lambda i: (0, 0, 0)),
            _resident((dff, d), lambda i: (0, 0)),
        ],
        out_specs=(tile, tile, wide, wide, pl.BlockSpec((G_MLP_ROWS, d), lambda i: (0, 0))),
        out_shape=(jax.ShapeDtypeStruct((s, d), F32), jax.ShapeDtypeStruct((s, d), BF16), jax.ShapeDtypeStruct((s, dff), BF16),
                   jax.ShapeDtypeStruct((s, dff), BF16), jax.ShapeDtypeStruct((G_MLP_ROWS, d), F32)),
        compiler_params=_cparams(("arbitrary",), 2 * _nbytes((dff, d), BF16) + 6 * _nbytes((tm, dff), F32) + 16 * _nbytes((tm, d), F32)),
    )(*_hbm(dx2, x1, y2, ru), mod, rows, wup_f, wdn_f)


def _mixer_core_bwd(dx1, y, yap, ybp, hs, proj, mod, rows, cwf, wr, wi, wa_f, wb_f, wo_f, l, tm, after):
    s, d = dx1.shape
    nh, bw, _ = wr.shape[1:]
    nt = s // tm
    per = tm // SUBLANES

    def body(dx1_ref, y_ref, yap_ref, ybp_ref, hs_ref, hsh_ref, proj_ref, projh_ref, mod_ref, rows_ref, cw_ref,
             wr_ref, wi_ref, wa_ref, wb_ref, wo_ref, after_ref,
             dproj_ref, dy_ref, m_ref, dyap_ref, dybp_ref, ya_ref, yb_ref, sm_ref, dwg_ref,
             abuf, dcbuf, dxbuf, al_s, dh_s, lam_s, lnext):
        i = pl.program_id(0)
        first_tile = i == nt - 1

        @pl.when(i == 0)
        def _():
            sm_ref[...] = jnp.zeros(sm_ref.shape, F32)
            dwg_ref[...] = jnp.zeros(dwg_ref.shape, F32)
            zero = jnp.zeros((SUBLANES, d), F32)
            abuf[...] = zero
            dcbuf[...] = zero
            dxbuf[...] = zero
            lnext[...] = zero

        def row(r):
            return rows_ref[r:r + 1, :]

        def tap(r):
            return cw_ref[r:r + 1, :]

        def acc(r, v):
            sm_ref[r:r + 1, :] += v

        keep_halo = jnp.where(first_tile, 0.0, 1.0)
        dy, dgt, dgpost = _postnorm_bwd(y_ref[...], dx1_ref[...], row(R_G_POST_MIX), mod_ref[M_GT_M:M_GT_M + 1, :])
        acc(G_MIX_GT, dgt)
        acc(G_MIX_GPOST, dgpost)
        dyb16 = dy.astype(BF16)
        dy_ref[...] = dyb16
        dm = _dot_nt(dyb16, wo_ref[...])
        sa = _sigmoid(proj_ref[:, 5 * d:6 * d])
        sb = _sigmoid(proj_ref[:, 6 * d:7 * d])
        yap = yap_ref[...]
        ybp = ybp_ref[...]
        m_ref[...] = (sa * yap + sb * ybp).astype(BF16)
        dyap_f = dm * sa
        dybp_f = dm * sb
        dyap = dyap_f.astype(BF16)
        dybp = dybp_f.astype(BF16)
        dyap_ref[...] = dyap
        dybp_ref[...] = dybp
        dproj_ref[:, 5 * d:6 * d] = (dyap_f * yap * (1.0 - sa)).astype(BF16)
        dproj_ref[:, 6 * d:7 * d] = (dybp_f * ybp * (1.0 - sb)).astype(BF16)
        dya = _dot_nt(dyap, wa_ref[...])
        dyb = _dot_nt(dybp, wb_ref[...])
        ba = proj_ref[:, 0:d]
        ca = proj_ref[:, d:2 * d]
        va = proj_ref[:, 2 * d:3 * d]
        cv = ca * va
        cvh = keep_halo * (projh_ref[:, d:2 * d] * projh_ref[:, 2 * d:3 * d])
        cvm2 = _shift_rows(cv, cvh, 2, False)
        cvm1 = _shift_rows(cv, cvh, 1, False)
        conv3 = ((row(R_CONV_A_B) + cvm2 * tap(CW_A)) + cvm1 * tap(CW_A + 1)) + cv * tap(CW_A + 2)
        ya_ref[...] = (ba * conv3).astype(BF16)
        dproj_ref[:, 0:d] = (dya * conv3).astype(BF16)
        dc3 = dya * ba
        acc(G_MIX_CAB, _colsum(dc3))
        acc(G_MIX_CAW, _colsum(dc3 * cvm2))
        acc(G_MIX_CAW + 1, _colsum(dc3 * cvm1))
        acc(G_MIX_CAW + 2, _colsum(dc3 * cv))
        dct = dcbuf[...]
        dcv = (dc3 * tap(CW_A + 2) + _shift_rows(dc3, dct, 1, True) * tap(CW_A + 1)) + _shift_rows(dc3, dct, 2, True) * tap(CW_A)
        dcbuf[...] = dc3[:SUBLANES, :]
        dproj_ref[:, d:2 * d] = (dcv * va).astype(BF16)
        dproj_ref[:, 2 * d:3 * d] = (dcv * ca).astype(BF16)
        xb = proj_ref[:, 3 * d:4 * d]
        gb = proj_ref[:, 4 * d:5 * d]
        xbh = keep_halo * projh_ref[:, 3 * d:4 * d]
        xm3 = _shift_rows(xb, xbh, 3, False)
        xm2 = _shift_rows(xb, xbh, 2, False)
        xm1 = _shift_rows(xb, xbh, 1, False)
        xc = (((row(R_CONV_B_B) + xm3 * tap(CW_B)) + xm2 * tap(CW_B + 1)) + xm1 * tap(CW_B + 2)) + xb * tap(CW_B + 3)
        lam = row(R_LAMBDA)
        sp = _softplus_neg(lam)
        xcb, r, ig, a, mult = _gates(xc, wr_ref, wi_ref, row(R_B_GATE_R), row(R_B_GATE_I), sp, nh, bw)
        gel, gsig = _gelu(gb)
        hs = hs_ref[...]
        yb_ref[...] = (hs * gel).astype(BF16)
        dproj_ref[:, 4 * d:5 * d] = (dyb * hs * _gelu_grad(gb, gsig)).astype(BF16)
        al_s[...] = _shift_rows(a, abuf[...], 1, True)
        abuf[...] = a[:SUBLANES, :]
        dh_s[...] = dyb * gel

        def blk(j, ln):
            o = pl.multiple_of((per - 1 - j) * SUBLANES, SUBLANES)
            a8, b8 = _scan_block(al_s[pl.ds(o, SUBLANES), :], dh_s[pl.ds(o, SUBLANES), :], reverse=True)
            l8 = b8 + a8 * ln
            lam_s[pl.ds(o, SUBLANES), :] = l8
            return jnp.broadcast_to(l8[0:1, :], (SUBLANES, d))

        lnext[...] = lax.fori_loop(0, per, blk, lnext[...])
        dbb = lam_s[...]
        da = dbb * _shift_rows(hs, keep_halo * hsh_ref[...], 1, False)
        dbx = dbb * xc
        dmult = dbx * ig
        dig = dbx * mult
        dxc = (dbb * mult) * ig
        dla = a * (da - (dmult * a) / mult)
        dlar = dla * r
        acc(G_MIX_LAM, _colsum(dlar) * (LRU_C * _sigmoid(-lam)))
        dzr = (dlar * (1.0 - r)) * (-LRU_C * sp)
        dzi = dig * ig * (1.0 - ig)
        acc(G_MIX_BR, _colsum(dzr))
        acc(G_MIX_BI, _colsum(dzi))
        dzrb = dzr.astype(BF16)
        dzib = dzi.astype(BF16)
        back = []
        for h in range(nh):
            sl = slice(h * bw, (h + 1) * bw)
            back.append(_dot_nt(dzrb[:, sl], wr_ref[h]) + _dot_nt(dzib[:, sl], wi_ref[h]))
            dwg_ref[0, h] += _dot_tn(xcb[:, sl], dzrb[:, sl])
            dwg_ref[1, h] += _dot_tn(xcb[:, sl], dzib[:, sl])
        dxc = dxc + jnp.concatenate(back, axis=1)
        acc(G_MIX_CBB, _colsum(dxc))
        acc(G_MIX_CBW, _colsum(dxc * xm3))
        acc(G_MIX_CBW + 1, _colsum(dxc * xm2))
        acc(G_MIX_CBW + 2, _colsum(dxc * xm1))
        acc(G_MIX_CBW + 3, _colsum(dxc * xb))
        dxt = dxbuf[...]
        dxb = (((dxc * tap(CW_B + 3) + _shift_rows(dxc, dxt, 1, True) * tap(CW_B + 2)) + _shift_rows(dxc, dxt, 2, True) * tap(CW_B + 1))
               + _shift_rows(dxc, dxt, 3, True) * tap(CW_B))
        dxbuf[...] = dxc[:SUBLANES, :]
        dproj_ref[:, 3 * d:4 * d] = dxb.astype(BF16)

    def rev(i):
        return (nt - 1 - i, 0)

    def halo(i):
        return (jnp.maximum((nt - 1 - i) * per - 1, 0), 0)

    tile = pl.BlockSpec((tm, d), rev)
    return pl.pallas_call(
        body,
        name="mixer_core_bwd",
        grid=(nt,),
        in_specs=[
            tile, tile, tile, tile, tile,
            pl.BlockSpec((SUBLANES, d), halo),
            pl.BlockSpec((tm, 7 * d), rev),
            pl.BlockSpec((SUBLANES, 7 * d), halo),
            _resident((None, N_MOD, d), lambda i: (l, 0, 0)),
            _resident((None, N_ROWS, d), lambda i: (l, 0, 0)),
            _resident((None, CW_ROWS, d), lambda i: (l, 0, 0)),
            _resident((None, nh, bw, bw), lambda i: (l, 0, 0, 0)),
            _resident((None, nh, bw, bw), lambda i: (l, 0, 0, 0)),
            _resident((d, d), lambda i: (0, 0)),
            _resident((d, d), lambda i: (0, 0)),
            _resident((d, d), lambda i: (0, 0)),
            ANY_SPEC,
        ],
        out_specs=(pl.BlockSpec((tm, 7 * d), rev),) + (tile,) * 6 + (
            pl.BlockSpec((G_MIX_ROWS, d), lambda i: (0, 0)), pl.BlockSpec((2, nh, bw, bw), lambda i: (0, 0, 0, 0))),
        out_shape=(jax.ShapeDtypeStruct((s, 7 * d), BF16),) + (jax.ShapeDtypeStruct((s, d), BF16),) * 6 + (
            jax.ShapeDtypeStruct((G_MIX_ROWS, d), F32), jax.ShapeDtypeStruct((2, nh, bw, bw), F32)),
        scratch_shapes=[pltpu.VMEM((SUBLANES, d), F32)] * 3 + [pltpu.VMEM((tm, d), F32)] * 3 + [pltpu.VMEM((SUBLANES, d), F32)],
        compiler_params=_cparams(("arbitrary",), 3 * _nbytes((d, d), BF16) + 3 * _nbytes((tm, 7 * d), F32) + 64 * _nbytes((tm, d), F32)),
    )(*_hbm(dx1, y, yap, ybp, hs, hs, proj, proj), mod, rows, cwf, wr, wi, wa_f, wb_f, wo_f, after)


def _in_proj_bwd(dproj, x, dx1, mod, rows, win_f, l, tm, after):
    s, d = x.shape
    nb, _, ci = win_f.shape

    def body(dp_ref, x_ref, dx1_ref, mod_ref, rows_ref, w_ref, after_ref, dx_ref, sm_ref):
        @pl.when(pl.program_id(0) == 0)
        def _():
            sm_ref[...] = jnp.zeros(sm_ref.shape, F32)

        dh = _dot_nt(dp_ref[:, 0:ci], w_ref[0])
        for j in range(1, nb):
            dh = dh + _dot_nt(dp_ref[:, j * ci:(j + 1) * ci], w_ref[j])
        dxn, dsc, dsh, dg = _prenorm_bwd(x_ref[...], dh, rows_ref[R_G_PRE_MIX:R_G_PRE_MIX + 1, :], mod_ref[M_SC_M:M_SC_M + 1, :])
        dx_ref[...] = dx1_ref[...] + dxn
        for r, v in ((G_IN_SC, dsc), (G_IN_SH, dsh), (G_IN_GPRE, dg)):
            sm_ref[r:r + 1, :] += v

    tile = pl.BlockSpec((tm, d), lambda i: (i, 0))
    return pl.pallas_call(
        body,
        name="in_proj_bwd",
        grid=(s // tm,),
        in_specs=[
            pl.BlockSpec((tm, nb * ci), lambda i: (i, 0)), tile, tile,
            _resident((None, N_MOD, d), lambda i: (l, 0, 0)),
            _resident((None, N_ROWS, d), lambda i: (l, 0, 0)),
            _resident((nb, d, ci), lambda i: (0, 0, 0)),
            ANY_SPEC,
        ],
        out_specs=(tile, pl.BlockSpec((G_IN_ROWS, d), lambda i: (0, 0))),
        out_shape=(jax.ShapeDtypeStruct((s, d), F32), jax.ShapeDtypeStruct((G_IN_ROWS, d), F32)),
        compiler_params=_cparams(("arbitrary",), _nbytes((nb, d, ci), BF16) + 2 * _nbytes((tm, nb * ci), BF16) + 16 * _nbytes((tm, d), F32)),
    )(*_hbm(dproj, x, dx1), mod, rows, win_f, after)


def _wgrad(a, b, cols_owned, ts, after):
    s, k1 = a.shape
    k2 = b.shape[1]
    ns = s // ts
    if cols_owned:
        nblk, bk1, bk2 = N_DEV, k1, k2 // N_DEV
        a_spec = pl.BlockSpec((ts, bk1), lambda j, t: (t, 0))
        b_spec = pl.BlockSpec((ts, bk2), lambda j, t: (t, j))
    else:
        bk1, bk2 = min(WGRAD_ROWS, k1), k2
        nblk = k1 // bk1
        a_spec = pl.BlockSpec((ts, bk1), lambda j, t: (t, j))
        b_spec = pl.BlockSpec((ts, bk2), lambda j, t: (t, 0))

    def body(a_ref, b_ref, after_ref, o_ref, acc_ref):
        t = pl.program_id(1)

        @pl.when(t == 0)
        def _():
            acc_ref[...] = jnp.zeros(acc_ref.shape, F32)

        acc_ref[...] += _dot_tn(a_ref[...], b_ref[...])

        @pl.when(t == ns - 1)
        def _():
            o_ref[...] = acc_ref[...].astype(BF16)

    out = pl.pallas_call(
        body,
        name="wgrad",
        grid=(nblk, ns),
        in_specs=[a_spec, b_spec, ANY_SPEC],
        out_specs=pl.BlockSpec((None, bk1, bk2), lambda j, t: (j, 0, 0)),
        out_shape=pltpu.HBM((nblk, bk1, bk2), BF16),
        scratch_shapes=[pltpu.VMEM((bk1, bk2), F32)],
        compiler_params=_cparams(("parallel", "arbitrary"), 4 * _nbytes((bk1, bk2), F32) + 4 * _nbytes((ts, bk1 + bk2), BF16)),
    )(pltpu.with_memory_space_constraint(a, pltpu.HBM), pltpu.with_memory_space_constraint(b, pltpu.HBM), after)
    return out if cols_owned else out.reshape(N_DEV, k1 // N_DEV, k2)


def _adam_update(w, g, m, v):
    m = ADAM_B1 * m + (1.0 - ADAM_B1) * g
    v = ADAM_B2 * v + (1.0 - ADAM_B2) * (g * g)
    m_hat = m / (1.0 - ADAM_B1 ** ADAM_STEP)
    v_hat = v / (1.0 - ADAM_B2 ** ADAM_STEP)
    delta = -ADAM_LR * (m_hat / (jnp.sqrt(v_hat) + ADAM_EPS) + ADAM_WD * w)
    return delta, m, v


def _sum_adamw(recv, w, m, v, tr, after):
    nl, ra, cb = w.shape
    assert nl == len(recv) == 2

    def body(r0_ref, r1_ref, w_ref, m_ref, v_ref, after_ref, g_ref, d_ref, nm_ref, nv_ref):
        def total(r_ref):
            g = r_ref[0].astype(F32)
            for j in range(1, N_DEV):
                g = g + r_ref[j].astype(F32)
            return g

        g = jnp.where(pl.program_id(0) == 0, total(r0_ref), total(r1_ref))
        g_ref[...] = g
        d_ref[...], nm_ref[...], nv_ref[...] = _adam_update(w_ref[...], g, m_ref[...], v_ref[...])

    blk = pl.BlockSpec((None, tr, cb), lambda l, i: (l, i, 0))
    return pl.pallas_call(
        body,
        name="sum_adamw",
        grid=(nl, ra // tr),
        in_specs=[pl.BlockSpec((N_DEV, tr, cb), lambda l, i: (0, i * (1 - l), 0)),
                  pl.BlockSpec((N_DEV, tr, cb), lambda l, i: (0, i * l, 0)), blk, blk, blk, ANY_SPEC],
        out_specs=(blk,) * 4,
        out_shape=(jax.ShapeDtypeStruct((nl, ra, cb), F32),) * 4,
        compiler_params=_cparams(("arbitrary", "arbitrary"), 6 * _nbytes((N_DEV, tr, cb), BF16) + 32 * _nbytes((tr, cb), F32)),
    )(recv[0], recv[1], w, m, v, after)


def _adamw(w, g, m, v):
    def body(w_ref, g_ref, m_ref, v_ref, d_ref, nm_ref, nv_ref):
        d_ref[...], nm_ref[...], nv_ref[...] = _adam_update(w_ref[...], g_ref[...], m_ref[...], v_ref[...])

    return pl.pallas_call(
        body,
        name="adamw",
        in_specs=[VMEM_SPEC] * 4,
        out_specs=(VMEM_SPEC,) * 3,
        out_shape=(jax.ShapeDtypeStruct(w.shape, F32),) * 3,
        compiler_params=_cparams(vmem=10 * _nbytes(w.shape, F32)),
    )(w, g, m, v)


def _adamw_tiled(w, g, m, v, tr):
    nl, ra, cb = w.shape

    def body(w_ref, g_ref, m_ref, v_ref, d_ref, nm_ref, nv_ref):
        d_ref[...], nm_ref[...], nv_ref[...] = _adam_update(w_ref[...], g_ref[...], m_ref[...], v_ref[...])

    blk = pl.BlockSpec((None, tr, cb), lambda l, i: (l, i, 0))
    return pl.pallas_call(
        body,
        name="adamw_tiled",
        grid=(nl, ra // tr),
        in_specs=[blk] * 4,
        out_specs=(blk,) * 3,
        out_shape=(jax.ShapeDtypeStruct((nl, ra, cb), F32),) * 3,
        compiler_params=_cparams(("parallel", "parallel")),
    )(*_hbm(w, g, m, v))


def _token_tile(s):
    return min(TOKEN_TILE, max(SUBLANES * 2, s // 4))


def kernel(x, c, w_mod, b_mod, g_pre_mix, g_post_mix, w_in, conv_a_w, conv_a_b, w_a_out, conv_b_w, conv_b_b, w_gate_r, b_gate_r, w_gate_i, b_gate_i, lru_lambda, w_b_out, w_o, g_pre_mlp, g_post_mlp, w_mlp_up, w_mlp_down, loss_target, m_w_mod, m_b_mod, m_g_pre_mix, m_g_post_mix, m_w_in, m_conv_a_w, m_conv_a_b, m_w_a_out, m_conv_b_w, m_conv_b_b, m_w_gate_r, m_b_gate_r, m_w_gate_i, m_b_gate_i, m_lru_lambda, m_w_b_out, m_w_o, m_g_pre_mlp, m_g_post_mlp, m_w_mlp_up, m_w_mlp_down, v_w_mod, v_b_mod, v_g_pre_mix, v_g_post_mix, v_w_in, v_conv_a_w, v_conv_a_b, v_w_a_out, v_conv_b_w, v_conv_b_b, v_w_gate_r, v_b_gate_r, v_w_gate_i, v_b_gate_i, v_lru_lambda, v_w_b_out, v_w_o, v_g_pre_mlp, v_g_post_mlp, v_w_mlp_up, v_w_mlp_down):
    nl = w_mod.shape[0]
    s, d = x.shape[1], x.shape[2]
    nh, bw = w_gate_r.shape[1], w_gate_r.shape[2]
    cwid = conv_a_w.shape[2]
    tm = _token_tile(s)
    tmx = min(2 * tm, s)
    ts = s
    _, me = _my_position()
    xs = x.reshape(s, d)
    target = loss_target.reshape(s, d)

    vec_names = (g_pre_mix, g_post_mix, conv_a_b, conv_b_b, b_gate_r, b_gate_i, lru_lambda, g_pre_mlp, g_post_mlp)
    rows = jnp.concatenate([jnp.stack(vec_names, axis=1), jnp.zeros((nl, N_ROWS - len(vec_names), d), F32)], axis=1)
    cw = jnp.concatenate([conv_a_w, conv_b_w, jnp.zeros((nl, CW_ROWS - 7, cwid), F32)], axis=1)

    large = {"w_in": w_in, "w_a_out": w_a_out, "w_b_out": w_b_out, "w_o": w_o, "w_mlp_up": w_mlp_up, "w_mlp_down": w_mlp_down}
    groups = (("in", ("w_in",)), ("mix", ("w_a_out", "w_b_out", "w_o")), ("mlp", ("w_mlp_up", "w_mlp_down")))
    mod, cact, cwf = _prep_small(c, w_mod, b_mod, cw)
    mod = mod.reshape(nl, N_MOD, d)
    gathers = {}
    tok = mod
    for l in range(nl):
        for gname, members in groups:
            gathers[l, gname] = _gather2_start([large[n][l].astype(BF16) for n in members], tok, f"gather_start_{gname}{l}")
            tok = gathers[l, gname][4]
    wr = w_gate_r.astype(BF16)
    wi = w_gate_i.astype(BF16)

    forwarded = {}

    def forward(l, gname, after):
        forwarded[l, gname] = _gather2_forward(gathers[l, gname], after, f"gather_forward_{gname}{l}")
        return forwarded[l, gname][4]

    def gathered(l, gname, after):
        return _gather2_wait(forwarded[l, gname], after, f"gather_wait_{gname}{l}")

    saved = []
    weights = []
    xin = xs
    for l in range(nl):
        if l == 0:
            tok = forward(0, "in", tok)
        (win_f,) = gathered(l, "in", tok if l == 0 else xin)
        if l > 0:
            tok = forward(l, "mix", win_f)
        proj, h = _in_proj_fwd(xin, mod, rows, win_f, l, tmx, tok)
        if l == 0:
            tok = forward(0, "mix", proj)
        wa_f, wb_f, wo_f = (w.reshape(d, d) for w in gathered(l, "mix", proj))
        if l > 0:
            tok = forward(l, "mlp", wo_f)
        x1, hs, yap, ybp, y = _mixer_core_fwd(proj, xin, mod, rows, cwf, wr, wi, wa_f, wb_f, wo_f, l, tm, tok)
        if l == 0:
            forward(0, "mlp", x1)
        wup_f, wdn_f = gathered(l, "mlp", x1)
        wdn_f = wdn_f.reshape(-1, d)
        if l + 1 < nl:
            tok = forward(l + 1, "in", wdn_f)
        x2, ru, y2, h2, *last = _mlp_fwd(x1, mod, rows, wup_f, wdn_f, l, tmx, tok, target if l + 1 == nl else None)
        saved.append((xin, proj, h, x1, hs, yap, ybp, y, ru, y2, h2))
        weights.append((win_f, wa_f, wb_f, wo_f, wup_f, wdn_f))
        xin = x2
    dx, loss_part = last[0], last[1][0, 0]

    scatters = {}
    small = [None] * nl
    gate_parts = [None] * nl

    def scatter(l, gname, parts):
        scatters[l, gname] = _exchange_start(parts, False, f"scatter_start_{gname}{l}")
        return scatters[l, gname][4]

    for l in reversed(range(nl)):
        xin, proj, h, x1, hs, yap, ybp, y, ru, y2, h2 = saved[l]
        win_f, wa_f, wb_f, wo_f, wup_f, wdn_f = weights[l]
        dx1, dy2, dup, act, sm_mlp = _mlp_bwd(dx, x1, y2, ru, mod, rows, wup_f, wdn_f, l, tm)
        g_up = _wgrad(h2, dup, True, ts, dx1)
        g_dn = _wgrad(act, dy2, False, ts, g_up)
        tok = scatter(l, "mlp", [g_up, g_dn])
        dproj, dy, m, dyap, dybp, ya, yb, sm_mix, dwg = _mixer_core_bwd(
            dx1, y, yap, ybp, hs, proj, mod, rows, cwf, wr, wi, wa_f, wb_f, wo_f, l, tm // 2, tok)
        gate_parts[l] = _exchange_start([dwg.astype(BF16)], True, f"gates_start{l}")
        g_a = _wgrad(ya, dyap, False, ts, gate_parts[l][4])
        g_b = _wgrad(yb, dybp, False, ts, g_a)
        g_o = _wgrad(m, dy, False, ts, g_b)
        tok = scatter(l, "mix", [g_a, g_b, g_o])
        tok = scatter(l, "in", [_wgrad(h, dproj, True, ts, tok)])
        dx, sm_in = _in_proj_bwd(dproj, xin, dx1, mod, rows, win_f, l, tm, tok)
        small[l] = jnp.concatenate([sm_mlp, sm_mix, sm_in], axis=0)
    grad_x = dx.reshape(x.shape)

    recv = {}
    big = {}
    moments = {"w_in": (w_in, m_w_in, v_w_in), "w_mlp_up": (w_mlp_up, m_w_mlp_up, v_w_mlp_up), "w_a_out": (w_a_out, m_w_a_out, v_w_a_out),
               "w_b_out": (w_b_out, m_w_b_out, v_w_b_out), "w_o": (w_o, m_w_o, v_w_o), "w_mlp_down": (w_mlp_down, m_w_mlp_down, v_w_mlp_down)}

    def collect(l, gname, after):
        for n, zone in zip(dict(groups)[gname], _exchange_wait(scatters[l, gname], after, False, f"scatter_wait_{gname}{l}")):
            recv[n, l] = zone

    def update(name, after):
        w, m_, v_ = moments[name]
        big[name] = _sum_adamw([recv[name, l] for l in range(nl)], w, m_, v_, min(ADAM_ROWS, w.shape[1]), after)
        return big[name][1]

    for l, gname in ((1, "mlp"), (1, "in"), (1, "mix"), (0, "mlp")):
        collect(l, gname, dx)
    done = update("w_mlp_down", update("w_mlp_up", dx))
    collect(0, "mix", done)
    for name in ("w_a_out", "w_b_out", "w_o"):
        done = update(name, done)
    collect(0, "in", done)
    done = update("w_in", done)

    lrows = jnp.concatenate(small, axis=0)
    lrows = lrows.at[G_LOSS_ROW, 0].set(loss_part)

    def lrow(a, l, r):
        return a[l * G_LAYER_ROWS + r]

    dm = jnp.stack([jnp.concatenate([lrow(lrows, l, G_MLP_ROWS + G_MIX_ROWS + G_IN_SH), lrow(lrows, l, G_MLP_ROWS + G_MIX_ROWS + G_IN_SC),
                                     lrow(lrows, l, G_MLP_ROWS + G_MIX_GT), lrow(lrows, l, G_MLP_SH), lrow(lrows, l, G_MLP_SC),
                                     lrow(lrows, l, G_MLP_GT)]) for l in range(nl)])
    dm8 = jnp.concatenate([dm[:, None, :], jnp.zeros((nl, SUBLANES - 1, N_MOD * d), F32)], axis=1)
    srows, g_w_mod = _reduce_small(lrows, dm8, cact, done)
    loss = srows[G_LOSS_ROW, 0]
    sgates = _sum_gathered([_exchange_wait(gate_parts[l], srows, True, f"gates_wait{l}")[0] for l in range(nl)])

    def srow(l, r):
        return lrow(srows, l, r)

    def per_layer(r):
        return jnp.stack([srow(l, r) for l in range(nl)])

    mix0 = G_MLP_ROWS
    in0 = G_MLP_ROWS + G_MIX_ROWS
    g_b_mod = jnp.stack([jnp.concatenate([srow(l, in0 + G_IN_SH), srow(l, in0 + G_IN_SC), srow(l, mix0 + G_MIX_GT),
                                          srow(l, G_MLP_SH), srow(l, G_MLP_SC), srow(l, G_MLP_GT)]) for l in range(nl)])
    conv_a_full = jnp.stack([jnp.stack([srow(l, mix0 + G_MIX_CAW + k) for k in range(3)]) for l in range(nl)])
    conv_b_full = jnp.stack([jnp.stack([srow(l, mix0 + G_MIX_CBW + k) for k in range(4)]) for l in range(nl)])
    grads = {
        "b_mod": g_b_mod,
        "g_pre_mix": per_layer(in0 + G_IN_GPRE),
        "g_post_mix": per_layer(mix0 + G_MIX_GPOST),
        "conv_a_w": lax.dynamic_slice_in_dim(conv_a_full, me * cwid, cwid, axis=2),
        "conv_a_b": per_layer(mix0 + G_MIX_CAB),
        "conv_b_w": lax.dynamic_slice_in_dim(conv_b_full, me * cwid, cwid, axis=2),
        "conv_b_b": per_layer(mix0 + G_MIX_CBB),
        "w_gate_r": sgates[:, 0],
        "b_gate_r": per_layer(mix0 + G_MIX_BR),
        "w_gate_i": sgates[:, 1],
        "b_gate_i": per_layer(mix0 + G_MIX_BI),
        "lru_lambda": per_layer(mix0 + G_MIX_LAM),
        "g_pre_mlp": per_layer(G_MLP_GPRE),
        "g_post_mlp": per_layer(G_MLP_GPOST),
    }
    params = {
        "b_mod": (b_mod, m_b_mod, v_b_mod), "g_pre_mix": (g_pre_mix, m_g_pre_mix, v_g_pre_mix), "g_post_mix": (g_post_mix, m_g_post_mix, v_g_post_mix),
        "conv_a_w": (conv_a_w, m_conv_a_w, v_conv_a_w), "conv_a_b": (conv_a_b, m_conv_a_b, v_conv_a_b),
        "conv_b_w": (conv_b_w, m_conv_b_w, v_conv_b_w), "conv_b_b": (conv_b_b, m_conv_b_b, v_conv_b_b),
        "w_gate_r": (w_gate_r, m_w_gate_r, v_w_gate_r), "b_gate_r": (b_gate_r, m_b_gate_r, v_b_gate_r),
        "w_gate_i": (w_gate_i, m_w_gate_i, v_w_gate_i), "b_gate_i": (b_gate_i, m_b_gate_i, v_b_gate_i),
        "lru_lambda": (lru_lambda, m_lru_lambda, v_lru_lambda), "g_pre_mlp": (g_pre_mlp, m_g_pre_mlp, v_g_pre_mlp),
        "g_post_mlp": (g_post_mlp, m_g_post_mlp, v_g_post_mlp),
    }
    out = {}
    for name, g in grads.items():
        w, m_, v_ = params[name]
        flat = (-1, w.shape[-1])
        dl, nm, nv = _adamw(w.reshape(flat), g.reshape(flat), m_.reshape(flat), v_.reshape(flat))
        out[name] = (g.reshape(w.shape), dl.reshape(w.shape), nm.reshape(w.shape), nv.reshape(w.shape))
    out["w_mod"] = (g_w_mod,) + tuple(_adamw_tiled(w_mod, g_w_mod, m_w_mod, v_w_mod, min(ADAM_ROWS, d)))
    out.update(big)

    order = ("w_mod", "b_mod", "g_pre_mix", "g_post_mix", "w_in", "conv_a_w", "conv_a_b", "w_a_out", "conv_b_w", "conv_b_b", "w_gate_r", "b_gate_r",
             "w_gate_i", "b_gate_i", "lru_lambda", "w_b_out", "w_o", "g_pre_mlp", "g_post_mlp", "w_mlp_up", "w_mlp_down")
    return (loss, grad_x) + tuple(out[n][0] for n in order) + tuple(out[n][1] for n in order) + tuple(out[n][2] for n in order) + tuple(out[n][3] for n in order)
```

```python
import jax
import jax.numpy as jnp
from jax import lax
from jax.experimental import pallas as pl
from jax.experimental.pallas import tpu as pltpu

F32, BF16 = jnp.float32, jnp.bfloat16
EPS = 1e-6
LRU_C = 8.0
N_DEV = 8
N_MOD = 6
SUBLANES = 8
VMEM_BUDGET = 56 * 1024 * 1024
WGRAD_ROWS = 512
TOKEN_TILE = 256
ADAM_ROWS = 256
ADAM_LR, ADAM_B1, ADAM_B2, ADAM_EPS, ADAM_WD, ADAM_STEP = 0.001, 0.9, 0.999, 1e-08, 0.01, 10
MESH = pl.DeviceIdType.MESH
VMEM_SPEC = pl.BlockSpec(memory_space=pltpu.VMEM)
ANY_SPEC = pl.BlockSpec(memory_space=pl.ANY)
HBM_SPEC = pl.BlockSpec(memory_space=pltpu.HBM)
SEM_SPEC = pl.BlockSpec(memory_space=pltpu.SEMAPHORE)
SIDE_EFFECT = pltpu.SideEffectType.DATAFLOW_SIDE_EFFECTING

R_G_PRE_MIX, R_G_POST_MIX, R_CONV_A_B, R_CONV_B_B, R_B_GATE_R, R_B_GATE_I, R_LAMBDA, R_G_PRE_MLP, R_G_POST_MLP = range(9)
N_ROWS = 16
M_SH_M, M_SC_M, M_GT_M, M_SH_F, M_SC_F, M_GT_F = range(6)
CW_A, CW_B, CW_ROWS = 0, 3, 8
G_MLP_GT, G_MLP_GPOST, G_MLP_SC, G_MLP_SH, G_MLP_GPRE, G_LOSS_ROW, G_MLP_ROWS = 0, 1, 2, 3, 4, 7, 8
(G_MIX_GT, G_MIX_GPOST, G_MIX_CAB, G_MIX_CAW, G_MIX_CBB, G_MIX_CBW, G_MIX_BR, G_MIX_BI, G_MIX_LAM) = 0, 1, 2, 3, 6, 7, 11, 12, 13
G_MIX_ROWS = 16
G_IN_SC, G_IN_SH, G_IN_GPRE, G_IN_ROWS = 0, 1, 2, 8
G_LAYER_ROWS = G_MLP_ROWS + G_MIX_ROWS + G_IN_ROWS


def _cparams(dims=None, vmem=None):
    kw = {}
    if dims is not None:
        kw["dimension_semantics"] = dims
    if vmem is not None:
        kw["vmem_limit_bytes"] = int(min(max(vmem, 16 * 1024 * 1024), VMEM_BUDGET))
    return pltpu.CompilerParams(**kw)


def _nbytes(shape, dtype):
    n = 1
    for s in shape:
        n *= s
    return n * jnp.dtype(dtype).itemsize


def _hbm(*arrays):
    return tuple(pltpu.with_memory_space_constraint(a, pltpu.HBM) for a in arrays)


def _resident(block, index_map):
    return pl.BlockSpec(block, index_map, pipeline_mode=pl.Buffered(1))


def _my_position():
    x, y, c = lax.axis_index("x"), lax.axis_index("y"), lax.axis_index("c")
    return (x, y, c), 4 * x + 2 * y + c


def _peer(pos, k):
    x, y, c = pos
    px = 1 - x if k & 4 else x
    py = 1 - y if k & 2 else y
    pc = 1 - c if k & 1 else c
    return (px, py, pc), 4 * px + 2 * py + pc


def _remote(src, dst, ssem, rsem, peer):
    return pltpu.make_async_remote_copy(src_ref=src, dst_ref=dst, send_sem=ssem, recv_sem=rsem, device_id=peer, device_id_type=MESH)


def _dot(a, b):
    return jnp.dot(a, b, preferred_element_type=F32)


def _dot_nt(a, b):
    return lax.dot_general(a, b, (((1,), (1,)), ((), ())), preferred_element_type=F32)


def _dot_tn(a, b):
    return lax.dot_general(a, b, (((0,), (0,)), ((), ())), preferred_element_type=F32)


def _colsum(v):
    return jnp.sum(v, axis=0, keepdims=True)


def _sigmoid(v):
    return jax.nn.sigmoid(v)


GELU_K, GELU_C = 0.7978845608028654, 0.044715


def _gelu(v):
    s = 1.0 / (1.0 + jnp.exp(v * (-2.0 * GELU_K - (2.0 * GELU_K * GELU_C) * (v * v))))
    return v * s, s


def _gelu_grad(v, s):
    return s * (1.0 + (v * (1.0 - s)) * (2.0 * GELU_K + (6.0 * GELU_K * GELU_C) * (v * v)))


def _neg_expm1_twice(v):
    t = jnp.tanh(v)
    return (-2.0 * t) / (1.0 - t)


def _softplus_neg(lam):
    z = -lam
    u = jnp.exp(-jnp.abs(z))
    w = 1.0 + u
    l1p = jnp.where(w == 1.0, u, jnp.log(w) * u / (w - 1.0))
    return jnp.maximum(z, 0.0) + l1p


def _rms(v):
    return lax.rsqrt(jnp.mean(v * v, axis=-1, keepdims=True) + EPS)


def _prenorm_bwd(xv, dh, g, sc):
    r = _rms(xv)
    xn = xv * r
    n = xn * g
    dsc = _colsum(dh * n)
    dsh = _colsum(dh)
    dn = dh * (1.0 + sc)
    dg = _colsum(dn * xn)
    dxn = dn * g
    dx = r * (dxn - xn * jnp.mean(dxn * xn, axis=-1, keepdims=True))
    return dx, dsc, dsh, dg


def _postnorm_bwd(yv, dout, g, gt):
    r = _rms(yv)
    yn = yv * r
    dgt = _colsum(dout * (yn * g))
    dn = dout * gt
    dg = _colsum(dn * yn)
    dyn = dn * g
    dy = r * (dyn - yn * jnp.mean(dyn * yn, axis=-1, keepdims=True))
    return dy, dgt, dg


def _gates(xc, wr_ref, wi_ref, b_r, b_i, sp, nh, bw):
    xcb = xc.astype(BF16)
    zr = jnp.concatenate([_dot(xcb[:, h * bw:(h + 1) * bw], wr_ref[h]) for h in range(nh)], axis=1) + b_r
    zi = jnp.concatenate([_dot(xcb[:, h * bw:(h + 1) * bw], wi_ref[h]) for h in range(nh)], axis=1) + b_i
    r = _sigmoid(zr)
    ig = _sigmoid(zi)
    la = (-LRU_C * r) * sp
    a = jnp.exp(la)
    mult = jnp.sqrt(_neg_expm1_twice(la))
    return xcb, r, ig, a, mult


def _shift_rows(cur, edge, k, up):
    t, dd = cur.shape
    blocks = cur.reshape(t // SUBLANES, SUBLANES, dd)
    row = lax.broadcasted_iota(jnp.int32, (1, SUBLANES, dd), 1)
    if up:
        r = pltpu.roll(blocks, SUBLANES - k, 1)
        nxt = jnp.concatenate([r[1:], pltpu.roll(edge, SUBLANES - k, 0)[None]], axis=0)
        out = jnp.where(row >= SUBLANES - k, nxt, r)
    else:
        r = pltpu.roll(blocks, k, 1)
        prv = jnp.concatenate([pltpu.roll(edge, k, 0)[None], r[:-1]], axis=0)
        out = jnp.where(row < k, prv, r)
    return out.reshape(t, dd)


def _scan_block(a8, b8, reverse):
    row = lax.broadcasted_iota(jnp.int32, a8.shape, 0)
    for s in (1, 2, 4):
        if reverse:
            keep = row < SUBLANES - s
            a_sh = pltpu.roll(a8, SUBLANES - s, 0)
            b_sh = pltpu.roll(b8, SUBLANES - s, 0)
        else:
            keep = row >= s
            a_sh = pltpu.roll(a8, s, 0)
            b_sh = pltpu.roll(b8, s, 0)
        b8 = b8 + a8 * jnp.where(keep, b_sh, 0.0)
        a8 = a8 * jnp.where(keep, a_sh, 1.0)
    return a8, b8


def _prep_small(c, w_mod, b_mod, cw):
    d = c.shape[1]
    cm = w_mod.shape[2]
    cwid = cw.shape[2]
    nl = w_mod.shape[0]

    def body(c_ref, wm_ref, bm_ref, cw_ref, mod_ref, cact_ref, cwf_ref, cbuf, pbuf, rbuf, ssem, rsem, lsem):
        pos, me = _my_position()
        me8 = pl.multiple_of(me * SUBLANES, SUBLANES)
        cbuf[pl.ds(me8, SUBLANES), :] = jnp.broadcast_to(c_ref[...], (SUBLANES, d))
        own_cw = pltpu.make_async_copy(cw_ref, cwf_ref.at[:, :, pl.ds(me * cwid, cwid)], lsem.at[0])
        own_cw.start()
        first = []
        for k in range(1, N_DEV):
            peer, _ = _peer(pos, k)
            rows = cbuf.at[pl.ds(me8, SUBLANES), :]
            first.append(_remote(rows, rows, ssem.at[0, k - 1], rsem.at[0, k - 1], peer))
            first.append(_remote(cw_ref, cwf_ref.at[:, :, pl.ds(me * cwid, cwid)], ssem.at[1, k - 1], rsem.at[1, k - 1], peer))
        for cp in first:
            cp.start()
        for k in range(1, N_DEV):
            peer, pj = _peer(pos, k)
            pj8 = pl.multiple_of(pj * SUBLANES, SUBLANES)
            rows = cbuf.at[pl.ds(pj8, SUBLANES), :]
            _remote(rows, rows, ssem.at[0, k - 1], rsem.at[0, k - 1], peer).wait_recv()
        cv = cbuf[...]
        cact = cv * _sigmoid(cv)
        cact_ref[...] = cact
        cb = cact.astype(BF16)
        for l in range(nl):
            pbuf[l] = _dot(cb, wm_ref[l].astype(BF16))
        own_p = pltpu.make_async_copy(pbuf.at[:, pl.ds(me8, SUBLANES), :], rbuf.at[me], lsem.at[1])
        own_p.start()
        second = []
        for k in range(1, N_DEV):
            peer, pj = _peer(pos, k)
            pj8 = pl.multiple_of(pj * SUBLANES, SUBLANES)
            second.append(_remote(pbuf.at[:, pl.ds(pj8, SUBLANES), :], rbuf.at[me], ssem.at[2, k - 1], rsem.at[2, k - 1], peer))
        for cp in second:
            cp.start()
        for k in range(1, N_DEV):
            peer, pj = _peer(pos, k)
            _remote(pbuf.at[:, pl.ds(0, SUBLANES), :], rbuf.at[pj], ssem.at[2, k - 1], rsem.at[2, k - 1], peer).wait_recv()
            _remote(cw_ref, cwf_ref.at[:, :, pl.ds(pj * cwid, cwid)], ssem.at[1, k - 1], rsem.at[1, k - 1], peer).wait_recv()
        own_p.wait()
        own_cw.wait()
        for l in range(nl):
            for j in range(N_DEV):
                mod_ref[l:l + 1, j * cm:(j + 1) * cm] = rbuf[j, l, 0:1, :] + bm_ref[l:l + 1, j * cm:(j + 1) * cm]
        for cp in first + second:
            cp.wait_send()

    return pl.pallas_call(
        body,
        name="prep_small",
        out_shape=(
            jax.ShapeDtypeStruct((nl, N_MOD * d), F32),
            jax.ShapeDtypeStruct((N_DEV * SUBLANES, d), F32),
            jax.ShapeDtypeStruct((nl, CW_ROWS, d), F32),
        ),
        in_specs=[VMEM_SPEC] * 4,
        out_specs=(VMEM_SPEC,) * 3,
        scratch_shapes=[
            pltpu.VMEM((N_DEV * SUBLANES, d), F32),
            pltpu.VMEM((nl, N_DEV * SUBLANES, cm), F32),
            pltpu.VMEM((N_DEV, nl, SUBLANES, cm), F32),
            pltpu.SemaphoreType.DMA((3, N_DEV - 1)),
            pltpu.SemaphoreType.DMA((3, N_DEV - 1)),
            pltpu.SemaphoreType.DMA((2,)),
        ],
        compiler_params=_cparams(vmem=3 * _nbytes(w_mod.shape, F32)),
    )(c, w_mod, b_mod, cw)


def _exchange_start(parts, gather, name):
    n = len(parts)
    lands = [lax.empty(((N_DEV,) + tuple(p.shape)) if gather else tuple(p.shape), p.dtype) for p in parts]

    def body(*refs):
        ins, lnd = refs[:n], refs[n:2 * n]
        ssem, rsem, token = refs[2 * n], refs[2 * n + 1], refs[-1]
        pos, me = _my_position()
        for k in range(1, N_DEV):
            peer, pj = _peer(pos, k)
            for t in range(n):
                src = ins[t] if gather else ins[t].at[pj]
                q = t * (N_DEV - 1) + k - 1
                _remote(src, lnd[t].at[me], ssem.at[q], rsem.at[q], peer).start()
        token[...] = jnp.zeros(token.shape, F32)

    out = pl.pallas_call(
        body,
        name=name,
        out_shape=(pltpu.SemaphoreType.DMA((n * (N_DEV - 1),)), pltpu.SemaphoreType.DMA((n * (N_DEV - 1),)))
        + tuple(pltpu.HBM(p.shape, p.dtype) for p in parts) + tuple(pltpu.HBM(p.shape, p.dtype) for p in lands)
        + (jax.ShapeDtypeStruct((SUBLANES, 128), F32),),
        in_specs=[HBM_SPEC] * (2 * n),
        out_specs=(SEM_SPEC, SEM_SPEC) + (HBM_SPEC,) * (2 * n) + (VMEM_SPEC,),
        input_output_aliases={i: 2 + i for i in range(2 * n)},
        compiler_params=pltpu.CompilerParams(has_side_effects=SIDE_EFFECT),
    )(*[pltpu.with_memory_space_constraint(p, pltpu.HBM) for p in list(parts) + lands])
    return out[0], out[1], list(out[2:2 + n]), list(out[2 + n:2 + 2 * n]), out[-1]


def _exchange_wait(started, after, gather, name):
    ssem, rsem, parts, lands, _ = started
    n = len(parts)

    def body(*refs):
        ins, lnd = refs[:n], refs[n:2 * n]
        ssem_ref, rsem_ref = refs[2 * n], refs[2 * n + 1]
        stage, lsem = refs[-1 - n:-1], refs[-1]
        pos, me = _my_position()
        load = []
        for t in range(n):
            src = ins[t] if gather else ins[t].at[me]
            load.append(pltpu.make_async_copy(src, stage[t], lsem.at[t]))
            load[-1].start()
        store = []
        for t in range(n):
            load[t].wait()
            store.append(pltpu.make_async_copy(stage[t], lnd[t].at[me], lsem.at[t]))
            store[-1].start()
        for k in range(1, N_DEV):
            peer, pj = _peer(pos, k)
            for t in range(n):
                src = ins[t] if gather else ins[t].at[pj]
                q = t * (N_DEV - 1) + k - 1
                _remote(src, lnd[t].at[me], ssem_ref.at[q], rsem_ref.at[q], peer).wait_send()
                _remote(src, lnd[t].at[pj], ssem_ref.at[q], rsem_ref.at[q], peer).wait_recv()
        for cp in store:
            cp.wait()

    out = pl.pallas_call(
        body,
        name=name,
        out_shape=tuple(pltpu.HBM(p.shape, p.dtype) for p in parts) + tuple(pltpu.HBM(p.shape, p.dtype) for p in lands),
        in_specs=[HBM_SPEC] * (2 * n) + [SEM_SPEC, SEM_SPEC, ANY_SPEC],
        out_specs=(HBM_SPEC,) * (2 * n),
        input_output_aliases={i: i for i in range(2 * n)},
        scratch_shapes=[pltpu.VMEM(tuple(z.shape[1:]), z.dtype) for z in lands] + [pltpu.SemaphoreType.DMA((n,))],
        compiler_params=pltpu.CompilerParams(has_side_effects=SIDE_EFFECT),
    )(*parts, *lands, ssem, rsem, after)
    return list(out[n:])


CHIP_PEERS = 3


def _chip_peer(pos, k, core):
    x, y, _ = pos
    px = 1 - x if k & 2 else x
    py = 1 - y if k & 1 else y
    return (px, py, core), 4 * px + 2 * py + core


def _gather2_start(parts, after, name):
    n = len(parts)
    lands = [lax.empty((N_DEV,) + tuple(p.shape), p.dtype) for p in parts]
    per = CHIP_PEERS + 1

    def body(*refs):
        ins, lnd = refs[:n], refs[n:2 * n]
        ssem, rsem, token = refs[2 * n + 1], refs[2 * n + 2], refs[-1]
        pos, me = _my_position()
        sibling = (pos[0], pos[1], 1 - pos[2])
        for t in range(n):
            _remote(ins[t], lnd[t].at[me], ssem.at[per * t], rsem.at[per * t], sibling).start()
        for k in range(1, per):
            peer, _ = _chip_peer(pos, k, pos[2])
            for t in range(n):
                _remote(ins[t], lnd[t].at[me], ssem.at[per * t + k], rsem.at[per * t + k], peer).start()
        token[...] = jnp.zeros(token.shape, F32)

    out = pl.pallas_call(
        body,
        name=name,
        out_shape=(pltpu.SemaphoreType.DMA((n * per,)), pltpu.SemaphoreType.DMA((n * per,)))
        + tuple(pltpu.HBM(p.shape, p.dtype) for p in parts) + tuple(pltpu.HBM(p.shape, p.dtype) for p in lands)
        + (jax.ShapeDtypeStruct((SUBLANES, 128), F32),),
        in_specs=[HBM_SPEC] * (2 * n) + [ANY_SPEC],
        out_specs=(SEM_SPEC, SEM_SPEC) + (HBM_SPEC,) * (2 * n) + (VMEM_SPEC,),
        input_output_aliases={i: 2 + i for i in range(2 * n)},
        compiler_params=pltpu.CompilerParams(has_side_effects=SIDE_EFFECT),
    )(*[pltpu.with_memory_space_constraint(p, pltpu.HBM) for p in list(parts) + lands], after)
    return out[0], out[1], list(out[2:2 + n]), list(out[2 + n:2 + 2 * n]), out[-1]


def _gather2_forward(started, after, name):
    ssem, rsem, parts, lands, _ = started
    n = len(lands)
    per = CHIP_PEERS + 1

    def body(*refs):
        lnd = refs[:n]
        rsem_a, fsend, frecv, token = refs[n], refs[2 * n + 2], refs[2 * n + 3], refs[2 * n + 4]
        pos, _ = _my_position()
        sibling = (pos[0], pos[1], 1 - pos[2])
        for k in range(1, per):
            peer, pk = _chip_peer(pos, k, pos[2])
            for t in range(n):
                block = lnd[t].at[pk]
                _remote(block, block, rsem_a.at[per * t + k], rsem_a.at[per * t + k], peer).wait_recv()
                q = CHIP_PEERS * t + k - 1
                _remote(block, block, fsend.at[q], frecv.at[q], sibling).start()
        token[...] = jnp.zeros(token.shape, F32)

    out = pl.pallas_call(
        body,
        name=name,
        out_shape=tuple(pltpu.HBM(p.shape, p.dtype) for p in lands)
        + (pltpu.SemaphoreType.DMA((n * CHIP_PEERS,)), pltpu.SemaphoreType.DMA((n * CHIP_PEERS,)),
           jax.ShapeDtypeStruct((SUBLANES, 128), F32)),
        in_specs=[HBM_SPEC] * n + [SEM_SPEC, ANY_SPEC],
        out_specs=(HBM_SPEC,) * n + (SEM_SPEC, SEM_SPEC, VMEM_SPEC),
        input_output_aliases={i: i for i in range(n)},
        compiler_params=pltpu.CompilerParams(has_side_effects=SIDE_EFFECT),
    )(*lands, rsem, after)
    return ssem, rsem, parts, list(out[:n]), out[n + 2], out[n], out[n + 1]


def _gather2_wait(forwarded, after, name):
    ssem, rsem, parts, lands, _, fsend, frecv = forwarded
    n = len(parts)
    per = CHIP_PEERS + 1

    def body(*refs):
        ins, lnd = refs[:n], refs[n:2 * n]
        ssem_a, rsem_a, fs, fr = refs[2 * n:2 * n + 4]
        stage, lsem = refs[-1 - n:-1], refs[-1]
        pos, me = _my_position()
        sibling = (pos[0], pos[1], 1 - pos[2])
        sib = 4 * pos[0] + 2 * pos[1] + 1 - pos[2]
        load = []
        for t in range(n):
            load.append(pltpu.make_async_copy(ins[t], stage[t], lsem.at[t]))
            load[-1].start()
        store = []
        for t in range(n):
            load[t].wait()
            store.append(pltpu.make_async_copy(stage[t], lnd[t].at[me], lsem.at[t]))
            store[-1].start()
        for t in range(n):
            _remote(ins[t], lnd[t].at[me], ssem_a.at[per * t], rsem_a.at[per * t], sibling).wait_send()
            _remote(ins[t], lnd[t].at[sib], ssem_a.at[per * t], rsem_a.at[per * t], sibling).wait_recv()
        for k in range(1, per):
            peer, pk = _chip_peer(pos, k, pos[2])
            _, qk = _chip_peer(pos, k, 1 - pos[2])
            for t in range(n):
                q = CHIP_PEERS * t + k - 1
                _remote(ins[t], lnd[t].at[me], ssem_a.at[per * t + k], rsem_a.at[per * t + k], peer).wait_send()
                _remote(lnd[t].at[pk], lnd[t].at[pk], fs.at[q], fr.at[q], sibling).wait_send()
                _remote(lnd[t].at[qk], lnd[t].at[qk], fs.at[q], fr.at[q], sibling).wait_recv()
        for cp in store:
            cp.wait()

    out = pl.pallas_call(
        body,
        name=name,
        out_shape=tuple(pltpu.HBM(p.shape, p.dtype) for p in parts) + tuple(pltpu.HBM(p.shape, p.dtype) for p in lands),
        in_specs=[HBM_SPEC] * (2 * n) + [SEM_SPEC] * 4 + [ANY_SPEC],
        out_specs=(HBM_SPEC,) * (2 * n),
        input_output_aliases={i: i for i in range(2 * n)},
        scratch_shapes=[pltpu.VMEM(tuple(z.shape[1:]), z.dtype) for z in lands] + [pltpu.SemaphoreType.DMA((n,))],
        compiler_params=pltpu.CompilerParams(has_side_effects=SIDE_EFFECT),
    )(*parts, *lands, ssem, rsem, fsend, frecv, after)
    return list(out[n:])


def _reduce_small(rows, dm8, cact, after):
    r, d = rows.shape
    nl = dm8.shape[0]
    cm = dm8.shape[2] // N_DEV
    lw = d // N_DEV

    def body(rows_ref, dm_ref, cact_ref, after_ref, orow_ref, owm_ref, gr, mine, dmr, ssem, rsem, lsem):
        pos, me = _my_position()
        me8 = pl.multiple_of(me * SUBLANES, SUBLANES)

        def lanes(j):
            return pl.ds(j * lw, lw)

        own = [pltpu.make_async_copy(dm_ref.at[:, :, pl.ds(me * cm, cm)], dmr.at[:, pl.ds(me8, SUBLANES), :], lsem.at[0]),
               pltpu.make_async_copy(rows_ref.at[:, lanes(me)], gr.at[me], lsem.at[1])]
        first = []
        for k in range(1, N_DEV):
            peer, pj = _peer(pos, k)
            first.append(_remote(rows_ref.at[:, lanes(pj)], gr.at[me], ssem.at[0, k - 1], rsem.at[0, k - 1], peer))
            first.append(_remote(dm_ref.at[:, :, pl.ds(pj * cm, cm)], dmr.at[:, pl.ds(me8, SUBLANES), :],
                                 ssem.at[1, k - 1], rsem.at[1, k - 1], peer))
        for cp in own + first:
            cp.start()
        for k in range(1, N_DEV):
            peer, pj = _peer(pos, k)
            pj8 = pl.multiple_of(pj * SUBLANES, SUBLANES)
            _remote(rows_ref.at[:, lanes(0)], gr.at[pj], ssem.at[0, k - 1], rsem.at[0, k - 1], peer).wait_recv()
            _remote(dm_ref.at[:, :, pl.ds(0, cm)], dmr.at[:, pl.ds(pj8, SUBLANES), :], ssem.at[1, k - 1], rsem.at[1, k - 1], peer).wait_recv()
        for cp in own:
            cp.wait()
        acc = gr[0]
        for j in range(1, N_DEV):
            acc = acc + gr[j]
        mine[...] = acc
        own_sum = pltpu.make_async_copy(mine, orow_ref.at[:, lanes(me)], lsem.at[1])
        own_sum.start()
        second = []
        for k in range(1, N_DEV):
            peer, _ = _peer(pos, k)
            second.append(_remote(mine, orow_ref.at[:, lanes(me)], ssem.at[2, k - 1], rsem.at[2, k - 1], peer))
            second[-1].start()
        cb = cact_ref[...].astype(BF16)
        for l in range(nl):
            owm_ref[l] = _dot_tn(cb, dmr[l].astype(BF16))
        for k in range(1, N_DEV):
            peer, pj = _peer(pos, k)
            _remote(mine, orow_ref.at[:, lanes(pj)], ssem.at[2, k - 1], rsem.at[2, k - 1], peer).wait_recv()
        own_sum.wait()
        for cp in first + second:
            cp.wait_send()

    return pl.pallas_call(
        body,
        name="reduce_small",
        out_shape=(jax.ShapeDtypeStruct((r, d), F32), jax.ShapeDtypeStruct((nl, d, cm), F32)),
        in_specs=[VMEM_SPEC] * 3 + [ANY_SPEC],
        out_specs=(VMEM_SPEC,) * 2,
        scratch_shapes=[
            pltpu.VMEM((N_DEV, r, lw), F32),
            pltpu.VMEM((r, lw), F32),
            pltpu.VMEM((nl, N_DEV * SUBLANES, cm), F32),
            pltpu.SemaphoreType.DMA((3, N_DEV - 1)),
            pltpu.SemaphoreType.DMA((3, N_DEV - 1)),
            pltpu.SemaphoreType.DMA((2,)),
        ],
        compiler_params=_cparams(vmem=4 * _nbytes((r, d), F32) + 6 * _nbytes((nl, d, cm), F32)),
    )(rows, dm8, cact, after)


def _sum_gathered(zones):
    nl = len(zones)

    def body(*refs):
        for l in range(nl):
            acc = refs[l][0].astype(F32)
            for j in range(1, N_DEV):
                acc = acc + refs[l][j].astype(F32)
            refs[nl][l] = acc

    return pl.pallas_call(
        body,
        name="sum_gathered",
        out_shape=jax.ShapeDtypeStruct((nl,) + tuple(zones[0].shape[1:]), F32),
        in_specs=[VMEM_SPEC] * nl,
        out_specs=VMEM_SPEC,
        compiler_params=_cparams(vmem=8 * nl * _nbytes(zones[0].shape, BF16)),
    )(*zones)


def _in_proj_fwd(x, mod, rows, win_f, l, tm, after):
    s, d = x.shape
    nb, _, ci = win_f.shape

    def body(x_ref, mod_ref, rows_ref, w_ref, after_ref, proj_ref, h_ref):
        xv = x_ref[...]
        g = rows_ref[R_G_PRE_MIX:R_G_PRE_MIX + 1, :]
        h = (xv * _rms(xv) * g) * (1.0 + mod_ref[M_SC_M:M_SC_M + 1, :]) + mod_ref[M_SH_M:M_SH_M + 1, :]
        hb = h.astype(BF16)
        h_ref[...] = hb
        for j in range(nb):
            proj_ref[:, j * ci:(j + 1) * ci] = _dot(hb, w_ref[j])

    return pl.pallas_call(
        body,
        name="in_proj_fwd",
        grid=(s // tm,),
        in_specs=[
            pl.BlockSpec((tm, d), lambda i: (i, 0)),
            _resident((None, N_MOD, d), lambda i: (l, 0, 0)),
            _resident((None, N_ROWS, d), lambda i: (l, 0, 0)),
            _resident((nb, d, ci), lambda i: (0, 0, 0)),
            ANY_SPEC,
        ],
        out_specs=(pl.BlockSpec((tm, nb * ci), lambda i: (i, 0)), pl.BlockSpec((tm, d), lambda i: (i, 0))),
        out_shape=(jax.ShapeDtypeStruct((s, nb * ci), F32), jax.ShapeDtypeStruct((s, d), BF16)),
        compiler_params=_cparams(("parallel",), _nbytes((nb, d, ci), BF16) + 3 * _nbytes((tm, nb * ci), F32) + 8 * _nbytes((tm, d), F32)),
    )(*_hbm(x), mod, rows, win_f, after)


def _mixer_core_fwd(proj, x, mod, rows, cwf, wr, wi, wa_f, wb_f, wo_f, l, tm, after):
    s, d = x.shape
    nh, bw, _ = wr.shape[1:]

    def body(proj_ref, x_ref, mod_ref, rows_ref, cw_ref, wr_ref, wi_ref, wa_ref, wb_ref, wo_ref, after_ref,
             x1_ref, hs_ref, yap_ref, ybp_ref, y_ref, cvbuf, xbbuf, a_s, b_s, hprev):
        i = pl.program_id(0)

        @pl.when(i == 0)
        def _():
            cvbuf[...] = jnp.zeros((SUBLANES, d), F32)
            xbbuf[...] = jnp.zeros((SUBLANES, d), F32)
            hprev[...] = jnp.zeros((SUBLANES, d), F32)

        def row(r):
            return rows_ref[r:r + 1, :]

        def tap(r):
            return cw_ref[r:r + 1, :]

        ba = proj_ref[:, 0:d]
        cv = proj_ref[:, d:2 * d] * proj_ref[:, 2 * d:3 * d]
        cvt = cvbuf[...]
        conv3 = ((row(R_CONV_A_B) + _shift_rows(cv, cvt, 2, False) * tap(CW_A)) + _shift_rows(cv, cvt, 1, False) * tap(CW_A + 1)) + cv * tap(CW_A + 2)
        ya = ba * conv3
        cvbuf[...] = cv[tm - SUBLANES:, :]
        xb = proj_ref[:, 3 * d:4 * d]
        xbt = xbbuf[...]
        xc = (((row(R_CONV_B_B) + _shift_rows(xb, xbt, 3, False) * tap(CW_B)) + _shift_rows(xb, xbt, 2, False) * tap(CW_B + 1))
              + _shift_rows(xb, xbt, 1, False) * tap(CW_B + 2)) + xb * tap(CW_B + 3)
        xbbuf[...] = xb[tm - SUBLANES:, :]
        sp = _softplus_neg(row(R_LAMBDA))
        _, _, ig, a, mult = _gates(xc, wr_ref, wi_ref, row(R_B_GATE_R), row(R_B_GATE_I), sp, nh, bw)
        a_s[...] = a
        b_s[...] = mult * (ig * xc)

        def blk(j, hp):
            o = pl.multiple_of(j * SUBLANES, SUBLANES)
            a8, b8 = _scan_block(a_s[pl.ds(o, SUBLANES), :], b_s[pl.ds(o, SUBLANES), :], reverse=False)
            h8 = b8 + a8 * hp
            hs_ref[pl.ds(o, SUBLANES), :] = h8
            return jnp.broadcast_to(h8[SUBLANES - 1:SUBLANES, :], (SUBLANES, d))

        hprev[...] = lax.fori_loop(0, tm // SUBLANES, blk, hprev[...])
        gel, _ = _gelu(proj_ref[:, 4 * d:5 * d])
        yb = hs_ref[...] * gel
        yap = _dot(ya.astype(BF16), wa_ref[...])
        ybp = _dot(yb.astype(BF16), wb_ref[...])
        yap_ref[...] = yap
        ybp_ref[...] = ybp
        m = _sigmoid(proj_ref[:, 5 * d:6 * d]) * yap + _sigmoid(proj_ref[:, 6 * d:7 * d]) * ybp
        y = _dot(m.astype(BF16), wo_ref[...])
        y_ref[...] = y
        x1_ref[...] = x_ref[...] + mod_ref[M_GT_M:M_GT_M + 1, :] * ((y * _rms(y)) * row(R_G_POST_MIX))

    tile = pl.BlockSpec((tm, d), lambda i: (i, 0))
    return pl.pallas_call(
        body,
        name="mixer_core_fwd",
        grid=(s // tm,),
        in_specs=[
            pl.BlockSpec((tm, 7 * d), lambda i: (i, 0)),
            tile,
            _resident((None, N_MOD, d), lambda i: (l, 0, 0)),
            _resident((None, N_ROWS, d), lambda i: (l, 0, 0)),
            _resident((None, CW_ROWS, d), lambda i: (l, 0, 0)),
            _resident((None, nh, bw, bw), lambda i: (l, 0, 0, 0)),
            _resident((None, nh, bw, bw), lambda i: (l, 0, 0, 0)),
            _resident((d, d), lambda i: (0, 0)),
            _resident((d, d), lambda i: (0, 0)),
            _resident((d, d), lambda i: (0, 0)),
            ANY_SPEC,
        ],
        out_specs=(tile,) * 5,
        out_shape=(jax.ShapeDtypeStruct((s, d), F32),) * 5,
        scratch_shapes=[
            pltpu.VMEM((SUBLANES, d), F32),
            pltpu.VMEM((SUBLANES, d), F32),
            pltpu.VMEM((tm, d), F32),
            pltpu.VMEM((tm, d), F32),
            pltpu.VMEM((SUBLANES, d), F32),
        ],
        compiler_params=_cparams(("arbitrary",), 3 * _nbytes((d, d), BF16) + 2 * _nbytes((tm, 7 * d), F32) + 40 * _nbytes((tm, d), F32)),
    )(*_hbm(proj, x), mod, rows, cwf, wr, wi, wa_f, wb_f, wo_f, after)


def _mlp_fwd(x1, mod, rows, wup_f, wdn_f, l, tm, after, target=None):
    s, d = x1.shape
    nb, _, cu = wup_f.shape
    dff = nb * cu
    with_loss = target is not None

    def body(x1_ref, mod_ref, rows_ref, wu_ref, wd_ref, after_ref, *refs):
        x2_ref, ru_ref, y2_ref, h2_ref = refs[-6:-2] if with_loss else refs
        xv = x1_ref[...]
        g = rows_ref[R_G_PRE_MLP:R_G_PRE_MLP + 1, :]
        h2 = ((xv * _rms(xv) * g) * (1.0 + mod_ref[M_SC_F:M_SC_F + 1, :]) + mod_ref[M_SH_F:M_SH_F + 1, :]).astype(BF16)
        h2_ref[...] = h2
        ru = jnp.concatenate([jnp.maximum(_dot(h2, wu_ref[j]), 0.0) for j in range(nb)], axis=1)
        ru_ref[...] = ru.astype(BF16)
        y2 = _dot((ru * ru).astype(BF16), wd_ref[...])
        y2_ref[...] = y2
        x2 = xv + mod_ref[M_GT_F:M_GT_F + 1, :] * ((y2 * _rms(y2)) * rows_ref[R_G_POST_MLP:R_G_POST_MLP + 1, :])
        x2_ref[...] = x2
        if with_loss:
            t_ref, dy_ref, loss_ref = refs[0], refs[-2], refs[-1]

            @pl.when(pl.program_id(0) == 0)
            def _():
                loss_ref[...] = jnp.zeros(loss_ref.shape, F32)

            e = x2 - t_ref[...]
            dy_ref[...] = e * (1.0 / d)
            loss_ref[...] += 0.5 * jnp.sum(jnp.mean(e * e, axis=-1, keepdims=True), axis=0, keepdims=True)

    tile = pl.BlockSpec((tm, d), lambda i: (i, 0))
    wide = pl.BlockSpec((tm, dff), lambda i: (i, 0))
    out_specs = (tile, wide, tile, tile)
    out_shape = (jax.ShapeDtypeStruct((s, d), F32), jax.ShapeDtypeStruct((s, dff), BF16),
                 jax.ShapeDtypeStruct((s, d), F32), jax.ShapeDtypeStruct((s, d), BF16))
    if with_loss:
        out_specs += (tile, pl.BlockSpec((SUBLANES, 128), lambda i: (0, 0)))
        out_shape += (jax.ShapeDtypeStruct((s, d), F32), jax.ShapeDtypeStruct((SUBLANES, 128), F32))
    return pl.pallas_call(
        body,
        name="mlp_fwd",
        grid=(s // tm,),
        in_specs=[
            tile,
            _resident((None, N_MOD, d), lambda i: (l, 0, 0)),
            _resident((None, N_ROWS, d), lambda i: (l, 0, 0)),
            _resident((nb, d, cu), lambda i: (0, 0, 0)),
            _resident((dff, d), lambda i: (0, 0)),
            ANY_SPEC,
        ] + ([tile] if with_loss else []),
        out_specs=out_specs,
        out_shape=out_shape,
        compiler_params=_cparams(("arbitrary",), 2 * _nbytes((dff, d), BF16) + 5 * _nbytes((tm, dff), F32) + 16 * _nbytes((tm, d), F32)),
    )(*_hbm(x1), mod, rows, wup_f, wdn_f, after, *(_hbm(target) if with_loss else ()))


def _mlp_bwd(dx2, x1, y2, ru, mod, rows, wup_f, wdn_f, l, tm):
    s, d = x1.shape
    nb, _, cu = wup_f.shape
    dff = nb * cu

    def body(dx2_ref, x1_ref, y2_ref, ru_ref, mod_ref, rows_ref, wu_ref, wd_ref, dx1_ref, dy2_ref, dup_ref, act_ref, sm_ref):
        @pl.when(pl.program_id(0) == 0)
        def _():
            sm_ref[...] = jnp.zeros(sm_ref.shape, F32)

        dout = dx2_ref[...]
        dy2, dgt, dgpost = _postnorm_bwd(y2_ref[...], dout, rows_ref[R_G_POST_MLP:R_G_POST_MLP + 1, :], mod_ref[M_GT_F:M_GT_F + 1, :])
        dy2b = dy2.astype(BF16)
        dy2_ref[...] = dy2b
        ruv = ru_ref[...].astype(F32)
        act_ref[...] = (ruv * ruv).astype(BF16)
        dup = (_dot_nt(dy2b, wd_ref[...]) * (2.0 * ruv)).astype(BF16)
        dup_ref[...] = dup
        dh2 = _dot_nt(dup[:, 0:cu], wu_ref[0])
        for j in range(1, nb):
            dh2 = dh2 + _dot_nt(dup[:, j * cu:(j + 1) * cu], wu_ref[j])
        dxn, dsc, dsh, dgpre = _prenorm_bwd(x1_ref[...], dh2, rows_ref[R_G_PRE_MLP:R_G_PRE_MLP + 1, :], mod_ref[M_SC_F:M_SC_F + 1, :])
        dx1_ref[...] = dout + dxn
        for r, v in ((G_MLP_GT, dgt), (G_MLP_GPOST, dgpost), (G_MLP_SC, dsc), (G_MLP_SH, dsh), (G_MLP_GPRE, dgpre)):
            sm_ref[r:r + 1, :] += v

    tile = pl.BlockSpec((tm, d), lambda i: (i, 0))
    wide = pl.BlockSpec((tm, dff), lambda i: (i, 0))
    return pl.pallas_call(
        body,
        name="mlp_bwd",
        grid=(s // tm,),
        in_specs=[
            tile, tile, tile, wide,
            _resident((None, N_MOD, d), lambda i: (l, 0, 0)),
            _resident((None, N_ROWS, d), lambda i: (l, 0, 0)),
            _resident((nb, d, cu), lambda i: (0, 0, 0)),
            _resident((dff, d), lambda i: (0, 0)),
        ],
        out_specs=(tile, tile, wide, wide, pl.BlockSpec((G_MLP_ROWS, d), lambda i: (0, 0))),
        out_shape=(jax.ShapeDtypeStruct((s, d), F32), jax.ShapeDtypeStruct((s, d), BF16), jax.ShapeDtypeStruct((s, dff), BF16),
                   jax.ShapeDtypeStruct((s, dff), BF16), jax.ShapeDtypeStruct((G_MLP_ROWS, d), F32)),
        compiler_params=_cparams(("arbitrary",), 2 * _nbytes((dff, d), BF16) + 6 * _nbytes((tm, dff), F32) + 16 * _nbytes((tm, d), F32)),
    )(*_hbm(dx2, x1, y2, ru), mod, rows, wup_f, wdn_f)


def _mixer_core_bwd(dx1, y, yap, ybp, hs, proj, mod, rows, cwf, wr, wi, wa_f, wb_f, wo_f, l, tm, after):
    s, d = dx1.shape
    nh, bw, _ = wr.shape[1:]
    nt = s // tm
    per = tm // SUBLANES

    def body(dx1_ref, y_ref, yap_ref, ybp_ref, hs_ref, hsh_ref, proj_ref, projh_ref, mod_ref, rows_ref, cw_ref,
             wr_ref, wi_ref, wa_ref, wb_ref, wo_ref, after_ref,
             dproj_ref, dy_ref, m_ref, dyap_ref, dybp_ref, ya_ref, yb_ref, sm_ref, dwg_ref,
             abuf, dcbuf, dxbuf, al_s, dh_s, lam_s, lnext):
        i = pl.program_id(0)
        first_tile = i == nt - 1

        @pl.when(i == 0)
        def _():
            sm_ref[...] = jnp.zeros(sm_ref.shape, F32)
            dwg_ref[...] = jnp.zeros(dwg_ref.shape, F32)
            zero = jnp.zeros((SUBLANES, d), F32)
            abuf[...] = zero
            dcbuf[...] = zero
            dxbuf[...] = zero
            lnext[...] = zero

        def row(r):
            return rows_ref[r:r + 1, :]

        def tap(r):
            return cw_ref[r:r + 1, :]

        def acc(r, v):
            sm_ref[r:r + 1, :] += v

        keep_halo = jnp.where(first_tile, 0.0, 1.0)
        dy, dgt, dgpost = _postnorm_bwd(y_ref[...], dx1_ref[...], row(R_G_POST_MIX), mod_ref[M_GT_M:M_GT_M + 1, :])
        acc(G_MIX_GT, dgt)
        acc(G_MIX_GPOST, dgpost)
        dyb16 = dy.astype(BF16)
        dy_ref[...] = dyb16
        dm = _dot_nt(dyb16, wo_ref[...])
        sa = _sigmoid(proj_ref[:, 5 * d:6 * d])
        sb = _sigmoid(proj_ref[:, 6 * d:7 * d])
        yap = yap_ref[...]
        ybp = ybp_ref[...]
        m_ref[...] = (sa * yap + sb * ybp).astype(BF16)
        dyap_f = dm * sa
        dybp_f = dm * sb
        dyap = dyap_f.astype(BF16)
        dybp = dybp_f.astype(BF16)
        dyap_ref[...] = dyap
        dybp_ref[...] = dybp
        dproj_ref[:, 5 * d:6 * d] = (dyap_f * yap * (1.0 - sa)).astype(BF16)
        dproj_ref[:, 6 * d:7 * d] = (dybp_f * ybp * (1.0 - sb)).astype(BF16)
        dya = _dot_nt(dyap, wa_ref[...])
        dyb = _dot_nt(dybp, wb_ref[...])
        ba = proj_ref[:, 0:d]
        ca = proj_ref[:, d:2 * d]
        va = proj_ref[:, 2 * d:3 * d]
        cv = ca * va
        cvh = keep_halo * (projh_ref[:, d:2 * d] * projh_ref[:, 2 * d:3 * d])
        cvm2 = _shift_rows(cv, cvh, 2, False)
        cvm1 = _shift_rows(cv, cvh, 1, False)
        conv3 = ((row(R_CONV_A_B) + cvm2 * tap(CW_A)) + cvm1 * tap(CW_A + 1)) + cv * tap(CW_A + 2)
        ya_ref[...] = (ba * conv3).astype(BF16)
        dproj_ref[:, 0:d] = (dya * conv3).astype(BF16)
        dc3 = dya * ba
        acc(G_MIX_CAB, _colsum(dc3))
        acc(G_MIX_CAW, _colsum(dc3 * cvm2))
        acc(G_MIX_CAW + 1, _colsum(dc3 * cvm1))
        acc(G_MIX_CAW + 2, _colsum(dc3 * cv))
        dct = dcbuf[...]
        dcv = (dc3 * tap(CW_A + 2) + _shift_rows(dc3, dct, 1, True) * tap(CW_A + 1)) + _shift_rows(dc3, dct, 2, True) * tap(CW_A)
        dcbuf[...] = dc3[:SUBLANES, :]
        dproj_ref[:, d:2 * d] = (dcv * va).astype(BF16)
        dproj_ref[:, 2 * d:3 * d] = (dcv * ca).astype(BF16)
        xb = proj_ref[:, 3 * d:4 * d]
        gb = proj_ref[:, 4 * d:5 * d]
        xbh = keep_halo * projh_ref[:, 3 * d:4 * d]
        xm3 = _shift_rows(xb, xbh, 3, False)
        xm2 = _shift_rows(xb, xbh, 2, False)
        xm1 = _shift_rows(xb, xbh, 1, False)
        xc = (((row(R_CONV_B_B) + xm3 * tap(CW_B)) + xm2 * tap(CW_B + 1)) + xm1 * tap(CW_B + 2)) + xb * tap(CW_B + 3)
        lam = row(R_LAMBDA)
        sp = _softplus_neg(lam)
        xcb, r, ig, a, mult = _gates(xc, wr_ref, wi_ref, row(R_B_GATE_R), row(R_B_GATE_I), sp, nh, bw)
        gel, gsig = _gelu(gb)
        hs = hs_ref[...]
        yb_ref[...] = (hs * gel).astype(BF16)
        dproj_ref[:, 4 * d:5 * d] = (dyb * hs * _gelu_grad(gb, gsig)).astype(BF16)
        al_s[...] = _shift_rows(a, abuf[...], 1, True)
        abuf[...] = a[:SUBLANES, :]
        dh_s[...] = dyb * gel

        def blk(j, ln):
            o = pl.multiple_of((per - 1 - j) * SUBLANES, SUBLANES)
            a8, b8 = _scan_block(al_s[pl.ds(o, SUBLANES), :], dh_s[pl.ds(o, SUBLANES), :], reverse=True)
            l8 = b8 + a8 * ln
            lam_s[pl.ds(o, SUBLANES), :] = l8
            return jnp.broadcast_to(l8[0:1, :], (SUBLANES, d))

        lnext[...] = lax.fori_loop(0, per, blk, lnext[...])
        dbb = lam_s[...]
        da = dbb * _shift_rows(hs, keep_halo * hsh_ref[...], 1, False)
        dbx = dbb * xc
        dmult = dbx * ig
        dig = dbx * mult
        dxc = (dbb * mult) * ig
        dla = a * (da - (dmult * a) / mult)
        dlar = dla * r
        acc(G_MIX_LAM, _colsum(dlar) * (LRU_C * _sigmoid(-lam)))
        dzr = (dlar * (1.0 - r)) * (-LRU_C * sp)
        dzi = dig * ig * (1.0 - ig)
        acc(G_MIX_BR, _colsum(dzr))
        acc(G_MIX_BI, _colsum(dzi))
        dzrb = dzr.astype(BF16)
        dzib = dzi.astype(BF16)
        back = []
        for h in range(nh):
            sl = slice(h * bw, (h + 1) * bw)
            back.append(_dot_nt(dzrb[:, sl], wr_ref[h]) + _dot_nt(dzib[:, sl], wi_ref[h]))
            dwg_ref[0, h] += _dot_tn(xcb[:, sl], dzrb[:, sl])
            dwg_ref[1, h] += _dot_tn(xcb[:, sl], dzib[:, sl])
        dxc = dxc + jnp.concatenate(back, axis=1)
        acc(G_MIX_CBB, _colsum(dxc))
        acc(G_MIX_CBW, _colsum(dxc * xm3))
        acc(G_MIX_CBW + 1, _colsum(dxc * xm2))
        acc(G_MIX_CBW + 2, _colsum(dxc * xm1))
        acc(G_MIX_CBW + 3, _colsum(dxc * xb))
        dxt = dxbuf[...]
        dxb = (((dxc * tap(CW_B + 3) + _shift_rows(dxc, dxt, 1, True) * tap(CW_B + 2)) + _shift_rows(dxc, dxt, 2, True) * tap(CW_B + 1))
               + _shift_rows(dxc, dxt, 3, True) * tap(CW_B))
        dxbuf[...] = dxc[:SUBLANES, :]
        dproj_ref[:, 3 * d:4 * d] = dxb.astype(BF16)

    def rev(i):
        return (nt - 1 - i, 0)

    def halo(i):
        return (jnp.maximum((nt - 1 - i) * per - 1, 0), 0)

    tile = pl.BlockSpec((tm, d), rev)
    return pl.pallas_call(
        body,
        name="mixer_core_bwd",
        grid=(nt,),
        in_specs=[
            tile, tile, tile, tile, tile,
            pl.BlockSpec((SUBLANES, d), halo),
            pl.BlockSpec((tm, 7 * d), rev),
            pl.BlockSpec((SUBLANES, 7 * d), halo),
            _resident((None, N_MOD, d), lambda i: (l, 0, 0)),
            _resident((None, N_ROWS, d), lambda i: (l, 0, 0)),
            _resident((None, CW_ROWS, d), lambda i: (l, 0, 0)),
            _resident((None, nh, bw, bw), lambda i: (l, 0, 0, 0)),
            _resident((None, nh, bw, bw), lambda i: (l, 0, 0, 0)),
            _resident((d, d), lambda i: (0, 0)),
            _resident((d, d), lambda i: (0, 0)),
            _resident((d, d), lambda i: (0, 0)),
            ANY_SPEC,
        ],
        out_specs=(pl.BlockSpec((tm, 7 * d), rev),) + (tile,) * 6 + (
            pl.BlockSpec((G_MIX_ROWS, d), lambda i: (0, 0)), pl.BlockSpec((2, nh, bw, bw), lambda i: (0, 0, 0, 0))),
        out_shape=(jax.ShapeDtypeStruct((s, 7 * d), BF16),) + (jax.ShapeDtypeStruct((s, d), BF16),) * 6 + (
            jax.ShapeDtypeStruct((G_MIX_ROWS, d), F32), jax.ShapeDtypeStruct((2, nh, bw, bw), F32)),
        scratch_shapes=[pltpu.VMEM((SUBLANES, d), F32)] * 3 + [pltpu.VMEM((tm, d), F32)] * 3 + [pltpu.VMEM((SUBLANES, d), F32)],
        compiler_params=_cparams(("arbitrary",), 3 * _nbytes((d, d), BF16) + 3 * _nbytes((tm, 7 * d), F32) + 64 * _nbytes((tm, d), F32)),
    )(*_hbm(dx1, y, yap, ybp, hs, hs, proj, proj), mod, rows, cwf, wr, wi, wa_f, wb_f, wo_f, after)


def _in_proj_bwd(dproj, x, dx1, mod, rows, win_f, l, tm, after):
    s, d = x.shape
    nb, _, ci = win_f.shape

    def body(dp_ref, x_ref, dx1_ref, mod_ref, rows_ref, w_ref, after_ref, dx_ref, sm_ref):
        @pl.when(pl.program_id(0) == 0)
        def _():
            sm_ref[...] = jnp.zeros(sm_ref.shape, F32)

        dh = _dot_nt(dp_ref[:, 0:ci], w_ref[0])
        for j in range(1, nb):
            dh = dh + _dot_nt(dp_ref[:, j * ci:(j + 1) * ci], w_ref[j])
        dxn, dsc, dsh, dg = _prenorm_bwd(x_ref[...], dh, rows_ref[R_G_PRE_MIX:R_G_PRE_MIX + 1, :], mod_ref[M_SC_M:M_SC_M + 1, :])
        dx_ref[...] = dx1_ref[...] + dxn
        for r, v in ((G_IN_SC, dsc), (G_IN_SH, dsh), (G_IN_GPRE, dg)):
            sm_ref[r:r + 1, :] += v

    tile = pl.BlockSpec((tm, d), lambda i: (i, 0))
    return pl.pallas_call(
        body,
        name="in_proj_bwd",
        grid=(s // tm,),
        in_specs=[
            pl.BlockSpec((tm, nb * ci), lambda i: (i, 0)), tile, tile,
            _resident((None, N_MOD, d), lambda i: (l, 0, 0)),
            _resident((None, N_ROWS, d), lambda i: (l, 0, 0)),
            _resident((nb, d, ci), lambda i: (0, 0, 0)),
            ANY_SPEC,
        ],
        out_specs=(tile, pl.BlockSpec((G_IN_ROWS, d), lambda i: (0, 0))),
        out_shape=(jax.ShapeDtypeStruct((s, d), F32), jax.ShapeDtypeStruct((G_IN_ROWS, d), F32)),
        compiler_params=_cparams(("arbitrary",), _nbytes((nb, d, ci), BF16) + 2 * _nbytes((tm, nb * ci), BF16) + 16 * _nbytes((tm, d), F32)),
    )(*_hbm(dproj, x, dx1), mod, rows, win_f, after)


def _wgrad(a, b, cols_owned, ts, after):
    s, k1 = a.shape
    k2 = b.shape[1]
    ns = s // ts
    if cols_owned:
        nblk, bk1, bk2 = N_DEV, k1, k2 // N_DEV
        a_spec = pl.BlockSpec((ts, bk1), lambda j, t: (t, 0))
        b_spec = pl.BlockSpec((ts, bk2), lambda j, t: (t, j))
    else:
        bk1, bk2 = min(WGRAD_ROWS, k1), k2
        nblk = k1 // bk1
        a_spec = pl.BlockSpec((ts, bk1), lambda j, t: (t, j))
        b_spec = pl.BlockSpec((ts, bk2), lambda j, t: (t, 0))

    def body(a_ref, b_ref, after_ref, o_ref, acc_ref):
        t = pl.program_id(1)

        @pl.when(t == 0)
        def _():
            acc_ref[...] = jnp.zeros(acc_ref.shape, F32)

        acc_ref[...] += _dot_tn(a_ref[...], b_ref[...])

        @pl.when(t == ns - 1)
        def _():
            o_ref[...] = acc_ref[...].astype(BF16)

    out = pl.pallas_call(
        body,
        name="wgrad",
        grid=(nblk, ns),
        in_specs=[a_spec, b_spec, ANY_SPEC],
        out_specs=pl.BlockSpec((None, bk1, bk2), lambda j, t: (j, 0, 0)),
        out_shape=pltpu.HBM((nblk, bk1, bk2), BF16),
        scratch_shapes=[pltpu.VMEM((bk1, bk2), F32)],
        compiler_params=_cparams(("parallel", "arbitrary"), 4 * _nbytes((bk1, bk2), F32) + 4 * _nbytes((ts, bk1 + bk2), BF16)),
    )(pltpu.with_memory_space_constraint(a, pltpu.HBM), pltpu.with_memory_space_constraint(b, pltpu.HBM), after)
    return out if cols_owned else out.reshape(N_DEV, k1 // N_DEV, k2)


def _adam_update(w, g, m, v):
    m = ADAM_B1 * m + (1.0 - ADAM_B1) * g
    v = ADAM_B2 * v + (1.0 - ADAM_B2) * (g * g)
    m_hat = m / (1.0 - ADAM_B1 ** ADAM_STEP)
    v_hat = v / (1.0 - ADAM_B2 ** ADAM_STEP)
    delta = -ADAM_LR * (m_hat / (jnp.sqrt(v_hat) + ADAM_EPS) + ADAM_WD * w)
    return delta, m, v


def _sum_adamw(recv, w, m, v, tr, after):
    nl, ra, cb = w.shape
    assert nl == len(recv) == 2

    def body(r0_ref, r1_ref, w_ref, m_ref, v_ref, after_ref, g_ref, d_ref, nm_ref, nv_ref):
        def total(r_ref):
            g = r_ref[0].astype(F32)
            for j in range(1, N_DEV):
                g = g + r_ref[j].astype(F32)
            return g

        g = jnp.where(pl.program_id(0) == 0, total(r0_ref), total(r1_ref))
        g_ref[...] = g
        d_ref[...], nm_ref[...], nv_ref[...] = _adam_update(w_ref[...], g, m_ref[...], v_ref[...])

    blk = pl.BlockSpec((None, tr, cb), lambda l, i: (l, i, 0))
    return pl.pallas_call(
        body,
        name="sum_adamw",
        grid=(nl, ra // tr),
        in_specs=[pl.BlockSpec((N_DEV, tr, cb), lambda l, i: (0, i * (1 - l), 0)),
                  pl.BlockSpec((N_DEV, tr, cb), lambda l, i: (0, i * l, 0)), blk, blk, blk, ANY_SPEC],
        out_specs=(blk,) * 4,
        out_shape=(jax.ShapeDtypeStruct((nl, ra, cb), F32),) * 4,
        compiler_params=_cparams(("arbitrary", "arbitrary"), 6 * _nbytes((N_DEV, tr, cb), BF16) + 32 * _nbytes((tr, cb), F32)),
    )(recv[0], recv[1], w, m, v, after)


def _adamw(w, g, m, v):
    def body(w_ref, g_ref, m_ref, v_ref, d_ref, nm_ref, nv_ref):
        d_ref[...], nm_ref[...], nv_ref[...] = _adam_update(w_ref[...], g_ref[...], m_ref[...], v_ref[...])

    return pl.pallas_call(
        body,
        name="adamw",
        in_specs=[VMEM_SPEC] * 4,
        out_specs=(VMEM_SPEC,) * 3,
        out_shape=(jax.ShapeDtypeStruct(w.shape, F32),) * 3,
        compiler_params=_cparams(vmem=10 * _nbytes(w.shape, F32)),
    )(w, g, m, v)


def _adamw_tiled(w, g, m, v, tr):
    nl, ra, cb = w.shape

    def body(w_ref, g_ref, m_ref, v_ref, d_ref, nm_ref, nv_ref):
        d_ref[...], nm_ref[...], nv_ref[...] = _adam_update(w_ref[...], g_ref[...], m_ref[...], v_ref[...])

    blk = pl.BlockSpec((None, tr, cb), lambda l, i: (l, i, 0))
    return pl.pallas_call(
        body,
        name="adamw_tiled",
        grid=(nl, ra // tr),
        in_specs=[blk] * 4,
        out_specs=(blk,) * 3,
        out_shape=(jax.ShapeDtypeStruct((nl, ra, cb), F32),) * 3,
        compiler_params=_cparams(("parallel", "parallel")),
    )(*_hbm(w, g, m, v))


def _token_tile(s):
    return min(TOKEN_TILE, max(SUBLANES * 2, s // 4))


def kernel(x, c, w_mod, b_mod, g_pre_mix, g_post_mix, w_in, conv_a_w, conv_a_b, w_a_out, conv_b_w, conv_b_b, w_gate_r, b_gate_r, w_gate_i, b_gate_i, lru_lambda, w_b_out, w_o, g_pre_mlp, g_post_mlp, w_mlp_up, w_mlp_down, loss_target, m_w_mod, m_b_mod, m_g_pre_mix, m_g_post_mix, m_w_in, m_conv_a_w, m_conv_a_b, m_w_a_out, m_conv_b_w, m_conv_b_b, m_w_gate_r, m_b_gate_r, m_w_gate_i, m_b_gate_i, m_lru_lambda, m_w_b_out, m_w_o, m_g_pre_mlp, m_g_post_mlp, m_w_mlp_up, m_w_mlp_down, v_w_mod, v_b_mod, v_g_pre_mix, v_g_post_mix, v_w_in, v_conv_a_w, v_conv_a_b, v_w_a_out, v_conv_b_w, v_conv_b_b, v_w_gate_r, v_b_gate_r, v_w_gate_i, v_b_gate_i, v_lru_lambda, v_w_b_out, v_w_o, v_g_pre_mlp, v_g_post_mlp, v_w_mlp_up, v_w_mlp_down):
    nl = w_mod.shape[0]
    s, d = x.shape[1], x.shape[2]
    nh, bw = w_gate_r.shape[1], w_gate_r.shape[2]
    cwid = conv_a_w.shape[2]
    tm = _token_tile(s)
    tmx = min(2 * tm, s)
    ts = s
    _, me = _my_position()
    xs = x.reshape(s, d)
    target = loss_target.reshape(s, d)

    vec_names = (g_pre_mix, g_post_mix, conv_a_b, conv_b_b, b_gate_r, b_gate_i, lru_lambda, g_pre_mlp, g_post_mlp)
    rows = jnp.concatenate([jnp.stack(vec_names, axis=1), jnp.zeros((nl, N_ROWS - len(vec_names), d), F32)], axis=1)
    cw = jnp.concatenate([conv_a_w, conv_b_w, jnp.zeros((nl, CW_ROWS - 7, cwid), F32)], axis=1)

    large = {"w_in": w_in, "w_a_out": w_a_out, "w_b_out": w_b_out, "w_o": w_o, "w_mlp_up": w_mlp_up, "w_mlp_down": w_mlp_down}
    groups = (("in", ("w_in",)), ("mix", ("w_a_out", "w_b_out", "w_o")), ("mlp", ("w_mlp_up", "w_mlp_down")))
    mod, cact, cwf = _prep_small(c, w_mod, b_mod, cw)
    mod = mod.reshape(nl, N_MOD, d)
    gathers = {}
    tok = mod
    for l in range(nl):
        for gname, members in groups:
            gathers[l, gname] = _gather2_start([large[n][l].astype(BF16) for n in members], tok, f"gather_start_{gname}{l}")
            tok = gathers[l, gname][4]
    wr = w_gate_r.astype(BF16)
    wi = w_gate_i.astype(BF16)

    forwarded = {}

    def forward(l, gname, after):
        forwarded[l, gname] = _gather2_forward(gathers[l, gname], after, f"gather_forward_{gname}{l}")
        return forwarded[l, gname][4]

    def gathered(l, gname, after):
        return _gather2_wait(forwarded[l, gname], after, f"gather_wait_{gname}{l}")

    saved = []
    weights = []
    xin = xs
    for l in range(nl):
        if l == 0:
            tok = forward(0, "in", tok)
        (win_f,) = gathered(l, "in", tok if l == 0 else xin)
        if l > 0:
            tok = forward(l, "mix", win_f)
        proj, h = _in_proj_fwd(xin, mod, rows, win_f, l, tmx, tok)
        if l == 0:
            tok = forward(0, "mix", proj)
        wa_f, wb_f, wo_f = (w.reshape(d, d) for w in gathered(l, "mix", proj))
        if l > 0:
            tok = forward(l, "mlp", wo_f)
        x1, hs, yap, ybp, y = _mixer_core_fwd(proj, xin, mod, rows, cwf, wr, wi, wa_f, wb_f, wo_f, l, tm, tok)
        if l == 0:
            forward(0, "mlp", x1)
        wup_f, wdn_f = gathered(l, "mlp", x1)
        wdn_f = wdn_f.reshape(-1, d)
        if l + 1 < nl:
            tok = forward(l + 1, "in", wdn_f)
        x2, ru, y2, h2, *last = _mlp_fwd(x1, mod, rows, wup_f, wdn_f, l, tmx, tok, target if l + 1 == nl else None)
        saved.append((xin, proj, h, x1, hs, yap, ybp, y, ru, y2, h2))
        weights.append((win_f, wa_f, wb_f, wo_f, wup_f, wdn_f))
        xin = x2
    dx, loss_part = last[0], last[1][0, 0]

    scatters = {}
    small = [None] * nl
    gate_parts = [None] * nl

    def scatter(l, gname, parts):
        scatters[l, gname] = _exchange_start(parts, False, f"scatter_start_{gname}{l}")
        return scatters[l, gname][4]

    for l in reversed(range(nl)):
        xin, proj, h, x1, hs, yap, ybp, y, ru, y2, h2 = saved[l]
        win_f, wa_f, wb_f, wo_f, wup_f, wdn_f = weights[l]
        dx1, dy2, dup, act, sm_mlp = _mlp_bwd(dx, x1, y2, ru, mod, rows, wup_f, wdn_f, l, tm)
        g_up = _wgrad(h2, dup, True, ts, dx1)
        g_dn = _wgrad(act, dy2, False, ts, g_up)
        tok = scatter(l, "mlp", [g_up, g_dn])
        dproj, dy, m, dyap, dybp, ya, yb, sm_mix, dwg = _mixer_core_bwd(
            dx1, y, yap, ybp, hs, proj, mod, rows, cwf, wr, wi, wa_f, wb_f, wo_f, l, tm // 2, tok)
        gate_parts[l] = _exchange_start([dwg.astype(BF16)], True, f"gates_start{l}")
        g_a = _wgrad(ya, dyap, False, ts, gate_parts[l][4])
        g_b = _wgrad(yb, dybp, False, ts, g_a)
        g_o = _wgrad(m, dy, False, ts, g_b)
        tok = scatter(l, "mix", [g_a, g_b, g_o])
        tok = scatter(l, "in", [_wgrad(h, dproj, True, ts, tok)])
        dx, sm_in = _in_proj_bwd(dproj, xin, dx1, mod, rows, win_f, l, tm, tok)
        small[l] = jnp.concatenate([sm_mlp, sm_mix, sm_in], axis=0)
    grad_x = dx.reshape(x.shape)

    recv = {}
    big = {}
    moments = {"w_in": (w_in, m_w_in, v_w_in), "w_mlp_up": (w_mlp_up, m_w_mlp_up, v_w_mlp_up), "w_a_out": (w_a_out, m_w_a_out, v_w_a_out),
               "w_b_out": (w_b_out, m_w_b_out, v_w_b_out), "w_o": (w_o, m_w_o, v_w_o), "w_mlp_down": (w_mlp_down, m_w_mlp_down, v_w_mlp_down)}

    def collect(l, gname, after):
        for n, zone in zip(dict(groups)[gname], _exchange_wait(scatters[l, gname], after, False, f"scatter_wait_{gname}{l}")):
            recv[n, l] = zone

    def update(name, after):
        w, m_, v_ = moments[name]
        big[name] = _sum_adamw([recv[name, l] for l in range(nl)], w, m_, v_, min(ADAM_ROWS, w.shape[1]), after)
        return big[name][1]

    for l, gname in ((1, "mlp"), (1, "in"), (1, "mix"), (0, "mlp")):
        collect(l, gname, dx)
    done = update("w_mlp_down", update("w_mlp_up", dx))
    collect(0, "mix", done)
    for name in ("w_a_out", "w_b_out", "w_o"):
        done = update(name, done)
    collect(0, "in", done)
    done = update("w_in", done)

    lrows = jnp.concatenate(small, axis=0)
    lrows = lrows.at[G_LOSS_ROW, 0].set(loss_part)

    def lrow(a, l, r):
        return a[l * G_LAYER_ROWS + r]

    dm = jnp.stack([jnp.concatenate([lrow(lrows, l, G_MLP_ROWS + G_MIX_ROWS + G_IN_SH), lrow(lrows, l, G_MLP_ROWS + G_MIX_ROWS + G_IN_SC),
                                     lrow(lrows, l, G_MLP_ROWS + G_MIX_GT), lrow(lrows, l, G_MLP_SH), lrow(lrows, l, G_MLP_SC),
                                     lrow(lrows, l, G_MLP_GT)]) for l in range(nl)])
    dm8 = jnp.concatenate([dm[:, None, :], jnp.zeros((nl, SUBLANES - 1, N_MOD * d), F32)], axis=1)
    srows, g_w_mod = _reduce_small(lrows, dm8, cact, done)
    loss = srows[G_LOSS_ROW, 0]
    sgates = _sum_gathered([_exchange_wait(gate_parts[l], srows, True, f"gates_wait{l}")[0] for l in range(nl)])

    def srow(l, r):
        return lrow(srows, l, r)

    def per_layer(r):
        return jnp.stack([srow(l, r) for l in range(nl)])

    mix0 = G_MLP_ROWS
    in0 = G_MLP_ROWS + G_MIX_ROWS
    g_b_mod = jnp.stack([jnp.concatenate([srow(l, in0 + G_IN_SH), srow(l, in0 + G_IN_SC), srow(l, mix0 + G_MIX_GT),
                                          srow(l, G_MLP_SH), srow(l, G_MLP_SC), srow(l, G_MLP_GT)]) for l in range(nl)])
    conv_a_full = jnp.stack([jnp.stack([srow(l, mix0 + G_MIX_CAW + k) for k in range(3)]) for l in range(nl)])
    conv_b_full = jnp.stack([jnp.stack([srow(l, mix0 + G_MIX_CBW + k) for k in range(4)]) for l in range(nl)])
    grads = {
        "b_mod": g_b_mod,
        "g_pre_mix": per_layer(in0 + G_IN_GPRE),
        "g_post_mix": per_layer(mix0 + G_MIX_GPOST),
        "conv_a_w": lax.dynamic_slice_in_dim(conv_a_full, me * cwid, cwid, axis=2),
        "conv_a_b": per_layer(mix0 + G_MIX_CAB),
        "conv_b_w": lax.dynamic_slice_in_dim(conv_b_full, me * cwid, cwid, axis=2),
        "conv_b_b": per_layer(mix0 + G_MIX_CBB),
        "w_gate_r": sgates[:, 0],
        "b_gate_r": per_layer(mix0 + G_MIX_BR),
        "w_gate_i": sgates[:, 1],
        "b_gate_i": per_layer(mix0 + G_MIX_BI),
        "lru_lambda": per_layer(mix0 + G_MIX_LAM),
        "g_pre_mlp": per_layer(G_MLP_GPRE),
        "g_post_mlp": per_layer(G_MLP_GPOST),
    }
    params = {
        "b_mod": (b_mod, m_b_mod, v_b_mod), "g_pre_mix": (g_pre_mix, m_g_pre_mix, v_g_pre_mix), "g_post_mix": (g_post_mix, m_g_post_mix, v_g_post_mix),
        "conv_a_w": (conv_a_w, m_conv_a_w, v_conv_a_w), "conv_a_b": (conv_a_b, m_conv_a_b, v_conv_a_b),
        "conv_b_w": (conv_b_w, m_conv_b_w, v_conv_b_w), "conv_b_b": (conv_b_b, m_conv_b_b, v_conv_b_b),
        "w_gate_r": (w_gate_r, m_w_gate_r, v_w_gate_r), "b_gate_r": (b_gate_r, m_b_gate_r, v_b_gate_r),
        "w_gate_i": (w_gate_i, m_w_gate_i, v_w_gate_i), "b_gate_i": (b_gate_i, m_b_gate_i, v_b_gate_i),
        "lru_lambda": (lru_lambda, m_lru_lambda, v_lru_lambda), "g_pre_mlp": (g_pre_mlp, m_g_pre_mlp, v_g_pre_mlp),
        "g_post_mlp": (g_post_mlp, m_g_post_mlp, v_g_post_mlp),
    }
    out = {}
    for name, g in grads.items():
        w, m_, v_ = params[name]
        flat = (-1, w.shape[-1])
        dl, nm, nv = _adamw(w.reshape(flat), g.reshape(flat), m_.reshape(flat), v_.reshape(flat))
        out[name] = (g.reshape(w.shape), dl.reshape(w.shape), nm.reshape(w.shape), nv.reshape(w.shape))
    out["w_mod"] = (g_w_mod,) + tuple(_adamw_tiled(w_mod, g_w_mod, m_w_mod, v_w_mod, min(ADAM_ROWS, d)))
    out.update(big)

    order = ("w_mod", "b_mod", "g_pre_mix", "g_post_mix", "w_in", "conv_a_w", "conv_a_b", "w_a_out", "conv_b_w", "conv_b_b", "w_gate_r", "b_gate_r",
             "w_gate_i", "b_gate_i", "lru_lambda", "w_b_out", "w_o", "g_pre_mlp", "g_post_mlp", "w_mlp_up", "w_mlp_down")
    return (loss, grad_x) + tuple(out[n][0] for n in order) + tuple(out[n][1] for n in order) + tuple(out[n][2] for n in order) + tuple(out[n][3] for n in order)
```

```python
import jax
import jax.numpy as jnp
from jax import lax
from jax.experimental import pallas as pl
from jax.experimental.pallas import tpu as pltpu

F32, BF16 = jnp.float32, jnp.bfloat16
EPS = 1e-6
LRU_C = 8.0
N_DEV = 8
N_MOD = 6
SUBLANES = 8
VMEM_BUDGET = 56 * 1024 * 1024
WGRAD_ROWS = 512
TOKEN_TILE = 256
ADAM_ROWS = 256
SCAN_ROWS = SUBLANES * SUBLANES
ADAM_LR, ADAM_B1, ADAM_B2, ADAM_EPS, ADAM_WD, ADAM_STEP = 0.001, 0.9, 0.999, 1e-08, 0.01, 10
MESH = pl.DeviceIdType.MESH
VMEM_SPEC = pl.BlockSpec(memory_space=pltpu.VMEM)
ANY_SPEC = pl.BlockSpec(memory_space=pl.ANY)
HBM_SPEC = pl.BlockSpec(memory_space=pltpu.HBM)
SEM_SPEC = pl.BlockSpec(memory_space=pltpu.SEMAPHORE)
SIDE_EFFECT = pltpu.SideEffectType.DATAFLOW_SIDE_EFFECTING

R_G_PRE_MIX, R_G_POST_MIX, R_CONV_A_B, R_CONV_B_B, R_B_GATE_R, R_B_GATE_I, R_LAMBDA, R_G_PRE_MLP, R_G_POST_MLP = range(9)
N_ROWS = 16
M_SH_M, M_SC_M, M_GT_M, M_SH_F, M_SC_F, M_GT_F = range(6)
CW_A, CW_B, CW_ROWS = 0, 3, 8
G_MLP_GT, G_MLP_GPOST, G_MLP_SC, G_MLP_SH, G_MLP_GPRE, G_LOSS_ROW, G_MLP_ROWS = 0, 1, 2, 3, 4, 7, 8
(G_MIX_GT, G_MIX_GPOST, G_MIX_CAB, G_MIX_CAW, G_MIX_CBB, G_MIX_CBW, G_MIX_BR, G_MIX_BI, G_MIX_LAM) = 0, 1, 2, 3, 6, 7, 11, 12, 13
G_MIX_ROWS = 16
G_IN_SC, G_IN_SH, G_IN_GPRE, G_IN_ROWS = 0, 1, 2, 8
G_LAYER_ROWS = G_MLP_ROWS + G_MIX_ROWS + G_IN_ROWS


def _cparams(dims=None, vmem=None):
    kw = {}
    if dims is not None:
        kw["dimension_semantics"] = dims
    if vmem is not None:
        kw["vmem_limit_bytes"] = int(min(max(vmem, 16 * 1024 * 1024), VMEM_BUDGET))
    return pltpu.CompilerParams(**kw)


def _nbytes(shape, dtype):
    n = 1
    for s in shape:
        n *= s
    return n * jnp.dtype(dtype).itemsize


def _hbm(*arrays):
    return tuple(pltpu.with_memory_space_constraint(a, pltpu.HBM) for a in arrays)


def _resident(block, index_map):
    return pl.BlockSpec(block, index_map, pipeline_mode=pl.Buffered(1))


def _my_position():
    x, y, c = lax.axis_index("x"), lax.axis_index("y"), lax.axis_index("c")
    return (x, y, c), 4 * x + 2 * y + c


def _peer(pos, k):
    x, y, c = pos
    px = 1 - x if k & 4 else x
    py = 1 - y if k & 2 else y
    pc = 1 - c if k & 1 else c
    return (px, py, pc), 4 * px + 2 * py + pc


def _remote(src, dst, ssem, rsem, peer):
    return pltpu.make_async_remote_copy(src_ref=src, dst_ref=dst, send_sem=ssem, recv_sem=rsem, device_id=peer, device_id_type=MESH)


def _dot(a, b):
    return jnp.dot(a, b, preferred_element_type=F32)


def _dot_nt(a, b):
    return lax.dot_general(a, b, (((1,), (1,)), ((), ())), preferred_element_type=F32)


def _dot_tn(a, b):
    return lax.dot_general(a, b, (((0,), (0,)), ((), ())), preferred_element_type=F32)


def _colsum(v):
    return jnp.sum(v, axis=0, keepdims=True)


def _sigmoid(v):
    return jax.nn.sigmoid(v)


GELU_K, GELU_C = 0.7978845608028654, 0.044715


def _gelu(v):
    s = 1.0 / (1.0 + jnp.exp(v * (-2.0 * GELU_K - (2.0 * GELU_K * GELU_C) * (v * v))))
    return v * s, s


def _gelu_grad(v, s):
    return s * (1.0 + (v * (1.0 - s)) * (2.0 * GELU_K + (6.0 * GELU_K * GELU_C) * (v * v)))


def _neg_expm1_twice(v):
    t = jnp.tanh(v)
    return (-2.0 * t) / (1.0 - t)


def _softplus_neg(lam):
    z = -lam
    u = jnp.exp(-jnp.abs(z))
    w = 1.0 + u
    l1p = jnp.where(w == 1.0, u, jnp.log(w) * u / (w - 1.0))
    return jnp.maximum(z, 0.0) + l1p


def _rms(v):
    return lax.rsqrt(jnp.mean(v * v, axis=-1, keepdims=True) + EPS)


def _prenorm_bwd(xv, dh, g, sc):
    r = _rms(xv)
    xn = xv * r
    n = xn * g
    dsc = _colsum(dh * n)
    dsh = _colsum(dh)
    dn = dh * (1.0 + sc)
    dg = _colsum(dn * xn)
    dxn = dn * g
    dx = r * (dxn - xn * jnp.mean(dxn * xn, axis=-1, keepdims=True))
    return dx, dsc, dsh, dg


def _postnorm_bwd(yv, dout, g, gt):
    r = _rms(yv)
    yn = yv * r
    dgt = _colsum(dout * (yn * g))
    dn = dout * gt
    dg = _colsum(dn * yn)
    dyn = dn * g
    dy = r * (dyn - yn * jnp.mean(dyn * yn, axis=-1, keepdims=True))
    return dy, dgt, dg


def _gates(xc, wr_ref, wi_ref, b_r, b_i, sp, nh, bw):
    xcb = xc.astype(BF16)
    zr = jnp.concatenate([_dot(xcb[:, h * bw:(h + 1) * bw], wr_ref[h]) for h in range(nh)], axis=1) + b_r
    zi = jnp.concatenate([_dot(xcb[:, h * bw:(h + 1) * bw], wi_ref[h]) for h in range(nh)], axis=1) + b_i
    r = _sigmoid(zr)
    ig = _sigmoid(zi)
    la = (-LRU_C * r) * sp
    a = jnp.exp(la)
    mult = jnp.sqrt(_neg_expm1_twice(la))
    return xcb, r, ig, a, mult


def _shift_rows(cur, edge, k, up):
    t, dd = cur.shape
    blocks = cur.reshape(t // SUBLANES, SUBLANES, dd)
    row = lax.broadcasted_iota(jnp.int32, (1, SUBLANES, dd), 1)
    if up:
        r = pltpu.roll(blocks, SUBLANES - k, 1)
        nxt = jnp.concatenate([r[1:], pltpu.roll(edge, SUBLANES - k, 0)[None]], axis=0)
        out = jnp.where(row >= SUBLANES - k, nxt, r)
    else:
        r = pltpu.roll(blocks, k, 1)
        prv = jnp.concatenate([pltpu.roll(edge, k, 0)[None], r[:-1]], axis=0)
        out = jnp.where(row < k, prv, r)
    return out.reshape(t, dd)


def _scan_group(a_ref, b_ref, o_ref, h, base, carry, reverse):
    def rows(k):
        return pl.ds(base + k, SUBLANES, stride=SUBLANES)

    order = range(SUBLANES - 1, -1, -1) if reverse else range(SUBLANES)
    loc, prod = {}, {}
    prev = None
    for k in order:
        a, b = a_ref[h, rows(k), :], b_ref[h, rows(k), :]
        loc[k] = b if prev is None else a * loc[prev] + b
        prod[k] = a if prev is None else a * prod[prev]
        prev = k
    pa, pb = _scan_block(prod[prev], loc[prev], reverse)
    ends = pb + pa * carry
    row = lax.broadcasted_iota(jnp.int32, ends.shape, 0)
    if reverse:
        into = jnp.where(row < SUBLANES - 1, pltpu.roll(ends, SUBLANES - 1, 0), carry)
        last = ends[0:1, :]
    else:
        into = jnp.where(row >= 1, pltpu.roll(ends, 1, 0), carry)
        last = ends[SUBLANES - 1:SUBLANES, :]
    for k in range(SUBLANES):
        o_ref[h, rows(k), :] = loc[k] + prod[k] * into
    return jnp.broadcast_to(last, ends.shape)


def _scan_block(a8, b8, reverse):
    row = lax.broadcasted_iota(jnp.int32, a8.shape, 0)
    for s in (1, 2, 4):
        if reverse:
            keep = row < SUBLANES - s
            a_sh = pltpu.roll(a8, SUBLANES - s, 0)
            b_sh = pltpu.roll(b8, SUBLANES - s, 0)
        else:
            keep = row >= s
            a_sh = pltpu.roll(a8, s, 0)
            b_sh = pltpu.roll(b8, s, 0)
        b8 = b8 + a8 * jnp.where(keep, b_sh, 0.0)
        a8 = a8 * jnp.where(keep, a_sh, 1.0)
    return a8, b8


def _prep_small(c, w_mod, b_mod, cw):
    d = c.shape[1]
    cm = w_mod.shape[2]
    cwid = cw.shape[2]
    nl = w_mod.shape[0]

    def body(c_ref, wm_ref, bm_ref, cw_ref, mod_ref, cact_ref, cwf_ref, cbuf, pbuf, rbuf, ssem, rsem, lsem):
        pos, me = _my_position()
        me8 = pl.multiple_of(me * SUBLANES, SUBLANES)
        cbuf[pl.ds(me8, SUBLANES), :] = jnp.broadcast_to(c_ref[...], (SUBLANES, d))
        own_cw = pltpu.make_async_copy(cw_ref, cwf_ref.at[:, :, pl.ds(me * cwid, cwid)], lsem.at[0])
        own_cw.start()
        first = []
        for k in range(1, N_DEV):
            peer, _ = _peer(pos, k)
            rows = cbuf.at[pl.ds(me8, SUBLANES), :]
            first.append(_remote(rows, rows, ssem.at[0, k - 1], rsem.at[0, k - 1], peer))
            first.append(_remote(cw_ref, cwf_ref.at[:, :, pl.ds(me * cwid, cwid)], ssem.at[1, k - 1], rsem.at[1, k - 1], peer))
        for cp in first:
            cp.start()
        for k in range(1, N_DEV):
            peer, pj = _peer(pos, k)
            pj8 = pl.multiple_of(pj * SUBLANES, SUBLANES)
            rows = cbuf.at[pl.ds(pj8, SUBLANES), :]
            _remote(rows, rows, ssem.at[0, k - 1], rsem.at[0, k - 1], peer).wait_recv()
        cv = cbuf[...]
        cact = cv * _sigmoid(cv)
        cact_ref[...] = cact
        cb = cact.astype(BF16)
        for l in range(nl):
            pbuf[l] = _dot(cb, wm_ref[l].astype(BF16))
        own_p = pltpu.make_async_copy(pbuf.at[:, pl.ds(me8, SUBLANES), :], rbuf.at[me], lsem.at[1])
        own_p.start()
        second = []
        for k in range(1, N_DEV):
            peer, pj = _peer(pos, k)
            pj8 = pl.multiple_of(pj * SUBLANES, SUBLANES)
            second.append(_remote(pbuf.at[:, pl.ds(pj8, SUBLANES), :], rbuf.at[me], ssem.at[2, k - 1], rsem.at[2, k - 1], peer))
        for cp in second:
            cp.start()
        for k in range(1, N_DEV):
            peer, pj = _peer(pos, k)
            _remote(pbuf.at[:, pl.ds(0, SUBLANES), :], rbuf.at[pj], ssem.at[2, k - 1], rsem.at[2, k - 1], peer).wait_recv()
            _remote(cw_ref, cwf_ref.at[:, :, pl.ds(pj * cwid, cwid)], ssem.at[1, k - 1], rsem.at[1, k - 1], peer).wait_recv()
        own_p.wait()
        own_cw.wait()
        for l in range(nl):
            for j in range(N_DEV):
                mod_ref[l:l + 1, j * cm:(j + 1) * cm] = rbuf[j, l, 0:1, :] + bm_ref[l:l + 1, j * cm:(j + 1) * cm]
        for cp in first + second:
            cp.wait_send()

    return pl.pallas_call(
        body,
        name="prep_small",
        out_shape=(
            jax.ShapeDtypeStruct((nl, N_MOD * d), F32),
            jax.ShapeDtypeStruct((N_DEV * SUBLANES, d), F32),
            jax.ShapeDtypeStruct((nl, CW_ROWS, d), F32),
        ),
        in_specs=[VMEM_SPEC] * 4,
        out_specs=(VMEM_SPEC,) * 3,
        scratch_shapes=[
            pltpu.VMEM((N_DEV * SUBLANES, d), F32),
            pltpu.VMEM((nl, N_DEV * SUBLANES, cm), F32),
            pltpu.VMEM((N_DEV, nl, SUBLANES, cm), F32),
            pltpu.SemaphoreType.DMA((3, N_DEV - 1)),
            pltpu.SemaphoreType.DMA((3, N_DEV - 1)),
            pltpu.SemaphoreType.DMA((2,)),
        ],
        compiler_params=_cparams(vmem=3 * _nbytes(w_mod.shape, F32)),
    )(c, w_mod, b_mod, cw)


def _exchange_start(parts, gather, name):
    n = len(parts)
    lands = [lax.empty(((N_DEV,) + tuple(p.shape)) if gather else tuple(p.shape), p.dtype) for p in parts]

    def body(*refs):
        ins, lnd = refs[:n], refs[n:2 * n]
        ssem, rsem, token = refs[2 * n], refs[2 * n + 1], refs[-1]
        pos, me = _my_position()
        for k in range(1, N_DEV):
            peer, pj = _peer(pos, k)
            for t in range(n):
                src = ins[t] if gather else ins[t].at[pj]
                q = t * (N_DEV - 1) + k - 1
                _remote(src, lnd[t].at[me], ssem.at[q], rsem.at[q], peer).start()
        token[...] = jnp.zeros(token.shape, F32)

    out = pl.pallas_call(
        body,
        name=name,
        out_shape=(pltpu.SemaphoreType.DMA((n * (N_DEV - 1),)), pltpu.SemaphoreType.DMA((n * (N_DEV - 1),)))
        + tuple(pltpu.HBM(p.shape, p.dtype) for p in parts) + tuple(pltpu.HBM(p.shape, p.dtype) for p in lands)
        + (jax.ShapeDtypeStruct((SUBLANES, 128), F32),),
        in_specs=[HBM_SPEC] * (2 * n),
        out_specs=(SEM_SPEC, SEM_SPEC) + (HBM_SPEC,) * (2 * n) + (VMEM_SPEC,),
        input_output_aliases={i: 2 + i for i in range(2 * n)},
        compiler_params=pltpu.CompilerParams(has_side_effects=SIDE_EFFECT),
    )(*[pltpu.with_memory_space_constraint(p, pltpu.HBM) for p in list(parts) + lands])
    return out[0], out[1], list(out[2:2 + n]), list(out[2 + n:2 + 2 * n]), out[-1]


def _exchange_wait(started, after, gather, name):
    ssem, rsem, parts, lands, _ = started
    n = len(parts)

    def body(*refs):
        ins, lnd = refs[:n], refs[n:2 * n]
        ssem_ref, rsem_ref = refs[2 * n], refs[2 * n + 1]
        stage, lsem = refs[-1 - n:-1], refs[-1]
        pos, me = _my_position()
        load = []
        for t in range(n):
            src = ins[t] if gather else ins[t].at[me]
            load.append(pltpu.make_async_copy(src, stage[t], lsem.at[t]))
            load[-1].start()
        store = []
        for t in range(n):
            load[t].wait()
            store.append(pltpu.make_async_copy(stage[t], lnd[t].at[me], lsem.at[t]))
            store[-1].start()
        for k in range(1, N_DEV):
            peer, pj = _peer(pos, k)
            for t in range(n):
                src = ins[t] if gather else ins[t].at[pj]
                q = t * (N_DEV - 1) + k - 1
                _remote(src, lnd[t].at[me], ssem_ref.at[q], rsem_ref.at[q], peer).wait_send()
                _remote(src, lnd[t].at[pj], ssem_ref.at[q], rsem_ref.at[q], peer).wait_recv()
        for cp in store:
            cp.wait()

    out = pl.pallas_call(
        body,
        name=name,
        out_shape=tuple(pltpu.HBM(p.shape, p.dtype) for p in parts) + tuple(pltpu.HBM(p.shape, p.dtype) for p in lands),
        in_specs=[HBM_SPEC] * (2 * n) + [SEM_SPEC, SEM_SPEC, ANY_SPEC],
        out_specs=(HBM_SPEC,) * (2 * n),
        input_output_aliases={i: i for i in range(2 * n)},
        scratch_shapes=[pltpu.VMEM(tuple(z.shape[1:]), z.dtype) for z in lands] + [pltpu.SemaphoreType.DMA((n,))],
        compiler_params=pltpu.CompilerParams(has_side_effects=SIDE_EFFECT),
    )(*parts, *lands, ssem, rsem, after)
    return list(out[n:])


CHIP_PEERS = 3


def _chip_peer(pos, k, core):
    x, y, _ = pos
    px = 1 - x if k & 2 else x
    py = 1 - y if k & 1 else y
    return (px, py, core), 4 * px + 2 * py + core


def _gather2_start(parts, after, name):
    n = len(parts)
    lands = [lax.empty((N_DEV,) + tuple(p.shape), p.dtype) for p in parts]
    per = CHIP_PEERS + 1

    def body(*refs):
        ins, lnd = refs[:n], refs[n:2 * n]
        ssem, rsem, token = refs[2 * n + 1], refs[2 * n + 2], refs[-1]
        pos, me = _my_position()
        sibling = (pos[0], pos[1], 1 - pos[2])
        for t in range(n):
            _remote(ins[t], lnd[t].at[me], ssem.at[per * t], rsem.at[per * t], sibling).start()
        for k in range(1, per):
            peer, _ = _chip_peer(pos, k, pos[2])
            for t in range(n):
                _remote(ins[t], lnd[t].at[me], ssem.at[per * t + k], rsem.at[per * t + k], peer).start()
        token[...] = jnp.zeros(token.shape, F32)

    out = pl.pallas_call(
        body,
        name=name,
        out_shape=(pltpu.SemaphoreType.DMA((n * per,)), pltpu.SemaphoreType.DMA((n * per,)))
        + tuple(pltpu.HBM(p.shape, p.dtype) for p in parts) + tuple(pltpu.HBM(p.shape, p.dtype) for p in lands)
        + (jax.ShapeDtypeStruct((SUBLANES, 128), F32),),
        in_specs=[HBM_SPEC] * (2 * n) + [ANY_SPEC],
        out_specs=(SEM_SPEC, SEM_SPEC) + (HBM_SPEC,) * (2 * n) + (VMEM_SPEC,),
        input_output_aliases={i: 2 + i for i in range(2 * n)},
        compiler_params=pltpu.CompilerParams(has_side_effects=SIDE_EFFECT),
    )(*[pltpu.with_memory_space_constraint(p, pltpu.HBM) for p in list(parts) + lands], after)
    return out[0], out[1], list(out[2:2 + n]), list(out[2 + n:2 + 2 * n]), out[-1]


def _gather2_forward(started, after, name):
    ssem, rsem, parts, lands, _ = started
    n = len(lands)
    per = CHIP_PEERS + 1

    def body(*refs):
        lnd = refs[:n]
        rsem_a, fsend, frecv, token = refs[n], refs[2 * n + 2], refs[2 * n + 3], refs[2 * n + 4]
        pos, _ = _my_position()
        sibling = (pos[0], pos[1], 1 - pos[2])
        for k in range(1, per):
            peer, pk = _chip_peer(pos, k, pos[2])
            for t in range(n):
                block = lnd[t].at[pk]
                _remote(block, block, rsem_a.at[per * t + k], rsem_a.at[per * t + k], peer).wait_recv()
                q = CHIP_PEERS * t + k - 1
                _remote(block, block, fsend.at[q], frecv.at[q], sibling).start()
        token[...] = jnp.zeros(token.shape, F32)

    out = pl.pallas_call(
        body,
        name=name,
        out_shape=tuple(pltpu.HBM(p.shape, p.dtype) for p in lands)
        + (pltpu.SemaphoreType.DMA((n * CHIP_PEERS,)), pltpu.SemaphoreType.DMA((n * CHIP_PEERS,)),
           jax.ShapeDtypeStruct((SUBLANES, 128), F32)),
        in_specs=[HBM_SPEC] * n + [SEM_SPEC, ANY_SPEC],
        out_specs=(HBM_SPEC,) * n + (SEM_SPEC, SEM_SPEC, VMEM_SPEC),
        input_output_aliases={i: i for i in range(n)},
        compiler_params=pltpu.CompilerParams(has_side_effects=SIDE_EFFECT),
    )(*lands, rsem, after)
    return ssem, rsem, parts, list(out[:n]), out[n + 2], out[n], out[n + 1]


def _gather2_wait(forwarded, after, name):
    ssem, rsem, parts, lands, _, fsend, frecv = forwarded
    n = len(parts)
    per = CHIP_PEERS + 1

    def body(*refs):
        ins, lnd = refs[:n], refs[n:2 * n]
        ssem_a, rsem_a, fs, fr = refs[2 * n:2 * n + 4]
        stage, lsem = refs[-1 - n:-1], refs[-1]
        pos, me = _my_position()
        sibling = (pos[0], pos[1], 1 - pos[2])
        sib = 4 * pos[0] + 2 * pos[1] + 1 - pos[2]
        load = []
        for t in range(n):
            load.append(pltpu.make_async_copy(ins[t], stage[t], lsem.at[t]))
            load[-1].start()
        store = []
        for t in range(n):
            load[t].wait()
            store.append(pltpu.make_async_copy(stage[t], lnd[t].at[me], lsem.at[t]))
            store[-1].start()
        for t in range(n):
            _remote(ins[t], lnd[t].at[me], ssem_a.at[per * t], rsem_a.at[per * t], sibling).wait_send()
            _remote(ins[t], lnd[t].at[sib], ssem_a.at[per * t], rsem_a.at[per * t], sibling).wait_recv()
        for k in range(1, per):
            peer, pk = _chip_peer(pos, k, pos[2])
            _, qk = _chip_peer(pos, k, 1 - pos[2])
            for t in range(n):
                q = CHIP_PEERS * t + k - 1
                _remote(ins[t], lnd[t].at[me], ssem_a.at[per * t + k], rsem_a.at[per * t + k], peer).wait_send()
                _remote(lnd[t].at[pk], lnd[t].at[pk], fs.at[q], fr.at[q], sibling).wait_send()
                _remote(lnd[t].at[qk], lnd[t].at[qk], fs.at[q], fr.at[q], sibling).wait_recv()
        for cp in store:
            cp.wait()

    out = pl.pallas_call(
        body,
        name=name,
        out_shape=tuple(pltpu.HBM(p.shape, p.dtype) for p in parts) + tuple(pltpu.HBM(p.shape, p.dtype) for p in lands),
        in_specs=[HBM_SPEC] * (2 * n) + [SEM_SPEC] * 4 + [ANY_SPEC],
        out_specs=(HBM_SPEC,) * (2 * n),
        input_output_aliases={i: i for i in range(2 * n)},
        scratch_shapes=[pltpu.VMEM(tuple(z.shape[1:]), z.dtype) for z in lands] + [pltpu.SemaphoreType.DMA((n,))],
        compiler_params=pltpu.CompilerParams(has_side_effects=SIDE_EFFECT),
    )(*parts, *lands, ssem, rsem, fsend, frecv, after)
    return list(out[n:])


def _reduce_small(rows, dm8, cact, after):
    r, d = rows.shape
    nl = dm8.shape[0]
    cm = dm8.shape[2] // N_DEV
    lw = d // N_DEV

    def body(rows_ref, dm_ref, cact_ref, after_ref, orow_ref, owm_ref, gr, mine, dmr, ssem, rsem, lsem):
        pos, me = _my_position()
        me8 = pl.multiple_of(me * SUBLANES, SUBLANES)

        def lanes(j):
            return pl.ds(j * lw, lw)

        own = [pltpu.make_async_copy(dm_ref.at[:, :, pl.ds(me * cm, cm)], dmr.at[:, pl.ds(me8, SUBLANES), :], lsem.at[0]),
               pltpu.make_async_copy(rows_ref.at[:, lanes(me)], gr.at[me], lsem.at[1])]
        first = []
        for k in range(1, N_DEV):
            peer, pj = _peer(pos, k)
            first.append(_remote(rows_ref.at[:, lanes(pj)], gr.at[me], ssem.at[0, k - 1], rsem.at[0, k - 1], peer))
            first.append(_remote(dm_ref.at[:, :, pl.ds(pj * cm, cm)], dmr.at[:, pl.ds(me8, SUBLANES), :],
                                 ssem.at[1, k - 1], rsem.at[1, k - 1], peer))
        for cp in own + first:
            cp.start()
        for k in range(1, N_DEV):
            peer, pj = _peer(pos, k)
            pj8 = pl.multiple_of(pj * SUBLANES, SUBLANES)
            _remote(rows_ref.at[:, lanes(0)], gr.at[pj], ssem.at[0, k - 1], rsem.at[0, k - 1], peer).wait_recv()
            _remote(dm_ref.at[:, :, pl.ds(0, cm)], dmr.at[:, pl.ds(pj8, SUBLANES), :], ssem.at[1, k - 1], rsem.at[1, k - 1], peer).wait_recv()
        for cp in own:
            cp.wait()
        acc = gr[0]
        for j in range(1, N_DEV):
            acc = acc + gr[j]
        mine[...] = acc
        own_sum = pltpu.make_async_copy(mine, orow_ref.at[:, lanes(me)], lsem.at[1])
        own_sum.start()
        second = []
        for k in range(1, N_DEV):
            peer, _ = _peer(pos, k)
            second.append(_remote(mine, orow_ref.at[:, lanes(me)], ssem.at[2, k - 1], rsem.at[2, k - 1], peer))
            second[-1].start()
        cb = cact_ref[...].astype(BF16)
        for l in range(nl):
            owm_ref[l] = _dot_tn(cb, dmr[l].astype(BF16))
        for k in range(1, N_DEV):
            peer, pj = _peer(pos, k)
            _remote(mine, orow_ref.at[:, lanes(pj)], ssem.at[2, k - 1], rsem.at[2, k - 1], peer).wait_recv()
        own_sum.wait()
        for cp in first + second:
            cp.wait_send()

    return pl.pallas_call(
        body,
        name="reduce_small",
        out_shape=(jax.ShapeDtypeStruct((r, d), F32), jax.ShapeDtypeStruct((nl, d, cm), F32)),
        in_specs=[VMEM_SPEC] * 3 + [ANY_SPEC],
        out_specs=(VMEM_SPEC,) * 2,
        scratch_shapes=[
            pltpu.VMEM((N_DEV, r, lw), F32),
            pltpu.VMEM((r, lw), F32),
            pltpu.VMEM((nl, N_DEV * SUBLANES, cm), F32),
            pltpu.SemaphoreType.DMA((3, N_DEV - 1)),
            pltpu.SemaphoreType.DMA((3, N_DEV - 1)),
            pltpu.SemaphoreType.DMA((2,)),
        ],
        compiler_params=_cparams(vmem=4 * _nbytes((r, d), F32) + 6 * _nbytes((nl, d, cm), F32)),
    )(rows, dm8, cact, after)


def _sum_gathered(zones):
    nl = len(zones)

    def body(*refs):
        for l in range(nl):
            acc = refs[l][0].astype(F32)
            for j in range(1, N_DEV):
                acc = acc + refs[l][j].astype(F32)
            refs[nl][l] = acc

    return pl.pallas_call(
        body,
        name="sum_gathered",
        out_shape=jax.ShapeDtypeStruct((nl,) + tuple(zones[0].shape[1:]), F32),
        in_specs=[VMEM_SPEC] * nl,
        out_specs=VMEM_SPEC,
        compiler_params=_cparams(vmem=8 * nl * _nbytes(zones[0].shape, BF16)),
    )(*zones)


def _in_proj_fwd(x, mod, rows, win_f, l, tm, after):
    s, d = x.shape
    nb, _, ci = win_f.shape

    def body(x_ref, mod_ref, rows_ref, w_ref, after_ref, proj_ref, h_ref):
        xv = x_ref[...]
        g = rows_ref[R_G_PRE_MIX:R_G_PRE_MIX + 1, :]
        h = (xv * _rms(xv) * g) * (1.0 + mod_ref[M_SC_M:M_SC_M + 1, :]) + mod_ref[M_SH_M:M_SH_M + 1, :]
        hb = h.astype(BF16)
        h_ref[...] = hb
        for j in range(nb):
            proj_ref[:, j * ci:(j + 1) * ci] = _dot(hb, w_ref[j])

    return pl.pallas_call(
        body,
        name="in_proj_fwd",
        grid=(s // tm,),
        in_specs=[
            pl.BlockSpec((tm, d), lambda i: (i, 0)),
            _resident((None, N_MOD, d), lambda i: (l, 0, 0)),
            _resident((None, N_ROWS, d), lambda i: (l, 0, 0)),
            _resident((nb, d, ci), lambda i: (0, 0, 0)),
            ANY_SPEC,
        ],
        out_specs=(pl.BlockSpec((tm, nb * ci), lambda i: (i, 0)), pl.BlockSpec((tm, d), lambda i: (i, 0))),
        out_shape=(jax.ShapeDtypeStruct((s, nb * ci), F32), jax.ShapeDtypeStruct((s, d), BF16)),
        compiler_params=_cparams(("parallel",), _nbytes((nb, d, ci), BF16) + 3 * _nbytes((tm, nb * ci), F32) + 8 * _nbytes((tm, d), F32)),
    )(*_hbm(x), mod, rows, win_f, after)


def _mixer_core_fwd(proj, x, mod, rows, cwf, wr, wi, wa_f, wb_f, wo_f, l, tm, after):
    s, d = x.shape
    nh, bw, _ = wr.shape[1:]

    def body(proj_ref, x_ref, mod_ref, rows_ref, cw_ref, wr_ref, wi_ref, wa_ref, wb_ref, wo_ref, after_ref,
             x1_ref, hs_ref, yap_ref, ybp_ref, y_ref, cvbuf, xbbuf, a_s, b_s, h_s, hprev):
        i = pl.program_id(0)

        @pl.when(i == 0)
        def _():
            cvbuf[...] = jnp.zeros((SUBLANES, d), F32)
            xbbuf[...] = jnp.zeros((SUBLANES, d), F32)
            hprev[...] = jnp.zeros((SUBLANES, d), F32)

        def row(r):
            return rows_ref[r:r + 1, :]

        def tap(r):
            return cw_ref[r:r + 1, :]

        ba = proj_ref[:, 0:d]
        cv = proj_ref[:, d:2 * d] * proj_ref[:, 2 * d:3 * d]
        cvt = cvbuf[...]
        conv3 = ((row(R_CONV_A_B) + _shift_rows(cv, cvt, 2, False) * tap(CW_A)) + _shift_rows(cv, cvt, 1, False) * tap(CW_A + 1)) + cv * tap(CW_A + 2)
        ya = ba * conv3
        cvbuf[...] = cv[tm - SUBLANES:, :]
        sp = _softplus_neg(row(R_LAMBDA))
        for h in range(nh):
            lo, hi = h * bw, (h + 1) * bw
            xb = proj_ref[:, 3 * d + lo:3 * d + hi]
            xbt = xbbuf[:, lo:hi]

            def hrow(r):
                return rows_ref[r:r + 1, lo:hi]

            def htap(r):
                return cw_ref[r:r + 1, lo:hi]

            xc = (((hrow(R_CONV_B_B) + _shift_rows(xb, xbt, 3, False) * htap(CW_B)) + _shift_rows(xb, xbt, 2, False) * htap(CW_B + 1))
                  + _shift_rows(xb, xbt, 1, False) * htap(CW_B + 2)) + xb * htap(CW_B + 3)
            xcb = xc.astype(BF16)
            r = _sigmoid(_dot(xcb, wr_ref[h]) + hrow(R_B_GATE_R))
            ig = _sigmoid(_dot(xcb, wi_ref[h]) + hrow(R_B_GATE_I))
            la = (-LRU_C * r) * sp[:, lo:hi]
            a_s[h] = jnp.exp(la)
            b_s[h] = jnp.sqrt(_neg_expm1_twice(la)) * (ig * xc)
        xbbuf[...] = proj_ref[tm - SUBLANES:, 3 * d:4 * d]

        def group(j, hp):
            base = pl.multiple_of(j * SCAN_ROWS, SCAN_ROWS)
            return jnp.concatenate([_scan_group(a_s, b_s, h_s, h, base, hp[:, h * bw:(h + 1) * bw], False) for h in range(nh)], axis=1)

        hprev[...] = lax.fori_loop(0, tm // SCAN_ROWS, group, hprev[...])
        for h in range(nh):
            hs_ref[:, h * bw:(h + 1) * bw] = h_s[h]
        gel, _ = _gelu(proj_ref[:, 4 * d:5 * d])
        yb = hs_ref[...] * gel
        yap = _dot(ya.astype(BF16), wa_ref[...])
        ybp = _dot(yb.astype(BF16), wb_ref[...])
        yap_ref[...] = yap
        ybp_ref[...] = ybp
        m = _sigmoid(proj_ref[:, 5 * d:6 * d]) * yap + _sigmoid(proj_ref[:, 6 * d:7 * d]) * ybp
        y = _dot(m.astype(BF16), wo_ref[...])
        y_ref[...] = y
        x1_ref[...] = x_ref[...] + mod_ref[M_GT_M:M_GT_M + 1, :] * ((y * _rms(y)) * row(R_G_POST_MIX))

    tile = pl.BlockSpec((tm, d), lambda i: (i, 0))
    return pl.pallas_call(
        body,
        name="mixer_core_fwd",
        grid=(s // tm,),
        in_specs=[
            pl.BlockSpec((tm, 7 * d), lambda i: (i, 0)),
            tile,
            _resident((None, N_MOD, d), lambda i: (l, 0, 0)),
            _resident((None, N_ROWS, d), lambda i: (l, 0, 0)),
            _resident((None, CW_ROWS, d), lambda i: (l, 0, 0)),
            _resident((None, nh, bw, bw), lambda i: (l, 0, 0, 0)),
            _resident((None, nh, bw, bw), lambda i: (l, 0, 0, 0)),
            _resident((d, d), lambda i: (0, 0)),
            _resident((d, d), lambda i: (0, 0)),
            _resident((d, d), lambda i: (0, 0)),
            ANY_SPEC,
        ],
        out_specs=(tile,) * 5,
        out_shape=(jax.ShapeDtypeStruct((s, d), F32),) * 5,
        scratch_shapes=[
            pltpu.VMEM((SUBLANES, d), F32),
            pltpu.VMEM((SUBLANES, d), F32),
            pltpu.VMEM((nh, tm, bw), F32),
            pltpu.VMEM((nh, tm, bw), F32),
            pltpu.VMEM((nh, tm, bw), F32),
            pltpu.VMEM((SUBLANES, d), F32),
        ],
        compiler_params=_cparams(("arbitrary",), 3 * _nbytes((d, d), BF16) + 2 * _nbytes((tm, 7 * d), F32) + 40 * _nbytes((tm, d), F32)),
    )(*_hbm(proj, x), mod, rows, cwf, wr, wi, wa_f, wb_f, wo_f, after)


def _mlp_fwd(x1, mod, rows, wup_f, wdn_f, l, tm, after, target=None):
    s, d = x1.shape
    nb, _, cu = wup_f.shape
    dff = nb * cu
    with_loss = target is not None

    def body(x1_ref, mod_ref, rows_ref, wu_ref, wd_ref, after_ref, *refs):
        x2_ref, ru_ref, y2_ref, h2_ref = refs[-6:-2] if with_loss else refs
        xv = x1_ref[...]
        g = rows_ref[R_G_PRE_MLP:R_G_PRE_MLP + 1, :]
        h2 = ((xv * _rms(xv) * g) * (1.0 + mod_ref[M_SC_F:M_SC_F + 1, :]) + mod_ref[M_SH_F:M_SH_F + 1, :]).astype(BF16)
        h2_ref[...] = h2
        ru = jnp.concatenate([jnp.maximum(_dot(h2, wu_ref[j]), 0.0) for j in range(nb)], axis=1)
        ru_ref[...] = ru.astype(BF16)
        y2 = _dot((ru * ru).astype(BF16), wd_ref[...])
        y2_ref[...] = y2
        x2 = xv + mod_ref[M_GT_F:M_GT_F + 1, :] * ((y2 * _rms(y2)) * rows_ref[R_G_POST_MLP:R_G_POST_MLP + 1, :])
        x2_ref[...] = x2
        if with_loss:
            t_ref, dy_ref, loss_ref = refs[0], refs[-2], refs[-1]

            @pl.when(pl.program_id(0) == 0)
            def _():
                loss_ref[...] = jnp.zeros(loss_ref.shape, F32)

            e = x2 - t_ref[...]
            dy_ref[...] = e * (1.0 / d)
            loss_ref[...] += 0.5 * jnp.sum(jnp.mean(e * e, axis=-1, keepdims=True), axis=0, keepdims=True)

    tile = pl.BlockSpec((tm, d), lambda i: (i, 0))
    wide = pl.BlockSpec((tm, dff), lambda i: (i, 0))
    out_specs = (tile, wide, tile, tile)
    out_shape = (jax.ShapeDtypeStruct((s, d), F32), jax.ShapeDtypeStruct((s, dff), BF16),
                 jax.ShapeDtypeStruct((s, d), F32), jax.ShapeDtypeStruct((s, d), BF16))
    if with_loss:
        out_specs += (tile, pl.BlockSpec((SUBLANES, 128), lambda i: (0, 0)))
        out_shape += (jax.ShapeDtypeStruct((s, d), F32), jax.ShapeDtypeStruct((SUBLANES, 128), F32))
    return pl.pallas_call(
        body,
        name="mlp_fwd",
        grid=(s // tm,),
        in_specs=[
            tile,
            _resident((None, N_MOD, d), lambda i: (l, 0, 0)),
            _resident((None, N_ROWS, d), lambda i: (l, 0, 0)),
            _resident((nb, d, cu), lambda i: (0, 0, 0)),
            _resident((dff, d), lambda i: (0, 0)),
            ANY_SPEC,
        ] + ([tile] if with_loss else []),
        out_specs=out_specs,
        out_shape=out_shape,
        compiler_params=_cparams(("arbitrary",), 2 * _nbytes((dff, d), BF16) + 5 * _nbytes((tm, dff), F32) + 16 * _nbytes((tm, d), F32)),
    )(*_hbm(x1), mod, rows, wup_f, wdn_f, after, *(_hbm(target) if with_loss else ()))


def _mlp_bwd(dx2, x1, y2, ru, mod, rows, wup_f, wdn_f, l, tm):
    s, d = x1.shape
    nb, _, cu = wup_f.shape
    dff = nb * cu

    def body(dx2_ref, x1_ref, y2_ref, ru_ref, mod_ref, rows_ref, wu_ref, wd_ref, dx1_ref, dy2_ref, dup_ref, act_ref, sm_ref):
        @pl.when(pl.program_id(0) == 0)
        def _():
            sm_ref[...] = jnp.zeros(sm_ref.shape, F32)

        dout = dx2_ref[...]
        dy2, dgt, dgpost = _postnorm_bwd(y2_ref[...], dout, rows_ref[R_G_POST_MLP:R_G_POST_MLP + 1, :], mod_ref[M_GT_F:M_GT_F + 1, :])
        dy2b = dy2.astype(BF16)
        dy2_ref[...] = dy2b
        ruv = ru_ref[...].astype(F32)
        act_ref[...] = (ruv * ruv).astype(BF16)
        dup = (_dot_nt(dy2b, wd_ref[...]) * (2.0 * ruv)).astype(BF16)
        dup_ref[...] = dup
        dh2 = _dot_nt(dup[:, 0:cu], wu_ref[0])
        for j in range(1, nb):
            dh2 = dh2 + _dot_nt(dup[:, j * cu:(j + 1) * cu], wu_ref[j])
        dxn, dsc, dsh, dgpre = _prenorm_bwd(x1_ref[...], dh2, rows_ref[R_G_PRE_MLP:R_G_PRE_MLP + 1, :], mod_ref[M_SC_F:M_SC_F + 1, :])
        dx1_ref[...] = dout + dxn
        for r, v in ((G_MLP_GT, dgt), (G_MLP_GPOST, dgpost), (G_MLP_SC, dsc), (G_MLP_SH, dsh), (G_MLP_GPRE, dgpre)):
            sm_ref[r:r + 1, :] += v

    tile = pl.BlockSpec((tm, d), lambda i: (i, 0))
    wide = pl.BlockSpec((tm, dff), lambda i: (i, 0))
    return pl.pallas_call(
        body,
        name="mlp_bwd",
        grid=(s // tm,),
        in_specs=[
            tile, tile, tile, wide,
            _resident((None, N_MOD, d), lambda i: (l, 0, 0)),
            _resident((None, N_ROWS, d), lambda i: (l, 0, 0)),
            _resident((nb, d, cu), lambda i: (0, 0, 0)),
            _resident((dff, d), lambda i: (0, 0)),
        ],
        out_specs=(tile, tile, wide, wide, pl.BlockSpec((G_MLP_ROWS, d), lambda i: (0, 0))),
        out_shape=(jax.ShapeDtypeStruct((s, d), F32), jax.ShapeDtypeStruct((s, d), BF16), jax.ShapeDtypeStruct((s, dff), BF16),
                   jax.ShapeDtypeStruct((s, dff), BF16), jax.ShapeDtypeStruct((G_MLP_ROWS, d), F32)),
        compiler_params=_cparams(("arbitrary",), 2 * _nbytes((dff, d), BF16) + 6 * _nbytes((tm, dff), F32) + 16 * _nbytes((tm, d), F32)),
    )(*_hbm(dx2, x1, y2, ru), mod, rows, wup_f, wdn_f)


def _mixer_core_bwd(dx1, y, yap, ybp, hs, proj, mod, rows, cwf, wr, wi, wa_f, wb_f, wo_f, l, tm, after):
    s, d = dx1.shape
    nh, bw, _ = wr.shape[1:]
    nt = s // tm
    per = tm // SUBLANES

    def body(dx1_ref, y_ref, yap_ref, ybp_ref, hs_ref, hsh_ref, proj_ref, projh_ref, mod_ref, rows_ref, cw_ref,
             wr_ref, wi_ref, wa_ref, wb_ref, wo_ref, after_ref,
             dproj_ref, dy_ref, m_ref, dyap_ref, dybp_ref, ya_ref, yb_ref, sm_ref, dwg_ref,
             abuf, dcbuf, dxbuf, al_s, dh_s, lam_s, lnext):
        i = pl.program_id(0)
        first_tile = i == nt - 1

        @pl.when(i == 0)
        def _():
            sm_ref[...] = jnp.zeros(sm_ref.shape, F32)
            dwg_ref[...] = jnp.zeros(dwg_ref.shape, F32)
            zero = jnp.zeros((SUBLANES, d), F32)
            abuf[...] = zero
            dcbuf[...] = zero
            dxbuf[...] = zero
            lnext[...] = zero

        def row(r):
            return rows_ref[r:r + 1, :]

        def tap(r):
            return cw_ref[r:r + 1, :]

        def acc(r, v):
            sm_ref[r:r + 1, :] += v

        keep_halo = jnp.where(first_tile, 0.0, 1.0)
        dy, dgt, dgpost = _postnorm_bwd(y_ref[...], dx1_ref[...], row(R_G_POST_MIX), mod_ref[M_GT_M:M_GT_M + 1, :])
        acc(G_MIX_GT, dgt)
        acc(G_MIX_GPOST, dgpost)
        dyb16 = dy.astype(BF16)
        dy_ref[...] = dyb16
        dm = _dot_nt(dyb16, wo_ref[...])
        sa = _sigmoid(proj_ref[:, 5 * d:6 * d])
        sb = _sigmoid(proj_ref[:, 6 * d:7 * d])
        yap = yap_ref[...]
        ybp = ybp_ref[...]
        m_ref[...] = (sa * yap + sb * ybp).astype(BF16)
        dyap_f = dm * sa
        dybp_f = dm * sb
        dyap = dyap_f.astype(BF16)
        dybp = dybp_f.astype(BF16)
        dyap_ref[...] = dyap
        dybp_ref[...] = dybp
        dproj_ref[:, 5 * d:6 * d] = (dyap_f * yap * (1.0 - sa)).astype(BF16)
        dproj_ref[:, 6 * d:7 * d] = (dybp_f * ybp * (1.0 - sb)).astype(BF16)
        dya = _dot_nt(dyap, wa_ref[...])
        dyb = _dot_nt(dybp, wb_ref[...])
        ba = proj_ref[:, 0:d]
        ca = proj_ref[:, d:2 * d]
        va = proj_ref[:, 2 * d:3 * d]
        cv = ca * va
        cvh = keep_halo * (projh_ref[:, d:2 * d] * projh_ref[:, 2 * d:3 * d])
        cvm2 = _shift_rows(cv, cvh, 2, False)
        cvm1 = _shift_rows(cv, cvh, 1, False)
        conv3 = ((row(R_CONV_A_B) + cvm2 * tap(CW_A)) + cvm1 * tap(CW_A + 1)) + cv * tap(CW_A + 2)
        ya_ref[...] = (ba * conv3).astype(BF16)
        dproj_ref[:, 0:d] = (dya * conv3).astype(BF16)
        dc3 = dya * ba
        acc(G_MIX_CAB, _colsum(dc3))
        acc(G_MIX_CAW, _colsum(dc3 * cvm2))
        acc(G_MIX_CAW + 1, _colsum(dc3 * cvm1))
        acc(G_MIX_CAW + 2, _colsum(dc3 * cv))
        dct = dcbuf[...]
        dcv = (dc3 * tap(CW_A + 2) + _shift_rows(dc3, dct, 1, True) * tap(CW_A + 1)) + _shift_rows(dc3, dct, 2, True) * tap(CW_A)
        dcbuf[...] = dc3[:SUBLANES, :]
        dproj_ref[:, d:2 * d] = (dcv * va).astype(BF16)
        dproj_ref[:, 2 * d:3 * d] = (dcv * ca).astype(BF16)
        xb = proj_ref[:, 3 * d:4 * d]
        gb = proj_ref[:, 4 * d:5 * d]
        xbh = keep_halo * projh_ref[:, 3 * d:4 * d]
        xm3 = _shift_rows(xb, xbh, 3, False)
        xm2 = _shift_rows(xb, xbh, 2, False)
        xm1 = _shift_rows(xb, xbh, 1, False)
        xc = (((row(R_CONV_B_B) + xm3 * tap(CW_B)) + xm2 * tap(CW_B + 1)) + xm1 * tap(CW_B + 2)) + xb * tap(CW_B + 3)
        lam = row(R_LAMBDA)
        sp = _softplus_neg(lam)
        xcb, r, ig, a, mult = _gates(xc, wr_ref, wi_ref, row(R_B_GATE_R), row(R_B_GATE_I), sp, nh, bw)
        gel, gsig = _gelu(gb)
        hs = hs_ref[...]
        yb_ref[...] = (hs * gel).astype(BF16)
        dproj_ref[:, 4 * d:5 * d] = (dyb * hs * _gelu_grad(gb, gsig)).astype(BF16)
        alpha = _shift_rows(a, abuf[...], 1, True)
        abuf[...] = a[:SUBLANES, :]
        dhs = dyb * gel
        for h in range(nh):
            al_s[h] = alpha[:, h * bw:(h + 1) * bw]
            dh_s[h] = dhs[:, h * bw:(h + 1) * bw]
        groups = tm // SCAN_ROWS

        def group(j, ln):
            base = pl.multiple_of((groups - 1 - j) * SCAN_ROWS, SCAN_ROWS)
            return jnp.concatenate([_scan_group(al_s, dh_s, lam_s, h, base, ln[:, h * bw:(h + 1) * bw], True) for h in range(nh)], axis=1)

        lnext[...] = lax.fori_loop(0, groups, group, lnext[...])
        dbb = jnp.concatenate([lam_s[h] for h in range(nh)], axis=1)
        da = dbb * _shift_rows(hs, keep_halo * hsh_ref[...], 1, False)
        dbx = dbb * xc
        dmult = dbx * ig
        dig = dbx * mult
        dxc = (dbb * mult) * ig
        dla = a * (da - (dmult * a) / mult)
        dlar = dla * r
        acc(G_MIX_LAM, _colsum(dlar) * (LRU_C * _sigmoid(-lam)))
        dzr = (dlar * (1.0 - r)) * (-LRU_C * sp)
        dzi = dig * ig * (1.0 - ig)
        acc(G_MIX_BR, _colsum(dzr))
        acc(G_MIX_BI, _colsum(dzi))
        dzrb = dzr.astype(BF16)
        dzib = dzi.astype(BF16)
        back = []
        for h in range(nh):
            sl = slice(h * bw, (h + 1) * bw)
            back.append(_dot_nt(dzrb[:, sl], wr_ref[h]) + _dot_nt(dzib[:, sl], wi_ref[h]))
            dwg_ref[0, h] += _dot_tn(xcb[:, sl], dzrb[:, sl])
            dwg_ref[1, h] += _dot_tn(xcb[:, sl], dzib[:, sl])
        dxc = dxc + jnp.concatenate(back, axis=1)
        acc(G_MIX_CBB, _colsum(dxc))
        acc(G_MIX_CBW, _colsum(dxc * xm3))
        acc(G_MIX_CBW + 1, _colsum(dxc * xm2))
        acc(G_MIX_CBW + 2, _colsum(dxc * xm1))
        acc(G_MIX_CBW + 3, _colsum(dxc * xb))
        dxt = dxbuf[...]
        dxb = (((dxc * tap(CW_B + 3) + _shift_rows(dxc, dxt, 1, True) * tap(CW_B + 2)) + _shift_rows(dxc, dxt, 2, True) * tap(CW_B + 1))
               + _shift_rows(dxc, dxt, 3, True) * tap(CW_B))
        dxbuf[...] = dxc[:SUBLANES, :]
        dproj_ref[:, 3 * d:4 * d] = dxb.astype(BF16)

    def rev(i):
        return (nt - 1 - i, 0)

    def halo(i):
        return (jnp.maximum((nt - 1 - i) * per - 1, 0), 0)

    tile = pl.BlockSpec((tm, d), rev)
    return pl.pallas_call(
        body,
        name="mixer_core_bwd",
        grid=(nt,),
        in_specs=[
            tile, tile, tile, tile, tile,
            pl.BlockSpec((SUBLANES, d), halo),
            pl.BlockSpec((tm, 7 * d), rev),
            pl.BlockSpec((SUBLANES, 7 * d), halo),
            _resident((None, N_MOD, d), lambda i: (l, 0, 0)),
            _resident((None, N_ROWS, d), lambda i: (l, 0, 0)),
            _resident((None, CW_ROWS, d), lambda i: (l, 0, 0)),
            _resident((None, nh, bw, bw), lambda i: (l, 0, 0, 0)),
            _resident((None, nh, bw, bw), lambda i: (l, 0, 0, 0)),
            _resident((d, d), lambda i: (0, 0)),
            _resident((d, d), lambda i: (0, 0)),
            _resident((d, d), lambda i: (0, 0)),
            ANY_SPEC,
        ],
        out_specs=(pl.BlockSpec((tm, 7 * d), rev),) + (tile,) * 6 + (
            pl.BlockSpec((G_MIX_ROWS, d), lambda i: (0, 0)), pl.BlockSpec((2, nh, bw, bw), lambda i: (0, 0, 0, 0))),
        out_shape=(jax.ShapeDtypeStruct((s, 7 * d), BF16),) + (jax.ShapeDtypeStruct((s, d), BF16),) * 6 + (
            jax.ShapeDtypeStruct((G_MIX_ROWS, d), F32), jax.ShapeDtypeStruct((2, nh, bw, bw), F32)),
        scratch_shapes=[pltpu.VMEM((SUBLANES, d), F32)] * 3 + [pltpu.VMEM((nh, tm, bw), F32)] * 3 + [pltpu.VMEM((SUBLANES, d), F32)],
        compiler_params=_cparams(("arbitrary",), 3 * _nbytes((d, d), BF16) + 3 * _nbytes((tm, 7 * d), F32) + 64 * _nbytes((tm, d), F32)),
    )(*_hbm(dx1, y, yap, ybp, hs, hs, proj, proj), mod, rows, cwf, wr, wi, wa_f, wb_f, wo_f, after)


def _in_proj_bwd(dproj, x, dx1, mod, rows, win_f, l, tm, after):
    s, d = x.shape
    nb, _, ci = win_f.shape

    def body(dp_ref, x_ref, dx1_ref, mod_ref, rows_ref, w_ref, after_ref, dx_ref, sm_ref):
        @pl.when(pl.program_id(0) == 0)
        def _():
            sm_ref[...] = jnp.zeros(sm_ref.shape, F32)

        dh = _dot_nt(dp_ref[:, 0:ci], w_ref[0])
        for j in range(1, nb):
            dh = dh + _dot_nt(dp_ref[:, j * ci:(j + 1) * ci], w_ref[j])
        dxn, dsc, dsh, dg = _prenorm_bwd(x_ref[...], dh, rows_ref[R_G_PRE_MIX:R_G_PRE_MIX + 1, :], mod_ref[M_SC_M:M_SC_M + 1, :])
        dx_ref[...] = dx1_ref[...] + dxn
        for r, v in ((G_IN_SC, dsc), (G_IN_SH, dsh), (G_IN_GPRE, dg)):
            sm_ref[r:r + 1, :] += v

    tile = pl.BlockSpec((tm, d), lambda i: (i, 0))
    return pl.pallas_call(
        body,
        name="in_proj_bwd",
        grid=(s // tm,),
        in_specs=[
            pl.BlockSpec((tm, nb * ci), lambda i: (i, 0)), tile, tile,
            _resident((None, N_MOD, d), lambda i: (l, 0, 0)),
            _resident((None, N_ROWS, d), lambda i: (l, 0, 0)),
            _resident((nb, d, ci), lambda i: (0, 0, 0)),
            ANY_SPEC,
        ],
        out_specs=(tile, pl.BlockSpec((G_IN_ROWS, d), lambda i: (0, 0))),
        out_shape=(jax.ShapeDtypeStruct((s, d), F32), jax.ShapeDtypeStruct((G_IN_ROWS, d), F32)),
        compiler_params=_cparams(("arbitrary",), _nbytes((nb, d, ci), BF16) + 2 * _nbytes((tm, nb * ci), BF16) + 16 * _nbytes((tm, d), F32)),
    )(*_hbm(dproj, x, dx1), mod, rows, win_f, after)


def _wgrad(a, b, cols_owned, ts, after):
    s, k1 = a.shape
    k2 = b.shape[1]
    ns = s // ts
    if cols_owned:
        nblk, bk1, bk2 = N_DEV, k1, k2 // N_DEV
        a_spec = pl.BlockSpec((ts, bk1), lambda j, t: (t, 0))
        b_spec = pl.BlockSpec((ts, bk2), lambda j, t: (t, j))
    else:
        bk1, bk2 = min(WGRAD_ROWS, k1), k2
        nblk = k1 // bk1
        a_spec = pl.BlockSpec((ts, bk1), lambda j, t: (t, j))
        b_spec = pl.BlockSpec((ts, bk2), lambda j, t: (t, 0))

    def body(a_ref, b_ref, after_ref, o_ref, acc_ref):
        t = pl.program_id(1)

        @pl.when(t == 0)
        def _():
            acc_ref[...] = jnp.zeros(acc_ref.shape, F32)

        acc_ref[...] += _dot_tn(a_ref[...], b_ref[...])

        @pl.when(t == ns - 1)
        def _():
            o_ref[...] = acc_ref[...].astype(BF16)

    out = pl.pallas_call(
        body,
        name="wgrad",
        grid=(nblk, ns),
        in_specs=[a_spec, b_spec, ANY_SPEC],
        out_specs=pl.BlockSpec((None, bk1, bk2), lambda j, t: (j, 0, 0)),
        out_shape=pltpu.HBM((nblk, bk1, bk2), BF16),
        scratch_shapes=[pltpu.VMEM((bk1, bk2), F32)],
        compiler_params=_cparams(("parallel", "arbitrary"), 4 * _nbytes((bk1, bk2), F32) + 4 * _nbytes((ts, bk1 + bk2), BF16)),
    )(pltpu.with_memory_space_constraint(a, pltpu.HBM), pltpu.with_memory_space_constraint(b, pltpu.HBM), after)
    return out if cols_owned else out.reshape(N_DEV, k1 // N_DEV, k2)


def _adam_update(w, g, m, v):
    m = ADAM_B1 * m + (1.0 - ADAM_B1) * g
    v = ADAM_B2 * v + (1.0 - ADAM_B2) * (g * g)
    m_hat = m / (1.0 - ADAM_B1 ** ADAM_STEP)
    v_hat = v / (1.0 - ADAM_B2 ** ADAM_STEP)
    delta = -ADAM_LR * (m_hat / (jnp.sqrt(v_hat) + ADAM_EPS) + ADAM_WD * w)
    return delta, m, v


def _sum_adamw(recv, w, m, v, tr, after):
    nl, ra, cb = w.shape
    assert nl == len(recv) == 2

    def body(r0_ref, r1_ref, w_ref, m_ref, v_ref, after_ref, g_ref, d_ref, nm_ref, nv_ref):
        def total(r_ref):
            g = r_ref[0].astype(F32)
            for j in range(1, N_DEV):
                g = g + r_ref[j].astype(F32)
            return g

        g = jnp.where(pl.program_id(0) == 0, total(r0_ref), total(r1_ref))
        g_ref[...] = g
        d_ref[...], nm_ref[...], nv_ref[...] = _adam_update(w_ref[...], g, m_ref[...], v_ref[...])

    blk = pl.BlockSpec((None, tr, cb), lambda l, i: (l, i, 0))
    return pl.pallas_call(
        body,
        name="sum_adamw",
        grid=(nl, ra // tr),
        in_specs=[pl.BlockSpec((N_DEV, tr, cb), lambda l, i: (0, i * (1 - l), 0)),
                  pl.BlockSpec((N_DEV, tr, cb), lambda l, i: (0, i * l, 0)), blk, blk, blk, ANY_SPEC],
        out_specs=(blk,) * 4,
        out_shape=(jax.ShapeDtypeStruct((nl, ra, cb), F32),) * 4,
        compiler_params=_cparams(("arbitrary", "arbitrary"), 6 * _nbytes((N_DEV, tr, cb), BF16) + 32 * _nbytes((tr, cb), F32)),
    )(recv[0], recv[1], w, m, v, after)


def _adamw(w, g, m, v):
    def body(w_ref, g_ref, m_ref, v_ref, d_ref, nm_ref, nv_ref):
        d_ref[...], nm_ref[...], nv_ref[...] = _adam_update(w_ref[...], g_ref[...], m_ref[...], v_ref[...])

    return pl.pallas_call(
        body,
        name="adamw",
        in_specs=[VMEM_SPEC] * 4,
        out_specs=(VMEM_SPEC,) * 3,
        out_shape=(jax.ShapeDtypeStruct(w.shape, F32),) * 3,
        compiler_params=_cparams(vmem=10 * _nbytes(w.shape, F32)),
    )(w, g, m, v)


def _adamw_tiled(w, g, m, v, tr):
    nl, ra, cb = w.shape

    def body(w_ref, g_ref, m_ref, v_ref, d_ref, nm_ref, nv_ref):
        d_ref[...], nm_ref[...], nv_ref[...] = _adam_update(w_ref[...], g_ref[...], m_ref[...], v_ref[...])

    blk = pl.BlockSpec((None, tr, cb), lambda l, i: (l, i, 0))
    return pl.pallas_call(
        body,
        name="adamw_tiled",
        grid=(nl, ra // tr),
        in_specs=[blk] * 4,
        out_specs=(blk,) * 3,
        out_shape=(jax.ShapeDtypeStruct((nl, ra, cb), F32),) * 3,
        compiler_params=_cparams(("parallel", "parallel")),
    )(*_hbm(w, g, m, v))


def _token_tile(s):
    return min(TOKEN_TILE, max(SUBLANES * 2, s // 4))


def kernel(x, c, w_mod, b_mod, g_pre_mix, g_post_mix, w_in, conv_a_w, conv_a_b, w_a_out, conv_b_w, conv_b_b, w_gate_r, b_gate_r, w_gate_i, b_gate_i, lru_lambda, w_b_out, w_o, g_pre_mlp, g_post_mlp, w_mlp_up, w_mlp_down, loss_target, m_w_mod, m_b_mod, m_g_pre_mix, m_g_post_mix, m_w_in, m_conv_a_w, m_conv_a_b, m_w_a_out, m_conv_b_w, m_conv_b_b, m_w_gate_r, m_b_gate_r, m_w_gate_i, m_b_gate_i, m_lru_lambda, m_w_b_out, m_w_o, m_g_pre_mlp, m_g_post_mlp, m_w_mlp_up, m_w_mlp_down, v_w_mod, v_b_mod, v_g_pre_mix, v_g_post_mix, v_w_in, v_conv_a_w, v_conv_a_b, v_w_a_out, v_conv_b_w, v_conv_b_b, v_w_gate_r, v_b_gate_r, v_w_gate_i, v_b_gate_i, v_lru_lambda, v_w_b_out, v_w_o, v_g_pre_mlp, v_g_post_mlp, v_w_mlp_up, v_w_mlp_down):
    nl = w_mod.shape[0]
    s, d = x.shape[1], x.shape[2]
    nh, bw = w_gate_r.shape[1], w_gate_r.shape[2]
    cwid = conv_a_w.shape[2]
    tm = _token_tile(s)
    tmx = min(2 * tm, s)
    ts = s
    _, me = _my_position()
    xs = x.reshape(s, d)
    target = loss_target.reshape(s, d)

    vec_names = (g_pre_mix, g_post_mix, conv_a_b, conv_b_b, b_gate_r, b_gate_i, lru_lambda, g_pre_mlp, g_post_mlp)
    rows = jnp.concatenate([jnp.stack(vec_names, axis=1), jnp.zeros((nl, N_ROWS - len(vec_names), d), F32)], axis=1)
    cw = jnp.concatenate([conv_a_w, conv_b_w, jnp.zeros((nl, CW_ROWS - 7, cwid), F32)], axis=1)

    large = {"w_in": w_in, "w_a_out": w_a_out, "w_b_out": w_b_out, "w_o": w_o, "w_mlp_up": w_mlp_up, "w_mlp_down": w_mlp_down}
    groups = (("in", ("w_in",)), ("mix", ("w_a_out", "w_b_out", "w_o")), ("mlp", ("w_mlp_up", "w_mlp_down")))
    mod, cact, cwf = _prep_small(c, w_mod, b_mod, cw)
    mod = mod.reshape(nl, N_MOD, d)
    gathers = {}
    tok = mod
    for l in range(nl):
        for gname, members in groups:
            gathers[l, gname] = _gather2_start([large[n][l].astype(BF16) for n in members], tok, f"gather_start_{gname}{l}")
            tok = gathers[l, gname][4]
    wr = w_gate_r.astype(BF16)
    wi = w_gate_i.astype(BF16)

    forwarded = {}

    def forward(l, gname, after):
        forwarded[l, gname] = _gather2_forward(gathers[l, gname], after, f"gather_forward_{gname}{l}")
        return forwarded[l, gname][4]

    def gathered(l, gname, after):
        return _gather2_wait(forwarded[l, gname], after, f"gather_wait_{gname}{l}")

    saved = []
    weights = []
    xin = xs
    for l in range(nl):
        if l == 0:
            tok = forward(0, "in", tok)
        (win_f,) = gathered(l, "in", tok if l == 0 else xin)
        if l > 0:
            tok = forward(l, "mix", win_f)
        proj, h = _in_proj_fwd(xin, mod, rows, win_f, l, tmx, tok)
        if l == 0:
            tok = forward(0, "mix", proj)
        wa_f, wb_f, wo_f = (w.reshape(d, d) for w in gathered(l, "mix", proj))
        if l > 0:
            tok = forward(l, "mlp", wo_f)
        x1, hs, yap, ybp, y = _mixer_core_fwd(proj, xin, mod, rows, cwf, wr, wi, wa_f, wb_f, wo_f, l, tm, tok)
        if l == 0:
            forward(0, "mlp", x1)
        wup_f, wdn_f = gathered(l, "mlp", x1)
        wdn_f = wdn_f.reshape(-1, d)
        if l + 1 < nl:
            tok = forward(l + 1, "in", wdn_f)
        x2, ru, y2, h2, *last = _mlp_fwd(x1, mod, rows, wup_f, wdn_f, l, tmx, tok, target if l + 1 == nl else None)
        saved.append((xin, proj, h, x1, hs, yap, ybp, y, ru, y2, h2))
        weights.append((win_f, wa_f, wb_f, wo_f, wup_f, wdn_f))
        xin = x2
    dx, loss_part = last[0], last[1][0, 0]

    scatters = {}
    small = [None] * nl
    gate_parts = [None] * nl

    def scatter(l, gname, parts):
        scatters[l, gname] = _exchange_start(parts, False, f"scatter_start_{gname}{l}")
        return scatters[l, gname][4]

    for l in reversed(range(nl)):
        xin, proj, h, x1, hs, yap, ybp, y, ru, y2, h2 = saved[l]
        win_f, wa_f, wb_f, wo_f, wup_f, wdn_f = weights[l]
        dx1, dy2, dup, act, sm_mlp = _mlp_bwd(dx, x1, y2, ru, mod, rows, wup_f, wdn_f, l, tm)
        g_up = _wgrad(h2, dup, True, ts, dx1)
        g_dn = _wgrad(act, dy2, False, ts, g_up)
        tok = scatter(l, "mlp", [g_up, g_dn])
        dproj, dy, m, dyap, dybp, ya, yb, sm_mix, dwg = _mixer_core_bwd(
            dx1, y, yap, ybp, hs, proj, mod, rows, cwf, wr, wi, wa_f, wb_f, wo_f, l, tm // 2, tok)
        gate_parts[l] = _exchange_start([dwg.astype(BF16)], True, f"gates_start{l}")
        g_a = _wgrad(ya, dyap, False, ts, gate_parts[l][4])
        g_b = _wgrad(yb, dybp, False, ts, g_a)
        g_o = _wgrad(m, dy, False, ts, g_b)
        tok = scatter(l, "mix", [g_a, g_b, g_o])
        tok = scatter(l, "in", [_wgrad(h, dproj, True, ts, tok)])
        dx, sm_in = _in_proj_bwd(dproj, xin, dx1, mod, rows, win_f, l, tm, tok)
        small[l] = jnp.concatenate([sm_mlp, sm_mix, sm_in], axis=0)
    grad_x = dx.reshape(x.shape)

    recv = {}
    big = {}
    moments = {"w_in": (w_in, m_w_in, v_w_in), "w_mlp_up": (w_mlp_up, m_w_mlp_up, v_w_mlp_up), "w_a_out": (w_a_out, m_w_a_out, v_w_a_out),
               "w_b_out": (w_b_out, m_w_b_out, v_w_b_out), "w_o": (w_o, m_w_o, v_w_o), "w_mlp_down": (w_mlp_down, m_w_mlp_down, v_w_mlp_down)}

    def collect(l, gname, after):
        for n, zone in zip(dict(groups)[gname], _exchange_wait(scatters[l, gname], after, False, f"scatter_wait_{gname}{l}")):
            recv[n, l] = zone

    def update(name, after):
        w, m_, v_ = moments[name]
        big[name] = _sum_adamw([recv[name, l] for l in range(nl)], w, m_, v_, min(ADAM_ROWS, w.shape[1]), after)
        return big[name][1]

    for l, gname in ((1, "mlp"), (1, "in"), (1, "mix"), (0, "mlp")):
        collect(l, gname, dx)
    done = update("w_mlp_down", update("w_mlp_up", dx))
    collect(0, "mix", done)
    for name in ("w_a_out", "w_b_out", "w_o"):
        done = update(name, done)
    collect(0, "in", done)
    done = update("w_in", done)

    lrows = jnp.concatenate(small, axis=0)
    lrows = lrows.at[G_LOSS_ROW, 0].set(loss_part)

    def lrow(a, l, r):
        return a[l * G_LAYER_ROWS + r]

    dm = jnp.stack([jnp.concatenate([lrow(lrows, l, G_MLP_ROWS + G_MIX_ROWS + G_IN_SH), lrow(lrows, l, G_MLP_ROWS + G_MIX_ROWS + G_IN_SC),
                                     lrow(lrows, l, G_MLP_ROWS + G_MIX_GT), lrow(lrows, l, G_MLP_SH), lrow(lrows, l, G_MLP_SC),
                                     lrow(lrows, l, G_MLP_GT)]) for l in range(nl)])
    dm8 = jnp.concatenate([dm[:, None, :], jnp.zeros((nl, SUBLANES - 1, N_MOD * d), F32)], axis=1)
    srows, g_w_mod = _reduce_small(lrows, dm8, cact, done)
    loss = srows[G_LOSS_ROW, 0]
    sgates = _sum_gathered([_exchange_wait(gate_parts[l], srows, True, f"gates_wait{l}")[0] for l in range(nl)])

    def srow(l, r):
        return lrow(srows, l, r)

    def per_layer(r):
        return jnp.stack([srow(l, r) for l in range(nl)])

    mix0 = G_MLP_ROWS
    in0 = G_MLP_ROWS + G_MIX_ROWS
    g_b_mod = jnp.stack([jnp.concatenate([srow(l, in0 + G_IN_SH), srow(l, in0 + G_IN_SC), srow(l, mix0 + G_MIX_GT),
                                          srow(l, G_MLP_SH), srow(l, G_MLP_SC), srow(l, G_MLP_GT)]) for l in range(nl)])
    conv_a_full = jnp.stack([jnp.stack([srow(l, mix0 + G_MIX_CAW + k) for k in range(3)]) for l in range(nl)])
    conv_b_full = jnp.stack([jnp.stack([srow(l, mix0 + G_MIX_CBW + k) for k in range(4)]) for l in range(nl)])
    grads = {
        "b_mod": g_b_mod,
        "g_pre_mix": per_layer(in0 + G_IN_GPRE),
        "g_post_mix": per_layer(mix0 + G_MIX_GPOST),
        "conv_a_w": lax.dynamic_slice_in_dim(conv_a_full, me * cwid, cwid, axis=2),
        "conv_a_b": per_layer(mix0 + G_MIX_CAB),
        "conv_b_w": lax.dynamic_slice_in_dim(conv_b_full, me * cwid, cwid, axis=2),
        "conv_b_b": per_layer(mix0 + G_MIX_CBB),
        "w_gate_r": sgates[:, 0],
        "b_gate_r": per_layer(mix0 + G_MIX_BR),
        "w_gate_i": sgates[:, 1],
        "b_gate_i": per_layer(mix0 + G_MIX_BI),
        "lru_lambda": per_layer(mix0 + G_MIX_LAM),
        "g_pre_mlp": per_layer(G_MLP_GPRE),
        "g_post_mlp": per_layer(G_MLP_GPOST),
    }
    params = {
        "b_mod": (b_mod, m_b_mod, v_b_mod), "g_pre_mix": (g_pre_mix, m_g_pre_mix, v_g_pre_mix), "g_post_mix": (g_post_mix, m_g_post_mix, v_g_post_mix),
        "conv_a_w": (conv_a_w, m_conv_a_w, v_conv_a_w), "conv_a_b": (conv_a_b, m_conv_a_b, v_conv_a_b),
        "conv_b_w": (conv_b_w, m_conv_b_w, v_conv_b_w), "conv_b_b": (conv_b_b, m_conv_b_b, v_conv_b_b),
        "w_gate_r": (w_gate_r, m_w_gate_r, v_w_gate_r), "b_gate_r": (b_gate_r, m_b_gate_r, v_b_gate_r),
        "w_gate_i": (w_gate_i, m_w_gate_i, v_w_gate_i), "b_gate_i": (b_gate_i, m_b_gate_i, v_b_gate_i),
        "lru_lambda": (lru_lambda, m_lru_lambda, v_lru_lambda), "g_pre_mlp": (g_pre_mlp, m_g_pre_mlp, v_g_pre_mlp),
        "g_post_mlp": (g_post_mlp, m_g_post_mlp, v_g_post_mlp),
    }
    out = {}
    for name, g in grads.items():
        w, m_, v_ = params[name]
        flat = (-1, w.shape[-1])
        dl, nm, nv = _adamw(w.reshape(flat), g.reshape(flat), m_.reshape(flat), v_.reshape(flat))
        out[name] = (g.reshape(w.shape), dl.reshape(w.shape), nm.reshape(w.shape), nv.reshape(w.shape))
    out["w_mod"] = (g_w_mod,) + tuple(_adamw_tiled(w_mod, g_w_mod, m_w_mod, v_w_mod, min(ADAM_ROWS, d)))
    out.update(big)

    order = ("w_mod", "b_mod", "g_pre_mix", "g_post_mix", "w_in", "conv_a_w", "conv_a_b", "w_a_out", "conv_b_w", "conv_b_b", "w_gate_r", "b_gate_r",
             "w_gate_i", "b_gate_i", "lru_lambda", "w_b_out", "w_o", "g_pre_mlp", "g_post_mlp", "w_mlp_up", "w_mlp_down")
    return (loss, grad_x) + tuple(out[n][0] for n in order) + tuple(out[n][1] for n in order) + tuple(out[n][2] for n in order) + tuple(out[n][3] for n in order)
```

```python
import math

import jax
import jax.numpy as jnp
from jax import lax
from jax.experimental import pallas as pl
from jax.experimental.pallas import tpu as pltpu

F32, BF16 = jnp.float32, jnp.bfloat16
EPS = 1e-6
LRU_C = 8.0
N_DEV = 8
N_MOD = 6
SUBLANES = 8
VMEM_BUDGET = 56 * 1024 * 1024
MXU_COLS = 256
WGRAD_ROWS = 512
TOKEN_TILE = 256
ADAM_ROWS = 256
SCAN_ROWS = SUBLANES * SUBLANES
ADAM_LR, ADAM_B1, ADAM_B2, ADAM_EPS, ADAM_WD, ADAM_STEP = 0.001, 0.9, 0.999, 1e-08, 0.01, 10
MESH = pl.DeviceIdType.MESH
VMEM_SPEC = pl.BlockSpec(memory_space=pltpu.VMEM)
ANY_SPEC = pl.BlockSpec(memory_space=pl.ANY)
HBM_SPEC = pl.BlockSpec(memory_space=pltpu.HBM)
SEM_SPEC = pl.BlockSpec(memory_space=pltpu.SEMAPHORE)
SIDE_EFFECT = pltpu.SideEffectType.DATAFLOW_SIDE_EFFECTING

R_G_PRE_MIX, R_G_POST_MIX, R_CONV_A_B, R_CONV_B_B, R_B_GATE_R, R_B_GATE_I, R_LAMBDA, R_G_PRE_MLP, R_G_POST_MLP = range(9)
N_ROWS = 16
M_SH_M, M_SC_M, M_GT_M, M_SH_F, M_SC_F, M_GT_F = range(6)
CW_A, CW_B, CW_ROWS = 0, 3, 8
G_MLP_GT, G_MLP_GPOST, G_MLP_SC, G_MLP_SH, G_MLP_GPRE, G_LOSS_ROW, G_MLP_ROWS = 0, 1, 2, 3, 4, 7, 8
(G_MIX_GT, G_MIX_GPOST, G_MIX_CAB, G_MIX_CAW, G_MIX_CBB, G_MIX_CBW, G_MIX_BR, G_MIX_BI, G_MIX_LAM) = 0, 1, 2, 3, 6, 7, 11, 12, 13
G_MIX_ROWS = 16
G_IN_SC, G_IN_SH, G_IN_GPRE, G_IN_ROWS = 0, 1, 2, 8
G_LAYER_ROWS = G_MLP_ROWS + G_MIX_ROWS + G_IN_ROWS


def _cparams(dims=None, vmem=None):
    kw = {}
    if dims is not None:
        kw["dimension_semantics"] = dims
    if vmem is not None:
        kw["vmem_limit_bytes"] = int(min(max(vmem, 16 * 1024 * 1024), VMEM_BUDGET))
    return pltpu.CompilerParams(**kw)


def _nbytes(shape, dtype):
    n = 1
    for s in shape:
        n *= s
    return n * jnp.dtype(dtype).itemsize


def _hbm(*arrays):
    return tuple(pltpu.with_memory_space_constraint(a, pltpu.HBM) for a in arrays)


def _resident(block, index_map):
    return pl.BlockSpec(block, index_map, pipeline_mode=pl.Buffered(1))


def _my_position():
    x, y, c = lax.axis_index("x"), lax.axis_index("y"), lax.axis_index("c")
    return (x, y, c), 4 * x + 2 * y + c


def _peer(pos, k):
    x, y, c = pos
    px = 1 - x if k & 4 else x
    py = 1 - y if k & 2 else y
    pc = 1 - c if k & 1 else c
    return (px, py, pc), 4 * px + 2 * py + pc


def _remote(src, dst, ssem, rsem, peer):
    return pltpu.make_async_remote_copy(src_ref=src, dst_ref=dst, send_sem=ssem, recv_sem=rsem, device_id=peer, device_id_type=MESH)


def _dot(a, b):
    return jnp.dot(a, b, preferred_element_type=F32)


def _dot_nt(a, b):
    return lax.dot_general(a, b, (((1,), (1,)), ((), ())), preferred_element_type=F32)


def _dot_tn(a, b):
    return lax.dot_general(a, b, (((0,), (0,)), ((), ())), preferred_element_type=F32)


def _colsum(v):
    return jnp.sum(v, axis=0, keepdims=True)


def _sigmoid(v):
    return jax.nn.sigmoid(v)


GELU_K, GELU_C = 0.7978845608028654, 0.044715


def _gelu(v):
    s = 1.0 / (1.0 + jnp.exp(v * (-2.0 * GELU_K - (2.0 * GELU_K * GELU_C) * (v * v))))
    return v * s, s


def _gelu_grad(v, s):
    return s * (1.0 + (v * (1.0 - s)) * (2.0 * GELU_K + (6.0 * GELU_K * GELU_C) * (v * v)))


def _neg_expm1_twice(v):
    t = jnp.tanh(v)
    return (-2.0 * t) / (1.0 - t)


def _softplus_neg(lam):
    z = -lam
    u = jnp.exp(-jnp.abs(z))
    w = 1.0 + u
    l1p = jnp.where(w == 1.0, u, jnp.log(w) * u / (w - 1.0))
    return jnp.maximum(z, 0.0) + l1p


def _rms(v):
    return lax.rsqrt(jnp.mean(v * v, axis=-1, keepdims=True) + EPS)


def _prenorm_bwd(xv, dh, g, sc):
    r = _rms(xv)
    xn = xv * r
    n = xn * g
    dsc = _colsum(dh * n)
    dsh = _colsum(dh)
    dn = dh * (1.0 + sc)
    dg = _colsum(dn * xn)
    dxn = dn * g
    dx = r * (dxn - xn * jnp.mean(dxn * xn, axis=-1, keepdims=True))
    return dx, dsc, dsh, dg


def _postnorm_bwd(yv, dout, g, gt):
    r = _rms(yv)
    yn = yv * r
    dgt = _colsum(dout * (yn * g))
    dn = dout * gt
    dg = _colsum(dn * yn)
    dyn = dn * g
    dy = r * (dyn - yn * jnp.mean(dyn * yn, axis=-1, keepdims=True))
    return dy, dgt, dg


def _gates(xc, wr_ref, wi_ref, b_r, b_i, sp, nh, bw):
    xcb = xc.astype(BF16)
    zr = jnp.concatenate([_dot(xcb[:, h * bw:(h + 1) * bw], wr_ref[h]) for h in range(nh)], axis=1) + b_r
    zi = jnp.concatenate([_dot(xcb[:, h * bw:(h + 1) * bw], wi_ref[h]) for h in range(nh)], axis=1) + b_i
    r = _sigmoid(zr)
    ig = _sigmoid(zi)
    la = (-LRU_C * r) * sp
    a = jnp.exp(la)
    mult = jnp.sqrt(_neg_expm1_twice(la))
    return xcb, r, ig, a, mult


def _shift_rows(cur, edge, k, up):
    t, dd = cur.shape
    blocks = cur.reshape(t // SUBLANES, SUBLANES, dd)
    row = lax.broadcasted_iota(jnp.int32, (1, SUBLANES, dd), 1)
    if up:
        r = pltpu.roll(blocks, SUBLANES - k, 1)
        nxt = jnp.concatenate([r[1:], pltpu.roll(edge, SUBLANES - k, 0)[None]], axis=0)
        out = jnp.where(row >= SUBLANES - k, nxt, r)
    else:
        r = pltpu.roll(blocks, k, 1)
        prv = jnp.concatenate([pltpu.roll(edge, k, 0)[None], r[:-1]], axis=0)
        out = jnp.where(row < k, prv, r)
    return out.reshape(t, dd)


def _scan_group(a_ref, b_ref, o_ref, h, base, carry, reverse):
    def rows(k):
        return pl.ds(base + k, SUBLANES, stride=SUBLANES)

    order = range(SUBLANES - 1, -1, -1) if reverse else range(SUBLANES)
    loc, prod = {}, {}
    prev = None
    for k in order:
        a, b = a_ref[h, rows(k), :], b_ref[h, rows(k), :]
        loc[k] = b if prev is None else a * loc[prev] + b
        prod[k] = a if prev is None else a * prod[prev]
        prev = k
    pa, pb = _scan_block(prod[prev], loc[prev], reverse)
    ends = pb + pa * carry
    row = lax.broadcasted_iota(jnp.int32, ends.shape, 0)
    if reverse:
        into = jnp.where(row < SUBLANES - 1, pltpu.roll(ends, SUBLANES - 1, 0), carry)
        last = ends[0:1, :]
    else:
        into = jnp.where(row >= 1, pltpu.roll(ends, 1, 0), carry)
        last = ends[SUBLANES - 1:SUBLANES, :]
    for k in range(SUBLANES):
        o_ref[h, rows(k), :] = loc[k] + prod[k] * into
    return jnp.broadcast_to(last, ends.shape)


def _scan_block(a8, b8, reverse):
    row = lax.broadcasted_iota(jnp.int32, a8.shape, 0)
    for s in (1, 2, 4):
        if reverse:
            keep = row < SUBLANES - s
            a_sh = pltpu.roll(a8, SUBLANES - s, 0)
            b_sh = pltpu.roll(b8, SUBLANES - s, 0)
        else:
            keep = row >= s
            a_sh = pltpu.roll(a8, s, 0)
            b_sh = pltpu.roll(b8, s, 0)
        b8 = b8 + a8 * jnp.where(keep, b_sh, 0.0)
        a8 = a8 * jnp.where(keep, a_sh, 1.0)
    return a8, b8


def _prep_small(c, w_mod, b_mod, cw):
    d = c.shape[1]
    cm = w_mod.shape[2]
    cwid = cw.shape[2]
    nl = w_mod.shape[0]

    def body(c_ref, wm_ref, bm_ref, cw_ref, mod_ref, cact_ref, cwf_ref, cbuf, pbuf, rbuf, ssem, rsem, lsem):
        pos, me = _my_position()
        me8 = pl.multiple_of(me * SUBLANES, SUBLANES)
        cbuf[pl.ds(me8, SUBLANES), :] = jnp.broadcast_to(c_ref[...], (SUBLANES, d))
        own_cw = pltpu.make_async_copy(cw_ref, cwf_ref.at[:, :, pl.ds(me * cwid, cwid)], lsem.at[0])
        own_cw.start()
        first = []
        for k in range(1, N_DEV):
            peer, _ = _peer(pos, k)
            rows = cbuf.at[pl.ds(me8, SUBLANES), :]
            first.append(_remote(rows, rows, ssem.at[0, k - 1], rsem.at[0, k - 1], peer))
            first.append(_remote(cw_ref, cwf_ref.at[:, :, pl.ds(me * cwid, cwid)], ssem.at[1, k - 1], rsem.at[1, k - 1], peer))
        for cp in first:
            cp.start()
        for k in range(1, N_DEV):
            peer, pj = _peer(pos, k)
            pj8 = pl.multiple_of(pj * SUBLANES, SUBLANES)
            rows = cbuf.at[pl.ds(pj8, SUBLANES), :]
            _remote(rows, rows, ssem.at[0, k - 1], rsem.at[0, k - 1], peer).wait_recv()
        cv = cbuf[...]
        cact = cv * _sigmoid(cv)
        cact_ref[...] = cact
        cb = cact.astype(BF16)
        for l in range(nl):
            pbuf[l] = _dot(cb, wm_ref[l].astype(BF16))
        own_p = pltpu.make_async_copy(pbuf.at[:, pl.ds(me8, SUBLANES), :], rbuf.at[me], lsem.at[1])
        own_p.start()
        second = []
        for k in range(1, N_DEV):
            peer, pj = _peer(pos, k)
            pj8 = pl.multiple_of(pj * SUBLANES, SUBLANES)
            second.append(_remote(pbuf.at[:, pl.ds(pj8, SUBLANES), :], rbuf.at[me], ssem.at[2, k - 1], rsem.at[2, k - 1], peer))
        for cp in second:
            cp.start()
        for k in range(1, N_DEV):
            peer, pj = _peer(pos, k)
            _remote(pbuf.at[:, pl.ds(0, SUBLANES), :], rbuf.at[pj], ssem.at[2, k - 1], rsem.at[2, k - 1], peer).wait_recv()
            _remote(cw_ref, cwf_ref.at[:, :, pl.ds(pj * cwid, cwid)], ssem.at[1, k - 1], rsem.at[1, k - 1], peer).wait_recv()
        own_p.wait()
        own_cw.wait()
        for l in range(nl):
            for j in range(N_DEV):
                mod_ref[l:l + 1, j * cm:(j + 1) * cm] = rbuf[j, l, 0:1, :] + bm_ref[l:l + 1, j * cm:(j + 1) * cm]
        for cp in first + second:
            cp.wait_send()

    return pl.pallas_call(
        body,
        name="prep_small",
        out_shape=(
            jax.ShapeDtypeStruct((nl, N_MOD * d), F32),
            jax.ShapeDtypeStruct((N_DEV * SUBLANES, d), F32),
            jax.ShapeDtypeStruct((nl, CW_ROWS, d), F32),
        ),
        in_specs=[VMEM_SPEC] * 4,
        out_specs=(VMEM_SPEC,) * 3,
        scratch_shapes=[
            pltpu.VMEM((N_DEV * SUBLANES, d), F32),
            pltpu.VMEM((nl, N_DEV * SUBLANES, cm), F32),
            pltpu.VMEM((N_DEV, nl, SUBLANES, cm), F32),
            pltpu.SemaphoreType.DMA((3, N_DEV - 1)),
            pltpu.SemaphoreType.DMA((3, N_DEV - 1)),
            pltpu.SemaphoreType.DMA((2,)),
        ],
        compiler_params=_cparams(vmem=3 * _nbytes(w_mod.shape, F32)),
    )(c, w_mod, b_mod, cw)


def _exchange_start(parts, gather, name):
    n = len(parts)
    lands = [lax.empty(((N_DEV,) + tuple(p.shape)) if gather else tuple(p.shape), p.dtype) for p in parts]

    def body(*refs):
        ins, lnd = refs[:n], refs[n:2 * n]
        ssem, rsem, token = refs[2 * n], refs[2 * n + 1], refs[-1]
        pos, me = _my_position()
        for k in range(1, N_DEV):
            peer, pj = _peer(pos, k)
            for t in range(n):
                src = ins[t] if gather else ins[t].at[pj]
                q = t * (N_DEV - 1) + k - 1
                _remote(src, lnd[t].at[me], ssem.at[q], rsem.at[q], peer).start()
        token[...] = jnp.zeros(token.shape, F32)

    out = pl.pallas_call(
        body,
        name=name,
        out_shape=(pltpu.SemaphoreType.DMA((n * (N_DEV - 1),)), pltpu.SemaphoreType.DMA((n * (N_DEV - 1),)))
        + tuple(pltpu.HBM(p.shape, p.dtype) for p in parts) + tuple(pltpu.HBM(p.shape, p.dtype) for p in lands)
        + (jax.ShapeDtypeStruct((SUBLANES, 128), F32),),
        in_specs=[HBM_SPEC] * (2 * n),
        out_specs=(SEM_SPEC, SEM_SPEC) + (HBM_SPEC,) * (2 * n) + (VMEM_SPEC,),
        input_output_aliases={i: 2 + i for i in range(2 * n)},
        compiler_params=pltpu.CompilerParams(has_side_effects=SIDE_EFFECT),
    )(*[pltpu.with_memory_space_constraint(p, pltpu.HBM) for p in list(parts) + lands])
    return out[0], out[1], list(out[2:2 + n]), list(out[2 + n:2 + 2 * n]), out[-1]


def _exchange_wait(started, after, gather, name):
    ssem, rsem, parts, lands, _ = started
    n = len(parts)

    def body(*refs):
        ins, lnd = refs[:n], refs[n:2 * n]
        ssem_ref, rsem_ref = refs[2 * n], refs[2 * n + 1]
        stage, lsem = refs[-1 - n:-1], refs[-1]
        pos, me = _my_position()
        load = []
        for t in range(n):
            src = ins[t] if gather else ins[t].at[me]
            load.append(pltpu.make_async_copy(src, stage[t], lsem.at[t]))
            load[-1].start()
        store = []
        for t in range(n):
            load[t].wait()
            store.append(pltpu.make_async_copy(stage[t], lnd[t].at[me], lsem.at[t]))
            store[-1].start()
        for k in range(1, N_DEV):
            peer, pj = _peer(pos, k)
            for t in range(n):
                src = ins[t] if gather else ins[t].at[pj]
                q = t * (N_DEV - 1) + k - 1
                _remote(src, lnd[t].at[me], ssem_ref.at[q], rsem_ref.at[q], peer).wait_send()
                _remote(src, lnd[t].at[pj], ssem_ref.at[q], rsem_ref.at[q], peer).wait_recv()
        for cp in store:
            cp.wait()

    out = pl.pallas_call(
        body,
        name=name,
        out_shape=tuple(pltpu.HBM(p.shape, p.dtype) for p in parts) + tuple(pltpu.HBM(p.shape, p.dtype) for p in lands),
        in_specs=[HBM_SPEC] * (2 * n) + [SEM_SPEC, SEM_SPEC, ANY_SPEC],
        out_specs=(HBM_SPEC,) * (2 * n),
        input_output_aliases={i: i for i in range(2 * n)},
        scratch_shapes=[pltpu.VMEM(tuple(z.shape[1:]), z.dtype) for z in lands] + [pltpu.SemaphoreType.DMA((n,))],
        compiler_params=pltpu.CompilerParams(has_side_effects=SIDE_EFFECT),
    )(*parts, *lands, ssem, rsem, after)
    return list(out[n:])


CHIP_PEERS = 3


def _chip_peer(pos, k, core):
    x, y, _ = pos
    px = 1 - x if k & 2 else x
    py = 1 - y if k & 1 else y
    return (px, py, core), 4 * px + 2 * py + core


def _gather2_start(parts, after, name):
    n = len(parts)
    lands = [lax.empty((N_DEV,) + tuple(p.shape), p.dtype) for p in parts]
    per = CHIP_PEERS + 1

    def body(*refs):
        ins, lnd = refs[:n], refs[n:2 * n]
        ssem, rsem, token = refs[2 * n + 1], refs[2 * n + 2], refs[-1]
        pos, me = _my_position()
        sibling = (pos[0], pos[1], 1 - pos[2])
        for t in range(n):
            _remote(ins[t], lnd[t].at[me], ssem.at[per * t], rsem.at[per * t], sibling).start()
        for k in range(1, per):
            peer, _ = _chip_peer(pos, k, pos[2])
            for t in range(n):
                _remote(ins[t], lnd[t].at[me], ssem.at[per * t + k], rsem.at[per * t + k], peer).start()
        token[...] = jnp.zeros(token.shape, F32)

    out = pl.pallas_call(
        body,
        name=name,
        out_shape=(pltpu.SemaphoreType.DMA((n * per,)), pltpu.SemaphoreType.DMA((n * per,)))
        + tuple(pltpu.HBM(p.shape, p.dtype) for p in parts) + tuple(pltpu.HBM(p.shape, p.dtype) for p in lands)
        + (jax.ShapeDtypeStruct((SUBLANES, 128), F32),),
        in_specs=[HBM_SPEC] * (2 * n) + [ANY_SPEC],
        out_specs=(SEM_SPEC, SEM_SPEC) + (HBM_SPEC,) * (2 * n) + (VMEM_SPEC,),
        input_output_aliases={i: 2 + i for i in range(2 * n)},
        compiler_params=pltpu.CompilerParams(has_side_effects=SIDE_EFFECT),
    )(*[pltpu.with_memory_space_constraint(p, pltpu.HBM) for p in list(parts) + lands], after)
    return out[0], out[1], list(out[2:2 + n]), list(out[2 + n:2 + 2 * n]), out[-1]


def _gather2_forward(started, after, name):
    ssem, rsem, parts, lands, _ = started
    n = len(lands)
    per = CHIP_PEERS + 1

    def body(*refs):
        lnd = refs[:n]
        rsem_a, fsend, frecv, token = refs[n], refs[2 * n + 2], refs[2 * n + 3], refs[2 * n + 4]
        pos, _ = _my_position()
        sibling = (pos[0], pos[1], 1 - pos[2])
        for k in range(1, per):
            peer, pk = _chip_peer(pos, k, pos[2])
            for t in range(n):
                block = lnd[t].at[pk]
                _remote(block, block, rsem_a.at[per * t + k], rsem_a.at[per * t + k], peer).wait_recv()
                q = CHIP_PEERS * t + k - 1
                _remote(block, block, fsend.at[q], frecv.at[q], sibling).start()
        token[...] = jnp.zeros(token.shape, F32)

    out = pl.pallas_call(
        body,
        name=name,
        out_shape=tuple(pltpu.HBM(p.shape, p.dtype) for p in lands)
        + (pltpu.SemaphoreType.DMA((n * CHIP_PEERS,)), pltpu.SemaphoreType.DMA((n * CHIP_PEERS,)),
           jax.ShapeDtypeStruct((SUBLANES, 128), F32)),
        in_specs=[HBM_SPEC] * n + [SEM_SPEC, ANY_SPEC],
        out_specs=(HBM_SPEC,) * n + (SEM_SPEC, SEM_SPEC, VMEM_SPEC),
        input_output_aliases={i: i for i in range(n)},
        compiler_params=pltpu.CompilerParams(has_side_effects=SIDE_EFFECT),
    )(*lands, rsem, after)
    return ssem, rsem, parts, list(out[:n]), out[n + 2], out[n], out[n + 1]


def _gather2_wait(forwarded, after, name):
    ssem, rsem, parts, lands, _, fsend, frecv = forwarded
    n = len(parts)
    per = CHIP_PEERS + 1

    def body(*refs):
        ins, lnd = refs[:n], refs[n:2 * n]
        ssem_a, rsem_a, fs, fr = refs[2 * n:2 * n + 4]
        stage, lsem = refs[-1 - n:-1], refs[-1]
        pos, me = _my_position()
        sibling = (pos[0], pos[1], 1 - pos[2])
        sib = 4 * pos[0] + 2 * pos[1] + 1 - pos[2]
        load = []
        for t in range(n):
            load.append(pltpu.make_async_copy(ins[t], stage[t], lsem.at[t]))
            load[-1].start()
        store = []
        for t in range(n):
            load[t].wait()
            store.append(pltpu.make_async_copy(stage[t], lnd[t].at[me], lsem.at[t]))
            store[-1].start()
        for t in range(n):
            _remote(ins[t], lnd[t].at[me], ssem_a.at[per * t], rsem_a.at[per * t], sibling).wait_send()
            _remote(ins[t], lnd[t].at[sib], ssem_a.at[per * t], rsem_a.at[per * t], sibling).wait_recv()
        for k in range(1, per):
            peer, pk = _chip_peer(pos, k, pos[2])
            _, qk = _chip_peer(pos, k, 1 - pos[2])
            for t in range(n):
                q = CHIP_PEERS * t + k - 1
                _remote(ins[t], lnd[t].at[me], ssem_a.at[per * t + k], rsem_a.at[per * t + k], peer).wait_send()
                _remote(lnd[t].at[pk], lnd[t].at[pk], fs.at[q], fr.at[q], sibling).wait_send()
                _remote(lnd[t].at[qk], lnd[t].at[qk], fs.at[q], fr.at[q], sibling).wait_recv()
        for cp in store:
            cp.wait()

    out = pl.pallas_call(
        body,
        name=name,
        out_shape=tuple(pltpu.HBM(p.shape, p.dtype) for p in parts) + tuple(pltpu.HBM(p.shape, p.dtype) for p in lands),
        in_specs=[HBM_SPEC] * (2 * n) + [SEM_SPEC] * 4 + [ANY_SPEC],
        out_specs=(HBM_SPEC,) * (2 * n),
        input_output_aliases={i: i for i in range(2 * n)},
        scratch_shapes=[pltpu.VMEM(tuple(z.shape[1:]), z.dtype) for z in lands] + [pltpu.SemaphoreType.DMA((n,))],
        compiler_params=pltpu.CompilerParams(has_side_effects=SIDE_EFFECT),
    )(*parts, *lands, ssem, rsem, fsend, frecv, after)
    return list(out[n:])


def _reduce_small(rows, dm8, cact, after):
    r, d = rows.shape
    nl = dm8.shape[0]
    cm = dm8.shape[2] // N_DEV
    lw = d // N_DEV

    def body(rows_ref, dm_ref, cact_ref, after_ref, orow_ref, owm_ref, gr, mine, dmr, ssem, rsem, lsem):
        pos, me = _my_position()
        me8 = pl.multiple_of(me * SUBLANES, SUBLANES)

        def lanes(j):
            return pl.ds(j * lw, lw)

        own = [pltpu.make_async_copy(dm_ref.at[:, :, pl.ds(me * cm, cm)], dmr.at[:, pl.ds(me8, SUBLANES), :], lsem.at[0]),
               pltpu.make_async_copy(rows_ref.at[:, lanes(me)], gr.at[me], lsem.at[1])]
        first = []
        for k in range(1, N_DEV):
            peer, pj = _peer(pos, k)
            first.append(_remote(rows_ref.at[:, lanes(pj)], gr.at[me], ssem.at[0, k - 1], rsem.at[0, k - 1], peer))
            first.append(_remote(dm_ref.at[:, :, pl.ds(pj * cm, cm)], dmr.at[:, pl.ds(me8, SUBLANES), :],
                                 ssem.at[1, k - 1], rsem.at[1, k - 1], peer))
        for cp in own + first:
            cp.start()
        for k in range(1, N_DEV):
            peer, pj = _peer(pos, k)
            pj8 = pl.multiple_of(pj * SUBLANES, SUBLANES)
            _remote(rows_ref.at[:, lanes(0)], gr.at[pj], ssem.at[0, k - 1], rsem.at[0, k - 1], peer).wait_recv()
            _remote(dm_ref.at[:, :, pl.ds(0, cm)], dmr.at[:, pl.ds(pj8, SUBLANES), :], ssem.at[1, k - 1], rsem.at[1, k - 1], peer).wait_recv()
        for cp in own:
            cp.wait()
        acc = gr[0]
        for j in range(1, N_DEV):
            acc = acc + gr[j]
        mine[...] = acc
        own_sum = pltpu.make_async_copy(mine, orow_ref.at[:, lanes(me)], lsem.at[1])
        own_sum.start()
        second = []
        for k in range(1, N_DEV):
            peer, _ = _peer(pos, k)
            second.append(_remote(mine, orow_ref.at[:, lanes(me)], ssem.at[2, k - 1], rsem.at[2, k - 1], peer))
            second[-1].start()
        cb = cact_ref[...].astype(BF16)
        for l in range(nl):
            owm_ref[l] = _dot_tn(cb, dmr[l].astype(BF16))
        for k in range(1, N_DEV):
            peer, pj = _peer(pos, k)
            _remote(mine, orow_ref.at[:, lanes(pj)], ssem.at[2, k - 1], rsem.at[2, k - 1], peer).wait_recv()
        own_sum.wait()
        for cp in first + second:
            cp.wait_send()

    return pl.pallas_call(
        body,
        name="reduce_small",
        out_shape=(jax.ShapeDtypeStruct((r, d), F32), jax.ShapeDtypeStruct((nl, d, cm), F32)),
        in_specs=[VMEM_SPEC] * 3 + [ANY_SPEC],
        out_specs=(VMEM_SPEC,) * 2,
        scratch_shapes=[
            pltpu.VMEM((N_DEV, r, lw), F32),
            pltpu.VMEM((r, lw), F32),
            pltpu.VMEM((nl, N_DEV * SUBLANES, cm), F32),
            pltpu.SemaphoreType.DMA((3, N_DEV - 1)),
            pltpu.SemaphoreType.DMA((3, N_DEV - 1)),
            pltpu.SemaphoreType.DMA((2,)),
        ],
        compiler_params=_cparams(vmem=4 * _nbytes((r, d), F32) + 6 * _nbytes((nl, d, cm), F32)),
    )(rows, dm8, cact, after)


def _sum_gathered(zones):
    nl = len(zones)

    def body(*refs):
        for l in range(nl):
            acc = refs[l][0].astype(F32)
            for j in range(1, N_DEV):
                acc = acc + refs[l][j].astype(F32)
            refs[nl][l] = acc

    return pl.pallas_call(
        body,
        name="sum_gathered",
        out_shape=jax.ShapeDtypeStruct((nl,) + tuple(zones[0].shape[1:]), F32),
        in_specs=[VMEM_SPEC] * nl,
        out_specs=VMEM_SPEC,
        compiler_params=_cparams(vmem=8 * nl * _nbytes(zones[0].shape, BF16)),
    )(*zones)


def _load_side_by_side(w_hbm, w_s, sem):
    nb, _, ci = w_hbm.shape
    copies = [pltpu.make_async_copy(w_hbm.at[j], w_s.at[:, j * ci:(j + 1) * ci], sem.at[j]) for j in range(nb)]
    for cp in copies:
        cp.start()
    for cp in copies:
        cp.wait()


def _in_proj_fwd(x, mod, rows, win_f, l, tm, after):
    s, d = x.shape
    nb, _, ci = win_f.shape
    n = nb * ci
    wide = min(n, ci * MXU_COLS // math.gcd(ci, MXU_COLS))

    def body(x_ref, mod_ref, rows_ref, w_hbm, after_ref, proj_ref, h_ref, w_s, sem):
        @pl.when(pl.program_id(0) == 0)
        def _():
            _load_side_by_side(w_hbm, w_s, sem)

        xv = x_ref[...]
        g = rows_ref[R_G_PRE_MIX:R_G_PRE_MIX + 1, :]
        h = (xv * _rms(xv) * g) * (1.0 + mod_ref[M_SC_M:M_SC_M + 1, :]) + mod_ref[M_SH_M:M_SH_M + 1, :]
        hb = h.astype(BF16)
        h_ref[...] = hb
        for c0 in range(0, n, wide):
            proj_ref[:, c0:c0 + wide] = _dot(hb, w_s[:, c0:c0 + wide])

    return pl.pallas_call(
        body,
        name="in_proj_fwd",
        grid=(s // tm,),
        in_specs=[
            pl.BlockSpec((tm, d), lambda i: (i, 0)),
            _resident((None, N_MOD, d), lambda i: (l, 0, 0)),
            _resident((None, N_ROWS, d), lambda i: (l, 0, 0)),
            ANY_SPEC,
            ANY_SPEC,
        ],
        out_specs=(pl.BlockSpec((tm, n), lambda i: (i, 0)), pl.BlockSpec((tm, d), lambda i: (i, 0))),
        out_shape=(jax.ShapeDtypeStruct((s, n), F32), jax.ShapeDtypeStruct((s, d), BF16)),
        scratch_shapes=[pltpu.VMEM((d, n), BF16), pltpu.SemaphoreType.DMA((nb,))],
        compiler_params=_cparams(("arbitrary",), _nbytes((d, n), BF16) + 3 * _nbytes((tm, n), F32) + 8 * _nbytes((tm, d), F32)),
    )(*_hbm(x), mod, rows, win_f, after)


def _mixer_core_fwd(proj, x, mod, rows, cwf, wr, wi, wa_f, wb_f, wo_f, l, tm, after):
    s, d = x.shape
    nh, bw, _ = wr.shape[1:]

    def body(proj_ref, x_ref, mod_ref, rows_ref, cw_ref, wr_ref, wi_ref, wa_ref, wb_ref, wo_ref, after_ref,
             x1_ref, hs_ref, yap_ref, ybp_ref, y_ref, cvbuf, xbbuf, a_s, b_s, h_s, hprev):
        i = pl.program_id(0)

        @pl.when(i == 0)
        def _():
            cvbuf[...] = jnp.zeros((SUBLANES, d), F32)
            xbbuf[...] = jnp.zeros((SUBLANES, d), F32)
            hprev[...] = jnp.zeros((SUBLANES, d), F32)

        def row(r):
            return rows_ref[r:r + 1, :]

        def tap(r):
            return cw_ref[r:r + 1, :]

        ba = proj_ref[:, 0:d]
        cv = proj_ref[:, d:2 * d] * proj_ref[:, 2 * d:3 * d]
        cvt = cvbuf[...]
        conv3 = ((row(R_CONV_A_B) + _shift_rows(cv, cvt, 2, False) * tap(CW_A)) + _shift_rows(cv, cvt, 1, False) * tap(CW_A + 1)) + cv * tap(CW_A + 2)
        ya = ba * conv3
        cvbuf[...] = cv[tm - SUBLANES:, :]
        sp = _softplus_neg(row(R_LAMBDA))
        for h in range(nh):
            lo, hi = h * bw, (h + 1) * bw
            xb = proj_ref[:, 3 * d + lo:3 * d + hi]
            xbt = xbbuf[:, lo:hi]

            def hrow(r):
                return rows_ref[r:r + 1, lo:hi]

            def htap(r):
                return cw_ref[r:r + 1, lo:hi]

            xc = (((hrow(R_CONV_B_B) + _shift_rows(xb, xbt, 3, False) * htap(CW_B)) + _shift_rows(xb, xbt, 2, False) * htap(CW_B + 1))
                  + _shift_rows(xb, xbt, 1, False) * htap(CW_B + 2)) + xb * htap(CW_B + 3)
            xcb = xc.astype(BF16)
            r = _sigmoid(_dot(xcb, wr_ref[h]) + hrow(R_B_GATE_R))
            ig = _sigmoid(_dot(xcb, wi_ref[h]) + hrow(R_B_GATE_I))
            la = (-LRU_C * r) * sp[:, lo:hi]
            a_s[h] = jnp.exp(la)
            b_s[h] = jnp.sqrt(_neg_expm1_twice(la)) * (ig * xc)
        xbbuf[...] = proj_ref[tm - SUBLANES:, 3 * d:4 * d]

        def group(j, hp):
            base = pl.multiple_of(j * SCAN_ROWS, SCAN_ROWS)
            return jnp.concatenate([_scan_group(a_s, b_s, h_s, h, base, hp[:, h * bw:(h + 1) * bw], False) for h in range(nh)], axis=1)

        hprev[...] = lax.fori_loop(0, tm // SCAN_ROWS, group, hprev[...])
        for h in range(nh):
            hs_ref[:, h * bw:(h + 1) * bw] = h_s[h]
        gel, _ = _gelu(proj_ref[:, 4 * d:5 * d])
        yb = hs_ref[...] * gel
        yap = _dot(ya.astype(BF16), wa_ref[...])
        ybp = _dot(yb.astype(BF16), wb_ref[...])
        yap_ref[...] = yap
        ybp_ref[...] = ybp
        m = _sigmoid(proj_ref[:, 5 * d:6 * d]) * yap + _sigmoid(proj_ref[:, 6 * d:7 * d]) * ybp
        y = _dot(m.astype(BF16), wo_ref[...])
        y_ref[...] = y
        x1_ref[...] = x_ref[...] + mod_ref[M_GT_M:M_GT_M + 1, :] * ((y * _rms(y)) * row(R_G_POST_MIX))

    tile = pl.BlockSpec((tm, d), lambda i: (i, 0))
    return pl.pallas_call(
        body,
        name="mixer_core_fwd",
        grid=(s // tm,),
        in_specs=[
            pl.BlockSpec((tm, 7 * d), lambda i: (i, 0)),
            tile,
            _resident((None, N_MOD, d), lambda i: (l, 0, 0)),
            _resident((None, N_ROWS, d), lambda i: (l, 0, 0)),
            _resident((None, CW_ROWS, d), lambda i: (l, 0, 0)),
            _resident((None, nh, bw, bw), lambda i: (l, 0, 0, 0)),
            _resident((None, nh, bw, bw), lambda i: (l, 0, 0, 0)),
            _resident((d, d), lambda i: (0, 0)),
            _resident((d, d), lambda i: (0, 0)),
            _resident((d, d), lambda i: (0, 0)),
            ANY_SPEC,
        ],
        out_specs=(tile,) * 5,
        out_shape=(jax.ShapeDtypeStruct((s, d), F32),) * 5,
        scratch_shapes=[
            pltpu.VMEM((SUBLANES, d), F32),
            pltpu.VMEM((SUBLANES, d), F32),
            pltpu.VMEM((nh, tm, bw), F32),
            pltpu.VMEM((nh, tm, bw), F32),
            pltpu.VMEM((nh, tm, bw), F32),
            pltpu.VMEM((SUBLANES, d), F32),
        ],
        compiler_params=_cparams(("arbitrary",), 3 * _nbytes((d, d), BF16) + 2 * _nbytes((tm, 7 * d), F32) + 40 * _nbytes((tm, d), F32)),
    )(*_hbm(proj, x), mod, rows, cwf, wr, wi, wa_f, wb_f, wo_f, after)


def _mlp_fwd(x1, mod, rows, wup_f, wdn_f, l, tm, after, target=None):
    s, d = x1.shape
    nb, _, cu = wup_f.shape
    dff = nb * cu
    with_loss = target is not None

    def body(x1_ref, mod_ref, rows_ref, wu_ref, wd_ref, after_ref, *refs):
        x2_ref, ru_ref, y2_ref, h2_ref = refs[-6:-2] if with_loss else refs
        xv = x1_ref[...]
        g = rows_ref[R_G_PRE_MLP:R_G_PRE_MLP + 1, :]
        h2 = ((xv * _rms(xv) * g) * (1.0 + mod_ref[M_SC_F:M_SC_F + 1, :]) + mod_ref[M_SH_F:M_SH_F + 1, :]).astype(BF16)
        h2_ref[...] = h2
        ru = jnp.concatenate([jnp.maximum(_dot(h2, wu_ref[j]), 0.0) for j in range(nb)], axis=1)
        ru_ref[...] = ru.astype(BF16)
        y2 = _dot((ru * ru).astype(BF16), wd_ref[...])
        y2_ref[...] = y2
        x2 = xv + mod_ref[M_GT_F:M_GT_F + 1, :] * ((y2 * _rms(y2)) * rows_ref[R_G_POST_MLP:R_G_POST_MLP + 1, :])
        x2_ref[...] = x2
        if with_loss:
            t_ref, dy_ref, loss_ref = refs[0], refs[-2], refs[-1]

            @pl.when(pl.program_id(0) == 0)
            def _():
                loss_ref[...] = jnp.zeros(loss_ref.shape, F32)

            e = x2 - t_ref[...]
            dy_ref[...] = e * (1.0 / d)
            loss_ref[...] += 0.5 * jnp.sum(jnp.mean(e * e, axis=-1, keepdims=True), axis=0, keepdims=True)

    tile = pl.BlockSpec((tm, d), lambda i: (i, 0))
    wide = pl.BlockSpec((tm, dff), lambda i: (i, 0))
    out_specs = (tile, wide, tile, tile)
    out_shape = (jax.ShapeDtypeStruct((s, d), F32), jax.ShapeDtypeStruct((s, dff), BF16),
                 jax.ShapeDtypeStruct((s, d), F32), jax.ShapeDtypeStruct((s, d), BF16))
    if with_loss:
        out_specs += (tile, pl.BlockSpec((SUBLANES, 128), lambda i: (0, 0)))
        out_shape += (jax.ShapeDtypeStruct((s, d), F32), jax.ShapeDtypeStruct((SUBLANES, 128), F32))
    return pl.pallas_call(
        body,
        name="mlp_fwd",
        grid=(s // tm,),
        in_specs=[
            tile,
            _resident((None, N_MOD, d), lambda i: (l, 0, 0)),
            _resident((None, N_ROWS, d), lambda i: (l, 0, 0)),
            _resident((nb, d, cu), lambda i: (0, 0, 0)),
            _resident((dff, d), lambda i: (0, 0)),
            ANY_SPEC,
        ] + ([tile] if with_loss else []),
        out_specs=out_specs,
        out_shape=out_shape,
        compiler_params=_cparams(("arbitrary",), 2 * _nbytes((dff, d), BF16) + 5 * _nbytes((tm, dff), F32) + 16 * _nbytes((tm, d), F32)),
    )(*_hbm(x1), mod, rows, wup_f, wdn_f, after, *(_hbm(target) if with_loss else ()))


def _mlp_bwd(dx2, x1, y2, ru, mod, rows, wup_f, wdn_f, l, tm):
    s, d = x1.shape
    nb, _, cu = wup_f.shape
    dff = nb * cu

    def body(dx2_ref, x1_ref, y2_ref, ru_ref, mod_ref, rows_ref, wu_ref, wd_ref, dx1_ref, dy2_ref, dup_ref, act_ref, sm_ref):
        @pl.when(pl.program_id(0) == 0)
        def _():
            sm_ref[...] = jnp.zeros(sm_ref.shape, F32)

        dout = dx2_ref[...]
        dy2, dgt, dgpost = _postnorm_bwd(y2_ref[...], dout, rows_ref[R_G_POST_MLP:R_G_POST_MLP + 1, :], mod_ref[M_GT_F:M_GT_F + 1, :])
        dy2b = dy2.astype(BF16)
        dy2_ref[...] = dy2b
        ruv = ru_ref[...].astype(F32)
        act_ref[...] = (ruv * ruv).astype(BF16)
        dup = (_dot_nt(dy2b, wd_ref[...]) * (2.0 * ruv)).astype(BF16)
        dup_ref[...] = dup
        dh2 = _dot_nt(dup[:, 0:cu], wu_ref[0])
        for j in range(1, nb):
            dh2 = dh2 + _dot_nt(dup[:, j * cu:(j + 1) * cu], wu_ref[j])
        dxn, dsc, dsh, dgpre = _prenorm_bwd(x1_ref[...], dh2, rows_ref[R_G_PRE_MLP:R_G_PRE_MLP + 1, :], mod_ref[M_SC_F:M_SC_F + 1, :])
        dx1_ref[...] = dout + dxn
        for r, v in ((G_MLP_GT, dgt), (G_MLP_GPOST, dgpost), (G_MLP_SC, dsc), (G_MLP_SH, dsh), (G_MLP_GPRE, dgpre)):
            sm_ref[r:r + 1, :] += v

    tile = pl.BlockSpec((tm, d), lambda i: (i, 0))
    wide = pl.BlockSpec((tm, dff), lambda i: (i, 0))
    return pl.pallas_call(
        body,
        name="mlp_bwd",
        grid=(s // tm,),
        in_specs=[
            tile, tile, tile, wide,
            _resident((None, N_MOD, d), lambda i: (l, 0, 0)),
            _resident((None, N_ROWS, d), lambda i: (l, 0, 0)),
            _resident((nb, d, cu), lambda i: (0, 0, 0)),
            _resident((dff, d), lambda i: (0, 0)),
        ],
        out_specs=(tile, tile, wide, wide, pl.BlockSpec((G_MLP_ROWS, d), lambda i: (0, 0))),
        out_shape=(jax.ShapeDtypeStruct((s, d), F32), jax.ShapeDtypeStruct((s, d), BF16), jax.ShapeDtypeStruct((s, dff), BF16),
                   jax.ShapeDtypeStruct((s, dff), BF16), jax.ShapeDtypeStruct((G_MLP_ROWS, d), F32)),
        compiler_params=_cparams(("arbitrary",), 2 * _nbytes((dff, d), BF16) + 6 * _nbytes((tm, dff), F32) + 16 * _nbytes((tm, d), F32)),
    )(*_hbm(dx2, x1, y2, ru), mod, rows, wup_f, wdn_f)


def _mixer_core_bwd(dx1, y, yap, ybp, hs, proj, mod, rows, cwf, wr, wi, wa_f, wb_f, wo_f, l, tm, after):
    s, d = dx1.shape
    nh, bw, _ = wr.shape[1:]
    nt = s // tm
    per = tm // SUBLANES

    def body(dx1_ref, y_ref, yap_ref, ybp_ref, hs_ref, hsh_ref, proj_ref, projh_ref, mod_ref, rows_ref, cw_ref,
             wr_ref, wi_ref, wa_ref, wb_ref, wo_ref, after_ref,
             dproj_ref, dy_ref, m_ref, dyap_ref, dybp_ref, ya_ref, yb_ref, sm_ref, dwg_ref,
             abuf, dcbuf, dxbuf, al_s, dh_s, lam_s, lnext):
        i = pl.program_id(0)
        first_tile = i == nt - 1

        @pl.when(i == 0)
        def _():
            sm_ref[...] = jnp.zeros(sm_ref.shape, F32)
            dwg_ref[...] = jnp.zeros(dwg_ref.shape, F32)
            zero = jnp.zeros((SUBLANES, d), F32)
            abuf[...] = zero
            dcbuf[...] = zero
            dxbuf[...] = zero
            lnext[...] = zero

        def row(r):
            return rows_ref[r:r + 1, :]

        def tap(r):
            return cw_ref[r:r + 1, :]

        def acc(r, v):
            sm_ref[r:r + 1, :] += v

        keep_halo = jnp.where(first_tile, 0.0, 1.0)
        dy, dgt, dgpost = _postnorm_bwd(y_ref[...], dx1_ref[...], row(R_G_POST_MIX), mod_ref[M_GT_M:M_GT_M + 1, :])
        acc(G_MIX_GT, dgt)
        acc(G_MIX_GPOST, dgpost)
        dyb16 = dy.astype(BF16)
        dy_ref[...] = dyb16
        dm = _dot_nt(dyb16, wo_ref[...])
        sa = _sigmoid(proj_ref[:, 5 * d:6 * d])
        sb = _sigmoid(proj_ref[:, 6 * d:7 * d])
        yap = yap_ref[...]
        ybp = ybp_ref[...]
        m_ref[...] = (sa * yap + sb * ybp).astype(BF16)
        dyap_f = dm * sa
        dybp_f = dm * sb
        dyap = dyap_f.astype(BF16)
        dybp = dybp_f.astype(BF16)
        dyap_ref[...] = dyap
        dybp_ref[...] = dybp
        dproj_ref[:, 5 * d:6 * d] = (dyap_f * yap * (1.0 - sa)).astype(BF16)
        dproj_ref[:, 6 * d:7 * d] = (dybp_f * ybp * (1.0 - sb)).astype(BF16)
        dya = _dot_nt(dyap, wa_ref[...])
        dyb = _dot_nt(dybp, wb_ref[...])
        ba = proj_ref[:, 0:d]
        ca = proj_ref[:, d:2 * d]
        va = proj_ref[:, 2 * d:3 * d]
        cv = ca * va
        cvh = keep_halo * (projh_ref[:, d:2 * d] * projh_ref[:, 2 * d:3 * d])
        cvm2 = _shift_rows(cv, cvh, 2, False)
        cvm1 = _shift_rows(cv, cvh, 1, False)
        conv3 = ((row(R_CONV_A_B) + cvm2 * tap(CW_A)) + cvm1 * tap(CW_A + 1)) + cv * tap(CW_A + 2)
        ya_ref[...] = (ba * conv3).astype(BF16)
        dproj_ref[:, 0:d] = (dya * conv3).astype(BF16)
        dc3 = dya * ba
        acc(G_MIX_CAB, _colsum(dc3))
        acc(G_MIX_CAW, _colsum(dc3 * cvm2))
        acc(G_MIX_CAW + 1, _colsum(dc3 * cvm1))
        acc(G_MIX_CAW + 2, _colsum(dc3 * cv))
        dct = dcbuf[...]
        dcv = (dc3 * tap(CW_A + 2) + _shift_rows(dc3, dct, 1, True) * tap(CW_A + 1)) + _shift_rows(dc3, dct, 2, True) * tap(CW_A)
        dcbuf[...] = dc3[:SUBLANES, :]
        dproj_ref[:, d:2 * d] = (dcv * va).astype(BF16)
        dproj_ref[:, 2 * d:3 * d] = (dcv * ca).astype(BF16)
        xb = proj_ref[:, 3 * d:4 * d]
        gb = proj_ref[:, 4 * d:5 * d]
        xbh = keep_halo * projh_ref[:, 3 * d:4 * d]
        xm3 = _shift_rows(xb, xbh, 3, False)
        xm2 = _shift_rows(xb, xbh, 2, False)
        xm1 = _shift_rows(xb, xbh, 1, False)
        xc = (((row(R_CONV_B_B) + xm3 * tap(CW_B)) + xm2 * tap(CW_B + 1)) + xm1 * tap(CW_B + 2)) + xb * tap(CW_B + 3)
        lam = row(R_LAMBDA)
        sp = _softplus_neg(lam)
        xcb, r, ig, a, mult = _gates(xc, wr_ref, wi_ref, row(R_B_GATE_R), row(R_B_GATE_I), sp, nh, bw)
        gel, gsig = _gelu(gb)
        hs = hs_ref[...]
        yb_ref[...] = (hs * gel).astype(BF16)
        dproj_ref[:, 4 * d:5 * d] = (dyb * hs * _gelu_grad(gb, gsig)).astype(BF16)
        alpha = _shift_rows(a, abuf[...], 1, True)
        abuf[...] = a[:SUBLANES, :]
        dhs = dyb * gel
        for h in range(nh):
            al_s[h] = alpha[:, h * bw:(h + 1) * bw]
            dh_s[h] = dhs[:, h * bw:(h + 1) * bw]
        groups = tm // SCAN_ROWS

        def group(j, ln):
            base = pl.multiple_of((groups - 1 - j) * SCAN_ROWS, SCAN_ROWS)
            return jnp.concatenate([_scan_group(al_s, dh_s, lam_s, h, base, ln[:, h * bw:(h + 1) * bw], True) for h in range(nh)], axis=1)

        lnext[...] = lax.fori_loop(0, groups, group, lnext[...])
        dbb = jnp.concatenate([lam_s[h] for h in range(nh)], axis=1)
        da = dbb * _shift_rows(hs, keep_halo * hsh_ref[...], 1, False)
        dbx = dbb * xc
        dmult = dbx * ig
        dig = dbx * mult
        dxc = (dbb * mult) * ig
        dla = a * (da - (dmult * a) / mult)
        dlar = dla * r
        acc(G_MIX_LAM, _colsum(dlar) * (LRU_C * _sigmoid(-lam)))
        dzr = (dlar * (1.0 - r)) * (-LRU_C * sp)
        dzi = dig * ig * (1.0 - ig)
        acc(G_MIX_BR, _colsum(dzr))
        acc(G_MIX_BI, _colsum(dzi))
        dzrb = dzr.astype(BF16)
        dzib = dzi.astype(BF16)
        back = []
        for h in range(nh):
            sl = slice(h * bw, (h + 1) * bw)
            back.append(_dot_nt(dzrb[:, sl], wr_ref[h]) + _dot_nt(dzib[:, sl], wi_ref[h]))
            dwg_ref[0, h] += _dot_tn(xcb[:, sl], dzrb[:, sl])
            dwg_ref[1, h] += _dot_tn(xcb[:, sl], dzib[:, sl])
        dxc = dxc + jnp.concatenate(back, axis=1)
        acc(G_MIX_CBB, _colsum(dxc))
        acc(G_MIX_CBW, _colsum(dxc * xm3))
        acc(G_MIX_CBW + 1, _colsum(dxc * xm2))
        acc(G_MIX_CBW + 2, _colsum(dxc * xm1))
        acc(G_MIX_CBW + 3, _colsum(dxc * xb))
        dxt = dxbuf[...]
        dxb = (((dxc * tap(CW_B + 3) + _shift_rows(dxc, dxt, 1, True) * tap(CW_B + 2)) + _shift_rows(dxc, dxt, 2, True) * tap(CW_B + 1))
               + _shift_rows(dxc, dxt, 3, True) * tap(CW_B))
        dxbuf[...] = dxc[:SUBLANES, :]
        dproj_ref[:, 3 * d:4 * d] = dxb.astype(BF16)

    def rev(i):
        return (nt - 1 - i, 0)

    def halo(i):
        return (jnp.maximum((nt - 1 - i) * per - 1, 0), 0)

    tile = pl.BlockSpec((tm, d), rev)
    return pl.pallas_call(
        body,
        name="mixer_core_bwd",
        grid=(nt,),
        in_specs=[
            tile, tile, tile, tile, tile,
            pl.BlockSpec((SUBLANES, d), halo),
            pl.BlockSpec((tm, 7 * d), rev),
            pl.BlockSpec((SUBLANES, 7 * d), halo),
            _resident((None, N_MOD, d), lambda i: (l, 0, 0)),
            _resident((None, N_ROWS, d), lambda i: (l, 0, 0)),
            _resident((None, CW_ROWS, d), lambda i: (l, 0, 0)),
            _resident((None, nh, bw, bw), lambda i: (l, 0, 0, 0)),
            _resident((None, nh, bw, bw), lambda i: (l, 0, 0, 0)),
            _resident((d, d), lambda i: (0, 0)),
            _resident((d, d), lambda i: (0, 0)),
            _resident((d, d), lambda i: (0, 0)),
            ANY_SPEC,
        ],
        out_specs=(pl.BlockSpec((tm, 7 * d), rev),) + (tile,) * 6 + (
            pl.BlockSpec((G_MIX_ROWS, d), lambda i: (0, 0)), pl.BlockSpec((2, nh, bw, bw), lambda i: (0, 0, 0, 0))),
        out_shape=(jax.ShapeDtypeStruct((s, 7 * d), BF16),) + (jax.ShapeDtypeStruct((s, d), BF16),) * 6 + (
            jax.ShapeDtypeStruct((G_MIX_ROWS, d), F32), jax.ShapeDtypeStruct((2, nh, bw, bw), F32)),
        scratch_shapes=[pltpu.VMEM((SUBLANES, d), F32)] * 3 + [pltpu.VMEM((nh, tm, bw), F32)] * 3 + [pltpu.VMEM((SUBLANES, d), F32)],
        compiler_params=_cparams(("arbitrary",), 3 * _nbytes((d, d), BF16) + 3 * _nbytes((tm, 7 * d), F32) + 64 * _nbytes((tm, d), F32)),
    )(*_hbm(dx1, y, yap, ybp, hs, hs, proj, proj), mod, rows, cwf, wr, wi, wa_f, wb_f, wo_f, after)


def _in_proj_bwd(dproj, x, dx1, mod, rows, win_f, l, tm, after):
    s, d = x.shape
    nb, _, ci = win_f.shape

    def body(dp_ref, x_ref, dx1_ref, mod_ref, rows_ref, w_hbm, after_ref, dx_ref, sm_ref, w_s, sem):
        @pl.when(pl.program_id(0) == 0)
        def _():
            sm_ref[...] = jnp.zeros(sm_ref.shape, F32)
            _load_side_by_side(w_hbm, w_s, sem)

        dh = _dot_nt(dp_ref[...], w_s[...])
        dxn, dsc, dsh, dg = _prenorm_bwd(x_ref[...], dh, rows_ref[R_G_PRE_MIX:R_G_PRE_MIX + 1, :], mod_ref[M_SC_M:M_SC_M + 1, :])
        dx_ref[...] = dx1_ref[...] + dxn
        for r, v in ((G_IN_SC, dsc), (G_IN_SH, dsh), (G_IN_GPRE, dg)):
            sm_ref[r:r + 1, :] += v

    tile = pl.BlockSpec((tm, d), lambda i: (i, 0))
    return pl.pallas_call(
        body,
        name="in_proj_bwd",
        grid=(s // tm,),
        in_specs=[
            pl.BlockSpec((tm, nb * ci), lambda i: (i, 0)), tile, tile,
            _resident((None, N_MOD, d), lambda i: (l, 0, 0)),
            _resident((None, N_ROWS, d), lambda i: (l, 0, 0)),
            ANY_SPEC,
            ANY_SPEC,
        ],
        out_specs=(tile, pl.BlockSpec((G_IN_ROWS, d), lambda i: (0, 0))),
        out_shape=(jax.ShapeDtypeStruct((s, d), F32), jax.ShapeDtypeStruct((G_IN_ROWS, d), F32)),
        scratch_shapes=[pltpu.VMEM((d, nb * ci), BF16), pltpu.SemaphoreType.DMA((nb,))],
        compiler_params=_cparams(("arbitrary",), _nbytes((nb, d, ci), BF16) + 2 * _nbytes((tm, nb * ci), BF16) + 16 * _nbytes((tm, d), F32)),
    )(*_hbm(dproj, x, dx1), mod, rows, win_f, after)


def _wgrad(a, b, cols_owned, ts, after):
    s, k1 = a.shape
    k2 = b.shape[1]
    ns = s // ts
    if cols_owned:
        nblk, bk1, bk2 = N_DEV, k1, k2 // N_DEV
        a_spec = pl.BlockSpec((ts, bk1), lambda j, t: (t, 0))
        b_spec = pl.BlockSpec((ts, bk2), lambda j, t: (t, j))
    else:
        bk1, bk2 = min(WGRAD_ROWS, k1), k2
        nblk = k1 // bk1
        a_spec = pl.BlockSpec((ts, bk1), lambda j, t: (t, j))
        b_spec = pl.BlockSpec((ts, bk2), lambda j, t: (t, 0))

    def body(a_ref, b_ref, after_ref, o_ref, acc_ref):
        t = pl.program_id(1)

        @pl.when(t == 0)
        def _():
            acc_ref[...] = jnp.zeros(acc_ref.shape, F32)

        acc_ref[...] += _dot_tn(a_ref[...], b_ref[...])

        @pl.when(t == ns - 1)
        def _():
            o_ref[...] = acc_ref[...].astype(BF16)

    out = pl.pallas_call(
        body,
        name="wgrad",
        grid=(nblk, ns),
        in_specs=[a_spec, b_spec, ANY_SPEC],
        out_specs=pl.BlockSpec((None, bk1, bk2), lambda j, t: (j, 0, 0)),
        out_shape=pltpu.HBM((nblk, bk1, bk2), BF16),
        scratch_shapes=[pltpu.VMEM((bk1, bk2), F32)],
        compiler_params=_cparams(("parallel", "arbitrary"), 4 * _nbytes((bk1, bk2), F32) + 4 * _nbytes((ts, bk1 + bk2), BF16)),
    )(pltpu.with_memory_space_constraint(a, pltpu.HBM), pltpu.with_memory_space_constraint(b, pltpu.HBM), after)
    return out if cols_owned else out.reshape(N_DEV, k1 // N_DEV, k2)


def _adam_update(w, g, m, v):
    m = ADAM_B1 * m + (1.0 - ADAM_B1) * g
    v = ADAM_B2 * v + (1.0 - ADAM_B2) * (g * g)
    m_hat = m / (1.0 - ADAM_B1 ** ADAM_STEP)
    v_hat = v / (1.0 - ADAM_B2 ** ADAM_STEP)
    delta = -ADAM_LR * (m_hat / (jnp.sqrt(v_hat) + ADAM_EPS) + ADAM_WD * w)
    return delta, m, v


def _sum_adamw(recv, w, m, v, tr, after):
    nl, ra, cb = w.shape
    assert nl == len(recv) == 2

    def body(r0_ref, r1_ref, w_ref, m_ref, v_ref, after_ref, g_ref, d_ref, nm_ref, nv_ref):
        def total(r_ref):
            g = r_ref[0].astype(F32)
            for j in range(1, N_DEV):
                g = g + r_ref[j].astype(F32)
            return g

        g = jnp.where(pl.program_id(0) == 0, total(r0_ref), total(r1_ref))
        g_ref[...] = g
        d_ref[...], nm_ref[...], nv_ref[...] = _adam_update(w_ref[...], g, m_ref[...], v_ref[...])

    blk = pl.BlockSpec((None, tr, cb), lambda l, i: (l, i, 0))
    return pl.pallas_call(
        body,
        name="sum_adamw",
        grid=(nl, ra // tr),
        in_specs=[pl.BlockSpec((N_DEV, tr, cb), lambda l, i: (0, i * (1 - l), 0)),
                  pl.BlockSpec((N_DEV, tr, cb), lambda l, i: (0, i * l, 0)), blk, blk, blk, ANY_SPEC],
        out_specs=(blk,) * 4,
        out_shape=(jax.ShapeDtypeStruct((nl, ra, cb), F32),) * 4,
        compiler_params=_cparams(("arbitrary", "arbitrary"), 6 * _nbytes((N_DEV, tr, cb), BF16) + 32 * _nbytes((tr, cb), F32)),
    )(recv[0], recv[1], w, m, v, after)


def _adamw(w, g, m, v):
    def body(w_ref, g_ref, m_ref, v_ref, d_ref, nm_ref, nv_ref):
        d_ref[...], nm_ref[...], nv_ref[...] = _adam_update(w_ref[...], g_ref[...], m_ref[...], v_ref[...])

    return pl.pallas_call(
        body,
        name="adamw",
        in_specs=[VMEM_SPEC] * 4,
        out_specs=(VMEM_SPEC,) * 3,
        out_shape=(jax.ShapeDtypeStruct(w.shape, F32),) * 3,
        compiler_params=_cparams(vmem=10 * _nbytes(w.shape, F32)),
    )(w, g, m, v)


def _adamw_tiled(w, g, m, v, tr):
    nl, ra, cb = w.shape

    def body(w_ref, g_ref, m_ref, v_ref, d_ref, nm_ref, nv_ref):
        d_ref[...], nm_ref[...], nv_ref[...] = _adam_update(w_ref[...], g_ref[...], m_ref[...], v_ref[...])

    blk = pl.BlockSpec((None, tr, cb), lambda l, i: (l, i, 0))
    return pl.pallas_call(
        body,
        name="adamw_tiled",
        grid=(nl, ra // tr),
        in_specs=[blk] * 4,
        out_specs=(blk,) * 3,
        out_shape=(jax.ShapeDtypeStruct((nl, ra, cb), F32),) * 3,
        compiler_params=_cparams(("parallel", "parallel")),
    )(*_hbm(w, g, m, v))


def _token_tile(s):
    return min(TOKEN_TILE, max(SUBLANES * 2, s // 4))


def kernel(x, c, w_mod, b_mod, g_pre_mix, g_post_mix, w_in, conv_a_w, conv_a_b, w_a_out, conv_b_w, conv_b_b, w_gate_r, b_gate_r, w_gate_i, b_gate_i, lru_lambda, w_b_out, w_o, g_pre_mlp, g_post_mlp, w_mlp_up, w_mlp_down, loss_target, m_w_mod, m_b_mod, m_g_pre_mix, m_g_post_mix, m_w_in, m_conv_a_w, m_conv_a_b, m_w_a_out, m_conv_b_w, m_conv_b_b, m_w_gate_r, m_b_gate_r, m_w_gate_i, m_b_gate_i, m_lru_lambda, m_w_b_out, m_w_o, m_g_pre_mlp, m_g_post_mlp, m_w_mlp_up, m_w_mlp_down, v_w_mod, v_b_mod, v_g_pre_mix, v_g_post_mix, v_w_in, v_conv_a_w, v_conv_a_b, v_w_a_out, v_conv_b_w, v_conv_b_b, v_w_gate_r, v_b_gate_r, v_w_gate_i, v_b_gate_i, v_lru_lambda, v_w_b_out, v_w_o, v_g_pre_mlp, v_g_post_mlp, v_w_mlp_up, v_w_mlp_down):
    nl = w_mod.shape[0]
    s, d = x.shape[1], x.shape[2]
    nh, bw = w_gate_r.shape[1], w_gate_r.shape[2]
    cwid = conv_a_w.shape[2]
    tm = _token_tile(s)
    tmx = min(2 * tm, s)
    ts = s
    _, me = _my_position()
    xs = x.reshape(s, d)
    target = loss_target.reshape(s, d)

    vec_names = (g_pre_mix, g_post_mix, conv_a_b, conv_b_b, b_gate_r, b_gate_i, lru_lambda, g_pre_mlp, g_post_mlp)
    rows = jnp.concatenate([jnp.stack(vec_names, axis=1), jnp.zeros((nl, N_ROWS - len(vec_names), d), F32)], axis=1)
    cw = jnp.concatenate([conv_a_w, conv_b_w, jnp.zeros((nl, CW_ROWS - 7, cwid), F32)], axis=1)

    large = {"w_in": w_in, "w_a_out": w_a_out, "w_b_out": w_b_out, "w_o": w_o, "w_mlp_up": w_mlp_up, "w_mlp_down": w_mlp_down}
    groups = (("in", ("w_in",)), ("mix", ("w_a_out", "w_b_out", "w_o")), ("mlp", ("w_mlp_up", "w_mlp_down")))
    mod, cact, cwf = _prep_small(c, w_mod, b_mod, cw)
    mod = mod.reshape(nl, N_MOD, d)
    gathers = {}
    tok = mod
    for l in range(nl):
        for gname, members in groups:
            gathers[l, gname] = _gather2_start([large[n][l].astype(BF16) for n in members], tok, f"gather_start_{gname}{l}")
            tok = gathers[l, gname][4]
    wr = w_gate_r.astype(BF16)
    wi = w_gate_i.astype(BF16)

    forwarded = {}

    def forward(l, gname, after):
        forwarded[l, gname] = _gather2_forward(gathers[l, gname], after, f"gather_forward_{gname}{l}")
        return forwarded[l, gname][4]

    def gathered(l, gname, after):
        return _gather2_wait(forwarded[l, gname], after, f"gather_wait_{gname}{l}")

    saved = []
    weights = []
    xin = xs
    for l in range(nl):
        if l == 0:
            tok = forward(0, "in", tok)
        (win_f,) = gathered(l, "in", tok if l == 0 else xin)
        if l > 0:
            tok = forward(l, "mix", win_f)
        proj, h = _in_proj_fwd(xin, mod, rows, win_f, l, tmx, tok)
        if l == 0:
            tok = forward(0, "mix", proj)
        wa_f, wb_f, wo_f = (w.reshape(d, d) for w in gathered(l, "mix", proj))
        if l > 0:
            tok = forward(l, "mlp", wo_f)
        x1, hs, yap, ybp, y = _mixer_core_fwd(proj, xin, mod, rows, cwf, wr, wi, wa_f, wb_f, wo_f, l, tm, tok)
        if l == 0:
            forward(0, "mlp", x1)
        wup_f, wdn_f = gathered(l, "mlp", x1)
        wdn_f = wdn_f.reshape(-1, d)
        if l + 1 < nl:
            tok = forward(l + 1, "in", wdn_f)
        x2, ru, y2, h2, *last = _mlp_fwd(x1, mod, rows, wup_f, wdn_f, l, tmx, tok, target if l + 1 == nl else None)
        saved.append((xin, proj, h, x1, hs, yap, ybp, y, ru, y2, h2))
        weights.append((win_f, wa_f, wb_f, wo_f, wup_f, wdn_f))
        xin = x2
    dx, loss_part = last[0], last[1][0, 0]

    scatters = {}
    small = [None] * nl
    gate_parts = [None] * nl

    def scatter(l, gname, parts):
        scatters[l, gname] = _exchange_start(parts, False, f"scatter_start_{gname}{l}")
        return scatters[l, gname][4]

    for l in reversed(range(nl)):
        xin, proj, h, x1, hs, yap, ybp, y, ru, y2, h2 = saved[l]
        win_f, wa_f, wb_f, wo_f, wup_f, wdn_f = weights[l]
        dx1, dy2, dup, act, sm_mlp = _mlp_bwd(dx, x1, y2, ru, mod, rows, wup_f, wdn_f, l, tm)
        g_up = _wgrad(h2, dup, True, ts, dx1)
        g_dn = _wgrad(act, dy2, False, ts, g_up)
        tok = scatter(l, "mlp", [g_up, g_dn])
        dproj, dy, m, dyap, dybp, ya, yb, sm_mix, dwg = _mixer_core_bwd(
            dx1, y, yap, ybp, hs, proj, mod, rows, cwf, wr, wi, wa_f, wb_f, wo_f, l, tm // 2, tok)
        gate_parts[l] = _exchange_start([dwg.astype(BF16)], True, f"gates_start{l}")
        g_a = _wgrad(ya, dyap, False, ts, gate_parts[l][4])
        g_b = _wgrad(yb, dybp, False, ts, g_a)
        g_o = _wgrad(m, dy, False, ts, g_b)
        tok = scatter(l, "mix", [g_a, g_b, g_o])
        tok = scatter(l, "in", [_wgrad(h, dproj, True, ts, tok)])
        dx, sm_in = _in_proj_bwd(dproj, xin, dx1, mod, rows, win_f, l, tm, tok)
        small[l] = jnp.concatenate([sm_mlp, sm_mix, sm_in], axis=0)
    grad_x = dx.reshape(x.shape)

    recv = {}
    big = {}
    moments = {"w_in": (w_in, m_w_in, v_w_in), "w_mlp_up": (w_mlp_up, m_w_mlp_up, v_w_mlp_up), "w_a_out": (w_a_out, m_w_a_out, v_w_a_out),
               "w_b_out": (w_b_out, m_w_b_out, v_w_b_out), "w_o": (w_o, m_w_o, v_w_o), "w_mlp_down": (w_mlp_down, m_w_mlp_down, v_w_mlp_down)}

    def collect(l, gname, after):
        for n, zone in zip(dict(groups)[gname], _exchange_wait(scatters[l, gname], after, False, f"scatter_wait_{gname}{l}")):
            recv[n, l] = zone

    def update(name, after):
        w, m_, v_ = moments[name]
        big[name] = _sum_adamw([recv[name, l] for l in range(nl)], w, m_, v_, min(ADAM_ROWS, w.shape[1]), after)
        return big[name][1]

    for l, gname in ((1, "mlp"), (1, "in"), (1, "mix"), (0, "mlp")):
        collect(l, gname, dx)
    done = update("w_mlp_down", update("w_mlp_up", dx))
    collect(0, "mix", done)
    for name in ("w_a_out", "w_b_out", "w_o"):
        done = update(name, done)
    collect(0, "in", done)
    done = update("w_in", done)

    lrows = jnp.concatenate(small, axis=0)
    lrows = lrows.at[G_LOSS_ROW, 0].set(loss_part)

    def lrow(a, l, r):
        return a[l * G_LAYER_ROWS + r]

    dm = jnp.stack([jnp.concatenate([lrow(lrows, l, G_MLP_ROWS + G_MIX_ROWS + G_IN_SH), lrow(lrows, l, G_MLP_ROWS + G_MIX_ROWS + G_IN_SC),
                                     lrow(lrows, l, G_MLP_ROWS + G_MIX_GT), lrow(lrows, l, G_MLP_SH), lrow(lrows, l, G_MLP_SC),
                                     lrow(lrows, l, G_MLP_GT)]) for l in range(nl)])
    dm8 = jnp.concatenate([dm[:, None, :], jnp.zeros((nl, SUBLANES - 1, N_MOD * d), F32)], axis=1)
    srows, g_w_mod = _reduce_small(lrows, dm8, cact, done)
    loss = srows[G_LOSS_ROW, 0]
    sgates = _sum_gathered([_exchange_wait(gate_parts[l], srows, True, f"gates_wait{l}")[0] for l in range(nl)])

    def srow(l, r):
        return lrow(srows, l, r)

    def per_layer(r):
        return jnp.stack([srow(l, r) for l in range(nl)])

    mix0 = G_MLP_ROWS
    in0 = G_MLP_ROWS + G_MIX_ROWS
    g_b_mod = jnp.stack([jnp.concatenate([srow(l, in0 + G_IN_SH), srow(l, in0 + G_IN_SC), srow(l, mix0 + G_MIX_GT),
                                          srow(l, G_MLP_SH), srow(l, G_MLP_SC), srow(l, G_MLP_GT)]) for l in range(nl)])
    conv_a_full = jnp.stack([jnp.stack([srow(l, mix0 + G_MIX_CAW + k) for k in range(3)]) for l in range(nl)])
    conv_b_full = jnp.stack([jnp.stack([srow(l, mix0 + G_MIX_CBW + k) for k in range(4)]) for l in range(nl)])
    grads = {
        "b_mod": g_b_mod,
        "g_pre_mix": per_layer(in0 + G_IN_GPRE),
        "g_post_mix": per_layer(mix0 + G_MIX_GPOST),
        "conv_a_w": lax.dynamic_slice_in_dim(conv_a_full, me * cwid, cwid, axis=2),
        "conv_a_b": per_layer(mix0 + G_MIX_CAB),
        "conv_b_w": lax.dynamic_slice_in_dim(conv_b_full, me * cwid, cwid, axis=2),
        "conv_b_b": per_layer(mix0 + G_MIX_CBB),
        "w_gate_r": sgates[:, 0],
        "b_gate_r": per_layer(mix0 + G_MIX_BR),
        "w_gate_i": sgates[:, 1],
        "b_gate_i": per_layer(mix0 + G_MIX_BI),
        "lru_lambda": per_layer(mix0 + G_MIX_LAM),
        "g_pre_mlp": per_layer(G_MLP_GPRE),
        "g_post_mlp": per_layer(G_MLP_GPOST),
    }
    params = {
        "b_mod": (b_mod, m_b_mod, v_b_mod), "g_pre_mix": (g_pre_mix, m_g_pre_mix, v_g_pre_mix), "g_post_mix": (g_post_mix, m_g_post_mix, v_g_post_mix),
        "conv_a_w": (conv_a_w, m_conv_a_w, v_conv_a_w), "conv_a_b": (conv_a_b, m_conv_a_b, v_conv_a_b),
        "conv_b_w": (conv_b_w, m_conv_b_w, v_conv_b_w), "conv_b_b": (conv_b_b, m_conv_b_b, v_conv_b_b),
        "w_gate_r": (w_gate_r, m_w_gate_r, v_w_gate_r), "b_gate_r": (b_gate_r, m_b_gate_r, v_b_gate_r),
        "w_gate_i": (w_gate_i, m_w_gate_i, v_w_gate_i), "b_gate_i": (b_gate_i, m_b_gate_i, v_b_gate_i),
        "lru_lambda": (lru_lambda, m_lru_lambda, v_lru_lambda), "g_pre_mlp": (g_pre_mlp, m_g_pre_mlp, v_g_pre_mlp),
        "g_post_mlp": (g_post_mlp, m_g_post_mlp, v_g_post_mlp),
    }
    out = {}
    for name, g in grads.items():
        w, m_, v_ = params[name]
        flat = (-1, w.shape[-1])
        dl, nm, nv = _adamw(w.reshape(flat), g.reshape(flat), m_.reshape(flat), v_.reshape(flat))
        out[name] = (g.reshape(w.shape), dl.reshape(w.shape), nm.reshape(w.shape), nv.reshape(w.shape))
    out["w_mod"] = (g_w_mod,) + tuple(_adamw_tiled(w_mod, g_w_mod, m_w_mod, v_w_mod, min(ADAM_ROWS, d)))
    out.update(big)

    order = ("w_mod", "b_mod", "g_pre_mix", "g_post_mix", "w_in", "conv_a_w", "conv_a_b", "w_a_out", "conv_b_w", "conv_b_b", "w_gate_r", "b_gate_r",
             "w_gate_i", "b_gate_i", "lru_lambda", "w_b_out", "w_o", "g_pre_mlp", "g_post_mlp", "w_mlp_up", "w_mlp_down")
    return (loss, grad_x) + tuple(out[n][0] for n in order) + tuple(out[n][1] for n in order) + tuple(out[n][2] for n in order) + tuple(out[n][3] for n in order)
```

```python
import math

import jax
import jax.numpy as jnp
from jax import lax
from jax.experimental import pallas as pl
from jax.experimental.pallas import tpu as pltpu

F32, BF16 = jnp.float32, jnp.bfloat16
EPS = 1e-6
LRU_C = 8.0
N_DEV = 8
N_MOD = 6
SUBLANES = 8
VMEM_BUDGET = 56 * 1024 * 1024
MXU_COLS = 256
WGRAD_ROWS = 512
TOKEN_TILE = 256
ADAM_ROWS = 256
SCAN_ROWS = SUBLANES * SUBLANES
ADAM_LR, ADAM_B1, ADAM_B2, ADAM_EPS, ADAM_WD, ADAM_STEP = 0.001, 0.9, 0.999, 1e-08, 0.01, 10
MESH = pl.DeviceIdType.MESH
VMEM_SPEC = pl.BlockSpec(memory_space=pltpu.VMEM)
ANY_SPEC = pl.BlockSpec(memory_space=pl.ANY)
HBM_SPEC = pl.BlockSpec(memory_space=pltpu.HBM)
SEM_SPEC = pl.BlockSpec(memory_space=pltpu.SEMAPHORE)
SIDE_EFFECT = pltpu.SideEffectType.DATAFLOW_SIDE_EFFECTING

R_G_PRE_MIX, R_G_POST_MIX, R_CONV_A_B, R_CONV_B_B, R_B_GATE_R, R_B_GATE_I, R_LAMBDA, R_G_PRE_MLP, R_G_POST_MLP = range(9)
N_ROWS = 16
M_SH_M, M_SC_M, M_GT_M, M_SH_F, M_SC_F, M_GT_F = range(6)
CW_A, CW_B, CW_ROWS = 0, 3, 8
G_MLP_GT, G_MLP_GPOST, G_MLP_SC, G_MLP_SH, G_MLP_GPRE, G_LOSS_ROW, G_MLP_ROWS = 0, 1, 2, 3, 4, 7, 8
(G_MIX_GT, G_MIX_GPOST, G_MIX_CAB, G_MIX_CAW, G_MIX_CBB, G_MIX_CBW, G_MIX_BR, G_MIX_BI, G_MIX_LAM) = 0, 1, 2, 3, 6, 7, 11, 12, 13
G_MIX_ROWS = 16
G_IN_SC, G_IN_SH, G_IN_GPRE, G_IN_ROWS = 0, 1, 2, 8
G_LAYER_ROWS = G_MLP_ROWS + G_MIX_ROWS + G_IN_ROWS


def _cparams(dims=None, vmem=None):
    kw = {}
    if dims is not None:
        kw["dimension_semantics"] = dims
    if vmem is not None:
        kw["vmem_limit_bytes"] = int(min(max(vmem, 16 * 1024 * 1024), VMEM_BUDGET))
    return pltpu.CompilerParams(**kw)


def _nbytes(shape, dtype):
    n = 1
    for s in shape:
        n *= s
    return n * jnp.dtype(dtype).itemsize


def _hbm(*arrays):
    return tuple(pltpu.with_memory_space_constraint(a, pltpu.HBM) for a in arrays)


def _resident(block, index_map):
    return pl.BlockSpec(block, index_map, pipeline_mode=pl.Buffered(1))


def _my_position():
    x, y, c = lax.axis_index("x"), lax.axis_index("y"), lax.axis_index("c")
    return (x, y, c), 4 * x + 2 * y + c


def _peer(pos, k):
    x, y, c = pos
    px = 1 - x if k & 4 else x
    py = 1 - y if k & 2 else y
    pc = 1 - c if k & 1 else c
    return (px, py, pc), 4 * px + 2 * py + pc


def _remote(src, dst, ssem, rsem, peer):
    return pltpu.make_async_remote_copy(src_ref=src, dst_ref=dst, send_sem=ssem, recv_sem=rsem, device_id=peer, device_id_type=MESH)


def _dot(a, b):
    return jnp.dot(a, b, preferred_element_type=F32)


def _dot_nt(a, b):
    return lax.dot_general(a, b, (((1,), (1,)), ((), ())), preferred_element_type=F32)


def _dot_tn(a, b):
    return lax.dot_general(a, b, (((0,), (0,)), ((), ())), preferred_element_type=F32)


def _colsum(v):
    return jnp.sum(v, axis=0, keepdims=True)


def _sigmoid(v):
    return jax.nn.sigmoid(v)


GELU_K, GELU_C = 0.7978845608028654, 0.044715


def _gelu(v):
    s = 1.0 / (1.0 + jnp.exp(v * (-2.0 * GELU_K - (2.0 * GELU_K * GELU_C) * (v * v))))
    return v * s, s


def _gelu_grad(v, s):
    return s * (1.0 + (v * (1.0 - s)) * (2.0 * GELU_K + (6.0 * GELU_K * GELU_C) * (v * v)))


def _neg_expm1_twice(v):
    t = jnp.tanh(v)
    return (-2.0 * t) / (1.0 - t)


def _softplus_neg(lam):
    z = -lam
    u = jnp.exp(-jnp.abs(z))
    w = 1.0 + u
    l1p = jnp.where(w == 1.0, u, jnp.log(w) * u / (w - 1.0))
    return jnp.maximum(z, 0.0) + l1p


def _rms(v):
    return lax.rsqrt(jnp.mean(v * v, axis=-1, keepdims=True) + EPS)


def _prenorm_bwd(xv, dh, g, sc):
    r = _rms(xv)
    xn = xv * r
    n = xn * g
    dsc = _colsum(dh * n)
    dsh = _colsum(dh)
    dn = dh * (1.0 + sc)
    dg = _colsum(dn * xn)
    dxn = dn * g
    dx = r * (dxn - xn * jnp.mean(dxn * xn, axis=-1, keepdims=True))
    return dx, dsc, dsh, dg


def _postnorm_bwd(yv, dout, g, gt):
    r = _rms(yv)
    yn = yv * r
    dgt = _colsum(dout * (yn * g))
    dn = dout * gt
    dg = _colsum(dn * yn)
    dyn = dn * g
    dy = r * (dyn - yn * jnp.mean(dyn * yn, axis=-1, keepdims=True))
    return dy, dgt, dg


def _gates(xc, wr_ref, wi_ref, b_r, b_i, sp, nh, bw):
    xcb = xc.astype(BF16)
    zr = jnp.concatenate([_dot(xcb[:, h * bw:(h + 1) * bw], wr_ref[h]) for h in range(nh)], axis=1) + b_r
    zi = jnp.concatenate([_dot(xcb[:, h * bw:(h + 1) * bw], wi_ref[h]) for h in range(nh)], axis=1) + b_i
    r = _sigmoid(zr)
    ig = _sigmoid(zi)
    la = (-LRU_C * r) * sp
    a = jnp.exp(la)
    mult = jnp.sqrt(_neg_expm1_twice(la))
    return xcb, r, ig, a, mult


def _shift_rows(cur, edge, k, up):
    t, dd = cur.shape
    blocks = cur.reshape(t // SUBLANES, SUBLANES, dd)
    row = lax.broadcasted_iota(jnp.int32, (1, SUBLANES, dd), 1)
    if up:
        r = pltpu.roll(blocks, SUBLANES - k, 1)
        nxt = jnp.concatenate([r[1:], pltpu.roll(edge, SUBLANES - k, 0)[None]], axis=0)
        out = jnp.where(row >= SUBLANES - k, nxt, r)
    else:
        r = pltpu.roll(blocks, k, 1)
        prv = jnp.concatenate([pltpu.roll(edge, k, 0)[None], r[:-1]], axis=0)
        out = jnp.where(row < k, prv, r)
    return out.reshape(t, dd)


def _scan_group(a_ref, b_ref, o_ref, h, base, carry, reverse):
    def rows(k):
        return pl.ds(base + k, SUBLANES, stride=SUBLANES)

    order = range(SUBLANES - 1, -1, -1) if reverse else range(SUBLANES)
    loc, prod = {}, {}
    prev = None
    for k in order:
        a, b = a_ref[h, rows(k), :], b_ref[h, rows(k), :]
        loc[k] = b if prev is None else a * loc[prev] + b
        prod[k] = a if prev is None else a * prod[prev]
        prev = k
    pa, pb = _scan_block(prod[prev], loc[prev], reverse)
    ends = pb + pa * carry
    row = lax.broadcasted_iota(jnp.int32, ends.shape, 0)
    if reverse:
        into = jnp.where(row < SUBLANES - 1, pltpu.roll(ends, SUBLANES - 1, 0), carry)
        last = ends[0:1, :]
    else:
        into = jnp.where(row >= 1, pltpu.roll(ends, 1, 0), carry)
        last = ends[SUBLANES - 1:SUBLANES, :]
    for k in range(SUBLANES):
        o_ref[h, rows(k), :] = loc[k] + prod[k] * into
    return jnp.broadcast_to(last, ends.shape)


def _scan_block(a8, b8, reverse):
    row = lax.broadcasted_iota(jnp.int32, a8.shape, 0)
    for s in (1, 2, 4):
        if reverse:
            keep = row < SUBLANES - s
            a_sh = pltpu.roll(a8, SUBLANES - s, 0)
            b_sh = pltpu.roll(b8, SUBLANES - s, 0)
        else:
            keep = row >= s
            a_sh = pltpu.roll(a8, s, 0)
            b_sh = pltpu.roll(b8, s, 0)
        b8 = b8 + a8 * jnp.where(keep, b_sh, 0.0)
        a8 = a8 * jnp.where(keep, a_sh, 1.0)
    return a8, b8


def _prep_small(c, w_mod, b_mod, cw):
    d = c.shape[1]
    cm = w_mod.shape[2]
    cwid = cw.shape[2]
    nl = w_mod.shape[0]

    def body(c_ref, wm_ref, bm_ref, cw_ref, mod_ref, cact_ref, cwf_ref, cbuf, pbuf, rbuf, ssem, rsem, lsem):
        pos, me = _my_position()
        me8 = pl.multiple_of(me * SUBLANES, SUBLANES)
        cbuf[pl.ds(me8, SUBLANES), :] = jnp.broadcast_to(c_ref[...], (SUBLANES, d))
        own_cw = pltpu.make_async_copy(cw_ref, cwf_ref.at[:, :, pl.ds(me * cwid, cwid)], lsem.at[0])
        own_cw.start()
        first = []
        for k in range(1, N_DEV):
            peer, _ = _peer(pos, k)
            rows = cbuf.at[pl.ds(me8, SUBLANES), :]
            first.append(_remote(rows, rows, ssem.at[0, k - 1], rsem.at[0, k - 1], peer))
            first.append(_remote(cw_ref, cwf_ref.at[:, :, pl.ds(me * cwid, cwid)], ssem.at[1, k - 1], rsem.at[1, k - 1], peer))
        for cp in first:
            cp.start()
        for k in range(1, N_DEV):
            peer, pj = _peer(pos, k)
            pj8 = pl.multiple_of(pj * SUBLANES, SUBLANES)
            rows = cbuf.at[pl.ds(pj8, SUBLANES), :]
            _remote(rows, rows, ssem.at[0, k - 1], rsem.at[0, k - 1], peer).wait_recv()
        cv = cbuf[...]
        cact = cv * _sigmoid(cv)
        cact_ref[...] = cact
        cb = cact.astype(BF16)
        for l in range(nl):
            pbuf[l] = _dot(cb, wm_ref[l].astype(BF16))
        own_p = pltpu.make_async_copy(pbuf.at[:, pl.ds(me8, SUBLANES), :], rbuf.at[me], lsem.at[1])
        own_p.start()
        second = []
        for k in range(1, N_DEV):
            peer, pj = _peer(pos, k)
            pj8 = pl.multiple_of(pj * SUBLANES, SUBLANES)
            second.append(_remote(pbuf.at[:, pl.ds(pj8, SUBLANES), :], rbuf.at[me], ssem.at[2, k - 1], rsem.at[2, k - 1], peer))
        for cp in second:
            cp.start()
        for k in range(1, N_DEV):
            peer, pj = _peer(pos, k)
            _remote(pbuf.at[:, pl.ds(0, SUBLANES), :], rbuf.at[pj], ssem.at[2, k - 1], rsem.at[2, k - 1], peer).wait_recv()
            _remote(cw_ref, cwf_ref.at[:, :, pl.ds(pj * cwid, cwid)], ssem.at[1, k - 1], rsem.at[1, k - 1], peer).wait_recv()
        own_p.wait()
        own_cw.wait()
        for l in range(nl):
            for j in range(N_DEV):
                mod_ref[l:l + 1, j * cm:(j + 1) * cm] = rbuf[j, l, 0:1, :] + bm_ref[l:l + 1, j * cm:(j + 1) * cm]
        for cp in first + second:
            cp.wait_send()

    return pl.pallas_call(
        body,
        name="prep_small",
        out_shape=(
            jax.ShapeDtypeStruct((nl, N_MOD * d), F32),
            jax.ShapeDtypeStruct((N_DEV * SUBLANES, d), F32),
            jax.ShapeDtypeStruct((nl, CW_ROWS, d), F32),
        ),
        in_specs=[VMEM_SPEC] * 4,
        out_specs=(VMEM_SPEC,) * 3,
        scratch_shapes=[
            pltpu.VMEM((N_DEV * SUBLANES, d), F32),
            pltpu.VMEM((nl, N_DEV * SUBLANES, cm), F32),
            pltpu.VMEM((N_DEV, nl, SUBLANES, cm), F32),
            pltpu.SemaphoreType.DMA((3, N_DEV - 1)),
            pltpu.SemaphoreType.DMA((3, N_DEV - 1)),
            pltpu.SemaphoreType.DMA((2,)),
        ],
        compiler_params=_cparams(vmem=3 * _nbytes(w_mod.shape, F32)),
    )(c, w_mod, b_mod, cw)


def _exchange_start(parts, gather, name):
    n = len(parts)
    lands = [lax.empty(((N_DEV,) + tuple(p.shape)) if gather else tuple(p.shape), p.dtype) for p in parts]

    def body(*refs):
        ins, lnd = refs[:n], refs[n:2 * n]
        ssem, rsem, token = refs[2 * n], refs[2 * n + 1], refs[-1]
        pos, me = _my_position()
        for k in range(1, N_DEV):
            peer, pj = _peer(pos, k)
            for t in range(n):
                src = ins[t] if gather else ins[t].at[pj]
                q = t * (N_DEV - 1) + k - 1
                _remote(src, lnd[t].at[me], ssem.at[q], rsem.at[q], peer).start()
        token[...] = jnp.zeros(token.shape, F32)

    out = pl.pallas_call(
        body,
        name=name,
        out_shape=(pltpu.SemaphoreType.DMA((n * (N_DEV - 1),)), pltpu.SemaphoreType.DMA((n * (N_DEV - 1),)))
        + tuple(pltpu.HBM(p.shape, p.dtype) for p in parts) + tuple(pltpu.HBM(p.shape, p.dtype) for p in lands)
        + (jax.ShapeDtypeStruct((SUBLANES, 128), F32),),
        in_specs=[HBM_SPEC] * (2 * n),
        out_specs=(SEM_SPEC, SEM_SPEC) + (HBM_SPEC,) * (2 * n) + (VMEM_SPEC,),
        input_output_aliases={i: 2 + i for i in range(2 * n)},
        compiler_params=pltpu.CompilerParams(has_side_effects=SIDE_EFFECT),
    )(*[pltpu.with_memory_space_constraint(p, pltpu.HBM) for p in list(parts) + lands])
    return out[0], out[1], list(out[2:2 + n]), list(out[2 + n:2 + 2 * n]), out[-1]


def _exchange_wait(started, after, gather, name):
    ssem, rsem, parts, lands, _ = started
    n = len(parts)

    def body(*refs):
        ins, lnd = refs[:n], refs[n:2 * n]
        ssem_ref, rsem_ref = refs[2 * n], refs[2 * n + 1]
        stage, lsem = refs[-1 - n:-1], refs[-1]
        pos, me = _my_position()
        load = []
        for t in range(n):
            src = ins[t] if gather else ins[t].at[me]
            load.append(pltpu.make_async_copy(src, stage[t], lsem.at[t]))
            load[-1].start()
        store = []
        for t in range(n):
            load[t].wait()
            store.append(pltpu.make_async_copy(stage[t], lnd[t].at[me], lsem.at[t]))
            store[-1].start()
        for k in range(1, N_DEV):
            peer, pj = _peer(pos, k)
            for t in range(n):
                src = ins[t] if gather else ins[t].at[pj]
                q = t * (N_DEV - 1) + k - 1
                _remote(src, lnd[t].at[me], ssem_ref.at[q], rsem_ref.at[q], peer).wait_send()
                _remote(src, lnd[t].at[pj], ssem_ref.at[q], rsem_ref.at[q], peer).wait_recv()
        for cp in store:
            cp.wait()

    out = pl.pallas_call(
        body,
        name=name,
        out_shape=tuple(pltpu.HBM(p.shape, p.dtype) for p in parts) + tuple(pltpu.HBM(p.shape, p.dtype) for p in lands),
        in_specs=[HBM_SPEC] * (2 * n) + [SEM_SPEC, SEM_SPEC, ANY_SPEC],
        out_specs=(HBM_SPEC,) * (2 * n),
        input_output_aliases={i: i for i in range(2 * n)},
        scratch_shapes=[pltpu.VMEM(tuple(z.shape[1:]), z.dtype) for z in lands] + [pltpu.SemaphoreType.DMA((n,))],
        compiler_params=pltpu.CompilerParams(has_side_effects=SIDE_EFFECT),
    )(*parts, *lands, ssem, rsem, after)
    return list(out[n:])


CHIP_PEERS = 3


def _chip_peer(pos, k, core):
    x, y, _ = pos
    px = 1 - x if k & 2 else x
    py = 1 - y if k & 1 else y
    return (px, py, core), 4 * px + 2 * py + core


def _gather2_start(parts, after, name):
    n = len(parts)
    lands = [lax.empty((N_DEV,) + tuple(p.shape), p.dtype) for p in parts]
    per = CHIP_PEERS + 1

    def body(*refs):
        ins, lnd = refs[:n], refs[n:2 * n]
        ssem, rsem, token = refs[2 * n + 1], refs[2 * n + 2], refs[-1]
        pos, me = _my_position()
        sibling = (pos[0], pos[1], 1 - pos[2])
        for t in range(n):
            _remote(ins[t], lnd[t].at[me], ssem.at[per * t], rsem.at[per * t], sibling).start()
        for k in range(1, per):
            peer, _ = _chip_peer(pos, k, pos[2])
            for t in range(n):
                _remote(ins[t], lnd[t].at[me], ssem.at[per * t + k], rsem.at[per * t + k], peer).start()
        token[...] = jnp.zeros(token.shape, F32)

    out = pl.pallas_call(
        body,
        name=name,
        out_shape=(pltpu.SemaphoreType.DMA((n * per,)), pltpu.SemaphoreType.DMA((n * per,)))
        + tuple(pltpu.HBM(p.shape, p.dtype) for p in parts) + tuple(pltpu.HBM(p.shape, p.dtype) for p in lands)
        + (jax.ShapeDtypeStruct((SUBLANES, 128), F32),),
        in_specs=[HBM_SPEC] * (2 * n) + [ANY_SPEC],
        out_specs=(SEM_SPEC, SEM_SPEC) + (HBM_SPEC,) * (2 * n) + (VMEM_SPEC,),
        input_output_aliases={i: 2 + i for i in range(2 * n)},
        compiler_params=pltpu.CompilerParams(has_side_effects=SIDE_EFFECT),
    )(*[pltpu.with_memory_space_constraint(p, pltpu.HBM) for p in list(parts) + lands], after)
    return out[0], out[1], list(out[2:2 + n]), list(out[2 + n:2 + 2 * n]), out[-1]


def _gather2_forward(started, after, name):
    ssem, rsem, parts, lands, _ = started
    n = len(lands)
    per = CHIP_PEERS + 1

    def body(*refs):
        lnd = refs[:n]
        rsem_a, fsend, frecv, token = refs[n], refs[2 * n + 2], refs[2 * n + 3], refs[2 * n + 4]
        pos, _ = _my_position()
        sibling = (pos[0], pos[1], 1 - pos[2])
        for k in range(1, per):
            peer, pk = _chip_peer(pos, k, pos[2])
            for t in range(n):
                block = lnd[t].at[pk]
                _remote(block, block, rsem_a.at[per * t + k], rsem_a.at[per * t + k], peer).wait_recv()
                q = CHIP_PEERS * t + k - 1
                _remote(block, block, fsend.at[q], frecv.at[q], sibling).start()
        token[...] = jnp.zeros(token.shape, F32)

    out = pl.pallas_call(
        body,
        name=name,
        out_shape=tuple(pltpu.HBM(p.shape, p.dtype) for p in lands)
        + (pltpu.SemaphoreType.DMA((n * CHIP_PEERS,)), pltpu.SemaphoreType.DMA((n * CHIP_PEERS,)),
           jax.ShapeDtypeStruct((SUBLANES, 128), F32)),
        in_specs=[HBM_SPEC] * n + [SEM_SPEC, ANY_SPEC],
        out_specs=(HBM_SPEC,) * n + (SEM_SPEC, SEM_SPEC, VMEM_SPEC),
        input_output_aliases={i: i for i in range(n)},
        compiler_params=pltpu.CompilerParams(has_side_effects=SIDE_EFFECT),
    )(*lands, rsem, after)
    return ssem, rsem, parts, list(out[:n]), out[n + 2], out[n], out[n + 1]


def _gather2_wait(forwarded, after, name):
    ssem, rsem, parts, lands, _, fsend, frecv = forwarded
    n = len(parts)
    per = CHIP_PEERS + 1

    def body(*refs):
        ins, lnd = refs[:n], refs[n:2 * n]
        ssem_a, rsem_a, fs, fr = refs[2 * n:2 * n + 4]
        stage, lsem = refs[-1 - n:-1], refs[-1]
        pos, me = _my_position()
        sibling = (pos[0], pos[1], 1 - pos[2])
        sib = 4 * pos[0] + 2 * pos[1] + 1 - pos[2]
        load = []
        for t in range(n):
            load.append(pltpu.make_async_copy(ins[t], stage[t], lsem.at[t]))
            load[-1].start()
        store = []
        for t in range(n):
            load[t].wait()
            store.append(pltpu.make_async_copy(stage[t], lnd[t].at[me], lsem.at[t]))
            store[-1].start()
        for t in range(n):
            _remote(ins[t], lnd[t].at[me], ssem_a.at[per * t], rsem_a.at[per * t], sibling).wait_send()
            _remote(ins[t], lnd[t].at[sib], ssem_a.at[per * t], rsem_a.at[per * t], sibling).wait_recv()
        for k in range(1, per):
            peer, pk = _chip_peer(pos, k, pos[2])
            _, qk = _chip_peer(pos, k, 1 - pos[2])
            for t in range(n):
                q = CHIP_PEERS * t + k - 1
                _remote(ins[t], lnd[t].at[me], ssem_a.at[per * t + k], rsem_a.at[per * t + k], peer).wait_send()
                _remote(lnd[t].at[pk], lnd[t].at[pk], fs.at[q], fr.at[q], sibling).wait_send()
                _remote(lnd[t].at[qk], lnd[t].at[qk], fs.at[q], fr.at[q], sibling).wait_recv()
        for cp in store:
            cp.wait()

    out = pl.pallas_call(
        body,
        name=name,
        out_shape=tuple(pltpu.HBM(p.shape, p.dtype) for p in parts) + tuple(pltpu.HBM(p.shape, p.dtype) for p in lands),
        in_specs=[HBM_SPEC] * (2 * n) + [SEM_SPEC] * 4 + [ANY_SPEC],
        out_specs=(HBM_SPEC,) * (2 * n),
        input_output_aliases={i: i for i in range(2 * n)},
        scratch_shapes=[pltpu.VMEM(tuple(z.shape[1:]), z.dtype) for z in lands] + [pltpu.SemaphoreType.DMA((n,))],
        compiler_params=pltpu.CompilerParams(has_side_effects=SIDE_EFFECT),
    )(*parts, *lands, ssem, rsem, fsend, frecv, after)
    return list(out[n:])


def _reduce_small(rows, dm8, cact, after):
    r, d = rows.shape
    nl = dm8.shape[0]
    cm = dm8.shape[2] // N_DEV
    lw = d // N_DEV

    def body(rows_ref, dm_ref, cact_ref, after_ref, orow_ref, owm_ref, gr, mine, dmr, ssem, rsem, lsem):
        pos, me = _my_position()
        me8 = pl.multiple_of(me * SUBLANES, SUBLANES)

        def lanes(j):
            return pl.ds(j * lw, lw)

        own = [pltpu.make_async_copy(dm_ref.at[:, :, pl.ds(me * cm, cm)], dmr.at[:, pl.ds(me8, SUBLANES), :], lsem.at[0]),
               pltpu.make_async_copy(rows_ref.at[:, lanes(me)], gr.at[me], lsem.at[1])]
        first = []
        for k in range(1, N_DEV):
            peer, pj = _peer(pos, k)
            first.append(_remote(rows_ref.at[:, lanes(pj)], gr.at[me], ssem.at[0, k - 1], rsem.at[0, k - 1], peer))
            first.append(_remote(dm_ref.at[:, :, pl.ds(pj * cm, cm)], dmr.at[:, pl.ds(me8, SUBLANES), :],
                                 ssem.at[1, k - 1], rsem.at[1, k - 1], peer))
        for cp in own + first:
            cp.start()
        for k in range(1, N_DEV):
            peer, pj = _peer(pos, k)
            pj8 = pl.multiple_of(pj * SUBLANES, SUBLANES)
            _remote(rows_ref.at[:, lanes(0)], gr.at[pj], ssem.at[0, k - 1], rsem.at[0, k - 1], peer).wait_recv()
            _remote(dm_ref.at[:, :, pl.ds(0, cm)], dmr.at[:, pl.ds(pj8, SUBLANES), :], ssem.at[1, k - 1], rsem.at[1, k - 1], peer).wait_recv()
        for cp in own:
            cp.wait()
        acc = gr[0]
        for j in range(1, N_DEV):
            acc = acc + gr[j]
        mine[...] = acc
        own_sum = pltpu.make_async_copy(mine, orow_ref.at[:, lanes(me)], lsem.at[1])
        own_sum.start()
        second = []
        for k in range(1, N_DEV):
            peer, _ = _peer(pos, k)
            second.append(_remote(mine, orow_ref.at[:, lanes(me)], ssem.at[2, k - 1], rsem.at[2, k - 1], peer))
            second[-1].start()
        cb = cact_ref[...].astype(BF16)
        for l in range(nl):
            owm_ref[l] = _dot_tn(cb, dmr[l].astype(BF16))
        for k in range(1, N_DEV):
            peer, pj = _peer(pos, k)
            _remote(mine, orow_ref.at[:, lanes(pj)], ssem.at[2, k - 1], rsem.at[2, k - 1], peer).wait_recv()
        own_sum.wait()
        for cp in first + second:
            cp.wait_send()

    return pl.pallas_call(
        body,
        name="reduce_small",
        out_shape=(jax.ShapeDtypeStruct((r, d), F32), jax.ShapeDtypeStruct((nl, d, cm), F32)),
        in_specs=[VMEM_SPEC] * 3 + [ANY_SPEC],
        out_specs=(VMEM_SPEC,) * 2,
        scratch_shapes=[
            pltpu.VMEM((N_DEV, r, lw), F32),
            pltpu.VMEM((r, lw), F32),
            pltpu.VMEM((nl, N_DEV * SUBLANES, cm), F32),
            pltpu.SemaphoreType.DMA((3, N_DEV - 1)),
            pltpu.SemaphoreType.DMA((3, N_DEV - 1)),
            pltpu.SemaphoreType.DMA((2,)),
        ],
        compiler_params=_cparams(vmem=4 * _nbytes((r, d), F32) + 6 * _nbytes((nl, d, cm), F32)),
    )(rows, dm8, cact, after)


def _sum_gathered(zones):
    nl = len(zones)

    def body(*refs):
        for l in range(nl):
            acc = refs[l][0].astype(F32)
            for j in range(1, N_DEV):
                acc = acc + refs[l][j].astype(F32)
            refs[nl][l] = acc

    return pl.pallas_call(
        body,
        name="sum_gathered",
        out_shape=jax.ShapeDtypeStruct((nl,) + tuple(zones[0].shape[1:]), F32),
        in_specs=[VMEM_SPEC] * nl,
        out_specs=VMEM_SPEC,
        compiler_params=_cparams(vmem=8 * nl * _nbytes(zones[0].shape, BF16)),
    )(*zones)


def _load_side_by_side(w_hbm, w_s, sem):
    nb, _, ci = w_hbm.shape
    copies = [pltpu.make_async_copy(w_hbm.at[j], w_s.at[:, j * ci:(j + 1) * ci], sem.at[j]) for j in range(nb)]
    for cp in copies:
        cp.start()
    for cp in copies:
        cp.wait()


def _in_proj_fwd(x, mod, rows, win_f, l, tm, after):
    s, d = x.shape
    nb, _, ci = win_f.shape
    n = nb * ci
    wide = min(n, ci * MXU_COLS // math.gcd(ci, MXU_COLS))

    def body(x_ref, mod_ref, rows_ref, w_hbm, after_ref, proj_ref, h_ref, w_s, sem):
        @pl.when(pl.program_id(0) == 0)
        def _():
            _load_side_by_side(w_hbm, w_s, sem)

        xv = x_ref[...]
        g = rows_ref[R_G_PRE_MIX:R_G_PRE_MIX + 1, :]
        h = (xv * _rms(xv) * g) * (1.0 + mod_ref[M_SC_M:M_SC_M + 1, :]) + mod_ref[M_SH_M:M_SH_M + 1, :]
        hb = h.astype(BF16)
        h_ref[...] = hb
        for c0 in range(0, n, wide):
            proj_ref[:, c0:c0 + wide] = _dot(hb, w_s[:, c0:c0 + wide])

    return pl.pallas_call(
        body,
        name="in_proj_fwd",
        grid=(s // tm,),
        in_specs=[
            pl.BlockSpec((tm, d), lambda i: (i, 0)),
            _resident((None, N_MOD, d), lambda i: (l, 0, 0)),
            _resident((None, N_ROWS, d), lambda i: (l, 0, 0)),
            ANY_SPEC,
            ANY_SPEC,
        ],
        out_specs=(pl.BlockSpec((tm, n), lambda i: (i, 0)), pl.BlockSpec((tm, d), lambda i: (i, 0))),
        out_shape=(jax.ShapeDtypeStruct((s, n), F32), jax.ShapeDtypeStruct((s, d), BF16)),
        scratch_shapes=[pltpu.VMEM((d, n), BF16), pltpu.SemaphoreType.DMA((nb,))],
        compiler_params=_cparams(("arbitrary",), _nbytes((d, n), BF16) + 3 * _nbytes((tm, n), F32) + 8 * _nbytes((tm, d), F32)),
    )(*_hbm(x), mod, rows, win_f, after)


def _mixer_core_fwd(proj, x, mod, rows, cwf, wr, wi, wa_f, wb_f, wo_f, l, tm, after):
    s, d = x.shape
    nh, bw, _ = wr.shape[1:]

    def body(proj_ref, x_ref, mod_ref, rows_ref, cw_ref, wr_ref, wi_ref, wa_ref, wb_ref, wo_ref, after_ref,
             x1_ref, hs_ref, yap_ref, ybp_ref, y_ref, cvbuf, xbbuf, a_s, b_s, h_s, hprev):
        i = pl.program_id(0)

        @pl.when(i == 0)
        def _():
            cvbuf[...] = jnp.zeros((SUBLANES, d), F32)
            xbbuf[...] = jnp.zeros((SUBLANES, d), F32)
            hprev[...] = jnp.zeros((SUBLANES, d), F32)

        def row(r):
            return rows_ref[r:r + 1, :]

        def tap(r):
            return cw_ref[r:r + 1, :]

        ba = proj_ref[:, 0:d]
        cv = proj_ref[:, d:2 * d] * proj_ref[:, 2 * d:3 * d]
        cvt = cvbuf[...]
        conv3 = ((row(R_CONV_A_B) + _shift_rows(cv, cvt, 2, False) * tap(CW_A)) + _shift_rows(cv, cvt, 1, False) * tap(CW_A + 1)) + cv * tap(CW_A + 2)
        ya = ba * conv3
        cvbuf[...] = cv[tm - SUBLANES:, :]
        sp = _softplus_neg(row(R_LAMBDA))
        for h in range(nh):
            lo, hi = h * bw, (h + 1) * bw
            xb = proj_ref[:, 3 * d + lo:3 * d + hi]
            xbt = xbbuf[:, lo:hi]

            def hrow(r):
                return rows_ref[r:r + 1, lo:hi]

            def htap(r):
                return cw_ref[r:r + 1, lo:hi]

            xc = (((hrow(R_CONV_B_B) + _shift_rows(xb, xbt, 3, False) * htap(CW_B)) + _shift_rows(xb, xbt, 2, False) * htap(CW_B + 1))
                  + _shift_rows(xb, xbt, 1, False) * htap(CW_B + 2)) + xb * htap(CW_B + 3)
            xcb = xc.astype(BF16)
            r = _sigmoid(_dot(xcb, wr_ref[h]) + hrow(R_B_GATE_R))
            ig = _sigmoid(_dot(xcb, wi_ref[h]) + hrow(R_B_GATE_I))
            la = (-LRU_C * r) * sp[:, lo:hi]
            a_s[h] = jnp.exp(la)
            b_s[h] = jnp.sqrt(_neg_expm1_twice(la)) * (ig * xc)
        xbbuf[...] = proj_ref[tm - SUBLANES:, 3 * d:4 * d]

        def group(j, hp):
            base = pl.multiple_of(j * SCAN_ROWS, SCAN_ROWS)
            return jnp.concatenate([_scan_group(a_s, b_s, h_s, h, base, hp[:, h * bw:(h + 1) * bw], False) for h in range(nh)], axis=1)

        hprev[...] = lax.fori_loop(0, tm // SCAN_ROWS, group, hprev[...])
        for h in range(nh):
            hs_ref[:, h * bw:(h + 1) * bw] = h_s[h]
        gel, _ = _gelu(proj_ref[:, 4 * d:5 * d])
        yb = hs_ref[...] * gel
        yap = _dot(ya.astype(BF16), wa_ref[...])
        ybp = _dot(yb.astype(BF16), wb_ref[...])
        yap_ref[...] = yap
        ybp_ref[...] = ybp
        m = _sigmoid(proj_ref[:, 5 * d:6 * d]) * yap + _sigmoid(proj_ref[:, 6 * d:7 * d]) * ybp
        y = _dot(m.astype(BF16), wo_ref[...])
        y_ref[...] = y
        x1_ref[...] = x_ref[...] + mod_ref[M_GT_M:M_GT_M + 1, :] * ((y * _rms(y)) * row(R_G_POST_MIX))

    tile = pl.BlockSpec((tm, d), lambda i: (i, 0))
    return pl.pallas_call(
        body,
        name="mixer_core_fwd",
        grid=(s // tm,),
        in_specs=[
            pl.BlockSpec((tm, 7 * d), lambda i: (i, 0)),
            tile,
            _resident((None, N_MOD, d), lambda i: (l, 0, 0)),
            _resident((None, N_ROWS, d), lambda i: (l, 0, 0)),
            _resident((None, CW_ROWS, d), lambda i: (l, 0, 0)),
            _resident((None, nh, bw, bw), lambda i: (l, 0, 0, 0)),
            _resident((None, nh, bw, bw), lambda i: (l, 0, 0, 0)),
            _resident((d, d), lambda i: (0, 0)),
            _resident((d, d), lambda i: (0, 0)),
            _resident((d, d), lambda i: (0, 0)),
            ANY_SPEC,
        ],
        out_specs=(tile,) * 5,
        out_shape=(jax.ShapeDtypeStruct((s, d), F32),) * 5,
        scratch_shapes=[
            pltpu.VMEM((SUBLANES, d), F32),
            pltpu.VMEM((SUBLANES, d), F32),
            pltpu.VMEM((nh, tm, bw), F32),
            pltpu.VMEM((nh, tm, bw), F32),
            pltpu.VMEM((nh, tm, bw), F32),
            pltpu.VMEM((SUBLANES, d), F32),
        ],
        compiler_params=_cparams(("arbitrary",), 3 * _nbytes((d, d), BF16) + 2 * _nbytes((tm, 7 * d), F32) + 40 * _nbytes((tm, d), F32)),
    )(*_hbm(proj, x), mod, rows, cwf, wr, wi, wa_f, wb_f, wo_f, after)


def _mlp_fwd(x1, mod, rows, wup_f, wdn_f, l, tm, after, target=None):
    s, d = x1.shape
    nb, _, cu = wup_f.shape
    dff = nb * cu
    with_loss = target is not None

    def body(x1_ref, mod_ref, rows_ref, wu_ref, wd_ref, after_ref, *refs):
        x2_ref, ru_ref, y2_ref, h2_ref = refs[-6:-2] if with_loss else refs
        xv = x1_ref[...]
        g = rows_ref[R_G_PRE_MLP:R_G_PRE_MLP + 1, :]
        h2 = ((xv * _rms(xv) * g) * (1.0 + mod_ref[M_SC_F:M_SC_F + 1, :]) + mod_ref[M_SH_F:M_SH_F + 1, :]).astype(BF16)
        h2_ref[...] = h2
        ru = jnp.concatenate([jnp.maximum(_dot(h2, wu_ref[j]), 0.0) for j in range(nb)], axis=1)
        ru_ref[...] = ru.astype(BF16)
        y2 = _dot((ru * ru).astype(BF16), wd_ref[...])
        y2_ref[...] = y2
        x2 = xv + mod_ref[M_GT_F:M_GT_F + 1, :] * ((y2 * _rms(y2)) * rows_ref[R_G_POST_MLP:R_G_POST_MLP + 1, :])
        x2_ref[...] = x2
        if with_loss:
            t_ref, dy_ref, loss_ref = refs[0], refs[-2], refs[-1]

            @pl.when(pl.program_id(0) == 0)
            def _():
                loss_ref[...] = jnp.zeros(loss_ref.shape, F32)

            e = x2 - t_ref[...]
            dy_ref[...] = e * (1.0 / d)
            loss_ref[...] += 0.5 * jnp.sum(jnp.mean(e * e, axis=-1, keepdims=True), axis=0, keepdims=True)

    tile = pl.BlockSpec((tm, d), lambda i: (i, 0))
    wide = pl.BlockSpec((tm, dff), lambda i: (i, 0))
    out_specs = (tile, wide, tile, tile)
    out_shape = (jax.ShapeDtypeStruct((s, d), F32), jax.ShapeDtypeStruct((s, dff), BF16),
                 jax.ShapeDtypeStruct((s, d), F32), jax.ShapeDtypeStruct((s, d), BF16))
    if with_loss:
        out_specs += (tile, pl.BlockSpec((SUBLANES, 128), lambda i: (0, 0)))
        out_shape += (jax.ShapeDtypeStruct((s, d), F32), jax.ShapeDtypeStruct((SUBLANES, 128), F32))
    return pl.pallas_call(
        body,
        name="mlp_fwd",
        grid=(s // tm,),
        in_specs=[
            tile,
            _resident((None, N_MOD, d), lambda i: (l, 0, 0)),
            _resident((None, N_ROWS, d), lambda i: (l, 0, 0)),
            _resident((nb, d, cu), lambda i: (0, 0, 0)),
            _resident((dff, d), lambda i: (0, 0)),
            ANY_SPEC,
        ] + ([tile] if with_loss else []),
        out_specs=out_specs,
        out_shape=out_shape,
        compiler_params=_cparams(("arbitrary",), 2 * _nbytes((dff, d), BF16) + 5 * _nbytes((tm, dff), F32) + 16 * _nbytes((tm, d), F32)),
    )(*_hbm(x1), mod, rows, wup_f, wdn_f, after, *(_hbm(target) if with_loss else ()))


def _mlp_bwd(dx2, x1, y2, ru, mod, rows, wup_f, wdn_f, l, tm):
    s, d = x1.shape
    nb, _, cu = wup_f.shape
    dff = nb * cu

    def body(dx2_ref, x1_ref, y2_ref, ru_ref, mod_ref, rows_ref, wu_ref, wd_ref, dx1_ref, dy2_ref, dup_ref, act_ref, sm_ref):
        @pl.when(pl.program_id(0) == 0)
        def _():
            sm_ref[...] = jnp.zeros(sm_ref.shape, F32)

        dout = dx2_ref[...]
        dy2, dgt, dgpost = _postnorm_bwd(y2_ref[...], dout, rows_ref[R_G_POST_MLP:R_G_POST_MLP + 1, :], mod_ref[M_GT_F:M_GT_F + 1, :])
        dy2b = dy2.astype(BF16)
        dy2_ref[...] = dy2b
        ruv = ru_ref[...].astype(F32)
        act_ref[...] = (ruv * ruv).astype(BF16)
        dup = (_dot_nt(dy2b, wd_ref[...]) * (2.0 * ruv)).astype(BF16)
        dup_ref[...] = dup
        dh2 = _dot_nt(dup[:, 0:cu], wu_ref[0])
        for j in range(1, nb):
            dh2 = dh2 + _dot_nt(dup[:, j * cu:(j + 1) * cu], wu_ref[j])
        dxn, dsc, dsh, dgpre = _prenorm_bwd(x1_ref[...], dh2, rows_ref[R_G_PRE_MLP:R_G_PRE_MLP + 1, :], mod_ref[M_SC_F:M_SC_F + 1, :])
        dx1_ref[...] = dout + dxn
        for r, v in ((G_MLP_GT, dgt), (G_MLP_GPOST, dgpost), (G_MLP_SC, dsc), (G_MLP_SH, dsh), (G_MLP_GPRE, dgpre)):
            sm_ref[r:r + 1, :] += v

    tile = pl.BlockSpec((tm, d), lambda i: (i, 0))
    wide = pl.BlockSpec((tm, dff), lambda i: (i, 0))
    return pl.pallas_call(
        body,
        name="mlp_bwd",
        grid=(s // tm,),
        in_specs=[
            tile, tile, tile, wide,
            _resident((None, N_MOD, d), lambda i: (l, 0, 0)),
            _resident((None, N_ROWS, d), lambda i: (l, 0, 0)),
            _resident((nb, d, cu), lambda i: (0, 0, 0)),
            _resident((dff, d), lambda i: (0, 0)),
        ],
        out_specs=(tile, tile, wide, wide, pl.BlockSpec((G_MLP_ROWS, d), lambda i: (0, 0))),
        out_shape=(jax.ShapeDtypeStruct((s, d), F32), jax.ShapeDtypeStruct((s, d), BF16), jax.ShapeDtypeStruct((s, dff), BF16),
                   jax.ShapeDtypeStruct((s, dff), BF16), jax.ShapeDtypeStruct((G_MLP_ROWS, d), F32)),
        compiler_params=_cparams(("arbitrary",), 2 * _nbytes((dff, d), BF16) + 6 * _nbytes((tm, dff), F32) + 16 * _nbytes((tm, d), F32)),
    )(*_hbm(dx2, x1, y2, ru), mod, rows, wup_f, wdn_f)


def _mixer_core_bwd(dx1, y, yap, ybp, hs, proj, mod, rows, cwf, wr, wi, wa_f, wb_f, wo_f, l, tm, after):
    s, d = dx1.shape
    nh, bw, _ = wr.shape[1:]
    nt = s // tm
    per = tm // SUBLANES

    def body(dx1_ref, y_ref, yap_ref, ybp_ref, hs_ref, hsh_ref, proj_ref, projh_ref, mod_ref, rows_ref, cw_ref,
             wr_ref, wi_ref, wa_ref, wb_ref, wo_ref, after_ref,
             dproj_ref, dy_ref, m_ref, dyap_ref, dybp_ref, ya_ref, yb_ref, sm_ref, dwg_ref,
             abuf, dcbuf, dxbuf, al_s, dh_s, lam_s, lnext):
        i = pl.program_id(0)
        first_tile = i == nt - 1

        @pl.when(i == 0)
        def _():
            sm_ref[...] = jnp.zeros(sm_ref.shape, F32)
            dwg_ref[...] = jnp.zeros(dwg_ref.shape, F32)
            zero = jnp.zeros((SUBLANES, d), F32)
            abuf[...] = zero
            dcbuf[...] = zero
            dxbuf[...] = zero
            lnext[...] = zero

        def row(r):
            return rows_ref[r:r + 1, :]

        def tap(r):
            return cw_ref[r:r + 1, :]

        def acc(r, v):
            sm_ref[r:r + 1, :] += v

        keep_halo = jnp.where(first_tile, 0.0, 1.0)
        dy, dgt, dgpost = _postnorm_bwd(y_ref[...], dx1_ref[...], row(R_G_POST_MIX), mod_ref[M_GT_M:M_GT_M + 1, :])
        acc(G_MIX_GT, dgt)
        acc(G_MIX_GPOST, dgpost)
        dyb16 = dy.astype(BF16)
        dy_ref[...] = dyb16
        dm = _dot_nt(dyb16, wo_ref[...])
        sa = _sigmoid(proj_ref[:, 5 * d:6 * d])
        sb = _sigmoid(proj_ref[:, 6 * d:7 * d])
        yap = yap_ref[...]
        ybp = ybp_ref[...]
        m_ref[...] = (sa * yap + sb * ybp).astype(BF16)
        dyap_f = dm * sa
        dybp_f = dm * sb
        dyap = dyap_f.astype(BF16)
        dybp = dybp_f.astype(BF16)
        dyap_ref[...] = dyap
        dybp_ref[...] = dybp
        dproj_ref[:, 5 * d:6 * d] = (dyap_f * yap * (1.0 - sa)).astype(BF16)
        dproj_ref[:, 6 * d:7 * d] = (dybp_f * ybp * (1.0 - sb)).astype(BF16)
        dya = _dot_nt(dyap, wa_ref[...])
        dyb = _dot_nt(dybp, wb_ref[...])
        ba = proj_ref[:, 0:d]
        ca = proj_ref[:, d:2 * d]
        va = proj_ref[:, 2 * d:3 * d]
        cv = ca * va
        cvh = keep_halo * (projh_ref[:, d:2 * d] * projh_ref[:, 2 * d:3 * d])
        cvm2 = _shift_rows(cv, cvh, 2, False)
        cvm1 = _shift_rows(cv, cvh, 1, False)
        conv3 = ((row(R_CONV_A_B) + cvm2 * tap(CW_A)) + cvm1 * tap(CW_A + 1)) + cv * tap(CW_A + 2)
        ya_ref[...] = (ba * conv3).astype(BF16)
        dproj_ref[:, 0:d] = (dya * conv3).astype(BF16)
        dc3 = dya * ba
        acc(G_MIX_CAB, _colsum(dc3))
        acc(G_MIX_CAW, _colsum(dc3 * cvm2))
        acc(G_MIX_CAW + 1, _colsum(dc3 * cvm1))
        acc(G_MIX_CAW + 2, _colsum(dc3 * cv))
        dct = dcbuf[...]
        dcv = (dc3 * tap(CW_A + 2) + _shift_rows(dc3, dct, 1, True) * tap(CW_A + 1)) + _shift_rows(dc3, dct, 2, True) * tap(CW_A)
        dcbuf[...] = dc3[:SUBLANES, :]
        dproj_ref[:, d:2 * d] = (dcv * va).astype(BF16)
        dproj_ref[:, 2 * d:3 * d] = (dcv * ca).astype(BF16)
        xb = proj_ref[:, 3 * d:4 * d]
        gb = proj_ref[:, 4 * d:5 * d]
        xbh = keep_halo * projh_ref[:, 3 * d:4 * d]
        xm3 = _shift_rows(xb, xbh, 3, False)
        xm2 = _shift_rows(xb, xbh, 2, False)
        xm1 = _shift_rows(xb, xbh, 1, False)
        xc = (((row(R_CONV_B_B) + xm3 * tap(CW_B)) + xm2 * tap(CW_B + 1)) + xm1 * tap(CW_B + 2)) + xb * tap(CW_B + 3)
        lam = row(R_LAMBDA)
        sp = _softplus_neg(lam)
        xcb, r, ig, a, mult = _gates(xc, wr_ref, wi_ref, row(R_B_GATE_R), row(R_B_GATE_I), sp, nh, bw)
        gel, gsig = _gelu(gb)
        hs = hs_ref[...]
        yb_ref[...] = (hs * gel).astype(BF16)
        dproj_ref[:, 4 * d:5 * d] = (dyb * hs * _gelu_grad(gb, gsig)).astype(BF16)
        alpha = _shift_rows(a, abuf[...], 1, True)
        abuf[...] = a[:SUBLANES, :]
        dhs = dyb * gel
        for h in range(nh):
            al_s[h] = alpha[:, h * bw:(h + 1) * bw]
            dh_s[h] = dhs[:, h * bw:(h + 1) * bw]
        groups = tm // SCAN_ROWS

        def group(j, ln):
            base = pl.multiple_of((groups - 1 - j) * SCAN_ROWS, SCAN_ROWS)
            return jnp.concatenate([_scan_group(al_s, dh_s, lam_s, h, base, ln[:, h * bw:(h + 1) * bw], True) for h in range(nh)], axis=1)

        lnext[...] = lax.fori_loop(0, groups, group, lnext[...])
        dbb = jnp.concatenate([lam_s[h] for h in range(nh)], axis=1)
        da = dbb * _shift_rows(hs, keep_halo * hsh_ref[...], 1, False)
        dbx = dbb * xc
        dmult = dbx * ig
        dig = dbx * mult
        dxc = (dbb * mult) * ig
        dla = a * (da - (dmult * a) / mult)
        dlar = dla * r
        acc(G_MIX_LAM, _colsum(dlar) * (LRU_C * _sigmoid(-lam)))
        dzr = (dlar * (1.0 - r)) * (-LRU_C * sp)
        dzi = dig * ig * (1.0 - ig)
        acc(G_MIX_BR, _colsum(dzr))
        acc(G_MIX_BI, _colsum(dzi))
        dzrb = dzr.astype(BF16)
        dzib = dzi.astype(BF16)
        back = []
        for h in range(nh):
            sl = slice(h * bw, (h + 1) * bw)
            back.append(_dot_nt(dzrb[:, sl], wr_ref[h]) + _dot_nt(dzib[:, sl], wi_ref[h]))
            dwg_ref[0, h] += _dot_tn(xcb[:, sl], dzrb[:, sl])
            dwg_ref[1, h] += _dot_tn(xcb[:, sl], dzib[:, sl])
        dxc = dxc + jnp.concatenate(back, axis=1)
        acc(G_MIX_CBB, _colsum(dxc))
        acc(G_MIX_CBW, _colsum(dxc * xm3))
        acc(G_MIX_CBW + 1, _colsum(dxc * xm2))
        acc(G_MIX_CBW + 2, _colsum(dxc * xm1))
        acc(G_MIX_CBW + 3, _colsum(dxc * xb))
        dxt = dxbuf[...]
        dxb = (((dxc * tap(CW_B + 3) + _shift_rows(dxc, dxt, 1, True) * tap(CW_B + 2)) + _shift_rows(dxc, dxt, 2, True) * tap(CW_B + 1))
               + _shift_rows(dxc, dxt, 3, True) * tap(CW_B))
        dxbuf[...] = dxc[:SUBLANES, :]
        dproj_ref[:, 3 * d:4 * d] = dxb.astype(BF16)

    def rev(i):
        return (nt - 1 - i, 0)

    def halo(i):
        return (jnp.maximum((nt - 1 - i) * per - 1, 0), 0)

    tile = pl.BlockSpec((tm, d), rev)
    return pl.pallas_call(
        body,
        name="mixer_core_bwd",
        grid=(nt,),
        in_specs=[
            tile, tile, tile, tile, tile,
            pl.BlockSpec((SUBLANES, d), halo),
            pl.BlockSpec((tm, 7 * d), rev),
            pl.BlockSpec((SUBLANES, 7 * d), halo),
            _resident((None, N_MOD, d), lambda i: (l, 0, 0)),
            _resident((None, N_ROWS, d), lambda i: (l, 0, 0)),
            _resident((None, CW_ROWS, d), lambda i: (l, 0, 0)),
            _resident((None, nh, bw, bw), lambda i: (l, 0, 0, 0)),
            _resident((None, nh, bw, bw), lambda i: (l, 0, 0, 0)),
            _resident((d, d), lambda i: (0, 0)),
            _resident((d, d), lambda i: (0, 0)),
            _resident((d, d), lambda i: (0, 0)),
            ANY_SPEC,
        ],
        out_specs=(pl.BlockSpec((tm, 7 * d), rev),) + (tile,) * 6 + (
            pl.BlockSpec((G_MIX_ROWS, d), lambda i: (0, 0)), pl.BlockSpec((2, nh, bw, bw), lambda i: (0, 0, 0, 0))),
        out_shape=(jax.ShapeDtypeStruct((s, 7 * d), BF16),) + (jax.ShapeDtypeStruct((s, d), BF16),) * 6 + (
            jax.ShapeDtypeStruct((G_MIX_ROWS, d), F32), jax.ShapeDtypeStruct((2, nh, bw, bw), F32)),
        scratch_shapes=[pltpu.VMEM((SUBLANES, d), F32)] * 3 + [pltpu.VMEM((nh, tm, bw), F32)] * 3 + [pltpu.VMEM((SUBLANES, d), F32)],
        compiler_params=_cparams(("arbitrary",), 3 * _nbytes((d, d), BF16) + 3 * _nbytes((tm, 7 * d), F32) + 64 * _nbytes((tm, d), F32)),
    )(*_hbm(dx1, y, yap, ybp, hs, hs, proj, proj), mod, rows, cwf, wr, wi, wa_f, wb_f, wo_f, after)


def _in_proj_bwd(dproj, x, dx1, mod, rows, win_f, l, tm, after):
    s, d = x.shape
    nb, _, ci = win_f.shape

    def body(dp_ref, x_ref, dx1_ref, mod_ref, rows_ref, w_hbm, after_ref, dx_ref, sm_ref, w_s, sem):
        @pl.when(pl.program_id(0) == 0)
        def _():
            sm_ref[...] = jnp.zeros(sm_ref.shape, F32)
            _load_side_by_side(w_hbm, w_s, sem)

        dh = _dot_nt(dp_ref[...], w_s[...])
        dxn, dsc, dsh, dg = _prenorm_bwd(x_ref[...], dh, rows_ref[R_G_PRE_MIX:R_G_PRE_MIX + 1, :], mod_ref[M_SC_M:M_SC_M + 1, :])
        dx_ref[...] = dx1_ref[...] + dxn
        for r, v in ((G_IN_SC, dsc), (G_IN_SH, dsh), (G_IN_GPRE, dg)):
            sm_ref[r:r + 1, :] += v

    tile = pl.BlockSpec((tm, d), lambda i: (i, 0))
    return pl.pallas_call(
        body,
        name="in_proj_bwd",
        grid=(s // tm,),
        in_specs=[
            pl.BlockSpec((tm, nb * ci), lambda i: (i, 0)), tile, tile,
            _resident((None, N_MOD, d), lambda i: (l, 0, 0)),
            _resident((None, N_ROWS, d), lambda i: (l, 0, 0)),
            ANY_SPEC,
            ANY_SPEC,
        ],
        out_specs=(tile, pl.BlockSpec((G_IN_ROWS, d), lambda i: (0, 0))),
        out_shape=(jax.ShapeDtypeStruct((s, d), F32), jax.ShapeDtypeStruct((G_IN_ROWS, d), F32)),
        scratch_shapes=[pltpu.VMEM((d, nb * ci), BF16), pltpu.SemaphoreType.DMA((nb,))],
        compiler_params=_cparams(("arbitrary",), _nbytes((nb, d, ci), BF16) + 2 * _nbytes((tm, nb * ci), BF16) + 16 * _nbytes((tm, d), F32)),
    )(*_hbm(dproj, x, dx1), mod, rows, win_f, after)


def _wgrad(a, b, cols_owned, after):
    s, k1 = a.shape
    k2 = b.shape[1]
    if cols_owned:
        c = k2 // N_DEV
        per = min(N_DEV, MXU_COLS // math.gcd(c, MXU_COLS))
        assert N_DEV % per == 0
        nblk, bk1 = N_DEV // per, k1
        a_spec = _resident((s, k1), lambda j: (0, 0))
        b_spec = pl.BlockSpec((s, per * c), lambda j: (0, j))
    else:
        per, c = 1, k2
        bk1 = min(WGRAD_ROWS, k1)
        nblk = k1 // bk1
        a_spec = pl.BlockSpec((s, bk1), lambda j: (0, j))
        b_spec = _resident((s, k2), lambda j: (0, 0))
    rows = min(WGRAD_ROWS, bk1)

    def body(a_ref, b_ref, after_ref, o_ref):
        for r0 in range(0, bk1, rows):
            g = _dot_tn(a_ref[:, r0:r0 + rows], b_ref[...]).astype(BF16)
            for p in range(per):
                o_ref[p, r0:r0 + rows, :] = g[:, p * c:(p + 1) * c]

    out = pl.pallas_call(
        body,
        name="wgrad",
        grid=(nblk,),
        in_specs=[a_spec, b_spec, ANY_SPEC],
        out_specs=pl.BlockSpec((per, bk1, c), lambda j: (j, 0, 0)),
        out_shape=pltpu.HBM((nblk * per, bk1, c), BF16),
        compiler_params=_cparams(("parallel",), 4 * _nbytes((bk1, per * c), F32) + 4 * _nbytes((s, bk1 + per * c), BF16)),
    )(pltpu.with_memory_space_constraint(a, pltpu.HBM), pltpu.with_memory_space_constraint(b, pltpu.HBM), after)
    return out if cols_owned else out.reshape(N_DEV, k1 // N_DEV, k2)


def _adam_update(w, g, m, v):
    m = ADAM_B1 * m + (1.0 - ADAM_B1) * g
    v = ADAM_B2 * v + (1.0 - ADAM_B2) * (g * g)
    m_hat = m / (1.0 - ADAM_B1 ** ADAM_STEP)
    v_hat = v / (1.0 - ADAM_B2 ** ADAM_STEP)
    delta = -ADAM_LR * (m_hat / (jnp.sqrt(v_hat) + ADAM_EPS) + ADAM_WD * w)
    return delta, m, v


def _sum_adamw(recv, w, m, v, tr, after):
    nl, ra, cb = w.shape
    assert nl == len(recv) == 2

    def body(r0_ref, r1_ref, w_ref, m_ref, v_ref, after_ref, g_ref, d_ref, nm_ref, nv_ref):
        def total(r_ref):
            g = r_ref[0].astype(F32)
            for j in range(1, N_DEV):
                g = g + r_ref[j].astype(F32)
            return g

        g = jnp.where(pl.program_id(0) == 0, total(r0_ref), total(r1_ref))
        g_ref[...] = g
        d_ref[...], nm_ref[...], nv_ref[...] = _adam_update(w_ref[...], g, m_ref[...], v_ref[...])

    blk = pl.BlockSpec((None, tr, cb), lambda l, i: (l, i, 0))
    return pl.pallas_call(
        body,
        name="sum_adamw",
        grid=(nl, ra // tr),
        in_specs=[pl.BlockSpec((N_DEV, tr, cb), lambda l, i: (0, i * (1 - l), 0)),
                  pl.BlockSpec((N_DEV, tr, cb), lambda l, i: (0, i * l, 0)), blk, blk, blk, ANY_SPEC],
        out_specs=(blk,) * 4,
        out_shape=(jax.ShapeDtypeStruct((nl, ra, cb), F32),) * 4,
        compiler_params=_cparams(("arbitrary", "arbitrary"), 6 * _nbytes((N_DEV, tr, cb), BF16) + 32 * _nbytes((tr, cb), F32)),
    )(recv[0], recv[1], w, m, v, after)


def _adamw(w, g, m, v):
    def body(w_ref, g_ref, m_ref, v_ref, d_ref, nm_ref, nv_ref):
        d_ref[...], nm_ref[...], nv_ref[...] = _adam_update(w_ref[...], g_ref[...], m_ref[...], v_ref[...])

    return pl.pallas_call(
        body,
        name="adamw",
        in_specs=[VMEM_SPEC] * 4,
        out_specs=(VMEM_SPEC,) * 3,
        out_shape=(jax.ShapeDtypeStruct(w.shape, F32),) * 3,
        compiler_params=_cparams(vmem=10 * _nbytes(w.shape, F32)),
    )(w, g, m, v)


def _adamw_tiled(w, g, m, v, tr):
    nl, ra, cb = w.shape

    def body(w_ref, g_ref, m_ref, v_ref, d_ref, nm_ref, nv_ref):
        d_ref[...], nm_ref[...], nv_ref[...] = _adam_update(w_ref[...], g_ref[...], m_ref[...], v_ref[...])

    blk = pl.BlockSpec((None, tr, cb), lambda l, i: (l, i, 0))
    return pl.pallas_call(
        body,
        name="adamw_tiled",
        grid=(nl, ra // tr),
        in_specs=[blk] * 4,
        out_specs=(blk,) * 3,
        out_shape=(jax.ShapeDtypeStruct((nl, ra, cb), F32),) * 3,
        compiler_params=_cparams(("parallel", "parallel")),
    )(*_hbm(w, g, m, v))


def _token_tile(s):
    return min(TOKEN_TILE, max(SUBLANES * 2, s // 4))


def kernel(x, c, w_mod, b_mod, g_pre_mix, g_post_mix, w_in, conv_a_w, conv_a_b, w_a_out, conv_b_w, conv_b_b, w_gate_r, b_gate_r, w_gate_i, b_gate_i, lru_lambda, w_b_out, w_o, g_pre_mlp, g_post_mlp, w_mlp_up, w_mlp_down, loss_target, m_w_mod, m_b_mod, m_g_pre_mix, m_g_post_mix, m_w_in, m_conv_a_w, m_conv_a_b, m_w_a_out, m_conv_b_w, m_conv_b_b, m_w_gate_r, m_b_gate_r, m_w_gate_i, m_b_gate_i, m_lru_lambda, m_w_b_out, m_w_o, m_g_pre_mlp, m_g_post_mlp, m_w_mlp_up, m_w_mlp_down, v_w_mod, v_b_mod, v_g_pre_mix, v_g_post_mix, v_w_in, v_conv_a_w, v_conv_a_b, v_w_a_out, v_conv_b_w, v_conv_b_b, v_w_gate_r, v_b_gate_r, v_w_gate_i, v_b_gate_i, v_lru_lambda, v_w_b_out, v_w_o, v_g_pre_mlp, v_g_post_mlp, v_w_mlp_up, v_w_mlp_down):
    nl = w_mod.shape[0]
    s, d = x.shape[1], x.shape[2]
    nh, bw = w_gate_r.shape[1], w_gate_r.shape[2]
    cwid = conv_a_w.shape[2]
    tm = _token_tile(s)
    tmx = min(2 * tm, s)
    _, me = _my_position()
    xs = x.reshape(s, d)
    target = loss_target.reshape(s, d)

    vec_names = (g_pre_mix, g_post_mix, conv_a_b, conv_b_b, b_gate_r, b_gate_i, lru_lambda, g_pre_mlp, g_post_mlp)
    rows = jnp.concatenate([jnp.stack(vec_names, axis=1), jnp.zeros((nl, N_ROWS - len(vec_names), d), F32)], axis=1)
    cw = jnp.concatenate([conv_a_w, conv_b_w, jnp.zeros((nl, CW_ROWS - 7, cwid), F32)], axis=1)

    large = {"w_in": w_in, "w_a_out": w_a_out, "w_b_out": w_b_out, "w_o": w_o, "w_mlp_up": w_mlp_up, "w_mlp_down": w_mlp_down}
    groups = (("in", ("w_in",)), ("mix", ("w_a_out", "w_b_out", "w_o")), ("mlp", ("w_mlp_up", "w_mlp_down")))
    mod, cact, cwf = _prep_small(c, w_mod, b_mod, cw)
    mod = mod.reshape(nl, N_MOD, d)
    gathers = {}
    tok = mod
    for l in range(nl):
        for gname, members in groups:
            gathers[l, gname] = _gather2_start([large[n][l].astype(BF16) for n in members], tok, f"gather_start_{gname}{l}")
            tok = gathers[l, gname][4]
    wr = w_gate_r.astype(BF16)
    wi = w_gate_i.astype(BF16)

    forwarded = {}

    def forward(l, gname, after):
        forwarded[l, gname] = _gather2_forward(gathers[l, gname], after, f"gather_forward_{gname}{l}")
        return forwarded[l, gname][4]

    def gathered(l, gname, after):
        return _gather2_wait(forwarded[l, gname], after, f"gather_wait_{gname}{l}")

    saved = []
    weights = []
    xin = xs
    for l in range(nl):
        if l == 0:
            tok = forward(0, "in", tok)
        (win_f,) = gathered(l, "in", tok if l == 0 else xin)
        if l > 0:
            tok = forward(l, "mix", win_f)
        proj, h = _in_proj_fwd(xin, mod, rows, win_f, l, tmx, tok)
        if l == 0:
            tok = forward(0, "mix", proj)
        wa_f, wb_f, wo_f = (w.reshape(d, d) for w in gathered(l, "mix", proj))
        if l > 0:
            tok = forward(l, "mlp", wo_f)
        x1, hs, yap, ybp, y = _mixer_core_fwd(proj, xin, mod, rows, cwf, wr, wi, wa_f, wb_f, wo_f, l, tm, tok)
        if l == 0:
            forward(0, "mlp", x1)
        wup_f, wdn_f = gathered(l, "mlp", x1)
        wdn_f = wdn_f.reshape(-1, d)
        if l + 1 < nl:
            tok = forward(l + 1, "in", wdn_f)
        x2, ru, y2, h2, *last = _mlp_fwd(x1, mod, rows, wup_f, wdn_f, l, tmx, tok, target if l + 1 == nl else None)
        saved.append((xin, proj, h, x1, hs, yap, ybp, y, ru, y2, h2))
        weights.append((win_f, wa_f, wb_f, wo_f, wup_f, wdn_f))
        xin = x2
    dx, loss_part = last[0], last[1][0, 0]

    scatters = {}
    small = [None] * nl
    gate_parts = [None] * nl

    def scatter(l, gname, parts):
        scatters[l, gname] = _exchange_start(parts, False, f"scatter_start_{gname}{l}")
        return scatters[l, gname][4]

    for l in reversed(range(nl)):
        xin, proj, h, x1, hs, yap, ybp, y, ru, y2, h2 = saved[l]
        win_f, wa_f, wb_f, wo_f, wup_f, wdn_f = weights[l]
        dx1, dy2, dup, act, sm_mlp = _mlp_bwd(dx, x1, y2, ru, mod, rows, wup_f, wdn_f, l, tm)
        g_up = _wgrad(h2, dup, True, dx1)
        g_dn = _wgrad(act, dy2, False, g_up)
        tok = scatter(l, "mlp", [g_up, g_dn])
        dproj, dy, m, dyap, dybp, ya, yb, sm_mix, dwg = _mixer_core_bwd(
            dx1, y, yap, ybp, hs, proj, mod, rows, cwf, wr, wi, wa_f, wb_f, wo_f, l, tm // 2, tok)
        gate_parts[l] = _exchange_start([dwg.astype(BF16)], True, f"gates_start{l}")
        g_a = _wgrad(ya, dyap, False, gate_parts[l][4])
        g_b = _wgrad(yb, dybp, False, g_a)
        g_o = _wgrad(m, dy, False, g_b)
        tok = scatter(l, "mix", [g_a, g_b, g_o])
        tok = scatter(l, "in", [_wgrad(h, dproj, True, tok)])
        dx, sm_in = _in_proj_bwd(dproj, xin, dx1, mod, rows, win_f, l, tm, tok)
        small[l] = jnp.concatenate([sm_mlp, sm_mix, sm_in], axis=0)
    grad_x = dx.reshape(x.shape)

    recv = {}
    big = {}
    moments = {"w_in": (w_in, m_w_in, v_w_in), "w_mlp_up": (w_mlp_up, m_w_mlp_up, v_w_mlp_up), "w_a_out": (w_a_out, m_w_a_out, v_w_a_out),
               "w_b_out": (w_b_out, m_w_b_out, v_w_b_out), "w_o": (w_o, m_w_o, v_w_o), "w_mlp_down": (w_mlp_down, m_w_mlp_down, v_w_mlp_down)}

    def collect(l, gname, after):
        for n, zone in zip(dict(groups)[gname], _exchange_wait(scatters[l, gname], after, False, f"scatter_wait_{gname}{l}")):
            recv[n, l] = zone

    def update(name, after):
        w, m_, v_ = moments[name]
        big[name] = _sum_adamw([recv[name, l] for l in range(nl)], w, m_, v_, min(ADAM_ROWS, w.shape[1]), after)
        return big[name][1]

    for l, gname in ((1, "mlp"), (1, "in"), (1, "mix"), (0, "mlp")):
        collect(l, gname, dx)
    done = update("w_mlp_down", update("w_mlp_up", dx))
    collect(0, "mix", done)
    for name in ("w_a_out", "w_b_out", "w_o"):
        done = update(name, done)
    collect(0, "in", done)
    done = update("w_in", done)

    lrows = jnp.concatenate(small, axis=0)
    lrows = lrows.at[G_LOSS_ROW, 0].set(loss_part)

    def lrow(a, l, r):
        return a[l * G_LAYER_ROWS + r]

    dm = jnp.stack([jnp.concatenate([lrow(lrows, l, G_MLP_ROWS + G_MIX_ROWS + G_IN_SH), lrow(lrows, l, G_MLP_ROWS + G_MIX_ROWS + G_IN_SC),
                                     lrow(lrows, l, G_MLP_ROWS + G_MIX_GT), lrow(lrows, l, G_MLP_SH), lrow(lrows, l, G_MLP_SC),
                                     lrow(lrows, l, G_MLP_GT)]) for l in range(nl)])
    dm8 = jnp.concatenate([dm[:, None, :], jnp.zeros((nl, SUBLANES - 1, N_MOD * d), F32)], axis=1)
    srows, g_w_mod = _reduce_small(lrows, dm8, cact, done)
    loss = srows[G_LOSS_ROW, 0]
    sgates = _sum_gathered([_exchange_wait(gate_parts[l], srows, True, f"gates_wait{l}")[0] for l in range(nl)])

    def srow(l, r):
        return lrow(srows, l, r)

    def per_layer(r):
        return jnp.stack([srow(l, r) for l in range(nl)])

    mix0 = G_MLP_ROWS
    in0 = G_MLP_ROWS + G_MIX_ROWS
    g_b_mod = jnp.stack([jnp.concatenate([srow(l, in0 + G_IN_SH), srow(l, in0 + G_IN_SC), srow(l, mix0 + G_MIX_GT),
                                          srow(l, G_MLP_SH), srow(l, G_MLP_SC), srow(l, G_MLP_GT)]) for l in range(nl)])
    conv_a_full = jnp.stack([jnp.stack([srow(l, mix0 + G_MIX_CAW + k) for k in range(3)]) for l in range(nl)])
    conv_b_full = jnp.stack([jnp.stack([srow(l, mix0 + G_MIX_CBW + k) for k in range(4)]) for l in range(nl)])
    grads = {
        "b_mod": g_b_mod,
        "g_pre_mix": per_layer(in0 + G_IN_GPRE),
        "g_post_mix": per_layer(mix0 + G_MIX_GPOST),
        "conv_a_w": lax.dynamic_slice_in_dim(conv_a_full, me * cwid, cwid, axis=2),
        "conv_a_b": per_layer(mix0 + G_MIX_CAB),
        "conv_b_w": lax.dynamic_slice_in_dim(conv_b_full, me * cwid, cwid, axis=2),
        "conv_b_b": per_layer(mix0 + G_MIX_CBB),
        "w_gate_r": sgates[:, 0],
        "b_gate_r": per_layer(mix0 + G_MIX_BR),
        "w_gate_i": sgates[:, 1],
        "b_gate_i": per_layer(mix0 + G_MIX_BI),
        "lru_lambda": per_layer(mix0 + G_MIX_LAM),
        "g_pre_mlp": per_layer(G_MLP_GPRE),
        "g_post_mlp": per_layer(G_MLP_GPOST),
    }
    params = {
        "b_mod": (b_mod, m_b_mod, v_b_mod), "g_pre_mix": (g_pre_mix, m_g_pre_mix, v_g_pre_mix), "g_post_mix": (g_post_mix, m_g_post_mix, v_g_post_mix),
        "conv_a_w": (conv_a_w, m_conv_a_w, v_conv_a_w), "conv_a_b": (conv_a_b, m_conv_a_b, v_conv_a_b),
        "conv_b_w": (conv_b_w, m_conv_b_w, v_conv_b_w), "conv_b_b": (conv_b_b, m_conv_b_b, v_conv_b_b),
        "w_gate_r": (w_gate_r, m_w_gate_r, v_w_gate_r), "b_gate_r": (b_gate_r, m_b_gate_r, v_b_gate_r),
        "w_gate_i": (w_gate_i, m_w_gate_i, v_w_gate_i), "b_gate_i": (b_gate_i, m_b_gate_i, v_b_gate_i),
        "lru_lambda": (lru_lambda, m_lru_lambda, v_lru_lambda), "g_pre_mlp": (g_pre_mlp, m_g_pre_mlp, v_g_pre_mlp),
        "g_post_mlp": (g_post_mlp, m_g_post_mlp, v_g_post_mlp),
    }
    out = {}
    for name, g in grads.items():
        w, m_, v_ = params[name]
        flat = (-1, w.shape[-1])
        dl, nm, nv = _adamw(w.reshape(flat), g.reshape(flat), m_.reshape(flat), v_.reshape(flat))
        out[name] = (g.reshape(w.shape), dl.reshape(w.shape), nm.reshape(w.shape), nv.reshape(w.shape))
    out["w_mod"] = (g_w_mod,) + tuple(_adamw_tiled(w_mod, g_w_mod, m_w_mod, v_w_mod, min(ADAM_ROWS, d)))
    out.update(big)

    order = ("w_mod", "b_mod", "g_pre_mix", "g_post_mix", "w_in", "conv_a_w", "conv_a_b", "w_a_out", "conv_b_w", "conv_b_b", "w_gate_r", "b_gate_r",
             "w_gate_i", "b_gate_i", "lru_lambda", "w_b_out", "w_o", "g_pre_mlp", "g_post_mlp", "w_mlp_up", "w_mlp_down")
    return (loss, grad_x) + tuple(out[n][0] for n in order) + tuple(out[n][1] for n in order) + tuple(out[n][2] for n in order) + tuple(out[n][3] for n in order)
```

```python
import math

import jax
import jax.numpy as jnp
from jax import lax
from jax.experimental import pallas as pl
from jax.experimental.pallas import tpu as pltpu

F32, BF16 = jnp.float32, jnp.bfloat16
EPS = 1e-6
LRU_C = 8.0
N_DEV = 8
N_MOD = 6
SUBLANES = 8
VMEM_BUDGET = 56 * 1024 * 1024
MXU_COLS = 256
WGRAD_ROWS = 512
TOKEN_TILE = 256
ADAM_ROWS = 256
SCAN_ROWS = SUBLANES * SUBLANES
ADAM_LR, ADAM_B1, ADAM_B2, ADAM_EPS, ADAM_WD, ADAM_STEP = 0.001, 0.9, 0.999, 1e-08, 0.01, 10
MESH = pl.DeviceIdType.MESH
VMEM_SPEC = pl.BlockSpec(memory_space=pltpu.VMEM)
ANY_SPEC = pl.BlockSpec(memory_space=pl.ANY)
HBM_SPEC = pl.BlockSpec(memory_space=pltpu.HBM)
SEM_SPEC = pl.BlockSpec(memory_space=pltpu.SEMAPHORE)
SIDE_EFFECT = pltpu.SideEffectType.DATAFLOW_SIDE_EFFECTING

R_G_PRE_MIX, R_G_POST_MIX, R_CONV_A_B, R_CONV_B_B, R_B_GATE_R, R_B_GATE_I, R_LAMBDA, R_G_PRE_MLP, R_G_POST_MLP = range(9)
N_ROWS = 16
M_SH_M, M_SC_M, M_GT_M, M_SH_F, M_SC_F, M_GT_F = range(6)
CW_A, CW_B, CW_ROWS = 0, 3, 8
G_MLP_GT, G_MLP_GPOST, G_MLP_SC, G_MLP_SH, G_MLP_GPRE, G_LOSS_ROW, G_MLP_ROWS = 0, 1, 2, 3, 4, 7, 8
(G_MIX_GT, G_MIX_GPOST, G_MIX_CAB, G_MIX_CAW, G_MIX_CBB, G_MIX_CBW, G_MIX_BR, G_MIX_BI, G_MIX_LAM) = 0, 1, 2, 3, 6, 7, 11, 12, 13
G_MIX_ROWS = 16
G_IN_SC, G_IN_SH, G_IN_GPRE, G_IN_ROWS = 0, 1, 2, 8
G_LAYER_ROWS = G_MLP_ROWS + G_MIX_ROWS + G_IN_ROWS


def _cparams(dims=None, vmem=None):
    kw = {}
    if dims is not None:
        kw["dimension_semantics"] = dims
    if vmem is not None:
        kw["vmem_limit_bytes"] = int(min(max(vmem, 16 * 1024 * 1024), VMEM_BUDGET))
    return pltpu.CompilerParams(**kw)


def _nbytes(shape, dtype):
    n = 1
    for s in shape:
        n *= s
    return n * jnp.dtype(dtype).itemsize


def _hbm(*arrays):
    return tuple(pltpu.with_memory_space_constraint(a, pltpu.HBM) for a in arrays)


def _resident(block, index_map):
    return pl.BlockSpec(block, index_map, pipeline_mode=pl.Buffered(1))


def _my_position():
    x, y, c = lax.axis_index("x"), lax.axis_index("y"), lax.axis_index("c")
    return (x, y, c), 4 * x + 2 * y + c


def _peer(pos, k):
    x, y, c = pos
    px = 1 - x if k & 4 else x
    py = 1 - y if k & 2 else y
    pc = 1 - c if k & 1 else c
    return (px, py, pc), 4 * px + 2 * py + pc


def _remote(src, dst, ssem, rsem, peer):
    return pltpu.make_async_remote_copy(src_ref=src, dst_ref=dst, send_sem=ssem, recv_sem=rsem, device_id=peer, device_id_type=MESH)


def _dot(a, b):
    return jnp.dot(a, b, preferred_element_type=F32)


def _dot_nt(a, b):
    return lax.dot_general(a, b, (((1,), (1,)), ((), ())), preferred_element_type=F32)


def _dot_tn(a, b):
    return lax.dot_general(a, b, (((0,), (0,)), ((), ())), preferred_element_type=F32)


def _colsum(v):
    return jnp.sum(v, axis=0, keepdims=True)


def _sigmoid(v):
    return jax.nn.sigmoid(v)


GELU_K, GELU_C = 0.7978845608028654, 0.044715


def _gelu(v):
    s = 1.0 / (1.0 + jnp.exp(v * (-2.0 * GELU_K - (2.0 * GELU_K * GELU_C) * (v * v))))
    return v * s, s


def _gelu_grad(v, s):
    return s * (1.0 + (v * (1.0 - s)) * (2.0 * GELU_K + (6.0 * GELU_K * GELU_C) * (v * v)))


def _neg_expm1_twice(v):
    t = jnp.tanh(v)
    return (-2.0 * t) / (1.0 - t)


def _softplus_neg(lam):
    z = -lam
    u = jnp.exp(-jnp.abs(z))
    w = 1.0 + u
    l1p = jnp.where(w == 1.0, u, jnp.log(w) * u / (w - 1.0))
    return jnp.maximum(z, 0.0) + l1p


def _rms(v):
    return lax.rsqrt(jnp.mean(v * v, axis=-1, keepdims=True) + EPS)


def _prenorm_bwd(xv, dh, g, sc):
    r = _rms(xv)
    xn = xv * r
    n = xn * g
    dsc = _colsum(dh * n)
    dsh = _colsum(dh)
    dn = dh * (1.0 + sc)
    dg = _colsum(dn * xn)
    dxn = dn * g
    dx = r * (dxn - xn * jnp.mean(dxn * xn, axis=-1, keepdims=True))
    return dx, dsc, dsh, dg


def _postnorm_bwd(yv, dout, g, gt):
    r = _rms(yv)
    yn = yv * r
    dgt = _colsum(dout * (yn * g))
    dn = dout * gt
    dg = _colsum(dn * yn)
    dyn = dn * g
    dy = r * (dyn - yn * jnp.mean(dyn * yn, axis=-1, keepdims=True))
    return dy, dgt, dg


def _gates(xc, wr_ref, wi_ref, b_r, b_i, sp, nh, bw):
    xcb = xc.astype(BF16)
    zr = jnp.concatenate([_dot(xcb[:, h * bw:(h + 1) * bw], wr_ref[h]) for h in range(nh)], axis=1) + b_r
    zi = jnp.concatenate([_dot(xcb[:, h * bw:(h + 1) * bw], wi_ref[h]) for h in range(nh)], axis=1) + b_i
    r = _sigmoid(zr)
    ig = _sigmoid(zi)
    la = (-LRU_C * r) * sp
    a = jnp.exp(la)
    mult = jnp.sqrt(_neg_expm1_twice(la))
    return xcb, r, ig, a, mult


def _shift_rows(cur, edge, k, up):
    t, dd = cur.shape
    blocks = cur.reshape(t // SUBLANES, SUBLANES, dd)
    row = lax.broadcasted_iota(jnp.int32, (1, SUBLANES, dd), 1)
    if up:
        r = pltpu.roll(blocks, SUBLANES - k, 1)
        nxt = jnp.concatenate([r[1:], pltpu.roll(edge, SUBLANES - k, 0)[None]], axis=0)
        out = jnp.where(row >= SUBLANES - k, nxt, r)
    else:
        r = pltpu.roll(blocks, k, 1)
        prv = jnp.concatenate([pltpu.roll(edge, k, 0)[None], r[:-1]], axis=0)
        out = jnp.where(row < k, prv, r)
    return out.reshape(t, dd)


def _scan_group(a_ref, b_ref, o_ref, h, base, carry, reverse):
    def rows(k):
        return pl.ds(base + k, SUBLANES, stride=SUBLANES)

    order = range(SUBLANES - 1, -1, -1) if reverse else range(SUBLANES)
    loc, prod = {}, {}
    prev = None
    for k in order:
        a, b = a_ref[h, rows(k), :], b_ref[h, rows(k), :]
        loc[k] = b if prev is None else a * loc[prev] + b
        prod[k] = a if prev is None else a * prod[prev]
        prev = k
    pa, pb = _scan_block(prod[prev], loc[prev], reverse)
    ends = pb + pa * carry
    row = lax.broadcasted_iota(jnp.int32, ends.shape, 0)
    if reverse:
        into = jnp.where(row < SUBLANES - 1, pltpu.roll(ends, SUBLANES - 1, 0), carry)
        last = ends[0:1, :]
    else:
        into = jnp.where(row >= 1, pltpu.roll(ends, 1, 0), carry)
        last = ends[SUBLANES - 1:SUBLANES, :]
    for k in range(SUBLANES):
        o_ref[h, rows(k), :] = loc[k] + prod[k] * into
    return jnp.broadcast_to(last, ends.shape)


def _scan_block(a8, b8, reverse):
    row = lax.broadcasted_iota(jnp.int32, a8.shape, 0)
    for s in (1, 2, 4):
        if reverse:
            keep = row < SUBLANES - s
            a_sh = pltpu.roll(a8, SUBLANES - s, 0)
            b_sh = pltpu.roll(b8, SUBLANES - s, 0)
        else:
            keep = row >= s
            a_sh = pltpu.roll(a8, s, 0)
            b_sh = pltpu.roll(b8, s, 0)
        b8 = b8 + a8 * jnp.where(keep, b_sh, 0.0)
        a8 = a8 * jnp.where(keep, a_sh, 1.0)
    return a8, b8


def _prep_small(c, w_mod, b_mod, cw):
    d = c.shape[1]
    cm = w_mod.shape[2]
    cwid = cw.shape[2]
    nl = w_mod.shape[0]

    def body(c_ref, wm_ref, bm_ref, cw_ref, mod_ref, cact_ref, cwf_ref, cbuf, pbuf, rbuf, ssem, rsem, lsem):
        pos, me = _my_position()
        me8 = pl.multiple_of(me * SUBLANES, SUBLANES)
        cbuf[pl.ds(me8, SUBLANES), :] = jnp.broadcast_to(c_ref[...], (SUBLANES, d))
        own_cw = pltpu.make_async_copy(cw_ref, cwf_ref.at[:, :, pl.ds(me * cwid, cwid)], lsem.at[0])
        own_cw.start()
        first = []
        for k in range(1, N_DEV):
            peer, _ = _peer(pos, k)
            rows = cbuf.at[pl.ds(me8, SUBLANES), :]
            first.append(_remote(rows, rows, ssem.at[0, k - 1], rsem.at[0, k - 1], peer))
            first.append(_remote(cw_ref, cwf_ref.at[:, :, pl.ds(me * cwid, cwid)], ssem.at[1, k - 1], rsem.at[1, k - 1], peer))
        for cp in first:
            cp.start()
        for k in range(1, N_DEV):
            peer, pj = _peer(pos, k)
            pj8 = pl.multiple_of(pj * SUBLANES, SUBLANES)
            rows = cbuf.at[pl.ds(pj8, SUBLANES), :]
            _remote(rows, rows, ssem.at[0, k - 1], rsem.at[0, k - 1], peer).wait_recv()
        cv = cbuf[...]
        cact = cv * _sigmoid(cv)
        cact_ref[...] = cact
        cb = cact.astype(BF16)
        for l in range(nl):
            pbuf[l] = _dot(cb, wm_ref[l].astype(BF16))
        own_p = pltpu.make_async_copy(pbuf.at[:, pl.ds(me8, SUBLANES), :], rbuf.at[me], lsem.at[1])
        own_p.start()
        second = []
        for k in range(1, N_DEV):
            peer, pj = _peer(pos, k)
            pj8 = pl.multiple_of(pj * SUBLANES, SUBLANES)
            second.append(_remote(pbuf.at[:, pl.ds(pj8, SUBLANES), :], rbuf.at[me], ssem.at[2, k - 1], rsem.at[2, k - 1], peer))
        for cp in second:
            cp.start()
        for k in range(1, N_DEV):
            peer, pj = _peer(pos, k)
            _remote(pbuf.at[:, pl.ds(0, SUBLANES), :], rbuf.at[pj], ssem.at[2, k - 1], rsem.at[2, k - 1], peer).wait_recv()
            _remote(cw_ref, cwf_ref.at[:, :, pl.ds(pj * cwid, cwid)], ssem.at[1, k - 1], rsem.at[1, k - 1], peer).wait_recv()
        own_p.wait()
        own_cw.wait()
        for l in range(nl):
            for j in range(N_DEV):
                mod_ref[l:l + 1, j * cm:(j + 1) * cm] = rbuf[j, l, 0:1, :] + bm_ref[l:l + 1, j * cm:(j + 1) * cm]
        for cp in first + second:
            cp.wait_send()

    return pl.pallas_call(
        body,
        name="prep_small",
        out_shape=(
            jax.ShapeDtypeStruct((nl, N_MOD * d), F32),
            jax.ShapeDtypeStruct((N_DEV * SUBLANES, d), F32),
            jax.ShapeDtypeStruct((nl, CW_ROWS, d), F32),
        ),
        in_specs=[VMEM_SPEC] * 4,
        out_specs=(VMEM_SPEC,) * 3,
        scratch_shapes=[
            pltpu.VMEM((N_DEV * SUBLANES, d), F32),
            pltpu.VMEM((nl, N_DEV * SUBLANES, cm), F32),
            pltpu.VMEM((N_DEV, nl, SUBLANES, cm), F32),
            pltpu.SemaphoreType.DMA((3, N_DEV - 1)),
            pltpu.SemaphoreType.DMA((3, N_DEV - 1)),
            pltpu.SemaphoreType.DMA((2,)),
        ],
        compiler_params=_cparams(vmem=3 * _nbytes(w_mod.shape, F32)),
    )(c, w_mod, b_mod, cw)


def _exchange_start(parts, gather, name):
    n = len(parts)
    lands = [lax.empty(((N_DEV,) + tuple(p.shape)) if gather else tuple(p.shape), p.dtype) for p in parts]

    def body(*refs):
        ins, lnd = refs[:n], refs[n:2 * n]
        ssem, rsem, token = refs[2 * n], refs[2 * n + 1], refs[-1]
        pos, me = _my_position()
        for k in range(1, N_DEV):
            peer, pj = _peer(pos, k)
            for t in range(n):
                src = ins[t] if gather else ins[t].at[pj]
                q = t * (N_DEV - 1) + k - 1
                _remote(src, lnd[t].at[me], ssem.at[q], rsem.at[q], peer).start()
        token[...] = jnp.zeros(token.shape, F32)

    out = pl.pallas_call(
        body,
        name=name,
        out_shape=(pltpu.SemaphoreType.DMA((n * (N_DEV - 1),)), pltpu.SemaphoreType.DMA((n * (N_DEV - 1),)))
        + tuple(pltpu.HBM(p.shape, p.dtype) for p in parts) + tuple(pltpu.HBM(p.shape, p.dtype) for p in lands)
        + (jax.ShapeDtypeStruct((SUBLANES, 128), F32),),
        in_specs=[HBM_SPEC] * (2 * n),
        out_specs=(SEM_SPEC, SEM_SPEC) + (HBM_SPEC,) * (2 * n) + (VMEM_SPEC,),
        input_output_aliases={i: 2 + i for i in range(2 * n)},
        compiler_params=pltpu.CompilerParams(has_side_effects=SIDE_EFFECT),
    )(*[pltpu.with_memory_space_constraint(p, pltpu.HBM) for p in list(parts) + lands])
    return out[0], out[1], list(out[2:2 + n]), list(out[2 + n:2 + 2 * n]), out[-1]


def _exchange_wait(started, after, gather, name):
    ssem, rsem, parts, lands, _ = started
    n = len(parts)

    def body(*refs):
        ins, lnd = refs[:n], refs[n:2 * n]
        ssem_ref, rsem_ref = refs[2 * n], refs[2 * n + 1]
        stage, lsem = refs[-1 - n:-1], refs[-1]
        pos, me = _my_position()
        load = []
        for t in range(n):
            src = ins[t] if gather else ins[t].at[me]
            load.append(pltpu.make_async_copy(src, stage[t], lsem.at[t]))
            load[-1].start()
        store = []
        for t in range(n):
            load[t].wait()
            store.append(pltpu.make_async_copy(stage[t], lnd[t].at[me], lsem.at[t]))
            store[-1].start()
        for k in range(1, N_DEV):
            peer, pj = _peer(pos, k)
            for t in range(n):
                src = ins[t] if gather else ins[t].at[pj]
                q = t * (N_DEV - 1) + k - 1
                _remote(src, lnd[t].at[me], ssem_ref.at[q], rsem_ref.at[q], peer).wait_send()
                _remote(src, lnd[t].at[pj], ssem_ref.at[q], rsem_ref.at[q], peer).wait_recv()
        for cp in store:
            cp.wait()

    out = pl.pallas_call(
        body,
        name=name,
        out_shape=tuple(pltpu.HBM(p.shape, p.dtype) for p in parts) + tuple(pltpu.HBM(p.shape, p.dtype) for p in lands),
        in_specs=[HBM_SPEC] * (2 * n) + [SEM_SPEC, SEM_SPEC, ANY_SPEC],
        out_specs=(HBM_SPEC,) * (2 * n),
        input_output_aliases={i: i for i in range(2 * n)},
        scratch_shapes=[pltpu.VMEM(tuple(z.shape[1:]), z.dtype) for z in lands] + [pltpu.SemaphoreType.DMA((n,))],
        compiler_params=pltpu.CompilerParams(has_side_effects=SIDE_EFFECT),
    )(*parts, *lands, ssem, rsem, after)
    return list(out[n:])


CHIP_PEERS = 3


def _chip_peer(pos, k, core):
    x, y, _ = pos
    px = 1 - x if k & 2 else x
    py = 1 - y if k & 1 else y
    return (px, py, core), 4 * px + 2 * py + core


def _gather2_start(parts, after, name):
    n = len(parts)
    lands = [lax.empty((N_DEV,) + tuple(p.shape), p.dtype) for p in parts]
    per = CHIP_PEERS + 1

    def body(*refs):
        ins, lnd = refs[:n], refs[n:2 * n]
        ssem, rsem, token = refs[2 * n + 1], refs[2 * n + 2], refs[-1]
        pos, me = _my_position()
        sibling = (pos[0], pos[1], 1 - pos[2])
        for t in range(n):
            _remote(ins[t], lnd[t].at[me], ssem.at[per * t], rsem.at[per * t], sibling).start()
        for k in range(1, per):
            peer, _ = _chip_peer(pos, k, pos[2])
            for t in range(n):
                _remote(ins[t], lnd[t].at[me], ssem.at[per * t + k], rsem.at[per * t + k], peer).start()
        token[...] = jnp.zeros(token.shape, F32)

    out = pl.pallas_call(
        body,
        name=name,
        out_shape=(pltpu.SemaphoreType.DMA((n * per,)), pltpu.SemaphoreType.DMA((n * per,)))
        + tuple(pltpu.HBM(p.shape, p.dtype) for p in parts) + tuple(pltpu.HBM(p.shape, p.dtype) for p in lands)
        + (jax.ShapeDtypeStruct((SUBLANES, 128), F32),),
        in_specs=[HBM_SPEC] * (2 * n) + [ANY_SPEC],
        out_specs=(SEM_SPEC, SEM_SPEC) + (HBM_SPEC,) * (2 * n) + (VMEM_SPEC,),
        input_output_aliases={i: 2 + i for i in range(2 * n)},
        compiler_params=pltpu.CompilerParams(has_side_effects=SIDE_EFFECT),
    )(*[pltpu.with_memory_space_constraint(p, pltpu.HBM) for p in list(parts) + lands], after)
    return out[0], out[1], list(out[2:2 + n]), list(out[2 + n:2 + 2 * n]), out[-1]


def _gather2_forward(started, after, name):
    ssem, rsem, parts, lands, _ = started
    n = len(lands)
    per = CHIP_PEERS + 1

    def body(*refs):
        lnd = refs[:n]
        rsem_a, fsend, frecv, token = refs[n], refs[2 * n + 2], refs[2 * n + 3], refs[2 * n + 4]
        pos, _ = _my_position()
        sibling = (pos[0], pos[1], 1 - pos[2])
        for k in range(1, per):
            peer, pk = _chip_peer(pos, k, pos[2])
            for t in range(n):
                block = lnd[t].at[pk]
                _remote(block, block, rsem_a.at[per * t + k], rsem_a.at[per * t + k], peer).wait_recv()
                q = CHIP_PEERS * t + k - 1
                _remote(block, block, fsend.at[q], frecv.at[q], sibling).start()
        token[...] = jnp.zeros(token.shape, F32)

    out = pl.pallas_call(
        body,
        name=name,
        out_shape=tuple(pltpu.HBM(p.shape, p.dtype) for p in lands)
        + (pltpu.SemaphoreType.DMA((n * CHIP_PEERS,)), pltpu.SemaphoreType.DMA((n * CHIP_PEERS,)),
           jax.ShapeDtypeStruct((SUBLANES, 128), F32)),
        in_specs=[HBM_SPEC] * n + [SEM_SPEC, ANY_SPEC],
        out_specs=(HBM_SPEC,) * n + (SEM_SPEC, SEM_SPEC, VMEM_SPEC),
        input_output_aliases={i: i for i in range(n)},
        compiler_params=pltpu.CompilerParams(has_side_effects=SIDE_EFFECT),
    )(*lands, rsem, after)
    return ssem, rsem, parts, list(out[:n]), out[n + 2], out[n], out[n + 1]


def _gather2_wait(forwarded, after, name):
    ssem, rsem, parts, lands, _, fsend, frecv = forwarded
    n = len(parts)
    per = CHIP_PEERS + 1

    def body(*refs):
        ins, lnd = refs[:n], refs[n:2 * n]
        ssem_a, rsem_a, fs, fr = refs[2 * n:2 * n + 4]
        stage, lsem = refs[-1 - n:-1], refs[-1]
        pos, me = _my_position()
        sibling = (pos[0], pos[1], 1 - pos[2])
        sib = 4 * pos[0] + 2 * pos[1] + 1 - pos[2]
        load = []
        for t in range(n):
            load.append(pltpu.make_async_copy(ins[t], stage[t], lsem.at[t]))
            load[-1].start()
        store = []
        for t in range(n):
            load[t].wait()
            store.append(pltpu.make_async_copy(stage[t], lnd[t].at[me], lsem.at[t]))
            store[-1].start()
        for t in range(n):
            _remote(ins[t], lnd[t].at[me], ssem_a.at[per * t], rsem_a.at[per * t], sibling).wait_send()
            _remote(ins[t], lnd[t].at[sib], ssem_a.at[per * t], rsem_a.at[per * t], sibling).wait_recv()
        for k in range(1, per):
            peer, pk = _chip_peer(pos, k, pos[2])
            _, qk = _chip_peer(pos, k, 1 - pos[2])
            for t in range(n):
                q = CHIP_PEERS * t + k - 1
                _remote(ins[t], lnd[t].at[me], ssem_a.at[per * t + k], rsem_a.at[per * t + k], peer).wait_send()
                _remote(lnd[t].at[pk], lnd[t].at[pk], fs.at[q], fr.at[q], sibling).wait_send()
                _remote(lnd[t].at[qk], lnd[t].at[qk], fs.at[q], fr.at[q], sibling).wait_recv()
        for cp in store:
            cp.wait()

    out = pl.pallas_call(
        body,
        name=name,
        out_shape=tuple(pltpu.HBM(p.shape, p.dtype) for p in parts) + tuple(pltpu.HBM(p.shape, p.dtype) for p in lands),
        in_specs=[HBM_SPEC] * (2 * n) + [SEM_SPEC] * 4 + [ANY_SPEC],
        out_specs=(HBM_SPEC,) * (2 * n),
        input_output_aliases={i: i for i in range(2 * n)},
        scratch_shapes=[pltpu.VMEM(tuple(z.shape[1:]), z.dtype) for z in lands] + [pltpu.SemaphoreType.DMA((n,))],
        compiler_params=pltpu.CompilerParams(has_side_effects=SIDE_EFFECT),
    )(*parts, *lands, ssem, rsem, fsend, frecv, after)
    return list(out[n:])


def _reduce_small(rows, dm8, cact, after):
    r, d = rows.shape
    nl = dm8.shape[0]
    cm = dm8.shape[2] // N_DEV
    lw = d // N_DEV

    def body(rows_ref, dm_ref, cact_ref, after_ref, orow_ref, owm_ref, gr, mine, dmr, ssem, rsem, lsem):
        pos, me = _my_position()
        me8 = pl.multiple_of(me * SUBLANES, SUBLANES)

        def lanes(j):
            return pl.ds(j * lw, lw)

        own = [pltpu.make_async_copy(dm_ref.at[:, :, pl.ds(me * cm, cm)], dmr.at[:, pl.ds(me8, SUBLANES), :], lsem.at[0]),
               pltpu.make_async_copy(rows_ref.at[:, lanes(me)], gr.at[me], lsem.at[1])]
        first = []
        for k in range(1, N_DEV):
            peer, pj = _peer(pos, k)
            first.append(_remote(rows_ref.at[:, lanes(pj)], gr.at[me], ssem.at[0, k - 1], rsem.at[0, k - 1], peer))
            first.append(_remote(dm_ref.at[:, :, pl.ds(pj * cm, cm)], dmr.at[:, pl.ds(me8, SUBLANES), :],
                                 ssem.at[1, k - 1], rsem.at[1, k - 1], peer))
        for cp in own + first:
            cp.start()
        for k in range(1, N_DEV):
            peer, pj = _peer(pos, k)
            pj8 = pl.multiple_of(pj * SUBLANES, SUBLANES)
            _remote(rows_ref.at[:, lanes(0)], gr.at[pj], ssem.at[0, k - 1], rsem.at[0, k - 1], peer).wait_recv()
            _remote(dm_ref.at[:, :, pl.ds(0, cm)], dmr.at[:, pl.ds(pj8, SUBLANES), :], ssem.at[1, k - 1], rsem.at[1, k - 1], peer).wait_recv()
        for cp in own:
            cp.wait()
        acc = gr[0]
        for j in range(1, N_DEV):
            acc = acc + gr[j]
        mine[...] = acc
        own_sum = pltpu.make_async_copy(mine, orow_ref.at[:, lanes(me)], lsem.at[1])
        own_sum.start()
        second = []
        for k in range(1, N_DEV):
            peer, _ = _peer(pos, k)
            second.append(_remote(mine, orow_ref.at[:, lanes(me)], ssem.at[2, k - 1], rsem.at[2, k - 1], peer))
            second[-1].start()
        cb = cact_ref[...].astype(BF16)
        for l in range(nl):
            owm_ref[l] = _dot_tn(cb, dmr[l].astype(BF16))
        for k in range(1, N_DEV):
            peer, pj = _peer(pos, k)
            _remote(mine, orow_ref.at[:, lanes(pj)], ssem.at[2, k - 1], rsem.at[2, k - 1], peer).wait_recv()
        own_sum.wait()
        for cp in first + second:
            cp.wait_send()

    return pl.pallas_call(
        body,
        name="reduce_small",
        out_shape=(jax.ShapeDtypeStruct((r, d), F32), jax.ShapeDtypeStruct((nl, d, cm), F32)),
        in_specs=[VMEM_SPEC] * 3 + [ANY_SPEC],
        out_specs=(VMEM_SPEC,) * 2,
        scratch_shapes=[
            pltpu.VMEM((N_DEV, r, lw), F32),
            pltpu.VMEM((r, lw), F32),
            pltpu.VMEM((nl, N_DEV * SUBLANES, cm), F32),
            pltpu.SemaphoreType.DMA((3, N_DEV - 1)),
            pltpu.SemaphoreType.DMA((3, N_DEV - 1)),
            pltpu.SemaphoreType.DMA((2,)),
        ],
        compiler_params=_cparams(vmem=4 * _nbytes((r, d), F32) + 6 * _nbytes((nl, d, cm), F32)),
    )(rows, dm8, cact, *_hbm(after))


def _sum_gathered(zones):
    nl = len(zones)

    def body(*refs):
        for l in range(nl):
            acc = refs[l][0].astype(F32)
            for j in range(1, N_DEV):
                acc = acc + refs[l][j].astype(F32)
            refs[nl][l] = acc

    return pl.pallas_call(
        body,
        name="sum_gathered",
        out_shape=jax.ShapeDtypeStruct((nl,) + tuple(zones[0].shape[1:]), F32),
        in_specs=[VMEM_SPEC] * nl,
        out_specs=VMEM_SPEC,
        compiler_params=_cparams(vmem=8 * nl * _nbytes(zones[0].shape, BF16)),
    )(*zones)


def _load_side_by_side(w_hbm, w_s, sem):
    nb, _, ci = w_hbm.shape
    copies = [pltpu.make_async_copy(w_hbm.at[j], w_s.at[:, j * ci:(j + 1) * ci], sem.at[j]) for j in range(nb)]
    for cp in copies:
        cp.start()
    for cp in copies:
        cp.wait()


def _in_proj_fwd(x, mod, rows, win_f, l, tm, after):
    s, d = x.shape
    nb, _, ci = win_f.shape
    n = nb * ci
    wide = min(n, ci * MXU_COLS // math.gcd(ci, MXU_COLS))

    def body(x_ref, mod_ref, rows_ref, w_hbm, after_ref, proj_ref, h_ref, w_s, sem):
        @pl.when(pl.program_id(0) == 0)
        def _():
            _load_side_by_side(w_hbm, w_s, sem)

        xv = x_ref[...]
        g = rows_ref[R_G_PRE_MIX:R_G_PRE_MIX + 1, :]
        h = (xv * _rms(xv) * g) * (1.0 + mod_ref[M_SC_M:M_SC_M + 1, :]) + mod_ref[M_SH_M:M_SH_M + 1, :]
        hb = h.astype(BF16)
        h_ref[...] = hb
        for c0 in range(0, n, wide):
            proj_ref[:, c0:c0 + wide] = _dot(hb, w_s[:, c0:c0 + wide])

    return pl.pallas_call(
        body,
        name="in_proj_fwd",
        grid=(s // tm,),
        in_specs=[
            pl.BlockSpec((tm, d), lambda i: (i, 0)),
            _resident((None, N_MOD, d), lambda i: (l, 0, 0)),
            _resident((None, N_ROWS, d), lambda i: (l, 0, 0)),
            ANY_SPEC,
            ANY_SPEC,
        ],
        out_specs=(pl.BlockSpec((tm, n), lambda i: (i, 0)), pl.BlockSpec((tm, d), lambda i: (i, 0))),
        out_shape=(jax.ShapeDtypeStruct((s, n), F32), jax.ShapeDtypeStruct((s, d), BF16)),
        scratch_shapes=[pltpu.VMEM((d, n), BF16), pltpu.SemaphoreType.DMA((nb,))],
        compiler_params=_cparams(("arbitrary",), _nbytes((d, n), BF16) + 3 * _nbytes((tm, n), F32) + 8 * _nbytes((tm, d), F32)),
    )(*_hbm(x), mod, rows, *_hbm(win_f), after)


def _mixer_core_fwd(proj, x, mod, rows, cwf, wr, wi, wa_f, wb_f, wo_f, l, tm, after):
    s, d = x.shape
    nh, bw, _ = wr.shape[1:]

    def body(proj_ref, x_ref, mod_ref, rows_ref, cw_ref, wr_ref, wi_ref, wa_ref, wb_ref, wo_ref, after_ref,
             x1_ref, hs_ref, yap_ref, ybp_ref, y_ref, cvbuf, xbbuf, a_s, b_s, h_s, hprev):
        i = pl.program_id(0)

        @pl.when(i == 0)
        def _():
            cvbuf[...] = jnp.zeros((SUBLANES, d), F32)
            xbbuf[...] = jnp.zeros((SUBLANES, d), F32)
            hprev[...] = jnp.zeros((SUBLANES, d), F32)

        def row(r):
            return rows_ref[r:r + 1, :]

        def tap(r):
            return cw_ref[r:r + 1, :]

        ba = proj_ref[:, 0:d]
        cv = proj_ref[:, d:2 * d] * proj_ref[:, 2 * d:3 * d]
        cvt = cvbuf[...]
        conv3 = ((row(R_CONV_A_B) + _shift_rows(cv, cvt, 2, False) * tap(CW_A)) + _shift_rows(cv, cvt, 1, False) * tap(CW_A + 1)) + cv * tap(CW_A + 2)
        ya = ba * conv3
        cvbuf[...] = cv[tm - SUBLANES:, :]
        sp = _softplus_neg(row(R_LAMBDA))
        for h in range(nh):
            lo, hi = h * bw, (h + 1) * bw
            xb = proj_ref[:, 3 * d + lo:3 * d + hi]
            xbt = xbbuf[:, lo:hi]

            def hrow(r):
                return rows_ref[r:r + 1, lo:hi]

            def htap(r):
                return cw_ref[r:r + 1, lo:hi]

            xc = (((hrow(R_CONV_B_B) + _shift_rows(xb, xbt, 3, False) * htap(CW_B)) + _shift_rows(xb, xbt, 2, False) * htap(CW_B + 1))
                  + _shift_rows(xb, xbt, 1, False) * htap(CW_B + 2)) + xb * htap(CW_B + 3)
            xcb = xc.astype(BF16)
            r = _sigmoid(_dot(xcb, wr_ref[h]) + hrow(R_B_GATE_R))
            ig = _sigmoid(_dot(xcb, wi_ref[h]) + hrow(R_B_GATE_I))
            la = (-LRU_C * r) * sp[:, lo:hi]
            a_s[h] = jnp.exp(la)
            b_s[h] = jnp.sqrt(_neg_expm1_twice(la)) * (ig * xc)
        xbbuf[...] = proj_ref[tm - SUBLANES:, 3 * d:4 * d]

        def group(j, hp):
            base = pl.multiple_of(j * SCAN_ROWS, SCAN_ROWS)
            return jnp.concatenate([_scan_group(a_s, b_s, h_s, h, base, hp[:, h * bw:(h + 1) * bw], False) for h in range(nh)], axis=1)

        hprev[...] = lax.fori_loop(0, tm // SCAN_ROWS, group, hprev[...])
        for h in range(nh):
            hs_ref[:, h * bw:(h + 1) * bw] = h_s[h]
        gel, _ = _gelu(proj_ref[:, 4 * d:5 * d])
        yb = hs_ref[...] * gel
        yap = _dot(ya.astype(BF16), wa_ref[...])
        ybp = _dot(yb.astype(BF16), wb_ref[...])
        yap_ref[...] = yap
        ybp_ref[...] = ybp
        m = _sigmoid(proj_ref[:, 5 * d:6 * d]) * yap + _sigmoid(proj_ref[:, 6 * d:7 * d]) * ybp
        y = _dot(m.astype(BF16), wo_ref[...])
        y_ref[...] = y
        x1_ref[...] = x_ref[...] + mod_ref[M_GT_M:M_GT_M + 1, :] * ((y * _rms(y)) * row(R_G_POST_MIX))

    tile = pl.BlockSpec((tm, d), lambda i: (i, 0))
    return pl.pallas_call(
        body,
        name="mixer_core_fwd",
        grid=(s // tm,),
        in_specs=[
            pl.BlockSpec((tm, 7 * d), lambda i: (i, 0)),
            tile,
            _resident((None, N_MOD, d), lambda i: (l, 0, 0)),
            _resident((None, N_ROWS, d), lambda i: (l, 0, 0)),
            _resident((None, CW_ROWS, d), lambda i: (l, 0, 0)),
            _resident((None, nh, bw, bw), lambda i: (l, 0, 0, 0)),
            _resident((None, nh, bw, bw), lambda i: (l, 0, 0, 0)),
            _resident((d, d), lambda i: (0, 0)),
            _resident((d, d), lambda i: (0, 0)),
            _resident((d, d), lambda i: (0, 0)),
            ANY_SPEC,
        ],
        out_specs=(tile,) * 5,
        out_shape=(jax.ShapeDtypeStruct((s, d), F32),) * 5,
        scratch_shapes=[
            pltpu.VMEM((SUBLANES, d), F32),
            pltpu.VMEM((SUBLANES, d), F32),
            pltpu.VMEM((nh, tm, bw), F32),
            pltpu.VMEM((nh, tm, bw), F32),
            pltpu.VMEM((nh, tm, bw), F32),
            pltpu.VMEM((SUBLANES, d), F32),
        ],
        compiler_params=_cparams(("arbitrary",), 3 * _nbytes((d, d), BF16) + 2 * _nbytes((tm, 7 * d), F32) + 40 * _nbytes((tm, d), F32)),
    )(*_hbm(proj, x), mod, rows, cwf, wr, wi, wa_f, wb_f, wo_f, after)


def _mlp_fwd(x1, mod, rows, wup_f, wdn_f, l, tm, after, target=None):
    s, d = x1.shape
    nb, _, cu = wup_f.shape
    dff = nb * cu
    with_loss = target is not None

    def body(x1_ref, mod_ref, rows_ref, wu_ref, wd_ref, after_ref, *refs):
        x2_ref, ru_ref, y2_ref, h2_ref = refs[-6:-2] if with_loss else refs
        xv = x1_ref[...]
        g = rows_ref[R_G_PRE_MLP:R_G_PRE_MLP + 1, :]
        h2 = ((xv * _rms(xv) * g) * (1.0 + mod_ref[M_SC_F:M_SC_F + 1, :]) + mod_ref[M_SH_F:M_SH_F + 1, :]).astype(BF16)
        h2_ref[...] = h2
        ru = jnp.concatenate([jnp.maximum(_dot(h2, wu_ref[j]), 0.0) for j in range(nb)], axis=1)
        ru_ref[...] = ru.astype(BF16)
        y2 = _dot((ru * ru).astype(BF16), wd_ref[...])
        y2_ref[...] = y2
        x2 = xv + mod_ref[M_GT_F:M_GT_F + 1, :] * ((y2 * _rms(y2)) * rows_ref[R_G_POST_MLP:R_G_POST_MLP + 1, :])
        x2_ref[...] = x2
        if with_loss:
            t_ref, dy_ref, loss_ref = refs[0], refs[-2], refs[-1]

            @pl.when(pl.program_id(0) == 0)
            def _():
                loss_ref[...] = jnp.zeros(loss_ref.shape, F32)

            e = x2 - t_ref[...]
            dy_ref[...] = e * (1.0 / d)
            loss_ref[...] += 0.5 * jnp.sum(jnp.mean(e * e, axis=-1, keepdims=True), axis=0, keepdims=True)

    tile = pl.BlockSpec((tm, d), lambda i: (i, 0))
    wide = pl.BlockSpec((tm, dff), lambda i: (i, 0))
    out_specs = (tile, wide, tile, tile)
    out_shape = (jax.ShapeDtypeStruct((s, d), F32), jax.ShapeDtypeStruct((s, dff), BF16),
                 jax.ShapeDtypeStruct((s, d), F32), jax.ShapeDtypeStruct((s, d), BF16))
    if with_loss:
        out_specs += (tile, pl.BlockSpec((SUBLANES, 128), lambda i: (0, 0)))
        out_shape += (jax.ShapeDtypeStruct((s, d), F32), jax.ShapeDtypeStruct((SUBLANES, 128), F32))
    return pl.pallas_call(
        body,
        name="mlp_fwd",
        grid=(s // tm,),
        in_specs=[
            tile,
            _resident((None, N_MOD, d), lambda i: (l, 0, 0)),
            _resident((None, N_ROWS, d), lambda i: (l, 0, 0)),
            _resident((nb, d, cu), lambda i: (0, 0, 0)),
            _resident((dff, d), lambda i: (0, 0)),
            ANY_SPEC,
        ] + ([tile] if with_loss else []),
        out_specs=out_specs,
        out_shape=out_shape,
        compiler_params=_cparams(("arbitrary",), 2 * _nbytes((dff, d), BF16) + 5 * _nbytes((tm, dff), F32) + 16 * _nbytes((tm, d), F32)),
    )(*_hbm(x1), mod, rows, wup_f, wdn_f, after, *(_hbm(target) if with_loss else ()))


def _mlp_bwd(dx2, x1, y2, ru, mod, rows, wup_f, wdn_f, l, tm):
    s, d = x1.shape
    nb, _, cu = wup_f.shape
    dff = nb * cu

    def body(dx2_ref, x1_ref, y2_ref, ru_ref, mod_ref, rows_ref, wu_ref, wd_ref, dx1_ref, dy2_ref, dup_ref, act_ref, sm_ref):
        @pl.when(pl.program_id(0) == 0)
        def _():
            sm_ref[...] = jnp.zeros(sm_ref.shape, F32)

        dout = dx2_ref[...]
        dy2, dgt, dgpost = _postnorm_bwd(y2_ref[...], dout, rows_ref[R_G_POST_MLP:R_G_POST_MLP + 1, :], mod_ref[M_GT_F:M_GT_F + 1, :])
        dy2b = dy2.astype(BF16)
        dy2_ref[...] = dy2b
        ruv = ru_ref[...].astype(F32)
        act_ref[...] = (ruv * ruv).astype(BF16)
        dup = (_dot_nt(dy2b, wd_ref[...]) * (2.0 * ruv)).astype(BF16)
        dup_ref[...] = dup
        dh2 = _dot_nt(dup[:, 0:cu], wu_ref[0])
        for j in range(1, nb):
            dh2 = dh2 + _dot_nt(dup[:, j * cu:(j + 1) * cu], wu_ref[j])
        dxn, dsc, dsh, dgpre = _prenorm_bwd(x1_ref[...], dh2, rows_ref[R_G_PRE_MLP:R_G_PRE_MLP + 1, :], mod_ref[M_SC_F:M_SC_F + 1, :])
        dx1_ref[...] = dout + dxn
        for r, v in ((G_MLP_GT, dgt), (G_MLP_GPOST, dgpost), (G_MLP_SC, dsc), (G_MLP_SH, dsh), (G_MLP_GPRE, dgpre)):
            sm_ref[r:r + 1, :] += v

    tile = pl.BlockSpec((tm, d), lambda i: (i, 0))
    wide = pl.BlockSpec((tm, dff), lambda i: (i, 0))
    return pl.pallas_call(
        body,
        name="mlp_bwd",
        grid=(s // tm,),
        in_specs=[
            tile, tile, tile, wide,
            _resident((None, N_MOD, d), lambda i: (l, 0, 0)),
            _resident((None, N_ROWS, d), lambda i: (l, 0, 0)),
            _resident((nb, d, cu), lambda i: (0, 0, 0)),
            _resident((dff, d), lambda i: (0, 0)),
        ],
        out_specs=(tile, tile, wide, wide, pl.BlockSpec((G_MLP_ROWS, d), lambda i: (0, 0))),
        out_shape=(jax.ShapeDtypeStruct((s, d), F32), jax.ShapeDtypeStruct((s, d), BF16), jax.ShapeDtypeStruct((s, dff), BF16),
                   jax.ShapeDtypeStruct((s, dff), BF16), jax.ShapeDtypeStruct((G_MLP_ROWS, d), F32)),
        compiler_params=_cparams(("arbitrary",), 2 * _nbytes((dff, d), BF16) + 6 * _nbytes((tm, dff), F32) + 16 * _nbytes((tm, d), F32)),
    )(*_hbm(dx2, x1, y2, ru), mod, rows, wup_f, wdn_f)


def _mixer_core_bwd(dx1, y, yap, ybp, hs, proj, mod, rows, cwf, wr, wi, wa_f, wb_f, wo_f, l, tm, after):
    s, d = dx1.shape
    nh, bw, _ = wr.shape[1:]
    nt = s // tm
    per = tm // SUBLANES

    def body(dx1_ref, y_ref, yap_ref, ybp_ref, hs_ref, hsh_ref, proj_ref, projh_ref, mod_ref, rows_ref, cw_ref,
             wr_ref, wi_ref, wa_ref, wb_ref, wo_ref, after_ref,
             dproj_ref, dy_ref, m_ref, dyap_ref, dybp_ref, ya_ref, yb_ref, sm_ref, dwg_ref,
             abuf, dcbuf, dxbuf, al_s, dh_s, lam_s, lnext):
        i = pl.program_id(0)
        first_tile = i == nt - 1

        @pl.when(i == 0)
        def _():
            sm_ref[...] = jnp.zeros(sm_ref.shape, F32)
            dwg_ref[...] = jnp.zeros(dwg_ref.shape, F32)
            zero = jnp.zeros((SUBLANES, d), F32)
            abuf[...] = zero
            dcbuf[...] = zero
            dxbuf[...] = zero
            lnext[...] = zero

        def row(r):
            return rows_ref[r:r + 1, :]

        def tap(r):
            return cw_ref[r:r + 1, :]

        def acc(r, v):
            sm_ref[r:r + 1, :] += v

        keep_halo = jnp.where(first_tile, 0.0, 1.0)
        dy, dgt, dgpost = _postnorm_bwd(y_ref[...], dx1_ref[...], row(R_G_POST_MIX), mod_ref[M_GT_M:M_GT_M + 1, :])
        acc(G_MIX_GT, dgt)
        acc(G_MIX_GPOST, dgpost)
        dyb16 = dy.astype(BF16)
        dy_ref[...] = dyb16
        dm = _dot_nt(dyb16, wo_ref[...])
        sa = _sigmoid(proj_ref[:, 5 * d:6 * d])
        sb = _sigmoid(proj_ref[:, 6 * d:7 * d])
        yap = yap_ref[...]
        ybp = ybp_ref[...]
        m_ref[...] = (sa * yap + sb * ybp).astype(BF16)
        dyap_f = dm * sa
        dybp_f = dm * sb
        dyap = dyap_f.astype(BF16)
        dybp = dybp_f.astype(BF16)
        dyap_ref[...] = dyap
        dybp_ref[...] = dybp
        dproj_ref[:, 5 * d:6 * d] = (dyap_f * yap * (1.0 - sa)).astype(BF16)
        dproj_ref[:, 6 * d:7 * d] = (dybp_f * ybp * (1.0 - sb)).astype(BF16)
        dya = _dot_nt(dyap, wa_ref[...])
        dyb = _dot_nt(dybp, wb_ref[...])
        ba = proj_ref[:, 0:d]
        ca = proj_ref[:, d:2 * d]
        va = proj_ref[:, 2 * d:3 * d]
        cv = ca * va
        cvh = keep_halo * (projh_ref[:, d:2 * d] * projh_ref[:, 2 * d:3 * d])
        cvm2 = _shift_rows(cv, cvh, 2, False)
        cvm1 = _shift_rows(cv, cvh, 1, False)
        conv3 = ((row(R_CONV_A_B) + cvm2 * tap(CW_A)) + cvm1 * tap(CW_A + 1)) + cv * tap(CW_A + 2)
        ya_ref[...] = (ba * conv3).astype(BF16)
        dproj_ref[:, 0:d] = (dya * conv3).astype(BF16)
        dc3 = dya * ba
        acc(G_MIX_CAB, _colsum(dc3))
        acc(G_MIX_CAW, _colsum(dc3 * cvm2))
        acc(G_MIX_CAW + 1, _colsum(dc3 * cvm1))
        acc(G_MIX_CAW + 2, _colsum(dc3 * cv))
        dct = dcbuf[...]
        dcv = (dc3 * tap(CW_A + 2) + _shift_rows(dc3, dct, 1, True) * tap(CW_A + 1)) + _shift_rows(dc3, dct, 2, True) * tap(CW_A)
        dcbuf[...] = dc3[:SUBLANES, :]
        dproj_ref[:, d:2 * d] = (dcv * va).astype(BF16)
        dproj_ref[:, 2 * d:3 * d] = (dcv * ca).astype(BF16)
        xb = proj_ref[:, 3 * d:4 * d]
        gb = proj_ref[:, 4 * d:5 * d]
        xbh = keep_halo * projh_ref[:, 3 * d:4 * d]
        xm3 = _shift_rows(xb, xbh, 3, False)
        xm2 = _shift_rows(xb, xbh, 2, False)
        xm1 = _shift_rows(xb, xbh, 1, False)
        xc = (((row(R_CONV_B_B) + xm3 * tap(CW_B)) + xm2 * tap(CW_B + 1)) + xm1 * tap(CW_B + 2)) + xb * tap(CW_B + 3)
        lam = row(R_LAMBDA)
        sp = _softplus_neg(lam)
        xcb, r, ig, a, mult = _gates(xc, wr_ref, wi_ref, row(R_B_GATE_R), row(R_B_GATE_I), sp, nh, bw)
        gel, gsig = _gelu(gb)
        hs = hs_ref[...]
        yb_ref[...] = (hs * gel).astype(BF16)
        dproj_ref[:, 4 * d:5 * d] = (dyb * hs * _gelu_grad(gb, gsig)).astype(BF16)
        alpha = _shift_rows(a, abuf[...], 1, True)
        abuf[...] = a[:SUBLANES, :]
        dhs = dyb * gel
        for h in range(nh):
            al_s[h] = alpha[:, h * bw:(h + 1) * bw]
            dh_s[h] = dhs[:, h * bw:(h + 1) * bw]
        groups = tm // SCAN_ROWS

        def group(j, ln):
            base = pl.multiple_of((groups - 1 - j) * SCAN_ROWS, SCAN_ROWS)
            return jnp.concatenate([_scan_group(al_s, dh_s, lam_s, h, base, ln[:, h * bw:(h + 1) * bw], True) for h in range(nh)], axis=1)

        lnext[...] = lax.fori_loop(0, groups, group, lnext[...])
        dbb = jnp.concatenate([lam_s[h] for h in range(nh)], axis=1)
        da = dbb * _shift_rows(hs, keep_halo * hsh_ref[...], 1, False)
        dbx = dbb * xc
        dmult = dbx * ig
        dig = dbx * mult
        dxc = (dbb * mult) * ig
        dla = a * (da - (dmult * a) / mult)
        dlar = dla * r
        acc(G_MIX_LAM, _colsum(dlar) * (LRU_C * _sigmoid(-lam)))
        dzr = (dlar * (1.0 - r)) * (-LRU_C * sp)
        dzi = dig * ig * (1.0 - ig)
        acc(G_MIX_BR, _colsum(dzr))
        acc(G_MIX_BI, _colsum(dzi))
        dzrb = dzr.astype(BF16)
        dzib = dzi.astype(BF16)
        back = []
        for h in range(nh):
            sl = slice(h * bw, (h + 1) * bw)
            back.append(_dot_nt(dzrb[:, sl], wr_ref[h]) + _dot_nt(dzib[:, sl], wi_ref[h]))
            dwg_ref[0, h] += _dot_tn(xcb[:, sl], dzrb[:, sl])
            dwg_ref[1, h] += _dot_tn(xcb[:, sl], dzib[:, sl])
        dxc = dxc + jnp.concatenate(back, axis=1)
        acc(G_MIX_CBB, _colsum(dxc))
        acc(G_MIX_CBW, _colsum(dxc * xm3))
        acc(G_MIX_CBW + 1, _colsum(dxc * xm2))
        acc(G_MIX_CBW + 2, _colsum(dxc * xm1))
        acc(G_MIX_CBW + 3, _colsum(dxc * xb))
        dxt = dxbuf[...]
        dxb = (((dxc * tap(CW_B + 3) + _shift_rows(dxc, dxt, 1, True) * tap(CW_B + 2)) + _shift_rows(dxc, dxt, 2, True) * tap(CW_B + 1))
               + _shift_rows(dxc, dxt, 3, True) * tap(CW_B))
        dxbuf[...] = dxc[:SUBLANES, :]
        dproj_ref[:, 3 * d:4 * d] = dxb.astype(BF16)

    def rev(i):
        return (nt - 1 - i, 0)

    def halo(i):
        return (jnp.maximum((nt - 1 - i) * per - 1, 0), 0)

    tile = pl.BlockSpec((tm, d), rev)
    return pl.pallas_call(
        body,
        name="mixer_core_bwd",
        grid=(nt,),
        in_specs=[
            tile, tile, tile, tile, tile,
            pl.BlockSpec((SUBLANES, d), halo),
            pl.BlockSpec((tm, 7 * d), rev),
            pl.BlockSpec((SUBLANES, 7 * d), halo),
            _resident((None, N_MOD, d), lambda i: (l, 0, 0)),
            _resident((None, N_ROWS, d), lambda i: (l, 0, 0)),
            _resident((None, CW_ROWS, d), lambda i: (l, 0, 0)),
            _resident((None, nh, bw, bw), lambda i: (l, 0, 0, 0)),
            _resident((None, nh, bw, bw), lambda i: (l, 0, 0, 0)),
            _resident((d, d), lambda i: (0, 0)),
            _resident((d, d), lambda i: (0, 0)),
            _resident((d, d), lambda i: (0, 0)),
            ANY_SPEC,
        ],
        out_specs=(pl.BlockSpec((tm, 7 * d), rev),) + (tile,) * 6 + (
            pl.BlockSpec((G_MIX_ROWS, d), lambda i: (0, 0)), pl.BlockSpec((2, nh, bw, bw), lambda i: (0, 0, 0, 0))),
        out_shape=(jax.ShapeDtypeStruct((s, 7 * d), BF16),) + (jax.ShapeDtypeStruct((s, d), BF16),) * 6 + (
            jax.ShapeDtypeStruct((G_MIX_ROWS, d), F32), jax.ShapeDtypeStruct((2, nh, bw, bw), F32)),
        scratch_shapes=[pltpu.VMEM((SUBLANES, d), F32)] * 3 + [pltpu.VMEM((nh, tm, bw), F32)] * 3 + [pltpu.VMEM((SUBLANES, d), F32)],
        compiler_params=_cparams(("arbitrary",), 3 * _nbytes((d, d), BF16) + 3 * _nbytes((tm, 7 * d), F32) + 64 * _nbytes((tm, d), F32)),
    )(*_hbm(dx1, y, yap, ybp, hs, hs, proj, proj), mod, rows, cwf, wr, wi, wa_f, wb_f, wo_f, after)


def _in_proj_bwd(dproj, x, dx1, mod, rows, win_f, l, tm, after):
    s, d = x.shape
    nb, _, ci = win_f.shape

    def body(dp_ref, x_ref, dx1_ref, mod_ref, rows_ref, w_hbm, after_ref, dx_ref, sm_ref, w_s, sem):
        @pl.when(pl.program_id(0) == 0)
        def _():
            sm_ref[...] = jnp.zeros(sm_ref.shape, F32)
            _load_side_by_side(w_hbm, w_s, sem)

        dh = _dot_nt(dp_ref[...], w_s[...])
        dxn, dsc, dsh, dg = _prenorm_bwd(x_ref[...], dh, rows_ref[R_G_PRE_MIX:R_G_PRE_MIX + 1, :], mod_ref[M_SC_M:M_SC_M + 1, :])
        dx_ref[...] = dx1_ref[...] + dxn
        for r, v in ((G_IN_SC, dsc), (G_IN_SH, dsh), (G_IN_GPRE, dg)):
            sm_ref[r:r + 1, :] += v

    tile = pl.BlockSpec((tm, d), lambda i: (i, 0))
    return pl.pallas_call(
        body,
        name="in_proj_bwd",
        grid=(s // tm,),
        in_specs=[
            pl.BlockSpec((tm, nb * ci), lambda i: (i, 0)), tile, tile,
            _resident((None, N_MOD, d), lambda i: (l, 0, 0)),
            _resident((None, N_ROWS, d), lambda i: (l, 0, 0)),
            ANY_SPEC,
            ANY_SPEC,
        ],
        out_specs=(tile, pl.BlockSpec((G_IN_ROWS, d), lambda i: (0, 0))),
        out_shape=(jax.ShapeDtypeStruct((s, d), F32), jax.ShapeDtypeStruct((G_IN_ROWS, d), F32)),
        scratch_shapes=[pltpu.VMEM((d, nb * ci), BF16), pltpu.SemaphoreType.DMA((nb,))],
        compiler_params=_cparams(("arbitrary",), _nbytes((nb, d, ci), BF16) + 2 * _nbytes((tm, nb * ci), BF16) + 16 * _nbytes((tm, d), F32)),
    )(*_hbm(dproj, x, dx1), mod, rows, *_hbm(win_f), after)


def _wgrad(a, b, cols_owned, after):
    s, k1 = a.shape
    k2 = b.shape[1]
    if cols_owned:
        c = k2 // N_DEV
        per = min(N_DEV, MXU_COLS // math.gcd(c, MXU_COLS))
        assert N_DEV % per == 0
        nblk, bk1 = N_DEV // per, k1
        a_spec = _resident((s, k1), lambda j: (0, 0))
        b_spec = pl.BlockSpec((s, per * c), lambda j: (0, j))
    else:
        per, c = 1, k2
        bk1 = min(WGRAD_ROWS, k1)
        nblk = k1 // bk1
        a_spec = pl.BlockSpec((s, bk1), lambda j: (0, j))
        b_spec = _resident((s, k2), lambda j: (0, 0))
    rows = min(WGRAD_ROWS, bk1)

    def body(a_ref, b_ref, after_ref, o_ref):
        for r0 in range(0, bk1, rows):
            g = _dot_tn(a_ref[:, r0:r0 + rows], b_ref[...]).astype(BF16)
            for p in range(per):
                o_ref[p, r0:r0 + rows, :] = g[:, p * c:(p + 1) * c]

    out = pl.pallas_call(
        body,
        name="wgrad",
        grid=(nblk,),
        in_specs=[a_spec, b_spec, ANY_SPEC],
        out_specs=pl.BlockSpec((per, bk1, c), lambda j: (j, 0, 0)),
        out_shape=pltpu.HBM((nblk * per, bk1, c), BF16),
        compiler_params=_cparams(("parallel",), 4 * _nbytes((bk1, per * c), F32) + 4 * _nbytes((s, bk1 + per * c), BF16)),
    )(pltpu.with_memory_space_constraint(a, pltpu.HBM), pltpu.with_memory_space_constraint(b, pltpu.HBM), after)
    return out if cols_owned else out.reshape(N_DEV, k1 // N_DEV, k2)


def _adam_update(w, g, m, v):
    m = ADAM_B1 * m + (1.0 - ADAM_B1) * g
    v = ADAM_B2 * v + (1.0 - ADAM_B2) * (g * g)
    m_hat = m / (1.0 - ADAM_B1 ** ADAM_STEP)
    v_hat = v / (1.0 - ADAM_B2 ** ADAM_STEP)
    delta = -ADAM_LR * (m_hat / (jnp.sqrt(v_hat) + ADAM_EPS) + ADAM_WD * w)
    return delta, m, v


def _sum_adamw(recv, w, m, v, tr, after):
    nl, ra, cb = w.shape
    assert nl == len(recv) == 2

    def body(r0_ref, r1_ref, w_ref, m_ref, v_ref, after_ref, g_ref, d_ref, nm_ref, nv_ref):
        def total(r_ref):
            g = r_ref[0].astype(F32)
            for j in range(1, N_DEV):
                g = g + r_ref[j].astype(F32)
            return g

        g = jnp.where(pl.program_id(0) == 0, total(r0_ref), total(r1_ref))
        g_ref[...] = g
        d_ref[...], nm_ref[...], nv_ref[...] = _adam_update(w_ref[...], g, m_ref[...], v_ref[...])

    blk = pl.BlockSpec((None, tr, cb), lambda l, i: (l, i, 0))
    return pl.pallas_call(
        body,
        name="sum_adamw",
        grid=(nl, ra // tr),
        in_specs=[pl.BlockSpec((N_DEV, tr, cb), lambda l, i: (0, i * (1 - l), 0)),
                  pl.BlockSpec((N_DEV, tr, cb), lambda l, i: (0, i * l, 0)), blk, blk, blk, ANY_SPEC],
        out_specs=(blk,) * 4,
        out_shape=(jax.ShapeDtypeStruct((nl, ra, cb), F32),) * 4,
        compiler_params=_cparams(("arbitrary", "arbitrary"), 6 * _nbytes((N_DEV, tr, cb), BF16) + 32 * _nbytes((tr, cb), F32)),
    )(recv[0], recv[1], w, m, v, *_hbm(after))


def _adamw(w, g, m, v):
    def body(w_ref, g_ref, m_ref, v_ref, d_ref, nm_ref, nv_ref):
        d_ref[...], nm_ref[...], nv_ref[...] = _adam_update(w_ref[...], g_ref[...], m_ref[...], v_ref[...])

    return pl.pallas_call(
        body,
        name="adamw",
        in_specs=[VMEM_SPEC] * 4,
        out_specs=(VMEM_SPEC,) * 3,
        out_shape=(jax.ShapeDtypeStruct(w.shape, F32),) * 3,
        compiler_params=_cparams(vmem=10 * _nbytes(w.shape, F32)),
    )(w, g, m, v)


def _adamw_tiled(w, g, m, v, tr):
    nl, ra, cb = w.shape

    def body(w_ref, g_ref, m_ref, v_ref, d_ref, nm_ref, nv_ref):
        d_ref[...], nm_ref[...], nv_ref[...] = _adam_update(w_ref[...], g_ref[...], m_ref[...], v_ref[...])

    blk = pl.BlockSpec((None, tr, cb), lambda l, i: (l, i, 0))
    return pl.pallas_call(
        body,
        name="adamw_tiled",
        grid=(nl, ra // tr),
        in_specs=[blk] * 4,
        out_specs=(blk,) * 3,
        out_shape=(jax.ShapeDtypeStruct((nl, ra, cb), F32),) * 3,
        compiler_params=_cparams(("parallel", "parallel")),
    )(*_hbm(w, g, m, v))


def _token_tile(s):
    return min(TOKEN_TILE, max(SUBLANES * 2, s // 4))


def kernel(x, c, w_mod, b_mod, g_pre_mix, g_post_mix, w_in, conv_a_w, conv_a_b, w_a_out, conv_b_w, conv_b_b, w_gate_r, b_gate_r, w_gate_i, b_gate_i, lru_lambda, w_b_out, w_o, g_pre_mlp, g_post_mlp, w_mlp_up, w_mlp_down, loss_target, m_w_mod, m_b_mod, m_g_pre_mix, m_g_post_mix, m_w_in, m_conv_a_w, m_conv_a_b, m_w_a_out, m_conv_b_w, m_conv_b_b, m_w_gate_r, m_b_gate_r, m_w_gate_i, m_b_gate_i, m_lru_lambda, m_w_b_out, m_w_o, m_g_pre_mlp, m_g_post_mlp, m_w_mlp_up, m_w_mlp_down, v_w_mod, v_b_mod, v_g_pre_mix, v_g_post_mix, v_w_in, v_conv_a_w, v_conv_a_b, v_w_a_out, v_conv_b_w, v_conv_b_b, v_w_gate_r, v_b_gate_r, v_w_gate_i, v_b_gate_i, v_lru_lambda, v_w_b_out, v_w_o, v_g_pre_mlp, v_g_post_mlp, v_w_mlp_up, v_w_mlp_down):
    nl = w_mod.shape[0]
    s, d = x.shape[1], x.shape[2]
    nh, bw = w_gate_r.shape[1], w_gate_r.shape[2]
    cwid = conv_a_w.shape[2]
    tm = _token_tile(s)
    tmx = min(2 * tm, s)
    _, me = _my_position()
    xs = x.reshape(s, d)
    target = loss_target.reshape(s, d)

    vec_names = (g_pre_mix, g_post_mix, conv_a_b, conv_b_b, b_gate_r, b_gate_i, lru_lambda, g_pre_mlp, g_post_mlp)
    rows = jnp.concatenate([jnp.stack(vec_names, axis=1), jnp.zeros((nl, N_ROWS - len(vec_names), d), F32)], axis=1)
    cw = jnp.concatenate([conv_a_w, conv_b_w, jnp.zeros((nl, CW_ROWS - 7, cwid), F32)], axis=1)

    large = {"w_in": w_in, "w_a_out": w_a_out, "w_b_out": w_b_out, "w_o": w_o, "w_mlp_up": w_mlp_up, "w_mlp_down": w_mlp_down}
    groups = (("in", ("w_in",)), ("mix", ("w_a_out", "w_b_out", "w_o")), ("mlp", ("w_mlp_up", "w_mlp_down")))
    mod, cact, cwf = _prep_small(c, w_mod, b_mod, cw)
    mod = mod.reshape(nl, N_MOD, d)
    gathers = {}
    tok = mod
    for l in range(nl):
        for gname, members in groups:
            gathers[l, gname] = _gather2_start([large[n][l].astype(BF16) for n in members], tok, f"gather_start_{gname}{l}")
            tok = gathers[l, gname][4]
    wr = w_gate_r.astype(BF16)
    wi = w_gate_i.astype(BF16)

    forwarded = {}

    def forward(l, gname, after):
        forwarded[l, gname] = _gather2_forward(gathers[l, gname], after, f"gather_forward_{gname}{l}")
        return forwarded[l, gname][4]

    def gathered(l, gname, after):
        return _gather2_wait(forwarded[l, gname], after, f"gather_wait_{gname}{l}")

    saved = []
    weights = []
    xin = xs
    for l in range(nl):
        if l == 0:
            tok = forward(0, "in", tok)
        (win_f,) = gathered(l, "in", tok if l == 0 else xin)
        if l > 0:
            tok = forward(l, "mix", win_f)
        proj, h = _in_proj_fwd(xin, mod, rows, win_f, l, tmx, tok)
        if l == 0:
            tok = forward(0, "mix", proj)
        wa_f, wb_f, wo_f = (w.reshape(d, d) for w in gathered(l, "mix", proj))
        if l > 0:
            tok = forward(l, "mlp", wo_f)
        x1, hs, yap, ybp, y = _mixer_core_fwd(proj, xin, mod, rows, cwf, wr, wi, wa_f, wb_f, wo_f, l, tm, tok)
        if l == 0:
            forward(0, "mlp", x1)
        wup_f, wdn_f = gathered(l, "mlp", x1)
        wdn_f = wdn_f.reshape(-1, d)
        if l + 1 < nl:
            tok = forward(l + 1, "in", wdn_f)
        x2, ru, y2, h2, *last = _mlp_fwd(x1, mod, rows, wup_f, wdn_f, l, tmx, tok, target if l + 1 == nl else None)
        saved.append((xin, proj, h, x1, hs, yap, ybp, y, ru, y2, h2))
        weights.append((win_f, wa_f, wb_f, wo_f, wup_f, wdn_f))
        xin = x2
    dx, loss_part = last[0], last[1][0, 0]

    scatters = {}
    small = [None] * nl
    gate_parts = [None] * nl

    def scatter(l, gname, parts):
        scatters[l, gname] = _exchange_start(parts, False, f"scatter_start_{gname}{l}")
        return scatters[l, gname][4]

    for l in reversed(range(nl)):
        xin, proj, h, x1, hs, yap, ybp, y, ru, y2, h2 = saved[l]
        win_f, wa_f, wb_f, wo_f, wup_f, wdn_f = weights[l]
        dx1, dy2, dup, act, sm_mlp = _mlp_bwd(dx, x1, y2, ru, mod, rows, wup_f, wdn_f, l, tm)
        g_up = _wgrad(h2, dup, True, dx1)
        g_dn = _wgrad(act, dy2, False, g_up)
        tok = scatter(l, "mlp", [g_up, g_dn])
        dproj, dy, m, dyap, dybp, ya, yb, sm_mix, dwg = _mixer_core_bwd(
            dx1, y, yap, ybp, hs, proj, mod, rows, cwf, wr, wi, wa_f, wb_f, wo_f, l, tm // 2, tok)
        gate_parts[l] = _exchange_start([dwg.astype(BF16)], True, f"gates_start{l}")
        g_a = _wgrad(ya, dyap, False, gate_parts[l][4])
        g_b = _wgrad(yb, dybp, False, g_a)
        g_o = _wgrad(m, dy, False, g_b)
        tok = scatter(l, "mix", [g_a, g_b, g_o])
        tok = scatter(l, "in", [_wgrad(h, dproj, True, tok)])
        dx, sm_in = _in_proj_bwd(dproj, xin, dx1, mod, rows, win_f, l, tm, tok)
        small[l] = jnp.concatenate([sm_mlp, sm_mix, sm_in], axis=0)
    grad_x = dx.reshape(x.shape)

    recv = {}
    big = {}
    moments = {"w_in": (w_in, m_w_in, v_w_in), "w_mlp_up": (w_mlp_up, m_w_mlp_up, v_w_mlp_up), "w_a_out": (w_a_out, m_w_a_out, v_w_a_out),
               "w_b_out": (w_b_out, m_w_b_out, v_w_b_out), "w_o": (w_o, m_w_o, v_w_o), "w_mlp_down": (w_mlp_down, m_w_mlp_down, v_w_mlp_down)}

    def collect(l, gname, after):
        for n, zone in zip(dict(groups)[gname], _exchange_wait(scatters[l, gname], after, False, f"scatter_wait_{gname}{l}")):
            recv[n, l] = zone

    def update(name, after):
        w, m_, v_ = moments[name]
        big[name] = _sum_adamw([recv[name, l] for l in range(nl)], w, m_, v_, min(ADAM_ROWS, w.shape[1]), after)
        return big[name][1]

    for l, gname in ((1, "mlp"), (1, "in"), (1, "mix"), (0, "mlp")):
        collect(l, gname, dx)
    done = update("w_mlp_down", update("w_mlp_up", dx))
    collect(0, "mix", done)
    for name in ("w_a_out", "w_b_out", "w_o"):
        done = update(name, done)
    collect(0, "in", done)
    done = update("w_in", done)

    lrows = jnp.concatenate(small, axis=0)
    lrows = lrows.at[G_LOSS_ROW, 0].set(loss_part)

    def lrow(a, l, r):
        return a[l * G_LAYER_ROWS + r]

    dm = jnp.stack([jnp.concatenate([lrow(lrows, l, G_MLP_ROWS + G_MIX_ROWS + G_IN_SH), lrow(lrows, l, G_MLP_ROWS + G_MIX_ROWS + G_IN_SC),
                                     lrow(lrows, l, G_MLP_ROWS + G_MIX_GT), lrow(lrows, l, G_MLP_SH), lrow(lrows, l, G_MLP_SC),
                                     lrow(lrows, l, G_MLP_GT)]) for l in range(nl)])
    dm8 = jnp.concatenate([dm[:, None, :], jnp.zeros((nl, SUBLANES - 1, N_MOD * d), F32)], axis=1)
    srows, g_w_mod = _reduce_small(lrows, dm8, cact, done)
    loss = srows[G_LOSS_ROW, 0]
    sgates = _sum_gathered([_exchange_wait(gate_parts[l], srows, True, f"gates_wait{l}")[0] for l in range(nl)])

    def srow(l, r):
        return lrow(srows, l, r)

    def per_layer(r):
        return jnp.stack([srow(l, r) for l in range(nl)])

    mix0 = G_MLP_ROWS
    in0 = G_MLP_ROWS + G_MIX_ROWS
    g_b_mod = jnp.stack([jnp.concatenate([srow(l, in0 + G_IN_SH), srow(l, in0 + G_IN_SC), srow(l, mix0 + G_MIX_GT),
                                          srow(l, G_MLP_SH), srow(l, G_MLP_SC), srow(l, G_MLP_GT)]) for l in range(nl)])
    conv_a_full = jnp.stack([jnp.stack([srow(l, mix0 + G_MIX_CAW + k) for k in range(3)]) for l in range(nl)])
    conv_b_full = jnp.stack([jnp.stack([srow(l, mix0 + G_MIX_CBW + k) for k in range(4)]) for l in range(nl)])
    grads = {
        "b_mod": g_b_mod,
        "g_pre_mix": per_layer(in0 + G_IN_GPRE),
        "g_post_mix": per_layer(mix0 + G_MIX_GPOST),
        "conv_a_w": lax.dynamic_slice_in_dim(conv_a_full, me * cwid, cwid, axis=2),
        "conv_a_b": per_layer(mix0 + G_MIX_CAB),
        "conv_b_w": lax.dynamic_slice_in_dim(conv_b_full, me * cwid, cwid, axis=2),
        "conv_b_b": per_layer(mix0 + G_MIX_CBB),
        "w_gate_r": sgates[:, 0],
        "b_gate_r": per_layer(mix0 + G_MIX_BR),
        "w_gate_i": sgates[:, 1],
        "b_gate_i": per_layer(mix0 + G_MIX_BI),
        "lru_lambda": per_layer(mix0 + G_MIX_LAM),
        "g_pre_mlp": per_layer(G_MLP_GPRE),
        "g_post_mlp": per_layer(G_MLP_GPOST),
    }
    params = {
        "b_mod": (b_mod, m_b_mod, v_b_mod), "g_pre_mix": (g_pre_mix, m_g_pre_mix, v_g_pre_mix), "g_post_mix": (g_post_mix, m_g_post_mix, v_g_post_mix),
        "conv_a_w": (conv_a_w, m_conv_a_w, v_conv_a_w), "conv_a_b": (conv_a_b, m_conv_a_b, v_conv_a_b),
        "conv_b_w": (conv_b_w, m_conv_b_w, v_conv_b_w), "conv_b_b": (conv_b_b, m_conv_b_b, v_conv_b_b),
        "w_gate_r": (w_gate_r, m_w_gate_r, v_w_gate_r), "b_gate_r": (b_gate_r, m_b_gate_r, v_b_gate_r),
        "w_gate_i": (w_gate_i, m_w_gate_i, v_w_gate_i), "b_gate_i": (b_gate_i, m_b_gate_i, v_b_gate_i),
        "lru_lambda": (lru_lambda, m_lru_lambda, v_lru_lambda), "g_pre_mlp": (g_pre_mlp, m_g_pre_mlp, v_g_pre_mlp),
        "g_post_mlp": (g_post_mlp, m_g_post_mlp, v_g_post_mlp),
    }
    out = {}
    for name, g in grads.items():
        w, m_, v_ = params[name]
        flat = (-1, w.shape[-1])
        dl, nm, nv = _adamw(w.reshape(flat), g.reshape(flat), m_.reshape(flat), v_.reshape(flat))
        out[name] = (g.reshape(w.shape), dl.reshape(w.shape), nm.reshape(w.shape), nv.reshape(w.shape))
    out["w_mod"] = (g_w_mod,) + tuple(_adamw_tiled(w_mod, g_w_mod, m_w_mod, v_w_mod, min(ADAM_ROWS, d)))
    out.update(big)

    order = ("w_mod", "b_mod", "g_pre_mix", "g_post_mix", "w_in", "conv_a_w", "conv_a_b", "w_a_out", "conv_b_w", "conv_b_b", "w_gate_r", "b_gate_r",
             "w_gate_i", "b_gate_i", "lru_lambda", "w_b_out", "w_o", "g_pre_mlp", "g_post_mlp", "w_mlp_up", "w_mlp_down")
    return (loss, grad_x) + tuple(out[n][0] for n in order) + tuple(out[n][1] for n in order) + tuple(out[n][2] for n in order) + tuple(out[n][3] for n in order)
```

```python
import math

import jax
import jax.numpy as jnp
from jax import lax
from jax.experimental import pallas as pl
from jax.experimental.pallas import tpu as pltpu

F32, BF16 = jnp.float32, jnp.bfloat16
EPS = 1e-6
LRU_C = 8.0
N_DEV = 8
N_MOD = 6
SUBLANES = 8
VMEM_BUDGET = 56 * 1024 * 1024
MXU_COLS = 256
WGRAD_ROWS = 512
TOKEN_TILE = 256
ADAM_ROWS = 256
SCAN_ROWS = SUBLANES * SUBLANES
ADAM_LR, ADAM_B1, ADAM_B2, ADAM_EPS, ADAM_WD, ADAM_STEP = 0.001, 0.9, 0.999, 1e-08, 0.01, 10
MESH = pl.DeviceIdType.MESH
VMEM_SPEC = pl.BlockSpec(memory_space=pltpu.VMEM)
ANY_SPEC = pl.BlockSpec(memory_space=pl.ANY)
HBM_SPEC = pl.BlockSpec(memory_space=pltpu.HBM)
SEM_SPEC = pl.BlockSpec(memory_space=pltpu.SEMAPHORE)
SIDE_EFFECT = pltpu.SideEffectType.DATAFLOW_SIDE_EFFECTING

R_G_PRE_MIX, R_G_POST_MIX, R_CONV_A_B, R_CONV_B_B, R_B_GATE_R, R_B_GATE_I, R_LAMBDA, R_G_PRE_MLP, R_G_POST_MLP = range(9)
N_ROWS = 16
M_SH_M, M_SC_M, M_GT_M, M_SH_F, M_SC_F, M_GT_F = range(6)
CW_A, CW_B, CW_ROWS = 0, 3, 8
G_MLP_GT, G_MLP_GPOST, G_MLP_SC, G_MLP_SH, G_MLP_GPRE, G_LOSS_ROW, G_MLP_ROWS = 0, 1, 2, 3, 4, 7, 8
(G_MIX_GT, G_MIX_GPOST, G_MIX_CAB, G_MIX_CAW, G_MIX_CBB, G_MIX_CBW, G_MIX_BR, G_MIX_BI, G_MIX_LAM) = 0, 1, 2, 3, 6, 7, 11, 12, 13
G_MIX_ROWS = 16
G_IN_SC, G_IN_SH, G_IN_GPRE, G_IN_ROWS = 0, 1, 2, 8
G_LAYER_ROWS = G_MLP_ROWS + G_MIX_ROWS + G_IN_ROWS


def _cparams(dims=None, vmem=None):
    kw = {}
    if dims is not None:
        kw["dimension_semantics"] = dims
    if vmem is not None:
        kw["vmem_limit_bytes"] = int(min(max(vmem, 16 * 1024 * 1024), VMEM_BUDGET))
    return pltpu.CompilerParams(**kw)


def _nbytes(shape, dtype):
    n = 1
    for s in shape:
        n *= s
    return n * jnp.dtype(dtype).itemsize


def _hbm(*arrays):
    return tuple(pltpu.with_memory_space_constraint(a, pltpu.HBM) for a in arrays)


def _resident(block, index_map):
    return pl.BlockSpec(block, index_map, pipeline_mode=pl.Buffered(1))


def _my_position():
    x, y, c = lax.axis_index("x"), lax.axis_index("y"), lax.axis_index("c")
    return (x, y, c), 4 * x + 2 * y + c


def _peer(pos, k):
    x, y, c = pos
    px = 1 - x if k & 4 else x
    py = 1 - y if k & 2 else y
    pc = 1 - c if k & 1 else c
    return (px, py, pc), 4 * px + 2 * py + pc


def _remote(src, dst, ssem, rsem, peer):
    return pltpu.make_async_remote_copy(src_ref=src, dst_ref=dst, send_sem=ssem, recv_sem=rsem, device_id=peer, device_id_type=MESH)


def _dot(a, b):
    return jnp.dot(a, b, preferred_element_type=F32)


def _dot_nt(a, b):
    return lax.dot_general(a, b, (((1,), (1,)), ((), ())), preferred_element_type=F32)


def _dot_tn(a, b):
    return lax.dot_general(a, b, (((0,), (0,)), ((), ())), preferred_element_type=F32)


def _colsum(v):
    return jnp.sum(v, axis=0, keepdims=True)


def _sigmoid(v):
    return jax.nn.sigmoid(v)


GELU_K, GELU_C = 0.7978845608028654, 0.044715


def _gelu(v):
    s = 1.0 / (1.0 + jnp.exp(v * (-2.0 * GELU_K - (2.0 * GELU_K * GELU_C) * (v * v))))
    return v * s, s


def _gelu_grad(v, s):
    return s * (1.0 + (v * (1.0 - s)) * (2.0 * GELU_K + (6.0 * GELU_K * GELU_C) * (v * v)))


def _neg_expm1_twice(v):
    t = jnp.tanh(v)
    return (-2.0 * t) / (1.0 - t)


def _softplus_neg(lam):
    z = -lam
    u = jnp.exp(-jnp.abs(z))
    w = 1.0 + u
    l1p = jnp.where(w == 1.0, u, jnp.log(w) * u / (w - 1.0))
    return jnp.maximum(z, 0.0) + l1p


def _rms(v):
    return lax.rsqrt(jnp.mean(v * v, axis=-1, keepdims=True) + EPS)


def _prenorm_bwd(xv, dh, g, sc):
    r = _rms(xv)
    xn = xv * r
    n = xn * g
    dsc = _colsum(dh * n)
    dsh = _colsum(dh)
    dn = dh * (1.0 + sc)
    dg = _colsum(dn * xn)
    dxn = dn * g
    dx = r * (dxn - xn * jnp.mean(dxn * xn, axis=-1, keepdims=True))
    return dx, dsc, dsh, dg


def _postnorm_bwd(yv, dout, g, gt):
    r = _rms(yv)
    yn = yv * r
    dgt = _colsum(dout * (yn * g))
    dn = dout * gt
    dg = _colsum(dn * yn)
    dyn = dn * g
    dy = r * (dyn - yn * jnp.mean(dyn * yn, axis=-1, keepdims=True))
    return dy, dgt, dg


def _gates(xc, wr_ref, wi_ref, b_r, b_i, sp, nh, bw):
    xcb = xc.astype(BF16)
    zr = jnp.concatenate([_dot(xcb[:, h * bw:(h + 1) * bw], wr_ref[h]) for h in range(nh)], axis=1) + b_r
    zi = jnp.concatenate([_dot(xcb[:, h * bw:(h + 1) * bw], wi_ref[h]) for h in range(nh)], axis=1) + b_i
    r = _sigmoid(zr)
    ig = _sigmoid(zi)
    la = (-LRU_C * r) * sp
    a = jnp.exp(la)
    mult = jnp.sqrt(_neg_expm1_twice(la))
    return xcb, r, ig, a, mult


def _shift_rows(cur, edge, k, up):
    t, dd = cur.shape
    blocks = cur.reshape(t // SUBLANES, SUBLANES, dd)
    row = lax.broadcasted_iota(jnp.int32, (1, SUBLANES, dd), 1)
    if up:
        r = pltpu.roll(blocks, SUBLANES - k, 1)
        nxt = jnp.concatenate([r[1:], pltpu.roll(edge, SUBLANES - k, 0)[None]], axis=0)
        out = jnp.where(row >= SUBLANES - k, nxt, r)
    else:
        r = pltpu.roll(blocks, k, 1)
        prv = jnp.concatenate([pltpu.roll(edge, k, 0)[None], r[:-1]], axis=0)
        out = jnp.where(row < k, prv, r)
    return out.reshape(t, dd)


def _scan_group(a_ref, b_ref, o_ref, h, base, carry, reverse):
    def rows(k):
        return pl.ds(base + k, SUBLANES, stride=SUBLANES)

    order = range(SUBLANES - 1, -1, -1) if reverse else range(SUBLANES)
    loc, prod = {}, {}
    prev = None
    for k in order:
        a, b = a_ref[h, rows(k), :], b_ref[h, rows(k), :]
        loc[k] = b if prev is None else a * loc[prev] + b
        prod[k] = a if prev is None else a * prod[prev]
        prev = k
    pa, pb = _scan_block(prod[prev], loc[prev], reverse)
    ends = pb + pa * carry
    row = lax.broadcasted_iota(jnp.int32, ends.shape, 0)
    if reverse:
        into = jnp.where(row < SUBLANES - 1, pltpu.roll(ends, SUBLANES - 1, 0), carry)
        last = ends[0:1, :]
    else:
        into = jnp.where(row >= 1, pltpu.roll(ends, 1, 0), carry)
        last = ends[SUBLANES - 1:SUBLANES, :]
    for k in range(SUBLANES):
        o_ref[h, rows(k), :] = loc[k] + prod[k] * into
    return jnp.broadcast_to(last, ends.shape)


def _scan_block(a8, b8, reverse):
    row = lax.broadcasted_iota(jnp.int32, a8.shape, 0)
    for s in (1, 2, 4):
        if reverse:
            keep = row < SUBLANES - s
            a_sh = pltpu.roll(a8, SUBLANES - s, 0)
            b_sh = pltpu.roll(b8, SUBLANES - s, 0)
        else:
            keep = row >= s
            a_sh = pltpu.roll(a8, s, 0)
            b_sh = pltpu.roll(b8, s, 0)
        b8 = b8 + a8 * jnp.where(keep, b_sh, 0.0)
        a8 = a8 * jnp.where(keep, a_sh, 1.0)
    return a8, b8


def _prep_small(c, w_mod, b_mod, cw):
    d = c.shape[1]
    cm = w_mod.shape[2]
    cwid = cw.shape[2]
    nl = w_mod.shape[0]

    def body(c_ref, wm_ref, bm_ref, cw_ref, mod_ref, cact_ref, cwf_ref, cbuf, pbuf, rbuf, ssem, rsem, lsem):
        pos, me = _my_position()
        me8 = pl.multiple_of(me * SUBLANES, SUBLANES)
        cbuf[pl.ds(me8, SUBLANES), :] = jnp.broadcast_to(c_ref[...], (SUBLANES, d))
        own_cw = pltpu.make_async_copy(cw_ref, cwf_ref.at[:, :, pl.ds(me * cwid, cwid)], lsem.at[0])
        own_cw.start()
        first = []
        for k in range(1, N_DEV):
            peer, _ = _peer(pos, k)
            rows = cbuf.at[pl.ds(me8, SUBLANES), :]
            first.append(_remote(rows, rows, ssem.at[0, k - 1], rsem.at[0, k - 1], peer))
            first.append(_remote(cw_ref, cwf_ref.at[:, :, pl.ds(me * cwid, cwid)], ssem.at[1, k - 1], rsem.at[1, k - 1], peer))
        for cp in first:
            cp.start()
        for k in range(1, N_DEV):
            peer, pj = _peer(pos, k)
            pj8 = pl.multiple_of(pj * SUBLANES, SUBLANES)
            rows = cbuf.at[pl.ds(pj8, SUBLANES), :]
            _remote(rows, rows, ssem.at[0, k - 1], rsem.at[0, k - 1], peer).wait_recv()
        cv = cbuf[...]
        cact = cv * _sigmoid(cv)
        cact_ref[...] = cact
        cb = cact.astype(BF16)
        for l in range(nl):
            pbuf[l] = _dot(cb, wm_ref[l].astype(BF16))
        own_p = pltpu.make_async_copy(pbuf.at[:, pl.ds(me8, SUBLANES), :], rbuf.at[me], lsem.at[1])
        own_p.start()
        second = []
        for k in range(1, N_DEV):
            peer, pj = _peer(pos, k)
            pj8 = pl.multiple_of(pj * SUBLANES, SUBLANES)
            second.append(_remote(pbuf.at[:, pl.ds(pj8, SUBLANES), :], rbuf.at[me], ssem.at[2, k - 1], rsem.at[2, k - 1], peer))
        for cp in second:
            cp.start()
        for k in range(1, N_DEV):
            peer, pj = _peer(pos, k)
            _remote(pbuf.at[:, pl.ds(0, SUBLANES), :], rbuf.at[pj], ssem.at[2, k - 1], rsem.at[2, k - 1], peer).wait_recv()
            _remote(cw_ref, cwf_ref.at[:, :, pl.ds(pj * cwid, cwid)], ssem.at[1, k - 1], rsem.at[1, k - 1], peer).wait_recv()
        own_p.wait()
        own_cw.wait()
        for l in range(nl):
            for j in range(N_DEV):
                mod_ref[l:l + 1, j * cm:(j + 1) * cm] = rbuf[j, l, 0:1, :] + bm_ref[l:l + 1, j * cm:(j + 1) * cm]
        for cp in first + second:
            cp.wait_send()

    return pl.pallas_call(
        body,
        name="prep_small",
        out_shape=(
            jax.ShapeDtypeStruct((nl, N_MOD * d), F32),
            jax.ShapeDtypeStruct((N_DEV * SUBLANES, d), F32),
            jax.ShapeDtypeStruct((nl, CW_ROWS, d), F32),
        ),
        in_specs=[VMEM_SPEC] * 4,
        out_specs=(VMEM_SPEC,) * 3,
        scratch_shapes=[
            pltpu.VMEM((N_DEV * SUBLANES, d), F32),
            pltpu.VMEM((nl, N_DEV * SUBLANES, cm), F32),
            pltpu.VMEM((N_DEV, nl, SUBLANES, cm), F32),
            pltpu.SemaphoreType.DMA((3, N_DEV - 1)),
            pltpu.SemaphoreType.DMA((3, N_DEV - 1)),
            pltpu.SemaphoreType.DMA((2,)),
        ],
        compiler_params=_cparams(vmem=3 * _nbytes(w_mod.shape, F32)),
    )(c, w_mod, b_mod, cw)


def _exchange_start(parts, gather, name):
    n = len(parts)
    lands = [lax.empty(((N_DEV,) + tuple(p.shape)) if gather else tuple(p.shape), p.dtype) for p in parts]

    def body(*refs):
        ins, lnd = refs[:n], refs[n:2 * n]
        ssem, rsem, token = refs[2 * n], refs[2 * n + 1], refs[-1]
        pos, me = _my_position()
        for k in range(1, N_DEV):
            peer, pj = _peer(pos, k)
            for t in range(n):
                src = ins[t] if gather else ins[t].at[pj]
                q = t * (N_DEV - 1) + k - 1
                _remote(src, lnd[t].at[me], ssem.at[q], rsem.at[q], peer).start()
        token[...] = jnp.zeros(token.shape, F32)

    out = pl.pallas_call(
        body,
        name=name,
        out_shape=(pltpu.SemaphoreType.DMA((n * (N_DEV - 1),)), pltpu.SemaphoreType.DMA((n * (N_DEV - 1),)))
        + tuple(pltpu.HBM(p.shape, p.dtype) for p in parts) + tuple(pltpu.HBM(p.shape, p.dtype) for p in lands)
        + (jax.ShapeDtypeStruct((SUBLANES, 128), F32),),
        in_specs=[HBM_SPEC] * (2 * n),
        out_specs=(SEM_SPEC, SEM_SPEC) + (HBM_SPEC,) * (2 * n) + (VMEM_SPEC,),
        input_output_aliases={i: 2 + i for i in range(2 * n)},
        compiler_params=pltpu.CompilerParams(has_side_effects=SIDE_EFFECT),
    )(*[pltpu.with_memory_space_constraint(p, pltpu.HBM) for p in list(parts) + lands])
    return out[0], out[1], list(out[2:2 + n]), list(out[2 + n:2 + 2 * n]), out[-1]


def _exchange_wait(started, after, gather, name):
    ssem, rsem, parts, lands, _ = started
    n = len(parts)

    def body(*refs):
        ins, lnd = refs[:n], refs[n:2 * n]
        ssem_ref, rsem_ref = refs[2 * n], refs[2 * n + 1]
        stage, lsem = refs[-1 - n:-1], refs[-1]
        pos, me = _my_position()
        load = []
        for t in range(n):
            src = ins[t] if gather else ins[t].at[me]
            load.append(pltpu.make_async_copy(src, stage[t], lsem.at[t]))
            load[-1].start()
        store = []
        for t in range(n):
            load[t].wait()
            store.append(pltpu.make_async_copy(stage[t], lnd[t].at[me], lsem.at[t]))
            store[-1].start()
        for k in range(1, N_DEV):
            peer, pj = _peer(pos, k)
            for t in range(n):
                src = ins[t] if gather else ins[t].at[pj]
                q = t * (N_DEV - 1) + k - 1
                _remote(src, lnd[t].at[me], ssem_ref.at[q], rsem_ref.at[q], peer).wait_send()
                _remote(src, lnd[t].at[pj], ssem_ref.at[q], rsem_ref.at[q], peer).wait_recv()
        for cp in store:
            cp.wait()

    out = pl.pallas_call(
        body,
        name=name,
        out_shape=tuple(pltpu.HBM(p.shape, p.dtype) for p in parts) + tuple(pltpu.HBM(p.shape, p.dtype) for p in lands),
        in_specs=[HBM_SPEC] * (2 * n) + [SEM_SPEC, SEM_SPEC, ANY_SPEC],
        out_specs=(HBM_SPEC,) * (2 * n),
        input_output_aliases={i: i for i in range(2 * n)},
        scratch_shapes=[pltpu.VMEM(tuple(z.shape[1:]), z.dtype) for z in lands] + [pltpu.SemaphoreType.DMA((n,))],
        compiler_params=pltpu.CompilerParams(has_side_effects=SIDE_EFFECT),
    )(*parts, *lands, ssem, rsem, after)
    return list(out[n:])


CHIP_PEERS = 3


def _chip_peer(pos, k, core):
    x, y, _ = pos
    px = 1 - x if k & 2 else x
    py = 1 - y if k & 1 else y
    return (px, py, core), 4 * px + 2 * py + core


def _gather2_start(parts, after, name):
    n = len(parts)
    lands = [lax.empty((N_DEV,) + tuple(p.shape), p.dtype) for p in parts]
    per = CHIP_PEERS + 1

    def body(*refs):
        ins, lnd = refs[:n], refs[n:2 * n]
        ssem, rsem, token = refs[2 * n + 1], refs[2 * n + 2], refs[-1]
        pos, me = _my_position()
        sibling = (pos[0], pos[1], 1 - pos[2])
        for t in range(n):
            _remote(ins[t], lnd[t].at[me], ssem.at[per * t], rsem.at[per * t], sibling).start()
        for k in range(1, per):
            peer, _ = _chip_peer(pos, k, pos[2])
            for t in range(n):
                _remote(ins[t], lnd[t].at[me], ssem.at[per * t + k], rsem.at[per * t + k], peer).start()
        token[...] = jnp.zeros(token.shape, F32)

    out = pl.pallas_call(
        body,
        name=name,
        out_shape=(pltpu.SemaphoreType.DMA((n * per,)), pltpu.SemaphoreType.DMA((n * per,)))
        + tuple(pltpu.HBM(p.shape, p.dtype) for p in parts) + tuple(pltpu.HBM(p.shape, p.dtype) for p in lands)
        + (jax.ShapeDtypeStruct((SUBLANES, 128), F32),),
        in_specs=[HBM_SPEC] * (2 * n) + [ANY_SPEC],
        out_specs=(SEM_SPEC, SEM_SPEC) + (HBM_SPEC,) * (2 * n) + (VMEM_SPEC,),
        input_output_aliases={i: 2 + i for i in range(2 * n)},
        compiler_params=pltpu.CompilerParams(has_side_effects=SIDE_EFFECT),
    )(*[pltpu.with_memory_space_constraint(p, pltpu.HBM) for p in list(parts) + lands], after)
    return out[0], out[1], list(out[2:2 + n]), list(out[2 + n:2 + 2 * n]), out[-1]


def _gather2_forward(started, after, name):
    ssem, rsem, parts, lands, _ = started
    n = len(lands)
    per = CHIP_PEERS + 1

    def body(*refs):
        lnd = refs[:n]
        rsem_a, fsend, frecv, token = refs[n], refs[2 * n + 2], refs[2 * n + 3], refs[2 * n + 4]
        pos, _ = _my_position()
        sibling = (pos[0], pos[1], 1 - pos[2])
        for k in range(1, per):
            peer, pk = _chip_peer(pos, k, pos[2])
            for t in range(n):
                block = lnd[t].at[pk]
                _remote(block, block, rsem_a.at[per * t + k], rsem_a.at[per * t + k], peer).wait_recv()
                q = CHIP_PEERS * t + k - 1
                _remote(block, block, fsend.at[q], frecv.at[q], sibling).start()
        token[...] = jnp.zeros(token.shape, F32)

    out = pl.pallas_call(
        body,
        name=name,
        out_shape=tuple(pltpu.HBM(p.shape, p.dtype) for p in lands)
        + (pltpu.SemaphoreType.DMA((n * CHIP_PEERS,)), pltpu.SemaphoreType.DMA((n * CHIP_PEERS,)),
           jax.ShapeDtypeStruct((SUBLANES, 128), F32)),
        in_specs=[HBM_SPEC] * n + [SEM_SPEC, ANY_SPEC],
        out_specs=(HBM_SPEC,) * n + (SEM_SPEC, SEM_SPEC, VMEM_SPEC),
        input_output_aliases={i: i for i in range(n)},
        compiler_params=pltpu.CompilerParams(has_side_effects=SIDE_EFFECT),
    )(*lands, rsem, after)
    return ssem, rsem, parts, list(out[:n]), out[n + 2], out[n], out[n + 1]


def _gather2_wait(forwarded, after, name):
    ssem, rsem, parts, lands, _, fsend, frecv = forwarded
    n = len(parts)
    per = CHIP_PEERS + 1

    def body(*refs):
        ins, lnd = refs[:n], refs[n:2 * n]
        ssem_a, rsem_a, fs, fr = refs[2 * n:2 * n + 4]
        stage, lsem = refs[-1 - n:-1], refs[-1]
        pos, me = _my_position()
        sibling = (pos[0], pos[1], 1 - pos[2])
        sib = 4 * pos[0] + 2 * pos[1] + 1 - pos[2]
        load = []
        for t in range(n):
            load.append(pltpu.make_async_copy(ins[t], stage[t], lsem.at[t]))
            load[-1].start()
        store = []
        for t in range(n):
            load[t].wait()
            store.append(pltpu.make_async_copy(stage[t], lnd[t].at[me], lsem.at[t]))
            store[-1].start()
        for t in range(n):
            _remote(ins[t], lnd[t].at[me], ssem_a.at[per * t], rsem_a.at[per * t], sibling).wait_send()
            _remote(ins[t], lnd[t].at[sib], ssem_a.at[per * t], rsem_a.at[per * t], sibling).wait_recv()
        for k in range(1, per):
            peer, pk = _chip_peer(pos, k, pos[2])
            _, qk = _chip_peer(pos, k, 1 - pos[2])
            for t in range(n):
                q = CHIP_PEERS * t + k - 1
                _remote(ins[t], lnd[t].at[me], ssem_a.at[per * t + k], rsem_a.at[per * t + k], peer).wait_send()
                _remote(lnd[t].at[pk], lnd[t].at[pk], fs.at[q], fr.at[q], sibling).wait_send()
                _remote(lnd[t].at[qk], lnd[t].at[qk], fs.at[q], fr.at[q], sibling).wait_recv()
        for cp in store:
            cp.wait()

    out = pl.pallas_call(
        body,
        name=name,
        out_shape=tuple(pltpu.HBM(p.shape, p.dtype) for p in parts) + tuple(pltpu.HBM(p.shape, p.dtype) for p in lands),
        in_specs=[HBM_SPEC] * (2 * n) + [SEM_SPEC] * 4 + [ANY_SPEC],
        out_specs=(HBM_SPEC,) * (2 * n),
        input_output_aliases={i: i for i in range(2 * n)},
        scratch_shapes=[pltpu.VMEM(tuple(z.shape[1:]), z.dtype) for z in lands] + [pltpu.SemaphoreType.DMA((n,))],
        compiler_params=pltpu.CompilerParams(has_side_effects=SIDE_EFFECT),
    )(*parts, *lands, ssem, rsem, fsend, frecv, after)
    return list(out[n:])


def _reduce_small(rows, dm8, cact, after):
    r, d = rows.shape
    nl = dm8.shape[0]
    cm = dm8.shape[2] // N_DEV
    lw = d // N_DEV

    def body(rows_ref, dm_ref, cact_ref, after_ref, orow_ref, owm_ref, gr, mine, dmr, ssem, rsem, lsem):
        pos, me = _my_position()
        me8 = pl.multiple_of(me * SUBLANES, SUBLANES)

        def lanes(j):
            return pl.ds(j * lw, lw)

        own = [pltpu.make_async_copy(dm_ref.at[:, :, pl.ds(me * cm, cm)], dmr.at[:, pl.ds(me8, SUBLANES), :], lsem.at[0]),
               pltpu.make_async_copy(rows_ref.at[:, lanes(me)], gr.at[me], lsem.at[1])]
        first = []
        for k in range(1, N_DEV):
            peer, pj = _peer(pos, k)
            first.append(_remote(rows_ref.at[:, lanes(pj)], gr.at[me], ssem.at[0, k - 1], rsem.at[0, k - 1], peer))
            first.append(_remote(dm_ref.at[:, :, pl.ds(pj * cm, cm)], dmr.at[:, pl.ds(me8, SUBLANES), :],
                                 ssem.at[1, k - 1], rsem.at[1, k - 1], peer))
        for cp in own + first:
            cp.start()
        for k in range(1, N_DEV):
            peer, pj = _peer(pos, k)
            pj8 = pl.multiple_of(pj * SUBLANES, SUBLANES)
            _remote(rows_ref.at[:, lanes(0)], gr.at[pj], ssem.at[0, k - 1], rsem.at[0, k - 1], peer).wait_recv()
            _remote(dm_ref.at[:, :, pl.ds(0, cm)], dmr.at[:, pl.ds(pj8, SUBLANES), :], ssem.at[1, k - 1], rsem.at[1, k - 1], peer).wait_recv()
        for cp in own:
            cp.wait()
        acc = gr[0]
        for j in range(1, N_DEV):
            acc = acc + gr[j]
        mine[...] = acc
        own_sum = pltpu.make_async_copy(mine, orow_ref.at[:, lanes(me)], lsem.at[1])
        own_sum.start()
        second = []
        for k in range(1, N_DEV):
            peer, _ = _peer(pos, k)
            second.append(_remote(mine, orow_ref.at[:, lanes(me)], ssem.at[2, k - 1], rsem.at[2, k - 1], peer))
            second[-1].start()
        cb = cact_ref[...].astype(BF16)
        for l in range(nl):
            owm_ref[l] = _dot_tn(cb, dmr[l].astype(BF16))
        for k in range(1, N_DEV):
            peer, pj = _peer(pos, k)
            _remote(mine, orow_ref.at[:, lanes(pj)], ssem.at[2, k - 1], rsem.at[2, k - 1], peer).wait_recv()
        own_sum.wait()
        for cp in first + second:
            cp.wait_send()

    return pl.pallas_call(
        body,
        name="reduce_small",
        out_shape=(jax.ShapeDtypeStruct((r, d), F32), jax.ShapeDtypeStruct((nl, d, cm), F32)),
        in_specs=[VMEM_SPEC] * 3 + [ANY_SPEC],
        out_specs=(VMEM_SPEC,) * 2,
        scratch_shapes=[
            pltpu.VMEM((N_DEV, r, lw), F32),
            pltpu.VMEM((r, lw), F32),
            pltpu.VMEM((nl, N_DEV * SUBLANES, cm), F32),
            pltpu.SemaphoreType.DMA((3, N_DEV - 1)),
            pltpu.SemaphoreType.DMA((3, N_DEV - 1)),
            pltpu.SemaphoreType.DMA((2,)),
        ],
        compiler_params=_cparams(vmem=4 * _nbytes((r, d), F32) + 6 * _nbytes((nl, d, cm), F32)),
    )(rows, dm8, cact, *_hbm(after))


def _sum_gathered(zones):
    nl = len(zones)

    def body(*refs):
        for l in range(nl):
            acc = refs[l][0].astype(F32)
            for j in range(1, N_DEV):
                acc = acc + refs[l][j].astype(F32)
            refs[nl][l] = acc

    return pl.pallas_call(
        body,
        name="sum_gathered",
        out_shape=jax.ShapeDtypeStruct((nl,) + tuple(zones[0].shape[1:]), F32),
        in_specs=[VMEM_SPEC] * nl,
        out_specs=VMEM_SPEC,
        compiler_params=_cparams(vmem=8 * nl * _nbytes(zones[0].shape, BF16)),
    )(*zones)


def _side_by_side(w_hbm, w_s, sem):
    nb, _, ci = w_hbm.shape
    return [pltpu.make_async_copy(w_hbm.at[j], w_s.at[:, j * ci:(j + 1) * ci], sem.at[j]) for j in range(nb)]


def _wide(n, ci):
    return min(n, ci * MXU_COLS // math.gcd(ci, MXU_COLS))


def _wide_chunks(copies, n, wide, first):
    per = len(copies) * wide // n
    for k, c0 in enumerate(range(0, n, wide)):
        @pl.when(first)
        def _():
            for cp in copies[k * per:(k + 1) * per]:
                cp.wait()

        yield c0


def _in_proj_fwd(x, mod, rows, win_f, l, tm, after):
    s, d = x.shape
    nb, _, ci = win_f.shape
    n = nb * ci
    wide = _wide(n, ci)

    def body(x_ref, mod_ref, rows_ref, w_hbm, after_ref, proj_ref, h_ref, w_s, sem):
        first = pl.program_id(0) == 0
        copies = _side_by_side(w_hbm, w_s, sem)

        @pl.when(first)
        def _():
            for cp in copies:
                cp.start()

        xv = x_ref[...]
        g = rows_ref[R_G_PRE_MIX:R_G_PRE_MIX + 1, :]
        h = (xv * _rms(xv) * g) * (1.0 + mod_ref[M_SC_M:M_SC_M + 1, :]) + mod_ref[M_SH_M:M_SH_M + 1, :]
        hb = h.astype(BF16)
        h_ref[...] = hb
        for c0 in _wide_chunks(copies, n, wide, first):
            proj_ref[:, c0:c0 + wide] = _dot(hb, w_s[:, c0:c0 + wide])

    return pl.pallas_call(
        body,
        name="in_proj_fwd",
        grid=(s // tm,),
        in_specs=[
            pl.BlockSpec((tm, d), lambda i: (i, 0)),
            _resident((None, N_MOD, d), lambda i: (l, 0, 0)),
            _resident((None, N_ROWS, d), lambda i: (l, 0, 0)),
            ANY_SPEC,
            ANY_SPEC,
        ],
        out_specs=(pl.BlockSpec((tm, n), lambda i: (i, 0)), pl.BlockSpec((tm, d), lambda i: (i, 0))),
        out_shape=(jax.ShapeDtypeStruct((s, n), F32), jax.ShapeDtypeStruct((s, d), BF16)),
        scratch_shapes=[pltpu.VMEM((d, n), BF16), pltpu.SemaphoreType.DMA((nb,))],
        compiler_params=_cparams(("arbitrary",), _nbytes((d, n), BF16) + 3 * _nbytes((tm, n), F32) + 8 * _nbytes((tm, d), F32)),
    )(*_hbm(x), mod, rows, *_hbm(win_f), after)


def _mixer_core_fwd(proj, x, mod, rows, cwf, wr, wi, wa_f, wb_f, wo_f, l, tm, after):
    s, d = x.shape
    nh, bw, _ = wr.shape[1:]

    def body(proj_ref, x_ref, mod_ref, rows_ref, cw_ref, wr_ref, wi_ref, wa_ref, wb_ref, wo_ref, after_ref,
             x1_ref, hs_ref, yap_ref, ybp_ref, y_ref, cvbuf, xbbuf, a_s, b_s, h_s, hprev):
        i = pl.program_id(0)

        @pl.when(i == 0)
        def _():
            cvbuf[...] = jnp.zeros((SUBLANES, d), F32)
            xbbuf[...] = jnp.zeros((SUBLANES, d), F32)
            hprev[...] = jnp.zeros((SUBLANES, d), F32)

        def row(r):
            return rows_ref[r:r + 1, :]

        def tap(r):
            return cw_ref[r:r + 1, :]

        ba = proj_ref[:, 0:d]
        cv = proj_ref[:, d:2 * d] * proj_ref[:, 2 * d:3 * d]
        cvt = cvbuf[...]
        conv3 = ((row(R_CONV_A_B) + _shift_rows(cv, cvt, 2, False) * tap(CW_A)) + _shift_rows(cv, cvt, 1, False) * tap(CW_A + 1)) + cv * tap(CW_A + 2)
        ya = ba * conv3
        cvbuf[...] = cv[tm - SUBLANES:, :]
        sp = _softplus_neg(row(R_LAMBDA))
        for h in range(nh):
            lo, hi = h * bw, (h + 1) * bw
            xb = proj_ref[:, 3 * d + lo:3 * d + hi]
            xbt = xbbuf[:, lo:hi]

            def hrow(r):
                return rows_ref[r:r + 1, lo:hi]

            def htap(r):
                return cw_ref[r:r + 1, lo:hi]

            xc = (((hrow(R_CONV_B_B) + _shift_rows(xb, xbt, 3, False) * htap(CW_B)) + _shift_rows(xb, xbt, 2, False) * htap(CW_B + 1))
                  + _shift_rows(xb, xbt, 1, False) * htap(CW_B + 2)) + xb * htap(CW_B + 3)
            xcb = xc.astype(BF16)
            r = _sigmoid(_dot(xcb, wr_ref[h]) + hrow(R_B_GATE_R))
            ig = _sigmoid(_dot(xcb, wi_ref[h]) + hrow(R_B_GATE_I))
            la = (-LRU_C * r) * sp[:, lo:hi]
            a_s[h] = jnp.exp(la)
            b_s[h] = jnp.sqrt(_neg_expm1_twice(la)) * (ig * xc)
        xbbuf[...] = proj_ref[tm - SUBLANES:, 3 * d:4 * d]

        def group(j, hp):
            base = pl.multiple_of(j * SCAN_ROWS, SCAN_ROWS)
            return jnp.concatenate([_scan_group(a_s, b_s, h_s, h, base, hp[:, h * bw:(h + 1) * bw], False) for h in range(nh)], axis=1)

        hprev[...] = lax.fori_loop(0, tm // SCAN_ROWS, group, hprev[...])
        for h in range(nh):
            hs_ref[:, h * bw:(h + 1) * bw] = h_s[h]
        gel, _ = _gelu(proj_ref[:, 4 * d:5 * d])
        yb = hs_ref[...] * gel
        yap = _dot(ya.astype(BF16), wa_ref[...])
        ybp = _dot(yb.astype(BF16), wb_ref[...])
        yap_ref[...] = yap
        ybp_ref[...] = ybp
        m = _sigmoid(proj_ref[:, 5 * d:6 * d]) * yap + _sigmoid(proj_ref[:, 6 * d:7 * d]) * ybp
        y = _dot(m.astype(BF16), wo_ref[...])
        y_ref[...] = y
        x1_ref[...] = x_ref[...] + mod_ref[M_GT_M:M_GT_M + 1, :] * ((y * _rms(y)) * row(R_G_POST_MIX))

    tile = pl.BlockSpec((tm, d), lambda i: (i, 0))
    return pl.pallas_call(
        body,
        name="mixer_core_fwd",
        grid=(s // tm,),
        in_specs=[
            pl.BlockSpec((tm, 7 * d), lambda i: (i, 0)),
            tile,
            _resident((None, N_MOD, d), lambda i: (l, 0, 0)),
            _resident((None, N_ROWS, d), lambda i: (l, 0, 0)),
            _resident((None, CW_ROWS, d), lambda i: (l, 0, 0)),
            _resident((None, nh, bw, bw), lambda i: (l, 0, 0, 0)),
            _resident((None, nh, bw, bw), lambda i: (l, 0, 0, 0)),
            _resident((d, d), lambda i: (0, 0)),
            _resident((d, d), lambda i: (0, 0)),
            _resident((d, d), lambda i: (0, 0)),
            ANY_SPEC,
        ],
        out_specs=(tile,) * 5,
        out_shape=(jax.ShapeDtypeStruct((s, d), F32),) * 5,
        scratch_shapes=[
            pltpu.VMEM((SUBLANES, d), F32),
            pltpu.VMEM((SUBLANES, d), F32),
            pltpu.VMEM((nh, tm, bw), F32),
            pltpu.VMEM((nh, tm, bw), F32),
            pltpu.VMEM((nh, tm, bw), F32),
            pltpu.VMEM((SUBLANES, d), F32),
        ],
        compiler_params=_cparams(("arbitrary",), 3 * _nbytes((d, d), BF16) + 2 * _nbytes((tm, 7 * d), F32) + 40 * _nbytes((tm, d), F32)),
    )(*_hbm(proj, x), mod, rows, cwf, wr, wi, wa_f, wb_f, wo_f, after)


def _mlp_fwd(x1, mod, rows, wup_f, wdn_f, l, tm, after, target=None):
    s, d = x1.shape
    nb, _, cu = wup_f.shape
    dff = nb * cu
    with_loss = target is not None

    def body(x1_ref, mod_ref, rows_ref, wu_ref, wd_ref, after_ref, *refs):
        x2_ref, ru_ref, y2_ref, h2_ref = refs[-6:-2] if with_loss else refs
        xv = x1_ref[...]
        g = rows_ref[R_G_PRE_MLP:R_G_PRE_MLP + 1, :]
        h2 = ((xv * _rms(xv) * g) * (1.0 + mod_ref[M_SC_F:M_SC_F + 1, :]) + mod_ref[M_SH_F:M_SH_F + 1, :]).astype(BF16)
        h2_ref[...] = h2
        ru = jnp.concatenate([jnp.maximum(_dot(h2, wu_ref[j]), 0.0) for j in range(nb)], axis=1)
        ru_ref[...] = ru.astype(BF16)
        y2 = _dot((ru * ru).astype(BF16), wd_ref[...])
        y2_ref[...] = y2
        x2 = xv + mod_ref[M_GT_F:M_GT_F + 1, :] * ((y2 * _rms(y2)) * rows_ref[R_G_POST_MLP:R_G_POST_MLP + 1, :])
        x2_ref[...] = x2
        if with_loss:
            t_ref, dy_ref, loss_ref = refs[0], refs[-2], refs[-1]

            @pl.when(pl.program_id(0) == 0)
            def _():
                loss_ref[...] = jnp.zeros(loss_ref.shape, F32)

            e = x2 - t_ref[...]
            dy_ref[...] = e * (1.0 / d)
            loss_ref[...] += 0.5 * jnp.sum(jnp.mean(e * e, axis=-1, keepdims=True), axis=0, keepdims=True)

    tile = pl.BlockSpec((tm, d), lambda i: (i, 0))
    wide = pl.BlockSpec((tm, dff), lambda i: (i, 0))
    out_specs = (tile, wide, tile, tile)
    out_shape = (jax.ShapeDtypeStruct((s, d), F32), jax.ShapeDtypeStruct((s, dff), BF16),
                 jax.ShapeDtypeStruct((s, d), F32), jax.ShapeDtypeStruct((s, d), BF16))
    if with_loss:
        out_specs += (tile, pl.BlockSpec((SUBLANES, 128), lambda i: (0, 0)))
        out_shape += (jax.ShapeDtypeStruct((s, d), F32), jax.ShapeDtypeStruct((SUBLANES, 128), F32))
    return pl.pallas_call(
        body,
        name="mlp_fwd",
        grid=(s // tm,),
        in_specs=[
            tile,
            _resident((None, N_MOD, d), lambda i: (l, 0, 0)),
            _resident((None, N_ROWS, d), lambda i: (l, 0, 0)),
            _resident((nb, d, cu), lambda i: (0, 0, 0)),
            _resident((dff, d), lambda i: (0, 0)),
            ANY_SPEC,
        ] + ([tile] if with_loss else []),
        out_specs=out_specs,
        out_shape=out_shape,
        compiler_params=_cparams(("arbitrary",), 2 * _nbytes((dff, d), BF16) + 5 * _nbytes((tm, dff), F32) + 16 * _nbytes((tm, d), F32)),
    )(*_hbm(x1), mod, rows, wup_f, wdn_f, after, *(_hbm(target) if with_loss else ()))


def _mlp_bwd(dx2, x1, y2, ru, mod, rows, wup_f, wdn_f, l, tm):
    s, d = x1.shape
    nb, _, cu = wup_f.shape
    dff = nb * cu

    def body(dx2_ref, x1_ref, y2_ref, ru_ref, mod_ref, rows_ref, wu_ref, wd_ref, dx1_ref, dy2_ref, dup_ref, act_ref, sm_ref):
        @pl.when(pl.program_id(0) == 0)
        def _():
            sm_ref[...] = jnp.zeros(sm_ref.shape, F32)

        dout = dx2_ref[...]
        dy2, dgt, dgpost = _postnorm_bwd(y2_ref[...], dout, rows_ref[R_G_POST_MLP:R_G_POST_MLP + 1, :], mod_ref[M_GT_F:M_GT_F + 1, :])
        dy2b = dy2.astype(BF16)
        dy2_ref[...] = dy2b
        ruv = ru_ref[...].astype(F32)
        act_ref[...] = (ruv * ruv).astype(BF16)
        dup = (_dot_nt(dy2b, wd_ref[...]) * (2.0 * ruv)).astype(BF16)
        dup_ref[...] = dup
        dh2 = _dot_nt(dup[:, 0:cu], wu_ref[0])
        for j in range(1, nb):
            dh2 = dh2 + _dot_nt(dup[:, j * cu:(j + 1) * cu], wu_ref[j])
        dxn, dsc, dsh, dgpre = _prenorm_bwd(x1_ref[...], dh2, rows_ref[R_G_PRE_MLP:R_G_PRE_MLP + 1, :], mod_ref[M_SC_F:M_SC_F + 1, :])
        dx1_ref[...] = dout + dxn
        for r, v in ((G_MLP_GT, dgt), (G_MLP_GPOST, dgpost), (G_MLP_SC, dsc), (G_MLP_SH, dsh), (G_MLP_GPRE, dgpre)):
            sm_ref[r:r + 1, :] += v

    tile = pl.BlockSpec((tm, d), lambda i: (i, 0))
    wide = pl.BlockSpec((tm, dff), lambda i: (i, 0))
    return pl.pallas_call(
        body,
        name="mlp_bwd",
        grid=(s // tm,),
        in_specs=[
            tile, tile, tile, wide,
            _resident((None, N_MOD, d), lambda i: (l, 0, 0)),
            _resident((None, N_ROWS, d), lambda i: (l, 0, 0)),
            _resident((nb, d, cu), lambda i: (0, 0, 0)),
            _resident((dff, d), lambda i: (0, 0)),
        ],
        out_specs=(tile, tile, wide, wide, pl.BlockSpec((G_MLP_ROWS, d), lambda i: (0, 0))),
        out_shape=(jax.ShapeDtypeStruct((s, d), F32), jax.ShapeDtypeStruct((s, d), BF16), jax.ShapeDtypeStruct((s, dff), BF16),
                   jax.ShapeDtypeStruct((s, dff), BF16), jax.ShapeDtypeStruct((G_MLP_ROWS, d), F32)),
        compiler_params=_cparams(("arbitrary",), 2 * _nbytes((dff, d), BF16) + 6 * _nbytes((tm, dff), F32) + 16 * _nbytes((tm, d), F32)),
    )(*_hbm(dx2, x1, y2, ru), mod, rows, wup_f, wdn_f)


def _mixer_core_bwd(dx1, y, yap, ybp, hs, proj, mod, rows, cwf, wr, wi, wa_f, wb_f, wo_f, l, tm, after):
    s, d = dx1.shape
    nh, bw, _ = wr.shape[1:]
    nt = s // tm
    per = tm // SUBLANES

    def body(dx1_ref, y_ref, yap_ref, ybp_ref, hs_ref, hsh_ref, proj_ref, projh_ref, mod_ref, rows_ref, cw_ref,
             wr_ref, wi_ref, wa_ref, wb_ref, wo_ref, after_ref,
             dproj_ref, dy_ref, m_ref, dyap_ref, dybp_ref, ya_ref, yb_ref, sm_ref, dwg_ref,
             abuf, dcbuf, dxbuf, al_s, dh_s, lam_s, lnext):
        i = pl.program_id(0)
        first_tile = i == nt - 1

        @pl.when(i == 0)
        def _():
            sm_ref[...] = jnp.zeros(sm_ref.shape, F32)
            dwg_ref[...] = jnp.zeros(dwg_ref.shape, F32)
            zero = jnp.zeros((SUBLANES, d), F32)
            abuf[...] = zero
            dcbuf[...] = zero
            dxbuf[...] = zero
            lnext[...] = zero

        def row(r):
            return rows_ref[r:r + 1, :]

        def tap(r):
            return cw_ref[r:r + 1, :]

        def acc(r, v):
            sm_ref[r:r + 1, :] += v

        keep_halo = jnp.where(first_tile, 0.0, 1.0)
        dy, dgt, dgpost = _postnorm_bwd(y_ref[...], dx1_ref[...], row(R_G_POST_MIX), mod_ref[M_GT_M:M_GT_M + 1, :])
        acc(G_MIX_GT, dgt)
        acc(G_MIX_GPOST, dgpost)
        dyb16 = dy.astype(BF16)
        dy_ref[...] = dyb16
        dm = _dot_nt(dyb16, wo_ref[...])
        sa = _sigmoid(proj_ref[:, 5 * d:6 * d])
        sb = _sigmoid(proj_ref[:, 6 * d:7 * d])
        yap = yap_ref[...]
        ybp = ybp_ref[...]
        m_ref[...] = (sa * yap + sb * ybp).astype(BF16)
        dyap_f = dm * sa
        dybp_f = dm * sb
        dyap = dyap_f.astype(BF16)
        dybp = dybp_f.astype(BF16)
        dyap_ref[...] = dyap
        dybp_ref[...] = dybp
        dproj_ref[:, 5 * d:6 * d] = (dyap_f * yap * (1.0 - sa)).astype(BF16)
        dproj_ref[:, 6 * d:7 * d] = (dybp_f * ybp * (1.0 - sb)).astype(BF16)
        dya = _dot_nt(dyap, wa_ref[...])
        dyb = _dot_nt(dybp, wb_ref[...])
        ba = proj_ref[:, 0:d]
        ca = proj_ref[:, d:2 * d]
        va = proj_ref[:, 2 * d:3 * d]
        cv = ca * va
        cvh = keep_halo * (projh_ref[:, d:2 * d] * projh_ref[:, 2 * d:3 * d])
        cvm2 = _shift_rows(cv, cvh, 2, False)
        cvm1 = _shift_rows(cv, cvh, 1, False)
        conv3 = ((row(R_CONV_A_B) + cvm2 * tap(CW_A)) + cvm1 * tap(CW_A + 1)) + cv * tap(CW_A + 2)
        ya_ref[...] = (ba * conv3).astype(BF16)
        dproj_ref[:, 0:d] = (dya * conv3).astype(BF16)
        dc3 = dya * ba
        acc(G_MIX_CAB, _colsum(dc3))
        acc(G_MIX_CAW, _colsum(dc3 * cvm2))
        acc(G_MIX_CAW + 1, _colsum(dc3 * cvm1))
        acc(G_MIX_CAW + 2, _colsum(dc3 * cv))
        dct = dcbuf[...]
        dcv = (dc3 * tap(CW_A + 2) + _shift_rows(dc3, dct, 1, True) * tap(CW_A + 1)) + _shift_rows(dc3, dct, 2, True) * tap(CW_A)
        dcbuf[...] = dc3[:SUBLANES, :]
        dproj_ref[:, d:2 * d] = (dcv * va).astype(BF16)
        dproj_ref[:, 2 * d:3 * d] = (dcv * ca).astype(BF16)
        xb = proj_ref[:, 3 * d:4 * d]
        gb = proj_ref[:, 4 * d:5 * d]
        xbh = keep_halo * projh_ref[:, 3 * d:4 * d]
        xm3 = _shift_rows(xb, xbh, 3, False)
        xm2 = _shift_rows(xb, xbh, 2, False)
        xm1 = _shift_rows(xb, xbh, 1, False)
        xc = (((row(R_CONV_B_B) + xm3 * tap(CW_B)) + xm2 * tap(CW_B + 1)) + xm1 * tap(CW_B + 2)) + xb * tap(CW_B + 3)
        lam = row(R_LAMBDA)
        sp = _softplus_neg(lam)
        xcb, r, ig, a, mult = _gates(xc, wr_ref, wi_ref, row(R_B_GATE_R), row(R_B_GATE_I), sp, nh, bw)
        gel, gsig = _gelu(gb)
        hs = hs_ref[...]
        yb_ref[...] = (hs * gel).astype(BF16)
        dproj_ref[:, 4 * d:5 * d] = (dyb * hs * _gelu_grad(gb, gsig)).astype(BF16)
        alpha = _shift_rows(a, abuf[...], 1, True)
        abuf[...] = a[:SUBLANES, :]
        dhs = dyb * gel
        for h in range(nh):
            al_s[h] = alpha[:, h * bw:(h + 1) * bw]
            dh_s[h] = dhs[:, h * bw:(h + 1) * bw]
        groups = tm // SCAN_ROWS

        def group(j, ln):
            base = pl.multiple_of((groups - 1 - j) * SCAN_ROWS, SCAN_ROWS)
            return jnp.concatenate([_scan_group(al_s, dh_s, lam_s, h, base, ln[:, h * bw:(h + 1) * bw], True) for h in range(nh)], axis=1)

        lnext[...] = lax.fori_loop(0, groups, group, lnext[...])
        dbb = jnp.concatenate([lam_s[h] for h in range(nh)], axis=1)
        da = dbb * _shift_rows(hs, keep_halo * hsh_ref[...], 1, False)
        dbx = dbb * xc
        dmult = dbx * ig
        dig = dbx * mult
        dxc = (dbb * mult) * ig
        dla = a * (da - (dmult * a) / mult)
        dlar = dla * r
        acc(G_MIX_LAM, _colsum(dlar) * (LRU_C * _sigmoid(-lam)))
        dzr = (dlar * (1.0 - r)) * (-LRU_C * sp)
        dzi = dig * ig * (1.0 - ig)
        acc(G_MIX_BR, _colsum(dzr))
        acc(G_MIX_BI, _colsum(dzi))
        dzrb = dzr.astype(BF16)
        dzib = dzi.astype(BF16)
        back = []
        for h in range(nh):
            sl = slice(h * bw, (h + 1) * bw)
            back.append(_dot_nt(dzrb[:, sl], wr_ref[h]) + _dot_nt(dzib[:, sl], wi_ref[h]))
            dwg_ref[0, h] += _dot_tn(xcb[:, sl], dzrb[:, sl])
            dwg_ref[1, h] += _dot_tn(xcb[:, sl], dzib[:, sl])
        dxc = dxc + jnp.concatenate(back, axis=1)
        acc(G_MIX_CBB, _colsum(dxc))
        acc(G_MIX_CBW, _colsum(dxc * xm3))
        acc(G_MIX_CBW + 1, _colsum(dxc * xm2))
        acc(G_MIX_CBW + 2, _colsum(dxc * xm1))
        acc(G_MIX_CBW + 3, _colsum(dxc * xb))
        dxt = dxbuf[...]
        dxb = (((dxc * tap(CW_B + 3) + _shift_rows(dxc, dxt, 1, True) * tap(CW_B + 2)) + _shift_rows(dxc, dxt, 2, True) * tap(CW_B + 1))
               + _shift_rows(dxc, dxt, 3, True) * tap(CW_B))
        dxbuf[...] = dxc[:SUBLANES, :]
        dproj_ref[:, 3 * d:4 * d] = dxb.astype(BF16)

    def rev(i):
        return (nt - 1 - i, 0)

    def halo(i):
        return (jnp.maximum((nt - 1 - i) * per - 1, 0), 0)

    tile = pl.BlockSpec((tm, d), rev)
    return pl.pallas_call(
        body,
        name="mixer_core_bwd",
        grid=(nt,),
        in_specs=[
            tile, tile, tile, tile, tile,
            pl.BlockSpec((SUBLANES, d), halo),
            pl.BlockSpec((tm, 7 * d), rev),
            pl.BlockSpec((SUBLANES, 7 * d), halo),
            _resident((None, N_MOD, d), lambda i: (l, 0, 0)),
            _resident((None, N_ROWS, d), lambda i: (l, 0, 0)),
            _resident((None, CW_ROWS, d), lambda i: (l, 0, 0)),
            _resident((None, nh, bw, bw), lambda i: (l, 0, 0, 0)),
            _resident((None, nh, bw, bw), lambda i: (l, 0, 0, 0)),
            _resident((d, d), lambda i: (0, 0)),
            _resident((d, d), lambda i: (0, 0)),
            _resident((d, d), lambda i: (0, 0)),
            ANY_SPEC,
        ],
        out_specs=(pl.BlockSpec((tm, 7 * d), rev),) + (tile,) * 6 + (
            pl.BlockSpec((G_MIX_ROWS, d), lambda i: (0, 0)), pl.BlockSpec((2, nh, bw, bw), lambda i: (0, 0, 0, 0))),
        out_shape=(jax.ShapeDtypeStruct((s, 7 * d), BF16),) + (jax.ShapeDtypeStruct((s, d), BF16),) * 6 + (
            jax.ShapeDtypeStruct((G_MIX_ROWS, d), F32), jax.ShapeDtypeStruct((2, nh, bw, bw), F32)),
        scratch_shapes=[pltpu.VMEM((SUBLANES, d), F32)] * 3 + [pltpu.VMEM((nh, tm, bw), F32)] * 3 + [pltpu.VMEM((SUBLANES, d), F32)],
        compiler_params=_cparams(("arbitrary",), 3 * _nbytes((d, d), BF16) + 3 * _nbytes((tm, 7 * d), F32) + 64 * _nbytes((tm, d), F32)),
    )(*_hbm(dx1, y, yap, ybp, hs, hs, proj, proj), mod, rows, cwf, wr, wi, wa_f, wb_f, wo_f, after)


def _in_proj_bwd(dproj, x, dx1, mod, rows, win_f, l, tm, after):
    s, d = x.shape
    nb, _, ci = win_f.shape
    n = nb * ci
    wide = _wide(n, ci)

    def body(dp_ref, x_ref, dx1_ref, mod_ref, rows_ref, w_hbm, after_ref, dx_ref, sm_ref, w_s, sem):
        first = pl.program_id(0) == 0
        copies = _side_by_side(w_hbm, w_s, sem)

        @pl.when(first)
        def _():
            for cp in copies:
                cp.start()
            sm_ref[...] = jnp.zeros(sm_ref.shape, F32)

        dh = None
        for c0 in _wide_chunks(copies, n, wide, first):
            part = _dot_nt(dp_ref[:, c0:c0 + wide], w_s[:, c0:c0 + wide])
            dh = part if dh is None else dh + part
        dxn, dsc, dsh, dg = _prenorm_bwd(x_ref[...], dh, rows_ref[R_G_PRE_MIX:R_G_PRE_MIX + 1, :], mod_ref[M_SC_M:M_SC_M + 1, :])
        dx_ref[...] = dx1_ref[...] + dxn
        for r, v in ((G_IN_SC, dsc), (G_IN_SH, dsh), (G_IN_GPRE, dg)):
            sm_ref[r:r + 1, :] += v

    tile = pl.BlockSpec((tm, d), lambda i: (i, 0))
    return pl.pallas_call(
        body,
        name="in_proj_bwd",
        grid=(s // tm,),
        in_specs=[
            pl.BlockSpec((tm, nb * ci), lambda i: (i, 0)), tile, tile,
            _resident((None, N_MOD, d), lambda i: (l, 0, 0)),
            _resident((None, N_ROWS, d), lambda i: (l, 0, 0)),
            ANY_SPEC,
            ANY_SPEC,
        ],
        out_specs=(tile, pl.BlockSpec((G_IN_ROWS, d), lambda i: (0, 0))),
        out_shape=(jax.ShapeDtypeStruct((s, d), F32), jax.ShapeDtypeStruct((G_IN_ROWS, d), F32)),
        scratch_shapes=[pltpu.VMEM((d, nb * ci), BF16), pltpu.SemaphoreType.DMA((nb,))],
        compiler_params=_cparams(("arbitrary",), _nbytes((nb, d, ci), BF16) + 2 * _nbytes((tm, nb * ci), BF16) + 16 * _nbytes((tm, d), F32)),
    )(*_hbm(dproj, x, dx1), mod, rows, *_hbm(win_f), after)


def _wgrad(a, b, cols_owned, after):
    s, k1 = a.shape
    k2 = b.shape[1]
    if cols_owned:
        c = k2 // N_DEV
        per = min(N_DEV, MXU_COLS // math.gcd(c, MXU_COLS))
        assert N_DEV % per == 0
        nblk, bk1 = N_DEV // per, k1
        a_spec = _resident((s, k1), lambda j: (0, 0))
        b_spec = pl.BlockSpec((s, per * c), lambda j: (0, j))
    else:
        per, c = 1, k2
        bk1 = min(WGRAD_ROWS, k1)
        nblk = k1 // bk1
        a_spec = pl.BlockSpec((s, bk1), lambda j: (0, j))
        b_spec = _resident((s, k2), lambda j: (0, 0))
    rows = min(WGRAD_ROWS, bk1)

    def body(a_ref, b_ref, after_ref, o_ref):
        for r0 in range(0, bk1, rows):
            g = _dot_tn(a_ref[:, r0:r0 + rows], b_ref[...]).astype(BF16)
            for p in range(per):
                o_ref[p, r0:r0 + rows, :] = g[:, p * c:(p + 1) * c]

    out = pl.pallas_call(
        body,
        name="wgrad",
        grid=(nblk,),
        in_specs=[a_spec, b_spec, ANY_SPEC],
        out_specs=pl.BlockSpec((per, bk1, c), lambda j: (j, 0, 0)),
        out_shape=pltpu.HBM((nblk * per, bk1, c), BF16),
        compiler_params=_cparams(("parallel",), 4 * _nbytes((bk1, per * c), F32) + 4 * _nbytes((s, bk1 + per * c), BF16)),
    )(pltpu.with_memory_space_constraint(a, pltpu.HBM), pltpu.with_memory_space_constraint(b, pltpu.HBM), after)
    return out if cols_owned else out.reshape(N_DEV, k1 // N_DEV, k2)


def _adam_update(w, g, m, v):
    m = ADAM_B1 * m + (1.0 - ADAM_B1) * g
    v = ADAM_B2 * v + (1.0 - ADAM_B2) * (g * g)
    m_hat = m / (1.0 - ADAM_B1 ** ADAM_STEP)
    v_hat = v / (1.0 - ADAM_B2 ** ADAM_STEP)
    delta = -ADAM_LR * (m_hat / (jnp.sqrt(v_hat) + ADAM_EPS) + ADAM_WD * w)
    return delta, m, v


def _sum_adamw(recv, w, m, v, tr, after):
    nl, ra, cb = w.shape
    assert nl == len(recv) == 2

    def body(r0_ref, r1_ref, w_ref, m_ref, v_ref, after_ref, g_ref, d_ref, nm_ref, nv_ref):
        def total(r_ref):
            g = r_ref[0].astype(F32)
            for j in range(1, N_DEV):
                g = g + r_ref[j].astype(F32)
            return g

        g = jnp.where(pl.program_id(0) == 0, total(r0_ref), total(r1_ref))
        g_ref[...] = g
        d_ref[...], nm_ref[...], nv_ref[...] = _adam_update(w_ref[...], g, m_ref[...], v_ref[...])

    blk = pl.BlockSpec((None, tr, cb), lambda l, i: (l, i, 0))
    return pl.pallas_call(
        body,
        name="sum_adamw",
        grid=(nl, ra // tr),
        in_specs=[pl.BlockSpec((N_DEV, tr, cb), lambda l, i: (0, i * (1 - l), 0)),
                  pl.BlockSpec((N_DEV, tr, cb), lambda l, i: (0, i * l, 0)), blk, blk, blk, ANY_SPEC],
        out_specs=(blk,) * 4,
        out_shape=(jax.ShapeDtypeStruct((nl, ra, cb), F32),) * 4,
        compiler_params=_cparams(("arbitrary", "arbitrary"), 6 * _nbytes((N_DEV, tr, cb), BF16) + 32 * _nbytes((tr, cb), F32)),
    )(recv[0], recv[1], w, m, v, *_hbm(after))


def _adamw(w, g, m, v):
    def body(w_ref, g_ref, m_ref, v_ref, d_ref, nm_ref, nv_ref):
        d_ref[...], nm_ref[...], nv_ref[...] = _adam_update(w_ref[...], g_ref[...], m_ref[...], v_ref[...])

    return pl.pallas_call(
        body,
        name="adamw",
        in_specs=[VMEM_SPEC] * 4,
        out_specs=(VMEM_SPEC,) * 3,
        out_shape=(jax.ShapeDtypeStruct(w.shape, F32),) * 3,
        compiler_params=_cparams(vmem=10 * _nbytes(w.shape, F32)),
    )(w, g, m, v)


def _adamw_tiled(w, g, m, v, tr):
    nl, ra, cb = w.shape

    def body(w_ref, g_ref, m_ref, v_ref, d_ref, nm_ref, nv_ref):
        d_ref[...], nm_ref[...], nv_ref[...] = _adam_update(w_ref[...], g_ref[...], m_ref[...], v_ref[...])

    blk = pl.BlockSpec((None, tr, cb), lambda l, i: (l, i, 0))
    return pl.pallas_call(
        body,
        name="adamw_tiled",
        grid=(nl, ra // tr),
        in_specs=[blk] * 4,
        out_specs=(blk,) * 3,
        out_shape=(jax.ShapeDtypeStruct((nl, ra, cb), F32),) * 3,
        compiler_params=_cparams(("parallel", "parallel")),
    )(*_hbm(w, g, m, v))


def _token_tile(s):
    return min(TOKEN_TILE, max(SUBLANES * 2, s // 4))


def kernel(x, c, w_mod, b_mod, g_pre_mix, g_post_mix, w_in, conv_a_w, conv_a_b, w_a_out, conv_b_w, conv_b_b, w_gate_r, b_gate_r, w_gate_i, b_gate_i, lru_lambda, w_b_out, w_o, g_pre_mlp, g_post_mlp, w_mlp_up, w_mlp_down, loss_target, m_w_mod, m_b_mod, m_g_pre_mix, m_g_post_mix, m_w_in, m_conv_a_w, m_conv_a_b, m_w_a_out, m_conv_b_w, m_conv_b_b, m_w_gate_r, m_b_gate_r, m_w_gate_i, m_b_gate_i, m_lru_lambda, m_w_b_out, m_w_o, m_g_pre_mlp, m_g_post_mlp, m_w_mlp_up, m_w_mlp_down, v_w_mod, v_b_mod, v_g_pre_mix, v_g_post_mix, v_w_in, v_conv_a_w, v_conv_a_b, v_w_a_out, v_conv_b_w, v_conv_b_b, v_w_gate_r, v_b_gate_r, v_w_gate_i, v_b_gate_i, v_lru_lambda, v_w_b_out, v_w_o, v_g_pre_mlp, v_g_post_mlp, v_w_mlp_up, v_w_mlp_down):
    nl = w_mod.shape[0]
    s, d = x.shape[1], x.shape[2]
    nh, bw = w_gate_r.shape[1], w_gate_r.shape[2]
    cwid = conv_a_w.shape[2]
    tm = _token_tile(s)
    tmx = min(2 * tm, s)
    _, me = _my_position()
    xs = x.reshape(s, d)
    target = loss_target.reshape(s, d)

    vec_names = (g_pre_mix, g_post_mix, conv_a_b, conv_b_b, b_gate_r, b_gate_i, lru_lambda, g_pre_mlp, g_post_mlp)
    rows = jnp.concatenate([jnp.stack(vec_names, axis=1), jnp.zeros((nl, N_ROWS - len(vec_names), d), F32)], axis=1)
    cw = jnp.concatenate([conv_a_w, conv_b_w, jnp.zeros((nl, CW_ROWS - 7, cwid), F32)], axis=1)

    large = {"w_in": w_in, "w_a_out": w_a_out, "w_b_out": w_b_out, "w_o": w_o, "w_mlp_up": w_mlp_up, "w_mlp_down": w_mlp_down}
    groups = (("in", ("w_in",)), ("mix", ("w_a_out", "w_b_out", "w_o")), ("mlp", ("w_mlp_up", "w_mlp_down")))
    mod, cact, cwf = _prep_small(c, w_mod, b_mod, cw)
    mod = mod.reshape(nl, N_MOD, d)
    gathers = {}
    tok = mod
    for l in range(nl):
        for gname, members in groups:
            gathers[l, gname] = _gather2_start([large[n][l].astype(BF16) for n in members], tok, f"gather_start_{gname}{l}")
            tok = gathers[l, gname][4]
    wr = w_gate_r.astype(BF16)
    wi = w_gate_i.astype(BF16)

    forwarded = {}

    def forward(l, gname, after):
        forwarded[l, gname] = _gather2_forward(gathers[l, gname], after, f"gather_forward_{gname}{l}")
        return forwarded[l, gname][4]

    def gathered(l, gname, after):
        return _gather2_wait(forwarded[l, gname], after, f"gather_wait_{gname}{l}")

    saved = []
    weights = []
    xin = xs
    for l in range(nl):
        if l == 0:
            tok = forward(0, "in", tok)
        (win_f,) = gathered(l, "in", tok if l == 0 else xin)
        if l > 0:
            tok = forward(l, "mix", win_f)
        proj, h = _in_proj_fwd(xin, mod, rows, win_f, l, tmx, tok)
        if l == 0:
            tok = forward(0, "mix", proj)
        wa_f, wb_f, wo_f = (w.reshape(d, d) for w in gathered(l, "mix", proj))
        if l > 0:
            tok = forward(l, "mlp", wo_f)
        x1, hs, yap, ybp, y = _mixer_core_fwd(proj, xin, mod, rows, cwf, wr, wi, wa_f, wb_f, wo_f, l, tm, tok)
        if l == 0:
            forward(0, "mlp", x1)
        wup_f, wdn_f = gathered(l, "mlp", x1)
        wdn_f = wdn_f.reshape(-1, d)
        if l + 1 < nl:
            tok = forward(l + 1, "in", wdn_f)
        x2, ru, y2, h2, *last = _mlp_fwd(x1, mod, rows, wup_f, wdn_f, l, tmx, tok, target if l + 1 == nl else None)
        saved.append((xin, proj, h, x1, hs, yap, ybp, y, ru, y2, h2))
        weights.append((win_f, wa_f, wb_f, wo_f, wup_f, wdn_f))
        xin = x2
    dx, loss_part = last[0], last[1][0, 0]

    scatters = {}
    small = [None] * nl
    gate_parts = [None] * nl

    def scatter(l, gname, parts):
        scatters[l, gname] = _exchange_start(parts, False, f"scatter_start_{gname}{l}")
        return scatters[l, gname][4]

    for l in reversed(range(nl)):
        xin, proj, h, x1, hs, yap, ybp, y, ru, y2, h2 = saved[l]
        win_f, wa_f, wb_f, wo_f, wup_f, wdn_f = weights[l]
        dx1, dy2, dup, act, sm_mlp = _mlp_bwd(dx, x1, y2, ru, mod, rows, wup_f, wdn_f, l, tm)
        g_up = _wgrad(h2, dup, True, dx1)
        g_dn = _wgrad(act, dy2, False, g_up)
        tok = scatter(l, "mlp", [g_up, g_dn])
        dproj, dy, m, dyap, dybp, ya, yb, sm_mix, dwg = _mixer_core_bwd(
            dx1, y, yap, ybp, hs, proj, mod, rows, cwf, wr, wi, wa_f, wb_f, wo_f, l, tm // 2, tok)
        gate_parts[l] = _exchange_start([dwg.astype(BF16)], True, f"gates_start{l}")
        g_a = _wgrad(ya, dyap, False, gate_parts[l][4])
        g_b = _wgrad(yb, dybp, False, g_a)
        g_o = _wgrad(m, dy, False, g_b)
        tok = scatter(l, "mix", [g_a, g_b, g_o])
        tok = scatter(l, "in", [_wgrad(h, dproj, True, tok)])
        dx, sm_in = _in_proj_bwd(dproj, xin, dx1, mod, rows, win_f, l, tm, tok)
        small[l] = jnp.concatenate([sm_mlp, sm_mix, sm_in], axis=0)
    grad_x = dx.reshape(x.shape)

    recv = {}
    big = {}
    moments = {"w_in": (w_in, m_w_in, v_w_in), "w_mlp_up": (w_mlp_up, m_w_mlp_up, v_w_mlp_up), "w_a_out": (w_a_out, m_w_a_out, v_w_a_out),
               "w_b_out": (w_b_out, m_w_b_out, v_w_b_out), "w_o": (w_o, m_w_o, v_w_o), "w_mlp_down": (w_mlp_down, m_w_mlp_down, v_w_mlp_down)}

    def collect(l, gname, after):
        for n, zone in zip(dict(groups)[gname], _exchange_wait(scatters[l, gname], after, False, f"scatter_wait_{gname}{l}")):
            recv[n, l] = zone

    def update(name, after):
        w, m_, v_ = moments[name]
        big[name] = _sum_adamw([recv[name, l] for l in range(nl)], w, m_, v_, min(ADAM_ROWS, w.shape[1]), after)
        return big[name][1]

    for l, gname in ((1, "mlp"), (1, "in"), (1, "mix"), (0, "mlp")):
        collect(l, gname, dx)
    done = update("w_mlp_down", update("w_mlp_up", dx))
    collect(0, "mix", done)
    for name in ("w_a_out", "w_b_out", "w_o"):
        done = update(name, done)
    collect(0, "in", done)
    done = update("w_in", done)

    lrows = jnp.concatenate(small, axis=0)
    lrows = lrows.at[G_LOSS_ROW, 0].set(loss_part)

    def lrow(a, l, r):
        return a[l * G_LAYER_ROWS + r]

    dm = jnp.stack([jnp.concatenate([lrow(lrows, l, G_MLP_ROWS + G_MIX_ROWS + G_IN_SH), lrow(lrows, l, G_MLP_ROWS + G_MIX_ROWS + G_IN_SC),
                                     lrow(lrows, l, G_MLP_ROWS + G_MIX_GT), lrow(lrows, l, G_MLP_SH), lrow(lrows, l, G_MLP_SC),
                                     lrow(lrows, l, G_MLP_GT)]) for l in range(nl)])
    dm8 = jnp.concatenate([dm[:, None, :], jnp.zeros((nl, SUBLANES - 1, N_MOD * d), F32)], axis=1)
    srows, g_w_mod = _reduce_small(lrows, dm8, cact, done)
    loss = srows[G_LOSS_ROW, 0]
    sgates = _sum_gathered([_exchange_wait(gate_parts[l], srows, True, f"gates_wait{l}")[0] for l in range(nl)])

    def srow(l, r):
        return lrow(srows, l, r)

    def per_layer(r):
        return jnp.stack([srow(l, r) for l in range(nl)])

    mix0 = G_MLP_ROWS
    in0 = G_MLP_ROWS + G_MIX_ROWS
    g_b_mod = jnp.stack([jnp.concatenate([srow(l, in0 + G_IN_SH), srow(l, in0 + G_IN_SC), srow(l, mix0 + G_MIX_GT),
                                          srow(l, G_MLP_SH), srow(l, G_MLP_SC), srow(l, G_MLP_GT)]) for l in range(nl)])
    conv_a_full = jnp.stack([jnp.stack([srow(l, mix0 + G_MIX_CAW + k) for k in range(3)]) for l in range(nl)])
    conv_b_full = jnp.stack([jnp.stack([srow(l, mix0 + G_MIX_CBW + k) for k in range(4)]) for l in range(nl)])
    grads = {
        "b_mod": g_b_mod,
        "g_pre_mix": per_layer(in0 + G_IN_GPRE),
        "g_post_mix": per_layer(mix0 + G_MIX_GPOST),
        "conv_a_w": lax.dynamic_slice_in_dim(conv_a_full, me * cwid, cwid, axis=2),
        "conv_a_b": per_layer(mix0 + G_MIX_CAB),
        "conv_b_w": lax.dynamic_slice_in_dim(conv_b_full, me * cwid, cwid, axis=2),
        "conv_b_b": per_layer(mix0 + G_MIX_CBB),
        "w_gate_r": sgates[:, 0],
        "b_gate_r": per_layer(mix0 + G_MIX_BR),
        "w_gate_i": sgates[:, 1],
        "b_gate_i": per_layer(mix0 + G_MIX_BI),
        "lru_lambda": per_layer(mix0 + G_MIX_LAM),
        "g_pre_mlp": per_layer(G_MLP_GPRE),
        "g_post_mlp": per_layer(G_MLP_GPOST),
    }
    params = {
        "b_mod": (b_mod, m_b_mod, v_b_mod), "g_pre_mix": (g_pre_mix, m_g_pre_mix, v_g_pre_mix), "g_post_mix": (g_post_mix, m_g_post_mix, v_g_post_mix),
        "conv_a_w": (conv_a_w, m_conv_a_w, v_conv_a_w), "conv_a_b": (conv_a_b, m_conv_a_b, v_conv_a_b),
        "conv_b_w": (conv_b_w, m_conv_b_w, v_conv_b_w), "conv_b_b": (conv_b_b, m_conv_b_b, v_conv_b_b),
        "w_gate_r": (w_gate_r, m_w_gate_r, v_w_gate_r), "b_gate_r": (b_gate_r, m_b_gate_r, v_b_gate_r),
        "w_gate_i": (w_gate_i, m_w_gate_i, v_w_gate_i), "b_gate_i": (b_gate_i, m_b_gate_i, v_b_gate_i),
        "lru_lambda": (lru_lambda, m_lru_lambda, v_lru_lambda), "g_pre_mlp": (g_pre_mlp, m_g_pre_mlp, v_g_pre_mlp),
        "g_post_mlp": (g_post_mlp, m_g_post_mlp, v_g_post_mlp),
    }
    out = {}
    for name, g in grads.items():
        w, m_, v_ = params[name]
        flat = (-1, w.shape[-1])
        dl, nm, nv = _adamw(w.reshape(flat), g.reshape(flat), m_.reshape(flat), v_.reshape(flat))
        out[name] = (g.reshape(w.shape), dl.reshape(w.shape), nm.reshape(w.shape), nv.reshape(w.shape))
    out["w_mod"] = (g_w_mod,) + tuple(_adamw_tiled(w_mod, g_w_mod, m_w_mod, v_w_mod, min(ADAM_ROWS, d)))
    out.update(big)

    order = ("w_mod", "b_mod", "g_pre_mix", "g_post_mix", "w_in", "conv_a_w", "conv_a_b", "w_a_out", "conv_b_w", "conv_b_b", "w_gate_r", "b_gate_r",
             "w_gate_i", "b_gate_i", "lru_lambda", "w_b_out", "w_o", "g_pre_mlp", "g_post_mlp", "w_mlp_up", "w_mlp_down")
    return (loss, grad_x) + tuple(out[n][0] for n in order) + tuple(out[n][1] for n in order) + tuple(out[n][2] for n in order) + tuple(out[n][3] for n in order)
```

```python
import math

import jax
import jax.numpy as jnp
from jax import lax
from jax.experimental import pallas as pl
from jax.experimental.pallas import tpu as pltpu

F32, BF16 = jnp.float32, jnp.bfloat16
EPS = 1e-6
LRU_C = 8.0
N_DEV = 8
N_MOD = 6
SUBLANES = 8
VMEM_BUDGET = 56 * 1024 * 1024
MXU_COLS = 256
WGRAD_ROWS = 512
TOKEN_TILE = 256
CHANNEL_CHUNK = 256
ADAM_ROWS = 256
SCAN_ROWS = SUBLANES * SUBLANES
ADAM_LR, ADAM_B1, ADAM_B2, ADAM_EPS, ADAM_WD, ADAM_STEP = 0.001, 0.9, 0.999, 1e-08, 0.01, 10
MESH = pl.DeviceIdType.MESH
VMEM_SPEC = pl.BlockSpec(memory_space=pltpu.VMEM)
ANY_SPEC = pl.BlockSpec(memory_space=pl.ANY)
HBM_SPEC = pl.BlockSpec(memory_space=pltpu.HBM)
SEM_SPEC = pl.BlockSpec(memory_space=pltpu.SEMAPHORE)
SIDE_EFFECT = pltpu.SideEffectType.DATAFLOW_SIDE_EFFECTING

R_G_PRE_MIX, R_G_POST_MIX, R_CONV_A_B, R_CONV_B_B, R_B_GATE_R, R_B_GATE_I, R_LAMBDA, R_G_PRE_MLP, R_G_POST_MLP = range(9)
N_ROWS = 16
M_SH_M, M_SC_M, M_GT_M, M_SH_F, M_SC_F, M_GT_F = range(6)
CW_A, CW_B, CW_ROWS = 0, 3, 8
G_MLP_GT, G_MLP_GPOST, G_MLP_SC, G_MLP_SH, G_MLP_GPRE, G_LOSS_ROW, G_MLP_ROWS = 0, 1, 2, 3, 4, 7, 8
(G_MIX_GT, G_MIX_GPOST, G_MIX_CAB, G_MIX_CAW, G_MIX_CBB, G_MIX_CBW, G_MIX_BR, G_MIX_BI, G_MIX_LAM) = 0, 1, 2, 3, 6, 7, 11, 12, 13
G_MIX_ROWS = 16
G_IN_SC, G_IN_SH, G_IN_GPRE, G_IN_ROWS = 0, 1, 2, 8
G_LAYER_ROWS = G_MLP_ROWS + G_MIX_ROWS + G_IN_ROWS


def _cparams(dims=None, vmem=None):
    kw = {}
    if dims is not None:
        kw["dimension_semantics"] = dims
    if vmem is not None:
        kw["vmem_limit_bytes"] = int(min(max(vmem, 16 * 1024 * 1024), VMEM_BUDGET))
    return pltpu.CompilerParams(**kw)


def _nbytes(shape, dtype):
    n = 1
    for s in shape:
        n *= s
    return n * jnp.dtype(dtype).itemsize


def _hbm(*arrays):
    return tuple(pltpu.with_memory_space_constraint(a, pltpu.HBM) for a in arrays)


def _resident(block, index_map):
    return pl.BlockSpec(block, index_map, pipeline_mode=pl.Buffered(1))


def _my_position():
    x, y, c = lax.axis_index("x"), lax.axis_index("y"), lax.axis_index("c")
    return (x, y, c), 4 * x + 2 * y + c


def _peer(pos, k):
    x, y, c = pos
    px = 1 - x if k & 4 else x
    py = 1 - y if k & 2 else y
    pc = 1 - c if k & 1 else c
    return (px, py, pc), 4 * px + 2 * py + pc


def _remote(src, dst, ssem, rsem, peer):
    return pltpu.make_async_remote_copy(src_ref=src, dst_ref=dst, send_sem=ssem, recv_sem=rsem, device_id=peer, device_id_type=MESH)


def _dot(a, b):
    return jnp.dot(a, b, preferred_element_type=F32)


def _dot_nt(a, b):
    return lax.dot_general(a, b, (((1,), (1,)), ((), ())), preferred_element_type=F32)


def _dot_tn(a, b):
    return lax.dot_general(a, b, (((0,), (0,)), ((), ())), preferred_element_type=F32)


def _colsum(v):
    return jnp.sum(v, axis=0, keepdims=True)


def _sigmoid(v):
    return jax.nn.sigmoid(v)


GELU_K, GELU_C = 0.7978845608028654, 0.044715


def _gelu(v):
    s = 1.0 / (1.0 + jnp.exp(v * (-2.0 * GELU_K - (2.0 * GELU_K * GELU_C) * (v * v))))
    return v * s, s


def _gelu_grad(v, s):
    return s * (1.0 + (v * (1.0 - s)) * (2.0 * GELU_K + (6.0 * GELU_K * GELU_C) * (v * v)))


def _neg_expm1_twice(v):
    t = jnp.tanh(v)
    return (-2.0 * t) / (1.0 - t)


def _softplus_neg(lam):
    z = -lam
    u = jnp.exp(-jnp.abs(z))
    w = 1.0 + u
    l1p = jnp.where(w == 1.0, u, jnp.log(w) * u / (w - 1.0))
    return jnp.maximum(z, 0.0) + l1p


def _rms(v):
    return lax.rsqrt(jnp.mean(v * v, axis=-1, keepdims=True) + EPS)


def _prenorm_bwd(xv, dh, g, sc):
    r = _rms(xv)
    xn = xv * r
    n = xn * g
    dsc = _colsum(dh * n)
    dsh = _colsum(dh)
    dn = dh * (1.0 + sc)
    dg = _colsum(dn * xn)
    dxn = dn * g
    dx = r * (dxn - xn * jnp.mean(dxn * xn, axis=-1, keepdims=True))
    return dx, dsc, dsh, dg


def _postnorm_bwd(yv, dout, g, gt):
    r = _rms(yv)
    yn = yv * r
    dgt = _colsum(dout * (yn * g))
    dn = dout * gt
    dg = _colsum(dn * yn)
    dyn = dn * g
    dy = r * (dyn - yn * jnp.mean(dyn * yn, axis=-1, keepdims=True))
    return dy, dgt, dg


def _gates(xc, wr_ref, wi_ref, b_r, b_i, sp, nh, bw):
    xcb = xc.astype(BF16)
    zr = jnp.concatenate([_dot(xcb[:, h * bw:(h + 1) * bw], wr_ref[h]) for h in range(nh)], axis=1) + b_r
    zi = jnp.concatenate([_dot(xcb[:, h * bw:(h + 1) * bw], wi_ref[h]) for h in range(nh)], axis=1) + b_i
    r = _sigmoid(zr)
    ig = _sigmoid(zi)
    la = (-LRU_C * r) * sp
    a = jnp.exp(la)
    mult = jnp.sqrt(_neg_expm1_twice(la))
    return xcb, r, ig, a, mult


def _shift_rows(cur, edge, k, up):
    t, dd = cur.shape
    blocks = cur.reshape(t // SUBLANES, SUBLANES, dd)
    row = lax.broadcasted_iota(jnp.int32, (1, SUBLANES, dd), 1)
    if up:
        r = pltpu.roll(blocks, SUBLANES - k, 1)
        nxt = jnp.concatenate([r[1:], pltpu.roll(edge, SUBLANES - k, 0)[None]], axis=0)
        out = jnp.where(row >= SUBLANES - k, nxt, r)
    else:
        r = pltpu.roll(blocks, k, 1)
        prv = jnp.concatenate([pltpu.roll(edge, k, 0)[None], r[:-1]], axis=0)
        out = jnp.where(row < k, prv, r)
    return out.reshape(t, dd)


def _scan_group(a_ref, b_ref, o_ref, h, base, carry, reverse):
    def rows(k):
        return pl.ds(base + k, SUBLANES, stride=SUBLANES)

    order = range(SUBLANES - 1, -1, -1) if reverse else range(SUBLANES)
    loc, prod = {}, {}
    prev = None
    for k in order:
        a, b = a_ref[h, rows(k), :], b_ref[h, rows(k), :]
        loc[k] = b if prev is None else a * loc[prev] + b
        prod[k] = a if prev is None else a * prod[prev]
        prev = k
    pa, pb = _scan_block(prod[prev], loc[prev], reverse)
    ends = pb + pa * carry
    row = lax.broadcasted_iota(jnp.int32, ends.shape, 0)
    if reverse:
        into = jnp.where(row < SUBLANES - 1, pltpu.roll(ends, SUBLANES - 1, 0), carry)
        last = ends[0:1, :]
    else:
        into = jnp.where(row >= 1, pltpu.roll(ends, 1, 0), carry)
        last = ends[SUBLANES - 1:SUBLANES, :]
    for k in range(SUBLANES):
        o_ref[h, rows(k), :] = loc[k] + prod[k] * into
    return jnp.broadcast_to(last, ends.shape)


def _scan_block(a8, b8, reverse):
    row = lax.broadcasted_iota(jnp.int32, a8.shape, 0)
    for s in (1, 2, 4):
        if reverse:
            keep = row < SUBLANES - s
            a_sh = pltpu.roll(a8, SUBLANES - s, 0)
            b_sh = pltpu.roll(b8, SUBLANES - s, 0)
        else:
            keep = row >= s
            a_sh = pltpu.roll(a8, s, 0)
            b_sh = pltpu.roll(b8, s, 0)
        b8 = b8 + a8 * jnp.where(keep, b_sh, 0.0)
        a8 = a8 * jnp.where(keep, a_sh, 1.0)
    return a8, b8


def _prep_small(c, w_mod, b_mod, cw):
    d = c.shape[1]
    cm = w_mod.shape[2]
    cwid = cw.shape[2]
    nl = w_mod.shape[0]

    def body(c_ref, wm_ref, bm_ref, cw_ref, mod_ref, cact_ref, cwf_ref, cbuf, pbuf, rbuf, ssem, rsem, lsem):
        pos, me = _my_position()
        me8 = pl.multiple_of(me * SUBLANES, SUBLANES)
        cbuf[pl.ds(me8, SUBLANES), :] = jnp.broadcast_to(c_ref[...], (SUBLANES, d))
        own_cw = pltpu.make_async_copy(cw_ref, cwf_ref.at[:, :, pl.ds(me * cwid, cwid)], lsem.at[0])
        own_cw.start()
        first = []
        for k in range(1, N_DEV):
            peer, _ = _peer(pos, k)
            rows = cbuf.at[pl.ds(me8, SUBLANES), :]
            first.append(_remote(rows, rows, ssem.at[0, k - 1], rsem.at[0, k - 1], peer))
            first.append(_remote(cw_ref, cwf_ref.at[:, :, pl.ds(me * cwid, cwid)], ssem.at[1, k - 1], rsem.at[1, k - 1], peer))
        for cp in first:
            cp.start()
        for k in range(1, N_DEV):
            peer, pj = _peer(pos, k)
            pj8 = pl.multiple_of(pj * SUBLANES, SUBLANES)
            rows = cbuf.at[pl.ds(pj8, SUBLANES), :]
            _remote(rows, rows, ssem.at[0, k - 1], rsem.at[0, k - 1], peer).wait_recv()
        cv = cbuf[...]
        cact = cv * _sigmoid(cv)
        cact_ref[...] = cact
        cb = cact.astype(BF16)
        for l in range(nl):
            pbuf[l] = _dot(cb, wm_ref[l].astype(BF16))
        own_p = pltpu.make_async_copy(pbuf.at[:, pl.ds(me8, SUBLANES), :], rbuf.at[me], lsem.at[1])
        own_p.start()
        second = []
        for k in range(1, N_DEV):
            peer, pj = _peer(pos, k)
            pj8 = pl.multiple_of(pj * SUBLANES, SUBLANES)
            second.append(_remote(pbuf.at[:, pl.ds(pj8, SUBLANES), :], rbuf.at[me], ssem.at[2, k - 1], rsem.at[2, k - 1], peer))
        for cp in second:
            cp.start()
        for k in range(1, N_DEV):
            peer, pj = _peer(pos, k)
            _remote(pbuf.at[:, pl.ds(0, SUBLANES), :], rbuf.at[pj], ssem.at[2, k - 1], rsem.at[2, k - 1], peer).wait_recv()
            _remote(cw_ref, cwf_ref.at[:, :, pl.ds(pj * cwid, cwid)], ssem.at[1, k - 1], rsem.at[1, k - 1], peer).wait_recv()
        own_p.wait()
        own_cw.wait()
        for l in range(nl):
            for j in range(N_DEV):
                mod_ref[l:l + 1, j * cm:(j + 1) * cm] = rbuf[j, l, 0:1, :] + bm_ref[l:l + 1, j * cm:(j + 1) * cm]
        for cp in first + second:
            cp.wait_send()

    return pl.pallas_call(
        body,
        name="prep_small",
        out_shape=(
            jax.ShapeDtypeStruct((nl, N_MOD * d), F32),
            jax.ShapeDtypeStruct((N_DEV * SUBLANES, d), F32),
            jax.ShapeDtypeStruct((nl, CW_ROWS, d), F32),
        ),
        in_specs=[VMEM_SPEC] * 4,
        out_specs=(VMEM_SPEC,) * 3,
        scratch_shapes=[
            pltpu.VMEM((N_DEV * SUBLANES, d), F32),
            pltpu.VMEM((nl, N_DEV * SUBLANES, cm), F32),
            pltpu.VMEM((N_DEV, nl, SUBLANES, cm), F32),
            pltpu.SemaphoreType.DMA((3, N_DEV - 1)),
            pltpu.SemaphoreType.DMA((3, N_DEV - 1)),
            pltpu.SemaphoreType.DMA((2,)),
        ],
        compiler_params=_cparams(vmem=3 * _nbytes(w_mod.shape, F32)),
    )(c, w_mod, b_mod, cw)


def _exchange_start(parts, gather, name):
    n = len(parts)
    lands = [lax.empty(((N_DEV,) + tuple(p.shape)) if gather else tuple(p.shape), p.dtype) for p in parts]

    def body(*refs):
        ins, lnd = refs[:n], refs[n:2 * n]
        ssem, rsem, token = refs[2 * n], refs[2 * n + 1], refs[-1]
        pos, me = _my_position()
        for k in range(1, N_DEV):
            peer, pj = _peer(pos, k)
            for t in range(n):
                src = ins[t] if gather else ins[t].at[pj]
                q = t * (N_DEV - 1) + k - 1
                _remote(src, lnd[t].at[me], ssem.at[q], rsem.at[q], peer).start()
        token[...] = jnp.zeros(token.shape, F32)

    out = pl.pallas_call(
        body,
        name=name,
        out_shape=(pltpu.SemaphoreType.DMA((n * (N_DEV - 1),)), pltpu.SemaphoreType.DMA((n * (N_DEV - 1),)))
        + tuple(pltpu.HBM(p.shape, p.dtype) for p in parts) + tuple(pltpu.HBM(p.shape, p.dtype) for p in lands)
        + (jax.ShapeDtypeStruct((SUBLANES, 128), F32),),
        in_specs=[HBM_SPEC] * (2 * n),
        out_specs=(SEM_SPEC, SEM_SPEC) + (HBM_SPEC,) * (2 * n) + (VMEM_SPEC,),
        input_output_aliases={i: 2 + i for i in range(2 * n)},
        compiler_params=pltpu.CompilerParams(has_side_effects=SIDE_EFFECT),
    )(*[pltpu.with_memory_space_constraint(p, pltpu.HBM) for p in list(parts) + lands])
    return out[0], out[1], list(out[2:2 + n]), list(out[2 + n:2 + 2 * n]), out[-1]


def _exchange_wait(started, after, gather, name):
    ssem, rsem, parts, lands, _ = started
    n = len(parts)

    def body(*refs):
        ins, lnd = refs[:n], refs[n:2 * n]
        ssem_ref, rsem_ref = refs[2 * n], refs[2 * n + 1]
        stage, lsem = refs[-1 - n:-1], refs[-1]
        pos, me = _my_position()
        load = []
        for t in range(n):
            src = ins[t] if gather else ins[t].at[me]
            load.append(pltpu.make_async_copy(src, stage[t], lsem.at[t]))
            load[-1].start()
        store = []
        for t in range(n):
            load[t].wait()
            store.append(pltpu.make_async_copy(stage[t], lnd[t].at[me], lsem.at[t]))
            store[-1].start()
        for k in range(1, N_DEV):
            peer, pj = _peer(pos, k)
            for t in range(n):
                src = ins[t] if gather else ins[t].at[pj]
                q = t * (N_DEV - 1) + k - 1
                _remote(src, lnd[t].at[me], ssem_ref.at[q], rsem_ref.at[q], peer).wait_send()
                _remote(src, lnd[t].at[pj], ssem_ref.at[q], rsem_ref.at[q], peer).wait_recv()
        for cp in store:
            cp.wait()

    out = pl.pallas_call(
        body,
        name=name,
        out_shape=tuple(pltpu.HBM(p.shape, p.dtype) for p in parts) + tuple(pltpu.HBM(p.shape, p.dtype) for p in lands),
        in_specs=[HBM_SPEC] * (2 * n) + [SEM_SPEC, SEM_SPEC, ANY_SPEC],
        out_specs=(HBM_SPEC,) * (2 * n),
        input_output_aliases={i: i for i in range(2 * n)},
        scratch_shapes=[pltpu.VMEM(tuple(z.shape[1:]), z.dtype) for z in lands] + [pltpu.SemaphoreType.DMA((n,))],
        compiler_params=pltpu.CompilerParams(has_side_effects=SIDE_EFFECT),
    )(*parts, *lands, ssem, rsem, after)
    return list(out[n:])


CHIP_PEERS = 3


def _chip_peer(pos, k, core):
    x, y, _ = pos
    px = 1 - x if k & 2 else x
    py = 1 - y if k & 1 else y
    return (px, py, core), 4 * px + 2 * py + core


def _gather2_start(parts, after, name):
    n = len(parts)
    lands = [lax.empty((N_DEV,) + tuple(p.shape), p.dtype) for p in parts]
    per = CHIP_PEERS + 1

    def body(*refs):
        ins, lnd = refs[:n], refs[n:2 * n]
        ssem, rsem, token = refs[2 * n + 1], refs[2 * n + 2], refs[-1]
        pos, me = _my_position()
        sibling = (pos[0], pos[1], 1 - pos[2])
        for t in range(n):
            _remote(ins[t], lnd[t].at[me], ssem.at[per * t], rsem.at[per * t], sibling).start()
        for k in range(1, per):
            peer, _ = _chip_peer(pos, k, pos[2])
            for t in range(n):
                _remote(ins[t], lnd[t].at[me], ssem.at[per * t + k], rsem.at[per * t + k], peer).start()
        token[...] = jnp.zeros(token.shape, F32)

    out = pl.pallas_call(
        body,
        name=name,
        out_shape=(pltpu.SemaphoreType.DMA((n * per,)), pltpu.SemaphoreType.DMA((n * per,)))
        + tuple(pltpu.HBM(p.shape, p.dtype) for p in parts) + tuple(pltpu.HBM(p.shape, p.dtype) for p in lands)
        + (jax.ShapeDtypeStruct((SUBLANES, 128), F32),),
        in_specs=[HBM_SPEC] * (2 * n) + [ANY_SPEC],
        out_specs=(SEM_SPEC, SEM_SPEC) + (HBM_SPEC,) * (2 * n) + (VMEM_SPEC,),
        input_output_aliases={i: 2 + i for i in range(2 * n)},
        compiler_params=pltpu.CompilerParams(has_side_effects=SIDE_EFFECT),
    )(*[pltpu.with_memory_space_constraint(p, pltpu.HBM) for p in list(parts) + lands], after)
    return out[0], out[1], list(out[2:2 + n]), list(out[2 + n:2 + 2 * n]), out[-1]


def _gather2_forward(started, after, name):
    ssem, rsem, parts, lands, _ = started
    n = len(lands)
    per = CHIP_PEERS + 1

    def body(*refs):
        lnd = refs[:n]
        rsem_a, fsend, frecv, token = refs[n], refs[2 * n + 2], refs[2 * n + 3], refs[2 * n + 4]
        pos, _ = _my_position()
        sibling = (pos[0], pos[1], 1 - pos[2])
        for k in range(1, per):
            peer, pk = _chip_peer(pos, k, pos[2])
            for t in range(n):
                block = lnd[t].at[pk]
                _remote(block, block, rsem_a.at[per * t + k], rsem_a.at[per * t + k], peer).wait_recv()
                q = CHIP_PEERS * t + k - 1
                _remote(block, block, fsend.at[q], frecv.at[q], sibling).start()
        token[...] = jnp.zeros(token.shape, F32)

    out = pl.pallas_call(
        body,
        name=name,
        out_shape=tuple(pltpu.HBM(p.shape, p.dtype) for p in lands)
        + (pltpu.SemaphoreType.DMA((n * CHIP_PEERS,)), pltpu.SemaphoreType.DMA((n * CHIP_PEERS,)),
           jax.ShapeDtypeStruct((SUBLANES, 128), F32)),
        in_specs=[HBM_SPEC] * n + [SEM_SPEC, ANY_SPEC],
        out_specs=(HBM_SPEC,) * n + (SEM_SPEC, SEM_SPEC, VMEM_SPEC),
        input_output_aliases={i: i for i in range(n)},
        compiler_params=pltpu.CompilerParams(has_side_effects=SIDE_EFFECT),
    )(*lands, rsem, after)
    return ssem, rsem, parts, list(out[:n]), out[n + 2], out[n], out[n + 1]


def _gather2_wait(forwarded, after, name):
    ssem, rsem, parts, lands, _, fsend, frecv = forwarded
    n = len(parts)
    per = CHIP_PEERS + 1

    def body(*refs):
        ins, lnd = refs[:n], refs[n:2 * n]
        ssem_a, rsem_a, fs, fr = refs[2 * n:2 * n + 4]
        stage, lsem = refs[-1 - n:-1], refs[-1]
        pos, me = _my_position()
        sibling = (pos[0], pos[1], 1 - pos[2])
        sib = 4 * pos[0] + 2 * pos[1] + 1 - pos[2]
        load = []
        for t in range(n):
            load.append(pltpu.make_async_copy(ins[t], stage[t], lsem.at[t]))
            load[-1].start()
        store = []
        for t in range(n):
            load[t].wait()
            store.append(pltpu.make_async_copy(stage[t], lnd[t].at[me], lsem.at[t]))
            store[-1].start()
        for t in range(n):
            _remote(ins[t], lnd[t].at[me], ssem_a.at[per * t], rsem_a.at[per * t], sibling).wait_send()
            _remote(ins[t], lnd[t].at[sib], ssem_a.at[per * t], rsem_a.at[per * t], sibling).wait_recv()
        for k in range(1, per):
            peer, pk = _chip_peer(pos, k, pos[2])
            _, qk = _chip_peer(pos, k, 1 - pos[2])
            for t in range(n):
                q = CHIP_PEERS * t + k - 1
                _remote(ins[t], lnd[t].at[me], ssem_a.at[per * t + k], rsem_a.at[per * t + k], peer).wait_send()
                _remote(lnd[t].at[pk], lnd[t].at[pk], fs.at[q], fr.at[q], sibling).wait_send()
                _remote(lnd[t].at[qk], lnd[t].at[qk], fs.at[q], fr.at[q], sibling).wait_recv()
        for cp in store:
            cp.wait()

    out = pl.pallas_call(
        body,
        name=name,
        out_shape=tuple(pltpu.HBM(p.shape, p.dtype) for p in parts) + tuple(pltpu.HBM(p.shape, p.dtype) for p in lands),
        in_specs=[HBM_SPEC] * (2 * n) + [SEM_SPEC] * 4 + [ANY_SPEC],
        out_specs=(HBM_SPEC,) * (2 * n),
        input_output_aliases={i: i for i in range(2 * n)},
        scratch_shapes=[pltpu.VMEM(tuple(z.shape[1:]), z.dtype) for z in lands] + [pltpu.SemaphoreType.DMA((n,))],
        compiler_params=pltpu.CompilerParams(has_side_effects=SIDE_EFFECT),
    )(*parts, *lands, ssem, rsem, fsend, frecv, after)
    return list(out[n:])


def _reduce_small(rows, dm8, cact, after):
    r, d = rows.shape
    nl = dm8.shape[0]
    cm = dm8.shape[2] // N_DEV
    lw = d // N_DEV

    def body(rows_ref, dm_ref, cact_ref, after_ref, orow_ref, owm_ref, gr, mine, dmr, ssem, rsem, lsem):
        pos, me = _my_position()
        me8 = pl.multiple_of(me * SUBLANES, SUBLANES)

        def lanes(j):
            return pl.ds(j * lw, lw)

        own = [pltpu.make_async_copy(dm_ref.at[:, :, pl.ds(me * cm, cm)], dmr.at[:, pl.ds(me8, SUBLANES), :], lsem.at[0]),
               pltpu.make_async_copy(rows_ref.at[:, lanes(me)], gr.at[me], lsem.at[1])]
        first = []
        for k in range(1, N_DEV):
            peer, pj = _peer(pos, k)
            first.append(_remote(rows_ref.at[:, lanes(pj)], gr.at[me], ssem.at[0, k - 1], rsem.at[0, k - 1], peer))
            first.append(_remote(dm_ref.at[:, :, pl.ds(pj * cm, cm)], dmr.at[:, pl.ds(me8, SUBLANES), :],
                                 ssem.at[1, k - 1], rsem.at[1, k - 1], peer))
        for cp in own + first:
            cp.start()
        for k in range(1, N_DEV):
            peer, pj = _peer(pos, k)
            pj8 = pl.multiple_of(pj * SUBLANES, SUBLANES)
            _remote(rows_ref.at[:, lanes(0)], gr.at[pj], ssem.at[0, k - 1], rsem.at[0, k - 1], peer).wait_recv()
            _remote(dm_ref.at[:, :, pl.ds(0, cm)], dmr.at[:, pl.ds(pj8, SUBLANES), :], ssem.at[1, k - 1], rsem.at[1, k - 1], peer).wait_recv()
        for cp in own:
            cp.wait()
        acc = gr[0]
        for j in range(1, N_DEV):
            acc = acc + gr[j]
        mine[...] = acc
        own_sum = pltpu.make_async_copy(mine, orow_ref.at[:, lanes(me)], lsem.at[1])
        own_sum.start()
        second = []
        for k in range(1, N_DEV):
            peer, _ = _peer(pos, k)
            second.append(_remote(mine, orow_ref.at[:, lanes(me)], ssem.at[2, k - 1], rsem.at[2, k - 1], peer))
            second[-1].start()
        cb = cact_ref[...].astype(BF16)
        for l in range(nl):
            owm_ref[l] = _dot_tn(cb, dmr[l].astype(BF16))
        for k in range(1, N_DEV):
            peer, pj = _peer(pos, k)
            _remote(mine, orow_ref.at[:, lanes(pj)], ssem.at[2, k - 1], rsem.at[2, k - 1], peer).wait_recv()
        own_sum.wait()
        for cp in first + second:
            cp.wait_send()

    return pl.pallas_call(
        body,
        name="reduce_small",
        out_shape=(jax.ShapeDtypeStruct((r, d), F32), jax.ShapeDtypeStruct((nl, d, cm), F32)),
        in_specs=[VMEM_SPEC] * 3 + [ANY_SPEC],
        out_specs=(VMEM_SPEC,) * 2,
        scratch_shapes=[
            pltpu.VMEM((N_DEV, r, lw), F32),
            pltpu.VMEM((r, lw), F32),
            pltpu.VMEM((nl, N_DEV * SUBLANES, cm), F32),
            pltpu.SemaphoreType.DMA((3, N_DEV - 1)),
            pltpu.SemaphoreType.DMA((3, N_DEV - 1)),
            pltpu.SemaphoreType.DMA((2,)),
        ],
        compiler_params=_cparams(vmem=4 * _nbytes((r, d), F32) + 6 * _nbytes((nl, d, cm), F32)),
    )(rows, dm8, cact, *_hbm(after))


def _sum_gathered(zones):
    nl = len(zones)

    def body(*refs):
        for l in range(nl):
            acc = refs[l][0].astype(F32)
            for j in range(1, N_DEV):
                acc = acc + refs[l][j].astype(F32)
            refs[nl][l] = acc

    return pl.pallas_call(
        body,
        name="sum_gathered",
        out_shape=jax.ShapeDtypeStruct((nl,) + tuple(zones[0].shape[1:]), F32),
        in_specs=[VMEM_SPEC] * nl,
        out_specs=VMEM_SPEC,
        compiler_params=_cparams(vmem=8 * nl * _nbytes(zones[0].shape, BF16)),
    )(*zones)


def _load_side_by_side(w_hbm, w_s, sem):
    nb, _, ci = w_hbm.shape
    copies = [pltpu.make_async_copy(w_hbm.at[j], w_s.at[:, j * ci:(j + 1) * ci], sem.at[j]) for j in range(nb)]
    for cp in copies:
        cp.start()
    for cp in copies:
        cp.wait()


def _in_proj_fwd(x, mod, rows, win_f, l, tm, after):
    s, d = x.shape
    nb, _, ci = win_f.shape
    n = nb * ci
    wide = min(n, ci * MXU_COLS // math.gcd(ci, MXU_COLS))

    def body(x_ref, mod_ref, rows_ref, w_hbm, after_ref, proj_ref, h_ref, w_s, sem):
        @pl.when(pl.program_id(0) == 0)
        def _():
            _load_side_by_side(w_hbm, w_s, sem)

        xv = x_ref[...]
        g = rows_ref[R_G_PRE_MIX:R_G_PRE_MIX + 1, :]
        h = (xv * _rms(xv) * g) * (1.0 + mod_ref[M_SC_M:M_SC_M + 1, :]) + mod_ref[M_SH_M:M_SH_M + 1, :]
        hb = h.astype(BF16)
        h_ref[...] = hb
        for c0 in range(0, n, wide):
            proj_ref[:, c0:c0 + wide] = _dot(hb, w_s[:, c0:c0 + wide])

    return pl.pallas_call(
        body,
        name="in_proj_fwd",
        grid=(s // tm,),
        in_specs=[
            pl.BlockSpec((tm, d), lambda i: (i, 0)),
            _resident((None, N_MOD, d), lambda i: (l, 0, 0)),
            _resident((None, N_ROWS, d), lambda i: (l, 0, 0)),
            ANY_SPEC,
            ANY_SPEC,
        ],
        out_specs=(pl.BlockSpec((tm, n), lambda i: (i, 0)), pl.BlockSpec((tm, d), lambda i: (i, 0))),
        out_shape=(jax.ShapeDtypeStruct((s, n), F32), jax.ShapeDtypeStruct((s, d), BF16)),
        scratch_shapes=[pltpu.VMEM((d, n), BF16), pltpu.SemaphoreType.DMA((nb,))],
        compiler_params=_cparams(("arbitrary",), _nbytes((d, n), BF16) + 3 * _nbytes((tm, n), F32) + 8 * _nbytes((tm, d), F32)),
    )(*_hbm(x), mod, rows, *_hbm(win_f), after)


def _mixer_core_fwd(proj, x, mod, rows, cwf, wr, wi, wa_f, wb_f, wo_f, l, tm, after):
    s, d = x.shape
    nh, bw, _ = wr.shape[1:]

    def body(proj_ref, x_ref, mod_ref, rows_ref, cw_ref, wr_ref, wi_ref, wa_ref, wb_ref, wo_ref, after_ref,
             x1_ref, hs_ref, yap_ref, ybp_ref, y_ref, cvbuf, xbbuf, a_s, b_s, h_s, hprev):
        i = pl.program_id(0)

        @pl.when(i == 0)
        def _():
            cvbuf[...] = jnp.zeros((SUBLANES, d), F32)
            xbbuf[...] = jnp.zeros((SUBLANES, d), F32)
            hprev[...] = jnp.zeros((SUBLANES, d), F32)

        def row(r):
            return rows_ref[r:r + 1, :]

        def tap(r):
            return cw_ref[r:r + 1, :]

        ba = proj_ref[:, 0:d]
        cv = proj_ref[:, d:2 * d] * proj_ref[:, 2 * d:3 * d]
        cvt = cvbuf[...]
        conv3 = ((row(R_CONV_A_B) + _shift_rows(cv, cvt, 2, False) * tap(CW_A)) + _shift_rows(cv, cvt, 1, False) * tap(CW_A + 1)) + cv * tap(CW_A + 2)
        ya = ba * conv3
        cvbuf[...] = cv[tm - SUBLANES:, :]
        sp = _softplus_neg(row(R_LAMBDA))
        for h in range(nh):
            lo, hi = h * bw, (h + 1) * bw
            xb = proj_ref[:, 3 * d + lo:3 * d + hi]
            xbt = xbbuf[:, lo:hi]

            def hrow(r):
                return rows_ref[r:r + 1, lo:hi]

            def htap(r):
                return cw_ref[r:r + 1, lo:hi]

            xc = (((hrow(R_CONV_B_B) + _shift_rows(xb, xbt, 3, False) * htap(CW_B)) + _shift_rows(xb, xbt, 2, False) * htap(CW_B + 1))
                  + _shift_rows(xb, xbt, 1, False) * htap(CW_B + 2)) + xb * htap(CW_B + 3)
            xcb = xc.astype(BF16)
            r = _sigmoid(_dot(xcb, wr_ref[h]) + hrow(R_B_GATE_R))
            ig = _sigmoid(_dot(xcb, wi_ref[h]) + hrow(R_B_GATE_I))
            la = (-LRU_C * r) * sp[:, lo:hi]
            a_s[h] = jnp.exp(la)
            b_s[h] = jnp.sqrt(_neg_expm1_twice(la)) * (ig * xc)
        xbbuf[...] = proj_ref[tm - SUBLANES:, 3 * d:4 * d]

        def group(j, hp):
            base = pl.multiple_of(j * SCAN_ROWS, SCAN_ROWS)
            return jnp.concatenate([_scan_group(a_s, b_s, h_s, h, base, hp[:, h * bw:(h + 1) * bw], False) for h in range(nh)], axis=1)

        hprev[...] = lax.fori_loop(0, tm // SCAN_ROWS, group, hprev[...])
        for h in range(nh):
            hs_ref[:, h * bw:(h + 1) * bw] = h_s[h]
        gel, _ = _gelu(proj_ref[:, 4 * d:5 * d])
        yb = hs_ref[...] * gel
        yap = _dot(ya.astype(BF16), wa_ref[...])
        ybp = _dot(yb.astype(BF16), wb_ref[...])
        yap_ref[...] = yap
        ybp_ref[...] = ybp
        m = _sigmoid(proj_ref[:, 5 * d:6 * d]) * yap + _sigmoid(proj_ref[:, 6 * d:7 * d]) * ybp
        y = _dot(m.astype(BF16), wo_ref[...])
        y_ref[...] = y
        x1_ref[...] = x_ref[...] + mod_ref[M_GT_M:M_GT_M + 1, :] * ((y * _rms(y)) * row(R_G_POST_MIX))

    tile = pl.BlockSpec((tm, d), lambda i: (i, 0))
    return pl.pallas_call(
        body,
        name="mixer_core_fwd",
        grid=(s // tm,),
        in_specs=[
            pl.BlockSpec((tm, 7 * d), lambda i: (i, 0)),
            tile,
            _resident((None, N_MOD, d), lambda i: (l, 0, 0)),
            _resident((None, N_ROWS, d), lambda i: (l, 0, 0)),
            _resident((None, CW_ROWS, d), lambda i: (l, 0, 0)),
            _resident((None, nh, bw, bw), lambda i: (l, 0, 0, 0)),
            _resident((None, nh, bw, bw), lambda i: (l, 0, 0, 0)),
            _resident((d, d), lambda i: (0, 0)),
            _resident((d, d), lambda i: (0, 0)),
            _resident((d, d), lambda i: (0, 0)),
            ANY_SPEC,
        ],
        out_specs=(tile,) * 5,
        out_shape=(jax.ShapeDtypeStruct((s, d), F32),) * 5,
        scratch_shapes=[
            pltpu.VMEM((SUBLANES, d), F32),
            pltpu.VMEM((SUBLANES, d), F32),
            pltpu.VMEM((nh, tm, bw), F32),
            pltpu.VMEM((nh, tm, bw), F32),
            pltpu.VMEM((nh, tm, bw), F32),
            pltpu.VMEM((SUBLANES, d), F32),
        ],
        compiler_params=_cparams(("arbitrary",), 3 * _nbytes((d, d), BF16) + 2 * _nbytes((tm, 7 * d), F32) + 40 * _nbytes((tm, d), F32)),
    )(*_hbm(proj, x), mod, rows, cwf, wr, wi, wa_f, wb_f, wo_f, after)


def _mlp_fwd(x1, mod, rows, wup_f, wdn_f, l, tm, after, target=None):
    s, d = x1.shape
    nb, _, cu = wup_f.shape
    dff = nb * cu
    with_loss = target is not None

    def body(x1_ref, mod_ref, rows_ref, wu_ref, wd_ref, after_ref, *refs):
        x2_ref, ru_ref, y2_ref, h2_ref = refs[-6:-2] if with_loss else refs
        xv = x1_ref[...]
        g = rows_ref[R_G_PRE_MLP:R_G_PRE_MLP + 1, :]
        h2 = ((xv * _rms(xv) * g) * (1.0 + mod_ref[M_SC_F:M_SC_F + 1, :]) + mod_ref[M_SH_F:M_SH_F + 1, :]).astype(BF16)
        h2_ref[...] = h2
        ru = jnp.concatenate([jnp.maximum(_dot(h2, wu_ref[j]), 0.0) for j in range(nb)], axis=1)
        ru_ref[...] = ru.astype(BF16)
        y2 = _dot((ru * ru).astype(BF16), wd_ref[...])
        y2_ref[...] = y2
        x2 = xv + mod_ref[M_GT_F:M_GT_F + 1, :] * ((y2 * _rms(y2)) * rows_ref[R_G_POST_MLP:R_G_POST_MLP + 1, :])
        x2_ref[...] = x2
        if with_loss:
            t_ref, dy_ref, loss_ref = refs[0], refs[-2], refs[-1]

            @pl.when(pl.program_id(0) == 0)
            def _():
                loss_ref[...] = jnp.zeros(loss_ref.shape, F32)

            e = x2 - t_ref[...]
            dy_ref[...] = e * (1.0 / d)
            loss_ref[...] += 0.5 * jnp.sum(jnp.mean(e * e, axis=-1, keepdims=True), axis=0, keepdims=True)

    tile = pl.BlockSpec((tm, d), lambda i: (i, 0))
    wide = pl.BlockSpec((tm, dff), lambda i: (i, 0))
    out_specs = (tile, wide, tile, tile)
    out_shape = (jax.ShapeDtypeStruct((s, d), F32), jax.ShapeDtypeStruct((s, dff), BF16),
                 jax.ShapeDtypeStruct((s, d), F32), jax.ShapeDtypeStruct((s, d), BF16))
    if with_loss:
        out_specs += (tile, pl.BlockSpec((SUBLANES, 128), lambda i: (0, 0)))
        out_shape += (jax.ShapeDtypeStruct((s, d), F32), jax.ShapeDtypeStruct((SUBLANES, 128), F32))
    return pl.pallas_call(
        body,
        name="mlp_fwd",
        grid=(s // tm,),
        in_specs=[
            tile,
            _resident((None, N_MOD, d), lambda i: (l, 0, 0)),
            _resident((None, N_ROWS, d), lambda i: (l, 0, 0)),
            _resident((nb, d, cu), lambda i: (0, 0, 0)),
            _resident((dff, d), lambda i: (0, 0)),
            ANY_SPEC,
        ] + ([tile] if with_loss else []),
        out_specs=out_specs,
        out_shape=out_shape,
        compiler_params=_cparams(("arbitrary",), 2 * _nbytes((dff, d), BF16) + 5 * _nbytes((tm, dff), F32) + 16 * _nbytes((tm, d), F32)),
    )(*_hbm(x1), mod, rows, wup_f, wdn_f, after, *(_hbm(target) if with_loss else ()))


def _mlp_bwd(dx2, x1, y2, ru, mod, rows, wup_f, wdn_f, l, tm):
    s, d = x1.shape
    nb, _, cu = wup_f.shape
    dff = nb * cu

    def body(dx2_ref, x1_ref, y2_ref, ru_ref, mod_ref, rows_ref, wu_ref, wd_ref, dx1_ref, dy2_ref, dup_ref, act_ref, sm_ref):
        @pl.when(pl.program_id(0) == 0)
        def _():
            sm_ref[...] = jnp.zeros(sm_ref.shape, F32)

        dout = dx2_ref[...]
        dy2, dgt, dgpost = _postnorm_bwd(y2_ref[...], dout, rows_ref[R_G_POST_MLP:R_G_POST_MLP + 1, :], mod_ref[M_GT_F:M_GT_F + 1, :])
        dy2b = dy2.astype(BF16)
        dy2_ref[...] = dy2b
        ruv = ru_ref[...].astype(F32)
        act_ref[...] = (ruv * ruv).astype(BF16)
        dup = (_dot_nt(dy2b, wd_ref[...]) * (2.0 * ruv)).astype(BF16)
        dup_ref[...] = dup
        dh2 = _dot_nt(dup[:, 0:cu], wu_ref[0])
        for j in range(1, nb):
            dh2 = dh2 + _dot_nt(dup[:, j * cu:(j + 1) * cu], wu_ref[j])
        dxn, dsc, dsh, dgpre = _prenorm_bwd(x1_ref[...], dh2, rows_ref[R_G_PRE_MLP:R_G_PRE_MLP + 1, :], mod_ref[M_SC_F:M_SC_F + 1, :])
        dx1_ref[...] = dout + dxn
        for r, v in ((G_MLP_GT, dgt), (G_MLP_GPOST, dgpost), (G_MLP_SC, dsc), (G_MLP_SH, dsh), (G_MLP_GPRE, dgpre)):
            sm_ref[r:r + 1, :] += v

    tile = pl.BlockSpec((tm, d), lambda i: (i, 0))
    wide = pl.BlockSpec((tm, dff), lambda i: (i, 0))
    return pl.pallas_call(
        body,
        name="mlp_bwd",
        grid=(s // tm,),
        in_specs=[
            tile, tile, tile, wide,
            _resident((None, N_MOD, d), lambda i: (l, 0, 0)),
            _resident((None, N_ROWS, d), lambda i: (l, 0, 0)),
            _resident((nb, d, cu), lambda i: (0, 0, 0)),
            _resident((dff, d), lambda i: (0, 0)),
        ],
        out_specs=(tile, tile, wide, wide, pl.BlockSpec((G_MLP_ROWS, d), lambda i: (0, 0))),
        out_shape=(jax.ShapeDtypeStruct((s, d), F32), jax.ShapeDtypeStruct((s, d), BF16), jax.ShapeDtypeStruct((s, dff), BF16),
                   jax.ShapeDtypeStruct((s, dff), BF16), jax.ShapeDtypeStruct((G_MLP_ROWS, d), F32)),
        compiler_params=_cparams(("arbitrary",), 2 * _nbytes((dff, d), BF16) + 6 * _nbytes((tm, dff), F32) + 16 * _nbytes((tm, d), F32)),
    )(*_hbm(dx2, x1, y2, ru), mod, rows, wup_f, wdn_f)


def _mixer_core_bwd(dx1, y, yap, ybp, hs, proj, mod, rows, cwf, wr, wi, wa_f, wb_f, wo_f, l, tm, after):
    s, d = dx1.shape
    nh, bw, _ = wr.shape[1:]
    nt = s // tm
    per = tm // SUBLANES

    def body(dx1_ref, y_ref, yap_ref, ybp_ref, hs_ref, hsh_ref, proj_ref, projh_ref, mod_ref, rows_ref, cw_ref,
             wr_ref, wi_ref, wa_ref, wb_ref, wo_ref, after_ref,
             dproj_ref, dy_ref, m_ref, dyap_ref, dybp_ref, ya_ref, yb_ref, sm_ref, dwg_ref,
             abuf, dcbuf, dxbuf, al_s, dh_s, lam_s, lnext):
        i = pl.program_id(0)
        first_tile = i == nt - 1

        @pl.when(i == 0)
        def _():
            sm_ref[...] = jnp.zeros(sm_ref.shape, F32)
            dwg_ref[...] = jnp.zeros(dwg_ref.shape, F32)
            zero = jnp.zeros((SUBLANES, d), F32)
            abuf[...] = zero
            dcbuf[...] = zero
            dxbuf[...] = zero
            lnext[...] = zero

        def row(r):
            return rows_ref[r:r + 1, :]

        def tap(r):
            return cw_ref[r:r + 1, :]

        def acc(r, v):
            sm_ref[r:r + 1, :] += v

        keep_halo = jnp.where(first_tile, 0.0, 1.0)
        dy, dgt, dgpost = _postnorm_bwd(y_ref[...], dx1_ref[...], row(R_G_POST_MIX), mod_ref[M_GT_M:M_GT_M + 1, :])
        acc(G_MIX_GT, dgt)
        acc(G_MIX_GPOST, dgpost)
        dyb16 = dy.astype(BF16)
        dy_ref[...] = dyb16
        dm = _dot_nt(dyb16, wo_ref[...])
        sa = _sigmoid(proj_ref[:, 5 * d:6 * d])
        sb = _sigmoid(proj_ref[:, 6 * d:7 * d])
        yap = yap_ref[...]
        ybp = ybp_ref[...]
        m_ref[...] = (sa * yap + sb * ybp).astype(BF16)
        dyap_f = dm * sa
        dybp_f = dm * sb
        dyap = dyap_f.astype(BF16)
        dybp = dybp_f.astype(BF16)
        dyap_ref[...] = dyap
        dybp_ref[...] = dybp
        dproj_ref[:, 5 * d:6 * d] = (dyap_f * yap * (1.0 - sa)).astype(BF16)
        dproj_ref[:, 6 * d:7 * d] = (dybp_f * ybp * (1.0 - sb)).astype(BF16)
        dya = _dot_nt(dyap, wa_ref[...])
        dyb = _dot_nt(dybp, wb_ref[...])
        cc = min(CHANNEL_CHUNK, d)
        for c0 in range(0, d, cc):
            cs = slice(c0, c0 + cc)

            def col(k):
                return slice(k * d + c0, k * d + c0 + cc)

            ba = proj_ref[:, col(0)]
            ca = proj_ref[:, col(1)]
            va = proj_ref[:, col(2)]
            cv = ca * va
            cvh = keep_halo * (projh_ref[:, col(1)] * projh_ref[:, col(2)])
            cvm2 = _shift_rows(cv, cvh, 2, False)
            cvm1 = _shift_rows(cv, cvh, 1, False)
            t0, t1, t2 = (cw_ref[CW_A + k:CW_A + k + 1, cs] for k in range(3))
            conv3 = ((rows_ref[R_CONV_A_B:R_CONV_A_B + 1, cs] + cvm2 * t0) + cvm1 * t1) + cv * t2
            dya_c = dya[:, cs]
            ya_ref[:, cs] = (ba * conv3).astype(BF16)
            dproj_ref[:, col(0)] = (dya_c * conv3).astype(BF16)
            dc3 = dya_c * ba
            sm_ref[G_MIX_CAB:G_MIX_CAB + 1, cs] += _colsum(dc3)
            for k, v in enumerate((cvm2, cvm1, cv)):
                sm_ref[G_MIX_CAW + k:G_MIX_CAW + k + 1, cs] += _colsum(dc3 * v)
            dct = dcbuf[:, cs]
            dcv = (dc3 * t2 + _shift_rows(dc3, dct, 1, True) * t1) + _shift_rows(dc3, dct, 2, True) * t0
            dcbuf[:, cs] = dc3[:SUBLANES, :]
            dproj_ref[:, col(1)] = (dcv * va).astype(BF16)
            dproj_ref[:, col(2)] = (dcv * ca).astype(BF16)
        xb = proj_ref[:, 3 * d:4 * d]
        gb = proj_ref[:, 4 * d:5 * d]
        xbh = keep_halo * projh_ref[:, 3 * d:4 * d]
        xm3 = _shift_rows(xb, xbh, 3, False)
        xm2 = _shift_rows(xb, xbh, 2, False)
        xm1 = _shift_rows(xb, xbh, 1, False)
        xc = (((row(R_CONV_B_B) + xm3 * tap(CW_B)) + xm2 * tap(CW_B + 1)) + xm1 * tap(CW_B + 2)) + xb * tap(CW_B + 3)
        lam = row(R_LAMBDA)
        sp = _softplus_neg(lam)
        xcb, r, ig, a, mult = _gates(xc, wr_ref, wi_ref, row(R_B_GATE_R), row(R_B_GATE_I), sp, nh, bw)
        gel, gsig = _gelu(gb)
        hs = hs_ref[...]
        yb_ref[...] = (hs * gel).astype(BF16)
        dproj_ref[:, 4 * d:5 * d] = (dyb * hs * _gelu_grad(gb, gsig)).astype(BF16)
        alpha = _shift_rows(a, abuf[...], 1, True)
        abuf[...] = a[:SUBLANES, :]
        dhs = dyb * gel
        for h in range(nh):
            al_s[h] = alpha[:, h * bw:(h + 1) * bw]
            dh_s[h] = dhs[:, h * bw:(h + 1) * bw]
        groups = tm // SCAN_ROWS

        def group(j, ln):
            base = pl.multiple_of((groups - 1 - j) * SCAN_ROWS, SCAN_ROWS)
            return jnp.concatenate([_scan_group(al_s, dh_s, lam_s, h, base, ln[:, h * bw:(h + 1) * bw], True) for h in range(nh)], axis=1)

        lnext[...] = lax.fori_loop(0, groups, group, lnext[...])
        dbb = jnp.concatenate([lam_s[h] for h in range(nh)], axis=1)
        da = dbb * _shift_rows(hs, keep_halo * hsh_ref[...], 1, False)
        dbx = dbb * xc
        dmult = dbx * ig
        dig = dbx * mult
        dxc = (dbb * mult) * ig
        dla = a * (da - (dmult * a) / mult)
        dlar = dla * r
        acc(G_MIX_LAM, _colsum(dlar) * (LRU_C * _sigmoid(-lam)))
        dzr = (dlar * (1.0 - r)) * (-LRU_C * sp)
        dzi = dig * ig * (1.0 - ig)
        acc(G_MIX_BR, _colsum(dzr))
        acc(G_MIX_BI, _colsum(dzi))
        dzrb = dzr.astype(BF16)
        dzib = dzi.astype(BF16)
        back = []
        for h in range(nh):
            sl = slice(h * bw, (h + 1) * bw)
            back.append(_dot_nt(dzrb[:, sl], wr_ref[h]) + _dot_nt(dzib[:, sl], wi_ref[h]))
            dwg_ref[0, h] += _dot_tn(xcb[:, sl], dzrb[:, sl])
            dwg_ref[1, h] += _dot_tn(xcb[:, sl], dzib[:, sl])
        dxc = dxc + jnp.concatenate(back, axis=1)
        acc(G_MIX_CBB, _colsum(dxc))
        acc(G_MIX_CBW, _colsum(dxc * xm3))
        acc(G_MIX_CBW + 1, _colsum(dxc * xm2))
        acc(G_MIX_CBW + 2, _colsum(dxc * xm1))
        acc(G_MIX_CBW + 3, _colsum(dxc * xb))
        dxt = dxbuf[...]
        dxb = (((dxc * tap(CW_B + 3) + _shift_rows(dxc, dxt, 1, True) * tap(CW_B + 2)) + _shift_rows(dxc, dxt, 2, True) * tap(CW_B + 1))
               + _shift_rows(dxc, dxt, 3, True) * tap(CW_B))
        dxbuf[...] = dxc[:SUBLANES, :]
        dproj_ref[:, 3 * d:4 * d] = dxb.astype(BF16)

    def rev(i):
        return (nt - 1 - i, 0)

    def halo(i):
        return (jnp.maximum((nt - 1 - i) * per - 1, 0), 0)

    tile = pl.BlockSpec((tm, d), rev)
    return pl.pallas_call(
        body,
        name="mixer_core_bwd",
        grid=(nt,),
        in_specs=[
            tile, tile, tile, tile, tile,
            pl.BlockSpec((SUBLANES, d), halo),
            pl.BlockSpec((tm, 7 * d), rev),
            pl.BlockSpec((SUBLANES, 7 * d), halo),
            _resident((None, N_MOD, d), lambda i: (l, 0, 0)),
            _resident((None, N_ROWS, d), lambda i: (l, 0, 0)),
            _resident((None, CW_ROWS, d), lambda i: (l, 0, 0)),
            _resident((None, nh, bw, bw), lambda i: (l, 0, 0, 0)),
            _resident((None, nh, bw, bw), lambda i: (l, 0, 0, 0)),
            _resident((d, d), lambda i: (0, 0)),
            _resident((d, d), lambda i: (0, 0)),
            _resident((d, d), lambda i: (0, 0)),
            ANY_SPEC,
        ],
        out_specs=(pl.BlockSpec((tm, 7 * d), rev),) + (tile,) * 6 + (
            pl.BlockSpec((G_MIX_ROWS, d), lambda i: (0, 0)), pl.BlockSpec((2, nh, bw, bw), lambda i: (0, 0, 0, 0))),
        out_shape=(jax.ShapeDtypeStruct((s, 7 * d), BF16),) + (jax.ShapeDtypeStruct((s, d), BF16),) * 6 + (
            jax.ShapeDtypeStruct((G_MIX_ROWS, d), F32), jax.ShapeDtypeStruct((2, nh, bw, bw), F32)),
        scratch_shapes=[pltpu.VMEM((SUBLANES, d), F32)] * 3 + [pltpu.VMEM((nh, tm, bw), F32)] * 3 + [pltpu.VMEM((SUBLANES, d), F32)],
        compiler_params=_cparams(("arbitrary",), 3 * _nbytes((d, d), BF16) + 3 * _nbytes((tm, 7 * d), F32) + 64 * _nbytes((tm, d), F32)),
    )(*_hbm(dx1, y, yap, ybp, hs, hs, proj, proj), mod, rows, cwf, wr, wi, wa_f, wb_f, wo_f, after)


def _in_proj_bwd(dproj, x, dx1, mod, rows, win_f, l, tm, after):
    s, d = x.shape
    nb, _, ci = win_f.shape

    def body(dp_ref, x_ref, dx1_ref, mod_ref, rows_ref, w_hbm, after_ref, dx_ref, sm_ref, w_s, sem):
        @pl.when(pl.program_id(0) == 0)
        def _():
            sm_ref[...] = jnp.zeros(sm_ref.shape, F32)
            _load_side_by_side(w_hbm, w_s, sem)

        dh = _dot_nt(dp_ref[...], w_s[...])
        dxn, dsc, dsh, dg = _prenorm_bwd(x_ref[...], dh, rows_ref[R_G_PRE_MIX:R_G_PRE_MIX + 1, :], mod_ref[M_SC_M:M_SC_M + 1, :])
        dx_ref[...] = dx1_ref[...] + dxn
        for r, v in ((G_IN_SC, dsc), (G_IN_SH, dsh), (G_IN_GPRE, dg)):
            sm_ref[r:r + 1, :] += v

    tile = pl.BlockSpec((tm, d), lambda i: (i, 0))
    return pl.pallas_call(
        body,
        name="in_proj_bwd",
        grid=(s // tm,),
        in_specs=[
            pl.BlockSpec((tm, nb * ci), lambda i: (i, 0)), tile, tile,
            _resident((None, N_MOD, d), lambda i: (l, 0, 0)),
            _resident((None, N_ROWS, d), lambda i: (l, 0, 0)),
            ANY_SPEC,
            ANY_SPEC,
        ],
        out_specs=(tile, pl.BlockSpec((G_IN_ROWS, d), lambda i: (0, 0))),
        out_shape=(jax.ShapeDtypeStruct((s, d), F32), jax.ShapeDtypeStruct((G_IN_ROWS, d), F32)),
        scratch_shapes=[pltpu.VMEM((d, nb * ci), BF16), pltpu.SemaphoreType.DMA((nb,))],
        compiler_params=_cparams(("arbitrary",), _nbytes((nb, d, ci), BF16) + 2 * _nbytes((tm, nb * ci), BF16) + 16 * _nbytes((tm, d), F32)),
    )(*_hbm(dproj, x, dx1), mod, rows, *_hbm(win_f), after)


def _wgrad(a, b, cols_owned, after):
    s, k1 = a.shape
    k2 = b.shape[1]
    if cols_owned:
        c = k2 // N_DEV
        per = min(N_DEV, MXU_COLS // math.gcd(c, MXU_COLS))
        assert N_DEV % per == 0
        nblk, bk1 = N_DEV // per, k1
        a_spec = _resident((s, k1), lambda j: (0, 0))
        b_spec = pl.BlockSpec((s, per * c), lambda j: (0, j))
    else:
        per, c = 1, k2
        bk1 = min(WGRAD_ROWS, k1)
        nblk = k1 // bk1
        a_spec = pl.BlockSpec((s, bk1), lambda j: (0, j))
        b_spec = _resident((s, k2), lambda j: (0, 0))
    rows = min(WGRAD_ROWS, bk1)

    def body(a_ref, b_ref, after_ref, o_ref):
        for r0 in range(0, bk1, rows):
            g = _dot_tn(a_ref[:, r0:r0 + rows], b_ref[...]).astype(BF16)
            for p in range(per):
                o_ref[p, r0:r0 + rows, :] = g[:, p * c:(p + 1) * c]

    out = pl.pallas_call(
        body,
        name="wgrad",
        grid=(nblk,),
        in_specs=[a_spec, b_spec, ANY_SPEC],
        out_specs=pl.BlockSpec((per, bk1, c), lambda j: (j, 0, 0)),
        out_shape=pltpu.HBM((nblk * per, bk1, c), BF16),
        compiler_params=_cparams(("parallel",), 4 * _nbytes((bk1, per * c), F32) + 4 * _nbytes((s, bk1 + per * c), BF16)),
    )(pltpu.with_memory_space_constraint(a, pltpu.HBM), pltpu.with_memory_space_constraint(b, pltpu.HBM), after)
    return out if cols_owned else out.reshape(N_DEV, k1 // N_DEV, k2)


def _adam_update(w, g, m, v):
    m = ADAM_B1 * m + (1.0 - ADAM_B1) * g
    v = ADAM_B2 * v + (1.0 - ADAM_B2) * (g * g)
    m_hat = m / (1.0 - ADAM_B1 ** ADAM_STEP)
    v_hat = v / (1.0 - ADAM_B2 ** ADAM_STEP)
    delta = -ADAM_LR * (m_hat / (jnp.sqrt(v_hat) + ADAM_EPS) + ADAM_WD * w)
    return delta, m, v


def _sum_adamw(recv, w, m, v, tr, after):
    nl, ra, cb = w.shape
    assert nl == len(recv) == 2

    def body(r0_ref, r1_ref, w_ref, m_ref, v_ref, after_ref, g_ref, d_ref, nm_ref, nv_ref):
        def total(r_ref):
            g = r_ref[0].astype(F32)
            for j in range(1, N_DEV):
                g = g + r_ref[j].astype(F32)
            return g

        g = jnp.where(pl.program_id(0) == 0, total(r0_ref), total(r1_ref))
        g_ref[...] = g
        d_ref[...], nm_ref[...], nv_ref[...] = _adam_update(w_ref[...], g, m_ref[...], v_ref[...])

    blk = pl.BlockSpec((None, tr, cb), lambda l, i: (l, i, 0))
    return pl.pallas_call(
        body,
        name="sum_adamw",
        grid=(nl, ra // tr),
        in_specs=[pl.BlockSpec((N_DEV, tr, cb), lambda l, i: (0, i * (1 - l), 0)),
                  pl.BlockSpec((N_DEV, tr, cb), lambda l, i: (0, i * l, 0)), blk, blk, blk, ANY_SPEC],
        out_specs=(blk,) * 4,
        out_shape=(jax.ShapeDtypeStruct((nl, ra, cb), F32),) * 4,
        compiler_params=_cparams(("arbitrary", "arbitrary"), 6 * _nbytes((N_DEV, tr, cb), BF16) + 32 * _nbytes((tr, cb), F32)),
    )(recv[0], recv[1], w, m, v, *_hbm(after))


def _adamw(w, g, m, v):
    def body(w_ref, g_ref, m_ref, v_ref, d_ref, nm_ref, nv_ref):
        d_ref[...], nm_ref[...], nv_ref[...] = _adam_update(w_ref[...], g_ref[...], m_ref[...], v_ref[...])

    return pl.pallas_call(
        body,
        name="adamw",
        in_specs=[VMEM_SPEC] * 4,
        out_specs=(VMEM_SPEC,) * 3,
        out_shape=(jax.ShapeDtypeStruct(w.shape, F32),) * 3,
        compiler_params=_cparams(vmem=10 * _nbytes(w.shape, F32)),
    )(w, g, m, v)


def _adamw_tiled(w, g, m, v, tr):
    nl, ra, cb = w.shape

    def body(w_ref, g_ref, m_ref, v_ref, d_ref, nm_ref, nv_ref):
        d_ref[...], nm_ref[...], nv_ref[...] = _adam_update(w_ref[...], g_ref[...], m_ref[...], v_ref[...])

    blk = pl.BlockSpec((None, tr, cb), lambda l, i: (l, i, 0))
    return pl.pallas_call(
        body,
        name="adamw_tiled",
        grid=(nl, ra // tr),
        in_specs=[blk] * 4,
        out_specs=(blk,) * 3,
        out_shape=(jax.ShapeDtypeStruct((nl, ra, cb), F32),) * 3,
        compiler_params=_cparams(("parallel", "parallel")),
    )(*_hbm(w, g, m, v))


def _token_tile(s):
    return min(TOKEN_TILE, max(SUBLANES * 2, s // 4))


def kernel(x, c, w_mod, b_mod, g_pre_mix, g_post_mix, w_in, conv_a_w, conv_a_b, w_a_out, conv_b_w, conv_b_b, w_gate_r, b_gate_r, w_gate_i, b_gate_i, lru_lambda, w_b_out, w_o, g_pre_mlp, g_post_mlp, w_mlp_up, w_mlp_down, loss_target, m_w_mod, m_b_mod, m_g_pre_mix, m_g_post_mix, m_w_in, m_conv_a_w, m_conv_a_b, m_w_a_out, m_conv_b_w, m_conv_b_b, m_w_gate_r, m_b_gate_r, m_w_gate_i, m_b_gate_i, m_lru_lambda, m_w_b_out, m_w_o, m_g_pre_mlp, m_g_post_mlp, m_w_mlp_up, m_w_mlp_down, v_w_mod, v_b_mod, v_g_pre_mix, v_g_post_mix, v_w_in, v_conv_a_w, v_conv_a_b, v_w_a_out, v_conv_b_w, v_conv_b_b, v_w_gate_r, v_b_gate_r, v_w_gate_i, v_b_gate_i, v_lru_lambda, v_w_b_out, v_w_o, v_g_pre_mlp, v_g_post_mlp, v_w_mlp_up, v_w_mlp_down):
    nl = w_mod.shape[0]
    s, d = x.shape[1], x.shape[2]
    nh, bw = w_gate_r.shape[1], w_gate_r.shape[2]
    cwid = conv_a_w.shape[2]
    tm = _token_tile(s)
    tmx = min(2 * tm, s)
    _, me = _my_position()
    xs = x.reshape(s, d)
    target = loss_target.reshape(s, d)

    vec_names = (g_pre_mix, g_post_mix, conv_a_b, conv_b_b, b_gate_r, b_gate_i, lru_lambda, g_pre_mlp, g_post_mlp)
    rows = jnp.concatenate([jnp.stack(vec_names, axis=1), jnp.zeros((nl, N_ROWS - len(vec_names), d), F32)], axis=1)
    cw = jnp.concatenate([conv_a_w, conv_b_w, jnp.zeros((nl, CW_ROWS - 7, cwid), F32)], axis=1)

    large = {"w_in": w_in, "w_a_out": w_a_out, "w_b_out": w_b_out, "w_o": w_o, "w_mlp_up": w_mlp_up, "w_mlp_down": w_mlp_down}
    groups = (("in", ("w_in",)), ("mix", ("w_a_out", "w_b_out", "w_o")), ("mlp", ("w_mlp_up", "w_mlp_down")))
    mod, cact, cwf = _prep_small(c, w_mod, b_mod, cw)
    mod = mod.reshape(nl, N_MOD, d)
    gathers = {}
    tok = mod
    for l in range(nl):
        for gname, members in groups:
            gathers[l, gname] = _gather2_start([large[n][l].astype(BF16) for n in members], tok, f"gather_start_{gname}{l}")
            tok = gathers[l, gname][4]
    wr = w_gate_r.astype(BF16)
    wi = w_gate_i.astype(BF16)

    forwarded = {}

    def forward(l, gname, after):
        forwarded[l, gname] = _gather2_forward(gathers[l, gname], after, f"gather_forward_{gname}{l}")
        return forwarded[l, gname][4]

    def gathered(l, gname, after):
        return _gather2_wait(forwarded[l, gname], after, f"gather_wait_{gname}{l}")

    saved = []
    weights = []
    xin = xs
    for l in range(nl):
        if l == 0:
            tok = forward(0, "in", tok)
        (win_f,) = gathered(l, "in", tok if l == 0 else xin)
        if l > 0:
            tok = forward(l, "mix", win_f)
        proj, h = _in_proj_fwd(xin, mod, rows, win_f, l, tmx, tok)
        if l == 0:
            tok = forward(0, "mix", proj)
        wa_f, wb_f, wo_f = (w.reshape(d, d) for w in gathered(l, "mix", proj))
        if l > 0:
            tok = forward(l, "mlp", wo_f)
        x1, hs, yap, ybp, y = _mixer_core_fwd(proj, xin, mod, rows, cwf, wr, wi, wa_f, wb_f, wo_f, l, tm, tok)
        if l == 0:
            forward(0, "mlp", x1)
        wup_f, wdn_f = gathered(l, "mlp", x1)
        wdn_f = wdn_f.reshape(-1, d)
        if l + 1 < nl:
            tok = forward(l + 1, "in", wdn_f)
        x2, ru, y2, h2, *last = _mlp_fwd(x1, mod, rows, wup_f, wdn_f, l, tmx, tok, target if l + 1 == nl else None)
        saved.append((xin, proj, h, x1, hs, yap, ybp, y, ru, y2, h2))
        weights.append((win_f, wa_f, wb_f, wo_f, wup_f, wdn_f))
        xin = x2
    dx, loss_part = last[0], last[1][0, 0]

    scatters = {}
    small = [None] * nl
    gate_parts = [None] * nl

    def scatter(l, gname, parts):
        scatters[l, gname] = _exchange_start(parts, False, f"scatter_start_{gname}{l}")
        return scatters[l, gname][4]

    for l in reversed(range(nl)):
        xin, proj, h, x1, hs, yap, ybp, y, ru, y2, h2 = saved[l]
        win_f, wa_f, wb_f, wo_f, wup_f, wdn_f = weights[l]
        dx1, dy2, dup, act, sm_mlp = _mlp_bwd(dx, x1, y2, ru, mod, rows, wup_f, wdn_f, l, tm)
        g_up = _wgrad(h2, dup, True, dx1)
        g_dn = _wgrad(act, dy2, False, g_up)
        tok = scatter(l, "mlp", [g_up, g_dn])
        dproj, dy, m, dyap, dybp, ya, yb, sm_mix, dwg = _mixer_core_bwd(
            dx1, y, yap, ybp, hs, proj, mod, rows, cwf, wr, wi, wa_f, wb_f, wo_f, l, tm // 2, tok)
        gate_parts[l] = _exchange_start([dwg.astype(BF16)], True, f"gates_start{l}")
        g_a = _wgrad(ya, dyap, False, gate_parts[l][4])
        g_b = _wgrad(yb, dybp, False, g_a)
        g_o = _wgrad(m, dy, False, g_b)
        tok = scatter(l, "mix", [g_a, g_b, g_o])
        tok = scatter(l, "in", [_wgrad(h, dproj, True, tok)])
        dx, sm_in = _in_proj_bwd(dproj, xin, dx1, mod, rows, win_f, l, tm, tok)
        small[l] = jnp.concatenate([sm_mlp, sm_mix, sm_in], axis=0)
    grad_x = dx.reshape(x.shape)

    recv = {}
    big = {}
    moments = {"w_in": (w_in, m_w_in, v_w_in), "w_mlp_up": (w_mlp_up, m_w_mlp_up, v_w_mlp_up), "w_a_out": (w_a_out, m_w_a_out, v_w_a_out),
               "w_b_out": (w_b_out, m_w_b_out, v_w_b_out), "w_o": (w_o, m_w_o, v_w_o), "w_mlp_down": (w_mlp_down, m_w_mlp_down, v_w_mlp_down)}

    def collect(l, gname, after):
        for n, zone in zip(dict(groups)[gname], _exchange_wait(scatters[l, gname], after, False, f"scatter_wait_{gname}{l}")):
            recv[n, l] = zone

    def update(name, after):
        w, m_, v_ = moments[name]
        big[name] = _sum_adamw([recv[name, l] for l in range(nl)], w, m_, v_, min(ADAM_ROWS, w.shape[1]), after)
        return big[name][1]

    for l, gname in ((1, "mlp"), (1, "in"), (1, "mix"), (0, "mlp")):
        collect(l, gname, dx)
    done = update("w_mlp_down", update("w_mlp_up", dx))
    collect(0, "mix", done)
    for name in ("w_a_out", "w_b_out", "w_o"):
        done = update(name, done)
    collect(0, "in", done)
    done = update("w_in", done)

    lrows = jnp.concatenate(small, axis=0)
    lrows = lrows.at[G_LOSS_ROW, 0].set(loss_part)

    def lrow(a, l, r):
        return a[l * G_LAYER_ROWS + r]

    dm = jnp.stack([jnp.concatenate([lrow(lrows, l, G_MLP_ROWS + G_MIX_ROWS + G_IN_SH), lrow(lrows, l, G_MLP_ROWS + G_MIX_ROWS + G_IN_SC),
                                     lrow(lrows, l, G_MLP_ROWS + G_MIX_GT), lrow(lrows, l, G_MLP_SH), lrow(lrows, l, G_MLP_SC),
                                     lrow(lrows, l, G_MLP_GT)]) for l in range(nl)])
    dm8 = jnp.concatenate([dm[:, None, :], jnp.zeros((nl, SUBLANES - 1, N_MOD * d), F32)], axis=1)
    srows, g_w_mod = _reduce_small(lrows, dm8, cact, done)
    loss = srows[G_LOSS_ROW, 0]
    sgates = _sum_gathered([_exchange_wait(gate_parts[l], srows, True, f"gates_wait{l}")[0] for l in range(nl)])

    def srow(l, r):
        return lrow(srows, l, r)

    def per_layer(r):
        return jnp.stack([srow(l, r) for l in range(nl)])

    mix0 = G_MLP_ROWS
    in0 = G_MLP_ROWS + G_MIX_ROWS
    g_b_mod = jnp.stack([jnp.concatenate([srow(l, in0 + G_IN_SH), srow(l, in0 + G_IN_SC), srow(l, mix0 + G_MIX_GT),
                                          srow(l, G_MLP_SH), srow(l, G_MLP_SC), srow(l, G_MLP_GT)]) for l in range(nl)])
    conv_a_full = jnp.stack([jnp.stack([srow(l, mix0 + G_MIX_CAW + k) for k in range(3)]) for l in range(nl)])
    conv_b_full = jnp.stack([jnp.stack([srow(l, mix0 + G_MIX_CBW + k) for k in range(4)]) for l in range(nl)])
    grads = {
        "b_mod": g_b_mod,
        "g_pre_mix": per_layer(in0 + G_IN_GPRE),
        "g_post_mix": per_layer(mix0 + G_MIX_GPOST),
        "conv_a_w": lax.dynamic_slice_in_dim(conv_a_full, me * cwid, cwid, axis=2),
        "conv_a_b": per_layer(mix0 + G_MIX_CAB),
        "conv_b_w": lax.dynamic_slice_in_dim(conv_b_full, me * cwid, cwid, axis=2),
        "conv_b_b": per_layer(mix0 + G_MIX_CBB),
        "w_gate_r": sgates[:, 0],
        "b_gate_r": per_layer(mix0 + G_MIX_BR),
        "w_gate_i": sgates[:, 1],
        "b_gate_i": per_layer(mix0 + G_MIX_BI),
        "lru_lambda": per_layer(mix0 + G_MIX_LAM),
        "g_pre_mlp": per_layer(G_MLP_GPRE),
        "g_post_mlp": per_layer(G_MLP_GPOST),
    }
    params = {
        "b_mod": (b_mod, m_b_mod, v_b_mod), "g_pre_mix": (g_pre_mix, m_g_pre_mix, v_g_pre_mix), "g_post_mix": (g_post_mix, m_g_post_mix, v_g_post_mix),
        "conv_a_w": (conv_a_w, m_conv_a_w, v_conv_a_w), "conv_a_b": (conv_a_b, m_conv_a_b, v_conv_a_b),
        "conv_b_w": (conv_b_w, m_conv_b_w, v_conv_b_w), "conv_b_b": (conv_b_b, m_conv_b_b, v_conv_b_b),
        "w_gate_r": (w_gate_r, m_w_gate_r, v_w_gate_r), "b_gate_r": (b_gate_r, m_b_gate_r, v_b_gate_r),
        "w_gate_i": (w_gate_i, m_w_gate_i, v_w_gate_i), "b_gate_i": (b_gate_i, m_b_gate_i, v_b_gate_i),
        "lru_lambda": (lru_lambda, m_lru_lambda, v_lru_lambda), "g_pre_mlp": (g_pre_mlp, m_g_pre_mlp, v_g_pre_mlp),
        "g_post_mlp": (g_post_mlp, m_g_post_mlp, v_g_post_mlp),
    }
    out = {}
    for name, g in grads.items():
        w, m_, v_ = params[name]
        flat = (-1, w.shape[-1])
        dl, nm, nv = _adamw(w.reshape(flat), g.reshape(flat), m_.reshape(flat), v_.reshape(flat))
        out[name] = (g.reshape(w.shape), dl.reshape(w.shape), nm.reshape(w.shape), nv.reshape(w.shape))
    out["w_mod"] = (g_w_mod,) + tuple(_adamw_tiled(w_mod, g_w_mod, m_w_mod, v_w_mod, min(ADAM_ROWS, d)))
    out.update(big)

    order = ("w_mod", "b_mod", "g_pre_mix", "g_post_mix", "w_in", "conv_a_w", "conv_a_b", "w_a_out", "conv_b_w", "conv_b_b", "w_gate_r", "b_gate_r",
             "w_gate_i", "b_gate_i", "lru_lambda", "w_b_out", "w_o", "g_pre_mlp", "g_post_mlp", "w_mlp_up", "w_mlp_down")
    return (loss, grad_x) + tuple(out[n][0] for n in order) + tuple(out[n][1] for n in order) + tuple(out[n][2] for n in order) + tuple(out[n][3] for n in order)
```

```python
import math

import jax
import jax.numpy as jnp
from jax import lax
from jax.experimental import pallas as pl
from jax.experimental.pallas import tpu as pltpu

F32, BF16 = jnp.float32, jnp.bfloat16
EPS = 1e-6
LRU_C = 8.0
N_DEV = 8
N_MOD = 6
SUBLANES = 8
VMEM_BUDGET = 56 * 1024 * 1024
MXU_COLS = 256
WGRAD_ROWS = 512
TOKEN_TILE = 256
ADAM_ROWS = 256
SCAN_ROWS = SUBLANES * SUBLANES
ADAM_LR, ADAM_B1, ADAM_B2, ADAM_EPS, ADAM_WD, ADAM_STEP = 0.001, 0.9, 0.999, 1e-08, 0.01, 10
MESH = pl.DeviceIdType.MESH
VMEM_SPEC = pl.BlockSpec(memory_space=pltpu.VMEM)
ANY_SPEC = pl.BlockSpec(memory_space=pl.ANY)
HBM_SPEC = pl.BlockSpec(memory_space=pltpu.HBM)
SEM_SPEC = pl.BlockSpec(memory_space=pltpu.SEMAPHORE)
SIDE_EFFECT = pltpu.SideEffectType.DATAFLOW_SIDE_EFFECTING

R_G_PRE_MIX, R_G_POST_MIX, R_CONV_A_B, R_CONV_B_B, R_B_GATE_R, R_B_GATE_I, R_LAMBDA, R_G_PRE_MLP, R_G_POST_MLP = range(9)
N_ROWS = 16
M_SH_M, M_SC_M, M_GT_M, M_SH_F, M_SC_F, M_GT_F = range(6)
CW_A, CW_B, CW_ROWS = 0, 3, 8
G_MLP_GT, G_MLP_GPOST, G_MLP_SC, G_MLP_SH, G_MLP_GPRE, G_LOSS_ROW, G_MLP_ROWS = 0, 1, 2, 3, 4, 7, 8
(G_MIX_GT, G_MIX_GPOST, G_MIX_CAB, G_MIX_CAW, G_MIX_CBB, G_MIX_CBW, G_MIX_BR, G_MIX_BI, G_MIX_LAM) = 0, 1, 2, 3, 6, 7, 11, 12, 13
G_MIX_ROWS = 16
G_IN_SC, G_IN_SH, G_IN_GPRE, G_IN_ROWS = 0, 1, 2, 8
G_LAYER_ROWS = G_MLP_ROWS + G_MIX_ROWS + G_IN_ROWS


def _cparams(dims=None, vmem=None):
    kw = {}
    if dims is not None:
        kw["dimension_semantics"] = dims
    if vmem is not None:
        kw["vmem_limit_bytes"] = int(min(max(vmem, 16 * 1024 * 1024), VMEM_BUDGET))
    return pltpu.CompilerParams(**kw)


def _nbytes(shape, dtype):
    n = 1
    for s in shape:
        n *= s
    return n * jnp.dtype(dtype).itemsize


def _hbm(*arrays):
    return tuple(pltpu.with_memory_space_constraint(a, pltpu.HBM) for a in arrays)


def _resident(block, index_map):
    return pl.BlockSpec(block, index_map, pipeline_mode=pl.Buffered(1))


def _my_position():
    x, y, c = lax.axis_index("x"), lax.axis_index("y"), lax.axis_index("c")
    return (x, y, c), 4 * x + 2 * y + c


def _peer(pos, k):
    x, y, c = pos
    px = 1 - x if k & 4 else x
    py = 1 - y if k & 2 else y
    pc = 1 - c if k & 1 else c
    return (px, py, pc), 4 * px + 2 * py + pc


def _remote(src, dst, ssem, rsem, peer):
    return pltpu.make_async_remote_copy(src_ref=src, dst_ref=dst, send_sem=ssem, recv_sem=rsem, device_id=peer, device_id_type=MESH)


def _dot(a, b):
    return jnp.dot(a, b, preferred_element_type=F32)


def _dot_nt(a, b):
    return lax.dot_general(a, b, (((1,), (1,)), ((), ())), preferred_element_type=F32)


def _dot_tn(a, b):
    return lax.dot_general(a, b, (((0,), (0,)), ((), ())), preferred_element_type=F32)


def _colsum(v):
    return jnp.sum(v, axis=0, keepdims=True)


def _sigmoid(v):
    return jax.nn.sigmoid(v)


GELU_K, GELU_C = 0.7978845608028654, 0.044715


def _gelu(v):
    s = 1.0 / (1.0 + jnp.exp(v * (-2.0 * GELU_K - (2.0 * GELU_K * GELU_C) * (v * v))))
    return v * s, s


def _gelu_grad(v, s):
    return s * (1.0 + (v * (1.0 - s)) * (2.0 * GELU_K + (6.0 * GELU_K * GELU_C) * (v * v)))


def _neg_expm1_twice(v):
    t = jnp.tanh(v)
    return (-2.0 * t) / (1.0 - t)


def _softplus_neg(lam):
    z = -lam
    u = jnp.exp(-jnp.abs(z))
    w = 1.0 + u
    l1p = jnp.where(w == 1.0, u, jnp.log(w) * u / (w - 1.0))
    return jnp.maximum(z, 0.0) + l1p


def _rms(v):
    return lax.rsqrt(jnp.mean(v * v, axis=-1, keepdims=True) + EPS)


def _prenorm_bwd(xv, dh, g, sc):
    r = _rms(xv)
    xn = xv * r
    n = xn * g
    dsc = _colsum(dh * n)
    dsh = _colsum(dh)
    dn = dh * (1.0 + sc)
    dg = _colsum(dn * xn)
    dxn = dn * g
    dx = r * (dxn - xn * jnp.mean(dxn * xn, axis=-1, keepdims=True))
    return dx, dsc, dsh, dg


def _postnorm_bwd(yv, dout, g, gt):
    r = _rms(yv)
    yn = yv * r
    dgt = _colsum(dout * (yn * g))
    dn = dout * gt
    dg = _colsum(dn * yn)
    dyn = dn * g
    dy = r * (dyn - yn * jnp.mean(dyn * yn, axis=-1, keepdims=True))
    return dy, dgt, dg


def _gates(xc, wr_ref, wi_ref, b_r, b_i, sp, nh, bw):
    xcb = xc.astype(BF16)
    zr = jnp.concatenate([_dot(xcb[:, h * bw:(h + 1) * bw], wr_ref[h]) for h in range(nh)], axis=1) + b_r
    zi = jnp.concatenate([_dot(xcb[:, h * bw:(h + 1) * bw], wi_ref[h]) for h in range(nh)], axis=1) + b_i
    r = _sigmoid(zr)
    ig = _sigmoid(zi)
    la = (-LRU_C * r) * sp
    a = jnp.exp(la)
    mult = jnp.sqrt(_neg_expm1_twice(la))
    return xcb, r, ig, a, mult


def _shift_rows(cur, edge, k, up):
    t, dd = cur.shape
    blocks = cur.reshape(t // SUBLANES, SUBLANES, dd)
    row = lax.broadcasted_iota(jnp.int32, (1, SUBLANES, dd), 1)
    if up:
        r = pltpu.roll(blocks, SUBLANES - k, 1)
        nxt = jnp.concatenate([r[1:], pltpu.roll(edge, SUBLANES - k, 0)[None]], axis=0)
        out = jnp.where(row >= SUBLANES - k, nxt, r)
    else:
        r = pltpu.roll(blocks, k, 1)
        prv = jnp.concatenate([pltpu.roll(edge, k, 0)[None], r[:-1]], axis=0)
        out = jnp.where(row < k, prv, r)
    return out.reshape(t, dd)


def _scan_group(a_ref, b_ref, o_ref, h, base, carry, reverse):
    def rows(k):
        return pl.ds(base + k, SUBLANES, stride=SUBLANES)

    order = range(SUBLANES - 1, -1, -1) if reverse else range(SUBLANES)
    loc, prod = {}, {}
    prev = None
    for k in order:
        a, b = a_ref[h, rows(k), :], b_ref[h, rows(k), :]
        loc[k] = b if prev is None else a * loc[prev] + b
        prod[k] = a if prev is None else a * prod[prev]
        prev = k
    pa, pb = _scan_block(prod[prev], loc[prev], reverse)
    ends = pb + pa * carry
    row = lax.broadcasted_iota(jnp.int32, ends.shape, 0)
    if reverse:
        into = jnp.where(row < SUBLANES - 1, pltpu.roll(ends, SUBLANES - 1, 0), carry)
        last = ends[0:1, :]
    else:
        into = jnp.where(row >= 1, pltpu.roll(ends, 1, 0), carry)
        last = ends[SUBLANES - 1:SUBLANES, :]
    for k in range(SUBLANES):
        o_ref[h, rows(k), :] = loc[k] + prod[k] * into
    return jnp.broadcast_to(last, ends.shape)


def _scan_block(a8, b8, reverse):
    row = lax.broadcasted_iota(jnp.int32, a8.shape, 0)
    for s in (1, 2, 4):
        if reverse:
            keep = row < SUBLANES - s
            a_sh = pltpu.roll(a8, SUBLANES - s, 0)
            b_sh = pltpu.roll(b8, SUBLANES - s, 0)
        else:
            keep = row >= s
            a_sh = pltpu.roll(a8, s, 0)
            b_sh = pltpu.roll(b8, s, 0)
        b8 = b8 + a8 * jnp.where(keep, b_sh, 0.0)
        a8 = a8 * jnp.where(keep, a_sh, 1.0)
    return a8, b8


def _prep_small(c, w_mod, b_mod, cw):
    d = c.shape[1]
    cm = w_mod.shape[2]
    cwid = cw.shape[2]
    nl = w_mod.shape[0]

    def body(c_ref, wm_ref, bm_ref, cw_ref, mod_ref, cact_ref, cwf_ref, cbuf, pbuf, rbuf, ssem, rsem, lsem):
        pos, me = _my_position()
        me8 = pl.multiple_of(me * SUBLANES, SUBLANES)
        cbuf[pl.ds(me8, SUBLANES), :] = jnp.broadcast_to(c_ref[...], (SUBLANES, d))
        own_cw = pltpu.make_async_copy(cw_ref, cwf_ref.at[:, :, pl.ds(me * cwid, cwid)], lsem.at[0])
        own_cw.start()
        first = []
        for k in range(1, N_DEV):
            peer, _ = _peer(pos, k)
            rows = cbuf.at[pl.ds(me8, SUBLANES), :]
            first.append(_remote(rows, rows, ssem.at[0, k - 1], rsem.at[0, k - 1], peer))
            first.append(_remote(cw_ref, cwf_ref.at[:, :, pl.ds(me * cwid, cwid)], ssem.at[1, k - 1], rsem.at[1, k - 1], peer))
        for cp in first:
            cp.start()
        for k in range(1, N_DEV):
            peer, pj = _peer(pos, k)
            pj8 = pl.multiple_of(pj * SUBLANES, SUBLANES)
            rows = cbuf.at[pl.ds(pj8, SUBLANES), :]
            _remote(rows, rows, ssem.at[0, k - 1], rsem.at[0, k - 1], peer).wait_recv()
        cv = cbuf[...]
        cact = cv * _sigmoid(cv)
        cact_ref[...] = cact
        cb = cact.astype(BF16)
        for l in range(nl):
            pbuf[l] = _dot(cb, wm_ref[l].astype(BF16))
        own_p = pltpu.make_async_copy(pbuf.at[:, pl.ds(me8, SUBLANES), :], rbuf.at[me], lsem.at[1])
        own_p.start()
        second = []
        for k in range(1, N_DEV):
            peer, pj = _peer(pos, k)
            pj8 = pl.multiple_of(pj * SUBLANES, SUBLANES)
            second.append(_remote(pbuf.at[:, pl.ds(pj8, SUBLANES), :], rbuf.at[me], ssem.at[2, k - 1], rsem.at[2, k - 1], peer))
        for cp in second:
            cp.start()
        for k in range(1, N_DEV):
            peer, pj = _peer(pos, k)
            _remote(pbuf.at[:, pl.ds(0, SUBLANES), :], rbuf.at[pj], ssem.at[2, k - 1], rsem.at[2, k - 1], peer).wait_recv()
            _remote(cw_ref, cwf_ref.at[:, :, pl.ds(pj * cwid, cwid)], ssem.at[1, k - 1], rsem.at[1, k - 1], peer).wait_recv()
        own_p.wait()
        own_cw.wait()
        for l in range(nl):
            for j in range(N_DEV):
                mod_ref[l:l + 1, j * cm:(j + 1) * cm] = rbuf[j, l, 0:1, :] + bm_ref[l:l + 1, j * cm:(j + 1) * cm]
        for cp in first + second:
            cp.wait_send()

    return pl.pallas_call(
        body,
        name="prep_small",
        out_shape=(
            jax.ShapeDtypeStruct((nl, N_MOD * d), F32),
            jax.ShapeDtypeStruct((N_DEV * SUBLANES, d), F32),
            jax.ShapeDtypeStruct((nl, CW_ROWS, d), F32),
        ),
        in_specs=[VMEM_SPEC] * 4,
        out_specs=(VMEM_SPEC,) * 3,
        scratch_shapes=[
            pltpu.VMEM((N_DEV * SUBLANES, d), F32),
            pltpu.VMEM((nl, N_DEV * SUBLANES, cm), F32),
            pltpu.VMEM((N_DEV, nl, SUBLANES, cm), F32),
            pltpu.SemaphoreType.DMA((3, N_DEV - 1)),
            pltpu.SemaphoreType.DMA((3, N_DEV - 1)),
            pltpu.SemaphoreType.DMA((2,)),
        ],
        compiler_params=_cparams(vmem=3 * _nbytes(w_mod.shape, F32)),
    )(c, w_mod, b_mod, cw)


def _exchange_start(parts, gather, name):
    n = len(parts)
    lands = [lax.empty(((N_DEV,) + tuple(p.shape)) if gather else tuple(p.shape), p.dtype) for p in parts]

    def body(*refs):
        ins, lnd = refs[:n], refs[n:2 * n]
        ssem, rsem, token = refs[2 * n], refs[2 * n + 1], refs[-1]
        pos, me = _my_position()
        for k in range(1, N_DEV):
            peer, pj = _peer(pos, k)
            for t in range(n):
                src = ins[t] if gather else ins[t].at[pj]
                q = t * (N_DEV - 1) + k - 1
                _remote(src, lnd[t].at[me], ssem.at[q], rsem.at[q], peer).start()
        token[...] = jnp.zeros(token.shape, F32)

    out = pl.pallas_call(
        body,
        name=name,
        out_shape=(pltpu.SemaphoreType.DMA((n * (N_DEV - 1),)), pltpu.SemaphoreType.DMA((n * (N_DEV - 1),)))
        + tuple(pltpu.HBM(p.shape, p.dtype) for p in parts) + tuple(pltpu.HBM(p.shape, p.dtype) for p in lands)
        + (jax.ShapeDtypeStruct((SUBLANES, 128), F32),),
        in_specs=[HBM_SPEC] * (2 * n),
        out_specs=(SEM_SPEC, SEM_SPEC) + (HBM_SPEC,) * (2 * n) + (VMEM_SPEC,),
        input_output_aliases={i: 2 + i for i in range(2 * n)},
        compiler_params=pltpu.CompilerParams(has_side_effects=SIDE_EFFECT),
    )(*[pltpu.with_memory_space_constraint(p, pltpu.HBM) for p in list(parts) + lands])
    return out[0], out[1], list(out[2:2 + n]), list(out[2 + n:2 + 2 * n]), out[-1]


def _exchange_wait(started, after, gather, name):
    ssem, rsem, parts, lands, _ = started
    n = len(parts)

    def body(*refs):
        ins, lnd = refs[:n], refs[n:2 * n]
        ssem_ref, rsem_ref = refs[2 * n], refs[2 * n + 1]
        stage, lsem = refs[-1 - n:-1], refs[-1]
        pos, me = _my_position()
        load = []
        for t in range(n):
            src = ins[t] if gather else ins[t].at[me]
            load.append(pltpu.make_async_copy(src, stage[t], lsem.at[t]))
            load[-1].start()
        store = []
        for t in range(n):
            load[t].wait()
            store.append(pltpu.make_async_copy(stage[t], lnd[t].at[me], lsem.at[t]))
            store[-1].start()
        for k in range(1, N_DEV):
            peer, pj = _peer(pos, k)
            for t in range(n):
                src = ins[t] if gather else ins[t].at[pj]
                q = t * (N_DEV - 1) + k - 1
                _remote(src, lnd[t].at[me], ssem_ref.at[q], rsem_ref.at[q], peer).wait_send()
                _remote(src, lnd[t].at[pj], ssem_ref.at[q], rsem_ref.at[q], peer).wait_recv()
        for cp in store:
            cp.wait()

    out = pl.pallas_call(
        body,
        name=name,
        out_shape=tuple(pltpu.HBM(p.shape, p.dtype) for p in parts) + tuple(pltpu.HBM(p.shape, p.dtype) for p in lands),
        in_specs=[HBM_SPEC] * (2 * n) + [SEM_SPEC, SEM_SPEC, ANY_SPEC],
        out_specs=(HBM_SPEC,) * (2 * n),
        input_output_aliases={i: i for i in range(2 * n)},
        scratch_shapes=[pltpu.VMEM(tuple(z.shape[1:]), z.dtype) for z in lands] + [pltpu.SemaphoreType.DMA((n,))],
        compiler_params=pltpu.CompilerParams(has_side_effects=SIDE_EFFECT),
    )(*parts, *lands, ssem, rsem, after)
    return list(out[n:])


CHIP_PEERS = 3


def _chip_peer(pos, k, core):
    x, y, _ = pos
    px = 1 - x if k & 2 else x
    py = 1 - y if k & 1 else y
    return (px, py, core), 4 * px + 2 * py + core


def _gather2_start(parts, after, name):
    n = len(parts)
    lands = [lax.empty((N_DEV,) + tuple(p.shape), p.dtype) for p in parts]
    per = CHIP_PEERS + 1

    def body(*refs):
        ins, lnd = refs[:n], refs[n:2 * n]
        ssem, rsem, token = refs[2 * n + 1], refs[2 * n + 2], refs[-1]
        pos, me = _my_position()
        sibling = (pos[0], pos[1], 1 - pos[2])
        for t in range(n):
            _remote(ins[t], lnd[t].at[me], ssem.at[per * t], rsem.at[per * t], sibling).start()
        for k in range(1, per):
            peer, _ = _chip_peer(pos, k, pos[2])
            for t in range(n):
                _remote(ins[t], lnd[t].at[me], ssem.at[per * t + k], rsem.at[per * t + k], peer).start()
        token[...] = jnp.zeros(token.shape, F32)

    out = pl.pallas_call(
        body,
        name=name,
        out_shape=(pltpu.SemaphoreType.DMA((n * per,)), pltpu.SemaphoreType.DMA((n * per,)))
        + tuple(pltpu.HBM(p.shape, p.dtype) for p in parts) + tuple(pltpu.HBM(p.shape, p.dtype) for p in lands)
        + (jax.ShapeDtypeStruct((SUBLANES, 128), F32),),
        in_specs=[HBM_SPEC] * (2 * n) + [ANY_SPEC],
        out_specs=(SEM_SPEC, SEM_SPEC) + (HBM_SPEC,) * (2 * n) + (VMEM_SPEC,),
        input_output_aliases={i: 2 + i for i in range(2 * n)},
        compiler_params=pltpu.CompilerParams(has_side_effects=SIDE_EFFECT),
    )(*[pltpu.with_memory_space_constraint(p, pltpu.HBM) for p in list(parts) + lands], after)
    return out[0], out[1], list(out[2:2 + n]), list(out[2 + n:2 + 2 * n]), out[-1]


def _gather2_forward(started, after, name):
    ssem, rsem, parts, lands, _ = started
    n = len(lands)
    per = CHIP_PEERS + 1

    def body(*refs):
        lnd = refs[:n]
        rsem_a, fsend, frecv, token = refs[n], refs[2 * n + 2], refs[2 * n + 3], refs[2 * n + 4]
        pos, _ = _my_position()
        sibling = (pos[0], pos[1], 1 - pos[2])
        for k in range(1, per):
            peer, pk = _chip_peer(pos, k, pos[2])
            for t in range(n):
                block = lnd[t].at[pk]
                _remote(block, block, rsem_a.at[per * t + k], rsem_a.at[per * t + k], peer).wait_recv()
                q = CHIP_PEERS * t + k - 1
                _remote(block, block, fsend.at[q], frecv.at[q], sibling).start()
        token[...] = jnp.zeros(token.shape, F32)

    out = pl.pallas_call(
        body,
        name=name,
        out_shape=tuple(pltpu.HBM(p.shape, p.dtype) for p in lands)
        + (pltpu.SemaphoreType.DMA((n * CHIP_PEERS,)), pltpu.SemaphoreType.DMA((n * CHIP_PEERS,)),
           jax.ShapeDtypeStruct((SUBLANES, 128), F32)),
        in_specs=[HBM_SPEC] * n + [SEM_SPEC, ANY_SPEC],
        out_specs=(HBM_SPEC,) * n + (SEM_SPEC, SEM_SPEC, VMEM_SPEC),
        input_output_aliases={i: i for i in range(n)},
        compiler_params=pltpu.CompilerParams(has_side_effects=SIDE_EFFECT),
    )(*lands, rsem, after)
    return ssem, rsem, parts, list(out[:n]), out[n + 2], out[n], out[n + 1]


def _gather2_wait(forwarded, after, name):
    ssem, rsem, parts, lands, _, fsend, frecv = forwarded
    n = len(parts)
    per = CHIP_PEERS + 1

    def body(*refs):
        ins, lnd = refs[:n], refs[n:2 * n]
        ssem_a, rsem_a, fs, fr = refs[2 * n:2 * n + 4]
        stage, lsem = refs[-1 - n:-1], refs[-1]
        pos, me = _my_position()
        sibling = (pos[0], pos[1], 1 - pos[2])
        sib = 4 * pos[0] + 2 * pos[1] + 1 - pos[2]
        load = []
        for t in range(n):
            load.append(pltpu.make_async_copy(ins[t], stage[t], lsem.at[t]))
            load[-1].start()
        store = []
        for t in range(n):
            load[t].wait()
            store.append(pltpu.make_async_copy(stage[t], lnd[t].at[me], lsem.at[t]))
            store[-1].start()
        for t in range(n):
            _remote(ins[t], lnd[t].at[me], ssem_a.at[per * t], rsem_a.at[per * t], sibling).wait_send()
            _remote(ins[t], lnd[t].at[sib], ssem_a.at[per * t], rsem_a.at[per * t], sibling).wait_recv()
        for k in range(1, per):
            peer, pk = _chip_peer(pos, k, pos[2])
            _, qk = _chip_peer(pos, k, 1 - pos[2])
            for t in range(n):
                q = CHIP_PEERS * t + k - 1
                _remote(ins[t], lnd[t].at[me], ssem_a.at[per * t + k], rsem_a.at[per * t + k], peer).wait_send()
                _remote(lnd[t].at[pk], lnd[t].at[pk], fs.at[q], fr.at[q], sibling).wait_send()
                _remote(lnd[t].at[qk], lnd[t].at[qk], fs.at[q], fr.at[q], sibling).wait_recv()
        for cp in store:
            cp.wait()

    out = pl.pallas_call(
        body,
        name=name,
        out_shape=tuple(pltpu.HBM(p.shape, p.dtype) for p in parts) + tuple(pltpu.HBM(p.shape, p.dtype) for p in lands),
        in_specs=[HBM_SPEC] * (2 * n) + [SEM_SPEC] * 4 + [ANY_SPEC],
        out_specs=(HBM_SPEC,) * (2 * n),
        input_output_aliases={i: i for i in range(2 * n)},
        scratch_shapes=[pltpu.VMEM(tuple(z.shape[1:]), z.dtype) for z in lands] + [pltpu.SemaphoreType.DMA((n,))],
        compiler_params=pltpu.CompilerParams(has_side_effects=SIDE_EFFECT),
    )(*parts, *lands, ssem, rsem, fsend, frecv, after)
    return list(out[n:])


def _reduce_small(rows, dm8, cact, after):
    r, d = rows.shape
    nl = dm8.shape[0]
    cm = dm8.shape[2] // N_DEV
    lw = d // N_DEV

    def body(rows_ref, dm_ref, cact_ref, after_ref, orow_ref, owm_ref, gr, mine, dmr, ssem, rsem, lsem):
        pos, me = _my_position()
        me8 = pl.multiple_of(me * SUBLANES, SUBLANES)

        def lanes(j):
            return pl.ds(j * lw, lw)

        own = [pltpu.make_async_copy(dm_ref.at[:, :, pl.ds(me * cm, cm)], dmr.at[:, pl.ds(me8, SUBLANES), :], lsem.at[0]),
               pltpu.make_async_copy(rows_ref.at[:, lanes(me)], gr.at[me], lsem.at[1])]
        first = []
        for k in range(1, N_DEV):
            peer, pj = _peer(pos, k)
            first.append(_remote(rows_ref.at[:, lanes(pj)], gr.at[me], ssem.at[0, k - 1], rsem.at[0, k - 1], peer))
            first.append(_remote(dm_ref.at[:, :, pl.ds(pj * cm, cm)], dmr.at[:, pl.ds(me8, SUBLANES), :],
                                 ssem.at[1, k - 1], rsem.at[1, k - 1], peer))
        for cp in own + first:
            cp.start()
        for k in range(1, N_DEV):
            peer, pj = _peer(pos, k)
            pj8 = pl.multiple_of(pj * SUBLANES, SUBLANES)
            _remote(rows_ref.at[:, lanes(0)], gr.at[pj], ssem.at[0, k - 1], rsem.at[0, k - 1], peer).wait_recv()
            _remote(dm_ref.at[:, :, pl.ds(0, cm)], dmr.at[:, pl.ds(pj8, SUBLANES), :], ssem.at[1, k - 1], rsem.at[1, k - 1], peer).wait_recv()
        for cp in own:
            cp.wait()
        acc = gr[0]
        for j in range(1, N_DEV):
            acc = acc + gr[j]
        mine[...] = acc
        own_sum = pltpu.make_async_copy(mine, orow_ref.at[:, lanes(me)], lsem.at[1])
        own_sum.start()
        second = []
        for k in range(1, N_DEV):
            peer, _ = _peer(pos, k)
            second.append(_remote(mine, orow_ref.at[:, lanes(me)], ssem.at[2, k - 1], rsem.at[2, k - 1], peer))
            second[-1].start()
        cb = cact_ref[...].astype(BF16)
        for l in range(nl):
            owm_ref[l] = _dot_tn(cb, dmr[l].astype(BF16))
        for k in range(1, N_DEV):
            peer, pj = _peer(pos, k)
            _remote(mine, orow_ref.at[:, lanes(pj)], ssem.at[2, k - 1], rsem.at[2, k - 1], peer).wait_recv()
        own_sum.wait()
        for cp in first + second:
            cp.wait_send()

    return pl.pallas_call(
        body,
        name="reduce_small",
        out_shape=(jax.ShapeDtypeStruct((r, d), F32), jax.ShapeDtypeStruct((nl, d, cm), F32)),
        in_specs=[VMEM_SPEC] * 3 + [ANY_SPEC],
        out_specs=(VMEM_SPEC,) * 2,
        scratch_shapes=[
            pltpu.VMEM((N_DEV, r, lw), F32),
            pltpu.VMEM((r, lw), F32),
            pltpu.VMEM((nl, N_DEV * SUBLANES, cm), F32),
            pltpu.SemaphoreType.DMA((3, N_DEV - 1)),
            pltpu.SemaphoreType.DMA((3, N_DEV - 1)),
            pltpu.SemaphoreType.DMA((2,)),
        ],
        compiler_params=_cparams(vmem=4 * _nbytes((r, d), F32) + 6 * _nbytes((nl, d, cm), F32)),
    )(rows, dm8, cact, *_hbm(after))


def _sum_gathered(zones):
    nl = len(zones)

    def body(*refs):
        for l in range(nl):
            acc = refs[l][0].astype(F32)
            for j in range(1, N_DEV):
                acc = acc + refs[l][j].astype(F32)
            refs[nl][l] = acc

    return pl.pallas_call(
        body,
        name="sum_gathered",
        out_shape=jax.ShapeDtypeStruct((nl,) + tuple(zones[0].shape[1:]), F32),
        in_specs=[VMEM_SPEC] * nl,
        out_specs=VMEM_SPEC,
        compiler_params=_cparams(vmem=8 * nl * _nbytes(zones[0].shape, BF16)),
    )(*zones)


def _load_side_by_side(w_hbm, w_s, sem):
    nb, _, ci = w_hbm.shape
    copies = [pltpu.make_async_copy(w_hbm.at[j], w_s.at[:, j * ci:(j + 1) * ci], sem.at[j]) for j in range(nb)]
    for cp in copies:
        cp.start()
    for cp in copies:
        cp.wait()


def _in_proj_fwd(x, mod, rows, win_f, l, tm, after):
    s, d = x.shape
    nb, _, ci = win_f.shape
    n = nb * ci
    wide = min(n, ci * MXU_COLS // math.gcd(ci, MXU_COLS))

    def body(x_ref, mod_ref, rows_ref, w_hbm, after_ref, proj_ref, h_ref, w_s, sem):
        first = pl.program_id(0) == 0
        copies = [pltpu.make_async_copy(w_hbm.at[j], w_s.at[:, j * ci:(j + 1) * ci], sem.at[j]) for j in range(nb)]

        @pl.when(first)
        def _():
            for cp in copies:
                cp.start()

        xv = x_ref[...]
        g = rows_ref[R_G_PRE_MIX:R_G_PRE_MIX + 1, :]
        h = (xv * _rms(xv) * g) * (1.0 + mod_ref[M_SC_M:M_SC_M + 1, :]) + mod_ref[M_SH_M:M_SH_M + 1, :]
        hb = h.astype(BF16)
        h_ref[...] = hb

        @pl.when(first)
        def _():
            for cp in copies:
                cp.wait()

        for c0 in range(0, n, wide):
            proj_ref[:, c0:c0 + wide] = _dot(hb, w_s[:, c0:c0 + wide])

    return pl.pallas_call(
        body,
        name="in_proj_fwd",
        grid=(s // tm,),
        in_specs=[
            pl.BlockSpec((tm, d), lambda i: (i, 0)),
            _resident((None, N_MOD, d), lambda i: (l, 0, 0)),
            _resident((None, N_ROWS, d), lambda i: (l, 0, 0)),
            ANY_SPEC,
            ANY_SPEC,
        ],
        out_specs=(pl.BlockSpec((tm, n), lambda i: (i, 0)), pl.BlockSpec((tm, d), lambda i: (i, 0))),
        out_shape=(jax.ShapeDtypeStruct((s, n), F32), jax.ShapeDtypeStruct((s, d), BF16)),
        scratch_shapes=[pltpu.VMEM((d, n), BF16), pltpu.SemaphoreType.DMA((nb,))],
        compiler_params=_cparams(("arbitrary",), _nbytes((d, n), BF16) + 3 * _nbytes((tm, n), F32) + 8 * _nbytes((tm, d), F32)),
    )(*_hbm(x), mod, rows, *_hbm(win_f), after)


def _mixer_core_fwd(proj, x, mod, rows, cwf, wr, wi, wa_f, wb_f, wo_f, l, tm, after):
    s, d = x.shape
    nh, bw, _ = wr.shape[1:]

    def body(proj_ref, x_ref, mod_ref, rows_ref, cw_ref, wr_ref, wi_ref, wa_ref, wb_ref, wo_ref, after_ref,
             x1_ref, hs_ref, yap_ref, ybp_ref, y_ref, cvbuf, xbbuf, a_s, b_s, h_s, hprev):
        i = pl.program_id(0)

        @pl.when(i == 0)
        def _():
            cvbuf[...] = jnp.zeros((SUBLANES, d), F32)
            xbbuf[...] = jnp.zeros((SUBLANES, d), F32)
            hprev[...] = jnp.zeros((SUBLANES, d), F32)

        def row(r):
            return rows_ref[r:r + 1, :]

        def tap(r):
            return cw_ref[r:r + 1, :]

        ba = proj_ref[:, 0:d]
        cv = proj_ref[:, d:2 * d] * proj_ref[:, 2 * d:3 * d]
        cvt = cvbuf[...]
        conv3 = ((row(R_CONV_A_B) + _shift_rows(cv, cvt, 2, False) * tap(CW_A)) + _shift_rows(cv, cvt, 1, False) * tap(CW_A + 1)) + cv * tap(CW_A + 2)
        ya = ba * conv3
        cvbuf[...] = cv[tm - SUBLANES:, :]
        sp = _softplus_neg(row(R_LAMBDA))
        for h in range(nh):
            lo, hi = h * bw, (h + 1) * bw
            xb = proj_ref[:, 3 * d + lo:3 * d + hi]
            xbt = xbbuf[:, lo:hi]

            def hrow(r):
                return rows_ref[r:r + 1, lo:hi]

            def htap(r):
                return cw_ref[r:r + 1, lo:hi]

            xc = (((hrow(R_CONV_B_B) + _shift_rows(xb, xbt, 3, False) * htap(CW_B)) + _shift_rows(xb, xbt, 2, False) * htap(CW_B + 1))
                  + _shift_rows(xb, xbt, 1, False) * htap(CW_B + 2)) + xb * htap(CW_B + 3)
            xcb = xc.astype(BF16)
            r = _sigmoid(_dot(xcb, wr_ref[h]) + hrow(R_B_GATE_R))
            ig = _sigmoid(_dot(xcb, wi_ref[h]) + hrow(R_B_GATE_I))
            la = (-LRU_C * r) * sp[:, lo:hi]
            a_s[h] = jnp.exp(la)
            b_s[h] = jnp.sqrt(_neg_expm1_twice(la)) * (ig * xc)
        xbbuf[...] = proj_ref[tm - SUBLANES:, 3 * d:4 * d]

        def group(j, hp):
            base = pl.multiple_of(j * SCAN_ROWS, SCAN_ROWS)
            return jnp.concatenate([_scan_group(a_s, b_s, h_s, h, base, hp[:, h * bw:(h + 1) * bw], False) for h in range(nh)], axis=1)

        hprev[...] = lax.fori_loop(0, tm // SCAN_ROWS, group, hprev[...])
        for h in range(nh):
            hs_ref[:, h * bw:(h + 1) * bw] = h_s[h]
        gel, _ = _gelu(proj_ref[:, 4 * d:5 * d])
        yb = hs_ref[...] * gel
        yap = _dot(ya.astype(BF16), wa_ref[...])
        ybp = _dot(yb.astype(BF16), wb_ref[...])
        yap_ref[...] = yap
        ybp_ref[...] = ybp
        m = _sigmoid(proj_ref[:, 5 * d:6 * d]) * yap + _sigmoid(proj_ref[:, 6 * d:7 * d]) * ybp
        y = _dot(m.astype(BF16), wo_ref[...])
        y_ref[...] = y
        x1_ref[...] = x_ref[...] + mod_ref[M_GT_M:M_GT_M + 1, :] * ((y * _rms(y)) * row(R_G_POST_MIX))

    tile = pl.BlockSpec((tm, d), lambda i: (i, 0))
    return pl.pallas_call(
        body,
        name="mixer_core_fwd",
        grid=(s // tm,),
        in_specs=[
            pl.BlockSpec((tm, 7 * d), lambda i: (i, 0)),
            tile,
            _resident((None, N_MOD, d), lambda i: (l, 0, 0)),
            _resident((None, N_ROWS, d), lambda i: (l, 0, 0)),
            _resident((None, CW_ROWS, d), lambda i: (l, 0, 0)),
            _resident((None, nh, bw, bw), lambda i: (l, 0, 0, 0)),
            _resident((None, nh, bw, bw), lambda i: (l, 0, 0, 0)),
            _resident((d, d), lambda i: (0, 0)),
            _resident((d, d), lambda i: (0, 0)),
            _resident((d, d), lambda i: (0, 0)),
            ANY_SPEC,
        ],
        out_specs=(tile,) * 5,
        out_shape=(jax.ShapeDtypeStruct((s, d), F32),) * 5,
        scratch_shapes=[
            pltpu.VMEM((SUBLANES, d), F32),
            pltpu.VMEM((SUBLANES, d), F32),
            pltpu.VMEM((nh, tm, bw), F32),
            pltpu.VMEM((nh, tm, bw), F32),
            pltpu.VMEM((nh, tm, bw), F32),
            pltpu.VMEM((SUBLANES, d), F32),
        ],
        compiler_params=_cparams(("arbitrary",), 3 * _nbytes((d, d), BF16) + 2 * _nbytes((tm, 7 * d), F32) + 40 * _nbytes((tm, d), F32)),
    )(*_hbm(proj, x), mod, rows, cwf, wr, wi, wa_f, wb_f, wo_f, after)


def _mlp_fwd(x1, mod, rows, wup_f, wdn_f, l, tm, after, target=None):
    s, d = x1.shape
    nb, _, cu = wup_f.shape
    dff = nb * cu
    with_loss = target is not None

    def body(x1_ref, mod_ref, rows_ref, wu_ref, wd_ref, after_ref, *refs):
        x2_ref, ru_ref, y2_ref, h2_ref = refs[-6:-2] if with_loss else refs
        xv = x1_ref[...]
        g = rows_ref[R_G_PRE_MLP:R_G_PRE_MLP + 1, :]
        h2 = ((xv * _rms(xv) * g) * (1.0 + mod_ref[M_SC_F:M_SC_F + 1, :]) + mod_ref[M_SH_F:M_SH_F + 1, :]).astype(BF16)
        h2_ref[...] = h2
        ru = jnp.concatenate([jnp.maximum(_dot(h2, wu_ref[j]), 0.0) for j in range(nb)], axis=1)
        ru_ref[...] = ru.astype(BF16)
        y2 = _dot((ru * ru).astype(BF16), wd_ref[...])
        y2_ref[...] = y2
        x2 = xv + mod_ref[M_GT_F:M_GT_F + 1, :] * ((y2 * _rms(y2)) * rows_ref[R_G_POST_MLP:R_G_POST_MLP + 1, :])
        x2_ref[...] = x2
        if with_loss:
            t_ref, dy_ref, loss_ref = refs[0], refs[-2], refs[-1]

            @pl.when(pl.program_id(0) == 0)
            def _():
                loss_ref[...] = jnp.zeros(loss_ref.shape, F32)

            e = x2 - t_ref[...]
            dy_ref[...] = e * (1.0 / d)
            loss_ref[...] += 0.5 * jnp.sum(jnp.mean(e * e, axis=-1, keepdims=True), axis=0, keepdims=True)

    tile = pl.BlockSpec((tm, d), lambda i: (i, 0))
    wide = pl.BlockSpec((tm, dff), lambda i: (i, 0))
    out_specs = (tile, wide, tile, tile)
    out_shape = (jax.ShapeDtypeStruct((s, d), F32), jax.ShapeDtypeStruct((s, dff), BF16),
                 jax.ShapeDtypeStruct((s, d), F32), jax.ShapeDtypeStruct((s, d), BF16))
    if with_loss:
        out_specs += (tile, pl.BlockSpec((SUBLANES, 128), lambda i: (0, 0)))
        out_shape += (jax.ShapeDtypeStruct((s, d), F32), jax.ShapeDtypeStruct((SUBLANES, 128), F32))
    return pl.pallas_call(
        body,
        name="mlp_fwd",
        grid=(s // tm,),
        in_specs=[
            tile,
            _resident((None, N_MOD, d), lambda i: (l, 0, 0)),
            _resident((None, N_ROWS, d), lambda i: (l, 0, 0)),
            _resident((nb, d, cu), lambda i: (0, 0, 0)),
            _resident((dff, d), lambda i: (0, 0)),
            ANY_SPEC,
        ] + ([tile] if with_loss else []),
        out_specs=out_specs,
        out_shape=out_shape,
        compiler_params=_cparams(("arbitrary",), 2 * _nbytes((dff, d), BF16) + 5 * _nbytes((tm, dff), F32) + 16 * _nbytes((tm, d), F32)),
    )(*_hbm(x1), mod, rows, wup_f, wdn_f, after, *(_hbm(target) if with_loss else ()))


def _mlp_bwd(dx2, x1, y2, ru, mod, rows, wup_f, wdn_f, l, tm):
    s, d = x1.shape
    nb, _, cu = wup_f.shape
    dff = nb * cu

    def body(dx2_ref, x1_ref, y2_ref, ru_ref, mod_ref, rows_ref, wu_ref, wd_ref, dx1_ref, dy2_ref, dup_ref, act_ref, sm_ref):
        @pl.when(pl.program_id(0) == 0)
        def _():
            sm_ref[...] = jnp.zeros(sm_ref.shape, F32)

        dout = dx2_ref[...]
        dy2, dgt, dgpost = _postnorm_bwd(y2_ref[...], dout, rows_ref[R_G_POST_MLP:R_G_POST_MLP + 1, :], mod_ref[M_GT_F:M_GT_F + 1, :])
        dy2b = dy2.astype(BF16)
        dy2_ref[...] = dy2b
        ruv = ru_ref[...].astype(F32)
        act_ref[...] = (ruv * ruv).astype(BF16)
        dup = (_dot_nt(dy2b, wd_ref[...]) * (2.0 * ruv)).astype(BF16)
        dup_ref[...] = dup
        dh2 = _dot_nt(dup[:, 0:cu], wu_ref[0])
        for j in range(1, nb):
            dh2 = dh2 + _dot_nt(dup[:, j * cu:(j + 1) * cu], wu_ref[j])
        dxn, dsc, dsh, dgpre = _prenorm_bwd(x1_ref[...], dh2, rows_ref[R_G_PRE_MLP:R_G_PRE_MLP + 1, :], mod_ref[M_SC_F:M_SC_F + 1, :])
        dx1_ref[...] = dout + dxn
        for r, v in ((G_MLP_GT, dgt), (G_MLP_GPOST, dgpost), (G_MLP_SC, dsc), (G_MLP_SH, dsh), (G_MLP_GPRE, dgpre)):
            sm_ref[r:r + 1, :] += v

    tile = pl.BlockSpec((tm, d), lambda i: (i, 0))
    wide = pl.BlockSpec((tm, dff), lambda i: (i, 0))
    return pl.pallas_call(
        body,
        name="mlp_bwd",
        grid=(s // tm,),
        in_specs=[
            tile, tile, tile, wide,
            _resident((None, N_MOD, d), lambda i: (l, 0, 0)),
            _resident((None, N_ROWS, d), lambda i: (l, 0, 0)),
            _resident((nb, d, cu), lambda i: (0, 0, 0)),
            _resident((dff, d), lambda i: (0, 0)),
        ],
        out_specs=(tile, tile, wide, wide, pl.BlockSpec((G_MLP_ROWS, d), lambda i: (0, 0))),
        out_shape=(jax.ShapeDtypeStruct((s, d), F32), jax.ShapeDtypeStruct((s, d), BF16), jax.ShapeDtypeStruct((s, dff), BF16),
                   jax.ShapeDtypeStruct((s, dff), BF16), jax.ShapeDtypeStruct((G_MLP_ROWS, d), F32)),
        compiler_params=_cparams(("arbitrary",), 2 * _nbytes((dff, d), BF16) + 6 * _nbytes((tm, dff), F32) + 16 * _nbytes((tm, d), F32)),
    )(*_hbm(dx2, x1, y2, ru), mod, rows, wup_f, wdn_f)


def _mixer_core_bwd(dx1, y, yap, ybp, hs, proj, mod, rows, cwf, wr, wi, wa_f, wb_f, wo_f, l, tm, after):
    s, d = dx1.shape
    nh, bw, _ = wr.shape[1:]
    nt = s // tm
    per = tm // SUBLANES

    def body(dx1_ref, y_ref, yap_ref, ybp_ref, hs_ref, hsh_ref, proj_ref, projh_ref, mod_ref, rows_ref, cw_ref,
             wr_ref, wi_ref, wa_ref, wb_ref, wo_ref, after_ref,
             dproj_ref, dy_ref, m_ref, dyap_ref, dybp_ref, ya_ref, yb_ref, sm_ref, dwg_ref,
             abuf, dcbuf, dxbuf, al_s, dh_s, lam_s, lnext):
        i = pl.program_id(0)
        first_tile = i == nt - 1

        @pl.when(i == 0)
        def _():
            sm_ref[...] = jnp.zeros(sm_ref.shape, F32)
            dwg_ref[...] = jnp.zeros(dwg_ref.shape, F32)
            zero = jnp.zeros((SUBLANES, d), F32)
            abuf[...] = zero
            dcbuf[...] = zero
            dxbuf[...] = zero
            lnext[...] = zero

        def row(r):
            return rows_ref[r:r + 1, :]

        def tap(r):
            return cw_ref[r:r + 1, :]

        def acc(r, v):
            sm_ref[r:r + 1, :] += v

        keep_halo = jnp.where(first_tile, 0.0, 1.0)
        dy, dgt, dgpost = _postnorm_bwd(y_ref[...], dx1_ref[...], row(R_G_POST_MIX), mod_ref[M_GT_M:M_GT_M + 1, :])
        acc(G_MIX_GT, dgt)
        acc(G_MIX_GPOST, dgpost)
        dyb16 = dy.astype(BF16)
        dy_ref[...] = dyb16
        dm = _dot_nt(dyb16, wo_ref[...])
        sa = _sigmoid(proj_ref[:, 5 * d:6 * d])
        sb = _sigmoid(proj_ref[:, 6 * d:7 * d])
        yap = yap_ref[...]
        ybp = ybp_ref[...]
        m_ref[...] = (sa * yap + sb * ybp).astype(BF16)
        dyap_f = dm * sa
        dybp_f = dm * sb
        dyap = dyap_f.astype(BF16)
        dybp = dybp_f.astype(BF16)
        dyap_ref[...] = dyap
        dybp_ref[...] = dybp
        dproj_ref[:, 5 * d:6 * d] = (dyap_f * yap * (1.0 - sa)).astype(BF16)
        dproj_ref[:, 6 * d:7 * d] = (dybp_f * ybp * (1.0 - sb)).astype(BF16)
        dya = _dot_nt(dyap, wa_ref[...])
        dyb = _dot_nt(dybp, wb_ref[...])
        ba = proj_ref[:, 0:d]
        ca = proj_ref[:, d:2 * d]
        va = proj_ref[:, 2 * d:3 * d]
        cv = ca * va
        cvh = keep_halo * (projh_ref[:, d:2 * d] * projh_ref[:, 2 * d:3 * d])
        cvm2 = _shift_rows(cv, cvh, 2, False)
        cvm1 = _shift_rows(cv, cvh, 1, False)
        conv3 = ((row(R_CONV_A_B) + cvm2 * tap(CW_A)) + cvm1 * tap(CW_A + 1)) + cv * tap(CW_A + 2)
        ya_ref[...] = (ba * conv3).astype(BF16)
        dproj_ref[:, 0:d] = (dya * conv3).astype(BF16)
        dc3 = dya * ba
        acc(G_MIX_CAB, _colsum(dc3))
        acc(G_MIX_CAW, _colsum(dc3 * cvm2))
        acc(G_MIX_CAW + 1, _colsum(dc3 * cvm1))
        acc(G_MIX_CAW + 2, _colsum(dc3 * cv))
        dct = dcbuf[...]
        dcv = (dc3 * tap(CW_A + 2) + _shift_rows(dc3, dct, 1, True) * tap(CW_A + 1)) + _shift_rows(dc3, dct, 2, True) * tap(CW_A)
        dcbuf[...] = dc3[:SUBLANES, :]
        dproj_ref[:, d:2 * d] = (dcv * va).astype(BF16)
        dproj_ref[:, 2 * d:3 * d] = (dcv * ca).astype(BF16)
        xb = proj_ref[:, 3 * d:4 * d]
        gb = proj_ref[:, 4 * d:5 * d]
        xbh = keep_halo * projh_ref[:, 3 * d:4 * d]
        xm3 = _shift_rows(xb, xbh, 3, False)
        xm2 = _shift_rows(xb, xbh, 2, False)
        xm1 = _shift_rows(xb, xbh, 1, False)
        xc = (((row(R_CONV_B_B) + xm3 * tap(CW_B)) + xm2 * tap(CW_B + 1)) + xm1 * tap(CW_B + 2)) + xb * tap(CW_B + 3)
        lam = row(R_LAMBDA)
        sp = _softplus_neg(lam)
        xcb, r, ig, a, mult = _gates(xc, wr_ref, wi_ref, row(R_B_GATE_R), row(R_B_GATE_I), sp, nh, bw)
        gel, gsig = _gelu(gb)
        hs = hs_ref[...]
        yb_ref[...] = (hs * gel).astype(BF16)
        dproj_ref[:, 4 * d:5 * d] = (dyb * hs * _gelu_grad(gb, gsig)).astype(BF16)
        alpha = _shift_rows(a, abuf[...], 1, True)
        abuf[...] = a[:SUBLANES, :]
        dhs = dyb * gel
        for h in range(nh):
            al_s[h] = alpha[:, h * bw:(h + 1) * bw]
            dh_s[h] = dhs[:, h * bw:(h + 1) * bw]
        groups = tm // SCAN_ROWS

        def group(j, ln):
            base = pl.multiple_of((groups - 1 - j) * SCAN_ROWS, SCAN_ROWS)
            return jnp.concatenate([_scan_group(al_s, dh_s, lam_s, h, base, ln[:, h * bw:(h + 1) * bw], True) for h in range(nh)], axis=1)

        lnext[...] = lax.fori_loop(0, groups, group, lnext[...])
        dbb = jnp.concatenate([lam_s[h] for h in range(nh)], axis=1)
        da = dbb * _shift_rows(hs, keep_halo * hsh_ref[...], 1, False)
        dbx = dbb * xc
        dmult = dbx * ig
        dig = dbx * mult
        dxc = (dbb * mult) * ig
        dla = a * (da - (dmult * a) / mult)
        dlar = dla * r
        acc(G_MIX_LAM, _colsum(dlar) * (LRU_C * _sigmoid(-lam)))
        dzr = (dlar * (1.0 - r)) * (-LRU_C * sp)
        dzi = dig * ig * (1.0 - ig)
        acc(G_MIX_BR, _colsum(dzr))
        acc(G_MIX_BI, _colsum(dzi))
        dzrb = dzr.astype(BF16)
        dzib = dzi.astype(BF16)
        back = []
        for h in range(nh):
            sl = slice(h * bw, (h + 1) * bw)
            back.append(_dot_nt(dzrb[:, sl], wr_ref[h]) + _dot_nt(dzib[:, sl], wi_ref[h]))
            dwg_ref[0, h] += _dot_tn(xcb[:, sl], dzrb[:, sl])
            dwg_ref[1, h] += _dot_tn(xcb[:, sl], dzib[:, sl])
        dxc = dxc + jnp.concatenate(back, axis=1)
        acc(G_MIX_CBB, _colsum(dxc))
        acc(G_MIX_CBW, _colsum(dxc * xm3))
        acc(G_MIX_CBW + 1, _colsum(dxc * xm2))
        acc(G_MIX_CBW + 2, _colsum(dxc * xm1))
        acc(G_MIX_CBW + 3, _colsum(dxc * xb))
        dxt = dxbuf[...]
        dxb = (((dxc * tap(CW_B + 3) + _shift_rows(dxc, dxt, 1, True) * tap(CW_B + 2)) + _shift_rows(dxc, dxt, 2, True) * tap(CW_B + 1))
               + _shift_rows(dxc, dxt, 3, True) * tap(CW_B))
        dxbuf[...] = dxc[:SUBLANES, :]
        dproj_ref[:, 3 * d:4 * d] = dxb.astype(BF16)

    def rev(i):
        return (nt - 1 - i, 0)

    def halo(i):
        return (jnp.maximum((nt - 1 - i) * per - 1, 0), 0)

    tile = pl.BlockSpec((tm, d), rev)
    return pl.pallas_call(
        body,
        name="mixer_core_bwd",
        grid=(nt,),
        in_specs=[
            tile, tile, tile, tile, tile,
            pl.BlockSpec((SUBLANES, d), halo),
            pl.BlockSpec((tm, 7 * d), rev),
            pl.BlockSpec((SUBLANES, 7 * d), halo),
            _resident((None, N_MOD, d), lambda i: (l, 0, 0)),
            _resident((None, N_ROWS, d), lambda i: (l, 0, 0)),
            _resident((None, CW_ROWS, d), lambda i: (l, 0, 0)),
            _resident((None, nh, bw, bw), lambda i: (l, 0, 0, 0)),
            _resident((None, nh, bw, bw), lambda i: (l, 0, 0, 0)),
            _resident((d, d), lambda i: (0, 0)),
            _resident((d, d), lambda i: (0, 0)),
            _resident((d, d), lambda i: (0, 0)),
            ANY_SPEC,
        ],
        out_specs=(pl.BlockSpec((tm, 7 * d), rev),) + (tile,) * 6 + (
            pl.BlockSpec((G_MIX_ROWS, d), lambda i: (0, 0)), pl.BlockSpec((2, nh, bw, bw), lambda i: (0, 0, 0, 0))),
        out_shape=(jax.ShapeDtypeStruct((s, 7 * d), BF16),) + (jax.ShapeDtypeStruct((s, d), BF16),) * 6 + (
            jax.ShapeDtypeStruct((G_MIX_ROWS, d), F32), jax.ShapeDtypeStruct((2, nh, bw, bw), F32)),
        scratch_shapes=[pltpu.VMEM((SUBLANES, d), F32)] * 3 + [pltpu.VMEM((nh, tm, bw), F32)] * 3 + [pltpu.VMEM((SUBLANES, d), F32)],
        compiler_params=_cparams(("arbitrary",), 3 * _nbytes((d, d), BF16) + 3 * _nbytes((tm, 7 * d), F32) + 64 * _nbytes((tm, d), F32)),
    )(*_hbm(dx1, y, yap, ybp, hs, hs, proj, proj), mod, rows, cwf, wr, wi, wa_f, wb_f, wo_f, after)


def _in_proj_bwd(dproj, x, dx1, mod, rows, win_f, l, tm, after):
    s, d = x.shape
    nb, _, ci = win_f.shape

    def body(dp_ref, x_ref, dx1_ref, mod_ref, rows_ref, w_hbm, after_ref, dx_ref, sm_ref, w_s, sem):
        @pl.when(pl.program_id(0) == 0)
        def _():
            sm_ref[...] = jnp.zeros(sm_ref.shape, F32)
            _load_side_by_side(w_hbm, w_s, sem)

        dh = _dot_nt(dp_ref[...], w_s[...])
        dxn, dsc, dsh, dg = _prenorm_bwd(x_ref[...], dh, rows_ref[R_G_PRE_MIX:R_G_PRE_MIX + 1, :], mod_ref[M_SC_M:M_SC_M + 1, :])
        dx_ref[...] = dx1_ref[...] + dxn
        for r, v in ((G_IN_SC, dsc), (G_IN_SH, dsh), (G_IN_GPRE, dg)):
            sm_ref[r:r + 1, :] += v

    tile = pl.BlockSpec((tm, d), lambda i: (i, 0))
    return pl.pallas_call(
        body,
        name="in_proj_bwd",
        grid=(s // tm,),
        in_specs=[
            pl.BlockSpec((tm, nb * ci), lambda i: (i, 0)), tile, tile,
            _resident((None, N_MOD, d), lambda i: (l, 0, 0)),
            _resident((None, N_ROWS, d), lambda i: (l, 0, 0)),
            ANY_SPEC,
            ANY_SPEC,
        ],
        out_specs=(tile, pl.BlockSpec((G_IN_ROWS, d), lambda i: (0, 0))),
        out_shape=(jax.ShapeDtypeStruct((s, d), F32), jax.ShapeDtypeStruct((G_IN_ROWS, d), F32)),
        scratch_shapes=[pltpu.VMEM((d, nb * ci), BF16), pltpu.SemaphoreType.DMA((nb,))],
        compiler_params=_cparams(("arbitrary",), _nbytes((nb, d, ci), BF16) + 2 * _nbytes((tm, nb * ci), BF16) + 16 * _nbytes((tm, d), F32)),
    )(*_hbm(dproj, x, dx1), mod, rows, *_hbm(win_f), after)


def _wgrad(a, b, cols_owned, after):
    s, k1 = a.shape
    k2 = b.shape[1]
    if cols_owned:
        c = k2 // N_DEV
        per = min(N_DEV, MXU_COLS // math.gcd(c, MXU_COLS))
        assert N_DEV % per == 0
        nblk, bk1 = N_DEV // per, k1
        a_spec = _resident((s, k1), lambda j: (0, 0))
        b_spec = pl.BlockSpec((s, per * c), lambda j: (0, j))
    else:
        per, c = 1, k2
        bk1 = min(WGRAD_ROWS, k1)
        nblk = k1 // bk1
        a_spec = pl.BlockSpec((s, bk1), lambda j: (0, j))
        b_spec = _resident((s, k2), lambda j: (0, 0))
    rows = min(WGRAD_ROWS, bk1)

    def body(a_ref, b_ref, after_ref, o_ref):
        for r0 in range(0, bk1, rows):
            g = _dot_tn(a_ref[:, r0:r0 + rows], b_ref[...]).astype(BF16)
            for p in range(per):
                o_ref[p, r0:r0 + rows, :] = g[:, p * c:(p + 1) * c]

    out = pl.pallas_call(
        body,
        name="wgrad",
        grid=(nblk,),
        in_specs=[a_spec, b_spec, ANY_SPEC],
        out_specs=pl.BlockSpec((per, bk1, c), lambda j: (j, 0, 0)),
        out_shape=pltpu.HBM((nblk * per, bk1, c), BF16),
        compiler_params=_cparams(("parallel",), 4 * _nbytes((bk1, per * c), F32) + 4 * _nbytes((s, bk1 + per * c), BF16)),
    )(pltpu.with_memory_space_constraint(a, pltpu.HBM), pltpu.with_memory_space_constraint(b, pltpu.HBM), after)
    return out if cols_owned else out.reshape(N_DEV, k1 // N_DEV, k2)


def _adam_update(w, g, m, v):
    m = ADAM_B1 * m + (1.0 - ADAM_B1) * g
    v = ADAM_B2 * v + (1.0 - ADAM_B2) * (g * g)
    m_hat = m / (1.0 - ADAM_B1 ** ADAM_STEP)
    v_hat = v / (1.0 - ADAM_B2 ** ADAM_STEP)
    delta = -ADAM_LR * (m_hat / (jnp.sqrt(v_hat) + ADAM_EPS) + ADAM_WD * w)
    return delta, m, v


def _sum_adamw(recv, w, m, v, tr, after):
    nl, ra, cb = w.shape
    assert nl == len(recv) == 2

    def body(r0_ref, r1_ref, w_ref, m_ref, v_ref, after_ref, g_ref, d_ref, nm_ref, nv_ref):
        def total(r_ref):
            g = r_ref[0].astype(F32)
            for j in range(1, N_DEV):
                g = g + r_ref[j].astype(F32)
            return g

        g = jnp.where(pl.program_id(0) == 0, total(r0_ref), total(r1_ref))
        g_ref[...] = g
        d_ref[...], nm_ref[...], nv_ref[...] = _adam_update(w_ref[...], g, m_ref[...], v_ref[...])

    blk = pl.BlockSpec((None, tr, cb), lambda l, i: (l, i, 0))
    return pl.pallas_call(
        body,
        name="sum_adamw",
        grid=(nl, ra // tr),
        in_specs=[pl.BlockSpec((N_DEV, tr, cb), lambda l, i: (0, i * (1 - l), 0)),
                  pl.BlockSpec((N_DEV, tr, cb), lambda l, i: (0, i * l, 0)), blk, blk, blk, ANY_SPEC],
        out_specs=(blk,) * 4,
        out_shape=(jax.ShapeDtypeStruct((nl, ra, cb), F32),) * 4,
        compiler_params=_cparams(("arbitrary", "arbitrary"), 6 * _nbytes((N_DEV, tr, cb), BF16) + 32 * _nbytes((tr, cb), F32)),
    )(recv[0], recv[1], w, m, v, *_hbm(after))


def _adamw(w, g, m, v):
    def body(w_ref, g_ref, m_ref, v_ref, d_ref, nm_ref, nv_ref):
        d_ref[...], nm_ref[...], nv_ref[...] = _adam_update(w_ref[...], g_ref[...], m_ref[...], v_ref[...])

    return pl.pallas_call(
        body,
        name="adamw",
        in_specs=[VMEM_SPEC] * 4,
        out_specs=(VMEM_SPEC,) * 3,
        out_shape=(jax.ShapeDtypeStruct(w.shape, F32),) * 3,
        compiler_params=_cparams(vmem=10 * _nbytes(w.shape, F32)),
    )(w, g, m, v)


def _adamw_tiled(w, g, m, v, tr):
    nl, ra, cb = w.shape

    def body(w_ref, g_ref, m_ref, v_ref, d_ref, nm_ref, nv_ref):
        d_ref[...], nm_ref[...], nv_ref[...] = _adam_update(w_ref[...], g_ref[...], m_ref[...], v_ref[...])

    blk = pl.BlockSpec((None, tr, cb), lambda l, i: (l, i, 0))
    return pl.pallas_call(
        body,
        name="adamw_tiled",
        grid=(nl, ra // tr),
        in_specs=[blk] * 4,
        out_specs=(blk,) * 3,
        out_shape=(jax.ShapeDtypeStruct((nl, ra, cb), F32),) * 3,
        compiler_params=_cparams(("parallel", "parallel")),
    )(*_hbm(w, g, m, v))


def _token_tile(s):
    return min(TOKEN_TILE, max(SUBLANES * 2, s // 4))


def kernel(x, c, w_mod, b_mod, g_pre_mix, g_post_mix, w_in, conv_a_w, conv_a_b, w_a_out, conv_b_w, conv_b_b, w_gate_r, b_gate_r, w_gate_i, b_gate_i, lru_lambda, w_b_out, w_o, g_pre_mlp, g_post_mlp, w_mlp_up, w_mlp_down, loss_target, m_w_mod, m_b_mod, m_g_pre_mix, m_g_post_mix, m_w_in, m_conv_a_w, m_conv_a_b, m_w_a_out, m_conv_b_w, m_conv_b_b, m_w_gate_r, m_b_gate_r, m_w_gate_i, m_b_gate_i, m_lru_lambda, m_w_b_out, m_w_o, m_g_pre_mlp, m_g_post_mlp, m_w_mlp_up, m_w_mlp_down, v_w_mod, v_b_mod, v_g_pre_mix, v_g_post_mix, v_w_in, v_conv_a_w, v_conv_a_b, v_w_a_out, v_conv_b_w, v_conv_b_b, v_w_gate_r, v_b_gate_r, v_w_gate_i, v_b_gate_i, v_lru_lambda, v_w_b_out, v_w_o, v_g_pre_mlp, v_g_post_mlp, v_w_mlp_up, v_w_mlp_down):
    nl = w_mod.shape[0]
    s, d = x.shape[1], x.shape[2]
    nh, bw = w_gate_r.shape[1], w_gate_r.shape[2]
    cwid = conv_a_w.shape[2]
    tm = _token_tile(s)
    tmx = min(2 * tm, s)
    _, me = _my_position()
    xs = x.reshape(s, d)
    target = loss_target.reshape(s, d)

    vec_names = (g_pre_mix, g_post_mix, conv_a_b, conv_b_b, b_gate_r, b_gate_i, lru_lambda, g_pre_mlp, g_post_mlp)
    rows = jnp.concatenate([jnp.stack(vec_names, axis=1), jnp.zeros((nl, N_ROWS - len(vec_names), d), F32)], axis=1)
    cw = jnp.concatenate([conv_a_w, conv_b_w, jnp.zeros((nl, CW_ROWS - 7, cwid), F32)], axis=1)

    large = {"w_in": w_in, "w_a_out": w_a_out, "w_b_out": w_b_out, "w_o": w_o, "w_mlp_up": w_mlp_up, "w_mlp_down": w_mlp_down}
    groups = (("in", ("w_in",)), ("mix", ("w_a_out", "w_b_out", "w_o")), ("mlp", ("w_mlp_up", "w_mlp_down")))
    mod, cact, cwf = _prep_small(c, w_mod, b_mod, cw)
    mod = mod.reshape(nl, N_MOD, d)
    gathers = {}
    tok = mod
    for l in range(nl):
        for gname, members in groups:
            gathers[l, gname] = _gather2_start([large[n][l].astype(BF16) for n in members], tok, f"gather_start_{gname}{l}")
            tok = gathers[l, gname][4]
    wr = w_gate_r.astype(BF16)
    wi = w_gate_i.astype(BF16)

    forwarded = {}

    def forward(l, gname, after):
        forwarded[l, gname] = _gather2_forward(gathers[l, gname], after, f"gather_forward_{gname}{l}")
        return forwarded[l, gname][4]

    def gathered(l, gname, after):
        return _gather2_wait(forwarded[l, gname], after, f"gather_wait_{gname}{l}")

    saved = []
    weights = []
    xin = xs
    for l in range(nl):
        if l == 0:
            tok = forward(0, "in", tok)
        (win_f,) = gathered(l, "in", tok if l == 0 else xin)
        if l > 0:
            tok = forward(l, "mix", win_f)
        proj, h = _in_proj_fwd(xin, mod, rows, win_f, l, tmx, tok)
        if l == 0:
            tok = forward(0, "mix", proj)
        wa_f, wb_f, wo_f = (w.reshape(d, d) for w in gathered(l, "mix", proj))
        if l > 0:
            tok = forward(l, "mlp", wo_f)
        x1, hs, yap, ybp, y = _mixer_core_fwd(proj, xin, mod, rows, cwf, wr, wi, wa_f, wb_f, wo_f, l, tm, tok)
        if l == 0:
            forward(0, "mlp", x1)
        wup_f, wdn_f = gathered(l, "mlp", x1)
        wdn_f = wdn_f.reshape(-1, d)
        if l + 1 < nl:
            tok = forward(l + 1, "in", wdn_f)
        x2, ru, y2, h2, *last = _mlp_fwd(x1, mod, rows, wup_f, wdn_f, l, tmx, tok, target if l + 1 == nl else None)
        saved.append((xin, proj, h, x1, hs, yap, ybp, y, ru, y2, h2))
        weights.append((win_f, wa_f, wb_f, wo_f, wup_f, wdn_f))
        xin = x2
    dx, loss_part = last[0], last[1][0, 0]

    scatters = {}
    small = [None] * nl
    gate_parts = [None] * nl

    def scatter(l, gname, parts):
        scatters[l, gname] = _exchange_start(parts, False, f"scatter_start_{gname}{l}")
        return scatters[l, gname][4]

    for l in reversed(range(nl)):
        xin, proj, h, x1, hs, yap, ybp, y, ru, y2, h2 = saved[l]
        win_f, wa_f, wb_f, wo_f, wup_f, wdn_f = weights[l]
        dx1, dy2, dup, act, sm_mlp = _mlp_bwd(dx, x1, y2, ru, mod, rows, wup_f, wdn_f, l, tm)
        g_up = _wgrad(h2, dup, True, dx1)
        g_dn = _wgrad(act, dy2, False, g_up)
        tok = scatter(l, "mlp", [g_up, g_dn])
        dproj, dy, m, dyap, dybp, ya, yb, sm_mix, dwg = _mixer_core_bwd(
            dx1, y, yap, ybp, hs, proj, mod, rows, cwf, wr, wi, wa_f, wb_f, wo_f, l, tm // 2, tok)
        gate_parts[l] = _exchange_start([dwg.astype(BF16)], True, f"gates_start{l}")
        g_a = _wgrad(ya, dyap, False, gate_parts[l][4])
        g_b = _wgrad(yb, dybp, False, g_a)
        g_o = _wgrad(m, dy, False, g_b)
        tok = scatter(l, "mix", [g_a, g_b, g_o])
        tok = scatter(l, "in", [_wgrad(h, dproj, True, tok)])
        dx, sm_in = _in_proj_bwd(dproj, xin, dx1, mod, rows, win_f, l, tm, tok)
        small[l] = jnp.concatenate([sm_mlp, sm_mix, sm_in], axis=0)
    grad_x = dx.reshape(x.shape)

    recv = {}
    big = {}
    moments = {"w_in": (w_in, m_w_in, v_w_in), "w_mlp_up": (w_mlp_up, m_w_mlp_up, v_w_mlp_up), "w_a_out": (w_a_out, m_w_a_out, v_w_a_out),
               "w_b_out": (w_b_out, m_w_b_out, v_w_b_out), "w_o": (w_o, m_w_o, v_w_o), "w_mlp_down": (w_mlp_down, m_w_mlp_down, v_w_mlp_down)}

    def collect(l, gname, after):
        for n, zone in zip(dict(groups)[gname], _exchange_wait(scatters[l, gname], after, False, f"scatter_wait_{gname}{l}")):
            recv[n, l] = zone

    def update(name, after):
        w, m_, v_ = moments[name]
        big[name] = _sum_adamw([recv[name, l] for l in range(nl)], w, m_, v_, min(ADAM_ROWS, w.shape[1]), after)
        return big[name][1]

    for l, gname in ((1, "mlp"), (1, "in"), (1, "mix"), (0, "mlp")):
        collect(l, gname, dx)
    done = update("w_mlp_down", update("w_mlp_up", dx))
    collect(0, "mix", done)
    for name in ("w_a_out", "w_b_out", "w_o"):
        done = update(name, done)
    collect(0, "in", done)
    done = update("w_in", done)

    lrows = jnp.concatenate(small, axis=0)
    lrows = lrows.at[G_LOSS_ROW, 0].set(loss_part)

    def lrow(a, l, r):
        return a[l * G_LAYER_ROWS + r]

    dm = jnp.stack([jnp.concatenate([lrow(lrows, l, G_MLP_ROWS + G_MIX_ROWS + G_IN_SH), lrow(lrows, l, G_MLP_ROWS + G_MIX_ROWS + G_IN_SC),
                                     lrow(lrows, l, G_MLP_ROWS + G_MIX_GT), lrow(lrows, l, G_MLP_SH), lrow(lrows, l, G_MLP_SC),
                                     lrow(lrows, l, G_MLP_GT)]) for l in range(nl)])
    dm8 = jnp.concatenate([dm[:, None, :], jnp.zeros((nl, SUBLANES - 1, N_MOD * d), F32)], axis=1)
    srows, g_w_mod = _reduce_small(lrows, dm8, cact, done)
    loss = srows[G_LOSS_ROW, 0]
    sgates = _sum_gathered([_exchange_wait(gate_parts[l], srows, True, f"gates_wait{l}")[0] for l in range(nl)])

    def srow(l, r):
        return lrow(srows, l, r)

    def per_layer(r):
        return jnp.stack([srow(l, r) for l in range(nl)])

    mix0 = G_MLP_ROWS
    in0 = G_MLP_ROWS + G_MIX_ROWS
    g_b_mod = jnp.stack([jnp.concatenate([srow(l, in0 + G_IN_SH), srow(l, in0 + G_IN_SC), srow(l, mix0 + G_MIX_GT),
                                          srow(l, G_MLP_SH), srow(l, G_MLP_SC), srow(l, G_MLP_GT)]) for l in range(nl)])
    conv_a_full = jnp.stack([jnp.stack([srow(l, mix0 + G_MIX_CAW + k) for k in range(3)]) for l in range(nl)])
    conv_b_full = jnp.stack([jnp.stack([srow(l, mix0 + G_MIX_CBW + k) for k in range(4)]) for l in range(nl)])
    grads = {
        "b_mod": g_b_mod,
        "g_pre_mix": per_layer(in0 + G_IN_GPRE),
        "g_post_mix": per_layer(mix0 + G_MIX_GPOST),
        "conv_a_w": lax.dynamic_slice_in_dim(conv_a_full, me * cwid, cwid, axis=2),
        "conv_a_b": per_layer(mix0 + G_MIX_CAB),
        "conv_b_w": lax.dynamic_slice_in_dim(conv_b_full, me * cwid, cwid, axis=2),
        "conv_b_b": per_layer(mix0 + G_MIX_CBB),
        "w_gate_r": sgates[:, 0],
        "b_gate_r": per_layer(mix0 + G_MIX_BR),
        "w_gate_i": sgates[:, 1],
        "b_gate_i": per_layer(mix0 + G_MIX_BI),
        "lru_lambda": per_layer(mix0 + G_MIX_LAM),
        "g_pre_mlp": per_layer(G_MLP_GPRE),
        "g_post_mlp": per_layer(G_MLP_GPOST),
    }
    params = {
        "b_mod": (b_mod, m_b_mod, v_b_mod), "g_pre_mix": (g_pre_mix, m_g_pre_mix, v_g_pre_mix), "g_post_mix": (g_post_mix, m_g_post_mix, v_g_post_mix),
        "conv_a_w": (conv_a_w, m_conv_a_w, v_conv_a_w), "conv_a_b": (conv_a_b, m_conv_a_b, v_conv_a_b),
        "conv_b_w": (conv_b_w, m_conv_b_w, v_conv_b_w), "conv_b_b": (conv_b_b, m_conv_b_b, v_conv_b_b),
        "w_gate_r": (w_gate_r, m_w_gate_r, v_w_gate_r), "b_gate_r": (b_gate_r, m_b_gate_r, v_b_gate_r),
        "w_gate_i": (w_gate_i, m_w_gate_i, v_w_gate_i), "b_gate_i": (b_gate_i, m_b_gate_i, v_b_gate_i),
        "lru_lambda": (lru_lambda, m_lru_lambda, v_lru_lambda), "g_pre_mlp": (g_pre_mlp, m_g_pre_mlp, v_g_pre_mlp),
        "g_post_mlp": (g_post_mlp, m_g_post_mlp, v_g_post_mlp),
    }
    out = {}
    for name, g in grads.items():
        w, m_, v_ = params[name]
        flat = (-1, w.shape[-1])
        dl, nm, nv = _adamw(w.reshape(flat), g.reshape(flat), m_.reshape(flat), v_.reshape(flat))
        out[name] = (g.reshape(w.shape), dl.reshape(w.shape), nm.reshape(w.shape), nv.reshape(w.shape))
    out["w_mod"] = (g_w_mod,) + tuple(_adamw_tiled(w_mod, g_w_mod, m_w_mod, v_w_mod, min(ADAM_ROWS, d)))
    out.update(big)

    order = ("w_mod", "b_mod", "g_pre_mix", "g_post_mix", "w_in", "conv_a_w", "conv_a_b", "w_a_out", "conv_b_w", "conv_b_b", "w_gate_r", "b_gate_r",
             "w_gate_i", "b_gate_i", "lru_lambda", "w_b_out", "w_o", "g_pre_mlp", "g_post_mlp", "w_mlp_up", "w_mlp_down")
    return (loss, grad_x) + tuple(out[n][0] for n in order) + tuple(out[n][1] for n in order) + tuple(out[n][2] for n in order) + tuple(out[n][3] for n in order)
```
